```python
import math
import jax, jax.numpy as jnp
from jax import lax
import numpy as np

D_MODEL = 1024
BATCH = 8
SEQ = 4096
DEPTH = 2

D_MIX = D_MODEL
N_MIXERS = 4
D_GROUP = D_MIX // N_MIXERS

MLA_HEADS = 4
MLA_Q_RANK = D_MODEL // 4
MLA_KV_RANK = D_MODEL // 8
MLA_NOPE = 64
MLA_ROPE = 32
MLA_V = D_GROUP // MLA_HEADS
ROPE_THETA = 10000.0

CONV_WIDTH = 3
CONV_CH = D_GROUP

POOL_WINDOWS = (2, 4, 8, 16)
POOL_GROUPS = len(POOL_WINDOWS)
POOL_CH = D_GROUP // POOL_GROUPS

SWA_HEADS = 4
SWA_KV_HEADS = 2
SWA_HEAD_DIM = D_GROUP // SWA_HEADS
SWA_WINDOW = 128

BLOCK = 128

D_FF = -(-8 * D_MODEL // (3 * 256)) * 256

RMS_EPS = 1e-6

IN_SPLITS = (MLA_Q_RANK, MLA_KV_RANK, MLA_ROPE,
             CONV_CH, CONV_CH, CONV_CH,
             D_GROUP,
             SWA_HEADS * SWA_HEAD_DIM,
             SWA_KV_HEADS * SWA_HEAD_DIM,
             SWA_KV_HEADS * SWA_HEAD_DIM)
D_IN = sum(IN_SPLITS)

kernel_name = "hybrid_parallel_mla_conv_pool_swa"


def _split_points():
    pts, acc = [], 0
    for w in IN_SPLITS[:-1]:
        acc += w
        pts.append(acc)
    return pts


def _alibi_slopes(n):
    return np.asarray([2.0 ** (-8.0 * (i + 1) / n) for i in range(n)], dtype=np.float32)


def rmsnorm(x, g):
    xf = x.astype(jnp.float32)
    y = xf * lax.rsqrt(jnp.mean(xf * xf, axis=-1, keepdims=True) + RMS_EPS)
    return (y * g.astype(jnp.float32)).astype(x.dtype)


def rope_tables(seq, dim, dtype):
    inv = 1.0 / (ROPE_THETA ** (jnp.arange(0, dim, 2, dtype=jnp.float32) / dim))
    ang = jnp.arange(seq, dtype=jnp.float32)[:, None] * inv[None, :]
    return jnp.cos(ang).astype(dtype), jnp.sin(ang).astype(dtype)


def apply_rope(x, cos, sin):
    x1, x2 = jnp.split(x, 2, axis=-1)
    c = cos[:, None, :]
    s = sin[:, None, :]
    return jnp.concatenate([x1 * c - x2 * s, x1 * s + x2 * c], axis=-1)


def mla_attention(c_q, c_kv, k_r, q_norm_g, kv_norm_g, w_uq, w_ukv, cos, sin):
    b, s, _ = c_q.shape
    dqk = MLA_NOPE + MLA_ROPE
    q = (rmsnorm(c_q, q_norm_g) @ w_uq).reshape(b, s, MLA_HEADS, dqk)
    q_nope, q_rot = q[..., :MLA_NOPE], q[..., MLA_NOPE:]
    q_rot = apply_rope(q_rot, cos, sin)
    kv = (rmsnorm(c_kv, kv_norm_g) @ w_ukv).reshape(b, s, MLA_HEADS, MLA_NOPE + MLA_V)
    k_nope, v = kv[..., :MLA_NOPE], kv[..., MLA_NOPE:]
    k_rot = apply_rope(k_r[:, :, None, :], cos, sin)
    k = jnp.concatenate([k_nope, jnp.broadcast_to(k_rot, (b, s, MLA_HEADS, MLA_ROPE))], axis=-1)
    q = jnp.concatenate([q_nope, q_rot], axis=-1) * (1.0 / math.sqrt(dqk))
    nb = s // BLOCK
    qb = q.reshape(b, nb, BLOCK, MLA_HEADS, dqk).transpose(1, 0, 2, 3, 4)
    key_pos = jnp.arange(s)

    def one_block(args):
        q_blk, i = args
        sc = jnp.einsum('bqhd,bkhd->bhqk', q_blk, k).astype(jnp.float32)
        q_pos = i * BLOCK + jnp.arange(BLOCK)
        causal = key_pos[None, :] <= q_pos[:, None]
        sc = jnp.where(causal[None, None], sc, -jnp.inf)
        p = jax.nn.softmax(sc, axis=-1).astype(v.dtype)
        return jnp.einsum('bhqk,bkhd->bqhd', p, v)

    out = lax.map(one_block, (qb, jnp.arange(nb)))
    return out.transpose(1, 0, 2, 3, 4).reshape(b, s, MLA_HEADS * MLA_V)


def short_gated_conv(gate_b, gate_c, u, conv_w):
    z = gate_c * u
    y = lax.conv_general_dilated(
        z, conv_w[:, None, :].astype(z.dtype), window_strides=(1,),
        padding=[(CONV_WIDTH - 1, 0)], dimension_numbers=('NWC', 'WIO', 'NWC'),
        feature_group_count=CONV_CH)
    return gate_b * y


def multiscale_pool(u, pool_w, pool_scale):
    b, s, _ = u.shape
    uf = u.astype(jnp.float32)
    cs = jnp.cumsum(uf, axis=1)
    pos = jnp.arange(s)
    outs = []
    for g, w in enumerate(POOL_WINDOWS):
        cs_g = cs[:, :, g * POOL_CH:(g + 1) * POOL_CH]
        lag = jnp.pad(cs_g, ((0, 0), (w, 0), (0, 0)))[:, :s]
        count = jnp.minimum(pos + 1, w).astype(jnp.float32)[None, :, None]
        outs.append((cs_g - lag) / count)
    pooled = jnp.stack(outs, axis=2) - uf.reshape(b, s, POOL_GROUPS, POOL_CH)
    mixed = jnp.einsum('bsgc,gcd->bsgd', pooled.astype(u.dtype), pool_w)
    return mixed.reshape(b, s, D_GROUP) * pool_scale


def swa_sink_attention(q, k, v, sinks, slopes):
    b, s, _, hd = q.shape
    grp = SWA_HEADS // SWA_KV_HEADS
    nb = s // BLOCK
    qb = q.reshape(b, nb, BLOCK, SWA_KV_HEADS, grp, hd)
    kb = k.reshape(b, nb, BLOCK, SWA_KV_HEADS, hd)
    vb = v.reshape(b, nb, BLOCK, SWA_KV_HEADS, hd)

    def with_prev(t):
        prev = jnp.pad(t, ((0, 0), (1, 0), (0, 0), (0, 0), (0, 0)))[:, :nb]
        return jnp.concatenate([prev, t], axis=2)

    kk, vv = with_prev(kb), with_prev(vb)
    sc = jnp.einsum('bnqkgd,bnskd->bnkgqs', qb, kk).astype(jnp.float32) * (1.0 / math.sqrt(hd))
    blk = jnp.arange(nb)[:, None] * BLOCK
    q_pos = blk + jnp.arange(BLOCK)[None, :]
    k_pos = blk - BLOCK + jnp.arange(2 * BLOCK)[None, :]
    dist = q_pos[:, :, None] - k_pos[:, None, :]
    valid = (dist >= 0) & (dist < SWA_WINDOW) & (k_pos[:, None, :] >= 0)
    sl = jnp.asarray(slopes).reshape(SWA_KV_HEADS, grp)
    bias = -sl[None, None, :, :, None, None] * dist.astype(jnp.float32)[None, :, None, None, :, :]
    sc = jnp.where(valid[None, :, None, None], sc + bias, -jnp.inf)
    sink = jnp.broadcast_to(
        sinks.astype(jnp.float32).reshape(SWA_KV_HEADS, grp)[None, None, :, :, None, None],
        sc.shape[:-1] + (1,))
    p = jax.nn.softmax(jnp.concatenate([sc, sink], axis=-1), axis=-1)[..., :-1].astype(v.dtype)
    out = jnp.einsum('bnkgqs,bnskd->bnqkgd', p, vv)
    return out.reshape(b, s, SWA_HEADS * hd)


def _fwd_setup_inputs(seed: int = 0) -> dict:
    key = jax.random.key(seed)
    ks = jax.random.split(key, 17)
    f32 = jnp.float32

    def dense(k, shape, fan_in):
        return jax.random.normal(k, shape, f32) * fan_in ** -0.5

    def gain(k, shape):
        return 1.0 + 0.05 * jax.random.normal(k, shape, f32)

    return {
        "x": jax.random.normal(ks[0], (BATCH, SEQ, D_MODEL), f32),
        "attn_norm": gain(ks[1], (DEPTH, D_MODEL)),
        "w_in": dense(ks[2], (DEPTH, D_MODEL, D_IN), D_MODEL),
        "mla_q_norm": gain(ks[3], (DEPTH, MLA_Q_RANK)),
        "w_uq": dense(ks[4], (DEPTH, MLA_Q_RANK, MLA_HEADS * (MLA_NOPE + MLA_ROPE)), MLA_Q_RANK),
        "mla_kv_norm": gain(ks[5], (DEPTH, MLA_KV_RANK)),
        "w_ukv": dense(ks[6], (DEPTH, MLA_KV_RANK, MLA_HEADS * (MLA_NOPE + MLA_V)), MLA_KV_RANK),
        "conv_w": dense(ks[7], (DEPTH, CONV_WIDTH, CONV_CH), CONV_WIDTH),
        "pool_w": dense(ks[8], (DEPTH, POOL_GROUPS, POOL_CH, POOL_CH), POOL_CH),
        "pool_scale": gain(ks[9], (DEPTH, D_GROUP)),
        "swa_sinks": 0.5 * jax.random.normal(ks[10], (DEPTH, SWA_HEADS), f32),
        "mix_norm": gain(ks[11], (DEPTH, D_MIX)),
        "w_o": dense(ks[12], (DEPTH, D_MIX, D_MODEL), D_MIX),
        "ffn_norm": gain(ks[13], (DEPTH, D_MODEL)),
        "w_gate_up": dense(ks[14], (DEPTH, D_MODEL, 2 * D_FF), D_MODEL),
        "w_down": dense(ks[15], (DEPTH, D_FF, D_MODEL), D_FF),
        "final_norm": gain(ks[16], (D_MODEL,)),
    }


def _fwd_reference(x, attn_norm, w_in, mla_q_norm, w_uq, mla_kv_norm, w_ukv, conv_w, pool_w,
              pool_scale, swa_sinks, mix_norm, w_o, ffn_norm, w_gate_up, w_down, final_norm):
    b, s, _ = x.shape
    cos, sin = rope_tables(s, MLA_ROPE, x.dtype)
    slopes = _alibi_slopes(SWA_HEADS)
    pts = _split_points()
    for l in range(DEPTH):
        h = rmsnorm(x, attn_norm[l])
        proj = h @ w_in[l]
        (c_q, c_kv, k_r, g_b, g_c, u_conv, u_pool,
         q_sw, k_sw, v_sw) = jnp.split(proj, pts, axis=-1)
        y_a = mla_attention(c_q, c_kv, k_r, mla_q_norm[l], mla_kv_norm[l],
                            w_uq[l], w_ukv[l], cos, sin)
        y_b = short_gated_conv(g_b, g_c, u_conv, conv_w[l])
        y_c = multiscale_pool(u_pool, pool_w[l], pool_scale[l])
        y_d = swa_sink_attention(q_sw.reshape(b, s, SWA_HEADS, SWA_HEAD_DIM),
                                 k_sw.reshape(b, s, SWA_KV_HEADS, SWA_HEAD_DIM),
                                 v_sw.reshape(b, s, SWA_KV_HEADS, SWA_HEAD_DIM),
                                 swa_sinks[l], slopes)
        groups = jnp.stack([y_a, y_b, y_c, y_d], axis=2)
        gf = groups.astype(jnp.float32)
        gf = gf * lax.rsqrt(jnp.mean(gf * gf, axis=-1, keepdims=True) + RMS_EPS)
        mixed = (gf.reshape(b, s, D_MIX) * mix_norm[l].astype(jnp.float32)).astype(x.dtype)
        x = x + mixed @ w_o[l]
        h2 = rmsnorm(x, ffn_norm[l])
        gate, up = jnp.split(h2 @ w_gate_up[l], 2, axis=-1)
        x = x + (jax.nn.silu(gate) * up) @ w_down[l]
    return rmsnorm(x, final_norm)


import jax as _jax
import jax.numpy as _jnp

TWIN_FORMAT = 'train_step'
FWD_PARAMS = ['x', 'attn_norm', 'w_in', 'mla_q_norm', 'w_uq', 'mla_kv_norm', 'w_ukv', 'conv_w', 'pool_w', 'pool_scale', 'swa_sinks', 'mix_norm', 'w_o', 'ffn_norm', 'w_gate_up', 'w_down', 'final_norm']
TWIN_WEIGHTS = ['attn_norm', 'w_in', 'mla_q_norm', 'w_uq', 'mla_kv_norm', 'w_ukv', 'conv_w', 'pool_w', 'pool_scale', 'swa_sinks', 'mix_norm', 'w_o', 'ffn_norm', 'w_gate_up', 'w_down', 'final_norm']
TWIN_DIFF_INPUT = 'x'
TWIN_INPUTS = ['x', 'attn_norm', 'w_in', 'mla_q_norm', 'w_uq', 'mla_kv_norm', 'w_ukv', 'conv_w', 'pool_w', 'pool_scale', 'swa_sinks', 'mix_norm', 'w_o', 'ffn_norm', 'w_gate_up', 'w_down', 'final_norm', 'loss_target', 'm_attn_norm', 'm_w_in', 'm_mla_q_norm', 'm_w_uq', 'm_mla_kv_norm', 'm_w_ukv', 'm_conv_w', 'm_pool_w', 'm_pool_scale', 'm_swa_sinks', 'm_mix_norm', 'm_w_o', 'm_ffn_norm', 'm_w_gate_up', 'm_w_down', 'm_final_norm', 'v_attn_norm', 'v_w_in', 'v_mla_q_norm', 'v_w_uq', 'v_mla_kv_norm', 'v_w_ukv', 'v_conv_w', 'v_pool_w', 'v_pool_scale', 'v_swa_sinks', 'v_mix_norm', 'v_w_o', 'v_ffn_norm', 'v_w_gate_up', 'v_w_down', 'v_final_norm']
TWIN_OUTPUTS = ['loss', 'grad_x', 'grad_attn_norm', 'grad_w_in', 'grad_mla_q_norm', 'grad_w_uq', 'grad_mla_kv_norm', 'grad_w_ukv', 'grad_conv_w', 'grad_pool_w', 'grad_pool_scale', 'grad_swa_sinks', 'grad_mix_norm', 'grad_w_o', 'grad_ffn_norm', 'grad_w_gate_up', 'grad_w_down', 'grad_final_norm', 'delta_attn_norm', 'delta_w_in', 'delta_mla_q_norm', 'delta_w_uq', 'delta_mla_kv_norm', 'delta_w_ukv', 'delta_conv_w', 'delta_pool_w', 'delta_pool_scale', 'delta_swa_sinks', 'delta_mix_norm', 'delta_w_o', 'delta_ffn_norm', 'delta_w_gate_up', 'delta_w_down', 'delta_final_norm', 'new_m_attn_norm', 'new_m_w_in', 'new_m_mla_q_norm', 'new_m_w_uq', 'new_m_mla_kv_norm', 'new_m_w_ukv', 'new_m_conv_w', 'new_m_pool_w', 'new_m_pool_scale', 'new_m_swa_sinks', 'new_m_mix_norm', 'new_m_w_o', 'new_m_ffn_norm', 'new_m_w_gate_up', 'new_m_w_down', 'new_m_final_norm', 'new_v_attn_norm', 'new_v_w_in', 'new_v_mla_q_norm', 'new_v_w_uq', 'new_v_mla_kv_norm', 'new_v_w_ukv', 'new_v_conv_w', 'new_v_pool_w', 'new_v_pool_scale', 'new_v_swa_sinks', 'new_v_mix_norm', 'new_v_w_o', 'new_v_ffn_norm', 'new_v_w_gate_up', 'new_v_w_down', 'new_v_final_norm']
TWIN_LEAF_KINDS = {'loss': 'loss', 'grad_x': 'grad_x', 'grad_attn_norm': 'grad_w', 'grad_w_in': 'grad_w', 'grad_mla_q_norm': 'grad_w', 'grad_w_uq': 'grad_w', 'grad_mla_kv_norm': 'grad_w', 'grad_w_ukv': 'grad_w', 'grad_conv_w': 'grad_w', 'grad_pool_w': 'grad_w', 'grad_pool_scale': 'grad_w', 'grad_swa_sinks': 'grad_w', 'grad_mix_norm': 'grad_w', 'grad_w_o': 'grad_w', 'grad_ffn_norm': 'grad_w', 'grad_w_gate_up': 'grad_w', 'grad_w_down': 'grad_w', 'grad_final_norm': 'grad_w', 'delta_attn_norm': 'delta_w', 'delta_w_in': 'delta_w', 'delta_mla_q_norm': 'delta_w', 'delta_w_uq': 'delta_w', 'delta_mla_kv_norm': 'delta_w', 'delta_w_ukv': 'delta_w', 'delta_conv_w': 'delta_w', 'delta_pool_w': 'delta_w', 'delta_pool_scale': 'delta_w', 'delta_swa_sinks': 'delta_w', 'delta_mix_norm': 'delta_w', 'delta_w_o': 'delta_w', 'delta_ffn_norm': 'delta_w', 'delta_w_gate_up': 'delta_w', 'delta_w_down': 'delta_w', 'delta_final_norm': 'delta_w', 'new_m_attn_norm': 'new_m', 'new_m_w_in': 'new_m', 'new_m_mla_q_norm': 'new_m', 'new_m_w_uq': 'new_m', 'new_m_mla_kv_norm': 'new_m', 'new_m_w_ukv': 'new_m', 'new_m_conv_w': 'new_m', 'new_m_pool_w': 'new_m', 'new_m_pool_scale': 'new_m', 'new_m_swa_sinks': 'new_m', 'new_m_mix_norm': 'new_m', 'new_m_w_o': 'new_m', 'new_m_ffn_norm': 'new_m', 'new_m_w_gate_up': 'new_m', 'new_m_w_down': 'new_m', 'new_m_final_norm': 'new_m', 'new_v_attn_norm': 'new_v', 'new_v_w_in': 'new_v', 'new_v_mla_q_norm': 'new_v', 'new_v_w_uq': 'new_v', 'new_v_mla_kv_norm': 'new_v', 'new_v_w_ukv': 'new_v', 'new_v_conv_w': 'new_v', 'new_v_pool_w': 'new_v', 'new_v_pool_scale': 'new_v', 'new_v_swa_sinks': 'new_v', 'new_v_mix_norm': 'new_v', 'new_v_w_o': 'new_v', 'new_v_ffn_norm': 'new_v', 'new_v_w_gate_up': 'new_v', 'new_v_w_down': 'new_v', 'new_v_final_norm': 'new_v'}


def _forward(args):
    return _fwd_reference(*[args[k] for k in FWD_PARAMS])


def _output_shape():
    out = _jax.eval_shape(lambda: _forward(_fwd_setup_inputs(0)))
    return out.shape, out.dtype

N_MICROBATCH = 1
ADAM_LR = 0.001
ADAM_B1 = 0.9
ADAM_B2 = 0.999
ADAM_EPS = 1e-08
ADAM_WD = 0.01
ADAM_STEP = 10
PER_EXAMPLE_BATCH_AXIS = {'x': 0, 'loss_target': 0}
SHARED_INPUTS = []
_WEIGHT_DTYPES = {'attn_norm': _jnp.float32, 'w_in': _jnp.float32, 'mla_q_norm': _jnp.float32, 'w_uq': _jnp.float32, 'mla_kv_norm': _jnp.float32, 'w_ukv': _jnp.float32, 'conv_w': _jnp.float32, 'pool_w': _jnp.float32, 'pool_scale': _jnp.float32, 'swa_sinks': _jnp.float32, 'mix_norm': _jnp.float32, 'w_o': _jnp.float32, 'ffn_norm': _jnp.float32, 'w_gate_up': _jnp.float32, 'w_down': _jnp.float32, 'final_norm': _jnp.float32}
MOMENT_SCALE = {'attn_norm': 2.146362e-01, 'w_in': 1.431937e-01, 'mla_q_norm': 1.469700e-01, 'w_uq': 1.014621e-01, 'mla_kv_norm': 4.050055e-01, 'w_ukv': 1.452787e-01, 'conv_w': 1.348819e-01, 'pool_w': 1.328809e-01, 'pool_scale': 1.287423e-01, 'swa_sinks': 1.021694e-01, 'mix_norm': 1.380603e-01, 'w_o': 1.444320e-01, 'ffn_norm': 9.787500e-02, 'w_gate_up': 4.256658e-02, 'w_down': 6.937305e-02, 'final_norm': 3.208075e+01}


def _to_microbatches(a, axis):
    t = _jnp.moveaxis(a, axis, 0)
    t = t.reshape((N_MICROBATCH, t.shape[0] // N_MICROBATCH) + t.shape[1:])
    return _jnp.moveaxis(t, 1, axis + 1)


def setup_inputs(seed: int = 0) -> dict:
    inp = _fwd_setup_inputs(seed)
    key = _jax.random.fold_in(_jax.random.key(seed), 7919)
    shape, _ = _output_shape()
    out = dict(inp)
    out["loss_target"] = _jax.random.normal(_jax.random.fold_in(key, 0), shape, _jnp.float32)
    for i, name in enumerate(TWIN_WEIGHTS):
        w = inp[name].astype(_jnp.float32)
        if MOMENT_SCALE is None:
            s = _jnp.sqrt(_jnp.mean(_jnp.square(w)) + 1e-30)
        else:
            s = MOMENT_SCALE[name]
        km, kv = _jax.random.split(_jax.random.fold_in(key, i + 1))
        out[name] = w
        out["m_" + name] = s * _jax.random.normal(km, w.shape, _jnp.float32)
        out["v_" + name] = (s * s) * _jax.random.uniform(kv, w.shape, _jnp.float32, 0.5, 1.5)
    if N_MICROBATCH > 1:
        for name, axis in PER_EXAMPLE_BATCH_AXIS.items():
            out[name] = _to_microbatches(out[name], axis)
    return {'x': out['x'], 'attn_norm': out['attn_norm'], 'w_in': out['w_in'], 'mla_q_norm': out['mla_q_norm'], 'w_uq': out['w_uq'], 'mla_kv_norm': out['mla_kv_norm'], 'w_ukv': out['w_ukv'], 'conv_w': out['conv_w'], 'pool_w': out['pool_w'], 'pool_scale': out['pool_scale'], 'swa_sinks': out['swa_sinks'], 'mix_norm': out['mix_norm'], 'w_o': out['w_o'], 'ffn_norm': out['ffn_norm'], 'w_gate_up': out['w_gate_up'], 'w_down': out['w_down'], 'final_norm': out['final_norm'], 'loss_target': out['loss_target'], 'm_attn_norm': out['m_attn_norm'], 'm_w_in': out['m_w_in'], 'm_mla_q_norm': out['m_mla_q_norm'], 'm_w_uq': out['m_w_uq'], 'm_mla_kv_norm': out['m_mla_kv_norm'], 'm_w_ukv': out['m_w_ukv'], 'm_conv_w': out['m_conv_w'], 'm_pool_w': out['m_pool_w'], 'm_pool_scale': out['m_pool_scale'], 'm_swa_sinks': out['m_swa_sinks'], 'm_mix_norm': out['m_mix_norm'], 'm_w_o': out['m_w_o'], 'm_ffn_norm': out['m_ffn_norm'], 'm_w_gate_up': out['m_w_gate_up'], 'm_w_down': out['m_w_down'], 'm_final_norm': out['m_final_norm'], 'v_attn_norm': out['v_attn_norm'], 'v_w_in': out['v_w_in'], 'v_mla_q_norm': out['v_mla_q_norm'], 'v_w_uq': out['v_w_uq'], 'v_mla_kv_norm': out['v_mla_kv_norm'], 'v_w_ukv': out['v_w_ukv'], 'v_conv_w': out['v_conv_w'], 'v_pool_w': out['v_pool_w'], 'v_pool_scale': out['v_pool_scale'], 'v_swa_sinks': out['v_swa_sinks'], 'v_mix_norm': out['v_mix_norm'], 'v_w_o': out['v_w_o'], 'v_ffn_norm': out['v_ffn_norm'], 'v_w_gate_up': out['v_w_gate_up'], 'v_w_down': out['v_w_down'], 'v_final_norm': out['v_final_norm']}


def _loss(weights, diff, rest, loss_target):
    with _jax.named_scope("forward"):
        args = {**rest, TWIN_DIFF_INPUT: diff, **{k: w.astype(_WEIGHT_DTYPES[k]) for k, w in weights.items()}}
        y = _forward(args)
    with _jax.named_scope("loss_head"):
        err = _jnp.square(y.astype(_jnp.float32) - loss_target)
        return 0.5 * _jnp.sum(_jnp.mean(err, axis=-1)) if err.ndim else 0.5 * err


def _adamw(w, g, m, v):
    m = ADAM_B1 * m + (1.0 - ADAM_B1) * g
    v = ADAM_B2 * v + (1.0 - ADAM_B2) * _jnp.square(g)
    m_hat = m / (1.0 - ADAM_B1 ** ADAM_STEP)
    v_hat = v / (1.0 - ADAM_B2 ** ADAM_STEP)
    delta = -ADAM_LR * (m_hat / (_jnp.sqrt(v_hat) + ADAM_EPS) + ADAM_WD * w)
    return delta, m, v


def reference(x, attn_norm, w_in, mla_q_norm, w_uq, mla_kv_norm, w_ukv, conv_w, pool_w, pool_scale, swa_sinks, mix_norm, w_o, ffn_norm, w_gate_up, w_down, final_norm, loss_target, m_attn_norm, m_w_in, m_mla_q_norm, m_w_uq, m_mla_kv_norm, m_w_ukv, m_conv_w, m_pool_w, m_pool_scale, m_swa_sinks, m_mix_norm, m_w_o, m_ffn_norm, m_w_gate_up, m_w_down, m_final_norm, v_attn_norm, v_w_in, v_mla_q_norm, v_w_uq, v_mla_kv_norm, v_w_ukv, v_conv_w, v_pool_w, v_pool_scale, v_swa_sinks, v_mix_norm, v_w_o, v_ffn_norm, v_w_gate_up, v_w_down, v_final_norm):
    given = dict(x=x, attn_norm=attn_norm, w_in=w_in, mla_q_norm=mla_q_norm, w_uq=w_uq, mla_kv_norm=mla_kv_norm, w_ukv=w_ukv, conv_w=conv_w, pool_w=pool_w, pool_scale=pool_scale, swa_sinks=swa_sinks, mix_norm=mix_norm, w_o=w_o, ffn_norm=ffn_norm, w_gate_up=w_gate_up, w_down=w_down, final_norm=final_norm, loss_target=loss_target, m_attn_norm=m_attn_norm, m_w_in=m_w_in, m_mla_q_norm=m_mla_q_norm, m_w_uq=m_w_uq, m_mla_kv_norm=m_mla_kv_norm, m_w_ukv=m_w_ukv, m_conv_w=m_conv_w, m_pool_w=m_pool_w, m_pool_scale=m_pool_scale, m_swa_sinks=m_swa_sinks, m_mix_norm=m_mix_norm, m_w_o=m_w_o, m_ffn_norm=m_ffn_norm, m_w_gate_up=m_w_gate_up, m_w_down=m_w_down, m_final_norm=m_final_norm, v_attn_norm=v_attn_norm, v_w_in=v_w_in, v_mla_q_norm=v_mla_q_norm, v_w_uq=v_w_uq, v_mla_kv_norm=v_mla_kv_norm, v_w_ukv=v_w_ukv, v_conv_w=v_conv_w, v_pool_w=v_pool_w, v_pool_scale=v_pool_scale, v_swa_sinks=v_swa_sinks, v_mix_norm=v_mix_norm, v_w_o=v_w_o, v_ffn_norm=v_ffn_norm, v_w_gate_up=v_w_gate_up, v_w_down=v_w_down, v_final_norm=v_final_norm)
    weights = {n: given[n] for n in TWIN_WEIGHTS}
    shared = {n: given[n] for n in SHARED_INPUTS}
    per_example = {n: given[n] for n in ['x']}
    grad_fn = _jax.value_and_grad(_loss, argnums=(0, 1))

    def one_microbatch(ex, loss_target):
        ex = dict(ex)
        diff = ex.pop(TWIN_DIFF_INPUT)
        return grad_fn(weights, diff, {**shared, **ex}, loss_target)

    if N_MICROBATCH == 1:
        loss, (grad_w, grad_x) = one_microbatch(per_example, given["loss_target"])
    else:
        def body(carry, xs):
            loss_sum, grad_sum = carry
            l_k, (gw_k, gx_k) = one_microbatch(xs[0], xs[1])
            with _jax.named_scope("update"):
                return (loss_sum + l_k, _jax.tree.map(_jnp.add, grad_sum, gw_k)), gx_k

        init = (_jnp.zeros((), _jnp.float32), _jax.tree.map(_jnp.zeros_like, weights))
        (loss, grad_w), grad_x = _jax.lax.scan(body, init, (per_example, given["loss_target"]))
    with _jax.named_scope("update"):
        delta_w, new_m, new_v = {}, {}, {}
        for n in TWIN_WEIGHTS:
            delta_w[n], new_m[n], new_v[n] = _adamw(weights[n], grad_w[n], given["m_" + n], given["v_" + n])
    return (loss, grad_x, *[grad_w[n] for n in TWIN_WEIGHTS], *[delta_w[n] for n in TWIN_WEIGHTS],
            *[new_m[n] for n in TWIN_WEIGHTS], *[new_v[n] for n in TWIN_WEIGHTS])
```

```python
import functools
import math

import numpy as np
import jax
import jax.numpy as jnp
from jax import lax
from jax.experimental import pallas as pl
from jax.experimental.pallas import tpu as pltpu

F32 = jnp.float32
BF16 = jnp.bfloat16

D_MODEL = 1024
DEPTH = 2
D_GROUP = 256
MLA_HEADS = 4
MLA_NOPE = 64
MLA_ROPE = 32
ROPE_THETA = 10000.0
POOL_WINDOWS = (2, 4, 8, 16)
SWA_HEADS = 4
SWA_KV_HEADS = 2
SWA_WINDOW = 128
D_FF = 2816
RMS_EPS = 1e-6
LANES = 128
HEAD = 64
VMEM_LIMIT = 48 * 1024 * 1024
NEG = -1e30

ADAM_LR = 0.001
ADAM_B1 = 0.9
ADAM_B2 = 0.999
ADAM_EPS = 1e-08
ADAM_WD = 0.01
ADAM_STEP = 10

N_DEV = 8
PACK_COLS = 1024
PACK_ROW_TILE = 128

C_QSW, C_CQ, C_GB, C_GC, C_UCONV, C_UPOOL = (0, 512), (512, 256), (768, 256), (1024, 256), (1280, 256), (1536, 256)
C_KSW, C_VSW, C_CKV, C_KR, C_KRP = (1792, 256), (2048, 256), (2304, 128), (2432, 128), (2560, 128)
D_IN_EXT = 2688
D_MIX_EXT = 1536

SHARDED = (("w_in", (DEPTH, 1024, 244), 2), ("w_uq", (DEPTH, 256, 48), 2), ("w_ukv", (DEPTH, 128, 64), 2),
           ("conv_w", (DEPTH, 3, 32), 2), ("w_o", (DEPTH, 128, 1024), 1), ("w_gate_up", (DEPTH, 1024, 704), 2),
           ("w_down", (DEPTH, 352, 1024), 1))
REPLICATED = (("attn_norm", (DEPTH, 1024)), ("mla_q_norm", (DEPTH, 256)), ("mla_kv_norm", (DEPTH, 128)),
              ("pool_w", (DEPTH, 4, 64, 64)), ("pool_scale", (DEPTH, 256)), ("swa_sinks", (DEPTH, 4)),
              ("mix_norm", (DEPTH, 1024)), ("ffn_norm", (DEPTH, 1024)), ("final_norm", (1024,)), ("loss", (1,)))
WEIGHT_ORDER = ("attn_norm", "w_in", "mla_q_norm", "w_uq", "mla_kv_norm", "w_ukv", "conv_w", "pool_w", "pool_scale",
                "swa_sinks", "mix_norm", "w_o", "ffn_norm", "w_gate_up", "w_down", "final_norm")


def _params(sem):
    return pltpu.CompilerParams(dimension_semantics=sem, vmem_limit_bytes=VMEM_LIMIT)


def _pick(dim, target):
    if dim <= target:
        return dim
    best = None
    for t in range(LANES, target + 1, LANES):
        if dim % t == 0:
            best = t
    assert best is not None, (dim, target)
    return best


def _mm(a, b, *, name, ta=False, tb=False, res=None, out_dtype=F32, tm=512, tn=1024, tk=1024):
    m, k = (a.shape[1], a.shape[0]) if ta else a.shape
    n = b.shape[0] if tb else b.shape[1]
    assert (b.shape[1] if tb else b.shape[0]) == k
    tm, tn, tk = _pick(m, tm), _pick(n, tn), _pick(k, tk)
    nk = k // tk
    dims = (((0 if ta else 1,), (1 if tb else 0,)), ((), ()))
    has_res = res is not None

    def body(*refs):
        a_ref, b_ref = refs[0], refs[1]
        o_ref, acc_ref = refs[-2], refs[-1]
        kk = pl.program_id(2)

        @pl.when(kk == 0)
        def _():
            acc_ref[...] = jnp.zeros_like(acc_ref)

        acc_ref[...] += lax.dot_general(a_ref[...].astype(BF16), b_ref[...].astype(BF16), dims,
                                        preferred_element_type=F32)

        @pl.when(kk == nk - 1)
        def _():
            r = acc_ref[...]
            if has_res:
                r = r + refs[2][...]
            o_ref[...] = r.astype(o_ref.dtype)

    a_spec = pl.BlockSpec((tk, tm), lambda i, j, kk: (kk, i)) if ta else pl.BlockSpec((tm, tk), lambda i, j, kk: (i, kk))
    b_spec = pl.BlockSpec((tn, tk), lambda i, j, kk: (j, kk)) if tb else pl.BlockSpec((tk, tn), lambda i, j, kk: (kk, j))
    in_specs, args = [a_spec, b_spec], [a, b]
    if has_res:
        in_specs.append(pl.BlockSpec((tm, tn), lambda i, j, kk: (i, j)))
        args.append(res)
    return pl.pallas_call(
        body, name=name, grid=(m // tm, n // tn, nk), in_specs=in_specs,
        out_specs=pl.BlockSpec((tm, tn), lambda i, j, kk: (i, j)),
        out_shape=jax.ShapeDtypeStruct((m, n), out_dtype),
        scratch_shapes=[pltpu.VMEM((tm, tn), F32)],
        compiler_params=_params(("parallel", "parallel", "arbitrary")),
    )(*args)


def _rowwise(fn, *, name, rows, ins, outs, reds=(), tm=512):
    tm = min(tm, rows)
    assert rows % tm == 0
    n_in, n_out = len(ins), len(outs)

    def body(*refs):
        vals = [r[...] for r in refs[:n_in]]
        res = fn(*vals)
        for r, v in zip(refs[n_in:n_in + n_out], res[:n_out]):
            r[...] = v.astype(r.dtype)
        if reds:
            @pl.when(pl.program_id(0) == 0)
            def _():
                for r in refs[n_in + n_out:]:
                    r[...] = jnp.zeros_like(r)

            for r, v in zip(refs[n_in + n_out:], res[n_out:]):
                r[...] += v

    in_specs, args = [], []
    for spec in ins:
        if spec[0] == "row":
            _, arr, width, blk = spec
            in_specs.append(pl.BlockSpec((tm, width), functools.partial(lambda i, blk: (i, blk), blk=blk)))
        else:
            arr = spec[1]
            in_specs.append(pl.BlockSpec(arr.shape, functools.partial(lambda i, nd: (0,) * nd, nd=arr.ndim)))
        args.append(arr)
    out_specs = [pl.BlockSpec((tm, w), lambda i: (i, 0)) for w, _ in outs]
    out_shape = [jax.ShapeDtypeStruct((rows, w), dt) for w, dt in outs]
    out_specs += [pl.BlockSpec((r, w), lambda i: (0, 0)) for r, w in reds]
    out_shape += [jax.ShapeDtypeStruct((r, w), F32) for r, w in reds]
    return pl.pallas_call(body, name=name, grid=(rows // tm,), in_specs=in_specs, out_specs=out_specs,
                          out_shape=out_shape, compiler_params=_params(("arbitrary",)))(*args)


def _rstd(x, n):
    return lax.rsqrt(jnp.sum(x * x, axis=-1, keepdims=True) * (1.0 / n) + RMS_EPS)


def _rms_fwd(x_spec, g, *, name, rows, width):
    def fn(x, gv):
        return (x * _rstd(x, width) * gv,)
    return _rowwise(fn, name=name, rows=rows, ins=[x_spec, ("full", g)], outs=[(width, BF16)])[0]


def _rms_bwd(x_spec, g, dy, res, *, name, rows, width, out_dtype):
    def fn(x, gv, dyv, *rest):
        r = _rstd(x, width)
        dyg = dyv * gv
        dx = r * dyg - x * (r * r * r) * (jnp.sum(dyg * x, axis=-1, keepdims=True) * (1.0 / width))
        if rest:
            dx = dx + rest[0]
        return dx, jnp.sum(dyv * x * r, axis=0, keepdims=True)

    ins = [x_spec, ("full", g), ("row", dy, width, 0)]
    if res is not None:
        ins.append(("row", res, width, 0))
    return _rowwise(fn, name=name, rows=rows, ins=ins, outs=[(width, out_dtype)], reds=[(1, width)])


def _attn_cfg(s, mla):
    t = min(512 if mla else 256, s)
    nq = s // t
    steps = nq if mla else 2
    return t, nq, steps


def _scores(q, k, hp_ref, h, qpos0, kpos0, t, mla, scale, transposed=False):
    if transposed:
        s = lax.dot_general(k, q, (((1,), (1,)), ((), ())), preferred_element_type=F32)
        dist = (qpos0 + lax.broadcasted_iota(jnp.int32, (t, t), 1)) - (kpos0 + lax.broadcasted_iota(jnp.int32, (t, t), 0))
    else:
        s = lax.dot_general(q, k, (((1,), (1,)), ((), ())), preferred_element_type=F32)
        dist = (qpos0 + lax.broadcasted_iota(jnp.int32, (t, t), 0)) - (kpos0 + lax.broadcasted_iota(jnp.int32, (t, t), 1))
    if mla:
        mask = dist >= 0
    else:
        s = s * scale - hp_ref[h, 0] * dist.astype(F32)
        mask = (dist >= 0) & (dist < SWA_WINDOW)
    return jnp.where(mask, s, NEG), mask


def _flash_fwd(qa, q_off, ka, k_off, va, v_off, hp, *, name, s, nh, group, mla, scale):
    t, nq, steps = _attn_cfg(s, mla)

    def kblk(i, j):
        return j if mla else i - 1 + j

    def body(hp_ref, q_ref, k_ref, v_ref, o_ref, lse_ref, m_s, l_s, acc_s):
        h, i, j = pl.program_id(0), pl.program_id(1), pl.program_id(2)
        kb = kblk(i, j)

        @pl.when(j == 0)
        def _():
            if mla:
                m_s[...] = jnp.full_like(m_s, NEG)
                l_s[...] = jnp.zeros_like(l_s)
            else:
                m_s[...] = jnp.full_like(m_s, hp_ref[h, 1])
                l_s[...] = jnp.ones_like(l_s)
            acc_s[...] = jnp.zeros_like(acc_s)

        @pl.when((kb <= i) if mla else (kb >= 0))
        def _():
            sc, mask = _scores(q_ref[...].astype(BF16), k_ref[...].astype(BF16), hp_ref, h, i * t, kb * t, t, mla, scale)
            m_prev = m_s[...]
            m_new = jnp.maximum(m_prev, jnp.max(sc, axis=1, keepdims=True))
            p = jnp.where(mask, jnp.exp(sc - m_new), 0.0)
            alpha = jnp.exp(m_prev - m_new)
            l_s[...] = alpha * l_s[...] + jnp.sum(p, axis=1, keepdims=True)
            acc_s[...] = alpha * acc_s[...] + jnp.dot(p.astype(BF16), v_ref[...].astype(BF16), preferred_element_type=F32)
            m_s[...] = m_new

        @pl.when(j == steps - 1)
        def _():
            o_ref[...] = acc_s[...] / l_s[...]
            lse_ref[...] = m_s[...] + jnp.log(l_s[...])

    def kv_map(off):
        if mla:
            return lambda h, i, j: (jnp.minimum(j, i), off + h // group)
        return lambda h, i, j: (jnp.maximum(i - 1 + j, 0), off + h // group)

    return pl.pallas_call(
        body, name=name, grid=(nh, nq, steps),
        in_specs=[pl.BlockSpec(memory_space=pltpu.SMEM),
                  pl.BlockSpec((t, LANES), lambda h, i, j: (i, q_off + h)),
                  pl.BlockSpec((t, LANES), kv_map(k_off)),
                  pl.BlockSpec((t, LANES), kv_map(v_off))],
        out_specs=[pl.BlockSpec((t, LANES), lambda h, i, j: (i, h)),
                   pl.BlockSpec((None, t, 1), lambda h, i, j: (h, i, 0))],
        out_shape=[jax.ShapeDtypeStruct((s, nh * LANES), F32), jax.ShapeDtypeStruct((nh, s, 1), F32)],
        scratch_shapes=[pltpu.VMEM((t, 1), F32), pltpu.VMEM((t, 1), F32), pltpu.VMEM((t, LANES), F32)],
        compiler_params=_params(("parallel", "parallel", "arbitrary")),
    )(hp, qa, ka, va)


def _flash_bwd_dq(qa, q_off, ka, k_off, va, v_off, o, do, lse, hp, *, name, s, nh, group, mla, scale, out_dtype):
    t, nq, steps = _attn_cfg(s, mla)

    def body(hp_ref, q_ref, k_ref, v_ref, o_ref, do_ref, lse_ref, dq_ref, d_ref, dsink_ref, acc_s, d_s):
        h, i, j = pl.program_id(0), pl.program_id(1), pl.program_id(2)
        kb = j if mla else i - 1 + j

        @pl.when(j == 0)
        def _():
            acc_s[...] = jnp.zeros_like(acc_s)
            d = jnp.sum(do_ref[...] * o_ref[...], axis=1, keepdims=True)
            d_s[...] = d
            d_ref[...] = d

            @pl.when(i == 0)
            def _():
                dsink_ref[...] = jnp.zeros_like(dsink_ref)

            if not mla:
                part = jnp.sum(-jnp.exp(hp_ref[h, 1] - lse_ref[...]) * d, axis=0, keepdims=True)
                dsink_ref[...] += jnp.broadcast_to(part, (1, LANES))

        @pl.when((kb <= i) if mla else (kb >= 0))
        def _():
            k = k_ref[...].astype(BF16)
            sc, mask = _scores(q_ref[...].astype(BF16), k, hp_ref, h, i * t, kb * t, t, mla, scale)
            p = jnp.where(mask, jnp.exp(sc - lse_ref[...]), 0.0)
            dp = lax.dot_general(do_ref[...].astype(BF16), v_ref[...].astype(BF16), (((1,), (1,)), ((), ())),
                                 preferred_element_type=F32)
            ds = p * (dp - d_s[...])
            acc_s[...] += jnp.dot(ds.astype(BF16), k, preferred_element_type=F32)

        @pl.when(j == steps - 1)
        def _():
            dq_ref[...] = (acc_s[...] * scale).astype(dq_ref.dtype)

    def kv_map(off):
        if mla:
            return lambda h, i, j: (jnp.minimum(j, i), off + h // group)
        return lambda h, i, j: (jnp.maximum(i - 1 + j, 0), off + h // group)

    q_map = lambda h, i, j: (i, h)
    return pl.pallas_call(
        body, name=name, grid=(nh, nq, steps),
        in_specs=[pl.BlockSpec(memory_space=pltpu.SMEM),
                  pl.BlockSpec((t, LANES), lambda h, i, j: (i, q_off + h)),
                  pl.BlockSpec((t, LANES), kv_map(k_off)),
                  pl.BlockSpec((t, LANES), kv_map(v_off)),
                  pl.BlockSpec((t, LANES), q_map), pl.BlockSpec((t, LANES), q_map),
                  pl.BlockSpec((None, t, 1), lambda h, i, j: (h, i, 0))],
        out_specs=[pl.BlockSpec((t, LANES), q_map),
                   pl.BlockSpec((None, t, 1), lambda h, i, j: (h, i, 0)),
                   pl.BlockSpec((None, 1, LANES), lambda h, i, j: (h, 0, 0))],
        out_shape=[jax.ShapeDtypeStruct((s, nh * LANES), out_dtype), jax.ShapeDtypeStruct((nh, s, 1), F32),
                   jax.ShapeDtypeStruct((nh, 1, LANES), F32)],
        scratch_shapes=[pltpu.VMEM((t, LANES), F32), pltpu.VMEM((t, 1), F32)],
        compiler_params=_params(("parallel", "arbitrary", "arbitrary")),
    )(hp, qa, ka, va, o, do, lse)


def _flash_bwd_dkv(qa, q_off, ka, k_off, va, v_off, do, lse_row, d_row, hp, *, name, s, nh, group, mla, scale,
                   dk_dtype, dv_dtype):
    t, nq, _ = _attn_cfg(s, mla)
    nkv = nh // group
    steps = nq if mla else 2 * group

    def head_qblk(kh, kb, j):
        if mla:
            return kh, j
        return kh * group + j // 2, kb + j % 2

    def body(hp_ref, q_ref, k_ref, v_ref, do_ref, lse_ref, d_ref, dk_ref, dv_ref, dk_s, dv_s):
        kh, kb, j = pl.program_id(0), pl.program_id(1), pl.program_id(2)
        h, qb = head_qblk(kh, kb, j)

        @pl.when(j == 0)
        def _():
            dk_s[...] = jnp.zeros_like(dk_s)
            dv_s[...] = jnp.zeros_like(dv_s)

        @pl.when((qb >= kb) if mla else (qb < nq))
        def _():
            q = q_ref[...].astype(BF16)
            do_b = do_ref[...].astype(BF16)
            sc, mask = _scores(q, k_ref[...].astype(BF16), hp_ref, h, qb * t, kb * t, t, mla, scale, transposed=True)
            p = jnp.where(mask, jnp.exp(sc - lse_ref[...]), 0.0)
            dv_s[...] += jnp.dot(p.astype(BF16), do_b, preferred_element_type=F32)
            dp = lax.dot_general(v_ref[...].astype(BF16), do_b, (((1,), (1,)), ((), ())), preferred_element_type=F32)
            ds = p * (dp - d_ref[...])
            dk_s[...] += jnp.dot(ds.astype(BF16), q, preferred_element_type=F32)

        @pl.when(j == steps - 1)
        def _():
            dk_ref[...] = (dk_s[...] * scale).astype(dk_ref.dtype)
            dv_ref[...] = dv_s[...].astype(dv_ref.dtype)

    def qmap(off):
        def f(kh, kb, j):
            h, qb = head_qblk(kh, kb, j)
            qb = jnp.maximum(qb, kb) if mla else jnp.minimum(qb, nq - 1)
            return qb, off + h
        return f

    def rowmap(kh, kb, j):
        h, qb = head_qblk(kh, kb, j)
        qb = jnp.maximum(qb, kb) if mla else jnp.minimum(qb, nq - 1)
        return h, 0, qb

    kvmap = lambda off: (lambda kh, kb, j: (kb, off + kh))
    return pl.pallas_call(
        body, name=name, grid=(nkv, nq, steps),
        in_specs=[pl.BlockSpec(memory_space=pltpu.SMEM),
                  pl.BlockSpec((t, LANES), qmap(q_off)),
                  pl.BlockSpec((t, LANES), kvmap(k_off)),
                  pl.BlockSpec((t, LANES), kvmap(v_off)),
                  pl.BlockSpec((t, LANES), qmap(0)),
                  pl.BlockSpec((None, 1, t), rowmap), pl.BlockSpec((None, 1, t), rowmap)],
        out_specs=[pl.BlockSpec((t, LANES), lambda kh, kb, j: (kb, kh)),
                   pl.BlockSpec((t, LANES), lambda kh, kb, j: (kb, kh))],
        out_shape=[jax.ShapeDtypeStruct((s, nkv * LANES), dk_dtype), jax.ShapeDtypeStruct((s, nkv * LANES), dv_dtype)],
        scratch_shapes=[pltpu.VMEM((t, LANES), F32), pltpu.VMEM((t, LANES), F32)],
        compiler_params=_params(("parallel", "parallel", "arbitrary")),
    )(hp, qa, ka, va, do, lse_row, d_row)


def _shift_down(z, k):
    rows = lax.broadcasted_iota(jnp.int32, z.shape, 0)
    return jnp.where(rows >= k, pltpu.roll(z, k, 0), 0.0)


def _shift_up(z, k):
    n = z.shape[0]
    rows = lax.broadcasted_iota(jnp.int32, z.shape, 0)
    return jnp.where(rows < n - k, pltpu.roll(z, n - k, 0), 0.0)


def _rows3(a, b, c):
    r = lax.broadcasted_iota(jnp.int32, (3, a.shape[1]), 0)
    return jnp.where(r == 0, a, jnp.where(r == 1, b, c))


def _col_spec(s, off):
    return pl.BlockSpec((s, LANES), functools.partial(lambda j, off: (0, off + j), off=off))


def _conv_fwd(proj, conv_w, *, name, s):
    def body(gb_ref, gc_ref, u_ref, w_ref, y_ref):
        w0, w1, w2 = w_ref[0:1, :], w_ref[1:2, :], w_ref[2:3, :]
        z = gc_ref[...] * u_ref[...]
        c = w2 * z + w1 * _shift_down(z, 1) + w0 * _shift_down(z, 2)
        y_ref[...] = gb_ref[...] * c

    return pl.pallas_call(
        body, name=name, grid=(2,),
        in_specs=[_col_spec(s, C_GB[0] // LANES), _col_spec(s, C_GC[0] // LANES), _col_spec(s, C_UCONV[0] // LANES),
                  pl.BlockSpec((3, LANES), lambda j: (0, j))],
        out_specs=_col_spec(s, 0), out_shape=jax.ShapeDtypeStruct((s, D_GROUP), F32),
        compiler_params=_params(("parallel",)),
    )(proj, proj, proj, conv_w)


def _conv_bwd(dy, proj, conv_w, *, name, s):
    def body(dy_ref, gb_ref, gc_ref, u_ref, w_ref, dgb_ref, dgc_ref, du_ref, dw_ref):
        w0, w1, w2 = w_ref[0:1, :], w_ref[1:2, :], w_ref[2:3, :]
        gc, u, dyv = gc_ref[...], u_ref[...], dy_ref[...]
        z = gc * u
        z1, z2 = _shift_down(z, 1), _shift_down(z, 2)
        c = w2 * z + w1 * z1 + w0 * z2
        dgb_ref[...] = (dyv * c).astype(dgb_ref.dtype)
        dc = dyv * gb_ref[...]
        dz = w2 * dc + w1 * _shift_up(dc, 1) + w0 * _shift_up(dc, 2)
        dgc_ref[...] = (dz * u).astype(dgc_ref.dtype)
        du_ref[...] = (dz * gc).astype(du_ref.dtype)
        dw_ref[...] = _rows3(jnp.sum(dc * z2, axis=0, keepdims=True), jnp.sum(dc * z1, axis=0, keepdims=True),
                             jnp.sum(dc * z, axis=0, keepdims=True))

    act = jax.ShapeDtypeStruct((s, D_GROUP), BF16)
    return pl.pallas_call(
        body, name=name, grid=(2,),
        in_specs=[_col_spec(s, 0), _col_spec(s, C_GB[0] // LANES), _col_spec(s, C_GC[0] // LANES),
                  _col_spec(s, C_UCONV[0] // LANES), pl.BlockSpec((3, LANES), lambda j: (0, j))],
        out_specs=[_col_spec(s, 0), _col_spec(s, 0), _col_spec(s, 0), pl.BlockSpec((3, LANES), lambda j: (0, j))],
        out_shape=[act, act, act, jax.ShapeDtypeStruct((3, D_GROUP), F32)],
        compiler_params=_params(("parallel",)),
    )(dy, proj, proj, proj, conv_w)


def _pool_select(j, lane, a2, a4, a8, a16):
    lo = lane < HEAD
    return jnp.where(j == 0, jnp.where(lo, a2, a4), jnp.where(lo, a8, a16))


def _pooled(u, j):
    s2 = u + _shift_down(u, 1)
    s4 = s2 + _shift_down(s2, 2)
    s8 = s4 + _shift_down(s4, 4)
    s16 = s8 + _shift_down(s8, 8)
    lane = lax.broadcasted_iota(jnp.int32, u.shape, 1)
    rows = lax.broadcasted_iota(jnp.int32, u.shape, 0)
    win = _pool_select(j, lane, 2, 4, 8, 16)
    count = jnp.minimum(rows + 1, win).astype(F32)
    return _pool_select(j, lane, s2, s4, s8, s16) / count - u, count


def _pool_fwd(proj, wbd, scale, *, name, s):
    def body(u_ref, w_ref, sc_ref, y_ref):
        pooled, _ = _pooled(u_ref[...], pl.program_id(0))
        y_ref[...] = jnp.dot(pooled.astype(BF16), w_ref[...].astype(BF16), preferred_element_type=F32) * sc_ref[...]

    return pl.pallas_call(
        body, name=name, grid=(2,),
        in_specs=[_col_spec(s, C_UPOOL[0] // LANES), pl.BlockSpec((None, LANES, LANES), lambda j: (j, 0, 0)),
                  pl.BlockSpec((1, LANES), lambda j: (0, j))],
        out_specs=_col_spec(s, 0), out_shape=jax.ShapeDtypeStruct((s, D_GROUP), F32),
        compiler_params=_params(("parallel",)),
    )(proj, wbd, scale)


def _pool_bwd(dy, proj, wbd, scale, *, name, s):
    def body(dy_ref, u_ref, w_ref, sc_ref, du_ref, dw_ref, dsc_ref):
        j = pl.program_id(0)
        pooled, count = _pooled(u_ref[...], j)
        pooled_b = pooled.astype(BF16)
        w_b = w_ref[...].astype(BF16)
        dyv = dy_ref[...]
        mixed = jnp.dot(pooled_b, w_b, preferred_element_type=F32)
        dsc_ref[...] = jnp.sum(dyv * mixed, axis=0, keepdims=True)
        dms = (dyv * sc_ref[...]).astype(BF16)
        dw_ref[...] = lax.dot_general(pooled_b, dms, (((0,), (0,)), ((), ())), preferred_element_type=F32)
        dpooled = lax.dot_general(dms, w_b, (((1,), (1,)), ((), ())), preferred_element_type=F32)
        r = dpooled / count
        a2 = r + _shift_up(r, 1)
        a4 = a2 + _shift_up(a2, 2)
        a8 = a4 + _shift_up(a4, 4)
        a16 = a8 + _shift_up(a8, 8)
        lane = lax.broadcasted_iota(jnp.int32, r.shape, 1)
        du_ref[...] = (_pool_select(j, lane, a2, a4, a8, a16) - dpooled).astype(du_ref.dtype)

    return pl.pallas_call(
        body, name=name, grid=(2,),
        in_specs=[_col_spec(s, 0), _col_spec(s, C_UPOOL[0] // LANES),
                  pl.BlockSpec((None, LANES, LANES), lambda j: (j, 0, 0)), pl.BlockSpec((1, LANES), lambda j: (0, j))],
        out_specs=[_col_spec(s, 0), pl.BlockSpec((None, LANES, LANES), lambda j: (j, 0, 0)),
                   pl.BlockSpec((1, LANES), lambda j: (0, j))],
        out_shape=[jax.ShapeDtypeStruct((s, D_GROUP), BF16), jax.ShapeDtypeStruct((2, LANES, LANES), F32),
                   jax.ShapeDtypeStruct((1, D_GROUP), F32)],
        compiler_params=_params(("parallel",)),
    )(dy, proj, wbd, scale)


def _mesh_pos():
    return lax.axis_index("x"), lax.axis_index("y"), lax.axis_index("c")


def _all_gather(xs, *, name):
    r, c = xs.shape

    def body(x_ref, out_ref, send_sems, recv_sems, local_sem):
        x, y, cc = _mesh_pos()
        me, sibling = (x, y, cc), (x, y, 1 - cc)
        chips = [(1 - x, y), (x, 1 - y), (1 - x, 1 - y)]

        def slot(px, py, pc):
            return out_ref.at[4 * px + 2 * py + pc]

        def copy(k, block, to, src=None):
            return pltpu.make_async_remote_copy(
                src_ref=slot(*block) if src is None else src, dst_ref=slot(*block), send_sem=send_sems.at[k],
                recv_sem=recv_sems.at[k], device_id=to, device_id_type=pl.DeviceIdType.MESH)

        mine = pltpu.make_async_copy(x_ref, slot(*me), local_sem)
        mine.start()
        first = [copy(0, me, sibling, src=x_ref)]
        first += [copy(1 + j, me, (*chip, cc), src=x_ref) for j, chip in enumerate(chips)]
        for cp in first:
            cp.start()
        passed = [copy(4 + j, (*chip, cc), sibling) for j, chip in enumerate(chips)]
        for j, chip in enumerate(chips):
            copy(1 + j, (*chip, cc), me).wait_recv()
            passed[j].start()
        copy(0, sibling, me).wait_recv()
        for j, chip in enumerate(chips):
            copy(4 + j, (*chip, 1 - cc), me).wait_recv()
        for cp in first + passed:
            cp.wait_send()
        mine.wait()

    return pl.pallas_call(
        body, name=name, out_shape=jax.ShapeDtypeStruct((N_DEV, r, c), xs.dtype),
        in_specs=[pl.BlockSpec(memory_space=pl.ANY)], out_specs=pl.BlockSpec(memory_space=pl.ANY),
        scratch_shapes=[pltpu.SemaphoreType.DMA((7,)), pltpu.SemaphoreType.DMA((7,)), pltpu.SemaphoreType.DMA],
    )(xs)


def _all_to_all(send, *, name):
    _, r, c = send.shape

    def body(s_ref, r_ref, send_sems, recv_sems, local_sem):
        x, y, cc = _mesh_pos()
        me = 4 * x + 2 * y + cc
        local = pltpu.make_async_copy(s_ref.at[me], r_ref.at[me], local_sem)
        local.start()
        copies = []
        for k in range(1, N_DEV):
            px = 1 - x if k & 4 else x
            py = 1 - y if k & 2 else y
            pc = 1 - cc if k & 1 else cc
            pid = 4 * px + 2 * py + pc
            copies.append((pltpu.make_async_remote_copy(
                src_ref=s_ref.at[pid], dst_ref=r_ref.at[me], send_sem=send_sems.at[k - 1], recv_sem=recv_sems.at[k - 1],
                device_id=(px, py, pc), device_id_type=pl.DeviceIdType.MESH),
                pltpu.make_async_remote_copy(
                src_ref=s_ref.at[pid], dst_ref=r_ref.at[pid], send_sem=send_sems.at[k - 1], recv_sem=recv_sems.at[k - 1],
                device_id=(px, py, pc), device_id_type=pl.DeviceIdType.MESH)))
        for out, _ in copies:
            out.start()
        for _, inc in copies:
            inc.wait_recv()
        for out, _ in copies:
            out.wait_send()
        local.wait()

    return pl.pallas_call(
        body, name=name, out_shape=jax.ShapeDtypeStruct(send.shape, send.dtype),
        in_specs=[pl.BlockSpec(memory_space=pl.ANY)], out_specs=pl.BlockSpec(memory_space=pl.ANY),
        scratch_shapes=[pltpu.SemaphoreType.DMA((7,)), pltpu.SemaphoreType.DMA((7,)), pltpu.SemaphoreType.DMA],
    )(send)


def _adamw(parts, w, m, v, *, name):
    rows, cols = w.shape
    tm = min(PACK_ROW_TILE, rows)

    def body(p_ref, w_ref, m_ref, v_ref, g_ref, d_ref, nm_ref, nv_ref):
        g = p_ref[0]
        for q in range(1, N_DEV):
            g = g + p_ref[q]
        mm = ADAM_B1 * m_ref[...] + (1.0 - ADAM_B1) * g
        vv = ADAM_B2 * v_ref[...] + (1.0 - ADAM_B2) * jnp.square(g)
        m_hat = mm / (1.0 - ADAM_B1 ** ADAM_STEP)
        v_hat = vv / (1.0 - ADAM_B2 ** ADAM_STEP)
        g_ref[...] = g
        d_ref[...] = -ADAM_LR * (m_hat / (jnp.sqrt(v_hat) + ADAM_EPS) + ADAM_WD * w_ref[...])
        nm_ref[...] = mm
        nv_ref[...] = vv

    spec = pl.BlockSpec((tm, cols), lambda i: (i, 0))
    out = jax.ShapeDtypeStruct((rows, cols), F32)
    return pl.pallas_call(
        body, name=name, grid=(rows // tm,),
        in_specs=[pl.BlockSpec((N_DEV, tm, cols), lambda i: (0, i, 0)), spec, spec, spec],
        out_specs=[spec] * 4, out_shape=[out] * 4, compiler_params=_params(("parallel",)),
    )(parts, w, m, v)


def _pack_rows(n):
    rows = -(-n // PACK_COLS)
    return -(-rows // PACK_ROW_TILE) * PACK_ROW_TILE if rows > PACK_ROW_TILE else -(-rows // 16) * 16


def _pack(arrs, lead=()):
    flat = jnp.concatenate([a.reshape(lead + (-1,)) for a in arrs], axis=-1)
    n = flat.shape[-1]
    rows = _pack_rows(n)
    flat = jnp.pad(flat, [(0, 0)] * len(lead) + [(0, rows * PACK_COLS - n)])
    return flat.reshape(lead + (rows, PACK_COLS))


def _unpack(packed, shapes, lead=()):
    flat = packed.reshape(lead + (-1,))
    out, off = [], 0
    for shp in shapes:
        n = int(np.prod(shp))
        out.append(flat[..., off:off + n].reshape(lead + tuple(shp)))
        off += n
    return out


def _shards_to_full(g, axis):
    g = jnp.moveaxis(g, 0, axis)
    shp = g.shape
    return g.reshape(shp[:axis] + (shp[axis] * shp[axis + 1],) + shp[axis + 2:])


def _full_to_shards(a, axis):
    shp = a.shape
    a = a.reshape(shp[:axis] + (N_DEV, shp[axis] // N_DEV) + shp[axis + 1:])
    return jnp.moveaxis(a, axis, 0)


def _zeros_like_cols(a, n):
    return jnp.zeros(a.shape[:-1] + (n,), a.dtype)


def _swap_cols(r, sign):
    half = MLA_ROPE // 2
    if sign > 0:
        return jnp.concatenate([-r[..., half:], r[..., :half]], axis=-1)
    return jnp.concatenate([r[..., half:], -r[..., :half]], axis=-1)


def _pad_heads(a, n):
    z = _zeros_like_cols(a, HEAD)
    return jnp.concatenate([p for h in range(n) for p in (a[..., h * HEAD:(h + 1) * HEAD], z)], axis=-1)


def _unpad_heads(a, n):
    return jnp.concatenate([a[..., h * LANES:h * LANES + HEAD] for h in range(n)], axis=-1)


def _ext_w_in(w):
    z = _zeros_like_cols(w, HEAD)
    kr = w[..., 384:416]
    krp = _swap_cols(kr, 1)
    return jnp.concatenate([
        _pad_heads(w[..., 1440:1696], 4), w[..., 0:256], w[..., 416:672], w[..., 672:928], w[..., 928:1184],
        w[..., 1184:1440], _pad_heads(w[..., 1696:1824], 2), _pad_heads(w[..., 1824:1952], 2), w[..., 256:384],
        z, kr, kr, z, krp, krp], axis=-1)


def _fold_w_in(e):
    kr = (e[..., 2496:2528] + e[..., 2528:2560]) + _swap_cols(e[..., 2624:2656] + e[..., 2656:2688], -1)
    return jnp.concatenate([
        e[..., 512:768], e[..., 2304:2432], kr, e[..., 768:1024], e[..., 1024:1280], e[..., 1280:1536],
        e[..., 1536:1792], _unpad_heads(e[..., 0:512], 4), _unpad_heads(e[..., 1792:2048], 2),
        _unpad_heads(e[..., 2048:2304], 2)], axis=-1)


def _ext_w_uq(w):
    parts = []
    for h in range(MLA_HEADS):
        rot = w[..., 96 * h + 64:96 * h + 96]
        parts += [w[..., 96 * h:96 * h + 64], rot, _swap_cols(rot, 1)]
    return jnp.concatenate(parts, axis=-1)


def _fold_w_uq(e):
    parts = []
    for h in range(MLA_HEADS):
        b = LANES * h
        parts += [e[..., b:b + 64], e[..., b + 64:b + 96] + _swap_cols(e[..., b + 96:b + 128], -1)]
    return jnp.concatenate(parts, axis=-1)


def _ext_w_ukv(w):
    k = jnp.concatenate([w[..., LANES * h:LANES * h + HEAD] for h in range(MLA_HEADS)], axis=-1)
    v = jnp.concatenate([w[..., LANES * h + HEAD:LANES * (h + 1)] for h in range(MLA_HEADS)], axis=-1)
    return jnp.concatenate([_pad_heads(k, 4), _pad_heads(v, 4)], axis=-1)


def _fold_w_ukv(e):
    parts = []
    for h in range(MLA_HEADS):
        parts += [e[..., LANES * h:LANES * h + HEAD], e[..., 512 + LANES * h:512 + LANES * h + HEAD]]
    return jnp.concatenate(parts, axis=-1)


def _ext_mix(a):
    return jnp.concatenate([_pad_heads(a[..., 0:256], 4), a[..., 256:768], _pad_heads(a[..., 768:1024], 4)], axis=-1)


def _fold_mix(e):
    return jnp.concatenate([_unpad_heads(e[..., 0:512], 4), e[..., 512:1024], _unpad_heads(e[..., 1024:1536], 4)], axis=-1)


def _rope_tables(s):
    inv = 1.0 / (ROPE_THETA ** (jnp.arange(0, MLA_ROPE, 2, dtype=F32) / MLA_ROPE))
    ang = jnp.arange(s, dtype=F32)[:, None] * inv[None, :]
    cos, sin = jnp.cos(ang), jnp.sin(ang)
    c32, s32 = jnp.concatenate([cos, cos], axis=1), jnp.concatenate([sin, sin], axis=1)
    zeros, ones = jnp.zeros((s, HEAD), F32), jnp.ones((s, HEAD), F32)
    tq = jnp.concatenate([ones, c32, s32], axis=1) * (1.0 / math.sqrt(MLA_NOPE + MLA_ROPE))
    return (jnp.tile(tq, (1, MLA_HEADS)), jnp.concatenate([zeros, c32, c32], axis=1),
            jnp.concatenate([zeros, s32, s32], axis=1))


def _gn(y):
    return y * _rstd(y, D_GROUP)


def _layer_fwd(x, w, tabs, l):
    s = x.shape[0]
    tq, tkc, tks = tabs
    n = lambda t: f"l{l}_{t}"
    h = _rms_fwd(("row", x, D_MODEL, 0), w["attn_norm"], name=n("attn_norm"), rows=s, width=D_MODEL)
    proj = _mm(h, w["w_in"], name=n("in_proj"))
    cqn = _rms_fwd(("row", proj, 256, C_CQ[0] // 256), w["mla_q_norm"], name=n("q_norm"), rows=s, width=256)
    ckvn = _rms_fwd(("row", proj, 128, C_CKV[0] // 128), w["mla_kv_norm"], name=n("kv_norm"), rows=s, width=128)
    q_ext = _mm(cqn, w["w_uq"], name=n("uq"))
    kv_ext = _mm(ckvn, w["w_ukv"], name=n("ukv"))

    def prep(qe, kvk, kvv, kr, krp, tqv, tc, ts):
        kb = kr * tc + krp * ts
        return qe * tqv, kvk + jnp.tile(kb, (1, MLA_HEADS)), kvv

    qm, km, vm = _rowwise(
        prep, name=n("mla_prep"), rows=s,
        ins=[("row", q_ext, 512, 0), ("row", kv_ext, 512, 0), ("row", kv_ext, 512, 1),
             ("row", proj, 128, C_KR[0] // 128), ("row", proj, 128, C_KRP[0] // 128),
             ("row", tq, 512, 0), ("row", tkc, 128, 0), ("row", tks, 128, 0)],
        outs=[(512, BF16)] * 3)
    y_a, lse_a = _flash_fwd(qm, 0, km, 0, vm, 0, w["hp_mla"], name=n("mla_fwd"), s=s, nh=MLA_HEADS, group=1,
                            mla=True, scale=1.0)
    y_b = _conv_fwd(proj, w["conv_w"], name=n("conv_fwd"), s=s)
    y_c = _pool_fwd(proj, w["pool_wbd"], w["pool_scale"], name=n("pool_fwd"), s=s)
    y_d, lse_d = _flash_fwd(proj, C_QSW[0] // LANES, proj, C_KSW[0] // LANES, proj, C_VSW[0] // LANES, w["hp_swa"],
                            name=n("swa_fwd"), s=s, nh=SWA_HEADS, group=SWA_HEADS // SWA_KV_HEADS, mla=False,
                            scale=1.0 / math.sqrt(HEAD))

    def mix(ya, yb, yc, yd, mn):
        return (jnp.concatenate([_gn(ya), _gn(yb), _gn(yc), _gn(yd)], axis=1) * mn,)

    mixed = _rowwise(mix, name=n("group_norm"), rows=s,
                     ins=[("row", y_a, 512, 0), ("row", y_b, 256, 0), ("row", y_c, 256, 0), ("row", y_d, 512, 0),
                          ("full", w["mix_norm"])], outs=[(D_MIX_EXT, BF16)])[0]
    x1 = _mm(mixed, w["w_o"], res=x, name=n("out_proj"))
    h2 = _rms_fwd(("row", x1, D_MODEL, 0), w["ffn_norm"], name=n("ffn_norm"), rows=s, width=D_MODEL)
    gu = _mm(h2, w["w_gate_up"], name=n("gate_up"))

    def swiglu(g, u):
        return (g * jax.nn.sigmoid(g) * u,)

    act = _rowwise(swiglu, name=n("swiglu"), rows=s, tm=256,
                   ins=[("row", gu, D_FF, 0), ("row", gu, D_FF, 1)], outs=[(D_FF, BF16)])[0]
    x2 = _mm(act, w["w_down"], res=x1, name=n("down"))
    saved = dict(x=x, h=h, proj=proj, cqn=cqn, ckvn=ckvn, qm=qm, km=km, vm=vm, y_a=y_a, lse_a=lse_a, y_b=y_b, y_c=y_c,
                 y_d=y_d, lse_d=lse_d, mixed=mixed, x1=x1, h2=h2, gu=gu, act=act)
    return x2, saved


def _layer_bwd(dx2, sv, w, tabs, l):
    s = dx2.shape[0]
    tq, tkc, tks = tabs
    n = lambda t: f"l{l}_{t}"
    g = {}
    dact = _mm(dx2, w["w_down"], tb=True, name=n("d_act"))
    g["w_down"] = _mm(sv["act"], dx2, ta=True, name=n("dw_down"))

    def swiglu_bwd(gt, u, da):
        sg = jax.nn.sigmoid(gt)
        return (jnp.concatenate([da * u * sg * (1.0 + gt * (1.0 - sg)), da * gt * sg], axis=1),)

    dgu = _rowwise(swiglu_bwd, name=n("swiglu_bwd"), rows=s, tm=256,
                   ins=[("row", sv["gu"], D_FF, 0), ("row", sv["gu"], D_FF, 1), ("row", dact, D_FF, 0)],
                   outs=[(2 * D_FF, BF16)])[0]
    dh2 = _mm(dgu, w["w_gate_up"], tb=True, name=n("d_h2"))
    g["w_gate_up"] = _mm(sv["h2"], dgu, ta=True, name=n("dw_gate_up"))
    dx1, g["ffn_norm"] = _rms_bwd(("row", sv["x1"], D_MODEL, 0), w["ffn_norm"], dh2, dx2, name=n("ffn_norm_bwd"),
                                  rows=s, width=D_MODEL, out_dtype=F32)
    dmixed = _mm(dx1, w["w_o"], tb=True, name=n("d_mixed"))
    g["w_o"] = _mm(sv["mixed"], dx1, ta=True, name=n("dw_o"))

    def mix_bwd(dm, ya, yb, yc, yd, mn):
        outs, dmn = [], []
        for y, lo, hi in ((ya, 0, 512), (yb, 512, 768), (yc, 768, 1024), (yd, 1024, 1536)):
            r = _rstd(y, D_GROUP)
            nrm = y * r
            dmg = dm[:, lo:hi]
            dn = dmg * mn[:, lo:hi]
            outs.append(r * (dn - nrm * (jnp.sum(dn * nrm, axis=-1, keepdims=True) * (1.0 / D_GROUP))))
            dmn.append(jnp.sum(dmg * nrm, axis=0, keepdims=True))
        return (*outs, jnp.concatenate(dmn, axis=1))

    dy_a, dy_b, dy_c, dy_d, g["mix_norm"] = _rowwise(
        mix_bwd, name=n("group_norm_bwd"), rows=s,
        ins=[("row", dmixed, D_MIX_EXT, 0), ("row", sv["y_a"], 512, 0), ("row", sv["y_b"], 256, 0),
             ("row", sv["y_c"], 256, 0), ("row", sv["y_d"], 512, 0), ("full", w["mix_norm"])],
        outs=[(512, F32), (256, F32), (256, F32), (512, F32)], reds=[(1, D_MIX_EXT)])

    proj = sv["proj"]
    swa = dict(s=s, nh=SWA_HEADS, group=SWA_HEADS // SWA_KV_HEADS, mla=False, scale=1.0 / math.sqrt(HEAD))
    offs = (proj, C_QSW[0] // LANES, proj, C_KSW[0] // LANES, proj, C_VSW[0] // LANES)
    dq_sw, dvec_d, dsink = _flash_bwd_dq(*offs, sv["y_d"], dy_d, sv["lse_d"], w["hp_swa"], name=n("swa_dq"),
                                         out_dtype=BF16, **swa)
    dk_sw, dv_sw = _flash_bwd_dkv(*offs, dy_d, sv["lse_d"].reshape(SWA_HEADS, 1, s), dvec_d.reshape(SWA_HEADS, 1, s),
                                  w["hp_swa"], name=n("swa_dkv"), dk_dtype=BF16, dv_dtype=BF16, **swa)
    g["swa_sinks"] = dsink[:, 0, 0]

    mla = dict(s=s, nh=MLA_HEADS, group=1, mla=True, scale=1.0)
    offs = (sv["qm"], 0, sv["km"], 0, sv["vm"], 0)
    dqm, dvec_a, _ = _flash_bwd_dq(*offs, sv["y_a"], dy_a, sv["lse_a"], w["hp_mla"], name=n("mla_dq"), out_dtype=F32, **mla)
    dkm, dvm = _flash_bwd_dkv(*offs, dy_a, sv["lse_a"].reshape(MLA_HEADS, 1, s), dvec_a.reshape(MLA_HEADS, 1, s),
                              w["hp_mla"], name=n("mla_dkv"), dk_dtype=F32, dv_dtype=BF16, **mla)

    def prep_bwd(dq, dk, tqv, tc, ts):
        dkb = dk[:, 0:128] + dk[:, 128:256] + dk[:, 256:384] + dk[:, 384:512]
        return dq * tqv, dk, dkb * tc, dkb * ts

    dq_ext, dkvk, dkr, dkrp = _rowwise(
        prep_bwd, name=n("mla_prep_bwd"), rows=s,
        ins=[("row", dqm, 512, 0), ("row", dkm, 512, 0), ("row", tq, 512, 0), ("row", tkc, 128, 0), ("row", tks, 128, 0)],
        outs=[(512, BF16), (512, BF16), (128, BF16), (128, BF16)])
    dkv_ext = jnp.concatenate([dkvk, dvm], axis=1)
    dcqn = _mm(dq_ext, w["w_uq"], tb=True, name=n("d_cqn"))
    g["w_uq"] = _mm(sv["cqn"], dq_ext, ta=True, name=n("dw_uq"))
    dckvn = _mm(dkv_ext, w["w_ukv"], tb=True, name=n("d_ckvn"))
    g["w_ukv"] = _mm(sv["ckvn"], dkv_ext, ta=True, name=n("dw_ukv"))
    dcq, g["mla_q_norm"] = _rms_bwd(("row", proj, 256, C_CQ[0] // 256), w["mla_q_norm"], dcqn, None,
                                    name=n("q_norm_bwd"), rows=s, width=256, out_dtype=BF16)
    dckv, g["mla_kv_norm"] = _rms_bwd(("row", proj, 128, C_CKV[0] // 128), w["mla_kv_norm"], dckvn, None,
                                      name=n("kv_norm_bwd"), rows=s, width=128, out_dtype=BF16)

    dgb, dgc, duc, g["conv_w"] = _conv_bwd(dy_b, proj, w["conv_w"], name=n("conv_bwd"), s=s)
    dup, g["pool_wbd"], g["pool_scale"] = _pool_bwd(dy_c, proj, w["pool_wbd"], w["pool_scale"], name=n("pool_bwd"), s=s)

    dproj = jnp.concatenate([dq_sw, dcq, dgb, dgc, duc, dup, dk_sw, dv_sw, dckv, dkr, dkrp], axis=1)
    dh = _mm(dproj, w["w_in"], tb=True, name=n("d_h"))
    g["w_in"] = _mm(sv["h"], dproj, ta=True, name=n("dw_in"))
    dx0, g["attn_norm"] = _rms_bwd(("row", sv["x"], D_MODEL, 0), w["attn_norm"], dh, dx1, name=n("attn_norm_bwd"),
                                   rows=s, width=D_MODEL, out_dtype=F32)
    return dx0, g


def _loss_head(x, target, g, *, s):
    def fn(xv, tv, gv):
        r = _rstd(xv, D_MODEL)
        e = xv * r * gv - tv
        part = jnp.sum(jnp.sum(e * e, axis=1, keepdims=True), axis=0, keepdims=True) * (0.5 / D_MODEL)
        dy = e * (1.0 / D_MODEL)
        dyg = dy * gv
        dx = r * dyg - xv * (r * r * r) * (jnp.sum(dyg * xv, axis=-1, keepdims=True) * (1.0 / D_MODEL))
        return dx, jnp.sum(dy * xv * r, axis=0, keepdims=True), jnp.broadcast_to(part, (1, LANES))

    return _rowwise(fn, name="loss_head", rows=s,
                    ins=[("row", x, D_MODEL, 0), ("row", target, D_MODEL, 0), ("full", g)],
                    outs=[(D_MODEL, F32)], reds=[(1, D_MODEL), (1, LANES)])


def _alibi_slopes(n):
    return np.asarray([2.0 ** (-8.0 * (i + 1) / n) for i in range(n)], dtype=np.float32)


def _layer_weights(full, rep, l):
    pw = rep["pool_w"][l]
    z = jnp.zeros((HEAD, HEAD), F32)
    wbd = jnp.stack([jnp.block([[pw[2 * j], z], [z, pw[2 * j + 1]]]) for j in range(2)])
    w_o = full["w_o"][l]
    w_o_ext = jnp.moveaxis(_ext_mix(jnp.moveaxis(w_o, 0, -1)), -1, 0)
    return dict(
        attn_norm=rep["attn_norm"][l][None], w_in=_ext_w_in(full["w_in"][l]), mla_q_norm=rep["mla_q_norm"][l][None],
        w_uq=_ext_w_uq(full["w_uq"][l]), mla_kv_norm=rep["mla_kv_norm"][l][None], w_ukv=_ext_w_ukv(full["w_ukv"][l]),
        conv_w=full["conv_w"][l], pool_wbd=wbd, pool_scale=rep["pool_scale"][l][None],
        mix_norm=_ext_mix(rep["mix_norm"][l])[None], w_o=w_o_ext, ffn_norm=rep["ffn_norm"][l][None],
        w_gate_up=full["w_gate_up"][l], w_down=full["w_down"][l],
        hp_mla=jnp.zeros((MLA_HEADS, 2), F32),
        hp_swa=jnp.stack([jnp.asarray(_alibi_slopes(SWA_HEADS)), rep["swa_sinks"][l]], axis=1))


def _local_step(x, target, full, rep):
    s = x.shape[0]
    tabs = _rope_tables(s)
    ws = [_layer_weights(full, rep, l) for l in range(DEPTH)]
    saved = []
    for l in range(DEPTH):
        x, sv = _layer_fwd(x, ws[l], tabs, l)
        saved.append(sv)
    dx, d_final, loss = _loss_head(x, target, rep["final_norm"][None], s=s)
    layer_grads = [None] * DEPTH
    for l in reversed(range(DEPTH)):
        dx, layer_grads[l] = _layer_bwd(dx, saved[l], ws[l], tabs, l)

    def stack(fn):
        return jnp.stack([fn(layer_grads[l]) for l in range(DEPTH)])

    grads = dict(
        attn_norm=stack(lambda g: g["attn_norm"][0]), w_in=stack(lambda g: _fold_w_in(g["w_in"])),
        mla_q_norm=stack(lambda g: g["mla_q_norm"][0]), w_uq=stack(lambda g: _fold_w_uq(g["w_uq"])),
        mla_kv_norm=stack(lambda g: g["mla_kv_norm"][0]), w_ukv=stack(lambda g: _fold_w_ukv(g["w_ukv"])),
        conv_w=stack(lambda g: g["conv_w"]),
        pool_w=stack(lambda g: jnp.stack([g["pool_wbd"][j // 2][HEAD * (j % 2):HEAD * (j % 2 + 1),
                                                                  HEAD * (j % 2):HEAD * (j % 2 + 1)] for j in range(4)])),
        pool_scale=stack(lambda g: g["pool_scale"][0]), swa_sinks=stack(lambda g: g["swa_sinks"]),
        mix_norm=stack(lambda g: _fold_mix(g["mix_norm"][0])),
        w_o=stack(lambda g: jnp.moveaxis(_fold_mix(jnp.moveaxis(g["w_o"], 0, -1)), -1, 0)),
        ffn_norm=stack(lambda g: g["ffn_norm"][0]), w_gate_up=stack(lambda g: g["w_gate_up"]),
        w_down=stack(lambda g: g["w_down"]), final_norm=d_final[0])
    return loss, dx, grads


def kernel(x, attn_norm, w_in, mla_q_norm, w_uq, mla_kv_norm, w_ukv, conv_w, pool_w, pool_scale, swa_sinks, mix_norm, w_o, ffn_norm, w_gate_up, w_down, final_norm, loss_target, m_attn_norm, m_w_in, m_mla_q_norm, m_w_uq, m_mla_kv_norm, m_w_ukv, m_conv_w, m_pool_w, m_pool_scale, m_swa_sinks, m_mix_norm, m_w_o, m_ffn_norm, m_w_gate_up, m_w_down, m_final_norm, v_attn_norm, v_w_in, v_mla_q_norm, v_w_uq, v_mla_kv_norm, v_w_ukv, v_conv_w, v_pool_w, v_pool_scale, v_swa_sinks, v_mix_norm, v_w_o, v_ffn_norm, v_w_gate_up, v_w_down, v_final_norm):
    given = dict(locals())
    sh_names = [nm for nm, _, _ in SHARDED]
    sh_shapes = [shp for _, shp, _ in SHARDED]
    sh_axis = {nm: ax for nm, _, ax in SHARDED}
    rep_names = [nm for nm, _ in REPLICATED]
    rep_shapes = [shp for _, shp in REPLICATED]

    def as_bf16(nm):
        return lax.bitcast_convert_type(given[nm], BF16) if nm == "conv_w" else given[nm].astype(BF16)

    gathered = _all_gather(_pack([as_bf16(nm) for nm in sh_names]), name="gather_weights")
    wire_shapes = [shp + (2,) if nm == "conv_w" else shp for nm, shp in zip(sh_names, sh_shapes)]
    full = dict(zip(sh_names, _unpack(gathered, wire_shapes, lead=(N_DEV,))))
    full["conv_w"] = lax.bitcast_convert_type(full["conv_w"], F32)
    full = {nm: _shards_to_full(a, sh_axis[nm]) for nm, a in full.items()}
    rep = {nm: given[nm] for nm in rep_names if nm != "loss"}

    loss, grad_x, grads = _local_step(x[0], loss_target[0], full, rep)

    send = _pack([_full_to_shards(grads[nm], sh_axis[nm]) for nm in sh_names], lead=(N_DEV,))
    recv = _all_to_all(send, name="exchange_grads")
    packs = [_pack([given[pre + nm] for nm in sh_names]) for pre in ("", "m_", "v_")]
    sh_out = [dict(zip(sh_names, _unpack(o, sh_shapes))) for o in _adamw(recv, *packs, name="adamw_sharded")]

    grads["loss"] = loss[0, :1]
    zero = jnp.zeros((1,), F32)
    parts = _all_gather(_pack([grads[nm] for nm in rep_names]), name="gather_small_grads")
    packs = [_pack([given.get(pre + nm, zero) for nm in rep_names]) for pre in ("", "m_", "v_")]
    rep_out = [dict(zip(rep_names, _unpack(o, rep_shapes))) for o in _adamw(parts, *packs, name="adamw_replicated")]

    def pick(i, nm):
        return sh_out[i][nm] if nm in sh_axis else rep_out[i][nm]

    out = [rep_out[0]["loss"][0], grad_x[None]]
    for i in range(4):
        out += [pick(i, nm) for nm in WEIGHT_ORDER]
    return tuple(out)
```

```python
import functools
import math

import numpy as np
import jax
import jax.numpy as jnp
from jax import lax
from jax.experimental import pallas as pl
from jax.experimental.pallas import tpu as pltpu

F32 = jnp.float32
BF16 = jnp.bfloat16

D_MODEL = 1024
DEPTH = 2
D_GROUP = 256
MLA_HEADS = 4
MLA_NOPE = 64
MLA_ROPE = 32
ROPE_THETA = 10000.0
POOL_WINDOWS = (2, 4, 8, 16)
SWA_HEADS = 4
SWA_KV_HEADS = 2
SWA_WINDOW = 128
D_FF = 2816
RMS_EPS = 1e-6
LANES = 128
HEAD = 64
VMEM_LIMIT = 48 * 1024 * 1024
NEG = -1e30

ADAM_LR = 0.001
ADAM_B1 = 0.9
ADAM_B2 = 0.999
ADAM_EPS = 1e-08
ADAM_WD = 0.01
ADAM_STEP = 10

N_DEV = 8
PACK_COLS = 1024

C_QSW, C_CQ, C_GB, C_GC, C_UCONV, C_UPOOL = (0, 512), (512, 256), (768, 256), (1024, 256), (1280, 256), (1536, 256)
C_KSW, C_VSW, C_CKV, C_KR, C_KRP = (1792, 256), (2048, 256), (2304, 128), (2432, 128), (2560, 128)
D_IN_EXT = 2688
D_MIX_EXT = 1536

SHARDED = (("w_in", (DEPTH, 1024, 244), 2), ("w_uq", (DEPTH, 256, 48), 2), ("w_ukv", (DEPTH, 128, 64), 2),
           ("conv_w", (DEPTH, 3, 32), 2), ("w_o", (DEPTH, 128, 1024), 1), ("w_gate_up", (DEPTH, 1024, 704), 2),
           ("w_down", (DEPTH, 352, 1024), 1))
REPLICATED = (("attn_norm", (DEPTH, 1024)), ("mla_q_norm", (DEPTH, 256)), ("mla_kv_norm", (DEPTH, 128)),
              ("pool_w", (DEPTH, 4, 64, 64)), ("pool_scale", (DEPTH, 256)), ("swa_sinks", (DEPTH, 4)),
              ("mix_norm", (DEPTH, 1024)), ("ffn_norm", (DEPTH, 1024)), ("final_norm", (1024,)), ("loss", (1,)))
WEIGHT_ORDER = ("attn_norm", "w_in", "mla_q_norm", "w_uq", "mla_kv_norm", "w_ukv", "conv_w", "pool_w", "pool_scale",
                "swa_sinks", "mix_norm", "w_o", "ffn_norm", "w_gate_up", "w_down", "final_norm")


def _params(sem):
    return pltpu.CompilerParams(dimension_semantics=sem, vmem_limit_bytes=VMEM_LIMIT)


def _pick(dim, target):
    if dim <= target:
        return dim
    best = None
    for t in range(LANES, target + 1, LANES):
        if dim % t == 0:
            best = t
    assert best is not None, (dim, target)
    return best


def _mm(a, b, *, name, ta=False, tb=False, res=None, out_dtype=F32, tm=512, tn=1024, tk=1024):
    m, k = (a.shape[1], a.shape[0]) if ta else a.shape
    n = b.shape[0] if tb else b.shape[1]
    assert (b.shape[1] if tb else b.shape[0]) == k
    tm, tn, tk = _pick(m, tm), _pick(n, tn), _pick(k, tk)
    nk = k // tk
    dims = (((0 if ta else 1,), (1 if tb else 0,)), ((), ()))
    has_res = res is not None

    def body(*refs):
        a_ref, b_ref = refs[0], refs[1]
        o_ref, acc_ref = refs[-2], refs[-1]
        kk = pl.program_id(2)

        @pl.when(kk == 0)
        def _():
            acc_ref[...] = jnp.zeros_like(acc_ref)

        acc_ref[...] += lax.dot_general(a_ref[...].astype(BF16), b_ref[...].astype(BF16), dims,
                                        preferred_element_type=F32)

        @pl.when(kk == nk - 1)
        def _():
            r = acc_ref[...]
            if has_res:
                r = r + refs[2][...]
            o_ref[...] = r.astype(o_ref.dtype)

    a_spec = pl.BlockSpec((tk, tm), lambda i, j, kk: (kk, i)) if ta else pl.BlockSpec((tm, tk), lambda i, j, kk: (i, kk))
    b_spec = pl.BlockSpec((tn, tk), lambda i, j, kk: (j, kk)) if tb else pl.BlockSpec((tk, tn), lambda i, j, kk: (kk, j))
    in_specs, args = [a_spec, b_spec], [a, b]
    if has_res:
        in_specs.append(pl.BlockSpec((tm, tn), lambda i, j, kk: (i, j)))
        args.append(res)
    return pl.pallas_call(
        body, name=name, grid=(m // tm, n // tn, nk), in_specs=in_specs,
        out_specs=pl.BlockSpec((tm, tn), lambda i, j, kk: (i, j)),
        out_shape=jax.ShapeDtypeStruct((m, n), out_dtype),
        scratch_shapes=[pltpu.VMEM((tm, tn), F32)],
        compiler_params=_params(("parallel", "parallel", "arbitrary")),
    )(*args)


def _rowwise(fn, *, name, rows, ins, outs, reds=(), tm=512):
    tm = min(tm, rows)
    assert rows % tm == 0
    n_in, n_out = len(ins), len(outs)

    def body(*refs):
        vals = [r[...] for r in refs[:n_in]]
        res = fn(*vals)
        for r, v in zip(refs[n_in:n_in + n_out], res[:n_out]):
            r[...] = v.astype(r.dtype)
        if reds:
            @pl.when(pl.program_id(0) == 0)
            def _():
                for r in refs[n_in + n_out:]:
                    r[...] = jnp.zeros_like(r)

            for r, v in zip(refs[n_in + n_out:], res[n_out:]):
                r[...] += v

    in_specs, args = [], []
    for spec in ins:
        if spec[0] == "row":
            _, arr, width, blk = spec
            in_specs.append(pl.BlockSpec((tm, width), functools.partial(lambda i, blk: (i, blk), blk=blk)))
        else:
            arr = spec[1]
            in_specs.append(pl.BlockSpec(arr.shape, functools.partial(lambda i, nd: (0,) * nd, nd=arr.ndim)))
        args.append(arr)
    out_specs = [pl.BlockSpec((tm, w), lambda i: (i, 0)) for w, _ in outs]
    out_shape = [jax.ShapeDtypeStruct((rows, w), dt) for w, dt in outs]
    out_specs += [pl.BlockSpec((r, w), lambda i: (0, 0)) for r, w in reds]
    out_shape += [jax.ShapeDtypeStruct((r, w), F32) for r, w in reds]
    return pl.pallas_call(body, name=name, grid=(rows // tm,), in_specs=in_specs, out_specs=out_specs,
                          out_shape=out_shape, compiler_params=_params(("arbitrary",)))(*args)


def _rstd(x, n):
    return lax.rsqrt(jnp.sum(x * x, axis=-1, keepdims=True) * (1.0 / n) + RMS_EPS)


def _rms_fwd(x_spec, g, *, name, rows, width):
    def fn(x, gv):
        return (x * _rstd(x, width) * gv,)
    return _rowwise(fn, name=name, rows=rows, ins=[x_spec, ("full", g)], outs=[(width, BF16)])[0]


def _rms_bwd(x_spec, g, dy, res, *, name, rows, width, out_dtype):
    def fn(x, gv, dyv, *rest):
        r = _rstd(x, width)
        dyg = dyv * gv
        dx = r * dyg - x * (r * r * r) * (jnp.sum(dyg * x, axis=-1, keepdims=True) * (1.0 / width))
        if rest:
            dx = dx + rest[0]
        return dx, jnp.sum(dyv * x * r, axis=0, keepdims=True)

    ins = [x_spec, ("full", g), ("row", dy, width, 0)]
    if res is not None:
        ins.append(("row", res, width, 0))
    return _rowwise(fn, name=name, rows=rows, ins=ins, outs=[(width, out_dtype)], reds=[(1, width)])


def _attn_cfg(s, mla):
    t = min(512 if mla else 256, s)
    nq = s // t
    steps = nq if mla else 2
    return t, nq, steps


def _scores(q, k, hp_ref, h, qpos0, kpos0, t, mla, scale, transposed=False):
    if transposed:
        s = lax.dot_general(k, q, (((1,), (1,)), ((), ())), preferred_element_type=F32)
        dist = (qpos0 + lax.broadcasted_iota(jnp.int32, (t, t), 1)) - (kpos0 + lax.broadcasted_iota(jnp.int32, (t, t), 0))
    else:
        s = lax.dot_general(q, k, (((1,), (1,)), ((), ())), preferred_element_type=F32)
        dist = (qpos0 + lax.broadcasted_iota(jnp.int32, (t, t), 0)) - (kpos0 + lax.broadcasted_iota(jnp.int32, (t, t), 1))
    if mla:
        mask = dist >= 0
    else:
        s = s * scale - hp_ref[h, 0] * dist.astype(F32)
        mask = (dist >= 0) & (dist < SWA_WINDOW)
    return jnp.where(mask, s, NEG), mask


def _flash_fwd(qa, q_off, ka, k_off, va, v_off, hp, *, name, s, nh, group, mla, scale):
    t, nq, steps = _attn_cfg(s, mla)

    def kblk(i, j):
        return j if mla else i - 1 + j

    def body(hp_ref, q_ref, k_ref, v_ref, o_ref, lse_ref, m_s, l_s, acc_s):
        h, i, j = pl.program_id(0), pl.program_id(1), pl.program_id(2)
        kb = kblk(i, j)

        @pl.when(j == 0)
        def _():
            if mla:
                m_s[...] = jnp.full_like(m_s, NEG)
                l_s[...] = jnp.zeros_like(l_s)
            else:
                m_s[...] = jnp.full_like(m_s, hp_ref[h, 1])
                l_s[...] = jnp.ones_like(l_s)
            acc_s[...] = jnp.zeros_like(acc_s)

        @pl.when((kb <= i) if mla else (kb >= 0))
        def _():
            sc, mask = _scores(q_ref[...].astype(BF16), k_ref[...].astype(BF16), hp_ref, h, i * t, kb * t, t, mla, scale)
            m_prev = m_s[...]
            m_new = jnp.maximum(m_prev, jnp.max(sc, axis=1, keepdims=True))
            p = jnp.where(mask, jnp.exp(sc - m_new), 0.0)
            alpha = jnp.exp(m_prev - m_new)
            l_s[...] = alpha * l_s[...] + jnp.sum(p, axis=1, keepdims=True)
            acc_s[...] = alpha * acc_s[...] + jnp.dot(p.astype(BF16), v_ref[...].astype(BF16), preferred_element_type=F32)
            m_s[...] = m_new

        @pl.when(j == steps - 1)
        def _():
            o_ref[...] = acc_s[...] / l_s[...]
            lse_ref[...] = m_s[...] + jnp.log(l_s[...])

    def kv_map(off):
        if mla:
            return lambda h, i, j: (jnp.minimum(j, i), off + h // group)
        return lambda h, i, j: (jnp.maximum(i - 1 + j, 0), off + h // group)

    return pl.pallas_call(
        body, name=name, grid=(nh, nq, steps),
        in_specs=[pl.BlockSpec(memory_space=pltpu.SMEM),
                  pl.BlockSpec((t, LANES), lambda h, i, j: (i, q_off + h)),
                  pl.BlockSpec((t, LANES), kv_map(k_off)),
                  pl.BlockSpec((t, LANES), kv_map(v_off))],
        out_specs=[pl.BlockSpec((t, LANES), lambda h, i, j: (i, h)),
                   pl.BlockSpec((None, t, 1), lambda h, i, j: (h, i, 0))],
        out_shape=[jax.ShapeDtypeStruct((s, nh * LANES), F32), jax.ShapeDtypeStruct((nh, s, 1), F32)],
        scratch_shapes=[pltpu.VMEM((t, 1), F32), pltpu.VMEM((t, 1), F32), pltpu.VMEM((t, LANES), F32)],
        compiler_params=_params(("parallel", "parallel", "arbitrary")),
    )(hp, qa, ka, va)


def _flash_bwd_dq(qa, q_off, ka, k_off, va, v_off, o, do, lse, hp, *, name, s, nh, group, mla, scale, out_dtype):
    t, nq, steps = _attn_cfg(s, mla)

    def body(hp_ref, q_ref, k_ref, v_ref, o_ref, do_ref, lse_ref, dq_ref, d_ref, dsink_ref, acc_s, d_s):
        h, i, j = pl.program_id(0), pl.program_id(1), pl.program_id(2)
        kb = j if mla else i - 1 + j

        @pl.when(j == 0)
        def _():
            acc_s[...] = jnp.zeros_like(acc_s)
            d = jnp.sum(do_ref[...] * o_ref[...], axis=1, keepdims=True)
            d_s[...] = d
            d_ref[...] = d

            @pl.when(i == 0)
            def _():
                dsink_ref[...] = jnp.zeros_like(dsink_ref)

            if not mla:
                part = jnp.sum(-jnp.exp(hp_ref[h, 1] - lse_ref[...]) * d, axis=0, keepdims=True)
                dsink_ref[...] += jnp.broadcast_to(part, (1, LANES))

        @pl.when((kb <= i) if mla else (kb >= 0))
        def _():
            k = k_ref[...].astype(BF16)
            sc, mask = _scores(q_ref[...].astype(BF16), k, hp_ref, h, i * t, kb * t, t, mla, scale)
            p = jnp.where(mask, jnp.exp(sc - lse_ref[...]), 0.0)
            dp = lax.dot_general(do_ref[...].astype(BF16), v_ref[...].astype(BF16), (((1,), (1,)), ((), ())),
                                 preferred_element_type=F32)
            ds = p * (dp - d_s[...])
            acc_s[...] += jnp.dot(ds.astype(BF16), k, preferred_element_type=F32)

        @pl.when(j == steps - 1)
        def _():
            dq_ref[...] = (acc_s[...] * scale).astype(dq_ref.dtype)

    def kv_map(off):
        if mla:
            return lambda h, i, j: (jnp.minimum(j, i), off + h // group)
        return lambda h, i, j: (jnp.maximum(i - 1 + j, 0), off + h // group)

    q_map = lambda h, i, j: (i, h)
    return pl.pallas_call(
        body, name=name, grid=(nh, nq, steps),
        in_specs=[pl.BlockSpec(memory_space=pltpu.SMEM),
                  pl.BlockSpec((t, LANES), lambda h, i, j: (i, q_off + h)),
                  pl.BlockSpec((t, LANES), kv_map(k_off)),
                  pl.BlockSpec((t, LANES), kv_map(v_off)),
                  pl.BlockSpec((t, LANES), q_map), pl.BlockSpec((t, LANES), q_map),
                  pl.BlockSpec((None, t, 1), lambda h, i, j: (h, i, 0))],
        out_specs=[pl.BlockSpec((t, LANES), q_map),
                   pl.BlockSpec((None, t, 1), lambda h, i, j: (h, i, 0)),
                   pl.BlockSpec((None, 1, LANES), lambda h, i, j: (h, 0, 0))],
        out_shape=[jax.ShapeDtypeStruct((s, nh * LANES), out_dtype), jax.ShapeDtypeStruct((nh, s, 1), F32),
                   jax.ShapeDtypeStruct((nh, 1, LANES), F32)],
        scratch_shapes=[pltpu.VMEM((t, LANES), F32), pltpu.VMEM((t, 1), F32)],
        compiler_params=_params(("parallel", "arbitrary", "arbitrary")),
    )(hp, qa, ka, va, o, do, lse)


def _flash_bwd_dkv(qa, q_off, ka, k_off, va, v_off, do, lse_row, d_row, hp, *, name, s, nh, group, mla, scale,
                   dk_dtype, dv_dtype):
    t, nq, _ = _attn_cfg(s, mla)
    nkv = nh // group
    steps = nq if mla else 2 * group

    def head_qblk(kh, kb, j):
        if mla:
            return kh, j
        return kh * group + j // 2, kb + j % 2

    def body(hp_ref, q_ref, k_ref, v_ref, do_ref, lse_ref, d_ref, dk_ref, dv_ref, dk_s, dv_s):
        kh, kb, j = pl.program_id(0), pl.program_id(1), pl.program_id(2)
        h, qb = head_qblk(kh, kb, j)

        @pl.when(j == 0)
        def _():
            dk_s[...] = jnp.zeros_like(dk_s)
            dv_s[...] = jnp.zeros_like(dv_s)

        @pl.when((qb >= kb) if mla else (qb < nq))
        def _():
            q = q_ref[...].astype(BF16)
            do_b = do_ref[...].astype(BF16)
            sc, mask = _scores(q, k_ref[...].astype(BF16), hp_ref, h, qb * t, kb * t, t, mla, scale, transposed=True)
            p = jnp.where(mask, jnp.exp(sc - lse_ref[...]), 0.0)
            dv_s[...] += jnp.dot(p.astype(BF16), do_b, preferred_element_type=F32)
            dp = lax.dot_general(v_ref[...].astype(BF16), do_b, (((1,), (1,)), ((), ())), preferred_element_type=F32)
            ds = p * (dp - d_ref[...])
            dk_s[...] += jnp.dot(ds.astype(BF16), q, preferred_element_type=F32)

        @pl.when(j == steps - 1)
        def _():
            dk_ref[...] = (dk_s[...] * scale).astype(dk_ref.dtype)
            dv_ref[...] = dv_s[...].astype(dv_ref.dtype)

    def qmap(off):
        def f(kh, kb, j):
            h, qb = head_qblk(kh, kb, j)
            qb = jnp.maximum(qb, kb) if mla else jnp.minimum(qb, nq - 1)
            return qb, off + h
        return f

    def rowmap(kh, kb, j):
        h, qb = head_qblk(kh, kb, j)
        qb = jnp.maximum(qb, kb) if mla else jnp.minimum(qb, nq - 1)
        return h, 0, qb

    kvmap = lambda off: (lambda kh, kb, j: (kb, off + kh))
    return pl.pallas_call(
        body, name=name, grid=(nkv, nq, steps),
        in_specs=[pl.BlockSpec(memory_space=pltpu.SMEM),
                  pl.BlockSpec((t, LANES), qmap(q_off)),
                  pl.BlockSpec((t, LANES), kvmap(k_off)),
                  pl.BlockSpec((t, LANES), kvmap(v_off)),
                  pl.BlockSpec((t, LANES), qmap(0)),
                  pl.BlockSpec((None, 1, t), rowmap), pl.BlockSpec((None, 1, t), rowmap)],
        out_specs=[pl.BlockSpec((t, LANES), lambda kh, kb, j: (kb, kh)),
                   pl.BlockSpec((t, LANES), lambda kh, kb, j: (kb, kh))],
        out_shape=[jax.ShapeDtypeStruct((s, nkv * LANES), dk_dtype), jax.ShapeDtypeStruct((s, nkv * LANES), dv_dtype)],
        scratch_shapes=[pltpu.VMEM((t, LANES), F32), pltpu.VMEM((t, LANES), F32)],
        compiler_params=_params(("parallel", "parallel", "arbitrary")),
    )(hp, qa, ka, va, do, lse_row, d_row)


def _shift_down(z, k):
    rows = lax.broadcasted_iota(jnp.int32, z.shape, 0)
    return jnp.where(rows >= k, pltpu.roll(z, k, 0), 0.0)


def _shift_up(z, k):
    n = z.shape[0]
    rows = lax.broadcasted_iota(jnp.int32, z.shape, 0)
    return jnp.where(rows < n - k, pltpu.roll(z, n - k, 0), 0.0)


def _rows3(a, b, c):
    r = lax.broadcasted_iota(jnp.int32, (3, a.shape[1]), 0)
    return jnp.where(r == 0, a, jnp.where(r == 1, b, c))


def _col_spec(s, off):
    return pl.BlockSpec((s, LANES), functools.partial(lambda j, off: (0, off + j), off=off))


def _conv_fwd(proj, conv_w, *, name, s):
    def body(gb_ref, gc_ref, u_ref, w_ref, y_ref):
        w0, w1, w2 = w_ref[0:1, :], w_ref[1:2, :], w_ref[2:3, :]
        z = gc_ref[...] * u_ref[...]
        c = w2 * z + w1 * _shift_down(z, 1) + w0 * _shift_down(z, 2)
        y_ref[...] = gb_ref[...] * c

    return pl.pallas_call(
        body, name=name, grid=(2,),
        in_specs=[_col_spec(s, C_GB[0] // LANES), _col_spec(s, C_GC[0] // LANES), _col_spec(s, C_UCONV[0] // LANES),
                  pl.BlockSpec((3, LANES), lambda j: (0, j))],
        out_specs=_col_spec(s, 0), out_shape=jax.ShapeDtypeStruct((s, D_GROUP), F32),
        compiler_params=_params(("parallel",)),
    )(proj, proj, proj, conv_w)


def _conv_bwd(dy, proj, conv_w, *, name, s):
    def body(dy_ref, gb_ref, gc_ref, u_ref, w_ref, dgb_ref, dgc_ref, du_ref, dw_ref):
        w0, w1, w2 = w_ref[0:1, :], w_ref[1:2, :], w_ref[2:3, :]
        gc, u, dyv = gc_ref[...], u_ref[...], dy_ref[...]
        z = gc * u
        z1, z2 = _shift_down(z, 1), _shift_down(z, 2)
        c = w2 * z + w1 * z1 + w0 * z2
        dgb_ref[...] = (dyv * c).astype(dgb_ref.dtype)
        dc = dyv * gb_ref[...]
        dz = w2 * dc + w1 * _shift_up(dc, 1) + w0 * _shift_up(dc, 2)
        dgc_ref[...] = (dz * u).astype(dgc_ref.dtype)
        du_ref[...] = (dz * gc).astype(du_ref.dtype)
        dw_ref[...] = _rows3(jnp.sum(dc * z2, axis=0, keepdims=True), jnp.sum(dc * z1, axis=0, keepdims=True),
                             jnp.sum(dc * z, axis=0, keepdims=True))

    act = jax.ShapeDtypeStruct((s, D_GROUP), BF16)
    return pl.pallas_call(
        body, name=name, grid=(2,),
        in_specs=[_col_spec(s, 0), _col_spec(s, C_GB[0] // LANES), _col_spec(s, C_GC[0] // LANES),
                  _col_spec(s, C_UCONV[0] // LANES), pl.BlockSpec((3, LANES), lambda j: (0, j))],
        out_specs=[_col_spec(s, 0), _col_spec(s, 0), _col_spec(s, 0), pl.BlockSpec((3, LANES), lambda j: (0, j))],
        out_shape=[act, act, act, jax.ShapeDtypeStruct((3, D_GROUP), F32)],
        compiler_params=_params(("parallel",)),
    )(dy, proj, proj, proj, conv_w)


def _pool_select(j, lane, a2, a4, a8, a16):
    lo = lane < HEAD
    return jnp.where(j == 0, jnp.where(lo, a2, a4), jnp.where(lo, a8, a16))


def _pooled(u, j):
    s2 = u + _shift_down(u, 1)
    s4 = s2 + _shift_down(s2, 2)
    s8 = s4 + _shift_down(s4, 4)
    s16 = s8 + _shift_down(s8, 8)
    lane = lax.broadcasted_iota(jnp.int32, u.shape, 1)
    rows = lax.broadcasted_iota(jnp.int32, u.shape, 0)
    win = _pool_select(j, lane, 2, 4, 8, 16)
    count = jnp.minimum(rows + 1, win).astype(F32)
    return _pool_select(j, lane, s2, s4, s8, s16) / count - u, count


def _pool_fwd(proj, wbd, scale, *, name, s):
    def body(u_ref, w_ref, sc_ref, y_ref):
        pooled, _ = _pooled(u_ref[...], pl.program_id(0))
        y_ref[...] = jnp.dot(pooled.astype(BF16), w_ref[...].astype(BF16), preferred_element_type=F32) * sc_ref[...]

    return pl.pallas_call(
        body, name=name, grid=(2,),
        in_specs=[_col_spec(s, C_UPOOL[0] // LANES), pl.BlockSpec((None, LANES, LANES), lambda j: (j, 0, 0)),
                  pl.BlockSpec((1, LANES), lambda j: (0, j))],
        out_specs=_col_spec(s, 0), out_shape=jax.ShapeDtypeStruct((s, D_GROUP), F32),
        compiler_params=_params(("parallel",)),
    )(proj, wbd, scale)


def _pool_bwd(dy, proj, wbd, scale, *, name, s):
    def body(dy_ref, u_ref, w_ref, sc_ref, du_ref, dw_ref, dsc_ref):
        j = pl.program_id(0)
        pooled, count = _pooled(u_ref[...], j)
        pooled_b = pooled.astype(BF16)
        w_b = w_ref[...].astype(BF16)
        dyv = dy_ref[...]
        mixed = jnp.dot(pooled_b, w_b, preferred_element_type=F32)
        dsc_ref[...] = jnp.sum(dyv * mixed, axis=0, keepdims=True)
        dms = (dyv * sc_ref[...]).astype(BF16)
        dw_ref[...] = lax.dot_general(pooled_b, dms, (((0,), (0,)), ((), ())), preferred_element_type=F32)
        dpooled = lax.dot_general(dms, w_b, (((1,), (1,)), ((), ())), preferred_element_type=F32)
        r = dpooled / count
        a2 = r + _shift_up(r, 1)
        a4 = a2 + _shift_up(a2, 2)
        a8 = a4 + _shift_up(a4, 4)
        a16 = a8 + _shift_up(a8, 8)
        lane = lax.broadcasted_iota(jnp.int32, r.shape, 1)
        du_ref[...] = (_pool_select(j, lane, a2, a4, a8, a16) - dpooled).astype(du_ref.dtype)

    return pl.pallas_call(
        body, name=name, grid=(2,),
        in_specs=[_col_spec(s, 0), _col_spec(s, C_UPOOL[0] // LANES),
                  pl.BlockSpec((None, LANES, LANES), lambda j: (j, 0, 0)), pl.BlockSpec((1, LANES), lambda j: (0, j))],
        out_specs=[_col_spec(s, 0), pl.BlockSpec((None, LANES, LANES), lambda j: (j, 0, 0)),
                   pl.BlockSpec((1, LANES), lambda j: (0, j))],
        out_shape=[jax.ShapeDtypeStruct((s, D_GROUP), BF16), jax.ShapeDtypeStruct((2, LANES, LANES), F32),
                   jax.ShapeDtypeStruct((1, D_GROUP), F32)],
        compiler_params=_params(("parallel",)),
    )(dy, proj, wbd, scale)


def _mesh_pos():
    return lax.axis_index("x"), lax.axis_index("y"), lax.axis_index("c")


def _any_specs(n):
    return [pl.BlockSpec(memory_space=pl.ANY)] * n


def _all_gather(xs, *, name):
    n = len(xs)

    def body(*refs):
        x_refs, out_refs = refs[:n], refs[n:2 * n]
        send_sems, recv_sems, local_sems = refs[2 * n:]
        x, y, cc = _mesh_pos()
        me, sibling = (x, y, cc), (x, y, 1 - cc)
        chips = [(1 - x, y), (x, 1 - y), (1 - x, 1 - y)]

        def slot(a, px, py, pc):
            return out_refs[a].at[4 * px + 2 * py + pc]

        def copy(a, k, block, to, src=None):
            return pltpu.make_async_remote_copy(
                src_ref=slot(a, *block) if src is None else src, dst_ref=slot(a, *block), send_sem=send_sems.at[a, k],
                recv_sem=recv_sems.at[a, k], device_id=to, device_id_type=pl.DeviceIdType.MESH)

        mine = [pltpu.make_async_copy(x_refs[a], slot(a, *me), local_sems.at[a]) for a in range(n)]
        first = []
        for a in range(n):
            first.append(copy(a, 0, me, sibling, src=x_refs[a]))
            first += [copy(a, 1 + j, me, (*chip, cc), src=x_refs[a]) for j, chip in enumerate(chips)]
        for cp in mine + first:
            cp.start()
        passed = []
        for j, chip in enumerate(chips):
            for a in range(n):
                copy(a, 1 + j, (*chip, cc), me).wait_recv()
                passed.append(copy(a, 4 + j, (*chip, cc), sibling))
                passed[-1].start()
        for a in range(n):
            copy(a, 0, sibling, me).wait_recv()
        for j, chip in enumerate(chips):
            for a in range(n):
                copy(a, 4 + j, (*chip, 1 - cc), me).wait_recv()
        for cp in first + passed:
            cp.wait_send()
        for cp in mine:
            cp.wait()

    return pl.pallas_call(
        body, name=name, out_shape=[jax.ShapeDtypeStruct((N_DEV,) + a.shape, a.dtype) for a in xs],
        in_specs=_any_specs(n), out_specs=_any_specs(n),
        scratch_shapes=[pltpu.SemaphoreType.DMA((n, 7)), pltpu.SemaphoreType.DMA((n, 7)), pltpu.SemaphoreType.DMA((n,))],
    )(*xs)


def _exchange_sibling(gs, *, name):
    n = len(gs)

    def body(*refs):
        g_refs, out_refs = refs[:n], refs[n:2 * n]
        send_sems, recv_sems = refs[2 * n:]
        x, y, cc = _mesh_pos()
        copies = [pltpu.make_async_remote_copy(
            src_ref=g_refs[a].at[1 - cc], dst_ref=out_refs[a], send_sem=send_sems.at[a], recv_sem=recv_sems.at[a],
            device_id=(x, y, 1 - cc), device_id_type=pl.DeviceIdType.MESH) for a in range(n)]
        for cp in copies:
            cp.start()
        for cp in copies:
            cp.wait()

    return pl.pallas_call(
        body, name=name, out_shape=[jax.ShapeDtypeStruct(g.shape[1:], g.dtype) for g in gs],
        in_specs=_any_specs(n), out_specs=_any_specs(n),
        scratch_shapes=[pltpu.SemaphoreType.DMA((n,)), pltpu.SemaphoreType.DMA((n,))],
    )(*gs)


def _row_tile(rows, target=512):
    if rows <= target:
        return rows
    best = None
    for t in range(8, target + 1, 8):
        if rows % t == 0:
            best = t
    assert best is not None, (rows, target)
    return best


def _add_own(g, other, core, *, name):
    _, _, rows, cols = g.shape
    tm = _row_tile(rows)

    def body(c_ref, g_ref, o_ref, out_ref):
        out_ref[...] = g_ref[...] + o_ref[...]

    return pl.pallas_call(
        body, name=name, out_shape=jax.ShapeDtypeStruct(other.shape, other.dtype),
        grid_spec=pltpu.PrefetchScalarGridSpec(
            num_scalar_prefetch=1, grid=(4, rows // tm),
            in_specs=[pl.BlockSpec((None, None, tm, cols), lambda p, i, c_ref: (c_ref[0], p, i, 0)),
                      pl.BlockSpec((None, tm, cols), lambda p, i, c_ref: (p, i, 0))],
            out_specs=pl.BlockSpec((None, tm, cols), lambda p, i, c_ref: (p, i, 0))),
        compiler_params=_params(("parallel", "parallel")),
    )(core, g, other)


def _exchange_chips(hs, *, name):
    n = len(hs)

    def body(*refs):
        h_refs, out_refs = refs[:n], refs[n:2 * n]
        send_sems, recv_sems, local_sems = refs[2 * n:]
        x, y, cc = _mesh_pos()
        my_chip = 2 * x + y
        chips = [(1 - x, y), (x, 1 - y), (1 - x, 1 - y)]
        local = [pltpu.make_async_copy(h_refs[a].at[my_chip], out_refs[a].at[my_chip], local_sems.at[a]) for a in range(n)]
        out, inc = [], []
        for a in range(n):
            for j, (px, py) in enumerate(chips):
                peer = 2 * px + py
                sems = dict(send_sem=send_sems.at[a, j], recv_sem=recv_sems.at[a, j], device_id=(px, py, cc),
                            device_id_type=pl.DeviceIdType.MESH)
                out.append(pltpu.make_async_remote_copy(src_ref=h_refs[a].at[peer], dst_ref=out_refs[a].at[my_chip], **sems))
                inc.append(pltpu.make_async_remote_copy(src_ref=h_refs[a].at[peer], dst_ref=out_refs[a].at[peer], **sems))
        for cp in local + out:
            cp.start()
        for cp in inc:
            cp.wait_recv()
        for cp in out:
            cp.wait_send()
        for cp in local:
            cp.wait()

    return pl.pallas_call(
        body, name=name, out_shape=[jax.ShapeDtypeStruct(h.shape, h.dtype) for h in hs],
        in_specs=_any_specs(n), out_specs=_any_specs(n),
        scratch_shapes=[pltpu.SemaphoreType.DMA((n, 3)), pltpu.SemaphoreType.DMA((n, 3)), pltpu.SemaphoreType.DMA((n,))],
    )(*hs)


def _adamw(parts, w, m, v, *, name):
    layers, rows, cols = w.shape
    assert len(parts) == layers
    tm = _row_tile(rows, 256)
    nr = rows // tm

    def body(*refs):
        p_refs = refs[:layers]
        w_ref, m_ref, v_ref, g_ref, d_ref, nm_ref, nv_ref, g_s = refs[layers:]
        for ll in range(layers):
            @pl.when(pl.program_id(0) == ll)
            def _(ll=ll):
                g = p_refs[ll][0]
                for q in range(1, p_refs[ll].shape[0]):
                    g = g + p_refs[ll][q]
                g_s[...] = g

        g = g_s[...]
        mm = ADAM_B1 * m_ref[...] + (1.0 - ADAM_B1) * g
        vv = ADAM_B2 * v_ref[...] + (1.0 - ADAM_B2) * jnp.square(g)
        m_hat = mm / (1.0 - ADAM_B1 ** ADAM_STEP)
        v_hat = vv / (1.0 - ADAM_B2 ** ADAM_STEP)
        g_ref[...] = g
        d_ref[...] = -ADAM_LR * (m_hat / (jnp.sqrt(v_hat) + ADAM_EPS) + ADAM_WD * w_ref[...])
        nm_ref[...] = mm
        nv_ref[...] = vv

    def part_spec(ll, p):
        return pl.BlockSpec((p, tm, cols), lambda l, i: (0, jnp.where(l == ll, i, jnp.where(l < ll, 0, nr - 1)), 0))

    spec = pl.BlockSpec((None, tm, cols), lambda l, i: (l, i, 0))
    out = jax.ShapeDtypeStruct(w.shape, F32)
    return pl.pallas_call(
        body, name=name, grid=(layers, nr),
        in_specs=[part_spec(ll, parts[ll].shape[0]) for ll in range(layers)] + [spec] * 3,
        out_specs=[spec] * 4, out_shape=[out] * 4, scratch_shapes=[pltpu.VMEM((tm, cols), F32)],
        compiler_params=_params(("arbitrary", "arbitrary")),
    )(*parts, w, m, v)


def _pack(arrs):
    flat = jnp.concatenate([a.reshape(-1) for a in arrs])
    rows = -(-flat.shape[0] // (PACK_COLS * 16)) * 16
    return jnp.pad(flat, (0, rows * PACK_COLS - flat.shape[0])).reshape(rows, PACK_COLS)


def _unpack(packed, shapes):
    flat = packed.reshape(-1)
    out, off = [], 0
    for shp in shapes:
        n = int(np.prod(shp))
        out.append(flat[off:off + n].reshape(shp))
        off += n
    return out


def _shards_to_full(g, axis):
    if axis == 0:
        return g.reshape(g.shape[0] * g.shape[1], g.shape[2])
    return jnp.transpose(g, (1, 0, 2)).reshape(g.shape[1], g.shape[0] * g.shape[2])


def _full_to_shards(a, axis):
    if axis == 0:
        return jnp.transpose(a.reshape(4, 2, a.shape[0] // N_DEV, a.shape[1]), (1, 0, 2, 3))
    return jnp.transpose(a.reshape(a.shape[0], 4, 2, a.shape[1] // N_DEV), (2, 1, 0, 3))


def _zeros_like_cols(a, n):
    return jnp.zeros(a.shape[:-1] + (n,), a.dtype)


def _swap_cols(r, sign):
    half = MLA_ROPE // 2
    if sign > 0:
        return jnp.concatenate([-r[..., half:], r[..., :half]], axis=-1)
    return jnp.concatenate([r[..., half:], -r[..., :half]], axis=-1)


def _pad_heads(a, n):
    z = _zeros_like_cols(a, HEAD)
    return jnp.concatenate([p for h in range(n) for p in (a[..., h * HEAD:(h + 1) * HEAD], z)], axis=-1)


def _unpad_heads(a, n):
    return jnp.concatenate([a[..., h * LANES:h * LANES + HEAD] for h in range(n)], axis=-1)


def _ext_w_in(w):
    z = _zeros_like_cols(w, HEAD)
    kr = w[..., 384:416]
    krp = _swap_cols(kr, 1)
    return jnp.concatenate([
        _pad_heads(w[..., 1440:1696], 4), w[..., 0:256], w[..., 416:672], w[..., 672:928], w[..., 928:1184],
        w[..., 1184:1440], _pad_heads(w[..., 1696:1824], 2), _pad_heads(w[..., 1824:1952], 2), w[..., 256:384],
        z, kr, kr, z, krp, krp], axis=-1)


def _fold_w_in(e):
    kr = (e[..., 2496:2528] + e[..., 2528:2560]) + _swap_cols(e[..., 2624:2656] + e[..., 2656:2688], -1)
    return jnp.concatenate([
        e[..., 512:768], e[..., 2304:2432], kr, e[..., 768:1024], e[..., 1024:1280], e[..., 1280:1536],
        e[..., 1536:1792], _unpad_heads(e[..., 0:512], 4), _unpad_heads(e[..., 1792:2048], 2),
        _unpad_heads(e[..., 2048:2304], 2)], axis=-1)


def _ext_w_uq(w):
    parts = []
    for h in range(MLA_HEADS):
        rot = w[..., 96 * h + 64:96 * h + 96]
        parts += [w[..., 96 * h:96 * h + 64], rot, _swap_cols(rot, 1)]
    return jnp.concatenate(parts, axis=-1)


def _fold_w_uq(e):
    parts = []
    for h in range(MLA_HEADS):
        b = LANES * h
        parts += [e[..., b:b + 64], e[..., b + 64:b + 96] + _swap_cols(e[..., b + 96:b + 128], -1)]
    return jnp.concatenate(parts, axis=-1)


def _ext_w_ukv(w):
    k = jnp.concatenate([w[..., LANES * h:LANES * h + HEAD] for h in range(MLA_HEADS)], axis=-1)
    v = jnp.concatenate([w[..., LANES * h + HEAD:LANES * (h + 1)] for h in range(MLA_HEADS)], axis=-1)
    return jnp.concatenate([_pad_heads(k, 4), _pad_heads(v, 4)], axis=-1)


def _fold_w_ukv(e):
    parts = []
    for h in range(MLA_HEADS):
        parts += [e[..., LANES * h:LANES * h + HEAD], e[..., 512 + LANES * h:512 + LANES * h + HEAD]]
    return jnp.concatenate(parts, axis=-1)


def _ext_mix(a):
    return jnp.concatenate([_pad_heads(a[..., 0:256], 4), a[..., 256:768], _pad_heads(a[..., 768:1024], 4)], axis=-1)


def _fold_mix(e):
    return jnp.concatenate([_unpad_heads(e[..., 0:512], 4), e[..., 512:1024], _unpad_heads(e[..., 1024:1536], 4)], axis=-1)


def _rope_tables(s):
    inv = 1.0 / (ROPE_THETA ** (jnp.arange(0, MLA_ROPE, 2, dtype=F32) / MLA_ROPE))
    ang = jnp.arange(s, dtype=F32)[:, None] * inv[None, :]
    cos, sin = jnp.cos(ang), jnp.sin(ang)
    c32, s32 = jnp.concatenate([cos, cos], axis=1), jnp.concatenate([sin, sin], axis=1)
    zeros, ones = jnp.zeros((s, HEAD), F32), jnp.ones((s, HEAD), F32)
    tq = jnp.concatenate([ones, c32, s32], axis=1) * (1.0 / math.sqrt(MLA_NOPE + MLA_ROPE))
    return (jnp.tile(tq, (1, MLA_HEADS)), jnp.concatenate([zeros, c32, c32], axis=1),
            jnp.concatenate([zeros, s32, s32], axis=1))


def _gn(y):
    return y * _rstd(y, D_GROUP)


def _layer_fwd(x, w, tabs, l):
    s = x.shape[0]
    tq, tkc, tks = tabs
    n = lambda t: f"l{l}_{t}"
    h = _rms_fwd(("row", x, D_MODEL, 0), w["attn_norm"], name=n("attn_norm"), rows=s, width=D_MODEL)
    proj = _mm(h, w["w_in"], name=n("in_proj"))
    cqn = _rms_fwd(("row", proj, 256, C_CQ[0] // 256), w["mla_q_norm"], name=n("q_norm"), rows=s, width=256)
    ckvn = _rms_fwd(("row", proj, 128, C_CKV[0] // 128), w["mla_kv_norm"], name=n("kv_norm"), rows=s, width=128)
    q_ext = _mm(cqn, w["w_uq"], name=n("uq"))
    kv_ext = _mm(ckvn, w["w_ukv"], name=n("ukv"))

    def prep(qe, kvk, kvv, kr, krp, tqv, tc, ts):
        kb = kr * tc + krp * ts
        return qe * tqv, kvk + jnp.tile(kb, (1, MLA_HEADS)), kvv

    qm, km, vm = _rowwise(
        prep, name=n("mla_prep"), rows=s,
        ins=[("row", q_ext, 512, 0), ("row", kv_ext, 512, 0), ("row", kv_ext, 512, 1),
             ("row", proj, 128, C_KR[0] // 128), ("row", proj, 128, C_KRP[0] // 128),
             ("row", tq, 512, 0), ("row", tkc, 128, 0), ("row", tks, 128, 0)],
        outs=[(512, BF16)] * 3)
    y_a, lse_a = _flash_fwd(qm, 0, km, 0, vm, 0, w["hp_mla"], name=n("mla_fwd"), s=s, nh=MLA_HEADS, group=1,
                            mla=True, scale=1.0)
    y_b = _conv_fwd(proj, w["conv_w"], name=n("conv_fwd"), s=s)
    y_c = _pool_fwd(proj, w["pool_wbd"], w["pool_scale"], name=n("pool_fwd"), s=s)
    y_d, lse_d = _flash_fwd(proj, C_QSW[0] // LANES, proj, C_KSW[0] // LANES, proj, C_VSW[0] // LANES, w["hp_swa"],
                            name=n("swa_fwd"), s=s, nh=SWA_HEADS, group=SWA_HEADS // SWA_KV_HEADS, mla=False,
                            scale=1.0 / math.sqrt(HEAD))

    def mix(ya, yb, yc, yd, mn):
        return (jnp.concatenate([_gn(ya), _gn(yb), _gn(yc), _gn(yd)], axis=1) * mn,)

    mixed = _rowwise(mix, name=n("group_norm"), rows=s,
                     ins=[("row", y_a, 512, 0), ("row", y_b, 256, 0), ("row", y_c, 256, 0), ("row", y_d, 512, 0),
                          ("full", w["mix_norm"])], outs=[(D_MIX_EXT, BF16)])[0]
    x1 = _mm(mixed, w["w_o"], res=x, name=n("out_proj"))
    h2 = _rms_fwd(("row", x1, D_MODEL, 0), w["ffn_norm"], name=n("ffn_norm"), rows=s, width=D_MODEL)
    gu = _mm(h2, w["w_gate_up"], name=n("gate_up"))

    def swiglu(g, u):
        return (g * jax.nn.sigmoid(g) * u,)

    act = _rowwise(swiglu, name=n("swiglu"), rows=s, tm=256,
                   ins=[("row", gu, D_FF, 0), ("row", gu, D_FF, 1)], outs=[(D_FF, BF16)])[0]
    x2 = _mm(act, w["w_down"], res=x1, name=n("down"))
    saved = dict(x=x, h=h, proj=proj, cqn=cqn, ckvn=ckvn, qm=qm, km=km, vm=vm, y_a=y_a, lse_a=lse_a, y_b=y_b, y_c=y_c,
                 y_d=y_d, lse_d=lse_d, mixed=mixed, x1=x1, h2=h2, gu=gu, act=act)
    return x2, saved


def _layer_bwd(dx2, sv, w, tabs, l):
    s = dx2.shape[0]
    tq, tkc, tks = tabs
    n = lambda t: f"l{l}_{t}"
    g = {}
    dact = _mm(dx2, w["w_down"], tb=True, name=n("d_act"))
    g["w_down"] = _mm(sv["act"], dx2, ta=True, name=n("dw_down"))

    def swiglu_bwd(gt, u, da):
        sg = jax.nn.sigmoid(gt)
        return (jnp.concatenate([da * u * sg * (1.0 + gt * (1.0 - sg)), da * gt * sg], axis=1),)

    dgu = _rowwise(swiglu_bwd, name=n("swiglu_bwd"), rows=s, tm=256,
                   ins=[("row", sv["gu"], D_FF, 0), ("row", sv["gu"], D_FF, 1), ("row", dact, D_FF, 0)],
                   outs=[(2 * D_FF, BF16)])[0]
    dh2 = _mm(dgu, w["w_gate_up"], tb=True, name=n("d_h2"))
    g["w_gate_up"] = _mm(sv["h2"], dgu, ta=True, name=n("dw_gate_up"))
    dx1, g["ffn_norm"] = _rms_bwd(("row", sv["x1"], D_MODEL, 0), w["ffn_norm"], dh2, dx2, name=n("ffn_norm_bwd"),
                                  rows=s, width=D_MODEL, out_dtype=F32)
    dmixed = _mm(dx1, w["w_o"], tb=True, name=n("d_mixed"))
    g["w_o"] = _mm(sv["mixed"], dx1, ta=True, name=n("dw_o"))

    def mix_bwd(dm, ya, yb, yc, yd, mn):
        outs, dmn = [], []
        for y, lo, hi in ((ya, 0, 512), (yb, 512, 768), (yc, 768, 1024), (yd, 1024, 1536)):
            r = _rstd(y, D_GROUP)
            nrm = y * r
            dmg = dm[:, lo:hi]
            dn = dmg * mn[:, lo:hi]
            outs.append(r * (dn - nrm * (jnp.sum(dn * nrm, axis=-1, keepdims=True) * (1.0 / D_GROUP))))
            dmn.append(jnp.sum(dmg * nrm, axis=0, keepdims=True))
        return (*outs, jnp.concatenate(dmn, axis=1))

    dy_a, dy_b, dy_c, dy_d, g["mix_norm"] = _rowwise(
        mix_bwd, name=n("group_norm_bwd"), rows=s,
        ins=[("row", dmixed, D_MIX_EXT, 0), ("row", sv["y_a"], 512, 0), ("row", sv["y_b"], 256, 0),
             ("row", sv["y_c"], 256, 0), ("row", sv["y_d"], 512, 0), ("full", w["mix_norm"])],
        outs=[(512, F32), (256, F32), (256, F32), (512, F32)], reds=[(1, D_MIX_EXT)])

    proj = sv["proj"]
    swa = dict(s=s, nh=SWA_HEADS, group=SWA_HEADS // SWA_KV_HEADS, mla=False, scale=1.0 / math.sqrt(HEAD))
    offs = (proj, C_QSW[0] // LANES, proj, C_KSW[0] // LANES, proj, C_VSW[0] // LANES)
    dq_sw, dvec_d, dsink = _flash_bwd_dq(*offs, sv["y_d"], dy_d, sv["lse_d"], w["hp_swa"], name=n("swa_dq"),
                                         out_dtype=BF16, **swa)
    dk_sw, dv_sw = _flash_bwd_dkv(*offs, dy_d, sv["lse_d"].reshape(SWA_HEADS, 1, s), dvec_d.reshape(SWA_HEADS, 1, s),
                                  w["hp_swa"], name=n("swa_dkv"), dk_dtype=BF16, dv_dtype=BF16, **swa)
    g["swa_sinks"] = dsink[:, 0, 0]

    mla = dict(s=s, nh=MLA_HEADS, group=1, mla=True, scale=1.0)
    offs = (sv["qm"], 0, sv["km"], 0, sv["vm"], 0)
    dqm, dvec_a, _ = _flash_bwd_dq(*offs, sv["y_a"], dy_a, sv["lse_a"], w["hp_mla"], name=n("mla_dq"), out_dtype=F32, **mla)
    dkm, dvm = _flash_bwd_dkv(*offs, dy_a, sv["lse_a"].reshape(MLA_HEADS, 1, s), dvec_a.reshape(MLA_HEADS, 1, s),
                              w["hp_mla"], name=n("mla_dkv"), dk_dtype=F32, dv_dtype=BF16, **mla)

    def prep_bwd(dq, dk, tqv, tc, ts):
        dkb = dk[:, 0:128] + dk[:, 128:256] + dk[:, 256:384] + dk[:, 384:512]
        return dq * tqv, dk, dkb * tc, dkb * ts

    dq_ext, dkvk, dkr, dkrp = _rowwise(
        prep_bwd, name=n("mla_prep_bwd"), rows=s,
        ins=[("row", dqm, 512, 0), ("row", dkm, 512, 0), ("row", tq, 512, 0), ("row", tkc, 128, 0), ("row", tks, 128, 0)],
        outs=[(512, BF16), (512, BF16), (128, BF16), (128, BF16)])
    dkv_ext = jnp.concatenate([dkvk, dvm], axis=1)
    dcqn = _mm(dq_ext, w["w_uq"], tb=True, name=n("d_cqn"))
    g["w_uq"] = _mm(sv["cqn"], dq_ext, ta=True, name=n("dw_uq"))
    dckvn = _mm(dkv_ext, w["w_ukv"], tb=True, name=n("d_ckvn"))
    g["w_ukv"] = _mm(sv["ckvn"], dkv_ext, ta=True, name=n("dw_ukv"))
    dcq, g["mla_q_norm"] = _rms_bwd(("row", proj, 256, C_CQ[0] // 256), w["mla_q_norm"], dcqn, None,
                                    name=n("q_norm_bwd"), rows=s, width=256, out_dtype=BF16)
    dckv, g["mla_kv_norm"] = _rms_bwd(("row", proj, 128, C_CKV[0] // 128), w["mla_kv_norm"], dckvn, None,
                                      name=n("kv_norm_bwd"), rows=s, width=128, out_dtype=BF16)

    dgb, dgc, duc, g["conv_w"] = _conv_bwd(dy_b, proj, w["conv_w"], name=n("conv_bwd"), s=s)
    dup, g["pool_wbd"], g["pool_scale"] = _pool_bwd(dy_c, proj, w["pool_wbd"], w["pool_scale"], name=n("pool_bwd"), s=s)

    dproj = jnp.concatenate([dq_sw, dcq, dgb, dgc, duc, dup, dk_sw, dv_sw, dckv, dkr, dkrp], axis=1)
    dh = _mm(dproj, w["w_in"], tb=True, name=n("d_h"))
    g["w_in"] = _mm(sv["h"], dproj, ta=True, name=n("dw_in"))
    dx0, g["attn_norm"] = _rms_bwd(("row", sv["x"], D_MODEL, 0), w["attn_norm"], dh, dx1, name=n("attn_norm_bwd"),
                                   rows=s, width=D_MODEL, out_dtype=F32)
    return dx0, g


def _loss_head(x, target, g, *, s):
    def fn(xv, tv, gv):
        r = _rstd(xv, D_MODEL)
        e = xv * r * gv - tv
        part = jnp.sum(jnp.sum(e * e, axis=1, keepdims=True), axis=0, keepdims=True) * (0.5 / D_MODEL)
        dy = e * (1.0 / D_MODEL)
        dyg = dy * gv
        dx = r * dyg - xv * (r * r * r) * (jnp.sum(dyg * xv, axis=-1, keepdims=True) * (1.0 / D_MODEL))
        return dx, jnp.sum(dy * xv * r, axis=0, keepdims=True), jnp.broadcast_to(part, (1, LANES))

    return _rowwise(fn, name="loss_head", rows=s,
                    ins=[("row", x, D_MODEL, 0), ("row", target, D_MODEL, 0), ("full", g)],
                    outs=[(D_MODEL, F32)], reds=[(1, D_MODEL), (1, LANES)])


def _alibi_slopes(n):
    return np.asarray([2.0 ** (-8.0 * (i + 1) / n) for i in range(n)], dtype=np.float32)


def _layer_weights(full, rep, l):
    pw = rep["pool_w"][l]
    z = jnp.zeros((HEAD, HEAD), F32)
    wbd = jnp.stack([jnp.block([[pw[2 * j], z], [z, pw[2 * j + 1]]]) for j in range(2)])
    w_o = full["w_o"][l]
    w_o_ext = jnp.moveaxis(_ext_mix(jnp.moveaxis(w_o, 0, -1)), -1, 0)
    return dict(
        attn_norm=rep["attn_norm"][l][None], w_in=_ext_w_in(full["w_in"][l]), mla_q_norm=rep["mla_q_norm"][l][None],
        w_uq=_ext_w_uq(full["w_uq"][l]), mla_kv_norm=rep["mla_kv_norm"][l][None], w_ukv=_ext_w_ukv(full["w_ukv"][l]),
        conv_w=full["conv_w"][l], pool_wbd=wbd, pool_scale=rep["pool_scale"][l][None],
        mix_norm=_ext_mix(rep["mix_norm"][l])[None], w_o=w_o_ext, ffn_norm=rep["ffn_norm"][l][None],
        w_gate_up=full["w_gate_up"][l], w_down=full["w_down"][l],
        hp_mla=jnp.zeros((MLA_HEADS, 2), F32),
        hp_swa=jnp.stack([jnp.asarray(_alibi_slopes(SWA_HEADS)), rep["swa_sinks"][l]], axis=1))


def _local_step(x, target, full, rep):
    s = x.shape[0]
    tabs = _rope_tables(s)
    ws = [_layer_weights(full, rep, l) for l in range(DEPTH)]
    saved = []
    for l in range(DEPTH):
        x, sv = _layer_fwd(x, ws[l], tabs, l)
        saved.append(sv)
    dx, d_final, loss = _loss_head(x, target, rep["final_norm"][None], s=s)
    layer_grads = [None] * DEPTH
    for l in reversed(range(DEPTH)):
        dx, layer_grads[l] = _layer_bwd(dx, saved[l], ws[l], tabs, l)

    def stack(fn):
        return jnp.stack([fn(layer_grads[l]) for l in range(DEPTH)])

    def layers(fn):
        return [fn(layer_grads[l]) for l in range(DEPTH)]

    grads = dict(
        attn_norm=stack(lambda g: g["attn_norm"][0]), w_in=layers(lambda g: _fold_w_in(g["w_in"])),
        mla_q_norm=stack(lambda g: g["mla_q_norm"][0]), w_uq=layers(lambda g: _fold_w_uq(g["w_uq"])),
        mla_kv_norm=stack(lambda g: g["mla_kv_norm"][0]), w_ukv=layers(lambda g: _fold_w_ukv(g["w_ukv"])),
        conv_w=layers(lambda g: g["conv_w"]),
        pool_w=stack(lambda g: jnp.stack([g["pool_wbd"][j // 2][HEAD * (j % 2):HEAD * (j % 2 + 1),
                                                                  HEAD * (j % 2):HEAD * (j % 2 + 1)] for j in range(4)])),
        pool_scale=stack(lambda g: g["pool_scale"][0]), swa_sinks=stack(lambda g: g["swa_sinks"]),
        mix_norm=stack(lambda g: _fold_mix(g["mix_norm"][0])),
        w_o=layers(lambda g: jnp.moveaxis(_fold_mix(jnp.moveaxis(g["w_o"], 0, -1)), -1, 0)),
        ffn_norm=stack(lambda g: g["ffn_norm"][0]), w_gate_up=layers(lambda g: g["w_gate_up"]),
        w_down=layers(lambda g: g["w_down"]), final_norm=d_final[0])
    return loss, dx, grads


def kernel(x, attn_norm, w_in, mla_q_norm, w_uq, mla_kv_norm, w_ukv, conv_w, pool_w, pool_scale, swa_sinks, mix_norm, w_o, ffn_norm, w_gate_up, w_down, final_norm, loss_target, m_attn_norm, m_w_in, m_mla_q_norm, m_w_uq, m_mla_kv_norm, m_w_ukv, m_conv_w, m_pool_w, m_pool_scale, m_swa_sinks, m_mix_norm, m_w_o, m_ffn_norm, m_w_gate_up, m_w_down, m_final_norm, v_attn_norm, v_w_in, v_mla_q_norm, v_w_uq, v_mla_kv_norm, v_w_ukv, v_conv_w, v_pool_w, v_pool_scale, v_swa_sinks, v_mix_norm, v_w_o, v_ffn_norm, v_w_gate_up, v_w_down, v_final_norm):
    given = dict(locals())
    sh_names = [nm for nm, _, _ in SHARDED]
    sh_axis = {nm: ax - 1 for nm, _, ax in SHARDED}
    rep_names = [nm for nm, _ in REPLICATED]
    rep_shapes = [shp for _, shp in REPLICATED]
    pairs = [(nm, l) for nm in sh_names for l in range(DEPTH)]

    def wire(nm, l):
        if nm == "conv_w":
            return lax.bitcast_convert_type(given[nm][l], BF16).reshape(3, -1)
        return given[nm][l].astype(BF16)

    gathered = dict(zip(pairs, _all_gather([wire(nm, l) for nm, l in pairs], name="gather_weights")))
    full = {nm: [None] * DEPTH for nm in sh_names}
    for (nm, l), g in gathered.items():
        if nm == "conv_w":
            g = lax.bitcast_convert_type(g.reshape(N_DEV, 3, -1, 2), F32)
        full[nm][l] = _shards_to_full(g, sh_axis[nm])
    rep = {nm: given[nm] for nm in rep_names if nm != "loss"}

    loss, grad_x, grads = _local_step(x[0], loss_target[0], full, rep)

    core = lax.axis_index("c").astype(jnp.int32).reshape(1)
    mine = [_full_to_shards(grads[nm][l], sh_axis[nm]) for nm, l in pairs]
    theirs = _exchange_sibling(mine, name="exchange_grads_sibling")
    chip_sums = [_add_own(g, o, core, name=f"chip_sum_{nm}_{l}") for (nm, l), g, o in zip(pairs, mine, theirs)]
    parts = dict(zip(pairs, _exchange_chips(chip_sums, name="exchange_grads_chips")))
    sh_out = {nm: _adamw([parts[nm, l] for l in range(DEPTH)], given[nm], given["m_" + nm], given["v_" + nm],
                         name=f"adamw_{nm}") for nm in sh_names}

    grads["loss"] = loss[0, :1]
    zero = jnp.zeros((1,), F32)
    small = _all_gather([_pack([grads[nm] for nm in rep_names])], name="gather_small_grads")
    packs = [_pack([given.get(pre + nm, zero) for nm in rep_names])[None] for pre in ("", "m_", "v_")]
    rep_out = [dict(zip(rep_names, _unpack(o[0], rep_shapes))) for o in _adamw(small, *packs, name="adamw_replicated")]

    out = [rep_out[0]["loss"][0], grad_x[None]]
    for i in range(4):
        out += [sh_out[nm][i] if nm in sh_axis else rep_out[i][nm] for nm in WEIGHT_ORDER]
    return tuple(out)
```

```python
import functools
import math

import numpy as np
import jax
import jax.numpy as jnp
from jax import lax
from jax.experimental import pallas as pl
from jax.experimental.pallas import tpu as pltpu

F32 = jnp.float32
BF16 = jnp.bfloat16

D_MODEL = 1024
DEPTH = 2
D_GROUP = 256
MLA_HEADS = 4
MLA_NOPE = 64
MLA_ROPE = 32
ROPE_THETA = 10000.0
POOL_WINDOWS = (2, 4, 8, 16)
SWA_HEADS = 4
SWA_KV_HEADS = 2
SWA_WINDOW = 128
D_FF = 2816
RMS_EPS = 1e-6
LANES = 128
HEAD = 64
VMEM_LIMIT = 48 * 1024 * 1024
NEG = -1e30

ADAM_LR = 0.001
ADAM_B1 = 0.9
ADAM_B2 = 0.999
ADAM_EPS = 1e-08
ADAM_WD = 0.01
ADAM_STEP = 10

N_DEV = 8
PACK_COLS = 1024

C_QSW, C_CQ, C_GB, C_GC, C_UCONV, C_UPOOL = (0, 512), (512, 256), (768, 256), (1024, 256), (1280, 256), (1536, 256)
C_KSW, C_VSW, C_CKV, C_KR, C_KRP = (1792, 256), (2048, 256), (2304, 128), (2432, 128), (2560, 128)
D_IN_EXT = 2688
D_MIX_EXT = 1536

SHARDED = (("w_in", (DEPTH, 1024, 244), 2), ("w_uq", (DEPTH, 256, 48), 2), ("w_ukv", (DEPTH, 128, 64), 2),
           ("conv_w", (DEPTH, 3, 32), 2), ("w_o", (DEPTH, 128, 1024), 1), ("w_gate_up", (DEPTH, 1024, 704), 2),
           ("w_down", (DEPTH, 352, 1024), 1))
REPLICATED = (("attn_norm", (DEPTH, 1024)), ("mla_q_norm", (DEPTH, 256)), ("mla_kv_norm", (DEPTH, 128)),
              ("pool_w", (DEPTH, 4, 64, 64)), ("pool_scale", (DEPTH, 256)), ("swa_sinks", (DEPTH, 4)),
              ("mix_norm", (DEPTH, 1024)), ("ffn_norm", (DEPTH, 1024)), ("final_norm", (1024,)), ("loss", (1,)))
WEIGHT_ORDER = ("attn_norm", "w_in", "mla_q_norm", "w_uq", "mla_kv_norm", "w_ukv", "conv_w", "pool_w", "pool_scale",
                "swa_sinks", "mix_norm", "w_o", "ffn_norm", "w_gate_up", "w_down", "final_norm")


def _params(sem):
    return pltpu.CompilerParams(dimension_semantics=sem, vmem_limit_bytes=VMEM_LIMIT)


def _pick(dim, target):
    if dim <= target:
        return dim
    best = None
    for t in range(LANES, target + 1, LANES):
        if dim % t == 0:
            best = t
    assert best is not None, (dim, target)
    return best


def _mm(a, b, *, name, ta=False, tb=False, res=None, out_dtype=F32, tm=512, tn=1024, tk=1024):
    m, k = (a.shape[1], a.shape[0]) if ta else a.shape
    n = b.shape[0] if tb else b.shape[1]
    assert (b.shape[1] if tb else b.shape[0]) == k
    tm, tn, tk = _pick(m, tm), _pick(n, tn), _pick(k, tk)
    nk = k // tk
    dims = (((0 if ta else 1,), (1 if tb else 0,)), ((), ()))
    has_res = res is not None

    def body(*refs):
        a_ref, b_ref = refs[0], refs[1]
        o_ref, acc_ref = refs[-2], refs[-1]
        kk = pl.program_id(2)

        @pl.when(kk == 0)
        def _():
            acc_ref[...] = jnp.zeros_like(acc_ref)

        acc_ref[...] += lax.dot_general(a_ref[...].astype(BF16), b_ref[...].astype(BF16), dims,
                                        preferred_element_type=F32)

        @pl.when(kk == nk - 1)
        def _():
            r = acc_ref[...]
            if has_res:
                r = r + refs[2][...]
            o_ref[...] = r.astype(o_ref.dtype)

    a_spec = pl.BlockSpec((tk, tm), lambda i, j, kk: (kk, i)) if ta else pl.BlockSpec((tm, tk), lambda i, j, kk: (i, kk))
    b_spec = pl.BlockSpec((tn, tk), lambda i, j, kk: (j, kk)) if tb else pl.BlockSpec((tk, tn), lambda i, j, kk: (kk, j))
    in_specs, args = [a_spec, b_spec], [a, b]
    if has_res:
        in_specs.append(pl.BlockSpec((tm, tn), lambda i, j, kk: (i, j)))
        args.append(res)
    return pl.pallas_call(
        body, name=name, grid=(m // tm, n // tn, nk), in_specs=in_specs,
        out_specs=pl.BlockSpec((tm, tn), lambda i, j, kk: (i, j)),
        out_shape=jax.ShapeDtypeStruct((m, n), out_dtype),
        scratch_shapes=[pltpu.VMEM((tm, tn), F32)],
        compiler_params=_params(("parallel", "parallel", "arbitrary")),
    )(*args)


def _rowwise(fn, *, name, rows, ins, outs, reds=(), tm=512):
    tm = min(tm, rows)
    assert rows % tm == 0
    n_in, n_out = len(ins), len(outs)

    def body(*refs):
        vals = [r[...] for r in refs[:n_in]]
        res = fn(*vals)
        for r, v in zip(refs[n_in:n_in + n_out], res[:n_out]):
            r[...] = v.astype(r.dtype)
        if reds:
            @pl.when(pl.program_id(0) == 0)
            def _():
                for r in refs[n_in + n_out:]:
                    r[...] = jnp.zeros_like(r)

            for r, v in zip(refs[n_in + n_out:], res[n_out:]):
                r[...] += v

    in_specs, args = [], []
    for spec in ins:
        if spec[0] == "row":
            _, arr, width, blk = spec
            in_specs.append(pl.BlockSpec((tm, width), functools.partial(lambda i, blk: (i, blk), blk=blk)))
        else:
            arr = spec[1]
            in_specs.append(pl.BlockSpec(arr.shape, functools.partial(lambda i, nd: (0,) * nd, nd=arr.ndim)))
        args.append(arr)
    out_specs = [pl.BlockSpec((tm, w), lambda i: (i, 0)) for w, _ in outs]
    out_shape = [jax.ShapeDtypeStruct((rows, w), dt) for w, dt in outs]
    out_specs += [pl.BlockSpec((r, w), lambda i: (0, 0)) for r, w in reds]
    out_shape += [jax.ShapeDtypeStruct((r, w), F32) for r, w in reds]
    return pl.pallas_call(body, name=name, grid=(rows // tm,), in_specs=in_specs, out_specs=out_specs,
                          out_shape=out_shape, compiler_params=_params(("arbitrary",)))(*args)


def _rstd(x, n):
    return lax.rsqrt(jnp.sum(x * x, axis=-1, keepdims=True) * (1.0 / n) + RMS_EPS)


def _rms_fwd(x_spec, g, *, name, rows, width):
    def fn(x, gv):
        return (x * _rstd(x, width) * gv,)
    return _rowwise(fn, name=name, rows=rows, ins=[x_spec, ("full", g)], outs=[(width, BF16)])[0]


def _rms_bwd(x_spec, g, dy, res, *, name, rows, width, out_dtype):
    def fn(x, gv, dyv, *rest):
        r = _rstd(x, width)
        dyg = dyv * gv
        dx = r * dyg - x * (r * r * r) * (jnp.sum(dyg * x, axis=-1, keepdims=True) * (1.0 / width))
        if rest:
            dx = dx + rest[0]
        return dx, jnp.sum(dyv * x * r, axis=0, keepdims=True)

    ins = [x_spec, ("full", g), ("row", dy, width, 0)]
    if res is not None:
        ins.append(("row", res, width, 0))
    return _rowwise(fn, name=name, rows=rows, ins=ins, outs=[(width, out_dtype)], reds=[(1, width)])


NT_DIMS = (((1,), (1,)), ((), ()))
TN_DIMS = (((0,), (0,)), ((), ()))


def _mla_tile(s):
    return min(512, s)


def _mla_fwd(qa, ka, va, *, name, s, nh):
    t = _mla_tile(s)
    nq = s // t

    def body(q_ref, k_ref, v_ref, o_ref, lse_ref, m_s, acc_s):
        i, j = pl.program_id(1), pl.program_id(2)

        @pl.when(j == 0)
        def _():
            m_s[...] = jnp.full_like(m_s, NEG)
            acc_s[...] = jnp.zeros_like(acc_s)

        def step(diag):
            sc = lax.dot_general(q_ref[...], k_ref[...], NT_DIMS, preferred_element_type=F32)
            if diag:
                dist = lax.broadcasted_iota(jnp.int32, sc.shape, 0) - lax.broadcasted_iota(jnp.int32, sc.shape, 1)
                sc = jnp.where(dist >= 0, sc, NEG)
            m_prev = m_s[...]
            m_new = jnp.maximum(m_prev, jnp.max(sc, axis=1, keepdims=True))
            p = jnp.exp(sc - m_new).astype(BF16)
            acc_s[...] = jnp.exp(m_prev - m_new) * acc_s[...] + jnp.dot(p, v_ref[...], preferred_element_type=F32)
            m_s[...] = m_new

        pl.when(j < i)(functools.partial(step, False))
        pl.when(j == i)(functools.partial(step, True))

        @pl.when(j == nq - 1)
        def _():
            acc = acc_s[...]
            lane = lax.broadcasted_iota(jnp.int32, acc.shape, 1)
            l = jnp.sum(jnp.where(lane == HEAD, acc, 0.0), axis=1, keepdims=True)
            o_ref[...] = jnp.where(lane < HEAD, acc / l, 0.0)
            lse_ref[...] = m_s[...] + jnp.log(l)

    kv_map = lambda h, i, j: (jnp.minimum(j, i), h)
    return pl.pallas_call(
        body, name=name, grid=(nh, nq, nq),
        in_specs=[pl.BlockSpec((t, LANES), lambda h, i, j: (i, h)), pl.BlockSpec((t, LANES), kv_map),
                  pl.BlockSpec((t, LANES), kv_map)],
        out_specs=[pl.BlockSpec((t, LANES), lambda h, i, j: (i, h)),
                   pl.BlockSpec((None, t, 1), lambda h, i, j: (h, i, 0))],
        out_shape=[jax.ShapeDtypeStruct((s, nh * LANES), F32), jax.ShapeDtypeStruct((nh, s, 1), F32)],
        scratch_shapes=[pltpu.VMEM((t, 1), F32), pltpu.VMEM((t, LANES), F32)],
        compiler_params=_params(("parallel", "parallel", "arbitrary")),
    )(qa, ka, va)


def _mla_dq(qa, ka, va, o, do, lse, *, name, s, nh):
    t = _mla_tile(s)
    nq = s // t

    def body(q_ref, k_ref, v_ref, o_ref, do_ref, lse_ref, dq_ref, d_ref, acc_s):
        i, j = pl.program_id(1), pl.program_id(2)

        @pl.when(j == 0)
        def _():
            acc_s[...] = jnp.zeros_like(acc_s)
            d_ref[...] = jnp.sum(do_ref[...] * o_ref[...], axis=1, keepdims=True)

        def step(diag):
            k = k_ref[...]
            sc = lax.dot_general(q_ref[...], k, NT_DIMS, preferred_element_type=F32)
            if diag:
                dist = lax.broadcasted_iota(jnp.int32, sc.shape, 0) - lax.broadcasted_iota(jnp.int32, sc.shape, 1)
                sc = jnp.where(dist >= 0, sc, NEG)
            p = jnp.exp(sc - lse_ref[...])
            dp = lax.dot_general(do_ref[...].astype(BF16), v_ref[...], NT_DIMS, preferred_element_type=F32)
            ds = p * (dp - d_ref[...])
            acc_s[...] += jnp.dot(ds.astype(BF16), k, preferred_element_type=F32)

        pl.when(j < i)(functools.partial(step, False))
        pl.when(j == i)(functools.partial(step, True))

        @pl.when(j == nq - 1)
        def _():
            dq_ref[...] = acc_s[...]

    kv_map = lambda h, i, j: (jnp.minimum(j, i), h)
    q_map = lambda h, i, j: (i, h)
    vec_map = lambda h, i, j: (h, i, 0)
    return pl.pallas_call(
        body, name=name, grid=(nh, nq, nq),
        in_specs=[pl.BlockSpec((t, LANES), q_map), pl.BlockSpec((t, LANES), kv_map), pl.BlockSpec((t, LANES), kv_map),
                  pl.BlockSpec((t, LANES), q_map), pl.BlockSpec((t, LANES), q_map), pl.BlockSpec((None, t, 1), vec_map)],
        out_specs=[pl.BlockSpec((t, LANES), q_map), pl.BlockSpec((None, t, 1), vec_map)],
        out_shape=[jax.ShapeDtypeStruct((s, nh * LANES), F32), jax.ShapeDtypeStruct((nh, s, 1), F32)],
        scratch_shapes=[pltpu.VMEM((t, LANES), F32)],
        compiler_params=_params(("parallel", "parallel", "arbitrary")),
    )(qa, ka, va, o, do, lse)


def _mla_dkv(qa, ka, va, do, lse_row, d_row, *, name, s, nh):
    t = _mla_tile(s)
    nq = s // t

    def body(q_ref, k_ref, v_ref, do_ref, lse_ref, d_ref, dk_ref, dv_ref, dk_s, dv_s):
        kb, j = pl.program_id(1), pl.program_id(2)

        @pl.when(j == 0)
        def _():
            dk_s[...] = jnp.zeros_like(dk_s)
            dv_s[...] = jnp.zeros_like(dv_s)

        def step(diag):
            q = q_ref[...]
            do_b = do_ref[...].astype(BF16)
            sc = lax.dot_general(k_ref[...], q, NT_DIMS, preferred_element_type=F32)
            if diag:
                dist = lax.broadcasted_iota(jnp.int32, sc.shape, 1) - lax.broadcasted_iota(jnp.int32, sc.shape, 0)
                sc = jnp.where(dist >= 0, sc, NEG)
            p = jnp.exp(sc - lse_ref[...])
            dv_s[...] += jnp.dot(p.astype(BF16), do_b, preferred_element_type=F32)
            dp = lax.dot_general(v_ref[...], do_b, NT_DIMS, preferred_element_type=F32)
            ds = p * (dp - d_ref[...])
            dk_s[...] += jnp.dot(ds.astype(BF16), q, preferred_element_type=F32)

        pl.when(j > kb)(functools.partial(step, False))
        pl.when(j == kb)(functools.partial(step, True))

        @pl.when(j == nq - 1)
        def _():
            dk_ref[...] = dk_s[...]
            dv_ref[...] = dv_s[...].astype(dv_ref.dtype)

    q_map = lambda h, kb, j: (jnp.maximum(j, kb), h)
    kv_map = lambda h, kb, j: (kb, h)
    row_map = lambda h, kb, j: (h, 0, jnp.maximum(j, kb))
    return pl.pallas_call(
        body, name=name, grid=(nh, nq, nq),
        in_specs=[pl.BlockSpec((t, LANES), q_map), pl.BlockSpec((t, LANES), kv_map), pl.BlockSpec((t, LANES), kv_map),
                  pl.BlockSpec((t, LANES), q_map), pl.BlockSpec((None, 1, t), row_map), pl.BlockSpec((None, 1, t), row_map)],
        out_specs=[pl.BlockSpec((t, LANES), kv_map), pl.BlockSpec((t, LANES), kv_map)],
        out_shape=[jax.ShapeDtypeStruct((s, nh * LANES), F32), jax.ShapeDtypeStruct((s, nh * LANES), BF16)],
        scratch_shapes=[pltpu.VMEM((t, LANES), F32), pltpu.VMEM((t, LANES), F32)],
        compiler_params=_params(("parallel", "parallel", "arbitrary")),
    )(qa, ka, va, do, lse_row, d_row)


SWA_PIECE = 128
SWA_KEYS = 2 * SWA_PIECE


def _swa_block(s):
    return min(512, s)


def _swa_piece(hp_ref, h, q, k_ref, v_ref, qpos0, scale):
    kstart = pl.multiple_of(jnp.maximum(qpos0 - SWA_PIECE, 0), SWA_PIECE)
    k = k_ref[pl.ds(kstart, SWA_KEYS), :].astype(BF16)
    v = v_ref[pl.ds(kstart, SWA_KEYS), :].astype(BF16)
    sc = lax.dot_general(q, k, NT_DIMS, preferred_element_type=F32)
    dist = (qpos0 + lax.broadcasted_iota(jnp.int32, sc.shape, 0)) - (kstart + lax.broadcasted_iota(jnp.int32, sc.shape, 1))
    sc = sc * scale - hp_ref[h, 0] * dist.astype(F32)
    sc = jnp.where((dist >= 0) & (dist < SWA_WINDOW), sc, NEG)
    return kstart, k, v, sc


def _swa_fwd(proj, hp, *, name, s, scale):
    tb = _swa_block(s)
    group = SWA_HEADS // SWA_KV_HEADS
    q_off, k_off, v_off = C_QSW[0] // LANES, C_KSW[0] // LANES, C_VSW[0] // LANES

    def body(hp_ref, q_ref, k_ref, v_ref, o_ref, lse_ref):
        h, i = pl.program_id(0), pl.program_id(1)
        sink = hp_ref[h, 1]
        for r in range(0, tb, SWA_PIECE):
            rows = pl.ds(r, SWA_PIECE)
            _, _, v, sc = _swa_piece(hp_ref, h, q_ref[rows, :].astype(BF16), k_ref, v_ref, i * tb + r, scale)
            m = jnp.maximum(jnp.max(sc, axis=1, keepdims=True), sink)
            p = jnp.exp(sc - m)
            l = jnp.sum(p, axis=1, keepdims=True) + jnp.exp(sink - m)
            o_ref[rows, :] = jnp.dot(p.astype(BF16), v, preferred_element_type=F32) / l
            lse_ref[rows, :] = m + jnp.log(l)

    whole = lambda off: pl.BlockSpec((s, LANES), lambda h, i: (0, off + h // group))
    return pl.pallas_call(
        body, name=name, grid=(SWA_HEADS, s // tb),
        in_specs=[pl.BlockSpec(memory_space=pltpu.SMEM), pl.BlockSpec((tb, LANES), lambda h, i: (i, q_off + h)),
                  whole(k_off), whole(v_off)],
        out_specs=[pl.BlockSpec((tb, LANES), lambda h, i: (i, h)), pl.BlockSpec((None, tb, 1), lambda h, i: (h, i, 0))],
        out_shape=[jax.ShapeDtypeStruct((s, SWA_HEADS * LANES), F32), jax.ShapeDtypeStruct((SWA_HEADS, s, 1), F32)],
        compiler_params=_params(("parallel", "parallel")),
    )(hp, proj, proj, proj)


def _swa_bwd(proj, o, do, lse, hp, *, name, s, scale):
    tb = _swa_block(s)
    nqb = s // tb
    group = SWA_HEADS // SWA_KV_HEADS
    q_off, k_off, v_off = C_QSW[0] // LANES, C_KSW[0] // LANES, C_VSW[0] // LANES

    def body(hp_ref, q_ref, k_ref, v_ref, o_ref, do_ref, lse_ref, dq_ref, dk_ref, dv_ref, dsink_ref):
        kh, g, i = pl.program_id(0), pl.program_id(1), pl.program_id(2)
        h = kh * group + g
        sink = hp_ref[h, 1]

        @pl.when((g == 0) & (i == 0))
        def _():
            dk_ref[...] = jnp.zeros_like(dk_ref)
            dv_ref[...] = jnp.zeros_like(dv_ref)

        @pl.when(i == 0)
        def _():
            dsink_ref[...] = jnp.zeros_like(dsink_ref)

        for r in range(0, tb, SWA_PIECE):
            rows = pl.ds(r, SWA_PIECE)
            q = q_ref[rows, :].astype(BF16)
            dov = do_ref[rows, :]
            do_b = dov.astype(BF16)
            lse_r = lse_ref[rows, :]
            d_r = jnp.sum(dov * o_ref[rows, :], axis=1, keepdims=True)
            kstart, k, v, sc = _swa_piece(hp_ref, h, q, k_ref, v_ref, i * tb + r, scale)
            p = jnp.exp(sc - lse_r)
            dp = lax.dot_general(do_b, v, NT_DIMS, preferred_element_type=F32)
            ds = (p * (dp - d_r)).astype(BF16)
            dq_ref[rows, :] = (jnp.dot(ds, k, preferred_element_type=F32) * scale).astype(dq_ref.dtype)
            win = pl.ds(kstart, SWA_KEYS)
            dk_ref[win, :] += lax.dot_general(ds, q, TN_DIMS, preferred_element_type=F32) * scale
            dv_ref[win, :] += lax.dot_general(p.astype(BF16), do_b, TN_DIMS, preferred_element_type=F32)
            part = jnp.sum(-jnp.exp(sink - lse_r) * d_r, axis=0, keepdims=True)
            dsink_ref[...] += jnp.broadcast_to(part, (1, LANES))

    whole = lambda off: pl.BlockSpec((s, LANES), lambda kh, g, i: (0, off + kh))
    q_map = lambda kh, g, i: (i, kh * group + g)
    return pl.pallas_call(
        body, name=name, grid=(SWA_KV_HEADS, group, nqb),
        in_specs=[pl.BlockSpec(memory_space=pltpu.SMEM),
                  pl.BlockSpec((tb, LANES), lambda kh, g, i: (i, q_off + kh * group + g)), whole(k_off), whole(v_off),
                  pl.BlockSpec((tb, LANES), q_map), pl.BlockSpec((tb, LANES), q_map),
                  pl.BlockSpec((None, tb, 1), lambda kh, g, i: (kh * group + g, i, 0))],
        out_specs=[pl.BlockSpec((tb, LANES), q_map), whole(0), whole(0),
                   pl.BlockSpec((None, 1, LANES), lambda kh, g, i: (kh * group + g, 0, 0))],
        out_shape=[jax.ShapeDtypeStruct((s, SWA_HEADS * LANES), BF16),
                   jax.ShapeDtypeStruct((s, SWA_KV_HEADS * LANES), F32), jax.ShapeDtypeStruct((s, SWA_KV_HEADS * LANES), F32),
                   jax.ShapeDtypeStruct((SWA_HEADS, 1, LANES), F32)],
        compiler_params=_params(("parallel", "arbitrary", "arbitrary")),
    )(hp, proj, proj, proj, o, do, lse)


def _shift_down(z, k):
    rows = lax.broadcasted_iota(jnp.int32, z.shape, 0)
    return jnp.where(rows >= k, pltpu.roll(z, k, 0), 0.0)


def _shift_up(z, k):
    n = z.shape[0]
    rows = lax.broadcasted_iota(jnp.int32, z.shape, 0)
    return jnp.where(rows < n - k, pltpu.roll(z, n - k, 0), 0.0)


def _rows3(a, b, c):
    r = lax.broadcasted_iota(jnp.int32, (3, a.shape[1]), 0)
    return jnp.where(r == 0, a, jnp.where(r == 1, b, c))


def _col_spec(s, off):
    return pl.BlockSpec((s, LANES), functools.partial(lambda j, off: (0, off + j), off=off))


def _conv_fwd(proj, conv_w, *, name, s):
    def body(gb_ref, gc_ref, u_ref, w_ref, y_ref):
        w0, w1, w2 = w_ref[0:1, :], w_ref[1:2, :], w_ref[2:3, :]
        z = gc_ref[...] * u_ref[...]
        c = w2 * z + w1 * _shift_down(z, 1) + w0 * _shift_down(z, 2)
        y_ref[...] = gb_ref[...] * c

    return pl.pallas_call(
        body, name=name, grid=(2,),
        in_specs=[_col_spec(s, C_GB[0] // LANES), _col_spec(s, C_GC[0] // LANES), _col_spec(s, C_UCONV[0] // LANES),
                  pl.BlockSpec((3, LANES), lambda j: (0, j))],
        out_specs=_col_spec(s, 0), out_shape=jax.ShapeDtypeStruct((s, D_GROUP), F32),
        compiler_params=_params(("parallel",)),
    )(proj, proj, proj, conv_w)


def _conv_bwd(dy, proj, conv_w, *, name, s):
    def body(dy_ref, gb_ref, gc_ref, u_ref, w_ref, dgb_ref, dgc_ref, du_ref, dw_ref):
        w0, w1, w2 = w_ref[0:1, :], w_ref[1:2, :], w_ref[2:3, :]
        gc, u, dyv = gc_ref[...], u_ref[...], dy_ref[...]
        z = gc * u
        z1, z2 = _shift_down(z, 1), _shift_down(z, 2)
        c = w2 * z + w1 * z1 + w0 * z2
        dgb_ref[...] = (dyv * c).astype(dgb_ref.dtype)
        dc = dyv * gb_ref[...]
        dz = w2 * dc + w1 * _shift_up(dc, 1) + w0 * _shift_up(dc, 2)
        dgc_ref[...] = (dz * u).astype(dgc_ref.dtype)
        du_ref[...] = (dz * gc).astype(du_ref.dtype)
        dw_ref[...] = _rows3(jnp.sum(dc * z2, axis=0, keepdims=True), jnp.sum(dc * z1, axis=0, keepdims=True),
                             jnp.sum(dc * z, axis=0, keepdims=True))

    act = jax.ShapeDtypeStruct((s, D_GROUP), BF16)
    return pl.pallas_call(
        body, name=name, grid=(2,),
        in_specs=[_col_spec(s, 0), _col_spec(s, C_GB[0] // LANES), _col_spec(s, C_GC[0] // LANES),
                  _col_spec(s, C_UCONV[0] // LANES), pl.BlockSpec((3, LANES), lambda j: (0, j))],
        out_specs=[_col_spec(s, 0), _col_spec(s, 0), _col_spec(s, 0), pl.BlockSpec((3, LANES), lambda j: (0, j))],
        out_shape=[act, act, act, jax.ShapeDtypeStruct((3, D_GROUP), F32)],
        compiler_params=_params(("parallel",)),
    )(dy, proj, proj, proj, conv_w)


def _pool_select(j, lane, a2, a4, a8, a16):
    lo = lane < HEAD
    return jnp.where(j == 0, jnp.where(lo, a2, a4), jnp.where(lo, a8, a16))


def _pooled(u, j):
    s2 = u + _shift_down(u, 1)
    s4 = s2 + _shift_down(s2, 2)
    s8 = s4 + _shift_down(s4, 4)
    s16 = s8 + _shift_down(s8, 8)
    lane = lax.broadcasted_iota(jnp.int32, u.shape, 1)
    rows = lax.broadcasted_iota(jnp.int32, u.shape, 0)
    win = _pool_select(j, lane, 2, 4, 8, 16)
    count = jnp.minimum(rows + 1, win).astype(F32)
    return _pool_select(j, lane, s2, s4, s8, s16) / count - u, count


def _pool_fwd(proj, wbd, scale, *, name, s):
    def body(u_ref, w_ref, sc_ref, y_ref):
        pooled, _ = _pooled(u_ref[...], pl.program_id(0))
        y_ref[...] = jnp.dot(pooled.astype(BF16), w_ref[...].astype(BF16), preferred_element_type=F32) * sc_ref[...]

    return pl.pallas_call(
        body, name=name, grid=(2,),
        in_specs=[_col_spec(s, C_UPOOL[0] // LANES), pl.BlockSpec((None, LANES, LANES), lambda j: (j, 0, 0)),
                  pl.BlockSpec((1, LANES), lambda j: (0, j))],
        out_specs=_col_spec(s, 0), out_shape=jax.ShapeDtypeStruct((s, D_GROUP), F32),
        compiler_params=_params(("parallel",)),
    )(proj, wbd, scale)


def _pool_bwd(dy, proj, wbd, scale, *, name, s):
    def body(dy_ref, u_ref, w_ref, sc_ref, du_ref, dw_ref, dsc_ref):
        j = pl.program_id(0)
        pooled, count = _pooled(u_ref[...], j)
        pooled_b = pooled.astype(BF16)
        w_b = w_ref[...].astype(BF16)
        dyv = dy_ref[...]
        mixed = jnp.dot(pooled_b, w_b, preferred_element_type=F32)
        dsc_ref[...] = jnp.sum(dyv * mixed, axis=0, keepdims=True)
        dms = (dyv * sc_ref[...]).astype(BF16)
        dw_ref[...] = lax.dot_general(pooled_b, dms, (((0,), (0,)), ((), ())), preferred_element_type=F32)
        dpooled = lax.dot_general(dms, w_b, (((1,), (1,)), ((), ())), preferred_element_type=F32)
        r = dpooled / count
        a2 = r + _shift_up(r, 1)
        a4 = a2 + _shift_up(a2, 2)
        a8 = a4 + _shift_up(a4, 4)
        a16 = a8 + _shift_up(a8, 8)
        lane = lax.broadcasted_iota(jnp.int32, r.shape, 1)
        du_ref[...] = (_pool_select(j, lane, a2, a4, a8, a16) - dpooled).astype(du_ref.dtype)

    return pl.pallas_call(
        body, name=name, grid=(2,),
        in_specs=[_col_spec(s, 0), _col_spec(s, C_UPOOL[0] // LANES),
                  pl.BlockSpec((None, LANES, LANES), lambda j: (j, 0, 0)), pl.BlockSpec((1, LANES), lambda j: (0, j))],
        out_specs=[_col_spec(s, 0), pl.BlockSpec((None, LANES, LANES), lambda j: (j, 0, 0)),
                   pl.BlockSpec((1, LANES), lambda j: (0, j))],
        out_shape=[jax.ShapeDtypeStruct((s, D_GROUP), BF16), jax.ShapeDtypeStruct((2, LANES, LANES), F32),
                   jax.ShapeDtypeStruct((1, D_GROUP), F32)],
        compiler_params=_params(("parallel",)),
    )(dy, proj, wbd, scale)


def _mesh_pos():
    return lax.axis_index("x"), lax.axis_index("y"), lax.axis_index("c")


def _any_specs(n):
    return [pl.BlockSpec(memory_space=pl.ANY)] * n


def _all_gather(xs, *, name):
    n = len(xs)

    def body(*refs):
        x_refs, out_refs = refs[:n], refs[n:2 * n]
        send_sems, recv_sems, local_sems = refs[2 * n:]
        x, y, cc = _mesh_pos()
        me, sibling = (x, y, cc), (x, y, 1 - cc)
        chips = [(1 - x, y), (x, 1 - y), (1 - x, 1 - y)]

        def slot(a, px, py, pc):
            return out_refs[a].at[4 * px + 2 * py + pc]

        def copy(a, k, block, to, src=None):
            return pltpu.make_async_remote_copy(
                src_ref=slot(a, *block) if src is None else src, dst_ref=slot(a, *block), send_sem=send_sems.at[a, k],
                recv_sem=recv_sems.at[a, k], device_id=to, device_id_type=pl.DeviceIdType.MESH)

        mine = [pltpu.make_async_copy(x_refs[a], slot(a, *me), local_sems.at[a]) for a in range(n)]
        first = []
        for a in range(n):
            first.append(copy(a, 0, me, sibling, src=x_refs[a]))
            first += [copy(a, 1 + j, me, (*chip, cc), src=x_refs[a]) for j, chip in enumerate(chips)]
        for cp in mine + first:
            cp.start()
        passed = []
        for j, chip in enumerate(chips):
            for a in range(n):
                copy(a, 1 + j, (*chip, cc), me).wait_recv()
                passed.append(copy(a, 4 + j, (*chip, cc), sibling))
                passed[-1].start()
        for a in range(n):
            copy(a, 0, sibling, me).wait_recv()
        for j, chip in enumerate(chips):
            for a in range(n):
                copy(a, 4 + j, (*chip, 1 - cc), me).wait_recv()
        for cp in first + passed:
            cp.wait_send()
        for cp in mine:
            cp.wait()

    return pl.pallas_call(
        body, name=name, out_shape=[jax.ShapeDtypeStruct((N_DEV,) + a.shape, a.dtype) for a in xs],
        in_specs=_any_specs(n), out_specs=_any_specs(n),
        scratch_shapes=[pltpu.SemaphoreType.DMA((n, 7)), pltpu.SemaphoreType.DMA((n, 7)), pltpu.SemaphoreType.DMA((n,))],
    )(*xs)


def _exchange_sibling(gs, *, name):
    n = len(gs)

    def body(*refs):
        g_refs, out_refs = refs[:n], refs[n:2 * n]
        send_sems, recv_sems = refs[2 * n:]
        x, y, cc = _mesh_pos()
        copies = [pltpu.make_async_remote_copy(
            src_ref=g_refs[a].at[1 - cc], dst_ref=out_refs[a], send_sem=send_sems.at[a], recv_sem=recv_sems.at[a],
            device_id=(x, y, 1 - cc), device_id_type=pl.DeviceIdType.MESH) for a in range(n)]
        for cp in copies:
            cp.start()
        for cp in copies:
            cp.wait()

    return pl.pallas_call(
        body, name=name, out_shape=[jax.ShapeDtypeStruct(g.shape[1:], g.dtype) for g in gs],
        in_specs=_any_specs(n), out_specs=_any_specs(n),
        scratch_shapes=[pltpu.SemaphoreType.DMA((n,)), pltpu.SemaphoreType.DMA((n,))],
    )(*gs)


def _row_tile(rows, target=512):
    if rows <= target:
        return rows
    best = None
    for t in range(8, target + 1, 8):
        if rows % t == 0:
            best = t
    assert best is not None, (rows, target)
    return best


def _add_own(g, other, core, *, name):
    _, _, rows, cols = g.shape
    tm = _row_tile(rows)

    def body(c_ref, g_ref, o_ref, out_ref):
        out_ref[...] = g_ref[...] + o_ref[...]

    return pl.pallas_call(
        body, name=name, out_shape=jax.ShapeDtypeStruct(other.shape, other.dtype),
        grid_spec=pltpu.PrefetchScalarGridSpec(
            num_scalar_prefetch=1, grid=(4, rows // tm),
            in_specs=[pl.BlockSpec((None, None, tm, cols), lambda p, i, c_ref: (c_ref[0], p, i, 0)),
                      pl.BlockSpec((None, tm, cols), lambda p, i, c_ref: (p, i, 0))],
            out_specs=pl.BlockSpec((None, tm, cols), lambda p, i, c_ref: (p, i, 0))),
        compiler_params=_params(("parallel", "parallel")),
    )(core, g, other)


def _exchange_chips(hs, *, name):
    n = len(hs)

    def body(*refs):
        h_refs, out_refs = refs[:n], refs[n:2 * n]
        send_sems, recv_sems, local_sems = refs[2 * n:]
        x, y, cc = _mesh_pos()
        my_chip = 2 * x + y
        chips = [(1 - x, y), (x, 1 - y), (1 - x, 1 - y)]
        local = [pltpu.make_async_copy(h_refs[a].at[my_chip], out_refs[a].at[my_chip], local_sems.at[a]) for a in range(n)]
        out, inc = [], []
        for a in range(n):
            for j, (px, py) in enumerate(chips):
                peer = 2 * px + py
                sems = dict(send_sem=send_sems.at[a, j], recv_sem=recv_sems.at[a, j], device_id=(px, py, cc),
                            device_id_type=pl.DeviceIdType.MESH)
                out.append(pltpu.make_async_remote_copy(src_ref=h_refs[a].at[peer], dst_ref=out_refs[a].at[my_chip], **sems))
                inc.append(pltpu.make_async_remote_copy(src_ref=h_refs[a].at[peer], dst_ref=out_refs[a].at[peer], **sems))
        for cp in local + out:
            cp.start()
        for cp in inc:
            cp.wait_recv()
        for cp in out:
            cp.wait_send()
        for cp in local:
            cp.wait()

    return pl.pallas_call(
        body, name=name, out_shape=[jax.ShapeDtypeStruct(h.shape, h.dtype) for h in hs],
        in_specs=_any_specs(n), out_specs=_any_specs(n),
        scratch_shapes=[pltpu.SemaphoreType.DMA((n, 3)), pltpu.SemaphoreType.DMA((n, 3)), pltpu.SemaphoreType.DMA((n,))],
    )(*hs)


def _adamw(parts, w, m, v, *, name):
    layers, rows, cols = w.shape
    assert len(parts) == layers
    tm = _row_tile(rows, 256)
    nr = rows // tm

    def body(*refs):
        p_refs = refs[:layers]
        w_ref, m_ref, v_ref, g_ref, d_ref, nm_ref, nv_ref, g_s = refs[layers:]
        for ll in range(layers):
            @pl.when(pl.program_id(0) == ll)
            def _(ll=ll):
                g = p_refs[ll][0]
                for q in range(1, p_refs[ll].shape[0]):
                    g = g + p_refs[ll][q]
                g_s[...] = g

        g = g_s[...]
        mm = ADAM_B1 * m_ref[...] + (1.0 - ADAM_B1) * g
        vv = ADAM_B2 * v_ref[...] + (1.0 - ADAM_B2) * jnp.square(g)
        m_hat = mm / (1.0 - ADAM_B1 ** ADAM_STEP)
        v_hat = vv / (1.0 - ADAM_B2 ** ADAM_STEP)
        g_ref[...] = g
        d_ref[...] = -ADAM_LR * (m_hat / (jnp.sqrt(v_hat) + ADAM_EPS) + ADAM_WD * w_ref[...])
        nm_ref[...] = mm
        nv_ref[...] = vv

    def part_spec(ll, p):
        return pl.BlockSpec((p, tm, cols), lambda l, i: (0, jnp.where(l == ll, i, jnp.where(l < ll, 0, nr - 1)), 0))

    spec = pl.BlockSpec((None, tm, cols), lambda l, i: (l, i, 0))
    out = jax.ShapeDtypeStruct(w.shape, F32)
    return pl.pallas_call(
        body, name=name, grid=(layers, nr),
        in_specs=[part_spec(ll, parts[ll].shape[0]) for ll in range(layers)] + [spec] * 3,
        out_specs=[spec] * 4, out_shape=[out] * 4, scratch_shapes=[pltpu.VMEM((tm, cols), F32)],
        compiler_params=_params(("arbitrary", "arbitrary")),
    )(*parts, w, m, v)


def _pack(arrs):
    flat = jnp.concatenate([a.reshape(-1) for a in arrs])
    rows = -(-flat.shape[0] // (PACK_COLS * 16)) * 16
    return jnp.pad(flat, (0, rows * PACK_COLS - flat.shape[0])).reshape(rows, PACK_COLS)


def _unpack(packed, shapes):
    flat = packed.reshape(-1)
    out, off = [], 0
    for shp in shapes:
        n = int(np.prod(shp))
        out.append(flat[off:off + n].reshape(shp))
        off += n
    return out


def _shards_to_full(g, axis):
    if axis == 0:
        return g.reshape(g.shape[0] * g.shape[1], g.shape[2])
    return jnp.transpose(g, (1, 0, 2)).reshape(g.shape[1], g.shape[0] * g.shape[2])


def _full_to_shards(a, axis):
    if axis == 0:
        return jnp.transpose(a.reshape(4, 2, a.shape[0] // N_DEV, a.shape[1]), (1, 0, 2, 3))
    return jnp.transpose(a.reshape(a.shape[0], 4, 2, a.shape[1] // N_DEV), (2, 1, 0, 3))


def _zeros_like_cols(a, n):
    return jnp.zeros(a.shape[:-1] + (n,), a.dtype)


def _swap_cols(r, sign):
    half = MLA_ROPE // 2
    if sign > 0:
        return jnp.concatenate([-r[..., half:], r[..., :half]], axis=-1)
    return jnp.concatenate([r[..., half:], -r[..., :half]], axis=-1)


def _pad_heads(a, n):
    z = _zeros_like_cols(a, HEAD)
    return jnp.concatenate([p for h in range(n) for p in (a[..., h * HEAD:(h + 1) * HEAD], z)], axis=-1)


def _unpad_heads(a, n):
    return jnp.concatenate([a[..., h * LANES:h * LANES + HEAD] for h in range(n)], axis=-1)


def _ext_w_in(w):
    z = _zeros_like_cols(w, HEAD)
    kr = w[..., 384:416]
    krp = _swap_cols(kr, 1)
    return jnp.concatenate([
        _pad_heads(w[..., 1440:1696], 4), w[..., 0:256], w[..., 416:672], w[..., 672:928], w[..., 928:1184],
        w[..., 1184:1440], _pad_heads(w[..., 1696:1824], 2), _pad_heads(w[..., 1824:1952], 2), w[..., 256:384],
        z, kr, kr, z, krp, krp], axis=-1)


def _fold_w_in(e):
    kr = (e[..., 2496:2528] + e[..., 2528:2560]) + _swap_cols(e[..., 2624:2656] + e[..., 2656:2688], -1)
    return jnp.concatenate([
        e[..., 512:768], e[..., 2304:2432], kr, e[..., 768:1024], e[..., 1024:1280], e[..., 1280:1536],
        e[..., 1536:1792], _unpad_heads(e[..., 0:512], 4), _unpad_heads(e[..., 1792:2048], 2),
        _unpad_heads(e[..., 2048:2304], 2)], axis=-1)


def _ext_w_uq(w):
    parts = []
    for h in range(MLA_HEADS):
        rot = w[..., 96 * h + 64:96 * h + 96]
        parts += [w[..., 96 * h:96 * h + 64], rot, _swap_cols(rot, 1)]
    return jnp.concatenate(parts, axis=-1)


def _fold_w_uq(e):
    parts = []
    for h in range(MLA_HEADS):
        b = LANES * h
        parts += [e[..., b:b + 64], e[..., b + 64:b + 96] + _swap_cols(e[..., b + 96:b + 128], -1)]
    return jnp.concatenate(parts, axis=-1)


def _ext_w_ukv(w):
    k = jnp.concatenate([w[..., LANES * h:LANES * h + HEAD] for h in range(MLA_HEADS)], axis=-1)
    v = jnp.concatenate([w[..., LANES * h + HEAD:LANES * (h + 1)] for h in range(MLA_HEADS)], axis=-1)
    return jnp.concatenate([_pad_heads(k, 4), _pad_heads(v, 4)], axis=-1)


def _fold_w_ukv(e):
    parts = []
    for h in range(MLA_HEADS):
        parts += [e[..., LANES * h:LANES * h + HEAD], e[..., 512 + LANES * h:512 + LANES * h + HEAD]]
    return jnp.concatenate(parts, axis=-1)


def _ext_mix(a):
    return jnp.concatenate([_pad_heads(a[..., 0:256], 4), a[..., 256:768], _pad_heads(a[..., 768:1024], 4)], axis=-1)


def _fold_mix(e):
    return jnp.concatenate([_unpad_heads(e[..., 0:512], 4), e[..., 512:1024], _unpad_heads(e[..., 1024:1536], 4)], axis=-1)


def _rope_tables(s):
    inv = 1.0 / (ROPE_THETA ** (jnp.arange(0, MLA_ROPE, 2, dtype=F32) / MLA_ROPE))
    ang = jnp.arange(s, dtype=F32)[:, None] * inv[None, :]
    cos, sin = jnp.cos(ang), jnp.sin(ang)
    c32, s32 = jnp.concatenate([cos, cos], axis=1), jnp.concatenate([sin, sin], axis=1)
    zeros, ones = jnp.zeros((s, HEAD), F32), jnp.ones((s, HEAD), F32)
    tq = jnp.concatenate([ones, c32, s32], axis=1) * (1.0 / math.sqrt(MLA_NOPE + MLA_ROPE))
    return (jnp.tile(tq, (1, MLA_HEADS)), jnp.concatenate([zeros, c32, c32], axis=1),
            jnp.concatenate([zeros, s32, s32], axis=1))


def _gn(y):
    return y * _rstd(y, D_GROUP)


def _layer_fwd(x, w, tabs, l):
    s = x.shape[0]
    tq, tkc, tks = tabs
    n = lambda t: f"l{l}_{t}"
    h = _rms_fwd(("row", x, D_MODEL, 0), w["attn_norm"], name=n("attn_norm"), rows=s, width=D_MODEL)
    proj = _mm(h, w["w_in"], name=n("in_proj"))
    cqn = _rms_fwd(("row", proj, 256, C_CQ[0] // 256), w["mla_q_norm"], name=n("q_norm"), rows=s, width=256)
    ckvn = _rms_fwd(("row", proj, 128, C_CKV[0] // 128), w["mla_kv_norm"], name=n("kv_norm"), rows=s, width=128)
    q_ext = _mm(cqn, w["w_uq"], name=n("uq"))
    kv_ext = _mm(ckvn, w["w_ukv"], name=n("ukv"))

    def prep(qe, kvk, kvv, kr, krp, tqv, tc, ts):
        kb = kr * tc + krp * ts
        lane = lax.broadcasted_iota(jnp.int32, kvv.shape, 1) & (LANES - 1)
        return qe * tqv, kvk + jnp.tile(kb, (1, MLA_HEADS)), jnp.where(lane == HEAD, 1.0, kvv)

    qm, km, vm = _rowwise(
        prep, name=n("mla_prep"), rows=s,
        ins=[("row", q_ext, 512, 0), ("row", kv_ext, 512, 0), ("row", kv_ext, 512, 1),
             ("row", proj, 128, C_KR[0] // 128), ("row", proj, 128, C_KRP[0] // 128),
             ("row", tq, 512, 0), ("row", tkc, 128, 0), ("row", tks, 128, 0)],
        outs=[(512, BF16)] * 3)
    y_a, lse_a = _mla_fwd(qm, km, vm, name=n("mla_fwd"), s=s, nh=MLA_HEADS)
    y_b = _conv_fwd(proj, w["conv_w"], name=n("conv_fwd"), s=s)
    y_c = _pool_fwd(proj, w["pool_wbd"], w["pool_scale"], name=n("pool_fwd"), s=s)
    y_d, lse_d = _swa_fwd(proj, w["hp_swa"], name=n("swa_fwd"), s=s, scale=1.0 / math.sqrt(HEAD))

    def mix(ya, yb, yc, yd, mn):
        return (jnp.concatenate([_gn(ya), _gn(yb), _gn(yc), _gn(yd)], axis=1) * mn,)

    mixed = _rowwise(mix, name=n("group_norm"), rows=s,
                     ins=[("row", y_a, 512, 0), ("row", y_b, 256, 0), ("row", y_c, 256, 0), ("row", y_d, 512, 0),
                          ("full", w["mix_norm"])], outs=[(D_MIX_EXT, BF16)])[0]
    x1 = _mm(mixed, w["w_o"], res=x, name=n("out_proj"))
    h2 = _rms_fwd(("row", x1, D_MODEL, 0), w["ffn_norm"], name=n("ffn_norm"), rows=s, width=D_MODEL)
    gu = _mm(h2, w["w_gate_up"], name=n("gate_up"))

    def swiglu(g, u):
        return (g * jax.nn.sigmoid(g) * u,)

    act = _rowwise(swiglu, name=n("swiglu"), rows=s, tm=256,
                   ins=[("row", gu, D_FF, 0), ("row", gu, D_FF, 1)], outs=[(D_FF, BF16)])[0]
    x2 = _mm(act, w["w_down"], res=x1, name=n("down"))
    saved = dict(x=x, h=h, proj=proj, cqn=cqn, ckvn=ckvn, qm=qm, km=km, vm=vm, y_a=y_a, lse_a=lse_a, y_b=y_b, y_c=y_c,
                 y_d=y_d, lse_d=lse_d, mixed=mixed, x1=x1, h2=h2, gu=gu, act=act)
    return x2, saved


def _layer_bwd(dx2, sv, w, tabs, l):
    s = dx2.shape[0]
    tq, tkc, tks = tabs
    n = lambda t: f"l{l}_{t}"
    g = {}
    dact = _mm(dx2, w["w_down"], tb=True, name=n("d_act"))
    g["w_down"] = _mm(sv["act"], dx2, ta=True, name=n("dw_down"))

    def swiglu_bwd(gt, u, da):
        sg = jax.nn.sigmoid(gt)
        return (jnp.concatenate([da * u * sg * (1.0 + gt * (1.0 - sg)), da * gt * sg], axis=1),)

    dgu = _rowwise(swiglu_bwd, name=n("swiglu_bwd"), rows=s, tm=256,
                   ins=[("row", sv["gu"], D_FF, 0), ("row", sv["gu"], D_FF, 1), ("row", dact, D_FF, 0)],
                   outs=[(2 * D_FF, BF16)])[0]
    dh2 = _mm(dgu, w["w_gate_up"], tb=True, name=n("d_h2"))
    g["w_gate_up"] = _mm(sv["h2"], dgu, ta=True, name=n("dw_gate_up"))
    dx1, g["ffn_norm"] = _rms_bwd(("row", sv["x1"], D_MODEL, 0), w["ffn_norm"], dh2, dx2, name=n("ffn_norm_bwd"),
                                  rows=s, width=D_MODEL, out_dtype=F32)
    dmixed = _mm(dx1, w["w_o"], tb=True, name=n("d_mixed"))
    g["w_o"] = _mm(sv["mixed"], dx1, ta=True, name=n("dw_o"))

    def mix_bwd(dm, ya, yb, yc, yd, mn):
        outs, dmn = [], []
        for y, lo, hi in ((ya, 0, 512), (yb, 512, 768), (yc, 768, 1024), (yd, 1024, 1536)):
            r = _rstd(y, D_GROUP)
            nrm = y * r
            dmg = dm[:, lo:hi]
            dn = dmg * mn[:, lo:hi]
            outs.append(r * (dn - nrm * (jnp.sum(dn * nrm, axis=-1, keepdims=True) * (1.0 / D_GROUP))))
            dmn.append(jnp.sum(dmg * nrm, axis=0, keepdims=True))
        return (*outs, jnp.concatenate(dmn, axis=1))

    dy_a, dy_b, dy_c, dy_d, g["mix_norm"] = _rowwise(
        mix_bwd, name=n("group_norm_bwd"), rows=s,
        ins=[("row", dmixed, D_MIX_EXT, 0), ("row", sv["y_a"], 512, 0), ("row", sv["y_b"], 256, 0),
             ("row", sv["y_c"], 256, 0), ("row", sv["y_d"], 512, 0), ("full", w["mix_norm"])],
        outs=[(512, F32), (256, F32), (256, F32), (512, F32)], reds=[(1, D_MIX_EXT)])

    proj = sv["proj"]
    dq_sw, dk_sw, dv_sw, dsink = _swa_bwd(proj, sv["y_d"], dy_d, sv["lse_d"], w["hp_swa"], name=n("swa_bwd"), s=s,
                                          scale=1.0 / math.sqrt(HEAD))
    g["swa_sinks"] = dsink[:, 0, 0]

    qkv = (sv["qm"], sv["km"], sv["vm"])
    dqm, dvec_a = _mla_dq(*qkv, sv["y_a"], dy_a, sv["lse_a"], name=n("mla_dq"), s=s, nh=MLA_HEADS)
    dkm, dvm = _mla_dkv(*qkv, dy_a, sv["lse_a"].reshape(MLA_HEADS, 1, s), dvec_a.reshape(MLA_HEADS, 1, s),
                        name=n("mla_dkv"), s=s, nh=MLA_HEADS)

    def prep_bwd(dq, dk, tqv, tc, ts):
        dkb = dk[:, 0:128] + dk[:, 128:256] + dk[:, 256:384] + dk[:, 384:512]
        return dq * tqv, dk, dkb * tc, dkb * ts

    dq_ext, dkvk, dkr, dkrp = _rowwise(
        prep_bwd, name=n("mla_prep_bwd"), rows=s,
        ins=[("row", dqm, 512, 0), ("row", dkm, 512, 0), ("row", tq, 512, 0), ("row", tkc, 128, 0), ("row", tks, 128, 0)],
        outs=[(512, BF16), (512, BF16), (128, BF16), (128, BF16)])
    dkv_ext = jnp.concatenate([dkvk, dvm], axis=1)
    dcqn = _mm(dq_ext, w["w_uq"], tb=True, name=n("d_cqn"))
    g["w_uq"] = _mm(sv["cqn"], dq_ext, ta=True, name=n("dw_uq"))
    dckvn = _mm(dkv_ext, w["w_ukv"], tb=True, name=n("d_ckvn"))
    g["w_ukv"] = _mm(sv["ckvn"], dkv_ext, ta=True, name=n("dw_ukv"))
    dcq, g["mla_q_norm"] = _rms_bwd(("row", proj, 256, C_CQ[0] // 256), w["mla_q_norm"], dcqn, None,
                                    name=n("q_norm_bwd"), rows=s, width=256, out_dtype=BF16)
    dckv, g["mla_kv_norm"] = _rms_bwd(("row", proj, 128, C_CKV[0] // 128), w["mla_kv_norm"], dckvn, None,
                                      name=n("kv_norm_bwd"), rows=s, width=128, out_dtype=BF16)

    dgb, dgc, duc, g["conv_w"] = _conv_bwd(dy_b, proj, w["conv_w"], name=n("conv_bwd"), s=s)
    dup, g["pool_wbd"], g["pool_scale"] = _pool_bwd(dy_c, proj, w["pool_wbd"], w["pool_scale"], name=n("pool_bwd"), s=s)

    dproj = jnp.concatenate([dq_sw, dcq, dgb, dgc, duc, dup, dk_sw.astype(BF16), dv_sw.astype(BF16), dckv, dkr, dkrp],
                            axis=1)
    dh = _mm(dproj, w["w_in"], tb=True, name=n("d_h"))
    g["w_in"] = _mm(sv["h"], dproj, ta=True, name=n("dw_in"))
    dx0, g["attn_norm"] = _rms_bwd(("row", sv["x"], D_MODEL, 0), w["attn_norm"], dh, dx1, name=n("attn_norm_bwd"),
                                   rows=s, width=D_MODEL, out_dtype=F32)
    return dx0, g


def _loss_head(x, target, g, *, s):
    def fn(xv, tv, gv):
        r = _rstd(xv, D_MODEL)
        e = xv * r * gv - tv
        part = jnp.sum(jnp.sum(e * e, axis=1, keepdims=True), axis=0, keepdims=True) * (0.5 / D_MODEL)
        dy = e * (1.0 / D_MODEL)
        dyg = dy * gv
        dx = r * dyg - xv * (r * r * r) * (jnp.sum(dyg * xv, axis=-1, keepdims=True) * (1.0 / D_MODEL))
        return dx, jnp.sum(dy * xv * r, axis=0, keepdims=True), jnp.broadcast_to(part, (1, LANES))

    return _rowwise(fn, name="loss_head", rows=s,
                    ins=[("row", x, D_MODEL, 0), ("row", target, D_MODEL, 0), ("full", g)],
                    outs=[(D_MODEL, F32)], reds=[(1, D_MODEL), (1, LANES)])


def _alibi_slopes(n):
    return np.asarray([2.0 ** (-8.0 * (i + 1) / n) for i in range(n)], dtype=np.float32)


def _layer_weights(full, rep, l):
    pw = rep["pool_w"][l]
    z = jnp.zeros((HEAD, HEAD), F32)
    wbd = jnp.stack([jnp.block([[pw[2 * j], z], [z, pw[2 * j + 1]]]) for j in range(2)])
    w_o = full["w_o"][l]
    w_o_ext = jnp.moveaxis(_ext_mix(jnp.moveaxis(w_o, 0, -1)), -1, 0)
    return dict(
        attn_norm=rep["attn_norm"][l][None], w_in=_ext_w_in(full["w_in"][l]), mla_q_norm=rep["mla_q_norm"][l][None],
        w_uq=_ext_w_uq(full["w_uq"][l]), mla_kv_norm=rep["mla_kv_norm"][l][None], w_ukv=_ext_w_ukv(full["w_ukv"][l]),
        conv_w=full["conv_w"][l], pool_wbd=wbd, pool_scale=rep["pool_scale"][l][None],
        mix_norm=_ext_mix(rep["mix_norm"][l])[None], w_o=w_o_ext, ffn_norm=rep["ffn_norm"][l][None],
        w_gate_up=full["w_gate_up"][l], w_down=full["w_down"][l],
        hp_swa=jnp.stack([jnp.asarray(_alibi_slopes(SWA_HEADS)), rep["swa_sinks"][l]], axis=1))


def _local_step(x, target, full, rep):
    s = x.shape[0]
    tabs = _rope_tables(s)
    ws = [_layer_weights(full, rep, l) for l in range(DEPTH)]
    saved = []
    for l in range(DEPTH):
        x, sv = _layer_fwd(x, ws[l], tabs, l)
        saved.append(sv)
    dx, d_final, loss = _loss_head(x, target, rep["final_norm"][None], s=s)
    layer_grads = [None] * DEPTH
    for l in reversed(range(DEPTH)):
        dx, layer_grads[l] = _layer_bwd(dx, saved[l], ws[l], tabs, l)

    def stack(fn):
        return jnp.stack([fn(layer_grads[l]) for l in range(DEPTH)])

    def layers(fn):
        return [fn(layer_grads[l]) for l in range(DEPTH)]

    grads = dict(
        attn_norm=stack(lambda g: g["attn_norm"][0]), w_in=layers(lambda g: _fold_w_in(g["w_in"])),
        mla_q_norm=stack(lambda g: g["mla_q_norm"][0]), w_uq=layers(lambda g: _fold_w_uq(g["w_uq"])),
        mla_kv_norm=stack(lambda g: g["mla_kv_norm"][0]), w_ukv=layers(lambda g: _fold_w_ukv(g["w_ukv"])),
        conv_w=layers(lambda g: g["conv_w"]),
        pool_w=stack(lambda g: jnp.stack([g["pool_wbd"][j // 2][HEAD * (j % 2):HEAD * (j % 2 + 1),
                                                                  HEAD * (j % 2):HEAD * (j % 2 + 1)] for j in range(4)])),
        pool_scale=stack(lambda g: g["pool_scale"][0]), swa_sinks=stack(lambda g: g["swa_sinks"]),
        mix_norm=stack(lambda g: _fold_mix(g["mix_norm"][0])),
        w_o=layers(lambda g: jnp.moveaxis(_fold_mix(jnp.moveaxis(g["w_o"], 0, -1)), -1, 0)),
        ffn_norm=stack(lambda g: g["ffn_norm"][0]), w_gate_up=layers(lambda g: g["w_gate_up"]),
        w_down=layers(lambda g: g["w_down"]), final_norm=d_final[0])
    return loss, dx, grads


def kernel(x, attn_norm, w_in, mla_q_norm, w_uq, mla_kv_norm, w_ukv, conv_w, pool_w, pool_scale, swa_sinks, mix_norm, w_o, ffn_norm, w_gate_up, w_down, final_norm, loss_target, m_attn_norm, m_w_in, m_mla_q_norm, m_w_uq, m_mla_kv_norm, m_w_ukv, m_conv_w, m_pool_w, m_pool_scale, m_swa_sinks, m_mix_norm, m_w_o, m_ffn_norm, m_w_gate_up, m_w_down, m_final_norm, v_attn_norm, v_w_in, v_mla_q_norm, v_w_uq, v_mla_kv_norm, v_w_ukv, v_conv_w, v_pool_w, v_pool_scale, v_swa_sinks, v_mix_norm, v_w_o, v_ffn_norm, v_w_gate_up, v_w_down, v_final_norm):
    given = dict(locals())
    sh_names = [nm for nm, _, _ in SHARDED]
    sh_axis = {nm: ax - 1 for nm, _, ax in SHARDED}
    rep_names = [nm for nm, _ in REPLICATED]
    rep_shapes = [shp for _, shp in REPLICATED]
    pairs = [(nm, l) for nm in sh_names for l in range(DEPTH)]

    def wire(nm, l):
        if nm == "conv_w":
            return lax.bitcast_convert_type(given[nm][l], BF16).reshape(3, -1)
        return given[nm][l].astype(BF16)

    gathered = dict(zip(pairs, _all_gather([wire(nm, l) for nm, l in pairs], name="gather_weights")))
    full = {nm: [None] * DEPTH for nm in sh_names}
    for (nm, l), g in gathered.items():
        if nm == "conv_w":
            g = lax.bitcast_convert_type(g.reshape(N_DEV, 3, -1, 2), F32)
        full[nm][l] = _shards_to_full(g, sh_axis[nm])
    rep = {nm: given[nm] for nm in rep_names if nm != "loss"}

    loss, grad_x, grads = _local_step(x[0], loss_target[0], full, rep)

    core = lax.axis_index("c").astype(jnp.int32).reshape(1)
    mine = [_full_to_shards(grads[nm][l], sh_axis[nm]) for nm, l in pairs]
    theirs = _exchange_sibling(mine, name="exchange_grads_sibling")
    chip_sums = [_add_own(g, o, core, name=f"chip_sum_{nm}_{l}") for (nm, l), g, o in zip(pairs, mine, theirs)]
    parts = dict(zip(pairs, _exchange_chips(chip_sums, name="exchange_grads_chips")))
    sh_out = {nm: _adamw([parts[nm, l] for l in range(DEPTH)], given[nm], given["m_" + nm], given["v_" + nm],
                         name=f"adamw_{nm}") for nm in sh_names}

    grads["loss"] = loss[0, :1]
    zero = jnp.zeros((1,), F32)
    small = _all_gather([_pack([grads[nm] for nm in rep_names])], name="gather_small_grads")
    packs = [_pack([given.get(pre + nm, zero) for nm in rep_names])[None] for pre in ("", "m_", "v_")]
    rep_out = [dict(zip(rep_names, _unpack(o[0], rep_shapes))) for o in _adamw(small, *packs, name="adamw_replicated")]

    out = [rep_out[0]["loss"][0], grad_x[None]]
    for i in range(4):
        out += [sh_out[nm][i] if nm in sh_axis else rep_out[i][nm] for nm in WEIGHT_ORDER]
    return tuple(out)
```

```python
import functools
import math

import numpy as np
import jax
import jax.numpy as jnp
from jax import lax
from jax.experimental import pallas as pl
from jax.experimental.pallas import tpu as pltpu

F32 = jnp.float32
BF16 = jnp.bfloat16

D_MODEL = 1024
DEPTH = 2
D_GROUP = 256
MLA_HEADS = 4
MLA_NOPE = 64
MLA_ROPE = 32
ROPE_THETA = 10000.0
POOL_WINDOWS = (2, 4, 8, 16)
SWA_HEADS = 4
SWA_KV_HEADS = 2
SWA_WINDOW = 128
D_FF = 2816
RMS_EPS = 1e-6
LANES = 128
HEAD = 64
VMEM_LIMIT = 48 * 1024 * 1024
NEG = -1e30

ADAM_LR = 0.001
ADAM_B1 = 0.9
ADAM_B2 = 0.999
ADAM_EPS = 1e-08
ADAM_WD = 0.01
ADAM_STEP = 10

N_DEV = 8
PACK_COLS = 1024

C_QSW, C_CQ, C_GB, C_GC, C_UCONV, C_UPOOL = (0, 512), (512, 256), (768, 256), (1024, 256), (1280, 256), (1536, 256)
C_KSW, C_VSW, C_CKV, C_KR, C_KRP = (1792, 256), (2048, 256), (2304, 128), (2432, 128), (2560, 128)
D_IN_EXT = 2688
D_MIX_EXT = 1536

SHARDED = (("w_in", (DEPTH, 1024, 244), 2), ("w_uq", (DEPTH, 256, 48), 2), ("w_ukv", (DEPTH, 128, 64), 2),
           ("conv_w", (DEPTH, 3, 32), 2), ("w_o", (DEPTH, 128, 1024), 1), ("w_gate_up", (DEPTH, 1024, 704), 2),
           ("w_down", (DEPTH, 352, 1024), 1))
REPLICATED = (("attn_norm", (DEPTH, 1024)), ("mla_q_norm", (DEPTH, 256)), ("mla_kv_norm", (DEPTH, 128)),
              ("pool_w", (DEPTH, 4, 64, 64)), ("pool_scale", (DEPTH, 256)), ("swa_sinks", (DEPTH, 4)),
              ("mix_norm", (DEPTH, 1024)), ("ffn_norm", (DEPTH, 1024)), ("final_norm", (1024,)), ("loss", (1,)))
WEIGHT_ORDER = ("attn_norm", "w_in", "mla_q_norm", "w_uq", "mla_kv_norm", "w_ukv", "conv_w", "pool_w", "pool_scale",
                "swa_sinks", "mix_norm", "w_o", "ffn_norm", "w_gate_up", "w_down", "final_norm")


def _params(sem):
    return pltpu.CompilerParams(dimension_semantics=sem, vmem_limit_bytes=VMEM_LIMIT)


def _pick(dim, target):
    if dim <= target:
        return dim
    best = None
    for t in range(LANES, target + 1, LANES):
        if dim % t == 0:
            best = t
    assert best is not None, (dim, target)
    return best


def _mm(a, b, *, name, ta=False, tb=False, res=None, out_dtype=F32, tm=512, tn=1024, tk=1024):
    m, k = (a.shape[1], a.shape[0]) if ta else a.shape
    n = b.shape[0] if tb else b.shape[1]
    assert (b.shape[1] if tb else b.shape[0]) == k
    tm, tn, tk = _pick(m, tm), _pick(n, tn), _pick(k, tk)
    nk = k // tk
    dims = (((0 if ta else 1,), (1 if tb else 0,)), ((), ()))
    has_res = res is not None

    def body(*refs):
        a_ref, b_ref = refs[0], refs[1]
        o_ref, acc_ref = refs[-2], refs[-1]
        kk = pl.program_id(2)

        @pl.when(kk == 0)
        def _():
            acc_ref[...] = jnp.zeros_like(acc_ref)

        acc_ref[...] += lax.dot_general(a_ref[...].astype(BF16), b_ref[...].astype(BF16), dims,
                                        preferred_element_type=F32)

        @pl.when(kk == nk - 1)
        def _():
            r = acc_ref[...]
            if has_res:
                r = r + refs[2][...]
            o_ref[...] = r.astype(o_ref.dtype)

    a_spec = pl.BlockSpec((tk, tm), lambda i, j, kk: (kk, i)) if ta else pl.BlockSpec((tm, tk), lambda i, j, kk: (i, kk))
    b_spec = pl.BlockSpec((tn, tk), lambda i, j, kk: (j, kk)) if tb else pl.BlockSpec((tk, tn), lambda i, j, kk: (kk, j))
    in_specs, args = [a_spec, b_spec], [a, b]
    if has_res:
        in_specs.append(pl.BlockSpec((tm, tn), lambda i, j, kk: (i, j)))
        args.append(res)
    return pl.pallas_call(
        body, name=name, grid=(m // tm, n // tn, nk), in_specs=in_specs,
        out_specs=pl.BlockSpec((tm, tn), lambda i, j, kk: (i, j)),
        out_shape=jax.ShapeDtypeStruct((m, n), out_dtype),
        scratch_shapes=[pltpu.VMEM((tm, tn), F32)],
        compiler_params=_params(("parallel", "parallel", "arbitrary")),
    )(*args)


def _rowwise(fn, *, name, rows, ins, outs, reds=(), tm=512):
    tm = min(tm, rows)
    assert rows % tm == 0
    n_in, n_out = len(ins), len(outs)

    def body(*refs):
        vals = [r[...] for r in refs[:n_in]]
        res = fn(*vals)
        for r, v in zip(refs[n_in:n_in + n_out], res[:n_out]):
            r[...] = v.astype(r.dtype)
        if reds:
            @pl.when(pl.program_id(0) == 0)
            def _():
                for r in refs[n_in + n_out:]:
                    r[...] = jnp.zeros_like(r)

            for r, v in zip(refs[n_in + n_out:], res[n_out:]):
                r[...] += v

    in_specs, args = [], []
    for spec in ins:
        if spec[0] == "row":
            _, arr, width, blk = spec
            in_specs.append(pl.BlockSpec((tm, width), functools.partial(lambda i, blk: (i, blk), blk=blk)))
        else:
            arr = spec[1]
            in_specs.append(pl.BlockSpec(arr.shape, functools.partial(lambda i, nd: (0,) * nd, nd=arr.ndim)))
        args.append(arr)
    out_specs = [pl.BlockSpec((tm, w), lambda i: (i, 0)) for w, _ in outs]
    out_shape = [jax.ShapeDtypeStruct((rows, w), dt) for w, dt in outs]
    out_specs += [pl.BlockSpec((r, w), lambda i: (0, 0)) for r, w in reds]
    out_shape += [jax.ShapeDtypeStruct((r, w), F32) for r, w in reds]
    return pl.pallas_call(body, name=name, grid=(rows // tm,), in_specs=in_specs, out_specs=out_specs,
                          out_shape=out_shape, compiler_params=_params(("arbitrary",)))(*args)


def _rstd(x, n):
    return lax.rsqrt(jnp.sum(x * x, axis=-1, keepdims=True) * (1.0 / n) + RMS_EPS)


def _rms_fwd(x_spec, g, *, name, rows, width):
    def fn(x, gv):
        return (x * _rstd(x, width) * gv,)
    return _rowwise(fn, name=name, rows=rows, ins=[x_spec, ("full", g)], outs=[(width, BF16)])[0]


def _rms_bwd(x_spec, g, dy, res, *, name, rows, width, out_dtype):
    def fn(x, gv, dyv, *rest):
        r = _rstd(x, width)
        dyg = dyv * gv
        dx = r * dyg - x * (r * r * r) * (jnp.sum(dyg * x, axis=-1, keepdims=True) * (1.0 / width))
        if rest:
            dx = dx + rest[0]
        return dx, jnp.sum(dyv * x * r, axis=0, keepdims=True)

    ins = [x_spec, ("full", g), ("row", dy, width, 0)]
    if res is not None:
        ins.append(("row", res, width, 0))
    return _rowwise(fn, name=name, rows=rows, ins=ins, outs=[(width, out_dtype)], reds=[(1, width)])


NT_DIMS = (((1,), (1,)), ((), ()))
TN_DIMS = (((0,), (0,)), ((), ()))


def _mla_tile(s):
    return min(512, s)


def _mla_fwd(qa, ka, va, *, name, s, nh):
    t = _mla_tile(s)
    nq = s // t

    def body(q_ref, k_ref, v_ref, o_ref, lse_ref, m_s, acc_s):
        i, j = pl.program_id(1), pl.program_id(2)

        @pl.when(j == 0)
        def _():
            m_s[...] = jnp.full_like(m_s, NEG)
            acc_s[...] = jnp.zeros_like(acc_s)

        def step(diag):
            sc = lax.dot_general(q_ref[...], k_ref[...], NT_DIMS, preferred_element_type=F32)
            if diag:
                dist = lax.broadcasted_iota(jnp.int32, sc.shape, 0) - lax.broadcasted_iota(jnp.int32, sc.shape, 1)
                sc = jnp.where(dist >= 0, sc, NEG)
            m_prev = m_s[...]
            m_new = jnp.maximum(m_prev, jnp.max(sc, axis=1, keepdims=True))
            p = jnp.exp(sc - m_new).astype(BF16)
            acc_s[...] = jnp.exp(m_prev - m_new) * acc_s[...] + jnp.dot(p, v_ref[...], preferred_element_type=F32)
            m_s[...] = m_new

        pl.when(j < i)(functools.partial(step, False))
        pl.when(j == i)(functools.partial(step, True))

        @pl.when(j == nq - 1)
        def _():
            acc = acc_s[...]
            lane = lax.broadcasted_iota(jnp.int32, acc.shape, 1)
            l = jnp.sum(jnp.where(lane == HEAD, acc, 0.0), axis=1, keepdims=True)
            o_ref[...] = jnp.where(lane < HEAD, acc / l, 0.0)
            lse_ref[...] = m_s[...] + jnp.log(l)

    kv_map = lambda h, i, j: (jnp.minimum(j, i), h)
    return pl.pallas_call(
        body, name=name, grid=(nh, nq, nq),
        in_specs=[pl.BlockSpec((t, LANES), lambda h, i, j: (i, h)), pl.BlockSpec((t, LANES), kv_map),
                  pl.BlockSpec((t, LANES), kv_map)],
        out_specs=[pl.BlockSpec((t, LANES), lambda h, i, j: (i, h)),
                   pl.BlockSpec((None, t, 1), lambda h, i, j: (h, i, 0))],
        out_shape=[jax.ShapeDtypeStruct((s, nh * LANES), F32), jax.ShapeDtypeStruct((nh, s, 1), F32)],
        scratch_shapes=[pltpu.VMEM((t, 1), F32), pltpu.VMEM((t, LANES), F32)],
        compiler_params=_params(("parallel", "parallel", "arbitrary")),
    )(qa, ka, va)


def _mla_dq(qa, ka, va, o, do, lse, *, name, s, nh):
    t = _mla_tile(s)
    nq = s // t

    def body(q_ref, k_ref, v_ref, o_ref, do_ref, lse_ref, dq_ref, d_ref, acc_s):
        i, j = pl.program_id(1), pl.program_id(2)

        @pl.when(j == 0)
        def _():
            acc_s[...] = jnp.zeros_like(acc_s)
            d_ref[...] = jnp.sum(do_ref[...] * o_ref[...], axis=1, keepdims=True)

        def step(diag):
            k = k_ref[...]
            sc = lax.dot_general(q_ref[...], k, NT_DIMS, preferred_element_type=F32)
            if diag:
                dist = lax.broadcasted_iota(jnp.int32, sc.shape, 0) - lax.broadcasted_iota(jnp.int32, sc.shape, 1)
                sc = jnp.where(dist >= 0, sc, NEG)
            p = jnp.exp(sc - lse_ref[...])
            dp = lax.dot_general(do_ref[...].astype(BF16), v_ref[...], NT_DIMS, preferred_element_type=F32)
            ds = p * (dp - d_ref[...])
            acc_s[...] += jnp.dot(ds.astype(BF16), k, preferred_element_type=F32)

        pl.when(j < i)(functools.partial(step, False))
        pl.when(j == i)(functools.partial(step, True))

        @pl.when(j == nq - 1)
        def _():
            dq_ref[...] = acc_s[...]

    kv_map = lambda h, i, j: (jnp.minimum(j, i), h)
    q_map = lambda h, i, j: (i, h)
    vec_map = lambda h, i, j: (h, i, 0)
    return pl.pallas_call(
        body, name=name, grid=(nh, nq, nq),
        in_specs=[pl.BlockSpec((t, LANES), q_map), pl.BlockSpec((t, LANES), kv_map), pl.BlockSpec((t, LANES), kv_map),
                  pl.BlockSpec((t, LANES), q_map), pl.BlockSpec((t, LANES), q_map), pl.BlockSpec((None, t, 1), vec_map)],
        out_specs=[pl.BlockSpec((t, LANES), q_map), pl.BlockSpec((None, t, 1), vec_map)],
        out_shape=[jax.ShapeDtypeStruct((s, nh * LANES), F32), jax.ShapeDtypeStruct((nh, s, 1), F32)],
        scratch_shapes=[pltpu.VMEM((t, LANES), F32)],
        compiler_params=_params(("parallel", "parallel", "arbitrary")),
    )(qa, ka, va, o, do, lse)


def _mla_dkv(qa, ka, va, do, lse_row, d_row, *, name, s, nh):
    t = _mla_tile(s)
    nq = s // t

    def body(q_ref, k_ref, v_ref, do_ref, lse_ref, d_ref, dk_ref, dv_ref, dk_s, dv_s):
        kb, j = pl.program_id(1), pl.program_id(2)

        @pl.when(j == 0)
        def _():
            dk_s[...] = jnp.zeros_like(dk_s)
            dv_s[...] = jnp.zeros_like(dv_s)

        def step(diag):
            q = q_ref[...]
            do_b = do_ref[...].astype(BF16)
            sc = lax.dot_general(k_ref[...], q, NT_DIMS, preferred_element_type=F32)
            if diag:
                dist = lax.broadcasted_iota(jnp.int32, sc.shape, 1) - lax.broadcasted_iota(jnp.int32, sc.shape, 0)
                sc = jnp.where(dist >= 0, sc, NEG)
            p = jnp.exp(sc - lse_ref[...])
            dv_s[...] += jnp.dot(p.astype(BF16), do_b, preferred_element_type=F32)
            dp = lax.dot_general(v_ref[...], do_b, NT_DIMS, preferred_element_type=F32)
            ds = p * (dp - d_ref[...])
            dk_s[...] += jnp.dot(ds.astype(BF16), q, preferred_element_type=F32)

        pl.when(j > kb)(functools.partial(step, False))
        pl.when(j == kb)(functools.partial(step, True))

        @pl.when(j == nq - 1)
        def _():
            dk_ref[...] = dk_s[...]
            dv_ref[...] = dv_s[...].astype(dv_ref.dtype)

    q_map = lambda h, kb, j: (jnp.maximum(j, kb), h)
    kv_map = lambda h, kb, j: (kb, h)
    row_map = lambda h, kb, j: (h, 0, jnp.maximum(j, kb))
    return pl.pallas_call(
        body, name=name, grid=(nh, nq, nq),
        in_specs=[pl.BlockSpec((t, LANES), q_map), pl.BlockSpec((t, LANES), kv_map), pl.BlockSpec((t, LANES), kv_map),
                  pl.BlockSpec((t, LANES), q_map), pl.BlockSpec((None, 1, t), row_map), pl.BlockSpec((None, 1, t), row_map)],
        out_specs=[pl.BlockSpec((t, LANES), kv_map), pl.BlockSpec((t, LANES), kv_map)],
        out_shape=[jax.ShapeDtypeStruct((s, nh * LANES), F32), jax.ShapeDtypeStruct((s, nh * LANES), BF16)],
        scratch_shapes=[pltpu.VMEM((t, LANES), F32), pltpu.VMEM((t, LANES), F32)],
        compiler_params=_params(("parallel", "parallel", "arbitrary")),
    )(qa, ka, va, do, lse_row, d_row)


SWA_PIECE = 128
SWA_KEYS = 2 * SWA_PIECE


def _swa_block(s):
    return min(512, s)


def _swa_piece(hp_ref, h, q, k_ref, v_ref, qpos0, scale):
    kstart = pl.multiple_of(jnp.maximum(qpos0 - SWA_PIECE, 0), SWA_PIECE)
    k = k_ref[pl.ds(kstart, SWA_KEYS), :].astype(BF16)
    v = v_ref[pl.ds(kstart, SWA_KEYS), :].astype(BF16)
    sc = lax.dot_general(q, k, NT_DIMS, preferred_element_type=F32)
    dist = (qpos0 + lax.broadcasted_iota(jnp.int32, sc.shape, 0)) - (kstart + lax.broadcasted_iota(jnp.int32, sc.shape, 1))
    sc = sc * scale - hp_ref[h, 0] * dist.astype(F32)
    sc = jnp.where((dist >= 0) & (dist < SWA_WINDOW), sc, NEG)
    return kstart, k, v, sc


def _swa_fwd(proj, hp, *, name, s, scale):
    tb = _swa_block(s)
    group = SWA_HEADS // SWA_KV_HEADS
    q_off, k_off, v_off = C_QSW[0] // LANES, C_KSW[0] // LANES, C_VSW[0] // LANES

    def body(hp_ref, q_ref, k_ref, v_ref, o_ref, lse_ref):
        h, i = pl.program_id(0), pl.program_id(1)
        sink = hp_ref[h, 1]
        for r in range(0, tb, SWA_PIECE):
            rows = pl.ds(r, SWA_PIECE)
            _, _, v, sc = _swa_piece(hp_ref, h, q_ref[rows, :].astype(BF16), k_ref, v_ref, i * tb + r, scale)
            m = jnp.maximum(jnp.max(sc, axis=1, keepdims=True), sink)
            p = jnp.exp(sc - m)
            l = jnp.sum(p, axis=1, keepdims=True) + jnp.exp(sink - m)
            o_ref[rows, :] = jnp.dot(p.astype(BF16), v, preferred_element_type=F32) / l
            lse_ref[rows, :] = m + jnp.log(l)

    whole = lambda off: pl.BlockSpec((s, LANES), lambda h, i: (0, off + h // group))
    return pl.pallas_call(
        body, name=name, grid=(SWA_HEADS, s // tb),
        in_specs=[pl.BlockSpec(memory_space=pltpu.SMEM), pl.BlockSpec((tb, LANES), lambda h, i: (i, q_off + h)),
                  whole(k_off), whole(v_off)],
        out_specs=[pl.BlockSpec((tb, LANES), lambda h, i: (i, h)), pl.BlockSpec((None, tb, 1), lambda h, i: (h, i, 0))],
        out_shape=[jax.ShapeDtypeStruct((s, SWA_HEADS * LANES), F32), jax.ShapeDtypeStruct((SWA_HEADS, s, 1), F32)],
        compiler_params=_params(("parallel", "parallel")),
    )(hp, proj, proj, proj)


def _swa_bwd(proj, o, do, lse, hp, *, name, s, scale):
    tb = _swa_block(s)
    nqb = s // tb
    group = SWA_HEADS // SWA_KV_HEADS
    q_off, k_off, v_off = C_QSW[0] // LANES, C_KSW[0] // LANES, C_VSW[0] // LANES

    def body(hp_ref, q_ref, k_ref, v_ref, o_ref, do_ref, lse_ref, dq_ref, dk_ref, dv_ref, dsink_ref):
        kh, g, i = pl.program_id(0), pl.program_id(1), pl.program_id(2)
        h = kh * group + g
        sink = hp_ref[h, 1]

        @pl.when((g == 0) & (i == 0))
        def _():
            dk_ref[...] = jnp.zeros_like(dk_ref)
            dv_ref[...] = jnp.zeros_like(dv_ref)

        @pl.when(i == 0)
        def _():
            dsink_ref[...] = jnp.zeros_like(dsink_ref)

        for r in range(0, tb, SWA_PIECE):
            rows = pl.ds(r, SWA_PIECE)
            q = q_ref[rows, :].astype(BF16)
            dov = do_ref[rows, :]
            do_b = dov.astype(BF16)
            lse_r = lse_ref[rows, :]
            d_r = jnp.sum(dov * o_ref[rows, :], axis=1, keepdims=True)
            kstart, k, v, sc = _swa_piece(hp_ref, h, q, k_ref, v_ref, i * tb + r, scale)
            p = jnp.exp(sc - lse_r)
            dp = lax.dot_general(do_b, v, NT_DIMS, preferred_element_type=F32)
            ds = (p * (dp - d_r)).astype(BF16)
            dq_ref[rows, :] = (jnp.dot(ds, k, preferred_element_type=F32) * scale).astype(dq_ref.dtype)
            win = pl.ds(kstart, SWA_KEYS)
            dk_ref[win, :] += lax.dot_general(ds, q, TN_DIMS, preferred_element_type=F32) * scale
            dv_ref[win, :] += lax.dot_general(p.astype(BF16), do_b, TN_DIMS, preferred_element_type=F32)
            part = jnp.sum(-jnp.exp(sink - lse_r) * d_r, axis=0, keepdims=True)
            dsink_ref[...] += jnp.broadcast_to(part, (1, LANES))

    whole = lambda off: pl.BlockSpec((s, LANES), lambda kh, g, i: (0, off + kh))
    q_map = lambda kh, g, i: (i, kh * group + g)
    return pl.pallas_call(
        body, name=name, grid=(SWA_KV_HEADS, group, nqb),
        in_specs=[pl.BlockSpec(memory_space=pltpu.SMEM),
                  pl.BlockSpec((tb, LANES), lambda kh, g, i: (i, q_off + kh * group + g)), whole(k_off), whole(v_off),
                  pl.BlockSpec((tb, LANES), q_map), pl.BlockSpec((tb, LANES), q_map),
                  pl.BlockSpec((None, tb, 1), lambda kh, g, i: (kh * group + g, i, 0))],
        out_specs=[pl.BlockSpec((tb, LANES), q_map), whole(0), whole(0),
                   pl.BlockSpec((None, 1, LANES), lambda kh, g, i: (kh * group + g, 0, 0))],
        out_shape=[jax.ShapeDtypeStruct((s, SWA_HEADS * LANES), BF16),
                   jax.ShapeDtypeStruct((s, SWA_KV_HEADS * LANES), F32), jax.ShapeDtypeStruct((s, SWA_KV_HEADS * LANES), F32),
                   jax.ShapeDtypeStruct((SWA_HEADS, 1, LANES), F32)],
        compiler_params=_params(("parallel", "arbitrary", "arbitrary")),
    )(hp, proj, proj, proj, o, do, lse)


def _shift_down(z, k):
    rows = lax.broadcasted_iota(jnp.int32, z.shape, 0)
    return jnp.where(rows >= k, pltpu.roll(z, k, 0), 0.0)


def _shift_up(z, k):
    n = z.shape[0]
    rows = lax.broadcasted_iota(jnp.int32, z.shape, 0)
    return jnp.where(rows < n - k, pltpu.roll(z, n - k, 0), 0.0)


def _rows3(a, b, c):
    r = lax.broadcasted_iota(jnp.int32, (3, a.shape[1]), 0)
    return jnp.where(r == 0, a, jnp.where(r == 1, b, c))


def _col_spec(s, off):
    return pl.BlockSpec((s, LANES), functools.partial(lambda j, off: (0, off + j), off=off))


def _conv_fwd(proj, conv_w, *, name, s):
    def body(gb_ref, gc_ref, u_ref, w_ref, y_ref):
        w0, w1, w2 = w_ref[0:1, :], w_ref[1:2, :], w_ref[2:3, :]
        z = gc_ref[...] * u_ref[...]
        c = w2 * z + w1 * _shift_down(z, 1) + w0 * _shift_down(z, 2)
        y_ref[...] = gb_ref[...] * c

    return pl.pallas_call(
        body, name=name, grid=(2,),
        in_specs=[_col_spec(s, C_GB[0] // LANES), _col_spec(s, C_GC[0] // LANES), _col_spec(s, C_UCONV[0] // LANES),
                  pl.BlockSpec((3, LANES), lambda j: (0, j))],
        out_specs=_col_spec(s, 0), out_shape=jax.ShapeDtypeStruct((s, D_GROUP), F32),
        compiler_params=_params(("parallel",)),
    )(proj, proj, proj, conv_w)


def _conv_bwd(dy, proj, conv_w, *, name, s):
    def body(dy_ref, gb_ref, gc_ref, u_ref, w_ref, dgb_ref, dgc_ref, du_ref, dw_ref):
        w0, w1, w2 = w_ref[0:1, :], w_ref[1:2, :], w_ref[2:3, :]
        gc, u, dyv = gc_ref[...], u_ref[...], dy_ref[...]
        z = gc * u
        z1, z2 = _shift_down(z, 1), _shift_down(z, 2)
        c = w2 * z + w1 * z1 + w0 * z2
        dgb_ref[...] = (dyv * c).astype(dgb_ref.dtype)
        dc = dyv * gb_ref[...]
        dz = w2 * dc + w1 * _shift_up(dc, 1) + w0 * _shift_up(dc, 2)
        dgc_ref[...] = (dz * u).astype(dgc_ref.dtype)
        du_ref[...] = (dz * gc).astype(du_ref.dtype)
        dw_ref[...] = _rows3(jnp.sum(dc * z2, axis=0, keepdims=True), jnp.sum(dc * z1, axis=0, keepdims=True),
                             jnp.sum(dc * z, axis=0, keepdims=True))

    act = jax.ShapeDtypeStruct((s, D_GROUP), BF16)
    return pl.pallas_call(
        body, name=name, grid=(2,),
        in_specs=[_col_spec(s, 0), _col_spec(s, C_GB[0] // LANES), _col_spec(s, C_GC[0] // LANES),
                  _col_spec(s, C_UCONV[0] // LANES), pl.BlockSpec((3, LANES), lambda j: (0, j))],
        out_specs=[_col_spec(s, 0), _col_spec(s, 0), _col_spec(s, 0), pl.BlockSpec((3, LANES), lambda j: (0, j))],
        out_shape=[act, act, act, jax.ShapeDtypeStruct((3, D_GROUP), F32)],
        compiler_params=_params(("parallel",)),
    )(dy, proj, proj, proj, conv_w)


def _pool_select(j, lane, a2, a4, a8, a16):
    lo = lane < HEAD
    return jnp.where(j == 0, jnp.where(lo, a2, a4), jnp.where(lo, a8, a16))


def _pooled(u, j):
    s2 = u + _shift_down(u, 1)
    s4 = s2 + _shift_down(s2, 2)
    s8 = s4 + _shift_down(s4, 4)
    s16 = s8 + _shift_down(s8, 8)
    lane = lax.broadcasted_iota(jnp.int32, u.shape, 1)
    rows = lax.broadcasted_iota(jnp.int32, u.shape, 0)
    win = _pool_select(j, lane, 2, 4, 8, 16)
    count = jnp.minimum(rows + 1, win).astype(F32)
    return _pool_select(j, lane, s2, s4, s8, s16) / count - u, count


def _pool_fwd(proj, wbd, scale, *, name, s):
    def body(u_ref, w_ref, sc_ref, y_ref):
        pooled, _ = _pooled(u_ref[...], pl.program_id(0))
        y_ref[...] = jnp.dot(pooled.astype(BF16), w_ref[...].astype(BF16), preferred_element_type=F32) * sc_ref[...]

    return pl.pallas_call(
        body, name=name, grid=(2,),
        in_specs=[_col_spec(s, C_UPOOL[0] // LANES), pl.BlockSpec((None, LANES, LANES), lambda j: (j, 0, 0)),
                  pl.BlockSpec((1, LANES), lambda j: (0, j))],
        out_specs=_col_spec(s, 0), out_shape=jax.ShapeDtypeStruct((s, D_GROUP), F32),
        compiler_params=_params(("parallel",)),
    )(proj, wbd, scale)


def _pool_bwd(dy, proj, wbd, scale, *, name, s):
    def body(dy_ref, u_ref, w_ref, sc_ref, du_ref, dw_ref, dsc_ref):
        j = pl.program_id(0)
        pooled, count = _pooled(u_ref[...], j)
        pooled_b = pooled.astype(BF16)
        w_b = w_ref[...].astype(BF16)
        dyv = dy_ref[...]
        mixed = jnp.dot(pooled_b, w_b, preferred_element_type=F32)
        dsc_ref[...] = jnp.sum(dyv * mixed, axis=0, keepdims=True)
        dms = (dyv * sc_ref[...]).astype(BF16)
        dw_ref[...] = lax.dot_general(pooled_b, dms, (((0,), (0,)), ((), ())), preferred_element_type=F32)
        dpooled = lax.dot_general(dms, w_b, (((1,), (1,)), ((), ())), preferred_element_type=F32)
        r = dpooled / count
        a2 = r + _shift_up(r, 1)
        a4 = a2 + _shift_up(a2, 2)
        a8 = a4 + _shift_up(a4, 4)
        a16 = a8 + _shift_up(a8, 8)
        lane = lax.broadcasted_iota(jnp.int32, r.shape, 1)
        du_ref[...] = (_pool_select(j, lane, a2, a4, a8, a16) - dpooled).astype(du_ref.dtype)

    return pl.pallas_call(
        body, name=name, grid=(2,),
        in_specs=[_col_spec(s, 0), _col_spec(s, C_UPOOL[0] // LANES),
                  pl.BlockSpec((None, LANES, LANES), lambda j: (j, 0, 0)), pl.BlockSpec((1, LANES), lambda j: (0, j))],
        out_specs=[_col_spec(s, 0), pl.BlockSpec((None, LANES, LANES), lambda j: (j, 0, 0)),
                   pl.BlockSpec((1, LANES), lambda j: (0, j))],
        out_shape=[jax.ShapeDtypeStruct((s, D_GROUP), BF16), jax.ShapeDtypeStruct((2, LANES, LANES), F32),
                   jax.ShapeDtypeStruct((1, D_GROUP), F32)],
        compiler_params=_params(("parallel",)),
    )(dy, proj, wbd, scale)


def _mesh_pos():
    return lax.axis_index("x"), lax.axis_index("y"), lax.axis_index("c")


def _any_specs(n):
    return [pl.BlockSpec(memory_space=pl.ANY)] * n


def _all_gather(xs, *, name):
    n = len(xs)

    def body(*refs):
        x_refs, out_refs = refs[:n], refs[n:2 * n]
        send_sems, recv_sems, local_sems = refs[2 * n:]
        x, y, cc = _mesh_pos()
        me, sibling = (x, y, cc), (x, y, 1 - cc)
        chips = [(1 - x, y), (x, 1 - y), (1 - x, 1 - y)]

        def slot(a, px, py, pc):
            return out_refs[a].at[4 * px + 2 * py + pc]

        def copy(a, k, block, to, src=None):
            return pltpu.make_async_remote_copy(
                src_ref=slot(a, *block) if src is None else src, dst_ref=slot(a, *block), send_sem=send_sems.at[a, k],
                recv_sem=recv_sems.at[a, k], device_id=to, device_id_type=pl.DeviceIdType.MESH)

        mine = [pltpu.make_async_copy(x_refs[a], slot(a, *me), local_sems.at[a]) for a in range(n)]
        first = []
        for a in range(n):
            first.append(copy(a, 0, me, sibling, src=x_refs[a]))
            first += [copy(a, 1 + j, me, (*chip, cc), src=x_refs[a]) for j, chip in enumerate(chips)]
        for cp in mine + first:
            cp.start()
        passed = []
        for j, chip in enumerate(chips):
            for a in range(n):
                copy(a, 1 + j, (*chip, cc), me).wait_recv()
                passed.append(copy(a, 4 + j, (*chip, cc), sibling))
                passed[-1].start()
        for a in range(n):
            copy(a, 0, sibling, me).wait_recv()
        for j, chip in enumerate(chips):
            for a in range(n):
                copy(a, 4 + j, (*chip, 1 - cc), me).wait_recv()
        for cp in first + passed:
            cp.wait_send()
        for cp in mine:
            cp.wait()

    return pl.pallas_call(
        body, name=name, out_shape=[jax.ShapeDtypeStruct((N_DEV,) + a.shape, a.dtype) for a in xs],
        in_specs=_any_specs(n), out_specs=_any_specs(n),
        scratch_shapes=[pltpu.SemaphoreType.DMA((n, 7)), pltpu.SemaphoreType.DMA((n, 7)), pltpu.SemaphoreType.DMA((n,))],
    )(*xs)


def _plan_gather(src_refs, land_refs, send_sems, recv_sems):
    x, y, cc = _mesh_pos()
    me = 4 * x + 2 * y + cc
    plan = []
    for a, (src, land) in enumerate(zip(src_refs, land_refs)):
        for k in range(1, N_DEV):
            px, py, pc = (1 - x if k & 4 else x), (1 - y if k & 2 else y), (1 - cc if k & 1 else cc)
            i = a * (N_DEV - 1) + k - 1
            sems = dict(send_sem=send_sems.at[i], recv_sem=recv_sems.at[i], device_id=(px, py, pc),
                        device_id_type=pl.DeviceIdType.MESH)
            plan.append((pltpu.make_async_remote_copy(src_ref=src, dst_ref=land.at[me], **sems),
                         pltpu.make_async_remote_copy(src_ref=src, dst_ref=land.at[4 * px + 2 * py + pc], **sems)))
    return plan


def _plan_sibling(src_refs, land_refs, send_sems, recv_sems):
    x, y, cc = _mesh_pos()
    plan = []
    for a, (src, land) in enumerate(zip(src_refs, land_refs)):
        cp = pltpu.make_async_remote_copy(
            src_ref=src.at[1 - cc], dst_ref=land, send_sem=send_sems.at[a], recv_sem=recv_sems.at[a],
            device_id=(x, y, 1 - cc), device_id_type=pl.DeviceIdType.MESH)
        plan.append((cp, cp))
    return plan


def _plan_chips(src_refs, land_refs, send_sems, recv_sems):
    x, y, cc = _mesh_pos()
    my_chip = 2 * x + y
    plan = []
    for a, (src, land) in enumerate(zip(src_refs, land_refs)):
        for j, (px, py) in enumerate([(1 - x, y), (x, 1 - y), (1 - x, 1 - y)]):
            peer = 2 * px + py
            sems = dict(send_sem=send_sems.at[3 * a + j], recv_sem=recv_sems.at[3 * a + j], device_id=(px, py, cc),
                        device_id_type=pl.DeviceIdType.MESH)
            plan.append((pltpu.make_async_remote_copy(src_ref=src.at[peer], dst_ref=land.at[my_chip], **sems),
                         pltpu.make_async_remote_copy(src_ref=src.at[peer], dst_ref=land.at[peer], **sems)))
    return plan


HBM_SPEC = pl.BlockSpec(memory_space=pltpu.HBM)
SEM_SPEC = pl.BlockSpec(memory_space=pltpu.SEMAPHORE)
ANY_SPEC = pl.BlockSpec(memory_space=pl.ANY)
SIDE_EFFECT = pltpu.CompilerParams(has_side_effects=pltpu.SideEffectType.DATAFLOW_SIDE_EFFECTING)


def _start_copies(plan, sems_per_array, srcs, land_shapes, after, *, name):
    n, n_after = len(srcs), len(after)

    def body(*refs):
        send_sems, recv_sems = refs[2 * n + n_after], refs[2 * n + n_after + 1]
        for out, _ in plan(refs[:n], refs[n:2 * n], send_sems, recv_sems):
            out.start()
        refs[-1][...] = jnp.zeros_like(refs[-1])

    lands = [lax.empty(shp, a.dtype) for shp, a in zip(land_shapes, srcs)]
    sem = pltpu.SemaphoreType.DMA((n * sems_per_array,))
    res = pl.pallas_call(
        body, name=name,
        out_shape=(sem, sem, *[pltpu.HBM(a.shape, a.dtype) for a in srcs + lands], jax.ShapeDtypeStruct((8, LANES), F32)),
        in_specs=[HBM_SPEC] * (2 * n) + [ANY_SPEC] * n_after,
        out_specs=(SEM_SPEC, SEM_SPEC, *[HBM_SPEC] * (2 * n), pl.BlockSpec(memory_space=pltpu.VMEM)),
        input_output_aliases={i: 2 + i for i in range(2 * n)}, compiler_params=SIDE_EFFECT,
    )(*[pltpu.with_memory_space_constraint(a, pltpu.HBM) for a in srcs + lands], *after)
    return (res[0], res[1], list(res[2:2 + n]), list(res[2 + n:2 + 2 * n])), res[-1]


def _wait_copies(plan, handle, after, *, name):
    send, recv, srcs, lands = handle
    n = len(srcs)

    def body(*refs):
        for out, inc in plan(refs[:n], refs[n:2 * n], refs[2 * n], refs[2 * n + 1]):
            out.wait_send()
            inc.wait_recv()

    res = pl.pallas_call(
        body, name=name, out_shape=tuple(pltpu.HBM(a.shape, a.dtype) for a in srcs + lands),
        in_specs=[HBM_SPEC] * (2 * n) + [SEM_SPEC, SEM_SPEC] + [ANY_SPEC] * len(after), out_specs=[HBM_SPEC] * (2 * n),
        input_output_aliases={i: i for i in range(2 * n)}, compiler_params=SIDE_EFFECT,
    )(*srcs, *lands, send, recv, *after)
    return list(res[:n]), list(res[n:])


def _row_tile(rows, target=512):
    if rows <= target:
        return rows
    best = None
    for t in range(8, target + 1, 8):
        if rows % t == 0:
            best = t
    assert best is not None, (rows, target)
    return best


def _add_own(g, other, core, *, name):
    _, _, rows, cols = g.shape
    tm = _row_tile(rows)

    def body(c_ref, g_ref, o_ref, out_ref):
        out_ref[...] = g_ref[...] + o_ref[...]

    return pl.pallas_call(
        body, name=name, out_shape=jax.ShapeDtypeStruct(other.shape, other.dtype),
        grid_spec=pltpu.PrefetchScalarGridSpec(
            num_scalar_prefetch=1, grid=(4, rows // tm),
            in_specs=[pl.BlockSpec((None, None, tm, cols), lambda p, i, c_ref: (c_ref[0], p, i, 0)),
                      pl.BlockSpec((None, tm, cols), lambda p, i, c_ref: (p, i, 0))],
            out_specs=pl.BlockSpec((None, tm, cols), lambda p, i, c_ref: (p, i, 0))),
        compiler_params=_params(("parallel", "parallel")),
    )(core, g, other)


def _adamw(parts, w, m, v, *, name):
    layers, rows, cols = w.shape
    assert len(parts) == layers
    tm = _row_tile(rows, 256)
    nr = rows // tm

    def body(*refs):
        p_refs = refs[:layers]
        w_ref, m_ref, v_ref, g_ref, d_ref, nm_ref, nv_ref, g_s = refs[layers:]
        for ll in range(layers):
            @pl.when(pl.program_id(0) == ll)
            def _(ll=ll):
                g = p_refs[ll][0]
                for q in range(1, p_refs[ll].shape[0]):
                    g = g + p_refs[ll][q]
                g_s[...] = g

        g = g_s[...]
        mm = ADAM_B1 * m_ref[...] + (1.0 - ADAM_B1) * g
        vv = ADAM_B2 * v_ref[...] + (1.0 - ADAM_B2) * jnp.square(g)
        m_hat = mm / (1.0 - ADAM_B1 ** ADAM_STEP)
        v_hat = vv / (1.0 - ADAM_B2 ** ADAM_STEP)
        g_ref[...] = g
        d_ref[...] = -ADAM_LR * (m_hat / (jnp.sqrt(v_hat) + ADAM_EPS) + ADAM_WD * w_ref[...])
        nm_ref[...] = mm
        nv_ref[...] = vv

    def part_spec(ll, p):
        return pl.BlockSpec((p, tm, cols), lambda l, i: (0, jnp.where(l == ll, i, jnp.where(l < ll, 0, nr - 1)), 0))

    spec = pl.BlockSpec((None, tm, cols), lambda l, i: (l, i, 0))
    out = jax.ShapeDtypeStruct(w.shape, F32)
    return pl.pallas_call(
        body, name=name, grid=(layers, nr),
        in_specs=[part_spec(ll, parts[ll].shape[0]) for ll in range(layers)] + [spec] * 3,
        out_specs=[spec] * 4, out_shape=[out] * 4, scratch_shapes=[pltpu.VMEM((tm, cols), F32)],
        compiler_params=_params(("arbitrary", "arbitrary")),
    )(*parts, w, m, v)


def _pack(arrs):
    flat = jnp.concatenate([a.reshape(-1) for a in arrs])
    rows = -(-flat.shape[0] // (PACK_COLS * 16)) * 16
    return jnp.pad(flat, (0, rows * PACK_COLS - flat.shape[0])).reshape(rows, PACK_COLS)


def _unpack(packed, shapes):
    flat = packed.reshape(-1)
    out, off = [], 0
    for shp in shapes:
        n = int(np.prod(shp))
        out.append(flat[off:off + n].reshape(shp))
        off += n
    return out


def _shards_to_full(g, axis):
    if axis == 0:
        return g.reshape(g.shape[0] * g.shape[1], g.shape[2])
    return jnp.transpose(g, (1, 0, 2)).reshape(g.shape[1], g.shape[0] * g.shape[2])


def _full_to_shards(a, axis):
    if axis == 0:
        return jnp.transpose(a.reshape(4, 2, a.shape[0] // N_DEV, a.shape[1]), (1, 0, 2, 3))
    return jnp.transpose(a.reshape(a.shape[0], 4, 2, a.shape[1] // N_DEV), (2, 1, 0, 3))


def _zeros_like_cols(a, n):
    return jnp.zeros(a.shape[:-1] + (n,), a.dtype)


def _swap_cols(r, sign):
    half = MLA_ROPE // 2
    if sign > 0:
        return jnp.concatenate([-r[..., half:], r[..., :half]], axis=-1)
    return jnp.concatenate([r[..., half:], -r[..., :half]], axis=-1)


def _pad_heads(a, n):
    z = _zeros_like_cols(a, HEAD)
    return jnp.concatenate([p for h in range(n) for p in (a[..., h * HEAD:(h + 1) * HEAD], z)], axis=-1)


def _unpad_heads(a, n):
    return jnp.concatenate([a[..., h * LANES:h * LANES + HEAD] for h in range(n)], axis=-1)


def _ext_w_in(w):
    z = _zeros_like_cols(w, HEAD)
    kr = w[..., 384:416]
    krp = _swap_cols(kr, 1)
    return jnp.concatenate([
        _pad_heads(w[..., 1440:1696], 4), w[..., 0:256], w[..., 416:672], w[..., 672:928], w[..., 928:1184],
        w[..., 1184:1440], _pad_heads(w[..., 1696:1824], 2), _pad_heads(w[..., 1824:1952], 2), w[..., 256:384],
        z, kr, kr, z, krp, krp], axis=-1)


def _fold_w_in(e):
    kr = (e[..., 2496:2528] + e[..., 2528:2560]) + _swap_cols(e[..., 2624:2656] + e[..., 2656:2688], -1)
    return jnp.concatenate([
        e[..., 512:768], e[..., 2304:2432], kr, e[..., 768:1024], e[..., 1024:1280], e[..., 1280:1536],
        e[..., 1536:1792], _unpad_heads(e[..., 0:512], 4), _unpad_heads(e[..., 1792:2048], 2),
        _unpad_heads(e[..., 2048:2304], 2)], axis=-1)


def _ext_w_uq(w):
    parts = []
    for h in range(MLA_HEADS):
        rot = w[..., 96 * h + 64:96 * h + 96]
        parts += [w[..., 96 * h:96 * h + 64], rot, _swap_cols(rot, 1)]
    return jnp.concatenate(parts, axis=-1)


def _fold_w_uq(e):
    parts = []
    for h in range(MLA_HEADS):
        b = LANES * h
        parts += [e[..., b:b + 64], e[..., b + 64:b + 96] + _swap_cols(e[..., b + 96:b + 128], -1)]
    return jnp.concatenate(parts, axis=-1)


def _ext_w_ukv(w):
    k = jnp.concatenate([w[..., LANES * h:LANES * h + HEAD] for h in range(MLA_HEADS)], axis=-1)
    v = jnp.concatenate([w[..., LANES * h + HEAD:LANES * (h + 1)] for h in range(MLA_HEADS)], axis=-1)
    return jnp.concatenate([_pad_heads(k, 4), _pad_heads(v, 4)], axis=-1)


def _fold_w_ukv(e):
    parts = []
    for h in range(MLA_HEADS):
        parts += [e[..., LANES * h:LANES * h + HEAD], e[..., 512 + LANES * h:512 + LANES * h + HEAD]]
    return jnp.concatenate(parts, axis=-1)


def _ext_mix(a):
    return jnp.concatenate([_pad_heads(a[..., 0:256], 4), a[..., 256:768], _pad_heads(a[..., 768:1024], 4)], axis=-1)


def _fold_mix(e):
    return jnp.concatenate([_unpad_heads(e[..., 0:512], 4), e[..., 512:1024], _unpad_heads(e[..., 1024:1536], 4)], axis=-1)


def _rope_tables(s):
    inv = 1.0 / (ROPE_THETA ** (jnp.arange(0, MLA_ROPE, 2, dtype=F32) / MLA_ROPE))
    ang = jnp.arange(s, dtype=F32)[:, None] * inv[None, :]
    cos, sin = jnp.cos(ang), jnp.sin(ang)
    c32, s32 = jnp.concatenate([cos, cos], axis=1), jnp.concatenate([sin, sin], axis=1)
    zeros, ones = jnp.zeros((s, HEAD), F32), jnp.ones((s, HEAD), F32)
    tq = jnp.concatenate([ones, c32, s32], axis=1) * (1.0 / math.sqrt(MLA_NOPE + MLA_ROPE))
    return (jnp.tile(tq, (1, MLA_HEADS)), jnp.concatenate([zeros, c32, c32], axis=1),
            jnp.concatenate([zeros, s32, s32], axis=1))


def _gn(y):
    return y * _rstd(y, D_GROUP)


def _mixer_fwd(x, w, tabs, l):
    s = x.shape[0]
    tq, tkc, tks = tabs
    n = lambda t: f"l{l}_{t}"
    h = _rms_fwd(("row", x, D_MODEL, 0), w["attn_norm"], name=n("attn_norm"), rows=s, width=D_MODEL)
    proj = _mm(h, w["w_in"], name=n("in_proj"))
    cqn = _rms_fwd(("row", proj, 256, C_CQ[0] // 256), w["mla_q_norm"], name=n("q_norm"), rows=s, width=256)
    ckvn = _rms_fwd(("row", proj, 128, C_CKV[0] // 128), w["mla_kv_norm"], name=n("kv_norm"), rows=s, width=128)
    q_ext = _mm(cqn, w["w_uq"], name=n("uq"))
    kv_ext = _mm(ckvn, w["w_ukv"], name=n("ukv"))

    def prep(qe, kvk, kvv, kr, krp, tqv, tc, ts):
        kb = kr * tc + krp * ts
        lane = lax.broadcasted_iota(jnp.int32, kvv.shape, 1) & (LANES - 1)
        return qe * tqv, kvk + jnp.tile(kb, (1, MLA_HEADS)), jnp.where(lane == HEAD, 1.0, kvv)

    qm, km, vm = _rowwise(
        prep, name=n("mla_prep"), rows=s,
        ins=[("row", q_ext, 512, 0), ("row", kv_ext, 512, 0), ("row", kv_ext, 512, 1),
             ("row", proj, 128, C_KR[0] // 128), ("row", proj, 128, C_KRP[0] // 128),
             ("row", tq, 512, 0), ("row", tkc, 128, 0), ("row", tks, 128, 0)],
        outs=[(512, BF16)] * 3)
    y_a, lse_a = _mla_fwd(qm, km, vm, name=n("mla_fwd"), s=s, nh=MLA_HEADS)
    y_b = _conv_fwd(proj, w["conv_w"], name=n("conv_fwd"), s=s)
    y_c = _pool_fwd(proj, w["pool_wbd"], w["pool_scale"], name=n("pool_fwd"), s=s)
    y_d, lse_d = _swa_fwd(proj, w["hp_swa"], name=n("swa_fwd"), s=s, scale=1.0 / math.sqrt(HEAD))

    def mix(ya, yb, yc, yd, mn):
        return (jnp.concatenate([_gn(ya), _gn(yb), _gn(yc), _gn(yd)], axis=1) * mn,)

    mixed = _rowwise(mix, name=n("group_norm"), rows=s,
                     ins=[("row", y_a, 512, 0), ("row", y_b, 256, 0), ("row", y_c, 256, 0), ("row", y_d, 512, 0),
                          ("full", w["mix_norm"])], outs=[(D_MIX_EXT, BF16)])[0]
    x1 = _mm(mixed, w["w_o"], res=x, name=n("out_proj"))
    saved = dict(x=x, h=h, proj=proj, cqn=cqn, ckvn=ckvn, qm=qm, km=km, vm=vm, y_a=y_a, lse_a=lse_a, y_b=y_b, y_c=y_c,
                 y_d=y_d, lse_d=lse_d, mixed=mixed)
    return x1, saved


def _ffn_fwd(x1, w, l):
    s = x1.shape[0]
    n = lambda t: f"l{l}_{t}"
    h2 = _rms_fwd(("row", x1, D_MODEL, 0), w["ffn_norm"], name=n("ffn_norm"), rows=s, width=D_MODEL)
    gu = _mm(h2, w["w_gate_up"], name=n("gate_up"))

    def swiglu(g, u):
        return (g * jax.nn.sigmoid(g) * u,)

    act = _rowwise(swiglu, name=n("swiglu"), rows=s, tm=256,
                   ins=[("row", gu, D_FF, 0), ("row", gu, D_FF, 1)], outs=[(D_FF, BF16)])[0]
    x2 = _mm(act, w["w_down"], res=x1, name=n("down"))
    return x2, dict(x1=x1, h2=h2, gu=gu, act=act)


def _ffn_bwd_down(dx2, sv, w, l):
    s = dx2.shape[0]
    n = lambda t: f"l{l}_{t}"
    dact = _mm(dx2, w["w_down"], tb=True, name=n("d_act"))
    g = dict(w_down=_mm(sv["act"], dx2, ta=True, name=n("dw_down")))

    def swiglu_bwd(gt, u, da):
        sg = jax.nn.sigmoid(gt)
        return (jnp.concatenate([da * u * sg * (1.0 + gt * (1.0 - sg)), da * gt * sg], axis=1),)

    dgu = _rowwise(swiglu_bwd, name=n("swiglu_bwd"), rows=s, tm=256,
                   ins=[("row", sv["gu"], D_FF, 0), ("row", sv["gu"], D_FF, 1), ("row", dact, D_FF, 0)],
                   outs=[(2 * D_FF, BF16)])[0]
    return dgu, g


def _ffn_bwd_up(dx2, dgu, sv, w, l):
    s = dx2.shape[0]
    n = lambda t: f"l{l}_{t}"
    dh2 = _mm(dgu, w["w_gate_up"], tb=True, name=n("d_h2"))
    g = dict(w_gate_up=_mm(sv["h2"], dgu, ta=True, name=n("dw_gate_up")))
    dx1, g["ffn_norm"] = _rms_bwd(("row", sv["x1"], D_MODEL, 0), w["ffn_norm"], dh2, dx2, name=n("ffn_norm_bwd"),
                                  rows=s, width=D_MODEL, out_dtype=F32)
    return dx1, g


def _mixer_bwd_out(dx1, sv, w, l):
    s = dx1.shape[0]
    n = lambda t: f"l{l}_{t}"
    dmixed = _mm(dx1, w["w_o"], tb=True, name=n("d_mixed"))
    g = dict(w_o=_mm(sv["mixed"], dx1, ta=True, name=n("dw_o")))

    def mix_bwd(dm, ya, yb, yc, yd, mn):
        outs, dmn = [], []
        for y, lo, hi in ((ya, 0, 512), (yb, 512, 768), (yc, 768, 1024), (yd, 1024, 1536)):
            r = _rstd(y, D_GROUP)
            nrm = y * r
            dmg = dm[:, lo:hi]
            dn = dmg * mn[:, lo:hi]
            outs.append(r * (dn - nrm * (jnp.sum(dn * nrm, axis=-1, keepdims=True) * (1.0 / D_GROUP))))
            dmn.append(jnp.sum(dmg * nrm, axis=0, keepdims=True))
        return (*outs, jnp.concatenate(dmn, axis=1))

    dy_a, dy_b, dy_c, dy_d, g["mix_norm"] = _rowwise(
        mix_bwd, name=n("group_norm_bwd"), rows=s,
        ins=[("row", dmixed, D_MIX_EXT, 0), ("row", sv["y_a"], 512, 0), ("row", sv["y_b"], 256, 0),
             ("row", sv["y_c"], 256, 0), ("row", sv["y_d"], 512, 0), ("full", w["mix_norm"])],
        outs=[(512, F32), (256, F32), (256, F32), (512, F32)], reds=[(1, D_MIX_EXT)])
    return (dy_a, dy_b, dy_c, dy_d), g


def _mixer_bwd_in(dx1, dys, sv, w, tabs, l):
    s = dx1.shape[0]
    tq, tkc, tks = tabs
    n = lambda t: f"l{l}_{t}"
    dy_a, dy_b, dy_c, dy_d = dys
    g = {}

    proj = sv["proj"]
    dq_sw, dk_sw, dv_sw, dsink = _swa_bwd(proj, sv["y_d"], dy_d, sv["lse_d"], w["hp_swa"], name=n("swa_bwd"), s=s,
                                          scale=1.0 / math.sqrt(HEAD))
    g["swa_sinks"] = dsink[:, 0, 0]

    qkv = (sv["qm"], sv["km"], sv["vm"])
    dqm, dvec_a = _mla_dq(*qkv, sv["y_a"], dy_a, sv["lse_a"], name=n("mla_dq"), s=s, nh=MLA_HEADS)
    dkm, dvm = _mla_dkv(*qkv, dy_a, sv["lse_a"].reshape(MLA_HEADS, 1, s), dvec_a.reshape(MLA_HEADS, 1, s),
                        name=n("mla_dkv"), s=s, nh=MLA_HEADS)

    def prep_bwd(dq, dk, tqv, tc, ts):
        dkb = dk[:, 0:128] + dk[:, 128:256] + dk[:, 256:384] + dk[:, 384:512]
        return dq * tqv, dk, dkb * tc, dkb * ts

    dq_ext, dkvk, dkr, dkrp = _rowwise(
        prep_bwd, name=n("mla_prep_bwd"), rows=s,
        ins=[("row", dqm, 512, 0), ("row", dkm, 512, 0), ("row", tq, 512, 0), ("row", tkc, 128, 0), ("row", tks, 128, 0)],
        outs=[(512, BF16), (512, BF16), (128, BF16), (128, BF16)])
    dkv_ext = jnp.concatenate([dkvk, dvm], axis=1)
    dcqn = _mm(dq_ext, w["w_uq"], tb=True, name=n("d_cqn"))
    g["w_uq"] = _mm(sv["cqn"], dq_ext, ta=True, name=n("dw_uq"))
    dckvn = _mm(dkv_ext, w["w_ukv"], tb=True, name=n("d_ckvn"))
    g["w_ukv"] = _mm(sv["ckvn"], dkv_ext, ta=True, name=n("dw_ukv"))
    dcq, g["mla_q_norm"] = _rms_bwd(("row", proj, 256, C_CQ[0] // 256), w["mla_q_norm"], dcqn, None,
                                    name=n("q_norm_bwd"), rows=s, width=256, out_dtype=BF16)
    dckv, g["mla_kv_norm"] = _rms_bwd(("row", proj, 128, C_CKV[0] // 128), w["mla_kv_norm"], dckvn, None,
                                      name=n("kv_norm_bwd"), rows=s, width=128, out_dtype=BF16)

    dgb, dgc, duc, g["conv_w"] = _conv_bwd(dy_b, proj, w["conv_w"], name=n("conv_bwd"), s=s)
    dup, g["pool_wbd"], g["pool_scale"] = _pool_bwd(dy_c, proj, w["pool_wbd"], w["pool_scale"], name=n("pool_bwd"), s=s)

    dproj = jnp.concatenate([dq_sw, dcq, dgb, dgc, duc, dup, dk_sw.astype(BF16), dv_sw.astype(BF16), dckv, dkr, dkrp],
                            axis=1)
    dh = _mm(dproj, w["w_in"], tb=True, name=n("d_h"))
    g["w_in"] = _mm(sv["h"], dproj, ta=True, name=n("dw_in"))
    dx0, g["attn_norm"] = _rms_bwd(("row", sv["x"], D_MODEL, 0), w["attn_norm"], dh, dx1, name=n("attn_norm_bwd"),
                                   rows=s, width=D_MODEL, out_dtype=F32)
    return dx0, g


def _loss_head(x, target, g, *, s):
    def fn(xv, tv, gv):
        r = _rstd(xv, D_MODEL)
        e = xv * r * gv - tv
        part = jnp.sum(jnp.sum(e * e, axis=1, keepdims=True), axis=0, keepdims=True) * (0.5 / D_MODEL)
        dy = e * (1.0 / D_MODEL)
        dyg = dy * gv
        dx = r * dyg - xv * (r * r * r) * (jnp.sum(dyg * xv, axis=-1, keepdims=True) * (1.0 / D_MODEL))
        return dx, jnp.sum(dy * xv * r, axis=0, keepdims=True), jnp.broadcast_to(part, (1, LANES))

    return _rowwise(fn, name="loss_head", rows=s,
                    ins=[("row", x, D_MODEL, 0), ("row", target, D_MODEL, 0), ("full", g)],
                    outs=[(D_MODEL, F32)], reds=[(1, D_MODEL), (1, LANES)])


def _alibi_slopes(n):
    return np.asarray([2.0 ** (-8.0 * (i + 1) / n) for i in range(n)], dtype=np.float32)


MIXER_WEIGHTS = ("w_in", "w_uq", "w_ukv", "conv_w", "w_o")
FFN_WEIGHTS = ("w_gate_up", "w_down")


def _mixer_weights(full, rep, l):
    pw = rep["pool_w"][l]
    z = jnp.zeros((HEAD, HEAD), F32)
    wbd = jnp.stack([jnp.block([[pw[2 * j], z], [z, pw[2 * j + 1]]]) for j in range(2)])
    w_o_ext = jnp.moveaxis(_ext_mix(jnp.moveaxis(full["w_o"], 0, -1)), -1, 0)
    return dict(
        attn_norm=rep["attn_norm"][l][None], w_in=_ext_w_in(full["w_in"]), mla_q_norm=rep["mla_q_norm"][l][None],
        w_uq=_ext_w_uq(full["w_uq"]), mla_kv_norm=rep["mla_kv_norm"][l][None], w_ukv=_ext_w_ukv(full["w_ukv"]),
        conv_w=full["conv_w"], pool_wbd=wbd, pool_scale=rep["pool_scale"][l][None],
        mix_norm=_ext_mix(rep["mix_norm"][l])[None], w_o=w_o_ext,
        hp_swa=jnp.stack([jnp.asarray(_alibi_slopes(SWA_HEADS)), rep["swa_sinks"][l]], axis=1))


def _ffn_weights(full, rep, l):
    return dict(ffn_norm=rep["ffn_norm"][l][None], w_gate_up=full["w_gate_up"], w_down=full["w_down"])


def _fold_grads(g):
    fold = dict(
        w_in=_fold_w_in, w_uq=_fold_w_uq, w_ukv=_fold_w_ukv,
        w_o=lambda e: jnp.moveaxis(_fold_mix(jnp.moveaxis(e, 0, -1)), -1, 0),
        mix_norm=lambda e: _fold_mix(e[0]),
        pool_wbd=lambda e: jnp.stack([e[j // 2][HEAD * (j % 2):HEAD * (j % 2 + 1), HEAD * (j % 2):HEAD * (j % 2 + 1)]
                                      for j in range(4)]))
    rows = ("attn_norm", "mla_q_norm", "mla_kv_norm", "pool_scale", "ffn_norm")
    out = {}
    for nm, e in g.items():
        out["pool_w" if nm == "pool_wbd" else nm] = fold[nm](e) if nm in fold else (e[0] if nm in rows else e)
    return out


def kernel(x, attn_norm, w_in, mla_q_norm, w_uq, mla_kv_norm, w_ukv, conv_w, pool_w, pool_scale, swa_sinks, mix_norm, w_o, ffn_norm, w_gate_up, w_down, final_norm, loss_target, m_attn_norm, m_w_in, m_mla_q_norm, m_w_uq, m_mla_kv_norm, m_w_ukv, m_conv_w, m_pool_w, m_pool_scale, m_swa_sinks, m_mix_norm, m_w_o, m_ffn_norm, m_w_gate_up, m_w_down, m_final_norm, v_attn_norm, v_w_in, v_mla_q_norm, v_w_uq, v_mla_kv_norm, v_w_ukv, v_conv_w, v_pool_w, v_pool_scale, v_swa_sinks, v_mix_norm, v_w_o, v_ffn_norm, v_w_gate_up, v_w_down, v_final_norm):
    given = dict(locals())
    sh_names = [nm for nm, _, _ in SHARDED]
    sh_axis = {nm: ax - 1 for nm, _, ax in SHARDED}
    rep_names = [nm for nm, _ in REPLICATED]
    rep_shapes = [shp for _, shp in REPLICATED]
    rep = {nm: given[nm] for nm in rep_names if nm != "loss"}
    me = 4 * lax.axis_index("x") + 2 * lax.axis_index("y") + lax.axis_index("c")
    my_chip = 2 * lax.axis_index("x") + lax.axis_index("y")
    core = lax.axis_index("c").astype(jnp.int32).reshape(1)

    def behind(token, a):
        return a + token[0, 0].astype(a.dtype)

    def wire(nm, l):
        if nm == "conv_w":
            return lax.bitcast_convert_type(given[nm][l], BF16).reshape(3, -1)
        return given[nm][l].astype(BF16)

    def whole(nm, g):
        if nm == "conv_w":
            g = lax.bitcast_convert_type(g.reshape(N_DEV, 3, -1, 2), F32)
        return _shards_to_full(g, sh_axis[nm])

    def gather_start(names, l, after, tag):
        srcs = [wire(nm, l) for nm in names]
        return _start_copies(_plan_gather, N_DEV - 1, srcs, [(N_DEV,) + a.shape for a in srcs], after, name=f"start_gather_{tag}")

    def gather_wait(names, handle, after, tag):
        srcs, lands = _wait_copies(_plan_gather, handle, after, name=f"wait_gather_{tag}")
        return {nm: whole(nm, lax.dynamic_update_index_in_dim(land, src, me, 0)) for nm, src, land in zip(names, srcs, lands)}

    got = _all_gather([wire(nm, 0) for nm in MIXER_WEIGHTS], name="gather_mixer0")
    full_m0 = {nm: whole(nm, g) for nm, g in zip(MIXER_WEIGHTS, got)}
    h_f0, tok = gather_start(FFN_WEIGHTS, 0, [], "ffn0")
    h_l1, tok = gather_start(MIXER_WEIGHTS + FFN_WEIGHTS, 1, [tok], "layer1")

    xs, target = x[0], loss_target[0]
    s = xs.shape[0]
    tabs = _rope_tables(s)
    wm, wf, svm, svf = [None] * DEPTH, [None] * DEPTH, [None] * DEPTH, [None] * DEPTH
    wm[0] = _mixer_weights(full_m0, rep, 0)
    wm[0]["attn_norm"] = behind(tok, wm[0]["attn_norm"])
    x1, svm[0] = _mixer_fwd(xs, wm[0], tabs, 0)
    wf[0] = _ffn_weights(gather_wait(FFN_WEIGHTS, h_f0, [x1], "ffn0"), rep, 0)
    x2, svf[0] = _ffn_fwd(x1, wf[0], 0)
    full_1 = gather_wait(MIXER_WEIGHTS + FFN_WEIGHTS, h_l1, [x2], "layer1")
    wm[1], wf[1] = _mixer_weights(full_1, rep, 1), _ffn_weights(full_1, rep, 1)
    x1, svm[1] = _mixer_fwd(x2, wm[1], tabs, 1)
    x2, svf[1] = _ffn_fwd(x1, wf[1], 1)
    dx, d_final, loss = _loss_head(x2, target, rep["final_norm"][None], s=s)

    parts = {}

    def reduce_start(grads, l, after, tag):
        names = [nm for nm in sh_names if nm in grads]
        mine = [_full_to_shards(grads[nm], sh_axis[nm]) for nm in names]
        handle, token = _start_copies(_plan_sibling, 1, mine, [m.shape[1:] for m in mine], after, name=f"start_sibling_{tag}")
        return (names, l, handle), token

    def reduce_mid(state, after, tag):
        names, l, handle = state
        mine, theirs = _wait_copies(_plan_sibling, handle, after, name=f"wait_sibling_{tag}")
        sums = [_add_own(g, o, core, name=f"chip_sum_{nm}_{l}") for nm, g, o in zip(names, mine, theirs)]
        handle, token = _start_copies(_plan_chips, 3, sums, [a.shape for a in sums], [], name=f"start_chips_{tag}")
        return (names, l, handle), token

    def reduce_end(state, after, tag):
        names, l, handle = state
        sums, lands = _wait_copies(_plan_chips, handle, after, name=f"wait_chips_{tag}")
        for nm, own, land in zip(names, sums, lands):
            parts[nm, l] = lax.dynamic_update_index_in_dim(land, lax.dynamic_index_in_dim(own, my_chip, 0, keepdims=False),
                                                           my_chip, 0)

    small = [None] * DEPTH
    in_flight = []
    pending = None
    for l in reversed(range(DEPTH)):
        dgu, g_down = _ffn_bwd_down(dx, svf[l], wf[l], l)
        if pending is not None:
            state, token = reduce_mid(pending, [dgu], f"mixer{l + 1}")
            in_flight.append((state, f"mixer{l + 1}"))
            wf[l]["ffn_norm"] = behind(token, wf[l]["ffn_norm"])
        dx1, g_up = _ffn_bwd_up(dx, dgu, svf[l], wf[l], l)
        g_ffn = _fold_grads({**g_down, **g_up})
        state, token = reduce_start(g_ffn, l, [], f"ffn{l}")
        wm[l]["mix_norm"] = behind(token, wm[l]["mix_norm"])
        dys, g_out = _mixer_bwd_out(dx1, svm[l], wm[l], l)
        state, token = reduce_mid(state, [dys[0]], f"ffn{l}")
        in_flight.append((state, f"ffn{l}"))
        svm[l]["lse_a"] = behind(token, svm[l]["lse_a"])
        wm[l]["hp_swa"] = behind(token, wm[l]["hp_swa"])
        dx, g_in = _mixer_bwd_in(dx1, dys, svm[l], wm[l], tabs, l)
        g_mixer = _fold_grads({**g_out, **g_in})
        small[l] = {**g_ffn, **g_mixer}
        pending, token = reduce_start(g_mixer, l, [], f"mixer{l}")
    state, token = reduce_mid(pending, [dx], "mixer0")
    in_flight.append((state, "mixer0"))
    for state, tag in in_flight:
        reduce_end(state, [], tag)
    grad_x = dx

    sh_out = {nm: _adamw([parts[nm, l] for l in range(DEPTH)], given[nm], given["m_" + nm], given["v_" + nm],
                         name=f"adamw_{nm}") for nm in sh_names}

    grads = {nm: jnp.stack([small[l][nm] for l in range(DEPTH)]) for nm in rep_names if nm in small[0]}
    grads["final_norm"] = d_final[0]
    grads["loss"] = loss[0, :1]
    zero = jnp.zeros((1,), F32)
    small = _all_gather([_pack([grads[nm] for nm in rep_names])], name="gather_small_grads")
    packs = [_pack([given.get(pre + nm, zero) for nm in rep_names])[None] for pre in ("", "m_", "v_")]
    rep_out = [dict(zip(rep_names, _unpack(o[0], rep_shapes))) for o in _adamw(small, *packs, name="adamw_replicated")]

    out = [rep_out[0]["loss"][0], grad_x[None]]
    for i in range(4):
        out += [sh_out[nm][i] if nm in sh_axis else rep_out[i][nm] for nm in WEIGHT_ORDER]
    return tuple(out)
```

```python
import functools
import math

import numpy as np
import jax
import jax.numpy as jnp
from jax import lax
from jax.experimental import pallas as pl
from jax.experimental.pallas import tpu as pltpu

F32 = jnp.float32
BF16 = jnp.bfloat16

D_MODEL = 1024
DEPTH = 2
D_GROUP = 256
MLA_HEADS = 4
MLA_NOPE = 64
MLA_ROPE = 32
ROPE_THETA = 10000.0
POOL_WINDOWS = (2, 4, 8, 16)
SWA_HEADS = 4
SWA_KV_HEADS = 2
SWA_WINDOW = 128
D_FF = 2816
GU_TILE = 256
RMS_EPS = 1e-6
LANES = 128
HEAD = 64
VMEM_LIMIT = 48 * 1024 * 1024
NEG = -1e30

ADAM_LR = 0.001
ADAM_B1 = 0.9
ADAM_B2 = 0.999
ADAM_EPS = 1e-08
ADAM_WD = 0.01
ADAM_STEP = 10

N_DEV = 8
PACK_COLS = 1024

C_QSW, C_CQ, C_GB, C_GC, C_UCONV, C_UPOOL = (0, 512), (512, 256), (768, 256), (1024, 256), (1280, 256), (1536, 256)
C_KSW, C_VSW, C_CKV, C_KR, C_KRP = (1792, 256), (2048, 256), (2304, 128), (2432, 128), (2560, 128)
D_IN_EXT = 2688
D_MIX_EXT = 1536

SHARDED = (("w_in", (DEPTH, 1024, 244), 2), ("w_uq", (DEPTH, 256, 48), 2), ("w_ukv", (DEPTH, 128, 64), 2),
           ("conv_w", (DEPTH, 3, 32), 2), ("w_o", (DEPTH, 128, 1024), 1), ("w_gate_up", (DEPTH, 1024, 704), 2),
           ("w_down", (DEPTH, 352, 1024), 1))
REPLICATED = (("attn_norm", (DEPTH, 1024)), ("mla_q_norm", (DEPTH, 256)), ("mla_kv_norm", (DEPTH, 128)),
              ("pool_w", (DEPTH, 4, 64, 64)), ("pool_scale", (DEPTH, 256)), ("swa_sinks", (DEPTH, 4)),
              ("mix_norm", (DEPTH, 1024)), ("ffn_norm", (DEPTH, 1024)), ("final_norm", (1024,)), ("loss", (1,)))
WEIGHT_ORDER = ("attn_norm", "w_in", "mla_q_norm", "w_uq", "mla_kv_norm", "w_ukv", "conv_w", "pool_w", "pool_scale",
                "swa_sinks", "mix_norm", "w_o", "ffn_norm", "w_gate_up", "w_down", "final_norm")


def _params(sem):
    return pltpu.CompilerParams(dimension_semantics=sem, vmem_limit_bytes=VMEM_LIMIT)


def _pick(dim, target):
    if dim <= target:
        return dim
    best = None
    for t in range(LANES, target + 1, LANES):
        if dim % t == 0:
            best = t
    assert best is not None, (dim, target)
    return best


def _mm(a, b, *, name, ta=False, tb=False, res=None, out_dtype=F32, tm=512, tn=1024, tk=1024, epilogue=None):
    m, k = (a.shape[1], a.shape[0]) if ta else a.shape
    n = b.shape[0] if tb else b.shape[1]
    assert (b.shape[1] if tb else b.shape[0]) == k
    tm, tn, tk = _pick(m, tm), _pick(n, tn), _pick(k, tk)
    nj, nk = n // tn, k // tk
    dims = (((0 if ta else 1,), (1 if tb else 0,)), ((), ()))
    fn, extra, outs = epilogue if epilogue is not None else (None, [], [(n, out_dtype)])
    if res is not None:
        assert epilogue is None
        fn, extra = (lambda acc, r: (acc + r,)), [res]
    n_in, n_out = 2 + len(extra), len(outs)

    def body(*refs):
        a_ref, b_ref, acc_ref = refs[0], refs[1], refs[-1]
        kk = pl.program_id(2)

        @pl.when(kk == 0)
        def _():
            acc_ref[...] = jnp.zeros_like(acc_ref)

        acc_ref[...] += lax.dot_general(a_ref[...].astype(BF16), b_ref[...].astype(BF16), dims,
                                        preferred_element_type=F32)

        @pl.when(kk == nk - 1)
        def _():
            tiles = (acc_ref[...],) if fn is None else fn(acc_ref[...], *[r[...] for r in refs[2:n_in]])
            for o_ref, tile in zip(refs[n_in:n_in + n_out], tiles):
                o_ref[...] = tile.astype(o_ref.dtype)

    def col_tiles(width):
        assert width % (nj * LANES) == 0, (width, nj)
        return pl.BlockSpec((tm, width // nj), lambda i, j, kk: (i, j))

    a_spec = pl.BlockSpec((tk, tm), lambda i, j, kk: (kk, i)) if ta else pl.BlockSpec((tm, tk), lambda i, j, kk: (i, kk))
    b_spec = pl.BlockSpec((tn, tk), lambda i, j, kk: (j, kk)) if tb else pl.BlockSpec((tk, tn), lambda i, j, kk: (kk, j))
    res_ = pl.pallas_call(
        body, name=name, grid=(m // tm, nj, nk), in_specs=[a_spec, b_spec] + [col_tiles(e.shape[1]) for e in extra],
        out_specs=[col_tiles(w) for w, _ in outs],
        out_shape=[jax.ShapeDtypeStruct((m, w), dt) for w, dt in outs],
        scratch_shapes=[pltpu.VMEM((tm, tn), F32)],
        compiler_params=_params(("parallel", "parallel", "arbitrary")),
    )(a, b, *extra)
    return res_[0] if epilogue is None else res_


def _rowwise(fn, *, name, rows, ins, outs, reds=(), tm=512):
    tm = min(tm, rows)
    assert rows % tm == 0
    n_in, n_out = len(ins), len(outs)

    def body(*refs):
        vals = [r[...] for r in refs[:n_in]]
        res = fn(*vals)
        for r, v in zip(refs[n_in:n_in + n_out], res[:n_out]):
            r[...] = v.astype(r.dtype)
        if reds:
            @pl.when(pl.program_id(0) == 0)
            def _():
                for r in refs[n_in + n_out:]:
                    r[...] = jnp.zeros_like(r)

            for r, v in zip(refs[n_in + n_out:], res[n_out:]):
                r[...] += v

    in_specs, args = [], []
    for spec in ins:
        if spec[0] == "row":
            _, arr, width, blk = spec
            in_specs.append(pl.BlockSpec((tm, width), functools.partial(lambda i, blk: (i, blk), blk=blk)))
        else:
            arr = spec[1]
            in_specs.append(pl.BlockSpec(arr.shape, functools.partial(lambda i, nd: (0,) * nd, nd=arr.ndim)))
        args.append(arr)
    out_specs = [pl.BlockSpec((tm, w), lambda i: (i, 0)) for w, _ in outs]
    out_shape = [jax.ShapeDtypeStruct((rows, w), dt) for w, dt in outs]
    out_specs += [pl.BlockSpec((r, w), lambda i: (0, 0)) for r, w in reds]
    out_shape += [jax.ShapeDtypeStruct((r, w), F32) for r, w in reds]
    return pl.pallas_call(body, name=name, grid=(rows // tm,), in_specs=in_specs, out_specs=out_specs,
                          out_shape=out_shape, compiler_params=_params(("arbitrary",)))(*args)


def _rstd(x, n):
    return lax.rsqrt(jnp.sum(x * x, axis=-1, keepdims=True) * (1.0 / n) + RMS_EPS)


def _rms_fwd(x_spec, g, *, name, rows, width):
    def fn(x, gv):
        return (x * _rstd(x, width) * gv,)
    return _rowwise(fn, name=name, rows=rows, ins=[x_spec, ("full", g)], outs=[(width, BF16)])[0]


def _rms_bwd(x_spec, g, dy, res, *, name, rows, width, out_dtype):
    def fn(x, gv, dyv, *rest):
        r = _rstd(x, width)
        dyg = dyv * gv
        dx = r * dyg - x * (r * r * r) * (jnp.sum(dyg * x, axis=-1, keepdims=True) * (1.0 / width))
        if rest:
            dx = dx + rest[0]
        return dx, jnp.sum(dyv * x * r, axis=0, keepdims=True)

    ins = [x_spec, ("full", g), ("row", dy, width, 0)]
    if res is not None:
        ins.append(("row", res, width, 0))
    return _rowwise(fn, name=name, rows=rows, ins=ins, outs=[(width, out_dtype)], reds=[(1, width)])


NT_DIMS = (((1,), (1,)), ((), ()))
TN_DIMS = (((0,), (0,)), ((), ()))


def _mla_tile(s):
    return min(512, s)


def _causal(shape, query_axis):
    return lax.broadcasted_iota(jnp.int32, shape, query_axis) >= lax.broadcasted_iota(jnp.int32, shape, 1 - query_axis)


def _mla_fwd(qa, ka, va, *, name, s, nh):
    t = _mla_tile(s)
    nq = s // t

    def body(q_ref, k_ref, v_ref, o_ref, lse_ref, m_s, acc_s):
        i, j = pl.program_id(0), pl.program_id(1)

        @pl.when(j == 0)
        def _():
            m_s[...] = jnp.full_like(m_s, NEG)
            acc_s[...] = jnp.zeros_like(acc_s)

        def step(diag):
            for h in range(nh):
                cols = pl.ds(h * LANES, LANES)
                sc = lax.dot_general(q_ref[:, cols], k_ref[:, cols], NT_DIMS, preferred_element_type=F32)
                if diag:
                    sc = jnp.where(_causal(sc.shape, 0), sc, NEG)
                m_prev = m_s[h]
                m_new = jnp.maximum(m_prev, jnp.max(sc, axis=1, keepdims=True))
                p = jnp.exp(sc - m_new).astype(BF16)
                acc_s[:, cols] = (jnp.exp(m_prev - m_new) * acc_s[:, cols]
                                  + jnp.dot(p, v_ref[:, cols], preferred_element_type=F32))
                m_s[h] = m_new

        pl.when(j < i)(functools.partial(step, False))
        pl.when(j == i)(functools.partial(step, True))

        @pl.when(j == nq - 1)
        def _():
            lane = lax.broadcasted_iota(jnp.int32, (t, LANES), 1)
            for h in range(nh):
                cols = pl.ds(h * LANES, LANES)
                acc = acc_s[:, cols]
                l = jnp.sum(jnp.where(lane == HEAD, acc, 0.0), axis=1, keepdims=True)
                o_ref[:, cols] = jnp.where(lane < HEAD, acc / l, 0.0)
                lse_ref[h] = m_s[h] + jnp.log(l)

    q_spec = pl.BlockSpec((t, nh * LANES), lambda i, j: (i, 0))
    kv_spec = pl.BlockSpec((t, nh * LANES), lambda i, j: (jnp.minimum(j, i), 0))
    return pl.pallas_call(
        body, name=name, grid=(nq, nq), in_specs=[q_spec, kv_spec, kv_spec],
        out_specs=[q_spec, pl.BlockSpec((nh, t, 1), lambda i, j: (0, i, 0))],
        out_shape=[jax.ShapeDtypeStruct((s, nh * LANES), F32), jax.ShapeDtypeStruct((nh, s, 1), F32)],
        scratch_shapes=[pltpu.VMEM((nh, t, 1), F32), pltpu.VMEM((t, nh * LANES), F32)],
        compiler_params=_params(("parallel", "arbitrary")),
    )(qa, ka, va)


def _mla_dq(qa, ka, va, o, do, lse, *, name, s, nh):
    t = _mla_tile(s)
    nq = s // t

    def body(q_ref, k_ref, v_ref, o_ref, do_ref, lse_ref, dq_ref, d_ref, acc_s):
        i, j = pl.program_id(0), pl.program_id(1)

        @pl.when(j == 0)
        def _():
            acc_s[...] = jnp.zeros_like(acc_s)
            for h in range(nh):
                cols = pl.ds(h * LANES, LANES)
                d_ref[h] = jnp.sum(do_ref[:, cols] * o_ref[:, cols], axis=1, keepdims=True)

        def step(diag):
            for h in range(nh):
                cols = pl.ds(h * LANES, LANES)
                k = k_ref[:, cols]
                sc = lax.dot_general(q_ref[:, cols], k, NT_DIMS, preferred_element_type=F32)
                if diag:
                    sc = jnp.where(_causal(sc.shape, 0), sc, NEG)
                p = jnp.exp(sc - lse_ref[h])
                dp = lax.dot_general(do_ref[:, cols].astype(BF16), v_ref[:, cols], NT_DIMS, preferred_element_type=F32)
                ds = p * (dp - d_ref[h])
                acc_s[:, cols] += jnp.dot(ds.astype(BF16), k, preferred_element_type=F32)

        pl.when(j < i)(functools.partial(step, False))
        pl.when(j == i)(functools.partial(step, True))

        @pl.when(j == nq - 1)
        def _():
            dq_ref[...] = acc_s[...]

    q_spec = pl.BlockSpec((t, nh * LANES), lambda i, j: (i, 0))
    kv_spec = pl.BlockSpec((t, nh * LANES), lambda i, j: (jnp.minimum(j, i), 0))
    vec_spec = pl.BlockSpec((nh, t, 1), lambda i, j: (0, i, 0))
    return pl.pallas_call(
        body, name=name, grid=(nq, nq), in_specs=[q_spec, kv_spec, kv_spec, q_spec, q_spec, vec_spec],
        out_specs=[q_spec, vec_spec],
        out_shape=[jax.ShapeDtypeStruct((s, nh * LANES), F32), jax.ShapeDtypeStruct((nh, s, 1), F32)],
        scratch_shapes=[pltpu.VMEM((t, nh * LANES), F32)],
        compiler_params=_params(("parallel", "arbitrary")),
    )(qa, ka, va, o, do, lse)


def _mla_dkv(qa, ka, va, do, lse_row, d_row, *, name, s, nh):
    t = _mla_tile(s)
    nq = s // t

    def body(q_ref, k_ref, v_ref, do_ref, lse_ref, d_ref, dk_ref, dv_ref, dk_s, dv_s):
        kb, j = pl.program_id(0), pl.program_id(1)

        @pl.when(j == 0)
        def _():
            dk_s[...] = jnp.zeros_like(dk_s)
            dv_s[...] = jnp.zeros_like(dv_s)

        def step(diag):
            for h in range(nh):
                cols = pl.ds(h * LANES, LANES)
                q = q_ref[:, cols]
                do_b = do_ref[:, cols].astype(BF16)
                sc = lax.dot_general(k_ref[:, cols], q, NT_DIMS, preferred_element_type=F32)
                if diag:
                    sc = jnp.where(_causal(sc.shape, 1), sc, NEG)
                p = jnp.exp(sc - lse_ref[h])
                dv_s[:, cols] += jnp.dot(p.astype(BF16), do_b, preferred_element_type=F32)
                dp = lax.dot_general(v_ref[:, cols], do_b, NT_DIMS, preferred_element_type=F32)
                ds = p * (dp - d_ref[h])
                dk_s[:, cols] += jnp.dot(ds.astype(BF16), q, preferred_element_type=F32)

        pl.when(j > kb)(functools.partial(step, False))
        pl.when(j == kb)(functools.partial(step, True))

        @pl.when(j == nq - 1)
        def _():
            dk_ref[...] = dk_s[...]
            dv_ref[...] = dv_s[...].astype(dv_ref.dtype)

    q_spec = pl.BlockSpec((t, nh * LANES), lambda kb, j: (jnp.maximum(j, kb), 0))
    kv_spec = pl.BlockSpec((t, nh * LANES), lambda kb, j: (kb, 0))
    row_spec = pl.BlockSpec((nh, 1, t), lambda kb, j: (0, 0, jnp.maximum(j, kb)))
    return pl.pallas_call(
        body, name=name, grid=(nq, nq), in_specs=[q_spec, kv_spec, kv_spec, q_spec, row_spec, row_spec],
        out_specs=[kv_spec, kv_spec],
        out_shape=[jax.ShapeDtypeStruct((s, nh * LANES), F32), jax.ShapeDtypeStruct((s, nh * LANES), BF16)],
        scratch_shapes=[pltpu.VMEM((t, nh * LANES), F32), pltpu.VMEM((t, nh * LANES), F32)],
        compiler_params=_params(("parallel", "arbitrary")),
    )(qa, ka, va, do, lse_row, d_row)


SWA_PIECE = 128
SWA_KEYS = 2 * SWA_PIECE


def _swa_block(s):
    return min(512, s)


def _swa_piece(hp_ref, h, q, k_ref, v_ref, qpos0, scale):
    kstart = pl.multiple_of(jnp.maximum(qpos0 - SWA_PIECE, 0), SWA_PIECE)
    k = k_ref[pl.ds(kstart, SWA_KEYS), :].astype(BF16)
    v = v_ref[pl.ds(kstart, SWA_KEYS), :].astype(BF16)
    sc = lax.dot_general(q, k, NT_DIMS, preferred_element_type=F32)
    dist = (qpos0 + lax.broadcasted_iota(jnp.int32, sc.shape, 0)) - (kstart + lax.broadcasted_iota(jnp.int32, sc.shape, 1))
    sc = sc * scale - hp_ref[h, 0] * dist.astype(F32)
    sc = jnp.where((dist >= 0) & (dist < SWA_WINDOW), sc, NEG)
    return kstart, k, v, sc


def _swa_fwd(proj, hp, *, name, s, scale):
    tb = _swa_block(s)
    group = SWA_HEADS // SWA_KV_HEADS
    q_off, k_off, v_off = C_QSW[0] // LANES, C_KSW[0] // LANES, C_VSW[0] // LANES

    def body(hp_ref, q_ref, k_ref, v_ref, o_ref, lse_ref):
        h, i = pl.program_id(0), pl.program_id(1)
        sink = hp_ref[h, 1]
        for r in range(0, tb, SWA_PIECE):
            rows = pl.ds(r, SWA_PIECE)
            _, _, v, sc = _swa_piece(hp_ref, h, q_ref[rows, :].astype(BF16), k_ref, v_ref, i * tb + r, scale)
            m = jnp.maximum(jnp.max(sc, axis=1, keepdims=True), sink)
            p = jnp.exp(sc - m)
            l = jnp.sum(p, axis=1, keepdims=True) + jnp.exp(sink - m)
            o_ref[rows, :] = jnp.dot(p.astype(BF16), v, preferred_element_type=F32) / l
            lse_ref[rows, :] = m + jnp.log(l)

    whole = lambda off: pl.BlockSpec((s, LANES), lambda h, i: (0, off + h // group))
    return pl.pallas_call(
        body, name=name, grid=(SWA_HEADS, s // tb),
        in_specs=[pl.BlockSpec(memory_space=pltpu.SMEM), pl.BlockSpec((tb, LANES), lambda h, i: (i, q_off + h)),
                  whole(k_off), whole(v_off)],
        out_specs=[pl.BlockSpec((tb, LANES), lambda h, i: (i, h)), pl.BlockSpec((None, tb, 1), lambda h, i: (h, i, 0))],
        out_shape=[jax.ShapeDtypeStruct((s, SWA_HEADS * LANES), F32), jax.ShapeDtypeStruct((SWA_HEADS, s, 1), F32)],
        compiler_params=_params(("parallel", "parallel")),
    )(hp, proj, proj, proj)


def _swa_bwd(proj, o, do, lse, hp, *, name, s, scale):
    tb = _swa_block(s)
    nqb = s // tb
    group = SWA_HEADS // SWA_KV_HEADS
    q_off, k_off, v_off = C_QSW[0] // LANES, C_KSW[0] // LANES, C_VSW[0] // LANES

    def body(hp_ref, q_ref, k_ref, v_ref, o_ref, do_ref, lse_ref, dq_ref, dk_ref, dv_ref, dsink_ref):
        kh, g, i = pl.program_id(0), pl.program_id(1), pl.program_id(2)
        h = kh * group + g
        sink = hp_ref[h, 1]

        @pl.when((g == 0) & (i == 0))
        def _():
            dk_ref[...] = jnp.zeros_like(dk_ref)
            dv_ref[...] = jnp.zeros_like(dv_ref)

        @pl.when(i == 0)
        def _():
            dsink_ref[...] = jnp.zeros_like(dsink_ref)

        for r in range(0, tb, SWA_PIECE):
            rows = pl.ds(r, SWA_PIECE)
            q = q_ref[rows, :].astype(BF16)
            dov = do_ref[rows, :]
            do_b = dov.astype(BF16)
            lse_r = lse_ref[rows, :]
            d_r = jnp.sum(dov * o_ref[rows, :], axis=1, keepdims=True)
            kstart, k, v, sc = _swa_piece(hp_ref, h, q, k_ref, v_ref, i * tb + r, scale)
            p = jnp.exp(sc - lse_r)
            dp = lax.dot_general(do_b, v, NT_DIMS, preferred_element_type=F32)
            ds = (p * (dp - d_r)).astype(BF16)
            dq_ref[rows, :] = (jnp.dot(ds, k, preferred_element_type=F32) * scale).astype(dq_ref.dtype)
            win = pl.ds(kstart, SWA_KEYS)
            dk_ref[win, :] += lax.dot_general(ds, q, TN_DIMS, preferred_element_type=F32) * scale
            dv_ref[win, :] += lax.dot_general(p.astype(BF16), do_b, TN_DIMS, preferred_element_type=F32)
            part = jnp.sum(-jnp.exp(sink - lse_r) * d_r, axis=0, keepdims=True)
            dsink_ref[...] += jnp.broadcast_to(part, (1, LANES))

    whole = lambda off: pl.BlockSpec((s, LANES), lambda kh, g, i: (0, off + kh))
    q_map = lambda kh, g, i: (i, kh * group + g)
    return pl.pallas_call(
        body, name=name, grid=(SWA_KV_HEADS, group, nqb),
        in_specs=[pl.BlockSpec(memory_space=pltpu.SMEM),
                  pl.BlockSpec((tb, LANES), lambda kh, g, i: (i, q_off + kh * group + g)), whole(k_off), whole(v_off),
                  pl.BlockSpec((tb, LANES), q_map), pl.BlockSpec((tb, LANES), q_map),
                  pl.BlockSpec((None, tb, 1), lambda kh, g, i: (kh * group + g, i, 0))],
        out_specs=[pl.BlockSpec((tb, LANES), q_map), whole(0), whole(0),
                   pl.BlockSpec((None, 1, LANES), lambda kh, g, i: (kh * group + g, 0, 0))],
        out_shape=[jax.ShapeDtypeStruct((s, SWA_HEADS * LANES), BF16),
                   jax.ShapeDtypeStruct((s, SWA_KV_HEADS * LANES), F32), jax.ShapeDtypeStruct((s, SWA_KV_HEADS * LANES), F32),
                   jax.ShapeDtypeStruct((SWA_HEADS, 1, LANES), F32)],
        compiler_params=_params(("parallel", "arbitrary", "arbitrary")),
    )(hp, proj, proj, proj, o, do, lse)


def _shift_down(z, k):
    rows = lax.broadcasted_iota(jnp.int32, z.shape, 0)
    return jnp.where(rows >= k, pltpu.roll(z, k, 0), 0.0)


def _shift_up(z, k):
    n = z.shape[0]
    rows = lax.broadcasted_iota(jnp.int32, z.shape, 0)
    return jnp.where(rows < n - k, pltpu.roll(z, n - k, 0), 0.0)


def _rows3(a, b, c):
    r = lax.broadcasted_iota(jnp.int32, (3, a.shape[1]), 0)
    return jnp.where(r == 0, a, jnp.where(r == 1, b, c))


def _col_spec(s, off):
    return pl.BlockSpec((s, LANES), functools.partial(lambda j, off: (0, off + j), off=off))


def _conv_fwd(proj, conv_w, *, name, s):
    def body(gb_ref, gc_ref, u_ref, w_ref, y_ref):
        w0, w1, w2 = w_ref[0:1, :], w_ref[1:2, :], w_ref[2:3, :]
        z = gc_ref[...] * u_ref[...]
        c = w2 * z + w1 * _shift_down(z, 1) + w0 * _shift_down(z, 2)
        y_ref[...] = gb_ref[...] * c

    return pl.pallas_call(
        body, name=name, grid=(2,),
        in_specs=[_col_spec(s, C_GB[0] // LANES), _col_spec(s, C_GC[0] // LANES), _col_spec(s, C_UCONV[0] // LANES),
                  pl.BlockSpec((3, LANES), lambda j: (0, j))],
        out_specs=_col_spec(s, 0), out_shape=jax.ShapeDtypeStruct((s, D_GROUP), F32),
        compiler_params=_params(("parallel",)),
    )(proj, proj, proj, conv_w)


def _conv_bwd(dy, proj, conv_w, *, name, s):
    def body(dy_ref, gb_ref, gc_ref, u_ref, w_ref, dgb_ref, dgc_ref, du_ref, dw_ref):
        w0, w1, w2 = w_ref[0:1, :], w_ref[1:2, :], w_ref[2:3, :]
        gc, u, dyv = gc_ref[...], u_ref[...], dy_ref[...]
        z = gc * u
        z1, z2 = _shift_down(z, 1), _shift_down(z, 2)
        c = w2 * z + w1 * z1 + w0 * z2
        dgb_ref[...] = (dyv * c).astype(dgb_ref.dtype)
        dc = dyv * gb_ref[...]
        dz = w2 * dc + w1 * _shift_up(dc, 1) + w0 * _shift_up(dc, 2)
        dgc_ref[...] = (dz * u).astype(dgc_ref.dtype)
        du_ref[...] = (dz * gc).astype(du_ref.dtype)
        dw_ref[...] = _rows3(jnp.sum(dc * z2, axis=0, keepdims=True), jnp.sum(dc * z1, axis=0, keepdims=True),
                             jnp.sum(dc * z, axis=0, keepdims=True))

    act = jax.ShapeDtypeStruct((s, D_GROUP), BF16)
    return pl.pallas_call(
        body, name=name, grid=(2,),
        in_specs=[_col_spec(s, 0), _col_spec(s, C_GB[0] // LANES), _col_spec(s, C_GC[0] // LANES),
                  _col_spec(s, C_UCONV[0] // LANES), pl.BlockSpec((3, LANES), lambda j: (0, j))],
        out_specs=[_col_spec(s, 0), _col_spec(s, 0), _col_spec(s, 0), pl.BlockSpec((3, LANES), lambda j: (0, j))],
        out_shape=[act, act, act, jax.ShapeDtypeStruct((3, D_GROUP), F32)],
        compiler_params=_params(("parallel",)),
    )(dy, proj, proj, proj, conv_w)


def _pool_select(j, lane, a2, a4, a8, a16):
    lo = lane < HEAD
    return jnp.where(j == 0, jnp.where(lo, a2, a4), jnp.where(lo, a8, a16))


def _pooled(u, j):
    s2 = u + _shift_down(u, 1)
    s4 = s2 + _shift_down(s2, 2)
    s8 = s4 + _shift_down(s4, 4)
    s16 = s8 + _shift_down(s8, 8)
    lane = lax.broadcasted_iota(jnp.int32, u.shape, 1)
    rows = lax.broadcasted_iota(jnp.int32, u.shape, 0)
    win = _pool_select(j, lane, 2, 4, 8, 16)
    count = jnp.minimum(rows + 1, win).astype(F32)
    return _pool_select(j, lane, s2, s4, s8, s16) / count - u, count


def _pool_fwd(proj, wbd, scale, *, name, s):
    def body(u_ref, w_ref, sc_ref, y_ref):
        pooled, _ = _pooled(u_ref[...], pl.program_id(0))
        y_ref[...] = jnp.dot(pooled.astype(BF16), w_ref[...].astype(BF16), preferred_element_type=F32) * sc_ref[...]

    return pl.pallas_call(
        body, name=name, grid=(2,),
        in_specs=[_col_spec(s, C_UPOOL[0] // LANES), pl.BlockSpec((None, LANES, LANES), lambda j: (j, 0, 0)),
                  pl.BlockSpec((1, LANES), lambda j: (0, j))],
        out_specs=_col_spec(s, 0), out_shape=jax.ShapeDtypeStruct((s, D_GROUP), F32),
        compiler_params=_params(("parallel",)),
    )(proj, wbd, scale)


def _pool_bwd(dy, proj, wbd, scale, *, name, s):
    def body(dy_ref, u_ref, w_ref, sc_ref, du_ref, dw_ref, dsc_ref):
        j = pl.program_id(0)
        pooled, count = _pooled(u_ref[...], j)
        pooled_b = pooled.astype(BF16)
        w_b = w_ref[...].astype(BF16)
        dyv = dy_ref[...]
        mixed = jnp.dot(pooled_b, w_b, preferred_element_type=F32)
        dsc_ref[...] = jnp.sum(dyv * mixed, axis=0, keepdims=True)
        dms = (dyv * sc_ref[...]).astype(BF16)
        dw_ref[...] = lax.dot_general(pooled_b, dms, (((0,), (0,)), ((), ())), preferred_element_type=F32)
        dpooled = lax.dot_general(dms, w_b, (((1,), (1,)), ((), ())), preferred_element_type=F32)
        r = dpooled / count
        a2 = r + _shift_up(r, 1)
        a4 = a2 + _shift_up(a2, 2)
        a8 = a4 + _shift_up(a4, 4)
        a16 = a8 + _shift_up(a8, 8)
        lane = lax.broadcasted_iota(jnp.int32, r.shape, 1)
        du_ref[...] = (_pool_select(j, lane, a2, a4, a8, a16) - dpooled).astype(du_ref.dtype)

    return pl.pallas_call(
        body, name=name, grid=(2,),
        in_specs=[_col_spec(s, 0), _col_spec(s, C_UPOOL[0] // LANES),
                  pl.BlockSpec((None, LANES, LANES), lambda j: (j, 0, 0)), pl.BlockSpec((1, LANES), lambda j: (0, j))],
        out_specs=[_col_spec(s, 0), pl.BlockSpec((None, LANES, LANES), lambda j: (j, 0, 0)),
                   pl.BlockSpec((1, LANES), lambda j: (0, j))],
        out_shape=[jax.ShapeDtypeStruct((s, D_GROUP), BF16), jax.ShapeDtypeStruct((2, LANES, LANES), F32),
                   jax.ShapeDtypeStruct((1, D_GROUP), F32)],
        compiler_params=_params(("parallel",)),
    )(dy, proj, wbd, scale)


def _mesh_pos():
    return lax.axis_index("x"), lax.axis_index("y"), lax.axis_index("c")


def _any_specs(n):
    return [pl.BlockSpec(memory_space=pl.ANY)] * n


def _all_gather(xs, *, name):
    n = len(xs)

    def body(*refs):
        x_refs, out_refs = refs[:n], refs[n:2 * n]
        send_sems, recv_sems, local_sems = refs[2 * n:]
        x, y, cc = _mesh_pos()
        me, sibling = (x, y, cc), (x, y, 1 - cc)
        chips = [(1 - x, y), (x, 1 - y), (1 - x, 1 - y)]

        def slot(a, px, py, pc):
            return out_refs[a].at[4 * px + 2 * py + pc]

        def copy(a, k, block, to, src=None):
            return pltpu.make_async_remote_copy(
                src_ref=slot(a, *block) if src is None else src, dst_ref=slot(a, *block), send_sem=send_sems.at[a, k],
                recv_sem=recv_sems.at[a, k], device_id=to, device_id_type=pl.DeviceIdType.MESH)

        mine = [pltpu.make_async_copy(x_refs[a], slot(a, *me), local_sems.at[a]) for a in range(n)]
        first = []
        for a in range(n):
            first.append(copy(a, 0, me, sibling, src=x_refs[a]))
            first += [copy(a, 1 + j, me, (*chip, cc), src=x_refs[a]) for j, chip in enumerate(chips)]
        for cp in mine + first:
            cp.start()
        passed = []
        for j, chip in enumerate(chips):
            for a in range(n):
                copy(a, 1 + j, (*chip, cc), me).wait_recv()
                passed.append(copy(a, 4 + j, (*chip, cc), sibling))
                passed[-1].start()
        for a in range(n):
            copy(a, 0, sibling, me).wait_recv()
        for j, chip in enumerate(chips):
            for a in range(n):
                copy(a, 4 + j, (*chip, 1 - cc), me).wait_recv()
        for cp in first + passed:
            cp.wait_send()
        for cp in mine:
            cp.wait()

    return pl.pallas_call(
        body, name=name, out_shape=[jax.ShapeDtypeStruct((N_DEV,) + a.shape, a.dtype) for a in xs],
        in_specs=_any_specs(n), out_specs=_any_specs(n),
        scratch_shapes=[pltpu.SemaphoreType.DMA((n, 7)), pltpu.SemaphoreType.DMA((n, 7)), pltpu.SemaphoreType.DMA((n,))],
    )(*xs)


def _plan_gather(src_refs, land_refs, send_sems, recv_sems):
    x, y, cc = _mesh_pos()
    me = 4 * x + 2 * y + cc
    plan = []
    for a, (src, land) in enumerate(zip(src_refs, land_refs)):
        for k in range(1, N_DEV):
            px, py, pc = (1 - x if k & 4 else x), (1 - y if k & 2 else y), (1 - cc if k & 1 else cc)
            i = a * (N_DEV - 1) + k - 1
            sems = dict(send_sem=send_sems.at[i], recv_sem=recv_sems.at[i], device_id=(px, py, pc),
                        device_id_type=pl.DeviceIdType.MESH)
            plan.append((pltpu.make_async_remote_copy(src_ref=src, dst_ref=land.at[me], **sems),
                         pltpu.make_async_remote_copy(src_ref=src, dst_ref=land.at[4 * px + 2 * py + pc], **sems)))
    return plan


def _plan_sibling(src_refs, land_refs, send_sems, recv_sems):
    x, y, cc = _mesh_pos()
    plan = []
    for a, (src, land) in enumerate(zip(src_refs, land_refs)):
        cp = pltpu.make_async_remote_copy(
            src_ref=src.at[1 - cc], dst_ref=land, send_sem=send_sems.at[a], recv_sem=recv_sems.at[a],
            device_id=(x, y, 1 - cc), device_id_type=pl.DeviceIdType.MESH)
        plan.append((cp, cp))
    return plan


def _plan_chips(src_refs, land_refs, send_sems, recv_sems):
    x, y, cc = _mesh_pos()
    my_chip = 2 * x + y
    plan = []
    for a, (src, land) in enumerate(zip(src_refs, land_refs)):
        for j, (px, py) in enumerate([(1 - x, y), (x, 1 - y), (1 - x, 1 - y)]):
            peer = 2 * px + py
            sems = dict(send_sem=send_sems.at[3 * a + j], recv_sem=recv_sems.at[3 * a + j], device_id=(px, py, cc),
                        device_id_type=pl.DeviceIdType.MESH)
            plan.append((pltpu.make_async_remote_copy(src_ref=src.at[peer], dst_ref=land.at[my_chip], **sems),
                         pltpu.make_async_remote_copy(src_ref=src.at[peer], dst_ref=land.at[peer], **sems)))
    return plan


HBM_SPEC = pl.BlockSpec(memory_space=pltpu.HBM)
SEM_SPEC = pl.BlockSpec(memory_space=pltpu.SEMAPHORE)
ANY_SPEC = pl.BlockSpec(memory_space=pl.ANY)
SIDE_EFFECT = pltpu.CompilerParams(has_side_effects=pltpu.SideEffectType.DATAFLOW_SIDE_EFFECTING)


def _start_copies(plan, sems_per_array, srcs, land_shapes, after, *, name):
    n, n_after = len(srcs), len(after)

    def body(*refs):
        send_sems, recv_sems = refs[2 * n + n_after], refs[2 * n + n_after + 1]
        for out, _ in plan(refs[:n], refs[n:2 * n], send_sems, recv_sems):
            out.start()
        refs[-1][...] = jnp.zeros_like(refs[-1])

    lands = [lax.empty(shp, a.dtype) for shp, a in zip(land_shapes, srcs)]
    sem = pltpu.SemaphoreType.DMA((n * sems_per_array,))
    res = pl.pallas_call(
        body, name=name,
        out_shape=(sem, sem, *[pltpu.HBM(a.shape, a.dtype) for a in srcs + lands], jax.ShapeDtypeStruct((8, LANES), F32)),
        in_specs=[HBM_SPEC] * (2 * n) + [ANY_SPEC] * n_after,
        out_specs=(SEM_SPEC, SEM_SPEC, *[HBM_SPEC] * (2 * n), pl.BlockSpec(memory_space=pltpu.VMEM)),
        input_output_aliases={i: 2 + i for i in range(2 * n)}, compiler_params=SIDE_EFFECT,
    )(*[pltpu.with_memory_space_constraint(a, pltpu.HBM) for a in srcs + lands], *after)
    return (res[0], res[1], list(res[2:2 + n]), list(res[2 + n:2 + 2 * n])), res[-1]


def _wait_copies(plan, handle, after, *, name):
    send, recv, srcs, lands = handle
    n = len(srcs)

    def body(*refs):
        for out, inc in plan(refs[:n], refs[n:2 * n], refs[2 * n], refs[2 * n + 1]):
            out.wait_send()
            inc.wait_recv()

    res = pl.pallas_call(
        body, name=name, out_shape=tuple(pltpu.HBM(a.shape, a.dtype) for a in srcs + lands),
        in_specs=[HBM_SPEC] * (2 * n) + [SEM_SPEC, SEM_SPEC] + [ANY_SPEC] * len(after), out_specs=[HBM_SPEC] * (2 * n),
        input_output_aliases={i: i for i in range(2 * n)}, compiler_params=SIDE_EFFECT,
    )(*srcs, *lands, send, recv, *after)
    return list(res[:n]), list(res[n:])


def _row_tile(rows, target=512):
    if rows <= target:
        return rows
    best = None
    for t in range(8, target + 1, 8):
        if rows % t == 0:
            best = t
    assert best is not None, (rows, target)
    return best


def _add_own(g, other, core, *, name):
    _, _, rows, cols = g.shape
    tm = _row_tile(rows)

    def body(c_ref, g_ref, o_ref, out_ref):
        out_ref[...] = g_ref[...] + o_ref[...]

    return pl.pallas_call(
        body, name=name, out_shape=jax.ShapeDtypeStruct(other.shape, other.dtype),
        grid_spec=pltpu.PrefetchScalarGridSpec(
            num_scalar_prefetch=1, grid=(4, rows // tm),
            in_specs=[pl.BlockSpec((None, None, tm, cols), lambda p, i, c_ref: (c_ref[0], p, i, 0)),
                      pl.BlockSpec((None, tm, cols), lambda p, i, c_ref: (p, i, 0))],
            out_specs=pl.BlockSpec((None, tm, cols), lambda p, i, c_ref: (p, i, 0))),
        compiler_params=_params(("parallel", "parallel")),
    )(core, g, other)


def _adamw(parts, w, m, v, *, name):
    layers, rows, cols = w.shape
    assert len(parts) == layers
    tm = _row_tile(rows, 256)
    nr = rows // tm

    def body(*refs):
        p_refs = refs[:layers]
        w_ref, m_ref, v_ref, g_ref, d_ref, nm_ref, nv_ref, g_s = refs[layers:]
        for ll in range(layers):
            @pl.when(pl.program_id(0) == ll)
            def _(ll=ll):
                g = p_refs[ll][0]
                for q in range(1, p_refs[ll].shape[0]):
                    g = g + p_refs[ll][q]
                g_s[...] = g

        g = g_s[...]
        mm = ADAM_B1 * m_ref[...] + (1.0 - ADAM_B1) * g
        vv = ADAM_B2 * v_ref[...] + (1.0 - ADAM_B2) * jnp.square(g)
        m_hat = mm / (1.0 - ADAM_B1 ** ADAM_STEP)
        v_hat = vv / (1.0 - ADAM_B2 ** ADAM_STEP)
        g_ref[...] = g
        d_ref[...] = -ADAM_LR * (m_hat / (jnp.sqrt(v_hat) + ADAM_EPS) + ADAM_WD * w_ref[...])
        nm_ref[...] = mm
        nv_ref[...] = vv

    def part_spec(ll, p):
        return pl.BlockSpec((p, tm, cols), lambda l, i: (0, jnp.where(l == ll, i, jnp.where(l < ll, 0, nr - 1)), 0))

    spec = pl.BlockSpec((None, tm, cols), lambda l, i: (l, i, 0))
    out = jax.ShapeDtypeStruct(w.shape, F32)
    return pl.pallas_call(
        body, name=name, grid=(layers, nr),
        in_specs=[part_spec(ll, parts[ll].shape[0]) for ll in range(layers)] + [spec] * 3,
        out_specs=[spec] * 4, out_shape=[out] * 4, scratch_shapes=[pltpu.VMEM((tm, cols), F32)],
        compiler_params=_params(("arbitrary", "arbitrary")),
    )(*parts, w, m, v)


def _pack(arrs):
    flat = jnp.concatenate([a.reshape(-1) for a in arrs])
    rows = -(-flat.shape[0] // (PACK_COLS * 16)) * 16
    return jnp.pad(flat, (0, rows * PACK_COLS - flat.shape[0])).reshape(rows, PACK_COLS)


def _unpack(packed, shapes):
    flat = packed.reshape(-1)
    out, off = [], 0
    for shp in shapes:
        n = int(np.prod(shp))
        out.append(flat[off:off + n].reshape(shp))
        off += n
    return out


def _shards_to_full(g, axis):
    if axis == 0:
        return g.reshape(g.shape[0] * g.shape[1], g.shape[2])
    return jnp.transpose(g, (1, 0, 2)).reshape(g.shape[1], g.shape[0] * g.shape[2])


def _full_to_shards(a, axis):
    if axis == 0:
        return jnp.transpose(a.reshape(4, 2, a.shape[0] // N_DEV, a.shape[1]), (1, 0, 2, 3))
    return jnp.transpose(a.reshape(a.shape[0], 4, 2, a.shape[1] // N_DEV), (2, 1, 0, 3))


def _zeros_like_cols(a, n):
    return jnp.zeros(a.shape[:-1] + (n,), a.dtype)


def _swap_cols(r, sign):
    half = MLA_ROPE // 2
    if sign > 0:
        return jnp.concatenate([-r[..., half:], r[..., :half]], axis=-1)
    return jnp.concatenate([r[..., half:], -r[..., :half]], axis=-1)


def _pad_heads(a, n):
    z = _zeros_like_cols(a, HEAD)
    return jnp.concatenate([p for h in range(n) for p in (a[..., h * HEAD:(h + 1) * HEAD], z)], axis=-1)


def _unpad_heads(a, n):
    return jnp.concatenate([a[..., h * LANES:h * LANES + HEAD] for h in range(n)], axis=-1)


def _ext_w_in(w):
    z = _zeros_like_cols(w, HEAD)
    kr = w[..., 384:416]
    krp = _swap_cols(kr, 1)
    return jnp.concatenate([
        _pad_heads(w[..., 1440:1696], 4), w[..., 0:256], w[..., 416:672], w[..., 672:928], w[..., 928:1184],
        w[..., 1184:1440], _pad_heads(w[..., 1696:1824], 2), _pad_heads(w[..., 1824:1952], 2), w[..., 256:384],
        z, kr, kr, z, krp, krp], axis=-1)


def _fold_w_in(e):
    kr = (e[..., 2496:2528] + e[..., 2528:2560]) + _swap_cols(e[..., 2624:2656] + e[..., 2656:2688], -1)
    return jnp.concatenate([
        e[..., 512:768], e[..., 2304:2432], kr, e[..., 768:1024], e[..., 1024:1280], e[..., 1280:1536],
        e[..., 1536:1792], _unpad_heads(e[..., 0:512], 4), _unpad_heads(e[..., 1792:2048], 2),
        _unpad_heads(e[..., 2048:2304], 2)], axis=-1)


def _ext_w_uq(w):
    parts = []
    for h in range(MLA_HEADS):
        rot = w[..., 96 * h + 64:96 * h + 96]
        parts += [w[..., 96 * h:96 * h + 64], rot, _swap_cols(rot, 1)]
    return jnp.concatenate(parts, axis=-1)


def _fold_w_uq(e):
    parts = []
    for h in range(MLA_HEADS):
        b = LANES * h
        parts += [e[..., b:b + 64], e[..., b + 64:b + 96] + _swap_cols(e[..., b + 96:b + 128], -1)]
    return jnp.concatenate(parts, axis=-1)


def _ext_w_ukv(w):
    k = jnp.concatenate([w[..., LANES * h:LANES * h + HEAD] for h in range(MLA_HEADS)], axis=-1)
    v = jnp.concatenate([w[..., LANES * h + HEAD:LANES * (h + 1)] for h in range(MLA_HEADS)], axis=-1)
    return jnp.concatenate([_pad_heads(k, 4), _pad_heads(v, 4)], axis=-1)


def _fold_w_ukv(e):
    parts = []
    for h in range(MLA_HEADS):
        parts += [e[..., LANES * h:LANES * h + HEAD], e[..., 512 + LANES * h:512 + LANES * h + HEAD]]
    return jnp.concatenate(parts, axis=-1)


def _ext_mix(a):
    return jnp.concatenate([_pad_heads(a[..., 0:256], 4), a[..., 256:768], _pad_heads(a[..., 768:1024], 4)], axis=-1)


def _fold_mix(e):
    return jnp.concatenate([_unpad_heads(e[..., 0:512], 4), e[..., 512:1024], _unpad_heads(e[..., 1024:1536], 4)], axis=-1)


def _rope_tables(s):
    inv = 1.0 / (ROPE_THETA ** (jnp.arange(0, MLA_ROPE, 2, dtype=F32) / MLA_ROPE))
    ang = jnp.arange(s, dtype=F32)[:, None] * inv[None, :]
    cos, sin = jnp.cos(ang), jnp.sin(ang)
    c32, s32 = jnp.concatenate([cos, cos], axis=1), jnp.concatenate([sin, sin], axis=1)
    zeros, ones = jnp.zeros((s, HEAD), F32), jnp.ones((s, HEAD), F32)
    tq = jnp.concatenate([ones, c32, s32], axis=1) * (1.0 / math.sqrt(MLA_NOPE + MLA_ROPE))
    return (jnp.tile(tq, (1, MLA_HEADS)), jnp.concatenate([zeros, c32, c32], axis=1),
            jnp.concatenate([zeros, s32, s32], axis=1))


def _gn(y):
    return y * _rstd(y, D_GROUP)


def _mixer_fwd(x, w, tabs, l):
    s = x.shape[0]
    tq, tkc, tks = tabs
    n = lambda t: f"l{l}_{t}"
    h = _rms_fwd(("row", x, D_MODEL, 0), w["attn_norm"], name=n("attn_norm"), rows=s, width=D_MODEL)
    proj = _mm(h, w["w_in"], name=n("in_proj"))
    cqn = _rms_fwd(("row", proj, 256, C_CQ[0] // 256), w["mla_q_norm"], name=n("q_norm"), rows=s, width=256)
    ckvn = _rms_fwd(("row", proj, 128, C_CKV[0] // 128), w["mla_kv_norm"], name=n("kv_norm"), rows=s, width=128)
    q_ext = _mm(cqn, w["w_uq"], name=n("uq"))
    kv_ext = _mm(ckvn, w["w_ukv"], name=n("ukv"))

    def prep(qe, kvk, kvv, kr, krp, tqv, tc, ts):
        kb = kr * tc + krp * ts
        lane = lax.broadcasted_iota(jnp.int32, kvv.shape, 1) & (LANES - 1)
        return qe * tqv, kvk + jnp.tile(kb, (1, MLA_HEADS)), jnp.where(lane == HEAD, 1.0, kvv)

    qm, km, vm = _rowwise(
        prep, name=n("mla_prep"), rows=s,
        ins=[("row", q_ext, 512, 0), ("row", kv_ext, 512, 0), ("row", kv_ext, 512, 1),
             ("row", proj, 128, C_KR[0] // 128), ("row", proj, 128, C_KRP[0] // 128),
             ("row", tq, 512, 0), ("row", tkc, 128, 0), ("row", tks, 128, 0)],
        outs=[(512, BF16)] * 3)
    y_a, lse_a = _mla_fwd(qm, km, vm, name=n("mla_fwd"), s=s, nh=MLA_HEADS)
    y_b = _conv_fwd(proj, w["conv_w"], name=n("conv_fwd"), s=s)
    y_c = _pool_fwd(proj, w["pool_wbd"], w["pool_scale"], name=n("pool_fwd"), s=s)
    y_d, lse_d = _swa_fwd(proj, w["hp_swa"], name=n("swa_fwd"), s=s, scale=1.0 / math.sqrt(HEAD))

    def mix(ya, yb, yc, yd, mn):
        return (jnp.concatenate([_gn(ya), _gn(yb), _gn(yc), _gn(yd)], axis=1) * mn,)

    mixed = _rowwise(mix, name=n("group_norm"), rows=s,
                     ins=[("row", y_a, 512, 0), ("row", y_b, 256, 0), ("row", y_c, 256, 0), ("row", y_d, 512, 0),
                          ("full", w["mix_norm"])], outs=[(D_MIX_EXT, BF16)])[0]
    x1 = _mm(mixed, w["w_o"], res=x, name=n("out_proj"))
    saved = dict(x=x, h=h, proj=proj, cqn=cqn, ckvn=ckvn, qm=qm, km=km, vm=vm, y_a=y_a, lse_a=lse_a, y_b=y_b, y_c=y_c,
                 y_d=y_d, lse_d=lse_d, mixed=mixed)
    return x1, saved


def _ffn_fwd(x1, w, l):
    s = x1.shape[0]
    n = lambda t: f"l{l}_{t}"
    h2 = _rms_fwd(("row", x1, D_MODEL, 0), w["ffn_norm"], name=n("ffn_norm"), rows=s, width=D_MODEL)

    def swiglu(gu):
        g, u = gu[:, :GU_TILE], gu[:, GU_TILE:]
        return gu, g * jax.nn.sigmoid(g) * u

    gu, act = _mm(h2, w["w_gate_up"], tn=2 * GU_TILE, name=n("gate_up"),
                  epilogue=(swiglu, [], [(2 * D_FF, F32), (D_FF, BF16)]))
    x2 = _mm(act, w["w_down"], res=x1, name=n("down"))
    return x2, dict(x1=x1, h2=h2, gu=gu, act=act)


def _ffn_bwd_down(dx2, sv, w, l):
    n = lambda t: f"l{l}_{t}"

    def swiglu_bwd(da, gu):
        gt, u = gu[:, :GU_TILE], gu[:, GU_TILE:]
        sg = jax.nn.sigmoid(gt)
        return (jnp.concatenate([da * u * sg * (1.0 + gt * (1.0 - sg)), da * gt * sg], axis=1),)

    dgu = _mm(dx2, w["w_down"], tb=True, tn=GU_TILE, name=n("d_act"),
              epilogue=(swiglu_bwd, [sv["gu"]], [(2 * D_FF, BF16)]))[0]
    g = dict(w_down=_mm(sv["act"], dx2, ta=True, name=n("dw_down")))
    return dgu, g


def _ffn_bwd_up(dx2, dgu, sv, w, l):
    s = dx2.shape[0]
    n = lambda t: f"l{l}_{t}"
    dh2 = _mm(dgu, w["w_gate_up"], tb=True, name=n("d_h2"))
    g = dict(w_gate_up=_mm(sv["h2"], dgu, ta=True, name=n("dw_gate_up")))
    dx1, g["ffn_norm"] = _rms_bwd(("row", sv["x1"], D_MODEL, 0), w["ffn_norm"], dh2, dx2, name=n("ffn_norm_bwd"),
                                  rows=s, width=D_MODEL, out_dtype=F32)
    return dx1, g


def _mixer_bwd_out(dx1, sv, w, l):
    s = dx1.shape[0]
    n = lambda t: f"l{l}_{t}"
    dmixed = _mm(dx1, w["w_o"], tb=True, name=n("d_mixed"))
    g = dict(w_o=_mm(sv["mixed"], dx1, ta=True, name=n("dw_o")))

    def mix_bwd(dm, ya, yb, yc, yd, mn):
        outs, dmn = [], []
        for y, lo, hi in ((ya, 0, 512), (yb, 512, 768), (yc, 768, 1024), (yd, 1024, 1536)):
            r = _rstd(y, D_GROUP)
            nrm = y * r
            dmg = dm[:, lo:hi]
            dn = dmg * mn[:, lo:hi]
            outs.append(r * (dn - nrm * (jnp.sum(dn * nrm, axis=-1, keepdims=True) * (1.0 / D_GROUP))))
            dmn.append(jnp.sum(dmg * nrm, axis=0, keepdims=True))
        return (*outs, jnp.concatenate(dmn, axis=1))

    dy_a, dy_b, dy_c, dy_d, g["mix_norm"] = _rowwise(
        mix_bwd, name=n("group_norm_bwd"), rows=s,
        ins=[("row", dmixed, D_MIX_EXT, 0), ("row", sv["y_a"], 512, 0), ("row", sv["y_b"], 256, 0),
             ("row", sv["y_c"], 256, 0), ("row", sv["y_d"], 512, 0), ("full", w["mix_norm"])],
        outs=[(512, F32), (256, F32), (256, F32), (512, F32)], reds=[(1, D_MIX_EXT)])
    return (dy_a, dy_b, dy_c, dy_d), g


def _mixer_bwd_in(dx1, dys, sv, w, tabs, l):
    s = dx1.shape[0]
    tq, tkc, tks = tabs
    n = lambda t: f"l{l}_{t}"
    dy_a, dy_b, dy_c, dy_d = dys
    g = {}

    proj = sv["proj"]
    dq_sw, dk_sw, dv_sw, dsink = _swa_bwd(proj, sv["y_d"], dy_d, sv["lse_d"], w["hp_swa"], name=n("swa_bwd"), s=s,
                                          scale=1.0 / math.sqrt(HEAD))
    g["swa_sinks"] = dsink[:, 0, 0]

    qkv = (sv["qm"], sv["km"], sv["vm"])
    dqm, dvec_a = _mla_dq(*qkv, sv["y_a"], dy_a, sv["lse_a"], name=n("mla_dq"), s=s, nh=MLA_HEADS)
    dkm, dvm = _mla_dkv(*qkv, dy_a, sv["lse_a"].reshape(MLA_HEADS, 1, s), dvec_a.reshape(MLA_HEADS, 1, s),
                        name=n("mla_dkv"), s=s, nh=MLA_HEADS)

    def prep_bwd(dq, dk, tqv, tc, ts):
        dkb = dk[:, 0:128] + dk[:, 128:256] + dk[:, 256:384] + dk[:, 384:512]
        return dq * tqv, dk, dkb * tc, dkb * ts

    dq_ext, dkvk, dkr, dkrp = _rowwise(
        prep_bwd, name=n("mla_prep_bwd"), rows=s,
        ins=[("row", dqm, 512, 0), ("row", dkm, 512, 0), ("row", tq, 512, 0), ("row", tkc, 128, 0), ("row", tks, 128, 0)],
        outs=[(512, BF16), (512, BF16), (128, BF16), (128, BF16)])
    dkv_ext = jnp.concatenate([dkvk, dvm], axis=1)
    dcqn = _mm(dq_ext, w["w_uq"], tb=True, name=n("d_cqn"))
    g["w_uq"] = _mm(sv["cqn"], dq_ext, ta=True, name=n("dw_uq"))
    dckvn = _mm(dkv_ext, w["w_ukv"], tb=True, name=n("d_ckvn"))
    g["w_ukv"] = _mm(sv["ckvn"], dkv_ext, ta=True, name=n("dw_ukv"))
    dcq, g["mla_q_norm"] = _rms_bwd(("row", proj, 256, C_CQ[0] // 256), w["mla_q_norm"], dcqn, None,
                                    name=n("q_norm_bwd"), rows=s, width=256, out_dtype=BF16)
    dckv, g["mla_kv_norm"] = _rms_bwd(("row", proj, 128, C_CKV[0] // 128), w["mla_kv_norm"], dckvn, None,
                                      name=n("kv_norm_bwd"), rows=s, width=128, out_dtype=BF16)

    dgb, dgc, duc, g["conv_w"] = _conv_bwd(dy_b, proj, w["conv_w"], name=n("conv_bwd"), s=s)
    dup, g["pool_wbd"], g["pool_scale"] = _pool_bwd(dy_c, proj, w["pool_wbd"], w["pool_scale"], name=n("pool_bwd"), s=s)

    dproj = jnp.concatenate([dq_sw, dcq, dgb, dgc, duc, dup, dk_sw.astype(BF16), dv_sw.astype(BF16), dckv, dkr, dkrp],
                            axis=1)
    dh = _mm(dproj, w["w_in"], tb=True, name=n("d_h"))
    g["w_in"] = _mm(sv["h"], dproj, ta=True, name=n("dw_in"))
    dx0, g["attn_norm"] = _rms_bwd(("row", sv["x"], D_MODEL, 0), w["attn_norm"], dh, dx1, name=n("attn_norm_bwd"),
                                   rows=s, width=D_MODEL, out_dtype=F32)
    return dx0, g


def _loss_head(x, target, g, *, s):
    def fn(xv, tv, gv):
        r = _rstd(xv, D_MODEL)
        e = xv * r * gv - tv
        part = jnp.sum(jnp.sum(e * e, axis=1, keepdims=True), axis=0, keepdims=True) * (0.5 / D_MODEL)
        dy = e * (1.0 / D_MODEL)
        dyg = dy * gv
        dx = r * dyg - xv * (r * r * r) * (jnp.sum(dyg * xv, axis=-1, keepdims=True) * (1.0 / D_MODEL))
        return dx, jnp.sum(dy * xv * r, axis=0, keepdims=True), jnp.broadcast_to(part, (1, LANES))

    return _rowwise(fn, name="loss_head", rows=s,
                    ins=[("row", x, D_MODEL, 0), ("row", target, D_MODEL, 0), ("full", g)],
                    outs=[(D_MODEL, F32)], reds=[(1, D_MODEL), (1, LANES)])


def _alibi_slopes(n):
    return np.asarray([2.0 ** (-8.0 * (i + 1) / n) for i in range(n)], dtype=np.float32)


MIXER_WEIGHTS = ("w_in", "w_uq", "w_ukv", "conv_w", "w_o")
FFN_WEIGHTS = ("w_gate_up", "w_down")


def _mixer_weights(full, rep, l):
    pw = rep["pool_w"][l]
    z = jnp.zeros((HEAD, HEAD), F32)
    wbd = jnp.stack([jnp.block([[pw[2 * j], z], [z, pw[2 * j + 1]]]) for j in range(2)])
    w_o_ext = jnp.moveaxis(_ext_mix(jnp.moveaxis(full["w_o"], 0, -1)), -1, 0)
    return dict(
        attn_norm=rep["attn_norm"][l][None], w_in=_ext_w_in(full["w_in"]), mla_q_norm=rep["mla_q_norm"][l][None],
        w_uq=_ext_w_uq(full["w_uq"]), mla_kv_norm=rep["mla_kv_norm"][l][None], w_ukv=_ext_w_ukv(full["w_ukv"]),
        conv_w=full["conv_w"], pool_wbd=wbd, pool_scale=rep["pool_scale"][l][None],
        mix_norm=_ext_mix(rep["mix_norm"][l])[None], w_o=w_o_ext,
        hp_swa=jnp.stack([jnp.asarray(_alibi_slopes(SWA_HEADS)), rep["swa_sinks"][l]], axis=1))


def _ext_w_gate_up(w):
    return jnp.swapaxes(w.reshape(w.shape[0], 2, D_FF // GU_TILE, GU_TILE), 1, 2).reshape(w.shape)


def _fold_w_gate_up(e):
    return jnp.swapaxes(e.reshape(e.shape[0], D_FF // GU_TILE, 2, GU_TILE), 1, 2).reshape(e.shape)


def _ffn_weights(full, rep, l):
    return dict(ffn_norm=rep["ffn_norm"][l][None], w_gate_up=_ext_w_gate_up(full["w_gate_up"]), w_down=full["w_down"])


def _fold_grads(g):
    fold = dict(
        w_in=_fold_w_in, w_uq=_fold_w_uq, w_ukv=_fold_w_ukv, w_gate_up=_fold_w_gate_up,
        w_o=lambda e: jnp.moveaxis(_fold_mix(jnp.moveaxis(e, 0, -1)), -1, 0),
        mix_norm=lambda e: _fold_mix(e[0]),
        pool_wbd=lambda e: jnp.stack([e[j // 2][HEAD * (j % 2):HEAD * (j % 2 + 1), HEAD * (j % 2):HEAD * (j % 2 + 1)]
                                      for j in range(4)]))
    rows = ("attn_norm", "mla_q_norm", "mla_kv_norm", "pool_scale", "ffn_norm")
    out = {}
    for nm, e in g.items():
        out["pool_w" if nm == "pool_wbd" else nm] = fold[nm](e) if nm in fold else (e[0] if nm in rows else e)
    return out


def kernel(x, attn_norm, w_in, mla_q_norm, w_uq, mla_kv_norm, w_ukv, conv_w, pool_w, pool_scale, swa_sinks, mix_norm, w_o, ffn_norm, w_gate_up, w_down, final_norm, loss_target, m_attn_norm, m_w_in, m_mla_q_norm, m_w_uq, m_mla_kv_norm, m_w_ukv, m_conv_w, m_pool_w, m_pool_scale, m_swa_sinks, m_mix_norm, m_w_o, m_ffn_norm, m_w_gate_up, m_w_down, m_final_norm, v_attn_norm, v_w_in, v_mla_q_norm, v_w_uq, v_mla_kv_norm, v_w_ukv, v_conv_w, v_pool_w, v_pool_scale, v_swa_sinks, v_mix_norm, v_w_o, v_ffn_norm, v_w_gate_up, v_w_down, v_final_norm):
    given = dict(locals())
    sh_names = [nm for nm, _, _ in SHARDED]
    sh_axis = {nm: ax - 1 for nm, _, ax in SHARDED}
    rep_names = [nm for nm, _ in REPLICATED]
    rep_shapes = [shp for _, shp in REPLICATED]
    rep = {nm: given[nm] for nm in rep_names if nm != "loss"}
    me = 4 * lax.axis_index("x") + 2 * lax.axis_index("y") + lax.axis_index("c")
    my_chip = 2 * lax.axis_index("x") + lax.axis_index("y")
    core = lax.axis_index("c").astype(jnp.int32).reshape(1)

    def behind(token, a):
        return a + token[0, 0].astype(a.dtype)

    def wire(nm, l):
        if nm == "conv_w":
            return lax.bitcast_convert_type(given[nm][l], BF16).reshape(3, -1)
        return given[nm][l].astype(BF16)

    def whole(nm, g):
        if nm == "conv_w":
            g = lax.bitcast_convert_type(g.reshape(N_DEV, 3, -1, 2), F32)
        return _shards_to_full(g, sh_axis[nm])

    def gather_start(names, l, after, tag):
        srcs = [wire(nm, l) for nm in names]
        return _start_copies(_plan_gather, N_DEV - 1, srcs, [(N_DEV,) + a.shape for a in srcs], after, name=f"start_gather_{tag}")

    def gather_wait(names, handle, after, tag):
        srcs, lands = _wait_copies(_plan_gather, handle, after, name=f"wait_gather_{tag}")
        return {nm: whole(nm, lax.dynamic_update_index_in_dim(land, src, me, 0)) for nm, src, land in zip(names, srcs, lands)}

    got = _all_gather([wire(nm, 0) for nm in MIXER_WEIGHTS], name="gather_mixer0")
    full_m0 = {nm: whole(nm, g) for nm, g in zip(MIXER_WEIGHTS, got)}
    h_f0, tok = gather_start(FFN_WEIGHTS, 0, [], "ffn0")
    h_l1, tok = gather_start(MIXER_WEIGHTS + FFN_WEIGHTS, 1, [tok], "layer1")

    xs, target = x[0], loss_target[0]
    s = xs.shape[0]
    tabs = _rope_tables(s)
    wm, wf, svm, svf = [None] * DEPTH, [None] * DEPTH, [None] * DEPTH, [None] * DEPTH
    wm[0] = _mixer_weights(full_m0, rep, 0)
    wm[0]["attn_norm"] = behind(tok, wm[0]["attn_norm"])
    x1, svm[0] = _mixer_fwd(xs, wm[0], tabs, 0)
    wf[0] = _ffn_weights(gather_wait(FFN_WEIGHTS, h_f0, [x1], "ffn0"), rep, 0)
    x2, svf[0] = _ffn_fwd(x1, wf[0], 0)
    full_1 = gather_wait(MIXER_WEIGHTS + FFN_WEIGHTS, h_l1, [x2], "layer1")
    wm[1], wf[1] = _mixer_weights(full_1, rep, 1), _ffn_weights(full_1, rep, 1)
    x1, svm[1] = _mixer_fwd(x2, wm[1], tabs, 1)
    x2, svf[1] = _ffn_fwd(x1, wf[1], 1)
    dx, d_final, loss = _loss_head(x2, target, rep["final_norm"][None], s=s)

    parts = {}

    def reduce_start(grads, l, after, tag):
        names = [nm for nm in sh_names if nm in grads]
        mine = [_full_to_shards(grads[nm], sh_axis[nm]) for nm in names]
        handle, token = _start_copies(_plan_sibling, 1, mine, [m.shape[1:] for m in mine], after, name=f"start_sibling_{tag}")
        return (names, l, handle), token

    def reduce_mid(state, after, tag):
        names, l, handle = state
        mine, theirs = _wait_copies(_plan_sibling, handle, after, name=f"wait_sibling_{tag}")
        sums = [_add_own(g, o, core, name=f"chip_sum_{nm}_{l}") for nm, g, o in zip(names, mine, theirs)]
        handle, token = _start_copies(_plan_chips, 3, sums, [a.shape for a in sums], [], name=f"start_chips_{tag}")
        return (names, l, handle), token

    def reduce_end(state, after, tag):
        names, l, handle = state
        sums, lands = _wait_copies(_plan_chips, handle, after, name=f"wait_chips_{tag}")
        for nm, own, land in zip(names, sums, lands):
            parts[nm, l] = lax.dynamic_update_index_in_dim(land, lax.dynamic_index_in_dim(own, my_chip, 0, keepdims=False),
                                                           my_chip, 0)

    small = [None] * DEPTH
    in_flight = []
    pending = None
    for l in reversed(range(DEPTH)):
        dgu, g_down = _ffn_bwd_down(dx, svf[l], wf[l], l)
        if pending is not None:
            state, token = reduce_mid(pending, [dgu], f"mixer{l + 1}")
            in_flight.append((state, f"mixer{l + 1}"))
            wf[l]["ffn_norm"] = behind(token, wf[l]["ffn_norm"])
        dx1, g_up = _ffn_bwd_up(dx, dgu, svf[l], wf[l], l)
        g_ffn = _fold_grads({**g_down, **g_up})
        state, token = reduce_start(g_ffn, l, [], f"ffn{l}")
        wm[l]["mix_norm"] = behind(token, wm[l]["mix_norm"])
        dys, g_out = _mixer_bwd_out(dx1, svm[l], wm[l], l)
        state, token = reduce_mid(state, [dys[0]], f"ffn{l}")
        in_flight.append((state, f"ffn{l}"))
        svm[l]["lse_a"] = behind(token, svm[l]["lse_a"])
        wm[l]["hp_swa"] = behind(token, wm[l]["hp_swa"])
        dx, g_in = _mixer_bwd_in(dx1, dys, svm[l], wm[l], tabs, l)
        g_mixer = _fold_grads({**g_out, **g_in})
        small[l] = {**g_ffn, **g_mixer}
        pending, token = reduce_start(g_mixer, l, [], f"mixer{l}")
    state, token = reduce_mid(pending, [dx], "mixer0")
    in_flight.append((state, "mixer0"))
    for state, tag in in_flight:
        reduce_end(state, [], tag)
    grad_x = dx

    sh_out = {nm: _adamw([parts[nm, l] for l in range(DEPTH)], given[nm], given["m_" + nm], given["v_" + nm],
                         name=f"adamw_{nm}") for nm in sh_names}

    grads = {nm: jnp.stack([small[l][nm] for l in range(DEPTH)]) for nm in rep_names if nm in small[0]}
    grads["final_norm"] = d_final[0]
    grads["loss"] = loss[0, :1]
    zero = jnp.zeros((1,), F32)
    small = _all_gather([_pack([grads[nm] for nm in rep_names])], name="gather_small_grads")
    packs = [_pack([given.get(pre + nm, zero) for nm in rep_names])[None] for pre in ("", "m_", "v_")]
    rep_out = [dict(zip(rep_names, _unpack(o[0], rep_shapes))) for o in _adamw(small, *packs, name="adamw_replicated")]

    out = [rep_out[0]["loss"][0], grad_x[None]]
    for i in range(4):
        out += [sh_out[nm][i] if nm in sh_axis else rep_out[i][nm] for nm in WEIGHT_ORDER]
    return tuple(out)
```

```python
import functools
import math

import numpy as np
import jax
import jax.numpy as jnp
from jax import lax
from jax.experimental import pallas as pl
from jax.experimental.pallas import tpu as pltpu

F32 = jnp.float32
BF16 = jnp.bfloat16

D_MODEL = 1024
DEPTH = 2
D_GROUP = 256
MLA_HEADS = 4
MLA_NOPE = 64
MLA_ROPE = 32
ROPE_THETA = 10000.0
POOL_WINDOWS = (2, 4, 8, 16)
SWA_HEADS = 4
SWA_KV_HEADS = 2
SWA_WINDOW = 128
D_FF = 2816
GU_TILE = 256
RMS_EPS = 1e-6
LANES = 128
HEAD = 64
VMEM_LIMIT = 48 * 1024 * 1024
NEG = -1e30

ADAM_LR = 0.001
ADAM_B1 = 0.9
ADAM_B2 = 0.999
ADAM_EPS = 1e-08
ADAM_WD = 0.01
ADAM_STEP = 10

N_DEV = 8
PACK_COLS = 1024

C_QSW, C_CQ, C_GB, C_GC, C_UCONV, C_UPOOL = (0, 512), (512, 256), (768, 256), (1024, 256), (1280, 256), (1536, 256)
C_KSW, C_VSW, C_CKV, C_KR, C_KRP = (1792, 256), (2048, 256), (2304, 128), (2432, 128), (2560, 128)
D_IN_EXT = 2688
D_MIX_EXT = 1536

SHARDED = (("w_in", (DEPTH, 1024, 244), 2), ("w_uq", (DEPTH, 256, 48), 2), ("w_ukv", (DEPTH, 128, 64), 2),
           ("conv_w", (DEPTH, 3, 32), 2), ("w_o", (DEPTH, 128, 1024), 1), ("w_gate_up", (DEPTH, 1024, 704), 2),
           ("w_down", (DEPTH, 352, 1024), 1))
REPLICATED = (("attn_norm", (DEPTH, 1024)), ("mla_q_norm", (DEPTH, 256)), ("mla_kv_norm", (DEPTH, 128)),
              ("pool_w", (DEPTH, 4, 64, 64)), ("pool_scale", (DEPTH, 256)), ("swa_sinks", (DEPTH, 4)),
              ("mix_norm", (DEPTH, 1024)), ("ffn_norm", (DEPTH, 1024)), ("final_norm", (1024,)), ("loss", (1,)))
WEIGHT_ORDER = ("attn_norm", "w_in", "mla_q_norm", "w_uq", "mla_kv_norm", "w_ukv", "conv_w", "pool_w", "pool_scale",
                "swa_sinks", "mix_norm", "w_o", "ffn_norm", "w_gate_up", "w_down", "final_norm")


def _params(sem):
    return pltpu.CompilerParams(dimension_semantics=sem, vmem_limit_bytes=VMEM_LIMIT)


def _pick(dim, target):
    if dim <= target:
        return dim
    best = None
    for t in range(LANES, target + 1, LANES):
        if dim % t == 0:
            best = t
    assert best is not None, (dim, target)
    return best


def _mm(a, b, *, name, ta=False, tb=False, res=None, out_dtype=F32, tm=1024, tn=1024, tk=1024, epilogue=None):
    m, k = (a.shape[1], a.shape[0]) if ta else a.shape
    n = b.shape[0] if tb else b.shape[1]
    assert (b.shape[1] if tb else b.shape[0]) == k
    tm, tn, tk = _pick(m, tm), _pick(n, tn), _pick(k, tk)
    nj, nk = n // tn, k // tk
    dims = (((0 if ta else 1,), (1 if tb else 0,)), ((), ()))
    fn, extra, outs = epilogue if epilogue is not None else (None, [], [(n, out_dtype)])
    if res is not None:
        assert epilogue is None
        fn, extra = (lambda acc, r: (acc + r,)), [res]
    n_in, n_out = 2 + len(extra), len(outs)

    def body(*refs):
        a_ref, b_ref, acc_ref = refs[0], refs[1], refs[-1]
        kk = pl.program_id(2)

        @pl.when(kk == 0)
        def _():
            acc_ref[...] = jnp.zeros_like(acc_ref)

        acc_ref[...] += lax.dot_general(a_ref[...].astype(BF16), b_ref[...].astype(BF16), dims,
                                        preferred_element_type=F32)

        @pl.when(kk == nk - 1)
        def _():
            tiles = (acc_ref[...],) if fn is None else fn(acc_ref[...], *[r[...] for r in refs[2:n_in]])
            for o_ref, tile in zip(refs[n_in:n_in + n_out], tiles):
                o_ref[...] = tile.astype(o_ref.dtype)

    def col_tiles(width):
        assert width % (nj * LANES) == 0, (width, nj)
        return pl.BlockSpec((tm, width // nj), lambda i, j, kk: (i, j))

    a_spec = pl.BlockSpec((tk, tm), lambda i, j, kk: (kk, i)) if ta else pl.BlockSpec((tm, tk), lambda i, j, kk: (i, kk))
    b_spec = pl.BlockSpec((tn, tk), lambda i, j, kk: (j, kk)) if tb else pl.BlockSpec((tk, tn), lambda i, j, kk: (kk, j))
    res_ = pl.pallas_call(
        body, name=name, grid=(m // tm, nj, nk), in_specs=[a_spec, b_spec] + [col_tiles(e.shape[1]) for e in extra],
        out_specs=[col_tiles(w) for w, _ in outs],
        out_shape=[jax.ShapeDtypeStruct((m, w), dt) for w, dt in outs],
        scratch_shapes=[pltpu.VMEM((tm, tn), F32)],
        compiler_params=_params(("parallel", "parallel", "arbitrary")),
    )(a, b, *extra)
    return res_[0] if epilogue is None else res_


def _rowwise(fn, *, name, rows, ins, outs, reds=(), tm=512):
    tm = min(tm, rows)
    assert rows % tm == 0
    n_in, n_out = len(ins), len(outs)

    def body(*refs):
        vals = [r[...] for r in refs[:n_in]]
        res = fn(*vals)
        for r, v in zip(refs[n_in:n_in + n_out], res[:n_out]):
            r[...] = v.astype(r.dtype)
        if reds:
            @pl.when(pl.program_id(0) == 0)
            def _():
                for r in refs[n_in + n_out:]:
                    r[...] = jnp.zeros_like(r)

            for r, v in zip(refs[n_in + n_out:], res[n_out:]):
                r[...] += v

    in_specs, args = [], []
    for spec in ins:
        if spec[0] == "row":
            _, arr, width, blk = spec
            in_specs.append(pl.BlockSpec((tm, width), functools.partial(lambda i, blk: (i, blk), blk=blk)))
        else:
            arr = spec[1]
            in_specs.append(pl.BlockSpec(arr.shape, functools.partial(lambda i, nd: (0,) * nd, nd=arr.ndim)))
        args.append(arr)
    out_specs = [pl.BlockSpec((tm, w), lambda i: (i, 0)) for w, _ in outs]
    out_shape = [jax.ShapeDtypeStruct((rows, w), dt) for w, dt in outs]
    out_specs += [pl.BlockSpec((r, w), lambda i: (0, 0)) for r, w in reds]
    out_shape += [jax.ShapeDtypeStruct((r, w), F32) for r, w in reds]
    return pl.pallas_call(body, name=name, grid=(rows // tm,), in_specs=in_specs, out_specs=out_specs,
                          out_shape=out_shape, compiler_params=_params(("arbitrary",)))(*args)


def _rstd(x, n):
    return lax.rsqrt(jnp.sum(x * x, axis=-1, keepdims=True) * (1.0 / n) + RMS_EPS)


def _rms_fwd(x_spec, g, *, name, rows, width):
    def fn(x, gv):
        return (x * _rstd(x, width) * gv,)
    return _rowwise(fn, name=name, rows=rows, ins=[x_spec, ("full", g)], outs=[(width, BF16)])[0]


def _rms_bwd(x_spec, g, dy, res, *, name, rows, width, out_dtypes):
    def fn(x, gv, dyv, *rest):
        r = _rstd(x, width)
        dyg = dyv * gv
        dx = r * dyg - x * (r * r * r) * (jnp.sum(dyg * x, axis=-1, keepdims=True) * (1.0 / width))
        if rest:
            dx = dx + rest[0]
        return (dx,) * len(out_dtypes) + (jnp.sum(dyv * x * r, axis=0, keepdims=True),)

    ins = [x_spec, ("full", g), ("row", dy, width, 0)]
    if res is not None:
        ins.append(("row", res, width, 0))
    return _rowwise(fn, name=name, rows=rows, ins=ins, outs=[(width, dt) for dt in out_dtypes], reds=[(1, width)])


NT_DIMS = (((1,), (1,)), ((), ()))
TN_DIMS = (((0,), (0,)), ((), ()))


def _mla_tile(s):
    return min(512, s)


def _causal(shape, query_axis):
    return lax.broadcasted_iota(jnp.int32, shape, query_axis) >= lax.broadcasted_iota(jnp.int32, shape, 1 - query_axis)


def _mla_fwd(qa, ka, va, *, name, s, nh):
    t = _mla_tile(s)
    nq = s // t

    def body(q_ref, k_ref, v_ref, o_ref, lse_ref, m_s, acc_s):
        i, j = pl.program_id(0), pl.program_id(1)

        @pl.when(j == 0)
        def _():
            m_s[...] = jnp.full_like(m_s, NEG)
            acc_s[...] = jnp.zeros_like(acc_s)

        def step(diag):
            for h in range(nh):
                cols = pl.ds(h * LANES, LANES)
                sc = lax.dot_general(q_ref[:, cols], k_ref[:, cols], NT_DIMS, preferred_element_type=F32)
                if diag:
                    sc = jnp.where(_causal(sc.shape, 0), sc, NEG)
                m_prev = m_s[h]
                m_new = jnp.maximum(m_prev, jnp.max(sc, axis=1, keepdims=True))
                p = jnp.exp(sc - m_new).astype(BF16)
                acc_s[:, cols] = (jnp.exp(m_prev - m_new) * acc_s[:, cols]
                                  + jnp.dot(p, v_ref[:, cols], preferred_element_type=F32))
                m_s[h] = m_new

        pl.when(j < i)(functools.partial(step, False))
        pl.when(j == i)(functools.partial(step, True))

        @pl.when(j == nq - 1)
        def _():
            lane = lax.broadcasted_iota(jnp.int32, (t, LANES), 1)
            for h in range(nh):
                cols = pl.ds(h * LANES, LANES)
                acc = acc_s[:, cols]
                l = jnp.sum(jnp.where(lane == HEAD, acc, 0.0), axis=1, keepdims=True)
                o_ref[:, cols] = jnp.where(lane < HEAD, acc / l, 0.0)
                lse_ref[h] = m_s[h] + jnp.log(l)

    q_spec = pl.BlockSpec((t, nh * LANES), lambda i, j: (i, 0))
    kv_spec = pl.BlockSpec((t, nh * LANES), lambda i, j: (jnp.minimum(j, i), 0))
    return pl.pallas_call(
        body, name=name, grid=(nq, nq), in_specs=[q_spec, kv_spec, kv_spec],
        out_specs=[q_spec, pl.BlockSpec((nh, t, 1), lambda i, j: (0, i, 0))],
        out_shape=[jax.ShapeDtypeStruct((s, nh * LANES), F32), jax.ShapeDtypeStruct((nh, s, 1), F32)],
        scratch_shapes=[pltpu.VMEM((nh, t, 1), F32), pltpu.VMEM((t, nh * LANES), F32)],
        compiler_params=_params(("parallel", "arbitrary")),
    )(qa, ka, va)


def _mla_dq(qa, ka, va, o, do, lse, after, *, name, s, nh):
    t = _mla_tile(s)
    nq = s // t

    def body(q_ref, k_ref, v_ref, o_ref, do_ref, lse_ref, dq_ref, d_ref, acc_s):
        i, j = pl.program_id(0), pl.program_id(1)

        @pl.when(j == 0)
        def _():
            acc_s[...] = jnp.zeros_like(acc_s)
            for h in range(nh):
                cols = pl.ds(h * LANES, LANES)
                d_ref[h] = jnp.sum(do_ref[:, cols] * o_ref[:, cols], axis=1, keepdims=True)

        def step(diag):
            for h in range(nh):
                cols = pl.ds(h * LANES, LANES)
                k = k_ref[:, cols]
                sc = lax.dot_general(q_ref[:, cols], k, NT_DIMS, preferred_element_type=F32)
                if diag:
                    sc = jnp.where(_causal(sc.shape, 0), sc, NEG)
                p = jnp.exp(sc - lse_ref[h])
                dp = lax.dot_general(do_ref[:, cols].astype(BF16), v_ref[:, cols], NT_DIMS, preferred_element_type=F32)
                ds = p * (dp - d_ref[h])
                acc_s[:, cols] += jnp.dot(ds.astype(BF16), k, preferred_element_type=F32)

        pl.when(j < i)(functools.partial(step, False))
        pl.when(j == i)(functools.partial(step, True))

        @pl.when(j == nq - 1)
        def _():
            dq_ref[...] = acc_s[...]

    q_spec = pl.BlockSpec((t, nh * LANES), lambda i, j: (i, 0))
    kv_spec = pl.BlockSpec((t, nh * LANES), lambda i, j: (jnp.minimum(j, i), 0))
    vec_spec = pl.BlockSpec((nh, t, 1), lambda i, j: (0, i, 0))
    def body_after(q_ref, k_ref, v_ref, o_ref, do_ref, lse_ref, after_ref, *rest):
        body(q_ref, k_ref, v_ref, o_ref, do_ref, lse_ref, *rest)

    return pl.pallas_call(
        body_after, name=name, grid=(nq, nq),
        in_specs=[q_spec, kv_spec, kv_spec, q_spec, q_spec, vec_spec, pl.BlockSpec(memory_space=pl.ANY)],
        out_specs=[q_spec, vec_spec],
        out_shape=[jax.ShapeDtypeStruct((s, nh * LANES), F32), jax.ShapeDtypeStruct((nh, s, 1), F32)],
        scratch_shapes=[pltpu.VMEM((t, nh * LANES), F32)],
        compiler_params=_params(("parallel", "arbitrary")),
    )(qa, ka, va, o, do, lse, after)


def _mla_dkv(qa, ka, va, do, lse_row, d_row, *, name, s, nh):
    t = _mla_tile(s)
    nq = s // t

    def body(q_ref, k_ref, v_ref, do_ref, lse_ref, d_ref, dk_ref, dv_ref, dk_s, dv_s):
        kb, j = pl.program_id(0), pl.program_id(1)

        @pl.when(j == 0)
        def _():
            dk_s[...] = jnp.zeros_like(dk_s)
            dv_s[...] = jnp.zeros_like(dv_s)

        def step(diag):
            for h in range(nh):
                cols = pl.ds(h * LANES, LANES)
                q = q_ref[:, cols]
                do_b = do_ref[:, cols].astype(BF16)
                sc = lax.dot_general(k_ref[:, cols], q, NT_DIMS, preferred_element_type=F32)
                if diag:
                    sc = jnp.where(_causal(sc.shape, 1), sc, NEG)
                p = jnp.exp(sc - lse_ref[h])
                dv_s[:, cols] += jnp.dot(p.astype(BF16), do_b, preferred_element_type=F32)
                dp = lax.dot_general(v_ref[:, cols], do_b, NT_DIMS, preferred_element_type=F32)
                ds = p * (dp - d_ref[h])
                dk_s[:, cols] += jnp.dot(ds.astype(BF16), q, preferred_element_type=F32)

        pl.when(j > kb)(functools.partial(step, False))
        pl.when(j == kb)(functools.partial(step, True))

        @pl.when(j == nq - 1)
        def _():
            dk_ref[...] = dk_s[...]
            dv_ref[...] = dv_s[...].astype(dv_ref.dtype)

    q_spec = pl.BlockSpec((t, nh * LANES), lambda kb, j: (jnp.maximum(j, kb), 0))
    kv_spec = pl.BlockSpec((t, nh * LANES), lambda kb, j: (kb, 0))
    row_spec = pl.BlockSpec((nh, 1, t), lambda kb, j: (0, 0, jnp.maximum(j, kb)))
    return pl.pallas_call(
        body, name=name, grid=(nq, nq), in_specs=[q_spec, kv_spec, kv_spec, q_spec, row_spec, row_spec],
        out_specs=[kv_spec, kv_spec],
        out_shape=[jax.ShapeDtypeStruct((s, nh * LANES), F32), jax.ShapeDtypeStruct((s, nh * LANES), BF16)],
        scratch_shapes=[pltpu.VMEM((t, nh * LANES), F32), pltpu.VMEM((t, nh * LANES), F32)],
        compiler_params=_params(("parallel", "arbitrary")),
    )(qa, ka, va, do, lse_row, d_row)


SWA_PIECE = 128
SWA_KEYS = 2 * SWA_PIECE


def _swa_block(s):
    return min(512, s)


def _swa_piece(hp_ref, h, q, k_ref, v_ref, qpos0, scale):
    kstart = pl.multiple_of(jnp.maximum(qpos0 - SWA_PIECE, 0), SWA_PIECE)
    k = k_ref[pl.ds(kstart, SWA_KEYS), :].astype(BF16)
    v = v_ref[pl.ds(kstart, SWA_KEYS), :].astype(BF16)
    sc = lax.dot_general(q, k, NT_DIMS, preferred_element_type=F32)
    dist = (qpos0 + lax.broadcasted_iota(jnp.int32, sc.shape, 0)) - (kstart + lax.broadcasted_iota(jnp.int32, sc.shape, 1))
    sc = sc * scale - hp_ref[h, 0] * dist.astype(F32)
    sc = jnp.where((dist >= 0) & (dist < SWA_WINDOW), sc, NEG)
    return kstart, k, v, sc


def _swa_fwd(proj, hp, *, name, s, scale):
    tb = _swa_block(s)
    group = SWA_HEADS // SWA_KV_HEADS
    q_off, k_off, v_off = C_QSW[0] // LANES, C_KSW[0] // LANES, C_VSW[0] // LANES

    def body(hp_ref, q_ref, k_ref, v_ref, o_ref, lse_ref):
        h, i = pl.program_id(0), pl.program_id(1)
        sink = hp_ref[h, 1]
        for r in range(0, tb, SWA_PIECE):
            rows = pl.ds(r, SWA_PIECE)
            _, _, v, sc = _swa_piece(hp_ref, h, q_ref[rows, :].astype(BF16), k_ref, v_ref, i * tb + r, scale)
            m = jnp.maximum(jnp.max(sc, axis=1, keepdims=True), sink)
            p = jnp.exp(sc - m)
            l = jnp.sum(p, axis=1, keepdims=True) + jnp.exp(sink - m)
            o_ref[rows, :] = jnp.dot(p.astype(BF16), v, preferred_element_type=F32) / l
            lse_ref[rows, :] = m + jnp.log(l)

    whole = lambda off: pl.BlockSpec((s, LANES), lambda h, i: (0, off + h // group))
    return pl.pallas_call(
        body, name=name, grid=(SWA_HEADS, s // tb),
        in_specs=[pl.BlockSpec(memory_space=pltpu.SMEM), pl.BlockSpec((tb, LANES), lambda h, i: (i, q_off + h)),
                  whole(k_off), whole(v_off)],
        out_specs=[pl.BlockSpec((tb, LANES), lambda h, i: (i, h)), pl.BlockSpec((None, tb, 1), lambda h, i: (h, i, 0))],
        out_shape=[jax.ShapeDtypeStruct((s, SWA_HEADS * LANES), F32), jax.ShapeDtypeStruct((SWA_HEADS, s, 1), F32)],
        compiler_params=_params(("parallel", "parallel")),
    )(hp, proj, proj, proj)


def _swa_bwd(proj, o, do, lse, hp, *, name, s, scale):
    tb = _swa_block(s)
    nqb = s // tb
    group = SWA_HEADS // SWA_KV_HEADS
    q_off, k_off, v_off = C_QSW[0] // LANES, C_KSW[0] // LANES, C_VSW[0] // LANES

    def body(hp_ref, q_ref, k_ref, v_ref, o_ref, do_ref, lse_ref, dq_ref, dk_ref, dv_ref, dsink_ref):
        kh, g, i = pl.program_id(0), pl.program_id(1), pl.program_id(2)
        h = kh * group + g
        sink = hp_ref[h, 1]

        @pl.when((g == 0) & (i == 0))
        def _():
            dk_ref[...] = jnp.zeros_like(dk_ref)
            dv_ref[...] = jnp.zeros_like(dv_ref)

        @pl.when(i == 0)
        def _():
            dsink_ref[...] = jnp.zeros_like(dsink_ref)

        for r in range(0, tb, SWA_PIECE):
            rows = pl.ds(r, SWA_PIECE)
            q = q_ref[rows, :].astype(BF16)
            dov = do_ref[rows, :]
            do_b = dov.astype(BF16)
            lse_r = lse_ref[rows, :]
            d_r = jnp.sum(dov * o_ref[rows, :], axis=1, keepdims=True)
            kstart, k, v, sc = _swa_piece(hp_ref, h, q, k_ref, v_ref, i * tb + r, scale)
            p = jnp.exp(sc - lse_r)
            dp = lax.dot_general(do_b, v, NT_DIMS, preferred_element_type=F32)
            ds = (p * (dp - d_r)).astype(BF16)
            dq_ref[rows, :] = (jnp.dot(ds, k, preferred_element_type=F32) * scale).astype(dq_ref.dtype)
            win = pl.ds(kstart, SWA_KEYS)
            dk_ref[win, :] += lax.dot_general(ds, q, TN_DIMS, preferred_element_type=F32) * scale
            dv_ref[win, :] += lax.dot_general(p.astype(BF16), do_b, TN_DIMS, preferred_element_type=F32)
            part = jnp.sum(-jnp.exp(sink - lse_r) * d_r, axis=0, keepdims=True)
            dsink_ref[...] += jnp.broadcast_to(part, (1, LANES))

    whole = lambda off: pl.BlockSpec((s, LANES), lambda kh, g, i: (0, off + kh))
    q_map = lambda kh, g, i: (i, kh * group + g)
    return pl.pallas_call(
        body, name=name, grid=(SWA_KV_HEADS, group, nqb),
        in_specs=[pl.BlockSpec(memory_space=pltpu.SMEM),
                  pl.BlockSpec((tb, LANES), lambda kh, g, i: (i, q_off + kh * group + g)), whole(k_off), whole(v_off),
                  pl.BlockSpec((tb, LANES), q_map), pl.BlockSpec((tb, LANES), q_map),
                  pl.BlockSpec((None, tb, 1), lambda kh, g, i: (kh * group + g, i, 0))],
        out_specs=[pl.BlockSpec((tb, LANES), q_map), whole(0), whole(0),
                   pl.BlockSpec((None, 1, LANES), lambda kh, g, i: (kh * group + g, 0, 0))],
        out_shape=[jax.ShapeDtypeStruct((s, SWA_HEADS * LANES), BF16),
                   jax.ShapeDtypeStruct((s, SWA_KV_HEADS * LANES), F32), jax.ShapeDtypeStruct((s, SWA_KV_HEADS * LANES), F32),
                   jax.ShapeDtypeStruct((SWA_HEADS, 1, LANES), F32)],
        compiler_params=_params(("parallel", "arbitrary", "arbitrary")),
    )(hp, proj, proj, proj, o, do, lse)


def _shift_down(z, k):
    rows = lax.broadcasted_iota(jnp.int32, z.shape, 0)
    return jnp.where(rows >= k, pltpu.roll(z, k, 0), 0.0)


def _shift_up(z, k):
    n = z.shape[0]
    rows = lax.broadcasted_iota(jnp.int32, z.shape, 0)
    return jnp.where(rows < n - k, pltpu.roll(z, n - k, 0), 0.0)


def _rows3(a, b, c):
    r = lax.broadcasted_iota(jnp.int32, (3, a.shape[1]), 0)
    return jnp.where(r == 0, a, jnp.where(r == 1, b, c))


def _col_spec(s, off):
    return pl.BlockSpec((s, LANES), functools.partial(lambda j, off: (0, off + j), off=off))


def _conv_fwd(proj, conv_w, *, name, s):
    def body(gb_ref, gc_ref, u_ref, w_ref, y_ref):
        w0, w1, w2 = w_ref[0:1, :], w_ref[1:2, :], w_ref[2:3, :]
        z = gc_ref[...] * u_ref[...]
        c = w2 * z + w1 * _shift_down(z, 1) + w0 * _shift_down(z, 2)
        y_ref[...] = gb_ref[...] * c

    return pl.pallas_call(
        body, name=name, grid=(2,),
        in_specs=[_col_spec(s, C_GB[0] // LANES), _col_spec(s, C_GC[0] // LANES), _col_spec(s, C_UCONV[0] // LANES),
                  pl.BlockSpec((3, LANES), lambda j: (0, j))],
        out_specs=_col_spec(s, 0), out_shape=jax.ShapeDtypeStruct((s, D_GROUP), F32),
        compiler_params=_params(("parallel",)),
    )(proj, proj, proj, conv_w)


def _conv_bwd(dy, proj, conv_w, *, name, s):
    def body(dy_ref, gb_ref, gc_ref, u_ref, w_ref, dgb_ref, dgc_ref, du_ref, dw_ref):
        w0, w1, w2 = w_ref[0:1, :], w_ref[1:2, :], w_ref[2:3, :]
        gc, u, dyv = gc_ref[...], u_ref[...], dy_ref[...]
        z = gc * u
        z1, z2 = _shift_down(z, 1), _shift_down(z, 2)
        c = w2 * z + w1 * z1 + w0 * z2
        dgb_ref[...] = (dyv * c).astype(dgb_ref.dtype)
        dc = dyv * gb_ref[...]
        dz = w2 * dc + w1 * _shift_up(dc, 1) + w0 * _shift_up(dc, 2)
        dgc_ref[...] = (dz * u).astype(dgc_ref.dtype)
        du_ref[...] = (dz * gc).astype(du_ref.dtype)
        dw_ref[...] = _rows3(jnp.sum(dc * z2, axis=0, keepdims=True), jnp.sum(dc * z1, axis=0, keepdims=True),
                             jnp.sum(dc * z, axis=0, keepdims=True))

    act = jax.ShapeDtypeStruct((s, D_GROUP), BF16)
    return pl.pallas_call(
        body, name=name, grid=(2,),
        in_specs=[_col_spec(s, 0), _col_spec(s, C_GB[0] // LANES), _col_spec(s, C_GC[0] // LANES),
                  _col_spec(s, C_UCONV[0] // LANES), pl.BlockSpec((3, LANES), lambda j: (0, j))],
        out_specs=[_col_spec(s, 0), _col_spec(s, 0), _col_spec(s, 0), pl.BlockSpec((3, LANES), lambda j: (0, j))],
        out_shape=[act, act, act, jax.ShapeDtypeStruct((3, D_GROUP), F32)],
        compiler_params=_params(("parallel",)),
    )(dy, proj, proj, proj, conv_w)


def _pool_select(j, lane, a2, a4, a8, a16):
    lo = lane < HEAD
    return jnp.where(j == 0, jnp.where(lo, a2, a4), jnp.where(lo, a8, a16))


def _pooled(u, j):
    s2 = u + _shift_down(u, 1)
    s4 = s2 + _shift_down(s2, 2)
    s8 = s4 + _shift_down(s4, 4)
    s16 = s8 + _shift_down(s8, 8)
    lane = lax.broadcasted_iota(jnp.int32, u.shape, 1)
    rows = lax.broadcasted_iota(jnp.int32, u.shape, 0)
    win = _pool_select(j, lane, 2, 4, 8, 16)
    count = jnp.minimum(rows + 1, win).astype(F32)
    return _pool_select(j, lane, s2, s4, s8, s16) / count - u, count


def _pool_fwd(proj, wbd, scale, *, name, s):
    def body(u_ref, w_ref, sc_ref, y_ref):
        pooled, _ = _pooled(u_ref[...], pl.program_id(0))
        y_ref[...] = jnp.dot(pooled.astype(BF16), w_ref[...].astype(BF16), preferred_element_type=F32) * sc_ref[...]

    return pl.pallas_call(
        body, name=name, grid=(2,),
        in_specs=[_col_spec(s, C_UPOOL[0] // LANES), pl.BlockSpec((None, LANES, LANES), lambda j: (j, 0, 0)),
                  pl.BlockSpec((1, LANES), lambda j: (0, j))],
        out_specs=_col_spec(s, 0), out_shape=jax.ShapeDtypeStruct((s, D_GROUP), F32),
        compiler_params=_params(("parallel",)),
    )(proj, wbd, scale)


def _pool_bwd(dy, proj, wbd, scale, *, name, s):
    def body(dy_ref, u_ref, w_ref, sc_ref, du_ref, dw_ref, dsc_ref):
        j = pl.program_id(0)
        pooled, count = _pooled(u_ref[...], j)
        pooled_b = pooled.astype(BF16)
        w_b = w_ref[...].astype(BF16)
        dyv = dy_ref[...]
        mixed = jnp.dot(pooled_b, w_b, preferred_element_type=F32)
        dsc_ref[...] = jnp.sum(dyv * mixed, axis=0, keepdims=True)
        dms = (dyv * sc_ref[...]).astype(BF16)
        dw_ref[...] = lax.dot_general(pooled_b, dms, (((0,), (0,)), ((), ())), preferred_element_type=F32)
        dpooled = lax.dot_general(dms, w_b, (((1,), (1,)), ((), ())), preferred_element_type=F32)
        r = dpooled / count
        a2 = r + _shift_up(r, 1)
        a4 = a2 + _shift_up(a2, 2)
        a8 = a4 + _shift_up(a4, 4)
        a16 = a8 + _shift_up(a8, 8)
        lane = lax.broadcasted_iota(jnp.int32, r.shape, 1)
        du_ref[...] = (_pool_select(j, lane, a2, a4, a8, a16) - dpooled).astype(du_ref.dtype)

    return pl.pallas_call(
        body, name=name, grid=(2,),
        in_specs=[_col_spec(s, 0), _col_spec(s, C_UPOOL[0] // LANES),
                  pl.BlockSpec((None, LANES, LANES), lambda j: (j, 0, 0)), pl.BlockSpec((1, LANES), lambda j: (0, j))],
        out_specs=[_col_spec(s, 0), pl.BlockSpec((None, LANES, LANES), lambda j: (j, 0, 0)),
                   pl.BlockSpec((1, LANES), lambda j: (0, j))],
        out_shape=[jax.ShapeDtypeStruct((s, D_GROUP), BF16), jax.ShapeDtypeStruct((2, LANES, LANES), F32),
                   jax.ShapeDtypeStruct((1, D_GROUP), F32)],
        compiler_params=_params(("parallel",)),
    )(dy, proj, wbd, scale)


def _mesh_pos():
    return lax.axis_index("x"), lax.axis_index("y"), lax.axis_index("c")


def _any_specs(n):
    return [pl.BlockSpec(memory_space=pl.ANY)] * n


def _all_gather(xs, *, name):
    n = len(xs)

    def body(*refs):
        x_refs, out_refs = refs[:n], refs[n:2 * n]
        send_sems, recv_sems, local_sems = refs[2 * n:]
        x, y, cc = _mesh_pos()
        me, sibling = (x, y, cc), (x, y, 1 - cc)
        chips = [(1 - x, y), (x, 1 - y), (1 - x, 1 - y)]

        def slot(a, px, py, pc):
            return out_refs[a].at[4 * px + 2 * py + pc]

        def copy(a, k, block, to, src=None):
            return pltpu.make_async_remote_copy(
                src_ref=slot(a, *block) if src is None else src, dst_ref=slot(a, *block), send_sem=send_sems.at[a, k],
                recv_sem=recv_sems.at[a, k], device_id=to, device_id_type=pl.DeviceIdType.MESH)

        mine = [pltpu.make_async_copy(x_refs[a], slot(a, *me), local_sems.at[a]) for a in range(n)]
        first = []
        for a in range(n):
            first.append(copy(a, 0, me, sibling, src=x_refs[a]))
            first += [copy(a, 1 + j, me, (*chip, cc), src=x_refs[a]) for j, chip in enumerate(chips)]
        for cp in mine + first:
            cp.start()
        passed = []
        for j, chip in enumerate(chips):
            for a in range(n):
                copy(a, 1 + j, (*chip, cc), me).wait_recv()
                passed.append(copy(a, 4 + j, (*chip, cc), sibling))
                passed[-1].start()
        for a in range(n):
            copy(a, 0, sibling, me).wait_recv()
        for j, chip in enumerate(chips):
            for a in range(n):
                copy(a, 4 + j, (*chip, 1 - cc), me).wait_recv()
        for cp in first + passed:
            cp.wait_send()
        for cp in mine:
            cp.wait()

    return pl.pallas_call(
        body, name=name, out_shape=[jax.ShapeDtypeStruct((N_DEV,) + a.shape, a.dtype) for a in xs],
        in_specs=_any_specs(n), out_specs=_any_specs(n),
        scratch_shapes=[pltpu.SemaphoreType.DMA((n, 7)), pltpu.SemaphoreType.DMA((n, 7)), pltpu.SemaphoreType.DMA((n,))],
    )(*xs)


def _plan_gather(src_refs, land_refs, send_sems, recv_sems):
    x, y, cc = _mesh_pos()
    me = 4 * x + 2 * y + cc
    plan = []
    for a, (src, land) in enumerate(zip(src_refs, land_refs)):
        for k in range(1, N_DEV):
            px, py, pc = (1 - x if k & 4 else x), (1 - y if k & 2 else y), (1 - cc if k & 1 else cc)
            i = a * (N_DEV - 1) + k - 1
            sems = dict(send_sem=send_sems.at[i], recv_sem=recv_sems.at[i], device_id=(px, py, pc),
                        device_id_type=pl.DeviceIdType.MESH)
            plan.append((pltpu.make_async_remote_copy(src_ref=src, dst_ref=land.at[me], **sems),
                         pltpu.make_async_remote_copy(src_ref=src, dst_ref=land.at[4 * px + 2 * py + pc], **sems)))
    return plan


def _plan_sibling(src_refs, land_refs, send_sems, recv_sems):
    x, y, cc = _mesh_pos()
    plan = []
    for a, (src, land) in enumerate(zip(src_refs, land_refs)):
        cp = pltpu.make_async_remote_copy(
            src_ref=src.at[1 - cc], dst_ref=land, send_sem=send_sems.at[a], recv_sem=recv_sems.at[a],
            device_id=(x, y, 1 - cc), device_id_type=pl.DeviceIdType.MESH)
        plan.append((cp, cp))
    return plan


def _plan_chips(src_refs, land_refs, send_sems, recv_sems):
    x, y, cc = _mesh_pos()
    my_chip = 2 * x + y
    plan = []
    for a, (src, land) in enumerate(zip(src_refs, land_refs)):
        for j, (px, py) in enumerate([(1 - x, y), (x, 1 - y), (1 - x, 1 - y)]):
            peer = 2 * px + py
            sems = dict(send_sem=send_sems.at[3 * a + j], recv_sem=recv_sems.at[3 * a + j], device_id=(px, py, cc),
                        device_id_type=pl.DeviceIdType.MESH)
            plan.append((pltpu.make_async_remote_copy(src_ref=src.at[peer], dst_ref=land.at[my_chip], **sems),
                         pltpu.make_async_remote_copy(src_ref=src.at[peer], dst_ref=land.at[peer], **sems)))
    return plan


HBM_SPEC = pl.BlockSpec(memory_space=pltpu.HBM)
SEM_SPEC = pl.BlockSpec(memory_space=pltpu.SEMAPHORE)
ANY_SPEC = pl.BlockSpec(memory_space=pl.ANY)
SIDE_EFFECT = pltpu.CompilerParams(has_side_effects=pltpu.SideEffectType.DATAFLOW_SIDE_EFFECTING)


def _start_copies(plan, sems_per_array, srcs, land_shapes, after, *, name):
    n, n_after = len(srcs), len(after)

    def body(*refs):
        send_sems, recv_sems = refs[2 * n + n_after], refs[2 * n + n_after + 1]
        for out, _ in plan(refs[:n], refs[n:2 * n], send_sems, recv_sems):
            out.start()
        refs[-1][...] = jnp.zeros_like(refs[-1])

    lands = [lax.empty(shp, a.dtype) for shp, a in zip(land_shapes, srcs)]
    sem = pltpu.SemaphoreType.DMA((n * sems_per_array,))
    res = pl.pallas_call(
        body, name=name,
        out_shape=(sem, sem, *[pltpu.HBM(a.shape, a.dtype) for a in srcs + lands], jax.ShapeDtypeStruct((8, LANES), F32)),
        in_specs=[HBM_SPEC] * (2 * n) + [ANY_SPEC] * n_after,
        out_specs=(SEM_SPEC, SEM_SPEC, *[HBM_SPEC] * (2 * n), pl.BlockSpec(memory_space=pltpu.VMEM)),
        input_output_aliases={i: 2 + i for i in range(2 * n)}, compiler_params=SIDE_EFFECT,
    )(*[pltpu.with_memory_space_constraint(a, pltpu.HBM) for a in srcs + lands], *after)
    return (res[0], res[1], list(res[2:2 + n]), list(res[2 + n:2 + 2 * n])), res[-1]


def _wait_copies(plan, handle, after, *, name):
    send, recv, srcs, lands = handle
    n = len(srcs)

    def body(*refs):
        for out, inc in plan(refs[:n], refs[n:2 * n], refs[2 * n], refs[2 * n + 1]):
            out.wait_send()
            inc.wait_recv()

    res = pl.pallas_call(
        body, name=name, out_shape=tuple(pltpu.HBM(a.shape, a.dtype) for a in srcs + lands),
        in_specs=[HBM_SPEC] * (2 * n) + [SEM_SPEC, SEM_SPEC] + [ANY_SPEC] * len(after), out_specs=[HBM_SPEC] * (2 * n),
        input_output_aliases={i: i for i in range(2 * n)}, compiler_params=SIDE_EFFECT,
    )(*srcs, *lands, send, recv, *after)
    return list(res[:n]), list(res[n:])


def _row_tile(rows, target=512):
    if rows <= target:
        return rows
    best = None
    for t in range(8, target + 1, 8):
        if rows % t == 0:
            best = t
    assert best is not None, (rows, target)
    return best


def _add_own(g, other, core, *, name):
    _, _, rows, cols = g.shape
    tm = _row_tile(rows)

    def body(c_ref, g_ref, o_ref, out_ref):
        out_ref[...] = g_ref[...] + o_ref[...]

    return pl.pallas_call(
        body, name=name, out_shape=jax.ShapeDtypeStruct(other.shape, other.dtype),
        grid_spec=pltpu.PrefetchScalarGridSpec(
            num_scalar_prefetch=1, grid=(4, rows // tm),
            in_specs=[pl.BlockSpec((None, None, tm, cols), lambda p, i, c_ref: (c_ref[0], p, i, 0)),
                      pl.BlockSpec((None, tm, cols), lambda p, i, c_ref: (p, i, 0))],
            out_specs=pl.BlockSpec((None, tm, cols), lambda p, i, c_ref: (p, i, 0))),
        compiler_params=_params(("parallel", "parallel")),
    )(core, g, other)


def _adamw(parts, w, m, v, *, name):
    layers, rows, cols = w.shape
    assert len(parts) == layers
    tm = _row_tile(rows, 256)
    nr = rows // tm

    def body(*refs):
        p_refs = refs[:layers]
        w_ref, m_ref, v_ref, g_ref, d_ref, nm_ref, nv_ref, g_s = refs[layers:]
        for ll in range(layers):
            @pl.when(pl.program_id(0) == ll)
            def _(ll=ll):
                g = p_refs[ll][0]
                for q in range(1, p_refs[ll].shape[0]):
                    g = g + p_refs[ll][q]
                g_s[...] = g

        g = g_s[...]
        mm = ADAM_B1 * m_ref[...] + (1.0 - ADAM_B1) * g
        vv = ADAM_B2 * v_ref[...] + (1.0 - ADAM_B2) * jnp.square(g)
        m_hat = mm / (1.0 - ADAM_B1 ** ADAM_STEP)
        v_hat = vv / (1.0 - ADAM_B2 ** ADAM_STEP)
        g_ref[...] = g
        d_ref[...] = -ADAM_LR * (m_hat / (jnp.sqrt(v_hat) + ADAM_EPS) + ADAM_WD * w_ref[...])
        nm_ref[...] = mm
        nv_ref[...] = vv

    def part_spec(ll, p):
        return pl.BlockSpec((p, tm, cols), lambda l, i: (0, jnp.where(l == ll, i, jnp.where(l < ll, 0, nr - 1)), 0))

    spec = pl.BlockSpec((None, tm, cols), lambda l, i: (l, i, 0))
    out = jax.ShapeDtypeStruct(w.shape, F32)
    return pl.pallas_call(
        body, name=name, grid=(layers, nr),
        in_specs=[part_spec(ll, parts[ll].shape[0]) for ll in range(layers)] + [spec] * 3,
        out_specs=[spec] * 4, out_shape=[out] * 4, scratch_shapes=[pltpu.VMEM((tm, cols), F32)],
        compiler_params=_params(("arbitrary", "arbitrary")),
    )(*parts, w, m, v)


def _pack(arrs):
    flat = jnp.concatenate([a.reshape(-1) for a in arrs])
    rows = -(-flat.shape[0] // (PACK_COLS * 16)) * 16
    return jnp.pad(flat, (0, rows * PACK_COLS - flat.shape[0])).reshape(rows, PACK_COLS)


def _unpack(packed, shapes):
    flat = packed.reshape(-1)
    out, off = [], 0
    for shp in shapes:
        n = int(np.prod(shp))
        out.append(flat[off:off + n].reshape(shp))
        off += n
    return out


def _shards_to_full(g, axis):
    if axis == 0:
        return g.reshape(g.shape[0] * g.shape[1], g.shape[2])
    return jnp.transpose(g, (1, 0, 2)).reshape(g.shape[1], g.shape[0] * g.shape[2])


def _full_to_shards(a, axis):
    if axis == 0:
        return jnp.transpose(a.reshape(4, 2, a.shape[0] // N_DEV, a.shape[1]), (1, 0, 2, 3))
    return jnp.transpose(a.reshape(a.shape[0], 4, 2, a.shape[1] // N_DEV), (2, 1, 0, 3))


def _zeros_like_cols(a, n):
    return jnp.zeros(a.shape[:-1] + (n,), a.dtype)


def _swap_cols(r, sign):
    half = MLA_ROPE // 2
    if sign > 0:
        return jnp.concatenate([-r[..., half:], r[..., :half]], axis=-1)
    return jnp.concatenate([r[..., half:], -r[..., :half]], axis=-1)


def _pad_heads(a, n):
    z = _zeros_like_cols(a, HEAD)
    return jnp.concatenate([p for h in range(n) for p in (a[..., h * HEAD:(h + 1) * HEAD], z)], axis=-1)


def _unpad_heads(a, n):
    return jnp.concatenate([a[..., h * LANES:h * LANES + HEAD] for h in range(n)], axis=-1)


def _ext_w_in(w):
    z = _zeros_like_cols(w, HEAD)
    kr = w[..., 384:416]
    krp = _swap_cols(kr, 1)
    return jnp.concatenate([
        _pad_heads(w[..., 1440:1696], 4), w[..., 0:256], w[..., 416:672], w[..., 672:928], w[..., 928:1184],
        w[..., 1184:1440], _pad_heads(w[..., 1696:1824], 2), _pad_heads(w[..., 1824:1952], 2), w[..., 256:384],
        z, kr, kr, z, krp, krp], axis=-1)


def _fold_w_in(e):
    kr = (e[..., 2496:2528] + e[..., 2528:2560]) + _swap_cols(e[..., 2624:2656] + e[..., 2656:2688], -1)
    return jnp.concatenate([
        e[..., 512:768], e[..., 2304:2432], kr, e[..., 768:1024], e[..., 1024:1280], e[..., 1280:1536],
        e[..., 1536:1792], _unpad_heads(e[..., 0:512], 4), _unpad_heads(e[..., 1792:2048], 2),
        _unpad_heads(e[..., 2048:2304], 2)], axis=-1)


def _ext_w_uq(w):
    parts = []
    for h in range(MLA_HEADS):
        rot = w[..., 96 * h + 64:96 * h + 96]
        parts += [w[..., 96 * h:96 * h + 64], rot, _swap_cols(rot, 1)]
    return jnp.concatenate(parts, axis=-1)


def _fold_w_uq(e):
    parts = []
    for h in range(MLA_HEADS):
        b = LANES * h
        parts += [e[..., b:b + 64], e[..., b + 64:b + 96] + _swap_cols(e[..., b + 96:b + 128], -1)]
    return jnp.concatenate(parts, axis=-1)


def _ext_w_ukv(w):
    k = jnp.concatenate([w[..., LANES * h:LANES * h + HEAD] for h in range(MLA_HEADS)], axis=-1)
    v = jnp.concatenate([w[..., LANES * h + HEAD:LANES * (h + 1)] for h in range(MLA_HEADS)], axis=-1)
    return jnp.concatenate([_pad_heads(k, 4), _pad_heads(v, 4)], axis=-1)


def _fold_w_ukv(e):
    parts = []
    for h in range(MLA_HEADS):
        parts += [e[..., LANES * h:LANES * h + HEAD], e[..., 512 + LANES * h:512 + LANES * h + HEAD]]
    return jnp.concatenate(parts, axis=-1)


def _ext_mix(a):
    return jnp.concatenate([_pad_heads(a[..., 0:256], 4), a[..., 256:768], _pad_heads(a[..., 768:1024], 4)], axis=-1)


def _fold_mix(e):
    return jnp.concatenate([_unpad_heads(e[..., 0:512], 4), e[..., 512:1024], _unpad_heads(e[..., 1024:1536], 4)], axis=-1)


def _rope_tables(s):
    inv = 1.0 / (ROPE_THETA ** (jnp.arange(0, MLA_ROPE, 2, dtype=F32) / MLA_ROPE))
    ang = jnp.arange(s, dtype=F32)[:, None] * inv[None, :]
    cos, sin = jnp.cos(ang), jnp.sin(ang)
    c32, s32 = jnp.concatenate([cos, cos], axis=1), jnp.concatenate([sin, sin], axis=1)
    zeros, ones = jnp.zeros((s, HEAD), F32), jnp.ones((s, HEAD), F32)
    tq = jnp.concatenate([ones, c32, s32], axis=1) * (1.0 / math.sqrt(MLA_NOPE + MLA_ROPE))
    return (jnp.tile(tq, (1, MLA_HEADS)), jnp.concatenate([zeros, c32, c32], axis=1),
            jnp.concatenate([zeros, s32, s32], axis=1))


def _gn(y):
    return y * _rstd(y, D_GROUP)


def _mixer_fwd(x, w, tabs, l):
    s = x.shape[0]
    tq, tkc, tks = tabs
    n = lambda t: f"l{l}_{t}"
    h = _rms_fwd(("row", x, D_MODEL, 0), w["attn_norm"], name=n("attn_norm"), rows=s, width=D_MODEL)
    proj = _mm(h, w["w_in"], name=n("in_proj"))
    cqn = _rms_fwd(("row", proj, 256, C_CQ[0] // 256), w["mla_q_norm"], name=n("q_norm"), rows=s, width=256)
    ckvn = _rms_fwd(("row", proj, 128, C_CKV[0] // 128), w["mla_kv_norm"], name=n("kv_norm"), rows=s, width=128)
    q_ext = _mm(cqn, w["w_uq"], name=n("uq"))
    kv_ext = _mm(ckvn, w["w_ukv"], name=n("ukv"))

    def prep(qe, kvk, kvv, kr, krp, tqv, tc, ts):
        kb = kr * tc + krp * ts
        lane = lax.broadcasted_iota(jnp.int32, kvv.shape, 1) & (LANES - 1)
        return qe * tqv, kvk + jnp.tile(kb, (1, MLA_HEADS)), jnp.where(lane == HEAD, 1.0, kvv)

    qm, km, vm = _rowwise(
        prep, name=n("mla_prep"), rows=s,
        ins=[("row", q_ext, 512, 0), ("row", kv_ext, 512, 0), ("row", kv_ext, 512, 1),
             ("row", proj, 128, C_KR[0] // 128), ("row", proj, 128, C_KRP[0] // 128),
             ("row", tq, 512, 0), ("row", tkc, 128, 0), ("row", tks, 128, 0)],
        outs=[(512, BF16)] * 3)
    y_a, lse_a = _mla_fwd(qm, km, vm, name=n("mla_fwd"), s=s, nh=MLA_HEADS)
    y_b = _conv_fwd(proj, w["conv_w"], name=n("conv_fwd"), s=s)
    y_c = _pool_fwd(proj, w["pool_wbd"], w["pool_scale"], name=n("pool_fwd"), s=s)
    y_d, lse_d = _swa_fwd(proj, w["hp_swa"], name=n("swa_fwd"), s=s, scale=1.0 / math.sqrt(HEAD))

    def mix(ya, yb, yc, yd, mn):
        return (jnp.concatenate([_gn(ya), _gn(yb), _gn(yc), _gn(yd)], axis=1) * mn,)

    mixed = _rowwise(mix, name=n("group_norm"), rows=s,
                     ins=[("row", y_a, 512, 0), ("row", y_b, 256, 0), ("row", y_c, 256, 0), ("row", y_d, 512, 0),
                          ("full", w["mix_norm"])], outs=[(D_MIX_EXT, BF16)])[0]
    x1 = _mm(mixed, w["w_o"], res=x, name=n("out_proj"))
    saved = dict(x=x, h=h, proj=proj, cqn=cqn, ckvn=ckvn, qm=qm, km=km, vm=vm, y_a=y_a, lse_a=lse_a, y_b=y_b, y_c=y_c,
                 y_d=y_d, lse_d=lse_d, mixed=mixed)
    return x1, saved


def _ffn_fwd(x1, w, l):
    s = x1.shape[0]
    n = lambda t: f"l{l}_{t}"
    h2 = _rms_fwd(("row", x1, D_MODEL, 0), w["ffn_norm"], name=n("ffn_norm"), rows=s, width=D_MODEL)

    def swiglu(gu):
        g, u = gu[:, :GU_TILE], gu[:, GU_TILE:]
        return gu, g * jax.nn.sigmoid(g) * u

    gu, act = _mm(h2, w["w_gate_up"], tn=2 * GU_TILE, name=n("gate_up"),
                  epilogue=(swiglu, [], [(2 * D_FF, BF16), (D_FF, BF16)]))
    x2 = _mm(act, w["w_down"], res=x1, tk=D_FF // 2, name=n("down"))
    return x2, dict(x1=x1, h2=h2, gu=gu, act=act)


def _ffn_bwd_down(dx2, sv, w, l):
    n = lambda t: f"l{l}_{t}"

    def swiglu_bwd(da, gu):
        gt, u = gu[:, :GU_TILE].astype(F32), gu[:, GU_TILE:].astype(F32)
        sg = jax.nn.sigmoid(gt)
        return (jnp.concatenate([da * u * sg * (1.0 + gt * (1.0 - sg)), da * gt * sg], axis=1),)

    dgu = _mm(dx2[1], w["w_down"], tb=True, tm=2048, tn=GU_TILE, name=n("d_act"),
              epilogue=(swiglu_bwd, [sv["gu"]], [(2 * D_FF, BF16)]))[0]
    g = dict(w_down=_mm(sv["act"], dx2[1], ta=True, tm=D_FF // 2, name=n("dw_down")))
    return dgu, g


def _ffn_bwd_up(dx2, dgu, sv, w, l):
    s = dgu.shape[0]
    n = lambda t: f"l{l}_{t}"
    dh2 = _mm(dgu, w["w_gate_up"], tb=True, tk=D_FF // 2, name=n("d_h2"))
    g = dict(w_gate_up=_mm(sv["h2"], dgu, ta=True, tn=D_FF // 2, name=n("dw_gate_up")))
    dx1, dx1_b, g["ffn_norm"] = _rms_bwd(("row", sv["x1"], D_MODEL, 0), w["ffn_norm"], dh2, dx2[0],
                                         name=n("ffn_norm_bwd"), rows=s, width=D_MODEL, out_dtypes=(F32, BF16))
    return (dx1, dx1_b), g


def _mixer_bwd_out(dx1, sv, w, l):
    s = dx1[1].shape[0]
    n = lambda t: f"l{l}_{t}"
    dmixed = _mm(dx1[1], w["w_o"], tb=True, name=n("d_mixed"))
    g = dict(w_o=_mm(sv["mixed"], dx1[1], ta=True, name=n("dw_o")))

    def mix_bwd(dm, ya, yb, yc, yd, mn):
        outs, dmn = [], []
        for y, lo, hi in ((ya, 0, 512), (yb, 512, 768), (yc, 768, 1024), (yd, 1024, 1536)):
            r = _rstd(y, D_GROUP)
            nrm = y * r
            dmg = dm[:, lo:hi]
            dn = dmg * mn[:, lo:hi]
            outs.append(r * (dn - nrm * (jnp.sum(dn * nrm, axis=-1, keepdims=True) * (1.0 / D_GROUP))))
            dmn.append(jnp.sum(dmg * nrm, axis=0, keepdims=True))
        return (*outs, jnp.concatenate(dmn, axis=1))

    dy_a, dy_b, dy_c, dy_d, g["mix_norm"] = _rowwise(
        mix_bwd, name=n("group_norm_bwd"), rows=s,
        ins=[("row", dmixed, D_MIX_EXT, 0), ("row", sv["y_a"], 512, 0), ("row", sv["y_b"], 256, 0),
             ("row", sv["y_c"], 256, 0), ("row", sv["y_d"], 512, 0), ("full", w["mix_norm"])],
        outs=[(512, F32), (256, F32), (256, F32), (512, F32)], reds=[(1, D_MIX_EXT)])
    return (dy_a, dy_b, dy_c, dy_d), g


def _mixer_bwd_in(dx1, dys, sv, w, tabs, l):
    s = dx1[0].shape[0]
    tq, tkc, tks = tabs
    n = lambda t: f"l{l}_{t}"
    dy_a, dy_b, dy_c, dy_d = dys
    g = {}

    proj = sv["proj"]
    dq_sw, dk_sw, dv_sw, dsink = _swa_bwd(proj, sv["y_d"], dy_d, sv["lse_d"], w["hp_swa"], name=n("swa_bwd"), s=s,
                                          scale=1.0 / math.sqrt(HEAD))
    g["swa_sinks"] = dsink[:, 0, 0]

    qkv = (sv["qm"], sv["km"], sv["vm"])
    dqm, dvec_a = _mla_dq(*qkv, sv["y_a"], dy_a, sv["lse_a"], w["hp_swa"], name=n("mla_dq"), s=s, nh=MLA_HEADS)
    dkm, dvm = _mla_dkv(*qkv, dy_a, sv["lse_a"].reshape(MLA_HEADS, 1, s), dvec_a.reshape(MLA_HEADS, 1, s),
                        name=n("mla_dkv"), s=s, nh=MLA_HEADS)

    def prep_bwd(dq, dk, tqv, tc, ts):
        dkb = dk[:, 0:128] + dk[:, 128:256] + dk[:, 256:384] + dk[:, 384:512]
        return dq * tqv, dk, dkb * tc, dkb * ts

    dq_ext, dkvk, dkr, dkrp = _rowwise(
        prep_bwd, name=n("mla_prep_bwd"), rows=s,
        ins=[("row", dqm, 512, 0), ("row", dkm, 512, 0), ("row", tq, 512, 0), ("row", tkc, 128, 0), ("row", tks, 128, 0)],
        outs=[(512, BF16), (512, BF16), (128, BF16), (128, BF16)])
    dkv_ext = jnp.concatenate([dkvk, dvm], axis=1)
    dcqn = _mm(dq_ext, w["w_uq"], tb=True, name=n("d_cqn"))
    g["w_uq"] = _mm(sv["cqn"], dq_ext, ta=True, name=n("dw_uq"))
    dckvn = _mm(dkv_ext, w["w_ukv"], tb=True, name=n("d_ckvn"))
    g["w_ukv"] = _mm(sv["ckvn"], dkv_ext, ta=True, name=n("dw_ukv"))
    dcq, g["mla_q_norm"] = _rms_bwd(("row", proj, 256, C_CQ[0] // 256), w["mla_q_norm"], dcqn, None,
                                    name=n("q_norm_bwd"), rows=s, width=256, out_dtypes=(BF16,))
    dckv, g["mla_kv_norm"] = _rms_bwd(("row", proj, 128, C_CKV[0] // 128), w["mla_kv_norm"], dckvn, None,
                                      name=n("kv_norm_bwd"), rows=s, width=128, out_dtypes=(BF16,))

    dgb, dgc, duc, g["conv_w"] = _conv_bwd(dy_b, proj, w["conv_w"], name=n("conv_bwd"), s=s)
    dup, g["pool_wbd"], g["pool_scale"] = _pool_bwd(dy_c, proj, w["pool_wbd"], w["pool_scale"], name=n("pool_bwd"), s=s)

    dproj = jnp.concatenate([dq_sw, dcq, dgb, dgc, duc, dup, dk_sw.astype(BF16), dv_sw.astype(BF16), dckv, dkr, dkrp],
                            axis=1)
    dh = _mm(dproj, w["w_in"], tb=True, name=n("d_h"))
    g["w_in"] = _mm(sv["h"], dproj, ta=True, name=n("dw_in"))
    dx0, dx0_b, g["attn_norm"] = _rms_bwd(("row", sv["x"], D_MODEL, 0), w["attn_norm"], dh, dx1[0],
                                          name=n("attn_norm_bwd"), rows=s, width=D_MODEL, out_dtypes=(F32, BF16))
    return (dx0, dx0_b), g


def _loss_head(x, target, g, *, s):
    def fn(xv, tv, gv):
        r = _rstd(xv, D_MODEL)
        e = xv * r * gv - tv
        part = jnp.sum(jnp.sum(e * e, axis=1, keepdims=True), axis=0, keepdims=True) * (0.5 / D_MODEL)
        dy = e * (1.0 / D_MODEL)
        dyg = dy * gv
        dx = r * dyg - xv * (r * r * r) * (jnp.sum(dyg * xv, axis=-1, keepdims=True) * (1.0 / D_MODEL))
        return dx, dx, jnp.sum(dy * xv * r, axis=0, keepdims=True), jnp.broadcast_to(part, (1, LANES))

    return _rowwise(fn, name="loss_head", rows=s,
                    ins=[("row", x, D_MODEL, 0), ("row", target, D_MODEL, 0), ("full", g)],
                    outs=[(D_MODEL, F32), (D_MODEL, BF16)], reds=[(1, D_MODEL), (1, LANES)])


def _alibi_slopes(n):
    return np.asarray([2.0 ** (-8.0 * (i + 1) / n) for i in range(n)], dtype=np.float32)


MIXER_WEIGHTS = ("w_in", "w_uq", "w_ukv", "conv_w", "w_o")
FFN_WEIGHTS = ("w_gate_up", "w_down")


def _mixer_weights(full, rep, l):
    pw = rep["pool_w"][l]
    z = jnp.zeros((HEAD, HEAD), F32)
    wbd = jnp.stack([jnp.block([[pw[2 * j], z], [z, pw[2 * j + 1]]]) for j in range(2)])
    w_o_ext = jnp.moveaxis(_ext_mix(jnp.moveaxis(full["w_o"], 0, -1)), -1, 0)
    return dict(
        attn_norm=rep["attn_norm"][l][None], w_in=_ext_w_in(full["w_in"]), mla_q_norm=rep["mla_q_norm"][l][None],
        w_uq=_ext_w_uq(full["w_uq"]), mla_kv_norm=rep["mla_kv_norm"][l][None], w_ukv=_ext_w_ukv(full["w_ukv"]),
        conv_w=full["conv_w"], pool_wbd=wbd, pool_scale=rep["pool_scale"][l][None],
        mix_norm=_ext_mix(rep["mix_norm"][l])[None], w_o=w_o_ext,
        hp_swa=jnp.stack([jnp.asarray(_alibi_slopes(SWA_HEADS)), rep["swa_sinks"][l]], axis=1))


def _ext_w_gate_up(w):
    return jnp.concatenate([w[:, half + j:half + j + GU_TILE] for j in range(0, D_FF, GU_TILE) for half in (0, D_FF)], axis=1)


def _fold_w_gate_up(e):
    return jnp.concatenate([e[:, 2 * j + half:2 * j + half + GU_TILE] for half in (0, GU_TILE) for j in range(0, D_FF, GU_TILE)],
                           axis=1)


def _ffn_weights(full, rep, l):
    return dict(ffn_norm=rep["ffn_norm"][l][None], w_gate_up=_ext_w_gate_up(full["w_gate_up"]), w_down=full["w_down"])


def _fold_grads(g):
    fold = dict(
        w_in=_fold_w_in, w_uq=_fold_w_uq, w_ukv=_fold_w_ukv, w_gate_up=_fold_w_gate_up,
        w_o=lambda e: jnp.moveaxis(_fold_mix(jnp.moveaxis(e, 0, -1)), -1, 0),
        mix_norm=lambda e: _fold_mix(e[0]),
        pool_wbd=lambda e: jnp.stack([e[j // 2][HEAD * (j % 2):HEAD * (j % 2 + 1), HEAD * (j % 2):HEAD * (j % 2 + 1)]
                                      for j in range(4)]))
    rows = ("attn_norm", "mla_q_norm", "mla_kv_norm", "pool_scale", "ffn_norm")
    out = {}
    for nm, e in g.items():
        out["pool_w" if nm == "pool_wbd" else nm] = fold[nm](e) if nm in fold else (e[0] if nm in rows else e)
    return out


def kernel(x, attn_norm, w_in, mla_q_norm, w_uq, mla_kv_norm, w_ukv, conv_w, pool_w, pool_scale, swa_sinks, mix_norm, w_o, ffn_norm, w_gate_up, w_down, final_norm, loss_target, m_attn_norm, m_w_in, m_mla_q_norm, m_w_uq, m_mla_kv_norm, m_w_ukv, m_conv_w, m_pool_w, m_pool_scale, m_swa_sinks, m_mix_norm, m_w_o, m_ffn_norm, m_w_gate_up, m_w_down, m_final_norm, v_attn_norm, v_w_in, v_mla_q_norm, v_w_uq, v_mla_kv_norm, v_w_ukv, v_conv_w, v_pool_w, v_pool_scale, v_swa_sinks, v_mix_norm, v_w_o, v_ffn_norm, v_w_gate_up, v_w_down, v_final_norm):
    given = dict(locals())
    sh_names = [nm for nm, _, _ in SHARDED]
    sh_axis = {nm: ax - 1 for nm, _, ax in SHARDED}
    rep_names = [nm for nm, _ in REPLICATED]
    rep_shapes = [shp for _, shp in REPLICATED]
    rep = {nm: given[nm] for nm in rep_names if nm != "loss"}
    me = 4 * lax.axis_index("x") + 2 * lax.axis_index("y") + lax.axis_index("c")
    my_chip = 2 * lax.axis_index("x") + lax.axis_index("y")
    core = lax.axis_index("c").astype(jnp.int32).reshape(1)

    def behind(token, a):
        return a + token[0, 0].astype(a.dtype)

    def wire(nm, l):
        if nm == "conv_w":
            return lax.bitcast_convert_type(given[nm][l], BF16).reshape(3, -1)
        return given[nm][l].astype(BF16)

    def whole(nm, g):
        if nm == "conv_w":
            g = lax.bitcast_convert_type(g.reshape(N_DEV, 3, -1, 2), F32)
        return _shards_to_full(g, sh_axis[nm])

    def gather_start(names, l, after, tag):
        srcs = [wire(nm, l) for nm in names]
        return _start_copies(_plan_gather, N_DEV - 1, srcs, [(N_DEV,) + a.shape for a in srcs], after, name=f"start_gather_{tag}")

    def gather_wait(names, handle, after, tag):
        srcs, lands = _wait_copies(_plan_gather, handle, after, name=f"wait_gather_{tag}")
        return {nm: whole(nm, lax.dynamic_update_index_in_dim(land, src, me, 0)) for nm, src, land in zip(names, srcs, lands)}

    got = _all_gather([wire(nm, 0) for nm in MIXER_WEIGHTS], name="gather_mixer0")
    full_m0 = {nm: whole(nm, g) for nm, g in zip(MIXER_WEIGHTS, got)}
    h_f0, tok = gather_start(FFN_WEIGHTS, 0, [], "ffn0")
    h_l1, tok = gather_start(MIXER_WEIGHTS + FFN_WEIGHTS, 1, [tok], "layer1")

    xs, target = x[0], loss_target[0]
    s = xs.shape[0]
    tabs = _rope_tables(s)
    wm, wf, svm, svf = [None] * DEPTH, [None] * DEPTH, [None] * DEPTH, [None] * DEPTH
    wm[0] = _mixer_weights(full_m0, rep, 0)
    wm[0]["attn_norm"] = behind(tok, wm[0]["attn_norm"])
    x1, svm[0] = _mixer_fwd(xs, wm[0], tabs, 0)
    wf[0] = _ffn_weights(gather_wait(FFN_WEIGHTS, h_f0, [x1], "ffn0"), rep, 0)
    x2, svf[0] = _ffn_fwd(x1, wf[0], 0)
    full_1 = gather_wait(MIXER_WEIGHTS + FFN_WEIGHTS, h_l1, [x2], "layer1")
    wm[1], wf[1] = _mixer_weights(full_1, rep, 1), _ffn_weights(full_1, rep, 1)
    x1, svm[1] = _mixer_fwd(x2, wm[1], tabs, 1)
    x2, svf[1] = _ffn_fwd(x1, wf[1], 1)
    dx_f, dx_b, d_final, loss = _loss_head(x2, target, rep["final_norm"][None], s=s)
    dx = (dx_f, dx_b)

    parts = {}

    def reduce_start(grads, l, after, tag):
        names = [nm for nm in sh_names if nm in grads]
        mine = [_full_to_shards(grads[nm], sh_axis[nm]) for nm in names]
        handle, token = _start_copies(_plan_sibling, 1, mine, [m.shape[1:] for m in mine], after, name=f"start_sibling_{tag}")
        return (names, l, handle), token

    def reduce_mid(state, after, tag):
        names, l, handle = state
        mine, theirs = _wait_copies(_plan_sibling, handle, after, name=f"wait_sibling_{tag}")
        sums = [_add_own(g, o, core, name=f"chip_sum_{nm}_{l}") for nm, g, o in zip(names, mine, theirs)]
        handle, token = _start_copies(_plan_chips, 3, sums, [a.shape for a in sums], [], name=f"start_chips_{tag}")
        return (names, l, handle), token

    def reduce_end(state, after, tag):
        names, l, handle = state
        sums, lands = _wait_copies(_plan_chips, handle, after, name=f"wait_chips_{tag}")
        for nm, own, land in zip(names, sums, lands):
            parts[nm, l] = lax.dynamic_update_index_in_dim(land, lax.dynamic_index_in_dim(own, my_chip, 0, keepdims=False),
                                                           my_chip, 0)

    small = [None] * DEPTH
    in_flight = []
    pending = None
    for l in reversed(range(DEPTH)):
        dgu, g_down = _ffn_bwd_down(dx, svf[l], wf[l], l)
        if pending is not None:
            state, token = reduce_mid(pending, [dgu], f"mixer{l + 1}")
            in_flight.append((state, f"mixer{l + 1}"))
            wf[l]["ffn_norm"] = behind(token, wf[l]["ffn_norm"])
        dx1, g_up = _ffn_bwd_up(dx, dgu, svf[l], wf[l], l)
        g_ffn = _fold_grads({**g_down, **g_up})
        state, token = reduce_start(g_ffn, l, [], f"ffn{l}")
        wm[l]["mix_norm"] = behind(token, wm[l]["mix_norm"])
        dys, g_out = _mixer_bwd_out(dx1, svm[l], wm[l], l)
        state, token = reduce_mid(state, [dys[0]], f"ffn{l}")
        in_flight.append((state, f"ffn{l}"))
        wm[l]["hp_swa"] = behind(token, wm[l]["hp_swa"])
        dx, g_in = _mixer_bwd_in(dx1, dys, svm[l], wm[l], tabs, l)
        g_mixer = _fold_grads({**g_out, **g_in})
        small[l] = {**g_ffn, **g_mixer}
        pending, token = reduce_start(g_mixer, l, [], f"mixer{l}")
    state, token = reduce_mid(pending, [dx[0]], "mixer0")
    in_flight.append((state, "mixer0"))
    for state, tag in in_flight:
        reduce_end(state, [], tag)
    grad_x = dx[0]

    sh_out = {nm: _adamw([parts[nm, l] for l in range(DEPTH)], given[nm], given["m_" + nm], given["v_" + nm],
                         name=f"adamw_{nm}") for nm in sh_names}

    grads = {nm: jnp.stack([small[l][nm] for l in range(DEPTH)]) for nm in rep_names if nm in small[0]}
    grads["final_norm"] = d_final[0]
    grads["loss"] = loss[0, :1]
    zero = jnp.zeros((1,), F32)
    small = _all_gather([_pack([grads[nm] for nm in rep_names])], name="gather_small_grads")
    packs = [_pack([given.get(pre + nm, zero) for nm in rep_names])[None] for pre in ("", "m_", "v_")]
    rep_out = [dict(zip(rep_names, _unpack(o[0], rep_shapes))) for o in _adamw(small, *packs, name="adamw_replicated")]

    out = [rep_out[0]["loss"][0], grad_x[None]]
    for i in range(4):
        out += [sh_out[nm][i] if nm in sh_axis else rep_out[i][nm] for nm in WEIGHT_ORDER]
    return tuple(out)
```

```python
import functools
import math

import numpy as np
import jax
import jax.numpy as jnp
from jax import lax
from jax.experimental import pallas as pl
from jax.experimental.pallas import tpu as pltpu

F32 = jnp.float32
BF16 = jnp.bfloat16

D_MODEL = 1024
DEPTH = 2
D_GROUP = 256
MLA_HEADS = 4
MLA_NOPE = 64
MLA_ROPE = 32
ROPE_THETA = 10000.0
POOL_WINDOWS = (2, 4, 8, 16)
SWA_HEADS = 4
SWA_KV_HEADS = 2
SWA_WINDOW = 128
D_FF = 2816
GU_TILE = 256
RMS_EPS = 1e-6
LANES = 128
HEAD = 64
VMEM_LIMIT = 48 * 1024 * 1024
NEG = -1e30

ADAM_LR = 0.001
ADAM_B1 = 0.9
ADAM_B2 = 0.999
ADAM_EPS = 1e-08
ADAM_WD = 0.01
ADAM_STEP = 10

N_DEV = 8
PACK_COLS = 1024

C_QSW, C_CQ, C_GB, C_GC, C_UCONV, C_UPOOL = (0, 512), (512, 256), (768, 256), (1024, 256), (1280, 256), (1536, 256)
C_KSW, C_VSW, C_CKV, C_KR, C_KRP = (1792, 256), (2048, 256), (2304, 128), (2432, 128), (2560, 128)
D_IN_EXT = 2688

SHARDED = (("w_in", (DEPTH, 1024, 244), 2), ("w_uq", (DEPTH, 256, 48), 2), ("w_ukv", (DEPTH, 128, 64), 2),
           ("conv_w", (DEPTH, 3, 32), 2), ("w_o", (DEPTH, 128, 1024), 1), ("w_gate_up", (DEPTH, 1024, 704), 2),
           ("w_down", (DEPTH, 352, 1024), 1))
REPLICATED = (("attn_norm", (DEPTH, 1024)), ("mla_q_norm", (DEPTH, 256)), ("mla_kv_norm", (DEPTH, 128)),
              ("pool_w", (DEPTH, 4, 64, 64)), ("pool_scale", (DEPTH, 256)), ("swa_sinks", (DEPTH, 4)),
              ("mix_norm", (DEPTH, 1024)), ("ffn_norm", (DEPTH, 1024)), ("final_norm", (1024,)), ("loss", (1,)))
WEIGHT_ORDER = ("attn_norm", "w_in", "mla_q_norm", "w_uq", "mla_kv_norm", "w_ukv", "conv_w", "pool_w", "pool_scale",
                "swa_sinks", "mix_norm", "w_o", "ffn_norm", "w_gate_up", "w_down", "final_norm")


def _params(sem):
    return pltpu.CompilerParams(dimension_semantics=sem, vmem_limit_bytes=VMEM_LIMIT)


def _pick(dim, target):
    if dim <= target:
        return dim
    best = None
    for t in range(LANES, target + 1, LANES):
        if dim % t == 0:
            best = t
    assert best is not None, (dim, target)
    return best


def _mm(a, b, *, name, ta=False, tb=False, res=None, out_dtype=F32, tm=1024, tn=1024, tk=1024, epilogue=None):
    m, k = (a.shape[1], a.shape[0]) if ta else a.shape
    n = b.shape[0] if tb else b.shape[1]
    assert (b.shape[1] if tb else b.shape[0]) == k
    tm, tn, tk = _pick(m, tm), _pick(n, tn), _pick(k, tk)
    nj, nk = n // tn, k // tk
    dims = (((0 if ta else 1,), (1 if tb else 0,)), ((), ()))
    fn, extra, outs = epilogue if epilogue is not None else (None, [], [(n, out_dtype)])
    if res is not None:
        assert epilogue is None
        fn, extra = (lambda acc, r: (acc + r,)), [res]
    n_in, n_out = 2 + len(extra), len(outs)

    def body(*refs):
        a_ref, b_ref, acc_ref = refs[0], refs[1], refs[-1]
        kk = pl.program_id(2)

        @pl.when(kk == 0)
        def _():
            acc_ref[...] = jnp.zeros_like(acc_ref)

        acc_ref[...] += lax.dot_general(a_ref[...].astype(BF16), b_ref[...].astype(BF16), dims,
                                        preferred_element_type=F32)

        @pl.when(kk == nk - 1)
        def _():
            tiles = (acc_ref[...],) if fn is None else fn(acc_ref[...], *[r[...] for r in refs[2:n_in]])
            for o_ref, tile in zip(refs[n_in:n_in + n_out], tiles):
                o_ref[...] = tile.astype(o_ref.dtype)

    def col_tiles(width):
        assert width % (nj * LANES) == 0, (width, nj)
        return pl.BlockSpec((tm, width // nj), lambda i, j, kk: (i, j))

    a_spec = pl.BlockSpec((tk, tm), lambda i, j, kk: (kk, i)) if ta else pl.BlockSpec((tm, tk), lambda i, j, kk: (i, kk))
    b_spec = pl.BlockSpec((tn, tk), lambda i, j, kk: (j, kk)) if tb else pl.BlockSpec((tk, tn), lambda i, j, kk: (kk, j))
    res_ = pl.pallas_call(
        body, name=name, grid=(m // tm, nj, nk), in_specs=[a_spec, b_spec] + [col_tiles(e.shape[1]) for e in extra],
        out_specs=[col_tiles(w) for w, _ in outs],
        out_shape=[jax.ShapeDtypeStruct((m, w), dt) for w, dt in outs],
        scratch_shapes=[pltpu.VMEM((tm, tn), F32)],
        compiler_params=_params(("parallel", "parallel", "arbitrary")),
    )(a, b, *extra)
    return res_[0] if epilogue is None else res_


def _rowwise(fn, *, name, rows, ins, outs, reds=(), tm=512):
    tm = min(tm, rows)
    assert rows % tm == 0
    n_in, n_out = len(ins), len(outs)

    def body(*refs):
        vals = [r[...] for r in refs[:n_in]]
        res = fn(*vals)
        for r, v in zip(refs[n_in:n_in + n_out], res[:n_out]):
            r[...] = v.astype(r.dtype)
        if reds:
            @pl.when(pl.program_id(0) == 0)
            def _():
                for r in refs[n_in + n_out:]:
                    r[...] = jnp.zeros_like(r)

            for r, v in zip(refs[n_in + n_out:], res[n_out:]):
                r[...] += v

    in_specs, args = [], []
    for spec in ins:
        if spec[0] == "row":
            _, arr, width, blk = spec
            in_specs.append(pl.BlockSpec((tm, width), functools.partial(lambda i, blk: (i, blk), blk=blk)))
        else:
            arr = spec[1]
            in_specs.append(pl.BlockSpec(arr.shape, functools.partial(lambda i, nd: (0,) * nd, nd=arr.ndim)))
        args.append(arr)
    out_specs = [pl.BlockSpec((tm, w), lambda i: (i, 0)) for w, _ in outs]
    out_shape = [jax.ShapeDtypeStruct((rows, w), dt) for w, dt in outs]
    out_specs += [pl.BlockSpec((r, w), lambda i: (0, 0)) for r, w in reds]
    out_shape += [jax.ShapeDtypeStruct((r, w), F32) for r, w in reds]
    return pl.pallas_call(body, name=name, grid=(rows // tm,), in_specs=in_specs, out_specs=out_specs,
                          out_shape=out_shape, compiler_params=_params(("arbitrary",)))(*args)


def _rstd(x, n):
    return lax.rsqrt(jnp.sum(x * x, axis=-1, keepdims=True) * (1.0 / n) + RMS_EPS)


def _rms_fwd(x_spec, g, *, name, rows, width):
    def fn(x, gv):
        return (x * _rstd(x, width) * gv,)
    return _rowwise(fn, name=name, rows=rows, ins=[x_spec, ("full", g)], outs=[(width, BF16)])[0]


def _rms_bwd(x_spec, g, dy, res, *, name, rows, width, out_dtypes):
    def fn(x, gv, dyv, *rest):
        r = _rstd(x, width)
        dyg = dyv * gv
        dx = r * dyg - x * (r * r * r) * (jnp.sum(dyg * x, axis=-1, keepdims=True) * (1.0 / width))
        if rest:
            dx = dx + rest[0]
        return (dx,) * len(out_dtypes) + (jnp.sum(dyv * x * r, axis=0, keepdims=True),)

    ins = [x_spec, ("full", g), ("row", dy, width, 0)]
    if res is not None:
        ins.append(("row", res, width, 0))
    return _rowwise(fn, name=name, rows=rows, ins=ins, outs=[(width, dt) for dt in out_dtypes], reds=[(1, width)])


NT_DIMS = (((1,), (1,)), ((), ()))
TN_DIMS = (((0,), (0,)), ((), ()))


def _mla_tile(s):
    return min(512, s)


def _causal(shape, query_axis):
    return lax.broadcasted_iota(jnp.int32, shape, query_axis) >= lax.broadcasted_iota(jnp.int32, shape, 1 - query_axis)


def _mla_fwd(qa, ka, va, *, name, s, nh):
    t = _mla_tile(s)
    nq = s // t

    def body(q_ref, k_ref, v_ref, o_ref, lse_ref, m_s, acc_s):
        i, j = pl.program_id(0), pl.program_id(1)

        @pl.when(j == 0)
        def _():
            m_s[...] = jnp.full_like(m_s, NEG)
            acc_s[...] = jnp.zeros_like(acc_s)

        def step(diag):
            for h in range(nh):
                cols = pl.ds(h * LANES, LANES)
                sc = lax.dot_general(q_ref[:, cols], k_ref[:, cols], NT_DIMS, preferred_element_type=F32)
                if diag:
                    sc = jnp.where(_causal(sc.shape, 0), sc, NEG)
                m_prev = m_s[h]
                m_new = jnp.maximum(m_prev, jnp.max(sc, axis=1, keepdims=True))
                p = jnp.exp(sc - m_new).astype(BF16)
                acc_s[:, cols] = (jnp.exp(m_prev - m_new) * acc_s[:, cols]
                                  + jnp.dot(p, v_ref[:, cols], preferred_element_type=F32))
                m_s[h] = m_new

        pl.when(j < i)(functools.partial(step, False))
        pl.when(j == i)(functools.partial(step, True))

        @pl.when(j == nq - 1)
        def _():
            lane = lax.broadcasted_iota(jnp.int32, (t, LANES), 1)
            for h in range(nh):
                cols = pl.ds(h * LANES, LANES)
                acc = acc_s[:, cols]
                l = jnp.sum(jnp.where(lane == HEAD, acc, 0.0), axis=1, keepdims=True)
                o_ref[:, cols] = jnp.where(lane < HEAD, acc / l, 0.0)
                lse_ref[h] = m_s[h] + jnp.log(l)

    q_spec = pl.BlockSpec((t, nh * LANES), lambda i, j: (i, 0))
    kv_spec = pl.BlockSpec((t, nh * LANES), lambda i, j: (jnp.minimum(j, i), 0))
    return pl.pallas_call(
        body, name=name, grid=(nq, nq), in_specs=[q_spec, kv_spec, kv_spec],
        out_specs=[q_spec, pl.BlockSpec((nh, t, 1), lambda i, j: (0, i, 0))],
        out_shape=[jax.ShapeDtypeStruct((s, nh * LANES), F32), jax.ShapeDtypeStruct((nh, s, 1), F32)],
        scratch_shapes=[pltpu.VMEM((nh, t, 1), F32), pltpu.VMEM((t, nh * LANES), F32)],
        compiler_params=_params(("parallel", "arbitrary")),
    )(qa, ka, va)


def _mla_dq(qa, ka, va, o, do, lse, after, *, name, s, nh):
    t = _mla_tile(s)
    nq = s // t

    def body(q_ref, k_ref, v_ref, o_ref, do_ref, lse_ref, dq_ref, d_ref, acc_s):
        i, j = pl.program_id(0), pl.program_id(1)

        @pl.when(j == 0)
        def _():
            acc_s[...] = jnp.zeros_like(acc_s)
            for h in range(nh):
                cols = pl.ds(h * LANES, LANES)
                d_ref[h] = jnp.sum(do_ref[:, cols] * o_ref[:, cols], axis=1, keepdims=True)

        def step(diag):
            for h in range(nh):
                cols = pl.ds(h * LANES, LANES)
                k = k_ref[:, cols]
                sc = lax.dot_general(q_ref[:, cols], k, NT_DIMS, preferred_element_type=F32)
                if diag:
                    sc = jnp.where(_causal(sc.shape, 0), sc, NEG)
                p = jnp.exp(sc - lse_ref[h])
                dp = lax.dot_general(do_ref[:, cols].astype(BF16), v_ref[:, cols], NT_DIMS, preferred_element_type=F32)
                ds = p * (dp - d_ref[h])
                acc_s[:, cols] += jnp.dot(ds.astype(BF16), k, preferred_element_type=F32)

        pl.when(j < i)(functools.partial(step, False))
        pl.when(j == i)(functools.partial(step, True))

        @pl.when(j == nq - 1)
        def _():
            dq_ref[...] = acc_s[...]

    q_spec = pl.BlockSpec((t, nh * LANES), lambda i, j: (i, 0))
    kv_spec = pl.BlockSpec((t, nh * LANES), lambda i, j: (jnp.minimum(j, i), 0))
    vec_spec = pl.BlockSpec((nh, t, 1), lambda i, j: (0, i, 0))
    def body_after(q_ref, k_ref, v_ref, o_ref, do_ref, lse_ref, after_ref, *rest):
        body(q_ref, k_ref, v_ref, o_ref, do_ref, lse_ref, *rest)

    return pl.pallas_call(
        body_after, name=name, grid=(nq, nq),
        in_specs=[q_spec, kv_spec, kv_spec, q_spec, q_spec, vec_spec, pl.BlockSpec(memory_space=pl.ANY)],
        out_specs=[q_spec, vec_spec],
        out_shape=[jax.ShapeDtypeStruct((s, nh * LANES), F32), jax.ShapeDtypeStruct((nh, s, 1), F32)],
        scratch_shapes=[pltpu.VMEM((t, nh * LANES), F32)],
        compiler_params=_params(("parallel", "arbitrary")),
    )(qa, ka, va, o, do, lse, after)


def _mla_dkv(qa, ka, va, do, lse_row, d_row, *, name, s, nh):
    t = _mla_tile(s)
    nq = s // t

    def body(q_ref, k_ref, v_ref, do_ref, lse_ref, d_ref, dk_ref, dv_ref, dk_s, dv_s):
        kb, j = pl.program_id(0), pl.program_id(1)

        @pl.when(j == 0)
        def _():
            dk_s[...] = jnp.zeros_like(dk_s)
            dv_s[...] = jnp.zeros_like(dv_s)

        def step(diag):
            for h in range(nh):
                cols = pl.ds(h * LANES, LANES)
                q = q_ref[:, cols]
                do_b = do_ref[:, cols].astype(BF16)
                sc = lax.dot_general(k_ref[:, cols], q, NT_DIMS, preferred_element_type=F32)
                if diag:
                    sc = jnp.where(_causal(sc.shape, 1), sc, NEG)
                p = jnp.exp(sc - lse_ref[h])
                dv_s[:, cols] += jnp.dot(p.astype(BF16), do_b, preferred_element_type=F32)
                dp = lax.dot_general(v_ref[:, cols], do_b, NT_DIMS, preferred_element_type=F32)
                ds = p * (dp - d_ref[h])
                dk_s[:, cols] += jnp.dot(ds.astype(BF16), q, preferred_element_type=F32)

        pl.when(j > kb)(functools.partial(step, False))
        pl.when(j == kb)(functools.partial(step, True))

        @pl.when(j == nq - 1)
        def _():
            dk_ref[...] = dk_s[...]
            dv_ref[...] = dv_s[...].astype(dv_ref.dtype)

    q_spec = pl.BlockSpec((t, nh * LANES), lambda kb, j: (jnp.maximum(j, kb), 0))
    kv_spec = pl.BlockSpec((t, nh * LANES), lambda kb, j: (kb, 0))
    row_spec = pl.BlockSpec((nh, 1, t), lambda kb, j: (0, 0, jnp.maximum(j, kb)))
    return pl.pallas_call(
        body, name=name, grid=(nq, nq), in_specs=[q_spec, kv_spec, kv_spec, q_spec, row_spec, row_spec],
        out_specs=[kv_spec, kv_spec],
        out_shape=[jax.ShapeDtypeStruct((s, nh * LANES), F32), jax.ShapeDtypeStruct((s, nh * LANES), BF16)],
        scratch_shapes=[pltpu.VMEM((t, nh * LANES), F32), pltpu.VMEM((t, nh * LANES), F32)],
        compiler_params=_params(("parallel", "arbitrary")),
    )(qa, ka, va, do, lse_row, d_row)


SWA_PIECE = 128
SWA_KEYS = 2 * SWA_PIECE


def _swa_block(s):
    return min(512, s)


def _swa_piece(hp_ref, h, q, k_ref, v_ref, qpos0, scale):
    kstart = pl.multiple_of(jnp.maximum(qpos0 - SWA_PIECE, 0), SWA_PIECE)
    k = k_ref[pl.ds(kstart, SWA_KEYS), :].astype(BF16)
    v = v_ref[pl.ds(kstart, SWA_KEYS), :].astype(BF16)
    sc = lax.dot_general(q, k, NT_DIMS, preferred_element_type=F32)
    dist = (qpos0 + lax.broadcasted_iota(jnp.int32, sc.shape, 0)) - (kstart + lax.broadcasted_iota(jnp.int32, sc.shape, 1))
    sc = sc * scale - hp_ref[h, 0] * dist.astype(F32)
    sc = jnp.where((dist >= 0) & (dist < SWA_WINDOW), sc, NEG)
    return kstart, k, v, sc


def _swa_fwd(proj, hp, *, name, s, scale):
    tb = _swa_block(s)
    group = SWA_HEADS // SWA_KV_HEADS
    q_off, k_off, v_off = C_QSW[0] // LANES, C_KSW[0] // LANES, C_VSW[0] // LANES

    def body(hp_ref, q_ref, k_ref, v_ref, o_ref, lse_ref):
        h, i = pl.program_id(0), pl.program_id(1)
        sink = hp_ref[h, 1]
        for r in range(0, tb, SWA_PIECE):
            rows = pl.ds(r, SWA_PIECE)
            _, _, v, sc = _swa_piece(hp_ref, h, q_ref[rows, :].astype(BF16), k_ref, v_ref, i * tb + r, scale)
            m = jnp.maximum(jnp.max(sc, axis=1, keepdims=True), sink)
            p = jnp.exp(sc - m)
            l = jnp.sum(p, axis=1, keepdims=True) + jnp.exp(sink - m)
            o_ref[rows, :] = jnp.dot(p.astype(BF16), v, preferred_element_type=F32) / l
            lse_ref[rows, :] = m + jnp.log(l)

    whole = lambda off: pl.BlockSpec((s, LANES), lambda h, i: (0, off + h // group))
    return pl.pallas_call(
        body, name=name, grid=(SWA_HEADS, s // tb),
        in_specs=[pl.BlockSpec(memory_space=pltpu.SMEM), pl.BlockSpec((tb, LANES), lambda h, i: (i, q_off + h)),
                  whole(k_off), whole(v_off)],
        out_specs=[pl.BlockSpec((tb, LANES), lambda h, i: (i, h)), pl.BlockSpec((None, tb, 1), lambda h, i: (h, i, 0))],
        out_shape=[jax.ShapeDtypeStruct((s, SWA_HEADS * LANES), F32), jax.ShapeDtypeStruct((SWA_HEADS, s, 1), F32)],
        compiler_params=_params(("parallel", "parallel")),
    )(hp, proj, proj, proj)


def _swa_bwd(proj, o, do, lse, hp, *, name, s, scale):
    tb = _swa_block(s)
    nqb = s // tb
    group = SWA_HEADS // SWA_KV_HEADS
    q_off, k_off, v_off = C_QSW[0] // LANES, C_KSW[0] // LANES, C_VSW[0] // LANES

    def body(hp_ref, q_ref, k_ref, v_ref, o_ref, do_ref, lse_ref, dq_ref, dk_ref, dv_ref, dsink_ref):
        kh, g, i = pl.program_id(0), pl.program_id(1), pl.program_id(2)
        h = kh * group + g
        sink = hp_ref[h, 1]

        @pl.when((g == 0) & (i == 0))
        def _():
            dk_ref[...] = jnp.zeros_like(dk_ref)
            dv_ref[...] = jnp.zeros_like(dv_ref)

        @pl.when(i == 0)
        def _():
            dsink_ref[...] = jnp.zeros_like(dsink_ref)

        for r in range(0, tb, SWA_PIECE):
            rows = pl.ds(r, SWA_PIECE)
            q = q_ref[rows, :].astype(BF16)
            dov = do_ref[rows, :]
            do_b = dov.astype(BF16)
            lse_r = lse_ref[rows, :]
            d_r = jnp.sum(dov * o_ref[rows, :], axis=1, keepdims=True)
            kstart, k, v, sc = _swa_piece(hp_ref, h, q, k_ref, v_ref, i * tb + r, scale)
            p = jnp.exp(sc - lse_r)
            dp = lax.dot_general(do_b, v, NT_DIMS, preferred_element_type=F32)
            ds = (p * (dp - d_r)).astype(BF16)
            dq_ref[rows, :] = (jnp.dot(ds, k, preferred_element_type=F32) * scale).astype(dq_ref.dtype)
            win = pl.ds(kstart, SWA_KEYS)
            dk_ref[win, :] += lax.dot_general(ds, q, TN_DIMS, preferred_element_type=F32) * scale
            dv_ref[win, :] += lax.dot_general(p.astype(BF16), do_b, TN_DIMS, preferred_element_type=F32)
            part = jnp.sum(-jnp.exp(sink - lse_r) * d_r, axis=0, keepdims=True)
            dsink_ref[...] += jnp.broadcast_to(part, (1, LANES))

    whole = lambda off: pl.BlockSpec((s, LANES), lambda kh, g, i: (0, off + kh))
    q_map = lambda kh, g, i: (i, kh * group + g)
    return pl.pallas_call(
        body, name=name, grid=(SWA_KV_HEADS, group, nqb),
        in_specs=[pl.BlockSpec(memory_space=pltpu.SMEM),
                  pl.BlockSpec((tb, LANES), lambda kh, g, i: (i, q_off + kh * group + g)), whole(k_off), whole(v_off),
                  pl.BlockSpec((tb, LANES), q_map), pl.BlockSpec((tb, LANES), q_map),
                  pl.BlockSpec((None, tb, 1), lambda kh, g, i: (kh * group + g, i, 0))],
        out_specs=[pl.BlockSpec((tb, LANES), q_map), whole(0), whole(0),
                   pl.BlockSpec((None, 1, LANES), lambda kh, g, i: (kh * group + g, 0, 0))],
        out_shape=[jax.ShapeDtypeStruct((s, SWA_HEADS * LANES), BF16),
                   jax.ShapeDtypeStruct((s, SWA_KV_HEADS * LANES), F32), jax.ShapeDtypeStruct((s, SWA_KV_HEADS * LANES), F32),
                   jax.ShapeDtypeStruct((SWA_HEADS, 1, LANES), F32)],
        compiler_params=_params(("parallel", "arbitrary", "arbitrary")),
    )(hp, proj, proj, proj, o, do, lse)


def _shift_down(z, k):
    rows = lax.broadcasted_iota(jnp.int32, z.shape, 0)
    return jnp.where(rows >= k, pltpu.roll(z, k, 0), 0.0)


def _shift_up(z, k):
    n = z.shape[0]
    rows = lax.broadcasted_iota(jnp.int32, z.shape, 0)
    return jnp.where(rows < n - k, pltpu.roll(z, n - k, 0), 0.0)


def _rows3(a, b, c):
    r = lax.broadcasted_iota(jnp.int32, (3, a.shape[1]), 0)
    return jnp.where(r == 0, a, jnp.where(r == 1, b, c))


def _col_spec(s, off):
    return pl.BlockSpec((s, LANES), functools.partial(lambda j, off: (0, off + j), off=off))


def _conv_fwd(proj, conv_w, *, name, s):
    def body(gb_ref, gc_ref, u_ref, w_ref, y_ref):
        w0, w1, w2 = w_ref[0:1, :], w_ref[1:2, :], w_ref[2:3, :]
        z = gc_ref[...] * u_ref[...]
        c = w2 * z + w1 * _shift_down(z, 1) + w0 * _shift_down(z, 2)
        y_ref[...] = gb_ref[...] * c

    return pl.pallas_call(
        body, name=name, grid=(2,),
        in_specs=[_col_spec(s, C_GB[0] // LANES), _col_spec(s, C_GC[0] // LANES), _col_spec(s, C_UCONV[0] // LANES),
                  pl.BlockSpec((3, LANES), lambda j: (0, j))],
        out_specs=_col_spec(s, 0), out_shape=jax.ShapeDtypeStruct((s, D_GROUP), F32),
        compiler_params=_params(("parallel",)),
    )(proj, proj, proj, conv_w)


def _conv_bwd(dy, proj, conv_w, *, name, s):
    def body(dy_ref, gb_ref, gc_ref, u_ref, w_ref, dgb_ref, dgc_ref, du_ref, dw_ref):
        w0, w1, w2 = w_ref[0:1, :], w_ref[1:2, :], w_ref[2:3, :]
        gc, u, dyv = gc_ref[...], u_ref[...], dy_ref[...]
        z = gc * u
        z1, z2 = _shift_down(z, 1), _shift_down(z, 2)
        c = w2 * z + w1 * z1 + w0 * z2
        dgb_ref[...] = (dyv * c).astype(dgb_ref.dtype)
        dc = dyv * gb_ref[...]
        dz = w2 * dc + w1 * _shift_up(dc, 1) + w0 * _shift_up(dc, 2)
        dgc_ref[...] = (dz * u).astype(dgc_ref.dtype)
        du_ref[...] = (dz * gc).astype(du_ref.dtype)
        dw_ref[...] = _rows3(jnp.sum(dc * z2, axis=0, keepdims=True), jnp.sum(dc * z1, axis=0, keepdims=True),
                             jnp.sum(dc * z, axis=0, keepdims=True))

    act = jax.ShapeDtypeStruct((s, D_GROUP), BF16)
    return pl.pallas_call(
        body, name=name, grid=(2,),
        in_specs=[_col_spec(s, 0), _col_spec(s, C_GB[0] // LANES), _col_spec(s, C_GC[0] // LANES),
                  _col_spec(s, C_UCONV[0] // LANES), pl.BlockSpec((3, LANES), lambda j: (0, j))],
        out_specs=[_col_spec(s, 0), _col_spec(s, 0), _col_spec(s, 0), pl.BlockSpec((3, LANES), lambda j: (0, j))],
        out_shape=[act, act, act, jax.ShapeDtypeStruct((3, D_GROUP), F32)],
        compiler_params=_params(("parallel",)),
    )(dy, proj, proj, proj, conv_w)


def _pool_select(j, lane, a2, a4, a8, a16):
    lo = lane < HEAD
    return jnp.where(j == 0, jnp.where(lo, a2, a4), jnp.where(lo, a8, a16))


def _pooled(u, j):
    s2 = u + _shift_down(u, 1)
    s4 = s2 + _shift_down(s2, 2)
    s8 = s4 + _shift_down(s4, 4)
    s16 = s8 + _shift_down(s8, 8)
    lane = lax.broadcasted_iota(jnp.int32, u.shape, 1)
    rows = lax.broadcasted_iota(jnp.int32, u.shape, 0)
    win = _pool_select(j, lane, 2, 4, 8, 16)
    count = jnp.minimum(rows + 1, win).astype(F32)
    return _pool_select(j, lane, s2, s4, s8, s16) / count - u, count


def _pool_fwd(proj, wbd, scale, *, name, s):
    def body(u_ref, w_ref, sc_ref, y_ref):
        pooled, _ = _pooled(u_ref[...], pl.program_id(0))
        y_ref[...] = jnp.dot(pooled.astype(BF16), w_ref[...].astype(BF16), preferred_element_type=F32) * sc_ref[...]

    return pl.pallas_call(
        body, name=name, grid=(2,),
        in_specs=[_col_spec(s, C_UPOOL[0] // LANES), pl.BlockSpec((None, LANES, LANES), lambda j: (j, 0, 0)),
                  pl.BlockSpec((1, LANES), lambda j: (0, j))],
        out_specs=_col_spec(s, 0), out_shape=jax.ShapeDtypeStruct((s, D_GROUP), F32),
        compiler_params=_params(("parallel",)),
    )(proj, wbd, scale)


def _pool_bwd(dy, proj, wbd, scale, *, name, s):
    def body(dy_ref, u_ref, w_ref, sc_ref, du_ref, dw_ref, dsc_ref):
        j = pl.program_id(0)
        pooled, count = _pooled(u_ref[...], j)
        pooled_b = pooled.astype(BF16)
        w_b = w_ref[...].astype(BF16)
        dyv = dy_ref[...]
        mixed = jnp.dot(pooled_b, w_b, preferred_element_type=F32)
        dsc_ref[...] = jnp.sum(dyv * mixed, axis=0, keepdims=True)
        dms = (dyv * sc_ref[...]).astype(BF16)
        dw_ref[...] = lax.dot_general(pooled_b, dms, (((0,), (0,)), ((), ())), preferred_element_type=F32)
        dpooled = lax.dot_general(dms, w_b, (((1,), (1,)), ((), ())), preferred_element_type=F32)
        r = dpooled / count
        a2 = r + _shift_up(r, 1)
        a4 = a2 + _shift_up(a2, 2)
        a8 = a4 + _shift_up(a4, 4)
        a16 = a8 + _shift_up(a8, 8)
        lane = lax.broadcasted_iota(jnp.int32, r.shape, 1)
        du_ref[...] = (_pool_select(j, lane, a2, a4, a8, a16) - dpooled).astype(du_ref.dtype)

    return pl.pallas_call(
        body, name=name, grid=(2,),
        in_specs=[_col_spec(s, 0), _col_spec(s, C_UPOOL[0] // LANES),
                  pl.BlockSpec((None, LANES, LANES), lambda j: (j, 0, 0)), pl.BlockSpec((1, LANES), lambda j: (0, j))],
        out_specs=[_col_spec(s, 0), pl.BlockSpec((None, LANES, LANES), lambda j: (j, 0, 0)),
                   pl.BlockSpec((1, LANES), lambda j: (0, j))],
        out_shape=[jax.ShapeDtypeStruct((s, D_GROUP), BF16), jax.ShapeDtypeStruct((2, LANES, LANES), F32),
                   jax.ShapeDtypeStruct((1, D_GROUP), F32)],
        compiler_params=_params(("parallel",)),
    )(dy, proj, wbd, scale)


def _mesh_pos():
    return lax.axis_index("x"), lax.axis_index("y"), lax.axis_index("c")


def _any_specs(n):
    return [pl.BlockSpec(memory_space=pl.ANY)] * n


def _all_gather(xs, *, name):
    n = len(xs)

    def body(*refs):
        x_refs, out_refs = refs[:n], refs[n:2 * n]
        send_sems, recv_sems, local_sems = refs[2 * n:]
        x, y, cc = _mesh_pos()
        me, sibling = (x, y, cc), (x, y, 1 - cc)
        chips = [(1 - x, y), (x, 1 - y), (1 - x, 1 - y)]

        def slot(a, px, py, pc):
            return out_refs[a].at[4 * px + 2 * py + pc]

        def copy(a, k, block, to, src=None):
            return pltpu.make_async_remote_copy(
                src_ref=slot(a, *block) if src is None else src, dst_ref=slot(a, *block), send_sem=send_sems.at[a, k],
                recv_sem=recv_sems.at[a, k], device_id=to, device_id_type=pl.DeviceIdType.MESH)

        mine = [pltpu.make_async_copy(x_refs[a], slot(a, *me), local_sems.at[a]) for a in range(n)]
        first = []
        for a in range(n):
            first.append(copy(a, 0, me, sibling, src=x_refs[a]))
            first += [copy(a, 1 + j, me, (*chip, cc), src=x_refs[a]) for j, chip in enumerate(chips)]
        for cp in mine + first:
            cp.start()
        passed = []
        for j, chip in enumerate(chips):
            for a in range(n):
                copy(a, 1 + j, (*chip, cc), me).wait_recv()
                passed.append(copy(a, 4 + j, (*chip, cc), sibling))
                passed[-1].start()
        for a in range(n):
            copy(a, 0, sibling, me).wait_recv()
        for j, chip in enumerate(chips):
            for a in range(n):
                copy(a, 4 + j, (*chip, 1 - cc), me).wait_recv()
        for cp in first + passed:
            cp.wait_send()
        for cp in mine:
            cp.wait()

    return pl.pallas_call(
        body, name=name, out_shape=[jax.ShapeDtypeStruct((N_DEV,) + a.shape, a.dtype) for a in xs],
        in_specs=_any_specs(n), out_specs=_any_specs(n),
        scratch_shapes=[pltpu.SemaphoreType.DMA((n, 7)), pltpu.SemaphoreType.DMA((n, 7)), pltpu.SemaphoreType.DMA((n,))],
    )(*xs)


def _plan_gather(src_refs, land_refs, send_sems, recv_sems):
    x, y, cc = _mesh_pos()
    me = 4 * x + 2 * y + cc
    plan = []
    for a, (src, land) in enumerate(zip(src_refs, land_refs)):
        for k in range(1, N_DEV):
            px, py, pc = (1 - x if k & 4 else x), (1 - y if k & 2 else y), (1 - cc if k & 1 else cc)
            i = a * (N_DEV - 1) + k - 1
            sems = dict(send_sem=send_sems.at[i], recv_sem=recv_sems.at[i], device_id=(px, py, pc),
                        device_id_type=pl.DeviceIdType.MESH)
            plan.append((pltpu.make_async_remote_copy(src_ref=src, dst_ref=land.at[me], **sems),
                         pltpu.make_async_remote_copy(src_ref=src, dst_ref=land.at[4 * px + 2 * py + pc], **sems)))
    return plan


def _plan_sibling(src_refs, land_refs, send_sems, recv_sems):
    x, y, cc = _mesh_pos()
    plan = []
    for a, (src, land) in enumerate(zip(src_refs, land_refs)):
        cp = pltpu.make_async_remote_copy(
            src_ref=src.at[1 - cc], dst_ref=land, send_sem=send_sems.at[a], recv_sem=recv_sems.at[a],
            device_id=(x, y, 1 - cc), device_id_type=pl.DeviceIdType.MESH)
        plan.append((cp, cp))
    return plan


def _plan_chips(src_refs, land_refs, send_sems, recv_sems):
    x, y, cc = _mesh_pos()
    my_chip = 2 * x + y
    plan = []
    for a, (src, land) in enumerate(zip(src_refs, land_refs)):
        for j, (px, py) in enumerate([(1 - x, y), (x, 1 - y), (1 - x, 1 - y)]):
            peer = 2 * px + py
            sems = dict(send_sem=send_sems.at[3 * a + j], recv_sem=recv_sems.at[3 * a + j], device_id=(px, py, cc),
                        device_id_type=pl.DeviceIdType.MESH)
            plan.append((pltpu.make_async_remote_copy(src_ref=src.at[peer], dst_ref=land.at[my_chip], **sems),
                         pltpu.make_async_remote_copy(src_ref=src.at[peer], dst_ref=land.at[peer], **sems)))
    return plan


HBM_SPEC = pl.BlockSpec(memory_space=pltpu.HBM)
SEM_SPEC = pl.BlockSpec(memory_space=pltpu.SEMAPHORE)
ANY_SPEC = pl.BlockSpec(memory_space=pl.ANY)
SIDE_EFFECT = pltpu.CompilerParams(has_side_effects=pltpu.SideEffectType.DATAFLOW_SIDE_EFFECTING)


def _start_copies(plan, sems_per_array, srcs, land_shapes, after, *, name):
    n, n_after = len(srcs), len(after)

    def body(*refs):
        send_sems, recv_sems = refs[2 * n + n_after], refs[2 * n + n_after + 1]
        for out, _ in plan(refs[:n], refs[n:2 * n], send_sems, recv_sems):
            out.start()
        refs[-1][...] = jnp.zeros_like(refs[-1])

    lands = [lax.empty(shp, a.dtype) for shp, a in zip(land_shapes, srcs)]
    sem = pltpu.SemaphoreType.DMA((n * sems_per_array,))
    res = pl.pallas_call(
        body, name=name,
        out_shape=(sem, sem, *[pltpu.HBM(a.shape, a.dtype) for a in srcs + lands], jax.ShapeDtypeStruct((8, LANES), F32)),
        in_specs=[HBM_SPEC] * (2 * n) + [ANY_SPEC] * n_after,
        out_specs=(SEM_SPEC, SEM_SPEC, *[HBM_SPEC] * (2 * n), pl.BlockSpec(memory_space=pltpu.VMEM)),
        input_output_aliases={i: 2 + i for i in range(2 * n)}, compiler_params=SIDE_EFFECT,
    )(*[pltpu.with_memory_space_constraint(a, pltpu.HBM) for a in srcs + lands], *after)
    return (res[0], res[1], list(res[2:2 + n]), list(res[2 + n:2 + 2 * n])), res[-1]


def _wait_copies(plan, handle, after, *, name):
    send, recv, srcs, lands = handle
    n = len(srcs)

    def body(*refs):
        for out, inc in plan(refs[:n], refs[n:2 * n], refs[2 * n], refs[2 * n + 1]):
            out.wait_send()
            inc.wait_recv()

    res = pl.pallas_call(
        body, name=name, out_shape=tuple(pltpu.HBM(a.shape, a.dtype) for a in srcs + lands),
        in_specs=[HBM_SPEC] * (2 * n) + [SEM_SPEC, SEM_SPEC] + [ANY_SPEC] * len(after), out_specs=[HBM_SPEC] * (2 * n),
        input_output_aliases={i: i for i in range(2 * n)}, compiler_params=SIDE_EFFECT,
    )(*srcs, *lands, send, recv, *after)
    return list(res[:n]), list(res[n:])


def _row_tile(rows, target=512):
    if rows <= target:
        return rows
    best = None
    for t in range(8, target + 1, 8):
        if rows % t == 0:
            best = t
    assert best is not None, (rows, target)
    return best


def _add_own(g, other, core, *, name):
    _, _, rows, cols = g.shape
    tm = _row_tile(rows)

    def body(c_ref, g_ref, o_ref, out_ref):
        out_ref[...] = g_ref[...] + o_ref[...]

    return pl.pallas_call(
        body, name=name, out_shape=jax.ShapeDtypeStruct(other.shape, other.dtype),
        grid_spec=pltpu.PrefetchScalarGridSpec(
            num_scalar_prefetch=1, grid=(4, rows // tm),
            in_specs=[pl.BlockSpec((None, None, tm, cols), lambda p, i, c_ref: (c_ref[0], p, i, 0)),
                      pl.BlockSpec((None, tm, cols), lambda p, i, c_ref: (p, i, 0))],
            out_specs=pl.BlockSpec((None, tm, cols), lambda p, i, c_ref: (p, i, 0))),
        compiler_params=_params(("parallel", "parallel")),
    )(core, g, other)


def _adamw(parts, w, m, v, *, name):
    layers, rows, cols = w.shape
    assert len(parts) == layers
    tm = _row_tile(rows, 256)
    nr = rows // tm

    def body(*refs):
        p_refs = refs[:layers]
        w_ref, m_ref, v_ref, g_ref, d_ref, nm_ref, nv_ref, g_s = refs[layers:]
        for ll in range(layers):
            @pl.when(pl.program_id(0) == ll)
            def _(ll=ll):
                g = p_refs[ll][0]
                for q in range(1, p_refs[ll].shape[0]):
                    g = g + p_refs[ll][q]
                g_s[...] = g

        g = g_s[...]
        mm = ADAM_B1 * m_ref[...] + (1.0 - ADAM_B1) * g
        vv = ADAM_B2 * v_ref[...] + (1.0 - ADAM_B2) * jnp.square(g)
        m_hat = mm / (1.0 - ADAM_B1 ** ADAM_STEP)
        v_hat = vv / (1.0 - ADAM_B2 ** ADAM_STEP)
        g_ref[...] = g
        d_ref[...] = -ADAM_LR * (m_hat / (jnp.sqrt(v_hat) + ADAM_EPS) + ADAM_WD * w_ref[...])
        nm_ref[...] = mm
        nv_ref[...] = vv

    def part_spec(ll, p):
        return pl.BlockSpec((p, tm, cols), lambda l, i: (0, jnp.where(l == ll, i, jnp.where(l < ll, 0, nr - 1)), 0))

    spec = pl.BlockSpec((None, tm, cols), lambda l, i: (l, i, 0))
    out = jax.ShapeDtypeStruct(w.shape, F32)
    return pl.pallas_call(
        body, name=name, grid=(layers, nr),
        in_specs=[part_spec(ll, parts[ll].shape[0]) for ll in range(layers)] + [spec] * 3,
        out_specs=[spec] * 4, out_shape=[out] * 4, scratch_shapes=[pltpu.VMEM((tm, cols), F32)],
        compiler_params=_params(("arbitrary", "arbitrary")),
    )(*parts, w, m, v)


def _pack(arrs):
    flat = jnp.concatenate([a.reshape(-1) for a in arrs])
    rows = -(-flat.shape[0] // (PACK_COLS * 16)) * 16
    return jnp.pad(flat, (0, rows * PACK_COLS - flat.shape[0])).reshape(rows, PACK_COLS)


def _unpack(packed, shapes):
    flat = packed.reshape(-1)
    out, off = [], 0
    for shp in shapes:
        n = int(np.prod(shp))
        out.append(flat[off:off + n].reshape(shp))
        off += n
    return out


def _shards_to_full(g, axis):
    if axis == 0:
        return g.reshape(g.shape[0] * g.shape[1], g.shape[2])
    return jnp.transpose(g, (1, 0, 2)).reshape(g.shape[1], g.shape[0] * g.shape[2])


def _full_to_shards(a, axis):
    if axis == 0:
        return jnp.transpose(a.reshape(4, 2, a.shape[0] // N_DEV, a.shape[1]), (1, 0, 2, 3))
    return jnp.transpose(a.reshape(a.shape[0], 4, 2, a.shape[1] // N_DEV), (2, 1, 0, 3))


def _zeros_like_cols(a, n):
    return jnp.zeros(a.shape[:-1] + (n,), a.dtype)


def _pad_heads(a, n):
    z = _zeros_like_cols(a, HEAD)
    return jnp.concatenate([p for h in range(n) for p in (a[..., h * HEAD:(h + 1) * HEAD], z)], axis=-1)


def _unpad_heads(a, n):
    return jnp.concatenate([a[..., h * LANES:h * LANES + HEAD] for h in range(n)], axis=-1)


def _seg(first, width, sign=1):
    return (width, [(first, sign)])


def _zero(width):
    return (width, [])


def _swapped(first):
    half = MLA_ROPE // 2
    return [_seg(first + half, half, -1), _seg(first, half)]


def _padded_heads(first, n):
    return [s for h in range(n) for s in (_seg(first + HEAD * h, HEAD), _zero(HEAD))]


def _layout_w_in():
    kr = 384
    return (_padded_heads(1440, 4) + [_seg(0, 256), _seg(416, 256), _seg(672, 256), _seg(928, 256), _seg(1184, 256)]
            + _padded_heads(1696, 2) + _padded_heads(1824, 2) + [_seg(256, 128)]
            + [_zero(HEAD), _seg(kr, MLA_ROPE), _seg(kr, MLA_ROPE)] + [_zero(HEAD)] + _swapped(kr) + _swapped(kr))


def _layout_w_uq():
    out = []
    for h in range(MLA_HEADS):
        out += [_seg(96 * h, MLA_NOPE), _seg(96 * h + MLA_NOPE, MLA_ROPE)] + _swapped(96 * h + MLA_NOPE)
    return out


def _layout_w_ukv():
    keys = [s for h in range(MLA_HEADS) for s in (_seg(LANES * h, HEAD), _zero(HEAD))]
    values = [s for h in range(MLA_HEADS) for s in (_seg(LANES * h + HEAD, HEAD), _zero(HEAD))]
    return keys + values


def _layout_w_gate_up():
    return [_seg(half + j, GU_TILE) for j in range(0, D_FF, GU_TILE) for half in (0, D_FF)]


LAYOUTS = dict(w_in=_layout_w_in(), w_uq=_layout_w_uq(), w_ukv=_layout_w_ukv(), w_gate_up=_layout_w_gate_up())
OWN_COLS = dict(w_in=1952, w_uq=384, w_ukv=512, w_gate_up=2 * D_FF)


def _plan_extend(layout, shard):
    plan = []
    for width, terms in layout:
        if not terms:
            plan.append((width, []))
            continue
        (first, sign), = terms
        while width:
            g, off = divmod(first, shard)
            w = min(width, shard - off)
            plan.append((w, [(g, off, sign)]))
            first, width = first + w, width - w
    return [plan]


def _plan_fold(layout, own_cols):
    sources = [[] for _ in range(own_cols)]
    e = 0
    for width, terms in layout:
        for first, sign in terms:
            for i in range(width):
                sources[first + i].append((e + i, sign))
        e += width
    shard = own_cols // N_DEV
    plans = {}
    for g in range(N_DEV):
        plan, n = [], g * shard
        while n < (g + 1) * shard:
            w = 1
            while n + w < (g + 1) * shard and [(c + w, sg) for c, sg in sources[n]] == sources[n + w]:
                w += 1
            plan.append((w, [(0, c, sg) for c, sg in sources[n]]))
            n += w
        plans[g] = plan
    return [plans[2 * p + c] for c in range(2) for p in range(4)]


def _assemble(src, plans, out_cols, out_dtype, *, name):
    g, rows, c = src.shape
    tm = _row_tile(rows, 256)

    def body(s_ref, o_ref):
        blocks = [s_ref[i].astype(F32) for i in range(g)]
        for d, plan in enumerate(plans):
            pieces = []
            for width, terms in plan:
                v = None
                for b, first, sign in terms:
                    t = blocks[b][:, first:first + width]
                    t = -t if sign < 0 else t
                    v = t if v is None else v + t
                pieces.append(jnp.zeros((tm, width), F32) if v is None else v)
            o_ref[d] = (pieces[0] if len(pieces) == 1 else jnp.concatenate(pieces, axis=1)).astype(o_ref.dtype)

    return pl.pallas_call(
        body, name=name, grid=(rows // tm,), in_specs=[pl.BlockSpec((g, tm, c), lambda i: (0, i, 0))],
        out_specs=pl.BlockSpec((len(plans), tm, out_cols), lambda i: (0, i, 0)),
        out_shape=jax.ShapeDtypeStruct((len(plans), rows, out_cols), out_dtype), compiler_params=_params(("parallel",)),
    )(src)


def _extend(nm, gathered, *, name):
    layout = LAYOUTS[nm]
    return _assemble(gathered, _plan_extend(layout, OWN_COLS[nm] // N_DEV), sum(w for w, _ in layout), BF16, name=name)[0]


def _fold_to_shards(nm, grad_ext, *, name):
    shards = _assemble(grad_ext[None], _plan_fold(LAYOUTS[nm], OWN_COLS[nm]), OWN_COLS[nm] // N_DEV, F32, name=name)
    return shards.reshape((2, 4) + shards.shape[1:])


def _rope_tables(s):
    inv = 1.0 / (ROPE_THETA ** (jnp.arange(0, MLA_ROPE, 2, dtype=F32) / MLA_ROPE))
    ang = jnp.arange(s, dtype=F32)[:, None] * inv[None, :]
    cos, sin = jnp.cos(ang), jnp.sin(ang)
    c32, s32 = jnp.concatenate([cos, cos], axis=1), jnp.concatenate([sin, sin], axis=1)
    zeros, ones = jnp.zeros((s, HEAD), F32), jnp.ones((s, HEAD), F32)
    tq = jnp.concatenate([ones, c32, s32], axis=1) * (1.0 / math.sqrt(MLA_NOPE + MLA_ROPE))
    return (jnp.tile(tq, (1, MLA_HEADS)), jnp.concatenate([zeros, c32, c32], axis=1),
            jnp.concatenate([zeros, s32, s32], axis=1))


def _gn(y):
    return y * _rstd(y, D_GROUP)


def _mixer_fwd(x, w, tabs, l):
    s = x.shape[0]
    tq, tkc, tks = tabs
    n = lambda t: f"l{l}_{t}"
    h = _rms_fwd(("row", x, D_MODEL, 0), w["attn_norm"], name=n("attn_norm"), rows=s, width=D_MODEL)
    proj = _mm(h, w["w_in"], name=n("in_proj"))
    cqn = _rms_fwd(("row", proj, 256, C_CQ[0] // 256), w["mla_q_norm"], name=n("q_norm"), rows=s, width=256)
    ckvn = _rms_fwd(("row", proj, 128, C_CKV[0] // 128), w["mla_kv_norm"], name=n("kv_norm"), rows=s, width=128)
    q_ext = _mm(cqn, w["w_uq"], name=n("uq"))
    kv_ext = _mm(ckvn, w["w_ukv"], name=n("ukv"))

    def prep(qe, kvk, kvv, kr, krp, tqv, tc, ts):
        kb = kr * tc + krp * ts
        lane = lax.broadcasted_iota(jnp.int32, kvv.shape, 1) & (LANES - 1)
        return qe * tqv, kvk + jnp.tile(kb, (1, MLA_HEADS)), jnp.where(lane == HEAD, 1.0, kvv)

    qm, km, vm = _rowwise(
        prep, name=n("mla_prep"), rows=s,
        ins=[("row", q_ext, 512, 0), ("row", kv_ext, 512, 0), ("row", kv_ext, 512, 1),
             ("row", proj, 128, C_KR[0] // 128), ("row", proj, 128, C_KRP[0] // 128),
             ("row", tq, 512, 0), ("row", tkc, 128, 0), ("row", tks, 128, 0)],
        outs=[(512, BF16)] * 3)
    y_a, lse_a = _mla_fwd(qm, km, vm, name=n("mla_fwd"), s=s, nh=MLA_HEADS)
    y_b = _conv_fwd(proj, w["conv_w"], name=n("conv_fwd"), s=s)
    y_c = _pool_fwd(proj, w["pool_wbd"], w["pool_scale"], name=n("pool_fwd"), s=s)
    y_d, lse_d = _swa_fwd(proj, w["hp_swa"], name=n("swa_fwd"), s=s, scale=1.0 / math.sqrt(HEAD))

    def mix(ya, yb, yc, yd, mn):
        return (jnp.concatenate([_gn(_unpad_heads(ya, 4)), _gn(yb), _gn(yc), _gn(_unpad_heads(yd, 4))], axis=1) * mn,)

    mixed = _rowwise(mix, name=n("group_norm"), rows=s,
                     ins=[("row", y_a, 512, 0), ("row", y_b, 256, 0), ("row", y_c, 256, 0), ("row", y_d, 512, 0),
                          ("full", w["mix_norm"])], outs=[(D_MODEL, BF16)])[0]
    x1 = _mm(mixed, w["w_o"], res=x, name=n("out_proj"))
    saved = dict(x=x, h=h, proj=proj, cqn=cqn, ckvn=ckvn, qm=qm, km=km, vm=vm, y_a=y_a, lse_a=lse_a, y_b=y_b, y_c=y_c,
                 y_d=y_d, lse_d=lse_d, mixed=mixed)
    return x1, saved


def _ffn_fwd(x1, w, l):
    s = x1.shape[0]
    n = lambda t: f"l{l}_{t}"
    h2 = _rms_fwd(("row", x1, D_MODEL, 0), w["ffn_norm"], name=n("ffn_norm"), rows=s, width=D_MODEL)

    def swiglu(gu):
        g, u = gu[:, :GU_TILE], gu[:, GU_TILE:]
        return gu, g * jax.nn.sigmoid(g) * u

    gu, act = _mm(h2, w["w_gate_up"], tn=2 * GU_TILE, name=n("gate_up"),
                  epilogue=(swiglu, [], [(2 * D_FF, BF16), (D_FF, BF16)]))
    x2 = _mm(act, w["w_down"], res=x1, tk=D_FF // 2, name=n("down"))
    return x2, dict(x1=x1, h2=h2, gu=gu, act=act)


def _ffn_bwd_down(dx2, sv, w, l):
    n = lambda t: f"l{l}_{t}"

    def swiglu_bwd(da, gu):
        gt, u = gu[:, :GU_TILE].astype(F32), gu[:, GU_TILE:].astype(F32)
        sg = jax.nn.sigmoid(gt)
        return (jnp.concatenate([da * u * sg * (1.0 + gt * (1.0 - sg)), da * gt * sg], axis=1),)

    dgu = _mm(dx2[1], w["w_down"], tb=True, tm=2048, tn=GU_TILE, name=n("d_act"),
              epilogue=(swiglu_bwd, [sv["gu"]], [(2 * D_FF, BF16)]))[0]
    g = dict(w_down=_mm(sv["act"], dx2[1], ta=True, tm=D_FF // 2, name=n("dw_down")))
    return dgu, g


def _ffn_bwd_up(dx2, dgu, sv, w, l):
    s = dgu.shape[0]
    n = lambda t: f"l{l}_{t}"
    dh2 = _mm(dgu, w["w_gate_up"], tb=True, tk=D_FF // 2, name=n("d_h2"))
    g = dict(w_gate_up=_mm(sv["h2"], dgu, ta=True, tn=D_FF // 2, name=n("dw_gate_up")))
    dx1, dx1_b, g["ffn_norm"] = _rms_bwd(("row", sv["x1"], D_MODEL, 0), w["ffn_norm"], dh2, dx2[0],
                                         name=n("ffn_norm_bwd"), rows=s, width=D_MODEL, out_dtypes=(F32, BF16))
    return (dx1, dx1_b), g


def _mixer_bwd_out(dx1, sv, w, l):
    s = dx1[1].shape[0]
    n = lambda t: f"l{l}_{t}"
    dmixed = _mm(dx1[1], w["w_o"], tb=True, name=n("d_mixed"))
    g = dict(w_o=_mm(sv["mixed"], dx1[1], ta=True, name=n("dw_o")))

    def mix_bwd(dm, ya, yb, yc, yd, mn):
        outs, dmn = [], []
        for i, y in enumerate((_unpad_heads(ya, 4), yb, yc, _unpad_heads(yd, 4))):
            lo, hi = i * D_GROUP, (i + 1) * D_GROUP
            r = _rstd(y, D_GROUP)
            nrm = y * r
            dmg = dm[:, lo:hi]
            dn = dmg * mn[:, lo:hi]
            dy = r * (dn - nrm * (jnp.sum(dn * nrm, axis=-1, keepdims=True) * (1.0 / D_GROUP)))
            outs.append(_pad_heads(dy, 4) if i in (0, 3) else dy)
            dmn.append(jnp.sum(dmg * nrm, axis=0, keepdims=True))
        return (*outs, jnp.concatenate(dmn, axis=1))

    dy_a, dy_b, dy_c, dy_d, g["mix_norm"] = _rowwise(
        mix_bwd, name=n("group_norm_bwd"), rows=s,
        ins=[("row", dmixed, D_MODEL, 0), ("row", sv["y_a"], 512, 0), ("row", sv["y_b"], 256, 0),
             ("row", sv["y_c"], 256, 0), ("row", sv["y_d"], 512, 0), ("full", w["mix_norm"])],
        outs=[(512, F32), (256, F32), (256, F32), (512, F32)], reds=[(1, D_MODEL)])
    return (dy_a, dy_b, dy_c, dy_d), g


def _mixer_bwd_in(dx1, dys, sv, w, tabs, l):
    s = dx1[0].shape[0]
    tq, tkc, tks = tabs
    n = lambda t: f"l{l}_{t}"
    dy_a, dy_b, dy_c, dy_d = dys
    g = {}

    proj = sv["proj"]
    dq_sw, dk_sw, dv_sw, dsink = _swa_bwd(proj, sv["y_d"], dy_d, sv["lse_d"], w["hp_swa"], name=n("swa_bwd"), s=s,
                                          scale=1.0 / math.sqrt(HEAD))
    g["swa_sinks"] = dsink[:, 0, 0]

    qkv = (sv["qm"], sv["km"], sv["vm"])
    dqm, dvec_a = _mla_dq(*qkv, sv["y_a"], dy_a, sv["lse_a"], w["hp_swa"], name=n("mla_dq"), s=s, nh=MLA_HEADS)
    dkm, dvm = _mla_dkv(*qkv, dy_a, sv["lse_a"].reshape(MLA_HEADS, 1, s), dvec_a.reshape(MLA_HEADS, 1, s),
                        name=n("mla_dkv"), s=s, nh=MLA_HEADS)

    def prep_bwd(dq, dk, tqv, tc, ts):
        dkb = dk[:, 0:128] + dk[:, 128:256] + dk[:, 256:384] + dk[:, 384:512]
        return dq * tqv, dk, dkb * tc, dkb * ts

    dq_ext, dkvk, dkr, dkrp = _rowwise(
        prep_bwd, name=n("mla_prep_bwd"), rows=s,
        ins=[("row", dqm, 512, 0), ("row", dkm, 512, 0), ("row", tq, 512, 0), ("row", tkc, 128, 0), ("row", tks, 128, 0)],
        outs=[(512, BF16), (512, BF16), (128, BF16), (128, BF16)])
    dkv_ext = jnp.concatenate([dkvk, dvm], axis=1)
    dcqn = _mm(dq_ext, w["w_uq"], tb=True, name=n("d_cqn"))
    g["w_uq"] = _mm(sv["cqn"], dq_ext, ta=True, name=n("dw_uq"))
    dckvn = _mm(dkv_ext, w["w_ukv"], tb=True, name=n("d_ckvn"))
    g["w_ukv"] = _mm(sv["ckvn"], dkv_ext, ta=True, name=n("dw_ukv"))
    dcq, g["mla_q_norm"] = _rms_bwd(("row", proj, 256, C_CQ[0] // 256), w["mla_q_norm"], dcqn, None,
                                    name=n("q_norm_bwd"), rows=s, width=256, out_dtypes=(BF16,))
    dckv, g["mla_kv_norm"] = _rms_bwd(("row", proj, 128, C_CKV[0] // 128), w["mla_kv_norm"], dckvn, None,
                                      name=n("kv_norm_bwd"), rows=s, width=128, out_dtypes=(BF16,))

    dgb, dgc, duc, g["conv_w"] = _conv_bwd(dy_b, proj, w["conv_w"], name=n("conv_bwd"), s=s)
    dup, g["pool_wbd"], g["pool_scale"] = _pool_bwd(dy_c, proj, w["pool_wbd"], w["pool_scale"], name=n("pool_bwd"), s=s)

    dproj = jnp.concatenate([dq_sw, dcq, dgb, dgc, duc, dup, dk_sw.astype(BF16), dv_sw.astype(BF16), dckv, dkr, dkrp],
                            axis=1)
    dh = _mm(dproj, w["w_in"], tb=True, name=n("d_h"))
    g["w_in"] = _mm(sv["h"], dproj, ta=True, name=n("dw_in"))
    dx0, dx0_b, g["attn_norm"] = _rms_bwd(("row", sv["x"], D_MODEL, 0), w["attn_norm"], dh, dx1[0],
                                          name=n("attn_norm_bwd"), rows=s, width=D_MODEL, out_dtypes=(F32, BF16))
    return (dx0, dx0_b), g


def _loss_head(x, target, g, *, s):
    def fn(xv, tv, gv):
        r = _rstd(xv, D_MODEL)
        e = xv * r * gv - tv
        part = jnp.sum(jnp.sum(e * e, axis=1, keepdims=True), axis=0, keepdims=True) * (0.5 / D_MODEL)
        dy = e * (1.0 / D_MODEL)
        dyg = dy * gv
        dx = r * dyg - xv * (r * r * r) * (jnp.sum(dyg * xv, axis=-1, keepdims=True) * (1.0 / D_MODEL))
        return dx, dx, jnp.sum(dy * xv * r, axis=0, keepdims=True), jnp.broadcast_to(part, (1, LANES))

    return _rowwise(fn, name="loss_head", rows=s,
                    ins=[("row", x, D_MODEL, 0), ("row", target, D_MODEL, 0), ("full", g)],
                    outs=[(D_MODEL, F32), (D_MODEL, BF16)], reds=[(1, D_MODEL), (1, LANES)])


def _alibi_slopes(n):
    return np.asarray([2.0 ** (-8.0 * (i + 1) / n) for i in range(n)], dtype=np.float32)


MIXER_WEIGHTS = ("w_in", "w_uq", "w_ukv", "conv_w", "w_o")
FFN_WEIGHTS = ("w_gate_up", "w_down")


def _mixer_weights(full, rep, l):
    pw = rep["pool_w"][l]
    z = jnp.zeros((HEAD, HEAD), F32)
    wbd = jnp.stack([jnp.block([[pw[2 * j], z], [z, pw[2 * j + 1]]]) for j in range(2)])
    return dict(
        attn_norm=rep["attn_norm"][l][None], w_in=full["w_in"], mla_q_norm=rep["mla_q_norm"][l][None],
        w_uq=full["w_uq"], mla_kv_norm=rep["mla_kv_norm"][l][None], w_ukv=full["w_ukv"],
        conv_w=full["conv_w"], pool_wbd=wbd, pool_scale=rep["pool_scale"][l][None],
        mix_norm=rep["mix_norm"][l][None], w_o=full["w_o"],
        hp_swa=jnp.stack([jnp.asarray(_alibi_slopes(SWA_HEADS)), rep["swa_sinks"][l]], axis=1))


def _ffn_weights(full, rep, l):
    return dict(ffn_norm=rep["ffn_norm"][l][None], w_gate_up=full["w_gate_up"], w_down=full["w_down"])


def _small_grads(g):
    rows = ("attn_norm", "mla_q_norm", "mla_kv_norm", "pool_scale", "ffn_norm", "mix_norm")
    out = {nm: g[nm][0] for nm in rows if nm in g}
    if "swa_sinks" in g:
        out["swa_sinks"] = g["swa_sinks"]
    if "pool_wbd" in g:
        e = g["pool_wbd"]
        out["pool_w"] = jnp.stack([e[j // 2][HEAD * (j % 2):HEAD * (j % 2 + 1), HEAD * (j % 2):HEAD * (j % 2 + 1)]
                                   for j in range(4)])
    return out


def kernel(x, attn_norm, w_in, mla_q_norm, w_uq, mla_kv_norm, w_ukv, conv_w, pool_w, pool_scale, swa_sinks, mix_norm, w_o, ffn_norm, w_gate_up, w_down, final_norm, loss_target, m_attn_norm, m_w_in, m_mla_q_norm, m_w_uq, m_mla_kv_norm, m_w_ukv, m_conv_w, m_pool_w, m_pool_scale, m_swa_sinks, m_mix_norm, m_w_o, m_ffn_norm, m_w_gate_up, m_w_down, m_final_norm, v_attn_norm, v_w_in, v_mla_q_norm, v_w_uq, v_mla_kv_norm, v_w_ukv, v_conv_w, v_pool_w, v_pool_scale, v_swa_sinks, v_mix_norm, v_w_o, v_ffn_norm, v_w_gate_up, v_w_down, v_final_norm):
    given = dict(locals())
    sh_names = [nm for nm, _, _ in SHARDED]
    sh_axis = {nm: ax - 1 for nm, _, ax in SHARDED}
    rep_names = [nm for nm, _ in REPLICATED]
    rep_shapes = [shp for _, shp in REPLICATED]
    rep = {nm: given[nm] for nm in rep_names if nm != "loss"}
    me = 4 * lax.axis_index("x") + 2 * lax.axis_index("y") + lax.axis_index("c")
    my_chip = 2 * lax.axis_index("x") + lax.axis_index("y")
    core = lax.axis_index("c").astype(jnp.int32).reshape(1)

    def behind(token, a):
        return a + token[0, 0].astype(a.dtype)

    def wire(nm, l):
        if nm == "conv_w":
            return lax.bitcast_convert_type(given[nm][l], BF16).reshape(3, -1)
        return given[nm][l].astype(BF16)

    def whole(nm, g, tag):
        if nm in LAYOUTS:
            return _extend(nm, g, name=f"extend_{nm}_{tag}")
        if nm == "conv_w":
            g = lax.bitcast_convert_type(g.reshape(N_DEV, 3, -1, 2), F32)
        return _shards_to_full(g, sh_axis[nm])

    def gather_start(names, l, after, tag):
        srcs = [wire(nm, l) for nm in names]
        return _start_copies(_plan_gather, N_DEV - 1, srcs, [(N_DEV,) + a.shape for a in srcs], after, name=f"start_gather_{tag}")

    def gather_wait(names, handle, after, tag):
        srcs, lands = _wait_copies(_plan_gather, handle, after, name=f"wait_gather_{tag}")
        return {nm: whole(nm, lax.dynamic_update_index_in_dim(land, src, me, 0), tag)
                for nm, src, land in zip(names, srcs, lands)}

    got = _all_gather([wire(nm, 0) for nm in MIXER_WEIGHTS], name="gather_mixer0")
    full_m0 = {nm: whole(nm, g, "mixer0") for nm, g in zip(MIXER_WEIGHTS, got)}
    h_f0, tok = gather_start(FFN_WEIGHTS, 0, [], "ffn0")
    h_l1, tok = gather_start(MIXER_WEIGHTS + FFN_WEIGHTS, 1, [tok], "layer1")

    xs, target = x[0], loss_target[0]
    s = xs.shape[0]
    tabs = _rope_tables(s)
    wm, wf, svm, svf = [None] * DEPTH, [None] * DEPTH, [None] * DEPTH, [None] * DEPTH
    wm[0] = _mixer_weights(full_m0, rep, 0)
    wm[0]["attn_norm"] = behind(tok, wm[0]["attn_norm"])
    x1, svm[0] = _mixer_fwd(xs, wm[0], tabs, 0)
    wf[0] = _ffn_weights(gather_wait(FFN_WEIGHTS, h_f0, [x1], "ffn0"), rep, 0)
    x2, svf[0] = _ffn_fwd(x1, wf[0], 0)
    full_1 = gather_wait(MIXER_WEIGHTS + FFN_WEIGHTS, h_l1, [x2], "layer1")
    wm[1], wf[1] = _mixer_weights(full_1, rep, 1), _ffn_weights(full_1, rep, 1)
    x1, svm[1] = _mixer_fwd(x2, wm[1], tabs, 1)
    x2, svf[1] = _ffn_fwd(x1, wf[1], 1)
    dx_f, dx_b, d_final, loss = _loss_head(x2, target, rep["final_norm"][None], s=s)
    dx = (dx_f, dx_b)

    parts = {}

    def reduce_start(grads, l, after, tag):
        names = [nm for nm in sh_names if nm in grads]
        mine = [_fold_to_shards(nm, grads[nm], name=f"fold_{nm}_{l}") if nm in LAYOUTS
                else _full_to_shards(grads[nm], sh_axis[nm]) for nm in names]
        handle, token = _start_copies(_plan_sibling, 1, mine, [m.shape[1:] for m in mine], after, name=f"start_sibling_{tag}")
        return (names, l, handle), token

    def reduce_mid(state, after, tag):
        names, l, handle = state
        mine, theirs = _wait_copies(_plan_sibling, handle, after, name=f"wait_sibling_{tag}")
        sums = [_add_own(g, o, core, name=f"chip_sum_{nm}_{l}") for nm, g, o in zip(names, mine, theirs)]
        handle, token = _start_copies(_plan_chips, 3, sums, [a.shape for a in sums], [], name=f"start_chips_{tag}")
        return (names, l, handle), token

    def reduce_end(state, after, tag):
        names, l, handle = state
        sums, lands = _wait_copies(_plan_chips, handle, after, name=f"wait_chips_{tag}")
        for nm, own, land in zip(names, sums, lands):
            parts[nm, l] = lax.dynamic_update_index_in_dim(land, lax.dynamic_index_in_dim(own, my_chip, 0, keepdims=False),
                                                           my_chip, 0)

    small = [None] * DEPTH
    in_flight = []
    pending = None
    for l in reversed(range(DEPTH)):
        dgu, g_down = _ffn_bwd_down(dx, svf[l], wf[l], l)
        if pending is not None:
            state, token = reduce_mid(pending, [dgu], f"mixer{l + 1}")
            in_flight.append((state, f"mixer{l + 1}"))
            wf[l]["ffn_norm"] = behind(token, wf[l]["ffn_norm"])
        dx1, g_up = _ffn_bwd_up(dx, dgu, svf[l], wf[l], l)
        g_ffn = {**g_down, **g_up}
        state, token = reduce_start(g_ffn, l, [], f"ffn{l}")
        wm[l]["mix_norm"] = behind(token, wm[l]["mix_norm"])
        dys, g_out = _mixer_bwd_out(dx1, svm[l], wm[l], l)
        state, token = reduce_mid(state, [dys[0]], f"ffn{l}")
        in_flight.append((state, f"ffn{l}"))
        wm[l]["hp_swa"] = behind(token, wm[l]["hp_swa"])
        dx, g_in = _mixer_bwd_in(dx1, dys, svm[l], wm[l], tabs, l)
        g_mixer = {**g_out, **g_in}
        small[l] = _small_grads({**g_ffn, **g_mixer})
        pending, token = reduce_start(g_mixer, l, [], f"mixer{l}")
    state, token = reduce_mid(pending, [dx[0]], "mixer0")
    in_flight.append((state, "mixer0"))
    for state, tag in in_flight:
        reduce_end(state, [], tag)
    grad_x = dx[0]

    sh_out = {nm: _adamw([parts[nm, l] for l in range(DEPTH)], given[nm], given["m_" + nm], given["v_" + nm],
                         name=f"adamw_{nm}") for nm in sh_names}

    grads = {nm: jnp.stack([small[l][nm] for l in range(DEPTH)]) for nm in rep_names if nm in small[0]}
    grads["final_norm"] = d_final[0]
    grads["loss"] = loss[0, :1]
    zero = jnp.zeros((1,), F32)
    small = _all_gather([_pack([grads[nm] for nm in rep_names])], name="gather_small_grads")
    packs = [_pack([given.get(pre + nm, zero) for nm in rep_names])[None] for pre in ("", "m_", "v_")]
    rep_out = [dict(zip(rep_names, _unpack(o[0], rep_shapes))) for o in _adamw(small, *packs, name="adamw_replicated")]

    out = [rep_out[0]["loss"][0], grad_x[None]]
    for i in range(4):
        out += [sh_out[nm][i] if nm in sh_axis else rep_out[i][nm] for nm in WEIGHT_ORDER]
    return tuple(out)
```

```python
import functools
import math

import numpy as np
import jax
import jax.numpy as jnp
from jax import lax
from jax.experimental import pallas as pl
from jax.experimental.pallas import tpu as pltpu

F32 = jnp.float32
BF16 = jnp.bfloat16

D_MODEL = 1024
DEPTH = 2
D_GROUP = 256
MLA_HEADS = 4
MLA_NOPE = 64
MLA_ROPE = 32
ROPE_THETA = 10000.0
POOL_WINDOWS = (2, 4, 8, 16)
SWA_HEADS = 4
SWA_KV_HEADS = 2
SWA_WINDOW = 128
D_FF = 2816
GU_TILE = 256
RMS_EPS = 1e-6
LANES = 128
HEAD = 64
VMEM_LIMIT = 48 * 1024 * 1024
NEG = -1e30

ADAM_LR = 0.001
ADAM_B1 = 0.9
ADAM_B2 = 0.999
ADAM_EPS = 1e-08
ADAM_WD = 0.01
ADAM_STEP = 10

N_DEV = 8
PACK_COLS = 1024

C_QSW, C_CQ, C_GB, C_GC, C_UCONV, C_UPOOL = (0, 512), (512, 256), (768, 256), (1024, 256), (1280, 256), (1536, 256)
C_KSW, C_VSW, C_CKV, C_KR, C_KRP = (1792, 256), (2048, 256), (2304, 128), (2432, 128), (2560, 128)
D_IN_EXT = 2688

SHARDED = (("w_in", (DEPTH, 1024, 244), 2), ("w_uq", (DEPTH, 256, 48), 2), ("w_ukv", (DEPTH, 128, 64), 2),
           ("conv_w", (DEPTH, 3, 32), 2), ("w_o", (DEPTH, 128, 1024), 1), ("w_gate_up", (DEPTH, 1024, 704), 2),
           ("w_down", (DEPTH, 352, 1024), 1))
REPLICATED = (("attn_norm", (DEPTH, 1024)), ("mla_q_norm", (DEPTH, 256)), ("mla_kv_norm", (DEPTH, 128)),
              ("pool_w", (DEPTH, 4, 64, 64)), ("pool_scale", (DEPTH, 256)), ("swa_sinks", (DEPTH, 4)),
              ("mix_norm", (DEPTH, 1024)), ("ffn_norm", (DEPTH, 1024)), ("final_norm", (1024,)), ("loss", (1,)))
WEIGHT_ORDER = ("attn_norm", "w_in", "mla_q_norm", "w_uq", "mla_kv_norm", "w_ukv", "conv_w", "pool_w", "pool_scale",
                "swa_sinks", "mix_norm", "w_o", "ffn_norm", "w_gate_up", "w_down", "final_norm")


def _params(sem):
    return pltpu.CompilerParams(dimension_semantics=sem, vmem_limit_bytes=VMEM_LIMIT)


def _pick(dim, target):
    if dim <= target:
        return dim
    best = None
    for t in range(LANES, target + 1, LANES):
        if dim % t == 0:
            best = t
    assert best is not None, (dim, target)
    return best


def _mm(a, b, *, name, ta=False, tb=False, res=None, out_dtype=F32, tm=1024, tn=1024, tk=1024, epilogue=None):
    m, k = (a.shape[1], a.shape[0]) if ta else a.shape
    n = b.shape[0] if tb else b.shape[1]
    assert (b.shape[1] if tb else b.shape[0]) == k
    tm, tn, tk = _pick(m, tm), _pick(n, tn), _pick(k, tk)
    nj, nk = n // tn, k // tk
    dims = (((0 if ta else 1,), (1 if tb else 0,)), ((), ()))
    fn, extra, outs = epilogue if epilogue is not None else (None, [], [(n, out_dtype)])
    if res is not None:
        assert epilogue is None
        fn, extra = (lambda acc, r: (acc + r,)), [res]
    n_in, n_out = 2 + len(extra), len(outs)

    def body(*refs):
        a_ref, b_ref, acc_ref = refs[0], refs[1], refs[-1]
        kk = pl.program_id(2)

        @pl.when(kk == 0)
        def _():
            acc_ref[...] = jnp.zeros_like(acc_ref)

        acc_ref[...] += lax.dot_general(a_ref[...].astype(BF16), b_ref[...].astype(BF16), dims,
                                        preferred_element_type=F32)

        @pl.when(kk == nk - 1)
        def _():
            tiles = (acc_ref[...],) if fn is None else fn(acc_ref[...], *[r[...] for r in refs[2:n_in]])
            for o_ref, tile in zip(refs[n_in:n_in + n_out], tiles):
                o_ref[...] = tile.astype(o_ref.dtype)

    def col_tiles(width):
        assert width % (nj * LANES) == 0, (width, nj)
        return pl.BlockSpec((tm, width // nj), lambda i, j, kk: (i, j))

    a_spec = pl.BlockSpec((tk, tm), lambda i, j, kk: (kk, i)) if ta else pl.BlockSpec((tm, tk), lambda i, j, kk: (i, kk))
    b_spec = pl.BlockSpec((tn, tk), lambda i, j, kk: (j, kk)) if tb else pl.BlockSpec((tk, tn), lambda i, j, kk: (kk, j))
    res_ = pl.pallas_call(
        body, name=name, grid=(m // tm, nj, nk), in_specs=[a_spec, b_spec] + [col_tiles(e.shape[1]) for e in extra],
        out_specs=[col_tiles(w) for w, _ in outs],
        out_shape=[jax.ShapeDtypeStruct((m, w), dt) for w, dt in outs],
        scratch_shapes=[pltpu.VMEM((tm, tn), F32)],
        compiler_params=_params(("parallel", "parallel", "arbitrary")),
    )(a, b, *extra)
    return res_[0] if epilogue is None else res_


def _rowwise(fn, *, name, rows, ins, outs, reds=(), tm=512):
    tm = min(tm, rows)
    assert rows % tm == 0
    n_in, n_out = len(ins), len(outs)

    def body(*refs):
        vals = [jnp.concatenate([r[h] for h in range(r.shape[0])], axis=1) if spec[0] == "heads" else r[...]
                for spec, r in zip(ins, refs[:n_in])]
        res = fn(*vals)
        for out, r, v in zip(outs, refs[n_in:n_in + n_out], res[:n_out]):
            if len(out) >= 3:
                for h in range(out[2]):
                    piece = v[:, h * LANES:(h + 1) * LANES]
                    r[h] = (piece.T if len(out) == 4 else piece).astype(r.dtype)
            else:
                r[...] = v.astype(r.dtype)
        if reds:
            @pl.when(pl.program_id(0) == 0)
            def _():
                for r in refs[n_in + n_out:]:
                    r[...] = jnp.zeros_like(r)

            for r, v in zip(refs[n_in + n_out:], res[n_out:]):
                r[...] += v

    in_specs, args = [], []
    for spec in ins:
        if spec[0] == "row":
            _, arr, width, blk = spec
            in_specs.append(pl.BlockSpec((tm, width), functools.partial(lambda i, blk: (i, blk), blk=blk)))
        elif spec[0] == "heads":
            arr = spec[1]
            in_specs.append(pl.BlockSpec((arr.shape[0], tm, LANES), lambda i: (0, i, 0)))
        else:
            arr = spec[1]
            in_specs.append(pl.BlockSpec(arr.shape, functools.partial(lambda i, nd: (0,) * nd, nd=arr.ndim)))
        args.append(arr)
    def out_spec(o):
        if len(o) == 4:
            return pl.BlockSpec((o[2], LANES, tm), lambda i: (0, 0, i)), (o[2], LANES, rows)
        if len(o) == 3:
            return pl.BlockSpec((o[2], tm, LANES), lambda i: (0, i, 0)), (o[2], rows, LANES)
        return pl.BlockSpec((tm, o[0]), lambda i: (i, 0)), (rows, o[0])

    out_specs = [out_spec(o)[0] for o in outs]
    out_shape = [jax.ShapeDtypeStruct(out_spec(o)[1], o[1]) for o in outs]
    out_specs += [pl.BlockSpec((r, w), lambda i: (0, 0)) for r, w in reds]
    out_shape += [jax.ShapeDtypeStruct((r, w), F32) for r, w in reds]
    return pl.pallas_call(body, name=name, grid=(rows // tm,), in_specs=in_specs, out_specs=out_specs,
                          out_shape=out_shape, compiler_params=_params(("arbitrary",)))(*args)


def _rstd(x, n):
    return lax.rsqrt(jnp.sum(x * x, axis=-1, keepdims=True) * (1.0 / n) + RMS_EPS)


def _rms_fwd(x_spec, g, *, name, rows, width):
    def fn(x, gv):
        return (x * _rstd(x, width) * gv,)
    return _rowwise(fn, name=name, rows=rows, ins=[x_spec, ("full", g)], outs=[(width, BF16)])[0]


def _rms_bwd(x_spec, g, dy, res, *, name, rows, width, out_dtypes):
    def fn(x, gv, dyv, *rest):
        r = _rstd(x, width)
        dyg = dyv * gv
        dx = r * dyg - x * (r * r * r) * (jnp.sum(dyg * x, axis=-1, keepdims=True) * (1.0 / width))
        if rest:
            dx = dx + rest[0]
        return (dx,) * len(out_dtypes) + (jnp.sum(dyv * x * r, axis=0, keepdims=True),)

    ins = [x_spec, ("full", g), ("row", dy, width, 0)]
    if res is not None:
        ins.append(("row", res, width, 0))
    return _rowwise(fn, name=name, rows=rows, ins=ins, outs=[(width, dt) for dt in out_dtypes], reds=[(1, width)])


NT_DIMS = (((1,), (1,)), ((), ()))
TN_DIMS = (((0,), (0,)), ((), ()))
BNT_DIMS = (((2,), (2,)), ((0,), (0,)))
BNN_DIMS = (((2,), (1,)), ((0,), (0,)))


def _mla_tile(s):
    return min(512, s)


def _causal(shape, query_axis):
    return lax.broadcasted_iota(jnp.int32, shape, query_axis) >= lax.broadcasted_iota(jnp.int32, shape, 1 - query_axis)


def _mla_fwd(qa, ka, vta, *, name, s, nh):
    t = _mla_tile(s)
    nq = s // t

    def body(q_ref, k_ref, vt_ref, o_ref, lse_ref, m_s, acc_s):
        i, j = pl.program_id(0), pl.program_id(1)

        @pl.when(j == 0)
        def _():
            m_s[...] = jnp.full_like(m_s, NEG)
            acc_s[...] = jnp.zeros_like(acc_s)

        def step(diag):
            sc = lax.dot_general(k_ref[...], q_ref[...], BNT_DIMS, preferred_element_type=F32)
            if diag:
                sc = jnp.where(_causal(sc.shape[1:], 1)[None], sc, NEG)
            m_prev = m_s[...]
            m_new = jnp.maximum(m_prev, jnp.max(sc, axis=1, keepdims=True))
            p = jnp.exp(sc - m_new).astype(BF16)
            acc_s[...] = (jnp.exp(m_prev - m_new) * acc_s[...]
                          + lax.dot_general(vt_ref[...], p, BNN_DIMS, preferred_element_type=F32))
            m_s[...] = m_new

        pl.when(j < i)(functools.partial(step, False))
        pl.when(j == i)(functools.partial(step, True))

        @pl.when(j == nq - 1)
        def _():
            row = lax.broadcasted_iota(jnp.int32, (LANES, t), 0)
            for h in range(nh):
                acc = acc_s[h]
                l = acc[HEAD:HEAD + 1, :]
                o_ref[h] = jnp.where(row < HEAD, acc / l, 0.0).T
                lse_ref[h] = m_s[h] + jnp.log(l)

    q_spec = pl.BlockSpec((nh, t, LANES), lambda i, j: (0, i, 0))
    k_spec = pl.BlockSpec((nh, t, LANES), lambda i, j: (0, jnp.minimum(j, i), 0))
    vt_spec = pl.BlockSpec((nh, LANES, t), lambda i, j: (0, 0, jnp.minimum(j, i)))
    return pl.pallas_call(
        body, name=name, grid=(nq, nq), in_specs=[q_spec, k_spec, vt_spec],
        out_specs=[q_spec, pl.BlockSpec((nh, 1, t), lambda i, j: (0, 0, i))],
        out_shape=[jax.ShapeDtypeStruct((nh, s, LANES), F32), jax.ShapeDtypeStruct((nh, 1, s), F32)],
        scratch_shapes=[pltpu.VMEM((nh, 1, t), F32), pltpu.VMEM((nh, LANES, t), F32)],
        compiler_params=_params(("parallel", "arbitrary")),
    )(qa, ka, vta)


def _mla_dq(qa, ka, va, o, do, lse, after, *, name, s, nh):
    t = _mla_tile(s)
    nq = s // t

    def body(q_ref, k_ref, v_ref, o_ref, do_ref, lse_ref, dq_ref, d_ref, acc_s):
        i, j = pl.program_id(0), pl.program_id(1)

        @pl.when(j == 0)
        def _():
            acc_s[...] = jnp.zeros_like(acc_s)
            d_ref[...] = jnp.sum(do_ref[...] * o_ref[...], axis=2, keepdims=True)

        def step(diag):
            k = k_ref[...]
            sc = lax.dot_general(q_ref[...], k, BNT_DIMS, preferred_element_type=F32)
            if diag:
                sc = jnp.where(_causal(sc.shape[1:], 0)[None], sc, NEG)
            p = jnp.exp(sc - lse_ref[...])
            dp = lax.dot_general(do_ref[...].astype(BF16), v_ref[...], BNT_DIMS, preferred_element_type=F32)
            ds = p * (dp - d_ref[...])
            acc_s[...] += lax.dot_general(ds.astype(BF16), k, BNN_DIMS, preferred_element_type=F32)

        pl.when(j < i)(functools.partial(step, False))
        pl.when(j == i)(functools.partial(step, True))

        @pl.when(j == nq - 1)
        def _():
            dq_ref[...] = acc_s[...]

    q_spec = pl.BlockSpec((nh, t, LANES), lambda i, j: (0, i, 0))
    kv_spec = pl.BlockSpec((nh, t, LANES), lambda i, j: (0, jnp.minimum(j, i), 0))
    vec_spec = pl.BlockSpec((nh, t, 1), lambda i, j: (0, i, 0))

    def body_after(q_ref, k_ref, v_ref, o_ref, do_ref, lse_ref, after_ref, *rest):
        body(q_ref, k_ref, v_ref, o_ref, do_ref, lse_ref, *rest)

    return pl.pallas_call(
        body_after, name=name, grid=(nq, nq),
        in_specs=[q_spec, kv_spec, kv_spec, q_spec, q_spec, vec_spec, pl.BlockSpec(memory_space=pl.ANY)],
        out_specs=[q_spec, vec_spec],
        out_shape=[jax.ShapeDtypeStruct((nh, s, LANES), F32), jax.ShapeDtypeStruct((nh, s, 1), F32)],
        scratch_shapes=[pltpu.VMEM((nh, t, LANES), F32)],
        compiler_params=_params(("parallel", "arbitrary")),
    )(qa, ka, va, o, do, lse, after)


def _mla_dkv(qa, ka, va, do, lse_row, d_row, *, name, s, nh):
    t = _mla_tile(s)
    nq = s // t

    def body(q_ref, k_ref, v_ref, do_ref, lse_ref, d_ref, dk_ref, dv_ref, dk_s, dv_s):
        kb, j = pl.program_id(0), pl.program_id(1)

        @pl.when(j == 0)
        def _():
            dk_s[...] = jnp.zeros_like(dk_s)
            dv_s[...] = jnp.zeros_like(dv_s)

        def step(diag):
            q = q_ref[...]
            do_b = do_ref[...].astype(BF16)
            sc = lax.dot_general(k_ref[...], q, BNT_DIMS, preferred_element_type=F32)
            if diag:
                sc = jnp.where(_causal(sc.shape[1:], 1)[None], sc, NEG)
            p = jnp.exp(sc - lse_ref[...])
            dv_s[...] += lax.dot_general(p.astype(BF16), do_b, BNN_DIMS, preferred_element_type=F32)
            dp = lax.dot_general(v_ref[...], do_b, BNT_DIMS, preferred_element_type=F32)
            ds = p * (dp - d_ref[...])
            dk_s[...] += lax.dot_general(ds.astype(BF16), q, BNN_DIMS, preferred_element_type=F32)

        pl.when(j > kb)(functools.partial(step, False))
        pl.when(j == kb)(functools.partial(step, True))

        @pl.when(j == nq - 1)
        def _():
            dk_ref[...] = dk_s[...]
            dv_ref[...] = dv_s[...].astype(dv_ref.dtype)

    q_spec = pl.BlockSpec((nh, t, LANES), lambda kb, j: (0, jnp.maximum(j, kb), 0))
    kv_spec = pl.BlockSpec((nh, t, LANES), lambda kb, j: (0, kb, 0))
    row_spec = pl.BlockSpec((nh, 1, t), lambda kb, j: (0, 0, jnp.maximum(j, kb)))
    return pl.pallas_call(
        body, name=name, grid=(nq, nq), in_specs=[q_spec, kv_spec, kv_spec, q_spec, row_spec, row_spec],
        out_specs=[kv_spec, kv_spec],
        out_shape=[jax.ShapeDtypeStruct((nh, s, LANES), F32), jax.ShapeDtypeStruct((nh, s, LANES), BF16)],
        scratch_shapes=[pltpu.VMEM((nh, t, LANES), F32), pltpu.VMEM((nh, t, LANES), F32)],
        compiler_params=_params(("parallel", "arbitrary")),
    )(qa, ka, va, do, lse_row, d_row)


SWA_PIECE = 128
SWA_KEYS = 2 * SWA_PIECE


def _swa_block(s):
    return min(512, s)


def _swa_piece(hp_ref, h, q, k_ref, v_ref, qpos0, scale):
    kstart = pl.multiple_of(jnp.maximum(qpos0 - SWA_PIECE, 0), SWA_PIECE)
    k = k_ref[pl.ds(kstart, SWA_KEYS), :].astype(BF16)
    v = v_ref[pl.ds(kstart, SWA_KEYS), :].astype(BF16)
    sc = lax.dot_general(q, k, NT_DIMS, preferred_element_type=F32)
    dist = (qpos0 + lax.broadcasted_iota(jnp.int32, sc.shape, 0)) - (kstart + lax.broadcasted_iota(jnp.int32, sc.shape, 1))
    sc = sc * scale - hp_ref[h, 0] * dist.astype(F32)
    sc = jnp.where((dist >= 0) & (dist < SWA_WINDOW), sc, NEG)
    return kstart, k, v, sc


def _swa_fwd(proj, hp, *, name, s, scale):
    tb = _swa_block(s)
    group = SWA_HEADS // SWA_KV_HEADS
    q_off, k_off, v_off = C_QSW[0] // LANES, C_KSW[0] // LANES, C_VSW[0] // LANES

    def body(hp_ref, q_ref, k_ref, v_ref, o_ref, lse_ref):
        h, i = pl.program_id(0), pl.program_id(1)
        sink = hp_ref[h, 1]
        for r in range(0, tb, SWA_PIECE):
            rows = pl.ds(r, SWA_PIECE)
            _, _, v, sc = _swa_piece(hp_ref, h, q_ref[rows, :].astype(BF16), k_ref, v_ref, i * tb + r, scale)
            m = jnp.maximum(jnp.max(sc, axis=1, keepdims=True), sink)
            p = jnp.exp(sc - m)
            l = jnp.sum(p, axis=1, keepdims=True) + jnp.exp(sink - m)
            o_ref[rows, :] = jnp.dot(p.astype(BF16), v, preferred_element_type=F32) / l
            lse_ref[rows, :] = m + jnp.log(l)

    whole = lambda off: pl.BlockSpec((s, LANES), lambda h, i: (0, off + h // group))
    return pl.pallas_call(
        body, name=name, grid=(SWA_HEADS, s // tb),
        in_specs=[pl.BlockSpec(memory_space=pltpu.SMEM), pl.BlockSpec((tb, LANES), lambda h, i: (i, q_off + h)),
                  whole(k_off), whole(v_off)],
        out_specs=[pl.BlockSpec((tb, LANES), lambda h, i: (i, h)), pl.BlockSpec((None, tb, 1), lambda h, i: (h, i, 0))],
        out_shape=[jax.ShapeDtypeStruct((s, SWA_HEADS * LANES), F32), jax.ShapeDtypeStruct((SWA_HEADS, s, 1), F32)],
        compiler_params=_params(("parallel", "parallel")),
    )(hp, proj, proj, proj)


def _swa_bwd(proj, o, do, lse, hp, *, name, s, scale):
    tb = _swa_block(s)
    nqb = s // tb
    group = SWA_HEADS // SWA_KV_HEADS
    q_off, k_off, v_off = C_QSW[0] // LANES, C_KSW[0] // LANES, C_VSW[0] // LANES

    def body(hp_ref, q_ref, k_ref, v_ref, o_ref, do_ref, lse_ref, dq_ref, dk_ref, dv_ref, dsink_ref):
        kh, g, i = pl.program_id(0), pl.program_id(1), pl.program_id(2)
        h = kh * group + g
        sink = hp_ref[h, 1]

        @pl.when((g == 0) & (i == 0))
        def _():
            dk_ref[...] = jnp.zeros_like(dk_ref)
            dv_ref[...] = jnp.zeros_like(dv_ref)

        @pl.when(i == 0)
        def _():
            dsink_ref[...] = jnp.zeros_like(dsink_ref)

        for r in range(0, tb, SWA_PIECE):
            rows = pl.ds(r, SWA_PIECE)
            q = q_ref[rows, :].astype(BF16)
            dov = do_ref[rows, :]
            do_b = dov.astype(BF16)
            lse_r = lse_ref[rows, :]
            d_r = jnp.sum(dov * o_ref[rows, :], axis=1, keepdims=True)
            kstart, k, v, sc = _swa_piece(hp_ref, h, q, k_ref, v_ref, i * tb + r, scale)
            p = jnp.exp(sc - lse_r)
            dp = lax.dot_general(do_b, v, NT_DIMS, preferred_element_type=F32)
            ds = (p * (dp - d_r)).astype(BF16)
            dq_ref[rows, :] = (jnp.dot(ds, k, preferred_element_type=F32) * scale).astype(dq_ref.dtype)
            win = pl.ds(kstart, SWA_KEYS)
            dk_ref[win, :] += lax.dot_general(ds, q, TN_DIMS, preferred_element_type=F32) * scale
            dv_ref[win, :] += lax.dot_general(p.astype(BF16), do_b, TN_DIMS, preferred_element_type=F32)
            part = jnp.sum(-jnp.exp(sink - lse_r) * d_r, axis=0, keepdims=True)
            dsink_ref[...] += jnp.broadcast_to(part, (1, LANES))

    whole = lambda off: pl.BlockSpec((s, LANES), lambda kh, g, i: (0, off + kh))
    q_map = lambda kh, g, i: (i, kh * group + g)
    return pl.pallas_call(
        body, name=name, grid=(SWA_KV_HEADS, group, nqb),
        in_specs=[pl.BlockSpec(memory_space=pltpu.SMEM),
                  pl.BlockSpec((tb, LANES), lambda kh, g, i: (i, q_off + kh * group + g)), whole(k_off), whole(v_off),
                  pl.BlockSpec((tb, LANES), q_map), pl.BlockSpec((tb, LANES), q_map),
                  pl.BlockSpec((None, tb, 1), lambda kh, g, i: (kh * group + g, i, 0))],
        out_specs=[pl.BlockSpec((tb, LANES), q_map), whole(0), whole(0),
                   pl.BlockSpec((None, 1, LANES), lambda kh, g, i: (kh * group + g, 0, 0))],
        out_shape=[jax.ShapeDtypeStruct((s, SWA_HEADS * LANES), BF16),
                   jax.ShapeDtypeStruct((s, SWA_KV_HEADS * LANES), F32), jax.ShapeDtypeStruct((s, SWA_KV_HEADS * LANES), F32),
                   jax.ShapeDtypeStruct((SWA_HEADS, 1, LANES), F32)],
        compiler_params=_params(("parallel", "arbitrary", "arbitrary")),
    )(hp, proj, proj, proj, o, do, lse)


def _shift_down(z, k):
    rows = lax.broadcasted_iota(jnp.int32, z.shape, 0)
    return jnp.where(rows >= k, pltpu.roll(z, k, 0), 0.0)


def _shift_up(z, k):
    n = z.shape[0]
    rows = lax.broadcasted_iota(jnp.int32, z.shape, 0)
    return jnp.where(rows < n - k, pltpu.roll(z, n - k, 0), 0.0)


def _rows3(a, b, c):
    r = lax.broadcasted_iota(jnp.int32, (3, a.shape[1]), 0)
    return jnp.where(r == 0, a, jnp.where(r == 1, b, c))


def _col_spec(s, off):
    return pl.BlockSpec((s, LANES), functools.partial(lambda j, off: (0, off + j), off=off))


def _conv_fwd(proj, conv_w, *, name, s):
    def body(gb_ref, gc_ref, u_ref, w_ref, y_ref):
        w0, w1, w2 = w_ref[0:1, :], w_ref[1:2, :], w_ref[2:3, :]
        z = gc_ref[...] * u_ref[...]
        c = w2 * z + w1 * _shift_down(z, 1) + w0 * _shift_down(z, 2)
        y_ref[...] = gb_ref[...] * c

    return pl.pallas_call(
        body, name=name, grid=(2,),
        in_specs=[_col_spec(s, C_GB[0] // LANES), _col_spec(s, C_GC[0] // LANES), _col_spec(s, C_UCONV[0] // LANES),
                  pl.BlockSpec((3, LANES), lambda j: (0, j))],
        out_specs=_col_spec(s, 0), out_shape=jax.ShapeDtypeStruct((s, D_GROUP), F32),
        compiler_params=_params(("parallel",)),
    )(proj, proj, proj, conv_w)


def _conv_bwd(dy, proj, conv_w, *, name, s):
    def body(dy_ref, gb_ref, gc_ref, u_ref, w_ref, dgb_ref, dgc_ref, du_ref, dw_ref):
        w0, w1, w2 = w_ref[0:1, :], w_ref[1:2, :], w_ref[2:3, :]
        gc, u, dyv = gc_ref[...], u_ref[...], dy_ref[...]
        z = gc * u
        z1, z2 = _shift_down(z, 1), _shift_down(z, 2)
        c = w2 * z + w1 * z1 + w0 * z2
        dgb_ref[...] = (dyv * c).astype(dgb_ref.dtype)
        dc = dyv * gb_ref[...]
        dz = w2 * dc + w1 * _shift_up(dc, 1) + w0 * _shift_up(dc, 2)
        dgc_ref[...] = (dz * u).astype(dgc_ref.dtype)
        du_ref[...] = (dz * gc).astype(du_ref.dtype)
        dw_ref[...] = _rows3(jnp.sum(dc * z2, axis=0, keepdims=True), jnp.sum(dc * z1, axis=0, keepdims=True),
                             jnp.sum(dc * z, axis=0, keepdims=True))

    act = jax.ShapeDtypeStruct((s, D_GROUP), BF16)
    return pl.pallas_call(
        body, name=name, grid=(2,),
        in_specs=[_col_spec(s, 0), _col_spec(s, C_GB[0] // LANES), _col_spec(s, C_GC[0] // LANES),
                  _col_spec(s, C_UCONV[0] // LANES), pl.BlockSpec((3, LANES), lambda j: (0, j))],
        out_specs=[_col_spec(s, 0), _col_spec(s, 0), _col_spec(s, 0), pl.BlockSpec((3, LANES), lambda j: (0, j))],
        out_shape=[act, act, act, jax.ShapeDtypeStruct((3, D_GROUP), F32)],
        compiler_params=_params(("parallel",)),
    )(dy, proj, proj, proj, conv_w)


def _pool_select(j, lane, a2, a4, a8, a16):
    lo = lane < HEAD
    return jnp.where(j == 0, jnp.where(lo, a2, a4), jnp.where(lo, a8, a16))


def _pooled(u, j):
    s2 = u + _shift_down(u, 1)
    s4 = s2 + _shift_down(s2, 2)
    s8 = s4 + _shift_down(s4, 4)
    s16 = s8 + _shift_down(s8, 8)
    lane = lax.broadcasted_iota(jnp.int32, u.shape, 1)
    rows = lax.broadcasted_iota(jnp.int32, u.shape, 0)
    win = _pool_select(j, lane, 2, 4, 8, 16)
    count = jnp.minimum(rows + 1, win).astype(F32)
    return _pool_select(j, lane, s2, s4, s8, s16) / count - u, count


def _pool_fwd(proj, wbd, scale, *, name, s):
    def body(u_ref, w_ref, sc_ref, y_ref):
        pooled, _ = _pooled(u_ref[...], pl.program_id(0))
        y_ref[...] = jnp.dot(pooled.astype(BF16), w_ref[...].astype(BF16), preferred_element_type=F32) * sc_ref[...]

    return pl.pallas_call(
        body, name=name, grid=(2,),
        in_specs=[_col_spec(s, C_UPOOL[0] // LANES), pl.BlockSpec((None, LANES, LANES), lambda j: (j, 0, 0)),
                  pl.BlockSpec((1, LANES), lambda j: (0, j))],
        out_specs=_col_spec(s, 0), out_shape=jax.ShapeDtypeStruct((s, D_GROUP), F32),
        compiler_params=_params(("parallel",)),
    )(proj, wbd, scale)


def _pool_bwd(dy, proj, wbd, scale, *, name, s):
    def body(dy_ref, u_ref, w_ref, sc_ref, du_ref, dw_ref, dsc_ref):
        j = pl.program_id(0)
        pooled, count = _pooled(u_ref[...], j)
        pooled_b = pooled.astype(BF16)
        w_b = w_ref[...].astype(BF16)
        dyv = dy_ref[...]
        mixed = jnp.dot(pooled_b, w_b, preferred_element_type=F32)
        dsc_ref[...] = jnp.sum(dyv * mixed, axis=0, keepdims=True)
        dms = (dyv * sc_ref[...]).astype(BF16)
        dw_ref[...] = lax.dot_general(pooled_b, dms, (((0,), (0,)), ((), ())), preferred_element_type=F32)
        dpooled = lax.dot_general(dms, w_b, (((1,), (1,)), ((), ())), preferred_element_type=F32)
        r = dpooled / count
        a2 = r + _shift_up(r, 1)
        a4 = a2 + _shift_up(a2, 2)
        a8 = a4 + _shift_up(a4, 4)
        a16 = a8 + _shift_up(a8, 8)
        lane = lax.broadcasted_iota(jnp.int32, r.shape, 1)
        du_ref[...] = (_pool_select(j, lane, a2, a4, a8, a16) - dpooled).astype(du_ref.dtype)

    return pl.pallas_call(
        body, name=name, grid=(2,),
        in_specs=[_col_spec(s, 0), _col_spec(s, C_UPOOL[0] // LANES),
                  pl.BlockSpec((None, LANES, LANES), lambda j: (j, 0, 0)), pl.BlockSpec((1, LANES), lambda j: (0, j))],
        out_specs=[_col_spec(s, 0), pl.BlockSpec((None, LANES, LANES), lambda j: (j, 0, 0)),
                   pl.BlockSpec((1, LANES), lambda j: (0, j))],
        out_shape=[jax.ShapeDtypeStruct((s, D_GROUP), BF16), jax.ShapeDtypeStruct((2, LANES, LANES), F32),
                   jax.ShapeDtypeStruct((1, D_GROUP), F32)],
        compiler_params=_params(("parallel",)),
    )(dy, proj, wbd, scale)


def _mesh_pos():
    return lax.axis_index("x"), lax.axis_index("y"), lax.axis_index("c")


def _any_specs(n):
    return [pl.BlockSpec(memory_space=pl.ANY)] * n


def _all_gather(xs, *, name):
    n = len(xs)

    def body(*refs):
        x_refs, out_refs = refs[:n], refs[n:2 * n]
        send_sems, recv_sems, local_sems = refs[2 * n:]
        x, y, cc = _mesh_pos()
        me, sibling = (x, y, cc), (x, y, 1 - cc)
        chips = [(1 - x, y), (x, 1 - y), (1 - x, 1 - y)]

        def slot(a, px, py, pc):
            return out_refs[a].at[4 * px + 2 * py + pc]

        def copy(a, k, block, to, src=None):
            return pltpu.make_async_remote_copy(
                src_ref=slot(a, *block) if src is None else src, dst_ref=slot(a, *block), send_sem=send_sems.at[a, k],
                recv_sem=recv_sems.at[a, k], device_id=to, device_id_type=pl.DeviceIdType.MESH)

        mine = [pltpu.make_async_copy(x_refs[a], slot(a, *me), local_sems.at[a]) for a in range(n)]
        first = []
        for a in range(n):
            first.append(copy(a, 0, me, sibling, src=x_refs[a]))
            first += [copy(a, 1 + j, me, (*chip, cc), src=x_refs[a]) for j, chip in enumerate(chips)]
        for cp in mine + first:
            cp.start()
        passed = []
        for j, chip in enumerate(chips):
            for a in range(n):
                copy(a, 1 + j, (*chip, cc), me).wait_recv()
                passed.append(copy(a, 4 + j, (*chip, cc), sibling))
                passed[-1].start()
        for a in range(n):
            copy(a, 0, sibling, me).wait_recv()
        for j, chip in enumerate(chips):
            for a in range(n):
                copy(a, 4 + j, (*chip, 1 - cc), me).wait_recv()
        for cp in first + passed:
            cp.wait_send()
        for cp in mine:
            cp.wait()

    return pl.pallas_call(
        body, name=name, out_shape=[jax.ShapeDtypeStruct((N_DEV,) + a.shape, a.dtype) for a in xs],
        in_specs=_any_specs(n), out_specs=_any_specs(n),
        scratch_shapes=[pltpu.SemaphoreType.DMA((n, 7)), pltpu.SemaphoreType.DMA((n, 7)), pltpu.SemaphoreType.DMA((n,))],
    )(*xs)


def _plan_gather(src_refs, land_refs, send_sems, recv_sems):
    x, y, cc = _mesh_pos()
    me = 4 * x + 2 * y + cc
    plan = []
    for a, (src, land) in enumerate(zip(src_refs, land_refs)):
        for k in range(1, N_DEV):
            px, py, pc = (1 - x if k & 4 else x), (1 - y if k & 2 else y), (1 - cc if k & 1 else cc)
            i = a * (N_DEV - 1) + k - 1
            sems = dict(send_sem=send_sems.at[i], recv_sem=recv_sems.at[i], device_id=(px, py, pc),
                        device_id_type=pl.DeviceIdType.MESH)
            plan.append((pltpu.make_async_remote_copy(src_ref=src, dst_ref=land.at[me], **sems),
                         pltpu.make_async_remote_copy(src_ref=src, dst_ref=land.at[4 * px + 2 * py + pc], **sems)))
    return plan


def _plan_sibling(src_refs, land_refs, send_sems, recv_sems):
    x, y, cc = _mesh_pos()
    plan = []
    for a, (src, land) in enumerate(zip(src_refs, land_refs)):
        cp = pltpu.make_async_remote_copy(
            src_ref=src.at[1 - cc], dst_ref=land, send_sem=send_sems.at[a], recv_sem=recv_sems.at[a],
            device_id=(x, y, 1 - cc), device_id_type=pl.DeviceIdType.MESH)
        plan.append((cp, cp))
    return plan


def _plan_chips(src_refs, land_refs, send_sems, recv_sems):
    x, y, cc = _mesh_pos()
    my_chip = 2 * x + y
    plan = []
    for a, (src, land) in enumerate(zip(src_refs, land_refs)):
        for j, (px, py) in enumerate([(1 - x, y), (x, 1 - y), (1 - x, 1 - y)]):
            peer = 2 * px + py
            sems = dict(send_sem=send_sems.at[3 * a + j], recv_sem=recv_sems.at[3 * a + j], device_id=(px, py, cc),
                        device_id_type=pl.DeviceIdType.MESH)
            plan.append((pltpu.make_async_remote_copy(src_ref=src.at[peer], dst_ref=land.at[my_chip], **sems),
                         pltpu.make_async_remote_copy(src_ref=src.at[peer], dst_ref=land.at[peer], **sems)))
    return plan


HBM_SPEC = pl.BlockSpec(memory_space=pltpu.HBM)
SEM_SPEC = pl.BlockSpec(memory_space=pltpu.SEMAPHORE)
ANY_SPEC = pl.BlockSpec(memory_space=pl.ANY)
SIDE_EFFECT = pltpu.CompilerParams(has_side_effects=pltpu.SideEffectType.DATAFLOW_SIDE_EFFECTING)


def _start_copies(plan, sems_per_array, srcs, land_shapes, after, *, name):
    n, n_after = len(srcs), len(after)

    def body(*refs):
        send_sems, recv_sems = refs[2 * n + n_after], refs[2 * n + n_after + 1]
        for out, _ in plan(refs[:n], refs[n:2 * n], send_sems, recv_sems):
            out.start()
        refs[-1][...] = jnp.zeros_like(refs[-1])

    lands = [lax.empty(shp, a.dtype) for shp, a in zip(land_shapes, srcs)]
    sem = pltpu.SemaphoreType.DMA((n * sems_per_array,))
    res = pl.pallas_call(
        body, name=name,
        out_shape=(sem, sem, *[pltpu.HBM(a.shape, a.dtype) for a in srcs + lands], jax.ShapeDtypeStruct((8, LANES), F32)),
        in_specs=[HBM_SPEC] * (2 * n) + [ANY_SPEC] * n_after,
        out_specs=(SEM_SPEC, SEM_SPEC, *[HBM_SPEC] * (2 * n), pl.BlockSpec(memory_space=pltpu.VMEM)),
        input_output_aliases={i: 2 + i for i in range(2 * n)}, compiler_params=SIDE_EFFECT,
    )(*[pltpu.with_memory_space_constraint(a, pltpu.HBM) for a in srcs + lands], *after)
    return (res[0], res[1], list(res[2:2 + n]), list(res[2 + n:2 + 2 * n])), res[-1]


def _wait_copies(plan, handle, after, *, name):
    send, recv, srcs, lands = handle
    n = len(srcs)

    def body(*refs):
        for out, inc in plan(refs[:n], refs[n:2 * n], refs[2 * n], refs[2 * n + 1]):
            out.wait_send()
            inc.wait_recv()

    res = pl.pallas_call(
        body, name=name, out_shape=tuple(pltpu.HBM(a.shape, a.dtype) for a in srcs + lands),
        in_specs=[HBM_SPEC] * (2 * n) + [SEM_SPEC, SEM_SPEC] + [ANY_SPEC] * len(after), out_specs=[HBM_SPEC] * (2 * n),
        input_output_aliases={i: i for i in range(2 * n)}, compiler_params=SIDE_EFFECT,
    )(*srcs, *lands, send, recv, *after)
    return list(res[:n]), list(res[n:])


def _row_tile(rows, target=512):
    if rows <= target:
        return rows
    best = None
    for t in range(8, target + 1, 8):
        if rows % t == 0:
            best = t
    assert best is not None, (rows, target)
    return best


def _add_own(g, other, core, *, name):
    _, _, rows, cols = g.shape
    tm = _row_tile(rows)

    def body(c_ref, g_ref, o_ref, out_ref):
        out_ref[...] = g_ref[...] + o_ref[...]

    return pl.pallas_call(
        body, name=name, out_shape=jax.ShapeDtypeStruct(other.shape, other.dtype),
        grid_spec=pltpu.PrefetchScalarGridSpec(
            num_scalar_prefetch=1, grid=(4, rows // tm),
            in_specs=[pl.BlockSpec((None, None, tm, cols), lambda p, i, c_ref: (c_ref[0], p, i, 0)),
                      pl.BlockSpec((None, tm, cols), lambda p, i, c_ref: (p, i, 0))],
            out_specs=pl.BlockSpec((None, tm, cols), lambda p, i, c_ref: (p, i, 0))),
        compiler_params=_params(("parallel", "parallel")),
    )(core, g, other)


def _adamw(parts, w, m, v, after, *, name):
    layers, rows, cols = w.shape
    assert len(parts) == layers
    tm = _row_tile(rows, 256)
    nr = rows // tm

    def body(*refs):
        p_refs = refs[:layers]
        w_ref, m_ref, v_ref, _, g_ref, d_ref, nm_ref, nv_ref, g_s = refs[layers:]
        for ll in range(layers):
            @pl.when(pl.program_id(0) == ll)
            def _(ll=ll):
                g = p_refs[ll][0]
                for q in range(1, p_refs[ll].shape[0]):
                    g = g + p_refs[ll][q]
                g_s[...] = g

        g = g_s[...]
        mm = ADAM_B1 * m_ref[...] + (1.0 - ADAM_B1) * g
        vv = ADAM_B2 * v_ref[...] + (1.0 - ADAM_B2) * jnp.square(g)
        m_hat = mm / (1.0 - ADAM_B1 ** ADAM_STEP)
        v_hat = vv / (1.0 - ADAM_B2 ** ADAM_STEP)
        g_ref[...] = g
        d_ref[...] = -ADAM_LR * (m_hat / (jnp.sqrt(v_hat) + ADAM_EPS) + ADAM_WD * w_ref[...])
        nm_ref[...] = mm
        nv_ref[...] = vv

    def part_spec(ll, p):
        return pl.BlockSpec((p, tm, cols), lambda l, i: (0, jnp.where(l == ll, i, jnp.where(l < ll, 0, nr - 1)), 0))

    spec = pl.BlockSpec((None, tm, cols), lambda l, i: (l, i, 0))
    out = jax.ShapeDtypeStruct(w.shape, F32)
    return pl.pallas_call(
        body, name=name, grid=(layers, nr),
        in_specs=[part_spec(ll, parts[ll].shape[0]) for ll in range(layers)] + [spec] * 3 + [pl.BlockSpec(memory_space=pl.ANY)],
        out_specs=[spec] * 4, out_shape=[out] * 4, scratch_shapes=[pltpu.VMEM((tm, cols), F32)],
        compiler_params=_params(("arbitrary", "arbitrary")),
    )(*parts, w, m, v, after)


def _pack(arrs):
    flat = jnp.concatenate([a.reshape(-1) for a in arrs])
    rows = -(-flat.shape[0] // (PACK_COLS * 16)) * 16
    return jnp.pad(flat, (0, rows * PACK_COLS - flat.shape[0])).reshape(rows, PACK_COLS)


def _unpack(packed, shapes):
    flat = packed.reshape(-1)
    out, off = [], 0
    for shp in shapes:
        n = int(np.prod(shp))
        out.append(flat[off:off + n].reshape(shp))
        off += n
    return out


def _shards_to_full(g, axis):
    if axis == 0:
        return g.reshape(g.shape[0] * g.shape[1], g.shape[2])
    return jnp.transpose(g, (1, 0, 2)).reshape(g.shape[1], g.shape[0] * g.shape[2])


def _full_to_shards(a, axis):
    if axis == 0:
        return jnp.transpose(a.reshape(4, 2, a.shape[0] // N_DEV, a.shape[1]), (1, 0, 2, 3))
    return jnp.transpose(a.reshape(a.shape[0], 4, 2, a.shape[1] // N_DEV), (2, 1, 0, 3))


def _zeros_like_cols(a, n):
    return jnp.zeros(a.shape[:-1] + (n,), a.dtype)


def _pad_heads(a, n):
    z = _zeros_like_cols(a, HEAD)
    return jnp.concatenate([p for h in range(n) for p in (a[..., h * HEAD:(h + 1) * HEAD], z)], axis=-1)


def _unpad_heads(a, n):
    return jnp.concatenate([a[..., h * LANES:h * LANES + HEAD] for h in range(n)], axis=-1)


def _seg(first, width, sign=1):
    return (width, [(first, sign)])


def _zero(width):
    return (width, [])


def _swapped(first):
    half = MLA_ROPE // 2
    return [_seg(first + half, half, -1), _seg(first, half)]


def _padded_heads(first, n):
    return [s for h in range(n) for s in (_seg(first + HEAD * h, HEAD), _zero(HEAD))]


def _layout_w_in():
    kr = 384
    return (_padded_heads(1440, 4) + [_seg(0, 256), _seg(416, 256), _seg(672, 256), _seg(928, 256), _seg(1184, 256)]
            + _padded_heads(1696, 2) + _padded_heads(1824, 2) + [_seg(256, 128)]
            + [_zero(HEAD), _seg(kr, MLA_ROPE), _seg(kr, MLA_ROPE)] + [_zero(HEAD)] + _swapped(kr) + _swapped(kr))


def _layout_w_uq():
    out = []
    for h in range(MLA_HEADS):
        out += [_seg(96 * h, MLA_NOPE), _seg(96 * h + MLA_NOPE, MLA_ROPE)] + _swapped(96 * h + MLA_NOPE)
    return out


def _layout_w_ukv():
    keys = [s for h in range(MLA_HEADS) for s in (_seg(LANES * h, HEAD), _zero(HEAD))]
    values = [s for h in range(MLA_HEADS) for s in (_seg(LANES * h + HEAD, HEAD), _zero(HEAD))]
    return keys + values


def _layout_w_gate_up():
    return [_seg(half + j, GU_TILE) for j in range(0, D_FF, GU_TILE) for half in (0, D_FF)]


LAYOUTS = dict(w_in=_layout_w_in(), w_uq=_layout_w_uq(), w_ukv=_layout_w_ukv(), w_gate_up=_layout_w_gate_up())
OWN_COLS = dict(w_in=1952, w_uq=384, w_ukv=512, w_gate_up=2 * D_FF)


def _plan_extend(layout, shard):
    plan = []
    for width, terms in layout:
        if not terms:
            plan.append((width, []))
            continue
        (first, sign), = terms
        while width:
            g, off = divmod(first, shard)
            w = min(width, shard - off)
            plan.append((w, [(g, off, sign)]))
            first, width = first + w, width - w
    return [plan]


def _plan_fold(layout, own_cols):
    sources = [[] for _ in range(own_cols)]
    e = 0
    for width, terms in layout:
        for first, sign in terms:
            for i in range(width):
                sources[first + i].append((e + i, sign))
        e += width
    shard = own_cols // N_DEV
    plans = {}
    for g in range(N_DEV):
        plan, n = [], g * shard
        while n < (g + 1) * shard:
            w = 1
            while n + w < (g + 1) * shard and [(c + w, sg) for c, sg in sources[n]] == sources[n + w]:
                w += 1
            plan.append((w, [(0, c, sg) for c, sg in sources[n]]))
            n += w
        plans[g] = plan
    return [plans[2 * p + c] for c in range(2) for p in range(4)]


def _assemble(src, plans, out_cols, out_dtype, *, name):
    g, rows, c = src.shape
    tm = _row_tile(rows, 256)

    def body(s_ref, o_ref):
        blocks = [s_ref[i].astype(F32) for i in range(g)]
        for d, plan in enumerate(plans):
            pieces = []
            for width, terms in plan:
                v = None
                for b, first, sign in terms:
                    t = blocks[b][:, first:first + width]
                    t = -t if sign < 0 else t
                    v = t if v is None else v + t
                pieces.append(jnp.zeros((tm, width), F32) if v is None else v)
            o_ref[d] = (pieces[0] if len(pieces) == 1 else jnp.concatenate(pieces, axis=1)).astype(o_ref.dtype)

    return pl.pallas_call(
        body, name=name, grid=(rows // tm,), in_specs=[pl.BlockSpec((g, tm, c), lambda i: (0, i, 0))],
        out_specs=pl.BlockSpec((len(plans), tm, out_cols), lambda i: (0, i, 0)),
        out_shape=jax.ShapeDtypeStruct((len(plans), rows, out_cols), out_dtype), compiler_params=_params(("parallel",)),
    )(src)


def _extend(nm, gathered, *, name):
    layout = LAYOUTS[nm]
    return _assemble(gathered, _plan_extend(layout, OWN_COLS[nm] // N_DEV), sum(w for w, _ in layout), BF16, name=name)[0]


def _fold_to_shards(nm, grad_ext, *, name):
    shards = _assemble(grad_ext[None], _plan_fold(LAYOUTS[nm], OWN_COLS[nm]), OWN_COLS[nm] // N_DEV, F32, name=name)
    return shards.reshape((2, 4) + shards.shape[1:])


def _rope_tables(s):
    inv = 1.0 / (ROPE_THETA ** (jnp.arange(0, MLA_ROPE, 2, dtype=F32) / MLA_ROPE))
    ang = jnp.arange(s, dtype=F32)[:, None] * inv[None, :]
    cos, sin = jnp.cos(ang), jnp.sin(ang)
    c32, s32 = jnp.concatenate([cos, cos], axis=1), jnp.concatenate([sin, sin], axis=1)
    zeros, ones = jnp.zeros((s, HEAD), F32), jnp.ones((s, HEAD), F32)
    tq = jnp.concatenate([ones, c32, s32], axis=1) * (1.0 / math.sqrt(MLA_NOPE + MLA_ROPE))
    return (jnp.tile(tq, (1, MLA_HEADS)), jnp.concatenate([zeros, c32, c32], axis=1),
            jnp.concatenate([zeros, s32, s32], axis=1))


def _gn(y):
    return y * _rstd(y, D_GROUP)


def _mixer_fwd(x, w, tabs, l):
    s = x.shape[0]
    tq, tkc, tks = tabs
    n = lambda t: f"l{l}_{t}"
    h = _rms_fwd(("row", x, D_MODEL, 0), w["attn_norm"], name=n("attn_norm"), rows=s, width=D_MODEL)
    proj = _mm(h, w["w_in"], name=n("in_proj"))
    cqn = _rms_fwd(("row", proj, 256, C_CQ[0] // 256), w["mla_q_norm"], name=n("q_norm"), rows=s, width=256)
    ckvn = _rms_fwd(("row", proj, 128, C_CKV[0] // 128), w["mla_kv_norm"], name=n("kv_norm"), rows=s, width=128)
    q_ext = _mm(cqn, w["w_uq"], name=n("uq"))
    kv_ext = _mm(ckvn, w["w_ukv"], name=n("ukv"))

    def prep(qe, kvk, kvv, kr, krp, tqv, tc, ts):
        kb = kr * tc + krp * ts
        lane = lax.broadcasted_iota(jnp.int32, kvv.shape, 1) & (LANES - 1)
        v = jnp.where(lane == HEAD, 1.0, kvv)
        return qe * tqv, kvk + jnp.tile(kb, (1, MLA_HEADS)), v, v

    qm, km, vm, vmt = _rowwise(
        prep, name=n("mla_prep"), rows=s,
        ins=[("row", q_ext, 512, 0), ("row", kv_ext, 512, 0), ("row", kv_ext, 512, 1),
             ("row", proj, 128, C_KR[0] // 128), ("row", proj, 128, C_KRP[0] // 128),
             ("row", tq, 512, 0), ("row", tkc, 128, 0), ("row", tks, 128, 0)],
        outs=[(512, BF16, MLA_HEADS)] * 3 + [(512, BF16, MLA_HEADS, "T")])
    y_a, lse_row = _mla_fwd(qm, km, vmt, name=n("mla_fwd"), s=s, nh=MLA_HEADS)
    lse_a = lse_row.reshape(MLA_HEADS, s, 1)
    y_b = _conv_fwd(proj, w["conv_w"], name=n("conv_fwd"), s=s)
    y_c = _pool_fwd(proj, w["pool_wbd"], w["pool_scale"], name=n("pool_fwd"), s=s)
    y_d, lse_d = _swa_fwd(proj, w["hp_swa"], name=n("swa_fwd"), s=s, scale=1.0 / math.sqrt(HEAD))

    def mix(ya, yb, yc, yd, mn):
        return (jnp.concatenate([_gn(_unpad_heads(ya, 4)), _gn(yb), _gn(yc), _gn(_unpad_heads(yd, 4))], axis=1) * mn,)

    mixed = _rowwise(mix, name=n("group_norm"), rows=s,
                     ins=[("heads", y_a), ("row", y_b, 256, 0), ("row", y_c, 256, 0), ("row", y_d, 512, 0),
                          ("full", w["mix_norm"])], outs=[(D_MODEL, BF16)])[0]
    x1 = _mm(mixed, w["w_o"], res=x, name=n("out_proj"))
    saved = dict(x=x, h=h, proj=proj, cqn=cqn, ckvn=ckvn, qm=qm, km=km, vm=vm, y_a=y_a, lse_a=lse_a, lse_row=lse_row,
                 y_b=y_b, y_c=y_c, y_d=y_d, lse_d=lse_d, mixed=mixed)
    return x1, saved


def _ffn_fwd(x1, w, l):
    s = x1.shape[0]
    n = lambda t: f"l{l}_{t}"
    h2 = _rms_fwd(("row", x1, D_MODEL, 0), w["ffn_norm"], name=n("ffn_norm"), rows=s, width=D_MODEL)

    def swiglu(gu):
        g, u = gu[:, :GU_TILE], gu[:, GU_TILE:]
        return gu, g * jax.nn.sigmoid(g) * u

    gu, act = _mm(h2, w["w_gate_up"], tn=2 * GU_TILE, name=n("gate_up"),
                  epilogue=(swiglu, [], [(2 * D_FF, BF16), (D_FF, BF16)]))
    x2 = _mm(act, w["w_down"], res=x1, tk=D_FF // 2, name=n("down"))
    return x2, dict(x1=x1, h2=h2, gu=gu, act=act)


def _ffn_bwd_down(dx2, sv, w, l):
    n = lambda t: f"l{l}_{t}"

    def swiglu_bwd(da, gu):
        gt, u = gu[:, :GU_TILE].astype(F32), gu[:, GU_TILE:].astype(F32)
        sg = jax.nn.sigmoid(gt)
        return (jnp.concatenate([da * u * sg * (1.0 + gt * (1.0 - sg)), da * gt * sg], axis=1),)

    dgu = _mm(dx2[1], w["w_down"], tb=True, tm=2048, tn=GU_TILE, name=n("d_act"),
              epilogue=(swiglu_bwd, [sv["gu"]], [(2 * D_FF, BF16)]))[0]
    g = dict(w_down=_mm(sv["act"], dx2[1], ta=True, tm=D_FF // 2, name=n("dw_down")))
    return dgu, g


def _ffn_bwd_up(dx2, dgu, sv, w, l):
    s = dgu.shape[0]
    n = lambda t: f"l{l}_{t}"
    dh2 = _mm(dgu, w["w_gate_up"], tb=True, tk=D_FF // 2, name=n("d_h2"))
    g = dict(w_gate_up=_mm(sv["h2"], dgu, ta=True, tn=D_FF // 2, name=n("dw_gate_up")))
    dx1, dx1_b, g["ffn_norm"] = _rms_bwd(("row", sv["x1"], D_MODEL, 0), w["ffn_norm"], dh2, dx2[0],
                                         name=n("ffn_norm_bwd"), rows=s, width=D_MODEL, out_dtypes=(F32, BF16))
    return (dx1, dx1_b), g


def _mixer_bwd_out(dx1, sv, w, l):
    s = dx1[1].shape[0]
    n = lambda t: f"l{l}_{t}"
    dmixed = _mm(dx1[1], w["w_o"], tb=True, name=n("d_mixed"))
    g = dict(w_o=_mm(sv["mixed"], dx1[1], ta=True, name=n("dw_o")))

    def mix_bwd(dm, ya, yb, yc, yd, mn):
        outs, dmn = [], []
        for i, y in enumerate((_unpad_heads(ya, 4), yb, yc, _unpad_heads(yd, 4))):
            lo, hi = i * D_GROUP, (i + 1) * D_GROUP
            r = _rstd(y, D_GROUP)
            nrm = y * r
            dmg = dm[:, lo:hi]
            dn = dmg * mn[:, lo:hi]
            dy = r * (dn - nrm * (jnp.sum(dn * nrm, axis=-1, keepdims=True) * (1.0 / D_GROUP)))
            outs.append(_pad_heads(dy, 4) if i in (0, 3) else dy)
            dmn.append(jnp.sum(dmg * nrm, axis=0, keepdims=True))
        return (*outs, jnp.concatenate(dmn, axis=1))

    dy_a, dy_b, dy_c, dy_d, g["mix_norm"] = _rowwise(
        mix_bwd, name=n("group_norm_bwd"), rows=s,
        ins=[("row", dmixed, D_MODEL, 0), ("heads", sv["y_a"]), ("row", sv["y_b"], 256, 0),
             ("row", sv["y_c"], 256, 0), ("row", sv["y_d"], 512, 0), ("full", w["mix_norm"])],
        outs=[(512, F32, MLA_HEADS), (256, F32), (256, F32), (512, F32)], reds=[(1, D_MODEL)])
    return (dy_a, dy_b, dy_c, dy_d), g


def _mixer_bwd_in(dx1, dys, sv, w, tabs, l):
    s = dx1[0].shape[0]
    tq, tkc, tks = tabs
    n = lambda t: f"l{l}_{t}"
    dy_a, dy_b, dy_c, dy_d = dys
    g = {}

    proj = sv["proj"]
    dq_sw, dk_sw, dv_sw, dsink = _swa_bwd(proj, sv["y_d"], dy_d, sv["lse_d"], w["hp_swa"], name=n("swa_bwd"), s=s,
                                          scale=1.0 / math.sqrt(HEAD))
    g["swa_sinks"] = dsink[:, 0, 0]

    qkv = (sv["qm"], sv["km"], sv["vm"])
    dqm, dvec_a = _mla_dq(*qkv, sv["y_a"], dy_a, sv["lse_a"], w["hp_swa"], name=n("mla_dq"), s=s, nh=MLA_HEADS)
    dkm, dvm = _mla_dkv(*qkv, dy_a, sv["lse_row"], dvec_a.reshape(MLA_HEADS, 1, s),
                        name=n("mla_dkv"), s=s, nh=MLA_HEADS)

    def prep_bwd(dq, dk, dv, tqv, tc, ts):
        dkb = dk[:, 0:128] + dk[:, 128:256] + dk[:, 256:384] + dk[:, 384:512]
        return dq * tqv, jnp.concatenate([dk, dv.astype(F32)], axis=1), dkb * tc, dkb * ts

    dq_ext, dkv_ext, dkr, dkrp = _rowwise(
        prep_bwd, name=n("mla_prep_bwd"), rows=s,
        ins=[("heads", dqm), ("heads", dkm), ("heads", dvm), ("row", tq, 512, 0), ("row", tkc, 128, 0), ("row", tks, 128, 0)],
        outs=[(512, BF16), (1024, BF16), (128, BF16), (128, BF16)])
    dcqn = _mm(dq_ext, w["w_uq"], tb=True, name=n("d_cqn"))
    g["w_uq"] = _mm(sv["cqn"], dq_ext, ta=True, name=n("dw_uq"))
    dckvn = _mm(dkv_ext, w["w_ukv"], tb=True, name=n("d_ckvn"))
    g["w_ukv"] = _mm(sv["ckvn"], dkv_ext, ta=True, name=n("dw_ukv"))
    dcq, g["mla_q_norm"] = _rms_bwd(("row", proj, 256, C_CQ[0] // 256), w["mla_q_norm"], dcqn, None,
                                    name=n("q_norm_bwd"), rows=s, width=256, out_dtypes=(BF16,))
    dckv, g["mla_kv_norm"] = _rms_bwd(("row", proj, 128, C_CKV[0] // 128), w["mla_kv_norm"], dckvn, None,
                                      name=n("kv_norm_bwd"), rows=s, width=128, out_dtypes=(BF16,))

    dgb, dgc, duc, g["conv_w"] = _conv_bwd(dy_b, proj, w["conv_w"], name=n("conv_bwd"), s=s)
    dup, g["pool_wbd"], g["pool_scale"] = _pool_bwd(dy_c, proj, w["pool_wbd"], w["pool_scale"], name=n("pool_bwd"), s=s)

    dproj = jnp.concatenate([dq_sw, dcq, dgb, dgc, duc, dup, dk_sw.astype(BF16), dv_sw.astype(BF16), dckv, dkr, dkrp],
                            axis=1)
    dh = _mm(dproj, w["w_in"], tb=True, name=n("d_h"))
    g["w_in"] = _mm(sv["h"], dproj, ta=True, name=n("dw_in"))
    dx0, dx0_b, g["attn_norm"] = _rms_bwd(("row", sv["x"], D_MODEL, 0), w["attn_norm"], dh, dx1[0],
                                          name=n("attn_norm_bwd"), rows=s, width=D_MODEL, out_dtypes=(F32, BF16))
    return (dx0, dx0_b), g


def _loss_head(x, target, g, *, s):
    def fn(xv, tv, gv):
        r = _rstd(xv, D_MODEL)
        e = xv * r * gv - tv
        part = jnp.sum(jnp.sum(e * e, axis=1, keepdims=True), axis=0, keepdims=True) * (0.5 / D_MODEL)
        dy = e * (1.0 / D_MODEL)
        dyg = dy * gv
        dx = r * dyg - xv * (r * r * r) * (jnp.sum(dyg * xv, axis=-1, keepdims=True) * (1.0 / D_MODEL))
        return dx, dx, jnp.sum(dy * xv * r, axis=0, keepdims=True), jnp.broadcast_to(part, (1, LANES))

    return _rowwise(fn, name="loss_head", rows=s,
                    ins=[("row", x, D_MODEL, 0), ("row", target, D_MODEL, 0), ("full", g)],
                    outs=[(D_MODEL, F32), (D_MODEL, BF16)], reds=[(1, D_MODEL), (1, LANES)])


def _alibi_slopes(n):
    return np.asarray([2.0 ** (-8.0 * (i + 1) / n) for i in range(n)], dtype=np.float32)


MIXER_WEIGHTS = ("w_in", "w_uq", "w_ukv", "conv_w", "w_o")
FFN_WEIGHTS = ("w_gate_up", "w_down")


def _mixer_weights(full, rep, l):
    pw = rep["pool_w"][l]
    z = jnp.zeros((HEAD, HEAD), F32)
    wbd = jnp.stack([jnp.block([[pw[2 * j], z], [z, pw[2 * j + 1]]]) for j in range(2)])
    return dict(
        attn_norm=rep["attn_norm"][l][None], w_in=full["w_in"], mla_q_norm=rep["mla_q_norm"][l][None],
        w_uq=full["w_uq"], mla_kv_norm=rep["mla_kv_norm"][l][None], w_ukv=full["w_ukv"],
        conv_w=full["conv_w"], pool_wbd=wbd, pool_scale=rep["pool_scale"][l][None],
        mix_norm=rep["mix_norm"][l][None], w_o=full["w_o"],
        hp_swa=jnp.stack([jnp.asarray(_alibi_slopes(SWA_HEADS)), rep["swa_sinks"][l]], axis=1))


def _ffn_weights(full, rep, l):
    return dict(ffn_norm=rep["ffn_norm"][l][None], w_gate_up=full["w_gate_up"], w_down=full["w_down"])


def _small_grads(g):
    rows = ("attn_norm", "mla_q_norm", "mla_kv_norm", "pool_scale", "ffn_norm", "mix_norm")
    out = {nm: g[nm][0] for nm in rows if nm in g}
    if "swa_sinks" in g:
        out["swa_sinks"] = g["swa_sinks"]
    if "pool_wbd" in g:
        e = g["pool_wbd"]
        out["pool_w"] = jnp.stack([e[j // 2][HEAD * (j % 2):HEAD * (j % 2 + 1), HEAD * (j % 2):HEAD * (j % 2 + 1)]
                                   for j in range(4)])
    return out


def kernel(x, attn_norm, w_in, mla_q_norm, w_uq, mla_kv_norm, w_ukv, conv_w, pool_w, pool_scale, swa_sinks, mix_norm, w_o, ffn_norm, w_gate_up, w_down, final_norm, loss_target, m_attn_norm, m_w_in, m_mla_q_norm, m_w_uq, m_mla_kv_norm, m_w_ukv, m_conv_w, m_pool_w, m_pool_scale, m_swa_sinks, m_mix_norm, m_w_o, m_ffn_norm, m_w_gate_up, m_w_down, m_final_norm, v_attn_norm, v_w_in, v_mla_q_norm, v_w_uq, v_mla_kv_norm, v_w_ukv, v_conv_w, v_pool_w, v_pool_scale, v_swa_sinks, v_mix_norm, v_w_o, v_ffn_norm, v_w_gate_up, v_w_down, v_final_norm):
    given = dict(locals())
    sh_names = [nm for nm, _, _ in SHARDED]
    sh_axis = {nm: ax - 1 for nm, _, ax in SHARDED}
    rep_names = [nm for nm, _ in REPLICATED]
    rep_shapes = [shp for _, shp in REPLICATED]
    rep = {nm: given[nm] for nm in rep_names if nm != "loss"}
    me = 4 * lax.axis_index("x") + 2 * lax.axis_index("y") + lax.axis_index("c")
    my_chip = 2 * lax.axis_index("x") + lax.axis_index("y")
    core = lax.axis_index("c").astype(jnp.int32).reshape(1)

    def behind(token, a):
        return a + token[0, 0].astype(a.dtype)

    def wire(nm, l):
        if nm == "conv_w":
            return lax.bitcast_convert_type(given[nm][l], BF16).reshape(3, -1)
        return given[nm][l].astype(BF16)

    def whole(nm, g, tag):
        if nm in LAYOUTS:
            return _extend(nm, g, name=f"extend_{nm}_{tag}")
        if nm == "conv_w":
            g = lax.bitcast_convert_type(g.reshape(N_DEV, 3, -1, 2), F32)
        return _shards_to_full(g, sh_axis[nm])

    def gather_start(names, l, after, tag):
        srcs = [wire(nm, l) for nm in names]
        return _start_copies(_plan_gather, N_DEV - 1, srcs, [(N_DEV,) + a.shape for a in srcs], after, name=f"start_gather_{tag}")

    def gather_wait(names, handle, after, tag):
        srcs, lands = _wait_copies(_plan_gather, handle, after, name=f"wait_gather_{tag}")
        return {nm: whole(nm, lax.dynamic_update_index_in_dim(land, src, me, 0), tag)
                for nm, src, land in zip(names, srcs, lands)}

    got = _all_gather([wire(nm, 0) for nm in MIXER_WEIGHTS], name="gather_mixer0")
    full_m0 = {nm: whole(nm, g, "mixer0") for nm, g in zip(MIXER_WEIGHTS, got)}
    h_f0, tok = gather_start(FFN_WEIGHTS, 0, [], "ffn0")
    h_l1, tok = gather_start(MIXER_WEIGHTS + FFN_WEIGHTS, 1, [tok], "layer1")

    xs, target = x[0], loss_target[0]
    s = xs.shape[0]
    tabs = _rope_tables(s)
    wm, wf, svm, svf = [None] * DEPTH, [None] * DEPTH, [None] * DEPTH, [None] * DEPTH
    wm[0] = _mixer_weights(full_m0, rep, 0)
    wm[0]["attn_norm"] = behind(tok, wm[0]["attn_norm"])
    x1, svm[0] = _mixer_fwd(xs, wm[0], tabs, 0)
    wf[0] = _ffn_weights(gather_wait(FFN_WEIGHTS, h_f0, [x1], "ffn0"), rep, 0)
    x2, svf[0] = _ffn_fwd(x1, wf[0], 0)
    full_1 = gather_wait(MIXER_WEIGHTS + FFN_WEIGHTS, h_l1, [x2], "layer1")
    wm[1], wf[1] = _mixer_weights(full_1, rep, 1), _ffn_weights(full_1, rep, 1)
    x1, svm[1] = _mixer_fwd(x2, wm[1], tabs, 1)
    x2, svf[1] = _ffn_fwd(x1, wf[1], 1)
    dx_f, dx_b, d_final, loss = _loss_head(x2, target, rep["final_norm"][None], s=s)
    dx = (dx_f, dx_b)

    parts = {}

    def reduce_start(grads, l, after, tag):
        names = [nm for nm in sh_names if nm in grads]
        mine = [_fold_to_shards(nm, grads[nm], name=f"fold_{nm}_{l}") if nm in LAYOUTS
                else _full_to_shards(grads[nm], sh_axis[nm]) for nm in names]
        handle, token = _start_copies(_plan_sibling, 1, mine, [m.shape[1:] for m in mine], after, name=f"start_sibling_{tag}")
        return (names, l, handle), token

    def reduce_mid(state, after, tag):
        names, l, handle = state
        mine, theirs = _wait_copies(_plan_sibling, handle, after, name=f"wait_sibling_{tag}")
        sums = [_add_own(g, o, core, name=f"chip_sum_{nm}_{l}") for nm, g, o in zip(names, mine, theirs)]
        handle, token = _start_copies(_plan_chips, 3, sums, [a.shape for a in sums], [], name=f"start_chips_{tag}")
        return (names, l, handle), token

    def reduce_end(state, after, tag):
        names, l, handle = state
        sums, lands = _wait_copies(_plan_chips, handle, after, name=f"wait_chips_{tag}")
        for nm, own, land in zip(names, sums, lands):
            parts[nm, l] = lax.dynamic_update_index_in_dim(land, lax.dynamic_index_in_dim(own, my_chip, 0, keepdims=False),
                                                           my_chip, 0)

    small = [None] * DEPTH
    in_flight = []
    pending = None
    for l in reversed(range(DEPTH)):
        dgu, g_down = _ffn_bwd_down(dx, svf[l], wf[l], l)
        if pending is not None:
            state, token = reduce_mid(pending, [dgu], f"mixer{l + 1}")
            in_flight.append((state, f"mixer{l + 1}"))
            wf[l]["ffn_norm"] = behind(token, wf[l]["ffn_norm"])
        dx1, g_up = _ffn_bwd_up(dx, dgu, svf[l], wf[l], l)
        g_ffn = {**g_down, **g_up}
        state, token = reduce_start(g_ffn, l, [], f"ffn{l}")
        wm[l]["mix_norm"] = behind(token, wm[l]["mix_norm"])
        dys, g_out = _mixer_bwd_out(dx1, svm[l], wm[l], l)
        state, token = reduce_mid(state, [dys[0]], f"ffn{l}")
        in_flight.append((state, f"ffn{l}"))
        wm[l]["hp_swa"] = behind(token, wm[l]["hp_swa"])
        dx, g_in = _mixer_bwd_in(dx1, dys, svm[l], wm[l], tabs, l)
        g_mixer = {**g_out, **g_in}
        small[l] = _small_grads({**g_ffn, **g_mixer})
        pending, token = reduce_start(g_mixer, l, [], f"mixer{l}")
    last, token = reduce_mid(pending, [dx[0]], "mixer0")
    for state, tag in in_flight:
        reduce_end(state, [], tag)
    grad_x = dx[0]

    def adamw(nm, after):
        return _adamw([parts[nm, l] for l in range(DEPTH)], given[nm], given["m_" + nm], given["v_" + nm], after,
                      name=f"adamw_{nm}")

    sh_out = {nm: adamw(nm, token) for nm in FFN_WEIGHTS}

    grads = {nm: jnp.stack([small[l][nm] for l in range(DEPTH)]) for nm in rep_names if nm in small[0]}
    grads["final_norm"] = d_final[0]
    grads["loss"] = loss[0, :1]
    zero = jnp.zeros((1,), F32)
    small = _all_gather([behind(token, _pack([grads[nm] for nm in rep_names]))], name="gather_small_grads")
    packs = [_pack([given.get(pre + nm, zero) for nm in rep_names])[None] for pre in ("", "m_", "v_")]
    rep_res = _adamw(small, *packs, token, name="adamw_replicated")
    rep_out = [dict(zip(rep_names, _unpack(o[0], rep_shapes))) for o in rep_res]

    reduce_end(last, [rep_res[0], sh_out["w_down"][0]], "mixer0")
    sh_out.update({nm: adamw(nm, token) for nm in MIXER_WEIGHTS})

    out = [rep_out[0]["loss"][0], grad_x[None]]
    for i in range(4):
        out += [sh_out[nm][i] if nm in sh_axis else rep_out[i][nm] for nm in WEIGHT_ORDER]
    return tuple(out)
```

```python
import functools
import math

import numpy as np
import jax
import jax.numpy as jnp
from jax import lax
from jax.experimental import pallas as pl
from jax.experimental.pallas import tpu as pltpu

F32 = jnp.float32
BF16 = jnp.bfloat16

D_MODEL = 1024
DEPTH = 2
D_GROUP = 256
MLA_HEADS = 4
MLA_NOPE = 64
MLA_ROPE = 32
ROPE_THETA = 10000.0
POOL_WINDOWS = (2, 4, 8, 16)
SWA_HEADS = 4
SWA_KV_HEADS = 2
SWA_WINDOW = 128
D_FF = 2816
GU_TILE = 256
RMS_EPS = 1e-6
LANES = 128
HEAD = 64
VMEM_LIMIT = 48 * 1024 * 1024
NEG = -1e30

ADAM_LR = 0.001
ADAM_B1 = 0.9
ADAM_B2 = 0.999
ADAM_EPS = 1e-08
ADAM_WD = 0.01
ADAM_STEP = 10

N_DEV = 8
PACK_COLS = 1024

C_QSW, C_CQ, C_GB, C_GC, C_UCONV, C_UPOOL = (0, 512), (512, 256), (768, 256), (1024, 256), (1280, 256), (1536, 256)
C_KSW, C_VSW, C_CKV, C_KR, C_KRP = (1792, 256), (2048, 256), (2304, 128), (2432, 128), (2560, 128)
D_IN_EXT = 2688

SHARDED = (("w_in", (DEPTH, 1024, 244), 2), ("w_uq", (DEPTH, 256, 48), 2), ("w_ukv", (DEPTH, 128, 64), 2),
           ("conv_w", (DEPTH, 3, 32), 2), ("w_o", (DEPTH, 128, 1024), 1), ("w_gate_up", (DEPTH, 1024, 704), 2),
           ("w_down", (DEPTH, 352, 1024), 1))
REPLICATED = (("attn_norm", (DEPTH, 1024)), ("mla_q_norm", (DEPTH, 256)), ("mla_kv_norm", (DEPTH, 128)),
              ("pool_w", (DEPTH, 4, 64, 64)), ("pool_scale", (DEPTH, 256)), ("swa_sinks", (DEPTH, 4)),
              ("mix_norm", (DEPTH, 1024)), ("ffn_norm", (DEPTH, 1024)), ("final_norm", (1024,)), ("loss", (1,)))
WEIGHT_ORDER = ("attn_norm", "w_in", "mla_q_norm", "w_uq", "mla_kv_norm", "w_ukv", "conv_w", "pool_w", "pool_scale",
                "swa_sinks", "mix_norm", "w_o", "ffn_norm", "w_gate_up", "w_down", "final_norm")


def _params(sem):
    return pltpu.CompilerParams(dimension_semantics=sem, vmem_limit_bytes=VMEM_LIMIT)


def _pick(dim, target):
    if dim <= target:
        return dim
    best = None
    for t in range(LANES, target + 1, LANES):
        if dim % t == 0:
            best = t
    assert best is not None, (dim, target)
    return best


def _mm(a, b, *, name, ta=False, tb=False, res=None, out_dtype=F32, tm=1024, tn=1024, tk=1024, epilogue=None):
    m, k = (a.shape[1], a.shape[0]) if ta else a.shape
    n = b.shape[0] if tb else b.shape[1]
    assert (b.shape[1] if tb else b.shape[0]) == k
    tm, tn, tk = _pick(m, tm), _pick(n, tn), _pick(k, tk)
    nj, nk = n // tn, k // tk
    dims = (((0 if ta else 1,), (1 if tb else 0,)), ((), ()))
    fn, extra, outs = epilogue if epilogue is not None else (None, [], [(n, out_dtype)])
    if res is not None:
        assert epilogue is None
        fn, extra = (lambda acc, r: (acc + r,)), [res]
    n_in, n_out = 2 + len(extra), len(outs)

    def body(*refs):
        a_ref, b_ref, acc_ref = refs[0], refs[1], refs[-1]
        kk = pl.program_id(2)

        @pl.when(kk == 0)
        def _():
            acc_ref[...] = jnp.zeros_like(acc_ref)

        acc_ref[...] += lax.dot_general(a_ref[...].astype(BF16), b_ref[...].astype(BF16), dims,
                                        preferred_element_type=F32)

        @pl.when(kk == nk - 1)
        def _():
            tiles = (acc_ref[...],) if fn is None else fn(acc_ref[...], *[r[...] for r in refs[2:n_in]])
            for o_ref, tile in zip(refs[n_in:n_in + n_out], tiles):
                o_ref[...] = tile.astype(o_ref.dtype)

    def col_tiles(width):
        assert width % (nj * LANES) == 0, (width, nj)
        return pl.BlockSpec((tm, width // nj), lambda i, j, kk: (i, j))

    a_spec = pl.BlockSpec((tk, tm), lambda i, j, kk: (kk, i)) if ta else pl.BlockSpec((tm, tk), lambda i, j, kk: (i, kk))
    b_spec = pl.BlockSpec((tn, tk), lambda i, j, kk: (j, kk)) if tb else pl.BlockSpec((tk, tn), lambda i, j, kk: (kk, j))
    res_ = pl.pallas_call(
        body, name=name, grid=(m // tm, nj, nk), in_specs=[a_spec, b_spec] + [col_tiles(e.shape[1]) for e in extra],
        out_specs=[col_tiles(w) for w, _ in outs],
        out_shape=[jax.ShapeDtypeStruct((m, w), dt) for w, dt in outs],
        scratch_shapes=[pltpu.VMEM((tm, tn), F32)],
        compiler_params=_params(("parallel", "parallel", "arbitrary")),
    )(a, b, *extra)
    return res_[0] if epilogue is None else res_


def _rowwise(fn, *, name, rows, ins, outs, reds=(), tm=512):
    tm = min(tm, rows)
    assert rows % tm == 0
    n_in, n_out = len(ins), len(outs)

    def body(*refs):
        vals = [jnp.concatenate([r[h] for h in range(r.shape[0])], axis=1) if spec[0] == "heads" else r[...]
                for spec, r in zip(ins, refs[:n_in])]
        res = fn(*vals)
        for out, r, v in zip(outs, refs[n_in:n_in + n_out], res[:n_out]):
            if len(out) >= 3:
                for h in range(out[2]):
                    piece = v[:, h * LANES:(h + 1) * LANES]
                    r[h] = (piece.T if len(out) == 4 else piece).astype(r.dtype)
            else:
                r[...] = v.astype(r.dtype)
        if reds:
            @pl.when(pl.program_id(0) == 0)
            def _():
                for r in refs[n_in + n_out:]:
                    r[...] = jnp.zeros_like(r)

            for r, v in zip(refs[n_in + n_out:], res[n_out:]):
                r[...] += v

    in_specs, args = [], []
    for spec in ins:
        if spec[0] == "row":
            _, arr, width, blk = spec
            in_specs.append(pl.BlockSpec((tm, width), functools.partial(lambda i, blk: (i, blk), blk=blk)))
        elif spec[0] == "heads":
            arr = spec[1]
            in_specs.append(pl.BlockSpec((arr.shape[0], tm, LANES), lambda i: (0, i, 0)))
        else:
            arr = spec[1]
            in_specs.append(pl.BlockSpec(arr.shape, functools.partial(lambda i, nd: (0,) * nd, nd=arr.ndim)))
        args.append(arr)
    def out_spec(o):
        if len(o) == 4:
            return pl.BlockSpec((o[2], LANES, tm), lambda i: (0, 0, i)), (o[2], LANES, rows)
        if len(o) == 3:
            return pl.BlockSpec((o[2], tm, LANES), lambda i: (0, i, 0)), (o[2], rows, LANES)
        return pl.BlockSpec((tm, o[0]), lambda i: (i, 0)), (rows, o[0])

    out_specs = [out_spec(o)[0] for o in outs]
    out_shape = [jax.ShapeDtypeStruct(out_spec(o)[1], o[1]) for o in outs]
    out_specs += [pl.BlockSpec((r, w), lambda i: (0, 0)) for r, w in reds]
    out_shape += [jax.ShapeDtypeStruct((r, w), F32) for r, w in reds]
    return pl.pallas_call(body, name=name, grid=(rows // tm,), in_specs=in_specs, out_specs=out_specs,
                          out_shape=out_shape, compiler_params=_params(("arbitrary",)))(*args)


def _rstd(x, n):
    return lax.rsqrt(jnp.sum(x * x, axis=-1, keepdims=True) * (1.0 / n) + RMS_EPS)


def _rms_fwd(x_spec, g, *, name, rows, width):
    def fn(x, gv):
        return (x * _rstd(x, width) * gv,)
    return _rowwise(fn, name=name, rows=rows, ins=[x_spec, ("full", g)], outs=[(width, BF16)])[0]


def _rms_bwd(x_spec, g, dy, res, *, name, rows, width, out_dtypes):
    def fn(x, gv, dyv, *rest):
        r = _rstd(x, width)
        dyg = dyv * gv
        dx = r * dyg - x * (r * r * r) * (jnp.sum(dyg * x, axis=-1, keepdims=True) * (1.0 / width))
        if rest:
            dx = dx + rest[0]
        return (dx,) * len(out_dtypes) + (jnp.sum(dyv * x * r, axis=0, keepdims=True),)

    ins = [x_spec, ("full", g), ("row", dy, width, 0)]
    if res is not None:
        ins.append(("row", res, width, 0))
    return _rowwise(fn, name=name, rows=rows, ins=ins, outs=[(width, dt) for dt in out_dtypes], reds=[(1, width)])


NT_DIMS = (((1,), (1,)), ((), ()))
TN_DIMS = (((0,), (0,)), ((), ()))
BNT_DIMS = (((2,), (2,)), ((0,), (0,)))
BNN_DIMS = (((2,), (1,)), ((0,), (0,)))


def _mla_tile(s):
    return min(512, s)


def _causal(shape, query_axis):
    return lax.broadcasted_iota(jnp.int32, shape, query_axis) >= lax.broadcasted_iota(jnp.int32, shape, 1 - query_axis)


def _mla_fwd(qa, ka, vta, *, name, s, nh):
    t = _mla_tile(s)
    nq = s // t

    def body(q_ref, k_ref, vt_ref, o_ref, lse_ref, m_s, acc_s):
        i, j = pl.program_id(0), pl.program_id(1)

        @pl.when(j == 0)
        def _():
            m_s[...] = jnp.full_like(m_s, NEG)
            acc_s[...] = jnp.zeros_like(acc_s)

        def step(diag):
            sc = lax.dot_general(k_ref[...], q_ref[...], BNT_DIMS, preferred_element_type=F32)
            if diag:
                sc = jnp.where(_causal(sc.shape[1:], 1)[None], sc, NEG)
            m_prev = m_s[...]
            m_new = jnp.maximum(m_prev, jnp.max(sc, axis=1, keepdims=True))
            p = jnp.exp(sc - m_new).astype(BF16)
            acc_s[...] = (jnp.exp(m_prev - m_new) * acc_s[...]
                          + lax.dot_general(vt_ref[...], p, BNN_DIMS, preferred_element_type=F32))
            m_s[...] = m_new

        pl.when(j < i)(functools.partial(step, False))
        pl.when(j == i)(functools.partial(step, True))

        @pl.when(j == nq - 1)
        def _():
            row = lax.broadcasted_iota(jnp.int32, (LANES, t), 0)
            for h in range(nh):
                acc = acc_s[h]
                l = acc[HEAD:HEAD + 1, :]
                o_ref[h] = jnp.where(row < HEAD, acc / l, 0.0).T
                lse_ref[h] = m_s[h] + jnp.log(l)

    q_spec = pl.BlockSpec((nh, t, LANES), lambda i, j: (0, i, 0))
    k_spec = pl.BlockSpec((nh, t, LANES), lambda i, j: (0, jnp.minimum(j, i), 0))
    vt_spec = pl.BlockSpec((nh, LANES, t), lambda i, j: (0, 0, jnp.minimum(j, i)))
    return pl.pallas_call(
        body, name=name, grid=(nq, nq), in_specs=[q_spec, k_spec, vt_spec],
        out_specs=[q_spec, pl.BlockSpec((nh, 1, t), lambda i, j: (0, 0, i))],
        out_shape=[jax.ShapeDtypeStruct((nh, s, LANES), F32), jax.ShapeDtypeStruct((nh, 1, s), F32)],
        scratch_shapes=[pltpu.VMEM((nh, 1, t), F32), pltpu.VMEM((nh, LANES, t), F32)],
        compiler_params=_params(("parallel", "arbitrary")),
    )(qa, ka, vta)


def _mla_dq(qa, ka, va, o, do, lse, after, *, name, s, nh):
    t = _mla_tile(s)
    nq = s // t

    def body(q_ref, k_ref, v_ref, o_ref, do_ref, lse_ref, dq_ref, d_ref, acc_s):
        i, j = pl.program_id(0), pl.program_id(1)

        @pl.when(j == 0)
        def _():
            acc_s[...] = jnp.zeros_like(acc_s)
            d_ref[...] = jnp.sum(do_ref[...] * o_ref[...], axis=2, keepdims=True)

        def step(diag):
            k = k_ref[...]
            sc = lax.dot_general(q_ref[...], k, BNT_DIMS, preferred_element_type=F32)
            if diag:
                sc = jnp.where(_causal(sc.shape[1:], 0)[None], sc, NEG)
            p = jnp.exp(sc - lse_ref[...])
            dp = lax.dot_general(do_ref[...].astype(BF16), v_ref[...], BNT_DIMS, preferred_element_type=F32)
            ds = p * (dp - d_ref[...])
            acc_s[...] += lax.dot_general(ds.astype(BF16), k, BNN_DIMS, preferred_element_type=F32)

        pl.when(j < i)(functools.partial(step, False))
        pl.when(j == i)(functools.partial(step, True))

        @pl.when(j == nq - 1)
        def _():
            dq_ref[...] = acc_s[...]

    q_spec = pl.BlockSpec((nh, t, LANES), lambda i, j: (0, i, 0))
    kv_spec = pl.BlockSpec((nh, t, LANES), lambda i, j: (0, jnp.minimum(j, i), 0))
    vec_spec = pl.BlockSpec((nh, t, 1), lambda i, j: (0, i, 0))

    def body_after(q_ref, k_ref, v_ref, o_ref, do_ref, lse_ref, after_ref, *rest):
        body(q_ref, k_ref, v_ref, o_ref, do_ref, lse_ref, *rest)

    return pl.pallas_call(
        body_after, name=name, grid=(nq, nq),
        in_specs=[q_spec, kv_spec, kv_spec, q_spec, q_spec, vec_spec, pl.BlockSpec(memory_space=pl.ANY)],
        out_specs=[q_spec, vec_spec],
        out_shape=[jax.ShapeDtypeStruct((nh, s, LANES), F32), jax.ShapeDtypeStruct((nh, s, 1), F32)],
        scratch_shapes=[pltpu.VMEM((nh, t, LANES), F32)],
        compiler_params=_params(("parallel", "arbitrary")),
    )(qa, ka, va, o, do, lse, after)


def _mla_dkv(qa, ka, va, do, lse_row, d_row, *, name, s, nh):
    t = _mla_tile(s)
    nq = s // t

    def body(q_ref, k_ref, v_ref, do_ref, lse_ref, d_ref, dk_ref, dv_ref, dk_s, dv_s):
        kb, j = pl.program_id(0), pl.program_id(1)

        @pl.when(j == 0)
        def _():
            dk_s[...] = jnp.zeros_like(dk_s)
            dv_s[...] = jnp.zeros_like(dv_s)

        def step(diag):
            q = q_ref[...]
            do_b = do_ref[...].astype(BF16)
            sc = lax.dot_general(k_ref[...], q, BNT_DIMS, preferred_element_type=F32)
            if diag:
                sc = jnp.where(_causal(sc.shape[1:], 1)[None], sc, NEG)
            p = jnp.exp(sc - lse_ref[...])
            dv_s[...] += lax.dot_general(p.astype(BF16), do_b, BNN_DIMS, preferred_element_type=F32)
            dp = lax.dot_general(v_ref[...], do_b, BNT_DIMS, preferred_element_type=F32)
            ds = p * (dp - d_ref[...])
            dk_s[...] += lax.dot_general(ds.astype(BF16), q, BNN_DIMS, preferred_element_type=F32)

        pl.when(j > kb)(functools.partial(step, False))
        pl.when(j == kb)(functools.partial(step, True))

        @pl.when(j == nq - 1)
        def _():
            dk_ref[...] = dk_s[...]
            dv_ref[...] = dv_s[...].astype(dv_ref.dtype)

    q_spec = pl.BlockSpec((nh, t, LANES), lambda kb, j: (0, jnp.maximum(j, kb), 0))
    kv_spec = pl.BlockSpec((nh, t, LANES), lambda kb, j: (0, kb, 0))
    row_spec = pl.BlockSpec((nh, 1, t), lambda kb, j: (0, 0, jnp.maximum(j, kb)))
    return pl.pallas_call(
        body, name=name, grid=(nq, nq), in_specs=[q_spec, kv_spec, kv_spec, q_spec, row_spec, row_spec],
        out_specs=[kv_spec, kv_spec],
        out_shape=[jax.ShapeDtypeStruct((nh, s, LANES), F32), jax.ShapeDtypeStruct((nh, s, LANES), BF16)],
        scratch_shapes=[pltpu.VMEM((nh, t, LANES), F32), pltpu.VMEM((nh, t, LANES), F32)],
        compiler_params=_params(("parallel", "arbitrary")),
    )(qa, ka, va, do, lse_row, d_row)


SWA_PIECE = 128
SWA_KEYS = 2 * SWA_PIECE


def _swa_block(s):
    return min(512, s)


def _swa_piece(hp_ref, h, q, k_ref, v_ref, qpos0, scale):
    kstart = pl.multiple_of(jnp.maximum(qpos0 - SWA_PIECE, 0), SWA_PIECE)
    k = k_ref[pl.ds(kstart, SWA_KEYS), :].astype(BF16)
    v = v_ref[pl.ds(kstart, SWA_KEYS), :].astype(BF16)
    sc = lax.dot_general(q, k, NT_DIMS, preferred_element_type=F32)
    dist = (qpos0 + lax.broadcasted_iota(jnp.int32, sc.shape, 0)) - (kstart + lax.broadcasted_iota(jnp.int32, sc.shape, 1))
    sc = sc * scale - hp_ref[h, 0] * dist.astype(F32)
    sc = jnp.where((dist >= 0) & (dist < SWA_WINDOW), sc, NEG)
    return kstart, k, v, sc


def _swa_fwd(proj, hp, *, name, s, scale):
    tb = _swa_block(s)
    group = SWA_HEADS // SWA_KV_HEADS
    q_off, k_off, v_off = C_QSW[0] // LANES, C_KSW[0] // LANES, C_VSW[0] // LANES

    def body(hp_ref, q_ref, k_ref, v_ref, o_ref, lse_ref):
        h, i = pl.program_id(0), pl.program_id(1)
        sink = hp_ref[h, 1]
        for r in range(0, tb, SWA_PIECE):
            rows = pl.ds(r, SWA_PIECE)
            _, _, v, sc = _swa_piece(hp_ref, h, q_ref[rows, :].astype(BF16), k_ref, v_ref, i * tb + r, scale)
            m = jnp.maximum(jnp.max(sc, axis=1, keepdims=True), sink)
            p = jnp.exp(sc - m)
            l = jnp.sum(p, axis=1, keepdims=True) + jnp.exp(sink - m)
            o_ref[rows, :] = jnp.dot(p.astype(BF16), v, preferred_element_type=F32) / l
            lse_ref[rows, :] = m + jnp.log(l)

    whole = lambda off: pl.BlockSpec((s, LANES), lambda h, i: (0, off + h // group))
    return pl.pallas_call(
        body, name=name, grid=(SWA_HEADS, s // tb),
        in_specs=[pl.BlockSpec(memory_space=pltpu.SMEM), pl.BlockSpec((tb, LANES), lambda h, i: (i, q_off + h)),
                  whole(k_off), whole(v_off)],
        out_specs=[pl.BlockSpec((tb, LANES), lambda h, i: (i, h)), pl.BlockSpec((None, tb, 1), lambda h, i: (h, i, 0))],
        out_shape=[jax.ShapeDtypeStruct((s, SWA_HEADS * LANES), F32), jax.ShapeDtypeStruct((SWA_HEADS, s, 1), F32)],
        compiler_params=_params(("parallel", "parallel")),
    )(hp, proj, proj, proj)


def _swa_bwd(proj, o, do, lse, hp, *, name, s, scale):
    tb = _swa_block(s)
    nqb = s // tb
    group = SWA_HEADS // SWA_KV_HEADS
    q_off, k_off, v_off = C_QSW[0] // LANES, C_KSW[0] // LANES, C_VSW[0] // LANES

    def body(hp_ref, q_ref, k_ref, v_ref, o_ref, do_ref, lse_ref, dq_ref, dk_ref, dv_ref, dsink_ref):
        kh, g, i = pl.program_id(0), pl.program_id(1), pl.program_id(2)
        h = kh * group + g
        sink = hp_ref[h, 1]

        @pl.when((g == 0) & (i == 0))
        def _():
            dk_ref[...] = jnp.zeros_like(dk_ref)
            dv_ref[...] = jnp.zeros_like(dv_ref)

        @pl.when(i == 0)
        def _():
            dsink_ref[...] = jnp.zeros_like(dsink_ref)

        for r in range(0, tb, SWA_PIECE):
            rows = pl.ds(r, SWA_PIECE)
            q = q_ref[rows, :].astype(BF16)
            dov = do_ref[rows, :]
            do_b = dov.astype(BF16)
            lse_r = lse_ref[rows, :]
            d_r = jnp.sum(dov * o_ref[rows, :], axis=1, keepdims=True)
            kstart, k, v, sc = _swa_piece(hp_ref, h, q, k_ref, v_ref, i * tb + r, scale)
            p = jnp.exp(sc - lse_r)
            dp = lax.dot_general(do_b, v, NT_DIMS, preferred_element_type=F32)
            ds = (p * (dp - d_r)).astype(BF16)
            dq_ref[rows, :] = (jnp.dot(ds, k, preferred_element_type=F32) * scale).astype(dq_ref.dtype)
            win = pl.ds(kstart, SWA_KEYS)
            dk_ref[win, :] += lax.dot_general(ds, q, TN_DIMS, preferred_element_type=F32) * scale
            dv_ref[win, :] += lax.dot_general(p.astype(BF16), do_b, TN_DIMS, preferred_element_type=F32)
            part = jnp.sum(-jnp.exp(sink - lse_r) * d_r, axis=0, keepdims=True)
            dsink_ref[...] += jnp.broadcast_to(part, (1, LANES))

    whole = lambda off: pl.BlockSpec((s, LANES), lambda kh, g, i: (0, off + kh))
    q_map = lambda kh, g, i: (i, kh * group + g)
    return pl.pallas_call(
        body, name=name, grid=(SWA_KV_HEADS, group, nqb),
        in_specs=[pl.BlockSpec(memory_space=pltpu.SMEM),
                  pl.BlockSpec((tb, LANES), lambda kh, g, i: (i, q_off + kh * group + g)), whole(k_off), whole(v_off),
                  pl.BlockSpec((tb, LANES), q_map), pl.BlockSpec((tb, LANES), q_map),
                  pl.BlockSpec((None, tb, 1), lambda kh, g, i: (kh * group + g, i, 0))],
        out_specs=[pl.BlockSpec((tb, LANES), q_map), whole(0), whole(0),
                   pl.BlockSpec((None, 1, LANES), lambda kh, g, i: (kh * group + g, 0, 0))],
        out_shape=[jax.ShapeDtypeStruct((s, SWA_HEADS * LANES), BF16),
                   jax.ShapeDtypeStruct((s, SWA_KV_HEADS * LANES), F32), jax.ShapeDtypeStruct((s, SWA_KV_HEADS * LANES), F32),
                   jax.ShapeDtypeStruct((SWA_HEADS, 1, LANES), F32)],
        compiler_params=_params(("parallel", "arbitrary", "arbitrary")),
    )(hp, proj, proj, proj, o, do, lse)


def _shift_down(z, k):
    rows = lax.broadcasted_iota(jnp.int32, z.shape, 0)
    return jnp.where(rows >= k, pltpu.roll(z, k, 0), 0.0)


def _shift_up(z, k):
    n = z.shape[0]
    rows = lax.broadcasted_iota(jnp.int32, z.shape, 0)
    return jnp.where(rows < n - k, pltpu.roll(z, n - k, 0), 0.0)


def _rows3(a, b, c):
    r = lax.broadcasted_iota(jnp.int32, (3, a.shape[1]), 0)
    return jnp.where(r == 0, a, jnp.where(r == 1, b, c))


def _col_spec(s, off):
    return pl.BlockSpec((s, LANES), functools.partial(lambda j, off: (0, off + j), off=off))


def _conv_fwd(proj, conv_w, *, name, s):
    def body(gb_ref, gc_ref, u_ref, w_ref, y_ref):
        w0, w1, w2 = w_ref[0:1, :], w_ref[1:2, :], w_ref[2:3, :]
        z = gc_ref[...] * u_ref[...]
        c = w2 * z + w1 * _shift_down(z, 1) + w0 * _shift_down(z, 2)
        y_ref[...] = gb_ref[...] * c

    return pl.pallas_call(
        body, name=name, grid=(2,),
        in_specs=[_col_spec(s, C_GB[0] // LANES), _col_spec(s, C_GC[0] // LANES), _col_spec(s, C_UCONV[0] // LANES),
                  pl.BlockSpec((3, LANES), lambda j: (0, j))],
        out_specs=_col_spec(s, 0), out_shape=jax.ShapeDtypeStruct((s, D_GROUP), F32),
        compiler_params=_params(("parallel",)),
    )(proj, proj, proj, conv_w)


def _conv_bwd(dy, proj, conv_w, *, name, s):
    def body(dy_ref, gb_ref, gc_ref, u_ref, w_ref, dgb_ref, dgc_ref, du_ref, dw_ref):
        w0, w1, w2 = w_ref[0:1, :], w_ref[1:2, :], w_ref[2:3, :]
        gc, u, dyv = gc_ref[...], u_ref[...], dy_ref[...]
        z = gc * u
        z1, z2 = _shift_down(z, 1), _shift_down(z, 2)
        c = w2 * z + w1 * z1 + w0 * z2
        dgb_ref[...] = (dyv * c).astype(dgb_ref.dtype)
        dc = dyv * gb_ref[...]
        dz = w2 * dc + w1 * _shift_up(dc, 1) + w0 * _shift_up(dc, 2)
        dgc_ref[...] = (dz * u).astype(dgc_ref.dtype)
        du_ref[...] = (dz * gc).astype(du_ref.dtype)
        dw_ref[...] = _rows3(jnp.sum(dc * z2, axis=0, keepdims=True), jnp.sum(dc * z1, axis=0, keepdims=True),
                             jnp.sum(dc * z, axis=0, keepdims=True))

    act = jax.ShapeDtypeStruct((s, D_GROUP), BF16)
    return pl.pallas_call(
        body, name=name, grid=(2,),
        in_specs=[_col_spec(s, 0), _col_spec(s, C_GB[0] // LANES), _col_spec(s, C_GC[0] // LANES),
                  _col_spec(s, C_UCONV[0] // LANES), pl.BlockSpec((3, LANES), lambda j: (0, j))],
        out_specs=[_col_spec(s, 0), _col_spec(s, 0), _col_spec(s, 0), pl.BlockSpec((3, LANES), lambda j: (0, j))],
        out_shape=[act, act, act, jax.ShapeDtypeStruct((3, D_GROUP), F32)],
        compiler_params=_params(("parallel",)),
    )(dy, proj, proj, proj, conv_w)


def _pool_select(j, lane, a2, a4, a8, a16):
    lo = lane < HEAD
    return jnp.where(j == 0, jnp.where(lo, a2, a4), jnp.where(lo, a8, a16))


def _pooled(u, j):
    s2 = u + _shift_down(u, 1)
    s4 = s2 + _shift_down(s2, 2)
    s8 = s4 + _shift_down(s4, 4)
    s16 = s8 + _shift_down(s8, 8)
    lane = lax.broadcasted_iota(jnp.int32, u.shape, 1)
    rows = lax.broadcasted_iota(jnp.int32, u.shape, 0)
    win = _pool_select(j, lane, 2, 4, 8, 16)
    count = jnp.minimum(rows + 1, win).astype(F32)
    return _pool_select(j, lane, s2, s4, s8, s16) / count - u, count


def _pool_fwd(proj, wbd, scale, *, name, s):
    def body(u_ref, w_ref, sc_ref, y_ref):
        pooled, _ = _pooled(u_ref[...], pl.program_id(0))
        y_ref[...] = jnp.dot(pooled.astype(BF16), w_ref[...].astype(BF16), preferred_element_type=F32) * sc_ref[...]

    return pl.pallas_call(
        body, name=name, grid=(2,),
        in_specs=[_col_spec(s, C_UPOOL[0] // LANES), pl.BlockSpec((None, LANES, LANES), lambda j: (j, 0, 0)),
                  pl.BlockSpec((1, LANES), lambda j: (0, j))],
        out_specs=_col_spec(s, 0), out_shape=jax.ShapeDtypeStruct((s, D_GROUP), F32),
        compiler_params=_params(("parallel",)),
    )(proj, wbd, scale)


def _pool_bwd(dy, proj, wbd, scale, *, name, s):
    def body(dy_ref, u_ref, w_ref, sc_ref, du_ref, dw_ref, dsc_ref):
        j = pl.program_id(0)
        pooled, count = _pooled(u_ref[...], j)
        pooled_b = pooled.astype(BF16)
        w_b = w_ref[...].astype(BF16)
        dyv = dy_ref[...]
        mixed = jnp.dot(pooled_b, w_b, preferred_element_type=F32)
        dsc_ref[...] = jnp.sum(dyv * mixed, axis=0, keepdims=True)
        dms = (dyv * sc_ref[...]).astype(BF16)
        dw_ref[...] = lax.dot_general(pooled_b, dms, (((0,), (0,)), ((), ())), preferred_element_type=F32)
        dpooled = lax.dot_general(dms, w_b, (((1,), (1,)), ((), ())), preferred_element_type=F32)
        r = dpooled / count
        a2 = r + _shift_up(r, 1)
        a4 = a2 + _shift_up(a2, 2)
        a8 = a4 + _shift_up(a4, 4)
        a16 = a8 + _shift_up(a8, 8)
        lane = lax.broadcasted_iota(jnp.int32, r.shape, 1)
        du_ref[...] = (_pool_select(j, lane, a2, a4, a8, a16) - dpooled).astype(du_ref.dtype)

    return pl.pallas_call(
        body, name=name, grid=(2,),
        in_specs=[_col_spec(s, 0), _col_spec(s, C_UPOOL[0] // LANES),
                  pl.BlockSpec((None, LANES, LANES), lambda j: (j, 0, 0)), pl.BlockSpec((1, LANES), lambda j: (0, j))],
        out_specs=[_col_spec(s, 0), pl.BlockSpec((None, LANES, LANES), lambda j: (j, 0, 0)),
                   pl.BlockSpec((1, LANES), lambda j: (0, j))],
        out_shape=[jax.ShapeDtypeStruct((s, D_GROUP), BF16), jax.ShapeDtypeStruct((2, LANES, LANES), F32),
                   jax.ShapeDtypeStruct((1, D_GROUP), F32)],
        compiler_params=_params(("parallel",)),
    )(dy, proj, wbd, scale)


def _mesh_pos():
    return lax.axis_index("x"), lax.axis_index("y"), lax.axis_index("c")


def _any_specs(n):
    return [pl.BlockSpec(memory_space=pl.ANY)] * n


def _all_gather(xs, *, name):
    n = len(xs)

    def body(*refs):
        x_refs, out_refs = refs[:n], refs[n:2 * n]
        send_sems, recv_sems, local_sems = refs[2 * n:]
        x, y, cc = _mesh_pos()
        me, sibling = (x, y, cc), (x, y, 1 - cc)
        chips = [(1 - x, y), (x, 1 - y), (1 - x, 1 - y)]

        def slot(a, px, py, pc):
            return out_refs[a].at[4 * px + 2 * py + pc]

        def copy(a, k, block, to, src=None):
            return pltpu.make_async_remote_copy(
                src_ref=slot(a, *block) if src is None else src, dst_ref=slot(a, *block), send_sem=send_sems.at[a, k],
                recv_sem=recv_sems.at[a, k], device_id=to, device_id_type=pl.DeviceIdType.MESH)

        mine = [pltpu.make_async_copy(x_refs[a], slot(a, *me), local_sems.at[a]) for a in range(n)]
        first = []
        for a in range(n):
            first.append(copy(a, 0, me, sibling, src=x_refs[a]))
            first += [copy(a, 1 + j, me, (*chip, cc), src=x_refs[a]) for j, chip in enumerate(chips)]
        for cp in mine + first:
            cp.start()
        passed = []
        for j, chip in enumerate(chips):
            for a in range(n):
                copy(a, 1 + j, (*chip, cc), me).wait_recv()
                passed.append(copy(a, 4 + j, (*chip, cc), sibling))
                passed[-1].start()
        for a in range(n):
            copy(a, 0, sibling, me).wait_recv()
        for j, chip in enumerate(chips):
            for a in range(n):
                copy(a, 4 + j, (*chip, 1 - cc), me).wait_recv()
        for cp in first + passed:
            cp.wait_send()
        for cp in mine:
            cp.wait()

    return pl.pallas_call(
        body, name=name, out_shape=[jax.ShapeDtypeStruct((N_DEV,) + a.shape, a.dtype) for a in xs],
        in_specs=_any_specs(n), out_specs=_any_specs(n),
        scratch_shapes=[pltpu.SemaphoreType.DMA((n, 7)), pltpu.SemaphoreType.DMA((n, 7)), pltpu.SemaphoreType.DMA((n,))],
    )(*xs)


def _plan_gather(src_refs, land_refs, send_sems, recv_sems):
    x, y, cc = _mesh_pos()
    me = 4 * x + 2 * y + cc
    plan = []
    for a, (src, land) in enumerate(zip(src_refs, land_refs)):
        for k in range(1, N_DEV):
            px, py, pc = (1 - x if k & 4 else x), (1 - y if k & 2 else y), (1 - cc if k & 1 else cc)
            i = a * (N_DEV - 1) + k - 1
            sems = dict(send_sem=send_sems.at[i], recv_sem=recv_sems.at[i], device_id=(px, py, pc),
                        device_id_type=pl.DeviceIdType.MESH)
            plan.append((pltpu.make_async_remote_copy(src_ref=src, dst_ref=land.at[me], **sems),
                         pltpu.make_async_remote_copy(src_ref=src, dst_ref=land.at[4 * px + 2 * py + pc], **sems)))
    return plan


def _plan_gather_near(src_refs, land_refs, send_sems, recv_sems):
    x, y, cc = _mesh_pos()
    me = 4 * x + 2 * y + cc
    plan = []
    for a, (src, land) in enumerate(zip(src_refs, land_refs)):
        for k, (px, py, pc) in enumerate([(x, y, 1 - cc), (1 - x, y, cc), (x, 1 - y, cc), (1 - x, 1 - y, cc)]):
            sems = dict(send_sem=send_sems.at[4 * a + k], recv_sem=recv_sems.at[4 * a + k], device_id=(px, py, pc),
                        device_id_type=pl.DeviceIdType.MESH)
            plan.append((pltpu.make_async_remote_copy(src_ref=src, dst_ref=land.at[me], **sems),
                         pltpu.make_async_remote_copy(src_ref=src, dst_ref=land.at[4 * px + 2 * py + pc], **sems)))
    return plan


def _plan_gather_pass(src_refs, land_refs, send_sems, recv_sems):
    x, y, cc = _mesh_pos()
    plan = []
    for a, land in enumerate(land_refs):
        for j, (px, py) in enumerate([(1 - x, y), (x, 1 - y), (1 - x, 1 - y)]):
            mine, theirs = land.at[4 * px + 2 * py + cc], land.at[4 * px + 2 * py + 1 - cc]
            sems = dict(send_sem=send_sems.at[3 * a + j], recv_sem=recv_sems.at[3 * a + j], device_id=(x, y, 1 - cc),
                        device_id_type=pl.DeviceIdType.MESH)
            plan.append((pltpu.make_async_remote_copy(src_ref=mine, dst_ref=mine, **sems),
                         pltpu.make_async_remote_copy(src_ref=mine, dst_ref=theirs, **sems)))
    return plan


def _plan_sibling(src_refs, land_refs, send_sems, recv_sems):
    x, y, cc = _mesh_pos()
    plan = []
    for a, (src, land) in enumerate(zip(src_refs, land_refs)):
        cp = pltpu.make_async_remote_copy(
            src_ref=src.at[1 - cc], dst_ref=land, send_sem=send_sems.at[a], recv_sem=recv_sems.at[a],
            device_id=(x, y, 1 - cc), device_id_type=pl.DeviceIdType.MESH)
        plan.append((cp, cp))
    return plan


def _plan_chips(src_refs, land_refs, send_sems, recv_sems):
    x, y, cc = _mesh_pos()
    my_chip = 2 * x + y
    plan = []
    for a, (src, land) in enumerate(zip(src_refs, land_refs)):
        for j, (px, py) in enumerate([(1 - x, y), (x, 1 - y), (1 - x, 1 - y)]):
            peer = 2 * px + py
            sems = dict(send_sem=send_sems.at[3 * a + j], recv_sem=recv_sems.at[3 * a + j], device_id=(px, py, cc),
                        device_id_type=pl.DeviceIdType.MESH)
            plan.append((pltpu.make_async_remote_copy(src_ref=src.at[peer], dst_ref=land.at[my_chip], **sems),
                         pltpu.make_async_remote_copy(src_ref=src.at[peer], dst_ref=land.at[peer], **sems)))
    return plan


HBM_SPEC = pl.BlockSpec(memory_space=pltpu.HBM)
SEM_SPEC = pl.BlockSpec(memory_space=pltpu.SEMAPHORE)
ANY_SPEC = pl.BlockSpec(memory_space=pl.ANY)
SIDE_EFFECT = pltpu.CompilerParams(has_side_effects=pltpu.SideEffectType.DATAFLOW_SIDE_EFFECTING)


def _start_copies(plan, sems_per_array, srcs, lands, after, *, name):
    lands = [lax.empty(l, a.dtype) if isinstance(l, tuple) else l for l, a in zip(lands, srcs or lands)]
    ns, n = len(srcs), len(srcs) + len(lands)

    def body(*refs):
        send_sems, recv_sems = refs[n + len(after)], refs[n + len(after) + 1]
        for out, _ in plan(refs[:ns], refs[ns:n], send_sems, recv_sems):
            out.start()
        refs[-1][...] = jnp.zeros_like(refs[-1])

    sem = pltpu.SemaphoreType.DMA((len(lands) * sems_per_array,))
    res = pl.pallas_call(
        body, name=name,
        out_shape=(sem, sem, *[pltpu.HBM(a.shape, a.dtype) for a in srcs + lands], jax.ShapeDtypeStruct((8, LANES), F32)),
        in_specs=[HBM_SPEC] * n + [ANY_SPEC] * len(after),
        out_specs=(SEM_SPEC, SEM_SPEC, *[HBM_SPEC] * n, pl.BlockSpec(memory_space=pltpu.VMEM)),
        input_output_aliases={i: 2 + i for i in range(n)}, compiler_params=SIDE_EFFECT,
    )(*[pltpu.with_memory_space_constraint(a, pltpu.HBM) for a in srcs + lands], *after)
    return (res[0], res[1], list(res[2:2 + ns]), list(res[2 + ns:2 + n])), res[-1]


def _wait_copies(plan, handle, after, *, name):
    send, recv, srcs, lands = handle
    ns, n = len(srcs), len(srcs) + len(lands)

    def body(*refs):
        for out, inc in plan(refs[:ns], refs[ns:n], refs[n], refs[n + 1]):
            out.wait_send()
            inc.wait_recv()

    res = pl.pallas_call(
        body, name=name, out_shape=tuple(pltpu.HBM(a.shape, a.dtype) for a in srcs + lands),
        in_specs=[HBM_SPEC] * n + [SEM_SPEC, SEM_SPEC] + [ANY_SPEC] * len(after), out_specs=[HBM_SPEC] * n,
        input_output_aliases={i: i for i in range(n)}, compiler_params=SIDE_EFFECT,
    )(*srcs, *lands, send, recv, *after)
    return list(res[:ns]), list(res[ns:])


def _row_tile(rows, target=512):
    if rows <= target:
        return rows
    best = None
    for t in range(8, target + 1, 8):
        if rows % t == 0:
            best = t
    assert best is not None, (rows, target)
    return best


def _add_own(g, other, core, *, name):
    _, _, rows, cols = g.shape
    tm = _row_tile(rows)

    def body(c_ref, g_ref, o_ref, out_ref):
        out_ref[...] = g_ref[...] + o_ref[...]

    return pl.pallas_call(
        body, name=name, out_shape=jax.ShapeDtypeStruct(other.shape, other.dtype),
        grid_spec=pltpu.PrefetchScalarGridSpec(
            num_scalar_prefetch=1, grid=(4, rows // tm),
            in_specs=[pl.BlockSpec((None, None, tm, cols), lambda p, i, c_ref: (c_ref[0], p, i, 0)),
                      pl.BlockSpec((None, tm, cols), lambda p, i, c_ref: (p, i, 0))],
            out_specs=pl.BlockSpec((None, tm, cols), lambda p, i, c_ref: (p, i, 0))),
        compiler_params=_params(("parallel", "parallel")),
    )(core, g, other)


def _adamw(parts, w, m, v, after, *, name):
    layers, rows, cols = w.shape
    assert len(parts) == layers
    tm = _row_tile(rows, 256)
    nr = rows // tm

    def body(*refs):
        p_refs = refs[:layers]
        w_ref, m_ref, v_ref, _, g_ref, d_ref, nm_ref, nv_ref, g_s = refs[layers:]
        for ll in range(layers):
            @pl.when(pl.program_id(0) == ll)
            def _(ll=ll):
                g = p_refs[ll][0]
                for q in range(1, p_refs[ll].shape[0]):
                    g = g + p_refs[ll][q]
                g_s[...] = g

        g = g_s[...]
        mm = ADAM_B1 * m_ref[...] + (1.0 - ADAM_B1) * g
        vv = ADAM_B2 * v_ref[...] + (1.0 - ADAM_B2) * jnp.square(g)
        m_hat = mm / (1.0 - ADAM_B1 ** ADAM_STEP)
        v_hat = vv / (1.0 - ADAM_B2 ** ADAM_STEP)
        g_ref[...] = g
        d_ref[...] = -ADAM_LR * (m_hat / (jnp.sqrt(v_hat) + ADAM_EPS) + ADAM_WD * w_ref[...])
        nm_ref[...] = mm
        nv_ref[...] = vv

    def part_spec(ll, p):
        return pl.BlockSpec((p, tm, cols), lambda l, i: (0, jnp.where(l == ll, i, jnp.where(l < ll, 0, nr - 1)), 0))

    spec = pl.BlockSpec((None, tm, cols), lambda l, i: (l, i, 0))
    out = jax.ShapeDtypeStruct(w.shape, F32)
    return pl.pallas_call(
        body, name=name, grid=(layers, nr),
        in_specs=[part_spec(ll, parts[ll].shape[0]) for ll in range(layers)] + [spec] * 3 + [pl.BlockSpec(memory_space=pl.ANY)],
        out_specs=[spec] * 4, out_shape=[out] * 4, scratch_shapes=[pltpu.VMEM((tm, cols), F32)],
        compiler_params=_params(("arbitrary", "arbitrary")),
    )(*parts, w, m, v, after)


def _pack(arrs):
    flat = jnp.concatenate([a.reshape(-1) for a in arrs])
    rows = -(-flat.shape[0] // (PACK_COLS * 16)) * 16
    return jnp.pad(flat, (0, rows * PACK_COLS - flat.shape[0])).reshape(rows, PACK_COLS)


def _unpack(packed, shapes):
    flat = packed.reshape(-1)
    out, off = [], 0
    for shp in shapes:
        n = int(np.prod(shp))
        out.append(flat[off:off + n].reshape(shp))
        off += n
    return out


def _shards_to_full(g, axis):
    if axis == 0:
        return g.reshape(g.shape[0] * g.shape[1], g.shape[2])
    return jnp.transpose(g, (1, 0, 2)).reshape(g.shape[1], g.shape[0] * g.shape[2])


def _full_to_shards(a, axis):
    if axis == 0:
        return jnp.transpose(a.reshape(4, 2, a.shape[0] // N_DEV, a.shape[1]), (1, 0, 2, 3))
    return jnp.transpose(a.reshape(a.shape[0], 4, 2, a.shape[1] // N_DEV), (2, 1, 0, 3))


def _zeros_like_cols(a, n):
    return jnp.zeros(a.shape[:-1] + (n,), a.dtype)


def _pad_heads(a, n):
    z = _zeros_like_cols(a, HEAD)
    return jnp.concatenate([p for h in range(n) for p in (a[..., h * HEAD:(h + 1) * HEAD], z)], axis=-1)


def _unpad_heads(a, n):
    return jnp.concatenate([a[..., h * LANES:h * LANES + HEAD] for h in range(n)], axis=-1)


def _seg(first, width, sign=1):
    return (width, [(first, sign)])


def _zero(width):
    return (width, [])


def _swapped(first):
    half = MLA_ROPE // 2
    return [_seg(first + half, half, -1), _seg(first, half)]


def _padded_heads(first, n):
    return [s for h in range(n) for s in (_seg(first + HEAD * h, HEAD), _zero(HEAD))]


def _layout_w_in():
    kr = 384
    return (_padded_heads(1440, 4) + [_seg(0, 256), _seg(416, 256), _seg(672, 256), _seg(928, 256), _seg(1184, 256)]
            + _padded_heads(1696, 2) + _padded_heads(1824, 2) + [_seg(256, 128)]
            + [_zero(HEAD), _seg(kr, MLA_ROPE), _seg(kr, MLA_ROPE)] + [_zero(HEAD)] + _swapped(kr) + _swapped(kr))


def _layout_w_uq():
    out = []
    for h in range(MLA_HEADS):
        out += [_seg(96 * h, MLA_NOPE), _seg(96 * h + MLA_NOPE, MLA_ROPE)] + _swapped(96 * h + MLA_NOPE)
    return out


def _layout_w_ukv():
    keys = [s for h in range(MLA_HEADS) for s in (_seg(LANES * h, HEAD), _zero(HEAD))]
    values = [s for h in range(MLA_HEADS) for s in (_seg(LANES * h + HEAD, HEAD), _zero(HEAD))]
    return keys + values


def _layout_w_gate_up():
    return [_seg(half + j, GU_TILE) for j in range(0, D_FF, GU_TILE) for half in (0, D_FF)]


LAYOUTS = dict(w_in=_layout_w_in(), w_uq=_layout_w_uq(), w_ukv=_layout_w_ukv(), w_gate_up=_layout_w_gate_up())
OWN_COLS = dict(w_in=1952, w_uq=384, w_ukv=512, w_gate_up=2 * D_FF)


def _plan_extend(layout, shard):
    plan = []
    for width, terms in layout:
        if not terms:
            plan.append((width, []))
            continue
        (first, sign), = terms
        while width:
            g, off = divmod(first, shard)
            w = min(width, shard - off)
            plan.append((w, [(g, off, sign)]))
            first, width = first + w, width - w
    return [plan]


def _plan_fold(layout, own_cols):
    sources = [[] for _ in range(own_cols)]
    e = 0
    for width, terms in layout:
        for first, sign in terms:
            for i in range(width):
                sources[first + i].append((e + i, sign))
        e += width
    shard = own_cols // N_DEV
    plans = {}
    for g in range(N_DEV):
        plan, n = [], g * shard
        while n < (g + 1) * shard:
            w = 1
            while n + w < (g + 1) * shard and [(c + w, sg) for c, sg in sources[n]] == sources[n + w]:
                w += 1
            plan.append((w, [(0, c, sg) for c, sg in sources[n]]))
            n += w
        plans[g] = plan
    return [plans[2 * p + c] for c in range(2) for p in range(4)]


def _assemble(src, plans, out_cols, out_dtype, *, name):
    g, rows, c = src.shape
    tm = _row_tile(rows, 256)

    def body(s_ref, o_ref):
        blocks = [s_ref[i].astype(F32) for i in range(g)]
        for d, plan in enumerate(plans):
            pieces = []
            for width, terms in plan:
                v = None
                for b, first, sign in terms:
                    t = blocks[b][:, first:first + width]
                    t = -t if sign < 0 else t
                    v = t if v is None else v + t
                pieces.append(jnp.zeros((tm, width), F32) if v is None else v)
            o_ref[d] = (pieces[0] if len(pieces) == 1 else jnp.concatenate(pieces, axis=1)).astype(o_ref.dtype)

    return pl.pallas_call(
        body, name=name, grid=(rows // tm,), in_specs=[pl.BlockSpec((g, tm, c), lambda i: (0, i, 0))],
        out_specs=pl.BlockSpec((len(plans), tm, out_cols), lambda i: (0, i, 0)),
        out_shape=jax.ShapeDtypeStruct((len(plans), rows, out_cols), out_dtype), compiler_params=_params(("parallel",)),
    )(src)


def _extend(nm, gathered, *, name):
    layout = LAYOUTS[nm]
    return _assemble(gathered, _plan_extend(layout, OWN_COLS[nm] // N_DEV), sum(w for w, _ in layout), BF16, name=name)[0]


def _fold_to_shards(nm, grad_ext, *, name):
    shards = _assemble(grad_ext[None], _plan_fold(LAYOUTS[nm], OWN_COLS[nm]), OWN_COLS[nm] // N_DEV, F32, name=name)
    return shards.reshape((2, 4) + shards.shape[1:])


def _rope_tables(s):
    inv = 1.0 / (ROPE_THETA ** (jnp.arange(0, MLA_ROPE, 2, dtype=F32) / MLA_ROPE))
    ang = jnp.arange(s, dtype=F32)[:, None] * inv[None, :]
    cos, sin = jnp.cos(ang), jnp.sin(ang)
    c32, s32 = jnp.concatenate([cos, cos], axis=1), jnp.concatenate([sin, sin], axis=1)
    zeros, ones = jnp.zeros((s, HEAD), F32), jnp.ones((s, HEAD), F32)
    tq = jnp.concatenate([ones, c32, s32], axis=1) * (1.0 / math.sqrt(MLA_NOPE + MLA_ROPE))
    return (jnp.tile(tq, (1, MLA_HEADS)), jnp.concatenate([zeros, c32, c32], axis=1),
            jnp.concatenate([zeros, s32, s32], axis=1))


def _gn(y):
    return y * _rstd(y, D_GROUP)


def _mixer_fwd(x, w, tabs, l):
    s = x.shape[0]
    tq, tkc, tks = tabs
    n = lambda t: f"l{l}_{t}"
    h = _rms_fwd(("row", x, D_MODEL, 0), w["attn_norm"], name=n("attn_norm"), rows=s, width=D_MODEL)
    proj = _mm(h, w["w_in"], name=n("in_proj"))
    cqn = _rms_fwd(("row", proj, 256, C_CQ[0] // 256), w["mla_q_norm"], name=n("q_norm"), rows=s, width=256)
    ckvn = _rms_fwd(("row", proj, 128, C_CKV[0] // 128), w["mla_kv_norm"], name=n("kv_norm"), rows=s, width=128)
    q_ext = _mm(cqn, w["w_uq"], name=n("uq"))
    kv_ext = _mm(ckvn, w["w_ukv"], name=n("ukv"))

    def prep(qe, kvk, kvv, kr, krp, tqv, tc, ts):
        kb = kr * tc + krp * ts
        lane = lax.broadcasted_iota(jnp.int32, kvv.shape, 1) & (LANES - 1)
        v = jnp.where(lane == HEAD, 1.0, kvv)
        return qe * tqv, kvk + jnp.tile(kb, (1, MLA_HEADS)), v, v

    qm, km, vm, vmt = _rowwise(
        prep, name=n("mla_prep"), rows=s,
        ins=[("row", q_ext, 512, 0), ("row", kv_ext, 512, 0), ("row", kv_ext, 512, 1),
             ("row", proj, 128, C_KR[0] // 128), ("row", proj, 128, C_KRP[0] // 128),
             ("row", tq, 512, 0), ("row", tkc, 128, 0), ("row", tks, 128, 0)],
        outs=[(512, BF16, MLA_HEADS)] * 3 + [(512, BF16, MLA_HEADS, "T")])
    y_a, lse_row = _mla_fwd(qm, km, vmt, name=n("mla_fwd"), s=s, nh=MLA_HEADS)
    lse_a = lse_row.reshape(MLA_HEADS, s, 1)
    y_b = _conv_fwd(proj, w["conv_w"], name=n("conv_fwd"), s=s)
    y_c = _pool_fwd(proj, w["pool_wbd"], w["pool_scale"], name=n("pool_fwd"), s=s)
    y_d, lse_d = _swa_fwd(proj, w["hp_swa"], name=n("swa_fwd"), s=s, scale=1.0 / math.sqrt(HEAD))

    def mix(ya, yb, yc, yd, mn):
        return (jnp.concatenate([_gn(_unpad_heads(ya, 4)), _gn(yb), _gn(yc), _gn(_unpad_heads(yd, 4))], axis=1) * mn,)

    mixed = _rowwise(mix, name=n("group_norm"), rows=s,
                     ins=[("heads", y_a), ("row", y_b, 256, 0), ("row", y_c, 256, 0), ("row", y_d, 512, 0),
                          ("full", w["mix_norm"])], outs=[(D_MODEL, BF16)])[0]
    x1 = _mm(mixed, w["w_o"], res=x, name=n("out_proj"))
    saved = dict(x=x, h=h, proj=proj, cqn=cqn, ckvn=ckvn, qm=qm, km=km, vm=vm, y_a=y_a, lse_a=lse_a, lse_row=lse_row,
                 y_b=y_b, y_c=y_c, y_d=y_d, lse_d=lse_d, mixed=mixed)
    return x1, saved


def _ffn_fwd(x1, w, l):
    s = x1.shape[0]
    n = lambda t: f"l{l}_{t}"
    h2 = _rms_fwd(("row", x1, D_MODEL, 0), w["ffn_norm"], name=n("ffn_norm"), rows=s, width=D_MODEL)

    def swiglu(gu):
        g, u = gu[:, :GU_TILE], gu[:, GU_TILE:]
        return gu, g * jax.nn.sigmoid(g) * u

    gu, act = _mm(h2, w["w_gate_up"], tn=2 * GU_TILE, name=n("gate_up"),
                  epilogue=(swiglu, [], [(2 * D_FF, BF16), (D_FF, BF16)]))
    x2 = _mm(act, w["w_down"], res=x1, tk=D_FF // 2, name=n("down"))
    return x2, dict(x1=x1, h2=h2, gu=gu, act=act)


def _ffn_bwd_down(dx2, sv, w, l):
    n = lambda t: f"l{l}_{t}"

    def swiglu_bwd(da, gu):
        gt, u = gu[:, :GU_TILE].astype(F32), gu[:, GU_TILE:].astype(F32)
        sg = jax.nn.sigmoid(gt)
        return (jnp.concatenate([da * u * sg * (1.0 + gt * (1.0 - sg)), da * gt * sg], axis=1),)

    dgu = _mm(dx2[1], w["w_down"], tb=True, tm=2048, tn=GU_TILE, name=n("d_act"),
              epilogue=(swiglu_bwd, [sv["gu"]], [(2 * D_FF, BF16)]))[0]
    g = dict(w_down=_mm(sv["act"], dx2[1], ta=True, tm=D_FF // 2, name=n("dw_down")))
    return dgu, g


def _ffn_bwd_up(dx2, dgu, sv, w, l):
    s = dgu.shape[0]
    n = lambda t: f"l{l}_{t}"
    dh2 = _mm(dgu, w["w_gate_up"], tb=True, tk=D_FF // 2, name=n("d_h2"))
    g = dict(w_gate_up=_mm(sv["h2"], dgu, ta=True, tn=D_FF // 2, name=n("dw_gate_up")))
    dx1, dx1_b, g["ffn_norm"] = _rms_bwd(("row", sv["x1"], D_MODEL, 0), w["ffn_norm"], dh2, dx2[0],
                                         name=n("ffn_norm_bwd"), rows=s, width=D_MODEL, out_dtypes=(F32, BF16))
    return (dx1, dx1_b), g


def _mixer_bwd_out(dx1, sv, w, l):
    s = dx1[1].shape[0]
    n = lambda t: f"l{l}_{t}"
    dmixed = _mm(dx1[1], w["w_o"], tb=True, name=n("d_mixed"))
    g = dict(w_o=_mm(sv["mixed"], dx1[1], ta=True, name=n("dw_o")))

    def mix_bwd(dm, ya, yb, yc, yd, mn):
        outs, dmn = [], []
        for i, y in enumerate((_unpad_heads(ya, 4), yb, yc, _unpad_heads(yd, 4))):
            lo, hi = i * D_GROUP, (i + 1) * D_GROUP
            r = _rstd(y, D_GROUP)
            nrm = y * r
            dmg = dm[:, lo:hi]
            dn = dmg * mn[:, lo:hi]
            dy = r * (dn - nrm * (jnp.sum(dn * nrm, axis=-1, keepdims=True) * (1.0 / D_GROUP)))
            outs.append(_pad_heads(dy, 4) if i in (0, 3) else dy)
            dmn.append(jnp.sum(dmg * nrm, axis=0, keepdims=True))
        return (*outs, jnp.concatenate(dmn, axis=1))

    dy_a, dy_b, dy_c, dy_d, g["mix_norm"] = _rowwise(
        mix_bwd, name=n("group_norm_bwd"), rows=s,
        ins=[("row", dmixed, D_MODEL, 0), ("heads", sv["y_a"]), ("row", sv["y_b"], 256, 0),
             ("row", sv["y_c"], 256, 0), ("row", sv["y_d"], 512, 0), ("full", w["mix_norm"])],
        outs=[(512, F32, MLA_HEADS), (256, F32), (256, F32), (512, F32)], reds=[(1, D_MODEL)])
    return (dy_a, dy_b, dy_c, dy_d), g


def _mixer_bwd_in(dx1, dys, sv, w, tabs, l):
    s = dx1[0].shape[0]
    tq, tkc, tks = tabs
    n = lambda t: f"l{l}_{t}"
    dy_a, dy_b, dy_c, dy_d = dys
    g = {}

    proj = sv["proj"]
    dq_sw, dk_sw, dv_sw, dsink = _swa_bwd(proj, sv["y_d"], dy_d, sv["lse_d"], w["hp_swa"], name=n("swa_bwd"), s=s,
                                          scale=1.0 / math.sqrt(HEAD))
    g["swa_sinks"] = dsink[:, 0, 0]

    qkv = (sv["qm"], sv["km"], sv["vm"])
    dqm, dvec_a = _mla_dq(*qkv, sv["y_a"], dy_a, sv["lse_a"], w["hp_swa"], name=n("mla_dq"), s=s, nh=MLA_HEADS)
    dkm, dvm = _mla_dkv(*qkv, dy_a, sv["lse_row"], dvec_a.reshape(MLA_HEADS, 1, s),
                        name=n("mla_dkv"), s=s, nh=MLA_HEADS)

    def prep_bwd(dq, dk, dv, tqv, tc, ts):
        dkb = dk[:, 0:128] + dk[:, 128:256] + dk[:, 256:384] + dk[:, 384:512]
        return dq * tqv, jnp.concatenate([dk, dv.astype(F32)], axis=1), dkb * tc, dkb * ts

    dq_ext, dkv_ext, dkr, dkrp = _rowwise(
        prep_bwd, name=n("mla_prep_bwd"), rows=s,
        ins=[("heads", dqm), ("heads", dkm), ("heads", dvm), ("row", tq, 512, 0), ("row", tkc, 128, 0), ("row", tks, 128, 0)],
        outs=[(512, BF16), (1024, BF16), (128, BF16), (128, BF16)])
    dcqn = _mm(dq_ext, w["w_uq"], tb=True, name=n("d_cqn"))
    g["w_uq"] = _mm(sv["cqn"], dq_ext, ta=True, name=n("dw_uq"))
    dckvn = _mm(dkv_ext, w["w_ukv"], tb=True, name=n("d_ckvn"))
    g["w_ukv"] = _mm(sv["ckvn"], dkv_ext, ta=True, name=n("dw_ukv"))
    dcq, g["mla_q_norm"] = _rms_bwd(("row", proj, 256, C_CQ[0] // 256), w["mla_q_norm"], dcqn, None,
                                    name=n("q_norm_bwd"), rows=s, width=256, out_dtypes=(BF16,))
    dckv, g["mla_kv_norm"] = _rms_bwd(("row", proj, 128, C_CKV[0] // 128), w["mla_kv_norm"], dckvn, None,
                                      name=n("kv_norm_bwd"), rows=s, width=128, out_dtypes=(BF16,))

    dgb, dgc, duc, g["conv_w"] = _conv_bwd(dy_b, proj, w["conv_w"], name=n("conv_bwd"), s=s)
    dup, g["pool_wbd"], g["pool_scale"] = _pool_bwd(dy_c, proj, w["pool_wbd"], w["pool_scale"], name=n("pool_bwd"), s=s)

    dproj = jnp.concatenate([dq_sw, dcq, dgb, dgc, duc, dup, dk_sw.astype(BF16), dv_sw.astype(BF16), dckv, dkr, dkrp],
                            axis=1)
    g["w_in"] = _mm(sv["h"], dproj, ta=True, name=n("dw_in"))
    return dproj, g


def _mixer_bwd_norm(dx1, dproj, sv, w, l):
    n = lambda t: f"l{l}_{t}"
    dh = _mm(dproj, w["w_in"], tb=True, name=n("d_h"))
    dx0, dx0_b, dg = _rms_bwd(("row", sv["x"], D_MODEL, 0), w["attn_norm"], dh, dx1[0], name=n("attn_norm_bwd"),
                              rows=dh.shape[0], width=D_MODEL, out_dtypes=(F32, BF16))
    return (dx0, dx0_b), dict(attn_norm=dg)


def _loss_head(x, target, g, *, s):
    def fn(xv, tv, gv):
        r = _rstd(xv, D_MODEL)
        e = xv * r * gv - tv
        part = jnp.sum(jnp.sum(e * e, axis=1, keepdims=True), axis=0, keepdims=True) * (0.5 / D_MODEL)
        dy = e * (1.0 / D_MODEL)
        dyg = dy * gv
        dx = r * dyg - xv * (r * r * r) * (jnp.sum(dyg * xv, axis=-1, keepdims=True) * (1.0 / D_MODEL))
        return dx, dx, jnp.sum(dy * xv * r, axis=0, keepdims=True), jnp.broadcast_to(part, (1, LANES))

    return _rowwise(fn, name="loss_head", rows=s,
                    ins=[("row", x, D_MODEL, 0), ("row", target, D_MODEL, 0), ("full", g)],
                    outs=[(D_MODEL, F32), (D_MODEL, BF16)], reds=[(1, D_MODEL), (1, LANES)])


def _alibi_slopes(n):
    return np.asarray([2.0 ** (-8.0 * (i + 1) / n) for i in range(n)], dtype=np.float32)


MIXER_WEIGHTS = ("w_in", "w_uq", "w_ukv", "conv_w", "w_o")
FFN_WEIGHTS = ("w_gate_up", "w_down")


def _mixer_weights(full, rep, l):
    pw = rep["pool_w"][l]
    z = jnp.zeros((HEAD, HEAD), F32)
    wbd = jnp.stack([jnp.block([[pw[2 * j], z], [z, pw[2 * j + 1]]]) for j in range(2)])
    return dict(
        attn_norm=rep["attn_norm"][l][None], w_in=full["w_in"], mla_q_norm=rep["mla_q_norm"][l][None],
        w_uq=full["w_uq"], mla_kv_norm=rep["mla_kv_norm"][l][None], w_ukv=full["w_ukv"],
        conv_w=full["conv_w"], pool_wbd=wbd, pool_scale=rep["pool_scale"][l][None],
        mix_norm=rep["mix_norm"][l][None], w_o=full["w_o"],
        hp_swa=jnp.stack([jnp.asarray(_alibi_slopes(SWA_HEADS)), rep["swa_sinks"][l]], axis=1))


def _ffn_weights(full, rep, l):
    return dict(ffn_norm=rep["ffn_norm"][l][None], w_gate_up=full["w_gate_up"], w_down=full["w_down"])


def _small_grads(g):
    rows = ("attn_norm", "mla_q_norm", "mla_kv_norm", "pool_scale", "ffn_norm", "mix_norm")
    out = {nm: g[nm][0] for nm in rows if nm in g}
    if "swa_sinks" in g:
        out["swa_sinks"] = g["swa_sinks"]
    if "pool_wbd" in g:
        e = g["pool_wbd"]
        out["pool_w"] = jnp.stack([e[j // 2][HEAD * (j % 2):HEAD * (j % 2 + 1), HEAD * (j % 2):HEAD * (j % 2 + 1)]
                                   for j in range(4)])
    return out


def kernel(x, attn_norm, w_in, mla_q_norm, w_uq, mla_kv_norm, w_ukv, conv_w, pool_w, pool_scale, swa_sinks, mix_norm, w_o, ffn_norm, w_gate_up, w_down, final_norm, loss_target, m_attn_norm, m_w_in, m_mla_q_norm, m_w_uq, m_mla_kv_norm, m_w_ukv, m_conv_w, m_pool_w, m_pool_scale, m_swa_sinks, m_mix_norm, m_w_o, m_ffn_norm, m_w_gate_up, m_w_down, m_final_norm, v_attn_norm, v_w_in, v_mla_q_norm, v_w_uq, v_mla_kv_norm, v_w_ukv, v_conv_w, v_pool_w, v_pool_scale, v_swa_sinks, v_mix_norm, v_w_o, v_ffn_norm, v_w_gate_up, v_w_down, v_final_norm):
    given = dict(locals())
    sh_names = [nm for nm, _, _ in SHARDED]
    sh_axis = {nm: ax - 1 for nm, _, ax in SHARDED}
    rep_names = [nm for nm, _ in REPLICATED]
    rep_shapes = [shp for _, shp in REPLICATED]
    rep = {nm: given[nm] for nm in rep_names if nm != "loss"}
    me = 4 * lax.axis_index("x") + 2 * lax.axis_index("y") + lax.axis_index("c")
    my_chip = 2 * lax.axis_index("x") + lax.axis_index("y")
    core = lax.axis_index("c").astype(jnp.int32).reshape(1)

    def behind(token, a):
        return a + token[0, 0].astype(a.dtype)

    def wire(nm, l):
        if nm == "conv_w":
            return lax.bitcast_convert_type(given[nm][l], BF16).reshape(3, -1)
        return given[nm][l].astype(BF16)

    def whole(nm, g, tag):
        if nm in LAYOUTS:
            return _extend(nm, g, name=f"extend_{nm}_{tag}")
        if nm == "conv_w":
            g = lax.bitcast_convert_type(g.reshape(N_DEV, 3, -1, 2), F32)
        return _shards_to_full(g, sh_axis[nm])

    def gather_start(names, l, after, tag):
        srcs = [wire(nm, l) for nm in names]
        return _start_copies(_plan_gather, N_DEV - 1, srcs, [(N_DEV,) + a.shape for a in srcs], after, name=f"start_gather_{tag}")

    def with_own(names, srcs, lands, tag):
        return {nm: whole(nm, lax.dynamic_update_index_in_dim(land, src, me, 0), tag)
                for nm, src, land in zip(names, srcs, lands)}

    def gather_wait(names, handle, after, tag):
        return with_own(names, *_wait_copies(_plan_gather, handle, after, name=f"wait_gather_{tag}"), tag)

    layer1 = MIXER_WEIGHTS + FFN_WEIGHTS
    got = _all_gather([wire(nm, 0) for nm in MIXER_WEIGHTS], name="gather_mixer0")
    full_m0 = {nm: whole(nm, g, "mixer0") for nm, g in zip(MIXER_WEIGHTS, got)}
    h_f0, tok = gather_start(FFN_WEIGHTS, 0, [], "ffn0")
    srcs_1 = [wire(nm, 1) for nm in layer1]
    h_l1, tok = _start_copies(_plan_gather_near, 4, srcs_1, [(N_DEV,) + a.shape for a in srcs_1], [tok],
                              name="start_gather_layer1")

    xs, target = x[0], loss_target[0]
    s = xs.shape[0]
    tabs = _rope_tables(s)
    wm, wf, svm, svf = [None] * DEPTH, [None] * DEPTH, [None] * DEPTH, [None] * DEPTH
    wm[0] = _mixer_weights(full_m0, rep, 0)
    wm[0]["attn_norm"] = behind(tok, wm[0]["attn_norm"])
    x1, svm[0] = _mixer_fwd(xs, wm[0], tabs, 0)
    wf[0] = _ffn_weights(gather_wait(FFN_WEIGHTS, h_f0, [x1], "ffn0"), rep, 0)
    srcs_1, lands_1 = _wait_copies(_plan_gather_near, h_l1, [x1], name="wait_gather_layer1")
    h_l1, tok = _start_copies(_plan_gather_pass, 3, [], lands_1, [], name="start_pass_layer1")
    wf[0]["ffn_norm"] = behind(tok, wf[0]["ffn_norm"])
    x2, svf[0] = _ffn_fwd(x1, wf[0], 0)
    _, lands_1 = _wait_copies(_plan_gather_pass, h_l1, [x2], name="wait_pass_layer1")
    full_1 = with_own(layer1, srcs_1, lands_1, "layer1")
    wm[1], wf[1] = _mixer_weights(full_1, rep, 1), _ffn_weights(full_1, rep, 1)
    x1, svm[1] = _mixer_fwd(x2, wm[1], tabs, 1)
    x2, svf[1] = _ffn_fwd(x1, wf[1], 1)
    dx_f, dx_b, d_final, loss = _loss_head(x2, target, rep["final_norm"][None], s=s)
    dx = (dx_f, dx_b)

    parts = {}

    def reduce_start(grads, l, after, tag):
        names = [nm for nm in sh_names if nm in grads]
        mine = [_fold_to_shards(nm, grads[nm], name=f"fold_{nm}_{l}") if nm in LAYOUTS
                else _full_to_shards(grads[nm], sh_axis[nm]) for nm in names]
        handle, token = _start_copies(_plan_sibling, 1, mine, [m.shape[1:] for m in mine], after, name=f"start_sibling_{tag}")
        return (names, l, handle), token

    def reduce_mid(state, after, tag):
        names, l, handle = state
        mine, theirs = _wait_copies(_plan_sibling, handle, after, name=f"wait_sibling_{tag}")
        sums = [_add_own(g, o, core, name=f"chip_sum_{nm}_{l}") for nm, g, o in zip(names, mine, theirs)]
        handle, token = _start_copies(_plan_chips, 3, sums, [a.shape for a in sums], [], name=f"start_chips_{tag}")
        return (names, l, handle), token

    def reduce_end(state, after, tag):
        names, l, handle = state
        sums, lands = _wait_copies(_plan_chips, handle, after, name=f"wait_chips_{tag}")
        for nm, own, land in zip(names, sums, lands):
            parts[nm, l] = lax.dynamic_update_index_in_dim(land, lax.dynamic_index_in_dim(own, my_chip, 0, keepdims=False),
                                                           my_chip, 0)

    small = [None] * DEPTH
    in_flight = []
    pending = None
    for l in reversed(range(DEPTH)):
        dgu, g_down = _ffn_bwd_down(dx, svf[l], wf[l], l)
        if pending is not None:
            state, token = reduce_mid(pending, [dgu], f"mixer{l + 1}")
            in_flight.append((state, f"mixer{l + 1}"))
            wf[l]["ffn_norm"] = behind(token, wf[l]["ffn_norm"])
        dx1, g_up = _ffn_bwd_up(dx, dgu, svf[l], wf[l], l)
        g_ffn = {**g_down, **g_up}
        state, token = reduce_start(g_ffn, l, [], f"ffn{l}")
        wm[l]["mix_norm"] = behind(token, wm[l]["mix_norm"])
        dys, g_out = _mixer_bwd_out(dx1, svm[l], wm[l], l)
        state, token = reduce_mid(state, [dys[0]], f"ffn{l}")
        in_flight.append((state, f"ffn{l}"))
        wm[l]["hp_swa"] = behind(token, wm[l]["hp_swa"])
        dproj, g_in = _mixer_bwd_in(dx1, dys, svm[l], wm[l], tabs, l)
        g_mixer = {**g_out, **g_in}
        pending, token = reduce_start(g_mixer, l, [], f"mixer{l}")
        wm[l]["attn_norm"] = behind(token, wm[l]["attn_norm"])
        dx, g_norm = _mixer_bwd_norm(dx1, dproj, svm[l], wm[l], l)
        small[l] = _small_grads({**g_ffn, **g_mixer, **g_norm})
    last, token = reduce_mid(pending, [dx[0]], "mixer0")
    for state, tag in in_flight:
        reduce_end(state, [], tag)
    grad_x = dx[0]

    def adamw(nm, after):
        return _adamw([parts[nm, l] for l in range(DEPTH)], given[nm], given["m_" + nm], given["v_" + nm], after,
                      name=f"adamw_{nm}")

    sh_out = {nm: adamw(nm, token) for nm in FFN_WEIGHTS}

    grads = {nm: jnp.stack([small[l][nm] for l in range(DEPTH)]) for nm in rep_names if nm in small[0]}
    grads["final_norm"] = d_final[0]
    grads["loss"] = loss[0, :1]
    zero = jnp.zeros((1,), F32)
    small = _all_gather([behind(token, _pack([grads[nm] for nm in rep_names]))], name="gather_small_grads")
    packs = [_pack([given.get(pre + nm, zero) for nm in rep_names])[None] for pre in ("", "m_", "v_")]
    rep_res = _adamw(small, *packs, token, name="adamw_replicated")
    rep_out = [dict(zip(rep_names, _unpack(o[0], rep_shapes))) for o in rep_res]

    reduce_end(last, [rep_res[0], sh_out["w_down"][0]], "mixer0")
    sh_out.update({nm: adamw(nm, token) for nm in MIXER_WEIGHTS})

    out = [rep_out[0]["loss"][0], grad_x[None]]
    for i in range(4):
        out += [sh_out[nm][i] if nm in sh_axis else rep_out[i][nm] for nm in WEIGHT_ORDER]
    return tuple(out)
```

```python
import functools
import math

import numpy as np
import jax
import jax.numpy as jnp
from jax import lax
from jax.experimental import pallas as pl
from jax.experimental.pallas import tpu as pltpu

F32 = jnp.float32
BF16 = jnp.bfloat16

D_MODEL = 1024
DEPTH = 2
D_GROUP = 256
MLA_HEADS = 4
MLA_NOPE = 64
MLA_ROPE = 32
ROPE_THETA = 10000.0
POOL_WINDOWS = (2, 4, 8, 16)
SWA_HEADS = 4
SWA_KV_HEADS = 2
SWA_WINDOW = 128
D_FF = 2816
GU_TILE = 256
RMS_EPS = 1e-6
LANES = 128
HEAD = 64
VMEM_LIMIT = 48 * 1024 * 1024
NEG = -1e30

ADAM_LR = 0.001
ADAM_B1 = 0.9
ADAM_B2 = 0.999
ADAM_EPS = 1e-08
ADAM_WD = 0.01
ADAM_STEP = 10

N_DEV = 8
PACK_COLS = 1024

C_QSW, C_CQ, C_GB, C_GC, C_UCONV, C_UPOOL = (0, 512), (512, 256), (768, 256), (1024, 256), (1280, 256), (1536, 256)
C_KSW, C_VSW, C_CKV, C_KR, C_KRP = (1792, 256), (2048, 256), (2304, 128), (2432, 128), (2560, 128)
D_IN_EXT = 2688

SHARDED = (("w_in", (DEPTH, 1024, 244), 2), ("w_uq", (DEPTH, 256, 48), 2), ("w_ukv", (DEPTH, 128, 64), 2),
           ("conv_w", (DEPTH, 3, 32), 2), ("w_o", (DEPTH, 128, 1024), 1), ("w_gate_up", (DEPTH, 1024, 704), 2),
           ("w_down", (DEPTH, 352, 1024), 1))
REPLICATED = (("attn_norm", (DEPTH, 1024)), ("mla_q_norm", (DEPTH, 256)), ("mla_kv_norm", (DEPTH, 128)),
              ("pool_w", (DEPTH, 4, 64, 64)), ("pool_scale", (DEPTH, 256)), ("swa_sinks", (DEPTH, 4)),
              ("mix_norm", (DEPTH, 1024)), ("ffn_norm", (DEPTH, 1024)), ("final_norm", (1024,)), ("loss", (1,)))
WEIGHT_ORDER = ("attn_norm", "w_in", "mla_q_norm", "w_uq", "mla_kv_norm", "w_ukv", "conv_w", "pool_w", "pool_scale",
                "swa_sinks", "mix_norm", "w_o", "ffn_norm", "w_gate_up", "w_down", "final_norm")


def _params(sem):
    return pltpu.CompilerParams(dimension_semantics=sem, vmem_limit_bytes=VMEM_LIMIT)


def _pick(dim, target):
    if dim <= target:
        return dim
    best = None
    for t in range(LANES, target + 1, LANES):
        if dim % t == 0:
            best = t
    assert best is not None, (dim, target)
    return best


def _mm(a, b, *, name, ta=False, tb=False, res=None, out_dtype=F32, tm=1024, tn=1024, tk=1024, epilogue=None):
    m, k = (a.shape[1], a.shape[0]) if ta else a.shape
    n = b.shape[0] if tb else b.shape[1]
    assert (b.shape[1] if tb else b.shape[0]) == k
    tm, tn, tk = _pick(m, tm), _pick(n, tn), _pick(k, tk)
    nj, nk = n // tn, k // tk
    dims = (((0 if ta else 1,), (1 if tb else 0,)), ((), ()))
    fn, extra, outs = epilogue if epilogue is not None else (None, [], [(n, out_dtype)])
    if res is not None:
        assert epilogue is None
        fn, extra = (lambda acc, r: (acc + r,)), [res]
    n_in, n_out = 2 + len(extra), len(outs)

    def body(*refs):
        a_ref, b_ref, acc_ref = refs[0], refs[1], refs[-1]
        kk = pl.program_id(2)

        @pl.when(kk == 0)
        def _():
            acc_ref[...] = jnp.zeros_like(acc_ref)

        acc_ref[...] += lax.dot_general(a_ref[...].astype(BF16), b_ref[...].astype(BF16), dims,
                                        preferred_element_type=F32)

        @pl.when(kk == nk - 1)
        def _():
            tiles = (acc_ref[...],) if fn is None else fn(acc_ref[...], *[r[...] for r in refs[2:n_in]])
            for o_ref, tile in zip(refs[n_in:n_in + n_out], tiles):
                o_ref[...] = tile.astype(o_ref.dtype)

    def col_tiles(width):
        assert width % (nj * LANES) == 0, (width, nj)
        return pl.BlockSpec((tm, width // nj), lambda i, j, kk: (i, j))

    a_spec = pl.BlockSpec((tk, tm), lambda i, j, kk: (kk, i)) if ta else pl.BlockSpec((tm, tk), lambda i, j, kk: (i, kk))
    b_spec = pl.BlockSpec((tn, tk), lambda i, j, kk: (j, kk)) if tb else pl.BlockSpec((tk, tn), lambda i, j, kk: (kk, j))
    res_ = pl.pallas_call(
        body, name=name, grid=(m // tm, nj, nk), in_specs=[a_spec, b_spec] + [col_tiles(e.shape[1]) for e in extra],
        out_specs=[col_tiles(w) for w, _ in outs],
        out_shape=[jax.ShapeDtypeStruct((m, w), dt) for w, dt in outs],
        scratch_shapes=[pltpu.VMEM((tm, tn), F32)],
        compiler_params=_params(("parallel", "parallel", "arbitrary")),
    )(a, b, *extra)
    return res_[0] if epilogue is None else res_


def _rowwise(fn, *, name, rows, ins, outs, reds=(), tm=512):
    tm = min(tm, rows)
    assert rows % tm == 0
    n_in, n_out = len(ins), len(outs)

    def body(*refs):
        vals = [jnp.concatenate([r[h] for h in range(r.shape[0])], axis=1) if spec[0] == "heads" else r[...]
                for spec, r in zip(ins, refs[:n_in])]
        res = fn(*vals)
        for out, r, v in zip(outs, refs[n_in:n_in + n_out], res[:n_out]):
            if len(out) >= 3:
                for h in range(out[2]):
                    piece = v[:, h * LANES:(h + 1) * LANES]
                    r[h] = (piece.T if len(out) == 4 else piece).astype(r.dtype)
            else:
                r[...] = v.astype(r.dtype)
        if reds:
            @pl.when(pl.program_id(0) == 0)
            def _():
                for r in refs[n_in + n_out:]:
                    r[...] = jnp.zeros_like(r)

            for r, v in zip(refs[n_in + n_out:], res[n_out:]):
                r[...] += v

    in_specs, args = [], []
    for spec in ins:
        if spec[0] == "row":
            _, arr, width, blk = spec
            in_specs.append(pl.BlockSpec((tm, width), functools.partial(lambda i, blk: (i, blk), blk=blk)))
        elif spec[0] == "heads":
            arr = spec[1]
            in_specs.append(pl.BlockSpec((arr.shape[0], tm, LANES), lambda i: (0, i, 0)))
        else:
            arr = spec[1]
            in_specs.append(pl.BlockSpec(arr.shape, functools.partial(lambda i, nd: (0,) * nd, nd=arr.ndim)))
        args.append(arr)
    def out_spec(o):
        if len(o) == 4:
            return pl.BlockSpec((o[2], LANES, tm), lambda i: (0, 0, i)), (o[2], LANES, rows)
        if len(o) == 3:
            return pl.BlockSpec((o[2], tm, LANES), lambda i: (0, i, 0)), (o[2], rows, LANES)
        return pl.BlockSpec((tm, o[0]), lambda i: (i, 0)), (rows, o[0])

    out_specs = [out_spec(o)[0] for o in outs]
    out_shape = [jax.ShapeDtypeStruct(out_spec(o)[1], o[1]) for o in outs]
    out_specs += [pl.BlockSpec((r, w), lambda i: (0, 0)) for r, w in reds]
    out_shape += [jax.ShapeDtypeStruct((r, w), F32) for r, w in reds]
    return pl.pallas_call(body, name=name, grid=(rows // tm,), in_specs=in_specs, out_specs=out_specs,
                          out_shape=out_shape, compiler_params=_params(("arbitrary",)))(*args)


def _rstd(x, n):
    return lax.rsqrt(jnp.sum(x * x, axis=-1, keepdims=True) * (1.0 / n) + RMS_EPS)


def _rms_fwd(x_spec, g, *, name, rows, width):
    def fn(x, gv):
        return (x * _rstd(x, width) * gv,)
    return _rowwise(fn, name=name, rows=rows, ins=[x_spec, ("full", g)], outs=[(width, BF16)])[0]


def _rms_bwd(x_spec, g, dy, res, *, name, rows, width, out_dtypes):
    def fn(x, gv, dyv, *rest):
        r = _rstd(x, width)
        dyg = dyv * gv
        dx = r * dyg - x * (r * r * r) * (jnp.sum(dyg * x, axis=-1, keepdims=True) * (1.0 / width))
        if rest:
            dx = dx + rest[0]
        return (dx,) * len(out_dtypes) + (jnp.sum(dyv * x * r, axis=0, keepdims=True),)

    ins = [x_spec, ("full", g), ("row", dy, width, 0)]
    if res is not None:
        ins.append(("row", res, width, 0))
    return _rowwise(fn, name=name, rows=rows, ins=ins, outs=[(width, dt) for dt in out_dtypes], reds=[(1, width)])


NT_DIMS = (((1,), (1,)), ((), ()))
TN_DIMS = (((0,), (0,)), ((), ()))
BNT_DIMS = (((2,), (2,)), ((0,), (0,)))
BNN_DIMS = (((2,), (1,)), ((0,), (0,)))


def _mla_tile(s):
    return min(512, s)


def _causal(shape, query_axis):
    return lax.broadcasted_iota(jnp.int32, shape, query_axis) >= lax.broadcasted_iota(jnp.int32, shape, 1 - query_axis)


def _mla_fwd(qa, ka, vta, *, name, s, nh):
    t = _mla_tile(s)
    nq = s // t

    def body(q_ref, k_ref, vt_ref, o_ref, lse_ref, m_s, acc_s):
        i, j = pl.program_id(0), pl.program_id(1)

        @pl.when(j == 0)
        def _():
            m_s[...] = jnp.full_like(m_s, NEG)
            acc_s[...] = jnp.zeros_like(acc_s)

        def step(diag):
            sc = lax.dot_general(k_ref[...], q_ref[...], BNT_DIMS, preferred_element_type=F32)
            if diag:
                sc = jnp.where(_causal(sc.shape[1:], 1)[None], sc, NEG)
            m_prev = m_s[...]
            m_new = jnp.maximum(m_prev, jnp.max(sc, axis=1, keepdims=True))
            p = jnp.exp(sc - m_new).astype(BF16)
            acc_s[...] = (jnp.exp(m_prev - m_new) * acc_s[...]
                          + lax.dot_general(vt_ref[...], p, BNN_DIMS, preferred_element_type=F32))
            m_s[...] = m_new

        pl.when(j < i)(functools.partial(step, False))
        pl.when(j == i)(functools.partial(step, True))

        @pl.when(j == nq - 1)
        def _():
            row = lax.broadcasted_iota(jnp.int32, (LANES, t), 0)
            for h in range(nh):
                acc = acc_s[h]
                l = acc[HEAD:HEAD + 1, :]
                o_ref[h] = jnp.where(row < HEAD, acc / l, 0.0).T
                lse_ref[h] = m_s[h] + jnp.log(l)

    q_spec = pl.BlockSpec((nh, t, LANES), lambda i, j: (0, i, 0))
    k_spec = pl.BlockSpec((nh, t, LANES), lambda i, j: (0, jnp.minimum(j, i), 0))
    vt_spec = pl.BlockSpec((nh, LANES, t), lambda i, j: (0, 0, jnp.minimum(j, i)))
    return pl.pallas_call(
        body, name=name, grid=(nq, nq), in_specs=[q_spec, k_spec, vt_spec],
        out_specs=[q_spec, pl.BlockSpec((nh, 1, t), lambda i, j: (0, 0, i))],
        out_shape=[jax.ShapeDtypeStruct((nh, s, LANES), F32), jax.ShapeDtypeStruct((nh, 1, s), F32)],
        scratch_shapes=[pltpu.VMEM((nh, 1, t), F32), pltpu.VMEM((nh, LANES, t), F32)],
        compiler_params=_params(("parallel", "arbitrary")),
    )(qa, ka, vta)


def _mla_dq(qa, ka, va, o, do, lse, after, *, name, s, nh):
    t = _mla_tile(s)
    nq = s // t

    def body(q_ref, k_ref, v_ref, o_ref, do_ref, lse_ref, dq_ref, d_ref, acc_s):
        i, j = pl.program_id(0), pl.program_id(1)

        @pl.when(j == 0)
        def _():
            acc_s[...] = jnp.zeros_like(acc_s)
            d_ref[...] = jnp.sum(do_ref[...] * o_ref[...], axis=2, keepdims=True)

        def step(diag):
            k = k_ref[...]
            sc = lax.dot_general(q_ref[...], k, BNT_DIMS, preferred_element_type=F32)
            if diag:
                sc = jnp.where(_causal(sc.shape[1:], 0)[None], sc, NEG)
            p = jnp.exp(sc - lse_ref[...])
            dp = lax.dot_general(do_ref[...].astype(BF16), v_ref[...], BNT_DIMS, preferred_element_type=F32)
            ds = p * (dp - d_ref[...])
            acc_s[...] += lax.dot_general(ds.astype(BF16), k, BNN_DIMS, preferred_element_type=F32)

        pl.when(j < i)(functools.partial(step, False))
        pl.when(j == i)(functools.partial(step, True))

        @pl.when(j == nq - 1)
        def _():
            dq_ref[...] = acc_s[...]

    q_spec = pl.BlockSpec((nh, t, LANES), lambda i, j: (0, i, 0))
    kv_spec = pl.BlockSpec((nh, t, LANES), lambda i, j: (0, jnp.minimum(j, i), 0))
    vec_spec = pl.BlockSpec((nh, t, 1), lambda i, j: (0, i, 0))

    def body_after(q_ref, k_ref, v_ref, o_ref, do_ref, lse_ref, after_ref, *rest):
        body(q_ref, k_ref, v_ref, o_ref, do_ref, lse_ref, *rest)

    return pl.pallas_call(
        body_after, name=name, grid=(nq, nq),
        in_specs=[q_spec, kv_spec, kv_spec, q_spec, q_spec, vec_spec, pl.BlockSpec(memory_space=pl.ANY)],
        out_specs=[q_spec, vec_spec],
        out_shape=[jax.ShapeDtypeStruct((nh, s, LANES), F32), jax.ShapeDtypeStruct((nh, s, 1), F32)],
        scratch_shapes=[pltpu.VMEM((nh, t, LANES), F32)],
        compiler_params=_params(("parallel", "arbitrary")),
    )(qa, ka, va, o, do, lse, after)


def _mla_dkv(qa, ka, va, do, lse_row, d_row, *, name, s, nh):
    t = _mla_tile(s)
    nq = s // t

    def body(q_ref, k_ref, v_ref, do_ref, lse_ref, d_ref, dk_ref, dv_ref, dk_s, dv_s):
        kb, j = pl.program_id(0), pl.program_id(1)

        @pl.when(j == 0)
        def _():
            dk_s[...] = jnp.zeros_like(dk_s)
            dv_s[...] = jnp.zeros_like(dv_s)

        def step(diag):
            q = q_ref[...]
            do_b = do_ref[...].astype(BF16)
            sc = lax.dot_general(k_ref[...], q, BNT_DIMS, preferred_element_type=F32)
            if diag:
                sc = jnp.where(_causal(sc.shape[1:], 1)[None], sc, NEG)
            p = jnp.exp(sc - lse_ref[...])
            dv_s[...] += lax.dot_general(p.astype(BF16), do_b, BNN_DIMS, preferred_element_type=F32)
            dp = lax.dot_general(v_ref[...], do_b, BNT_DIMS, preferred_element_type=F32)
            ds = p * (dp - d_ref[...])
            dk_s[...] += lax.dot_general(ds.astype(BF16), q, BNN_DIMS, preferred_element_type=F32)

        pl.when(j > kb)(functools.partial(step, False))
        pl.when(j == kb)(functools.partial(step, True))

        @pl.when(j == nq - 1)
        def _():
            dk_ref[...] = dk_s[...]
            dv_ref[...] = dv_s[...].astype(dv_ref.dtype)

    q_spec = pl.BlockSpec((nh, t, LANES), lambda kb, j: (0, jnp.maximum(j, kb), 0))
    kv_spec = pl.BlockSpec((nh, t, LANES), lambda kb, j: (0, kb, 0))
    row_spec = pl.BlockSpec((nh, 1, t), lambda kb, j: (0, 0, jnp.maximum(j, kb)))
    return pl.pallas_call(
        body, name=name, grid=(nq, nq), in_specs=[q_spec, kv_spec, kv_spec, q_spec, row_spec, row_spec],
        out_specs=[kv_spec, kv_spec],
        out_shape=[jax.ShapeDtypeStruct((nh, s, LANES), F32), jax.ShapeDtypeStruct((nh, s, LANES), BF16)],
        scratch_shapes=[pltpu.VMEM((nh, t, LANES), F32), pltpu.VMEM((nh, t, LANES), F32)],
        compiler_params=_params(("parallel", "arbitrary")),
    )(qa, ka, va, do, lse_row, d_row)


SWA_PIECE = 128
SWA_KEYS = 2 * SWA_PIECE


def _swa_block(s):
    return min(512, s)


def _swa_piece(hp_ref, h, q, k_ref, v_ref, qpos0, scale):
    kstart = pl.multiple_of(jnp.maximum(qpos0 - SWA_PIECE, 0), SWA_PIECE)
    k = k_ref[pl.ds(kstart, SWA_KEYS), :].astype(BF16)
    v = v_ref[pl.ds(kstart, SWA_KEYS), :].astype(BF16)
    sc = lax.dot_general(q, k, NT_DIMS, preferred_element_type=F32)
    dist = (qpos0 + lax.broadcasted_iota(jnp.int32, sc.shape, 0)) - (kstart + lax.broadcasted_iota(jnp.int32, sc.shape, 1))
    sc = sc * scale - hp_ref[h, 0] * dist.astype(F32)
    sc = jnp.where((dist >= 0) & (dist < SWA_WINDOW), sc, NEG)
    return kstart, k, v, sc


def _swa_fwd(proj, hp, *, name, s, scale):
    tb = _swa_block(s)
    group = SWA_HEADS // SWA_KV_HEADS
    q_off, k_off, v_off = C_QSW[0] // LANES, C_KSW[0] // LANES, C_VSW[0] // LANES

    def body(hp_ref, q_ref, k_ref, v_ref, o_ref, lse_ref):
        h, i = pl.program_id(0), pl.program_id(1)
        sink = hp_ref[h, 1]
        for r in range(0, tb, SWA_PIECE):
            rows = pl.ds(r, SWA_PIECE)
            _, _, v, sc = _swa_piece(hp_ref, h, q_ref[rows, :].astype(BF16), k_ref, v_ref, i * tb + r, scale)
            m = jnp.maximum(jnp.max(sc, axis=1, keepdims=True), sink)
            p = jnp.exp(sc - m)
            l = jnp.sum(p, axis=1, keepdims=True) + jnp.exp(sink - m)
            o_ref[rows, :] = jnp.dot(p.astype(BF16), v, preferred_element_type=F32) / l
            lse_ref[rows, :] = m + jnp.log(l)

    whole = lambda off: pl.BlockSpec((s, LANES), lambda h, i: (0, off + h // group))
    return pl.pallas_call(
        body, name=name, grid=(SWA_HEADS, s // tb),
        in_specs=[pl.BlockSpec(memory_space=pltpu.SMEM), pl.BlockSpec((tb, LANES), lambda h, i: (i, q_off + h)),
                  whole(k_off), whole(v_off)],
        out_specs=[pl.BlockSpec((tb, LANES), lambda h, i: (i, h)), pl.BlockSpec((None, tb, 1), lambda h, i: (h, i, 0))],
        out_shape=[jax.ShapeDtypeStruct((s, SWA_HEADS * LANES), F32), jax.ShapeDtypeStruct((SWA_HEADS, s, 1), F32)],
        compiler_params=_params(("parallel", "parallel")),
    )(hp, proj, proj, proj)


def _swa_bwd(proj, o, do, lse, hp, *, name, s, scale):
    tb = _swa_block(s)
    nqb = s // tb
    group = SWA_HEADS // SWA_KV_HEADS
    q_off, k_off, v_off = C_QSW[0] // LANES, C_KSW[0] // LANES, C_VSW[0] // LANES

    def body(hp_ref, q_ref, k_ref, v_ref, o_ref, do_ref, lse_ref, dq_ref, dk_ref, dv_ref, dsink_ref):
        kh, g, i = pl.program_id(0), pl.program_id(1), pl.program_id(2)
        h = kh * group + g
        sink = hp_ref[h, 1]

        @pl.when((g == 0) & (i == 0))
        def _():
            dk_ref[...] = jnp.zeros_like(dk_ref)
            dv_ref[...] = jnp.zeros_like(dv_ref)

        @pl.when(i == 0)
        def _():
            dsink_ref[...] = jnp.zeros_like(dsink_ref)

        for r in range(0, tb, SWA_PIECE):
            rows = pl.ds(r, SWA_PIECE)
            q = q_ref[rows, :].astype(BF16)
            dov = do_ref[rows, :]
            do_b = dov.astype(BF16)
            lse_r = lse_ref[rows, :]
            d_r = jnp.sum(dov * o_ref[rows, :], axis=1, keepdims=True)
            kstart, k, v, sc = _swa_piece(hp_ref, h, q, k_ref, v_ref, i * tb + r, scale)
            p = jnp.exp(sc - lse_r)
            dp = lax.dot_general(do_b, v, NT_DIMS, preferred_element_type=F32)
            ds = (p * (dp - d_r)).astype(BF16)
            dq_ref[rows, :] = (jnp.dot(ds, k, preferred_element_type=F32) * scale).astype(dq_ref.dtype)
            win = pl.ds(kstart, SWA_KEYS)
            dk_ref[win, :] += lax.dot_general(ds, q, TN_DIMS, preferred_element_type=F32) * scale
            dv_ref[win, :] += lax.dot_general(p.astype(BF16), do_b, TN_DIMS, preferred_element_type=F32)
            part = jnp.sum(-jnp.exp(sink - lse_r) * d_r, axis=0, keepdims=True)
            dsink_ref[...] += jnp.broadcast_to(part, (1, LANES))

    whole = lambda off: pl.BlockSpec((s, LANES), lambda kh, g, i: (0, off + kh))
    q_map = lambda kh, g, i: (i, kh * group + g)
    return pl.pallas_call(
        body, name=name, grid=(SWA_KV_HEADS, group, nqb),
        in_specs=[pl.BlockSpec(memory_space=pltpu.SMEM),
                  pl.BlockSpec((tb, LANES), lambda kh, g, i: (i, q_off + kh * group + g)), whole(k_off), whole(v_off),
                  pl.BlockSpec((tb, LANES), q_map), pl.BlockSpec((tb, LANES), q_map),
                  pl.BlockSpec((None, tb, 1), lambda kh, g, i: (kh * group + g, i, 0))],
        out_specs=[pl.BlockSpec((tb, LANES), q_map), whole(0), whole(0),
                   pl.BlockSpec((None, 1, LANES), lambda kh, g, i: (kh * group + g, 0, 0))],
        out_shape=[jax.ShapeDtypeStruct((s, SWA_HEADS * LANES), BF16),
                   jax.ShapeDtypeStruct((s, SWA_KV_HEADS * LANES), F32), jax.ShapeDtypeStruct((s, SWA_KV_HEADS * LANES), F32),
                   jax.ShapeDtypeStruct((SWA_HEADS, 1, LANES), F32)],
        compiler_params=_params(("parallel", "arbitrary", "arbitrary")),
    )(hp, proj, proj, proj, o, do, lse)


def _shift_down(z, k):
    rows = lax.broadcasted_iota(jnp.int32, z.shape, 0)
    return jnp.where(rows >= k, pltpu.roll(z, k, 0), 0.0)


def _shift_up(z, k):
    n = z.shape[0]
    rows = lax.broadcasted_iota(jnp.int32, z.shape, 0)
    return jnp.where(rows < n - k, pltpu.roll(z, n - k, 0), 0.0)


def _rows3(a, b, c):
    r = lax.broadcasted_iota(jnp.int32, (3, a.shape[1]), 0)
    return jnp.where(r == 0, a, jnp.where(r == 1, b, c))


def _col_spec(s, off):
    return pl.BlockSpec((s, LANES), functools.partial(lambda j, off: (0, off + j), off=off))


def _conv_fwd(proj, conv_w, *, name, s):
    def body(gb_ref, gc_ref, u_ref, w_ref, y_ref):
        w0, w1, w2 = w_ref[0:1, :], w_ref[1:2, :], w_ref[2:3, :]
        z = gc_ref[...] * u_ref[...]
        c = w2 * z + w1 * _shift_down(z, 1) + w0 * _shift_down(z, 2)
        y_ref[...] = gb_ref[...] * c

    return pl.pallas_call(
        body, name=name, grid=(2,),
        in_specs=[_col_spec(s, C_GB[0] // LANES), _col_spec(s, C_GC[0] // LANES), _col_spec(s, C_UCONV[0] // LANES),
                  pl.BlockSpec((3, LANES), lambda j: (0, j))],
        out_specs=_col_spec(s, 0), out_shape=jax.ShapeDtypeStruct((s, D_GROUP), F32),
        compiler_params=_params(("parallel",)),
    )(proj, proj, proj, conv_w)


def _conv_bwd(dy, proj, conv_w, *, name, s):
    def body(dy_ref, gb_ref, gc_ref, u_ref, w_ref, dgb_ref, dgc_ref, du_ref, dw_ref):
        w0, w1, w2 = w_ref[0:1, :], w_ref[1:2, :], w_ref[2:3, :]
        gc, u, dyv = gc_ref[...], u_ref[...], dy_ref[...]
        z = gc * u
        z1, z2 = _shift_down(z, 1), _shift_down(z, 2)
        c = w2 * z + w1 * z1 + w0 * z2
        dgb_ref[...] = (dyv * c).astype(dgb_ref.dtype)
        dc = dyv * gb_ref[...]
        dz = w2 * dc + w1 * _shift_up(dc, 1) + w0 * _shift_up(dc, 2)
        dgc_ref[...] = (dz * u).astype(dgc_ref.dtype)
        du_ref[...] = (dz * gc).astype(du_ref.dtype)
        dw_ref[...] = _rows3(jnp.sum(dc * z2, axis=0, keepdims=True), jnp.sum(dc * z1, axis=0, keepdims=True),
                             jnp.sum(dc * z, axis=0, keepdims=True))

    act = jax.ShapeDtypeStruct((s, D_GROUP), BF16)
    return pl.pallas_call(
        body, name=name, grid=(2,),
        in_specs=[_col_spec(s, 0), _col_spec(s, C_GB[0] // LANES), _col_spec(s, C_GC[0] // LANES),
                  _col_spec(s, C_UCONV[0] // LANES), pl.BlockSpec((3, LANES), lambda j: (0, j))],
        out_specs=[_col_spec(s, 0), _col_spec(s, 0), _col_spec(s, 0), pl.BlockSpec((3, LANES), lambda j: (0, j))],
        out_shape=[act, act, act, jax.ShapeDtypeStruct((3, D_GROUP), F32)],
        compiler_params=_params(("parallel",)),
    )(dy, proj, proj, proj, conv_w)


def _pool_select(j, lane, a2, a4, a8, a16):
    lo = lane < HEAD
    return jnp.where(j == 0, jnp.where(lo, a2, a4), jnp.where(lo, a8, a16))


def _pooled(u, j):
    s2 = u + _shift_down(u, 1)
    s4 = s2 + _shift_down(s2, 2)
    s8 = s4 + _shift_down(s4, 4)
    s16 = s8 + _shift_down(s8, 8)
    lane = lax.broadcasted_iota(jnp.int32, u.shape, 1)
    rows = lax.broadcasted_iota(jnp.int32, u.shape, 0)
    win = _pool_select(j, lane, 2, 4, 8, 16)
    count = jnp.minimum(rows + 1, win).astype(F32)
    return _pool_select(j, lane, s2, s4, s8, s16) / count - u, count


def _pool_fwd(proj, wbd, scale, *, name, s):
    def body(u_ref, w_ref, sc_ref, y_ref):
        pooled, _ = _pooled(u_ref[...], pl.program_id(0))
        y_ref[...] = jnp.dot(pooled.astype(BF16), w_ref[...].astype(BF16), preferred_element_type=F32) * sc_ref[...]

    return pl.pallas_call(
        body, name=name, grid=(2,),
        in_specs=[_col_spec(s, C_UPOOL[0] // LANES), pl.BlockSpec((None, LANES, LANES), lambda j: (j, 0, 0)),
                  pl.BlockSpec((1, LANES), lambda j: (0, j))],
        out_specs=_col_spec(s, 0), out_shape=jax.ShapeDtypeStruct((s, D_GROUP), F32),
        compiler_params=_params(("parallel",)),
    )(proj, wbd, scale)


def _pool_bwd(dy, proj, wbd, scale, *, name, s):
    def body(dy_ref, u_ref, w_ref, sc_ref, du_ref, dw_ref, dsc_ref):
        j = pl.program_id(0)
        pooled, count = _pooled(u_ref[...], j)
        pooled_b = pooled.astype(BF16)
        w_b = w_ref[...].astype(BF16)
        dyv = dy_ref[...]
        mixed = jnp.dot(pooled_b, w_b, preferred_element_type=F32)
        dsc_ref[...] = jnp.sum(dyv * mixed, axis=0, keepdims=True)
        dms = (dyv * sc_ref[...]).astype(BF16)
        dw_ref[...] = lax.dot_general(pooled_b, dms, (((0,), (0,)), ((), ())), preferred_element_type=F32)
        dpooled = lax.dot_general(dms, w_b, (((1,), (1,)), ((), ())), preferred_element_type=F32)
        r = dpooled / count
        a2 = r + _shift_up(r, 1)
        a4 = a2 + _shift_up(a2, 2)
        a8 = a4 + _shift_up(a4, 4)
        a16 = a8 + _shift_up(a8, 8)
        lane = lax.broadcasted_iota(jnp.int32, r.shape, 1)
        du_ref[...] = (_pool_select(j, lane, a2, a4, a8, a16) - dpooled).astype(du_ref.dtype)

    return pl.pallas_call(
        body, name=name, grid=(2,),
        in_specs=[_col_spec(s, 0), _col_spec(s, C_UPOOL[0] // LANES),
                  pl.BlockSpec((None, LANES, LANES), lambda j: (j, 0, 0)), pl.BlockSpec((1, LANES), lambda j: (0, j))],
        out_specs=[_col_spec(s, 0), pl.BlockSpec((None, LANES, LANES), lambda j: (j, 0, 0)),
                   pl.BlockSpec((1, LANES), lambda j: (0, j))],
        out_shape=[jax.ShapeDtypeStruct((s, D_GROUP), BF16), jax.ShapeDtypeStruct((2, LANES, LANES), F32),
                   jax.ShapeDtypeStruct((1, D_GROUP), F32)],
        compiler_params=_params(("parallel",)),
    )(dy, proj, wbd, scale)


def _mesh_pos():
    return lax.axis_index("x"), lax.axis_index("y"), lax.axis_index("c")


def _any_specs(n):
    return [pl.BlockSpec(memory_space=pl.ANY)] * n


def _all_gather(xs, *, name):
    n = len(xs)

    def body(*refs):
        x_refs, out_refs = refs[:n], refs[n:2 * n]
        send_sems, recv_sems, local_sems = refs[2 * n:]
        x, y, cc = _mesh_pos()
        me, sibling = (x, y, cc), (x, y, 1 - cc)
        chips = [(1 - x, y), (x, 1 - y), (1 - x, 1 - y)]

        def slot(a, px, py, pc):
            return out_refs[a].at[4 * px + 2 * py + pc]

        def copy(a, k, block, to, src=None):
            return pltpu.make_async_remote_copy(
                src_ref=slot(a, *block) if src is None else src, dst_ref=slot(a, *block), send_sem=send_sems.at[a, k],
                recv_sem=recv_sems.at[a, k], device_id=to, device_id_type=pl.DeviceIdType.MESH)

        mine = [pltpu.make_async_copy(x_refs[a], slot(a, *me), local_sems.at[a]) for a in range(n)]
        first = []
        for a in range(n):
            first.append(copy(a, 0, me, sibling, src=x_refs[a]))
            first += [copy(a, 1 + j, me, (*chip, cc), src=x_refs[a]) for j, chip in enumerate(chips)]
        for cp in mine + first:
            cp.start()
        passed = []
        for j, chip in enumerate(chips):
            for a in range(n):
                copy(a, 1 + j, (*chip, cc), me).wait_recv()
                passed.append(copy(a, 4 + j, (*chip, cc), sibling))
                passed[-1].start()
        for a in range(n):
            copy(a, 0, sibling, me).wait_recv()
        for j, chip in enumerate(chips):
            for a in range(n):
                copy(a, 4 + j, (*chip, 1 - cc), me).wait_recv()
        for cp in first + passed:
            cp.wait_send()
        for cp in mine:
            cp.wait()

    return pl.pallas_call(
        body, name=name, out_shape=[jax.ShapeDtypeStruct((N_DEV,) + a.shape, a.dtype) for a in xs],
        in_specs=_any_specs(n), out_specs=_any_specs(n),
        scratch_shapes=[pltpu.SemaphoreType.DMA((n, 7)), pltpu.SemaphoreType.DMA((n, 7)), pltpu.SemaphoreType.DMA((n,))],
    )(*xs)


def _plan_gather(src_refs, land_refs, send_sems, recv_sems):
    x, y, cc = _mesh_pos()
    me = 4 * x + 2 * y + cc
    plan = []
    for a, (src, land) in enumerate(zip(src_refs, land_refs)):
        for k in range(1, N_DEV):
            px, py, pc = (1 - x if k & 4 else x), (1 - y if k & 2 else y), (1 - cc if k & 1 else cc)
            i = a * (N_DEV - 1) + k - 1
            sems = dict(send_sem=send_sems.at[i], recv_sem=recv_sems.at[i], device_id=(px, py, pc),
                        device_id_type=pl.DeviceIdType.MESH)
            plan.append((pltpu.make_async_remote_copy(src_ref=src, dst_ref=land.at[me], **sems),
                         pltpu.make_async_remote_copy(src_ref=src, dst_ref=land.at[4 * px + 2 * py + pc], **sems)))
    return plan


def _plan_gather_near(src_refs, land_refs, send_sems, recv_sems):
    x, y, cc = _mesh_pos()
    me = 4 * x + 2 * y + cc
    plan = []
    for a, (src, land) in enumerate(zip(src_refs, land_refs)):
        for k, (px, py, pc) in enumerate([(x, y, 1 - cc), (1 - x, y, cc), (x, 1 - y, cc), (1 - x, 1 - y, cc)]):
            sems = dict(send_sem=send_sems.at[4 * a + k], recv_sem=recv_sems.at[4 * a + k], device_id=(px, py, pc),
                        device_id_type=pl.DeviceIdType.MESH)
            plan.append((pltpu.make_async_remote_copy(src_ref=src, dst_ref=land.at[me], **sems),
                         pltpu.make_async_remote_copy(src_ref=src, dst_ref=land.at[4 * px + 2 * py + pc], **sems)))
    return plan


def _plan_gather_pass(src_refs, land_refs, send_sems, recv_sems):
    x, y, cc = _mesh_pos()
    plan = []
    for a, land in enumerate(land_refs):
        for j, (px, py) in enumerate([(1 - x, y), (x, 1 - y), (1 - x, 1 - y)]):
            mine, theirs = land.at[4 * px + 2 * py + cc], land.at[4 * px + 2 * py + 1 - cc]
            sems = dict(send_sem=send_sems.at[3 * a + j], recv_sem=recv_sems.at[3 * a + j], device_id=(x, y, 1 - cc),
                        device_id_type=pl.DeviceIdType.MESH)
            plan.append((pltpu.make_async_remote_copy(src_ref=mine, dst_ref=mine, **sems),
                         pltpu.make_async_remote_copy(src_ref=mine, dst_ref=theirs, **sems)))
    return plan


def _plan_sibling(src_refs, land_refs, send_sems, recv_sems):
    x, y, cc = _mesh_pos()
    plan = []
    for a, (src, land) in enumerate(zip(src_refs, land_refs)):
        cp = pltpu.make_async_remote_copy(
            src_ref=src.at[1 - cc], dst_ref=land, send_sem=send_sems.at[a], recv_sem=recv_sems.at[a],
            device_id=(x, y, 1 - cc), device_id_type=pl.DeviceIdType.MESH)
        plan.append((cp, cp))
    return plan


def _plan_chips(src_refs, land_refs, send_sems, recv_sems):
    x, y, cc = _mesh_pos()
    my_chip = 2 * x + y
    plan = []
    for a, (src, land) in enumerate(zip(src_refs, land_refs)):
        for j, (px, py) in enumerate([(1 - x, y), (x, 1 - y), (1 - x, 1 - y)]):
            peer = 2 * px + py
            sems = dict(send_sem=send_sems.at[3 * a + j], recv_sem=recv_sems.at[3 * a + j], device_id=(px, py, cc),
                        device_id_type=pl.DeviceIdType.MESH)
            plan.append((pltpu.make_async_remote_copy(src_ref=src.at[peer], dst_ref=land.at[my_chip], **sems),
                         pltpu.make_async_remote_copy(src_ref=src.at[peer], dst_ref=land.at[peer], **sems)))
    return plan


HBM_SPEC = pl.BlockSpec(memory_space=pltpu.HBM)
SEM_SPEC = pl.BlockSpec(memory_space=pltpu.SEMAPHORE)
ANY_SPEC = pl.BlockSpec(memory_space=pl.ANY)
SIDE_EFFECT = pltpu.CompilerParams(has_side_effects=pltpu.SideEffectType.DATAFLOW_SIDE_EFFECTING)


def _start_copies(plan, sems_per_array, srcs, lands, after, *, name):
    lands = [lax.empty(l, a.dtype) if isinstance(l, tuple) else l for l, a in zip(lands, srcs or lands)]
    ns, n = len(srcs), len(srcs) + len(lands)

    def body(*refs):
        send_sems, recv_sems = refs[n + len(after)], refs[n + len(after) + 1]
        for out, _ in plan(refs[:ns], refs[ns:n], send_sems, recv_sems):
            out.start()
        refs[-1][...] = jnp.zeros_like(refs[-1])

    sem = pltpu.SemaphoreType.DMA((len(lands) * sems_per_array,))
    res = pl.pallas_call(
        body, name=name,
        out_shape=(sem, sem, *[pltpu.HBM(a.shape, a.dtype) for a in srcs + lands], jax.ShapeDtypeStruct((8, LANES), F32)),
        in_specs=[HBM_SPEC] * n + [ANY_SPEC] * len(after),
        out_specs=(SEM_SPEC, SEM_SPEC, *[HBM_SPEC] * n, pl.BlockSpec(memory_space=pltpu.VMEM)),
        input_output_aliases={i: 2 + i for i in range(n)}, compiler_params=SIDE_EFFECT,
    )(*[pltpu.with_memory_space_constraint(a, pltpu.HBM) for a in srcs + lands], *after)
    return (res[0], res[1], list(res[2:2 + ns]), list(res[2 + ns:2 + n])), res[-1]


def _wait_copies(plan, handle, after, *, name):
    send, recv, srcs, lands = handle
    ns, n = len(srcs), len(srcs) + len(lands)

    def body(*refs):
        for out, inc in plan(refs[:ns], refs[ns:n], refs[n], refs[n + 1]):
            out.wait_send()
            inc.wait_recv()

    res = pl.pallas_call(
        body, name=name, out_shape=tuple(pltpu.HBM(a.shape, a.dtype) for a in srcs + lands),
        in_specs=[HBM_SPEC] * n + [SEM_SPEC, SEM_SPEC] + [ANY_SPEC] * len(after), out_specs=[HBM_SPEC] * n,
        input_output_aliases={i: i for i in range(n)}, compiler_params=SIDE_EFFECT,
    )(*srcs, *lands, send, recv, *after)
    return list(res[:ns]), list(res[ns:])


def _row_tile(rows, target=512):
    if rows <= target:
        return rows
    best = None
    for t in range(8, target + 1, 8):
        if rows % t == 0:
            best = t
    assert best is not None, (rows, target)
    return best


def _add_own(g, other, core, *, name):
    _, _, rows, cols = g.shape
    tm = _row_tile(rows)

    def body(c_ref, g_ref, o_ref, out_ref):
        out_ref[...] = g_ref[...] + o_ref[...]

    return pl.pallas_call(
        body, name=name, out_shape=jax.ShapeDtypeStruct(other.shape, other.dtype),
        grid_spec=pltpu.PrefetchScalarGridSpec(
            num_scalar_prefetch=1, grid=(4, rows // tm),
            in_specs=[pl.BlockSpec((None, None, tm, cols), lambda p, i, c_ref: (c_ref[0], p, i, 0)),
                      pl.BlockSpec((None, tm, cols), lambda p, i, c_ref: (p, i, 0))],
            out_specs=pl.BlockSpec((None, tm, cols), lambda p, i, c_ref: (p, i, 0))),
        compiler_params=_params(("parallel", "parallel")),
    )(core, g, other)


def _adamw(parts, w, m, v, after, *, name):
    layers, rows, cols = w.shape
    assert len(parts) == layers
    tm = _row_tile(rows, 256)
    nr = rows // tm

    def body(*refs):
        p_refs = refs[:layers]
        w_ref, m_ref, v_ref, _, g_ref, d_ref, nm_ref, nv_ref, g_s = refs[layers:]
        for ll in range(layers):
            @pl.when(pl.program_id(0) == ll)
            def _(ll=ll):
                g = p_refs[ll][0]
                for q in range(1, p_refs[ll].shape[0]):
                    g = g + p_refs[ll][q]
                g_s[...] = g

        g = g_s[...]
        mm = ADAM_B1 * m_ref[...] + (1.0 - ADAM_B1) * g
        vv = ADAM_B2 * v_ref[...] + (1.0 - ADAM_B2) * jnp.square(g)
        m_hat = mm / (1.0 - ADAM_B1 ** ADAM_STEP)
        v_hat = vv / (1.0 - ADAM_B2 ** ADAM_STEP)
        g_ref[...] = g
        d_ref[...] = -ADAM_LR * (m_hat / (jnp.sqrt(v_hat) + ADAM_EPS) + ADAM_WD * w_ref[...])
        nm_ref[...] = mm
        nv_ref[...] = vv

    def part_spec(ll, p):
        return pl.BlockSpec((p, tm, cols), lambda l, i: (0, jnp.where(l == ll, i, jnp.where(l < ll, 0, nr - 1)), 0))

    spec = pl.BlockSpec((None, tm, cols), lambda l, i: (l, i, 0))
    out = jax.ShapeDtypeStruct(w.shape, F32)
    return pl.pallas_call(
        body, name=name, grid=(layers, nr),
        in_specs=[part_spec(ll, parts[ll].shape[0]) for ll in range(layers)] + [spec] * 3 + [pl.BlockSpec(memory_space=pl.ANY)],
        out_specs=[spec] * 4, out_shape=[out] * 4, scratch_shapes=[pltpu.VMEM((tm, cols), F32)],
        compiler_params=_params(("arbitrary", "arbitrary")),
    )(*parts, w, m, v, after)


def _pack(arrs):
    flat = jnp.concatenate([a.reshape(-1) for a in arrs])
    rows = -(-flat.shape[0] // (PACK_COLS * 16)) * 16
    return jnp.pad(flat, (0, rows * PACK_COLS - flat.shape[0])).reshape(rows, PACK_COLS)


def _unpack(packed, shapes):
    flat = packed.reshape(-1)
    out, off = [], 0
    for shp in shapes:
        n = int(np.prod(shp))
        out.append(flat[off:off + n].reshape(shp))
        off += n
    return out


def _shards_to_full(g, axis):
    if axis == 0:
        return g.reshape(g.shape[0] * g.shape[1], g.shape[2])
    return jnp.transpose(g, (1, 0, 2)).reshape(g.shape[1], g.shape[0] * g.shape[2])


def _full_to_shards(a, axis):
    if axis == 0:
        return jnp.transpose(a.reshape(4, 2, a.shape[0] // N_DEV, a.shape[1]), (1, 0, 2, 3))
    return jnp.transpose(a.reshape(a.shape[0], 4, 2, a.shape[1] // N_DEV), (2, 1, 0, 3))


def _zeros_like_cols(a, n):
    return jnp.zeros(a.shape[:-1] + (n,), a.dtype)


def _pad_heads(a, n):
    z = _zeros_like_cols(a, HEAD)
    return jnp.concatenate([p for h in range(n) for p in (a[..., h * HEAD:(h + 1) * HEAD], z)], axis=-1)


def _unpad_heads(a, n):
    return jnp.concatenate([a[..., h * LANES:h * LANES + HEAD] for h in range(n)], axis=-1)


def _seg(first, width, sign=1):
    return (width, [(first, sign)])


def _zero(width):
    return (width, [])


def _swapped(first):
    half = MLA_ROPE // 2
    return [_seg(first + half, half, -1), _seg(first, half)]


def _padded_heads(first, n):
    return [s for h in range(n) for s in (_seg(first + HEAD * h, HEAD), _zero(HEAD))]


def _layout_w_in():
    kr = 384
    return (_padded_heads(1440, 4) + [_seg(0, 256), _seg(416, 256), _seg(672, 256), _seg(928, 256), _seg(1184, 256)]
            + _padded_heads(1696, 2) + _padded_heads(1824, 2) + [_seg(256, 128)]
            + [_zero(HEAD), _seg(kr, MLA_ROPE), _seg(kr, MLA_ROPE)] + [_zero(HEAD)] + _swapped(kr) + _swapped(kr))


def _layout_w_uq():
    out = []
    for h in range(MLA_HEADS):
        out += [_seg(96 * h, MLA_NOPE), _seg(96 * h + MLA_NOPE, MLA_ROPE)] + _swapped(96 * h + MLA_NOPE)
    return out


def _layout_w_ukv():
    keys = [s for h in range(MLA_HEADS) for s in (_seg(LANES * h, HEAD), _zero(HEAD))]
    values = [s for h in range(MLA_HEADS) for s in (_seg(LANES * h + HEAD, HEAD), _zero(HEAD))]
    return keys + values


def _layout_w_gate_up():
    return [_seg(half + j, GU_TILE) for j in range(0, D_FF, GU_TILE) for half in (0, D_FF)]


LAYOUTS = dict(w_in=_layout_w_in(), w_uq=_layout_w_uq(), w_ukv=_layout_w_ukv(), w_gate_up=_layout_w_gate_up())
OWN_COLS = dict(w_in=1952, w_uq=384, w_ukv=512, w_gate_up=2 * D_FF)


def _plan_extend(layout, shard):
    plan = []
    for width, terms in layout:
        if not terms:
            plan.append((width, []))
            continue
        (first, sign), = terms
        while width:
            g, off = divmod(first, shard)
            w = min(width, shard - off)
            plan.append((w, [(g, off, sign)]))
            first, width = first + w, width - w
    return [plan]


def _plan_fold(layout, own_cols):
    sources = [[] for _ in range(own_cols)]
    e = 0
    for width, terms in layout:
        for first, sign in terms:
            for i in range(width):
                sources[first + i].append((e + i, sign))
        e += width
    shard = own_cols // N_DEV
    plans = {}
    for g in range(N_DEV):
        plan, n = [], g * shard
        while n < (g + 1) * shard:
            w = 1
            while n + w < (g + 1) * shard and [(c + w, sg) for c, sg in sources[n]] == sources[n + w]:
                w += 1
            plan.append((w, [(0, c, sg) for c, sg in sources[n]]))
            n += w
        plans[g] = plan
    return [plans[2 * p + c] for c in range(2) for p in range(4)]


def _assemble(src, plans, out_cols, out_dtype, *, name):
    g, rows, c = src.shape
    tm = _row_tile(rows, 256)

    def body(s_ref, o_ref):
        blocks = [s_ref[i].astype(F32) for i in range(g)]
        for d, plan in enumerate(plans):
            pieces = []
            for width, terms in plan:
                v = None
                for b, first, sign in terms:
                    t = blocks[b][:, first:first + width]
                    t = -t if sign < 0 else t
                    v = t if v is None else v + t
                pieces.append(jnp.zeros((tm, width), F32) if v is None else v)
            o_ref[d] = (pieces[0] if len(pieces) == 1 else jnp.concatenate(pieces, axis=1)).astype(o_ref.dtype)

    return pl.pallas_call(
        body, name=name, grid=(rows // tm,), in_specs=[pl.BlockSpec((g, tm, c), lambda i: (0, i, 0))],
        out_specs=pl.BlockSpec((len(plans), tm, out_cols), lambda i: (0, i, 0)),
        out_shape=jax.ShapeDtypeStruct((len(plans), rows, out_cols), out_dtype), compiler_params=_params(("parallel",)),
    )(src)


def _extend(nm, gathered, *, name):
    layout = LAYOUTS[nm]
    return _assemble(gathered, _plan_extend(layout, OWN_COLS[nm] // N_DEV), sum(w for w, _ in layout), BF16, name=name)[0]


def _fold_to_shards(nm, grad_ext, *, name):
    shards = _assemble(grad_ext[None], _plan_fold(LAYOUTS[nm], OWN_COLS[nm]), OWN_COLS[nm] // N_DEV, F32, name=name)
    return shards.reshape((2, 4) + shards.shape[1:])


def _rope_tables(s):
    inv = 1.0 / (ROPE_THETA ** (jnp.arange(0, MLA_ROPE, 2, dtype=F32) / MLA_ROPE))
    ang = jnp.arange(s, dtype=F32)[:, None] * inv[None, :]
    cos, sin = jnp.cos(ang), jnp.sin(ang)
    c32, s32 = jnp.concatenate([cos, cos], axis=1), jnp.concatenate([sin, sin], axis=1)
    zeros, ones = jnp.zeros((s, HEAD), F32), jnp.ones((s, HEAD), F32)
    tq = jnp.concatenate([ones, c32, s32], axis=1) * (1.0 / math.sqrt(MLA_NOPE + MLA_ROPE))
    return (jnp.tile(tq, (1, MLA_HEADS)), jnp.concatenate([zeros, c32, c32], axis=1),
            jnp.concatenate([zeros, s32, s32], axis=1))


def _gn(y):
    return y * _rstd(y, D_GROUP)


def _mixer_fwd(x, w, tabs, l):
    s = x.shape[0]
    tq, tkc, tks = tabs
    n = lambda t: f"l{l}_{t}"
    h = _rms_fwd(("row", x, D_MODEL, 0), w["attn_norm"], name=n("attn_norm"), rows=s, width=D_MODEL)
    proj = _mm(h, w["w_in"], name=n("in_proj"))
    def prep(cq, ckv, kr, krp, gq, gkv, wuq, wukv, tqv, tc, ts):
        cqn = (cq * _rstd(cq, 256) * gq).astype(BF16)
        ckvn = (ckv * _rstd(ckv, 128) * gkv).astype(BF16)
        qe = jnp.dot(cqn, wuq, preferred_element_type=F32)
        kve = jnp.dot(ckvn, wukv, preferred_element_type=F32)
        kb = kr * tc + krp * ts
        kvv = kve[:, 512:]
        lane = lax.broadcasted_iota(jnp.int32, kvv.shape, 1) & (LANES - 1)
        v = jnp.where(lane == HEAD, 1.0, kvv)
        return cqn, ckvn, qe * tqv, kve[:, :512] + jnp.tile(kb, (1, MLA_HEADS)), v, v

    cqn, ckvn, qm, km, vm, vmt = _rowwise(
        prep, name=n("mla_prep"), rows=s,
        ins=[("row", proj, 256, C_CQ[0] // 256), ("row", proj, 128, C_CKV[0] // 128),
             ("row", proj, 128, C_KR[0] // 128), ("row", proj, 128, C_KRP[0] // 128),
             ("full", w["mla_q_norm"]), ("full", w["mla_kv_norm"]), ("full", w["w_uq"]), ("full", w["w_ukv"]),
             ("row", tq, 512, 0), ("row", tkc, 128, 0), ("row", tks, 128, 0)],
        outs=[(256, BF16), (128, BF16)] + [(512, BF16, MLA_HEADS)] * 3 + [(512, BF16, MLA_HEADS, "T")])
    y_a, lse_row = _mla_fwd(qm, km, vmt, name=n("mla_fwd"), s=s, nh=MLA_HEADS)
    lse_a = lse_row.reshape(MLA_HEADS, s, 1)
    y_b = _conv_fwd(proj, w["conv_w"], name=n("conv_fwd"), s=s)
    y_c = _pool_fwd(proj, w["pool_wbd"], w["pool_scale"], name=n("pool_fwd"), s=s)
    y_d, lse_d = _swa_fwd(proj, w["hp_swa"], name=n("swa_fwd"), s=s, scale=1.0 / math.sqrt(HEAD))

    def mix(ya, yb, yc, yd, mn):
        return (jnp.concatenate([_gn(_unpad_heads(ya, 4)), _gn(yb), _gn(yc), _gn(_unpad_heads(yd, 4))], axis=1) * mn,)

    mixed = _rowwise(mix, name=n("group_norm"), rows=s,
                     ins=[("heads", y_a), ("row", y_b, 256, 0), ("row", y_c, 256, 0), ("row", y_d, 512, 0),
                          ("full", w["mix_norm"])], outs=[(D_MODEL, BF16)])[0]
    x1 = _mm(mixed, w["w_o"], res=x, name=n("out_proj"))
    saved = dict(x=x, h=h, proj=proj, cqn=cqn, ckvn=ckvn, qm=qm, km=km, vm=vm, y_a=y_a, lse_a=lse_a, lse_row=lse_row,
                 y_b=y_b, y_c=y_c, y_d=y_d, lse_d=lse_d, mixed=mixed)
    return x1, saved


def _ffn_fwd(x1, w, l):
    s = x1.shape[0]
    n = lambda t: f"l{l}_{t}"
    h2 = _rms_fwd(("row", x1, D_MODEL, 0), w["ffn_norm"], name=n("ffn_norm"), rows=s, width=D_MODEL)

    def swiglu(gu):
        g, u = gu[:, :GU_TILE], gu[:, GU_TILE:]
        return gu, g * jax.nn.sigmoid(g) * u

    gu, act = _mm(h2, w["w_gate_up"], tm=2048, tn=2 * GU_TILE, name=n("gate_up"),
                  epilogue=(swiglu, [], [(2 * D_FF, BF16), (D_FF, BF16)]))
    x2 = _mm(act, w["w_down"], res=x1, tk=D_FF // 2, name=n("down"))
    return x2, dict(x1=x1, h2=h2, gu=gu, act=act)


def _ffn_bwd_down(dx2, sv, w, l):
    n = lambda t: f"l{l}_{t}"

    def swiglu_bwd(da, gu):
        gt, u = gu[:, :GU_TILE].astype(F32), gu[:, GU_TILE:].astype(F32)
        sg = jax.nn.sigmoid(gt)
        return (jnp.concatenate([da * u * sg * (1.0 + gt * (1.0 - sg)), da * gt * sg], axis=1),)

    dgu = _mm(dx2[1], w["w_down"], tb=True, tm=2048, tn=GU_TILE, name=n("d_act"),
              epilogue=(swiglu_bwd, [sv["gu"]], [(2 * D_FF, BF16)]))[0]
    g = dict(w_down=_mm(sv["act"], dx2[1], ta=True, tm=D_FF // 2, name=n("dw_down")))
    return dgu, g


def _ffn_bwd_up(dx2, dgu, sv, w, l):
    s = dgu.shape[0]
    n = lambda t: f"l{l}_{t}"
    dh2 = _mm(dgu, w["w_gate_up"], tb=True, tk=D_FF // 2, name=n("d_h2"))
    g = dict(w_gate_up=_mm(sv["h2"], dgu, ta=True, tn=D_FF // 2, name=n("dw_gate_up")))
    dx1, dx1_b, g["ffn_norm"] = _rms_bwd(("row", sv["x1"], D_MODEL, 0), w["ffn_norm"], dh2, dx2[0],
                                         name=n("ffn_norm_bwd"), rows=s, width=D_MODEL, out_dtypes=(F32, BF16))
    return (dx1, dx1_b), g


def _mixer_bwd_out(dx1, sv, w, l):
    s = dx1[1].shape[0]
    n = lambda t: f"l{l}_{t}"
    dmixed = _mm(dx1[1], w["w_o"], tb=True, name=n("d_mixed"))
    g = dict(w_o=_mm(sv["mixed"], dx1[1], ta=True, name=n("dw_o")))

    def mix_bwd(dm, ya, yb, yc, yd, mn):
        outs, dmn = [], []
        for i, y in enumerate((_unpad_heads(ya, 4), yb, yc, _unpad_heads(yd, 4))):
            lo, hi = i * D_GROUP, (i + 1) * D_GROUP
            r = _rstd(y, D_GROUP)
            nrm = y * r
            dmg = dm[:, lo:hi]
            dn = dmg * mn[:, lo:hi]
            dy = r * (dn - nrm * (jnp.sum(dn * nrm, axis=-1, keepdims=True) * (1.0 / D_GROUP)))
            outs.append(_pad_heads(dy, 4) if i in (0, 3) else dy)
            dmn.append(jnp.sum(dmg * nrm, axis=0, keepdims=True))
        return (*outs, jnp.concatenate(dmn, axis=1))

    dy_a, dy_b, dy_c, dy_d, g["mix_norm"] = _rowwise(
        mix_bwd, name=n("group_norm_bwd"), rows=s,
        ins=[("row", dmixed, D_MODEL, 0), ("heads", sv["y_a"]), ("row", sv["y_b"], 256, 0),
             ("row", sv["y_c"], 256, 0), ("row", sv["y_d"], 512, 0), ("full", w["mix_norm"])],
        outs=[(512, F32, MLA_HEADS), (256, F32), (256, F32), (512, F32)], reds=[(1, D_MODEL)])
    return (dy_a, dy_b, dy_c, dy_d), g


def _mixer_bwd_in(dx1, dys, sv, w, tabs, l):
    s = dx1[0].shape[0]
    tq, tkc, tks = tabs
    n = lambda t: f"l{l}_{t}"
    dy_a, dy_b, dy_c, dy_d = dys
    g = {}

    proj = sv["proj"]
    dq_sw, dk_sw, dv_sw, dsink = _swa_bwd(proj, sv["y_d"], dy_d, sv["lse_d"], w["hp_swa"], name=n("swa_bwd"), s=s,
                                          scale=1.0 / math.sqrt(HEAD))
    g["swa_sinks"] = dsink[:, 0, 0]

    qkv = (sv["qm"], sv["km"], sv["vm"])
    dqm, dvec_a = _mla_dq(*qkv, sv["y_a"], dy_a, sv["lse_a"], w["hp_swa"], name=n("mla_dq"), s=s, nh=MLA_HEADS)
    dkm, dvm = _mla_dkv(*qkv, dy_a, sv["lse_row"], dvec_a.reshape(MLA_HEADS, 1, s),
                        name=n("mla_dkv"), s=s, nh=MLA_HEADS)

    def rms_bwd(x, gv, dy, width):
        r = _rstd(x, width)
        dyg = dy * gv
        dx = r * dyg - x * (r * r * r) * (jnp.sum(dyg * x, axis=-1, keepdims=True) * (1.0 / width))
        return dx, jnp.sum(dy * x * r, axis=0, keepdims=True)

    def prep_bwd(dq, dk, dv, cq, ckv, gq, gkv, wuq, wukv, tqv, tc, ts):
        dkb = dk[:, 0:128] + dk[:, 128:256] + dk[:, 256:384] + dk[:, 384:512]
        dq_ext = (dq * tqv).astype(BF16)
        dkv_ext = jnp.concatenate([dk.astype(BF16), dv], axis=1)
        dcqn = lax.dot_general(dq_ext, wuq, NT_DIMS, preferred_element_type=F32)
        dckvn = lax.dot_general(dkv_ext, wukv, NT_DIMS, preferred_element_type=F32)
        dcq, dgq = rms_bwd(cq, gq, dcqn, 256)
        dckv, dgkv = rms_bwd(ckv, gkv, dckvn, 128)
        return dq_ext, dkv_ext, dkb * tc, dkb * ts, dcq, dckv, dgq, dgkv

    dq_ext, dkv_ext, dkr, dkrp, dcq, dckv, g["mla_q_norm"], g["mla_kv_norm"] = _rowwise(
        prep_bwd, name=n("mla_prep_bwd"), rows=s,
        ins=[("heads", dqm), ("heads", dkm), ("heads", dvm), ("row", proj, 256, C_CQ[0] // 256),
             ("row", proj, 128, C_CKV[0] // 128), ("full", w["mla_q_norm"]), ("full", w["mla_kv_norm"]),
             ("full", w["w_uq"]), ("full", w["w_ukv"]), ("row", tq, 512, 0), ("row", tkc, 128, 0), ("row", tks, 128, 0)],
        outs=[(512, BF16), (1024, BF16), (128, BF16), (128, BF16), (256, BF16), (128, BF16)],
        reds=[(1, 256), (1, 128)])
    g["w_uq"] = _mm(sv["cqn"], dq_ext, ta=True, name=n("dw_uq"))
    g["w_ukv"] = _mm(sv["ckvn"], dkv_ext, ta=True, name=n("dw_ukv"))

    dgb, dgc, duc, g["conv_w"] = _conv_bwd(dy_b, proj, w["conv_w"], name=n("conv_bwd"), s=s)
    dup, g["pool_wbd"], g["pool_scale"] = _pool_bwd(dy_c, proj, w["pool_wbd"], w["pool_scale"], name=n("pool_bwd"), s=s)

    dproj = jnp.concatenate([dq_sw, dcq, dgb, dgc, duc, dup, dk_sw.astype(BF16), dv_sw.astype(BF16), dckv, dkr, dkrp],
                            axis=1)
    g["w_in"] = _mm(sv["h"], dproj, ta=True, name=n("dw_in"))
    return dproj, g


def _mixer_bwd_norm(dx1, dproj, sv, w, l):
    n = lambda t: f"l{l}_{t}"
    dh = _mm(dproj, w["w_in"], tb=True, name=n("d_h"))
    dx0, dx0_b, dg = _rms_bwd(("row", sv["x"], D_MODEL, 0), w["attn_norm"], dh, dx1[0], name=n("attn_norm_bwd"),
                              rows=dh.shape[0], width=D_MODEL, out_dtypes=(F32, BF16))
    return (dx0, dx0_b), dict(attn_norm=dg)


def _loss_head(x, target, g, *, s):
    def fn(xv, tv, gv):
        r = _rstd(xv, D_MODEL)
        e = xv * r * gv - tv
        part = jnp.sum(jnp.sum(e * e, axis=1, keepdims=True), axis=0, keepdims=True) * (0.5 / D_MODEL)
        dy = e * (1.0 / D_MODEL)
        dyg = dy * gv
        dx = r * dyg - xv * (r * r * r) * (jnp.sum(dyg * xv, axis=-1, keepdims=True) * (1.0 / D_MODEL))
        return dx, dx, jnp.sum(dy * xv * r, axis=0, keepdims=True), jnp.broadcast_to(part, (1, LANES))

    return _rowwise(fn, name="loss_head", rows=s,
                    ins=[("row", x, D_MODEL, 0), ("row", target, D_MODEL, 0), ("full", g)],
                    outs=[(D_MODEL, F32), (D_MODEL, BF16)], reds=[(1, D_MODEL), (1, LANES)])


def _alibi_slopes(n):
    return np.asarray([2.0 ** (-8.0 * (i + 1) / n) for i in range(n)], dtype=np.float32)


MIXER_WEIGHTS = ("w_in", "w_uq", "w_ukv", "conv_w", "w_o")
FFN_WEIGHTS = ("w_gate_up", "w_down")


def _mixer_weights(full, rep, l):
    pw = rep["pool_w"][l]
    z = jnp.zeros((HEAD, HEAD), F32)
    wbd = jnp.stack([jnp.block([[pw[2 * j], z], [z, pw[2 * j + 1]]]) for j in range(2)])
    return dict(
        attn_norm=rep["attn_norm"][l][None], w_in=full["w_in"], mla_q_norm=rep["mla_q_norm"][l][None],
        w_uq=full["w_uq"], mla_kv_norm=rep["mla_kv_norm"][l][None], w_ukv=full["w_ukv"],
        conv_w=full["conv_w"], pool_wbd=wbd, pool_scale=rep["pool_scale"][l][None],
        mix_norm=rep["mix_norm"][l][None], w_o=full["w_o"],
        hp_swa=jnp.stack([jnp.asarray(_alibi_slopes(SWA_HEADS)), rep["swa_sinks"][l]], axis=1))


def _ffn_weights(full, rep, l):
    return dict(ffn_norm=rep["ffn_norm"][l][None], w_gate_up=full["w_gate_up"], w_down=full["w_down"])


def _small_grads(g):
    rows = ("attn_norm", "mla_q_norm", "mla_kv_norm", "pool_scale", "ffn_norm", "mix_norm")
    out = {nm: g[nm][0] for nm in rows if nm in g}
    if "swa_sinks" in g:
        out["swa_sinks"] = g["swa_sinks"]
    if "pool_wbd" in g:
        e = g["pool_wbd"]
        out["pool_w"] = jnp.stack([e[j // 2][HEAD * (j % 2):HEAD * (j % 2 + 1), HEAD * (j % 2):HEAD * (j % 2 + 1)]
                                   for j in range(4)])
    return out


def kernel(x, attn_norm, w_in, mla_q_norm, w_uq, mla_kv_norm, w_ukv, conv_w, pool_w, pool_scale, swa_sinks, mix_norm, w_o, ffn_norm, w_gate_up, w_down, final_norm, loss_target, m_attn_norm, m_w_in, m_mla_q_norm, m_w_uq, m_mla_kv_norm, m_w_ukv, m_conv_w, m_pool_w, m_pool_scale, m_swa_sinks, m_mix_norm, m_w_o, m_ffn_norm, m_w_gate_up, m_w_down, m_final_norm, v_attn_norm, v_w_in, v_mla_q_norm, v_w_uq, v_mla_kv_norm, v_w_ukv, v_conv_w, v_pool_w, v_pool_scale, v_swa_sinks, v_mix_norm, v_w_o, v_ffn_norm, v_w_gate_up, v_w_down, v_final_norm):
    given = dict(locals())
    sh_names = [nm for nm, _, _ in SHARDED]
    sh_axis = {nm: ax - 1 for nm, _, ax in SHARDED}
    rep_names = [nm for nm, _ in REPLICATED]
    rep_shapes = [shp for _, shp in REPLICATED]
    rep = {nm: given[nm] for nm in rep_names if nm != "loss"}
    me = 4 * lax.axis_index("x") + 2 * lax.axis_index("y") + lax.axis_index("c")
    my_chip = 2 * lax.axis_index("x") + lax.axis_index("y")
    core = lax.axis_index("c").astype(jnp.int32).reshape(1)

    def behind(token, a):
        return a + token[0, 0].astype(a.dtype)

    def wire(nm, l):
        if nm == "conv_w":
            return lax.bitcast_convert_type(given[nm][l], BF16).reshape(3, -1)
        return given[nm][l].astype(BF16)

    def whole(nm, g, tag):
        if nm in LAYOUTS:
            return _extend(nm, g, name=f"extend_{nm}_{tag}")
        if nm == "conv_w":
            g = lax.bitcast_convert_type(g.reshape(N_DEV, 3, -1, 2), F32)
        return _shards_to_full(g, sh_axis[nm])

    def gather_start(names, l, after, tag):
        srcs = [wire(nm, l) for nm in names]
        return _start_copies(_plan_gather, N_DEV - 1, srcs, [(N_DEV,) + a.shape for a in srcs], after, name=f"start_gather_{tag}")

    def with_own(names, srcs, lands, tag):
        return {nm: whole(nm, lax.dynamic_update_index_in_dim(land, src, me, 0), tag)
                for nm, src, land in zip(names, srcs, lands)}

    def gather_wait(names, handle, after, tag):
        return with_own(names, *_wait_copies(_plan_gather, handle, after, name=f"wait_gather_{tag}"), tag)

    layer1 = MIXER_WEIGHTS + FFN_WEIGHTS
    got = _all_gather([wire(nm, 0) for nm in MIXER_WEIGHTS], name="gather_mixer0")
    full_m0 = {nm: whole(nm, g, "mixer0") for nm, g in zip(MIXER_WEIGHTS, got)}
    h_f0, tok = gather_start(FFN_WEIGHTS, 0, [], "ffn0")
    srcs_1 = [wire(nm, 1) for nm in layer1]
    h_l1, tok = _start_copies(_plan_gather_near, 4, srcs_1, [(N_DEV,) + a.shape for a in srcs_1], [tok],
                              name="start_gather_layer1")

    xs, target = x[0], loss_target[0]
    s = xs.shape[0]
    tabs = _rope_tables(s)
    wm, wf, svm, svf = [None] * DEPTH, [None] * DEPTH, [None] * DEPTH, [None] * DEPTH
    wm[0] = _mixer_weights(full_m0, rep, 0)
    wm[0]["attn_norm"] = behind(tok, wm[0]["attn_norm"])
    x1, svm[0] = _mixer_fwd(xs, wm[0], tabs, 0)
    wf[0] = _ffn_weights(gather_wait(FFN_WEIGHTS, h_f0, [x1], "ffn0"), rep, 0)
    srcs_1, lands_1 = _wait_copies(_plan_gather_near, h_l1, [x1], name="wait_gather_layer1")
    h_l1, tok = _start_copies(_plan_gather_pass, 3, [], lands_1, [], name="start_pass_layer1")
    wf[0]["ffn_norm"] = behind(tok, wf[0]["ffn_norm"])
    x2, svf[0] = _ffn_fwd(x1, wf[0], 0)
    _, lands_1 = _wait_copies(_plan_gather_pass, h_l1, [x2], name="wait_pass_layer1")
    full_1 = with_own(layer1, srcs_1, lands_1, "layer1")
    wm[1], wf[1] = _mixer_weights(full_1, rep, 1), _ffn_weights(full_1, rep, 1)
    x1, svm[1] = _mixer_fwd(x2, wm[1], tabs, 1)
    x2, svf[1] = _ffn_fwd(x1, wf[1], 1)
    dx_f, dx_b, d_final, loss = _loss_head(x2, target, rep["final_norm"][None], s=s)
    dx = (dx_f, dx_b)

    parts = {}

    def reduce_start(grads, l, after, tag):
        names = [nm for nm in sh_names if nm in grads]
        mine = [_fold_to_shards(nm, grads[nm], name=f"fold_{nm}_{l}") if nm in LAYOUTS
                else _full_to_shards(grads[nm], sh_axis[nm]) for nm in names]
        handle, token = _start_copies(_plan_sibling, 1, mine, [m.shape[1:] for m in mine], after, name=f"start_sibling_{tag}")
        return (names, l, handle), token

    def reduce_mid(state, after, tag):
        names, l, handle = state
        mine, theirs = _wait_copies(_plan_sibling, handle, after, name=f"wait_sibling_{tag}")
        sums = [_add_own(g, o, core, name=f"chip_sum_{nm}_{l}") for nm, g, o in zip(names, mine, theirs)]
        handle, token = _start_copies(_plan_chips, 3, sums, [a.shape for a in sums], [], name=f"start_chips_{tag}")
        return (names, l, handle), token

    def reduce_end(state, after, tag):
        names, l, handle = state
        sums, lands = _wait_copies(_plan_chips, handle, after, name=f"wait_chips_{tag}")
        for nm, own, land in zip(names, sums, lands):
            parts[nm, l] = lax.dynamic_update_index_in_dim(land, lax.dynamic_index_in_dim(own, my_chip, 0, keepdims=False),
                                                           my_chip, 0)

    small = [None] * DEPTH
    in_flight = []
    pending = None
    for l in reversed(range(DEPTH)):
        dgu, g_down = _ffn_bwd_down(dx, svf[l], wf[l], l)
        if pending is not None:
            state, token = reduce_mid(pending, [dgu], f"mixer{l + 1}")
            in_flight.append((state, f"mixer{l + 1}"))
            wf[l]["ffn_norm"] = behind(token, wf[l]["ffn_norm"])
        dx1, g_up = _ffn_bwd_up(dx, dgu, svf[l], wf[l], l)
        g_ffn = {**g_down, **g_up}
        state, token = reduce_start(g_ffn, l, [], f"ffn{l}")
        wm[l]["mix_norm"] = behind(token, wm[l]["mix_norm"])
        dys, g_out = _mixer_bwd_out(dx1, svm[l], wm[l], l)
        state, token = reduce_mid(state, [dys[0]], f"ffn{l}")
        in_flight.append((state, f"ffn{l}"))
        wm[l]["hp_swa"] = behind(token, wm[l]["hp_swa"])
        dproj, g_in = _mixer_bwd_in(dx1, dys, svm[l], wm[l], tabs, l)
        g_mixer = {**g_out, **g_in}
        pending, token = reduce_start(g_mixer, l, [], f"mixer{l}")
        wm[l]["attn_norm"] = behind(token, wm[l]["attn_norm"])
        dx, g_norm = _mixer_bwd_norm(dx1, dproj, svm[l], wm[l], l)
        small[l] = _small_grads({**g_ffn, **g_mixer, **g_norm})
    last, token = reduce_mid(pending, [dx[0]], "mixer0")
    for state, tag in in_flight:
        reduce_end(state, [], tag)
    grad_x = dx[0]

    def adamw(nm, after):
        return _adamw([parts[nm, l] for l in range(DEPTH)], given[nm], given["m_" + nm], given["v_" + nm], after,
                      name=f"adamw_{nm}")

    sh_out = {nm: adamw(nm, token) for nm in FFN_WEIGHTS}

    grads = {nm: jnp.stack([small[l][nm] for l in range(DEPTH)]) for nm in rep_names if nm in small[0]}
    grads["final_norm"] = d_final[0]
    grads["loss"] = loss[0, :1]
    zero = jnp.zeros((1,), F32)
    small = _all_gather([behind(token, _pack([grads[nm] for nm in rep_names]))], name="gather_small_grads")
    packs = [_pack([given.get(pre + nm, zero) for nm in rep_names])[None] for pre in ("", "m_", "v_")]
    rep_res = _adamw(small, *packs, token, name="adamw_replicated")
    rep_out = [dict(zip(rep_names, _unpack(o[0], rep_shapes))) for o in rep_res]

    reduce_end(last, [rep_res[0], sh_out["w_down"][0]], "mixer0")
    sh_out.update({nm: adamw(nm, token) for nm in MIXER_WEIGHTS})

    out = [rep_out[0]["loss"][0], grad_x[None]]
    for i in range(4):
        out += [sh_out[nm][i] if nm in sh_axis else rep_out[i][nm] for nm in WEIGHT_ORDER]
    return tuple(out)
```

```python
import functools
import math

import numpy as np
import jax
import jax.numpy as jnp
from jax import lax
from jax.experimental import pallas as pl
from jax.experimental.pallas import tpu as pltpu

F32 = jnp.float32
BF16 = jnp.bfloat16

D_MODEL = 1024
DEPTH = 2
D_GROUP = 256
MLA_HEADS = 4
MLA_NOPE = 64
MLA_ROPE = 32
ROPE_THETA = 10000.0
POOL_WINDOWS = (2, 4, 8, 16)
SWA_HEADS = 4
SWA_KV_HEADS = 2
SWA_WINDOW = 128
D_FF = 2816
GU_TILE = 256
RMS_EPS = 1e-6
LANES = 128
HEAD = 64
VMEM_LIMIT = 48 * 1024 * 1024
NEG = -1e30

ADAM_LR = 0.001
ADAM_B1 = 0.9
ADAM_B2 = 0.999
ADAM_EPS = 1e-08
ADAM_WD = 0.01
ADAM_STEP = 10

N_DEV = 8
PACK_COLS = 1024

C_QSW, C_CQ, C_GB, C_GC, C_UCONV, C_UPOOL = (0, 512), (512, 256), (768, 256), (1024, 256), (1280, 256), (1536, 256)
C_KSW, C_VSW, C_CKV, C_KR, C_KRP = (1792, 256), (2048, 256), (2304, 128), (2432, 128), (2560, 128)
D_IN_EXT = 2688

SHARDED = (("w_in", (DEPTH, 1024, 244), 2), ("w_uq", (DEPTH, 256, 48), 2), ("w_ukv", (DEPTH, 128, 64), 2),
           ("conv_w", (DEPTH, 3, 32), 2), ("w_o", (DEPTH, 128, 1024), 1), ("w_gate_up", (DEPTH, 1024, 704), 2),
           ("w_down", (DEPTH, 352, 1024), 1))
REPLICATED = (("attn_norm", (DEPTH, 1024)), ("mla_q_norm", (DEPTH, 256)), ("mla_kv_norm", (DEPTH, 128)),
              ("pool_w", (DEPTH, 4, 64, 64)), ("pool_scale", (DEPTH, 256)), ("swa_sinks", (DEPTH, 4)),
              ("mix_norm", (DEPTH, 1024)), ("ffn_norm", (DEPTH, 1024)), ("final_norm", (1024,)), ("loss", (1,)))
WEIGHT_ORDER = ("attn_norm", "w_in", "mla_q_norm", "w_uq", "mla_kv_norm", "w_ukv", "conv_w", "pool_w", "pool_scale",
                "swa_sinks", "mix_norm", "w_o", "ffn_norm", "w_gate_up", "w_down", "final_norm")


def _params(sem):
    return pltpu.CompilerParams(dimension_semantics=sem, vmem_limit_bytes=VMEM_LIMIT)


def _pick(dim, target):
    if dim <= target:
        return dim
    best = None
    for t in range(LANES, target + 1, LANES):
        if dim % t == 0:
            best = t
    assert best is not None, (dim, target)
    return best


def _mm(a, b, *, name, ta=False, tb=False, res=None, out_dtype=F32, tm=1024, tn=1024, tk=1024, epilogue=None):
    m, k = (a.shape[1], a.shape[0]) if ta else a.shape
    n = b.shape[0] if tb else b.shape[1]
    assert (b.shape[1] if tb else b.shape[0]) == k
    tm, tn, tk = _pick(m, tm), _pick(n, tn), _pick(k, tk)
    nj, nk = n // tn, k // tk
    dims = (((0 if ta else 1,), (1 if tb else 0,)), ((), ()))
    fn, extra, outs = epilogue if epilogue is not None else (None, [], [(n, out_dtype)])
    if res is not None:
        assert epilogue is None
        fn, extra = (lambda acc, r: (acc + r,)), [res]
    n_in, n_out = 2 + len(extra), len(outs)

    def body(*refs):
        a_ref, b_ref, acc_ref = refs[0], refs[1], refs[-1]
        kk = pl.program_id(2)

        @pl.when(kk == 0)
        def _():
            acc_ref[...] = jnp.zeros_like(acc_ref)

        acc_ref[...] += lax.dot_general(a_ref[...].astype(BF16), b_ref[...].astype(BF16), dims,
                                        preferred_element_type=F32)

        @pl.when(kk == nk - 1)
        def _():
            tiles = (acc_ref[...],) if fn is None else fn(acc_ref[...], *[r[...] for r in refs[2:n_in]])
            for o_ref, tile in zip(refs[n_in:n_in + n_out], tiles):
                o_ref[...] = tile.astype(o_ref.dtype)

    def col_tiles(width):
        assert width % (nj * LANES) == 0, (width, nj)
        return pl.BlockSpec((tm, width // nj), lambda i, j, kk: (i, j))

    a_spec = pl.BlockSpec((tk, tm), lambda i, j, kk: (kk, i)) if ta else pl.BlockSpec((tm, tk), lambda i, j, kk: (i, kk))
    b_spec = pl.BlockSpec((tn, tk), lambda i, j, kk: (j, kk)) if tb else pl.BlockSpec((tk, tn), lambda i, j, kk: (kk, j))
    res_ = pl.pallas_call(
        body, name=name, grid=(m // tm, nj, nk), in_specs=[a_spec, b_spec] + [col_tiles(e.shape[1]) for e in extra],
        out_specs=[col_tiles(w) for w, _ in outs],
        out_shape=[jax.ShapeDtypeStruct((m, w), dt) for w, dt in outs],
        scratch_shapes=[pltpu.VMEM((tm, tn), F32)],
        compiler_params=_params(("parallel", "parallel", "arbitrary")),
    )(a, b, *extra)
    return res_[0] if epilogue is None else res_


def _rowwise(fn, *, name, rows, ins, outs, reds=(), tm=512):
    tm = min(tm, rows)
    assert rows % tm == 0
    n_in, n_out = len(ins), len(outs)

    def body(*refs):
        vals = [jnp.concatenate([r[h] for h in range(r.shape[0])], axis=1) if spec[0] == "heads" else r[...]
                for spec, r in zip(ins, refs[:n_in])]
        res = fn(*vals)
        for out, r, v in zip(outs, refs[n_in:n_in + n_out], res[:n_out]):
            if len(out) >= 3:
                for h in range(out[2]):
                    piece = v[:, h * LANES:(h + 1) * LANES]
                    r[h] = (piece.T if len(out) == 4 else piece).astype(r.dtype)
            else:
                r[...] = v.astype(r.dtype)
        if reds:
            @pl.when(pl.program_id(0) == 0)
            def _():
                for r in refs[n_in + n_out:]:
                    r[...] = jnp.zeros_like(r)

            for r, v in zip(refs[n_in + n_out:], res[n_out:]):
                r[...] += v

    in_specs, args = [], []
    for spec in ins:
        if spec[0] == "row":
            _, arr, width, blk = spec
            in_specs.append(pl.BlockSpec((tm, width), functools.partial(lambda i, blk: (i, blk), blk=blk)))
        elif spec[0] == "heads":
            arr = spec[1]
            in_specs.append(pl.BlockSpec((arr.shape[0], tm, LANES), lambda i: (0, i, 0)))
        else:
            arr = spec[1]
            in_specs.append(pl.BlockSpec(arr.shape, functools.partial(lambda i, nd: (0,) * nd, nd=arr.ndim)))
        args.append(arr)
    def out_spec(o):
        if len(o) == 4:
            return pl.BlockSpec((o[2], LANES, tm), lambda i: (0, 0, i)), (o[2], LANES, rows)
        if len(o) == 3:
            return pl.BlockSpec((o[2], tm, LANES), lambda i: (0, i, 0)), (o[2], rows, LANES)
        return pl.BlockSpec((tm, o[0]), lambda i: (i, 0)), (rows, o[0])

    out_specs = [out_spec(o)[0] for o in outs]
    out_shape = [jax.ShapeDtypeStruct(out_spec(o)[1], o[1]) for o in outs]
    out_specs += [pl.BlockSpec((r, w), lambda i: (0, 0)) for r, w in reds]
    out_shape += [jax.ShapeDtypeStruct((r, w), F32) for r, w in reds]
    return pl.pallas_call(body, name=name, grid=(rows // tm,), in_specs=in_specs, out_specs=out_specs,
                          out_shape=out_shape, compiler_params=_params(("arbitrary",)))(*args)


def _rstd(x, n):
    return lax.rsqrt(jnp.sum(x * x, axis=-1, keepdims=True) * (1.0 / n) + RMS_EPS)


def _rms_fwd(x_spec, g, *, name, rows, width):
    def fn(x, gv):
        return (x * _rstd(x, width) * gv,)
    return _rowwise(fn, name=name, rows=rows, ins=[x_spec, ("full", g)], outs=[(width, BF16)])[0]


def _rms_bwd(x_spec, g, dy, res, *, name, rows, width, out_dtypes):
    def fn(x, gv, dyv, *rest):
        r = _rstd(x, width)
        dyg = dyv * gv
        dx = r * dyg - x * (r * r * r) * (jnp.sum(dyg * x, axis=-1, keepdims=True) * (1.0 / width))
        if rest:
            dx = dx + rest[0]
        return (dx,) * len(out_dtypes) + (jnp.sum(dyv * x * r, axis=0, keepdims=True),)

    ins = [x_spec, ("full", g), ("row", dy, width, 0)]
    if res is not None:
        ins.append(("row", res, width, 0))
    return _rowwise(fn, name=name, rows=rows, ins=ins, outs=[(width, dt) for dt in out_dtypes], reds=[(1, width)])


NT_DIMS = (((1,), (1,)), ((), ()))
TN_DIMS = (((0,), (0,)), ((), ()))
BNT_DIMS = (((2,), (2,)), ((0,), (0,)))
BNN_DIMS = (((2,), (1,)), ((0,), (0,)))


def _mla_tile(s):
    return min(512, s)


def _causal(shape, query_axis):
    return lax.broadcasted_iota(jnp.int32, shape, query_axis) >= lax.broadcasted_iota(jnp.int32, shape, 1 - query_axis)


def _mla_fwd(qa, ka, vta, *, name, s, nh):
    t = _mla_tile(s)
    nq = s // t

    def body(q_ref, k_ref, vt_ref, o_ref, lse_ref, m_s, acc_s):
        i, j = pl.program_id(0), pl.program_id(1)

        @pl.when(j == 0)
        def _():
            m_s[...] = jnp.full_like(m_s, NEG)
            acc_s[...] = jnp.zeros_like(acc_s)

        def step(diag):
            sc = lax.dot_general(k_ref[...], q_ref[...], BNT_DIMS, preferred_element_type=F32)
            if diag:
                sc = jnp.where(_causal(sc.shape[1:], 1)[None], sc, NEG)
            m_prev = m_s[...]
            m_new = jnp.maximum(m_prev, jnp.max(sc, axis=1, keepdims=True))
            p = jnp.exp(sc - m_new).astype(BF16)
            acc_s[...] = (jnp.exp(m_prev - m_new) * acc_s[...]
                          + lax.dot_general(vt_ref[...], p, BNN_DIMS, preferred_element_type=F32))
            m_s[...] = m_new

        pl.when(j < i)(functools.partial(step, False))
        pl.when(j == i)(functools.partial(step, True))

        @pl.when(j == nq - 1)
        def _():
            row = lax.broadcasted_iota(jnp.int32, (LANES, t), 0)
            for h in range(nh):
                acc = acc_s[h]
                l = acc[HEAD:HEAD + 1, :]
                o_ref[h] = jnp.where(row < HEAD, acc / l, 0.0).T
                lse_ref[h] = m_s[h] + jnp.log(l)

    q_spec = pl.BlockSpec((nh, t, LANES), lambda i, j: (0, i, 0))
    k_spec = pl.BlockSpec((nh, t, LANES), lambda i, j: (0, jnp.minimum(j, i), 0))
    vt_spec = pl.BlockSpec((nh, LANES, t), lambda i, j: (0, 0, jnp.minimum(j, i)))
    return pl.pallas_call(
        body, name=name, grid=(nq, nq), in_specs=[q_spec, k_spec, vt_spec],
        out_specs=[q_spec, pl.BlockSpec((nh, 1, t), lambda i, j: (0, 0, i))],
        out_shape=[jax.ShapeDtypeStruct((nh, s, LANES), F32), jax.ShapeDtypeStruct((nh, 1, s), F32)],
        scratch_shapes=[pltpu.VMEM((nh, 1, t), F32), pltpu.VMEM((nh, LANES, t), F32)],
        compiler_params=_params(("parallel", "arbitrary")),
    )(qa, ka, vta)


def _mla_bwd(qa, ka, kta, va, o, do, lse_row, after, *, name, s, nh):
    t = _mla_tile(s)
    nq = s // t

    def body(q_ref, k_ref, kt_ref, v_ref, o_ref, do_ref, lse_ref, after_ref, dq_hbm, dk_ref, dv_ref,
             dqt_s, dk_s, dv_s, d_s, stage, sem):
        kb, j = pl.program_id(0), pl.program_id(1)
        cols = pl.ds(pl.multiple_of(j * t, t), t)

        @pl.when(j == 0)
        def _():
            dk_s[...] = jnp.zeros_like(dk_s)
            dv_s[...] = jnp.zeros_like(dv_s)

        @pl.when(kb == 0)
        def _():
            dqt_s[:, :, cols] = jnp.zeros((nh, LANES, t), F32)
            for h in range(nh):
                d_col = jnp.sum(do_ref[h] * o_ref[h], axis=1, keepdims=True)
                d_s[h, :, cols] = jnp.broadcast_to(d_col, (t, LANES)).T[0:1, :]

        def step(diag):
            q = q_ref[...]
            do_b = do_ref[...].astype(BF16)
            sc = lax.dot_general(k_ref[...], q, BNT_DIMS, preferred_element_type=F32)
            if diag:
                sc = jnp.where(_causal(sc.shape[1:], 1)[None], sc, NEG)
            p = jnp.exp(sc - lse_ref[...])
            dv_s[...] += lax.dot_general(p.astype(BF16), do_b, BNN_DIMS, preferred_element_type=F32)
            dp = lax.dot_general(v_ref[...], do_b, BNT_DIMS, preferred_element_type=F32)
            ds = (p * (dp - d_s[:, :, cols])).astype(BF16)
            dk_s[...] += lax.dot_general(ds, q, BNN_DIMS, preferred_element_type=F32)
            dqt_s[:, :, cols] += lax.dot_general(kt_ref[...], ds, BNN_DIMS, preferred_element_type=F32)

        pl.when(j > kb)(functools.partial(step, False))
        pl.when(j == kb)(functools.partial(step, True))

        @pl.when(j == kb)
        def _():
            for h in range(nh):
                stage[h] = dqt_s[h, :, cols].T
            out = pltpu.make_async_copy(stage, dq_hbm.at[:, cols, :], sem)
            out.start()
            out.wait()

        @pl.when(j == nq - 1)
        def _():
            dk_ref[...] = dk_s[...]
            dv_ref[...] = dv_s[...].astype(dv_ref.dtype)

    q_spec = pl.BlockSpec((nh, t, LANES), lambda kb, j: (0, jnp.maximum(j, kb), 0))
    kv_spec = pl.BlockSpec((nh, t, LANES), lambda kb, j: (0, kb, 0))
    kt_spec = pl.BlockSpec((nh, LANES, t), lambda kb, j: (0, 0, kb))
    row_spec = pl.BlockSpec((nh, 1, t), lambda kb, j: (0, 0, jnp.maximum(j, kb)))
    whole = jax.ShapeDtypeStruct((nh, s, LANES), F32)
    return pl.pallas_call(
        body, name=name, grid=(nq, nq),
        in_specs=[q_spec, kv_spec, kt_spec, kv_spec, q_spec, q_spec, row_spec, pl.BlockSpec(memory_space=pl.ANY)],
        out_specs=[pl.BlockSpec(memory_space=pl.ANY), kv_spec, kv_spec],
        out_shape=[whole, whole, jax.ShapeDtypeStruct((nh, s, LANES), BF16)],
        scratch_shapes=[pltpu.VMEM((nh, LANES, s), F32), pltpu.VMEM((nh, t, LANES), F32), pltpu.VMEM((nh, t, LANES), F32),
                        pltpu.VMEM((nh, 1, s), F32), pltpu.VMEM((nh, t, LANES), F32), pltpu.SemaphoreType.DMA],
        compiler_params=_params(("arbitrary", "arbitrary")),
    )(qa, ka, kta, va, o, do, lse_row, after)


SWA_PIECE = 128
SWA_KEYS = 2 * SWA_PIECE


def _swa_block(s):
    return min(512, s)


def _swa_piece(hp_ref, h, q, k_ref, v_ref, qpos0, scale):
    kstart = pl.multiple_of(jnp.maximum(qpos0 - SWA_PIECE, 0), SWA_PIECE)
    k = k_ref[pl.ds(kstart, SWA_KEYS), :].astype(BF16)
    v = v_ref[pl.ds(kstart, SWA_KEYS), :].astype(BF16)
    sc = lax.dot_general(q, k, NT_DIMS, preferred_element_type=F32)
    dist = (qpos0 + lax.broadcasted_iota(jnp.int32, sc.shape, 0)) - (kstart + lax.broadcasted_iota(jnp.int32, sc.shape, 1))
    sc = sc * scale - hp_ref[h, 0] * dist.astype(F32)
    sc = jnp.where((dist >= 0) & (dist < SWA_WINDOW), sc, NEG)
    return kstart, k, v, sc


def _swa_fwd(proj, hp, *, name, s, scale):
    tb = _swa_block(s)
    group = SWA_HEADS // SWA_KV_HEADS
    q_off, k_off, v_off = C_QSW[0] // LANES, C_KSW[0] // LANES, C_VSW[0] // LANES

    def body(hp_ref, q_ref, k_ref, v_ref, o_ref, lse_ref):
        h, i = pl.program_id(0), pl.program_id(1)
        sink = hp_ref[h, 1]
        for r in range(0, tb, SWA_PIECE):
            rows = pl.ds(r, SWA_PIECE)
            _, _, v, sc = _swa_piece(hp_ref, h, q_ref[rows, :].astype(BF16), k_ref, v_ref, i * tb + r, scale)
            m = jnp.maximum(jnp.max(sc, axis=1, keepdims=True), sink)
            p = jnp.exp(sc - m)
            l = jnp.sum(p, axis=1, keepdims=True) + jnp.exp(sink - m)
            o_ref[rows, :] = jnp.dot(p.astype(BF16), v, preferred_element_type=F32) / l
            lse_ref[rows, :] = m + jnp.log(l)

    whole = lambda off: pl.BlockSpec((s, LANES), lambda h, i: (0, off + h // group))
    return pl.pallas_call(
        body, name=name, grid=(SWA_HEADS, s // tb),
        in_specs=[pl.BlockSpec(memory_space=pltpu.SMEM), pl.BlockSpec((tb, LANES), lambda h, i: (i, q_off + h)),
                  whole(k_off), whole(v_off)],
        out_specs=[pl.BlockSpec((tb, LANES), lambda h, i: (i, h)), pl.BlockSpec((None, tb, 1), lambda h, i: (h, i, 0))],
        out_shape=[jax.ShapeDtypeStruct((s, SWA_HEADS * LANES), F32), jax.ShapeDtypeStruct((SWA_HEADS, s, 1), F32)],
        compiler_params=_params(("parallel", "parallel")),
    )(hp, proj, proj, proj)


def _swa_bwd(proj, o, do, lse, hp, *, name, s, scale):
    tb = _swa_block(s)
    nqb = s // tb
    group = SWA_HEADS // SWA_KV_HEADS
    q_off, k_off, v_off = C_QSW[0] // LANES, C_KSW[0] // LANES, C_VSW[0] // LANES

    def body(hp_ref, q_ref, k_ref, v_ref, o_ref, do_ref, lse_ref, dq_ref, dk_ref, dv_ref, dsink_ref):
        kh, g, i = pl.program_id(0), pl.program_id(1), pl.program_id(2)
        h = kh * group + g
        sink = hp_ref[h, 1]

        @pl.when((g == 0) & (i == 0))
        def _():
            dk_ref[...] = jnp.zeros_like(dk_ref)
            dv_ref[...] = jnp.zeros_like(dv_ref)

        @pl.when(i == 0)
        def _():
            dsink_ref[...] = jnp.zeros_like(dsink_ref)

        for r in range(0, tb, SWA_PIECE):
            rows = pl.ds(r, SWA_PIECE)
            q = q_ref[rows, :].astype(BF16)
            dov = do_ref[rows, :]
            do_b = dov.astype(BF16)
            lse_r = lse_ref[rows, :]
            d_r = jnp.sum(dov * o_ref[rows, :], axis=1, keepdims=True)
            kstart, k, v, sc = _swa_piece(hp_ref, h, q, k_ref, v_ref, i * tb + r, scale)
            p = jnp.exp(sc - lse_r)
            dp = lax.dot_general(do_b, v, NT_DIMS, preferred_element_type=F32)
            ds = (p * (dp - d_r)).astype(BF16)
            dq_ref[rows, :] = (jnp.dot(ds, k, preferred_element_type=F32) * scale).astype(dq_ref.dtype)
            win = pl.ds(kstart, SWA_KEYS)
            dk_ref[win, :] += lax.dot_general(ds, q, TN_DIMS, preferred_element_type=F32) * scale
            dv_ref[win, :] += lax.dot_general(p.astype(BF16), do_b, TN_DIMS, preferred_element_type=F32)
            part = jnp.sum(-jnp.exp(sink - lse_r) * d_r, axis=0, keepdims=True)
            dsink_ref[...] += jnp.broadcast_to(part, (1, LANES))

    whole = lambda off: pl.BlockSpec((s, LANES), lambda kh, g, i: (0, off + kh))
    q_map = lambda kh, g, i: (i, kh * group + g)
    return pl.pallas_call(
        body, name=name, grid=(SWA_KV_HEADS, group, nqb),
        in_specs=[pl.BlockSpec(memory_space=pltpu.SMEM),
                  pl.BlockSpec((tb, LANES), lambda kh, g, i: (i, q_off + kh * group + g)), whole(k_off), whole(v_off),
                  pl.BlockSpec((tb, LANES), q_map), pl.BlockSpec((tb, LANES), q_map),
                  pl.BlockSpec((None, tb, 1), lambda kh, g, i: (kh * group + g, i, 0))],
        out_specs=[pl.BlockSpec((tb, LANES), q_map), whole(0), whole(0),
                   pl.BlockSpec((None, 1, LANES), lambda kh, g, i: (kh * group + g, 0, 0))],
        out_shape=[jax.ShapeDtypeStruct((s, SWA_HEADS * LANES), BF16),
                   jax.ShapeDtypeStruct((s, SWA_KV_HEADS * LANES), F32), jax.ShapeDtypeStruct((s, SWA_KV_HEADS * LANES), F32),
                   jax.ShapeDtypeStruct((SWA_HEADS, 1, LANES), F32)],
        compiler_params=_params(("parallel", "arbitrary", "arbitrary")),
    )(hp, proj, proj, proj, o, do, lse)


def _shift_down(z, k):
    rows = lax.broadcasted_iota(jnp.int32, z.shape, 0)
    return jnp.where(rows >= k, pltpu.roll(z, k, 0), 0.0)


def _shift_up(z, k):
    n = z.shape[0]
    rows = lax.broadcasted_iota(jnp.int32, z.shape, 0)
    return jnp.where(rows < n - k, pltpu.roll(z, n - k, 0), 0.0)


def _rows3(a, b, c):
    r = lax.broadcasted_iota(jnp.int32, (3, a.shape[1]), 0)
    return jnp.where(r == 0, a, jnp.where(r == 1, b, c))


def _col_spec(s, off):
    return pl.BlockSpec((s, LANES), functools.partial(lambda j, off: (0, off + j), off=off))


def _conv_fwd(proj, conv_w, *, name, s):
    def body(gb_ref, gc_ref, u_ref, w_ref, y_ref):
        w0, w1, w2 = w_ref[0:1, :], w_ref[1:2, :], w_ref[2:3, :]
        z = gc_ref[...] * u_ref[...]
        c = w2 * z + w1 * _shift_down(z, 1) + w0 * _shift_down(z, 2)
        y_ref[...] = gb_ref[...] * c

    return pl.pallas_call(
        body, name=name, grid=(2,),
        in_specs=[_col_spec(s, C_GB[0] // LANES), _col_spec(s, C_GC[0] // LANES), _col_spec(s, C_UCONV[0] // LANES),
                  pl.BlockSpec((3, LANES), lambda j: (0, j))],
        out_specs=_col_spec(s, 0), out_shape=jax.ShapeDtypeStruct((s, D_GROUP), F32),
        compiler_params=_params(("parallel",)),
    )(proj, proj, proj, conv_w)


def _conv_bwd(dy, proj, conv_w, *, name, s):
    def body(dy_ref, gb_ref, gc_ref, u_ref, w_ref, dgb_ref, dgc_ref, du_ref, dw_ref):
        w0, w1, w2 = w_ref[0:1, :], w_ref[1:2, :], w_ref[2:3, :]
        gc, u, dyv = gc_ref[...], u_ref[...], dy_ref[...]
        z = gc * u
        z1, z2 = _shift_down(z, 1), _shift_down(z, 2)
        c = w2 * z + w1 * z1 + w0 * z2
        dgb_ref[...] = (dyv * c).astype(dgb_ref.dtype)
        dc = dyv * gb_ref[...]
        dz = w2 * dc + w1 * _shift_up(dc, 1) + w0 * _shift_up(dc, 2)
        dgc_ref[...] = (dz * u).astype(dgc_ref.dtype)
        du_ref[...] = (dz * gc).astype(du_ref.dtype)
        dw_ref[...] = _rows3(jnp.sum(dc * z2, axis=0, keepdims=True), jnp.sum(dc * z1, axis=0, keepdims=True),
                             jnp.sum(dc * z, axis=0, keepdims=True))

    act = jax.ShapeDtypeStruct((s, D_GROUP), BF16)
    return pl.pallas_call(
        body, name=name, grid=(2,),
        in_specs=[_col_spec(s, 0), _col_spec(s, C_GB[0] // LANES), _col_spec(s, C_GC[0] // LANES),
                  _col_spec(s, C_UCONV[0] // LANES), pl.BlockSpec((3, LANES), lambda j: (0, j))],
        out_specs=[_col_spec(s, 0), _col_spec(s, 0), _col_spec(s, 0), pl.BlockSpec((3, LANES), lambda j: (0, j))],
        out_shape=[act, act, act, jax.ShapeDtypeStruct((3, D_GROUP), F32)],
        compiler_params=_params(("parallel",)),
    )(dy, proj, proj, proj, conv_w)


def _pool_select(j, lane, a2, a4, a8, a16):
    lo = lane < HEAD
    return jnp.where(j == 0, jnp.where(lo, a2, a4), jnp.where(lo, a8, a16))


def _pooled(u, j):
    s2 = u + _shift_down(u, 1)
    s4 = s2 + _shift_down(s2, 2)
    s8 = s4 + _shift_down(s4, 4)
    s16 = s8 + _shift_down(s8, 8)
    lane = lax.broadcasted_iota(jnp.int32, u.shape, 1)
    rows = lax.broadcasted_iota(jnp.int32, u.shape, 0)
    win = _pool_select(j, lane, 2, 4, 8, 16)
    count = jnp.minimum(rows + 1, win).astype(F32)
    return _pool_select(j, lane, s2, s4, s8, s16) / count - u, count


def _pool_fwd(proj, wbd, scale, *, name, s):
    def body(u_ref, w_ref, sc_ref, y_ref):
        pooled, _ = _pooled(u_ref[...], pl.program_id(0))
        y_ref[...] = jnp.dot(pooled.astype(BF16), w_ref[...].astype(BF16), preferred_element_type=F32) * sc_ref[...]

    return pl.pallas_call(
        body, name=name, grid=(2,),
        in_specs=[_col_spec(s, C_UPOOL[0] // LANES), pl.BlockSpec((None, LANES, LANES), lambda j: (j, 0, 0)),
                  pl.BlockSpec((1, LANES), lambda j: (0, j))],
        out_specs=_col_spec(s, 0), out_shape=jax.ShapeDtypeStruct((s, D_GROUP), F32),
        compiler_params=_params(("parallel",)),
    )(proj, wbd, scale)


def _pool_bwd(dy, proj, wbd, scale, *, name, s):
    def body(dy_ref, u_ref, w_ref, sc_ref, du_ref, dw_ref, dsc_ref):
        j = pl.program_id(0)
        pooled, count = _pooled(u_ref[...], j)
        pooled_b = pooled.astype(BF16)
        w_b = w_ref[...].astype(BF16)
        dyv = dy_ref[...]
        mixed = jnp.dot(pooled_b, w_b, preferred_element_type=F32)
        dsc_ref[...] = jnp.sum(dyv * mixed, axis=0, keepdims=True)
        dms = (dyv * sc_ref[...]).astype(BF16)
        dw_ref[...] = lax.dot_general(pooled_b, dms, (((0,), (0,)), ((), ())), preferred_element_type=F32)
        dpooled = lax.dot_general(dms, w_b, (((1,), (1,)), ((), ())), preferred_element_type=F32)
        r = dpooled / count
        a2 = r + _shift_up(r, 1)
        a4 = a2 + _shift_up(a2, 2)
        a8 = a4 + _shift_up(a4, 4)
        a16 = a8 + _shift_up(a8, 8)
        lane = lax.broadcasted_iota(jnp.int32, r.shape, 1)
        du_ref[...] = (_pool_select(j, lane, a2, a4, a8, a16) - dpooled).astype(du_ref.dtype)

    return pl.pallas_call(
        body, name=name, grid=(2,),
        in_specs=[_col_spec(s, 0), _col_spec(s, C_UPOOL[0] // LANES),
                  pl.BlockSpec((None, LANES, LANES), lambda j: (j, 0, 0)), pl.BlockSpec((1, LANES), lambda j: (0, j))],
        out_specs=[_col_spec(s, 0), pl.BlockSpec((None, LANES, LANES), lambda j: (j, 0, 0)),
                   pl.BlockSpec((1, LANES), lambda j: (0, j))],
        out_shape=[jax.ShapeDtypeStruct((s, D_GROUP), BF16), jax.ShapeDtypeStruct((2, LANES, LANES), F32),
                   jax.ShapeDtypeStruct((1, D_GROUP), F32)],
        compiler_params=_params(("parallel",)),
    )(dy, proj, wbd, scale)


def _mesh_pos():
    return lax.axis_index("x"), lax.axis_index("y"), lax.axis_index("c")


def _any_specs(n):
    return [pl.BlockSpec(memory_space=pl.ANY)] * n


def _all_gather(xs, *, name):
    n = len(xs)

    def body(*refs):
        x_refs, out_refs = refs[:n], refs[n:2 * n]
        send_sems, recv_sems, local_sems = refs[2 * n:]
        x, y, cc = _mesh_pos()
        me, sibling = (x, y, cc), (x, y, 1 - cc)
        chips = [(1 - x, y), (x, 1 - y), (1 - x, 1 - y)]

        def slot(a, px, py, pc):
            return out_refs[a].at[4 * px + 2 * py + pc]

        def copy(a, k, block, to, src=None):
            return pltpu.make_async_remote_copy(
                src_ref=slot(a, *block) if src is None else src, dst_ref=slot(a, *block), send_sem=send_sems.at[a, k],
                recv_sem=recv_sems.at[a, k], device_id=to, device_id_type=pl.DeviceIdType.MESH)

        mine = [pltpu.make_async_copy(x_refs[a], slot(a, *me), local_sems.at[a]) for a in range(n)]
        first = []
        for a in range(n):
            first.append(copy(a, 0, me, sibling, src=x_refs[a]))
            first += [copy(a, 1 + j, me, (*chip, cc), src=x_refs[a]) for j, chip in enumerate(chips)]
        for cp in mine + first:
            cp.start()
        passed = []
        for j, chip in enumerate(chips):
            for a in range(n):
                copy(a, 1 + j, (*chip, cc), me).wait_recv()
                passed.append(copy(a, 4 + j, (*chip, cc), sibling))
                passed[-1].start()
        for a in range(n):
            copy(a, 0, sibling, me).wait_recv()
        for j, chip in enumerate(chips):
            for a in range(n):
                copy(a, 4 + j, (*chip, 1 - cc), me).wait_recv()
        for cp in first + passed:
            cp.wait_send()
        for cp in mine:
            cp.wait()

    return pl.pallas_call(
        body, name=name, out_shape=[jax.ShapeDtypeStruct((N_DEV,) + a.shape, a.dtype) for a in xs],
        in_specs=_any_specs(n), out_specs=_any_specs(n),
        scratch_shapes=[pltpu.SemaphoreType.DMA((n, 7)), pltpu.SemaphoreType.DMA((n, 7)), pltpu.SemaphoreType.DMA((n,))],
    )(*xs)


def _plan_gather(src_refs, land_refs, send_sems, recv_sems):
    x, y, cc = _mesh_pos()
    me = 4 * x + 2 * y + cc
    plan = []
    for a, (src, land) in enumerate(zip(src_refs, land_refs)):
        for k in range(1, N_DEV):
            px, py, pc = (1 - x if k & 4 else x), (1 - y if k & 2 else y), (1 - cc if k & 1 else cc)
            i = a * (N_DEV - 1) + k - 1
            sems = dict(send_sem=send_sems.at[i], recv_sem=recv_sems.at[i], device_id=(px, py, pc),
                        device_id_type=pl.DeviceIdType.MESH)
            plan.append((pltpu.make_async_remote_copy(src_ref=src, dst_ref=land.at[me], **sems),
                         pltpu.make_async_remote_copy(src_ref=src, dst_ref=land.at[4 * px + 2 * py + pc], **sems)))
    return plan


def _plan_gather_near(src_refs, land_refs, send_sems, recv_sems):
    x, y, cc = _mesh_pos()
    me = 4 * x + 2 * y + cc
    plan = []
    for a, (src, land) in enumerate(zip(src_refs, land_refs)):
        for k, (px, py, pc) in enumerate([(x, y, 1 - cc), (1 - x, y, cc), (x, 1 - y, cc), (1 - x, 1 - y, cc)]):
            sems = dict(send_sem=send_sems.at[4 * a + k], recv_sem=recv_sems.at[4 * a + k], device_id=(px, py, pc),
                        device_id_type=pl.DeviceIdType.MESH)
            plan.append((pltpu.make_async_remote_copy(src_ref=src, dst_ref=land.at[me], **sems),
                         pltpu.make_async_remote_copy(src_ref=src, dst_ref=land.at[4 * px + 2 * py + pc], **sems)))
    return plan


def _plan_gather_pass(src_refs, land_refs, send_sems, recv_sems):
    x, y, cc = _mesh_pos()
    plan = []
    for a, land in enumerate(land_refs):
        for j, (px, py) in enumerate([(1 - x, y), (x, 1 - y), (1 - x, 1 - y)]):
            mine, theirs = land.at[4 * px + 2 * py + cc], land.at[4 * px + 2 * py + 1 - cc]
            sems = dict(send_sem=send_sems.at[3 * a + j], recv_sem=recv_sems.at[3 * a + j], device_id=(x, y, 1 - cc),
                        device_id_type=pl.DeviceIdType.MESH)
            plan.append((pltpu.make_async_remote_copy(src_ref=mine, dst_ref=mine, **sems),
                         pltpu.make_async_remote_copy(src_ref=mine, dst_ref=theirs, **sems)))
    return plan


def _plan_sibling(src_refs, land_refs, send_sems, recv_sems):
    x, y, cc = _mesh_pos()
    plan = []
    for a, (src, land) in enumerate(zip(src_refs, land_refs)):
        cp = pltpu.make_async_remote_copy(
            src_ref=src.at[1 - cc], dst_ref=land, send_sem=send_sems.at[a], recv_sem=recv_sems.at[a],
            device_id=(x, y, 1 - cc), device_id_type=pl.DeviceIdType.MESH)
        plan.append((cp, cp))
    return plan


def _plan_chips(src_refs, land_refs, send_sems, recv_sems):
    x, y, cc = _mesh_pos()
    my_chip = 2 * x + y
    plan = []
    for a, (src, land) in enumerate(zip(src_refs, land_refs)):
        for j, (px, py) in enumerate([(1 - x, y), (x, 1 - y), (1 - x, 1 - y)]):
            peer = 2 * px + py
            sems = dict(send_sem=send_sems.at[3 * a + j], recv_sem=recv_sems.at[3 * a + j], device_id=(px, py, cc),
                        device_id_type=pl.DeviceIdType.MESH)
            plan.append((pltpu.make_async_remote_copy(src_ref=src.at[peer], dst_ref=land.at[my_chip], **sems),
                         pltpu.make_async_remote_copy(src_ref=src.at[peer], dst_ref=land.at[peer], **sems)))
    return plan


HBM_SPEC = pl.BlockSpec(memory_space=pltpu.HBM)
SEM_SPEC = pl.BlockSpec(memory_space=pltpu.SEMAPHORE)
ANY_SPEC = pl.BlockSpec(memory_space=pl.ANY)
SIDE_EFFECT = pltpu.CompilerParams(has_side_effects=pltpu.SideEffectType.DATAFLOW_SIDE_EFFECTING)


def _start_copies(plan, sems_per_array, srcs, lands, after, *, name):
    lands = [lax.empty(l, a.dtype) if isinstance(l, tuple) else l for l, a in zip(lands, srcs or lands)]
    ns, n = len(srcs), len(srcs) + len(lands)

    def body(*refs):
        send_sems, recv_sems = refs[n + len(after)], refs[n + len(after) + 1]
        for out, _ in plan(refs[:ns], refs[ns:n], send_sems, recv_sems):
            out.start()
        refs[-1][...] = jnp.zeros_like(refs[-1])

    sem = pltpu.SemaphoreType.DMA((len(lands) * sems_per_array,))
    res = pl.pallas_call(
        body, name=name,
        out_shape=(sem, sem, *[pltpu.HBM(a.shape, a.dtype) for a in srcs + lands], jax.ShapeDtypeStruct((8, LANES), F32)),
        in_specs=[HBM_SPEC] * n + [ANY_SPEC] * len(after),
        out_specs=(SEM_SPEC, SEM_SPEC, *[HBM_SPEC] * n, pl.BlockSpec(memory_space=pltpu.VMEM)),
        input_output_aliases={i: 2 + i for i in range(n)}, compiler_params=SIDE_EFFECT,
    )(*[pltpu.with_memory_space_constraint(a, pltpu.HBM) for a in srcs + lands], *after)
    return (res[0], res[1], list(res[2:2 + ns]), list(res[2 + ns:2 + n])), res[-1]


def _wait_copies(plan, handle, after, *, name):
    send, recv, srcs, lands = handle
    ns, n = len(srcs), len(srcs) + len(lands)

    def body(*refs):
        for out, inc in plan(refs[:ns], refs[ns:n], refs[n], refs[n + 1]):
            out.wait_send()
            inc.wait_recv()

    res = pl.pallas_call(
        body, name=name, out_shape=tuple(pltpu.HBM(a.shape, a.dtype) for a in srcs + lands),
        in_specs=[HBM_SPEC] * n + [SEM_SPEC, SEM_SPEC] + [ANY_SPEC] * len(after), out_specs=[HBM_SPEC] * n,
        input_output_aliases={i: i for i in range(n)}, compiler_params=SIDE_EFFECT,
    )(*srcs, *lands, send, recv, *after)
    return list(res[:ns]), list(res[ns:])


def _row_tile(rows, target=512):
    if rows <= target:
        return rows
    best = None
    for t in range(8, target + 1, 8):
        if rows % t == 0:
            best = t
    assert best is not None, (rows, target)
    return best


def _add_own(g, other, core, *, name):
    _, _, rows, cols = g.shape
    tm = _row_tile(rows)

    def body(c_ref, g_ref, o_ref, out_ref):
        out_ref[...] = g_ref[...] + o_ref[...]

    return pl.pallas_call(
        body, name=name, out_shape=jax.ShapeDtypeStruct(other.shape, other.dtype),
        grid_spec=pltpu.PrefetchScalarGridSpec(
            num_scalar_prefetch=1, grid=(4, rows // tm),
            in_specs=[pl.BlockSpec((None, None, tm, cols), lambda p, i, c_ref: (c_ref[0], p, i, 0)),
                      pl.BlockSpec((None, tm, cols), lambda p, i, c_ref: (p, i, 0))],
            out_specs=pl.BlockSpec((None, tm, cols), lambda p, i, c_ref: (p, i, 0))),
        compiler_params=_params(("parallel", "parallel")),
    )(core, g, other)


def _adamw(parts, w, m, v, after, *, name):
    layers, rows, cols = w.shape
    assert len(parts) == layers
    tm = _row_tile(rows, 256)
    nr = rows // tm

    def body(*refs):
        p_refs = refs[:layers]
        w_ref, m_ref, v_ref, _, g_ref, d_ref, nm_ref, nv_ref, g_s = refs[layers:]
        for ll in range(layers):
            @pl.when(pl.program_id(0) == ll)
            def _(ll=ll):
                g = p_refs[ll][0]
                for q in range(1, p_refs[ll].shape[0]):
                    g = g + p_refs[ll][q]
                g_s[...] = g

        g = g_s[...]
        mm = ADAM_B1 * m_ref[...] + (1.0 - ADAM_B1) * g
        vv = ADAM_B2 * v_ref[...] + (1.0 - ADAM_B2) * jnp.square(g)
        m_hat = mm / (1.0 - ADAM_B1 ** ADAM_STEP)
        v_hat = vv / (1.0 - ADAM_B2 ** ADAM_STEP)
        g_ref[...] = g
        d_ref[...] = -ADAM_LR * (m_hat / (jnp.sqrt(v_hat) + ADAM_EPS) + ADAM_WD * w_ref[...])
        nm_ref[...] = mm
        nv_ref[...] = vv

    def part_spec(ll, p):
        return pl.BlockSpec((p, tm, cols), lambda l, i: (0, jnp.where(l == ll, i, jnp.where(l < ll, 0, nr - 1)), 0))

    spec = pl.BlockSpec((None, tm, cols), lambda l, i: (l, i, 0))
    out = jax.ShapeDtypeStruct(w.shape, F32)
    return pl.pallas_call(
        body, name=name, grid=(layers, nr),
        in_specs=[part_spec(ll, parts[ll].shape[0]) for ll in range(layers)] + [spec] * 3 + [pl.BlockSpec(memory_space=pl.ANY)],
        out_specs=[spec] * 4, out_shape=[out] * 4, scratch_shapes=[pltpu.VMEM((tm, cols), F32)],
        compiler_params=_params(("arbitrary", "arbitrary")),
    )(*parts, w, m, v, after)


def _pack(arrs):
    flat = jnp.concatenate([a.reshape(-1) for a in arrs])
    rows = -(-flat.shape[0] // (PACK_COLS * 16)) * 16
    return jnp.pad(flat, (0, rows * PACK_COLS - flat.shape[0])).reshape(rows, PACK_COLS)


def _unpack(packed, shapes):
    flat = packed.reshape(-1)
    out, off = [], 0
    for shp in shapes:
        n = int(np.prod(shp))
        out.append(flat[off:off + n].reshape(shp))
        off += n
    return out


def _shards_to_full(g, axis):
    if axis == 0:
        return g.reshape(g.shape[0] * g.shape[1], g.shape[2])
    return jnp.transpose(g, (1, 0, 2)).reshape(g.shape[1], g.shape[0] * g.shape[2])


def _full_to_shards(a, axis):
    if axis == 0:
        return jnp.transpose(a.reshape(4, 2, a.shape[0] // N_DEV, a.shape[1]), (1, 0, 2, 3))
    return jnp.transpose(a.reshape(a.shape[0], 4, 2, a.shape[1] // N_DEV), (2, 1, 0, 3))


def _zeros_like_cols(a, n):
    return jnp.zeros(a.shape[:-1] + (n,), a.dtype)


def _pad_heads(a, n):
    z = _zeros_like_cols(a, HEAD)
    return jnp.concatenate([p for h in range(n) for p in (a[..., h * HEAD:(h + 1) * HEAD], z)], axis=-1)


def _unpad_heads(a, n):
    return jnp.concatenate([a[..., h * LANES:h * LANES + HEAD] for h in range(n)], axis=-1)


def _seg(first, width, sign=1):
    return (width, [(first, sign)])


def _zero(width):
    return (width, [])


def _swapped(first):
    half = MLA_ROPE // 2
    return [_seg(first + half, half, -1), _seg(first, half)]


def _padded_heads(first, n):
    return [s for h in range(n) for s in (_seg(first + HEAD * h, HEAD), _zero(HEAD))]


def _layout_w_in():
    kr = 384
    return (_padded_heads(1440, 4) + [_seg(0, 256), _seg(416, 256), _seg(672, 256), _seg(928, 256), _seg(1184, 256)]
            + _padded_heads(1696, 2) + _padded_heads(1824, 2) + [_seg(256, 128)]
            + [_zero(HEAD), _seg(kr, MLA_ROPE), _seg(kr, MLA_ROPE)] + [_zero(HEAD)] + _swapped(kr) + _swapped(kr))


def _layout_w_uq():
    out = []
    for h in range(MLA_HEADS):
        out += [_seg(96 * h, MLA_NOPE), _seg(96 * h + MLA_NOPE, MLA_ROPE)] + _swapped(96 * h + MLA_NOPE)
    return out


def _layout_w_ukv():
    keys = [s for h in range(MLA_HEADS) for s in (_seg(LANES * h, HEAD), _zero(HEAD))]
    values = [s for h in range(MLA_HEADS) for s in (_seg(LANES * h + HEAD, HEAD), _zero(HEAD))]
    return keys + values


def _layout_w_gate_up():
    return [_seg(half + j, GU_TILE) for j in range(0, D_FF, GU_TILE) for half in (0, D_FF)]


LAYOUTS = dict(w_in=_layout_w_in(), w_uq=_layout_w_uq(), w_ukv=_layout_w_ukv(), w_gate_up=_layout_w_gate_up())
OWN_COLS = dict(w_in=1952, w_uq=384, w_ukv=512, w_gate_up=2 * D_FF)


def _plan_extend(layout, shard):
    plan = []
    for width, terms in layout:
        if not terms:
            plan.append((width, []))
            continue
        (first, sign), = terms
        while width:
            g, off = divmod(first, shard)
            w = min(width, shard - off)
            plan.append((w, [(g, off, sign)]))
            first, width = first + w, width - w
    return [plan]


def _plan_fold(layout, own_cols):
    sources = [[] for _ in range(own_cols)]
    e = 0
    for width, terms in layout:
        for first, sign in terms:
            for i in range(width):
                sources[first + i].append((e + i, sign))
        e += width
    shard = own_cols // N_DEV
    plans = {}
    for g in range(N_DEV):
        plan, n = [], g * shard
        while n < (g + 1) * shard:
            w = 1
            while n + w < (g + 1) * shard and [(c + w, sg) for c, sg in sources[n]] == sources[n + w]:
                w += 1
            plan.append((w, [(0, c, sg) for c, sg in sources[n]]))
            n += w
        plans[g] = plan
    return [plans[2 * p + c] for c in range(2) for p in range(4)]


def _assemble(src, plans, out_cols, out_dtype, *, name):
    g, rows, c = src.shape
    tm = _row_tile(rows, 256)

    def body(s_ref, o_ref):
        blocks = [s_ref[i].astype(F32) for i in range(g)]
        for d, plan in enumerate(plans):
            pieces = []
            for width, terms in plan:
                v = None
                for b, first, sign in terms:
                    t = blocks[b][:, first:first + width]
                    t = -t if sign < 0 else t
                    v = t if v is None else v + t
                pieces.append(jnp.zeros((tm, width), F32) if v is None else v)
            o_ref[d] = (pieces[0] if len(pieces) == 1 else jnp.concatenate(pieces, axis=1)).astype(o_ref.dtype)

    return pl.pallas_call(
        body, name=name, grid=(rows // tm,), in_specs=[pl.BlockSpec((g, tm, c), lambda i: (0, i, 0))],
        out_specs=pl.BlockSpec((len(plans), tm, out_cols), lambda i: (0, i, 0)),
        out_shape=jax.ShapeDtypeStruct((len(plans), rows, out_cols), out_dtype), compiler_params=_params(("parallel",)),
    )(src)


def _extend(nm, gathered, *, name):
    layout = LAYOUTS[nm]
    return _assemble(gathered, _plan_extend(layout, OWN_COLS[nm] // N_DEV), sum(w for w, _ in layout), BF16, name=name)[0]


def _fold_to_shards(nm, grad_ext, *, name):
    shards = _assemble(grad_ext[None], _plan_fold(LAYOUTS[nm], OWN_COLS[nm]), OWN_COLS[nm] // N_DEV, F32, name=name)
    return shards.reshape((2, 4) + shards.shape[1:])


def _rope_tables(s):
    inv = 1.0 / (ROPE_THETA ** (jnp.arange(0, MLA_ROPE, 2, dtype=F32) / MLA_ROPE))
    ang = jnp.arange(s, dtype=F32)[:, None] * inv[None, :]
    cos, sin = jnp.cos(ang), jnp.sin(ang)
    c32, s32 = jnp.concatenate([cos, cos], axis=1), jnp.concatenate([sin, sin], axis=1)
    zeros, ones = jnp.zeros((s, HEAD), F32), jnp.ones((s, HEAD), F32)
    tq = jnp.concatenate([ones, c32, s32], axis=1) * (1.0 / math.sqrt(MLA_NOPE + MLA_ROPE))
    return (jnp.tile(tq, (1, MLA_HEADS)), jnp.concatenate([zeros, c32, c32], axis=1),
            jnp.concatenate([zeros, s32, s32], axis=1))


def _gn(y):
    return y * _rstd(y, D_GROUP)


def _mixer_fwd(x, w, tabs, l):
    s = x.shape[0]
    tq, tkc, tks = tabs
    n = lambda t: f"l{l}_{t}"
    h = _rms_fwd(("row", x, D_MODEL, 0), w["attn_norm"], name=n("attn_norm"), rows=s, width=D_MODEL)
    proj = _mm(h, w["w_in"], name=n("in_proj"))
    def prep(cq, ckv, kr, krp, gq, gkv, wuq, wukv, tqv, tc, ts):
        cqn = (cq * _rstd(cq, 256) * gq).astype(BF16)
        ckvn = (ckv * _rstd(ckv, 128) * gkv).astype(BF16)
        qe = jnp.dot(cqn, wuq, preferred_element_type=F32)
        kve = jnp.dot(ckvn, wukv, preferred_element_type=F32)
        kb = kr * tc + krp * ts
        kvv = kve[:, 512:]
        lane = lax.broadcasted_iota(jnp.int32, kvv.shape, 1) & (LANES - 1)
        v = jnp.where(lane == HEAD, 1.0, kvv)
        k = kve[:, :512] + jnp.tile(kb, (1, MLA_HEADS))
        return cqn, ckvn, qe * tqv, k, k, v, v

    cqn, ckvn, qm, km, kmt, vm, vmt = _rowwise(
        prep, name=n("mla_prep"), rows=s,
        ins=[("row", proj, 256, C_CQ[0] // 256), ("row", proj, 128, C_CKV[0] // 128),
             ("row", proj, 128, C_KR[0] // 128), ("row", proj, 128, C_KRP[0] // 128),
             ("full", w["mla_q_norm"]), ("full", w["mla_kv_norm"]), ("full", w["w_uq"]), ("full", w["w_ukv"]),
             ("row", tq, 512, 0), ("row", tkc, 128, 0), ("row", tks, 128, 0)],
        outs=[(256, BF16), (128, BF16), (512, BF16, MLA_HEADS), (512, BF16, MLA_HEADS), (512, BF16, MLA_HEADS, "T"),
              (512, BF16, MLA_HEADS), (512, BF16, MLA_HEADS, "T")])
    y_a, lse_row = _mla_fwd(qm, km, vmt, name=n("mla_fwd"), s=s, nh=MLA_HEADS)
    y_b = _conv_fwd(proj, w["conv_w"], name=n("conv_fwd"), s=s)
    y_c = _pool_fwd(proj, w["pool_wbd"], w["pool_scale"], name=n("pool_fwd"), s=s)
    y_d, lse_d = _swa_fwd(proj, w["hp_swa"], name=n("swa_fwd"), s=s, scale=1.0 / math.sqrt(HEAD))

    def mix(ya, yb, yc, yd, mn):
        return (jnp.concatenate([_gn(_unpad_heads(ya, 4)), _gn(yb), _gn(yc), _gn(_unpad_heads(yd, 4))], axis=1) * mn,)

    mixed = _rowwise(mix, name=n("group_norm"), rows=s,
                     ins=[("heads", y_a), ("row", y_b, 256, 0), ("row", y_c, 256, 0), ("row", y_d, 512, 0),
                          ("full", w["mix_norm"])], outs=[(D_MODEL, BF16)])[0]
    x1 = _mm(mixed, w["w_o"], res=x, name=n("out_proj"))
    saved = dict(x=x, h=h, proj=proj, cqn=cqn, ckvn=ckvn, qm=qm, km=km, kmt=kmt, vm=vm, y_a=y_a, lse_row=lse_row,
                 y_b=y_b, y_c=y_c, y_d=y_d, lse_d=lse_d, mixed=mixed)
    return x1, saved


def _ffn_fwd(x1, w, l):
    s = x1.shape[0]
    n = lambda t: f"l{l}_{t}"
    h2 = _rms_fwd(("row", x1, D_MODEL, 0), w["ffn_norm"], name=n("ffn_norm"), rows=s, width=D_MODEL)

    def swiglu(gu):
        g, u = gu[:, :GU_TILE], gu[:, GU_TILE:]
        return gu, g * jax.nn.sigmoid(g) * u

    gu, act = _mm(h2, w["w_gate_up"], tm=2048, tn=2 * GU_TILE, name=n("gate_up"),
                  epilogue=(swiglu, [], [(2 * D_FF, BF16), (D_FF, BF16)]))
    x2 = _mm(act, w["w_down"], res=x1, tk=D_FF // 2, name=n("down"))
    return x2, dict(x1=x1, h2=h2, gu=gu, act=act)


def _ffn_bwd_down(dx2, sv, w, l):
    n = lambda t: f"l{l}_{t}"

    def swiglu_bwd(da, gu):
        gt, u = gu[:, :GU_TILE].astype(F32), gu[:, GU_TILE:].astype(F32)
        sg = jax.nn.sigmoid(gt)
        return (jnp.concatenate([da * u * sg * (1.0 + gt * (1.0 - sg)), da * gt * sg], axis=1),)

    dgu = _mm(dx2[1], w["w_down"], tb=True, tm=2048, tn=GU_TILE, name=n("d_act"),
              epilogue=(swiglu_bwd, [sv["gu"]], [(2 * D_FF, BF16)]))[0]
    g = dict(w_down=_mm(sv["act"], dx2[1], ta=True, tm=D_FF // 2, name=n("dw_down")))
    return dgu, g


def _ffn_bwd_up(dx2, dgu, sv, w, l):
    s = dgu.shape[0]
    n = lambda t: f"l{l}_{t}"
    dh2 = _mm(dgu, w["w_gate_up"], tb=True, tk=D_FF // 2, name=n("d_h2"))
    g = dict(w_gate_up=_mm(sv["h2"], dgu, ta=True, tn=D_FF // 2, name=n("dw_gate_up")))
    dx1, dx1_b, g["ffn_norm"] = _rms_bwd(("row", sv["x1"], D_MODEL, 0), w["ffn_norm"], dh2, dx2[0],
                                         name=n("ffn_norm_bwd"), rows=s, width=D_MODEL, out_dtypes=(F32, BF16))
    return (dx1, dx1_b), g


def _mixer_bwd_out(dx1, sv, w, l):
    s = dx1[1].shape[0]
    n = lambda t: f"l{l}_{t}"
    dmixed = _mm(dx1[1], w["w_o"], tb=True, name=n("d_mixed"))
    g = dict(w_o=_mm(sv["mixed"], dx1[1], ta=True, name=n("dw_o")))

    def mix_bwd(dm, ya, yb, yc, yd, mn):
        outs, dmn = [], []
        for i, y in enumerate((_unpad_heads(ya, 4), yb, yc, _unpad_heads(yd, 4))):
            lo, hi = i * D_GROUP, (i + 1) * D_GROUP
            r = _rstd(y, D_GROUP)
            nrm = y * r
            dmg = dm[:, lo:hi]
            dn = dmg * mn[:, lo:hi]
            dy = r * (dn - nrm * (jnp.sum(dn * nrm, axis=-1, keepdims=True) * (1.0 / D_GROUP)))
            outs.append(_pad_heads(dy, 4) if i in (0, 3) else dy)
            dmn.append(jnp.sum(dmg * nrm, axis=0, keepdims=True))
        return (*outs, jnp.concatenate(dmn, axis=1))

    dy_a, dy_b, dy_c, dy_d, g["mix_norm"] = _rowwise(
        mix_bwd, name=n("group_norm_bwd"), rows=s,
        ins=[("row", dmixed, D_MODEL, 0), ("heads", sv["y_a"]), ("row", sv["y_b"], 256, 0),
             ("row", sv["y_c"], 256, 0), ("row", sv["y_d"], 512, 0), ("full", w["mix_norm"])],
        outs=[(512, F32, MLA_HEADS), (256, F32), (256, F32), (512, F32)], reds=[(1, D_MODEL)])
    return (dy_a, dy_b, dy_c, dy_d), g


def _mixer_bwd_in(dx1, dys, sv, w, tabs, l):
    s = dx1[0].shape[0]
    tq, tkc, tks = tabs
    n = lambda t: f"l{l}_{t}"
    dy_a, dy_b, dy_c, dy_d = dys
    g = {}

    proj = sv["proj"]
    dq_sw, dk_sw, dv_sw, dsink = _swa_bwd(proj, sv["y_d"], dy_d, sv["lse_d"], w["hp_swa"], name=n("swa_bwd"), s=s,
                                          scale=1.0 / math.sqrt(HEAD))
    g["swa_sinks"] = dsink[:, 0, 0]

    dqm, dkm, dvm = _mla_bwd(sv["qm"], sv["km"], sv["kmt"], sv["vm"], sv["y_a"], dy_a, sv["lse_row"], w["hp_swa"],
                             name=n("mla_bwd"), s=s, nh=MLA_HEADS)

    def rms_bwd(x, gv, dy, width):
        r = _rstd(x, width)
        dyg = dy * gv
        dx = r * dyg - x * (r * r * r) * (jnp.sum(dyg * x, axis=-1, keepdims=True) * (1.0 / width))
        return dx, jnp.sum(dy * x * r, axis=0, keepdims=True)

    def prep_bwd(dq, dk, dv, cq, ckv, gq, gkv, wuq, wukv, tqv, tc, ts):
        dkb = dk[:, 0:128] + dk[:, 128:256] + dk[:, 256:384] + dk[:, 384:512]
        dq_ext = (dq * tqv).astype(BF16)
        dkv_ext = jnp.concatenate([dk.astype(BF16), dv], axis=1)
        dcqn = lax.dot_general(dq_ext, wuq, NT_DIMS, preferred_element_type=F32)
        dckvn = lax.dot_general(dkv_ext, wukv, NT_DIMS, preferred_element_type=F32)
        dcq, dgq = rms_bwd(cq, gq, dcqn, 256)
        dckv, dgkv = rms_bwd(ckv, gkv, dckvn, 128)
        return dq_ext, dkv_ext, dkb * tc, dkb * ts, dcq, dckv, dgq, dgkv

    dq_ext, dkv_ext, dkr, dkrp, dcq, dckv, g["mla_q_norm"], g["mla_kv_norm"] = _rowwise(
        prep_bwd, name=n("mla_prep_bwd"), rows=s,
        ins=[("heads", dqm), ("heads", dkm), ("heads", dvm), ("row", proj, 256, C_CQ[0] // 256),
             ("row", proj, 128, C_CKV[0] // 128), ("full", w["mla_q_norm"]), ("full", w["mla_kv_norm"]),
             ("full", w["w_uq"]), ("full", w["w_ukv"]), ("row", tq, 512, 0), ("row", tkc, 128, 0), ("row", tks, 128, 0)],
        outs=[(512, BF16), (1024, BF16), (128, BF16), (128, BF16), (256, BF16), (128, BF16)],
        reds=[(1, 256), (1, 128)])
    g["w_uq"] = _mm(sv["cqn"], dq_ext, ta=True, name=n("dw_uq"))
    g["w_ukv"] = _mm(sv["ckvn"], dkv_ext, ta=True, name=n("dw_ukv"))

    dgb, dgc, duc, g["conv_w"] = _conv_bwd(dy_b, proj, w["conv_w"], name=n("conv_bwd"), s=s)
    dup, g["pool_wbd"], g["pool_scale"] = _pool_bwd(dy_c, proj, w["pool_wbd"], w["pool_scale"], name=n("pool_bwd"), s=s)

    dproj = jnp.concatenate([dq_sw, dcq, dgb, dgc, duc, dup, dk_sw.astype(BF16), dv_sw.astype(BF16), dckv, dkr, dkrp],
                            axis=1)
    g["w_in"] = _mm(sv["h"], dproj, ta=True, name=n("dw_in"))
    return dproj, g


def _mixer_bwd_norm(dx1, dproj, sv, w, l):
    n = lambda t: f"l{l}_{t}"
    dh = _mm(dproj, w["w_in"], tb=True, name=n("d_h"))
    dx0, dx0_b, dg = _rms_bwd(("row", sv["x"], D_MODEL, 0), w["attn_norm"], dh, dx1[0], name=n("attn_norm_bwd"),
                              rows=dh.shape[0], width=D_MODEL, out_dtypes=(F32, BF16))
    return (dx0, dx0_b), dict(attn_norm=dg)


def _loss_head(x, target, g, *, s):
    def fn(xv, tv, gv):
        r = _rstd(xv, D_MODEL)
        e = xv * r * gv - tv
        part = jnp.sum(jnp.sum(e * e, axis=1, keepdims=True), axis=0, keepdims=True) * (0.5 / D_MODEL)
        dy = e * (1.0 / D_MODEL)
        dyg = dy * gv
        dx = r * dyg - xv * (r * r * r) * (jnp.sum(dyg * xv, axis=-1, keepdims=True) * (1.0 / D_MODEL))
        return dx, dx, jnp.sum(dy * xv * r, axis=0, keepdims=True), jnp.broadcast_to(part, (1, LANES))

    return _rowwise(fn, name="loss_head", rows=s,
                    ins=[("row", x, D_MODEL, 0), ("row", target, D_MODEL, 0), ("full", g)],
                    outs=[(D_MODEL, F32), (D_MODEL, BF16)], reds=[(1, D_MODEL), (1, LANES)])


def _alibi_slopes(n):
    return np.asarray([2.0 ** (-8.0 * (i + 1) / n) for i in range(n)], dtype=np.float32)


MIXER_WEIGHTS = ("w_in", "w_uq", "w_ukv", "conv_w", "w_o")
FFN_WEIGHTS = ("w_gate_up", "w_down")


def _mixer_weights(full, rep, l):
    pw = rep["pool_w"][l]
    z = jnp.zeros((HEAD, HEAD), F32)
    wbd = jnp.stack([jnp.block([[pw[2 * j], z], [z, pw[2 * j + 1]]]) for j in range(2)])
    return dict(
        attn_norm=rep["attn_norm"][l][None], w_in=full["w_in"], mla_q_norm=rep["mla_q_norm"][l][None],
        w_uq=full["w_uq"], mla_kv_norm=rep["mla_kv_norm"][l][None], w_ukv=full["w_ukv"],
        conv_w=full["conv_w"], pool_wbd=wbd, pool_scale=rep["pool_scale"][l][None],
        mix_norm=rep["mix_norm"][l][None], w_o=full["w_o"],
        hp_swa=jnp.stack([jnp.asarray(_alibi_slopes(SWA_HEADS)), rep["swa_sinks"][l]], axis=1))


def _ffn_weights(full, rep, l):
    return dict(ffn_norm=rep["ffn_norm"][l][None], w_gate_up=full["w_gate_up"], w_down=full["w_down"])


def _small_grads(g):
    rows = ("attn_norm", "mla_q_norm", "mla_kv_norm", "pool_scale", "ffn_norm", "mix_norm")
    out = {nm: g[nm][0] for nm in rows if nm in g}
    if "swa_sinks" in g:
        out["swa_sinks"] = g["swa_sinks"]
    if "pool_wbd" in g:
        e = g["pool_wbd"]
        out["pool_w"] = jnp.stack([e[j // 2][HEAD * (j % 2):HEAD * (j % 2 + 1), HEAD * (j % 2):HEAD * (j % 2 + 1)]
                                   for j in range(4)])
    return out


def kernel(x, attn_norm, w_in, mla_q_norm, w_uq, mla_kv_norm, w_ukv, conv_w, pool_w, pool_scale, swa_sinks, mix_norm, w_o, ffn_norm, w_gate_up, w_down, final_norm, loss_target, m_attn_norm, m_w_in, m_mla_q_norm, m_w_uq, m_mla_kv_norm, m_w_ukv, m_conv_w, m_pool_w, m_pool_scale, m_swa_sinks, m_mix_norm, m_w_o, m_ffn_norm, m_w_gate_up, m_w_down, m_final_norm, v_attn_norm, v_w_in, v_mla_q_norm, v_w_uq, v_mla_kv_norm, v_w_ukv, v_conv_w, v_pool_w, v_pool_scale, v_swa_sinks, v_mix_norm, v_w_o, v_ffn_norm, v_w_gate_up, v_w_down, v_final_norm):
    given = dict(locals())
    sh_names = [nm for nm, _, _ in SHARDED]
    sh_axis = {nm: ax - 1 for nm, _, ax in SHARDED}
    rep_names = [nm for nm, _ in REPLICATED]
    rep_shapes = [shp for _, shp in REPLICATED]
    rep = {nm: given[nm] for nm in rep_names if nm != "loss"}
    me = 4 * lax.axis_index("x") + 2 * lax.axis_index("y") + lax.axis_index("c")
    my_chip = 2 * lax.axis_index("x") + lax.axis_index("y")
    core = lax.axis_index("c").astype(jnp.int32).reshape(1)

    def behind(token, a):
        return a + token[0, 0].astype(a.dtype)

    def wire(nm, l):
        if nm == "conv_w":
            return lax.bitcast_convert_type(given[nm][l], BF16).reshape(3, -1)
        return given[nm][l].astype(BF16)

    def whole(nm, g, tag):
        if nm in LAYOUTS:
            return _extend(nm, g, name=f"extend_{nm}_{tag}")
        if nm == "conv_w":
            g = lax.bitcast_convert_type(g.reshape(N_DEV, 3, -1, 2), F32)
        return _shards_to_full(g, sh_axis[nm])

    def gather_start(names, l, after, tag):
        srcs = [wire(nm, l) for nm in names]
        return _start_copies(_plan_gather, N_DEV - 1, srcs, [(N_DEV,) + a.shape for a in srcs], after, name=f"start_gather_{tag}")

    def with_own(names, srcs, lands, tag):
        return {nm: whole(nm, lax.dynamic_update_index_in_dim(land, src, me, 0), tag)
                for nm, src, land in zip(names, srcs, lands)}

    def gather_wait(names, handle, after, tag):
        return with_own(names, *_wait_copies(_plan_gather, handle, after, name=f"wait_gather_{tag}"), tag)

    layer1 = MIXER_WEIGHTS + FFN_WEIGHTS
    got = _all_gather([wire(nm, 0) for nm in MIXER_WEIGHTS], name="gather_mixer0")
    full_m0 = {nm: whole(nm, g, "mixer0") for nm, g in zip(MIXER_WEIGHTS, got)}
    h_f0, tok = gather_start(FFN_WEIGHTS, 0, [], "ffn0")
    srcs_1 = [wire(nm, 1) for nm in layer1]
    h_l1, tok = _start_copies(_plan_gather_near, 4, srcs_1, [(N_DEV,) + a.shape for a in srcs_1], [tok],
                              name="start_gather_layer1")

    xs, target = x[0], loss_target[0]
    s = xs.shape[0]
    tabs = _rope_tables(s)
    wm, wf, svm, svf = [None] * DEPTH, [None] * DEPTH, [None] * DEPTH, [None] * DEPTH
    wm[0] = _mixer_weights(full_m0, rep, 0)
    wm[0]["attn_norm"] = behind(tok, wm[0]["attn_norm"])
    x1, svm[0] = _mixer_fwd(xs, wm[0], tabs, 0)
    wf[0] = _ffn_weights(gather_wait(FFN_WEIGHTS, h_f0, [x1], "ffn0"), rep, 0)
    srcs_1, lands_1 = _wait_copies(_plan_gather_near, h_l1, [x1], name="wait_gather_layer1")
    h_l1, tok = _start_copies(_plan_gather_pass, 3, [], lands_1, [], name="start_pass_layer1")
    wf[0]["ffn_norm"] = behind(tok, wf[0]["ffn_norm"])
    x2, svf[0] = _ffn_fwd(x1, wf[0], 0)
    _, lands_1 = _wait_copies(_plan_gather_pass, h_l1, [x2], name="wait_pass_layer1")
    full_1 = with_own(layer1, srcs_1, lands_1, "layer1")
    wm[1], wf[1] = _mixer_weights(full_1, rep, 1), _ffn_weights(full_1, rep, 1)
    x1, svm[1] = _mixer_fwd(x2, wm[1], tabs, 1)
    x2, svf[1] = _ffn_fwd(x1, wf[1], 1)
    dx_f, dx_b, d_final, loss = _loss_head(x2, target, rep["final_norm"][None], s=s)
    dx = (dx_f, dx_b)

    parts = {}

    def reduce_start(grads, l, after, tag):
        names = [nm for nm in sh_names if nm in grads]
        mine = [_fold_to_shards(nm, grads[nm], name=f"fold_{nm}_{l}") if nm in LAYOUTS
                else _full_to_shards(grads[nm], sh_axis[nm]) for nm in names]
        handle, token = _start_copies(_plan_sibling, 1, mine, [m.shape[1:] for m in mine], after, name=f"start_sibling_{tag}")
        return (names, l, handle), token

    def reduce_mid(state, after, tag):
        names, l, handle = state
        mine, theirs = _wait_copies(_plan_sibling, handle, after, name=f"wait_sibling_{tag}")
        sums = [_add_own(g, o, core, name=f"chip_sum_{nm}_{l}") for nm, g, o in zip(names, mine, theirs)]
        handle, token = _start_copies(_plan_chips, 3, sums, [a.shape for a in sums], [], name=f"start_chips_{tag}")
        return (names, l, handle), token

    def reduce_end(state, after, tag):
        names, l, handle = state
        sums, lands = _wait_copies(_plan_chips, handle, after, name=f"wait_chips_{tag}")
        for nm, own, land in zip(names, sums, lands):
            parts[nm, l] = lax.dynamic_update_index_in_dim(land, lax.dynamic_index_in_dim(own, my_chip, 0, keepdims=False),
                                                           my_chip, 0)

    small = [None] * DEPTH
    in_flight = []
    pending = None
    for l in reversed(range(DEPTH)):
        dgu, g_down = _ffn_bwd_down(dx, svf[l], wf[l], l)
        if pending is not None:
            state, token = reduce_mid(pending, [dgu], f"mixer{l + 1}")
            in_flight.append((state, f"mixer{l + 1}"))
            wf[l]["ffn_norm"] = behind(token, wf[l]["ffn_norm"])
        dx1, g_up = _ffn_bwd_up(dx, dgu, svf[l], wf[l], l)
        g_ffn = {**g_down, **g_up}
        state, token = reduce_start(g_ffn, l, [], f"ffn{l}")
        wm[l]["mix_norm"] = behind(token, wm[l]["mix_norm"])
        dys, g_out = _mixer_bwd_out(dx1, svm[l], wm[l], l)
        state, token = reduce_mid(state, [dys[0]], f"ffn{l}")
        in_flight.append((state, f"ffn{l}"))
        wm[l]["hp_swa"] = behind(token, wm[l]["hp_swa"])
        dproj, g_in = _mixer_bwd_in(dx1, dys, svm[l], wm[l], tabs, l)
        g_mixer = {**g_out, **g_in}
        pending, token = reduce_start(g_mixer, l, [], f"mixer{l}")
        wm[l]["attn_norm"] = behind(token, wm[l]["attn_norm"])
        dx, g_norm = _mixer_bwd_norm(dx1, dproj, svm[l], wm[l], l)
        small[l] = _small_grads({**g_ffn, **g_mixer, **g_norm})
    grads = {nm: jnp.stack([small[l][nm] for l in range(DEPTH)]) for nm in rep_names if nm in small[0]}
    grads["final_norm"] = d_final[0]
    grads["loss"] = loss[0, :1]
    zero = jnp.zeros((1,), F32)
    small = _all_gather([behind(token, _pack([grads[nm] for nm in rep_names]))], name="gather_small_grads")

    last, token = reduce_mid(pending, [dx[0], small[0]], "mixer0")
    for state, tag in in_flight:
        reduce_end(state, [], tag)
    grad_x = dx[0]

    def adamw(nm, after):
        return _adamw([parts[nm, l] for l in range(DEPTH)], given[nm], given["m_" + nm], given["v_" + nm], after,
                      name=f"adamw_{nm}")

    sh_out = {nm: adamw(nm, token) for nm in FFN_WEIGHTS}
    packs = [_pack([given.get(pre + nm, zero) for nm in rep_names])[None] for pre in ("", "m_", "v_")]
    rep_res = _adamw(small, *packs, token, name="adamw_replicated")
    rep_out = [dict(zip(rep_names, _unpack(o[0], rep_shapes))) for o in rep_res]

    reduce_end(last, [rep_res[0], sh_out["w_down"][0]], "mixer0")
    sh_out.update({nm: adamw(nm, token) for nm in MIXER_WEIGHTS})

    out = [rep_out[0]["loss"][0], grad_x[None]]
    for i in range(4):
        out += [sh_out[nm][i] if nm in sh_axis else rep_out[i][nm] for nm in WEIGHT_ORDER]
    return tuple(out)
```

```python
import functools
import math

import numpy as np
import jax
import jax.numpy as jnp
from jax import lax
from jax.experimental import pallas as pl
from jax.experimental.pallas import tpu as pltpu

F32 = jnp.float32
BF16 = jnp.bfloat16

D_MODEL = 1024
DEPTH = 2
D_GROUP = 256
MLA_HEADS = 4
MLA_NOPE = 64
MLA_ROPE = 32
ROPE_THETA = 10000.0
POOL_WINDOWS = (2, 4, 8, 16)
SWA_HEADS = 4
SWA_KV_HEADS = 2
SWA_WINDOW = 128
D_FF = 2816
GU_TILE = 256
RMS_EPS = 1e-6
LANES = 128
HEAD = 64
VMEM_LIMIT = 48 * 1024 * 1024
NEG = -1e30

ADAM_LR = 0.001
ADAM_B1 = 0.9
ADAM_B2 = 0.999
ADAM_EPS = 1e-08
ADAM_WD = 0.01
ADAM_STEP = 10

N_DEV = 8
PACK_COLS = 1024

C_QSW, C_CQ, C_GB, C_GC, C_UCONV, C_UPOOL = (0, 512), (512, 256), (768, 256), (1024, 256), (1280, 256), (1536, 256)
C_KSW, C_VSW, C_CKV, C_KR, C_KRP = (1792, 256), (2048, 256), (2304, 128), (2432, 128), (2560, 128)
D_IN_EXT = 2688

SHARDED = (("w_in", (DEPTH, 1024, 244), 2), ("w_uq", (DEPTH, 256, 48), 2), ("w_ukv", (DEPTH, 128, 64), 2),
           ("conv_w", (DEPTH, 3, 32), 2), ("w_o", (DEPTH, 128, 1024), 1), ("w_gate_up", (DEPTH, 1024, 704), 2),
           ("w_down", (DEPTH, 352, 1024), 1))
REPLICATED = (("attn_norm", (DEPTH, 1024)), ("mla_q_norm", (DEPTH, 256)), ("mla_kv_norm", (DEPTH, 128)),
              ("pool_w", (DEPTH, 4, 64, 64)), ("pool_scale", (DEPTH, 256)), ("swa_sinks", (DEPTH, 4)),
              ("mix_norm", (DEPTH, 1024)), ("ffn_norm", (DEPTH, 1024)), ("final_norm", (1024,)), ("loss", (1,)))
WEIGHT_ORDER = ("attn_norm", "w_in", "mla_q_norm", "w_uq", "mla_kv_norm", "w_ukv", "conv_w", "pool_w", "pool_scale",
                "swa_sinks", "mix_norm", "w_o", "ffn_norm", "w_gate_up", "w_down", "final_norm")


def _params(sem):
    return pltpu.CompilerParams(dimension_semantics=sem, vmem_limit_bytes=VMEM_LIMIT)


def _pick(dim, target):
    if dim <= target:
        return dim
    best = None
    for t in range(LANES, target + 1, LANES):
        if dim % t == 0:
            best = t
    assert best is not None, (dim, target)
    return best


def _mm(a, b, *, name, ta=False, tb=False, res=None, out_dtype=F32, tm=1024, tn=1024, tk=1024, epilogue=None):
    m, k = (a.shape[1], a.shape[0]) if ta else a.shape
    n = b.shape[0] if tb else b.shape[1]
    assert (b.shape[1] if tb else b.shape[0]) == k
    tm, tn, tk = _pick(m, tm), _pick(n, tn), _pick(k, tk)
    nj, nk = n // tn, k // tk
    dims = (((0 if ta else 1,), (1 if tb else 0,)), ((), ()))
    fn, extra, outs = epilogue if epilogue is not None else (None, [], [(n, out_dtype)])
    if res is not None:
        assert epilogue is None
        fn, extra = (lambda acc, r: (acc + r,)), [res]
    n_in, n_out = 2 + len(extra), len(outs)

    def body(*refs):
        a_ref, b_ref, acc_ref = refs[0], refs[1], refs[-1]
        kk = pl.program_id(2)

        @pl.when(kk == 0)
        def _():
            acc_ref[...] = jnp.zeros_like(acc_ref)

        acc_ref[...] += lax.dot_general(a_ref[...].astype(BF16), b_ref[...].astype(BF16), dims,
                                        preferred_element_type=F32)

        @pl.when(kk == nk - 1)
        def _():
            tiles = (acc_ref[...],) if fn is None else fn(acc_ref[...], *[r[...] for r in refs[2:n_in]])
            for o_ref, tile in zip(refs[n_in:n_in + n_out], tiles):
                o_ref[...] = tile.astype(o_ref.dtype)

    def col_tiles(width):
        assert width % (nj * LANES) == 0, (width, nj)
        return pl.BlockSpec((tm, width // nj), lambda i, j, kk: (i, j))

    a_spec = pl.BlockSpec((tk, tm), lambda i, j, kk: (kk, i)) if ta else pl.BlockSpec((tm, tk), lambda i, j, kk: (i, kk))
    b_spec = pl.BlockSpec((tn, tk), lambda i, j, kk: (j, kk)) if tb else pl.BlockSpec((tk, tn), lambda i, j, kk: (kk, j))
    res_ = pl.pallas_call(
        body, name=name, grid=(m // tm, nj, nk), in_specs=[a_spec, b_spec] + [col_tiles(e.shape[1]) for e in extra],
        out_specs=[col_tiles(w) for w, _ in outs],
        out_shape=[jax.ShapeDtypeStruct((m, w), dt) for w, dt in outs],
        scratch_shapes=[pltpu.VMEM((tm, tn), F32)],
        compiler_params=_params(("parallel", "parallel", "arbitrary")),
    )(a, b, *extra)
    return res_[0] if epilogue is None else res_


def _rowwise(fn, *, name, rows, ins, outs, reds=(), tm=512):
    tm = min(tm, rows)
    assert rows % tm == 0
    n_in, n_out = len(ins), len(outs)

    def body(*refs):
        vals = [jnp.concatenate([r[h] for h in range(r.shape[0])], axis=1) if spec[0] == "heads" else r[...]
                for spec, r in zip(ins, refs[:n_in])]
        res = fn(*vals)
        for out, r, v in zip(outs, refs[n_in:n_in + n_out], res[:n_out]):
            if len(out) >= 3:
                for h in range(out[2]):
                    piece = v[:, h * LANES:(h + 1) * LANES]
                    r[h] = (piece.T if len(out) == 4 else piece).astype(r.dtype)
            else:
                r[...] = v.astype(r.dtype)
        if reds:
            @pl.when(pl.program_id(0) == 0)
            def _():
                for r in refs[n_in + n_out:]:
                    r[...] = jnp.zeros_like(r)

            for r, v in zip(refs[n_in + n_out:], res[n_out:]):
                r[...] += v

    in_specs, args = [], []
    for spec in ins:
        if spec[0] == "row":
            _, arr, width, blk = spec
            in_specs.append(pl.BlockSpec((tm, width), functools.partial(lambda i, blk: (i, blk), blk=blk)))
        elif spec[0] == "heads":
            arr = spec[1]
            in_specs.append(pl.BlockSpec((arr.shape[0], tm, LANES), lambda i: (0, i, 0)))
        else:
            arr = spec[1]
            in_specs.append(pl.BlockSpec(arr.shape, functools.partial(lambda i, nd: (0,) * nd, nd=arr.ndim)))
        args.append(arr)
    def out_spec(o):
        if len(o) == 4:
            return pl.BlockSpec((o[2], LANES, tm), lambda i: (0, 0, i)), (o[2], LANES, rows)
        if len(o) == 3:
            return pl.BlockSpec((o[2], tm, LANES), lambda i: (0, i, 0)), (o[2], rows, LANES)
        return pl.BlockSpec((tm, o[0]), lambda i: (i, 0)), (rows, o[0])

    out_specs = [out_spec(o)[0] for o in outs]
    out_shape = [jax.ShapeDtypeStruct(out_spec(o)[1], o[1]) for o in outs]
    out_specs += [pl.BlockSpec((r, w), lambda i: (0, 0)) for r, w in reds]
    out_shape += [jax.ShapeDtypeStruct((r, w), F32) for r, w in reds]
    return pl.pallas_call(body, name=name, grid=(rows // tm,), in_specs=in_specs, out_specs=out_specs,
                          out_shape=out_shape, compiler_params=_params(("arbitrary",)))(*args)


def _rstd(x, n):
    return lax.rsqrt(jnp.sum(x * x, axis=-1, keepdims=True) * (1.0 / n) + RMS_EPS)


def _rms_fwd(x_spec, g, *, name, rows, width):
    def fn(x, gv):
        return (x * _rstd(x, width) * gv,)
    return _rowwise(fn, name=name, rows=rows, ins=[x_spec, ("full", g)], outs=[(width, BF16)])[0]


def _rms_bwd(x_spec, g, dy, res, *, name, rows, width, out_dtypes):
    def fn(x, gv, dyv, *rest):
        r = _rstd(x, width)
        dyg = dyv * gv
        dx = r * dyg - x * (r * r * r) * (jnp.sum(dyg * x, axis=-1, keepdims=True) * (1.0 / width))
        if rest:
            dx = dx + rest[0]
        return (dx,) * len(out_dtypes) + (jnp.sum(dyv * x * r, axis=0, keepdims=True),)

    ins = [x_spec, ("full", g), ("row", dy, width, 0)]
    if res is not None:
        ins.append(("row", res, width, 0))
    return _rowwise(fn, name=name, rows=rows, ins=ins, outs=[(width, dt) for dt in out_dtypes], reds=[(1, width)])


NT_DIMS = (((1,), (1,)), ((), ()))
TN_DIMS = (((0,), (0,)), ((), ()))
BNT_DIMS = (((2,), (2,)), ((0,), (0,)))
BNN_DIMS = (((2,), (1,)), ((0,), (0,)))


def _mla_tile(s):
    return min(512, s)


def _causal(shape, query_axis):
    return lax.broadcasted_iota(jnp.int32, shape, query_axis) >= lax.broadcasted_iota(jnp.int32, shape, 1 - query_axis)


def _mla_fwd(qa, ka, vta, *, name, s, nh):
    t = _mla_tile(s)
    nq = s // t

    def body(q_ref, k_ref, vt_ref, o_ref, lse_ref, m_s, acc_s):
        i, j = pl.program_id(0), pl.program_id(1)

        @pl.when(j == 0)
        def _():
            m_s[...] = jnp.full_like(m_s, NEG)
            acc_s[...] = jnp.zeros_like(acc_s)

        def step(diag):
            sc = lax.dot_general(k_ref[...], q_ref[...], BNT_DIMS, preferred_element_type=F32)
            if diag:
                sc = jnp.where(_causal(sc.shape[1:], 1)[None], sc, NEG)
            m_prev = m_s[...]
            m_new = jnp.maximum(m_prev, jnp.max(sc, axis=1, keepdims=True))
            p = jnp.exp(sc - m_new).astype(BF16)
            acc_s[...] = (jnp.exp(m_prev - m_new) * acc_s[...]
                          + lax.dot_general(vt_ref[...], p, BNN_DIMS, preferred_element_type=F32))
            m_s[...] = m_new

        pl.when(j < i)(functools.partial(step, False))
        pl.when(j == i)(functools.partial(step, True))

        @pl.when(j == nq - 1)
        def _():
            row = lax.broadcasted_iota(jnp.int32, (LANES, t), 0)
            for h in range(nh):
                acc = acc_s[h]
                l = acc[HEAD:HEAD + 1, :]
                o_ref[h] = jnp.where(row < HEAD, acc / l, 0.0).T
                lse_ref[h] = m_s[h] + jnp.log(l)

    q_spec = pl.BlockSpec((nh, t, LANES), lambda i, j: (0, i, 0))
    k_spec = pl.BlockSpec((nh, t, LANES), lambda i, j: (0, jnp.minimum(j, i), 0))
    vt_spec = pl.BlockSpec((nh, LANES, t), lambda i, j: (0, 0, jnp.minimum(j, i)))
    return pl.pallas_call(
        body, name=name, grid=(nq, nq), in_specs=[q_spec, k_spec, vt_spec],
        out_specs=[q_spec, pl.BlockSpec((nh, 1, t), lambda i, j: (0, 0, i))],
        out_shape=[jax.ShapeDtypeStruct((nh, s, LANES), F32), jax.ShapeDtypeStruct((nh, 1, s), F32)],
        scratch_shapes=[pltpu.VMEM((nh, 1, t), F32), pltpu.VMEM((nh, LANES, t), F32)],
        compiler_params=_params(("parallel", "arbitrary")),
    )(qa, ka, vta)


def _mla_bwd(qa, ka, kta, va, o, do, lse_row, after, *, name, s, nh):
    t = _mla_tile(s)
    nq = s // t

    def body(q_ref, k_ref, kt_ref, v_ref, o_ref, do_ref, lse_ref, after_ref, dq_hbm, dk_ref, dv_ref,
             dqt_s, dk_s, dv_s, d_s, stage, sem):
        kb, j = pl.program_id(0), pl.program_id(1)
        cols = pl.ds(pl.multiple_of(j * t, t), t)

        @pl.when(j == 0)
        def _():
            dk_s[...] = jnp.zeros_like(dk_s)
            dv_s[...] = jnp.zeros_like(dv_s)

        @pl.when(kb == 0)
        def _():
            dqt_s[:, :, cols] = jnp.zeros((nh, LANES, t), F32)
            for h in range(nh):
                d_col = jnp.sum(do_ref[h] * o_ref[h], axis=1, keepdims=True)
                d_s[h, :, cols] = jnp.broadcast_to(d_col, (t, LANES)).T[0:1, :]

        def step(diag):
            q = q_ref[...]
            do_b = do_ref[...].astype(BF16)
            sc = lax.dot_general(k_ref[...], q, BNT_DIMS, preferred_element_type=F32)
            if diag:
                sc = jnp.where(_causal(sc.shape[1:], 1)[None], sc, NEG)
            p = jnp.exp(sc - lse_ref[...])
            dv_s[...] += lax.dot_general(p.astype(BF16), do_b, BNN_DIMS, preferred_element_type=F32)
            dp = lax.dot_general(v_ref[...], do_b, BNT_DIMS, preferred_element_type=F32)
            ds = (p * (dp - d_s[:, :, cols])).astype(BF16)
            dk_s[...] += lax.dot_general(ds, q, BNN_DIMS, preferred_element_type=F32)
            dqt_s[:, :, cols] += lax.dot_general(kt_ref[...], ds, BNN_DIMS, preferred_element_type=F32)

        pl.when(j > kb)(functools.partial(step, False))
        pl.when(j == kb)(functools.partial(step, True))

        @pl.when(j == kb)
        def _():
            for h in range(nh):
                stage[h] = dqt_s[h, :, cols].T
            out = pltpu.make_async_copy(stage, dq_hbm.at[:, cols, :], sem)
            out.start()
            out.wait()

        @pl.when(j == nq - 1)
        def _():
            dk_ref[...] = dk_s[...]
            dv_ref[...] = dv_s[...].astype(dv_ref.dtype)

    q_spec = pl.BlockSpec((nh, t, LANES), lambda kb, j: (0, jnp.maximum(j, kb), 0))
    kv_spec = pl.BlockSpec((nh, t, LANES), lambda kb, j: (0, kb, 0))
    kt_spec = pl.BlockSpec((nh, LANES, t), lambda kb, j: (0, 0, kb))
    row_spec = pl.BlockSpec((nh, 1, t), lambda kb, j: (0, 0, jnp.maximum(j, kb)))
    whole = jax.ShapeDtypeStruct((nh, s, LANES), F32)
    return pl.pallas_call(
        body, name=name, grid=(nq, nq),
        in_specs=[q_spec, kv_spec, kt_spec, kv_spec, q_spec, q_spec, row_spec, pl.BlockSpec(memory_space=pl.ANY)],
        out_specs=[pl.BlockSpec(memory_space=pl.ANY), kv_spec, kv_spec],
        out_shape=[whole, whole, jax.ShapeDtypeStruct((nh, s, LANES), BF16)],
        scratch_shapes=[pltpu.VMEM((nh, LANES, s), F32), pltpu.VMEM((nh, t, LANES), F32), pltpu.VMEM((nh, t, LANES), F32),
                        pltpu.VMEM((nh, 1, s), F32), pltpu.VMEM((nh, t, LANES), F32), pltpu.SemaphoreType.DMA],
        compiler_params=_params(("arbitrary", "arbitrary")),
    )(qa, ka, kta, va, o, do, lse_row, after)


SWA_PIECE = 128
SWA_KEYS = 2 * SWA_PIECE


def _swa_block(s):
    return min(512, s)


def _swa_piece(hp_ref, h, q, k_ref, v_ref, qpos0, scale):
    kstart = pl.multiple_of(jnp.maximum(qpos0 - SWA_PIECE, 0), SWA_PIECE)
    k = k_ref[pl.ds(kstart, SWA_KEYS), :].astype(BF16)
    v = v_ref[pl.ds(kstart, SWA_KEYS), :].astype(BF16)
    sc = lax.dot_general(q, k, NT_DIMS, preferred_element_type=F32)
    dist = (qpos0 + lax.broadcasted_iota(jnp.int32, sc.shape, 0)) - (kstart + lax.broadcasted_iota(jnp.int32, sc.shape, 1))
    sc = sc * scale - hp_ref[h, 0] * dist.astype(F32)
    sc = jnp.where((dist >= 0) & (dist < SWA_WINDOW), sc, NEG)
    return kstart, k, v, sc


def _swa_fwd(proj, hp, *, name, s, scale):
    tb = _swa_block(s)
    group = SWA_HEADS // SWA_KV_HEADS
    q_off, k_off, v_off = C_QSW[0] // LANES, C_KSW[0] // LANES, C_VSW[0] // LANES

    def body(hp_ref, q_ref, k_ref, v_ref, o_ref, lse_ref):
        h, i = pl.program_id(0), pl.program_id(1)
        sink = hp_ref[h, 1]
        for r in range(0, tb, SWA_PIECE):
            rows = pl.ds(r, SWA_PIECE)
            _, _, v, sc = _swa_piece(hp_ref, h, q_ref[rows, :].astype(BF16), k_ref, v_ref, i * tb + r, scale)
            m = jnp.maximum(jnp.max(sc, axis=1, keepdims=True), sink)
            p = jnp.exp(sc - m)
            l = jnp.sum(p, axis=1, keepdims=True) + jnp.exp(sink - m)
            o_ref[rows, :] = jnp.dot(p.astype(BF16), v, preferred_element_type=F32) / l
            lse_ref[rows, :] = m + jnp.log(l)

    whole = lambda off: pl.BlockSpec((s, LANES), lambda h, i: (0, off + h // group))
    return pl.pallas_call(
        body, name=name, grid=(SWA_HEADS, s // tb),
        in_specs=[pl.BlockSpec(memory_space=pltpu.SMEM), pl.BlockSpec((tb, LANES), lambda h, i: (i, q_off + h)),
                  whole(k_off), whole(v_off)],
        out_specs=[pl.BlockSpec((tb, LANES), lambda h, i: (i, h)), pl.BlockSpec((None, tb, 1), lambda h, i: (h, i, 0))],
        out_shape=[jax.ShapeDtypeStruct((s, SWA_HEADS * LANES), F32), jax.ShapeDtypeStruct((SWA_HEADS, s, 1), F32)],
        compiler_params=_params(("parallel", "parallel")),
    )(hp, proj, proj, proj)


def _swa_bwd(proj, o, do, lse, hp, *, name, s, scale):
    tb = _swa_block(s)
    nqb = s // tb
    group = SWA_HEADS // SWA_KV_HEADS
    q_off, k_off, v_off = C_QSW[0] // LANES, C_KSW[0] // LANES, C_VSW[0] // LANES

    def body(hp_ref, q_ref, k_ref, v_ref, o_ref, do_ref, lse_ref, dq_ref, dk_ref, dv_ref, dsink_ref):
        kh, g, i = pl.program_id(0), pl.program_id(1), pl.program_id(2)
        h = kh * group + g
        sink = hp_ref[h, 1]

        @pl.when((g == 0) & (i == 0))
        def _():
            dk_ref[...] = jnp.zeros_like(dk_ref)
            dv_ref[...] = jnp.zeros_like(dv_ref)

        @pl.when(i == 0)
        def _():
            dsink_ref[...] = jnp.zeros_like(dsink_ref)

        for r in range(0, tb, SWA_PIECE):
            rows = pl.ds(r, SWA_PIECE)
            q = q_ref[rows, :].astype(BF16)
            dov = do_ref[rows, :]
            do_b = dov.astype(BF16)
            lse_r = lse_ref[rows, :]
            d_r = jnp.sum(dov * o_ref[rows, :], axis=1, keepdims=True)
            kstart, k, v, sc = _swa_piece(hp_ref, h, q, k_ref, v_ref, i * tb + r, scale)
            p = jnp.exp(sc - lse_r)
            dp = lax.dot_general(do_b, v, NT_DIMS, preferred_element_type=F32)
            ds = (p * (dp - d_r)).astype(BF16)
            dq_ref[rows, :] = (jnp.dot(ds, k, preferred_element_type=F32) * scale).astype(dq_ref.dtype)
            win = pl.ds(kstart, SWA_KEYS)
            dk_ref[win, :] += lax.dot_general(ds, q, TN_DIMS, preferred_element_type=F32) * scale
            dv_ref[win, :] += lax.dot_general(p.astype(BF16), do_b, TN_DIMS, preferred_element_type=F32)
            part = jnp.sum(-jnp.exp(sink - lse_r) * d_r, axis=0, keepdims=True)
            dsink_ref[...] += jnp.broadcast_to(part, (1, LANES))

    whole = lambda off: pl.BlockSpec((s, LANES), lambda kh, g, i: (0, off + kh))
    q_map = lambda kh, g, i: (i, kh * group + g)
    return pl.pallas_call(
        body, name=name, grid=(SWA_KV_HEADS, group, nqb),
        in_specs=[pl.BlockSpec(memory_space=pltpu.SMEM),
                  pl.BlockSpec((tb, LANES), lambda kh, g, i: (i, q_off + kh * group + g)), whole(k_off), whole(v_off),
                  pl.BlockSpec((tb, LANES), q_map), pl.BlockSpec((tb, LANES), q_map),
                  pl.BlockSpec((None, tb, 1), lambda kh, g, i: (kh * group + g, i, 0))],
        out_specs=[pl.BlockSpec((tb, LANES), q_map), whole(0), whole(0),
                   pl.BlockSpec((None, 1, LANES), lambda kh, g, i: (kh * group + g, 0, 0))],
        out_shape=[jax.ShapeDtypeStruct((s, SWA_HEADS * LANES), BF16),
                   jax.ShapeDtypeStruct((s, SWA_KV_HEADS * LANES), F32), jax.ShapeDtypeStruct((s, SWA_KV_HEADS * LANES), F32),
                   jax.ShapeDtypeStruct((SWA_HEADS, 1, LANES), F32)],
        compiler_params=_params(("parallel", "arbitrary", "arbitrary")),
    )(hp, proj, proj, proj, o, do, lse)


def _shift_down(z, k):
    rows = lax.broadcasted_iota(jnp.int32, z.shape, 0)
    return jnp.where(rows >= k, pltpu.roll(z, k, 0), 0.0)


def _shift_up(z, k):
    n = z.shape[0]
    rows = lax.broadcasted_iota(jnp.int32, z.shape, 0)
    return jnp.where(rows < n - k, pltpu.roll(z, n - k, 0), 0.0)


def _rows3(a, b, c):
    r = lax.broadcasted_iota(jnp.int32, (3, a.shape[1]), 0)
    return jnp.where(r == 0, a, jnp.where(r == 1, b, c))


def _col_spec(s, off):
    return pl.BlockSpec((s, LANES), functools.partial(lambda j, off: (0, off + j), off=off))


def _conv_fwd(proj, conv_w, *, name, s):
    def body(gb_ref, gc_ref, u_ref, w_ref, y_ref):
        w0, w1, w2 = w_ref[0:1, :], w_ref[1:2, :], w_ref[2:3, :]
        z = gc_ref[...] * u_ref[...]
        c = w2 * z + w1 * _shift_down(z, 1) + w0 * _shift_down(z, 2)
        y_ref[...] = gb_ref[...] * c

    return pl.pallas_call(
        body, name=name, grid=(2,),
        in_specs=[_col_spec(s, C_GB[0] // LANES), _col_spec(s, C_GC[0] // LANES), _col_spec(s, C_UCONV[0] // LANES),
                  pl.BlockSpec((3, LANES), lambda j: (0, j))],
        out_specs=_col_spec(s, 0), out_shape=jax.ShapeDtypeStruct((s, D_GROUP), F32),
        compiler_params=_params(("parallel",)),
    )(proj, proj, proj, conv_w)


def _conv_bwd(dy, proj, conv_w, *, name, s):
    def body(dy_ref, gb_ref, gc_ref, u_ref, w_ref, dgb_ref, dgc_ref, du_ref, dw_ref):
        w0, w1, w2 = w_ref[0:1, :], w_ref[1:2, :], w_ref[2:3, :]
        gc, u, dyv = gc_ref[...], u_ref[...], dy_ref[...]
        z = gc * u
        z1, z2 = _shift_down(z, 1), _shift_down(z, 2)
        c = w2 * z + w1 * z1 + w0 * z2
        dgb_ref[...] = (dyv * c).astype(dgb_ref.dtype)
        dc = dyv * gb_ref[...]
        dz = w2 * dc + w1 * _shift_up(dc, 1) + w0 * _shift_up(dc, 2)
        dgc_ref[...] = (dz * u).astype(dgc_ref.dtype)
        du_ref[...] = (dz * gc).astype(du_ref.dtype)
        dw_ref[...] = _rows3(jnp.sum(dc * z2, axis=0, keepdims=True), jnp.sum(dc * z1, axis=0, keepdims=True),
                             jnp.sum(dc * z, axis=0, keepdims=True))

    act = jax.ShapeDtypeStruct((s, D_GROUP), BF16)
    return pl.pallas_call(
        body, name=name, grid=(2,),
        in_specs=[_col_spec(s, 0), _col_spec(s, C_GB[0] // LANES), _col_spec(s, C_GC[0] // LANES),
                  _col_spec(s, C_UCONV[0] // LANES), pl.BlockSpec((3, LANES), lambda j: (0, j))],
        out_specs=[_col_spec(s, 0), _col_spec(s, 0), _col_spec(s, 0), pl.BlockSpec((3, LANES), lambda j: (0, j))],
        out_shape=[act, act, act, jax.ShapeDtypeStruct((3, D_GROUP), F32)],
        compiler_params=_params(("parallel",)),
    )(dy, proj, proj, proj, conv_w)


def _pool_select(j, lane, a2, a4, a8, a16):
    lo = lane < HEAD
    return jnp.where(j == 0, jnp.where(lo, a2, a4), jnp.where(lo, a8, a16))


def _pooled(u, j):
    s2 = u + _shift_down(u, 1)
    s4 = s2 + _shift_down(s2, 2)
    s8 = s4 + _shift_down(s4, 4)
    s16 = s8 + _shift_down(s8, 8)
    lane = lax.broadcasted_iota(jnp.int32, u.shape, 1)
    rows = lax.broadcasted_iota(jnp.int32, u.shape, 0)
    win = _pool_select(j, lane, 2, 4, 8, 16)
    count = jnp.minimum(rows + 1, win).astype(F32)
    return _pool_select(j, lane, s2, s4, s8, s16) / count - u, count


def _pool_fwd(proj, wbd, scale, *, name, s):
    def body(u_ref, w_ref, sc_ref, y_ref):
        pooled, _ = _pooled(u_ref[...], pl.program_id(0))
        y_ref[...] = jnp.dot(pooled.astype(BF16), w_ref[...].astype(BF16), preferred_element_type=F32) * sc_ref[...]

    return pl.pallas_call(
        body, name=name, grid=(2,),
        in_specs=[_col_spec(s, C_UPOOL[0] // LANES), pl.BlockSpec((None, LANES, LANES), lambda j: (j, 0, 0)),
                  pl.BlockSpec((1, LANES), lambda j: (0, j))],
        out_specs=_col_spec(s, 0), out_shape=jax.ShapeDtypeStruct((s, D_GROUP), F32),
        compiler_params=_params(("parallel",)),
    )(proj, wbd, scale)


def _pool_bwd(dy, proj, wbd, scale, *, name, s):
    def body(dy_ref, u_ref, w_ref, sc_ref, du_ref, dw_ref, dsc_ref):
        j = pl.program_id(0)
        pooled, count = _pooled(u_ref[...], j)
        pooled_b = pooled.astype(BF16)
        w_b = w_ref[...].astype(BF16)
        dyv = dy_ref[...]
        mixed = jnp.dot(pooled_b, w_b, preferred_element_type=F32)
        dsc_ref[...] = jnp.sum(dyv * mixed, axis=0, keepdims=True)
        dms = (dyv * sc_ref[...]).astype(BF16)
        dw_ref[...] = lax.dot_general(pooled_b, dms, (((0,), (0,)), ((), ())), preferred_element_type=F32)
        dpooled = lax.dot_general(dms, w_b, (((1,), (1,)), ((), ())), preferred_element_type=F32)
        r = dpooled / count
        a2 = r + _shift_up(r, 1)
        a4 = a2 + _shift_up(a2, 2)
        a8 = a4 + _shift_up(a4, 4)
        a16 = a8 + _shift_up(a8, 8)
        lane = lax.broadcasted_iota(jnp.int32, r.shape, 1)
        du_ref[...] = (_pool_select(j, lane, a2, a4, a8, a16) - dpooled).astype(du_ref.dtype)

    return pl.pallas_call(
        body, name=name, grid=(2,),
        in_specs=[_col_spec(s, 0), _col_spec(s, C_UPOOL[0] // LANES),
                  pl.BlockSpec((None, LANES, LANES), lambda j: (j, 0, 0)), pl.BlockSpec((1, LANES), lambda j: (0, j))],
        out_specs=[_col_spec(s, 0), pl.BlockSpec((None, LANES, LANES), lambda j: (j, 0, 0)),
                   pl.BlockSpec((1, LANES), lambda j: (0, j))],
        out_shape=[jax.ShapeDtypeStruct((s, D_GROUP), BF16), jax.ShapeDtypeStruct((2, LANES, LANES), F32),
                   jax.ShapeDtypeStruct((1, D_GROUP), F32)],
        compiler_params=_params(("parallel",)),
    )(dy, proj, wbd, scale)


def _mesh_pos():
    return lax.axis_index("x"), lax.axis_index("y"), lax.axis_index("c")


def _any_specs(n):
    return [pl.BlockSpec(memory_space=pl.ANY)] * n


def _all_gather(xs, *, name):
    n = len(xs)

    def body(*refs):
        x_refs, out_refs = refs[:n], refs[n:2 * n]
        send_sems, recv_sems, local_sems = refs[2 * n:]
        x, y, cc = _mesh_pos()
        me, sibling = (x, y, cc), (x, y, 1 - cc)
        chips = [(1 - x, y), (x, 1 - y), (1 - x, 1 - y)]

        def slot(a, px, py, pc):
            return out_refs[a].at[4 * px + 2 * py + pc]

        def copy(a, k, block, to, src=None):
            return pltpu.make_async_remote_copy(
                src_ref=slot(a, *block) if src is None else src, dst_ref=slot(a, *block), send_sem=send_sems.at[a, k],
                recv_sem=recv_sems.at[a, k], device_id=to, device_id_type=pl.DeviceIdType.MESH)

        mine = [pltpu.make_async_copy(x_refs[a], slot(a, *me), local_sems.at[a]) for a in range(n)]
        first = []
        for a in range(n):
            first.append(copy(a, 0, me, sibling, src=x_refs[a]))
            first += [copy(a, 1 + j, me, (*chip, cc), src=x_refs[a]) for j, chip in enumerate(chips)]
        for cp in mine + first:
            cp.start()
        passed = []
        for j, chip in enumerate(chips):
            for a in range(n):
                copy(a, 1 + j, (*chip, cc), me).wait_recv()
                passed.append(copy(a, 4 + j, (*chip, cc), sibling))
                passed[-1].start()
        for a in range(n):
            copy(a, 0, sibling, me).wait_recv()
        for j, chip in enumerate(chips):
            for a in range(n):
                copy(a, 4 + j, (*chip, 1 - cc), me).wait_recv()
        for cp in first + passed:
            cp.wait_send()
        for cp in mine:
            cp.wait()

    return pl.pallas_call(
        body, name=name, out_shape=[jax.ShapeDtypeStruct((N_DEV,) + a.shape, a.dtype) for a in xs],
        in_specs=_any_specs(n), out_specs=_any_specs(n),
        scratch_shapes=[pltpu.SemaphoreType.DMA((n, 7)), pltpu.SemaphoreType.DMA((n, 7)), pltpu.SemaphoreType.DMA((n,))],
    )(*xs)


def _plan_gather_near(src_refs, land_refs, send_sems, recv_sems):
    x, y, cc = _mesh_pos()
    me = 4 * x + 2 * y + cc
    plan = []
    for a, (src, land) in enumerate(zip(src_refs, land_refs)):
        for k, (px, py, pc) in enumerate([(x, y, 1 - cc), (1 - x, y, cc), (x, 1 - y, cc), (1 - x, 1 - y, cc)]):
            sems = dict(send_sem=send_sems.at[4 * a + k], recv_sem=recv_sems.at[4 * a + k], device_id=(px, py, pc),
                        device_id_type=pl.DeviceIdType.MESH)
            plan.append((pltpu.make_async_remote_copy(src_ref=src, dst_ref=land.at[me], **sems),
                         pltpu.make_async_remote_copy(src_ref=src, dst_ref=land.at[4 * px + 2 * py + pc], **sems)))
    return plan


def _plan_gather_pass(src_refs, land_refs, send_sems, recv_sems):
    x, y, cc = _mesh_pos()
    plan = []
    for a, land in enumerate(land_refs):
        for j, (px, py) in enumerate([(1 - x, y), (x, 1 - y), (1 - x, 1 - y)]):
            mine, theirs = land.at[4 * px + 2 * py + cc], land.at[4 * px + 2 * py + 1 - cc]
            sems = dict(send_sem=send_sems.at[3 * a + j], recv_sem=recv_sems.at[3 * a + j], device_id=(x, y, 1 - cc),
                        device_id_type=pl.DeviceIdType.MESH)
            plan.append((pltpu.make_async_remote_copy(src_ref=mine, dst_ref=mine, **sems),
                         pltpu.make_async_remote_copy(src_ref=mine, dst_ref=theirs, **sems)))
    return plan


def _plan_sibling(src_refs, land_refs, send_sems, recv_sems):
    x, y, cc = _mesh_pos()
    plan = []
    for a, (src, land) in enumerate(zip(src_refs, land_refs)):
        cp = pltpu.make_async_remote_copy(
            src_ref=src.at[1 - cc], dst_ref=land, send_sem=send_sems.at[a], recv_sem=recv_sems.at[a],
            device_id=(x, y, 1 - cc), device_id_type=pl.DeviceIdType.MESH)
        plan.append((cp, cp))
    return plan


def _plan_chips(src_refs, land_refs, send_sems, recv_sems):
    x, y, cc = _mesh_pos()
    my_chip = 2 * x + y
    plan = []
    for a, (src, land) in enumerate(zip(src_refs, land_refs)):
        for j, (px, py) in enumerate([(1 - x, y), (x, 1 - y), (1 - x, 1 - y)]):
            peer = 2 * px + py
            sems = dict(send_sem=send_sems.at[3 * a + j], recv_sem=recv_sems.at[3 * a + j], device_id=(px, py, cc),
                        device_id_type=pl.DeviceIdType.MESH)
            plan.append((pltpu.make_async_remote_copy(src_ref=src.at[peer], dst_ref=land.at[my_chip], **sems),
                         pltpu.make_async_remote_copy(src_ref=src.at[peer], dst_ref=land.at[peer], **sems)))
    return plan


HBM_SPEC = pl.BlockSpec(memory_space=pltpu.HBM)
SEM_SPEC = pl.BlockSpec(memory_space=pltpu.SEMAPHORE)
ANY_SPEC = pl.BlockSpec(memory_space=pl.ANY)
SIDE_EFFECT = pltpu.CompilerParams(has_side_effects=pltpu.SideEffectType.DATAFLOW_SIDE_EFFECTING)


def _start_copies(plan, sems_per_array, srcs, lands, after, *, name):
    lands = [lax.empty(l, a.dtype) if isinstance(l, tuple) else l for l, a in zip(lands, srcs or lands)]
    ns, n = len(srcs), len(srcs) + len(lands)

    def body(*refs):
        send_sems, recv_sems = refs[n + len(after)], refs[n + len(after) + 1]
        for out, _ in plan(refs[:ns], refs[ns:n], send_sems, recv_sems):
            out.start()
        refs[-1][...] = jnp.zeros_like(refs[-1])

    sem = pltpu.SemaphoreType.DMA((len(lands) * sems_per_array,))
    res = pl.pallas_call(
        body, name=name,
        out_shape=(sem, sem, *[pltpu.HBM(a.shape, a.dtype) for a in srcs + lands], jax.ShapeDtypeStruct((8, LANES), F32)),
        in_specs=[HBM_SPEC] * n + [ANY_SPEC] * len(after),
        out_specs=(SEM_SPEC, SEM_SPEC, *[HBM_SPEC] * n, pl.BlockSpec(memory_space=pltpu.VMEM)),
        input_output_aliases={i: 2 + i for i in range(n)}, compiler_params=SIDE_EFFECT,
    )(*[pltpu.with_memory_space_constraint(a, pltpu.HBM) for a in srcs + lands], *after)
    return (res[0], res[1], list(res[2:2 + ns]), list(res[2 + ns:2 + n])), res[-1]


def _wait_copies(plan, handle, after, *, name):
    send, recv, srcs, lands = handle
    ns, n = len(srcs), len(srcs) + len(lands)

    def body(*refs):
        for out, inc in plan(refs[:ns], refs[ns:n], refs[n], refs[n + 1]):
            out.wait_send()
            inc.wait_recv()

    res = pl.pallas_call(
        body, name=name, out_shape=tuple(pltpu.HBM(a.shape, a.dtype) for a in srcs + lands),
        in_specs=[HBM_SPEC] * n + [SEM_SPEC, SEM_SPEC] + [ANY_SPEC] * len(after), out_specs=[HBM_SPEC] * n,
        input_output_aliases={i: i for i in range(n)}, compiler_params=SIDE_EFFECT,
    )(*srcs, *lands, send, recv, *after)
    return list(res[:ns]), list(res[ns:])


def _row_tile(rows, target=512):
    if rows <= target:
        return rows
    best = None
    for t in range(8, target + 1, 8):
        if rows % t == 0:
            best = t
    assert best is not None, (rows, target)
    return best


def _add_own(g, other, core, *, name):
    _, _, rows, cols = g.shape
    tm = _row_tile(rows)

    def body(c_ref, g_ref, o_ref, out_ref):
        out_ref[...] = g_ref[...] + o_ref[...]

    return pl.pallas_call(
        body, name=name, out_shape=jax.ShapeDtypeStruct(other.shape, other.dtype),
        grid_spec=pltpu.PrefetchScalarGridSpec(
            num_scalar_prefetch=1, grid=(4, rows // tm),
            in_specs=[pl.BlockSpec((None, None, tm, cols), lambda p, i, c_ref: (c_ref[0], p, i, 0)),
                      pl.BlockSpec((None, tm, cols), lambda p, i, c_ref: (p, i, 0))],
            out_specs=pl.BlockSpec((None, tm, cols), lambda p, i, c_ref: (p, i, 0))),
        compiler_params=_params(("parallel", "parallel")),
    )(core, g, other)


def _adamw(parts, w, m, v, after, *, name):
    layers, rows, cols = w.shape
    assert len(parts) == layers
    tm = _row_tile(rows, 256)
    nr = rows // tm

    def body(*refs):
        p_refs = refs[:layers]
        w_ref, m_ref, v_ref, _, g_ref, d_ref, nm_ref, nv_ref, g_s = refs[layers:]
        for ll in range(layers):
            @pl.when(pl.program_id(0) == ll)
            def _(ll=ll):
                g = p_refs[ll][0]
                for q in range(1, p_refs[ll].shape[0]):
                    g = g + p_refs[ll][q]
                g_s[...] = g

        g = g_s[...]
        mm = ADAM_B1 * m_ref[...] + (1.0 - ADAM_B1) * g
        vv = ADAM_B2 * v_ref[...] + (1.0 - ADAM_B2) * jnp.square(g)
        m_hat = mm / (1.0 - ADAM_B1 ** ADAM_STEP)
        v_hat = vv / (1.0 - ADAM_B2 ** ADAM_STEP)
        g_ref[...] = g
        d_ref[...] = -ADAM_LR * (m_hat / (jnp.sqrt(v_hat) + ADAM_EPS) + ADAM_WD * w_ref[...])
        nm_ref[...] = mm
        nv_ref[...] = vv

    def part_spec(ll, p):
        return pl.BlockSpec((p, tm, cols), lambda l, i: (0, jnp.where(l == ll, i, jnp.where(l < ll, 0, nr - 1)), 0))

    spec = pl.BlockSpec((None, tm, cols), lambda l, i: (l, i, 0))
    out = jax.ShapeDtypeStruct(w.shape, F32)
    return pl.pallas_call(
        body, name=name, grid=(layers, nr),
        in_specs=[part_spec(ll, parts[ll].shape[0]) for ll in range(layers)] + [spec] * 3 + [pl.BlockSpec(memory_space=pl.ANY)],
        out_specs=[spec] * 4, out_shape=[out] * 4, scratch_shapes=[pltpu.VMEM((tm, cols), F32)],
        compiler_params=_params(("arbitrary", "arbitrary")),
    )(*parts, w, m, v, after)


def _pack(arrs):
    flat = jnp.concatenate([a.reshape(-1) for a in arrs])
    rows = -(-flat.shape[0] // (PACK_COLS * 16)) * 16
    return jnp.pad(flat, (0, rows * PACK_COLS - flat.shape[0])).reshape(rows, PACK_COLS)


def _unpack(packed, shapes):
    flat = packed.reshape(-1)
    out, off = [], 0
    for shp in shapes:
        n = int(np.prod(shp))
        out.append(flat[off:off + n].reshape(shp))
        off += n
    return out


def _shards_to_full(g, axis):
    if axis == 0:
        return g.reshape(g.shape[0] * g.shape[1], g.shape[2])
    return jnp.transpose(g, (1, 0, 2)).reshape(g.shape[1], g.shape[0] * g.shape[2])


def _full_to_shards(a, axis):
    if axis == 0:
        return jnp.transpose(a.reshape(4, 2, a.shape[0] // N_DEV, a.shape[1]), (1, 0, 2, 3))
    return jnp.transpose(a.reshape(a.shape[0], 4, 2, a.shape[1] // N_DEV), (2, 1, 0, 3))


def _zeros_like_cols(a, n):
    return jnp.zeros(a.shape[:-1] + (n,), a.dtype)


def _pad_heads(a, n):
    z = _zeros_like_cols(a, HEAD)
    return jnp.concatenate([p for h in range(n) for p in (a[..., h * HEAD:(h + 1) * HEAD], z)], axis=-1)


def _unpad_heads(a, n):
    return jnp.concatenate([a[..., h * LANES:h * LANES + HEAD] for h in range(n)], axis=-1)


def _seg(first, width, sign=1):
    return (width, [(first, sign)])


def _zero(width):
    return (width, [])


def _swapped(first):
    half = MLA_ROPE // 2
    return [_seg(first + half, half, -1), _seg(first, half)]


def _padded_heads(first, n):
    return [s for h in range(n) for s in (_seg(first + HEAD * h, HEAD), _zero(HEAD))]


def _layout_w_in():
    kr = 384
    return (_padded_heads(1440, 4) + [_seg(0, 256), _seg(416, 256), _seg(672, 256), _seg(928, 256), _seg(1184, 256)]
            + _padded_heads(1696, 2) + _padded_heads(1824, 2) + [_seg(256, 128)]
            + [_zero(HEAD), _seg(kr, MLA_ROPE), _seg(kr, MLA_ROPE)] + [_zero(HEAD)] + _swapped(kr) + _swapped(kr))


def _layout_w_uq():
    out = []
    for h in range(MLA_HEADS):
        out += [_seg(96 * h, MLA_NOPE), _seg(96 * h + MLA_NOPE, MLA_ROPE)] + _swapped(96 * h + MLA_NOPE)
    return out


def _layout_w_ukv():
    keys = [s for h in range(MLA_HEADS) for s in (_seg(LANES * h, HEAD), _zero(HEAD))]
    values = [s for h in range(MLA_HEADS) for s in (_seg(LANES * h + HEAD, HEAD), _zero(HEAD))]
    return keys + values


def _layout_w_gate_up():
    return [_seg(half + j, GU_TILE) for j in range(0, D_FF, GU_TILE) for half in (0, D_FF)]


LAYOUTS = dict(w_in=_layout_w_in(), w_uq=_layout_w_uq(), w_ukv=_layout_w_ukv(), w_gate_up=_layout_w_gate_up())
OWN_COLS = dict(w_in=1952, w_uq=384, w_ukv=512, w_gate_up=2 * D_FF)


def _plan_extend(layout, shard):
    plan = []
    for width, terms in layout:
        if not terms:
            plan.append((width, []))
            continue
        (first, sign), = terms
        while width:
            g, off = divmod(first, shard)
            w = min(width, shard - off)
            plan.append((w, [(g, off, sign)]))
            first, width = first + w, width - w
    return [plan]


def _plan_fold(layout, own_cols):
    sources = [[] for _ in range(own_cols)]
    e = 0
    for width, terms in layout:
        for first, sign in terms:
            for i in range(width):
                sources[first + i].append((e + i, sign))
        e += width
    shard = own_cols // N_DEV
    plans = {}
    for g in range(N_DEV):
        plan, n = [], g * shard
        while n < (g + 1) * shard:
            w = 1
            while n + w < (g + 1) * shard and [(c + w, sg) for c, sg in sources[n]] == sources[n + w]:
                w += 1
            plan.append((w, [(0, c, sg) for c, sg in sources[n]]))
            n += w
        plans[g] = plan
    return [plans[2 * p + c] for c in range(2) for p in range(4)]


def _assemble(src, plans, out_cols, out_dtype, *, name):
    g, rows, c = src.shape
    tm = _row_tile(rows, 256)

    def body(s_ref, o_ref):
        blocks = [s_ref[i].astype(F32) for i in range(g)]
        for d, plan in enumerate(plans):
            pieces = []
            for width, terms in plan:
                v = None
                for b, first, sign in terms:
                    t = blocks[b][:, first:first + width]
                    t = -t if sign < 0 else t
                    v = t if v is None else v + t
                pieces.append(jnp.zeros((tm, width), F32) if v is None else v)
            o_ref[d] = (pieces[0] if len(pieces) == 1 else jnp.concatenate(pieces, axis=1)).astype(o_ref.dtype)

    return pl.pallas_call(
        body, name=name, grid=(rows // tm,), in_specs=[pl.BlockSpec((g, tm, c), lambda i: (0, i, 0))],
        out_specs=pl.BlockSpec((len(plans), tm, out_cols), lambda i: (0, i, 0)),
        out_shape=jax.ShapeDtypeStruct((len(plans), rows, out_cols), out_dtype), compiler_params=_params(("parallel",)),
    )(src)


def _extend(nm, gathered, *, name):
    layout = LAYOUTS[nm]
    return _assemble(gathered, _plan_extend(layout, OWN_COLS[nm] // N_DEV), sum(w for w, _ in layout), BF16, name=name)[0]


def _fold_to_shards(nm, grad_ext, *, name):
    shards = _assemble(grad_ext[None], _plan_fold(LAYOUTS[nm], OWN_COLS[nm]), OWN_COLS[nm] // N_DEV, F32, name=name)
    return shards.reshape((2, 4) + shards.shape[1:])


def _rope_tables(s):
    inv = 1.0 / (ROPE_THETA ** (jnp.arange(0, MLA_ROPE, 2, dtype=F32) / MLA_ROPE))
    ang = jnp.arange(s, dtype=F32)[:, None] * inv[None, :]
    cos, sin = jnp.cos(ang), jnp.sin(ang)
    c32, s32 = jnp.concatenate([cos, cos], axis=1), jnp.concatenate([sin, sin], axis=1)
    zeros, ones = jnp.zeros((s, HEAD), F32), jnp.ones((s, HEAD), F32)
    tq = jnp.concatenate([ones, c32, s32], axis=1) * (1.0 / math.sqrt(MLA_NOPE + MLA_ROPE))
    return (jnp.tile(tq, (1, MLA_HEADS)), jnp.concatenate([zeros, c32, c32], axis=1),
            jnp.concatenate([zeros, s32, s32], axis=1))


def _gn(y):
    return y * _rstd(y, D_GROUP)


def _mixer_fwd(x, w, tabs, l, after_attention=None):
    s = x.shape[0]
    tq, tkc, tks = tabs
    n = lambda t: f"l{l}_{t}"
    h = _rms_fwd(("row", x, D_MODEL, 0), w["attn_norm"], name=n("attn_norm"), rows=s, width=D_MODEL)
    proj = _mm(h, w["w_in"], name=n("in_proj"))
    def prep(cq, ckv, kr, krp, gq, gkv, wuq, wukv, tqv, tc, ts):
        cqn = (cq * _rstd(cq, 256) * gq).astype(BF16)
        ckvn = (ckv * _rstd(ckv, 128) * gkv).astype(BF16)
        qe = jnp.dot(cqn, wuq, preferred_element_type=F32)
        kve = jnp.dot(ckvn, wukv, preferred_element_type=F32)
        kb = kr * tc + krp * ts
        kvv = kve[:, 512:]
        lane = lax.broadcasted_iota(jnp.int32, kvv.shape, 1) & (LANES - 1)
        v = jnp.where(lane == HEAD, 1.0, kvv)
        k = kve[:, :512] + jnp.tile(kb, (1, MLA_HEADS))
        return cqn, ckvn, qe * tqv, k, k, v, v

    cqn, ckvn, qm, km, kmt, vm, vmt = _rowwise(
        prep, name=n("mla_prep"), rows=s,
        ins=[("row", proj, 256, C_CQ[0] // 256), ("row", proj, 128, C_CKV[0] // 128),
             ("row", proj, 128, C_KR[0] // 128), ("row", proj, 128, C_KRP[0] // 128),
             ("full", w["mla_q_norm"]), ("full", w["mla_kv_norm"]), ("full", w["w_uq"]), ("full", w["w_ukv"]),
             ("row", tq, 512, 0), ("row", tkc, 128, 0), ("row", tks, 128, 0)],
        outs=[(256, BF16), (128, BF16), (512, BF16, MLA_HEADS), (512, BF16, MLA_HEADS), (512, BF16, MLA_HEADS, "T"),
              (512, BF16, MLA_HEADS), (512, BF16, MLA_HEADS, "T")])
    y_a, lse_row = _mla_fwd(qm, km, vmt, name=n("mla_fwd"), s=s, nh=MLA_HEADS)
    mix_norm = w["mix_norm"]
    if after_attention is not None:
        mix_norm = mix_norm + after_attention(y_a)[0, 0]
    y_b = _conv_fwd(proj, w["conv_w"], name=n("conv_fwd"), s=s)
    y_c = _pool_fwd(proj, w["pool_wbd"], w["pool_scale"], name=n("pool_fwd"), s=s)
    y_d, lse_d = _swa_fwd(proj, w["hp_swa"], name=n("swa_fwd"), s=s, scale=1.0 / math.sqrt(HEAD))

    def mix(ya, yb, yc, yd, mn):
        return (jnp.concatenate([_gn(_unpad_heads(ya, 4)), _gn(yb), _gn(yc), _gn(_unpad_heads(yd, 4))], axis=1) * mn,)

    mixed = _rowwise(mix, name=n("group_norm"), rows=s,
                     ins=[("heads", y_a), ("row", y_b, 256, 0), ("row", y_c, 256, 0), ("row", y_d, 512, 0),
                          ("full", mix_norm)], outs=[(D_MODEL, BF16)])[0]
    x1 = _mm(mixed, w["w_o"], res=x, name=n("out_proj"))
    saved = dict(x=x, h=h, proj=proj, cqn=cqn, ckvn=ckvn, qm=qm, km=km, kmt=kmt, vm=vm, y_a=y_a, lse_row=lse_row,
                 y_b=y_b, y_c=y_c, y_d=y_d, lse_d=lse_d, mixed=mixed)
    return x1, saved


def _ffn_fwd(x1, w, l):
    s = x1.shape[0]
    n = lambda t: f"l{l}_{t}"
    h2 = _rms_fwd(("row", x1, D_MODEL, 0), w["ffn_norm"], name=n("ffn_norm"), rows=s, width=D_MODEL)

    def swiglu(gu):
        g, u = gu[:, :GU_TILE], gu[:, GU_TILE:]
        return gu, g * jax.nn.sigmoid(g) * u

    gu, act = _mm(h2, w["w_gate_up"], tm=2048, tn=2 * GU_TILE, name=n("gate_up"),
                  epilogue=(swiglu, [], [(2 * D_FF, BF16), (D_FF, BF16)]))
    x2 = _mm(act, w["w_down"], res=x1, tk=D_FF // 2, name=n("down"))
    return x2, dict(x1=x1, h2=h2, gu=gu, act=act)


def _ffn_bwd_down(dx2, sv, w, l):
    n = lambda t: f"l{l}_{t}"

    def swiglu_bwd(da, gu):
        gt, u = gu[:, :GU_TILE].astype(F32), gu[:, GU_TILE:].astype(F32)
        sg = jax.nn.sigmoid(gt)
        return (jnp.concatenate([da * u * sg * (1.0 + gt * (1.0 - sg)), da * gt * sg], axis=1),)

    dgu = _mm(dx2[1], w["w_down"], tb=True, tm=2048, tn=GU_TILE, name=n("d_act"),
              epilogue=(swiglu_bwd, [sv["gu"]], [(2 * D_FF, BF16)]))[0]
    g = dict(w_down=_mm(sv["act"], dx2[1], ta=True, tm=D_FF // 2, name=n("dw_down")))
    return dgu, g


def _ffn_bwd_up(dx2, dgu, sv, w, l):
    s = dgu.shape[0]
    n = lambda t: f"l{l}_{t}"
    dh2 = _mm(dgu, w["w_gate_up"], tb=True, tk=D_FF // 2, name=n("d_h2"))
    g = dict(w_gate_up=_mm(sv["h2"], dgu, ta=True, tn=D_FF // 2, name=n("dw_gate_up")))
    dx1, dx1_b, g["ffn_norm"] = _rms_bwd(("row", sv["x1"], D_MODEL, 0), w["ffn_norm"], dh2, dx2[0],
                                         name=n("ffn_norm_bwd"), rows=s, width=D_MODEL, out_dtypes=(F32, BF16))
    return (dx1, dx1_b), g


def _mixer_bwd_out(dx1, sv, w, l):
    s = dx1[1].shape[0]
    n = lambda t: f"l{l}_{t}"
    dmixed = _mm(dx1[1], w["w_o"], tb=True, name=n("d_mixed"))
    g = dict(w_o=_mm(sv["mixed"], dx1[1], ta=True, name=n("dw_o")))

    def mix_bwd(dm, ya, yb, yc, yd, mn):
        outs, dmn = [], []
        for i, y in enumerate((_unpad_heads(ya, 4), yb, yc, _unpad_heads(yd, 4))):
            lo, hi = i * D_GROUP, (i + 1) * D_GROUP
            r = _rstd(y, D_GROUP)
            nrm = y * r
            dmg = dm[:, lo:hi]
            dn = dmg * mn[:, lo:hi]
            dy = r * (dn - nrm * (jnp.sum(dn * nrm, axis=-1, keepdims=True) * (1.0 / D_GROUP)))
            outs.append(_pad_heads(dy, 4) if i in (0, 3) else dy)
            dmn.append(jnp.sum(dmg * nrm, axis=0, keepdims=True))
        return (*outs, jnp.concatenate(dmn, axis=1))

    dy_a, dy_b, dy_c, dy_d, g["mix_norm"] = _rowwise(
        mix_bwd, name=n("group_norm_bwd"), rows=s,
        ins=[("row", dmixed, D_MODEL, 0), ("heads", sv["y_a"]), ("row", sv["y_b"], 256, 0),
             ("row", sv["y_c"], 256, 0), ("row", sv["y_d"], 512, 0), ("full", w["mix_norm"])],
        outs=[(512, F32, MLA_HEADS), (256, F32), (256, F32), (512, F32)], reds=[(1, D_MODEL)])
    return (dy_a, dy_b, dy_c, dy_d), g


def _mixer_bwd_in(dx1, dys, sv, w, tabs, l):
    s = dx1[0].shape[0]
    tq, tkc, tks = tabs
    n = lambda t: f"l{l}_{t}"
    dy_a, dy_b, dy_c, dy_d = dys
    g = {}

    proj = sv["proj"]
    dq_sw, dk_sw, dv_sw, dsink = _swa_bwd(proj, sv["y_d"], dy_d, sv["lse_d"], w["hp_swa"], name=n("swa_bwd"), s=s,
                                          scale=1.0 / math.sqrt(HEAD))
    g["swa_sinks"] = dsink[:, 0, 0]

    dqm, dkm, dvm = _mla_bwd(sv["qm"], sv["km"], sv["kmt"], sv["vm"], sv["y_a"], dy_a, sv["lse_row"], w["hp_swa"],
                             name=n("mla_bwd"), s=s, nh=MLA_HEADS)

    def rms_bwd(x, gv, dy, width):
        r = _rstd(x, width)
        dyg = dy * gv
        dx = r * dyg - x * (r * r * r) * (jnp.sum(dyg * x, axis=-1, keepdims=True) * (1.0 / width))
        return dx, jnp.sum(dy * x * r, axis=0, keepdims=True)

    def prep_bwd(dq, dk, dv, cq, ckv, gq, gkv, wuq, wukv, tqv, tc, ts):
        dkb = dk[:, 0:128] + dk[:, 128:256] + dk[:, 256:384] + dk[:, 384:512]
        dq_ext = (dq * tqv).astype(BF16)
        dkv_ext = jnp.concatenate([dk.astype(BF16), dv], axis=1)
        dcqn = lax.dot_general(dq_ext, wuq, NT_DIMS, preferred_element_type=F32)
        dckvn = lax.dot_general(dkv_ext, wukv, NT_DIMS, preferred_element_type=F32)
        dcq, dgq = rms_bwd(cq, gq, dcqn, 256)
        dckv, dgkv = rms_bwd(ckv, gkv, dckvn, 128)
        return dq_ext, dkv_ext, dkb * tc, dkb * ts, dcq, dckv, dgq, dgkv

    dq_ext, dkv_ext, dkr, dkrp, dcq, dckv, g["mla_q_norm"], g["mla_kv_norm"] = _rowwise(
        prep_bwd, name=n("mla_prep_bwd"), rows=s,
        ins=[("heads", dqm), ("heads", dkm), ("heads", dvm), ("row", proj, 256, C_CQ[0] // 256),
             ("row", proj, 128, C_CKV[0] // 128), ("full", w["mla_q_norm"]), ("full", w["mla_kv_norm"]),
             ("full", w["w_uq"]), ("full", w["w_ukv"]), ("row", tq, 512, 0), ("row", tkc, 128, 0), ("row", tks, 128, 0)],
        outs=[(512, BF16), (1024, BF16), (128, BF16), (128, BF16), (256, BF16), (128, BF16)],
        reds=[(1, 256), (1, 128)])
    g["w_uq"] = _mm(sv["cqn"], dq_ext, ta=True, name=n("dw_uq"))
    g["w_ukv"] = _mm(sv["ckvn"], dkv_ext, ta=True, name=n("dw_ukv"))

    dgb, dgc, duc, g["conv_w"] = _conv_bwd(dy_b, proj, w["conv_w"], name=n("conv_bwd"), s=s)
    dup, g["pool_wbd"], g["pool_scale"] = _pool_bwd(dy_c, proj, w["pool_wbd"], w["pool_scale"], name=n("pool_bwd"), s=s)

    dproj = jnp.concatenate([dq_sw, dcq, dgb, dgc, duc, dup, dk_sw.astype(BF16), dv_sw.astype(BF16), dckv, dkr, dkrp],
                            axis=1)
    g["w_in"] = _mm(sv["h"], dproj, ta=True, name=n("dw_in"))
    return dproj, g


def _mixer_bwd_norm(dx1, dproj, sv, w, l):
    n = lambda t: f"l{l}_{t}"
    dh = _mm(dproj, w["w_in"], tb=True, name=n("d_h"))
    dx0, dx0_b, dg = _rms_bwd(("row", sv["x"], D_MODEL, 0), w["attn_norm"], dh, dx1[0], name=n("attn_norm_bwd"),
                              rows=dh.shape[0], width=D_MODEL, out_dtypes=(F32, BF16))
    return (dx0, dx0_b), dict(attn_norm=dg)


def _loss_head(x, target, g, *, s):
    def fn(xv, tv, gv):
        r = _rstd(xv, D_MODEL)
        e = xv * r * gv - tv
        part = jnp.sum(jnp.sum(e * e, axis=1, keepdims=True), axis=0, keepdims=True) * (0.5 / D_MODEL)
        dy = e * (1.0 / D_MODEL)
        dyg = dy * gv
        dx = r * dyg - xv * (r * r * r) * (jnp.sum(dyg * xv, axis=-1, keepdims=True) * (1.0 / D_MODEL))
        return dx, dx, jnp.sum(dy * xv * r, axis=0, keepdims=True), jnp.broadcast_to(part, (1, LANES))

    return _rowwise(fn, name="loss_head", rows=s,
                    ins=[("row", x, D_MODEL, 0), ("row", target, D_MODEL, 0), ("full", g)],
                    outs=[(D_MODEL, F32), (D_MODEL, BF16)], reds=[(1, D_MODEL), (1, LANES)])


def _alibi_slopes(n):
    return np.asarray([2.0 ** (-8.0 * (i + 1) / n) for i in range(n)], dtype=np.float32)


MIXER_WEIGHTS = ("w_in", "w_uq", "w_ukv", "conv_w", "w_o")
FFN_WEIGHTS = ("w_gate_up", "w_down")


def _mixer_weights(full, rep, l):
    pw = rep["pool_w"][l]
    z = jnp.zeros((HEAD, HEAD), F32)
    wbd = jnp.stack([jnp.block([[pw[2 * j], z], [z, pw[2 * j + 1]]]) for j in range(2)])
    return dict(
        attn_norm=rep["attn_norm"][l][None], w_in=full["w_in"], mla_q_norm=rep["mla_q_norm"][l][None],
        w_uq=full["w_uq"], mla_kv_norm=rep["mla_kv_norm"][l][None], w_ukv=full["w_ukv"],
        conv_w=full["conv_w"], pool_wbd=wbd, pool_scale=rep["pool_scale"][l][None],
        mix_norm=rep["mix_norm"][l][None], w_o=full["w_o"],
        hp_swa=jnp.stack([jnp.asarray(_alibi_slopes(SWA_HEADS)), rep["swa_sinks"][l]], axis=1))


def _ffn_weights(full, rep, l):
    return dict(ffn_norm=rep["ffn_norm"][l][None], w_gate_up=full["w_gate_up"], w_down=full["w_down"])


def _small_grads(g):
    rows = ("attn_norm", "mla_q_norm", "mla_kv_norm", "pool_scale", "ffn_norm", "mix_norm")
    out = {nm: g[nm][0] for nm in rows if nm in g}
    if "swa_sinks" in g:
        out["swa_sinks"] = g["swa_sinks"]
    if "pool_wbd" in g:
        e = g["pool_wbd"]
        out["pool_w"] = jnp.stack([e[j // 2][HEAD * (j % 2):HEAD * (j % 2 + 1), HEAD * (j % 2):HEAD * (j % 2 + 1)]
                                   for j in range(4)])
    return out


def kernel(x, attn_norm, w_in, mla_q_norm, w_uq, mla_kv_norm, w_ukv, conv_w, pool_w, pool_scale, swa_sinks, mix_norm, w_o, ffn_norm, w_gate_up, w_down, final_norm, loss_target, m_attn_norm, m_w_in, m_mla_q_norm, m_w_uq, m_mla_kv_norm, m_w_ukv, m_conv_w, m_pool_w, m_pool_scale, m_swa_sinks, m_mix_norm, m_w_o, m_ffn_norm, m_w_gate_up, m_w_down, m_final_norm, v_attn_norm, v_w_in, v_mla_q_norm, v_w_uq, v_mla_kv_norm, v_w_ukv, v_conv_w, v_pool_w, v_pool_scale, v_swa_sinks, v_mix_norm, v_w_o, v_ffn_norm, v_w_gate_up, v_w_down, v_final_norm):
    given = dict(locals())
    sh_names = [nm for nm, _, _ in SHARDED]
    sh_axis = {nm: ax - 1 for nm, _, ax in SHARDED}
    rep_names = [nm for nm, _ in REPLICATED]
    rep_shapes = [shp for _, shp in REPLICATED]
    rep = {nm: given[nm] for nm in rep_names if nm != "loss"}
    me = 4 * lax.axis_index("x") + 2 * lax.axis_index("y") + lax.axis_index("c")
    my_chip = 2 * lax.axis_index("x") + lax.axis_index("y")
    core = lax.axis_index("c").astype(jnp.int32).reshape(1)

    def behind(token, a):
        return a + token[0, 0].astype(a.dtype)

    def wire(nm, l):
        if nm == "conv_w":
            return lax.bitcast_convert_type(given[nm][l], BF16).reshape(3, -1)
        return given[nm][l].astype(BF16)

    def whole(nm, g, tag):
        if nm in LAYOUTS:
            return _extend(nm, g, name=f"extend_{nm}_{tag}")
        if nm == "conv_w":
            g = lax.bitcast_convert_type(g.reshape(N_DEV, 3, -1, 2), F32)
        return _shards_to_full(g, sh_axis[nm])

    def near_start(names, l, after, tag):
        srcs = [wire(nm, l) for nm in names]
        return _start_copies(_plan_gather_near, 4, srcs, [(N_DEV,) + a.shape for a in srcs], after, name=f"start_gather_{tag}")

    def pass_on(handle, after, tag):
        srcs, lands = _wait_copies(_plan_gather_near, handle, after, name=f"wait_gather_{tag}")
        handle, token = _start_copies(_plan_gather_pass, 3, [], lands, [], name=f"start_pass_{tag}")
        return (srcs, handle), token

    def gathered(names, state, after, tag):
        srcs, handle = state
        _, lands = _wait_copies(_plan_gather_pass, handle, after, name=f"wait_pass_{tag}")
        return {nm: whole(nm, lax.dynamic_update_index_in_dim(land, src, me, 0), tag)
                for nm, src, land in zip(names, srcs, lands)}

    layer1 = MIXER_WEIGHTS + FFN_WEIGHTS
    got = _all_gather([wire(nm, 0) for nm in MIXER_WEIGHTS], name="gather_mixer0")
    full_m0 = {nm: whole(nm, g, "mixer0") for nm, g in zip(MIXER_WEIGHTS, got)}
    h_f0, tok = near_start(FFN_WEIGHTS, 0, [], "ffn0")
    h_l1, tok = near_start(layer1, 1, [tok], "layer1")

    xs, target = x[0], loss_target[0]
    s = xs.shape[0]
    tabs = _rope_tables(s)
    wm, wf, svm, svf = [None] * DEPTH, [None] * DEPTH, [None] * DEPTH, [None] * DEPTH
    wm[0] = _mixer_weights(full_m0, rep, 0)
    wm[0]["attn_norm"] = behind(tok, wm[0]["attn_norm"])
    passing = {}

    def pass_ffn0(y_a):
        passing["ffn0"], token = pass_on(h_f0, [y_a], "ffn0")
        return token

    x1, svm[0] = _mixer_fwd(xs, wm[0], tabs, 0, after_attention=pass_ffn0)
    wf[0] = _ffn_weights(gathered(FFN_WEIGHTS, passing["ffn0"], [x1], "ffn0"), rep, 0)
    passing["layer1"], tok = pass_on(h_l1, [x1], "layer1")
    wf[0]["ffn_norm"] = behind(tok, wf[0]["ffn_norm"])
    x2, svf[0] = _ffn_fwd(x1, wf[0], 0)
    full_1 = gathered(layer1, passing["layer1"], [x2], "layer1")
    wm[1], wf[1] = _mixer_weights(full_1, rep, 1), _ffn_weights(full_1, rep, 1)
    x1, svm[1] = _mixer_fwd(x2, wm[1], tabs, 1)
    x2, svf[1] = _ffn_fwd(x1, wf[1], 1)
    dx_f, dx_b, d_final, loss = _loss_head(x2, target, rep["final_norm"][None], s=s)
    dx = (dx_f, dx_b)

    parts = {}

    def reduce_start(grads, l, after, tag):
        names = [nm for nm in sh_names if nm in grads]
        mine = [_fold_to_shards(nm, grads[nm], name=f"fold_{nm}_{l}") if nm in LAYOUTS
                else _full_to_shards(grads[nm], sh_axis[nm]) for nm in names]
        handle, token = _start_copies(_plan_sibling, 1, mine, [m.shape[1:] for m in mine], after, name=f"start_sibling_{tag}")
        return (names, l, handle), token

    def reduce_mid(state, after, tag):
        names, l, handle = state
        mine, theirs = _wait_copies(_plan_sibling, handle, after, name=f"wait_sibling_{tag}")
        sums = [_add_own(g, o, core, name=f"chip_sum_{nm}_{l}") for nm, g, o in zip(names, mine, theirs)]
        handle, token = _start_copies(_plan_chips, 3, sums, [a.shape for a in sums], [], name=f"start_chips_{tag}")
        return (names, l, handle), token

    def reduce_end(state, after, tag):
        names, l, handle = state
        sums, lands = _wait_copies(_plan_chips, handle, after, name=f"wait_chips_{tag}")
        for nm, own, land in zip(names, sums, lands):
            parts[nm, l] = lax.dynamic_update_index_in_dim(land, lax.dynamic_index_in_dim(own, my_chip, 0, keepdims=False),
                                                           my_chip, 0)

    small = [None] * DEPTH
    in_flight = []
    pending = None
    for l in reversed(range(DEPTH)):
        dgu, g_down = _ffn_bwd_down(dx, svf[l], wf[l], l)
        if pending is not None:
            state, token = reduce_mid(pending, [dgu], f"mixer{l + 1}")
            in_flight.append((state, f"mixer{l + 1}"))
            wf[l]["ffn_norm"] = behind(token, wf[l]["ffn_norm"])
        dx1, g_up = _ffn_bwd_up(dx, dgu, svf[l], wf[l], l)
        g_ffn = {**g_down, **g_up}
        state, token = reduce_start(g_ffn, l, [], f"ffn{l}")
        wm[l]["mix_norm"] = behind(token, wm[l]["mix_norm"])
        dys, g_out = _mixer_bwd_out(dx1, svm[l], wm[l], l)
        state, token = reduce_mid(state, [dys[0]], f"ffn{l}")
        in_flight.append((state, f"ffn{l}"))
        wm[l]["hp_swa"] = behind(token, wm[l]["hp_swa"])
        dproj, g_in = _mixer_bwd_in(dx1, dys, svm[l], wm[l], tabs, l)
        g_mixer = {**g_out, **g_in}
        pending, token = reduce_start(g_mixer, l, [], f"mixer{l}")
        wm[l]["attn_norm"] = behind(token, wm[l]["attn_norm"])
        dx, g_norm = _mixer_bwd_norm(dx1, dproj, svm[l], wm[l], l)
        small[l] = _small_grads({**g_ffn, **g_mixer, **g_norm})
    grads = {nm: jnp.stack([small[l][nm] for l in range(DEPTH)]) for nm in rep_names if nm in small[0]}
    grads["final_norm"] = d_final[0]
    grads["loss"] = loss[0, :1]
    zero = jnp.zeros((1,), F32)
    small = _all_gather([behind(token, _pack([grads[nm] for nm in rep_names]))], name="gather_small_grads")

    last, token = reduce_mid(pending, [dx[0], small[0]], "mixer0")
    for state, tag in in_flight:
        reduce_end(state, [], tag)
    grad_x = dx[0]

    def adamw(nm, after):
        return _adamw([parts[nm, l] for l in range(DEPTH)], given[nm], given["m_" + nm], given["v_" + nm], after,
                      name=f"adamw_{nm}")

    sh_out = {nm: adamw(nm, token) for nm in FFN_WEIGHTS}
    packs = [_pack([given.get(pre + nm, zero) for nm in rep_names])[None] for pre in ("", "m_", "v_")]
    rep_res = _adamw(small, *packs, token, name="adamw_replicated")
    rep_out = [dict(zip(rep_names, _unpack(o[0], rep_shapes))) for o in rep_res]

    reduce_end(last, [rep_res[0], sh_out["w_down"][0]], "mixer0")
    sh_out.update({nm: adamw(nm, token) for nm in MIXER_WEIGHTS})

    out = [rep_out[0]["loss"][0], grad_x[None]]
    for i in range(4):
        out += [sh_out[nm][i] if nm in sh_axis else rep_out[i][nm] for nm in WEIGHT_ORDER]
    return tuple(out)
```

```python
import functools
import math

import numpy as np
import jax
import jax.numpy as jnp
from jax import lax
from jax.experimental import pallas as pl
from jax.experimental.pallas import tpu as pltpu

F32 = jnp.float32
BF16 = jnp.bfloat16

D_MODEL = 1024
DEPTH = 2
D_GROUP = 256
MLA_HEADS = 4
MLA_NOPE = 64
MLA_ROPE = 32
ROPE_THETA = 10000.0
POOL_WINDOWS = (2, 4, 8, 16)
SWA_HEADS = 4
SWA_KV_HEADS = 2
SWA_WINDOW = 128
D_FF = 2816
GU_TILE = 256
RMS_EPS = 1e-6
LANES = 128
HEAD = 64
VMEM_LIMIT = 48 * 1024 * 1024
NEG = -1e30

ADAM_LR = 0.001
ADAM_B1 = 0.9
ADAM_B2 = 0.999
ADAM_EPS = 1e-08
ADAM_WD = 0.01
ADAM_STEP = 10

N_DEV = 8
PACK_COLS = 1024

C_QSW, C_CQ, C_GB, C_GC, C_UCONV, C_UPOOL = (0, 512), (512, 256), (768, 256), (1024, 256), (1280, 256), (1536, 256)
C_KSW, C_VSW, C_CKV, C_KR, C_KRP = (1792, 256), (2048, 256), (2304, 128), (2432, 128), (2560, 128)
D_IN_EXT = 2688

SHARDED = (("w_in", (DEPTH, 1024, 244), 2), ("w_uq", (DEPTH, 256, 48), 2), ("w_ukv", (DEPTH, 128, 64), 2),
           ("conv_w", (DEPTH, 3, 32), 2), ("w_o", (DEPTH, 128, 1024), 1), ("w_gate_up", (DEPTH, 1024, 704), 2),
           ("w_down", (DEPTH, 352, 1024), 1))
REPLICATED = (("attn_norm", (DEPTH, 1024)), ("mla_q_norm", (DEPTH, 256)), ("mla_kv_norm", (DEPTH, 128)),
              ("pool_w", (DEPTH, 4, 64, 64)), ("pool_scale", (DEPTH, 256)), ("swa_sinks", (DEPTH, 4)),
              ("mix_norm", (DEPTH, 1024)), ("ffn_norm", (DEPTH, 1024)), ("final_norm", (1024,)), ("loss", (1,)))
WEIGHT_ORDER = ("attn_norm", "w_in", "mla_q_norm", "w_uq", "mla_kv_norm", "w_ukv", "conv_w", "pool_w", "pool_scale",
                "swa_sinks", "mix_norm", "w_o", "ffn_norm", "w_gate_up", "w_down", "final_norm")


def _params(sem):
    return pltpu.CompilerParams(dimension_semantics=sem, vmem_limit_bytes=VMEM_LIMIT)


def _pick(dim, target):
    if dim <= target:
        return dim
    best = None
    for t in range(LANES, target + 1, LANES):
        if dim % t == 0:
            best = t
    assert best is not None, (dim, target)
    return best


def _mm(a, b, *, name, ta=False, tb=False, res=None, out_dtype=F32, tm=1024, tn=1024, tk=1024, epilogue=None):
    m, k = (a.shape[1], a.shape[0]) if ta else a.shape
    n = b.shape[0] if tb else b.shape[1]
    assert (b.shape[1] if tb else b.shape[0]) == k
    tm, tn, tk = _pick(m, tm), _pick(n, tn), _pick(k, tk)
    nj, nk = n // tn, k // tk
    dims = (((0 if ta else 1,), (1 if tb else 0,)), ((), ()))
    fn, extra, outs = epilogue if epilogue is not None else (None, [], [(n, out_dtype)])
    if res is not None:
        assert epilogue is None
        fn, extra = (lambda acc, r: (acc + r,)), [res]
    n_in, n_out = 2 + len(extra), len(outs)

    def body(*refs):
        a_ref, b_ref, acc_ref = refs[0], refs[1], refs[-1]
        kk = pl.program_id(2)

        @pl.when(kk == 0)
        def _():
            acc_ref[...] = jnp.zeros_like(acc_ref)

        acc_ref[...] += lax.dot_general(a_ref[...].astype(BF16), b_ref[...].astype(BF16), dims,
                                        preferred_element_type=F32)

        @pl.when(kk == nk - 1)
        def _():
            tiles = (acc_ref[...],) if fn is None else fn(acc_ref[...], *[r[...] for r in refs[2:n_in]])
            for o_ref, tile in zip(refs[n_in:n_in + n_out], tiles):
                o_ref[...] = tile.astype(o_ref.dtype)

    def col_tiles(width):
        assert width % (nj * LANES) == 0, (width, nj)
        return pl.BlockSpec((tm, width // nj), lambda i, j, kk: (i, j))

    a_spec = pl.BlockSpec((tk, tm), lambda i, j, kk: (kk, i)) if ta else pl.BlockSpec((tm, tk), lambda i, j, kk: (i, kk))
    b_spec = pl.BlockSpec((tn, tk), lambda i, j, kk: (j, kk)) if tb else pl.BlockSpec((tk, tn), lambda i, j, kk: (kk, j))
    res_ = pl.pallas_call(
        body, name=name, grid=(m // tm, nj, nk), in_specs=[a_spec, b_spec] + [col_tiles(e.shape[1]) for e in extra],
        out_specs=[col_tiles(w) for w, _ in outs],
        out_shape=[jax.ShapeDtypeStruct((m, w), dt) for w, dt in outs],
        scratch_shapes=[pltpu.VMEM((tm, tn), F32)],
        compiler_params=_params(("parallel", "parallel", "arbitrary")),
    )(a, b, *extra)
    return res_[0] if epilogue is None else res_


def _rowwise(fn, *, name, rows, ins, outs, reds=(), tm=512):
    tm = min(tm, rows)
    assert rows % tm == 0
    n_in, n_out = len(ins), len(outs)

    def body(*refs):
        vals = [jnp.concatenate([r[h] for h in range(r.shape[0])], axis=1) if spec[0] == "heads" else r[...]
                for spec, r in zip(ins, refs[:n_in])]
        res = fn(*vals)
        for out, r, v in zip(outs, refs[n_in:n_in + n_out], res[:n_out]):
            if len(out) >= 3:
                for h in range(out[2]):
                    piece = v[:, h * LANES:(h + 1) * LANES]
                    r[h] = (piece.T if len(out) == 4 else piece).astype(r.dtype)
            else:
                r[...] = v.astype(r.dtype)
        if reds:
            @pl.when(pl.program_id(0) == 0)
            def _():
                for r in refs[n_in + n_out:]:
                    r[...] = jnp.zeros_like(r)

            for r, v in zip(refs[n_in + n_out:], res[n_out:]):
                r[...] += v

    in_specs, args = [], []
    for spec in ins:
        if spec[0] == "row":
            _, arr, width, blk = spec
            in_specs.append(pl.BlockSpec((tm, width), functools.partial(lambda i, blk: (i, blk), blk=blk)))
        elif spec[0] == "heads":
            arr = spec[1]
            in_specs.append(pl.BlockSpec((arr.shape[0], tm, LANES), lambda i: (0, i, 0)))
        else:
            arr = spec[1]
            in_specs.append(pl.BlockSpec(arr.shape, functools.partial(lambda i, nd: (0,) * nd, nd=arr.ndim)))
        args.append(arr)
    def out_spec(o):
        if len(o) == 4:
            return pl.BlockSpec((o[2], LANES, tm), lambda i: (0, 0, i)), (o[2], LANES, rows)
        if len(o) == 3:
            return pl.BlockSpec((o[2], tm, LANES), lambda i: (0, i, 0)), (o[2], rows, LANES)
        return pl.BlockSpec((tm, o[0]), lambda i: (i, 0)), (rows, o[0])

    out_specs = [out_spec(o)[0] for o in outs]
    out_shape = [jax.ShapeDtypeStruct(out_spec(o)[1], o[1]) for o in outs]
    out_specs += [pl.BlockSpec((r, w), lambda i: (0, 0)) for r, w in reds]
    out_shape += [jax.ShapeDtypeStruct((r, w), F32) for r, w in reds]
    return pl.pallas_call(body, name=name, grid=(rows // tm,), in_specs=in_specs, out_specs=out_specs,
                          out_shape=out_shape, compiler_params=_params(("arbitrary",)))(*args)


def _rstd(x, n):
    return lax.rsqrt(jnp.sum(x * x, axis=-1, keepdims=True) * (1.0 / n) + RMS_EPS)


def _rms_fwd(x_spec, g, *, name, rows, width):
    def fn(x, gv):
        return (x * _rstd(x, width) * gv,)
    return _rowwise(fn, name=name, rows=rows, ins=[x_spec, ("full", g)], outs=[(width, BF16)])[0]


def _rms_bwd(x_spec, g, dy, res, *, name, rows, width, out_dtypes):
    def fn(x, gv, dyv, *rest):
        r = _rstd(x, width)
        dyg = dyv * gv
        dx = r * dyg - x * (r * r * r) * (jnp.sum(dyg * x, axis=-1, keepdims=True) * (1.0 / width))
        if rest:
            dx = dx + rest[0]
        return (dx,) * len(out_dtypes) + (jnp.sum(dyv * x * r, axis=0, keepdims=True),)

    ins = [x_spec, ("full", g), ("row", dy, width, 0)]
    if res is not None:
        ins.append(("row", res, width, 0))
    return _rowwise(fn, name=name, rows=rows, ins=ins, outs=[(width, dt) for dt in out_dtypes], reds=[(1, width)])


NT_DIMS = (((1,), (1,)), ((), ()))
TN_DIMS = (((0,), (0,)), ((), ()))
BNT_DIMS = (((2,), (2,)), ((0,), (0,)))
BNN_DIMS = (((2,), (1,)), ((0,), (0,)))


def _mla_tile(s):
    return min(512, s)


def _causal(shape, query_axis):
    return lax.broadcasted_iota(jnp.int32, shape, query_axis) >= lax.broadcasted_iota(jnp.int32, shape, 1 - query_axis)


def _mla_fwd(qa, ka, vta, *, name, s, nh):
    t = _mla_tile(s)
    nq = s // t

    def body(q_ref, k_ref, vt_ref, o_ref, lse_ref, m_s, acc_s):
        i, j = pl.program_id(0), pl.program_id(1)

        @pl.when(j == 0)
        def _():
            m_s[...] = jnp.full_like(m_s, NEG)
            acc_s[...] = jnp.zeros_like(acc_s)

        def step(diag):
            sc = lax.dot_general(k_ref[...], q_ref[...], BNT_DIMS, preferred_element_type=F32)
            if diag:
                sc = jnp.where(_causal(sc.shape[1:], 1)[None], sc, NEG)
            m_prev = m_s[...]
            m_new = jnp.maximum(m_prev, jnp.max(sc, axis=1, keepdims=True))
            p = jnp.exp(sc - m_new).astype(BF16)
            acc_s[...] = (jnp.exp(m_prev - m_new) * acc_s[...]
                          + lax.dot_general(vt_ref[...], p, BNN_DIMS, preferred_element_type=F32))
            m_s[...] = m_new

        pl.when(j < i)(functools.partial(step, False))
        pl.when(j == i)(functools.partial(step, True))

        @pl.when(j == nq - 1)
        def _():
            row = lax.broadcasted_iota(jnp.int32, (LANES, t), 0)
            for h in range(nh):
                acc = acc_s[h]
                l = acc[HEAD:HEAD + 1, :]
                o_ref[h] = jnp.where(row < HEAD, acc / l, 0.0).T
                lse_ref[h] = m_s[h] + jnp.log(l)

    q_spec = pl.BlockSpec((nh, t, LANES), lambda i, j: (0, i, 0))
    k_spec = pl.BlockSpec((nh, t, LANES), lambda i, j: (0, jnp.minimum(j, i), 0))
    vt_spec = pl.BlockSpec((nh, LANES, t), lambda i, j: (0, 0, jnp.minimum(j, i)))
    return pl.pallas_call(
        body, name=name, grid=(nq, nq), in_specs=[q_spec, k_spec, vt_spec],
        out_specs=[q_spec, pl.BlockSpec((nh, 1, t), lambda i, j: (0, 0, i))],
        out_shape=[jax.ShapeDtypeStruct((nh, s, LANES), F32), jax.ShapeDtypeStruct((nh, 1, s), F32)],
        scratch_shapes=[pltpu.VMEM((nh, 1, t), F32), pltpu.VMEM((nh, LANES, t), F32)],
        compiler_params=_params(("parallel", "arbitrary")),
    )(qa, ka, vta)


def _mla_bwd(qa, ka, kta, va, o, do, lse_row, after, *, name, s, nh):
    t = _mla_tile(s)
    nq = s // t

    def body(q_ref, k_ref, kt_ref, v_ref, o_ref, do_ref, lse_ref, after_ref, dq_hbm, dk_ref, dv_ref,
             dqt_s, dk_s, dv_s, d_s, stage, sem):
        kb, j = pl.program_id(0), pl.program_id(1)
        cols = pl.ds(pl.multiple_of(j * t, t), t)

        @pl.when(j == 0)
        def _():
            dk_s[...] = jnp.zeros_like(dk_s)
            dv_s[...] = jnp.zeros_like(dv_s)

        @pl.when(kb == 0)
        def _():
            dqt_s[:, :, cols] = jnp.zeros((nh, LANES, t), F32)
            for h in range(nh):
                d_col = jnp.sum(do_ref[h] * o_ref[h], axis=1, keepdims=True)
                d_s[h, :, cols] = jnp.broadcast_to(d_col, (t, LANES)).T[0:1, :]

        def step(diag):
            q = q_ref[...]
            do_b = do_ref[...].astype(BF16)
            sc = lax.dot_general(k_ref[...], q, BNT_DIMS, preferred_element_type=F32)
            if diag:
                sc = jnp.where(_causal(sc.shape[1:], 1)[None], sc, NEG)
            p = jnp.exp(sc - lse_ref[...])
            dv_s[...] += lax.dot_general(p.astype(BF16), do_b, BNN_DIMS, preferred_element_type=F32)
            dp = lax.dot_general(v_ref[...], do_b, BNT_DIMS, preferred_element_type=F32)
            ds = (p * (dp - d_s[:, :, cols])).astype(BF16)
            dk_s[...] += lax.dot_general(ds, q, BNN_DIMS, preferred_element_type=F32)
            dqt_s[:, :, cols] += lax.dot_general(kt_ref[...], ds, BNN_DIMS, preferred_element_type=F32)

        pl.when(j > kb)(functools.partial(step, False))
        pl.when(j == kb)(functools.partial(step, True))

        @pl.when(j == kb)
        def _():
            for h in range(nh):
                stage[h] = dqt_s[h, :, cols].T
            out = pltpu.make_async_copy(stage, dq_hbm.at[:, cols, :], sem)
            out.start()
            out.wait()

        @pl.when(j == nq - 1)
        def _():
            dk_ref[...] = dk_s[...]
            dv_ref[...] = dv_s[...].astype(dv_ref.dtype)

    q_spec = pl.BlockSpec((nh, t, LANES), lambda kb, j: (0, jnp.maximum(j, kb), 0))
    kv_spec = pl.BlockSpec((nh, t, LANES), lambda kb, j: (0, kb, 0))
    kt_spec = pl.BlockSpec((nh, LANES, t), lambda kb, j: (0, 0, kb))
    row_spec = pl.BlockSpec((nh, 1, t), lambda kb, j: (0, 0, jnp.maximum(j, kb)))
    whole = jax.ShapeDtypeStruct((nh, s, LANES), F32)
    return pl.pallas_call(
        body, name=name, grid=(nq, nq),
        in_specs=[q_spec, kv_spec, kt_spec, kv_spec, q_spec, q_spec, row_spec, pl.BlockSpec(memory_space=pl.ANY)],
        out_specs=[pl.BlockSpec(memory_space=pl.ANY), kv_spec, kv_spec],
        out_shape=[whole, whole, jax.ShapeDtypeStruct((nh, s, LANES), BF16)],
        scratch_shapes=[pltpu.VMEM((nh, LANES, s), F32), pltpu.VMEM((nh, t, LANES), F32), pltpu.VMEM((nh, t, LANES), F32),
                        pltpu.VMEM((nh, 1, s), F32), pltpu.VMEM((nh, t, LANES), F32), pltpu.SemaphoreType.DMA],
        compiler_params=_params(("arbitrary", "arbitrary")),
    )(qa, ka, kta, va, o, do, lse_row, after)


SWA_PIECE = 128
SWA_KEYS = 2 * SWA_PIECE


def _swa_block(s):
    return min(512, s)


def _swa_piece(hp_ref, h, q, k_ref, v_ref, qpos0, scale):
    kstart = pl.multiple_of(jnp.maximum(qpos0 - SWA_PIECE, 0), SWA_PIECE)
    k = k_ref[pl.ds(kstart, SWA_KEYS), :].astype(BF16)
    v = v_ref[pl.ds(kstart, SWA_KEYS), :].astype(BF16)
    sc = lax.dot_general(q, k, NT_DIMS, preferred_element_type=F32)
    dist = (qpos0 + lax.broadcasted_iota(jnp.int32, sc.shape, 0)) - (kstart + lax.broadcasted_iota(jnp.int32, sc.shape, 1))
    sc = sc * scale - hp_ref[h, 0] * dist.astype(F32)
    sc = jnp.where((dist >= 0) & (dist < SWA_WINDOW), sc, NEG)
    return kstart, k, v, sc


def _swa_fwd(proj, hp, *, name, s, scale):
    tb = _swa_block(s)
    group = SWA_HEADS // SWA_KV_HEADS
    q_off, k_off, v_off = C_QSW[0] // LANES, C_KSW[0] // LANES, C_VSW[0] // LANES

    def body(hp_ref, q_ref, k_ref, v_ref, o_ref, lse_ref):
        h, i = pl.program_id(0), pl.program_id(1)
        sink = hp_ref[h, 1]
        for r in range(0, tb, SWA_PIECE):
            rows = pl.ds(r, SWA_PIECE)
            _, _, v, sc = _swa_piece(hp_ref, h, q_ref[rows, :].astype(BF16), k_ref, v_ref, i * tb + r, scale)
            m = jnp.maximum(jnp.max(sc, axis=1, keepdims=True), sink)
            p = jnp.exp(sc - m)
            l = jnp.sum(p, axis=1, keepdims=True) + jnp.exp(sink - m)
            o_ref[rows, :] = jnp.dot(p.astype(BF16), v, preferred_element_type=F32) / l
            lse_ref[rows, :] = m + jnp.log(l)

    whole = lambda off: pl.BlockSpec((s, LANES), lambda h, i: (0, off + h // group))
    return pl.pallas_call(
        body, name=name, grid=(SWA_HEADS, s // tb),
        in_specs=[pl.BlockSpec(memory_space=pltpu.SMEM), pl.BlockSpec((tb, LANES), lambda h, i: (i, q_off + h)),
                  whole(k_off), whole(v_off)],
        out_specs=[pl.BlockSpec((tb, LANES), lambda h, i: (i, h)), pl.BlockSpec((None, tb, 1), lambda h, i: (h, i, 0))],
        out_shape=[jax.ShapeDtypeStruct((s, SWA_HEADS * LANES), F32), jax.ShapeDtypeStruct((SWA_HEADS, s, 1), F32)],
        compiler_params=_params(("parallel", "parallel")),
    )(hp, proj, proj, proj)


def _swa_bwd(proj, o, do, lse, hp, *, name, s, scale):
    tb = _swa_block(s)
    nqb = s // tb
    group = SWA_HEADS // SWA_KV_HEADS
    q_off, k_off, v_off = C_QSW[0] // LANES, C_KSW[0] // LANES, C_VSW[0] // LANES

    def body(hp_ref, q_ref, k_ref, v_ref, o_ref, do_ref, lse_ref, dq_ref, dk_ref, dv_ref, dsink_ref):
        kh, g, i = pl.program_id(0), pl.program_id(1), pl.program_id(2)
        h = kh * group + g
        sink = hp_ref[h, 1]

        @pl.when((g == 0) & (i == 0))
        def _():
            dk_ref[...] = jnp.zeros_like(dk_ref)
            dv_ref[...] = jnp.zeros_like(dv_ref)

        @pl.when(i == 0)
        def _():
            dsink_ref[...] = jnp.zeros_like(dsink_ref)

        for r in range(0, tb, SWA_PIECE):
            rows = pl.ds(r, SWA_PIECE)
            q = q_ref[rows, :].astype(BF16)
            dov = do_ref[rows, :]
            do_b = dov.astype(BF16)
            lse_r = lse_ref[rows, :]
            d_r = jnp.sum(dov * o_ref[rows, :], axis=1, keepdims=True)
            kstart, k, v, sc = _swa_piece(hp_ref, h, q, k_ref, v_ref, i * tb + r, scale)
            p = jnp.exp(sc - lse_r)
            dp = lax.dot_general(do_b, v, NT_DIMS, preferred_element_type=F32)
            ds = (p * (dp - d_r)).astype(BF16)
            dq_ref[rows, :] = (jnp.dot(ds, k, preferred_element_type=F32) * scale).astype(dq_ref.dtype)
            win = pl.ds(kstart, SWA_KEYS)
            dk_ref[win, :] += lax.dot_general(ds, q, TN_DIMS, preferred_element_type=F32) * scale
            dv_ref[win, :] += lax.dot_general(p.astype(BF16), do_b, TN_DIMS, preferred_element_type=F32)
            part = jnp.sum(-jnp.exp(sink - lse_r) * d_r, axis=0, keepdims=True)
            dsink_ref[...] += jnp.broadcast_to(part, (1, LANES))

    whole = lambda off: pl.BlockSpec((s, LANES), lambda kh, g, i: (0, off + kh))
    q_map = lambda kh, g, i: (i, kh * group + g)
    return pl.pallas_call(
        body, name=name, grid=(SWA_KV_HEADS, group, nqb),
        in_specs=[pl.BlockSpec(memory_space=pltpu.SMEM),
                  pl.BlockSpec((tb, LANES), lambda kh, g, i: (i, q_off + kh * group + g)), whole(k_off), whole(v_off),
                  pl.BlockSpec((tb, LANES), q_map), pl.BlockSpec((tb, LANES), q_map),
                  pl.BlockSpec((None, tb, 1), lambda kh, g, i: (kh * group + g, i, 0))],
        out_specs=[pl.BlockSpec((tb, LANES), q_map), whole(0), whole(0),
                   pl.BlockSpec((None, 1, LANES), lambda kh, g, i: (kh * group + g, 0, 0))],
        out_shape=[jax.ShapeDtypeStruct((s, SWA_HEADS * LANES), BF16),
                   jax.ShapeDtypeStruct((s, SWA_KV_HEADS * LANES), F32), jax.ShapeDtypeStruct((s, SWA_KV_HEADS * LANES), F32),
                   jax.ShapeDtypeStruct((SWA_HEADS, 1, LANES), F32)],
        compiler_params=_params(("parallel", "arbitrary", "arbitrary")),
    )(hp, proj, proj, proj, o, do, lse)


def _shift_down(z, k):
    rows = lax.broadcasted_iota(jnp.int32, z.shape, 0)
    return jnp.where(rows >= k, pltpu.roll(z, k, 0), 0.0)


def _shift_up(z, k):
    n = z.shape[0]
    rows = lax.broadcasted_iota(jnp.int32, z.shape, 0)
    return jnp.where(rows < n - k, pltpu.roll(z, n - k, 0), 0.0)


def _rows3(a, b, c):
    r = lax.broadcasted_iota(jnp.int32, (3, a.shape[1]), 0)
    return jnp.where(r == 0, a, jnp.where(r == 1, b, c))


def _col_spec(s, off):
    return pl.BlockSpec((s, LANES), functools.partial(lambda j, off: (0, off + j), off=off))


def _conv_fwd(proj, conv_w, *, name, s):
    def body(gb_ref, gc_ref, u_ref, w_ref, y_ref):
        w0, w1, w2 = w_ref[0:1, :], w_ref[1:2, :], w_ref[2:3, :]
        z = gc_ref[...] * u_ref[...]
        c = w2 * z + w1 * _shift_down(z, 1) + w0 * _shift_down(z, 2)
        y_ref[...] = gb_ref[...] * c

    return pl.pallas_call(
        body, name=name, grid=(2,),
        in_specs=[_col_spec(s, C_GB[0] // LANES), _col_spec(s, C_GC[0] // LANES), _col_spec(s, C_UCONV[0] // LANES),
                  pl.BlockSpec((3, LANES), lambda j: (0, j))],
        out_specs=_col_spec(s, 0), out_shape=jax.ShapeDtypeStruct((s, D_GROUP), F32),
        compiler_params=_params(("parallel",)),
    )(proj, proj, proj, conv_w)


def _conv_bwd(dy, proj, conv_w, *, name, s):
    def body(dy_ref, gb_ref, gc_ref, u_ref, w_ref, dgb_ref, dgc_ref, du_ref, dw_ref):
        w0, w1, w2 = w_ref[0:1, :], w_ref[1:2, :], w_ref[2:3, :]
        gc, u, dyv = gc_ref[...], u_ref[...], dy_ref[...]
        z = gc * u
        z1, z2 = _shift_down(z, 1), _shift_down(z, 2)
        c = w2 * z + w1 * z1 + w0 * z2
        dgb_ref[...] = (dyv * c).astype(dgb_ref.dtype)
        dc = dyv * gb_ref[...]
        dz = w2 * dc + w1 * _shift_up(dc, 1) + w0 * _shift_up(dc, 2)
        dgc_ref[...] = (dz * u).astype(dgc_ref.dtype)
        du_ref[...] = (dz * gc).astype(du_ref.dtype)
        dw_ref[...] = _rows3(jnp.sum(dc * z2, axis=0, keepdims=True), jnp.sum(dc * z1, axis=0, keepdims=True),
                             jnp.sum(dc * z, axis=0, keepdims=True))

    act = jax.ShapeDtypeStruct((s, D_GROUP), BF16)
    return pl.pallas_call(
        body, name=name, grid=(2,),
        in_specs=[_col_spec(s, 0), _col_spec(s, C_GB[0] // LANES), _col_spec(s, C_GC[0] // LANES),
                  _col_spec(s, C_UCONV[0] // LANES), pl.BlockSpec((3, LANES), lambda j: (0, j))],
        out_specs=[_col_spec(s, 0), _col_spec(s, 0), _col_spec(s, 0), pl.BlockSpec((3, LANES), lambda j: (0, j))],
        out_shape=[act, act, act, jax.ShapeDtypeStruct((3, D_GROUP), F32)],
        compiler_params=_params(("parallel",)),
    )(dy, proj, proj, proj, conv_w)


def _pool_select(j, lane, a2, a4, a8, a16):
    lo = lane < HEAD
    return jnp.where(j == 0, jnp.where(lo, a2, a4), jnp.where(lo, a8, a16))


def _pooled(u, j):
    s2 = u + _shift_down(u, 1)
    s4 = s2 + _shift_down(s2, 2)
    s8 = s4 + _shift_down(s4, 4)
    s16 = s8 + _shift_down(s8, 8)
    lane = lax.broadcasted_iota(jnp.int32, u.shape, 1)
    rows = lax.broadcasted_iota(jnp.int32, u.shape, 0)
    win = _pool_select(j, lane, 2, 4, 8, 16)
    count = jnp.minimum(rows + 1, win).astype(F32)
    return _pool_select(j, lane, s2, s4, s8, s16) / count - u, count


def _pool_fwd(proj, wbd, scale, *, name, s):
    def body(u_ref, w_ref, sc_ref, y_ref):
        pooled, _ = _pooled(u_ref[...], pl.program_id(0))
        y_ref[...] = jnp.dot(pooled.astype(BF16), w_ref[...].astype(BF16), preferred_element_type=F32) * sc_ref[...]

    return pl.pallas_call(
        body, name=name, grid=(2,),
        in_specs=[_col_spec(s, C_UPOOL[0] // LANES), pl.BlockSpec((None, LANES, LANES), lambda j: (j, 0, 0)),
                  pl.BlockSpec((1, LANES), lambda j: (0, j))],
        out_specs=_col_spec(s, 0), out_shape=jax.ShapeDtypeStruct((s, D_GROUP), F32),
        compiler_params=_params(("parallel",)),
    )(proj, wbd, scale)


def _pool_bwd(dy, proj, wbd, scale, *, name, s):
    def body(dy_ref, u_ref, w_ref, sc_ref, du_ref, dw_ref, dsc_ref):
        j = pl.program_id(0)
        pooled, count = _pooled(u_ref[...], j)
        pooled_b = pooled.astype(BF16)
        w_b = w_ref[...].astype(BF16)
        dyv = dy_ref[...]
        mixed = jnp.dot(pooled_b, w_b, preferred_element_type=F32)
        dsc_ref[...] = jnp.sum(dyv * mixed, axis=0, keepdims=True)
        dms = (dyv * sc_ref[...]).astype(BF16)
        dw_ref[...] = lax.dot_general(pooled_b, dms, (((0,), (0,)), ((), ())), preferred_element_type=F32)
        dpooled = lax.dot_general(dms, w_b, (((1,), (1,)), ((), ())), preferred_element_type=F32)
        r = dpooled / count
        a2 = r + _shift_up(r, 1)
        a4 = a2 + _shift_up(a2, 2)
        a8 = a4 + _shift_up(a4, 4)
        a16 = a8 + _shift_up(a8, 8)
        lane = lax.broadcasted_iota(jnp.int32, r.shape, 1)
        du_ref[...] = (_pool_select(j, lane, a2, a4, a8, a16) - dpooled).astype(du_ref.dtype)

    return pl.pallas_call(
        body, name=name, grid=(2,),
        in_specs=[_col_spec(s, 0), _col_spec(s, C_UPOOL[0] // LANES),
                  pl.BlockSpec((None, LANES, LANES), lambda j: (j, 0, 0)), pl.BlockSpec((1, LANES), lambda j: (0, j))],
        out_specs=[_col_spec(s, 0), pl.BlockSpec((None, LANES, LANES), lambda j: (j, 0, 0)),
                   pl.BlockSpec((1, LANES), lambda j: (0, j))],
        out_shape=[jax.ShapeDtypeStruct((s, D_GROUP), BF16), jax.ShapeDtypeStruct((2, LANES, LANES), F32),
                   jax.ShapeDtypeStruct((1, D_GROUP), F32)],
        compiler_params=_params(("parallel",)),
    )(dy, proj, wbd, scale)


def _mesh_pos():
    return lax.axis_index("x"), lax.axis_index("y"), lax.axis_index("c")


def _any_specs(n):
    return [pl.BlockSpec(memory_space=pl.ANY)] * n


def _all_gather(xs, *, name):
    n = len(xs)

    def body(*refs):
        x_refs, out_refs = refs[:n], refs[n:2 * n]
        send_sems, recv_sems, local_sems = refs[2 * n:]
        x, y, cc = _mesh_pos()
        me, sibling = (x, y, cc), (x, y, 1 - cc)
        chips = [(1 - x, y), (x, 1 - y), (1 - x, 1 - y)]

        def slot(a, px, py, pc):
            return out_refs[a].at[4 * px + 2 * py + pc]

        def copy(a, k, block, to, src=None):
            return pltpu.make_async_remote_copy(
                src_ref=slot(a, *block) if src is None else src, dst_ref=slot(a, *block), send_sem=send_sems.at[a, k],
                recv_sem=recv_sems.at[a, k], device_id=to, device_id_type=pl.DeviceIdType.MESH)

        mine = [pltpu.make_async_copy(x_refs[a], slot(a, *me), local_sems.at[a]) for a in range(n)]
        first = []
        for a in range(n):
            first.append(copy(a, 0, me, sibling, src=x_refs[a]))
            first += [copy(a, 1 + j, me, (*chip, cc), src=x_refs[a]) for j, chip in enumerate(chips)]
        for cp in mine + first:
            cp.start()
        passed = []
        for j, chip in enumerate(chips):
            for a in range(n):
                copy(a, 1 + j, (*chip, cc), me).wait_recv()
                passed.append(copy(a, 4 + j, (*chip, cc), sibling))
                passed[-1].start()
        for a in range(n):
            copy(a, 0, sibling, me).wait_recv()
        for j, chip in enumerate(chips):
            for a in range(n):
                copy(a, 4 + j, (*chip, 1 - cc), me).wait_recv()
        for cp in first + passed:
            cp.wait_send()
        for cp in mine:
            cp.wait()

    return pl.pallas_call(
        body, name=name, out_shape=[jax.ShapeDtypeStruct((N_DEV,) + a.shape, a.dtype) for a in xs],
        in_specs=_any_specs(n), out_specs=_any_specs(n),
        scratch_shapes=[pltpu.SemaphoreType.DMA((n, 7)), pltpu.SemaphoreType.DMA((n, 7)), pltpu.SemaphoreType.DMA((n,))],
    )(*xs)


def _plan_gather_near(src_refs, land_refs, send_sems, recv_sems):
    x, y, cc = _mesh_pos()
    me = 4 * x + 2 * y + cc
    plan = []
    for a, (src, land) in enumerate(zip(src_refs, land_refs)):
        for k, (px, py, pc) in enumerate([(x, y, 1 - cc), (1 - x, y, cc), (x, 1 - y, cc), (1 - x, 1 - y, cc)]):
            sems = dict(send_sem=send_sems.at[4 * a + k], recv_sem=recv_sems.at[4 * a + k], device_id=(px, py, pc),
                        device_id_type=pl.DeviceIdType.MESH)
            plan.append((pltpu.make_async_remote_copy(src_ref=src, dst_ref=land.at[me], **sems),
                         pltpu.make_async_remote_copy(src_ref=src, dst_ref=land.at[4 * px + 2 * py + pc], **sems)))
    return plan


def _plan_gather_pass(src_refs, land_refs, send_sems, recv_sems):
    x, y, cc = _mesh_pos()
    plan = []
    for a, land in enumerate(land_refs):
        for j, (px, py) in enumerate([(1 - x, y), (x, 1 - y), (1 - x, 1 - y)]):
            mine, theirs = land.at[4 * px + 2 * py + cc], land.at[4 * px + 2 * py + 1 - cc]
            sems = dict(send_sem=send_sems.at[3 * a + j], recv_sem=recv_sems.at[3 * a + j], device_id=(x, y, 1 - cc),
                        device_id_type=pl.DeviceIdType.MESH)
            plan.append((pltpu.make_async_remote_copy(src_ref=mine, dst_ref=mine, **sems),
                         pltpu.make_async_remote_copy(src_ref=mine, dst_ref=theirs, **sems)))
    return plan


def _plan_sibling(src_refs, land_refs, send_sems, recv_sems):
    x, y, cc = _mesh_pos()
    plan = []
    for a, (src, land) in enumerate(zip(src_refs, land_refs)):
        cp = pltpu.make_async_remote_copy(
            src_ref=src.at[1 - cc], dst_ref=land, send_sem=send_sems.at[a], recv_sem=recv_sems.at[a],
            device_id=(x, y, 1 - cc), device_id_type=pl.DeviceIdType.MESH)
        plan.append((cp, cp))
    return plan


def _plan_chips(src_refs, land_refs, send_sems, recv_sems):
    x, y, cc = _mesh_pos()
    my_chip = 2 * x + y
    plan = []
    for a, (src, land) in enumerate(zip(src_refs, land_refs)):
        for j, (px, py) in enumerate([(1 - x, y), (x, 1 - y), (1 - x, 1 - y)]):
            peer = 2 * px + py
            sems = dict(send_sem=send_sems.at[3 * a + j], recv_sem=recv_sems.at[3 * a + j], device_id=(px, py, cc),
                        device_id_type=pl.DeviceIdType.MESH)
            plan.append((pltpu.make_async_remote_copy(src_ref=src.at[peer], dst_ref=land.at[my_chip], **sems),
                         pltpu.make_async_remote_copy(src_ref=src.at[peer], dst_ref=land.at[peer], **sems)))
    return plan


HBM_SPEC = pl.BlockSpec(memory_space=pltpu.HBM)
SEM_SPEC = pl.BlockSpec(memory_space=pltpu.SEMAPHORE)
ANY_SPEC = pl.BlockSpec(memory_space=pl.ANY)
SIDE_EFFECT = pltpu.CompilerParams(has_side_effects=pltpu.SideEffectType.DATAFLOW_SIDE_EFFECTING)


def _start_copies(plan, sems_per_array, srcs, lands, after, *, name):
    lands = [lax.empty(l, a.dtype) if isinstance(l, tuple) else l for l, a in zip(lands, srcs or lands)]
    ns, n = len(srcs), len(srcs) + len(lands)

    def body(*refs):
        send_sems, recv_sems = refs[n + len(after)], refs[n + len(after) + 1]
        for out, _ in plan(refs[:ns], refs[ns:n], send_sems, recv_sems):
            out.start()
        refs[-1][...] = jnp.zeros_like(refs[-1])

    sem = pltpu.SemaphoreType.DMA((len(lands) * sems_per_array,))
    res = pl.pallas_call(
        body, name=name,
        out_shape=(sem, sem, *[pltpu.HBM(a.shape, a.dtype) for a in srcs + lands], jax.ShapeDtypeStruct((8, LANES), F32)),
        in_specs=[HBM_SPEC] * n + [ANY_SPEC] * len(after),
        out_specs=(SEM_SPEC, SEM_SPEC, *[HBM_SPEC] * n, pl.BlockSpec(memory_space=pltpu.VMEM)),
        input_output_aliases={i: 2 + i for i in range(n)}, compiler_params=SIDE_EFFECT,
    )(*[pltpu.with_memory_space_constraint(a, pltpu.HBM) for a in srcs + lands], *after)
    return (res[0], res[1], list(res[2:2 + ns]), list(res[2 + ns:2 + n])), res[-1]


def _wait_copies(plan, handle, after, *, name):
    send, recv, srcs, lands = handle
    ns, n = len(srcs), len(srcs) + len(lands)

    def body(*refs):
        for out, inc in plan(refs[:ns], refs[ns:n], refs[n], refs[n + 1]):
            out.wait_send()
            inc.wait_recv()

    res = pl.pallas_call(
        body, name=name, out_shape=tuple(pltpu.HBM(a.shape, a.dtype) for a in srcs + lands),
        in_specs=[HBM_SPEC] * n + [SEM_SPEC, SEM_SPEC] + [ANY_SPEC] * len(after), out_specs=[HBM_SPEC] * n,
        input_output_aliases={i: i for i in range(n)}, compiler_params=SIDE_EFFECT,
    )(*srcs, *lands, send, recv, *after)
    return list(res[:ns]), list(res[ns:])


def _row_tile(rows, target=512):
    if rows <= target:
        return rows
    best = None
    for t in range(8, target + 1, 8):
        if rows % t == 0:
            best = t
    assert best is not None, (rows, target)
    return best


def _add_own(g, other, core, *, name):
    _, _, rows, cols = g.shape
    tm = _row_tile(rows)

    def body(c_ref, g_ref, o_ref, out_ref):
        out_ref[...] = g_ref[...] + o_ref[...]

    return pl.pallas_call(
        body, name=name, out_shape=jax.ShapeDtypeStruct(other.shape, other.dtype),
        grid_spec=pltpu.PrefetchScalarGridSpec(
            num_scalar_prefetch=1, grid=(4, rows // tm),
            in_specs=[pl.BlockSpec((None, None, tm, cols), lambda p, i, c_ref: (c_ref[0], p, i, 0)),
                      pl.BlockSpec((None, tm, cols), lambda p, i, c_ref: (p, i, 0))],
            out_specs=pl.BlockSpec((None, tm, cols), lambda p, i, c_ref: (p, i, 0))),
        compiler_params=_params(("parallel", "parallel")),
    )(core, g, other)


def _adamw(parts, w, m, v, after, *, name):
    layers, rows, cols = w.shape
    assert len(parts) == layers
    tm = _row_tile(rows, 256)
    nr = rows // tm

    def body(*refs):
        p_refs = refs[:layers]
        w_ref, m_ref, v_ref, _, g_ref, d_ref, nm_ref, nv_ref, g_s = refs[layers:]
        for ll in range(layers):
            @pl.when(pl.program_id(0) == ll)
            def _(ll=ll):
                g = p_refs[ll][0]
                for q in range(1, p_refs[ll].shape[0]):
                    g = g + p_refs[ll][q]
                g_s[...] = g

        g = g_s[...]
        mm = ADAM_B1 * m_ref[...] + (1.0 - ADAM_B1) * g
        vv = ADAM_B2 * v_ref[...] + (1.0 - ADAM_B2) * jnp.square(g)
        m_hat = mm / (1.0 - ADAM_B1 ** ADAM_STEP)
        v_hat = vv / (1.0 - ADAM_B2 ** ADAM_STEP)
        g_ref[...] = g
        d_ref[...] = -ADAM_LR * (m_hat / (jnp.sqrt(v_hat) + ADAM_EPS) + ADAM_WD * w_ref[...])
        nm_ref[...] = mm
        nv_ref[...] = vv

    def part_spec(ll, p):
        return pl.BlockSpec((p, tm, cols), lambda l, i: (0, jnp.where(l == ll, i, jnp.where(l < ll, 0, nr - 1)), 0))

    spec = pl.BlockSpec((None, tm, cols), lambda l, i: (l, i, 0))
    out = jax.ShapeDtypeStruct(w.shape, F32)
    return pl.pallas_call(
        body, name=name, grid=(layers, nr),
        in_specs=[part_spec(ll, parts[ll].shape[0]) for ll in range(layers)] + [spec] * 3 + [pl.BlockSpec(memory_space=pl.ANY)],
        out_specs=[spec] * 4, out_shape=[out] * 4, scratch_shapes=[pltpu.VMEM((tm, cols), F32)],
        compiler_params=_params(("arbitrary", "arbitrary")),
    )(*parts, w, m, v, after)


def _pack(arrs):
    flat = jnp.concatenate([a.reshape(-1) for a in arrs])
    rows = -(-flat.shape[0] // (PACK_COLS * 16)) * 16
    return jnp.pad(flat, (0, rows * PACK_COLS - flat.shape[0])).reshape(rows, PACK_COLS)


def _unpack(packed, shapes):
    flat = packed.reshape(-1)
    out, off = [], 0
    for shp in shapes:
        n = int(np.prod(shp))
        out.append(flat[off:off + n].reshape(shp))
        off += n
    return out


def _shards_to_full(g, axis):
    if axis == 0:
        return g.reshape(g.shape[0] * g.shape[1], g.shape[2])
    return jnp.transpose(g, (1, 0, 2)).reshape(g.shape[1], g.shape[0] * g.shape[2])


def _full_to_shards(a, axis):
    if axis == 0:
        return jnp.transpose(a.reshape(4, 2, a.shape[0] // N_DEV, a.shape[1]), (1, 0, 2, 3))
    return jnp.transpose(a.reshape(a.shape[0], 4, 2, a.shape[1] // N_DEV), (2, 1, 0, 3))


def _zeros_like_cols(a, n):
    return jnp.zeros(a.shape[:-1] + (n,), a.dtype)


def _pad_heads(a, n):
    z = _zeros_like_cols(a, HEAD)
    return jnp.concatenate([p for h in range(n) for p in (a[..., h * HEAD:(h + 1) * HEAD], z)], axis=-1)


def _unpad_heads(a, n):
    return jnp.concatenate([a[..., h * LANES:h * LANES + HEAD] for h in range(n)], axis=-1)


def _seg(first, width, sign=1):
    return (width, [(first, sign)])


def _zero(width):
    return (width, [])


def _swapped(first):
    half = MLA_ROPE // 2
    return [_seg(first + half, half, -1), _seg(first, half)]


def _padded_heads(first, n):
    return [s for h in range(n) for s in (_seg(first + HEAD * h, HEAD), _zero(HEAD))]


def _layout_w_in():
    kr = 384
    return (_padded_heads(1440, 4) + [_seg(0, 256), _seg(416, 256), _seg(672, 256), _seg(928, 256), _seg(1184, 256)]
            + _padded_heads(1696, 2) + _padded_heads(1824, 2) + [_seg(256, 128)]
            + [_zero(HEAD), _seg(kr, MLA_ROPE), _seg(kr, MLA_ROPE)] + [_zero(HEAD)] + _swapped(kr) + _swapped(kr))


def _layout_w_uq():
    out = []
    for h in range(MLA_HEADS):
        out += [_seg(96 * h, MLA_NOPE), _seg(96 * h + MLA_NOPE, MLA_ROPE)] + _swapped(96 * h + MLA_NOPE)
    return out


def _layout_w_ukv():
    keys = [s for h in range(MLA_HEADS) for s in (_seg(LANES * h, HEAD), _zero(HEAD))]
    values = [s for h in range(MLA_HEADS) for s in (_seg(LANES * h + HEAD, HEAD), _zero(HEAD))]
    return keys + values


def _layout_w_gate_up():
    return [_seg(half + j, GU_TILE) for j in range(0, D_FF, GU_TILE) for half in (0, D_FF)]


LAYOUTS = dict(w_in=_layout_w_in(), w_uq=_layout_w_uq(), w_ukv=_layout_w_ukv(), w_gate_up=_layout_w_gate_up())
OWN_COLS = dict(w_in=1952, w_uq=384, w_ukv=512, w_gate_up=2 * D_FF)


def _plan_extend(layout, shard):
    plan = []
    for width, terms in layout:
        if not terms:
            plan.append((width, []))
            continue
        (first, sign), = terms
        while width:
            g, off = divmod(first, shard)
            w = min(width, shard - off)
            plan.append((w, [(g, off, sign)]))
            first, width = first + w, width - w
    return [plan]


def _plan_fold(layout, own_cols):
    sources = [[] for _ in range(own_cols)]
    e = 0
    for width, terms in layout:
        for first, sign in terms:
            for i in range(width):
                sources[first + i].append((e + i, sign))
        e += width
    shard = own_cols // N_DEV
    plans = {}
    for g in range(N_DEV):
        plan, n = [], g * shard
        while n < (g + 1) * shard:
            w = 1
            while n + w < (g + 1) * shard and [(c + w, sg) for c, sg in sources[n]] == sources[n + w]:
                w += 1
            plan.append((w, [(0, c, sg) for c, sg in sources[n]]))
            n += w
        plans[g] = plan
    return [plans[2 * p + c] for c in range(2) for p in range(4)]


def _assemble(src, plans, out_cols, out_dtype, *, name):
    g, rows, c = src.shape
    tm = _row_tile(rows, 256)

    def body(s_ref, o_ref):
        blocks = [s_ref[i].astype(F32) for i in range(g)]
        for d, plan in enumerate(plans):
            pieces = []
            for width, terms in plan:
                v = None
                for b, first, sign in terms:
                    t = blocks[b][:, first:first + width]
                    t = -t if sign < 0 else t
                    v = t if v is None else v + t
                pieces.append(jnp.zeros((tm, width), F32) if v is None else v)
            o_ref[d] = (pieces[0] if len(pieces) == 1 else jnp.concatenate(pieces, axis=1)).astype(o_ref.dtype)

    return pl.pallas_call(
        body, name=name, grid=(rows // tm,), in_specs=[pl.BlockSpec((g, tm, c), lambda i: (0, i, 0))],
        out_specs=pl.BlockSpec((len(plans), tm, out_cols), lambda i: (0, i, 0)),
        out_shape=jax.ShapeDtypeStruct((len(plans), rows, out_cols), out_dtype), compiler_params=_params(("parallel",)),
    )(src)


def _extend(nm, gathered, *, name):
    layout = LAYOUTS[nm]
    return _assemble(gathered, _plan_extend(layout, OWN_COLS[nm] // N_DEV), sum(w for w, _ in layout), BF16, name=name)[0]


def _fold_to_shards(nm, grad_ext, *, name):
    shards = _assemble(grad_ext[None], _plan_fold(LAYOUTS[nm], OWN_COLS[nm]), OWN_COLS[nm] // N_DEV, F32, name=name)
    return shards.reshape((2, 4) + shards.shape[1:])


def _rope_tables(s):
    inv = 1.0 / (ROPE_THETA ** (jnp.arange(0, MLA_ROPE, 2, dtype=F32) / MLA_ROPE))
    ang = jnp.arange(s, dtype=F32)[:, None] * inv[None, :]
    cos, sin = jnp.cos(ang), jnp.sin(ang)
    c32, s32 = jnp.concatenate([cos, cos], axis=1), jnp.concatenate([sin, sin], axis=1)
    zeros, ones = jnp.zeros((s, HEAD), F32), jnp.ones((s, HEAD), F32)
    tq = jnp.concatenate([ones, c32, s32], axis=1) * (1.0 / math.sqrt(MLA_NOPE + MLA_ROPE))
    return (jnp.tile(tq, (1, MLA_HEADS)), jnp.concatenate([zeros, c32, c32], axis=1),
            jnp.concatenate([zeros, s32, s32], axis=1))


def _gn(y):
    return y * _rstd(y, D_GROUP)


def _mixer_fwd(x, w, tabs, l, after_attention=None):
    s = x.shape[0]
    tq, tkc, tks = tabs
    n = lambda t: f"l{l}_{t}"
    h = _rms_fwd(("row", x, D_MODEL, 0), w["attn_norm"], name=n("attn_norm"), rows=s, width=D_MODEL)
    proj = _mm(h, w["w_in"], name=n("in_proj"))
    def prep(cq, ckv, kr, krp, gq, gkv, wuq, wukv, tqv, tc, ts):
        cqn = (cq * _rstd(cq, 256) * gq).astype(BF16)
        ckvn = (ckv * _rstd(ckv, 128) * gkv).astype(BF16)
        qe = jnp.dot(cqn, wuq, preferred_element_type=F32)
        kve = jnp.dot(ckvn, wukv, preferred_element_type=F32)
        kb = kr * tc + krp * ts
        kvv = kve[:, 512:]
        lane = lax.broadcasted_iota(jnp.int32, kvv.shape, 1) & (LANES - 1)
        v = jnp.where(lane == HEAD, 1.0, kvv)
        k = kve[:, :512] + jnp.tile(kb, (1, MLA_HEADS))
        return cqn, ckvn, qe * tqv, k, k, v, v

    cqn, ckvn, qm, km, kmt, vm, vmt = _rowwise(
        prep, name=n("mla_prep"), rows=s,
        ins=[("row", proj, 256, C_CQ[0] // 256), ("row", proj, 128, C_CKV[0] // 128),
             ("row", proj, 128, C_KR[0] // 128), ("row", proj, 128, C_KRP[0] // 128),
             ("full", w["mla_q_norm"]), ("full", w["mla_kv_norm"]), ("full", w["w_uq"]), ("full", w["w_ukv"]),
             ("row", tq, 512, 0), ("row", tkc, 128, 0), ("row", tks, 128, 0)],
        outs=[(256, BF16), (128, BF16), (512, BF16, MLA_HEADS), (512, BF16, MLA_HEADS), (512, BF16, MLA_HEADS, "T"),
              (512, BF16, MLA_HEADS), (512, BF16, MLA_HEADS, "T")])
    y_a, lse_row = _mla_fwd(qm, km, vmt, name=n("mla_fwd"), s=s, nh=MLA_HEADS)
    mix_norm = w["mix_norm"]
    if after_attention is not None:
        mix_norm = mix_norm + after_attention(y_a)[0, 0]
    y_b = _conv_fwd(proj, w["conv_w"], name=n("conv_fwd"), s=s)
    y_c = _pool_fwd(proj, w["pool_wbd"], w["pool_scale"], name=n("pool_fwd"), s=s)
    y_d, lse_d = _swa_fwd(proj, w["hp_swa"], name=n("swa_fwd"), s=s, scale=1.0 / math.sqrt(HEAD))

    def mix(ya, yb, yc, yd, mn):
        return (jnp.concatenate([_gn(_unpad_heads(ya, 4)), _gn(yb), _gn(yc), _gn(_unpad_heads(yd, 4))], axis=1) * mn,)

    mixed = _rowwise(mix, name=n("group_norm"), rows=s,
                     ins=[("heads", y_a), ("row", y_b, 256, 0), ("row", y_c, 256, 0), ("row", y_d, 512, 0),
                          ("full", mix_norm)], outs=[(D_MODEL, BF16)])[0]
    x1 = _mm(mixed, w["w_o"], res=x, name=n("out_proj"))
    saved = dict(x=x, h=h, proj=proj, cqn=cqn, ckvn=ckvn, qm=qm, km=km, kmt=kmt, vm=vm, y_a=y_a, lse_row=lse_row,
                 y_b=y_b, y_c=y_c, y_d=y_d, lse_d=lse_d, mixed=mixed)
    return x1, saved


def _ffn_fwd(x1, w, l):
    s = x1.shape[0]
    n = lambda t: f"l{l}_{t}"
    h2 = _rms_fwd(("row", x1, D_MODEL, 0), w["ffn_norm"], name=n("ffn_norm"), rows=s, width=D_MODEL)

    def swiglu(gu):
        g, u = gu[:, :GU_TILE], gu[:, GU_TILE:]
        return gu, g * jax.nn.sigmoid(g) * u

    gu, act = _mm(h2, w["w_gate_up"], tm=2048, tn=2 * GU_TILE, name=n("gate_up"),
                  epilogue=(swiglu, [], [(2 * D_FF, BF16), (D_FF, BF16)]))
    x2 = _mm(act, w["w_down"], res=x1, tk=D_FF // 2, name=n("down"))
    return x2, dict(x1=x1, h2=h2, gu=gu, act=act)


def _ffn_bwd_down(dx2, sv, w, l):
    n = lambda t: f"l{l}_{t}"

    def swiglu_bwd(da, gu):
        gt, u = gu[:, :GU_TILE].astype(F32), gu[:, GU_TILE:].astype(F32)
        sg = jax.nn.sigmoid(gt)
        return (jnp.concatenate([da * u * sg * (1.0 + gt * (1.0 - sg)), da * gt * sg], axis=1),)

    dgu = _mm(dx2[1], w["w_down"], tb=True, tm=2048, tn=GU_TILE, name=n("d_act"),
              epilogue=(swiglu_bwd, [sv["gu"]], [(2 * D_FF, BF16)]))[0]
    g = dict(w_down=_mm(sv["act"], dx2[1], ta=True, tm=D_FF // 2, name=n("dw_down")))
    return dgu, g


def _ffn_bwd_up(dx2, dgu, sv, w, l):
    s = dgu.shape[0]
    n = lambda t: f"l{l}_{t}"
    dh2 = _mm(dgu, w["w_gate_up"], tb=True, tm=2048, tk=D_FF // 2, name=n("d_h2"))
    g = dict(w_gate_up=_mm(sv["h2"], dgu, ta=True, tn=D_FF // 2, name=n("dw_gate_up")))
    dx1, dx1_b, g["ffn_norm"] = _rms_bwd(("row", sv["x1"], D_MODEL, 0), w["ffn_norm"], dh2, dx2[0],
                                         name=n("ffn_norm_bwd"), rows=s, width=D_MODEL, out_dtypes=(F32, BF16))
    return (dx1, dx1_b), g


def _mixer_bwd_out(dx1, sv, w, l):
    s = dx1[1].shape[0]
    n = lambda t: f"l{l}_{t}"
    dmixed = _mm(dx1[1], w["w_o"], tb=True, name=n("d_mixed"))
    g = dict(w_o=_mm(sv["mixed"], dx1[1], ta=True, name=n("dw_o")))

    def mix_bwd(dm, ya, yb, yc, yd, mn):
        outs, dmn = [], []
        for i, y in enumerate((_unpad_heads(ya, 4), yb, yc, _unpad_heads(yd, 4))):
            lo, hi = i * D_GROUP, (i + 1) * D_GROUP
            r = _rstd(y, D_GROUP)
            nrm = y * r
            dmg = dm[:, lo:hi]
            dn = dmg * mn[:, lo:hi]
            dy = r * (dn - nrm * (jnp.sum(dn * nrm, axis=-1, keepdims=True) * (1.0 / D_GROUP)))
            outs.append(_pad_heads(dy, 4) if i in (0, 3) else dy)
            dmn.append(jnp.sum(dmg * nrm, axis=0, keepdims=True))
        return (*outs, jnp.concatenate(dmn, axis=1))

    dy_a, dy_b, dy_c, dy_d, g["mix_norm"] = _rowwise(
        mix_bwd, name=n("group_norm_bwd"), rows=s,
        ins=[("row", dmixed, D_MODEL, 0), ("heads", sv["y_a"]), ("row", sv["y_b"], 256, 0),
             ("row", sv["y_c"], 256, 0), ("row", sv["y_d"], 512, 0), ("full", w["mix_norm"])],
        outs=[(512, F32, MLA_HEADS), (256, F32), (256, F32), (512, F32)], reds=[(1, D_MODEL)])
    return (dy_a, dy_b, dy_c, dy_d), g


def _mixer_bwd_in(dx1, dys, sv, w, tabs, l):
    s = dx1[0].shape[0]
    tq, tkc, tks = tabs
    n = lambda t: f"l{l}_{t}"
    dy_a, dy_b, dy_c, dy_d = dys
    g = {}

    proj = sv["proj"]
    dq_sw, dk_sw, dv_sw, dsink = _swa_bwd(proj, sv["y_d"], dy_d, sv["lse_d"], w["hp_swa"], name=n("swa_bwd"), s=s,
                                          scale=1.0 / math.sqrt(HEAD))
    g["swa_sinks"] = dsink[:, 0, 0]

    dqm, dkm, dvm = _mla_bwd(sv["qm"], sv["km"], sv["kmt"], sv["vm"], sv["y_a"], dy_a, sv["lse_row"], w["hp_swa"],
                             name=n("mla_bwd"), s=s, nh=MLA_HEADS)

    def rms_bwd(x, gv, dy, width):
        r = _rstd(x, width)
        dyg = dy * gv
        dx = r * dyg - x * (r * r * r) * (jnp.sum(dyg * x, axis=-1, keepdims=True) * (1.0 / width))
        return dx, jnp.sum(dy * x * r, axis=0, keepdims=True)

    def prep_bwd(dq, dk, dv, cq, ckv, gq, gkv, wuq, wukv, tqv, tc, ts):
        dkb = dk[:, 0:128] + dk[:, 128:256] + dk[:, 256:384] + dk[:, 384:512]
        dq_ext = (dq * tqv).astype(BF16)
        dkv_ext = jnp.concatenate([dk.astype(BF16), dv], axis=1)
        dcqn = lax.dot_general(dq_ext, wuq, NT_DIMS, preferred_element_type=F32)
        dckvn = lax.dot_general(dkv_ext, wukv, NT_DIMS, preferred_element_type=F32)
        dcq, dgq = rms_bwd(cq, gq, dcqn, 256)
        dckv, dgkv = rms_bwd(ckv, gkv, dckvn, 128)
        return dq_ext, dkv_ext, dkb * tc, dkb * ts, dcq, dckv, dgq, dgkv

    dq_ext, dkv_ext, dkr, dkrp, dcq, dckv, g["mla_q_norm"], g["mla_kv_norm"] = _rowwise(
        prep_bwd, name=n("mla_prep_bwd"), rows=s,
        ins=[("heads", dqm), ("heads", dkm), ("heads", dvm), ("row", proj, 256, C_CQ[0] // 256),
             ("row", proj, 128, C_CKV[0] // 128), ("full", w["mla_q_norm"]), ("full", w["mla_kv_norm"]),
             ("full", w["w_uq"]), ("full", w["w_ukv"]), ("row", tq, 512, 0), ("row", tkc, 128, 0), ("row", tks, 128, 0)],
        outs=[(512, BF16), (1024, BF16), (128, BF16), (128, BF16), (256, BF16), (128, BF16)],
        reds=[(1, 256), (1, 128)])
    g["w_uq"] = _mm(sv["cqn"], dq_ext, ta=True, name=n("dw_uq"))
    g["w_ukv"] = _mm(sv["ckvn"], dkv_ext, ta=True, name=n("dw_ukv"))

    dgb, dgc, duc, g["conv_w"] = _conv_bwd(dy_b, proj, w["conv_w"], name=n("conv_bwd"), s=s)
    dup, g["pool_wbd"], g["pool_scale"] = _pool_bwd(dy_c, proj, w["pool_wbd"], w["pool_scale"], name=n("pool_bwd"), s=s)

    dproj = jnp.concatenate([dq_sw, dcq, dgb, dgc, duc, dup, dk_sw.astype(BF16), dv_sw.astype(BF16), dckv, dkr, dkrp],
                            axis=1)
    g["w_in"] = _mm(sv["h"], dproj, ta=True, name=n("dw_in"))
    return dproj, g


def _mixer_bwd_norm(dx1, dproj, sv, w, l):
    n = lambda t: f"l{l}_{t}"
    dh = _mm(dproj, w["w_in"], tb=True, tm=2048, name=n("d_h"))
    dx0, dx0_b, dg = _rms_bwd(("row", sv["x"], D_MODEL, 0), w["attn_norm"], dh, dx1[0], name=n("attn_norm_bwd"),
                              rows=dh.shape[0], width=D_MODEL, out_dtypes=(F32, BF16))
    return (dx0, dx0_b), dict(attn_norm=dg)


def _loss_head(x, target, g, *, s):
    def fn(xv, tv, gv):
        r = _rstd(xv, D_MODEL)
        e = xv * r * gv - tv
        part = jnp.sum(jnp.sum(e * e, axis=1, keepdims=True), axis=0, keepdims=True) * (0.5 / D_MODEL)
        dy = e * (1.0 / D_MODEL)
        dyg = dy * gv
        dx = r * dyg - xv * (r * r * r) * (jnp.sum(dyg * xv, axis=-1, keepdims=True) * (1.0 / D_MODEL))
        return dx, dx, jnp.sum(dy * xv * r, axis=0, keepdims=True), jnp.broadcast_to(part, (1, LANES))

    return _rowwise(fn, name="loss_head", rows=s,
                    ins=[("row", x, D_MODEL, 0), ("row", target, D_MODEL, 0), ("full", g)],
                    outs=[(D_MODEL, F32), (D_MODEL, BF16)], reds=[(1, D_MODEL), (1, LANES)])


def _alibi_slopes(n):
    return np.asarray([2.0 ** (-8.0 * (i + 1) / n) for i in range(n)], dtype=np.float32)


MIXER_WEIGHTS = ("w_in", "w_uq", "w_ukv", "conv_w", "w_o")
FFN_WEIGHTS = ("w_gate_up", "w_down")


def _mixer_weights(full, rep, l):
    pw = rep["pool_w"][l]
    z = jnp.zeros((HEAD, HEAD), F32)
    wbd = jnp.stack([jnp.block([[pw[2 * j], z], [z, pw[2 * j + 1]]]) for j in range(2)])
    return dict(
        attn_norm=rep["attn_norm"][l][None], w_in=full["w_in"], mla_q_norm=rep["mla_q_norm"][l][None],
        w_uq=full["w_uq"], mla_kv_norm=rep["mla_kv_norm"][l][None], w_ukv=full["w_ukv"],
        conv_w=full["conv_w"], pool_wbd=wbd, pool_scale=rep["pool_scale"][l][None],
        mix_norm=rep["mix_norm"][l][None], w_o=full["w_o"],
        hp_swa=jnp.stack([jnp.asarray(_alibi_slopes(SWA_HEADS)), rep["swa_sinks"][l]], axis=1))


def _ffn_weights(full, rep, l):
    return dict(ffn_norm=rep["ffn_norm"][l][None], w_gate_up=full["w_gate_up"], w_down=full["w_down"])


def _small_grads(g):
    rows = ("attn_norm", "mla_q_norm", "mla_kv_norm", "pool_scale", "ffn_norm", "mix_norm")
    out = {nm: g[nm][0] for nm in rows if nm in g}
    if "swa_sinks" in g:
        out["swa_sinks"] = g["swa_sinks"]
    if "pool_wbd" in g:
        e = g["pool_wbd"]
        out["pool_w"] = jnp.stack([e[j // 2][HEAD * (j % 2):HEAD * (j % 2 + 1), HEAD * (j % 2):HEAD * (j % 2 + 1)]
                                   for j in range(4)])
    return out


def kernel(x, attn_norm, w_in, mla_q_norm, w_uq, mla_kv_norm, w_ukv, conv_w, pool_w, pool_scale, swa_sinks, mix_norm, w_o, ffn_norm, w_gate_up, w_down, final_norm, loss_target, m_attn_norm, m_w_in, m_mla_q_norm, m_w_uq, m_mla_kv_norm, m_w_ukv, m_conv_w, m_pool_w, m_pool_scale, m_swa_sinks, m_mix_norm, m_w_o, m_ffn_norm, m_w_gate_up, m_w_down, m_final_norm, v_attn_norm, v_w_in, v_mla_q_norm, v_w_uq, v_mla_kv_norm, v_w_ukv, v_conv_w, v_pool_w, v_pool_scale, v_swa_sinks, v_mix_norm, v_w_o, v_ffn_norm, v_w_gate_up, v_w_down, v_final_norm):
    given = dict(locals())
    sh_names = [nm for nm, _, _ in SHARDED]
    sh_axis = {nm: ax - 1 for nm, _, ax in SHARDED}
    rep_names = [nm for nm, _ in REPLICATED]
    rep_shapes = [shp for _, shp in REPLICATED]
    rep = {nm: given[nm] for nm in rep_names if nm != "loss"}
    me = 4 * lax.axis_index("x") + 2 * lax.axis_index("y") + lax.axis_index("c")
    my_chip = 2 * lax.axis_index("x") + lax.axis_index("y")
    core = lax.axis_index("c").astype(jnp.int32).reshape(1)

    def behind(token, a):
        return a + token[0, 0].astype(a.dtype)

    def wire(nm, l):
        if nm == "conv_w":
            return lax.bitcast_convert_type(given[nm][l], BF16).reshape(3, -1)
        return given[nm][l].astype(BF16)

    def whole(nm, g, tag):
        if nm in LAYOUTS:
            return _extend(nm, g, name=f"extend_{nm}_{tag}")
        if nm == "conv_w":
            g = lax.bitcast_convert_type(g.reshape(N_DEV, 3, -1, 2), F32)
        return _shards_to_full(g, sh_axis[nm])

    def near_start(names, l, after, tag):
        srcs = [wire(nm, l) for nm in names]
        return _start_copies(_plan_gather_near, 4, srcs, [(N_DEV,) + a.shape for a in srcs], after, name=f"start_gather_{tag}")

    def pass_on(handle, after, tag):
        srcs, lands = _wait_copies(_plan_gather_near, handle, after, name=f"wait_gather_{tag}")
        handle, token = _start_copies(_plan_gather_pass, 3, [], lands, [], name=f"start_pass_{tag}")
        return (srcs, handle), token

    def gathered(names, state, after, tag):
        srcs, handle = state
        _, lands = _wait_copies(_plan_gather_pass, handle, after, name=f"wait_pass_{tag}")
        return {nm: whole(nm, lax.dynamic_update_index_in_dim(land, src, me, 0), tag)
                for nm, src, land in zip(names, srcs, lands)}

    layer1 = MIXER_WEIGHTS + FFN_WEIGHTS
    got = _all_gather([wire(nm, 0) for nm in MIXER_WEIGHTS], name="gather_mixer0")
    full_m0 = {nm: whole(nm, g, "mixer0") for nm, g in zip(MIXER_WEIGHTS, got)}
    h_f0, tok = near_start(FFN_WEIGHTS, 0, [], "ffn0")
    h_l1, tok = near_start(layer1, 1, [tok], "layer1")

    xs, target = x[0], loss_target[0]
    s = xs.shape[0]
    tabs = _rope_tables(s)
    wm, wf, svm, svf = [None] * DEPTH, [None] * DEPTH, [None] * DEPTH, [None] * DEPTH
    wm[0] = _mixer_weights(full_m0, rep, 0)
    wm[0]["attn_norm"] = behind(tok, wm[0]["attn_norm"])
    passing = {}

    def pass_ffn0(y_a):
        passing["ffn0"], token = pass_on(h_f0, [y_a], "ffn0")
        return token

    x1, svm[0] = _mixer_fwd(xs, wm[0], tabs, 0, after_attention=pass_ffn0)
    wf[0] = _ffn_weights(gathered(FFN_WEIGHTS, passing["ffn0"], [x1], "ffn0"), rep, 0)
    passing["layer1"], tok = pass_on(h_l1, [x1], "layer1")
    wf[0]["ffn_norm"] = behind(tok, wf[0]["ffn_norm"])
    x2, svf[0] = _ffn_fwd(x1, wf[0], 0)
    full_1 = gathered(layer1, passing["layer1"], [x2], "layer1")
    wm[1], wf[1] = _mixer_weights(full_1, rep, 1), _ffn_weights(full_1, rep, 1)
    x1, svm[1] = _mixer_fwd(x2, wm[1], tabs, 1)
    x2, svf[1] = _ffn_fwd(x1, wf[1], 1)
    dx_f, dx_b, d_final, loss = _loss_head(x2, target, rep["final_norm"][None], s=s)
    dx = (dx_f, dx_b)

    parts = {}

    def reduce_start(grads, l, after, tag):
        names = [nm for nm in sh_names if nm in grads]
        mine = [_fold_to_shards(nm, grads[nm], name=f"fold_{nm}_{l}") if nm in LAYOUTS
                else _full_to_shards(grads[nm], sh_axis[nm]) for nm in names]
        handle, token = _start_copies(_plan_sibling, 1, mine, [m.shape[1:] for m in mine], after, name=f"start_sibling_{tag}")
        return (names, l, handle), token

    def reduce_mid(state, after, tag):
        names, l, handle = state
        mine, theirs = _wait_copies(_plan_sibling, handle, after, name=f"wait_sibling_{tag}")
        sums = [_add_own(g, o, core, name=f"chip_sum_{nm}_{l}") for nm, g, o in zip(names, mine, theirs)]
        handle, token = _start_copies(_plan_chips, 3, sums, [a.shape for a in sums], [], name=f"start_chips_{tag}")
        return (names, l, handle), token

    def reduce_end(state, after, tag):
        names, l, handle = state
        sums, lands = _wait_copies(_plan_chips, handle, after, name=f"wait_chips_{tag}")
        for nm, own, land in zip(names, sums, lands):
            parts[nm, l] = lax.dynamic_update_index_in_dim(land, lax.dynamic_index_in_dim(own, my_chip, 0, keepdims=False),
                                                           my_chip, 0)

    small = [None] * DEPTH
    in_flight = []
    pending = None
    for l in reversed(range(DEPTH)):
        dgu, g_down = _ffn_bwd_down(dx, svf[l], wf[l], l)
        if pending is not None:
            state, token = reduce_mid(pending, [dgu], f"mixer{l + 1}")
            in_flight.append((state, f"mixer{l + 1}"))
            wf[l]["ffn_norm"] = behind(token, wf[l]["ffn_norm"])
        dx1, g_up = _ffn_bwd_up(dx, dgu, svf[l], wf[l], l)
        g_ffn = {**g_down, **g_up}
        state, token = reduce_start(g_ffn, l, [], f"ffn{l}")
        wm[l]["mix_norm"] = behind(token, wm[l]["mix_norm"])
        dys, g_out = _mixer_bwd_out(dx1, svm[l], wm[l], l)
        state, token = reduce_mid(state, [dys[0]], f"ffn{l}")
        in_flight.append((state, f"ffn{l}"))
        wm[l]["hp_swa"] = behind(token, wm[l]["hp_swa"])
        dproj, g_in = _mixer_bwd_in(dx1, dys, svm[l], wm[l], tabs, l)
        g_mixer = {**g_out, **g_in}
        pending, token = reduce_start(g_mixer, l, [], f"mixer{l}")
        wm[l]["attn_norm"] = behind(token, wm[l]["attn_norm"])
        dx, g_norm = _mixer_bwd_norm(dx1, dproj, svm[l], wm[l], l)
        small[l] = _small_grads({**g_ffn, **g_mixer, **g_norm})
    grads = {nm: jnp.stack([small[l][nm] for l in range(DEPTH)]) for nm in rep_names if nm in small[0]}
    grads["final_norm"] = d_final[0]
    grads["loss"] = loss[0, :1]
    zero = jnp.zeros((1,), F32)
    small = _all_gather([behind(token, _pack([grads[nm] for nm in rep_names]))], name="gather_small_grads")

    last, token = reduce_mid(pending, [dx[0], small[0]], "mixer0")
    for state, tag in in_flight:
        reduce_end(state, [], tag)
    grad_x = dx[0]

    def adamw(nm, after):
        return _adamw([parts[nm, l] for l in range(DEPTH)], given[nm], given["m_" + nm], given["v_" + nm], after,
                      name=f"adamw_{nm}")

    sh_out = {nm: adamw(nm, token) for nm in FFN_WEIGHTS}
    packs = [_pack([given.get(pre + nm, zero) for nm in rep_names])[None] for pre in ("", "m_", "v_")]
    rep_res = _adamw(small, *packs, token, name="adamw_replicated")
    rep_out = [dict(zip(rep_names, _unpack(o[0], rep_shapes))) for o in rep_res]

    reduce_end(last, [rep_res[0], sh_out["w_down"][0]], "mixer0")
    sh_out.update({nm: adamw(nm, token) for nm in MIXER_WEIGHTS})

    out = [rep_out[0]["loss"][0], grad_x[None]]
    for i in range(4):
        out += [sh_out[nm][i] if nm in sh_axis else rep_out[i][nm] for nm in WEIGHT_ORDER]
    return tuple(out)
```

```python
import functools
import math

import numpy as np
import jax
import jax.numpy as jnp
from jax import lax
from jax.experimental import pallas as pl
from jax.experimental.pallas import tpu as pltpu

F32 = jnp.float32
BF16 = jnp.bfloat16

D_MODEL = 1024
DEPTH = 2
D_GROUP = 256
MLA_HEADS = 4
MLA_NOPE = 64
MLA_ROPE = 32
ROPE_THETA = 10000.0
POOL_WINDOWS = (2, 4, 8, 16)
SWA_HEADS = 4
SWA_KV_HEADS = 2
SWA_WINDOW = 128
D_FF = 2816
GU_TILE = 256
RMS_EPS = 1e-6
LANES = 128
HEAD = 64
VMEM_LIMIT = 48 * 1024 * 1024
NEG = -1e30

ADAM_LR = 0.001
ADAM_B1 = 0.9
ADAM_B2 = 0.999
ADAM_EPS = 1e-08
ADAM_WD = 0.01
ADAM_STEP = 10

N_DEV = 8
PACK_COLS = 1024

C_QSW, C_CQ, C_GB, C_GC, C_UCONV, C_UPOOL = (0, 512), (512, 256), (768, 256), (1024, 256), (1280, 256), (1536, 256)
C_KSW, C_VSW, C_CKV, C_KR, C_KRP = (1792, 256), (2048, 256), (2304, 128), (2432, 128), (2560, 128)
D_IN_EXT = 2688

SHARDED = (("w_in", (DEPTH, 1024, 244), 2), ("w_uq", (DEPTH, 256, 48), 2), ("w_ukv", (DEPTH, 128, 64), 2),
           ("conv_w", (DEPTH, 3, 32), 2), ("w_o", (DEPTH, 128, 1024), 1), ("w_gate_up", (DEPTH, 1024, 704), 2),
           ("w_down", (DEPTH, 352, 1024), 1))
REPLICATED = (("attn_norm", (DEPTH, 1024)), ("mla_q_norm", (DEPTH, 256)), ("mla_kv_norm", (DEPTH, 128)),
              ("pool_w", (DEPTH, 4, 64, 64)), ("pool_scale", (DEPTH, 256)), ("swa_sinks", (DEPTH, 4)),
              ("mix_norm", (DEPTH, 1024)), ("ffn_norm", (DEPTH, 1024)), ("final_norm", (1024,)), ("loss", (1,)))
WEIGHT_ORDER = ("attn_norm", "w_in", "mla_q_norm", "w_uq", "mla_kv_norm", "w_ukv", "conv_w", "pool_w", "pool_scale",
                "swa_sinks", "mix_norm", "w_o", "ffn_norm", "w_gate_up", "w_down", "final_norm")


def _params(sem):
    return pltpu.CompilerParams(dimension_semantics=sem, vmem_limit_bytes=VMEM_LIMIT)


def _pick(dim, target):
    if dim <= target:
        return dim
    best = None
    for t in range(LANES, target + 1, LANES):
        if dim % t == 0:
            best = t
    assert best is not None, (dim, target)
    return best


def _mm(a, b, *, name, ta=False, tb=False, res=None, out_dtype=F32, tm=1024, tn=1024, tk=1024, epilogue=None):
    m, k = (a.shape[1], a.shape[0]) if ta else a.shape
    n = b.shape[0] if tb else b.shape[1]
    assert (b.shape[1] if tb else b.shape[0]) == k
    tm, tn, tk = _pick(m, tm), _pick(n, tn), _pick(k, tk)
    nj, nk = n // tn, k // tk
    dims = (((0 if ta else 1,), (1 if tb else 0,)), ((), ()))
    fn, extra, outs = epilogue if epilogue is not None else (None, [], [(n, out_dtype)])
    if res is not None:
        assert epilogue is None
        fn, extra = (lambda acc, r: (acc + r,)), [res]
    n_in, n_out = 2 + len(extra), len(outs)

    def body(*refs):
        a_ref, b_ref, acc_ref = refs[0], refs[1], refs[-1]
        kk = pl.program_id(2)

        @pl.when(kk == 0)
        def _():
            acc_ref[...] = jnp.zeros_like(acc_ref)

        acc_ref[...] += lax.dot_general(a_ref[...].astype(BF16), b_ref[...].astype(BF16), dims,
                                        preferred_element_type=F32)

        @pl.when(kk == nk - 1)
        def _():
            tiles = (acc_ref[...],) if fn is None else fn(acc_ref[...], *[r[...] for r in refs[2:n_in]])
            for o_ref, tile in zip(refs[n_in:n_in + n_out], tiles):
                o_ref[...] = tile.astype(o_ref.dtype)

    def col_tiles(width):
        assert width % (nj * LANES) == 0, (width, nj)
        return pl.BlockSpec((tm, width // nj), lambda i, j, kk: (i, j))

    a_spec = pl.BlockSpec((tk, tm), lambda i, j, kk: (kk, i)) if ta else pl.BlockSpec((tm, tk), lambda i, j, kk: (i, kk))
    b_spec = pl.BlockSpec((tn, tk), lambda i, j, kk: (j, kk)) if tb else pl.BlockSpec((tk, tn), lambda i, j, kk: (kk, j))
    res_ = pl.pallas_call(
        body, name=name, grid=(m // tm, nj, nk), in_specs=[a_spec, b_spec] + [col_tiles(e.shape[1]) for e in extra],
        out_specs=[col_tiles(w) for w, _ in outs],
        out_shape=[jax.ShapeDtypeStruct((m, w), dt) for w, dt in outs],
        scratch_shapes=[pltpu.VMEM((tm, tn), F32)],
        compiler_params=_params(("parallel", "parallel", "arbitrary")),
    )(a, b, *extra)
    return res_[0] if epilogue is None else res_


def _rowwise(fn, *, name, rows, ins, outs, reds=(), tm=512):
    tm = min(tm, rows)
    assert rows % tm == 0
    n_in, n_out = len(ins), len(outs)

    def body(*refs):
        vals = [jnp.concatenate([r[h] for h in range(r.shape[0])], axis=1) if spec[0] == "heads" else r[...]
                for spec, r in zip(ins, refs[:n_in])]
        res = fn(*vals)
        for out, r, v in zip(outs, refs[n_in:n_in + n_out], res[:n_out]):
            if len(out) >= 3:
                for h in range(out[2]):
                    piece = v[:, h * LANES:(h + 1) * LANES]
                    r[h] = (piece.T if len(out) == 4 else piece).astype(r.dtype)
            else:
                r[...] = v.astype(r.dtype)
        if reds:
            @pl.when(pl.program_id(0) == 0)
            def _():
                for r in refs[n_in + n_out:]:
                    r[...] = jnp.zeros_like(r)

            for r, v in zip(refs[n_in + n_out:], res[n_out:]):
                r[...] += v

    in_specs, args = [], []
    for spec in ins:
        if spec[0] == "row":
            _, arr, width, blk = spec
            in_specs.append(pl.BlockSpec((tm, width), functools.partial(lambda i, blk: (i, blk), blk=blk)))
        elif spec[0] == "heads":
            arr = spec[1]
            in_specs.append(pl.BlockSpec((arr.shape[0], tm, LANES), lambda i: (0, i, 0)))
        else:
            arr = spec[1]
            in_specs.append(pl.BlockSpec(arr.shape, functools.partial(lambda i, nd: (0,) * nd, nd=arr.ndim)))
        args.append(arr)
    def out_spec(o):
        if len(o) == 4:
            return pl.BlockSpec((o[2], LANES, tm), lambda i: (0, 0, i)), (o[2], LANES, rows)
        if len(o) == 3:
            return pl.BlockSpec((o[2], tm, LANES), lambda i: (0, i, 0)), (o[2], rows, LANES)
        return pl.BlockSpec((tm, o[0]), lambda i: (i, 0)), (rows, o[0])

    out_specs = [out_spec(o)[0] for o in outs]
    out_shape = [jax.ShapeDtypeStruct(out_spec(o)[1], o[1]) for o in outs]
    out_specs += [pl.BlockSpec((r, w), lambda i: (0, 0)) for r, w in reds]
    out_shape += [jax.ShapeDtypeStruct((r, w), F32) for r, w in reds]
    return pl.pallas_call(body, name=name, grid=(rows // tm,), in_specs=in_specs, out_specs=out_specs,
                          out_shape=out_shape, compiler_params=_params(("arbitrary",)))(*args)


def _rstd(x, n):
    return lax.rsqrt(jnp.sum(x * x, axis=-1, keepdims=True) * (1.0 / n) + RMS_EPS)


def _rms_fwd(x_spec, g, *, name, rows, width):
    def fn(x, gv):
        return (x * _rstd(x, width) * gv,)
    return _rowwise(fn, name=name, rows=rows, ins=[x_spec, ("full", g)], outs=[(width, BF16)])[0]


def _rms_bwd(x_spec, g, dy, res, *, name, rows, width, out_dtypes):
    def fn(x, gv, dyv, *rest):
        r = _rstd(x, width)
        dyg = dyv * gv
        dx = r * dyg - x * (r * r * r) * (jnp.sum(dyg * x, axis=-1, keepdims=True) * (1.0 / width))
        if rest:
            dx = dx + rest[0]
        return (dx,) * len(out_dtypes) + (jnp.sum(dyv * x * r, axis=0, keepdims=True),)

    ins = [x_spec, ("full", g), ("row", dy, width, 0)]
    if res is not None:
        ins.append(("row", res, width, 0))
    return _rowwise(fn, name=name, rows=rows, ins=ins, outs=[(width, dt) for dt in out_dtypes], reds=[(1, width)])


NT_DIMS = (((1,), (1,)), ((), ()))
TN_DIMS = (((0,), (0,)), ((), ()))
BNT_DIMS = (((2,), (2,)), ((0,), (0,)))
BNN_DIMS = (((2,), (1,)), ((0,), (0,)))


def _mla_tile(s):
    return min(512, s)


def _block_tables(pairs):
    return jnp.asarray([a for a, _ in pairs], jnp.int32), jnp.asarray([b for _, b in pairs], jnp.int32)


def _causal(shape, query_axis):
    return lax.broadcasted_iota(jnp.int32, shape, query_axis) >= lax.broadcasted_iota(jnp.int32, shape, 1 - query_axis)


def _mla_fwd(qa, ka, vta, *, name, s, nh):
    t = _mla_tile(s)
    nq = s // t

    pairs = [(i, j) for i in range(nq) for j in range(i + 1)]

    def body(qb_ref, kb_ref, q_ref, k_ref, vt_ref, o_ref, lse_ref, m_s, acc_s):
        i, j = qb_ref[pl.program_id(0)], kb_ref[pl.program_id(0)]

        @pl.when(j == 0)
        def _():
            m_s[...] = jnp.full_like(m_s, NEG)
            acc_s[...] = jnp.zeros_like(acc_s)

        def step(diag):
            sc = lax.dot_general(k_ref[...], q_ref[...], BNT_DIMS, preferred_element_type=F32)
            if diag:
                sc = jnp.where(_causal(sc.shape[1:], 1)[None], sc, NEG)
            m_prev = m_s[...]
            m_new = jnp.maximum(m_prev, jnp.max(sc, axis=1, keepdims=True))
            p = jnp.exp(sc - m_new).astype(BF16)
            acc_s[...] = (jnp.exp(m_prev - m_new) * acc_s[...]
                          + lax.dot_general(vt_ref[...], p, BNN_DIMS, preferred_element_type=F32))
            m_s[...] = m_new

        pl.when(j < i)(functools.partial(step, False))
        pl.when(j == i)(functools.partial(step, True))

        @pl.when(j == i)
        def _():
            row = lax.broadcasted_iota(jnp.int32, (LANES, t), 0)
            for h in range(nh):
                acc = acc_s[h]
                l = acc[HEAD:HEAD + 1, :]
                o_ref[h] = jnp.where(row < HEAD, acc / l, 0.0).T
                lse_ref[h] = m_s[h] + jnp.log(l)

    q_spec = pl.BlockSpec((nh, t, LANES), lambda p, qb, kb: (0, qb[p], 0))
    k_spec = pl.BlockSpec((nh, t, LANES), lambda p, qb, kb: (0, kb[p], 0))
    vt_spec = pl.BlockSpec((nh, LANES, t), lambda p, qb, kb: (0, 0, kb[p]))
    return pl.pallas_call(
        body, name=name,
        grid_spec=pltpu.PrefetchScalarGridSpec(
            num_scalar_prefetch=2, grid=(len(pairs),), in_specs=[q_spec, k_spec, vt_spec],
            out_specs=[q_spec, pl.BlockSpec((nh, 1, t), lambda p, qb, kb: (0, 0, qb[p]))],
            scratch_shapes=[pltpu.VMEM((nh, 1, t), F32), pltpu.VMEM((nh, LANES, t), F32)]),
        out_shape=[jax.ShapeDtypeStruct((nh, s, LANES), F32), jax.ShapeDtypeStruct((nh, 1, s), F32)],
        compiler_params=_params(("arbitrary",)),
    )(*_block_tables(pairs), qa, ka, vta)


def _mla_bwd(qa, ka, kta, va, o, do, lse_row, after, *, name, s, nh):
    t = _mla_tile(s)
    nq = s // t

    pairs = [(kb, j) for kb in range(nq) for j in range(kb, nq)]

    def body(kb_ref, qb_ref, q_ref, k_ref, kt_ref, v_ref, o_ref, do_ref, lse_ref, after_ref, dq_hbm, dk_ref, dv_ref,
             dqt_s, dk_s, dv_s, d_s, stage, sem):
        kb, j = kb_ref[pl.program_id(0)], qb_ref[pl.program_id(0)]
        cols = pl.ds(pl.multiple_of(j * t, t), t)

        @pl.when(j == kb)
        def _():
            dk_s[...] = jnp.zeros_like(dk_s)
            dv_s[...] = jnp.zeros_like(dv_s)

        @pl.when(kb == 0)
        def _():
            dqt_s[:, :, cols] = jnp.zeros((nh, LANES, t), F32)
            for h in range(nh):
                d_col = jnp.sum(do_ref[h] * o_ref[h], axis=1, keepdims=True)
                d_s[h, :, cols] = jnp.broadcast_to(d_col, (t, LANES)).T[0:1, :]

        def step(diag):
            q = q_ref[...]
            do_b = do_ref[...].astype(BF16)
            sc = lax.dot_general(k_ref[...], q, BNT_DIMS, preferred_element_type=F32)
            if diag:
                sc = jnp.where(_causal(sc.shape[1:], 1)[None], sc, NEG)
            p = jnp.exp(sc - lse_ref[...])
            dv_s[...] += lax.dot_general(p.astype(BF16), do_b, BNN_DIMS, preferred_element_type=F32)
            dp = lax.dot_general(v_ref[...], do_b, BNT_DIMS, preferred_element_type=F32)
            ds = (p * (dp - d_s[:, :, cols])).astype(BF16)
            dk_s[...] += lax.dot_general(ds, q, BNN_DIMS, preferred_element_type=F32)
            dqt_s[:, :, cols] += lax.dot_general(kt_ref[...], ds, BNN_DIMS, preferred_element_type=F32)

        pl.when(j > kb)(functools.partial(step, False))
        pl.when(j == kb)(functools.partial(step, True))

        @pl.when(j == kb)
        def _():
            for h in range(nh):
                stage[h] = dqt_s[h, :, cols].T
            out = pltpu.make_async_copy(stage, dq_hbm.at[:, cols, :], sem)
            out.start()
            out.wait()

        @pl.when(j == nq - 1)
        def _():
            dk_ref[...] = dk_s[...]
            dv_ref[...] = dv_s[...].astype(dv_ref.dtype)

    q_spec = pl.BlockSpec((nh, t, LANES), lambda p, kb, qb: (0, qb[p], 0))
    kv_spec = pl.BlockSpec((nh, t, LANES), lambda p, kb, qb: (0, kb[p], 0))
    kt_spec = pl.BlockSpec((nh, LANES, t), lambda p, kb, qb: (0, 0, kb[p]))
    row_spec = pl.BlockSpec((nh, 1, t), lambda p, kb, qb: (0, 0, qb[p]))
    whole = jax.ShapeDtypeStruct((nh, s, LANES), F32)
    return pl.pallas_call(
        body, name=name,
        grid_spec=pltpu.PrefetchScalarGridSpec(
            num_scalar_prefetch=2, grid=(len(pairs),),
            in_specs=[q_spec, kv_spec, kt_spec, kv_spec, q_spec, q_spec, row_spec, pl.BlockSpec(memory_space=pl.ANY)],
            out_specs=[pl.BlockSpec(memory_space=pl.ANY), kv_spec, kv_spec],
            scratch_shapes=[pltpu.VMEM((nh, LANES, s), F32), pltpu.VMEM((nh, t, LANES), F32),
                            pltpu.VMEM((nh, t, LANES), F32), pltpu.VMEM((nh, 1, s), F32), pltpu.VMEM((nh, t, LANES), F32),
                            pltpu.SemaphoreType.DMA]),
        out_shape=[whole, whole, jax.ShapeDtypeStruct((nh, s, LANES), BF16)],
        compiler_params=_params(("arbitrary",)),
    )(*_block_tables(pairs), qa, ka, kta, va, o, do, lse_row, after)


SWA_PIECE = 128
SWA_KEYS = 2 * SWA_PIECE


def _swa_block(s):
    return min(512, s)


def _swa_piece(hp_ref, h, q, k_ref, v_ref, qpos0, scale):
    kstart = pl.multiple_of(jnp.maximum(qpos0 - SWA_PIECE, 0), SWA_PIECE)
    k = k_ref[pl.ds(kstart, SWA_KEYS), :].astype(BF16)
    v = v_ref[pl.ds(kstart, SWA_KEYS), :].astype(BF16)
    sc = lax.dot_general(q, k, NT_DIMS, preferred_element_type=F32)
    dist = (qpos0 + lax.broadcasted_iota(jnp.int32, sc.shape, 0)) - (kstart + lax.broadcasted_iota(jnp.int32, sc.shape, 1))
    sc = sc * scale - hp_ref[h, 0] * dist.astype(F32)
    sc = jnp.where((dist >= 0) & (dist < SWA_WINDOW), sc, NEG)
    return kstart, k, v, sc


def _swa_fwd(proj, hp, *, name, s, scale):
    tb = _swa_block(s)
    group = SWA_HEADS // SWA_KV_HEADS
    q_off, k_off, v_off = C_QSW[0] // LANES, C_KSW[0] // LANES, C_VSW[0] // LANES

    def body(hp_ref, q_ref, k_ref, v_ref, o_ref, lse_ref):
        h, i = pl.program_id(0), pl.program_id(1)
        sink = hp_ref[h, 1]
        for r in range(0, tb, SWA_PIECE):
            rows = pl.ds(r, SWA_PIECE)
            _, _, v, sc = _swa_piece(hp_ref, h, q_ref[rows, :].astype(BF16), k_ref, v_ref, i * tb + r, scale)
            m = jnp.maximum(jnp.max(sc, axis=1, keepdims=True), sink)
            p = jnp.exp(sc - m)
            l = jnp.sum(p, axis=1, keepdims=True) + jnp.exp(sink - m)
            o_ref[rows, :] = jnp.dot(p.astype(BF16), v, preferred_element_type=F32) / l
            lse_ref[rows, :] = m + jnp.log(l)

    whole = lambda off: pl.BlockSpec((s, LANES), lambda h, i: (0, off + h // group))
    return pl.pallas_call(
        body, name=name, grid=(SWA_HEADS, s // tb),
        in_specs=[pl.BlockSpec(memory_space=pltpu.SMEM), pl.BlockSpec((tb, LANES), lambda h, i: (i, q_off + h)),
                  whole(k_off), whole(v_off)],
        out_specs=[pl.BlockSpec((tb, LANES), lambda h, i: (i, h)), pl.BlockSpec((None, tb, 1), lambda h, i: (h, i, 0))],
        out_shape=[jax.ShapeDtypeStruct((s, SWA_HEADS * LANES), F32), jax.ShapeDtypeStruct((SWA_HEADS, s, 1), F32)],
        compiler_params=_params(("parallel", "parallel")),
    )(hp, proj, proj, proj)


def _swa_bwd(proj, o, do, lse, hp, *, name, s, scale):
    tb = _swa_block(s)
    nqb = s // tb
    group = SWA_HEADS // SWA_KV_HEADS
    q_off, k_off, v_off = C_QSW[0] // LANES, C_KSW[0] // LANES, C_VSW[0] // LANES

    def body(hp_ref, q_ref, k_ref, v_ref, o_ref, do_ref, lse_ref, dq_ref, dk_ref, dv_ref, dsink_ref):
        kh, g, i = pl.program_id(0), pl.program_id(1), pl.program_id(2)
        h = kh * group + g
        sink = hp_ref[h, 1]

        @pl.when((g == 0) & (i == 0))
        def _():
            dk_ref[...] = jnp.zeros_like(dk_ref)
            dv_ref[...] = jnp.zeros_like(dv_ref)

        @pl.when(i == 0)
        def _():
            dsink_ref[...] = jnp.zeros_like(dsink_ref)

        for r in range(0, tb, SWA_PIECE):
            rows = pl.ds(r, SWA_PIECE)
            q = q_ref[rows, :].astype(BF16)
            dov = do_ref[rows, :]
            do_b = dov.astype(BF16)
            lse_r = lse_ref[rows, :]
            d_r = jnp.sum(dov * o_ref[rows, :], axis=1, keepdims=True)
            kstart, k, v, sc = _swa_piece(hp_ref, h, q, k_ref, v_ref, i * tb + r, scale)
            p = jnp.exp(sc - lse_r)
            dp = lax.dot_general(do_b, v, NT_DIMS, preferred_element_type=F32)
            ds = (p * (dp - d_r)).astype(BF16)
            dq_ref[rows, :] = (jnp.dot(ds, k, preferred_element_type=F32) * scale).astype(dq_ref.dtype)
            win = pl.ds(kstart, SWA_KEYS)
            dk_ref[win, :] += lax.dot_general(ds, q, TN_DIMS, preferred_element_type=F32) * scale
            dv_ref[win, :] += lax.dot_general(p.astype(BF16), do_b, TN_DIMS, preferred_element_type=F32)
            part = jnp.sum(-jnp.exp(sink - lse_r) * d_r, axis=0, keepdims=True)
            dsink_ref[...] += jnp.broadcast_to(part, (1, LANES))

    whole = lambda off: pl.BlockSpec((s, LANES), lambda kh, g, i: (0, off + kh))
    q_map = lambda kh, g, i: (i, kh * group + g)
    return pl.pallas_call(
        body, name=name, grid=(SWA_KV_HEADS, group, nqb),
        in_specs=[pl.BlockSpec(memory_space=pltpu.SMEM),
                  pl.BlockSpec((tb, LANES), lambda kh, g, i: (i, q_off + kh * group + g)), whole(k_off), whole(v_off),
                  pl.BlockSpec((tb, LANES), q_map), pl.BlockSpec((tb, LANES), q_map),
                  pl.BlockSpec((None, tb, 1), lambda kh, g, i: (kh * group + g, i, 0))],
        out_specs=[pl.BlockSpec((tb, LANES), q_map), whole(0), whole(0),
                   pl.BlockSpec((None, 1, LANES), lambda kh, g, i: (kh * group + g, 0, 0))],
        out_shape=[jax.ShapeDtypeStruct((s, SWA_HEADS * LANES), BF16),
                   jax.ShapeDtypeStruct((s, SWA_KV_HEADS * LANES), F32), jax.ShapeDtypeStruct((s, SWA_KV_HEADS * LANES), F32),
                   jax.ShapeDtypeStruct((SWA_HEADS, 1, LANES), F32)],
        compiler_params=_params(("parallel", "arbitrary", "arbitrary")),
    )(hp, proj, proj, proj, o, do, lse)


def _shift_down(z, k):
    rows = lax.broadcasted_iota(jnp.int32, z.shape, 0)
    return jnp.where(rows >= k, pltpu.roll(z, k, 0), 0.0)


def _shift_up(z, k):
    n = z.shape[0]
    rows = lax.broadcasted_iota(jnp.int32, z.shape, 0)
    return jnp.where(rows < n - k, pltpu.roll(z, n - k, 0), 0.0)


def _rows3(a, b, c):
    r = lax.broadcasted_iota(jnp.int32, (3, a.shape[1]), 0)
    return jnp.where(r == 0, a, jnp.where(r == 1, b, c))


def _col_spec(s, off):
    return pl.BlockSpec((s, LANES), functools.partial(lambda j, off: (0, off + j), off=off))


def _conv_fwd(proj, conv_w, *, name, s):
    def body(gb_ref, gc_ref, u_ref, w_ref, y_ref):
        w0, w1, w2 = w_ref[0:1, :], w_ref[1:2, :], w_ref[2:3, :]
        z = gc_ref[...] * u_ref[...]
        c = w2 * z + w1 * _shift_down(z, 1) + w0 * _shift_down(z, 2)
        y_ref[...] = gb_ref[...] * c

    return pl.pallas_call(
        body, name=name, grid=(2,),
        in_specs=[_col_spec(s, C_GB[0] // LANES), _col_spec(s, C_GC[0] // LANES), _col_spec(s, C_UCONV[0] // LANES),
                  pl.BlockSpec((3, LANES), lambda j: (0, j))],
        out_specs=_col_spec(s, 0), out_shape=jax.ShapeDtypeStruct((s, D_GROUP), F32),
        compiler_params=_params(("parallel",)),
    )(proj, proj, proj, conv_w)


def _conv_bwd(dy, proj, conv_w, *, name, s):
    def body(dy_ref, gb_ref, gc_ref, u_ref, w_ref, dgb_ref, dgc_ref, du_ref, dw_ref):
        w0, w1, w2 = w_ref[0:1, :], w_ref[1:2, :], w_ref[2:3, :]
        gc, u, dyv = gc_ref[...], u_ref[...], dy_ref[...]
        z = gc * u
        z1, z2 = _shift_down(z, 1), _shift_down(z, 2)
        c = w2 * z + w1 * z1 + w0 * z2
        dgb_ref[...] = (dyv * c).astype(dgb_ref.dtype)
        dc = dyv * gb_ref[...]
        dz = w2 * dc + w1 * _shift_up(dc, 1) + w0 * _shift_up(dc, 2)
        dgc_ref[...] = (dz * u).astype(dgc_ref.dtype)
        du_ref[...] = (dz * gc).astype(du_ref.dtype)
        dw_ref[...] = _rows3(jnp.sum(dc * z2, axis=0, keepdims=True), jnp.sum(dc * z1, axis=0, keepdims=True),
                             jnp.sum(dc * z, axis=0, keepdims=True))

    act = jax.ShapeDtypeStruct((s, D_GROUP), BF16)
    return pl.pallas_call(
        body, name=name, grid=(2,),
        in_specs=[_col_spec(s, 0), _col_spec(s, C_GB[0] // LANES), _col_spec(s, C_GC[0] // LANES),
                  _col_spec(s, C_UCONV[0] // LANES), pl.BlockSpec((3, LANES), lambda j: (0, j))],
        out_specs=[_col_spec(s, 0), _col_spec(s, 0), _col_spec(s, 0), pl.BlockSpec((3, LANES), lambda j: (0, j))],
        out_shape=[act, act, act, jax.ShapeDtypeStruct((3, D_GROUP), F32)],
        compiler_params=_params(("parallel",)),
    )(dy, proj, proj, proj, conv_w)


def _pool_select(j, lane, a2, a4, a8, a16):
    lo = lane < HEAD
    return jnp.where(j == 0, jnp.where(lo, a2, a4), jnp.where(lo, a8, a16))


def _pooled(u, j):
    s2 = u + _shift_down(u, 1)
    s4 = s2 + _shift_down(s2, 2)
    s8 = s4 + _shift_down(s4, 4)
    s16 = s8 + _shift_down(s8, 8)
    lane = lax.broadcasted_iota(jnp.int32, u.shape, 1)
    rows = lax.broadcasted_iota(jnp.int32, u.shape, 0)
    win = _pool_select(j, lane, 2, 4, 8, 16)
    count = jnp.minimum(rows + 1, win).astype(F32)
    return _pool_select(j, lane, s2, s4, s8, s16) / count - u, count


def _pool_fwd(proj, wbd, scale, *, name, s):
    def body(u_ref, w_ref, sc_ref, y_ref):
        pooled, _ = _pooled(u_ref[...], pl.program_id(0))
        y_ref[...] = jnp.dot(pooled.astype(BF16), w_ref[...].astype(BF16), preferred_element_type=F32) * sc_ref[...]

    return pl.pallas_call(
        body, name=name, grid=(2,),
        in_specs=[_col_spec(s, C_UPOOL[0] // LANES), pl.BlockSpec((None, LANES, LANES), lambda j: (j, 0, 0)),
                  pl.BlockSpec((1, LANES), lambda j: (0, j))],
        out_specs=_col_spec(s, 0), out_shape=jax.ShapeDtypeStruct((s, D_GROUP), F32),
        compiler_params=_params(("parallel",)),
    )(proj, wbd, scale)


def _pool_bwd(dy, proj, wbd, scale, *, name, s):
    def body(dy_ref, u_ref, w_ref, sc_ref, du_ref, dw_ref, dsc_ref):
        j = pl.program_id(0)
        pooled, count = _pooled(u_ref[...], j)
        pooled_b = pooled.astype(BF16)
        w_b = w_ref[...].astype(BF16)
        dyv = dy_ref[...]
        mixed = jnp.dot(pooled_b, w_b, preferred_element_type=F32)
        dsc_ref[...] = jnp.sum(dyv * mixed, axis=0, keepdims=True)
        dms = (dyv * sc_ref[...]).astype(BF16)
        dw_ref[...] = lax.dot_general(pooled_b, dms, (((0,), (0,)), ((), ())), preferred_element_type=F32)
        dpooled = lax.dot_general(dms, w_b, (((1,), (1,)), ((), ())), preferred_element_type=F32)
        r = dpooled / count
        a2 = r + _shift_up(r, 1)
        a4 = a2 + _shift_up(a2, 2)
        a8 = a4 + _shift_up(a4, 4)
        a16 = a8 + _shift_up(a8, 8)
        lane = lax.broadcasted_iota(jnp.int32, r.shape, 1)
        du_ref[...] = (_pool_select(j, lane, a2, a4, a8, a16) - dpooled).astype(du_ref.dtype)

    return pl.pallas_call(
        body, name=name, grid=(2,),
        in_specs=[_col_spec(s, 0), _col_spec(s, C_UPOOL[0] // LANES),
                  pl.BlockSpec((None, LANES, LANES), lambda j: (j, 0, 0)), pl.BlockSpec((1, LANES), lambda j: (0, j))],
        out_specs=[_col_spec(s, 0), pl.BlockSpec((None, LANES, LANES), lambda j: (j, 0, 0)),
                   pl.BlockSpec((1, LANES), lambda j: (0, j))],
        out_shape=[jax.ShapeDtypeStruct((s, D_GROUP), BF16), jax.ShapeDtypeStruct((2, LANES, LANES), F32),
                   jax.ShapeDtypeStruct((1, D_GROUP), F32)],
        compiler_params=_params(("parallel",)),
    )(dy, proj, wbd, scale)


def _mesh_pos():
    return lax.axis_index("x"), lax.axis_index("y"), lax.axis_index("c")


def _any_specs(n):
    return [pl.BlockSpec(memory_space=pl.ANY)] * n


def _all_gather(xs, *, name):
    n = len(xs)

    def body(*refs):
        x_refs, out_refs = refs[:n], refs[n:2 * n]
        send_sems, recv_sems, local_sems = refs[2 * n:]
        x, y, cc = _mesh_pos()
        me, sibling = (x, y, cc), (x, y, 1 - cc)
        chips = [(1 - x, y), (x, 1 - y), (1 - x, 1 - y)]

        def slot(a, px, py, pc):
            return out_refs[a].at[4 * px + 2 * py + pc]

        def copy(a, k, block, to, src=None):
            return pltpu.make_async_remote_copy(
                src_ref=slot(a, *block) if src is None else src, dst_ref=slot(a, *block), send_sem=send_sems.at[a, k],
                recv_sem=recv_sems.at[a, k], device_id=to, device_id_type=pl.DeviceIdType.MESH)

        mine = [pltpu.make_async_copy(x_refs[a], slot(a, *me), local_sems.at[a]) for a in range(n)]
        first = []
        for a in range(n):
            first.append(copy(a, 0, me, sibling, src=x_refs[a]))
            first += [copy(a, 1 + j, me, (*chip, cc), src=x_refs[a]) for j, chip in enumerate(chips)]
        for cp in mine + first:
            cp.start()
        passed = []
        for j, chip in enumerate(chips):
            for a in range(n):
                copy(a, 1 + j, (*chip, cc), me).wait_recv()
                passed.append(copy(a, 4 + j, (*chip, cc), sibling))
                passed[-1].start()
        for a in range(n):
            copy(a, 0, sibling, me).wait_recv()
        for j, chip in enumerate(chips):
            for a in range(n):
                copy(a, 4 + j, (*chip, 1 - cc), me).wait_recv()
        for cp in first + passed:
            cp.wait_send()
        for cp in mine:
            cp.wait()

    return pl.pallas_call(
        body, name=name, out_shape=[jax.ShapeDtypeStruct((N_DEV,) + a.shape, a.dtype) for a in xs],
        in_specs=_any_specs(n), out_specs=_any_specs(n),
        scratch_shapes=[pltpu.SemaphoreType.DMA((n, 7)), pltpu.SemaphoreType.DMA((n, 7)), pltpu.SemaphoreType.DMA((n,))],
    )(*xs)


def _plan_gather_near(src_refs, land_refs, send_sems, recv_sems):
    x, y, cc = _mesh_pos()
    me = 4 * x + 2 * y + cc
    plan = []
    for a, (src, land) in enumerate(zip(src_refs, land_refs)):
        for k, (px, py, pc) in enumerate([(x, y, 1 - cc), (1 - x, y, cc), (x, 1 - y, cc), (1 - x, 1 - y, cc)]):
            sems = dict(send_sem=send_sems.at[4 * a + k], recv_sem=recv_sems.at[4 * a + k], device_id=(px, py, pc),
                        device_id_type=pl.DeviceIdType.MESH)
            plan.append((pltpu.make_async_remote_copy(src_ref=src, dst_ref=land.at[me], **sems),
                         pltpu.make_async_remote_copy(src_ref=src, dst_ref=land.at[4 * px + 2 * py + pc], **sems)))
    return plan


def _plan_gather_pass(src_refs, land_refs, send_sems, recv_sems):
    x, y, cc = _mesh_pos()
    plan = []
    for a, land in enumerate(land_refs):
        for j, (px, py) in enumerate([(1 - x, y), (x, 1 - y), (1 - x, 1 - y)]):
            mine, theirs = land.at[4 * px + 2 * py + cc], land.at[4 * px + 2 * py + 1 - cc]
            sems = dict(send_sem=send_sems.at[3 * a + j], recv_sem=recv_sems.at[3 * a + j], device_id=(x, y, 1 - cc),
                        device_id_type=pl.DeviceIdType.MESH)
            plan.append((pltpu.make_async_remote_copy(src_ref=mine, dst_ref=mine, **sems),
                         pltpu.make_async_remote_copy(src_ref=mine, dst_ref=theirs, **sems)))
    return plan


def _plan_sibling(src_refs, land_refs, send_sems, recv_sems):
    x, y, cc = _mesh_pos()
    plan = []
    for a, (src, land) in enumerate(zip(src_refs, land_refs)):
        cp = pltpu.make_async_remote_copy(
            src_ref=src.at[1 - cc], dst_ref=land, send_sem=send_sems.at[a], recv_sem=recv_sems.at[a],
            device_id=(x, y, 1 - cc), device_id_type=pl.DeviceIdType.MESH)
        plan.append((cp, cp))
    return plan


def _plan_chips(src_refs, land_refs, send_sems, recv_sems):
    x, y, cc = _mesh_pos()
    my_chip = 2 * x + y
    plan = []
    for a, (src, land) in enumerate(zip(src_refs, land_refs)):
        for j, (px, py) in enumerate([(1 - x, y), (x, 1 - y), (1 - x, 1 - y)]):
            peer = 2 * px + py
            sems = dict(send_sem=send_sems.at[3 * a + j], recv_sem=recv_sems.at[3 * a + j], device_id=(px, py, cc),
                        device_id_type=pl.DeviceIdType.MESH)
            plan.append((pltpu.make_async_remote_copy(src_ref=src.at[peer], dst_ref=land.at[my_chip], **sems),
                         pltpu.make_async_remote_copy(src_ref=src.at[peer], dst_ref=land.at[peer], **sems)))
    return plan


HBM_SPEC = pl.BlockSpec(memory_space=pltpu.HBM)
SEM_SPEC = pl.BlockSpec(memory_space=pltpu.SEMAPHORE)
ANY_SPEC = pl.BlockSpec(memory_space=pl.ANY)
SIDE_EFFECT = pltpu.CompilerParams(has_side_effects=pltpu.SideEffectType.DATAFLOW_SIDE_EFFECTING)


def _start_copies(plan, sems_per_array, srcs, lands, after, *, name):
    lands = [lax.empty(l, a.dtype) if isinstance(l, tuple) else l for l, a in zip(lands, srcs or lands)]
    ns, n = len(srcs), len(srcs) + len(lands)

    def body(*refs):
        send_sems, recv_sems = refs[n + len(after)], refs[n + len(after) + 1]
        for out, _ in plan(refs[:ns], refs[ns:n], send_sems, recv_sems):
            out.start()
        refs[-1][...] = jnp.zeros_like(refs[-1])

    sem = pltpu.SemaphoreType.DMA((len(lands) * sems_per_array,))
    res = pl.pallas_call(
        body, name=name,
        out_shape=(sem, sem, *[pltpu.HBM(a.shape, a.dtype) for a in srcs + lands], jax.ShapeDtypeStruct((8, LANES), F32)),
        in_specs=[HBM_SPEC] * n + [ANY_SPEC] * len(after),
        out_specs=(SEM_SPEC, SEM_SPEC, *[HBM_SPEC] * n, pl.BlockSpec(memory_space=pltpu.VMEM)),
        input_output_aliases={i: 2 + i for i in range(n)}, compiler_params=SIDE_EFFECT,
    )(*[pltpu.with_memory_space_constraint(a, pltpu.HBM) for a in srcs + lands], *after)
    return (res[0], res[1], list(res[2:2 + ns]), list(res[2 + ns:2 + n])), res[-1]


def _wait_copies(plan, handle, after, *, name):
    send, recv, srcs, lands = handle
    ns, n = len(srcs), len(srcs) + len(lands)

    def body(*refs):
        for out, inc in plan(refs[:ns], refs[ns:n], refs[n], refs[n + 1]):
            out.wait_send()
            inc.wait_recv()

    res = pl.pallas_call(
        body, name=name, out_shape=tuple(pltpu.HBM(a.shape, a.dtype) for a in srcs + lands),
        in_specs=[HBM_SPEC] * n + [SEM_SPEC, SEM_SPEC] + [ANY_SPEC] * len(after), out_specs=[HBM_SPEC] * n,
        input_output_aliases={i: i for i in range(n)}, compiler_params=SIDE_EFFECT,
    )(*srcs, *lands, send, recv, *after)
    return list(res[:ns]), list(res[ns:])


def _row_tile(rows, target=512):
    if rows <= target:
        return rows
    best = None
    for t in range(8, target + 1, 8):
        if rows % t == 0:
            best = t
    assert best is not None, (rows, target)
    return best


def _add_own(g, other, core, *, name):
    _, _, rows, cols = g.shape
    tm = _row_tile(rows)

    def body(c_ref, g_ref, o_ref, out_ref):
        out_ref[...] = g_ref[...] + o_ref[...]

    return pl.pallas_call(
        body, name=name, out_shape=jax.ShapeDtypeStruct(other.shape, other.dtype),
        grid_spec=pltpu.PrefetchScalarGridSpec(
            num_scalar_prefetch=1, grid=(4, rows // tm),
            in_specs=[pl.BlockSpec((None, None, tm, cols), lambda p, i, c_ref: (c_ref[0], p, i, 0)),
                      pl.BlockSpec((None, tm, cols), lambda p, i, c_ref: (p, i, 0))],
            out_specs=pl.BlockSpec((None, tm, cols), lambda p, i, c_ref: (p, i, 0))),
        compiler_params=_params(("parallel", "parallel")),
    )(core, g, other)


def _adamw(parts, w, m, v, after, *, name):
    layers, rows, cols = w.shape
    assert len(parts) == layers
    tm = _row_tile(rows, 256)
    nr = rows // tm

    def body(*refs):
        p_refs = refs[:layers]
        w_ref, m_ref, v_ref, _, g_ref, d_ref, nm_ref, nv_ref, g_s = refs[layers:]
        for ll in range(layers):
            @pl.when(pl.program_id(0) == ll)
            def _(ll=ll):
                g = p_refs[ll][0]
                for q in range(1, p_refs[ll].shape[0]):
                    g = g + p_refs[ll][q]
                g_s[...] = g

        g = g_s[...]
        mm = ADAM_B1 * m_ref[...] + (1.0 - ADAM_B1) * g
        vv = ADAM_B2 * v_ref[...] + (1.0 - ADAM_B2) * jnp.square(g)
        m_hat = mm / (1.0 - ADAM_B1 ** ADAM_STEP)
        v_hat = vv / (1.0 - ADAM_B2 ** ADAM_STEP)
        g_ref[...] = g
        d_ref[...] = -ADAM_LR * (m_hat / (jnp.sqrt(v_hat) + ADAM_EPS) + ADAM_WD * w_ref[...])
        nm_ref[...] = mm
        nv_ref[...] = vv

    def part_spec(ll, p):
        return pl.BlockSpec((p, tm, cols), lambda l, i: (0, jnp.where(l == ll, i, jnp.where(l < ll, 0, nr - 1)), 0))

    spec = pl.BlockSpec((None, tm, cols), lambda l, i: (l, i, 0))
    out = jax.ShapeDtypeStruct(w.shape, F32)
    return pl.pallas_call(
        body, name=name, grid=(layers, nr),
        in_specs=[part_spec(ll, parts[ll].shape[0]) for ll in range(layers)] + [spec] * 3 + [pl.BlockSpec(memory_space=pl.ANY)],
        out_specs=[spec] * 4, out_shape=[out] * 4, scratch_shapes=[pltpu.VMEM((tm, cols), F32)],
        compiler_params=_params(("arbitrary", "arbitrary")),
    )(*parts, w, m, v, after)


def _pack(arrs):
    flat = jnp.concatenate([a.reshape(-1) for a in arrs])
    rows = -(-flat.shape[0] // (PACK_COLS * 16)) * 16
    return jnp.pad(flat, (0, rows * PACK_COLS - flat.shape[0])).reshape(rows, PACK_COLS)


def _unpack(packed, shapes):
    flat = packed.reshape(-1)
    out, off = [], 0
    for shp in shapes:
        n = int(np.prod(shp))
        out.append(flat[off:off + n].reshape(shp))
        off += n
    return out


def _shards_to_full(g, axis):
    if axis == 0:
        return g.reshape(g.shape[0] * g.shape[1], g.shape[2])
    return jnp.transpose(g, (1, 0, 2)).reshape(g.shape[1], g.shape[0] * g.shape[2])


def _full_to_shards(a, axis):
    if axis == 0:
        return jnp.transpose(a.reshape(4, 2, a.shape[0] // N_DEV, a.shape[1]), (1, 0, 2, 3))
    return jnp.transpose(a.reshape(a.shape[0], 4, 2, a.shape[1] // N_DEV), (2, 1, 0, 3))


def _zeros_like_cols(a, n):
    return jnp.zeros(a.shape[:-1] + (n,), a.dtype)


def _pad_heads(a, n):
    z = _zeros_like_cols(a, HEAD)
    return jnp.concatenate([p for h in range(n) for p in (a[..., h * HEAD:(h + 1) * HEAD], z)], axis=-1)


def _unpad_heads(a, n):
    return jnp.concatenate([a[..., h * LANES:h * LANES + HEAD] for h in range(n)], axis=-1)


def _seg(first, width, sign=1):
    return (width, [(first, sign)])


def _zero(width):
    return (width, [])


def _swapped(first):
    half = MLA_ROPE // 2
    return [_seg(first + half, half, -1), _seg(first, half)]


def _padded_heads(first, n):
    return [s for h in range(n) for s in (_seg(first + HEAD * h, HEAD), _zero(HEAD))]


def _layout_w_in():
    kr = 384
    return (_padded_heads(1440, 4) + [_seg(0, 256), _seg(416, 256), _seg(672, 256), _seg(928, 256), _seg(1184, 256)]
            + _padded_heads(1696, 2) + _padded_heads(1824, 2) + [_seg(256, 128)]
            + [_zero(HEAD), _seg(kr, MLA_ROPE), _seg(kr, MLA_ROPE)] + [_zero(HEAD)] + _swapped(kr) + _swapped(kr))


def _layout_w_uq():
    out = []
    for h in range(MLA_HEADS):
        out += [_seg(96 * h, MLA_NOPE), _seg(96 * h + MLA_NOPE, MLA_ROPE)] + _swapped(96 * h + MLA_NOPE)
    return out


def _layout_w_ukv():
    keys = [s for h in range(MLA_HEADS) for s in (_seg(LANES * h, HEAD), _zero(HEAD))]
    values = [s for h in range(MLA_HEADS) for s in (_seg(LANES * h + HEAD, HEAD), _zero(HEAD))]
    return keys + values


def _layout_w_gate_up():
    return [_seg(half + j, GU_TILE) for j in range(0, D_FF, GU_TILE) for half in (0, D_FF)]


LAYOUTS = dict(w_in=_layout_w_in(), w_uq=_layout_w_uq(), w_ukv=_layout_w_ukv(), w_gate_up=_layout_w_gate_up())
OWN_COLS = dict(w_in=1952, w_uq=384, w_ukv=512, w_gate_up=2 * D_FF)


def _plan_extend(layout, shard):
    plan = []
    for width, terms in layout:
        if not terms:
            plan.append((width, []))
            continue
        (first, sign), = terms
        while width:
            g, off = divmod(first, shard)
            w = min(width, shard - off)
            plan.append((w, [(g, off, sign)]))
            first, width = first + w, width - w
    return [plan]


def _plan_fold(layout, own_cols):
    sources = [[] for _ in range(own_cols)]
    e = 0
    for width, terms in layout:
        for first, sign in terms:
            for i in range(width):
                sources[first + i].append((e + i, sign))
        e += width
    shard = own_cols // N_DEV
    plans = {}
    for g in range(N_DEV):
        plan, n = [], g * shard
        while n < (g + 1) * shard:
            w = 1
            while n + w < (g + 1) * shard and [(c + w, sg) for c, sg in sources[n]] == sources[n + w]:
                w += 1
            plan.append((w, [(0, c, sg) for c, sg in sources[n]]))
            n += w
        plans[g] = plan
    return [plans[2 * p + c] for c in range(2) for p in range(4)]


def _assemble(src, plans, out_cols, out_dtype, *, name):
    g, rows, c = src.shape
    tm = _row_tile(rows, 256)

    def body(s_ref, o_ref):
        blocks = [s_ref[i].astype(F32) for i in range(g)]
        for d, plan in enumerate(plans):
            pieces = []
            for width, terms in plan:
                v = None
                for b, first, sign in terms:
                    t = blocks[b][:, first:first + width]
                    t = -t if sign < 0 else t
                    v = t if v is None else v + t
                pieces.append(jnp.zeros((tm, width), F32) if v is None else v)
            o_ref[d] = (pieces[0] if len(pieces) == 1 else jnp.concatenate(pieces, axis=1)).astype(o_ref.dtype)

    return pl.pallas_call(
        body, name=name, grid=(rows // tm,), in_specs=[pl.BlockSpec((g, tm, c), lambda i: (0, i, 0))],
        out_specs=pl.BlockSpec((len(plans), tm, out_cols), lambda i: (0, i, 0)),
        out_shape=jax.ShapeDtypeStruct((len(plans), rows, out_cols), out_dtype), compiler_params=_params(("parallel",)),
    )(src)


def _extend(nm, gathered, *, name):
    layout = LAYOUTS[nm]
    return _assemble(gathered, _plan_extend(layout, OWN_COLS[nm] // N_DEV), sum(w for w, _ in layout), BF16, name=name)[0]


def _fold_to_shards(nm, grad_ext, *, name):
    shards = _assemble(grad_ext[None], _plan_fold(LAYOUTS[nm], OWN_COLS[nm]), OWN_COLS[nm] // N_DEV, F32, name=name)
    return shards.reshape((2, 4) + shards.shape[1:])


def _rope_tables(s):
    inv = 1.0 / (ROPE_THETA ** (jnp.arange(0, MLA_ROPE, 2, dtype=F32) / MLA_ROPE))
    ang = jnp.arange(s, dtype=F32)[:, None] * inv[None, :]
    cos, sin = jnp.cos(ang), jnp.sin(ang)
    c32, s32 = jnp.concatenate([cos, cos], axis=1), jnp.concatenate([sin, sin], axis=1)
    zeros, ones = jnp.zeros((s, HEAD), F32), jnp.ones((s, HEAD), F32)
    tq = jnp.concatenate([ones, c32, s32], axis=1) * (1.0 / math.sqrt(MLA_NOPE + MLA_ROPE))
    return (jnp.tile(tq, (1, MLA_HEADS)), jnp.concatenate([zeros, c32, c32], axis=1),
            jnp.concatenate([zeros, s32, s32], axis=1))


def _gn(y):
    return y * _rstd(y, D_GROUP)


def _mixer_fwd(x, w, tabs, l, after_attention=None):
    s = x.shape[0]
    tq, tkc, tks = tabs
    n = lambda t: f"l{l}_{t}"
    h = _rms_fwd(("row", x, D_MODEL, 0), w["attn_norm"], name=n("attn_norm"), rows=s, width=D_MODEL)
    proj = _mm(h, w["w_in"], name=n("in_proj"))
    def prep(cq, ckv, kr, krp, gq, gkv, wuq, wukv, tqv, tc, ts):
        cqn = (cq * _rstd(cq, 256) * gq).astype(BF16)
        ckvn = (ckv * _rstd(ckv, 128) * gkv).astype(BF16)
        qe = jnp.dot(cqn, wuq, preferred_element_type=F32)
        kve = jnp.dot(ckvn, wukv, preferred_element_type=F32)
        kb = kr * tc + krp * ts
        kvv = kve[:, 512:]
        lane = lax.broadcasted_iota(jnp.int32, kvv.shape, 1) & (LANES - 1)
        v = jnp.where(lane == HEAD, 1.0, kvv)
        k = kve[:, :512] + jnp.tile(kb, (1, MLA_HEADS))
        return cqn, ckvn, qe * tqv, k, k, v, v

    cqn, ckvn, qm, km, kmt, vm, vmt = _rowwise(
        prep, name=n("mla_prep"), rows=s,
        ins=[("row", proj, 256, C_CQ[0] // 256), ("row", proj, 128, C_CKV[0] // 128),
             ("row", proj, 128, C_KR[0] // 128), ("row", proj, 128, C_KRP[0] // 128),
             ("full", w["mla_q_norm"]), ("full", w["mla_kv_norm"]), ("full", w["w_uq"]), ("full", w["w_ukv"]),
             ("row", tq, 512, 0), ("row", tkc, 128, 0), ("row", tks, 128, 0)],
        outs=[(256, BF16), (128, BF16), (512, BF16, MLA_HEADS), (512, BF16, MLA_HEADS), (512, BF16, MLA_HEADS, "T"),
              (512, BF16, MLA_HEADS), (512, BF16, MLA_HEADS, "T")])
    y_a, lse_row = _mla_fwd(qm, km, vmt, name=n("mla_fwd"), s=s, nh=MLA_HEADS)
    mix_norm = w["mix_norm"]
    if after_attention is not None:
        mix_norm = mix_norm + after_attention(y_a)[0, 0]
    y_b = _conv_fwd(proj, w["conv_w"], name=n("conv_fwd"), s=s)
    y_c = _pool_fwd(proj, w["pool_wbd"], w["pool_scale"], name=n("pool_fwd"), s=s)
    y_d, lse_d = _swa_fwd(proj, w["hp_swa"], name=n("swa_fwd"), s=s, scale=1.0 / math.sqrt(HEAD))

    def mix(ya, yb, yc, yd, mn):
        return (jnp.concatenate([_gn(_unpad_heads(ya, 4)), _gn(yb), _gn(yc), _gn(_unpad_heads(yd, 4))], axis=1) * mn,)

    mixed = _rowwise(mix, name=n("group_norm"), rows=s,
                     ins=[("heads", y_a), ("row", y_b, 256, 0), ("row", y_c, 256, 0), ("row", y_d, 512, 0),
                          ("full", mix_norm)], outs=[(D_MODEL, BF16)])[0]
    x1 = _mm(mixed, w["w_o"], res=x, name=n("out_proj"))
    saved = dict(x=x, h=h, proj=proj, cqn=cqn, ckvn=ckvn, qm=qm, km=km, kmt=kmt, vm=vm, y_a=y_a, lse_row=lse_row,
                 y_b=y_b, y_c=y_c, y_d=y_d, lse_d=lse_d, mixed=mixed)
    return x1, saved


def _ffn_fwd(x1, w, l):
    s = x1.shape[0]
    n = lambda t: f"l{l}_{t}"
    h2 = _rms_fwd(("row", x1, D_MODEL, 0), w["ffn_norm"], name=n("ffn_norm"), rows=s, width=D_MODEL)

    def swiglu(gu):
        g, u = gu[:, :GU_TILE], gu[:, GU_TILE:]
        return gu, g * jax.nn.sigmoid(g) * u

    gu, act = _mm(h2, w["w_gate_up"], tm=2048, tn=2 * GU_TILE, name=n("gate_up"),
                  epilogue=(swiglu, [], [(2 * D_FF, BF16), (D_FF, BF16)]))
    x2 = _mm(act, w["w_down"], res=x1, tk=D_FF // 2, name=n("down"))
    return x2, dict(x1=x1, h2=h2, gu=gu, act=act)


def _ffn_bwd_down(dx2, sv, w, l):
    n = lambda t: f"l{l}_{t}"

    def swiglu_bwd(da, gu):
        gt, u = gu[:, :GU_TILE].astype(F32), gu[:, GU_TILE:].astype(F32)
        sg = jax.nn.sigmoid(gt)
        return (jnp.concatenate([da * u * sg * (1.0 + gt * (1.0 - sg)), da * gt * sg], axis=1),)

    dgu = _mm(dx2[1], w["w_down"], tb=True, tm=2048, tn=GU_TILE, name=n("d_act"),
              epilogue=(swiglu_bwd, [sv["gu"]], [(2 * D_FF, BF16)]))[0]
    g = dict(w_down=_mm(sv["act"], dx2[1], ta=True, tm=D_FF // 2, name=n("dw_down")))
    return dgu, g


def _ffn_bwd_up(dx2, dgu, sv, w, l):
    s = dgu.shape[0]
    n = lambda t: f"l{l}_{t}"
    dh2 = _mm(dgu, w["w_gate_up"], tb=True, tm=2048, tk=D_FF // 2, name=n("d_h2"))
    g = dict(w_gate_up=_mm(sv["h2"], dgu, ta=True, tn=D_FF // 2, name=n("dw_gate_up")))
    dx1, dx1_b, g["ffn_norm"] = _rms_bwd(("row", sv["x1"], D_MODEL, 0), w["ffn_norm"], dh2, dx2[0],
                                         name=n("ffn_norm_bwd"), rows=s, width=D_MODEL, out_dtypes=(F32, BF16))
    return (dx1, dx1_b), g


def _mixer_bwd_out(dx1, sv, w, l):
    s = dx1[1].shape[0]
    n = lambda t: f"l{l}_{t}"
    dmixed = _mm(dx1[1], w["w_o"], tb=True, name=n("d_mixed"))
    g = dict(w_o=_mm(sv["mixed"], dx1[1], ta=True, name=n("dw_o")))

    def mix_bwd(dm, ya, yb, yc, yd, mn):
        outs, dmn = [], []
        for i, y in enumerate((_unpad_heads(ya, 4), yb, yc, _unpad_heads(yd, 4))):
            lo, hi = i * D_GROUP, (i + 1) * D_GROUP
            r = _rstd(y, D_GROUP)
            nrm = y * r
            dmg = dm[:, lo:hi]
            dn = dmg * mn[:, lo:hi]
            dy = r * (dn - nrm * (jnp.sum(dn * nrm, axis=-1, keepdims=True) * (1.0 / D_GROUP)))
            outs.append(_pad_heads(dy, 4) if i in (0, 3) else dy)
            dmn.append(jnp.sum(dmg * nrm, axis=0, keepdims=True))
        return (*outs, jnp.concatenate(dmn, axis=1))

    dy_a, dy_b, dy_c, dy_d, g["mix_norm"] = _rowwise(
        mix_bwd, name=n("group_norm_bwd"), rows=s,
        ins=[("row", dmixed, D_MODEL, 0), ("heads", sv["y_a"]), ("row", sv["y_b"], 256, 0),
             ("row", sv["y_c"], 256, 0), ("row", sv["y_d"], 512, 0), ("full", w["mix_norm"])],
        outs=[(512, F32, MLA_HEADS), (256, F32), (256, F32), (512, F32)], reds=[(1, D_MODEL)])
    return (dy_a, dy_b, dy_c, dy_d), g


def _mixer_bwd_in(dx1, dys, sv, w, tabs, l):
    s = dx1[0].shape[0]
    tq, tkc, tks = tabs
    n = lambda t: f"l{l}_{t}"
    dy_a, dy_b, dy_c, dy_d = dys
    g = {}

    proj = sv["proj"]
    dq_sw, dk_sw, dv_sw, dsink = _swa_bwd(proj, sv["y_d"], dy_d, sv["lse_d"], w["hp_swa"], name=n("swa_bwd"), s=s,
                                          scale=1.0 / math.sqrt(HEAD))
    g["swa_sinks"] = dsink[:, 0, 0]

    dqm, dkm, dvm = _mla_bwd(sv["qm"], sv["km"], sv["kmt"], sv["vm"], sv["y_a"], dy_a, sv["lse_row"], w["hp_swa"],
                             name=n("mla_bwd"), s=s, nh=MLA_HEADS)

    def rms_bwd(x, gv, dy, width):
        r = _rstd(x, width)
        dyg = dy * gv
        dx = r * dyg - x * (r * r * r) * (jnp.sum(dyg * x, axis=-1, keepdims=True) * (1.0 / width))
        return dx, jnp.sum(dy * x * r, axis=0, keepdims=True)

    def prep_bwd(dq, dk, dv, cq, ckv, gq, gkv, wuq, wukv, tqv, tc, ts):
        dkb = dk[:, 0:128] + dk[:, 128:256] + dk[:, 256:384] + dk[:, 384:512]
        dq_ext = (dq * tqv).astype(BF16)
        dkv_ext = jnp.concatenate([dk.astype(BF16), dv], axis=1)
        dcqn = lax.dot_general(dq_ext, wuq, NT_DIMS, preferred_element_type=F32)
        dckvn = lax.dot_general(dkv_ext, wukv, NT_DIMS, preferred_element_type=F32)
        dcq, dgq = rms_bwd(cq, gq, dcqn, 256)
        dckv, dgkv = rms_bwd(ckv, gkv, dckvn, 128)
        return dq_ext, dkv_ext, dkb * tc, dkb * ts, dcq, dckv, dgq, dgkv

    dq_ext, dkv_ext, dkr, dkrp, dcq, dckv, g["mla_q_norm"], g["mla_kv_norm"] = _rowwise(
        prep_bwd, name=n("mla_prep_bwd"), rows=s,
        ins=[("heads", dqm), ("heads", dkm), ("heads", dvm), ("row", proj, 256, C_CQ[0] // 256),
             ("row", proj, 128, C_CKV[0] // 128), ("full", w["mla_q_norm"]), ("full", w["mla_kv_norm"]),
             ("full", w["w_uq"]), ("full", w["w_ukv"]), ("row", tq, 512, 0), ("row", tkc, 128, 0), ("row", tks, 128, 0)],
        outs=[(512, BF16), (1024, BF16), (128, BF16), (128, BF16), (256, BF16), (128, BF16)],
        reds=[(1, 256), (1, 128)])
    g["w_uq"] = _mm(sv["cqn"], dq_ext, ta=True, name=n("dw_uq"))
    g["w_ukv"] = _mm(sv["ckvn"], dkv_ext, ta=True, name=n("dw_ukv"))

    dgb, dgc, duc, g["conv_w"] = _conv_bwd(dy_b, proj, w["conv_w"], name=n("conv_bwd"), s=s)
    dup, g["pool_wbd"], g["pool_scale"] = _pool_bwd(dy_c, proj, w["pool_wbd"], w["pool_scale"], name=n("pool_bwd"), s=s)

    dproj = jnp.concatenate([dq_sw, dcq, dgb, dgc, duc, dup, dk_sw.astype(BF16), dv_sw.astype(BF16), dckv, dkr, dkrp],
                            axis=1)
    g["w_in"] = _mm(sv["h"], dproj, ta=True, name=n("dw_in"))
    return dproj, g


def _mixer_bwd_norm(dx1, dproj, sv, w, l):
    n = lambda t: f"l{l}_{t}"
    dh = _mm(dproj, w["w_in"], tb=True, tm=2048, name=n("d_h"))
    dx0, dx0_b, dg = _rms_bwd(("row", sv["x"], D_MODEL, 0), w["attn_norm"], dh, dx1[0], name=n("attn_norm_bwd"),
                              rows=dh.shape[0], width=D_MODEL, out_dtypes=(F32, BF16))
    return (dx0, dx0_b), dict(attn_norm=dg)


def _loss_head(x, target, g, *, s):
    def fn(xv, tv, gv):
        r = _rstd(xv, D_MODEL)
        e = xv * r * gv - tv
        part = jnp.sum(jnp.sum(e * e, axis=1, keepdims=True), axis=0, keepdims=True) * (0.5 / D_MODEL)
        dy = e * (1.0 / D_MODEL)
        dyg = dy * gv
        dx = r * dyg - xv * (r * r * r) * (jnp.sum(dyg * xv, axis=-1, keepdims=True) * (1.0 / D_MODEL))
        return dx, dx, jnp.sum(dy * xv * r, axis=0, keepdims=True), jnp.broadcast_to(part, (1, LANES))

    return _rowwise(fn, name="loss_head", rows=s,
                    ins=[("row", x, D_MODEL, 0), ("row", target, D_MODEL, 0), ("full", g)],
                    outs=[(D_MODEL, F32), (D_MODEL, BF16)], reds=[(1, D_MODEL), (1, LANES)])


def _alibi_slopes(n):
    return np.asarray([2.0 ** (-8.0 * (i + 1) / n) for i in range(n)], dtype=np.float32)


MIXER_WEIGHTS = ("w_in", "w_uq", "w_ukv", "conv_w", "w_o")
FFN_WEIGHTS = ("w_gate_up", "w_down")


def _mixer_weights(full, rep, l):
    pw = rep["pool_w"][l]
    z = jnp.zeros((HEAD, HEAD), F32)
    wbd = jnp.stack([jnp.block([[pw[2 * j], z], [z, pw[2 * j + 1]]]) for j in range(2)])
    return dict(
        attn_norm=rep["attn_norm"][l][None], w_in=full["w_in"], mla_q_norm=rep["mla_q_norm"][l][None],
        w_uq=full["w_uq"], mla_kv_norm=rep["mla_kv_norm"][l][None], w_ukv=full["w_ukv"],
        conv_w=full["conv_w"], pool_wbd=wbd, pool_scale=rep["pool_scale"][l][None],
        mix_norm=rep["mix_norm"][l][None], w_o=full["w_o"],
        hp_swa=jnp.stack([jnp.asarray(_alibi_slopes(SWA_HEADS)), rep["swa_sinks"][l]], axis=1))


def _ffn_weights(full, rep, l):
    return dict(ffn_norm=rep["ffn_norm"][l][None], w_gate_up=full["w_gate_up"], w_down=full["w_down"])


def _small_grads(g):
    rows = ("attn_norm", "mla_q_norm", "mla_kv_norm", "pool_scale", "ffn_norm", "mix_norm")
    out = {nm: g[nm][0] for nm in rows if nm in g}
    if "swa_sinks" in g:
        out["swa_sinks"] = g["swa_sinks"]
    if "pool_wbd" in g:
        e = g["pool_wbd"]
        out["pool_w"] = jnp.stack([e[j // 2][HEAD * (j % 2):HEAD * (j % 2 + 1), HEAD * (j % 2):HEAD * (j % 2 + 1)]
                                   for j in range(4)])
    return out


def kernel(x, attn_norm, w_in, mla_q_norm, w_uq, mla_kv_norm, w_ukv, conv_w, pool_w, pool_scale, swa_sinks, mix_norm, w_o, ffn_norm, w_gate_up, w_down, final_norm, loss_target, m_attn_norm, m_w_in, m_mla_q_norm, m_w_uq, m_mla_kv_norm, m_w_ukv, m_conv_w, m_pool_w, m_pool_scale, m_swa_sinks, m_mix_norm, m_w_o, m_ffn_norm, m_w_gate_up, m_w_down, m_final_norm, v_attn_norm, v_w_in, v_mla_q_norm, v_w_uq, v_mla_kv_norm, v_w_ukv, v_conv_w, v_pool_w, v_pool_scale, v_swa_sinks, v_mix_norm, v_w_o, v_ffn_norm, v_w_gate_up, v_w_down, v_final_norm):
    given = dict(locals())
    sh_names = [nm for nm, _, _ in SHARDED]
    sh_axis = {nm: ax - 1 for nm, _, ax in SHARDED}
    rep_names = [nm for nm, _ in REPLICATED]
    rep_shapes = [shp for _, shp in REPLICATED]
    rep = {nm: given[nm] for nm in rep_names if nm != "loss"}
    me = 4 * lax.axis_index("x") + 2 * lax.axis_index("y") + lax.axis_index("c")
    my_chip = 2 * lax.axis_index("x") + lax.axis_index("y")
    core = lax.axis_index("c").astype(jnp.int32).reshape(1)

    def behind(token, a):
        return a + token[0, 0].astype(a.dtype)

    def wire(nm, l):
        if nm == "conv_w":
            return lax.bitcast_convert_type(given[nm][l], BF16).reshape(3, -1)
        return given[nm][l].astype(BF16)

    def whole(nm, g, tag):
        if nm in LAYOUTS:
            return _extend(nm, g, name=f"extend_{nm}_{tag}")
        if nm == "conv_w":
            g = lax.bitcast_convert_type(g.reshape(N_DEV, 3, -1, 2), F32)
        return _shards_to_full(g, sh_axis[nm])

    def near_start(names, l, after, tag):
        srcs = [wire(nm, l) for nm in names]
        return _start_copies(_plan_gather_near, 4, srcs, [(N_DEV,) + a.shape for a in srcs], after, name=f"start_gather_{tag}")

    def pass_on(handle, after, tag):
        srcs, lands = _wait_copies(_plan_gather_near, handle, after, name=f"wait_gather_{tag}")
        handle, token = _start_copies(_plan_gather_pass, 3, [], lands, [], name=f"start_pass_{tag}")
        return (srcs, handle), token

    def gathered(names, state, after, tag):
        srcs, handle = state
        _, lands = _wait_copies(_plan_gather_pass, handle, after, name=f"wait_pass_{tag}")
        return {nm: whole(nm, lax.dynamic_update_index_in_dim(land, src, me, 0), tag)
                for nm, src, land in zip(names, srcs, lands)}

    layer1 = MIXER_WEIGHTS + FFN_WEIGHTS
    got = _all_gather([wire(nm, 0) for nm in MIXER_WEIGHTS], name="gather_mixer0")
    full_m0 = {nm: whole(nm, g, "mixer0") for nm, g in zip(MIXER_WEIGHTS, got)}
    h_f0, tok = near_start(FFN_WEIGHTS, 0, [], "ffn0")
    h_l1, tok = near_start(layer1, 1, [tok], "layer1")

    xs, target = x[0], loss_target[0]
    s = xs.shape[0]
    tabs = _rope_tables(s)
    wm, wf, svm, svf = [None] * DEPTH, [None] * DEPTH, [None] * DEPTH, [None] * DEPTH
    wm[0] = _mixer_weights(full_m0, rep, 0)
    wm[0]["attn_norm"] = behind(tok, wm[0]["attn_norm"])
    passing = {}

    def pass_ffn0(y_a):
        passing["ffn0"], token = pass_on(h_f0, [y_a], "ffn0")
        return token

    x1, svm[0] = _mixer_fwd(xs, wm[0], tabs, 0, after_attention=pass_ffn0)
    wf[0] = _ffn_weights(gathered(FFN_WEIGHTS, passing["ffn0"], [x1], "ffn0"), rep, 0)
    passing["layer1"], tok = pass_on(h_l1, [x1], "layer1")
    wf[0]["ffn_norm"] = behind(tok, wf[0]["ffn_norm"])
    x2, svf[0] = _ffn_fwd(x1, wf[0], 0)
    full_1 = gathered(layer1, passing["layer1"], [x2], "layer1")
    wm[1], wf[1] = _mixer_weights(full_1, rep, 1), _ffn_weights(full_1, rep, 1)
    x1, svm[1] = _mixer_fwd(x2, wm[1], tabs, 1)
    x2, svf[1] = _ffn_fwd(x1, wf[1], 1)
    dx_f, dx_b, d_final, loss = _loss_head(x2, target, rep["final_norm"][None], s=s)
    dx = (dx_f, dx_b)

    parts = {}

    def reduce_start(grads, l, after, tag):
        names = [nm for nm in sh_names if nm in grads]
        mine = [_fold_to_shards(nm, grads[nm], name=f"fold_{nm}_{l}") if nm in LAYOUTS
                else _full_to_shards(grads[nm], sh_axis[nm]) for nm in names]
        handle, token = _start_copies(_plan_sibling, 1, mine, [m.shape[1:] for m in mine], after, name=f"start_sibling_{tag}")
        return (names, l, handle), token

    def reduce_mid(state, after, tag):
        names, l, handle = state
        mine, theirs = _wait_copies(_plan_sibling, handle, after, name=f"wait_sibling_{tag}")
        sums = [_add_own(g, o, core, name=f"chip_sum_{nm}_{l}") for nm, g, o in zip(names, mine, theirs)]
        handle, token = _start_copies(_plan_chips, 3, sums, [a.shape for a in sums], [], name=f"start_chips_{tag}")
        return (names, l, handle), token

    def reduce_end(state, after, tag):
        names, l, handle = state
        sums, lands = _wait_copies(_plan_chips, handle, after, name=f"wait_chips_{tag}")
        for nm, own, land in zip(names, sums, lands):
            parts[nm, l] = lax.dynamic_update_index_in_dim(land, lax.dynamic_index_in_dim(own, my_chip, 0, keepdims=False),
                                                           my_chip, 0)

    small = [None] * DEPTH
    in_flight = []
    pending = None
    for l in reversed(range(DEPTH)):
        dgu, g_down = _ffn_bwd_down(dx, svf[l], wf[l], l)
        if pending is not None:
            state, token = reduce_mid(pending, [dgu], f"mixer{l + 1}")
            in_flight.append((state, f"mixer{l + 1}"))
            wf[l]["ffn_norm"] = behind(token, wf[l]["ffn_norm"])
        dx1, g_up = _ffn_bwd_up(dx, dgu, svf[l], wf[l], l)
        g_ffn = {**g_down, **g_up}
        state, token = reduce_start(g_ffn, l, [], f"ffn{l}")
        wm[l]["mix_norm"] = behind(token, wm[l]["mix_norm"])
        dys, g_out = _mixer_bwd_out(dx1, svm[l], wm[l], l)
        state, token = reduce_mid(state, [dys[0]], f"ffn{l}")
        in_flight.append((state, f"ffn{l}"))
        wm[l]["hp_swa"] = behind(token, wm[l]["hp_swa"])
        dproj, g_in = _mixer_bwd_in(dx1, dys, svm[l], wm[l], tabs, l)
        g_mixer = {**g_out, **g_in}
        pending, token = reduce_start(g_mixer, l, [], f"mixer{l}")
        wm[l]["attn_norm"] = behind(token, wm[l]["attn_norm"])
        dx, g_norm = _mixer_bwd_norm(dx1, dproj, svm[l], wm[l], l)
        small[l] = _small_grads({**g_ffn, **g_mixer, **g_norm})
    grads = {nm: jnp.stack([small[l][nm] for l in range(DEPTH)]) for nm in rep_names if nm in small[0]}
    grads["final_norm"] = d_final[0]
    grads["loss"] = loss[0, :1]
    zero = jnp.zeros((1,), F32)
    small = _all_gather([behind(token, _pack([grads[nm] for nm in rep_names]))], name="gather_small_grads")

    last, token = reduce_mid(pending, [dx[0], small[0]], "mixer0")
    for state, tag in in_flight:
        reduce_end(state, [], tag)
    grad_x = dx[0]

    def adamw(nm, after):
        return _adamw([parts[nm, l] for l in range(DEPTH)], given[nm], given["m_" + nm], given["v_" + nm], after,
                      name=f"adamw_{nm}")

    sh_out = {nm: adamw(nm, token) for nm in FFN_WEIGHTS}
    packs = [_pack([given.get(pre + nm, zero) for nm in rep_names])[None] for pre in ("", "m_", "v_")]
    rep_res = _adamw(small, *packs, token, name="adamw_replicated")
    rep_out = [dict(zip(rep_names, _unpack(o[0], rep_shapes))) for o in rep_res]

    reduce_end(last, [rep_res[0], sh_out["w_down"][0]], "mixer0")
    sh_out.update({nm: adamw(nm, token) for nm in MIXER_WEIGHTS})

    out = [rep_out[0]["loss"][0], grad_x[None]]
    for i in range(4):
        out += [sh_out[nm][i] if nm in sh_axis else rep_out[i][nm] for nm in WEIGHT_ORDER]
    return tuple(out)
```

```python
import functools
import math

import numpy as np
import jax
import jax.numpy as jnp
from jax import lax
from jax.experimental import pallas as pl
from jax.experimental.pallas import tpu as pltpu

F32 = jnp.float32
BF16 = jnp.bfloat16

D_MODEL = 1024
DEPTH = 2
D_GROUP = 256
MLA_HEADS = 4
MLA_NOPE = 64
MLA_ROPE = 32
ROPE_THETA = 10000.0
POOL_WINDOWS = (2, 4, 8, 16)
SWA_HEADS = 4
SWA_KV_HEADS = 2
SWA_WINDOW = 128
D_FF = 2816
GU_TILE = 256
RMS_EPS = 1e-6
LANES = 128
HEAD = 64
VMEM_LIMIT = 48 * 1024 * 1024
NEG = -1e30

ADAM_LR = 0.001
ADAM_B1 = 0.9
ADAM_B2 = 0.999
ADAM_EPS = 1e-08
ADAM_WD = 0.01
ADAM_STEP = 10

N_DEV = 8
PACK_COLS = 1024

C_QSW, C_CQ, C_GB, C_GC, C_UCONV, C_UPOOL = (0, 512), (512, 256), (768, 256), (1024, 256), (1280, 256), (1536, 256)
C_KSW, C_VSW, C_CKV, C_KR, C_KRP = (1792, 256), (2048, 256), (2304, 128), (2432, 128), (2560, 128)

SHARDED = (("w_in", (DEPTH, 1024, 244), 2), ("w_uq", (DEPTH, 256, 48), 2), ("w_ukv", (DEPTH, 128, 64), 2),
           ("conv_w", (DEPTH, 3, 32), 2), ("w_o", (DEPTH, 128, 1024), 1), ("w_gate_up", (DEPTH, 1024, 704), 2),
           ("w_down", (DEPTH, 352, 1024), 1))
REPLICATED = (("attn_norm", (DEPTH, 1024)), ("mla_q_norm", (DEPTH, 256)), ("mla_kv_norm", (DEPTH, 128)),
              ("pool_w", (DEPTH, 4, 64, 64)), ("pool_scale", (DEPTH, 256)), ("swa_sinks", (DEPTH, 4)),
              ("mix_norm", (DEPTH, 1024)), ("ffn_norm", (DEPTH, 1024)), ("final_norm", (1024,)), ("loss", (1,)))
WEIGHT_ORDER = ("attn_norm", "w_in", "mla_q_norm", "w_uq", "mla_kv_norm", "w_ukv", "conv_w", "pool_w", "pool_scale",
                "swa_sinks", "mix_norm", "w_o", "ffn_norm", "w_gate_up", "w_down", "final_norm")


def _params(sem):
    return pltpu.CompilerParams(dimension_semantics=sem, vmem_limit_bytes=VMEM_LIMIT)


def _pick(dim, target):
    if dim <= target:
        return dim
    best = None
    for t in range(LANES, target + 1, LANES):
        if dim % t == 0:
            best = t
    assert best is not None, (dim, target)
    return best


def _mm(a, b, *, name, ta=False, tb=False, res=None, out_dtype=F32, tm=1024, tn=1024, tk=1024, epilogue=None):
    m, k = (a.shape[1], a.shape[0]) if ta else a.shape
    n = b.shape[0] if tb else b.shape[1]
    assert (b.shape[1] if tb else b.shape[0]) == k
    tm, tn, tk = _pick(m, tm), _pick(n, tn), _pick(k, tk)
    nj, nk = n // tn, k // tk
    dims = (((0 if ta else 1,), (1 if tb else 0,)), ((), ()))
    fn, extra, outs = epilogue if epilogue is not None else (None, [], [(n, out_dtype)])
    if res is not None:
        assert epilogue is None
        fn, extra = (lambda acc, r: (acc + r,)), [res]
    n_in, n_out = 2 + len(extra), len(outs)

    def body(*refs):
        a_ref, b_ref, acc_ref = refs[0], refs[1], refs[-1]
        kk = pl.program_id(2)

        def product():
            return lax.dot_general(a_ref[...].astype(BF16), b_ref[...].astype(BF16), dims, preferred_element_type=F32)

        def finish(acc):
            tiles = (acc,) if fn is None else fn(acc, *[r[...] for r in refs[2:n_in]])
            for o_ref, tile in zip(refs[n_in:n_in + n_out], tiles):
                o_ref[...] = tile.astype(o_ref.dtype)

        if nk == 1:
            finish(product())
            return

        @pl.when(kk == 0)
        def _():
            acc_ref[...] = jnp.zeros_like(acc_ref)

        acc_ref[...] += product()

        @pl.when(kk == nk - 1)
        def _():
            finish(acc_ref[...])

    def col_tiles(width):
        assert width % (nj * LANES) == 0, (width, nj)
        return pl.BlockSpec((tm, width // nj), lambda i, j, kk: (i, j))

    a_spec = pl.BlockSpec((tk, tm), lambda i, j, kk: (kk, i)) if ta else pl.BlockSpec((tm, tk), lambda i, j, kk: (i, kk))
    b_spec = pl.BlockSpec((tn, tk), lambda i, j, kk: (j, kk)) if tb else pl.BlockSpec((tk, tn), lambda i, j, kk: (kk, j))
    res_ = pl.pallas_call(
        body, name=name, grid=(m // tm, nj, nk), in_specs=[a_spec, b_spec] + [col_tiles(e.shape[1]) for e in extra],
        out_specs=[col_tiles(w) for w, _ in outs],
        out_shape=[jax.ShapeDtypeStruct((m, w), dt) for w, dt in outs],
        scratch_shapes=[pltpu.VMEM((tm, tn), F32)] if nk > 1 else [],
        compiler_params=_params(("parallel", "parallel", "arbitrary")),
    )(a, b, *extra)
    return res_[0] if epilogue is None else res_


def _rowwise(fn, *, name, rows, ins, outs, reds=(), tm=512):
    tm = min(tm, rows)
    assert rows % tm == 0
    n_in, n_out = len(ins), len(outs)

    def body(*refs):
        vals = [jnp.concatenate([r[h] for h in range(r.shape[0])], axis=1) if spec[0] == "heads" else r[...]
                for spec, r in zip(ins, refs[:n_in])]
        res = fn(*vals)
        for out, r, v in zip(outs, refs[n_in:n_in + n_out], res[:n_out]):
            if len(out) >= 3:
                for h in range(out[2]):
                    piece = v[:, h * LANES:(h + 1) * LANES]
                    r[h] = (piece.T if len(out) == 4 else piece).astype(r.dtype)
            else:
                r[...] = v.astype(r.dtype)
        if reds:
            @pl.when(pl.program_id(0) == 0)
            def _():
                for r in refs[n_in + n_out:]:
                    r[...] = jnp.zeros_like(r)

            for r, v in zip(refs[n_in + n_out:], res[n_out:]):
                r[...] += v

    in_specs, args = [], []
    for spec in ins:
        if spec[0] == "row":
            _, arr, width, blk = spec
            in_specs.append(pl.BlockSpec((tm, width), functools.partial(lambda i, blk: (i, blk), blk=blk)))
        elif spec[0] == "heads":
            arr = spec[1]
            in_specs.append(pl.BlockSpec((arr.shape[0], tm, LANES), lambda i: (0, i, 0)))
        else:
            arr = spec[1]
            in_specs.append(pl.BlockSpec(arr.shape, functools.partial(lambda i, nd: (0,) * nd, nd=arr.ndim)))
        args.append(arr)
    def out_spec(o):
        if len(o) == 4:
            return pl.BlockSpec((o[2], LANES, tm), lambda i: (0, 0, i)), (o[2], LANES, rows)
        if len(o) == 3:
            return pl.BlockSpec((o[2], tm, LANES), lambda i: (0, i, 0)), (o[2], rows, LANES)
        return pl.BlockSpec((tm, o[0]), lambda i: (i, 0)), (rows, o[0])

    out_specs = [out_spec(o)[0] for o in outs]
    out_shape = [jax.ShapeDtypeStruct(out_spec(o)[1], o[1]) for o in outs]
    out_specs += [pl.BlockSpec((r, w), lambda i: (0, 0)) for r, w in reds]
    out_shape += [jax.ShapeDtypeStruct((r, w), F32) for r, w in reds]
    return pl.pallas_call(body, name=name, grid=(rows // tm,), in_specs=in_specs, out_specs=out_specs,
                          out_shape=out_shape, compiler_params=_params(("arbitrary",)))(*args)


def _rstd(x, n):
    return lax.rsqrt(jnp.sum(x * x, axis=-1, keepdims=True) * (1.0 / n) + RMS_EPS)


def _rms_fwd(x_spec, g, *, name, rows, width):
    def fn(x, gv):
        return (x * _rstd(x, width) * gv,)
    return _rowwise(fn, name=name, rows=rows, ins=[x_spec, ("full", g)], outs=[(width, BF16)])[0]


def _rms_bwd(x_spec, g, dy, res, *, name, rows, width, out_dtypes):
    def fn(x, gv, dyv, *rest):
        r = _rstd(x, width)
        dyg = dyv * gv
        dx = r * dyg - x * (r * r * r) * (jnp.sum(dyg * x, axis=-1, keepdims=True) * (1.0 / width))
        if rest:
            dx = dx + rest[0]
        return (dx,) * len(out_dtypes) + (jnp.sum(dyv * x * r, axis=0, keepdims=True),)

    ins = [x_spec, ("full", g), ("row", dy, width, 0)]
    if res is not None:
        ins.append(("row", res, width, 0))
    return _rowwise(fn, name=name, rows=rows, ins=ins, outs=[(width, dt) for dt in out_dtypes], reds=[(1, width)])


NT_DIMS = (((1,), (1,)), ((), ()))
TN_DIMS = (((0,), (0,)), ((), ()))
BNT_DIMS = (((2,), (2,)), ((0,), (0,)))
BNN_DIMS = (((2,), (1,)), ((0,), (0,)))


def _mla_tile(s):
    return min(512, s)


def _block_tables(pairs):
    return jnp.asarray([a for a, _ in pairs], jnp.int32), jnp.asarray([b for _, b in pairs], jnp.int32)


def _causal(shape, query_axis):
    return lax.broadcasted_iota(jnp.int32, shape, query_axis) >= lax.broadcasted_iota(jnp.int32, shape, 1 - query_axis)


def _mla_fwd(qa, ka, vta, *, name, s, nh):
    t = _mla_tile(s)
    nq = s // t

    pairs = [(i, j) for i in range(nq) for j in range(i + 1)]

    def body(qb_ref, kb_ref, q_ref, k_ref, vt_ref, o_ref, lse_ref, m_s, acc_s):
        i, j = qb_ref[pl.program_id(0)], kb_ref[pl.program_id(0)]

        @pl.when(j == 0)
        def _():
            m_s[...] = jnp.full_like(m_s, NEG)
            acc_s[...] = jnp.zeros_like(acc_s)

        def step(diag):
            sc = lax.dot_general(k_ref[...], q_ref[...], BNT_DIMS, preferred_element_type=F32)
            if diag:
                sc = jnp.where(_causal(sc.shape[1:], 1)[None], sc, NEG)
            m_prev = m_s[...]
            m_new = jnp.maximum(m_prev, jnp.max(sc, axis=1, keepdims=True))
            p = jnp.exp(sc - m_new).astype(BF16)
            acc_s[...] = (jnp.exp(m_prev - m_new) * acc_s[...]
                          + lax.dot_general(vt_ref[...], p, BNN_DIMS, preferred_element_type=F32))
            m_s[...] = m_new

        pl.when(j < i)(functools.partial(step, False))
        pl.when(j == i)(functools.partial(step, True))

        @pl.when(j == i)
        def _():
            row = lax.broadcasted_iota(jnp.int32, (LANES, t), 0)
            for h in range(nh):
                acc = acc_s[h]
                l = acc[HEAD:HEAD + 1, :]
                o_ref[h] = jnp.where(row < HEAD, acc / l, 0.0).T
                lse_ref[h] = m_s[h] + jnp.log(l)

    q_spec = pl.BlockSpec((nh, t, LANES), lambda p, qb, kb: (0, qb[p], 0))
    k_spec = pl.BlockSpec((nh, t, LANES), lambda p, qb, kb: (0, kb[p], 0))
    vt_spec = pl.BlockSpec((nh, LANES, t), lambda p, qb, kb: (0, 0, kb[p]))
    return pl.pallas_call(
        body, name=name,
        grid_spec=pltpu.PrefetchScalarGridSpec(
            num_scalar_prefetch=2, grid=(len(pairs),), in_specs=[q_spec, k_spec, vt_spec],
            out_specs=[q_spec, pl.BlockSpec((nh, 1, t), lambda p, qb, kb: (0, 0, qb[p]))],
            scratch_shapes=[pltpu.VMEM((nh, 1, t), F32), pltpu.VMEM((nh, LANES, t), F32)]),
        out_shape=[jax.ShapeDtypeStruct((nh, s, LANES), F32), jax.ShapeDtypeStruct((nh, 1, s), F32)],
        compiler_params=_params(("arbitrary",)),
    )(*_block_tables(pairs), qa, ka, vta)


def _mla_bwd(qa, ka, kta, va, o, do, lse_row, after, *, name, s, nh):
    t = _mla_tile(s)
    nq = s // t

    pairs = [(kb, j) for kb in range(nq) for j in range(kb, nq)]

    def body(kb_ref, qb_ref, q_ref, k_ref, kt_ref, v_ref, o_ref, do_ref, lse_ref, after_ref, dq_hbm, dk_ref, dv_ref,
             dqt_s, dk_s, dv_s, d_s, stage, sem):
        kb, j = kb_ref[pl.program_id(0)], qb_ref[pl.program_id(0)]
        cols = pl.ds(pl.multiple_of(j * t, t), t)

        @pl.when(j == kb)
        def _():
            dk_s[...] = jnp.zeros_like(dk_s)
            dv_s[...] = jnp.zeros_like(dv_s)

        @pl.when(kb == 0)
        def _():
            dqt_s[:, :, cols] = jnp.zeros((nh, LANES, t), F32)
            for h in range(nh):
                d_col = jnp.sum(do_ref[h] * o_ref[h], axis=1, keepdims=True)
                d_s[h, :, cols] = jnp.broadcast_to(d_col, (t, LANES)).T[0:1, :]

        def step(diag):
            q = q_ref[...]
            do_b = do_ref[...].astype(BF16)
            sc = lax.dot_general(k_ref[...], q, BNT_DIMS, preferred_element_type=F32)
            if diag:
                sc = jnp.where(_causal(sc.shape[1:], 1)[None], sc, NEG)
            p = jnp.exp(sc - lse_ref[...])
            dv_s[...] += lax.dot_general(p.astype(BF16), do_b, BNN_DIMS, preferred_element_type=F32)
            dp = lax.dot_general(v_ref[...], do_b, BNT_DIMS, preferred_element_type=F32)
            ds = (p * (dp - d_s[:, :, cols])).astype(BF16)
            dk_s[...] += lax.dot_general(ds, q, BNN_DIMS, preferred_element_type=F32)
            dqt_s[:, :, cols] += lax.dot_general(kt_ref[...], ds, BNN_DIMS, preferred_element_type=F32)

        pl.when(j > kb)(functools.partial(step, False))
        pl.when(j == kb)(functools.partial(step, True))

        @pl.when(j == kb)
        def _():
            for h in range(nh):
                stage[h] = dqt_s[h, :, cols].T
            out = pltpu.make_async_copy(stage, dq_hbm.at[:, cols, :], sem)
            out.start()
            out.wait()

        @pl.when(j == nq - 1)
        def _():
            dk_ref[...] = dk_s[...]
            dv_ref[...] = dv_s[...].astype(dv_ref.dtype)

    q_spec = pl.BlockSpec((nh, t, LANES), lambda p, kb, qb: (0, qb[p], 0))
    kv_spec = pl.BlockSpec((nh, t, LANES), lambda p, kb, qb: (0, kb[p], 0))
    kt_spec = pl.BlockSpec((nh, LANES, t), lambda p, kb, qb: (0, 0, kb[p]))
    row_spec = pl.BlockSpec((nh, 1, t), lambda p, kb, qb: (0, 0, qb[p]))
    whole = jax.ShapeDtypeStruct((nh, s, LANES), F32)
    return pl.pallas_call(
        body, name=name,
        grid_spec=pltpu.PrefetchScalarGridSpec(
            num_scalar_prefetch=2, grid=(len(pairs),),
            in_specs=[q_spec, kv_spec, kt_spec, kv_spec, q_spec, q_spec, row_spec, pl.BlockSpec(memory_space=pl.ANY)],
            out_specs=[pl.BlockSpec(memory_space=pl.ANY), kv_spec, kv_spec],
            scratch_shapes=[pltpu.VMEM((nh, LANES, s), F32), pltpu.VMEM((nh, t, LANES), F32),
                            pltpu.VMEM((nh, t, LANES), F32), pltpu.VMEM((nh, 1, s), F32), pltpu.VMEM((nh, t, LANES), F32),
                            pltpu.SemaphoreType.DMA]),
        out_shape=[whole, whole, jax.ShapeDtypeStruct((nh, s, LANES), BF16)],
        compiler_params=_params(("arbitrary",)),
    )(*_block_tables(pairs), qa, ka, kta, va, o, do, lse_row, after)


SWA_PIECE = 128
SWA_KEYS = 2 * SWA_PIECE


def _swa_block(s):
    return min(512, s)


def _swa_piece(hp_ref, h, q, k_ref, v_ref, qpos0, scale):
    kstart = pl.multiple_of(jnp.maximum(qpos0 - SWA_PIECE, 0), SWA_PIECE)
    k = k_ref[pl.ds(kstart, SWA_KEYS), :].astype(BF16)
    v = v_ref[pl.ds(kstart, SWA_KEYS), :].astype(BF16)
    sc = lax.dot_general(q, k, NT_DIMS, preferred_element_type=F32)
    dist = (qpos0 + lax.broadcasted_iota(jnp.int32, sc.shape, 0)) - (kstart + lax.broadcasted_iota(jnp.int32, sc.shape, 1))
    sc = sc * scale - hp_ref[h, 0] * dist.astype(F32)
    sc = jnp.where((dist >= 0) & (dist < SWA_WINDOW), sc, NEG)
    return kstart, k, v, sc


def _swa_fwd(proj, hp, *, name, s, scale):
    tb = _swa_block(s)
    group = SWA_HEADS // SWA_KV_HEADS
    q_off, k_off, v_off = C_QSW[0] // LANES, C_KSW[0] // LANES, C_VSW[0] // LANES

    def body(hp_ref, q_ref, k_ref, v_ref, o_ref, lse_ref):
        h, i = pl.program_id(0), pl.program_id(1)
        sink = hp_ref[h, 1]
        for r in range(0, tb, SWA_PIECE):
            rows = pl.ds(r, SWA_PIECE)
            _, _, v, sc = _swa_piece(hp_ref, h, q_ref[rows, :].astype(BF16), k_ref, v_ref, i * tb + r, scale)
            m = jnp.maximum(jnp.max(sc, axis=1, keepdims=True), sink)
            p = jnp.exp(sc - m)
            l = jnp.sum(p, axis=1, keepdims=True) + jnp.exp(sink - m)
            o_ref[rows, :] = jnp.dot(p.astype(BF16), v, preferred_element_type=F32) / l
            lse_ref[rows, :] = m + jnp.log(l)

    whole = lambda off: pl.BlockSpec((s, LANES), lambda h, i: (0, off + h // group))
    return pl.pallas_call(
        body, name=name, grid=(SWA_HEADS, s // tb),
        in_specs=[pl.BlockSpec(memory_space=pltpu.SMEM), pl.BlockSpec((tb, LANES), lambda h, i: (i, q_off + h)),
                  whole(k_off), whole(v_off)],
        out_specs=[pl.BlockSpec((tb, LANES), lambda h, i: (i, h)), pl.BlockSpec((None, tb, 1), lambda h, i: (h, i, 0))],
        out_shape=[jax.ShapeDtypeStruct((s, SWA_HEADS * LANES), F32), jax.ShapeDtypeStruct((SWA_HEADS, s, 1), F32)],
        compiler_params=_params(("parallel", "parallel")),
    )(hp, proj, proj, proj)


def _swa_bwd(proj, o, do, lse, hp, *, name, s, scale):
    tb = _swa_block(s)
    nqb = s // tb
    group = SWA_HEADS // SWA_KV_HEADS
    q_off, k_off, v_off = C_QSW[0] // LANES, C_KSW[0] // LANES, C_VSW[0] // LANES

    def body(hp_ref, q_ref, k_ref, v_ref, o_ref, do_ref, lse_ref, dq_ref, dk_ref, dv_ref, dsink_ref):
        kh, g, i = pl.program_id(0), pl.program_id(1), pl.program_id(2)
        h = kh * group + g
        sink = hp_ref[h, 1]

        @pl.when((g == 0) & (i == 0))
        def _():
            dk_ref[...] = jnp.zeros_like(dk_ref)
            dv_ref[...] = jnp.zeros_like(dv_ref)

        @pl.when(i == 0)
        def _():
            dsink_ref[...] = jnp.zeros_like(dsink_ref)

        for r in range(0, tb, SWA_PIECE):
            rows = pl.ds(r, SWA_PIECE)
            q = q_ref[rows, :].astype(BF16)
            dov = do_ref[rows, :]
            do_b = dov.astype(BF16)
            lse_r = lse_ref[rows, :]
            d_r = jnp.sum(dov * o_ref[rows, :], axis=1, keepdims=True)
            kstart, k, v, sc = _swa_piece(hp_ref, h, q, k_ref, v_ref, i * tb + r, scale)
            p = jnp.exp(sc - lse_r)
            dp = lax.dot_general(do_b, v, NT_DIMS, preferred_element_type=F32)
            ds = (p * (dp - d_r)).astype(BF16)
            dq_ref[rows, :] = (jnp.dot(ds, k, preferred_element_type=F32) * scale).astype(dq_ref.dtype)
            win = pl.ds(kstart, SWA_KEYS)
            dk_ref[win, :] += lax.dot_general(ds, q, TN_DIMS, preferred_element_type=F32) * scale
            dv_ref[win, :] += lax.dot_general(p.astype(BF16), do_b, TN_DIMS, preferred_element_type=F32)
            part = jnp.sum(-jnp.exp(sink - lse_r) * d_r, axis=0, keepdims=True)
            dsink_ref[...] += jnp.broadcast_to(part, (1, LANES))

    whole = lambda off: pl.BlockSpec((s, LANES), lambda kh, g, i: (0, off + kh))
    q_map = lambda kh, g, i: (i, kh * group + g)
    return pl.pallas_call(
        body, name=name, grid=(SWA_KV_HEADS, group, nqb),
        in_specs=[pl.BlockSpec(memory_space=pltpu.SMEM),
                  pl.BlockSpec((tb, LANES), lambda kh, g, i: (i, q_off + kh * group + g)), whole(k_off), whole(v_off),
                  pl.BlockSpec((tb, LANES), q_map), pl.BlockSpec((tb, LANES), q_map),
                  pl.BlockSpec((None, tb, 1), lambda kh, g, i: (kh * group + g, i, 0))],
        out_specs=[pl.BlockSpec((tb, LANES), q_map), whole(0), whole(0),
                   pl.BlockSpec((None, 1, LANES), lambda kh, g, i: (kh * group + g, 0, 0))],
        out_shape=[jax.ShapeDtypeStruct((s, SWA_HEADS * LANES), BF16),
                   jax.ShapeDtypeStruct((s, SWA_KV_HEADS * LANES), F32), jax.ShapeDtypeStruct((s, SWA_KV_HEADS * LANES), F32),
                   jax.ShapeDtypeStruct((SWA_HEADS, 1, LANES), F32)],
        compiler_params=_params(("parallel", "arbitrary", "arbitrary")),
    )(hp, proj, proj, proj, o, do, lse)


def _shift_down(z, k):
    rows = lax.broadcasted_iota(jnp.int32, z.shape, 0)
    return jnp.where(rows >= k, pltpu.roll(z, k, 0), 0.0)


def _shift_up(z, k):
    n = z.shape[0]
    rows = lax.broadcasted_iota(jnp.int32, z.shape, 0)
    return jnp.where(rows < n - k, pltpu.roll(z, n - k, 0), 0.0)


def _rows3(a, b, c):
    r = lax.broadcasted_iota(jnp.int32, (3, a.shape[1]), 0)
    return jnp.where(r == 0, a, jnp.where(r == 1, b, c))


def _col_spec(s, off):
    return pl.BlockSpec((s, LANES), functools.partial(lambda j, off: (0, off + j), off=off))


def _conv_fwd(proj, conv_w, *, name, s):
    def body(gb_ref, gc_ref, u_ref, w_ref, y_ref):
        w0, w1, w2 = w_ref[0:1, :], w_ref[1:2, :], w_ref[2:3, :]
        z = gc_ref[...] * u_ref[...]
        c = w2 * z + w1 * _shift_down(z, 1) + w0 * _shift_down(z, 2)
        y_ref[...] = gb_ref[...] * c

    return pl.pallas_call(
        body, name=name, grid=(2,),
        in_specs=[_col_spec(s, C_GB[0] // LANES), _col_spec(s, C_GC[0] // LANES), _col_spec(s, C_UCONV[0] // LANES),
                  pl.BlockSpec((3, LANES), lambda j: (0, j))],
        out_specs=_col_spec(s, 0), out_shape=jax.ShapeDtypeStruct((s, D_GROUP), F32),
        compiler_params=_params(("parallel",)),
    )(proj, proj, proj, conv_w)


def _conv_bwd(dy, proj, conv_w, *, name, s):
    def body(dy_ref, gb_ref, gc_ref, u_ref, w_ref, dgb_ref, dgc_ref, du_ref, dw_ref):
        w0, w1, w2 = w_ref[0:1, :], w_ref[1:2, :], w_ref[2:3, :]
        gc, u, dyv = gc_ref[...], u_ref[...], dy_ref[...]
        z = gc * u
        z1, z2 = _shift_down(z, 1), _shift_down(z, 2)
        c = w2 * z + w1 * z1 + w0 * z2
        dgb_ref[...] = (dyv * c).astype(dgb_ref.dtype)
        dc = dyv * gb_ref[...]
        dz = w2 * dc + w1 * _shift_up(dc, 1) + w0 * _shift_up(dc, 2)
        dgc_ref[...] = (dz * u).astype(dgc_ref.dtype)
        du_ref[...] = (dz * gc).astype(du_ref.dtype)
        dw_ref[...] = _rows3(jnp.sum(dc * z2, axis=0, keepdims=True), jnp.sum(dc * z1, axis=0, keepdims=True),
                             jnp.sum(dc * z, axis=0, keepdims=True))

    act = jax.ShapeDtypeStruct((s, D_GROUP), BF16)
    return pl.pallas_call(
        body, name=name, grid=(2,),
        in_specs=[_col_spec(s, 0), _col_spec(s, C_GB[0] // LANES), _col_spec(s, C_GC[0] // LANES),
                  _col_spec(s, C_UCONV[0] // LANES), pl.BlockSpec((3, LANES), lambda j: (0, j))],
        out_specs=[_col_spec(s, 0), _col_spec(s, 0), _col_spec(s, 0), pl.BlockSpec((3, LANES), lambda j: (0, j))],
        out_shape=[act, act, act, jax.ShapeDtypeStruct((3, D_GROUP), F32)],
        compiler_params=_params(("parallel",)),
    )(dy, proj, proj, proj, conv_w)


def _pool_select(j, lane, a2, a4, a8, a16):
    lo = lane < HEAD
    return jnp.where(j == 0, jnp.where(lo, a2, a4), jnp.where(lo, a8, a16))


def _pooled(u, j):
    s2 = u + _shift_down(u, 1)
    s4 = s2 + _shift_down(s2, 2)
    s8 = s4 + _shift_down(s4, 4)
    s16 = s8 + _shift_down(s8, 8)
    lane = lax.broadcasted_iota(jnp.int32, u.shape, 1)
    rows = lax.broadcasted_iota(jnp.int32, u.shape, 0)
    win = _pool_select(j, lane, *POOL_WINDOWS)
    count = jnp.minimum(rows + 1, win).astype(F32)
    return _pool_select(j, lane, s2, s4, s8, s16) / count - u, count


def _pool_fwd(proj, wbd, scale, *, name, s):
    def body(u_ref, w_ref, sc_ref, y_ref):
        pooled, _ = _pooled(u_ref[...], pl.program_id(0))
        y_ref[...] = jnp.dot(pooled.astype(BF16), w_ref[...].astype(BF16), preferred_element_type=F32) * sc_ref[...]

    return pl.pallas_call(
        body, name=name, grid=(2,),
        in_specs=[_col_spec(s, C_UPOOL[0] // LANES), pl.BlockSpec((None, LANES, LANES), lambda j: (j, 0, 0)),
                  pl.BlockSpec((1, LANES), lambda j: (0, j))],
        out_specs=_col_spec(s, 0), out_shape=jax.ShapeDtypeStruct((s, D_GROUP), F32),
        compiler_params=_params(("parallel",)),
    )(proj, wbd, scale)


def _pool_bwd(dy, proj, wbd, scale, *, name, s):
    def body(dy_ref, u_ref, w_ref, sc_ref, du_ref, dw_ref, dsc_ref):
        j = pl.program_id(0)
        pooled, count = _pooled(u_ref[...], j)
        pooled_b = pooled.astype(BF16)
        w_b = w_ref[...].astype(BF16)
        dyv = dy_ref[...]
        mixed = jnp.dot(pooled_b, w_b, preferred_element_type=F32)
        dsc_ref[...] = jnp.sum(dyv * mixed, axis=0, keepdims=True)
        dms = (dyv * sc_ref[...]).astype(BF16)
        dw_ref[...] = lax.dot_general(pooled_b, dms, (((0,), (0,)), ((), ())), preferred_element_type=F32)
        dpooled = lax.dot_general(dms, w_b, (((1,), (1,)), ((), ())), preferred_element_type=F32)
        r = dpooled / count
        a2 = r + _shift_up(r, 1)
        a4 = a2 + _shift_up(a2, 2)
        a8 = a4 + _shift_up(a4, 4)
        a16 = a8 + _shift_up(a8, 8)
        lane = lax.broadcasted_iota(jnp.int32, r.shape, 1)
        du_ref[...] = (_pool_select(j, lane, a2, a4, a8, a16) - dpooled).astype(du_ref.dtype)

    return pl.pallas_call(
        body, name=name, grid=(2,),
        in_specs=[_col_spec(s, 0), _col_spec(s, C_UPOOL[0] // LANES),
                  pl.BlockSpec((None, LANES, LANES), lambda j: (j, 0, 0)), pl.BlockSpec((1, LANES), lambda j: (0, j))],
        out_specs=[_col_spec(s, 0), pl.BlockSpec((None, LANES, LANES), lambda j: (j, 0, 0)),
                   pl.BlockSpec((1, LANES), lambda j: (0, j))],
        out_shape=[jax.ShapeDtypeStruct((s, D_GROUP), BF16), jax.ShapeDtypeStruct((2, LANES, LANES), F32),
                   jax.ShapeDtypeStruct((1, D_GROUP), F32)],
        compiler_params=_params(("parallel",)),
    )(dy, proj, wbd, scale)


def _mesh_pos():
    return lax.axis_index("x"), lax.axis_index("y"), lax.axis_index("c")


def _any_specs(n):
    return [pl.BlockSpec(memory_space=pl.ANY)] * n


def _all_gather(xs, *, name):
    n = len(xs)

    def body(*refs):
        x_refs, out_refs = refs[:n], refs[n:2 * n]
        send_sems, recv_sems, local_sems = refs[2 * n:]
        x, y, cc = _mesh_pos()
        me, sibling = (x, y, cc), (x, y, 1 - cc)
        chips = [(1 - x, y), (x, 1 - y), (1 - x, 1 - y)]

        def slot(a, px, py, pc):
            return out_refs[a].at[4 * px + 2 * py + pc]

        def copy(a, k, block, to, src=None):
            return pltpu.make_async_remote_copy(
                src_ref=slot(a, *block) if src is None else src, dst_ref=slot(a, *block), send_sem=send_sems.at[a, k],
                recv_sem=recv_sems.at[a, k], device_id=to, device_id_type=pl.DeviceIdType.MESH)

        mine = [pltpu.make_async_copy(x_refs[a], slot(a, *me), local_sems.at[a]) for a in range(n)]
        first = []
        for a in range(n):
            first.append(copy(a, 0, me, sibling, src=x_refs[a]))
            first += [copy(a, 1 + j, me, (*chip, cc), src=x_refs[a]) for j, chip in enumerate(chips)]
        for cp in mine + first:
            cp.start()
        passed = []
        for j, chip in enumerate(chips):
            for a in range(n):
                copy(a, 1 + j, (*chip, cc), me).wait_recv()
                passed.append(copy(a, 4 + j, (*chip, cc), sibling))
                passed[-1].start()
        for a in range(n):
            copy(a, 0, sibling, me).wait_recv()
        for j, chip in enumerate(chips):
            for a in range(n):
                copy(a, 4 + j, (*chip, 1 - cc), me).wait_recv()
        for cp in first + passed:
            cp.wait_send()
        for cp in mine:
            cp.wait()

    return pl.pallas_call(
        body, name=name, out_shape=[jax.ShapeDtypeStruct((N_DEV,) + a.shape, a.dtype) for a in xs],
        in_specs=_any_specs(n), out_specs=_any_specs(n),
        scratch_shapes=[pltpu.SemaphoreType.DMA((n, 7)), pltpu.SemaphoreType.DMA((n, 7)), pltpu.SemaphoreType.DMA((n,))],
    )(*xs)


def _plan_gather_near(src_refs, land_refs, send_sems, recv_sems):
    x, y, cc = _mesh_pos()
    me = 4 * x + 2 * y + cc
    plan = []
    for a, (src, land) in enumerate(zip(src_refs, land_refs)):
        for k, (px, py, pc) in enumerate([(x, y, 1 - cc), (1 - x, y, cc), (x, 1 - y, cc), (1 - x, 1 - y, cc)]):
            sems = dict(send_sem=send_sems.at[4 * a + k], recv_sem=recv_sems.at[4 * a + k], device_id=(px, py, pc),
                        device_id_type=pl.DeviceIdType.MESH)
            plan.append((pltpu.make_async_remote_copy(src_ref=src, dst_ref=land.at[me], **sems),
                         pltpu.make_async_remote_copy(src_ref=src, dst_ref=land.at[4 * px + 2 * py + pc], **sems)))
    return plan


def _plan_gather_pass(src_refs, land_refs, send_sems, recv_sems):
    x, y, cc = _mesh_pos()
    plan = []
    for a, land in enumerate(land_refs):
        for j, (px, py) in enumerate([(1 - x, y), (x, 1 - y), (1 - x, 1 - y)]):
            mine, theirs = land.at[4 * px + 2 * py + cc], land.at[4 * px + 2 * py + 1 - cc]
            sems = dict(send_sem=send_sems.at[3 * a + j], recv_sem=recv_sems.at[3 * a + j], device_id=(x, y, 1 - cc),
                        device_id_type=pl.DeviceIdType.MESH)
            plan.append((pltpu.make_async_remote_copy(src_ref=mine, dst_ref=mine, **sems),
                         pltpu.make_async_remote_copy(src_ref=mine, dst_ref=theirs, **sems)))
    return plan


def _plan_sibling(src_refs, land_refs, send_sems, recv_sems):
    x, y, cc = _mesh_pos()
    plan = []
    for a, (src, land) in enumerate(zip(src_refs, land_refs)):
        cp = pltpu.make_async_remote_copy(
            src_ref=src.at[1 - cc], dst_ref=land, send_sem=send_sems.at[a], recv_sem=recv_sems.at[a],
            device_id=(x, y, 1 - cc), device_id_type=pl.DeviceIdType.MESH)
        plan.append((cp, cp))
    return plan


def _plan_chips(src_refs, land_refs, send_sems, recv_sems):
    x, y, cc = _mesh_pos()
    my_chip = 2 * x + y
    plan = []
    for a, (src, land) in enumerate(zip(src_refs, land_refs)):
        for j, (px, py) in enumerate([(1 - x, y), (x, 1 - y), (1 - x, 1 - y)]):
            peer = 2 * px + py
            sems = dict(send_sem=send_sems.at[3 * a + j], recv_sem=recv_sems.at[3 * a + j], device_id=(px, py, cc),
                        device_id_type=pl.DeviceIdType.MESH)
            plan.append((pltpu.make_async_remote_copy(src_ref=src.at[peer], dst_ref=land.at[my_chip], **sems),
                         pltpu.make_async_remote_copy(src_ref=src.at[peer], dst_ref=land.at[peer], **sems)))
    return plan


HBM_SPEC = pl.BlockSpec(memory_space=pltpu.HBM)
SEM_SPEC = pl.BlockSpec(memory_space=pltpu.SEMAPHORE)
ANY_SPEC = pl.BlockSpec(memory_space=pl.ANY)
SIDE_EFFECT = pltpu.CompilerParams(has_side_effects=pltpu.SideEffectType.DATAFLOW_SIDE_EFFECTING)


def _start_copies(plan, sems_per_array, srcs, lands, after, *, name):
    lands = [lax.empty(l, a.dtype) if isinstance(l, tuple) else l for l, a in zip(lands, srcs or lands)]
    ns, n = len(srcs), len(srcs) + len(lands)

    def body(*refs):
        send_sems, recv_sems = refs[n + len(after)], refs[n + len(after) + 1]
        for out, _ in plan(refs[:ns], refs[ns:n], send_sems, recv_sems):
            out.start()
        refs[-1][...] = jnp.zeros_like(refs[-1])

    sem = pltpu.SemaphoreType.DMA((len(lands) * sems_per_array,))
    res = pl.pallas_call(
        body, name=name,
        out_shape=(sem, sem, *[pltpu.HBM(a.shape, a.dtype) for a in srcs + lands], jax.ShapeDtypeStruct((8, LANES), F32)),
        in_specs=[HBM_SPEC] * n + [ANY_SPEC] * len(after),
        out_specs=(SEM_SPEC, SEM_SPEC, *[HBM_SPEC] * n, pl.BlockSpec(memory_space=pltpu.VMEM)),
        input_output_aliases={i: 2 + i for i in range(n)}, compiler_params=SIDE_EFFECT,
    )(*[pltpu.with_memory_space_constraint(a, pltpu.HBM) for a in srcs + lands], *after)
    return (res[0], res[1], list(res[2:2 + ns]), list(res[2 + ns:2 + n])), res[-1]


def _wait_copies(plan, handle, after, *, name):
    send, recv, srcs, lands = handle
    ns, n = len(srcs), len(srcs) + len(lands)

    def body(*refs):
        for out, inc in plan(refs[:ns], refs[ns:n], refs[n], refs[n + 1]):
            out.wait_send()
            inc.wait_recv()

    res = pl.pallas_call(
        body, name=name, out_shape=tuple(pltpu.HBM(a.shape, a.dtype) for a in srcs + lands),
        in_specs=[HBM_SPEC] * n + [SEM_SPEC, SEM_SPEC] + [ANY_SPEC] * len(after), out_specs=[HBM_SPEC] * n,
        input_output_aliases={i: i for i in range(n)}, compiler_params=SIDE_EFFECT,
    )(*srcs, *lands, send, recv, *after)
    return list(res[:ns]), list(res[ns:])


def _row_tile(rows, target=512):
    if rows <= target:
        return rows
    best = None
    for t in range(8, target + 1, 8):
        if rows % t == 0:
            best = t
    assert best is not None, (rows, target)
    return best


def _add_own(g, other, core, *, name):
    _, _, rows, cols = g.shape
    tm = _row_tile(rows)

    def body(c_ref, g_ref, o_ref, out_ref):
        out_ref[...] = g_ref[...] + o_ref[...]

    return pl.pallas_call(
        body, name=name, out_shape=jax.ShapeDtypeStruct(other.shape, other.dtype),
        grid_spec=pltpu.PrefetchScalarGridSpec(
            num_scalar_prefetch=1, grid=(4, rows // tm),
            in_specs=[pl.BlockSpec((None, None, tm, cols), lambda p, i, c_ref: (c_ref[0], p, i, 0)),
                      pl.BlockSpec((None, tm, cols), lambda p, i, c_ref: (p, i, 0))],
            out_specs=pl.BlockSpec((None, tm, cols), lambda p, i, c_ref: (p, i, 0))),
        compiler_params=_params(("parallel", "parallel")),
    )(core, g, other)


def _adamw(parts, w, m, v, after, *, name):
    layers, rows, cols = w.shape
    assert len(parts) == layers
    tm = _row_tile(rows, 256)
    nr = rows // tm

    def body(*refs):
        p_refs = refs[:layers]
        w_ref, m_ref, v_ref, _, g_ref, d_ref, nm_ref, nv_ref, g_s = refs[layers:]
        for ll in range(layers):
            @pl.when(pl.program_id(0) == ll)
            def _(ll=ll):
                g = p_refs[ll][0]
                for q in range(1, p_refs[ll].shape[0]):
                    g = g + p_refs[ll][q]
                g_s[...] = g

        g = g_s[...]
        mm = ADAM_B1 * m_ref[...] + (1.0 - ADAM_B1) * g
        vv = ADAM_B2 * v_ref[...] + (1.0 - ADAM_B2) * jnp.square(g)
        m_hat = mm / (1.0 - ADAM_B1 ** ADAM_STEP)
        v_hat = vv / (1.0 - ADAM_B2 ** ADAM_STEP)
        g_ref[...] = g
        d_ref[...] = -ADAM_LR * (m_hat / (jnp.sqrt(v_hat) + ADAM_EPS) + ADAM_WD * w_ref[...])
        nm_ref[...] = mm
        nv_ref[...] = vv

    def part_spec(ll, p):
        return pl.BlockSpec((p, tm, cols), lambda l, i: (0, jnp.where(l == ll, i, jnp.where(l < ll, 0, nr - 1)), 0))

    spec = pl.BlockSpec((None, tm, cols), lambda l, i: (l, i, 0))
    out = jax.ShapeDtypeStruct(w.shape, F32)
    return pl.pallas_call(
        body, name=name, grid=(layers, nr),
        in_specs=[part_spec(ll, parts[ll].shape[0]) for ll in range(layers)] + [spec] * 3 + [pl.BlockSpec(memory_space=pl.ANY)],
        out_specs=[spec] * 4, out_shape=[out] * 4, scratch_shapes=[pltpu.VMEM((tm, cols), F32)],
        compiler_params=_params(("arbitrary", "arbitrary")),
    )(*parts, w, m, v, after)


def _pack(arrs):
    flat = jnp.concatenate([a.reshape(-1) for a in arrs])
    rows = -(-flat.shape[0] // (PACK_COLS * 16)) * 16
    return jnp.pad(flat, (0, rows * PACK_COLS - flat.shape[0])).reshape(rows, PACK_COLS)


def _unpack(packed, shapes):
    flat = packed.reshape(-1)
    out, off = [], 0
    for shp in shapes:
        n = int(np.prod(shp))
        out.append(flat[off:off + n].reshape(shp))
        off += n
    return out


def _shards_to_full(g, axis):
    if axis == 0:
        return g.reshape(g.shape[0] * g.shape[1], g.shape[2])
    return jnp.transpose(g, (1, 0, 2)).reshape(g.shape[1], g.shape[0] * g.shape[2])


def _full_to_shards(a, axis):
    if axis == 0:
        return jnp.transpose(a.reshape(4, 2, a.shape[0] // N_DEV, a.shape[1]), (1, 0, 2, 3))
    return jnp.transpose(a.reshape(a.shape[0], 4, 2, a.shape[1] // N_DEV), (2, 1, 0, 3))


def _zeros_like_cols(a, n):
    return jnp.zeros(a.shape[:-1] + (n,), a.dtype)


def _pad_heads(a, n):
    z = _zeros_like_cols(a, HEAD)
    return jnp.concatenate([p for h in range(n) for p in (a[..., h * HEAD:(h + 1) * HEAD], z)], axis=-1)


def _unpad_heads(a, n):
    return jnp.concatenate([a[..., h * LANES:h * LANES + HEAD] for h in range(n)], axis=-1)


def _seg(first, width, sign=1):
    return (width, [(first, sign)])


def _zero(width):
    return (width, [])


def _swapped(first):
    half = MLA_ROPE // 2
    return [_seg(first + half, half, -1), _seg(first, half)]


def _padded_heads(first, n):
    return [s for h in range(n) for s in (_seg(first + HEAD * h, HEAD), _zero(HEAD))]


def _layout_w_in():
    kr = 384
    return (_padded_heads(1440, 4) + [_seg(0, 256), _seg(416, 256), _seg(672, 256), _seg(928, 256), _seg(1184, 256)]
            + _padded_heads(1696, 2) + _padded_heads(1824, 2) + [_seg(256, 128)]
            + [_zero(HEAD), _seg(kr, MLA_ROPE), _seg(kr, MLA_ROPE)] + [_zero(HEAD)] + _swapped(kr) + _swapped(kr))


def _layout_w_uq():
    out = []
    for h in range(MLA_HEADS):
        out += [_seg(96 * h, MLA_NOPE), _seg(96 * h + MLA_NOPE, MLA_ROPE)] + _swapped(96 * h + MLA_NOPE)
    return out


def _layout_w_ukv():
    keys = [s for h in range(MLA_HEADS) for s in (_seg(LANES * h, HEAD), _zero(HEAD))]
    values = [s for h in range(MLA_HEADS) for s in (_seg(LANES * h + HEAD, HEAD), _zero(HEAD))]
    return keys + values


def _layout_w_gate_up():
    return [_seg(half + j, GU_TILE) for j in range(0, D_FF, GU_TILE) for half in (0, D_FF)]


LAYOUTS = dict(w_in=_layout_w_in(), w_uq=_layout_w_uq(), w_ukv=_layout_w_ukv(), w_gate_up=_layout_w_gate_up())
OWN_COLS = dict(w_in=1952, w_uq=384, w_ukv=512, w_gate_up=2 * D_FF)


def _plan_extend(layout, shard):
    plan = []
    for width, terms in layout:
        if not terms:
            plan.append((width, []))
            continue
        (first, sign), = terms
        while width:
            g, off = divmod(first, shard)
            w = min(width, shard - off)
            plan.append((w, [(g, off, sign)]))
            first, width = first + w, width - w
    return [plan]


def _plan_fold(layout, own_cols):
    sources = [[] for _ in range(own_cols)]
    e = 0
    for width, terms in layout:
        for first, sign in terms:
            for i in range(width):
                sources[first + i].append((e + i, sign))
        e += width
    shard = own_cols // N_DEV
    plans = {}
    for g in range(N_DEV):
        plan, n = [], g * shard
        while n < (g + 1) * shard:
            w = 1
            while n + w < (g + 1) * shard and [(c + w, sg) for c, sg in sources[n]] == sources[n + w]:
                w += 1
            plan.append((w, [(0, c, sg) for c, sg in sources[n]]))
            n += w
        plans[g] = plan
    return [plans[2 * p + c] for c in range(2) for p in range(4)]


def _assemble(src, plans, out_cols, out_dtype, *, name):
    g, rows, c = src.shape
    tm = _row_tile(rows, 256)

    def body(s_ref, o_ref):
        blocks = [s_ref[i].astype(F32) for i in range(g)]
        for d, plan in enumerate(plans):
            pieces = []
            for width, terms in plan:
                v = None
                for b, first, sign in terms:
                    t = blocks[b][:, first:first + width]
                    t = -t if sign < 0 else t
                    v = t if v is None else v + t
                pieces.append(jnp.zeros((tm, width), F32) if v is None else v)
            o_ref[d] = (pieces[0] if len(pieces) == 1 else jnp.concatenate(pieces, axis=1)).astype(o_ref.dtype)

    return pl.pallas_call(
        body, name=name, grid=(rows // tm,), in_specs=[pl.BlockSpec((g, tm, c), lambda i: (0, i, 0))],
        out_specs=pl.BlockSpec((len(plans), tm, out_cols), lambda i: (0, i, 0)),
        out_shape=jax.ShapeDtypeStruct((len(plans), rows, out_cols), out_dtype), compiler_params=_params(("parallel",)),
    )(src)


def _extend(nm, gathered, *, name):
    layout = LAYOUTS[nm]
    return _assemble(gathered, _plan_extend(layout, OWN_COLS[nm] // N_DEV), sum(w for w, _ in layout), BF16, name=name)[0]


def _fold_to_shards(nm, grad_ext, *, name):
    shards = _assemble(grad_ext[None], _plan_fold(LAYOUTS[nm], OWN_COLS[nm]), OWN_COLS[nm] // N_DEV, F32, name=name)
    return shards.reshape((2, 4) + shards.shape[1:])


def _rope_tables(s):
    inv = 1.0 / (ROPE_THETA ** (jnp.arange(0, MLA_ROPE, 2, dtype=F32) / MLA_ROPE))
    ang = jnp.arange(s, dtype=F32)[:, None] * inv[None, :]
    cos, sin = jnp.cos(ang), jnp.sin(ang)
    c32, s32 = jnp.concatenate([cos, cos], axis=1), jnp.concatenate([sin, sin], axis=1)
    zeros, ones = jnp.zeros((s, HEAD), F32), jnp.ones((s, HEAD), F32)
    tq = jnp.concatenate([ones, c32, s32], axis=1) * (1.0 / math.sqrt(MLA_NOPE + MLA_ROPE))
    return (jnp.tile(tq, (1, MLA_HEADS)), jnp.concatenate([zeros, c32, c32], axis=1),
            jnp.concatenate([zeros, s32, s32], axis=1))


def _gn(y):
    return y * _rstd(y, D_GROUP)


def _mixer_fwd(x, w, tabs, l, after_attention=None):
    s = x.shape[0]
    tq, tkc, tks = tabs
    n = lambda t: f"l{l}_{t}"
    h = _rms_fwd(("row", x, D_MODEL, 0), w["attn_norm"], name=n("attn_norm"), rows=s, width=D_MODEL)
    proj = _mm(h, w["w_in"], name=n("in_proj"))
    def prep(cq, ckv, kr, krp, gq, gkv, wuq, wukv, tqv, tc, ts):
        cqn = (cq * _rstd(cq, 256) * gq).astype(BF16)
        ckvn = (ckv * _rstd(ckv, 128) * gkv).astype(BF16)
        qe = jnp.dot(cqn, wuq, preferred_element_type=F32)
        kve = jnp.dot(ckvn, wukv, preferred_element_type=F32)
        kb = kr * tc + krp * ts
        kvv = kve[:, 512:]
        lane = lax.broadcasted_iota(jnp.int32, kvv.shape, 1) & (LANES - 1)
        v = jnp.where(lane == HEAD, 1.0, kvv)
        k = kve[:, :512] + jnp.tile(kb, (1, MLA_HEADS))
        return cqn, ckvn, qe * tqv, k, k, v, v

    cqn, ckvn, qm, km, kmt, vm, vmt = _rowwise(
        prep, name=n("mla_prep"), rows=s,
        ins=[("row", proj, 256, C_CQ[0] // 256), ("row", proj, 128, C_CKV[0] // 128),
             ("row", proj, 128, C_KR[0] // 128), ("row", proj, 128, C_KRP[0] // 128),
             ("full", w["mla_q_norm"]), ("full", w["mla_kv_norm"]), ("full", w["w_uq"]), ("full", w["w_ukv"]),
             ("row", tq, 512, 0), ("row", tkc, 128, 0), ("row", tks, 128, 0)],
        outs=[(256, BF16), (128, BF16), (512, BF16, MLA_HEADS), (512, BF16, MLA_HEADS), (512, BF16, MLA_HEADS, "T"),
              (512, BF16, MLA_HEADS), (512, BF16, MLA_HEADS, "T")])
    y_a, lse_row = _mla_fwd(qm, km, vmt, name=n("mla_fwd"), s=s, nh=MLA_HEADS)
    mix_norm = w["mix_norm"]
    if after_attention is not None:
        mix_norm = mix_norm + after_attention(y_a)[0, 0]
    y_b = _conv_fwd(proj, w["conv_w"], name=n("conv_fwd"), s=s)
    y_c = _pool_fwd(proj, w["pool_wbd"], w["pool_scale"], name=n("pool_fwd"), s=s)
    y_d, lse_d = _swa_fwd(proj, w["hp_swa"], name=n("swa_fwd"), s=s, scale=1.0 / math.sqrt(HEAD))

    def mix(ya, yb, yc, yd, mn):
        return (jnp.concatenate([_gn(_unpad_heads(ya, 4)), _gn(yb), _gn(yc), _gn(_unpad_heads(yd, 4))], axis=1) * mn,)

    mixed = _rowwise(mix, name=n("group_norm"), rows=s,
                     ins=[("heads", y_a), ("row", y_b, 256, 0), ("row", y_c, 256, 0), ("row", y_d, 512, 0),
                          ("full", mix_norm)], outs=[(D_MODEL, BF16)])[0]
    x1 = _mm(mixed, w["w_o"], res=x, name=n("out_proj"))
    saved = dict(x=x, h=h, proj=proj, cqn=cqn, ckvn=ckvn, qm=qm, km=km, kmt=kmt, vm=vm, y_a=y_a, lse_row=lse_row,
                 y_b=y_b, y_c=y_c, y_d=y_d, lse_d=lse_d, mixed=mixed)
    return x1, saved


def _ffn_fwd(x1, w, l):
    s = x1.shape[0]
    n = lambda t: f"l{l}_{t}"
    h2 = _rms_fwd(("row", x1, D_MODEL, 0), w["ffn_norm"], name=n("ffn_norm"), rows=s, width=D_MODEL)

    def swiglu(gu):
        g, u = gu[:, :GU_TILE], gu[:, GU_TILE:]
        return gu, g * jax.nn.sigmoid(g) * u

    gu, act = _mm(h2, w["w_gate_up"], tm=2048, tn=2 * GU_TILE, name=n("gate_up"),
                  epilogue=(swiglu, [], [(2 * D_FF, BF16), (D_FF, BF16)]))
    x2 = _mm(act, w["w_down"], res=x1, tk=D_FF // 2, name=n("down"))
    return x2, dict(x1=x1, h2=h2, gu=gu, act=act)


def _ffn_bwd_down(dx2, sv, w, l):
    n = lambda t: f"l{l}_{t}"

    def swiglu_bwd(da, gu):
        gt, u = gu[:, :GU_TILE].astype(F32), gu[:, GU_TILE:].astype(F32)
        sg = jax.nn.sigmoid(gt)
        return (jnp.concatenate([da * u * sg * (1.0 + gt * (1.0 - sg)), da * gt * sg], axis=1),)

    dgu = _mm(dx2[1], w["w_down"], tb=True, tm=2048, tn=GU_TILE, name=n("d_act"),
              epilogue=(swiglu_bwd, [sv["gu"]], [(2 * D_FF, BF16)]))[0]
    g = dict(w_down=_mm(sv["act"], dx2[1], ta=True, tm=D_FF // 2, name=n("dw_down")))
    return dgu, g


def _ffn_bwd_up(dx2, dgu, sv, w, l):
    s = dgu.shape[0]
    n = lambda t: f"l{l}_{t}"
    dh2 = _mm(dgu, w["w_gate_up"], tb=True, tk=D_FF // 2, name=n("d_h2"))
    g = dict(w_gate_up=_mm(sv["h2"], dgu, ta=True, tn=D_FF // 2, name=n("dw_gate_up")))
    dx1, dx1_b, g["ffn_norm"] = _rms_bwd(("row", sv["x1"], D_MODEL, 0), w["ffn_norm"], dh2, dx2[0],
                                         name=n("ffn_norm_bwd"), rows=s, width=D_MODEL, out_dtypes=(F32, BF16))
    return (dx1, dx1_b), g


def _mixer_bwd_out(dx1, sv, w, l):
    s = dx1[1].shape[0]
    n = lambda t: f"l{l}_{t}"
    dmixed = _mm(dx1[1], w["w_o"], tb=True, name=n("d_mixed"))
    g = dict(w_o=_mm(sv["mixed"], dx1[1], ta=True, name=n("dw_o")))

    def mix_bwd(dm, ya, yb, yc, yd, mn):
        outs, dmn = [], []
        for i, y in enumerate((_unpad_heads(ya, 4), yb, yc, _unpad_heads(yd, 4))):
            lo, hi = i * D_GROUP, (i + 1) * D_GROUP
            r = _rstd(y, D_GROUP)
            nrm = y * r
            dmg = dm[:, lo:hi]
            dn = dmg * mn[:, lo:hi]
            dy = r * (dn - nrm * (jnp.sum(dn * nrm, axis=-1, keepdims=True) * (1.0 / D_GROUP)))
            outs.append(_pad_heads(dy, 4) if i in (0, 3) else dy)
            dmn.append(jnp.sum(dmg * nrm, axis=0, keepdims=True))
        return (*outs, jnp.concatenate(dmn, axis=1))

    dy_a, dy_b, dy_c, dy_d, g["mix_norm"] = _rowwise(
        mix_bwd, name=n("group_norm_bwd"), rows=s,
        ins=[("row", dmixed, D_MODEL, 0), ("heads", sv["y_a"]), ("row", sv["y_b"], 256, 0),
             ("row", sv["y_c"], 256, 0), ("row", sv["y_d"], 512, 0), ("full", w["mix_norm"])],
        outs=[(512, F32, MLA_HEADS), (256, F32), (256, F32), (512, F32)], reds=[(1, D_MODEL)])
    return (dy_a, dy_b, dy_c, dy_d), g


def _mixer_bwd_in(dx1, dys, sv, w, tabs, l):
    s = dx1[0].shape[0]
    tq, tkc, tks = tabs
    n = lambda t: f"l{l}_{t}"
    dy_a, dy_b, dy_c, dy_d = dys
    g = {}

    proj = sv["proj"]
    dq_sw, dk_sw, dv_sw, dsink = _swa_bwd(proj, sv["y_d"], dy_d, sv["lse_d"], w["hp_swa"], name=n("swa_bwd"), s=s,
                                          scale=1.0 / math.sqrt(HEAD))
    g["swa_sinks"] = dsink[:, 0, 0]

    dqm, dkm, dvm = _mla_bwd(sv["qm"], sv["km"], sv["kmt"], sv["vm"], sv["y_a"], dy_a, sv["lse_row"], w["hp_swa"],
                             name=n("mla_bwd"), s=s, nh=MLA_HEADS)

    def rms_bwd(x, gv, dy, width):
        r = _rstd(x, width)
        dyg = dy * gv
        dx = r * dyg - x * (r * r * r) * (jnp.sum(dyg * x, axis=-1, keepdims=True) * (1.0 / width))
        return dx, jnp.sum(dy * x * r, axis=0, keepdims=True)

    def prep_bwd(dq, dk, dv, cq, ckv, gq, gkv, wuq, wukv, tqv, tc, ts):
        dkb = dk[:, 0:128] + dk[:, 128:256] + dk[:, 256:384] + dk[:, 384:512]
        dq_ext = (dq * tqv).astype(BF16)
        dkv_ext = jnp.concatenate([dk.astype(BF16), dv], axis=1)
        dcqn = lax.dot_general(dq_ext, wuq, NT_DIMS, preferred_element_type=F32)
        dckvn = lax.dot_general(dkv_ext, wukv, NT_DIMS, preferred_element_type=F32)
        dcq, dgq = rms_bwd(cq, gq, dcqn, 256)
        dckv, dgkv = rms_bwd(ckv, gkv, dckvn, 128)
        return dq_ext, dkv_ext, dkb * tc, dkb * ts, dcq, dckv, dgq, dgkv

    dq_ext, dkv_ext, dkr, dkrp, dcq, dckv, g["mla_q_norm"], g["mla_kv_norm"] = _rowwise(
        prep_bwd, name=n("mla_prep_bwd"), rows=s,
        ins=[("heads", dqm), ("heads", dkm), ("heads", dvm), ("row", proj, 256, C_CQ[0] // 256),
             ("row", proj, 128, C_CKV[0] // 128), ("full", w["mla_q_norm"]), ("full", w["mla_kv_norm"]),
             ("full", w["w_uq"]), ("full", w["w_ukv"]), ("row", tq, 512, 0), ("row", tkc, 128, 0), ("row", tks, 128, 0)],
        outs=[(512, BF16), (1024, BF16), (128, BF16), (128, BF16), (256, BF16), (128, BF16)],
        reds=[(1, 256), (1, 128)])
    g["w_uq"] = _mm(sv["cqn"], dq_ext, ta=True, name=n("dw_uq"))
    g["w_ukv"] = _mm(sv["ckvn"], dkv_ext, ta=True, name=n("dw_ukv"))

    dgb, dgc, duc, g["conv_w"] = _conv_bwd(dy_b, proj, w["conv_w"], name=n("conv_bwd"), s=s)
    dup, g["pool_wbd"], g["pool_scale"] = _pool_bwd(dy_c, proj, w["pool_wbd"], w["pool_scale"], name=n("pool_bwd"), s=s)

    dproj = jnp.concatenate([dq_sw, dcq, dgb, dgc, duc, dup, dk_sw.astype(BF16), dv_sw.astype(BF16), dckv, dkr, dkrp],
                            axis=1)
    g["w_in"] = _mm(sv["h"], dproj, ta=True, name=n("dw_in"))
    return dproj, g


def _mixer_bwd_norm(dx1, dproj, sv, w, l):
    n = lambda t: f"l{l}_{t}"
    dh = _mm(dproj, w["w_in"], tb=True, name=n("d_h"))
    dx0, dx0_b, dg = _rms_bwd(("row", sv["x"], D_MODEL, 0), w["attn_norm"], dh, dx1[0], name=n("attn_norm_bwd"),
                              rows=dh.shape[0], width=D_MODEL, out_dtypes=(F32, BF16))
    return (dx0, dx0_b), dict(attn_norm=dg)


def _loss_head(x, target, g, *, s):
    def fn(xv, tv, gv):
        r = _rstd(xv, D_MODEL)
        e = xv * r * gv - tv
        part = jnp.sum(jnp.sum(e * e, axis=1, keepdims=True), axis=0, keepdims=True) * (0.5 / D_MODEL)
        dy = e * (1.0 / D_MODEL)
        dyg = dy * gv
        dx = r * dyg - xv * (r * r * r) * (jnp.sum(dyg * xv, axis=-1, keepdims=True) * (1.0 / D_MODEL))
        return dx, dx, jnp.sum(dy * xv * r, axis=0, keepdims=True), jnp.broadcast_to(part, (1, LANES))

    return _rowwise(fn, name="loss_head", rows=s,
                    ins=[("row", x, D_MODEL, 0), ("row", target, D_MODEL, 0), ("full", g)],
                    outs=[(D_MODEL, F32), (D_MODEL, BF16)], reds=[(1, D_MODEL), (1, LANES)])


def _alibi_slopes(n):
    return np.asarray([2.0 ** (-8.0 * (i + 1) / n) for i in range(n)], dtype=np.float32)


MIXER_WEIGHTS = ("w_in", "w_uq", "w_ukv", "conv_w", "w_o")
FFN_WEIGHTS = ("w_gate_up", "w_down")


def _mixer_weights(full, rep, l):
    pw = rep["pool_w"][l]
    z = jnp.zeros((HEAD, HEAD), F32)
    wbd = jnp.stack([jnp.block([[pw[2 * j], z], [z, pw[2 * j + 1]]]) for j in range(2)])
    return dict(
        attn_norm=rep["attn_norm"][l][None], w_in=full["w_in"], mla_q_norm=rep["mla_q_norm"][l][None],
        w_uq=full["w_uq"], mla_kv_norm=rep["mla_kv_norm"][l][None], w_ukv=full["w_ukv"],
        conv_w=full["conv_w"], pool_wbd=wbd, pool_scale=rep["pool_scale"][l][None],
        mix_norm=rep["mix_norm"][l][None], w_o=full["w_o"],
        hp_swa=jnp.stack([jnp.asarray(_alibi_slopes(SWA_HEADS)), rep["swa_sinks"][l]], axis=1))


def _ffn_weights(full, rep, l):
    return dict(ffn_norm=rep["ffn_norm"][l][None], w_gate_up=full["w_gate_up"], w_down=full["w_down"])


def _small_grads(g):
    rows = ("attn_norm", "mla_q_norm", "mla_kv_norm", "pool_scale", "ffn_norm", "mix_norm")
    out = {nm: g[nm][0] for nm in rows if nm in g}
    if "swa_sinks" in g:
        out["swa_sinks"] = g["swa_sinks"]
    if "pool_wbd" in g:
        e = g["pool_wbd"]
        out["pool_w"] = jnp.stack([e[j // 2][HEAD * (j % 2):HEAD * (j % 2 + 1), HEAD * (j % 2):HEAD * (j % 2 + 1)]
                                   for j in range(4)])
    return out


def kernel(x, attn_norm, w_in, mla_q_norm, w_uq, mla_kv_norm, w_ukv, conv_w, pool_w, pool_scale, swa_sinks, mix_norm, w_o, ffn_norm, w_gate_up, w_down, final_norm, loss_target, m_attn_norm, m_w_in, m_mla_q_norm, m_w_uq, m_mla_kv_norm, m_w_ukv, m_conv_w, m_pool_w, m_pool_scale, m_swa_sinks, m_mix_norm, m_w_o, m_ffn_norm, m_w_gate_up, m_w_down, m_final_norm, v_attn_norm, v_w_in, v_mla_q_norm, v_w_uq, v_mla_kv_norm, v_w_ukv, v_conv_w, v_pool_w, v_pool_scale, v_swa_sinks, v_mix_norm, v_w_o, v_ffn_norm, v_w_gate_up, v_w_down, v_final_norm):
    given = dict(locals())
    sh_names = [nm for nm, _, _ in SHARDED]
    sh_axis = {nm: ax - 1 for nm, _, ax in SHARDED}
    rep_names = [nm for nm, _ in REPLICATED]
    rep_shapes = [shp for _, shp in REPLICATED]
    rep = {nm: given[nm] for nm in rep_names if nm != "loss"}
    me = 4 * lax.axis_index("x") + 2 * lax.axis_index("y") + lax.axis_index("c")
    my_chip = 2 * lax.axis_index("x") + lax.axis_index("y")
    core = lax.axis_index("c").astype(jnp.int32).reshape(1)

    def behind(token, a):
        return a + token[0, 0].astype(a.dtype)

    def wire(nm, l):
        if nm == "conv_w":
            return lax.bitcast_convert_type(given[nm][l], BF16).reshape(3, -1)
        return given[nm][l].astype(BF16)

    def whole(nm, g, tag):
        if nm in LAYOUTS:
            return _extend(nm, g, name=f"extend_{nm}_{tag}")
        if nm == "conv_w":
            g = lax.bitcast_convert_type(g.reshape(N_DEV, 3, -1, 2), F32)
        return _shards_to_full(g, sh_axis[nm])

    def near_start(names, l, after, tag):
        srcs = [wire(nm, l) for nm in names]
        return _start_copies(_plan_gather_near, 4, srcs, [(N_DEV,) + a.shape for a in srcs], after, name=f"start_gather_{tag}")

    def pass_on(handle, after, tag):
        srcs, lands = _wait_copies(_plan_gather_near, handle, after, name=f"wait_gather_{tag}")
        handle, token = _start_copies(_plan_gather_pass, 3, [], lands, [], name=f"start_pass_{tag}")
        return (srcs, handle), token

    def gathered(names, state, after, tag):
        srcs, handle = state
        _, lands = _wait_copies(_plan_gather_pass, handle, after, name=f"wait_pass_{tag}")
        return {nm: whole(nm, lax.dynamic_update_index_in_dim(land, src, me, 0), tag)
                for nm, src, land in zip(names, srcs, lands)}

    layer1 = MIXER_WEIGHTS + FFN_WEIGHTS
    got = _all_gather([wire(nm, 0) for nm in MIXER_WEIGHTS], name="gather_mixer0")
    full_m0 = {nm: whole(nm, g, "mixer0") for nm, g in zip(MIXER_WEIGHTS, got)}
    h_f0, tok = near_start(FFN_WEIGHTS, 0, [], "ffn0")
    h_l1, tok = near_start(layer1, 1, [tok], "layer1")

    xs, target = x[0], loss_target[0]
    s = xs.shape[0]
    tabs = _rope_tables(s)
    wm, wf, svm, svf = [None] * DEPTH, [None] * DEPTH, [None] * DEPTH, [None] * DEPTH
    wm[0] = _mixer_weights(full_m0, rep, 0)
    wm[0]["attn_norm"] = behind(tok, wm[0]["attn_norm"])
    passing = {}

    def pass_ffn0(y_a):
        passing["ffn0"], token = pass_on(h_f0, [y_a], "ffn0")
        return token

    x1, svm[0] = _mixer_fwd(xs, wm[0], tabs, 0, after_attention=pass_ffn0)
    wf[0] = _ffn_weights(gathered(FFN_WEIGHTS, passing["ffn0"], [x1], "ffn0"), rep, 0)
    passing["layer1"], tok = pass_on(h_l1, [x1], "layer1")
    wf[0]["ffn_norm"] = behind(tok, wf[0]["ffn_norm"])
    x2, svf[0] = _ffn_fwd(x1, wf[0], 0)
    full_1 = gathered(layer1, passing["layer1"], [x2], "layer1")
    wm[1], wf[1] = _mixer_weights(full_1, rep, 1), _ffn_weights(full_1, rep, 1)
    x1, svm[1] = _mixer_fwd(x2, wm[1], tabs, 1)
    x2, svf[1] = _ffn_fwd(x1, wf[1], 1)
    dx_f, dx_b, d_final, loss = _loss_head(x2, target, rep["final_norm"][None], s=s)
    dx = (dx_f, dx_b)

    parts = {}

    def reduce_start(grads, l, after, tag):
        names = [nm for nm in sh_names if nm in grads]
        mine = [_fold_to_shards(nm, grads[nm], name=f"fold_{nm}_{l}") if nm in LAYOUTS
                else _full_to_shards(grads[nm], sh_axis[nm]) for nm in names]
        handle, token = _start_copies(_plan_sibling, 1, mine, [m.shape[1:] for m in mine], after, name=f"start_sibling_{tag}")
        return (names, l, handle), token

    def reduce_mid(state, after, tag):
        names, l, handle = state
        mine, theirs = _wait_copies(_plan_sibling, handle, after, name=f"wait_sibling_{tag}")
        sums = [_add_own(g, o, core, name=f"chip_sum_{nm}_{l}") for nm, g, o in zip(names, mine, theirs)]
        handle, token = _start_copies(_plan_chips, 3, sums, [a.shape for a in sums], [], name=f"start_chips_{tag}")
        return (names, l, handle), token

    def reduce_end(state, after, tag):
        names, l, handle = state
        sums, lands = _wait_copies(_plan_chips, handle, after, name=f"wait_chips_{tag}")
        for nm, own, land in zip(names, sums, lands):
            parts[nm, l] = lax.dynamic_update_index_in_dim(land, lax.dynamic_index_in_dim(own, my_chip, 0, keepdims=False),
                                                           my_chip, 0)

    small = [None] * DEPTH
    in_flight = []
    pending = None
    for l in reversed(range(DEPTH)):
        dgu, g_down = _ffn_bwd_down(dx, svf[l], wf[l], l)
        if pending is not None:
            state, token = reduce_mid(pending, [dgu], f"mixer{l + 1}")
            in_flight.append((state, f"mixer{l + 1}"))
            wf[l]["ffn_norm"] = behind(token, wf[l]["ffn_norm"])
        dx1, g_up = _ffn_bwd_up(dx, dgu, svf[l], wf[l], l)
        g_ffn = {**g_down, **g_up}
        state, token = reduce_start(g_ffn, l, [], f"ffn{l}")
        wm[l]["mix_norm"] = behind(token, wm[l]["mix_norm"])
        dys, g_out = _mixer_bwd_out(dx1, svm[l], wm[l], l)
        state, token = reduce_mid(state, [dys[0]], f"ffn{l}")
        in_flight.append((state, f"ffn{l}"))
        wm[l]["hp_swa"] = behind(token, wm[l]["hp_swa"])
        dproj, g_in = _mixer_bwd_in(dx1, dys, svm[l], wm[l], tabs, l)
        g_mixer = {**g_out, **g_in}
        pending, token = reduce_start(g_mixer, l, [], f"mixer{l}")
        wm[l]["attn_norm"] = behind(token, wm[l]["attn_norm"])
        dx, g_norm = _mixer_bwd_norm(dx1, dproj, svm[l], wm[l], l)
        small[l] = _small_grads({**g_ffn, **g_mixer, **g_norm})
    grads = {nm: jnp.stack([small[l][nm] for l in range(DEPTH)]) for nm in rep_names if nm in small[0]}
    grads["final_norm"] = d_final[0]
    grads["loss"] = loss[0, :1]
    zero = jnp.zeros((1,), F32)
    small = _all_gather([behind(token, _pack([grads[nm] for nm in rep_names]))], name="gather_small_grads")

    last, token = reduce_mid(pending, [dx[0], small[0]], "mixer0")
    for state, tag in in_flight:
        reduce_end(state, [], tag)
    grad_x = dx[0]

    def adamw(nm, after):
        return _adamw([parts[nm, l] for l in range(DEPTH)], given[nm], given["m_" + nm], given["v_" + nm], after,
                      name=f"adamw_{nm}")

    sh_out = {nm: adamw(nm, token) for nm in FFN_WEIGHTS}
    packs = [_pack([given.get(pre + nm, zero) for nm in rep_names])[None] for pre in ("", "m_", "v_")]
    rep_res = _adamw(small, *packs, token, name="adamw_replicated")
    rep_out = [dict(zip(rep_names, _unpack(o[0], rep_shapes))) for o in rep_res]

    reduce_end(last, [rep_res[0], sh_out["w_down"][0]], "mixer0")
    sh_out.update({nm: adamw(nm, token) for nm in MIXER_WEIGHTS})

    out = [rep_out[0]["loss"][0], grad_x[None]]
    for i in range(4):
        out += [sh_out[nm][i] if nm in sh_axis else rep_out[i][nm] for nm in WEIGHT_ORDER]
    return tuple(out)
```

```python
import functools
import math

import numpy as np
import jax
import jax.numpy as jnp
from jax import lax
from jax.experimental import pallas as pl
from jax.experimental.pallas import tpu as pltpu

F32 = jnp.float32
BF16 = jnp.bfloat16

D_MODEL = 1024
DEPTH = 2
D_GROUP = 256
MLA_HEADS = 4
MLA_NOPE = 64
MLA_ROPE = 32
ROPE_THETA = 10000.0
POOL_WINDOWS = (2, 4, 8, 16)
SWA_HEADS = 4
SWA_KV_HEADS = 2
SWA_WINDOW = 128
D_FF = 2816
GU_TILE = 256
RMS_EPS = 1e-6
LANES = 128
HEAD = 64
VMEM_LIMIT = 48 * 1024 * 1024
NEG = -1e30

ADAM_LR = 0.001
ADAM_B1 = 0.9
ADAM_B2 = 0.999
ADAM_EPS = 1e-08
ADAM_WD = 0.01
ADAM_STEP = 10

N_DEV = 8
PACK_COLS = 1024

C_QSW, C_CQ, C_GB, C_GC, C_UCONV, C_UPOOL = (0, 512), (512, 256), (768, 256), (1024, 256), (1280, 256), (1536, 256)
C_KSW, C_VSW, C_CKV, C_KR, C_KRP = (1792, 256), (2048, 256), (2304, 128), (2432, 128), (2560, 128)

SHARDED = (("w_in", (DEPTH, 1024, 244), 2), ("w_uq", (DEPTH, 256, 48), 2), ("w_ukv", (DEPTH, 128, 64), 2),
           ("conv_w", (DEPTH, 3, 32), 2), ("w_o", (DEPTH, 128, 1024), 1), ("w_gate_up", (DEPTH, 1024, 704), 2),
           ("w_down", (DEPTH, 352, 1024), 1))
REPLICATED = (("attn_norm", (DEPTH, 1024)), ("mla_q_norm", (DEPTH, 256)), ("mla_kv_norm", (DEPTH, 128)),
              ("pool_w", (DEPTH, 4, 64, 64)), ("pool_scale", (DEPTH, 256)), ("swa_sinks", (DEPTH, 4)),
              ("mix_norm", (DEPTH, 1024)), ("ffn_norm", (DEPTH, 1024)), ("final_norm", (1024,)), ("loss", (1,)))
WEIGHT_ORDER = ("attn_norm", "w_in", "mla_q_norm", "w_uq", "mla_kv_norm", "w_ukv", "conv_w", "pool_w", "pool_scale",
                "swa_sinks", "mix_norm", "w_o", "ffn_norm", "w_gate_up", "w_down", "final_norm")


def _params(sem):
    return pltpu.CompilerParams(dimension_semantics=sem, vmem_limit_bytes=VMEM_LIMIT)


def _pick(dim, target):
    if dim <= target:
        return dim
    best = None
    for t in range(LANES, target + 1, LANES):
        if dim % t == 0:
            best = t
    assert best is not None, (dim, target)
    return best


def _mm(a, b, *, name, ta=False, tb=False, res=None, out_dtype=F32, tm=1024, tn=1024, tk=1024, epilogue=None):
    m, k = (a.shape[1], a.shape[0]) if ta else a.shape
    n = b.shape[0] if tb else b.shape[1]
    assert (b.shape[1] if tb else b.shape[0]) == k
    tm, tn, tk = _pick(m, tm), _pick(n, tn), _pick(k, tk)
    nj, nk = n // tn, k // tk
    dims = (((0 if ta else 1,), (1 if tb else 0,)), ((), ()))
    fn, extra, outs = epilogue if epilogue is not None else (None, [], [(n, out_dtype)])
    if res is not None:
        assert epilogue is None
        fn, extra = (lambda acc, r: (acc + r,)), [res]
    n_in, n_out = 2 + len(extra), len(outs)

    def body(*refs):
        a_ref, b_ref, acc_ref = refs[0], refs[1], refs[-1]
        kk = pl.program_id(2)

        def product():
            return lax.dot_general(a_ref[...].astype(BF16), b_ref[...].astype(BF16), dims, preferred_element_type=F32)

        def finish(acc):
            tiles = (acc,) if fn is None else fn(acc, *[r[...] for r in refs[2:n_in]])
            for o_ref, tile in zip(refs[n_in:n_in + n_out], tiles):
                o_ref[...] = tile.astype(o_ref.dtype)

        if nk == 1:
            finish(product())
            return

        @pl.when(kk == 0)
        def _():
            acc_ref[...] = jnp.zeros_like(acc_ref)

        acc_ref[...] += product()

        @pl.when(kk == nk - 1)
        def _():
            finish(acc_ref[...])

    def col_tiles(width):
        assert width % (nj * LANES) == 0, (width, nj)
        return pl.BlockSpec((tm, width // nj), lambda i, j, kk: (i, j))

    a_spec = pl.BlockSpec((tk, tm), lambda i, j, kk: (kk, i)) if ta else pl.BlockSpec((tm, tk), lambda i, j, kk: (i, kk))
    b_spec = pl.BlockSpec((tn, tk), lambda i, j, kk: (j, kk)) if tb else pl.BlockSpec((tk, tn), lambda i, j, kk: (kk, j))
    res_ = pl.pallas_call(
        body, name=name, grid=(m // tm, nj, nk), in_specs=[a_spec, b_spec] + [col_tiles(e.shape[1]) for e in extra],
        out_specs=[col_tiles(w) for w, _ in outs],
        out_shape=[jax.ShapeDtypeStruct((m, w), dt) for w, dt in outs],
        scratch_shapes=[pltpu.VMEM((tm, tn), F32)] if nk > 1 else [],
        compiler_params=_params(("parallel", "parallel", "arbitrary")),
    )(a, b, *extra)
    return res_[0] if epilogue is None else res_


def _rowwise(fn, *, name, rows, ins, outs, reds=(), tm=512):
    tm = min(tm, rows)
    assert rows % tm == 0
    n_in, n_out = len(ins), len(outs)

    def body(*refs):
        vals = [jnp.concatenate([r[h] for h in range(r.shape[0])], axis=1) if spec[0] == "heads" else r[...]
                for spec, r in zip(ins, refs[:n_in])]
        res = fn(*vals)
        for out, r, v in zip(outs, refs[n_in:n_in + n_out], res[:n_out]):
            if len(out) >= 3:
                for h in range(out[2]):
                    piece = v[:, h * LANES:(h + 1) * LANES]
                    r[h] = (piece.T if len(out) == 4 else piece).astype(r.dtype)
            else:
                r[...] = v.astype(r.dtype)
        if reds:
            @pl.when(pl.program_id(0) == 0)
            def _():
                for r in refs[n_in + n_out:]:
                    r[...] = jnp.zeros_like(r)

            for r, v in zip(refs[n_in + n_out:], res[n_out:]):
                r[...] += v

    in_specs, args = [], []
    for spec in ins:
        if spec[0] == "row":
            _, arr, width, blk = spec
            in_specs.append(pl.BlockSpec((tm, width), functools.partial(lambda i, blk: (i, blk), blk=blk)))
        elif spec[0] == "heads":
            arr = spec[1]
            in_specs.append(pl.BlockSpec((arr.shape[0], tm, LANES), lambda i: (0, i, 0)))
        else:
            arr = spec[1]
            in_specs.append(pl.BlockSpec(arr.shape, functools.partial(lambda i, nd: (0,) * nd, nd=arr.ndim)))
        args.append(arr)
    def out_spec(o):
        if len(o) == 4:
            return pl.BlockSpec((o[2], LANES, tm), lambda i: (0, 0, i)), (o[2], LANES, rows)
        if len(o) == 3:
            return pl.BlockSpec((o[2], tm, LANES), lambda i: (0, i, 0)), (o[2], rows, LANES)
        return pl.BlockSpec((tm, o[0]), lambda i: (i, 0)), (rows, o[0])

    out_specs = [out_spec(o)[0] for o in outs]
    out_shape = [jax.ShapeDtypeStruct(out_spec(o)[1], o[1]) for o in outs]
    out_specs += [pl.BlockSpec((r, w), lambda i: (0, 0)) for r, w in reds]
    out_shape += [jax.ShapeDtypeStruct((r, w), F32) for r, w in reds]
    return pl.pallas_call(body, name=name, grid=(rows // tm,), in_specs=in_specs, out_specs=out_specs,
                          out_shape=out_shape, compiler_params=_params(("arbitrary",)))(*args)


def _rstd(x, n):
    return lax.rsqrt(jnp.sum(x * x, axis=-1, keepdims=True) * (1.0 / n) + RMS_EPS)


def _rms_fwd(x_spec, g, *, name, rows, width):
    def fn(x, gv):
        return (x * _rstd(x, width) * gv,)
    return _rowwise(fn, name=name, rows=rows, ins=[x_spec, ("full", g)], outs=[(width, BF16)])[0]


def _rms_bwd(x_spec, g, dy, res, *, name, rows, width, out_dtypes):
    def fn(x, gv, dyv, *rest):
        r = _rstd(x, width)
        dyg = dyv * gv
        dx = r * dyg - x * (r * r * r) * (jnp.sum(dyg * x, axis=-1, keepdims=True) * (1.0 / width))
        if rest:
            dx = dx + rest[0]
        return (dx,) * len(out_dtypes) + (jnp.sum(dyv * x * r, axis=0, keepdims=True),)

    ins = [x_spec, ("full", g), ("row", dy, width, 0)]
    if res is not None:
        ins.append(("row", res, width, 0))
    return _rowwise(fn, name=name, rows=rows, ins=ins, outs=[(width, dt) for dt in out_dtypes], reds=[(1, width)])


NT_DIMS = (((1,), (1,)), ((), ()))
TN_DIMS = (((0,), (0,)), ((), ()))
BNT_DIMS = (((2,), (2,)), ((0,), (0,)))
BNN_DIMS = (((2,), (1,)), ((0,), (0,)))


def _mla_tile(s):
    return min(512, s)


def _block_tables(pairs):
    return jnp.asarray([a for a, _ in pairs], jnp.int32), jnp.asarray([b for _, b in pairs], jnp.int32)


def _causal(shape, query_axis):
    return lax.broadcasted_iota(jnp.int32, shape, query_axis) >= lax.broadcasted_iota(jnp.int32, shape, 1 - query_axis)


def _mla_fwd(qa, ka, vta, *, name, s, nh):
    t = _mla_tile(s)
    nq = s // t

    pairs = [(i, j) for i in range(nq) for j in range(i + 1)]

    def body(qb_ref, kb_ref, q_ref, k_ref, vt_ref, o_ref, lse_ref, m_s, acc_s):
        i, j = qb_ref[pl.program_id(0)], kb_ref[pl.program_id(0)]

        @pl.when(j == 0)
        def _():
            m_s[...] = jnp.full_like(m_s, NEG)
            acc_s[...] = jnp.zeros_like(acc_s)

        def step(diag):
            sc = lax.dot_general(k_ref[...], q_ref[...], BNT_DIMS, preferred_element_type=F32)
            if diag:
                sc = jnp.where(_causal(sc.shape[1:], 1)[None], sc, NEG)
            m_prev = m_s[...]
            m_new = jnp.maximum(m_prev, jnp.max(sc, axis=1, keepdims=True))
            p = jnp.exp(sc - m_new).astype(BF16)
            acc_s[...] = (jnp.exp(m_prev - m_new) * acc_s[...]
                          + lax.dot_general(vt_ref[...], p, BNN_DIMS, preferred_element_type=F32))
            m_s[...] = m_new

        pl.when(j < i)(functools.partial(step, False))
        pl.when(j == i)(functools.partial(step, True))

        @pl.when(j == i)
        def _():
            row = lax.broadcasted_iota(jnp.int32, (LANES, t), 0)
            for h in range(nh):
                acc = acc_s[h]
                l = acc[HEAD:HEAD + 1, :]
                o_ref[h] = jnp.where(row < HEAD, acc / l, 0.0).T
                lse_ref[h] = m_s[h] + jnp.log(l)

    q_spec = pl.BlockSpec((nh, t, LANES), lambda p, qb, kb: (0, qb[p], 0))
    k_spec = pl.BlockSpec((nh, t, LANES), lambda p, qb, kb: (0, kb[p], 0))
    vt_spec = pl.BlockSpec((nh, LANES, t), lambda p, qb, kb: (0, 0, kb[p]))
    return pl.pallas_call(
        body, name=name,
        grid_spec=pltpu.PrefetchScalarGridSpec(
            num_scalar_prefetch=2, grid=(len(pairs),), in_specs=[q_spec, k_spec, vt_spec],
            out_specs=[q_spec, pl.BlockSpec((nh, 1, t), lambda p, qb, kb: (0, 0, qb[p]))],
            scratch_shapes=[pltpu.VMEM((nh, 1, t), F32), pltpu.VMEM((nh, LANES, t), F32)]),
        out_shape=[jax.ShapeDtypeStruct((nh, s, LANES), F32), jax.ShapeDtypeStruct((nh, 1, s), F32)],
        compiler_params=_params(("arbitrary",)),
    )(*_block_tables(pairs), qa, ka, vta)


def _mla_bwd(qa, ka, kta, va, o, do, lse_row, after, *, name, s, nh):
    t = _mla_tile(s)
    nq = s // t

    pairs = [(kb, j) for kb in range(nq) for j in range(kb, nq)]

    def body(kb_ref, qb_ref, q_ref, k_ref, kt_ref, v_ref, o_ref, do_ref, lse_ref, after_ref, dq_hbm, dk_ref, dv_ref,
             dqt_s, dk_s, dv_s, d_s, stage, sem):
        kb, j = kb_ref[pl.program_id(0)], qb_ref[pl.program_id(0)]
        cols = pl.ds(pl.multiple_of(j * t, t), t)

        @pl.when(j == kb)
        def _():
            dk_s[...] = jnp.zeros_like(dk_s)
            dv_s[...] = jnp.zeros_like(dv_s)

        @pl.when(kb == 0)
        def _():
            dqt_s[:, :, cols] = jnp.zeros((nh, LANES, t), F32)
            for h in range(nh):
                d_col = jnp.sum(do_ref[h] * o_ref[h], axis=1, keepdims=True)
                d_s[h, :, cols] = jnp.broadcast_to(d_col, (t, LANES)).T[0:1, :]

        def step(diag):
            q = q_ref[...]
            do_b = do_ref[...].astype(BF16)
            sc = lax.dot_general(k_ref[...], q, BNT_DIMS, preferred_element_type=F32)
            if diag:
                sc = jnp.where(_causal(sc.shape[1:], 1)[None], sc, NEG)
            p = jnp.exp(sc - lse_ref[...])
            dv_s[...] += lax.dot_general(p.astype(BF16), do_b, BNN_DIMS, preferred_element_type=F32)
            dp = lax.dot_general(v_ref[...], do_b, BNT_DIMS, preferred_element_type=F32)
            ds = (p * (dp - d_s[:, :, cols])).astype(BF16)
            dk_s[...] += lax.dot_general(ds, q, BNN_DIMS, preferred_element_type=F32)
            dqt_s[:, :, cols] += lax.dot_general(kt_ref[...], ds, BNN_DIMS, preferred_element_type=F32)

        pl.when(j > kb)(functools.partial(step, False))
        pl.when(j == kb)(functools.partial(step, True))

        @pl.when(j == kb)
        def _():
            for h in range(nh):
                stage[h] = dqt_s[h, :, cols].T
            out = pltpu.make_async_copy(stage, dq_hbm.at[:, cols, :], sem)
            out.start()
            out.wait()

        @pl.when(j == nq - 1)
        def _():
            dk_ref[...] = dk_s[...]
            dv_ref[...] = dv_s[...].astype(dv_ref.dtype)

    q_spec = pl.BlockSpec((nh, t, LANES), lambda p, kb, qb: (0, qb[p], 0))
    kv_spec = pl.BlockSpec((nh, t, LANES), lambda p, kb, qb: (0, kb[p], 0))
    kt_spec = pl.BlockSpec((nh, LANES, t), lambda p, kb, qb: (0, 0, kb[p]))
    row_spec = pl.BlockSpec((nh, 1, t), lambda p, kb, qb: (0, 0, qb[p]))
    whole = jax.ShapeDtypeStruct((nh, s, LANES), F32)
    return pl.pallas_call(
        body, name=name,
        grid_spec=pltpu.PrefetchScalarGridSpec(
            num_scalar_prefetch=2, grid=(len(pairs),),
            in_specs=[q_spec, kv_spec, kt_spec, kv_spec, q_spec, q_spec, row_spec, pl.BlockSpec(memory_space=pl.ANY)],
            out_specs=[pl.BlockSpec(memory_space=pl.ANY), kv_spec, kv_spec],
            scratch_shapes=[pltpu.VMEM((nh, LANES, s), F32), pltpu.VMEM((nh, t, LANES), F32),
                            pltpu.VMEM((nh, t, LANES), F32), pltpu.VMEM((nh, 1, s), F32), pltpu.VMEM((nh, t, LANES), F32),
                            pltpu.SemaphoreType.DMA]),
        out_shape=[whole, whole, jax.ShapeDtypeStruct((nh, s, LANES), BF16)],
        compiler_params=_params(("arbitrary",)),
    )(*_block_tables(pairs), qa, ka, kta, va, o, do, lse_row, after)


SWA_PIECE = 128
SWA_KEYS = 2 * SWA_PIECE


def _swa_block(s):
    return min(512, s)


def _swa_piece(hp_ref, h, q, k_ref, v_ref, qpos0, scale):
    kstart = pl.multiple_of(jnp.maximum(qpos0 - SWA_PIECE, 0), SWA_PIECE)
    k = k_ref[pl.ds(kstart, SWA_KEYS), :].astype(BF16)
    v = v_ref[pl.ds(kstart, SWA_KEYS), :].astype(BF16)
    sc = lax.dot_general(q, k, NT_DIMS, preferred_element_type=F32)
    dist = (qpos0 + lax.broadcasted_iota(jnp.int32, sc.shape, 0)) - (kstart + lax.broadcasted_iota(jnp.int32, sc.shape, 1))
    sc = sc * scale - hp_ref[h, 0] * dist.astype(F32)
    sc = jnp.where((dist >= 0) & (dist < SWA_WINDOW), sc, NEG)
    return kstart, k, v, sc


def _swa_fwd(proj, hp, *, name, s, scale):
    tb = _swa_block(s)
    group = SWA_HEADS // SWA_KV_HEADS
    q_off, k_off, v_off = C_QSW[0] // LANES, C_KSW[0] // LANES, C_VSW[0] // LANES

    def body(hp_ref, q_ref, k_ref, v_ref, o_ref, lse_ref):
        h, i = pl.program_id(0), pl.program_id(1)
        sink = hp_ref[h, 1]
        for r in range(0, tb, SWA_PIECE):
            rows = pl.ds(r, SWA_PIECE)
            _, _, v, sc = _swa_piece(hp_ref, h, q_ref[rows, :].astype(BF16), k_ref, v_ref, i * tb + r, scale)
            m = jnp.maximum(jnp.max(sc, axis=1, keepdims=True), sink)
            p = jnp.exp(sc - m)
            l = jnp.sum(p, axis=1, keepdims=True) + jnp.exp(sink - m)
            o_ref[rows, :] = jnp.dot(p.astype(BF16), v, preferred_element_type=F32) / l
            lse_ref[rows, :] = m + jnp.log(l)

    whole = lambda off: pl.BlockSpec((s, LANES), lambda h, i: (0, off + h // group))
    return pl.pallas_call(
        body, name=name, grid=(SWA_HEADS, s // tb),
        in_specs=[pl.BlockSpec(memory_space=pltpu.SMEM), pl.BlockSpec((tb, LANES), lambda h, i: (i, q_off + h)),
                  whole(k_off), whole(v_off)],
        out_specs=[pl.BlockSpec((tb, LANES), lambda h, i: (i, h)), pl.BlockSpec((None, tb, 1), lambda h, i: (h, i, 0))],
        out_shape=[jax.ShapeDtypeStruct((s, SWA_HEADS * LANES), F32), jax.ShapeDtypeStruct((SWA_HEADS, s, 1), F32)],
        compiler_params=_params(("parallel", "parallel")),
    )(hp, proj, proj, proj)


def _swa_bwd(proj, o, do, lse, hp, *, name, s, scale):
    tb = _swa_block(s)
    nqb = s // tb
    group = SWA_HEADS // SWA_KV_HEADS
    q_off, k_off, v_off = C_QSW[0] // LANES, C_KSW[0] // LANES, C_VSW[0] // LANES

    def body(hp_ref, q_ref, k_ref, v_ref, o_ref, do_ref, lse_ref, dq_ref, dk_ref, dv_ref, dsink_ref):
        kh, g, i = pl.program_id(0), pl.program_id(1), pl.program_id(2)
        h = kh * group + g
        sink = hp_ref[h, 1]

        @pl.when((g == 0) & (i == 0))
        def _():
            dk_ref[...] = jnp.zeros_like(dk_ref)
            dv_ref[...] = jnp.zeros_like(dv_ref)

        @pl.when(i == 0)
        def _():
            dsink_ref[...] = jnp.zeros_like(dsink_ref)

        for r in range(0, tb, SWA_PIECE):
            rows = pl.ds(r, SWA_PIECE)
            q = q_ref[rows, :].astype(BF16)
            dov = do_ref[rows, :]
            do_b = dov.astype(BF16)
            lse_r = lse_ref[rows, :]
            d_r = jnp.sum(dov * o_ref[rows, :], axis=1, keepdims=True)
            kstart, k, v, sc = _swa_piece(hp_ref, h, q, k_ref, v_ref, i * tb + r, scale)
            p = jnp.exp(sc - lse_r)
            dp = lax.dot_general(do_b, v, NT_DIMS, preferred_element_type=F32)
            ds = (p * (dp - d_r)).astype(BF16)
            dq_ref[rows, :] = (jnp.dot(ds, k, preferred_element_type=F32) * scale).astype(dq_ref.dtype)
            win = pl.ds(kstart, SWA_KEYS)
            dk_ref[win, :] += lax.dot_general(ds, q, TN_DIMS, preferred_element_type=F32) * scale
            dv_ref[win, :] += lax.dot_general(p.astype(BF16), do_b, TN_DIMS, preferred_element_type=F32)
            part = jnp.sum(-jnp.exp(sink - lse_r) * d_r, axis=0, keepdims=True)
            dsink_ref[...] += jnp.broadcast_to(part, (1, LANES))

    whole = lambda off: pl.BlockSpec((s, LANES), lambda kh, g, i: (0, off + kh))
    q_map = lambda kh, g, i: (i, kh * group + g)
    return pl.pallas_call(
        body, name=name, grid=(SWA_KV_HEADS, group, nqb),
        in_specs=[pl.BlockSpec(memory_space=pltpu.SMEM),
                  pl.BlockSpec((tb, LANES), lambda kh, g, i: (i, q_off + kh * group + g)), whole(k_off), whole(v_off),
                  pl.BlockSpec((tb, LANES), q_map), pl.BlockSpec((tb, LANES), q_map),
                  pl.BlockSpec((None, tb, 1), lambda kh, g, i: (kh * group + g, i, 0))],
        out_specs=[pl.BlockSpec((tb, LANES), q_map), whole(0), whole(0),
                   pl.BlockSpec((None, 1, LANES), lambda kh, g, i: (kh * group + g, 0, 0))],
        out_shape=[jax.ShapeDtypeStruct((s, SWA_HEADS * LANES), BF16),
                   jax.ShapeDtypeStruct((s, SWA_KV_HEADS * LANES), F32), jax.ShapeDtypeStruct((s, SWA_KV_HEADS * LANES), F32),
                   jax.ShapeDtypeStruct((SWA_HEADS, 1, LANES), F32)],
        compiler_params=_params(("parallel", "arbitrary", "arbitrary")),
    )(hp, proj, proj, proj, o, do, lse)


def _shift_down(z, k):
    rows = lax.broadcasted_iota(jnp.int32, z.shape, 0)
    return jnp.where(rows >= k, pltpu.roll(z, k, 0), 0.0)


def _shift_up(z, k):
    n = z.shape[0]
    rows = lax.broadcasted_iota(jnp.int32, z.shape, 0)
    return jnp.where(rows < n - k, pltpu.roll(z, n - k, 0), 0.0)


def _rows3(a, b, c):
    r = lax.broadcasted_iota(jnp.int32, (3, a.shape[1]), 0)
    return jnp.where(r == 0, a, jnp.where(r == 1, b, c))


def _col_spec(s, off):
    return pl.BlockSpec((s, LANES), functools.partial(lambda j, off: (0, off + j), off=off))


def _conv_fwd(proj, conv_w, *, name, s):
    def body(gb_ref, gc_ref, u_ref, w_ref, y_ref):
        w0, w1, w2 = w_ref[0:1, :], w_ref[1:2, :], w_ref[2:3, :]
        z = gc_ref[...] * u_ref[...]
        c = w2 * z + w1 * _shift_down(z, 1) + w0 * _shift_down(z, 2)
        y_ref[...] = gb_ref[...] * c

    return pl.pallas_call(
        body, name=name, grid=(2,),
        in_specs=[_col_spec(s, C_GB[0] // LANES), _col_spec(s, C_GC[0] // LANES), _col_spec(s, C_UCONV[0] // LANES),
                  pl.BlockSpec((3, LANES), lambda j: (0, j))],
        out_specs=_col_spec(s, 0), out_shape=jax.ShapeDtypeStruct((s, D_GROUP), F32),
        compiler_params=_params(("parallel",)),
    )(proj, proj, proj, conv_w)


def _conv_bwd(dy, proj, conv_w, *, name, s):
    def body(dy_ref, gb_ref, gc_ref, u_ref, w_ref, dgb_ref, dgc_ref, du_ref, dw_ref):
        w0, w1, w2 = w_ref[0:1, :], w_ref[1:2, :], w_ref[2:3, :]
        gc, u, dyv = gc_ref[...], u_ref[...], dy_ref[...]
        z = gc * u
        z1, z2 = _shift_down(z, 1), _shift_down(z, 2)
        c = w2 * z + w1 * z1 + w0 * z2
        dgb_ref[...] = (dyv * c).astype(dgb_ref.dtype)
        dc = dyv * gb_ref[...]
        dz = w2 * dc + w1 * _shift_up(dc, 1) + w0 * _shift_up(dc, 2)
        dgc_ref[...] = (dz * u).astype(dgc_ref.dtype)
        du_ref[...] = (dz * gc).astype(du_ref.dtype)
        dw_ref[...] = _rows3(jnp.sum(dc * z2, axis=0, keepdims=True), jnp.sum(dc * z1, axis=0, keepdims=True),
                             jnp.sum(dc * z, axis=0, keepdims=True))

    act = jax.ShapeDtypeStruct((s, D_GROUP), BF16)
    return pl.pallas_call(
        body, name=name, grid=(2,),
        in_specs=[_col_spec(s, 0), _col_spec(s, C_GB[0] // LANES), _col_spec(s, C_GC[0] // LANES),
                  _col_spec(s, C_UCONV[0] // LANES), pl.BlockSpec((3, LANES), lambda j: (0, j))],
        out_specs=[_col_spec(s, 0), _col_spec(s, 0), _col_spec(s, 0), pl.BlockSpec((3, LANES), lambda j: (0, j))],
        out_shape=[act, act, act, jax.ShapeDtypeStruct((3, D_GROUP), F32)],
        compiler_params=_params(("parallel",)),
    )(dy, proj, proj, proj, conv_w)


def _pool_select(j, lane, a2, a4, a8, a16):
    lo = lane < HEAD
    return jnp.where(j == 0, jnp.where(lo, a2, a4), jnp.where(lo, a8, a16))


def _pooled(u, j):
    s2 = u + _shift_down(u, 1)
    s4 = s2 + _shift_down(s2, 2)
    s8 = s4 + _shift_down(s4, 4)
    s16 = s8 + _shift_down(s8, 8)
    lane = lax.broadcasted_iota(jnp.int32, u.shape, 1)
    rows = lax.broadcasted_iota(jnp.int32, u.shape, 0)
    win = _pool_select(j, lane, *POOL_WINDOWS)
    count = jnp.minimum(rows + 1, win).astype(F32)
    return _pool_select(j, lane, s2, s4, s8, s16) / count - u, count


def _pool_fwd(proj, wbd, scale, *, name, s):
    def body(u_ref, w_ref, sc_ref, y_ref):
        pooled, _ = _pooled(u_ref[...], pl.program_id(0))
        y_ref[...] = jnp.dot(pooled.astype(BF16), w_ref[...].astype(BF16), preferred_element_type=F32) * sc_ref[...]

    return pl.pallas_call(
        body, name=name, grid=(2,),
        in_specs=[_col_spec(s, C_UPOOL[0] // LANES), pl.BlockSpec((None, LANES, LANES), lambda j: (j, 0, 0)),
                  pl.BlockSpec((1, LANES), lambda j: (0, j))],
        out_specs=_col_spec(s, 0), out_shape=jax.ShapeDtypeStruct((s, D_GROUP), F32),
        compiler_params=_params(("parallel",)),
    )(proj, wbd, scale)


def _pool_bwd(dy, proj, wbd, scale, *, name, s):
    def body(dy_ref, u_ref, w_ref, sc_ref, du_ref, dw_ref, dsc_ref):
        j = pl.program_id(0)
        pooled, count = _pooled(u_ref[...], j)
        pooled_b = pooled.astype(BF16)
        w_b = w_ref[...].astype(BF16)
        dyv = dy_ref[...]
        mixed = jnp.dot(pooled_b, w_b, preferred_element_type=F32)
        dsc_ref[...] = jnp.sum(dyv * mixed, axis=0, keepdims=True)
        dms = (dyv * sc_ref[...]).astype(BF16)
        dw_ref[...] = lax.dot_general(pooled_b, dms, (((0,), (0,)), ((), ())), preferred_element_type=F32)
        dpooled = lax.dot_general(dms, w_b, (((1,), (1,)), ((), ())), preferred_element_type=F32)
        r = dpooled / count
        a2 = r + _shift_up(r, 1)
        a4 = a2 + _shift_up(a2, 2)
        a8 = a4 + _shift_up(a4, 4)
        a16 = a8 + _shift_up(a8, 8)
        lane = lax.broadcasted_iota(jnp.int32, r.shape, 1)
        du_ref[...] = (_pool_select(j, lane, a2, a4, a8, a16) - dpooled).astype(du_ref.dtype)

    return pl.pallas_call(
        body, name=name, grid=(2,),
        in_specs=[_col_spec(s, 0), _col_spec(s, C_UPOOL[0] // LANES),
                  pl.BlockSpec((None, LANES, LANES), lambda j: (j, 0, 0)), pl.BlockSpec((1, LANES), lambda j: (0, j))],
        out_specs=[_col_spec(s, 0), pl.BlockSpec((None, LANES, LANES), lambda j: (j, 0, 0)),
                   pl.BlockSpec((1, LANES), lambda j: (0, j))],
        out_shape=[jax.ShapeDtypeStruct((s, D_GROUP), BF16), jax.ShapeDtypeStruct((2, LANES, LANES), F32),
                   jax.ShapeDtypeStruct((1, D_GROUP), F32)],
        compiler_params=_params(("parallel",)),
    )(dy, proj, wbd, scale)


def _mesh_pos():
    return lax.axis_index("x"), lax.axis_index("y"), lax.axis_index("c")


def _any_specs(n):
    return [pl.BlockSpec(memory_space=pl.ANY)] * n


def _all_gather(xs, *, name):
    n = len(xs)

    def body(*refs):
        x_refs, out_refs = refs[:n], refs[n:2 * n]
        send_sems, recv_sems, local_sems = refs[2 * n:]
        x, y, cc = _mesh_pos()
        me, sibling = (x, y, cc), (x, y, 1 - cc)
        chips = [(1 - x, y), (x, 1 - y), (1 - x, 1 - y)]

        def slot(a, px, py, pc):
            return out_refs[a].at[4 * px + 2 * py + pc]

        def copy(a, k, block, to, src=None):
            return pltpu.make_async_remote_copy(
                src_ref=slot(a, *block) if src is None else src, dst_ref=slot(a, *block), send_sem=send_sems.at[a, k],
                recv_sem=recv_sems.at[a, k], device_id=to, device_id_type=pl.DeviceIdType.MESH)

        mine = [pltpu.make_async_copy(x_refs[a], slot(a, *me), local_sems.at[a]) for a in range(n)]
        first = []
        for a in range(n):
            first.append(copy(a, 0, me, sibling, src=x_refs[a]))
            first += [copy(a, 1 + j, me, (*chip, cc), src=x_refs[a]) for j, chip in enumerate(chips)]
        for cp in mine + first:
            cp.start()
        passed = []
        for j, chip in enumerate(chips):
            for a in range(n):
                copy(a, 1 + j, (*chip, cc), me).wait_recv()
                passed.append(copy(a, 4 + j, (*chip, cc), sibling))
                passed[-1].start()
        for a in range(n):
            copy(a, 0, sibling, me).wait_recv()
        for j, chip in enumerate(chips):
            for a in range(n):
                copy(a, 4 + j, (*chip, 1 - cc), me).wait_recv()
        for cp in first + passed:
            cp.wait_send()
        for cp in mine:
            cp.wait()

    return pl.pallas_call(
        body, name=name, out_shape=[jax.ShapeDtypeStruct((N_DEV,) + a.shape, a.dtype) for a in xs],
        in_specs=_any_specs(n), out_specs=_any_specs(n),
        scratch_shapes=[pltpu.SemaphoreType.DMA((n, 7)), pltpu.SemaphoreType.DMA((n, 7)), pltpu.SemaphoreType.DMA((n,))],
    )(*xs)


def _plan_gather_near(src_refs, land_refs, send_sems, recv_sems):
    x, y, cc = _mesh_pos()
    me = 4 * x + 2 * y + cc
    plan = []
    for a, (src, land) in enumerate(zip(src_refs, land_refs)):
        for k, (px, py, pc) in enumerate([(x, y, 1 - cc), (1 - x, y, cc), (x, 1 - y, cc), (1 - x, 1 - y, cc)]):
            sems = dict(send_sem=send_sems.at[4 * a + k], recv_sem=recv_sems.at[4 * a + k], device_id=(px, py, pc),
                        device_id_type=pl.DeviceIdType.MESH)
            plan.append((pltpu.make_async_remote_copy(src_ref=src, dst_ref=land.at[me], **sems),
                         pltpu.make_async_remote_copy(src_ref=src, dst_ref=land.at[4 * px + 2 * py + pc], **sems)))
    return plan


def _plan_gather_pass(src_refs, land_refs, send_sems, recv_sems):
    x, y, cc = _mesh_pos()
    plan = []
    for a, land in enumerate(land_refs):
        for j, (px, py) in enumerate([(1 - x, y), (x, 1 - y), (1 - x, 1 - y)]):
            mine, theirs = land.at[4 * px + 2 * py + cc], land.at[4 * px + 2 * py + 1 - cc]
            sems = dict(send_sem=send_sems.at[3 * a + j], recv_sem=recv_sems.at[3 * a + j], device_id=(x, y, 1 - cc),
                        device_id_type=pl.DeviceIdType.MESH)
            plan.append((pltpu.make_async_remote_copy(src_ref=mine, dst_ref=mine, **sems),
                         pltpu.make_async_remote_copy(src_ref=mine, dst_ref=theirs, **sems)))
    return plan


def _plan_sibling(src_refs, land_refs, send_sems, recv_sems):
    x, y, cc = _mesh_pos()
    plan = []
    for a, (src, land) in enumerate(zip(src_refs, land_refs)):
        cp = pltpu.make_async_remote_copy(
            src_ref=src.at[1 - cc], dst_ref=land, send_sem=send_sems.at[a], recv_sem=recv_sems.at[a],
            device_id=(x, y, 1 - cc), device_id_type=pl.DeviceIdType.MESH)
        plan.append((cp, cp))
    return plan


def _plan_chips(src_refs, land_refs, send_sems, recv_sems):
    x, y, cc = _mesh_pos()
    my_chip = 2 * x + y
    plan = []
    for a, (src, land) in enumerate(zip(src_refs, land_refs)):
        for j, (px, py) in enumerate([(1 - x, y), (x, 1 - y), (1 - x, 1 - y)]):
            peer = 2 * px + py
            sems = dict(send_sem=send_sems.at[3 * a + j], recv_sem=recv_sems.at[3 * a + j], device_id=(px, py, cc),
                        device_id_type=pl.DeviceIdType.MESH)
            plan.append((pltpu.make_async_remote_copy(src_ref=src.at[peer], dst_ref=land.at[my_chip], **sems),
                         pltpu.make_async_remote_copy(src_ref=src.at[peer], dst_ref=land.at[peer], **sems)))
    return plan


HBM_SPEC = pl.BlockSpec(memory_space=pltpu.HBM)
SEM_SPEC = pl.BlockSpec(memory_space=pltpu.SEMAPHORE)
ANY_SPEC = pl.BlockSpec(memory_space=pl.ANY)
SIDE_EFFECT = pltpu.CompilerParams(has_side_effects=pltpu.SideEffectType.DATAFLOW_SIDE_EFFECTING)


def _start_copies(plan, sems_per_array, srcs, lands, after, *, name):
    lands = [lax.empty(l, a.dtype) if isinstance(l, tuple) else l for l, a in zip(lands, srcs or lands)]
    ns, n = len(srcs), len(srcs) + len(lands)

    def body(*refs):
        send_sems, recv_sems = refs[n + len(after)], refs[n + len(after) + 1]
        for out, _ in plan(refs[:ns], refs[ns:n], send_sems, recv_sems):
            out.start()
        refs[-1][...] = jnp.zeros_like(refs[-1])

    sem = pltpu.SemaphoreType.DMA((len(lands) * sems_per_array,))
    res = pl.pallas_call(
        body, name=name,
        out_shape=(sem, sem, *[pltpu.HBM(a.shape, a.dtype) for a in srcs + lands], jax.ShapeDtypeStruct((8, LANES), F32)),
        in_specs=[HBM_SPEC] * n + [ANY_SPEC] * len(after),
        out_specs=(SEM_SPEC, SEM_SPEC, *[HBM_SPEC] * n, pl.BlockSpec(memory_space=pltpu.VMEM)),
        input_output_aliases={i: 2 + i for i in range(n)}, compiler_params=SIDE_EFFECT,
    )(*[pltpu.with_memory_space_constraint(a, pltpu.HBM) for a in srcs + lands], *after)
    return (res[0], res[1], list(res[2:2 + ns]), list(res[2 + ns:2 + n])), res[-1]


def _wait_copies(plan, handle, after, *, name):
    send, recv, srcs, lands = handle
    ns, n = len(srcs), len(srcs) + len(lands)

    def body(*refs):
        for out, inc in plan(refs[:ns], refs[ns:n], refs[n], refs[n + 1]):
            out.wait_send()
            inc.wait_recv()

    res = pl.pallas_call(
        body, name=name, out_shape=tuple(pltpu.HBM(a.shape, a.dtype) for a in srcs + lands),
        in_specs=[HBM_SPEC] * n + [SEM_SPEC, SEM_SPEC] + [ANY_SPEC] * len(after), out_specs=[HBM_SPEC] * n,
        input_output_aliases={i: i for i in range(n)}, compiler_params=SIDE_EFFECT,
    )(*srcs, *lands, send, recv, *after)
    return list(res[:ns]), list(res[ns:])


def _row_tile(rows, target=512):
    if rows <= target:
        return rows
    best = None
    for t in range(8, target + 1, 8):
        if rows % t == 0:
            best = t
    assert best is not None, (rows, target)
    return best


def _add_own(g, other, core, *, name):
    _, _, rows, cols = g.shape
    tm = _row_tile(rows)

    def body(c_ref, g_ref, o_ref, out_ref):
        out_ref[...] = g_ref[...] + o_ref[...]

    return pl.pallas_call(
        body, name=name, out_shape=jax.ShapeDtypeStruct(other.shape, other.dtype),
        grid_spec=pltpu.PrefetchScalarGridSpec(
            num_scalar_prefetch=1, grid=(4, rows // tm),
            in_specs=[pl.BlockSpec((None, None, tm, cols), lambda p, i, c_ref: (c_ref[0], p, i, 0)),
                      pl.BlockSpec((None, tm, cols), lambda p, i, c_ref: (p, i, 0))],
            out_specs=pl.BlockSpec((None, tm, cols), lambda p, i, c_ref: (p, i, 0))),
        compiler_params=_params(("parallel", "parallel")),
    )(core, g, other)


def _adamw(parts, w, m, v, after, *, name):
    layers, rows, cols = w.shape
    assert len(parts) == layers
    tm = _row_tile(rows, 256)
    nr = rows // tm

    def body(*refs):
        p_refs = refs[:layers]
        w_ref, m_ref, v_ref, _, g_ref, d_ref, nm_ref, nv_ref, g_s = refs[layers:]
        for ll in range(layers):
            @pl.when(pl.program_id(0) == ll)
            def _(ll=ll):
                g = p_refs[ll][0]
                for q in range(1, p_refs[ll].shape[0]):
                    g = g + p_refs[ll][q]
                g_s[...] = g

        g = g_s[...]
        mm = ADAM_B1 * m_ref[...] + (1.0 - ADAM_B1) * g
        vv = ADAM_B2 * v_ref[...] + (1.0 - ADAM_B2) * jnp.square(g)
        m_hat = mm / (1.0 - ADAM_B1 ** ADAM_STEP)
        v_hat = vv / (1.0 - ADAM_B2 ** ADAM_STEP)
        g_ref[...] = g
        d_ref[...] = -ADAM_LR * (m_hat / (jnp.sqrt(v_hat) + ADAM_EPS) + ADAM_WD * w_ref[...])
        nm_ref[...] = mm
        nv_ref[...] = vv

    def part_spec(ll, p):
        return pl.BlockSpec((p, tm, cols), lambda l, i: (0, jnp.where(l == ll, i, jnp.where(l < ll, 0, nr - 1)), 0))

    spec = pl.BlockSpec((None, tm, cols), lambda l, i: (l, i, 0))
    out = jax.ShapeDtypeStruct(w.shape, F32)
    return pl.pallas_call(
        body, name=name, grid=(layers, nr),
        in_specs=[part_spec(ll, parts[ll].shape[0]) for ll in range(layers)] + [spec] * 3 + [pl.BlockSpec(memory_space=pl.ANY)],
        out_specs=[spec] * 4, out_shape=[out] * 4, scratch_shapes=[pltpu.VMEM((tm, cols), F32)],
        compiler_params=_params(("arbitrary", "arbitrary")),
    )(*parts, w, m, v, after)


def _pack(arrs):
    flat = jnp.concatenate([a.reshape(-1) for a in arrs])
    rows = -(-flat.shape[0] // (PACK_COLS * 16)) * 16
    return jnp.pad(flat, (0, rows * PACK_COLS - flat.shape[0])).reshape(rows, PACK_COLS)


def _unpack(packed, shapes):
    flat = packed.reshape(-1)
    out, off = [], 0
    for shp in shapes:
        n = int(np.prod(shp))
        out.append(flat[off:off + n].reshape(shp))
        off += n
    return out


def _shards_to_full(g, axis):
    if axis == 0:
        return g.reshape(g.shape[0] * g.shape[1], g.shape[2])
    return jnp.transpose(g, (1, 0, 2)).reshape(g.shape[1], g.shape[0] * g.shape[2])


def _full_to_shards(a, axis):
    if axis == 0:
        return jnp.transpose(a.reshape(4, 2, a.shape[0] // N_DEV, a.shape[1]), (1, 0, 2, 3))
    return jnp.transpose(a.reshape(a.shape[0], 4, 2, a.shape[1] // N_DEV), (2, 1, 0, 3))


def _zeros_like_cols(a, n):
    return jnp.zeros(a.shape[:-1] + (n,), a.dtype)


def _pad_heads(a, n):
    z = _zeros_like_cols(a, HEAD)
    return jnp.concatenate([p for h in range(n) for p in (a[..., h * HEAD:(h + 1) * HEAD], z)], axis=-1)


def _unpad_heads(a, n):
    return jnp.concatenate([a[..., h * LANES:h * LANES + HEAD] for h in range(n)], axis=-1)


def _seg(first, width, sign=1):
    return (width, [(first, sign)])


def _zero(width):
    return (width, [])


def _swapped(first):
    half = MLA_ROPE // 2
    return [_seg(first + half, half, -1), _seg(first, half)]


def _padded_heads(first, n):
    return [s for h in range(n) for s in (_seg(first + HEAD * h, HEAD), _zero(HEAD))]


def _layout_w_in():
    kr = 384
    return (_padded_heads(1440, 4) + [_seg(0, 256), _seg(416, 256), _seg(672, 256), _seg(928, 256), _seg(1184, 256)]
            + _padded_heads(1696, 2) + _padded_heads(1824, 2) + [_seg(256, 128)]
            + [_zero(HEAD), _seg(kr, MLA_ROPE), _seg(kr, MLA_ROPE)] + [_zero(HEAD)] + _swapped(kr) + _swapped(kr))


def _layout_w_uq():
    out = []
    for h in range(MLA_HEADS):
        out += [_seg(96 * h, MLA_NOPE), _seg(96 * h + MLA_NOPE, MLA_ROPE)] + _swapped(96 * h + MLA_NOPE)
    return out


def _layout_w_ukv():
    keys = [s for h in range(MLA_HEADS) for s in (_seg(LANES * h, HEAD), _zero(HEAD))]
    values = [s for h in range(MLA_HEADS) for s in (_seg(LANES * h + HEAD, HEAD), _zero(HEAD))]
    return keys + values


def _layout_w_gate_up():
    return [_seg(half + j, GU_TILE) for j in range(0, D_FF, GU_TILE) for half in (0, D_FF)]


LAYOUTS = dict(w_in=_layout_w_in(), w_uq=_layout_w_uq(), w_ukv=_layout_w_ukv(), w_gate_up=_layout_w_gate_up())
SWAPPED_UPDATE = ("w_gate_up",)
OWN_COLS = dict(w_in=1952, w_uq=384, w_ukv=512, w_gate_up=2 * D_FF)


def _plan_extend(layout, shard):
    plan = []
    for width, terms in layout:
        if not terms:
            plan.append((width, []))
            continue
        (first, sign), = terms
        while width:
            g, off = divmod(first, shard)
            w = min(width, shard - off)
            plan.append((w, [(g, off, sign)]))
            first, width = first + w, width - w
    return [plan]


def _plan_fold(layout, own_cols):
    sources = [[] for _ in range(own_cols)]
    e = 0
    for width, terms in layout:
        for first, sign in terms:
            for i in range(width):
                sources[first + i].append((e + i, sign))
        e += width
    shard = own_cols // N_DEV
    plans = {}
    for g in range(N_DEV):
        plan, n = [], g * shard
        while n < (g + 1) * shard:
            w = 1
            while n + w < (g + 1) * shard and [(c + w, sg) for c, sg in sources[n]] == sources[n + w]:
                w += 1
            plan.append((w, [(0, c, sg) for c, sg in sources[n]]))
            n += w
        plans[g] = plan
    return [plans[2 * p + c] for c in range(2) for p in range(4)]


def _assemble(src, plans, out_cols, out_dtype, *, name, transposed=False):
    g, rows, c = src.shape
    tm = _row_tile(rows, 256)
    pad = -out_cols % LANES if transposed else 0

    def body(s_ref, o_ref):
        blocks = [s_ref[i].astype(F32) for i in range(g)]
        for d, plan in enumerate(plans):
            pieces = []
            for width, terms in plan + ([(pad, [])] if pad else []):
                v = None
                for b, first, sign in terms:
                    t = blocks[b][:, first:first + width]
                    t = -t if sign < 0 else t
                    v = t if v is None else v + t
                pieces.append(jnp.zeros((tm, width), F32) if v is None else v)
            block = pieces[0] if len(pieces) == 1 else jnp.concatenate(pieces, axis=1)
            o_ref[d] = (block.T[:out_cols, :] if transposed else block).astype(o_ref.dtype)

    block, shape = ((out_cols, tm), (out_cols, rows)) if transposed else ((tm, out_cols), (rows, out_cols))
    return pl.pallas_call(
        body, name=name, grid=(rows // tm,), in_specs=[pl.BlockSpec((g, tm, c), lambda i: (0, i, 0))],
        out_specs=pl.BlockSpec((len(plans),) + block, (lambda i: (0, 0, i)) if transposed else (lambda i: (0, i, 0))),
        out_shape=jax.ShapeDtypeStruct((len(plans),) + shape, out_dtype), compiler_params=_params(("parallel",)),
    )(src)


def _extend(nm, gathered, *, name):
    layout = LAYOUTS[nm]
    return _assemble(gathered, _plan_extend(layout, OWN_COLS[nm] // N_DEV), sum(w for w, _ in layout), BF16, name=name)[0]


def _fold_to_shards(nm, grad_ext, *, name):
    shards = _assemble(grad_ext[None], _plan_fold(LAYOUTS[nm], OWN_COLS[nm]), OWN_COLS[nm] // N_DEV, F32, name=name,
                       transposed=nm in SWAPPED_UPDATE)
    return shards.reshape((2, 4) + shards.shape[1:])


def _rope_tables(s):
    inv = 1.0 / (ROPE_THETA ** (jnp.arange(0, MLA_ROPE, 2, dtype=F32) / MLA_ROPE))
    ang = jnp.arange(s, dtype=F32)[:, None] * inv[None, :]
    cos, sin = jnp.cos(ang), jnp.sin(ang)
    c32, s32 = jnp.concatenate([cos, cos], axis=1), jnp.concatenate([sin, sin], axis=1)
    zeros, ones = jnp.zeros((s, HEAD), F32), jnp.ones((s, HEAD), F32)
    tq = jnp.concatenate([ones, c32, s32], axis=1) * (1.0 / math.sqrt(MLA_NOPE + MLA_ROPE))
    return (jnp.tile(tq, (1, MLA_HEADS)), jnp.concatenate([zeros, c32, c32], axis=1),
            jnp.concatenate([zeros, s32, s32], axis=1))


def _gn(y):
    return y * _rstd(y, D_GROUP)


def _mixer_fwd(x, w, tabs, l, after_attention=None):
    s = x.shape[0]
    tq, tkc, tks = tabs
    n = lambda t: f"l{l}_{t}"
    h = _rms_fwd(("row", x, D_MODEL, 0), w["attn_norm"], name=n("attn_norm"), rows=s, width=D_MODEL)
    proj = _mm(h, w["w_in"], name=n("in_proj"))
    def prep(cq, ckv, kr, krp, gq, gkv, wuq, wukv, tqv, tc, ts):
        cqn = (cq * _rstd(cq, 256) * gq).astype(BF16)
        ckvn = (ckv * _rstd(ckv, 128) * gkv).astype(BF16)
        qe = jnp.dot(cqn, wuq, preferred_element_type=F32)
        kve = jnp.dot(ckvn, wukv, preferred_element_type=F32)
        kb = kr * tc + krp * ts
        kvv = kve[:, 512:]
        lane = lax.broadcasted_iota(jnp.int32, kvv.shape, 1) & (LANES - 1)
        v = jnp.where(lane == HEAD, 1.0, kvv)
        k = kve[:, :512] + jnp.tile(kb, (1, MLA_HEADS))
        return cqn, ckvn, qe * tqv, k, k, v, v

    cqn, ckvn, qm, km, kmt, vm, vmt = _rowwise(
        prep, name=n("mla_prep"), rows=s,
        ins=[("row", proj, 256, C_CQ[0] // 256), ("row", proj, 128, C_CKV[0] // 128),
             ("row", proj, 128, C_KR[0] // 128), ("row", proj, 128, C_KRP[0] // 128),
             ("full", w["mla_q_norm"]), ("full", w["mla_kv_norm"]), ("full", w["w_uq"]), ("full", w["w_ukv"]),
             ("row", tq, 512, 0), ("row", tkc, 128, 0), ("row", tks, 128, 0)],
        outs=[(256, BF16), (128, BF16), (512, BF16, MLA_HEADS), (512, BF16, MLA_HEADS), (512, BF16, MLA_HEADS, "T"),
              (512, BF16, MLA_HEADS), (512, BF16, MLA_HEADS, "T")])
    y_a, lse_row = _mla_fwd(qm, km, vmt, name=n("mla_fwd"), s=s, nh=MLA_HEADS)
    mix_norm = w["mix_norm"]
    if after_attention is not None:
        mix_norm = mix_norm + after_attention(y_a)[0, 0]
    y_b = _conv_fwd(proj, w["conv_w"], name=n("conv_fwd"), s=s)
    y_c = _pool_fwd(proj, w["pool_wbd"], w["pool_scale"], name=n("pool_fwd"), s=s)
    y_d, lse_d = _swa_fwd(proj, w["hp_swa"], name=n("swa_fwd"), s=s, scale=1.0 / math.sqrt(HEAD))

    def mix(ya, yb, yc, yd, mn):
        return (jnp.concatenate([_gn(_unpad_heads(ya, 4)), _gn(yb), _gn(yc), _gn(_unpad_heads(yd, 4))], axis=1) * mn,)

    mixed = _rowwise(mix, name=n("group_norm"), rows=s,
                     ins=[("heads", y_a), ("row", y_b, 256, 0), ("row", y_c, 256, 0), ("row", y_d, 512, 0),
                          ("full", mix_norm)], outs=[(D_MODEL, BF16)])[0]
    x1 = _mm(mixed, w["w_o"], res=x, name=n("out_proj"))
    saved = dict(x=x, h=h, proj=proj, cqn=cqn, ckvn=ckvn, qm=qm, km=km, kmt=kmt, vm=vm, y_a=y_a, lse_row=lse_row,
                 y_b=y_b, y_c=y_c, y_d=y_d, lse_d=lse_d, mixed=mixed)
    return x1, saved


def _ffn_fwd(x1, w, l):
    s = x1.shape[0]
    n = lambda t: f"l{l}_{t}"
    h2 = _rms_fwd(("row", x1, D_MODEL, 0), w["ffn_norm"], name=n("ffn_norm"), rows=s, width=D_MODEL)

    def swiglu(gu):
        g, u = gu[:, :GU_TILE], gu[:, GU_TILE:]
        return gu, g * jax.nn.sigmoid(g) * u

    gu, act = _mm(h2, w["w_gate_up"], tm=2048, tn=2 * GU_TILE, name=n("gate_up"),
                  epilogue=(swiglu, [], [(2 * D_FF, BF16), (D_FF, BF16)]))
    x2 = _mm(act, w["w_down"], res=x1, tk=D_FF // 2, name=n("down"))
    return x2, dict(x1=x1, h2=h2, gu=gu, act=act)


def _ffn_bwd_down(dx2, sv, w, l):
    n = lambda t: f"l{l}_{t}"

    def swiglu_bwd(da, gu):
        gt, u = gu[:, :GU_TILE].astype(F32), gu[:, GU_TILE:].astype(F32)
        sg = jax.nn.sigmoid(gt)
        return (jnp.concatenate([da * u * sg * (1.0 + gt * (1.0 - sg)), da * gt * sg], axis=1),)

    dgu = _mm(dx2[1], w["w_down"], tb=True, tm=2048, tn=GU_TILE, name=n("d_act"),
              epilogue=(swiglu_bwd, [sv["gu"]], [(2 * D_FF, BF16)]))[0]
    g = dict(w_down=_mm(sv["act"], dx2[1], ta=True, tm=D_FF // 2, name=n("dw_down")))
    return dgu, g


def _ffn_bwd_up(dx2, dgu, sv, w, l):
    s = dgu.shape[0]
    n = lambda t: f"l{l}_{t}"
    dh2 = _mm(dgu, w["w_gate_up"], tb=True, tk=D_FF // 2, name=n("d_h2"))
    g = dict(w_gate_up=_mm(sv["h2"], dgu, ta=True, tn=D_FF // 2, name=n("dw_gate_up")))
    dx1, dx1_b, g["ffn_norm"] = _rms_bwd(("row", sv["x1"], D_MODEL, 0), w["ffn_norm"], dh2, dx2[0],
                                         name=n("ffn_norm_bwd"), rows=s, width=D_MODEL, out_dtypes=(F32, BF16))
    return (dx1, dx1_b), g


def _mixer_bwd_out(dx1, sv, w, l):
    s = dx1[1].shape[0]
    n = lambda t: f"l{l}_{t}"
    dmixed = _mm(dx1[1], w["w_o"], tb=True, name=n("d_mixed"))
    g = dict(w_o=_mm(sv["mixed"], dx1[1], ta=True, name=n("dw_o")))

    def mix_bwd(dm, ya, yb, yc, yd, mn):
        outs, dmn = [], []
        for i, y in enumerate((_unpad_heads(ya, 4), yb, yc, _unpad_heads(yd, 4))):
            lo, hi = i * D_GROUP, (i + 1) * D_GROUP
            r = _rstd(y, D_GROUP)
            nrm = y * r
            dmg = dm[:, lo:hi]
            dn = dmg * mn[:, lo:hi]
            dy = r * (dn - nrm * (jnp.sum(dn * nrm, axis=-1, keepdims=True) * (1.0 / D_GROUP)))
            outs.append(_pad_heads(dy, 4) if i in (0, 3) else dy)
            dmn.append(jnp.sum(dmg * nrm, axis=0, keepdims=True))
        return (*outs, jnp.concatenate(dmn, axis=1))

    dy_a, dy_b, dy_c, dy_d, g["mix_norm"] = _rowwise(
        mix_bwd, name=n("group_norm_bwd"), rows=s,
        ins=[("row", dmixed, D_MODEL, 0), ("heads", sv["y_a"]), ("row", sv["y_b"], 256, 0),
             ("row", sv["y_c"], 256, 0), ("row", sv["y_d"], 512, 0), ("full", w["mix_norm"])],
        outs=[(512, F32, MLA_HEADS), (256, F32), (256, F32), (512, F32)], reds=[(1, D_MODEL)])
    return (dy_a, dy_b, dy_c, dy_d), g


def _mixer_bwd_in(dx1, dys, sv, w, tabs, l):
    s = dx1[0].shape[0]
    tq, tkc, tks = tabs
    n = lambda t: f"l{l}_{t}"
    dy_a, dy_b, dy_c, dy_d = dys
    g = {}

    proj = sv["proj"]
    dq_sw, dk_sw, dv_sw, dsink = _swa_bwd(proj, sv["y_d"], dy_d, sv["lse_d"], w["hp_swa"], name=n("swa_bwd"), s=s,
                                          scale=1.0 / math.sqrt(HEAD))
    g["swa_sinks"] = dsink[:, 0, 0]

    dqm, dkm, dvm = _mla_bwd(sv["qm"], sv["km"], sv["kmt"], sv["vm"], sv["y_a"], dy_a, sv["lse_row"], w["hp_swa"],
                             name=n("mla_bwd"), s=s, nh=MLA_HEADS)

    def rms_bwd(x, gv, dy, width):
        r = _rstd(x, width)
        dyg = dy * gv
        dx = r * dyg - x * (r * r * r) * (jnp.sum(dyg * x, axis=-1, keepdims=True) * (1.0 / width))
        return dx, jnp.sum(dy * x * r, axis=0, keepdims=True)

    def prep_bwd(dq, dk, dv, cq, ckv, gq, gkv, wuq, wukv, tqv, tc, ts):
        dkb = dk[:, 0:128] + dk[:, 128:256] + dk[:, 256:384] + dk[:, 384:512]
        dq_ext = (dq * tqv).astype(BF16)
        dkv_ext = jnp.concatenate([dk.astype(BF16), dv], axis=1)
        dcqn = lax.dot_general(dq_ext, wuq, NT_DIMS, preferred_element_type=F32)
        dckvn = lax.dot_general(dkv_ext, wukv, NT_DIMS, preferred_element_type=F32)
        dcq, dgq = rms_bwd(cq, gq, dcqn, 256)
        dckv, dgkv = rms_bwd(ckv, gkv, dckvn, 128)
        return dq_ext, dkv_ext, dkb * tc, dkb * ts, dcq, dckv, dgq, dgkv

    dq_ext, dkv_ext, dkr, dkrp, dcq, dckv, g["mla_q_norm"], g["mla_kv_norm"] = _rowwise(
        prep_bwd, name=n("mla_prep_bwd"), rows=s,
        ins=[("heads", dqm), ("heads", dkm), ("heads", dvm), ("row", proj, 256, C_CQ[0] // 256),
             ("row", proj, 128, C_CKV[0] // 128), ("full", w["mla_q_norm"]), ("full", w["mla_kv_norm"]),
             ("full", w["w_uq"]), ("full", w["w_ukv"]), ("row", tq, 512, 0), ("row", tkc, 128, 0), ("row", tks, 128, 0)],
        outs=[(512, BF16), (1024, BF16), (128, BF16), (128, BF16), (256, BF16), (128, BF16)],
        reds=[(1, 256), (1, 128)])
    g["w_uq"] = _mm(sv["cqn"], dq_ext, ta=True, name=n("dw_uq"))
    g["w_ukv"] = _mm(sv["ckvn"], dkv_ext, ta=True, name=n("dw_ukv"))

    dgb, dgc, duc, g["conv_w"] = _conv_bwd(dy_b, proj, w["conv_w"], name=n("conv_bwd"), s=s)
    dup, g["pool_wbd"], g["pool_scale"] = _pool_bwd(dy_c, proj, w["pool_wbd"], w["pool_scale"], name=n("pool_bwd"), s=s)

    dproj = jnp.concatenate([dq_sw, dcq, dgb, dgc, duc, dup, dk_sw.astype(BF16), dv_sw.astype(BF16), dckv, dkr, dkrp],
                            axis=1)
    g["w_in"] = _mm(sv["h"], dproj, ta=True, name=n("dw_in"))
    return dproj, g


def _mixer_bwd_norm(dx1, dproj, sv, w, l):
    n = lambda t: f"l{l}_{t}"
    dh = _mm(dproj, w["w_in"], tb=True, name=n("d_h"))
    dx0, dx0_b, dg = _rms_bwd(("row", sv["x"], D_MODEL, 0), w["attn_norm"], dh, dx1[0], name=n("attn_norm_bwd"),
                              rows=dh.shape[0], width=D_MODEL, out_dtypes=(F32, BF16))
    return (dx0, dx0_b), dict(attn_norm=dg)


def _loss_head(x, target, g, *, s):
    def fn(xv, tv, gv):
        r = _rstd(xv, D_MODEL)
        e = xv * r * gv - tv
        part = jnp.sum(jnp.sum(e * e, axis=1, keepdims=True), axis=0, keepdims=True) * (0.5 / D_MODEL)
        dy = e * (1.0 / D_MODEL)
        dyg = dy * gv
        dx = r * dyg - xv * (r * r * r) * (jnp.sum(dyg * xv, axis=-1, keepdims=True) * (1.0 / D_MODEL))
        return dx, dx, jnp.sum(dy * xv * r, axis=0, keepdims=True), jnp.broadcast_to(part, (1, LANES))

    return _rowwise(fn, name="loss_head", rows=s,
                    ins=[("row", x, D_MODEL, 0), ("row", target, D_MODEL, 0), ("full", g)],
                    outs=[(D_MODEL, F32), (D_MODEL, BF16)], reds=[(1, D_MODEL), (1, LANES)])


def _alibi_slopes(n):
    return np.asarray([2.0 ** (-8.0 * (i + 1) / n) for i in range(n)], dtype=np.float32)


MIXER_WEIGHTS = ("w_in", "w_uq", "w_ukv", "conv_w", "w_o")
FFN_WEIGHTS = ("w_gate_up", "w_down")


def _mixer_weights(full, rep, l):
    pw = rep["pool_w"][l]
    z = jnp.zeros((HEAD, HEAD), F32)
    wbd = jnp.stack([jnp.block([[pw[2 * j], z], [z, pw[2 * j + 1]]]) for j in range(2)])
    return dict(
        attn_norm=rep["attn_norm"][l][None], w_in=full["w_in"], mla_q_norm=rep["mla_q_norm"][l][None],
        w_uq=full["w_uq"], mla_kv_norm=rep["mla_kv_norm"][l][None], w_ukv=full["w_ukv"],
        conv_w=full["conv_w"], pool_wbd=wbd, pool_scale=rep["pool_scale"][l][None],
        mix_norm=rep["mix_norm"][l][None], w_o=full["w_o"],
        hp_swa=jnp.stack([jnp.asarray(_alibi_slopes(SWA_HEADS)), rep["swa_sinks"][l]], axis=1))


def _ffn_weights(full, rep, l):
    return dict(ffn_norm=rep["ffn_norm"][l][None], w_gate_up=full["w_gate_up"], w_down=full["w_down"])


def _small_grads(g):
    rows = ("attn_norm", "mla_q_norm", "mla_kv_norm", "pool_scale", "ffn_norm", "mix_norm")
    out = {nm: g[nm][0] for nm in rows if nm in g}
    if "swa_sinks" in g:
        out["swa_sinks"] = g["swa_sinks"]
    if "pool_wbd" in g:
        e = g["pool_wbd"]
        out["pool_w"] = jnp.stack([e[j // 2][HEAD * (j % 2):HEAD * (j % 2 + 1), HEAD * (j % 2):HEAD * (j % 2 + 1)]
                                   for j in range(4)])
    return out


def kernel(x, attn_norm, w_in, mla_q_norm, w_uq, mla_kv_norm, w_ukv, conv_w, pool_w, pool_scale, swa_sinks, mix_norm, w_o, ffn_norm, w_gate_up, w_down, final_norm, loss_target, m_attn_norm, m_w_in, m_mla_q_norm, m_w_uq, m_mla_kv_norm, m_w_ukv, m_conv_w, m_pool_w, m_pool_scale, m_swa_sinks, m_mix_norm, m_w_o, m_ffn_norm, m_w_gate_up, m_w_down, m_final_norm, v_attn_norm, v_w_in, v_mla_q_norm, v_w_uq, v_mla_kv_norm, v_w_ukv, v_conv_w, v_pool_w, v_pool_scale, v_swa_sinks, v_mix_norm, v_w_o, v_ffn_norm, v_w_gate_up, v_w_down, v_final_norm):
    given = dict(locals())
    sh_names = [nm for nm, _, _ in SHARDED]
    sh_axis = {nm: ax - 1 for nm, _, ax in SHARDED}
    rep_names = [nm for nm, _ in REPLICATED]
    rep_shapes = [shp for _, shp in REPLICATED]
    rep = {nm: given[nm] for nm in rep_names if nm != "loss"}
    me = 4 * lax.axis_index("x") + 2 * lax.axis_index("y") + lax.axis_index("c")
    my_chip = 2 * lax.axis_index("x") + lax.axis_index("y")
    core = lax.axis_index("c").astype(jnp.int32).reshape(1)

    def behind(token, a):
        return a + token[0, 0].astype(a.dtype)

    def wire(nm, l):
        if nm == "conv_w":
            return lax.bitcast_convert_type(given[nm][l], BF16).reshape(3, -1)
        return given[nm][l].astype(BF16)

    def whole(nm, g, tag):
        if nm in LAYOUTS:
            return _extend(nm, g, name=f"extend_{nm}_{tag}")
        if nm == "conv_w":
            g = lax.bitcast_convert_type(g.reshape(N_DEV, 3, -1, 2), F32)
        return _shards_to_full(g, sh_axis[nm])

    def near_start(names, l, after, tag):
        srcs = [wire(nm, l) for nm in names]
        return _start_copies(_plan_gather_near, 4, srcs, [(N_DEV,) + a.shape for a in srcs], after, name=f"start_gather_{tag}")

    def pass_on(handle, after, tag):
        srcs, lands = _wait_copies(_plan_gather_near, handle, after, name=f"wait_gather_{tag}")
        handle, token = _start_copies(_plan_gather_pass, 3, [], lands, [], name=f"start_pass_{tag}")
        return (srcs, handle), token

    def gathered(names, state, after, tag):
        srcs, handle = state
        _, lands = _wait_copies(_plan_gather_pass, handle, after, name=f"wait_pass_{tag}")
        return {nm: whole(nm, lax.dynamic_update_index_in_dim(land, src, me, 0), tag)
                for nm, src, land in zip(names, srcs, lands)}

    layer1 = MIXER_WEIGHTS + FFN_WEIGHTS
    got = _all_gather([wire(nm, 0) for nm in MIXER_WEIGHTS], name="gather_mixer0")
    full_m0 = {nm: whole(nm, g, "mixer0") for nm, g in zip(MIXER_WEIGHTS, got)}
    h_f0, tok = near_start(FFN_WEIGHTS, 0, [], "ffn0")
    h_l1, tok = near_start(layer1, 1, [tok], "layer1")

    xs, target = x[0], loss_target[0]
    s = xs.shape[0]
    tabs = _rope_tables(s)
    wm, wf, svm, svf = [None] * DEPTH, [None] * DEPTH, [None] * DEPTH, [None] * DEPTH
    wm[0] = _mixer_weights(full_m0, rep, 0)
    wm[0]["attn_norm"] = behind(tok, wm[0]["attn_norm"])
    passing = {}

    def pass_ffn0(y_a):
        passing["ffn0"], token = pass_on(h_f0, [y_a], "ffn0")
        return token

    x1, svm[0] = _mixer_fwd(xs, wm[0], tabs, 0, after_attention=pass_ffn0)
    wf[0] = _ffn_weights(gathered(FFN_WEIGHTS, passing["ffn0"], [x1], "ffn0"), rep, 0)
    passing["layer1"], tok = pass_on(h_l1, [x1], "layer1")
    wf[0]["ffn_norm"] = behind(tok, wf[0]["ffn_norm"])
    x2, svf[0] = _ffn_fwd(x1, wf[0], 0)
    full_1 = gathered(layer1, passing["layer1"], [x2], "layer1")
    wm[1], wf[1] = _mixer_weights(full_1, rep, 1), _ffn_weights(full_1, rep, 1)
    x1, svm[1] = _mixer_fwd(x2, wm[1], tabs, 1)
    x2, svf[1] = _ffn_fwd(x1, wf[1], 1)
    dx_f, dx_b, d_final, loss = _loss_head(x2, target, rep["final_norm"][None], s=s)
    dx = (dx_f, dx_b)

    parts = {}

    def reduce_start(grads, l, after, tag):
        names = [nm for nm in sh_names if nm in grads]
        mine = [_fold_to_shards(nm, grads[nm], name=f"fold_{nm}_{l}") if nm in LAYOUTS
                else _full_to_shards(grads[nm], sh_axis[nm]) for nm in names]
        handle, token = _start_copies(_plan_sibling, 1, mine, [m.shape[1:] for m in mine], after, name=f"start_sibling_{tag}")
        return (names, l, handle), token

    def reduce_mid(state, after, tag):
        names, l, handle = state
        mine, theirs = _wait_copies(_plan_sibling, handle, after, name=f"wait_sibling_{tag}")
        sums = [_add_own(g, o, core, name=f"chip_sum_{nm}_{l}") for nm, g, o in zip(names, mine, theirs)]
        handle, token = _start_copies(_plan_chips, 3, sums, [a.shape for a in sums], [], name=f"start_chips_{tag}")
        return (names, l, handle), token

    def reduce_end(state, after, tag):
        names, l, handle = state
        sums, lands = _wait_copies(_plan_chips, handle, after, name=f"wait_chips_{tag}")
        for nm, own, land in zip(names, sums, lands):
            parts[nm, l] = lax.dynamic_update_index_in_dim(land, lax.dynamic_index_in_dim(own, my_chip, 0, keepdims=False),
                                                           my_chip, 0)

    small = [None] * DEPTH
    in_flight = []
    pending = None
    for l in reversed(range(DEPTH)):
        dgu, g_down = _ffn_bwd_down(dx, svf[l], wf[l], l)
        if pending is not None:
            state, token = reduce_mid(pending, [dgu], f"mixer{l + 1}")
            in_flight.append((state, f"mixer{l + 1}"))
            wf[l]["ffn_norm"] = behind(token, wf[l]["ffn_norm"])
        dx1, g_up = _ffn_bwd_up(dx, dgu, svf[l], wf[l], l)
        g_ffn = {**g_down, **g_up}
        state, token = reduce_start(g_ffn, l, [], f"ffn{l}")
        wm[l]["mix_norm"] = behind(token, wm[l]["mix_norm"])
        dys, g_out = _mixer_bwd_out(dx1, svm[l], wm[l], l)
        state, token = reduce_mid(state, [dys[0]], f"ffn{l}")
        in_flight.append((state, f"ffn{l}"))
        wm[l]["hp_swa"] = behind(token, wm[l]["hp_swa"])
        dproj, g_in = _mixer_bwd_in(dx1, dys, svm[l], wm[l], tabs, l)
        g_mixer = {**g_out, **g_in}
        pending, token = reduce_start(g_mixer, l, [], f"mixer{l}")
        wm[l]["attn_norm"] = behind(token, wm[l]["attn_norm"])
        dx, g_norm = _mixer_bwd_norm(dx1, dproj, svm[l], wm[l], l)
        small[l] = _small_grads({**g_ffn, **g_mixer, **g_norm})
    grads = {nm: jnp.stack([small[l][nm] for l in range(DEPTH)]) for nm in rep_names if nm in small[0]}
    grads["final_norm"] = d_final[0]
    grads["loss"] = loss[0, :1]
    zero = jnp.zeros((1,), F32)
    small = _all_gather([behind(token, _pack([grads[nm] for nm in rep_names]))], name="gather_small_grads")

    last, token = reduce_mid(pending, [dx[0], small[0]], "mixer0")
    for state, tag in in_flight:
        reduce_end(state, [], tag)
    grad_x = dx[0]

    def adamw(nm, after):
        swap = (lambda a: jnp.swapaxes(a, 1, 2)) if nm in SWAPPED_UPDATE else (lambda a: a)
        res = _adamw([parts[nm, l] for l in range(DEPTH)], swap(given[nm]), swap(given["m_" + nm]), swap(given["v_" + nm]),
                     after, name=f"adamw_{nm}")
        return [swap(a) for a in res]

    sh_out = {nm: adamw(nm, token) for nm in FFN_WEIGHTS}
    packs = [_pack([given.get(pre + nm, zero) for nm in rep_names])[None] for pre in ("", "m_", "v_")]
    rep_res = _adamw(small, *packs, token, name="adamw_replicated")
    rep_out = [dict(zip(rep_names, _unpack(o[0], rep_shapes))) for o in rep_res]

    reduce_end(last, [rep_res[0], sh_out["w_down"][0]], "mixer0")
    sh_out.update({nm: adamw(nm, token) for nm in MIXER_WEIGHTS})

    out = [rep_out[0]["loss"][0], grad_x[None]]
    for i in range(4):
        out += [sh_out[nm][i] if nm in sh_axis else rep_out[i][nm] for nm in WEIGHT_ORDER]
    return tuple(out)
```

```python
import functools
import math

import numpy as np
import jax
import jax.numpy as jnp
from jax import lax
from jax.experimental import pallas as pl
from jax.experimental.pallas import tpu as pltpu

F32 = jnp.float32
BF16 = jnp.bfloat16

D_MODEL = 1024
DEPTH = 2
D_GROUP = 256
MLA_HEADS = 4
MLA_NOPE = 64
MLA_ROPE = 32
ROPE_THETA = 10000.0
POOL_WINDOWS = (2, 4, 8, 16)
SWA_HEADS = 4
SWA_KV_HEADS = 2
SWA_WINDOW = 128
D_FF = 2816
GU_TILE = 256
RMS_EPS = 1e-6
LANES = 128
HEAD = 64
VMEM_LIMIT = 48 * 1024 * 1024
NEG = -1e30

ADAM_LR = 0.001
ADAM_B1 = 0.9
ADAM_B2 = 0.999
ADAM_EPS = 1e-08
ADAM_WD = 0.01
ADAM_STEP = 10

N_DEV = 8
PACK_COLS = 1024

C_QSW, C_CQ, C_GB, C_GC, C_UCONV, C_UPOOL = (0, 512), (512, 256), (768, 256), (1024, 256), (1280, 256), (1536, 256)
C_KSW, C_VSW, C_CKV, C_KR, C_KRP = (1792, 256), (2048, 256), (2304, 128), (2432, 128), (2560, 128)

SHARDED = (("w_in", (DEPTH, 1024, 244), 2), ("w_uq", (DEPTH, 256, 48), 2), ("w_ukv", (DEPTH, 128, 64), 2),
           ("conv_w", (DEPTH, 3, 32), 2), ("w_o", (DEPTH, 128, 1024), 1), ("w_gate_up", (DEPTH, 1024, 704), 2),
           ("w_down", (DEPTH, 352, 1024), 1))
REPLICATED = (("attn_norm", (DEPTH, 1024)), ("mla_q_norm", (DEPTH, 256)), ("mla_kv_norm", (DEPTH, 128)),
              ("pool_w", (DEPTH, 4, 64, 64)), ("pool_scale", (DEPTH, 256)), ("swa_sinks", (DEPTH, 4)),
              ("mix_norm", (DEPTH, 1024)), ("ffn_norm", (DEPTH, 1024)), ("final_norm", (1024,)), ("loss", (1,)))
WEIGHT_ORDER = ("attn_norm", "w_in", "mla_q_norm", "w_uq", "mla_kv_norm", "w_ukv", "conv_w", "pool_w", "pool_scale",
                "swa_sinks", "mix_norm", "w_o", "ffn_norm", "w_gate_up", "w_down", "final_norm")


def _params(sem):
    return pltpu.CompilerParams(dimension_semantics=sem, vmem_limit_bytes=VMEM_LIMIT)


def _pick(dim, target):
    if dim <= target:
        return dim
    best = None
    for t in range(LANES, target + 1, LANES):
        if dim % t == 0:
            best = t
    assert best is not None, (dim, target)
    return best


def _mm(a, b, *, name, ta=False, tb=False, res=None, out_dtype=F32, tm=1024, tn=1024, tk=1024, epilogue=None):
    m, k = (a.shape[1], a.shape[0]) if ta else a.shape
    n = b.shape[0] if tb else b.shape[1]
    assert (b.shape[1] if tb else b.shape[0]) == k
    tm, tn, tk = _pick(m, tm), _pick(n, tn), _pick(k, tk)
    nj, nk = n // tn, k // tk
    dims = (((0 if ta else 1,), (1 if tb else 0,)), ((), ()))
    fn, extra, outs = epilogue if epilogue is not None else (None, [], [(n, out_dtype)])
    if res is not None:
        assert epilogue is None
        fn, extra = (lambda acc, r: (acc + r,)), [res]
    n_in, n_out = 2 + len(extra), len(outs)

    def body(*refs):
        a_ref, b_ref, acc_ref = refs[0], refs[1], refs[-1]
        kk = pl.program_id(2)

        def product():
            return lax.dot_general(a_ref[...].astype(BF16), b_ref[...].astype(BF16), dims, preferred_element_type=F32)

        def finish(acc):
            tiles = (acc,) if fn is None else fn(acc, *[r[...] for r in refs[2:n_in]])
            for o_ref, tile in zip(refs[n_in:n_in + n_out], tiles):
                o_ref[...] = tile.astype(o_ref.dtype)

        if nk == 1:
            finish(product())
            return

        @pl.when(kk == 0)
        def _():
            acc_ref[...] = jnp.zeros_like(acc_ref)

        acc_ref[...] += product()

        @pl.when(kk == nk - 1)
        def _():
            finish(acc_ref[...])

    def col_tiles(width):
        assert width % (nj * LANES) == 0, (width, nj)
        return pl.BlockSpec((tm, width // nj), lambda i, j, kk: (i, j))

    a_spec = pl.BlockSpec((tk, tm), lambda i, j, kk: (kk, i)) if ta else pl.BlockSpec((tm, tk), lambda i, j, kk: (i, kk))
    b_spec = pl.BlockSpec((tn, tk), lambda i, j, kk: (j, kk)) if tb else pl.BlockSpec((tk, tn), lambda i, j, kk: (kk, j))
    res_ = pl.pallas_call(
        body, name=name, grid=(m // tm, nj, nk), in_specs=[a_spec, b_spec] + [col_tiles(e.shape[1]) for e in extra],
        out_specs=[col_tiles(w) for w, _ in outs],
        out_shape=[jax.ShapeDtypeStruct((m, w), dt) for w, dt in outs],
        scratch_shapes=[pltpu.VMEM((tm, tn), F32)] if nk > 1 else [],
        compiler_params=_params(("parallel", "parallel", "arbitrary")),
    )(a, b, *extra)
    return res_[0] if epilogue is None else res_


def _rowwise(fn, *, name, rows, ins, outs, reds=(), tm=512):
    tm = min(tm, rows)
    assert rows % tm == 0
    n_in, n_out = len(ins), len(outs)

    def body(*refs):
        vals = [jnp.concatenate([r[h] for h in range(r.shape[0])], axis=1) if spec[0] == "heads" else r[...]
                for spec, r in zip(ins, refs[:n_in])]
        res = fn(*vals)
        for out, r, v in zip(outs, refs[n_in:n_in + n_out], res[:n_out]):
            if len(out) >= 3:
                for h in range(out[2]):
                    piece = v[:, h * LANES:(h + 1) * LANES]
                    r[h] = (piece.T if len(out) == 4 else piece).astype(r.dtype)
            else:
                r[...] = v.astype(r.dtype)
        if reds:
            @pl.when(pl.program_id(0) == 0)
            def _():
                for r in refs[n_in + n_out:]:
                    r[...] = jnp.zeros_like(r)

            for r, v in zip(refs[n_in + n_out:], res[n_out:]):
                r[...] += v

    in_specs, args = [], []
    for spec in ins:
        if spec[0] == "row":
            _, arr, width, blk = spec
            in_specs.append(pl.BlockSpec((tm, width), functools.partial(lambda i, blk: (i, blk), blk=blk)))
        elif spec[0] == "heads":
            arr = spec[1]
            in_specs.append(pl.BlockSpec((arr.shape[0], tm, LANES), lambda i: (0, i, 0)))
        else:
            arr = spec[1]
            in_specs.append(pl.BlockSpec(arr.shape, functools.partial(lambda i, nd: (0,) * nd, nd=arr.ndim)))
        args.append(arr)
    def out_spec(o):
        if len(o) == 4:
            return pl.BlockSpec((o[2], LANES, tm), lambda i: (0, 0, i)), (o[2], LANES, rows)
        if len(o) == 3:
            return pl.BlockSpec((o[2], tm, LANES), lambda i: (0, i, 0)), (o[2], rows, LANES)
        return pl.BlockSpec((tm, o[0]), lambda i: (i, 0)), (rows, o[0])

    out_specs = [out_spec(o)[0] for o in outs]
    out_shape = [jax.ShapeDtypeStruct(out_spec(o)[1], o[1]) for o in outs]
    out_specs += [pl.BlockSpec((r, w), lambda i: (0, 0)) for r, w in reds]
    out_shape += [jax.ShapeDtypeStruct((r, w), F32) for r, w in reds]
    return pl.pallas_call(body, name=name, grid=(rows // tm,), in_specs=in_specs, out_specs=out_specs,
                          out_shape=out_shape, compiler_params=_params(("arbitrary",)))(*args)


def _rstd(x, n):
    return lax.rsqrt(jnp.sum(x * x, axis=-1, keepdims=True) * (1.0 / n) + RMS_EPS)


def _rms_fwd(x_spec, g, *, name, rows, width):
    def fn(x, gv):
        return (x * _rstd(x, width) * gv,)
    return _rowwise(fn, name=name, rows=rows, ins=[x_spec, ("full", g)], outs=[(width, BF16)])[0]


def _rms_bwd(x_spec, g, dy, res, *, name, rows, width, out_dtypes):
    def fn(x, gv, dyv, *rest):
        r = _rstd(x, width)
        dyg = dyv * gv
        dx = r * dyg - x * (r * r * r) * (jnp.sum(dyg * x, axis=-1, keepdims=True) * (1.0 / width))
        if rest:
            dx = dx + rest[0]
        return (dx,) * len(out_dtypes) + (jnp.sum(dyv * x * r, axis=0, keepdims=True),)

    ins = [x_spec, ("full", g), ("row", dy, width, 0)]
    if res is not None:
        ins.append(("row", res, width, 0))
    return _rowwise(fn, name=name, rows=rows, ins=ins, outs=[(width, dt) for dt in out_dtypes], reds=[(1, width)])


NT_DIMS = (((1,), (1,)), ((), ()))
TN_DIMS = (((0,), (0,)), ((), ()))
BNT_DIMS = (((2,), (2,)), ((0,), (0,)))
BNN_DIMS = (((2,), (1,)), ((0,), (0,)))


def _mla_tile(s):
    return min(512, s)


def _block_tables(pairs):
    return jnp.asarray([a for a, _ in pairs], jnp.int32), jnp.asarray([b for _, b in pairs], jnp.int32)


def _causal(shape, query_axis):
    return lax.broadcasted_iota(jnp.int32, shape, query_axis) >= lax.broadcasted_iota(jnp.int32, shape, 1 - query_axis)


def _mla_fwd(qa, ka, vta, *, name, s, nh):
    t = _mla_tile(s)
    nq = s // t

    pairs = [(i, j) for i in range(nq) for j in range(i + 1)]

    def body(qb_ref, kb_ref, q_ref, k_ref, vt_ref, o_ref, lse_ref, m_s, acc_s):
        i, j = qb_ref[pl.program_id(0)], kb_ref[pl.program_id(0)]

        @pl.when(j == 0)
        def _():
            m_s[...] = jnp.full_like(m_s, NEG)
            acc_s[...] = jnp.zeros_like(acc_s)

        def step(diag):
            sc = lax.dot_general(k_ref[...], q_ref[...], BNT_DIMS, preferred_element_type=F32)
            if diag:
                sc = jnp.where(_causal(sc.shape[1:], 1)[None], sc, NEG)
            m_prev = m_s[...]
            m_new = jnp.maximum(m_prev, jnp.max(sc, axis=1, keepdims=True))
            p = jnp.exp(sc - m_new).astype(BF16)
            acc_s[...] = (jnp.exp(m_prev - m_new) * acc_s[...]
                          + lax.dot_general(vt_ref[...], p, BNN_DIMS, preferred_element_type=F32))
            m_s[...] = m_new

        pl.when(j < i)(functools.partial(step, False))
        pl.when(j == i)(functools.partial(step, True))

        @pl.when(j == i)
        def _():
            row = lax.broadcasted_iota(jnp.int32, (LANES, t), 0)
            for h in range(nh):
                acc = acc_s[h]
                l = acc[HEAD:HEAD + 1, :]
                o_ref[h] = jnp.where(row < HEAD, acc / l, 0.0).T
                lse_ref[h] = m_s[h] + jnp.log(l)

    q_spec = pl.BlockSpec((nh, t, LANES), lambda p, qb, kb: (0, qb[p], 0))
    k_spec = pl.BlockSpec((nh, t, LANES), lambda p, qb, kb: (0, kb[p], 0))
    vt_spec = pl.BlockSpec((nh, LANES, t), lambda p, qb, kb: (0, 0, kb[p]))
    return pl.pallas_call(
        body, name=name,
        grid_spec=pltpu.PrefetchScalarGridSpec(
            num_scalar_prefetch=2, grid=(len(pairs),), in_specs=[q_spec, k_spec, vt_spec],
            out_specs=[q_spec, pl.BlockSpec((nh, 1, t), lambda p, qb, kb: (0, 0, qb[p]))],
            scratch_shapes=[pltpu.VMEM((nh, 1, t), F32), pltpu.VMEM((nh, LANES, t), F32)]),
        out_shape=[jax.ShapeDtypeStruct((nh, s, LANES), F32), jax.ShapeDtypeStruct((nh, 1, s), F32)],
        compiler_params=_params(("arbitrary",)),
    )(*_block_tables(pairs), qa, ka, vta)


def _mla_bwd(qa, ka, kta, va, o, do, lse_row, after, *, name, s, nh):
    t = _mla_tile(s)
    nq = s // t

    pairs = [(kb, j) for kb in range(nq) for j in range(kb, nq)]

    def body(kb_ref, qb_ref, q_ref, k_ref, kt_ref, v_ref, o_ref, do_ref, lse_ref, after_ref, dq_hbm, dk_ref, dv_ref,
             dqt_s, dk_s, dv_s, d_s, stage, sem):
        kb, j = kb_ref[pl.program_id(0)], qb_ref[pl.program_id(0)]
        cols = pl.ds(pl.multiple_of(j * t, t), t)

        @pl.when(j == kb)
        def _():
            dk_s[...] = jnp.zeros_like(dk_s)
            dv_s[...] = jnp.zeros_like(dv_s)

        @pl.when(kb == 0)
        def _():
            dqt_s[:, :, cols] = jnp.zeros((nh, LANES, t), F32)
            for h in range(nh):
                d_col = jnp.sum(do_ref[h] * o_ref[h], axis=1, keepdims=True)
                d_s[h, :, cols] = jnp.broadcast_to(d_col, (t, LANES)).T[0:1, :]

        def step(diag):
            q = q_ref[...]
            do_b = do_ref[...].astype(BF16)
            sc = lax.dot_general(k_ref[...], q, BNT_DIMS, preferred_element_type=F32)
            if diag:
                sc = jnp.where(_causal(sc.shape[1:], 1)[None], sc, NEG)
            p = jnp.exp(sc - lse_ref[...])
            dv_s[...] += lax.dot_general(p.astype(BF16), do_b, BNN_DIMS, preferred_element_type=F32)
            dp = lax.dot_general(v_ref[...], do_b, BNT_DIMS, preferred_element_type=F32)
            ds = (p * (dp - d_s[:, :, cols])).astype(BF16)
            dk_s[...] += lax.dot_general(ds, q, BNN_DIMS, preferred_element_type=F32)
            dqt_s[:, :, cols] += lax.dot_general(kt_ref[...], ds, BNN_DIMS, preferred_element_type=F32)

        pl.when(j > kb)(functools.partial(step, False))
        pl.when(j == kb)(functools.partial(step, True))

        @pl.when(j == kb)
        def _():
            for h in range(nh):
                stage[h] = dqt_s[h, :, cols].T
            out = pltpu.make_async_copy(stage, dq_hbm.at[:, cols, :], sem)
            out.start()
            out.wait()

        @pl.when(j == nq - 1)
        def _():
            dk_ref[...] = dk_s[...]
            dv_ref[...] = dv_s[...].astype(dv_ref.dtype)

    q_spec = pl.BlockSpec((nh, t, LANES), lambda p, kb, qb: (0, qb[p], 0))
    kv_spec = pl.BlockSpec((nh, t, LANES), lambda p, kb, qb: (0, kb[p], 0))
    kt_spec = pl.BlockSpec((nh, LANES, t), lambda p, kb, qb: (0, 0, kb[p]))
    row_spec = pl.BlockSpec((nh, 1, t), lambda p, kb, qb: (0, 0, qb[p]))
    whole = jax.ShapeDtypeStruct((nh, s, LANES), F32)
    return pl.pallas_call(
        body, name=name,
        grid_spec=pltpu.PrefetchScalarGridSpec(
            num_scalar_prefetch=2, grid=(len(pairs),),
            in_specs=[q_spec, kv_spec, kt_spec, kv_spec, q_spec, q_spec, row_spec, pl.BlockSpec(memory_space=pl.ANY)],
            out_specs=[pl.BlockSpec(memory_space=pl.ANY), kv_spec, kv_spec],
            scratch_shapes=[pltpu.VMEM((nh, LANES, s), F32), pltpu.VMEM((nh, t, LANES), F32),
                            pltpu.VMEM((nh, t, LANES), F32), pltpu.VMEM((nh, 1, s), F32), pltpu.VMEM((nh, t, LANES), F32),
                            pltpu.SemaphoreType.DMA]),
        out_shape=[whole, whole, jax.ShapeDtypeStruct((nh, s, LANES), BF16)],
        compiler_params=_params(("arbitrary",)),
    )(*_block_tables(pairs), qa, ka, kta, va, o, do, lse_row, after)


SWA_PIECE = 128
SWA_KEYS = 2 * SWA_PIECE


def _swa_block(s):
    return min(512, s)


def _swa_piece(hp_ref, h, q, k_ref, v_ref, qpos0, scale):
    kstart = pl.multiple_of(jnp.maximum(qpos0 - SWA_PIECE, 0), SWA_PIECE)
    k = k_ref[pl.ds(kstart, SWA_KEYS), :].astype(BF16)
    v = v_ref[pl.ds(kstart, SWA_KEYS), :].astype(BF16)
    sc = lax.dot_general(q, k, NT_DIMS, preferred_element_type=F32)
    dist = (qpos0 + lax.broadcasted_iota(jnp.int32, sc.shape, 0)) - (kstart + lax.broadcasted_iota(jnp.int32, sc.shape, 1))
    sc = sc * scale - hp_ref[h, 0] * dist.astype(F32)
    sc = jnp.where((dist >= 0) & (dist < SWA_WINDOW), sc, NEG)
    return kstart, k, v, sc


def _swa_fwd(proj, hp, *, name, s, scale):
    tb = _swa_block(s)
    group = SWA_HEADS // SWA_KV_HEADS
    q_off, k_off, v_off = C_QSW[0] // LANES, C_KSW[0] // LANES, C_VSW[0] // LANES

    def body(hp_ref, q_ref, k_ref, v_ref, o_ref, lse_ref):
        h, i = pl.program_id(0), pl.program_id(1)
        sink = hp_ref[h, 1]
        for r in range(0, tb, SWA_PIECE):
            rows = pl.ds(r, SWA_PIECE)
            _, _, v, sc = _swa_piece(hp_ref, h, q_ref[rows, :].astype(BF16), k_ref, v_ref, i * tb + r, scale)
            m = jnp.maximum(jnp.max(sc, axis=1, keepdims=True), sink)
            p = jnp.exp(sc - m)
            l = jnp.sum(p, axis=1, keepdims=True) + jnp.exp(sink - m)
            o_ref[rows, :] = jnp.dot(p.astype(BF16), v, preferred_element_type=F32) / l
            lse_ref[rows, :] = m + jnp.log(l)

    whole = lambda off: pl.BlockSpec((s, LANES), lambda h, i: (0, off + h // group))
    return pl.pallas_call(
        body, name=name, grid=(SWA_HEADS, s // tb),
        in_specs=[pl.BlockSpec(memory_space=pltpu.SMEM), pl.BlockSpec((tb, LANES), lambda h, i: (i, q_off + h)),
                  whole(k_off), whole(v_off)],
        out_specs=[pl.BlockSpec((tb, LANES), lambda h, i: (i, h)), pl.BlockSpec((None, tb, 1), lambda h, i: (h, i, 0))],
        out_shape=[jax.ShapeDtypeStruct((s, SWA_HEADS * LANES), F32), jax.ShapeDtypeStruct((SWA_HEADS, s, 1), F32)],
        compiler_params=_params(("parallel", "parallel")),
    )(hp, proj, proj, proj)


def _swa_bwd(proj, o, do, lse, hp, *, name, s, scale):
    tb = _swa_block(s)
    nqb = s // tb
    group = SWA_HEADS // SWA_KV_HEADS
    q_off, k_off, v_off = C_QSW[0] // LANES, C_KSW[0] // LANES, C_VSW[0] // LANES

    def body(hp_ref, q_ref, k_ref, v_ref, o_ref, do_ref, lse_ref, dq_ref, dk_ref, dv_ref, dsink_ref):
        kh, g, i = pl.program_id(0), pl.program_id(1), pl.program_id(2)
        h = kh * group + g
        sink = hp_ref[h, 1]

        @pl.when((g == 0) & (i == 0))
        def _():
            dk_ref[...] = jnp.zeros_like(dk_ref)
            dv_ref[...] = jnp.zeros_like(dv_ref)

        @pl.when(i == 0)
        def _():
            dsink_ref[...] = jnp.zeros_like(dsink_ref)

        for r in range(0, tb, SWA_PIECE):
            rows = pl.ds(r, SWA_PIECE)
            q = q_ref[rows, :].astype(BF16)
            dov = do_ref[rows, :]
            do_b = dov.astype(BF16)
            lse_r = lse_ref[rows, :]
            d_r = jnp.sum(dov * o_ref[rows, :], axis=1, keepdims=True)
            kstart, k, v, sc = _swa_piece(hp_ref, h, q, k_ref, v_ref, i * tb + r, scale)
            p = jnp.exp(sc - lse_r)
            dp = lax.dot_general(do_b, v, NT_DIMS, preferred_element_type=F32)
            ds = (p * (dp - d_r)).astype(BF16)
            dq_ref[rows, :] = (jnp.dot(ds, k, preferred_element_type=F32) * scale).astype(dq_ref.dtype)
            win = pl.ds(kstart, SWA_KEYS)
            dk_ref[win, :] += lax.dot_general(ds, q, TN_DIMS, preferred_element_type=F32) * scale
            dv_ref[win, :] += lax.dot_general(p.astype(BF16), do_b, TN_DIMS, preferred_element_type=F32)
            part = jnp.sum(-jnp.exp(sink - lse_r) * d_r, axis=0, keepdims=True)
            dsink_ref[...] += jnp.broadcast_to(part, (1, LANES))

    whole = lambda off: pl.BlockSpec((s, LANES), lambda kh, g, i: (0, off + kh))
    q_map = lambda kh, g, i: (i, kh * group + g)
    return pl.pallas_call(
        body, name=name, grid=(SWA_KV_HEADS, group, nqb),
        in_specs=[pl.BlockSpec(memory_space=pltpu.SMEM),
                  pl.BlockSpec((tb, LANES), lambda kh, g, i: (i, q_off + kh * group + g)), whole(k_off), whole(v_off),
                  pl.BlockSpec((tb, LANES), q_map), pl.BlockSpec((tb, LANES), q_map),
                  pl.BlockSpec((None, tb, 1), lambda kh, g, i: (kh * group + g, i, 0))],
        out_specs=[pl.BlockSpec((tb, LANES), q_map), whole(0), whole(0),
                   pl.BlockSpec((None, 1, LANES), lambda kh, g, i: (kh * group + g, 0, 0))],
        out_shape=[jax.ShapeDtypeStruct((s, SWA_HEADS * LANES), BF16),
                   jax.ShapeDtypeStruct((s, SWA_KV_HEADS * LANES), F32), jax.ShapeDtypeStruct((s, SWA_KV_HEADS * LANES), F32),
                   jax.ShapeDtypeStruct((SWA_HEADS, 1, LANES), F32)],
        compiler_params=_params(("parallel", "arbitrary", "arbitrary")),
    )(hp, proj, proj, proj, o, do, lse)


def _shift_down(z, k):
    rows = lax.broadcasted_iota(jnp.int32, z.shape, 0)
    return jnp.where(rows >= k, pltpu.roll(z, k, 0), 0.0)


def _shift_up(z, k):
    n = z.shape[0]
    rows = lax.broadcasted_iota(jnp.int32, z.shape, 0)
    return jnp.where(rows < n - k, pltpu.roll(z, n - k, 0), 0.0)


def _rows3(a, b, c):
    r = lax.broadcasted_iota(jnp.int32, (3, a.shape[1]), 0)
    return jnp.where(r == 0, a, jnp.where(r == 1, b, c))


def _col_spec(s, off):
    return pl.BlockSpec((s, LANES), functools.partial(lambda j, off: (0, off + j), off=off))


def _conv_fwd(proj, conv_w, *, name, s):
    def body(gb_ref, gc_ref, u_ref, w_ref, y_ref):
        w0, w1, w2 = w_ref[0:1, :], w_ref[1:2, :], w_ref[2:3, :]
        z = gc_ref[...] * u_ref[...]
        c = w2 * z + w1 * _shift_down(z, 1) + w0 * _shift_down(z, 2)
        y_ref[...] = gb_ref[...] * c

    return pl.pallas_call(
        body, name=name, grid=(2,),
        in_specs=[_col_spec(s, C_GB[0] // LANES), _col_spec(s, C_GC[0] // LANES), _col_spec(s, C_UCONV[0] // LANES),
                  pl.BlockSpec((3, LANES), lambda j: (0, j))],
        out_specs=_col_spec(s, 0), out_shape=jax.ShapeDtypeStruct((s, D_GROUP), F32),
        compiler_params=_params(("parallel",)),
    )(proj, proj, proj, conv_w)


def _conv_bwd(dy, proj, conv_w, *, name, s):
    def body(dy_ref, gb_ref, gc_ref, u_ref, w_ref, dgb_ref, dgc_ref, du_ref, dw_ref):
        w0, w1, w2 = w_ref[0:1, :], w_ref[1:2, :], w_ref[2:3, :]
        gc, u, dyv = gc_ref[...], u_ref[...], dy_ref[...]
        z = gc * u
        z1, z2 = _shift_down(z, 1), _shift_down(z, 2)
        c = w2 * z + w1 * z1 + w0 * z2
        dgb_ref[...] = (dyv * c).astype(dgb_ref.dtype)
        dc = dyv * gb_ref[...]
        dz = w2 * dc + w1 * _shift_up(dc, 1) + w0 * _shift_up(dc, 2)
        dgc_ref[...] = (dz * u).astype(dgc_ref.dtype)
        du_ref[...] = (dz * gc).astype(du_ref.dtype)
        dw_ref[...] = _rows3(jnp.sum(dc * z2, axis=0, keepdims=True), jnp.sum(dc * z1, axis=0, keepdims=True),
                             jnp.sum(dc * z, axis=0, keepdims=True))

    act = jax.ShapeDtypeStruct((s, D_GROUP), BF16)
    return pl.pallas_call(
        body, name=name, grid=(2,),
        in_specs=[_col_spec(s, 0), _col_spec(s, C_GB[0] // LANES), _col_spec(s, C_GC[0] // LANES),
                  _col_spec(s, C_UCONV[0] // LANES), pl.BlockSpec((3, LANES), lambda j: (0, j))],
        out_specs=[_col_spec(s, 0), _col_spec(s, 0), _col_spec(s, 0), pl.BlockSpec((3, LANES), lambda j: (0, j))],
        out_shape=[act, act, act, jax.ShapeDtypeStruct((3, D_GROUP), F32)],
        compiler_params=_params(("parallel",)),
    )(dy, proj, proj, proj, conv_w)


def _pool_select(j, lane, a2, a4, a8, a16):
    lo = lane < HEAD
    return jnp.where(j == 0, jnp.where(lo, a2, a4), jnp.where(lo, a8, a16))


def _pooled(u, j):
    s2 = u + _shift_down(u, 1)
    s4 = s2 + _shift_down(s2, 2)
    s8 = s4 + _shift_down(s4, 4)
    s16 = s8 + _shift_down(s8, 8)
    lane = lax.broadcasted_iota(jnp.int32, u.shape, 1)
    rows = lax.broadcasted_iota(jnp.int32, u.shape, 0)
    win = _pool_select(j, lane, *POOL_WINDOWS)
    count = jnp.minimum(rows + 1, win).astype(F32)
    return _pool_select(j, lane, s2, s4, s8, s16) / count - u, count


def _pool_fwd(proj, wbd, scale, *, name, s):
    def body(u_ref, w_ref, sc_ref, y_ref):
        pooled, _ = _pooled(u_ref[...], pl.program_id(0))
        y_ref[...] = jnp.dot(pooled.astype(BF16), w_ref[...].astype(BF16), preferred_element_type=F32) * sc_ref[...]

    return pl.pallas_call(
        body, name=name, grid=(2,),
        in_specs=[_col_spec(s, C_UPOOL[0] // LANES), pl.BlockSpec((None, LANES, LANES), lambda j: (j, 0, 0)),
                  pl.BlockSpec((1, LANES), lambda j: (0, j))],
        out_specs=_col_spec(s, 0), out_shape=jax.ShapeDtypeStruct((s, D_GROUP), F32),
        compiler_params=_params(("parallel",)),
    )(proj, wbd, scale)


def _pool_bwd(dy, proj, wbd, scale, *, name, s):
    def body(dy_ref, u_ref, w_ref, sc_ref, du_ref, dw_ref, dsc_ref):
        j = pl.program_id(0)
        pooled, count = _pooled(u_ref[...], j)
        pooled_b = pooled.astype(BF16)
        w_b = w_ref[...].astype(BF16)
        dyv = dy_ref[...]
        mixed = jnp.dot(pooled_b, w_b, preferred_element_type=F32)
        dsc_ref[...] = jnp.sum(dyv * mixed, axis=0, keepdims=True)
        dms = (dyv * sc_ref[...]).astype(BF16)
        dw_ref[...] = lax.dot_general(pooled_b, dms, (((0,), (0,)), ((), ())), preferred_element_type=F32)
        dpooled = lax.dot_general(dms, w_b, (((1,), (1,)), ((), ())), preferred_element_type=F32)
        r = dpooled / count
        a2 = r + _shift_up(r, 1)
        a4 = a2 + _shift_up(a2, 2)
        a8 = a4 + _shift_up(a4, 4)
        a16 = a8 + _shift_up(a8, 8)
        lane = lax.broadcasted_iota(jnp.int32, r.shape, 1)
        du_ref[...] = (_pool_select(j, lane, a2, a4, a8, a16) - dpooled).astype(du_ref.dtype)

    return pl.pallas_call(
        body, name=name, grid=(2,),
        in_specs=[_col_spec(s, 0), _col_spec(s, C_UPOOL[0] // LANES),
                  pl.BlockSpec((None, LANES, LANES), lambda j: (j, 0, 0)), pl.BlockSpec((1, LANES), lambda j: (0, j))],
        out_specs=[_col_spec(s, 0), pl.BlockSpec((None, LANES, LANES), lambda j: (j, 0, 0)),
                   pl.BlockSpec((1, LANES), lambda j: (0, j))],
        out_shape=[jax.ShapeDtypeStruct((s, D_GROUP), BF16), jax.ShapeDtypeStruct((2, LANES, LANES), F32),
                   jax.ShapeDtypeStruct((1, D_GROUP), F32)],
        compiler_params=_params(("parallel",)),
    )(dy, proj, wbd, scale)


def _mesh_pos():
    return lax.axis_index("x"), lax.axis_index("y"), lax.axis_index("c")


def _any_specs(n):
    return [pl.BlockSpec(memory_space=pl.ANY)] * n


def _all_gather(xs, *, name):
    n = len(xs)

    def body(*refs):
        x_refs, out_refs = refs[:n], refs[n:2 * n]
        send_sems, recv_sems, local_sems = refs[2 * n:]
        x, y, cc = _mesh_pos()
        me, sibling = (x, y, cc), (x, y, 1 - cc)
        chips = [(1 - x, y), (x, 1 - y), (1 - x, 1 - y)]

        def slot(a, px, py, pc):
            return out_refs[a].at[4 * px + 2 * py + pc]

        def copy(a, k, block, to, src=None):
            return pltpu.make_async_remote_copy(
                src_ref=slot(a, *block) if src is None else src, dst_ref=slot(a, *block), send_sem=send_sems.at[a, k],
                recv_sem=recv_sems.at[a, k], device_id=to, device_id_type=pl.DeviceIdType.MESH)

        mine = [pltpu.make_async_copy(x_refs[a], slot(a, *me), local_sems.at[a]) for a in range(n)]
        first = []
        for a in range(n):
            first.append(copy(a, 0, me, sibling, src=x_refs[a]))
            first += [copy(a, 1 + j, me, (*chip, cc), src=x_refs[a]) for j, chip in enumerate(chips)]
        for cp in mine + first:
            cp.start()
        passed = []
        for j, chip in enumerate(chips):
            for a in range(n):
                copy(a, 1 + j, (*chip, cc), me).wait_recv()
                passed.append(copy(a, 4 + j, (*chip, cc), sibling))
                passed[-1].start()
        for a in range(n):
            copy(a, 0, sibling, me).wait_recv()
        for j, chip in enumerate(chips):
            for a in range(n):
                copy(a, 4 + j, (*chip, 1 - cc), me).wait_recv()
        for cp in first + passed:
            cp.wait_send()
        for cp in mine:
            cp.wait()

    return pl.pallas_call(
        body, name=name, out_shape=[jax.ShapeDtypeStruct((N_DEV,) + a.shape, a.dtype) for a in xs],
        in_specs=_any_specs(n), out_specs=_any_specs(n),
        scratch_shapes=[pltpu.SemaphoreType.DMA((n, 7)), pltpu.SemaphoreType.DMA((n, 7)), pltpu.SemaphoreType.DMA((n,))],
    )(*xs)


def _plan_gather_near(src_refs, land_refs, send_sems, recv_sems):
    x, y, cc = _mesh_pos()
    me = 4 * x + 2 * y + cc
    plan = []
    for a, (src, land) in enumerate(zip(src_refs, land_refs)):
        for k, (px, py, pc) in enumerate([(x, y, 1 - cc), (1 - x, y, cc), (x, 1 - y, cc), (1 - x, 1 - y, cc)]):
            sems = dict(send_sem=send_sems.at[4 * a + k], recv_sem=recv_sems.at[4 * a + k], device_id=(px, py, pc),
                        device_id_type=pl.DeviceIdType.MESH)
            plan.append((pltpu.make_async_remote_copy(src_ref=src, dst_ref=land.at[me], **sems),
                         pltpu.make_async_remote_copy(src_ref=src, dst_ref=land.at[4 * px + 2 * py + pc], **sems)))
    return plan


def _plan_gather_pass(src_refs, land_refs, send_sems, recv_sems):
    x, y, cc = _mesh_pos()
    plan = []
    for a, land in enumerate(land_refs):
        for j, (px, py) in enumerate([(1 - x, y), (x, 1 - y), (1 - x, 1 - y)]):
            mine, theirs = land.at[4 * px + 2 * py + cc], land.at[4 * px + 2 * py + 1 - cc]
            sems = dict(send_sem=send_sems.at[3 * a + j], recv_sem=recv_sems.at[3 * a + j], device_id=(x, y, 1 - cc),
                        device_id_type=pl.DeviceIdType.MESH)
            plan.append((pltpu.make_async_remote_copy(src_ref=mine, dst_ref=mine, **sems),
                         pltpu.make_async_remote_copy(src_ref=mine, dst_ref=theirs, **sems)))
    return plan


def _plan_sibling(src_refs, land_refs, send_sems, recv_sems):
    x, y, cc = _mesh_pos()
    plan = []
    for a, (src, land) in enumerate(zip(src_refs, land_refs)):
        cp = pltpu.make_async_remote_copy(
            src_ref=src.at[1 - cc], dst_ref=land, send_sem=send_sems.at[a], recv_sem=recv_sems.at[a],
            device_id=(x, y, 1 - cc), device_id_type=pl.DeviceIdType.MESH)
        plan.append((cp, cp))
    return plan


def _plan_chips(src_refs, land_refs, send_sems, recv_sems):
    x, y, cc = _mesh_pos()
    my_chip = 2 * x + y
    plan = []
    for a, (src, land) in enumerate(zip(src_refs, land_refs)):
        for j, (px, py) in enumerate([(1 - x, y), (x, 1 - y), (1 - x, 1 - y)]):
            peer = 2 * px + py
            sems = dict(send_sem=send_sems.at[3 * a + j], recv_sem=recv_sems.at[3 * a + j], device_id=(px, py, cc),
                        device_id_type=pl.DeviceIdType.MESH)
            plan.append((pltpu.make_async_remote_copy(src_ref=src.at[peer], dst_ref=land.at[my_chip], **sems),
                         pltpu.make_async_remote_copy(src_ref=src.at[peer], dst_ref=land.at[peer], **sems)))
    return plan


HBM_SPEC = pl.BlockSpec(memory_space=pltpu.HBM)
SEM_SPEC = pl.BlockSpec(memory_space=pltpu.SEMAPHORE)
ANY_SPEC = pl.BlockSpec(memory_space=pl.ANY)
SIDE_EFFECT = pltpu.CompilerParams(has_side_effects=pltpu.SideEffectType.DATAFLOW_SIDE_EFFECTING)


def _start_copies(plan, sems_per_array, srcs, lands, after, *, name):
    lands = [lax.empty(l, a.dtype) if isinstance(l, tuple) else l for l, a in zip(lands, srcs or lands)]
    ns, n = len(srcs), len(srcs) + len(lands)

    def body(*refs):
        send_sems, recv_sems = refs[n + len(after)], refs[n + len(after) + 1]
        for out, _ in plan(refs[:ns], refs[ns:n], send_sems, recv_sems):
            out.start()
        refs[-1][...] = jnp.zeros_like(refs[-1])

    sem = pltpu.SemaphoreType.DMA((len(lands) * sems_per_array,))
    res = pl.pallas_call(
        body, name=name,
        out_shape=(sem, sem, *[pltpu.HBM(a.shape, a.dtype) for a in srcs + lands], jax.ShapeDtypeStruct((8, LANES), F32)),
        in_specs=[HBM_SPEC] * n + [ANY_SPEC] * len(after),
        out_specs=(SEM_SPEC, SEM_SPEC, *[HBM_SPEC] * n, pl.BlockSpec(memory_space=pltpu.VMEM)),
        input_output_aliases={i: 2 + i for i in range(n)}, compiler_params=SIDE_EFFECT,
    )(*[pltpu.with_memory_space_constraint(a, pltpu.HBM) for a in srcs + lands], *after)
    return (res[0], res[1], list(res[2:2 + ns]), list(res[2 + ns:2 + n])), res[-1]


def _wait_copies(plan, handle, after, *, name):
    send, recv, srcs, lands = handle
    ns, n = len(srcs), len(srcs) + len(lands)

    def body(*refs):
        for out, inc in plan(refs[:ns], refs[ns:n], refs[n], refs[n + 1]):
            out.wait_send()
            inc.wait_recv()

    res = pl.pallas_call(
        body, name=name, out_shape=tuple(pltpu.HBM(a.shape, a.dtype) for a in srcs + lands),
        in_specs=[HBM_SPEC] * n + [SEM_SPEC, SEM_SPEC] + [ANY_SPEC] * len(after), out_specs=[HBM_SPEC] * n,
        input_output_aliases={i: i for i in range(n)}, compiler_params=SIDE_EFFECT,
    )(*srcs, *lands, send, recv, *after)
    return list(res[:ns]), list(res[ns:])


def _row_tile(rows, target=512):
    if rows <= target:
        return rows
    best = None
    for t in range(8, target + 1, 8):
        if rows % t == 0:
            best = t
    assert best is not None, (rows, target)
    return best


def _add_own(g, other, core, *, name):
    _, _, rows, cols = g.shape
    tm = _row_tile(rows)

    def body(c_ref, g_ref, o_ref, out_ref):
        out_ref[...] = g_ref[...] + o_ref[...]

    return pl.pallas_call(
        body, name=name, out_shape=jax.ShapeDtypeStruct(other.shape, other.dtype),
        grid_spec=pltpu.PrefetchScalarGridSpec(
            num_scalar_prefetch=1, grid=(4, rows // tm),
            in_specs=[pl.BlockSpec((None, None, tm, cols), lambda p, i, c_ref: (c_ref[0], p, i, 0)),
                      pl.BlockSpec((None, tm, cols), lambda p, i, c_ref: (p, i, 0))],
            out_specs=pl.BlockSpec((None, tm, cols), lambda p, i, c_ref: (p, i, 0))),
        compiler_params=_params(("parallel", "parallel")),
    )(core, g, other)


def _adamw(parts, w, m, v, after, *, name):
    layers, rows, cols = w.shape
    assert len(parts) == layers
    tm = _row_tile(rows, 256)
    nr = rows // tm

    def body(*refs):
        p_refs = refs[:layers]
        w_ref, m_ref, v_ref, _, g_ref, d_ref, nm_ref, nv_ref, g_s = refs[layers:]
        for ll in range(layers):
            @pl.when(pl.program_id(0) == ll)
            def _(ll=ll):
                g = p_refs[ll][0]
                for q in range(1, p_refs[ll].shape[0]):
                    g = g + p_refs[ll][q]
                g_s[...] = g

        g = g_s[...]
        mm = ADAM_B1 * m_ref[...] + (1.0 - ADAM_B1) * g
        vv = ADAM_B2 * v_ref[...] + (1.0 - ADAM_B2) * jnp.square(g)
        m_hat = mm / (1.0 - ADAM_B1 ** ADAM_STEP)
        v_hat = vv / (1.0 - ADAM_B2 ** ADAM_STEP)
        g_ref[...] = g
        d_ref[...] = -ADAM_LR * (m_hat / (jnp.sqrt(v_hat) + ADAM_EPS) + ADAM_WD * w_ref[...])
        nm_ref[...] = mm
        nv_ref[...] = vv

    def part_spec(ll, p):
        return pl.BlockSpec((p, tm, cols), lambda l, i: (0, jnp.where(l == ll, i, jnp.where(l < ll, 0, nr - 1)), 0))

    spec = pl.BlockSpec((None, tm, cols), lambda l, i: (l, i, 0))
    out = jax.ShapeDtypeStruct(w.shape, F32)
    return pl.pallas_call(
        body, name=name, grid=(layers, nr),
        in_specs=[part_spec(ll, parts[ll].shape[0]) for ll in range(layers)] + [spec] * 3 + [pl.BlockSpec(memory_space=pl.ANY)],
        out_specs=[spec] * 4, out_shape=[out] * 4, scratch_shapes=[pltpu.VMEM((tm, cols), F32)],
        compiler_params=_params(("arbitrary", "arbitrary")),
    )(*parts, w, m, v, after)


def _pack(arrs):
    flat = jnp.concatenate([a.reshape(-1) for a in arrs])
    rows = -(-flat.shape[0] // (PACK_COLS * 16)) * 16
    return jnp.pad(flat, (0, rows * PACK_COLS - flat.shape[0])).reshape(rows, PACK_COLS)


def _unpack(packed, shapes):
    flat = packed.reshape(-1)
    out, off = [], 0
    for shp in shapes:
        n = int(np.prod(shp))
        out.append(flat[off:off + n].reshape(shp))
        off += n
    return out


def _shards_to_full(g, axis):
    if axis == 0:
        return g.reshape(g.shape[0] * g.shape[1], g.shape[2])
    return jnp.transpose(g, (1, 0, 2)).reshape(g.shape[1], g.shape[0] * g.shape[2])


def _full_to_shards(a, axis):
    if axis == 0:
        return jnp.transpose(a.reshape(4, 2, a.shape[0] // N_DEV, a.shape[1]), (1, 0, 2, 3))
    return jnp.transpose(a.reshape(a.shape[0], 4, 2, a.shape[1] // N_DEV), (2, 1, 0, 3))


def _zeros_like_cols(a, n):
    return jnp.zeros(a.shape[:-1] + (n,), a.dtype)


def _pad_heads(a, n):
    z = _zeros_like_cols(a, HEAD)
    return jnp.concatenate([p for h in range(n) for p in (a[..., h * HEAD:(h + 1) * HEAD], z)], axis=-1)


def _unpad_heads(a, n):
    return jnp.concatenate([a[..., h * LANES:h * LANES + HEAD] for h in range(n)], axis=-1)


def _seg(first, width, sign=1):
    return (width, [(first, sign)])


def _zero(width):
    return (width, [])


def _swapped(first):
    half = MLA_ROPE // 2
    return [_seg(first + half, half, -1), _seg(first, half)]


def _padded_heads(first, n):
    return [s for h in range(n) for s in (_seg(first + HEAD * h, HEAD), _zero(HEAD))]


def _layout_w_in():
    kr = 384
    return (_padded_heads(1440, 4) + [_seg(0, 256), _seg(416, 256), _seg(672, 256), _seg(928, 256), _seg(1184, 256)]
            + _padded_heads(1696, 2) + _padded_heads(1824, 2) + [_seg(256, 128)]
            + [_zero(HEAD), _seg(kr, MLA_ROPE), _seg(kr, MLA_ROPE)] + [_zero(HEAD)] + _swapped(kr) + _swapped(kr))


def _layout_w_uq():
    out = []
    for h in range(MLA_HEADS):
        out += [_seg(96 * h, MLA_NOPE), _seg(96 * h + MLA_NOPE, MLA_ROPE)] + _swapped(96 * h + MLA_NOPE)
    return out


def _layout_w_ukv():
    keys = [s for h in range(MLA_HEADS) for s in (_seg(LANES * h, HEAD), _zero(HEAD))]
    values = [s for h in range(MLA_HEADS) for s in (_seg(LANES * h + HEAD, HEAD), _zero(HEAD))]
    return keys + values


def _layout_w_gate_up():
    return [_seg(half + j, GU_TILE) for j in range(0, D_FF, GU_TILE) for half in (0, D_FF)]


LAYOUTS = dict(w_in=_layout_w_in(), w_uq=_layout_w_uq(), w_ukv=_layout_w_ukv(), w_gate_up=_layout_w_gate_up())
SWAPPED_UPDATE = ("w_gate_up",)
OWN_COLS = dict(w_in=1952, w_uq=384, w_ukv=512, w_gate_up=2 * D_FF)


def _plan_extend(layout, shard):
    plan = []
    for width, terms in layout:
        if not terms:
            plan.append((width, []))
            continue
        (first, sign), = terms
        while width:
            g, off = divmod(first, shard)
            w = min(width, shard - off)
            plan.append((w, [(g, off, sign)]))
            first, width = first + w, width - w
    return [plan]


def _plan_fold(layout, own_cols):
    sources = [[] for _ in range(own_cols)]
    e = 0
    for width, terms in layout:
        for first, sign in terms:
            for i in range(width):
                sources[first + i].append((e + i, sign))
        e += width
    shard = own_cols // N_DEV
    plans = {}
    for g in range(N_DEV):
        plan, n = [], g * shard
        while n < (g + 1) * shard:
            w = 1
            while n + w < (g + 1) * shard and [(c + w, sg) for c, sg in sources[n]] == sources[n + w]:
                w += 1
            plan.append((w, [(0, c, sg) for c, sg in sources[n]]))
            n += w
        plans[g] = plan
    return [plans[2 * p + c] for c in range(2) for p in range(4)]


def _assemble(src, plans, out_cols, out_dtype, *, name, transposed=False):
    g, rows, c = src.shape
    tm = _row_tile(rows, 256)
    pad = -out_cols % LANES if transposed else 0

    def body(s_ref, o_ref):
        blocks = [s_ref[i].astype(F32) for i in range(g)]
        for d, plan in enumerate(plans):
            pieces = []
            for width, terms in plan + ([(pad, [])] if pad else []):
                v = None
                for b, first, sign in terms:
                    t = blocks[b][:, first:first + width]
                    t = -t if sign < 0 else t
                    v = t if v is None else v + t
                pieces.append(jnp.zeros((tm, width), F32) if v is None else v)
            block = pieces[0] if len(pieces) == 1 else jnp.concatenate(pieces, axis=1)
            o_ref[d] = (block.T[:out_cols, :] if transposed else block).astype(o_ref.dtype)

    block, shape = ((out_cols, tm), (out_cols, rows)) if transposed else ((tm, out_cols), (rows, out_cols))
    return pl.pallas_call(
        body, name=name, grid=(rows // tm,), in_specs=[pl.BlockSpec((g, tm, c), lambda i: (0, i, 0))],
        out_specs=pl.BlockSpec((len(plans),) + block, (lambda i: (0, 0, i)) if transposed else (lambda i: (0, i, 0))),
        out_shape=jax.ShapeDtypeStruct((len(plans),) + shape, out_dtype), compiler_params=_params(("parallel",)),
    )(src)


def _extend(nm, gathered, *, name):
    layout = LAYOUTS[nm]
    return _assemble(gathered, _plan_extend(layout, OWN_COLS[nm] // N_DEV), sum(w for w, _ in layout), BF16, name=name)[0]


def _fold_to_shards(nm, grad_ext, *, name):
    shards = _assemble(grad_ext[None], _plan_fold(LAYOUTS[nm], OWN_COLS[nm]), OWN_COLS[nm] // N_DEV, F32, name=name,
                       transposed=nm in SWAPPED_UPDATE)
    return shards.reshape((2, 4) + shards.shape[1:])


def _rope_tables(s):
    inv = 1.0 / (ROPE_THETA ** (jnp.arange(0, MLA_ROPE, 2, dtype=F32) / MLA_ROPE))
    ang = jnp.arange(s, dtype=F32)[:, None] * inv[None, :]
    cos, sin = jnp.cos(ang), jnp.sin(ang)
    c32, s32 = jnp.concatenate([cos, cos], axis=1), jnp.concatenate([sin, sin], axis=1)
    zeros, ones = jnp.zeros((s, HEAD), F32), jnp.ones((s, HEAD), F32)
    tq = jnp.concatenate([ones, c32, s32], axis=1) * (1.0 / math.sqrt(MLA_NOPE + MLA_ROPE))
    return (jnp.tile(tq, (1, MLA_HEADS)), jnp.concatenate([zeros, c32, c32], axis=1),
            jnp.concatenate([zeros, s32, s32], axis=1))


def _gn(y):
    return y * _rstd(y, D_GROUP)


def _mixer_fwd(x, w, tabs, l, after_attention=None):
    s = x.shape[0]
    tq, tkc, tks = tabs
    n = lambda t: f"l{l}_{t}"
    h = _rms_fwd(("row", x, D_MODEL, 0), w["attn_norm"], name=n("attn_norm"), rows=s, width=D_MODEL)
    proj = _mm(h, w["w_in"], name=n("in_proj"))
    def prep(cq, ckv, kr, krp, gq, gkv, wuq, wukv, tqv, tc, ts):
        cqn = (cq * _rstd(cq, 256) * gq).astype(BF16)
        ckvn = (ckv * _rstd(ckv, 128) * gkv).astype(BF16)
        qe = jnp.dot(cqn, wuq, preferred_element_type=F32)
        kve = jnp.dot(ckvn, wukv, preferred_element_type=F32)
        kb = kr * tc + krp * ts
        kvv = kve[:, 512:]
        lane = lax.broadcasted_iota(jnp.int32, kvv.shape, 1) & (LANES - 1)
        v = jnp.where(lane == HEAD, 1.0, kvv)
        k = kve[:, :512] + jnp.tile(kb, (1, MLA_HEADS))
        return cqn, ckvn, qe * tqv, k, k, v, v

    cqn, ckvn, qm, km, kmt, vm, vmt = _rowwise(
        prep, name=n("mla_prep"), rows=s,
        ins=[("row", proj, 256, C_CQ[0] // 256), ("row", proj, 128, C_CKV[0] // 128),
             ("row", proj, 128, C_KR[0] // 128), ("row", proj, 128, C_KRP[0] // 128),
             ("full", w["mla_q_norm"]), ("full", w["mla_kv_norm"]), ("full", w["w_uq"]), ("full", w["w_ukv"]),
             ("row", tq, 512, 0), ("row", tkc, 128, 0), ("row", tks, 128, 0)],
        outs=[(256, BF16), (128, BF16), (512, BF16, MLA_HEADS), (512, BF16, MLA_HEADS), (512, BF16, MLA_HEADS, "T"),
              (512, BF16, MLA_HEADS), (512, BF16, MLA_HEADS, "T")])
    y_a, lse_row = _mla_fwd(qm, km, vmt, name=n("mla_fwd"), s=s, nh=MLA_HEADS)
    mix_norm = w["mix_norm"]
    if after_attention is not None:
        mix_norm = mix_norm + after_attention(y_a)[0, 0]
    y_b = _conv_fwd(proj, w["conv_w"], name=n("conv_fwd"), s=s)
    y_c = _pool_fwd(proj, w["pool_wbd"], w["pool_scale"], name=n("pool_fwd"), s=s)
    y_d, lse_d = _swa_fwd(proj, w["hp_swa"], name=n("swa_fwd"), s=s, scale=1.0 / math.sqrt(HEAD))

    def mix(ya, yb, yc, yd, mn):
        return (jnp.concatenate([_gn(_unpad_heads(ya, 4)), _gn(yb), _gn(yc), _gn(_unpad_heads(yd, 4))], axis=1) * mn,)

    mixed = _rowwise(mix, name=n("group_norm"), rows=s,
                     ins=[("heads", y_a), ("row", y_b, 256, 0), ("row", y_c, 256, 0), ("row", y_d, 512, 0),
                          ("full", mix_norm)], outs=[(D_MODEL, BF16)])[0]
    x1 = _mm(mixed, w["w_o"], res=x, name=n("out_proj"))
    saved = dict(x=x, h=h, proj=proj, cqn=cqn, ckvn=ckvn, qm=qm, km=km, kmt=kmt, vm=vm, y_a=y_a, lse_row=lse_row,
                 y_b=y_b, y_c=y_c, y_d=y_d, lse_d=lse_d, mixed=mixed)
    return x1, saved


def _ffn_fwd(x1, w, l):
    s = x1.shape[0]
    n = lambda t: f"l{l}_{t}"
    h2 = _rms_fwd(("row", x1, D_MODEL, 0), w["ffn_norm"], name=n("ffn_norm"), rows=s, width=D_MODEL)

    def swiglu(gu):
        g, u = gu[:, :GU_TILE], gu[:, GU_TILE:]
        return gu, g * jax.nn.sigmoid(g) * u

    gu, act = _mm(h2, w["w_gate_up"], tm=2048, tn=2 * GU_TILE, name=n("gate_up"),
                  epilogue=(swiglu, [], [(2 * D_FF, BF16), (D_FF, BF16)]))
    x2 = _mm(act, w["w_down"], res=x1, tk=D_FF // 2, name=n("down"))
    return x2, dict(x1=x1, h2=h2, gu=gu, act=act)


def _ffn_bwd_down(dx2, sv, w, l):
    n = lambda t: f"l{l}_{t}"

    def swiglu_bwd(da, gu):
        gt, u = gu[:, :GU_TILE].astype(F32), gu[:, GU_TILE:].astype(F32)
        sg = jax.nn.sigmoid(gt)
        return (jnp.concatenate([da * u * sg * (1.0 + gt * (1.0 - sg)), da * gt * sg], axis=1),)

    dgu = _mm(dx2[1], w["w_down"], tb=True, tm=2048, tn=GU_TILE, name=n("d_act"),
              epilogue=(swiglu_bwd, [sv["gu"]], [(2 * D_FF, BF16)]))[0]
    g = dict(w_down=_mm(sv["act"], dx2[1], ta=True, tm=D_FF // 2, name=n("dw_down")))
    return dgu, g


def _ffn_bwd_up(dx2, dgu, sv, w, l):
    s = dgu.shape[0]
    n = lambda t: f"l{l}_{t}"
    dh2 = _mm(dgu, w["w_gate_up"], tb=True, tk=D_FF // 2, name=n("d_h2"))
    g = dict(w_gate_up=_mm(sv["h2"], dgu, ta=True, tn=D_FF // 2, name=n("dw_gate_up")))
    dx1, dx1_b, g["ffn_norm"] = _rms_bwd(("row", sv["x1"], D_MODEL, 0), w["ffn_norm"], dh2, dx2[0],
                                         name=n("ffn_norm_bwd"), rows=s, width=D_MODEL, out_dtypes=(F32, BF16))
    return (dx1, dx1_b), g


def _mixer_bwd_out(dx1, sv, w, l):
    s = dx1[1].shape[0]
    n = lambda t: f"l{l}_{t}"
    dmixed = _mm(dx1[1], w["w_o"], tb=True, name=n("d_mixed"))
    g = dict(w_o=_mm(sv["mixed"], dx1[1], ta=True, name=n("dw_o")))

    def mix_bwd(dm, ya, yb, yc, yd, mn):
        outs, dmn = [], []
        for i, y in enumerate((_unpad_heads(ya, 4), yb, yc, _unpad_heads(yd, 4))):
            lo, hi = i * D_GROUP, (i + 1) * D_GROUP
            r = _rstd(y, D_GROUP)
            nrm = y * r
            dmg = dm[:, lo:hi]
            dn = dmg * mn[:, lo:hi]
            dy = r * (dn - nrm * (jnp.sum(dn * nrm, axis=-1, keepdims=True) * (1.0 / D_GROUP)))
            outs.append(_pad_heads(dy, 4) if i in (0, 3) else dy)
            dmn.append(jnp.sum(dmg * nrm, axis=0, keepdims=True))
        return (*outs, jnp.concatenate(dmn, axis=1))

    dy_a, dy_b, dy_c, dy_d, g["mix_norm"] = _rowwise(
        mix_bwd, name=n("group_norm_bwd"), rows=s,
        ins=[("row", dmixed, D_MODEL, 0), ("heads", sv["y_a"]), ("row", sv["y_b"], 256, 0),
             ("row", sv["y_c"], 256, 0), ("row", sv["y_d"], 512, 0), ("full", w["mix_norm"])],
        outs=[(512, F32, MLA_HEADS), (256, F32), (256, F32), (512, F32)], reds=[(1, D_MODEL)])
    return (dy_a, dy_b, dy_c, dy_d), g


def _mixer_bwd_in(dx1, dys, sv, w, tabs, l):
    s = dx1[0].shape[0]
    tq, tkc, tks = tabs
    n = lambda t: f"l{l}_{t}"
    dy_a, dy_b, dy_c, dy_d = dys
    g = {}

    proj = sv["proj"]
    dq_sw, dk_sw, dv_sw, dsink = _swa_bwd(proj, sv["y_d"], dy_d, sv["lse_d"], w["hp_swa"], name=n("swa_bwd"), s=s,
                                          scale=1.0 / math.sqrt(HEAD))
    g["swa_sinks"] = dsink[:, 0, 0]

    dqm, dkm, dvm = _mla_bwd(sv["qm"], sv["km"], sv["kmt"], sv["vm"], sv["y_a"], dy_a, sv["lse_row"], w["hp_swa"],
                             name=n("mla_bwd"), s=s, nh=MLA_HEADS)

    def rms_bwd(x, gv, dy, width):
        r = _rstd(x, width)
        dyg = dy * gv
        dx = r * dyg - x * (r * r * r) * (jnp.sum(dyg * x, axis=-1, keepdims=True) * (1.0 / width))
        return dx, jnp.sum(dy * x * r, axis=0, keepdims=True)

    def prep_bwd(dq, dk, dv, cq, ckv, gq, gkv, wuq, wukv, tqv, tc, ts):
        dkb = dk[:, 0:128] + dk[:, 128:256] + dk[:, 256:384] + dk[:, 384:512]
        dq_ext = (dq * tqv).astype(BF16)
        dkv_ext = jnp.concatenate([dk.astype(BF16), dv], axis=1)
        dcqn = lax.dot_general(dq_ext, wuq, NT_DIMS, preferred_element_type=F32)
        dckvn = lax.dot_general(dkv_ext, wukv, NT_DIMS, preferred_element_type=F32)
        dcq, dgq = rms_bwd(cq, gq, dcqn, 256)
        dckv, dgkv = rms_bwd(ckv, gkv, dckvn, 128)
        return dq_ext, dkv_ext, dkb * tc, dkb * ts, dcq, dckv, dgq, dgkv

    dq_ext, dkv_ext, dkr, dkrp, dcq, dckv, g["mla_q_norm"], g["mla_kv_norm"] = _rowwise(
        prep_bwd, name=n("mla_prep_bwd"), rows=s,
        ins=[("heads", dqm), ("heads", dkm), ("heads", dvm), ("row", proj, 256, C_CQ[0] // 256),
             ("row", proj, 128, C_CKV[0] // 128), ("full", w["mla_q_norm"]), ("full", w["mla_kv_norm"]),
             ("full", w["w_uq"]), ("full", w["w_ukv"]), ("row", tq, 512, 0), ("row", tkc, 128, 0), ("row", tks, 128, 0)],
        outs=[(512, BF16), (1024, BF16), (128, BF16), (128, BF16), (256, BF16), (128, BF16)],
        reds=[(1, 256), (1, 128)])
    g["w_uq"] = _mm(sv["cqn"], dq_ext, ta=True, name=n("dw_uq"))
    g["w_ukv"] = _mm(sv["ckvn"], dkv_ext, ta=True, name=n("dw_ukv"))

    dgb, dgc, duc, g["conv_w"] = _conv_bwd(dy_b, proj, w["conv_w"], name=n("conv_bwd"), s=s)
    dup, g["pool_wbd"], g["pool_scale"] = _pool_bwd(dy_c, proj, w["pool_wbd"], w["pool_scale"], name=n("pool_bwd"), s=s)

    dproj = jnp.concatenate([dq_sw, dcq, dgb, dgc, duc, dup, dk_sw.astype(BF16), dv_sw.astype(BF16), dckv, dkr, dkrp],
                            axis=1)
    g["w_in"] = _mm(sv["h"], dproj, ta=True, name=n("dw_in"))
    return dproj, g


def _mixer_bwd_norm(dx1, dproj, sv, w, l):
    n = lambda t: f"l{l}_{t}"
    dh = _mm(dproj, w["w_in"], tb=True, name=n("d_h"))
    dx0, dx0_b, dg = _rms_bwd(("row", sv["x"], D_MODEL, 0), w["attn_norm"], dh, dx1[0], name=n("attn_norm_bwd"),
                              rows=dh.shape[0], width=D_MODEL, out_dtypes=(F32, BF16))
    return (dx0, dx0_b), dict(attn_norm=dg)


def _loss_head(x, target, g, *, s):
    def fn(xv, tv, gv):
        r = _rstd(xv, D_MODEL)
        e = xv * r * gv - tv
        part = jnp.sum(jnp.sum(e * e, axis=1, keepdims=True), axis=0, keepdims=True) * (0.5 / D_MODEL)
        dy = e * (1.0 / D_MODEL)
        dyg = dy * gv
        dx = r * dyg - xv * (r * r * r) * (jnp.sum(dyg * xv, axis=-1, keepdims=True) * (1.0 / D_MODEL))
        return dx, dx, jnp.sum(dy * xv * r, axis=0, keepdims=True), jnp.broadcast_to(part, (1, LANES))

    return _rowwise(fn, name="loss_head", rows=s,
                    ins=[("row", x, D_MODEL, 0), ("row", target, D_MODEL, 0), ("full", g)],
                    outs=[(D_MODEL, F32), (D_MODEL, BF16)], reds=[(1, D_MODEL), (1, LANES)])


def _alibi_slopes(n):
    return np.asarray([2.0 ** (-8.0 * (i + 1) / n) for i in range(n)], dtype=np.float32)


MIXER_WEIGHTS = ("w_in", "w_uq", "w_ukv", "conv_w", "w_o")
FFN_WEIGHTS = ("w_gate_up", "w_down")


def _mixer_weights(full, rep, l):
    pw = rep["pool_w"][l]
    z = jnp.zeros((HEAD, HEAD), F32)
    wbd = jnp.stack([jnp.block([[pw[2 * j], z], [z, pw[2 * j + 1]]]) for j in range(2)])
    return dict(
        attn_norm=rep["attn_norm"][l][None], w_in=full["w_in"], mla_q_norm=rep["mla_q_norm"][l][None],
        w_uq=full["w_uq"], mla_kv_norm=rep["mla_kv_norm"][l][None], w_ukv=full["w_ukv"],
        conv_w=full["conv_w"], pool_wbd=wbd, pool_scale=rep["pool_scale"][l][None],
        mix_norm=rep["mix_norm"][l][None], w_o=full["w_o"],
        hp_swa=jnp.stack([jnp.asarray(_alibi_slopes(SWA_HEADS)), rep["swa_sinks"][l]], axis=1))


def _ffn_weights(full, rep, l):
    return dict(ffn_norm=rep["ffn_norm"][l][None], w_gate_up=full["w_gate_up"], w_down=full["w_down"])


def _small_grads(g):
    rows = ("attn_norm", "mla_q_norm", "mla_kv_norm", "pool_scale", "ffn_norm", "mix_norm")
    out = {nm: g[nm][0] for nm in rows if nm in g}
    if "swa_sinks" in g:
        out["swa_sinks"] = g["swa_sinks"]
    if "pool_wbd" in g:
        e = g["pool_wbd"]
        out["pool_w"] = jnp.stack([e[j // 2][HEAD * (j % 2):HEAD * (j % 2 + 1), HEAD * (j % 2):HEAD * (j % 2 + 1)]
                                   for j in range(4)])
    return out


def kernel(x, attn_norm, w_in, mla_q_norm, w_uq, mla_kv_norm, w_ukv, conv_w, pool_w, pool_scale, swa_sinks, mix_norm, w_o, ffn_norm, w_gate_up, w_down, final_norm, loss_target, m_attn_norm, m_w_in, m_mla_q_norm, m_w_uq, m_mla_kv_norm, m_w_ukv, m_conv_w, m_pool_w, m_pool_scale, m_swa_sinks, m_mix_norm, m_w_o, m_ffn_norm, m_w_gate_up, m_w_down, m_final_norm, v_attn_norm, v_w_in, v_mla_q_norm, v_w_uq, v_mla_kv_norm, v_w_ukv, v_conv_w, v_pool_w, v_pool_scale, v_swa_sinks, v_mix_norm, v_w_o, v_ffn_norm, v_w_gate_up, v_w_down, v_final_norm):
    given = dict(locals())
    sh_names = [nm for nm, _, _ in SHARDED]
    sh_axis = {nm: ax - 1 for nm, _, ax in SHARDED}
    rep_names = [nm for nm, _ in REPLICATED]
    flat_names = [nm for nm in rep_names if nm != "pool_w"]
    flat_shapes = [shp for nm, shp in REPLICATED if nm != "pool_w"]
    rep = {nm: given[nm] for nm in rep_names if nm != "loss"}
    me = 4 * lax.axis_index("x") + 2 * lax.axis_index("y") + lax.axis_index("c")
    my_chip = 2 * lax.axis_index("x") + lax.axis_index("y")
    core = lax.axis_index("c").astype(jnp.int32).reshape(1)

    def behind(token, a):
        return a + token[0, 0].astype(a.dtype)

    def wire(nm, l):
        if nm == "conv_w":
            return lax.bitcast_convert_type(given[nm][l], BF16).reshape(3, -1)
        return given[nm][l].astype(BF16)

    def whole(nm, g, tag):
        if nm in LAYOUTS:
            return _extend(nm, g, name=f"extend_{nm}_{tag}")
        if nm == "conv_w":
            g = lax.bitcast_convert_type(g.reshape(N_DEV, 3, -1, 2), F32)
        return _shards_to_full(g, sh_axis[nm])

    def near_start(names, l, after, tag):
        srcs = [wire(nm, l) for nm in names]
        return _start_copies(_plan_gather_near, 4, srcs, [(N_DEV,) + a.shape for a in srcs], after, name=f"start_gather_{tag}")

    def pass_on(handle, after, tag):
        srcs, lands = _wait_copies(_plan_gather_near, handle, after, name=f"wait_gather_{tag}")
        handle, token = _start_copies(_plan_gather_pass, 3, [], lands, [], name=f"start_pass_{tag}")
        return (srcs, handle), token

    def gathered(names, state, after, tag):
        srcs, handle = state
        _, lands = _wait_copies(_plan_gather_pass, handle, after, name=f"wait_pass_{tag}")
        return {nm: whole(nm, lax.dynamic_update_index_in_dim(land, src, me, 0), tag)
                for nm, src, land in zip(names, srcs, lands)}

    layer1 = MIXER_WEIGHTS + FFN_WEIGHTS
    got = _all_gather([wire(nm, 0) for nm in MIXER_WEIGHTS], name="gather_mixer0")
    full_m0 = {nm: whole(nm, g, "mixer0") for nm, g in zip(MIXER_WEIGHTS, got)}
    h_f0, tok = near_start(FFN_WEIGHTS, 0, [], "ffn0")
    h_l1, tok = near_start(layer1, 1, [tok], "layer1")

    xs, target = x[0], loss_target[0]
    s = xs.shape[0]
    tabs = _rope_tables(s)
    wm, wf, svm, svf = [None] * DEPTH, [None] * DEPTH, [None] * DEPTH, [None] * DEPTH
    wm[0] = _mixer_weights(full_m0, rep, 0)
    wm[0]["attn_norm"] = behind(tok, wm[0]["attn_norm"])
    passing = {}

    def pass_ffn0(y_a):
        passing["ffn0"], token = pass_on(h_f0, [y_a], "ffn0")
        return token

    x1, svm[0] = _mixer_fwd(xs, wm[0], tabs, 0, after_attention=pass_ffn0)
    wf[0] = _ffn_weights(gathered(FFN_WEIGHTS, passing["ffn0"], [x1], "ffn0"), rep, 0)
    passing["layer1"], tok = pass_on(h_l1, [x1], "layer1")
    wf[0]["ffn_norm"] = behind(tok, wf[0]["ffn_norm"])
    x2, svf[0] = _ffn_fwd(x1, wf[0], 0)
    full_1 = gathered(layer1, passing["layer1"], [x2], "layer1")
    wm[1], wf[1] = _mixer_weights(full_1, rep, 1), _ffn_weights(full_1, rep, 1)
    x1, svm[1] = _mixer_fwd(x2, wm[1], tabs, 1)
    x2, svf[1] = _ffn_fwd(x1, wf[1], 1)
    dx_f, dx_b, d_final, loss = _loss_head(x2, target, rep["final_norm"][None], s=s)
    dx = (dx_f, dx_b)

    parts = {}

    def reduce_start(grads, l, after, tag):
        names = [nm for nm in sh_names if nm in grads]
        mine = [_fold_to_shards(nm, grads[nm], name=f"fold_{nm}_{l}") if nm in LAYOUTS
                else _full_to_shards(grads[nm], sh_axis[nm]) for nm in names]
        handle, token = _start_copies(_plan_sibling, 1, mine, [m.shape[1:] for m in mine], after, name=f"start_sibling_{tag}")
        return (names, l, handle), token

    def reduce_mid(state, after, tag):
        names, l, handle = state
        mine, theirs = _wait_copies(_plan_sibling, handle, after, name=f"wait_sibling_{tag}")
        sums = [_add_own(g, o, core, name=f"chip_sum_{nm}_{l}") for nm, g, o in zip(names, mine, theirs)]
        handle, token = _start_copies(_plan_chips, 3, sums, [a.shape for a in sums], [], name=f"start_chips_{tag}")
        return (names, l, handle), token

    def reduce_end(state, after, tag):
        names, l, handle = state
        sums, lands = _wait_copies(_plan_chips, handle, after, name=f"wait_chips_{tag}")
        for nm, own, land in zip(names, sums, lands):
            parts[nm, l] = lax.dynamic_update_index_in_dim(land, lax.dynamic_index_in_dim(own, my_chip, 0, keepdims=False),
                                                           my_chip, 0)

    small = [None] * DEPTH
    in_flight = []
    pending = None
    for l in reversed(range(DEPTH)):
        dgu, g_down = _ffn_bwd_down(dx, svf[l], wf[l], l)
        if pending is not None:
            state, token = reduce_mid(pending, [dgu], f"mixer{l + 1}")
            in_flight.append((state, f"mixer{l + 1}"))
            wf[l]["ffn_norm"] = behind(token, wf[l]["ffn_norm"])
        dx1, g_up = _ffn_bwd_up(dx, dgu, svf[l], wf[l], l)
        g_ffn = {**g_down, **g_up}
        state, token = reduce_start(g_ffn, l, [], f"ffn{l}")
        wm[l]["mix_norm"] = behind(token, wm[l]["mix_norm"])
        dys, g_out = _mixer_bwd_out(dx1, svm[l], wm[l], l)
        state, token = reduce_mid(state, [dys[0]], f"ffn{l}")
        in_flight.append((state, f"ffn{l}"))
        wm[l]["hp_swa"] = behind(token, wm[l]["hp_swa"])
        dproj, g_in = _mixer_bwd_in(dx1, dys, svm[l], wm[l], tabs, l)
        g_mixer = {**g_out, **g_in}
        pending, token = reduce_start(g_mixer, l, [], f"mixer{l}")
        wm[l]["attn_norm"] = behind(token, wm[l]["attn_norm"])
        dx, g_norm = _mixer_bwd_norm(dx1, dproj, svm[l], wm[l], l)
        small[l] = _small_grads({**g_ffn, **g_mixer, **g_norm})
    grads = {nm: jnp.stack([small[l][nm] for l in range(DEPTH)]) for nm in rep_names if nm in small[0]}
    grads["final_norm"] = d_final[0]
    grads["loss"] = loss[0, :1]
    zero = jnp.zeros((1,), F32)
    pool_rows = DEPTH * 4 * HEAD
    small, pool_parts = _all_gather([behind(token, _pack([grads[nm] for nm in flat_names])),
                                     grads["pool_w"].reshape(pool_rows, HEAD)], name="gather_small_grads")

    last, token = reduce_mid(pending, [dx[0], small], "mixer0")
    for state, tag in in_flight:
        reduce_end(state, [], tag)
    grad_x = dx[0]

    def adamw(nm, after):
        swap = (lambda a: jnp.swapaxes(a, 1, 2)) if nm in SWAPPED_UPDATE else (lambda a: a)
        res = _adamw([parts[nm, l] for l in range(DEPTH)], swap(given[nm]), swap(given["m_" + nm]), swap(given["v_" + nm]),
                     after, name=f"adamw_{nm}")
        return [swap(a) for a in res]

    sh_out = {nm: adamw(nm, token) for nm in FFN_WEIGHTS}
    packs = [_pack([given.get(pre + nm, zero) for nm in flat_names])[None] for pre in ("", "m_", "v_")]
    rep_res = _adamw([small], *packs, token, name="adamw_replicated")
    rep_out = [dict(zip(flat_names, _unpack(o[0], flat_shapes))) for o in rep_res]
    pool_res = _adamw([pool_parts], *[given[pre + "pool_w"].reshape(1, pool_rows, HEAD) for pre in ("", "m_", "v_")], token,
                      name="adamw_pool_w")
    for d, o in zip(rep_out, pool_res):
        d["pool_w"] = o.reshape(given["pool_w"].shape)

    reduce_end(last, [rep_res[0], sh_out["w_down"][0]], "mixer0")
    sh_out.update({nm: adamw(nm, token) for nm in MIXER_WEIGHTS})

    out = [rep_out[0]["loss"][0], grad_x[None]]
    for i in range(4):
        out += [sh_out[nm][i] if nm in sh_axis else rep_out[i][nm] for nm in WEIGHT_ORDER]
    return tuple(out)
```

```python
import functools
import math

import numpy as np
import jax
import jax.numpy as jnp
from jax import lax
from jax.experimental import pallas as pl
from jax.experimental.pallas import tpu as pltpu

F32 = jnp.float32
BF16 = jnp.bfloat16

D_MODEL = 1024
DEPTH = 2
D_GROUP = 256
MLA_HEADS = 4
MLA_NOPE = 64
MLA_ROPE = 32
ROPE_THETA = 10000.0
POOL_WINDOWS = (2, 4, 8, 16)
SWA_HEADS = 4
SWA_KV_HEADS = 2
SWA_WINDOW = 128
D_FF = 2816
GU_TILE = 256
RMS_EPS = 1e-6
LANES = 128
HEAD = 64
VMEM_LIMIT = 48 * 1024 * 1024
NEG = -1e30

ADAM_LR = 0.001
ADAM_B1 = 0.9
ADAM_B2 = 0.999
ADAM_EPS = 1e-08
ADAM_WD = 0.01
ADAM_STEP = 10

N_DEV = 8
PACK_COLS = 1024

C_QSW, C_CQ, C_GB, C_GC, C_UCONV, C_UPOOL = (0, 512), (512, 256), (768, 256), (1024, 256), (1280, 256), (1536, 256)
C_KSW, C_VSW, C_CKV, C_KR, C_KRP = (1792, 256), (2048, 256), (2304, 128), (2432, 128), (2560, 128)

SHARDED = (("w_in", (DEPTH, 1024, 244), 2), ("w_uq", (DEPTH, 256, 48), 2), ("w_ukv", (DEPTH, 128, 64), 2),
           ("conv_w", (DEPTH, 3, 32), 2), ("w_o", (DEPTH, 128, 1024), 1), ("w_gate_up", (DEPTH, 1024, 704), 2),
           ("w_down", (DEPTH, 352, 1024), 1))
REPLICATED = (("attn_norm", (DEPTH, 1024)), ("mla_q_norm", (DEPTH, 256)), ("mla_kv_norm", (DEPTH, 128)),
              ("pool_w", (DEPTH, 4, 64, 64)), ("pool_scale", (DEPTH, 256)), ("swa_sinks", (DEPTH, 4)),
              ("mix_norm", (DEPTH, 1024)), ("ffn_norm", (DEPTH, 1024)), ("final_norm", (1024,)), ("loss", (1,)))
WEIGHT_ORDER = ("attn_norm", "w_in", "mla_q_norm", "w_uq", "mla_kv_norm", "w_ukv", "conv_w", "pool_w", "pool_scale",
                "swa_sinks", "mix_norm", "w_o", "ffn_norm", "w_gate_up", "w_down", "final_norm")


def _params(sem):
    return pltpu.CompilerParams(dimension_semantics=sem, vmem_limit_bytes=VMEM_LIMIT)


def _pick(dim, target):
    if dim <= target:
        return dim
    best = None
    for t in range(LANES, target + 1, LANES):
        if dim % t == 0:
            best = t
    assert best is not None, (dim, target)
    return best


def _mm(a, b, *, name, ta=False, tb=False, res=None, out_dtype=F32, tm=1024, tn=1024, tk=1024, epilogue=None):
    m, k = (a.shape[1], a.shape[0]) if ta else a.shape
    n = b.shape[0] if tb else b.shape[1]
    assert (b.shape[1] if tb else b.shape[0]) == k
    tm, tn, tk = _pick(m, tm), _pick(n, tn), _pick(k, tk)
    nj, nk = n // tn, k // tk
    dims = (((0 if ta else 1,), (1 if tb else 0,)), ((), ()))
    fn, extra, outs = epilogue if epilogue is not None else (None, [], [(n, out_dtype)])
    if res is not None:
        assert epilogue is None
        fn, extra = (lambda acc, r: (acc + r,)), [res]
    n_in, n_out = 2 + len(extra), len(outs)

    def body(*refs):
        a_ref, b_ref, acc_ref = refs[0], refs[1], refs[-1]
        kk = pl.program_id(2)

        def product():
            return lax.dot_general(a_ref[...].astype(BF16), b_ref[...].astype(BF16), dims, preferred_element_type=F32)

        def finish(acc):
            tiles = (acc,) if fn is None else fn(acc, *[r[...] for r in refs[2:n_in]])
            for o_ref, tile in zip(refs[n_in:n_in + n_out], tiles):
                o_ref[...] = tile.astype(o_ref.dtype)

        if nk == 1:
            finish(product())
            return

        @pl.when(kk == 0)
        def _():
            acc_ref[...] = jnp.zeros_like(acc_ref)

        acc_ref[...] += product()

        @pl.when(kk == nk - 1)
        def _():
            finish(acc_ref[...])

    def col_tiles(width):
        assert width % (nj * LANES) == 0, (width, nj)
        return pl.BlockSpec((tm, width // nj), lambda i, j, kk: (i, j))

    a_spec = pl.BlockSpec((tk, tm), lambda i, j, kk: (kk, i)) if ta else pl.BlockSpec((tm, tk), lambda i, j, kk: (i, kk))
    b_spec = pl.BlockSpec((tn, tk), lambda i, j, kk: (j, kk)) if tb else pl.BlockSpec((tk, tn), lambda i, j, kk: (kk, j))
    res_ = pl.pallas_call(
        body, name=name, grid=(m // tm, nj, nk), in_specs=[a_spec, b_spec] + [col_tiles(e.shape[1]) for e in extra],
        out_specs=[col_tiles(w) for w, _ in outs],
        out_shape=[jax.ShapeDtypeStruct((m, w), dt) for w, dt in outs],
        scratch_shapes=[pltpu.VMEM((tm, tn), F32)] if nk > 1 else [],
        compiler_params=_params(("parallel", "parallel", "arbitrary")),
    )(a, b, *extra)
    return res_[0] if epilogue is None else res_


def _rowwise(fn, *, name, rows, ins, outs, reds=(), tm=512):
    tm = min(tm, rows)
    assert rows % tm == 0
    n_in, n_out = len(ins), len(outs)

    def body(*refs):
        vals = [jnp.concatenate([r[h] for h in range(r.shape[0])], axis=1) if spec[0] == "heads" else r[...]
                for spec, r in zip(ins, refs[:n_in])]
        res = fn(*vals)
        for out, r, v in zip(outs, refs[n_in:n_in + n_out], res[:n_out]):
            if len(out) >= 3:
                for h in range(out[2]):
                    piece = v[:, h * LANES:(h + 1) * LANES]
                    r[h] = (piece.T if len(out) == 4 else piece).astype(r.dtype)
            else:
                r[...] = v.astype(r.dtype)
        if reds:
            @pl.when(pl.program_id(0) == 0)
            def _():
                for r in refs[n_in + n_out:]:
                    r[...] = jnp.zeros_like(r)

            for r, v in zip(refs[n_in + n_out:], res[n_out:]):
                r[...] += v

    in_specs, args = [], []
    for spec in ins:
        if spec[0] == "row":
            _, arr, width, blk = spec
            in_specs.append(pl.BlockSpec((tm, width), functools.partial(lambda i, blk: (i, blk), blk=blk)))
        elif spec[0] == "heads":
            arr = spec[1]
            in_specs.append(pl.BlockSpec((arr.shape[0], tm, LANES), lambda i: (0, i, 0)))
        else:
            arr = spec[1]
            in_specs.append(pl.BlockSpec(arr.shape, functools.partial(lambda i, nd: (0,) * nd, nd=arr.ndim)))
        args.append(arr)
    def out_spec(o):
        if len(o) == 4:
            return pl.BlockSpec((o[2], LANES, tm), lambda i: (0, 0, i)), (o[2], LANES, rows)
        if len(o) == 3:
            return pl.BlockSpec((o[2], tm, LANES), lambda i: (0, i, 0)), (o[2], rows, LANES)
        return pl.BlockSpec((tm, o[0]), lambda i: (i, 0)), (rows, o[0])

    out_specs = [out_spec(o)[0] for o in outs]
    out_shape = [jax.ShapeDtypeStruct(out_spec(o)[1], o[1]) for o in outs]
    out_specs += [pl.BlockSpec((r, w), lambda i: (0, 0)) for r, w in reds]
    out_shape += [jax.ShapeDtypeStruct((r, w), F32) for r, w in reds]
    return pl.pallas_call(body, name=name, grid=(rows // tm,), in_specs=in_specs, out_specs=out_specs,
                          out_shape=out_shape, compiler_params=_params(("arbitrary",)))(*args)


def _rstd(x, n):
    return lax.rsqrt(jnp.sum(x * x, axis=-1, keepdims=True) * (1.0 / n) + RMS_EPS)


def _rms_fwd(x_spec, g, *, name, rows, width):
    def fn(x, gv):
        return (x * _rstd(x, width) * gv,)
    return _rowwise(fn, name=name, rows=rows, ins=[x_spec, ("full", g)], outs=[(width, BF16)])[0]


def _rms_bwd(x_spec, g, dy, res, *, name, rows, width, out_dtypes):
    def fn(x, gv, dyv, *rest):
        r = _rstd(x, width)
        dyg = dyv * gv
        dx = r * dyg - x * (r * r * r) * (jnp.sum(dyg * x, axis=-1, keepdims=True) * (1.0 / width))
        if rest:
            dx = dx + rest[0]
        return (dx,) * len(out_dtypes) + (jnp.sum(dyv * x * r, axis=0, keepdims=True),)

    ins = [x_spec, ("full", g), ("row", dy, width, 0)]
    if res is not None:
        ins.append(("row", res, width, 0))
    return _rowwise(fn, name=name, rows=rows, ins=ins, outs=[(width, dt) for dt in out_dtypes], reds=[(1, width)])


NT_DIMS = (((1,), (1,)), ((), ()))
TN_DIMS = (((0,), (0,)), ((), ()))
BNT_DIMS = (((2,), (2,)), ((0,), (0,)))
BNN_DIMS = (((2,), (1,)), ((0,), (0,)))


def _mla_tile(s):
    return min(512, s)


def _block_tables(pairs):
    return jnp.asarray([a for a, _ in pairs], jnp.int32), jnp.asarray([b for _, b in pairs], jnp.int32)


def _causal(shape, query_axis):
    return lax.broadcasted_iota(jnp.int32, shape, query_axis) >= lax.broadcasted_iota(jnp.int32, shape, 1 - query_axis)


def _mla_fwd(qa, ka, vta, *, name, s, nh):
    t = _mla_tile(s)
    nq = s // t

    pairs = [(i, j) for i in range(nq) for j in range(i + 1)]

    def body(qb_ref, kb_ref, q_ref, k_ref, vt_ref, o_ref, lse_ref, m_s, acc_s):
        i, j = qb_ref[pl.program_id(0)], kb_ref[pl.program_id(0)]

        @pl.when(j == 0)
        def _():
            m_s[...] = jnp.full_like(m_s, NEG)
            acc_s[...] = jnp.zeros_like(acc_s)

        def step(diag):
            sc = lax.dot_general(k_ref[...], q_ref[...], BNT_DIMS, preferred_element_type=F32)
            if diag:
                sc = jnp.where(_causal(sc.shape[1:], 1)[None], sc, NEG)
            m_prev = m_s[...]
            m_new = jnp.maximum(m_prev, jnp.max(sc, axis=1, keepdims=True))
            p = jnp.exp(sc - m_new).astype(BF16)
            acc_s[...] = (jnp.exp(m_prev - m_new) * acc_s[...]
                          + lax.dot_general(vt_ref[...], p, BNN_DIMS, preferred_element_type=F32))
            m_s[...] = m_new

        pl.when(j < i)(functools.partial(step, False))
        pl.when(j == i)(functools.partial(step, True))

        @pl.when(j == i)
        def _():
            row = lax.broadcasted_iota(jnp.int32, (LANES, t), 0)
            for h in range(nh):
                acc = acc_s[h]
                l = acc[HEAD:HEAD + 1, :]
                o_ref[h] = jnp.where(row < HEAD, acc / l, 0.0).T
                lse_ref[h] = m_s[h] + jnp.log(l)

    q_spec = pl.BlockSpec((nh, t, LANES), lambda p, qb, kb: (0, qb[p], 0))
    k_spec = pl.BlockSpec((nh, t, LANES), lambda p, qb, kb: (0, kb[p], 0))
    vt_spec = pl.BlockSpec((nh, LANES, t), lambda p, qb, kb: (0, 0, kb[p]))
    return pl.pallas_call(
        body, name=name,
        grid_spec=pltpu.PrefetchScalarGridSpec(
            num_scalar_prefetch=2, grid=(len(pairs),), in_specs=[q_spec, k_spec, vt_spec],
            out_specs=[q_spec, pl.BlockSpec((nh, 1, t), lambda p, qb, kb: (0, 0, qb[p]))],
            scratch_shapes=[pltpu.VMEM((nh, 1, t), F32), pltpu.VMEM((nh, LANES, t), F32)]),
        out_shape=[jax.ShapeDtypeStruct((nh, s, LANES), F32), jax.ShapeDtypeStruct((nh, 1, s), F32)],
        compiler_params=_params(("arbitrary",)),
    )(*_block_tables(pairs), qa, ka, vta)


def _mla_bwd(qa, ka, kta, va, o, do, lse_row, after, *, name, s, nh):
    t = _mla_tile(s)
    nq = s // t

    pairs = [(kb, j) for kb in range(nq) for j in range(kb, nq)]

    def body(kb_ref, qb_ref, q_ref, k_ref, kt_ref, v_ref, o_ref, do_ref, lse_ref, after_ref, dq_hbm, dk_ref, dv_ref,
             dqt_s, dk_s, dv_s, d_s, stage, sem):
        kb, j = kb_ref[pl.program_id(0)], qb_ref[pl.program_id(0)]
        cols = pl.ds(pl.multiple_of(j * t, t), t)

        @pl.when(j == kb)
        def _():
            dk_s[...] = jnp.zeros_like(dk_s)
            dv_s[...] = jnp.zeros_like(dv_s)

        @pl.when(kb == 0)
        def _():
            dqt_s[:, :, cols] = jnp.zeros((nh, LANES, t), F32)
            for h in range(nh):
                d_col = jnp.sum(do_ref[h] * o_ref[h], axis=1, keepdims=True)
                d_s[h, :, cols] = jnp.broadcast_to(d_col, (t, LANES)).T[0:1, :]

        def step(diag):
            q = q_ref[...]
            do_b = do_ref[...].astype(BF16)
            sc = lax.dot_general(k_ref[...], q, BNT_DIMS, preferred_element_type=F32)
            if diag:
                sc = jnp.where(_causal(sc.shape[1:], 1)[None], sc, NEG)
            p = jnp.exp(sc - lse_ref[...])
            dv_s[...] += lax.dot_general(p.astype(BF16), do_b, BNN_DIMS, preferred_element_type=F32)
            dp = lax.dot_general(v_ref[...], do_b, BNT_DIMS, preferred_element_type=F32)
            ds = (p * (dp - d_s[:, :, cols])).astype(BF16)
            dk_s[...] += lax.dot_general(ds, q, BNN_DIMS, preferred_element_type=F32)
            dqt_s[:, :, cols] += lax.dot_general(kt_ref[...], ds, BNN_DIMS, preferred_element_type=F32)

        pl.when(j > kb)(functools.partial(step, False))
        pl.when(j == kb)(functools.partial(step, True))

        @pl.when(j == kb)
        def _():
            for h in range(nh):
                stage[h] = dqt_s[h, :, cols].T
            out = pltpu.make_async_copy(stage, dq_hbm.at[:, cols, :], sem)
            out.start()
            out.wait()

        @pl.when(j == nq - 1)
        def _():
            dk_ref[...] = dk_s[...]
            dv_ref[...] = dv_s[...].astype(dv_ref.dtype)

    q_spec = pl.BlockSpec((nh, t, LANES), lambda p, kb, qb: (0, qb[p], 0))
    kv_spec = pl.BlockSpec((nh, t, LANES), lambda p, kb, qb: (0, kb[p], 0))
    kt_spec = pl.BlockSpec((nh, LANES, t), lambda p, kb, qb: (0, 0, kb[p]))
    row_spec = pl.BlockSpec((nh, 1, t), lambda p, kb, qb: (0, 0, qb[p]))
    whole = jax.ShapeDtypeStruct((nh, s, LANES), F32)
    return pl.pallas_call(
        body, name=name,
        grid_spec=pltpu.PrefetchScalarGridSpec(
            num_scalar_prefetch=2, grid=(len(pairs),),
            in_specs=[q_spec, kv_spec, kt_spec, kv_spec, q_spec, q_spec, row_spec, pl.BlockSpec(memory_space=pl.ANY)],
            out_specs=[pl.BlockSpec(memory_space=pl.ANY), kv_spec, kv_spec],
            scratch_shapes=[pltpu.VMEM((nh, LANES, s), F32), pltpu.VMEM((nh, t, LANES), F32),
                            pltpu.VMEM((nh, t, LANES), F32), pltpu.VMEM((nh, 1, s), F32), pltpu.VMEM((nh, t, LANES), F32),
                            pltpu.SemaphoreType.DMA]),
        out_shape=[whole, whole, jax.ShapeDtypeStruct((nh, s, LANES), BF16)],
        compiler_params=_params(("arbitrary",)),
    )(*_block_tables(pairs), qa, ka, kta, va, o, do, lse_row, after)


SWA_PIECE = 128
SWA_KEYS = 2 * SWA_PIECE


def _swa_block(s):
    return min(512, s)


def _swa_piece(hp_ref, h, q, k_ref, v_ref, qpos0, scale):
    kstart = pl.multiple_of(jnp.maximum(qpos0 - SWA_PIECE, 0), SWA_PIECE)
    k = k_ref[pl.ds(kstart, SWA_KEYS), :].astype(BF16)
    v = v_ref[pl.ds(kstart, SWA_KEYS), :].astype(BF16)
    sc = lax.dot_general(q, k, NT_DIMS, preferred_element_type=F32)
    dist = (qpos0 + lax.broadcasted_iota(jnp.int32, sc.shape, 0)) - (kstart + lax.broadcasted_iota(jnp.int32, sc.shape, 1))
    sc = sc * scale - hp_ref[h, 0] * dist.astype(F32)
    sc = jnp.where((dist >= 0) & (dist < SWA_WINDOW), sc, NEG)
    return kstart, k, v, sc


def _swa_fwd(proj, hp, *, name, s, scale):
    tb = _swa_block(s)
    group = SWA_HEADS // SWA_KV_HEADS
    q_off, k_off, v_off = C_QSW[0] // LANES, C_KSW[0] // LANES, C_VSW[0] // LANES

    def body(hp_ref, q_ref, k_ref, v_ref, o_ref, lse_ref):
        h, i = pl.program_id(0), pl.program_id(1)
        sink = hp_ref[h, 1]
        for r in range(0, tb, SWA_PIECE):
            rows = pl.ds(r, SWA_PIECE)
            _, _, v, sc = _swa_piece(hp_ref, h, q_ref[rows, :].astype(BF16), k_ref, v_ref, i * tb + r, scale)
            m = jnp.maximum(jnp.max(sc, axis=1, keepdims=True), sink)
            p = jnp.exp(sc - m)
            l = jnp.sum(p, axis=1, keepdims=True) + jnp.exp(sink - m)
            o_ref[rows, :] = jnp.dot(p.astype(BF16), v, preferred_element_type=F32) / l
            lse_ref[rows, :] = m + jnp.log(l)

    whole = lambda off: pl.BlockSpec((s, LANES), lambda h, i: (0, off + h // group))
    return pl.pallas_call(
        body, name=name, grid=(SWA_HEADS, s // tb),
        in_specs=[pl.BlockSpec(memory_space=pltpu.SMEM), pl.BlockSpec((tb, LANES), lambda h, i: (i, q_off + h)),
                  whole(k_off), whole(v_off)],
        out_specs=[pl.BlockSpec((tb, LANES), lambda h, i: (i, h)), pl.BlockSpec((None, tb, 1), lambda h, i: (h, i, 0))],
        out_shape=[jax.ShapeDtypeStruct((s, SWA_HEADS * LANES), F32), jax.ShapeDtypeStruct((SWA_HEADS, s, 1), F32)],
        compiler_params=_params(("parallel", "parallel")),
    )(hp, proj, proj, proj)


def _swa_bwd(proj, o, do, lse, hp, *, name, s, scale):
    tb = _swa_block(s)
    nqb = s // tb
    group = SWA_HEADS // SWA_KV_HEADS
    q_off, k_off, v_off = C_QSW[0] // LANES, C_KSW[0] // LANES, C_VSW[0] // LANES

    def body(hp_ref, q_ref, k_ref, v_ref, o_ref, do_ref, lse_ref, dq_ref, dk_ref, dv_ref, dsink_ref):
        kh, g, i = pl.program_id(0), pl.program_id(1), pl.program_id(2)
        h = kh * group + g
        sink = hp_ref[h, 1]

        @pl.when((g == 0) & (i == 0))
        def _():
            dk_ref[...] = jnp.zeros_like(dk_ref)
            dv_ref[...] = jnp.zeros_like(dv_ref)

        @pl.when(i == 0)
        def _():
            dsink_ref[...] = jnp.zeros_like(dsink_ref)

        for r in range(0, tb, SWA_PIECE):
            rows = pl.ds(r, SWA_PIECE)
            q = q_ref[rows, :].astype(BF16)
            dov = do_ref[rows, :]
            do_b = dov.astype(BF16)
            lse_r = lse_ref[rows, :]
            d_r = jnp.sum(dov * o_ref[rows, :], axis=1, keepdims=True)
            kstart, k, v, sc = _swa_piece(hp_ref, h, q, k_ref, v_ref, i * tb + r, scale)
            p = jnp.exp(sc - lse_r)
            dp = lax.dot_general(do_b, v, NT_DIMS, preferred_element_type=F32)
            ds = (p * (dp - d_r)).astype(BF16)
            dq_ref[rows, :] = (jnp.dot(ds, k, preferred_element_type=F32) * scale).astype(dq_ref.dtype)
            win = pl.ds(kstart, SWA_KEYS)
            dk_ref[win, :] += lax.dot_general(ds, q, TN_DIMS, preferred_element_type=F32) * scale
            dv_ref[win, :] += lax.dot_general(p.astype(BF16), do_b, TN_DIMS, preferred_element_type=F32)
            part = jnp.sum(-jnp.exp(sink - lse_r) * d_r, axis=0, keepdims=True)
            dsink_ref[...] += jnp.broadcast_to(part, (1, LANES))

    whole = lambda off: pl.BlockSpec((s, LANES), lambda kh, g, i: (0, off + kh))
    q_map = lambda kh, g, i: (i, kh * group + g)
    return pl.pallas_call(
        body, name=name, grid=(SWA_KV_HEADS, group, nqb),
        in_specs=[pl.BlockSpec(memory_space=pltpu.SMEM),
                  pl.BlockSpec((tb, LANES), lambda kh, g, i: (i, q_off + kh * group + g)), whole(k_off), whole(v_off),
                  pl.BlockSpec((tb, LANES), q_map), pl.BlockSpec((tb, LANES), q_map),
                  pl.BlockSpec((None, tb, 1), lambda kh, g, i: (kh * group + g, i, 0))],
        out_specs=[pl.BlockSpec((tb, LANES), q_map), whole(0), whole(0),
                   pl.BlockSpec((None, 1, LANES), lambda kh, g, i: (kh * group + g, 0, 0))],
        out_shape=[jax.ShapeDtypeStruct((s, SWA_HEADS * LANES), BF16),
                   jax.ShapeDtypeStruct((s, SWA_KV_HEADS * LANES), F32), jax.ShapeDtypeStruct((s, SWA_KV_HEADS * LANES), F32),
                   jax.ShapeDtypeStruct((SWA_HEADS, 1, LANES), F32)],
        compiler_params=_params(("parallel", "arbitrary", "arbitrary")),
    )(hp, proj, proj, proj, o, do, lse)


def _shift_down(z, k):
    rows = lax.broadcasted_iota(jnp.int32, z.shape, 0)
    return jnp.where(rows >= k, pltpu.roll(z, k, 0), 0.0)


def _shift_up(z, k):
    n = z.shape[0]
    rows = lax.broadcasted_iota(jnp.int32, z.shape, 0)
    return jnp.where(rows < n - k, pltpu.roll(z, n - k, 0), 0.0)


def _rows3(a, b, c):
    r = lax.broadcasted_iota(jnp.int32, (3, a.shape[1]), 0)
    return jnp.where(r == 0, a, jnp.where(r == 1, b, c))


def _col_spec(s, off):
    return pl.BlockSpec((s, LANES), functools.partial(lambda j, off: (0, off + j), off=off))


def _conv_fwd(proj, conv_w, *, name, s):
    def body(gb_ref, gc_ref, u_ref, w_ref, y_ref):
        w0, w1, w2 = w_ref[0:1, :], w_ref[1:2, :], w_ref[2:3, :]
        z = gc_ref[...] * u_ref[...]
        c = w2 * z + w1 * _shift_down(z, 1) + w0 * _shift_down(z, 2)
        y_ref[...] = gb_ref[...] * c

    return pl.pallas_call(
        body, name=name, grid=(2,),
        in_specs=[_col_spec(s, C_GB[0] // LANES), _col_spec(s, C_GC[0] // LANES), _col_spec(s, C_UCONV[0] // LANES),
                  pl.BlockSpec((3, LANES), lambda j: (0, j))],
        out_specs=_col_spec(s, 0), out_shape=jax.ShapeDtypeStruct((s, D_GROUP), F32),
        compiler_params=_params(("parallel",)),
    )(proj, proj, proj, conv_w)


def _conv_bwd(dy, proj, conv_w, *, name, s):
    def body(dy_ref, gb_ref, gc_ref, u_ref, w_ref, dgb_ref, dgc_ref, du_ref, dw_ref):
        w0, w1, w2 = w_ref[0:1, :], w_ref[1:2, :], w_ref[2:3, :]
        gc, u, dyv = gc_ref[...], u_ref[...], dy_ref[...]
        z = gc * u
        z1, z2 = _shift_down(z, 1), _shift_down(z, 2)
        c = w2 * z + w1 * z1 + w0 * z2
        dgb_ref[...] = (dyv * c).astype(dgb_ref.dtype)
        dc = dyv * gb_ref[...]
        dz = w2 * dc + w1 * _shift_up(dc, 1) + w0 * _shift_up(dc, 2)
        dgc_ref[...] = (dz * u).astype(dgc_ref.dtype)
        du_ref[...] = (dz * gc).astype(du_ref.dtype)
        dw_ref[...] = _rows3(jnp.sum(dc * z2, axis=0, keepdims=True), jnp.sum(dc * z1, axis=0, keepdims=True),
                             jnp.sum(dc * z, axis=0, keepdims=True))

    act = jax.ShapeDtypeStruct((s, D_GROUP), BF16)
    return pl.pallas_call(
        body, name=name, grid=(2,),
        in_specs=[_col_spec(s, 0), _col_spec(s, C_GB[0] // LANES), _col_spec(s, C_GC[0] // LANES),
                  _col_spec(s, C_UCONV[0] // LANES), pl.BlockSpec((3, LANES), lambda j: (0, j))],
        out_specs=[_col_spec(s, 0), _col_spec(s, 0), _col_spec(s, 0), pl.BlockSpec((3, LANES), lambda j: (0, j))],
        out_shape=[act, act, act, jax.ShapeDtypeStruct((3, D_GROUP), F32)],
        compiler_params=_params(("parallel",)),
    )(dy, proj, proj, proj, conv_w)


def _pool_select(j, lane, a2, a4, a8, a16):
    lo = lane < HEAD
    return jnp.where(j == 0, jnp.where(lo, a2, a4), jnp.where(lo, a8, a16))


def _pooled(u, j):
    s2 = u + _shift_down(u, 1)
    s4 = s2 + _shift_down(s2, 2)
    s8 = s4 + _shift_down(s4, 4)
    s16 = s8 + _shift_down(s8, 8)
    lane = lax.broadcasted_iota(jnp.int32, u.shape, 1)
    rows = lax.broadcasted_iota(jnp.int32, u.shape, 0)
    win = _pool_select(j, lane, *POOL_WINDOWS)
    count = jnp.minimum(rows + 1, win).astype(F32)
    return _pool_select(j, lane, s2, s4, s8, s16) / count - u, count


def _pool_fwd(proj, wbd, scale, *, name, s):
    def body(u_ref, w_ref, sc_ref, y_ref):
        pooled, _ = _pooled(u_ref[...], pl.program_id(0))
        y_ref[...] = jnp.dot(pooled.astype(BF16), w_ref[...].astype(BF16), preferred_element_type=F32) * sc_ref[...]

    return pl.pallas_call(
        body, name=name, grid=(2,),
        in_specs=[_col_spec(s, C_UPOOL[0] // LANES), pl.BlockSpec((None, LANES, LANES), lambda j: (j, 0, 0)),
                  pl.BlockSpec((1, LANES), lambda j: (0, j))],
        out_specs=_col_spec(s, 0), out_shape=jax.ShapeDtypeStruct((s, D_GROUP), F32),
        compiler_params=_params(("parallel",)),
    )(proj, wbd, scale)


def _pool_bwd(dy, proj, wbd, scale, *, name, s):
    def body(dy_ref, u_ref, w_ref, sc_ref, du_ref, dw_ref, dsc_ref):
        j = pl.program_id(0)
        pooled, count = _pooled(u_ref[...], j)
        pooled_b = pooled.astype(BF16)
        w_b = w_ref[...].astype(BF16)
        dyv = dy_ref[...]
        mixed = jnp.dot(pooled_b, w_b, preferred_element_type=F32)
        dsc_ref[...] = jnp.sum(dyv * mixed, axis=0, keepdims=True)
        dms = (dyv * sc_ref[...]).astype(BF16)
        dw_ref[...] = lax.dot_general(pooled_b, dms, (((0,), (0,)), ((), ())), preferred_element_type=F32)
        dpooled = lax.dot_general(dms, w_b, (((1,), (1,)), ((), ())), preferred_element_type=F32)
        r = dpooled / count
        a2 = r + _shift_up(r, 1)
        a4 = a2 + _shift_up(a2, 2)
        a8 = a4 + _shift_up(a4, 4)
        a16 = a8 + _shift_up(a8, 8)
        lane = lax.broadcasted_iota(jnp.int32, r.shape, 1)
        du_ref[...] = (_pool_select(j, lane, a2, a4, a8, a16) - dpooled).astype(du_ref.dtype)

    return pl.pallas_call(
        body, name=name, grid=(2,),
        in_specs=[_col_spec(s, 0), _col_spec(s, C_UPOOL[0] // LANES),
                  pl.BlockSpec((None, LANES, LANES), lambda j: (j, 0, 0)), pl.BlockSpec((1, LANES), lambda j: (0, j))],
        out_specs=[_col_spec(s, 0), pl.BlockSpec((None, LANES, LANES), lambda j: (j, 0, 0)),
                   pl.BlockSpec((1, LANES), lambda j: (0, j))],
        out_shape=[jax.ShapeDtypeStruct((s, D_GROUP), BF16), jax.ShapeDtypeStruct((2, LANES, LANES), F32),
                   jax.ShapeDtypeStruct((1, D_GROUP), F32)],
        compiler_params=_params(("parallel",)),
    )(dy, proj, wbd, scale)


def _mesh_pos():
    return lax.axis_index("x"), lax.axis_index("y"), lax.axis_index("c")


def _any_specs(n):
    return [pl.BlockSpec(memory_space=pl.ANY)] * n


def _all_gather(xs, *, name):
    n = len(xs)

    def body(*refs):
        x_refs, out_refs = refs[:n], refs[n:2 * n]
        send_sems, recv_sems, local_sems = refs[2 * n:]
        x, y, cc = _mesh_pos()
        me, sibling = (x, y, cc), (x, y, 1 - cc)
        chips = [(1 - x, y), (x, 1 - y), (1 - x, 1 - y)]

        def slot(a, px, py, pc):
            return out_refs[a].at[4 * px + 2 * py + pc]

        def copy(a, k, block, to, src=None):
            return pltpu.make_async_remote_copy(
                src_ref=slot(a, *block) if src is None else src, dst_ref=slot(a, *block), send_sem=send_sems.at[a, k],
                recv_sem=recv_sems.at[a, k], device_id=to, device_id_type=pl.DeviceIdType.MESH)

        mine = [pltpu.make_async_copy(x_refs[a], slot(a, *me), local_sems.at[a]) for a in range(n)]
        first = []
        for a in range(n):
            first.append(copy(a, 0, me, sibling, src=x_refs[a]))
            first += [copy(a, 1 + j, me, (*chip, cc), src=x_refs[a]) for j, chip in enumerate(chips)]
        for cp in mine + first:
            cp.start()
        passed = []
        for j, chip in enumerate(chips):
            for a in range(n):
                copy(a, 1 + j, (*chip, cc), me).wait_recv()
                passed.append(copy(a, 4 + j, (*chip, cc), sibling))
                passed[-1].start()
        for a in range(n):
            copy(a, 0, sibling, me).wait_recv()
        for j, chip in enumerate(chips):
            for a in range(n):
                copy(a, 4 + j, (*chip, 1 - cc), me).wait_recv()
        for cp in first + passed:
            cp.wait_send()
        for cp in mine:
            cp.wait()

    return pl.pallas_call(
        body, name=name, out_shape=[jax.ShapeDtypeStruct((N_DEV,) + a.shape, a.dtype) for a in xs],
        in_specs=_any_specs(n), out_specs=_any_specs(n),
        scratch_shapes=[pltpu.SemaphoreType.DMA((n, 7)), pltpu.SemaphoreType.DMA((n, 7)), pltpu.SemaphoreType.DMA((n,))],
    )(*xs)


def _plan_gather_near(src_refs, land_refs, send_sems, recv_sems):
    x, y, cc = _mesh_pos()
    me = 4 * x + 2 * y + cc
    plan = []
    for a, (src, land) in enumerate(zip(src_refs, land_refs)):
        for k, (px, py, pc) in enumerate([(x, y, 1 - cc), (1 - x, y, cc), (x, 1 - y, cc), (1 - x, 1 - y, cc)]):
            sems = dict(send_sem=send_sems.at[4 * a + k], recv_sem=recv_sems.at[4 * a + k], device_id=(px, py, pc),
                        device_id_type=pl.DeviceIdType.MESH)
            plan.append((pltpu.make_async_remote_copy(src_ref=src, dst_ref=land.at[me], **sems),
                         pltpu.make_async_remote_copy(src_ref=src, dst_ref=land.at[4 * px + 2 * py + pc], **sems)))
    return plan


def _plan_gather_pass(src_refs, land_refs, send_sems, recv_sems):
    x, y, cc = _mesh_pos()
    plan = []
    for a, land in enumerate(land_refs):
        for j, (px, py) in enumerate([(1 - x, y), (x, 1 - y), (1 - x, 1 - y)]):
            mine, theirs = land.at[4 * px + 2 * py + cc], land.at[4 * px + 2 * py + 1 - cc]
            sems = dict(send_sem=send_sems.at[3 * a + j], recv_sem=recv_sems.at[3 * a + j], device_id=(x, y, 1 - cc),
                        device_id_type=pl.DeviceIdType.MESH)
            plan.append((pltpu.make_async_remote_copy(src_ref=mine, dst_ref=mine, **sems),
                         pltpu.make_async_remote_copy(src_ref=mine, dst_ref=theirs, **sems)))
    return plan


def _plan_sibling(src_refs, land_refs, send_sems, recv_sems):
    x, y, cc = _mesh_pos()
    plan = []
    for a, (src, land) in enumerate(zip(src_refs, land_refs)):
        cp = pltpu.make_async_remote_copy(
            src_ref=src.at[1 - cc], dst_ref=land, send_sem=send_sems.at[a], recv_sem=recv_sems.at[a],
            device_id=(x, y, 1 - cc), device_id_type=pl.DeviceIdType.MESH)
        plan.append((cp, cp))
    return plan


def _plan_chips(src_refs, land_refs, send_sems, recv_sems):
    x, y, cc = _mesh_pos()
    my_chip = 2 * x + y
    plan = []
    for a, (src, land) in enumerate(zip(src_refs, land_refs)):
        for j, (px, py) in enumerate([(1 - x, y), (x, 1 - y), (1 - x, 1 - y)]):
            peer = 2 * px + py
            sems = dict(send_sem=send_sems.at[3 * a + j], recv_sem=recv_sems.at[3 * a + j], device_id=(px, py, cc),
                        device_id_type=pl.DeviceIdType.MESH)
            plan.append((pltpu.make_async_remote_copy(src_ref=src.at[peer], dst_ref=land.at[my_chip], **sems),
                         pltpu.make_async_remote_copy(src_ref=src.at[peer], dst_ref=land.at[peer], **sems)))
    return plan


HBM_SPEC = pl.BlockSpec(memory_space=pltpu.HBM)
SEM_SPEC = pl.BlockSpec(memory_space=pltpu.SEMAPHORE)
ANY_SPEC = pl.BlockSpec(memory_space=pl.ANY)
SIDE_EFFECT = pltpu.CompilerParams(has_side_effects=pltpu.SideEffectType.DATAFLOW_SIDE_EFFECTING)


def _start_copies(plan, sems_per_array, srcs, lands, after, *, name):
    lands = [lax.empty(l, a.dtype) if isinstance(l, tuple) else l for l, a in zip(lands, srcs or lands)]
    ns, n = len(srcs), len(srcs) + len(lands)

    def body(*refs):
        send_sems, recv_sems = refs[n + len(after)], refs[n + len(after) + 1]
        for out, _ in plan(refs[:ns], refs[ns:n], send_sems, recv_sems):
            out.start()
        refs[-1][...] = jnp.zeros_like(refs[-1])

    sem = pltpu.SemaphoreType.DMA((len(lands) * sems_per_array,))
    res = pl.pallas_call(
        body, name=name,
        out_shape=(sem, sem, *[pltpu.HBM(a.shape, a.dtype) for a in srcs + lands], jax.ShapeDtypeStruct((8, LANES), F32)),
        in_specs=[HBM_SPEC] * n + [ANY_SPEC] * len(after),
        out_specs=(SEM_SPEC, SEM_SPEC, *[HBM_SPEC] * n, pl.BlockSpec(memory_space=pltpu.VMEM)),
        input_output_aliases={i: 2 + i for i in range(n)}, compiler_params=SIDE_EFFECT,
    )(*[pltpu.with_memory_space_constraint(a, pltpu.HBM) for a in srcs + lands], *after)
    return (res[0], res[1], list(res[2:2 + ns]), list(res[2 + ns:2 + n])), res[-1]


def _wait_copies(plan, handle, after, *, name):
    send, recv, srcs, lands = handle
    ns, n = len(srcs), len(srcs) + len(lands)

    def body(*refs):
        for out, inc in plan(refs[:ns], refs[ns:n], refs[n], refs[n + 1]):
            out.wait_send()
            inc.wait_recv()

    res = pl.pallas_call(
        body, name=name, out_shape=tuple(pltpu.HBM(a.shape, a.dtype) for a in srcs + lands),
        in_specs=[HBM_SPEC] * n + [SEM_SPEC, SEM_SPEC] + [ANY_SPEC] * len(after), out_specs=[HBM_SPEC] * n,
        input_output_aliases={i: i for i in range(n)}, compiler_params=SIDE_EFFECT,
    )(*srcs, *lands, send, recv, *after)
    return list(res[:ns]), list(res[ns:])


def _row_tile(rows, target=512):
    if rows <= target:
        return rows
    best = None
    for t in range(8, target + 1, 8):
        if rows % t == 0:
            best = t
    assert best is not None, (rows, target)
    return best


def _add_own(g, other, core, *, name):
    _, _, rows, cols = g.shape
    tm = _row_tile(rows)

    def body(c_ref, g_ref, o_ref, out_ref):
        out_ref[...] = g_ref[...] + o_ref[...]

    return pl.pallas_call(
        body, name=name, out_shape=jax.ShapeDtypeStruct(other.shape, other.dtype),
        grid_spec=pltpu.PrefetchScalarGridSpec(
            num_scalar_prefetch=1, grid=(4, rows // tm),
            in_specs=[pl.BlockSpec((None, None, tm, cols), lambda p, i, c_ref: (c_ref[0], p, i, 0)),
                      pl.BlockSpec((None, tm, cols), lambda p, i, c_ref: (p, i, 0))],
            out_specs=pl.BlockSpec((None, tm, cols), lambda p, i, c_ref: (p, i, 0))),
        compiler_params=_params(("parallel", "parallel")),
    )(core, g, other)


def _adamw(parts, w, m, v, after, *, name):
    layers, rows, cols = w.shape
    assert len(parts) == layers
    tm = _row_tile(rows, 256)
    nr = rows // tm

    def body(*refs):
        p_refs = refs[:layers]
        w_ref, m_ref, v_ref, _, g_ref, d_ref, nm_ref, nv_ref, g_s = refs[layers:]
        for ll in range(layers):
            @pl.when(pl.program_id(0) == ll)
            def _(ll=ll):
                g = p_refs[ll][0]
                for q in range(1, p_refs[ll].shape[0]):
                    g = g + p_refs[ll][q]
                g_s[...] = g

        g = g_s[...]
        mm = ADAM_B1 * m_ref[...] + (1.0 - ADAM_B1) * g
        vv = ADAM_B2 * v_ref[...] + (1.0 - ADAM_B2) * jnp.square(g)
        m_hat = mm / (1.0 - ADAM_B1 ** ADAM_STEP)
        v_hat = vv / (1.0 - ADAM_B2 ** ADAM_STEP)
        g_ref[...] = g
        d_ref[...] = -ADAM_LR * (m_hat / (jnp.sqrt(v_hat) + ADAM_EPS) + ADAM_WD * w_ref[...])
        nm_ref[...] = mm
        nv_ref[...] = vv

    def part_spec(ll, p):
        return pl.BlockSpec((p, tm, cols), lambda l, i: (0, jnp.where(l == ll, i, jnp.where(l < ll, 0, nr - 1)), 0))

    spec = pl.BlockSpec((None, tm, cols), lambda l, i: (l, i, 0))
    out = jax.ShapeDtypeStruct(w.shape, F32)
    return pl.pallas_call(
        body, name=name, grid=(layers, nr),
        in_specs=[part_spec(ll, parts[ll].shape[0]) for ll in range(layers)] + [spec] * 3 + [pl.BlockSpec(memory_space=pl.ANY)],
        out_specs=[spec] * 4, out_shape=[out] * 4, scratch_shapes=[pltpu.VMEM((tm, cols), F32)],
        compiler_params=_params(("arbitrary", "arbitrary")),
    )(*parts, w, m, v, after)


def _pack(arrs):
    flat = jnp.concatenate([a.reshape(-1) for a in arrs])
    rows = -(-flat.shape[0] // (PACK_COLS * 16)) * 16
    return jnp.pad(flat, (0, rows * PACK_COLS - flat.shape[0])).reshape(rows, PACK_COLS)


def _unpack(packed, shapes):
    flat = packed.reshape(-1)
    out, off = [], 0
    for shp in shapes:
        n = int(np.prod(shp))
        out.append(flat[off:off + n].reshape(shp))
        off += n
    return out


def _shards_to_full(g, axis):
    if axis == 0:
        return g.reshape(g.shape[0] * g.shape[1], g.shape[2])
    return jnp.transpose(g, (1, 0, 2)).reshape(g.shape[1], g.shape[0] * g.shape[2])


def _full_to_shards(a, axis):
    if axis == 0:
        return jnp.transpose(a.reshape(4, 2, a.shape[0] // N_DEV, a.shape[1]), (1, 0, 2, 3))
    return jnp.transpose(a.reshape(a.shape[0], 4, 2, a.shape[1] // N_DEV), (2, 1, 0, 3))


def _zeros_like_cols(a, n):
    return jnp.zeros(a.shape[:-1] + (n,), a.dtype)


def _pad_heads(a, n):
    z = _zeros_like_cols(a, HEAD)
    return jnp.concatenate([p for h in range(n) for p in (a[..., h * HEAD:(h + 1) * HEAD], z)], axis=-1)


def _unpad_heads(a, n):
    return jnp.concatenate([a[..., h * LANES:h * LANES + HEAD] for h in range(n)], axis=-1)


def _seg(first, width, sign=1):
    return (width, [(first, sign)])


def _zero(width):
    return (width, [])


def _swapped(first):
    half = MLA_ROPE // 2
    return [_seg(first + half, half, -1), _seg(first, half)]


def _padded_heads(first, n):
    return [s for h in range(n) for s in (_seg(first + HEAD * h, HEAD), _zero(HEAD))]


def _layout_w_in():
    kr = 384
    return (_padded_heads(1440, 4) + [_seg(0, 256), _seg(416, 256), _seg(672, 256), _seg(928, 256), _seg(1184, 256)]
            + _padded_heads(1696, 2) + _padded_heads(1824, 2) + [_seg(256, 128)]
            + [_zero(HEAD), _seg(kr, MLA_ROPE), _seg(kr, MLA_ROPE)] + [_zero(HEAD)] + _swapped(kr) + _swapped(kr))


def _layout_w_uq():
    out = []
    for h in range(MLA_HEADS):
        out += [_seg(96 * h, MLA_NOPE), _seg(96 * h + MLA_NOPE, MLA_ROPE)] + _swapped(96 * h + MLA_NOPE)
    return out


def _layout_w_ukv():
    keys = [s for h in range(MLA_HEADS) for s in (_seg(LANES * h, HEAD), _zero(HEAD))]
    values = [s for h in range(MLA_HEADS) for s in (_seg(LANES * h + HEAD, HEAD), _zero(HEAD))]
    return keys + values


def _layout_w_gate_up():
    return [_seg(half + j, GU_TILE) for j in range(0, D_FF, GU_TILE) for half in (0, D_FF)]


LAYOUTS = dict(w_in=_layout_w_in(), w_uq=_layout_w_uq(), w_ukv=_layout_w_ukv(), w_gate_up=_layout_w_gate_up())
SWAPPED_UPDATE = ("w_gate_up",)
OWN_COLS = dict(w_in=1952, w_uq=384, w_ukv=512, w_gate_up=2 * D_FF)


def _plan_extend(layout, shard):
    plan = []
    for width, terms in layout:
        if not terms:
            plan.append((width, []))
            continue
        (first, sign), = terms
        while width:
            g, off = divmod(first, shard)
            w = min(width, shard - off)
            plan.append((w, [(g, off, sign)]))
            first, width = first + w, width - w
    return [plan]


def _plan_fold(layout, own_cols):
    sources = [[] for _ in range(own_cols)]
    e = 0
    for width, terms in layout:
        for first, sign in terms:
            for i in range(width):
                sources[first + i].append((e + i, sign))
        e += width
    shard = own_cols // N_DEV
    plans = {}
    for g in range(N_DEV):
        plan, n = [], g * shard
        while n < (g + 1) * shard:
            w = 1
            while n + w < (g + 1) * shard and [(c + w, sg) for c, sg in sources[n]] == sources[n + w]:
                w += 1
            plan.append((w, [(0, c, sg) for c, sg in sources[n]]))
            n += w
        plans[g] = plan
    return [plans[2 * p + c] for c in range(2) for p in range(4)]


def _assemble(src, plans, out_cols, out_dtype, *, name, transposed=False):
    g, rows, c = src.shape
    tm = _row_tile(rows, 256)
    pad = -out_cols % LANES if transposed else 0

    def body(s_ref, o_ref):
        blocks = [s_ref[i].astype(F32) for i in range(g)]
        for d, plan in enumerate(plans):
            pieces = []
            for width, terms in plan + ([(pad, [])] if pad else []):
                v = None
                for b, first, sign in terms:
                    t = blocks[b][:, first:first + width]
                    t = -t if sign < 0 else t
                    v = t if v is None else v + t
                pieces.append(jnp.zeros((tm, width), F32) if v is None else v)
            block = pieces[0] if len(pieces) == 1 else jnp.concatenate(pieces, axis=1)
            o_ref[d] = (block.T[:out_cols, :] if transposed else block).astype(o_ref.dtype)

    block, shape = ((out_cols, tm), (out_cols, rows)) if transposed else ((tm, out_cols), (rows, out_cols))
    return pl.pallas_call(
        body, name=name, grid=(rows // tm,), in_specs=[pl.BlockSpec((g, tm, c), lambda i: (0, i, 0))],
        out_specs=pl.BlockSpec((len(plans),) + block, (lambda i: (0, 0, i)) if transposed else (lambda i: (0, i, 0))),
        out_shape=jax.ShapeDtypeStruct((len(plans),) + shape, out_dtype), compiler_params=_params(("parallel",)),
    )(src)


def _extend(nm, gathered, *, name):
    layout = LAYOUTS[nm]
    return _assemble(gathered, _plan_extend(layout, OWN_COLS[nm] // N_DEV), sum(w for w, _ in layout), BF16, name=name)[0]


def _fold_to_shards(nm, grad_ext, *, name):
    shards = _assemble(grad_ext[None], _plan_fold(LAYOUTS[nm], OWN_COLS[nm]), OWN_COLS[nm] // N_DEV, F32, name=name,
                       transposed=nm in SWAPPED_UPDATE)
    return shards.reshape((2, 4) + shards.shape[1:])


def _rope_tables(s):
    inv = 1.0 / (ROPE_THETA ** (jnp.arange(0, MLA_ROPE, 2, dtype=F32) / MLA_ROPE))
    ang = jnp.arange(s, dtype=F32)[:, None] * inv[None, :]
    cos, sin = jnp.cos(ang), jnp.sin(ang)
    c32, s32 = jnp.concatenate([cos, cos], axis=1), jnp.concatenate([sin, sin], axis=1)
    zeros, ones = jnp.zeros((s, HEAD), F32), jnp.ones((s, HEAD), F32)
    tq = jnp.concatenate([ones, c32, s32], axis=1) * (1.0 / math.sqrt(MLA_NOPE + MLA_ROPE))
    return (jnp.tile(tq, (1, MLA_HEADS)), jnp.concatenate([zeros, c32, c32], axis=1),
            jnp.concatenate([zeros, s32, s32], axis=1))


def _gn(y):
    return y * _rstd(y, D_GROUP)


def _mixer_fwd(x, w, tabs, l, after_attention=None):
    s = x.shape[0]
    tq, tkc, tks = tabs
    n = lambda t: f"l{l}_{t}"
    h = _rms_fwd(("row", x, D_MODEL, 0), w["attn_norm"], name=n("attn_norm"), rows=s, width=D_MODEL)
    proj = _mm(h, w["w_in"], name=n("in_proj"))
    def prep(cq, ckv, kr, krp, gq, gkv, wuq, wukv, tqv, tc, ts):
        cqn = (cq * _rstd(cq, 256) * gq).astype(BF16)
        ckvn = (ckv * _rstd(ckv, 128) * gkv).astype(BF16)
        qe = jnp.dot(cqn, wuq, preferred_element_type=F32)
        kve = jnp.dot(ckvn, wukv, preferred_element_type=F32)
        kb = kr * tc + krp * ts
        kvv = kve[:, 512:]
        lane = lax.broadcasted_iota(jnp.int32, kvv.shape, 1) & (LANES - 1)
        v = jnp.where(lane == HEAD, 1.0, kvv)
        k = kve[:, :512] + jnp.tile(kb, (1, MLA_HEADS))
        return cqn, ckvn, qe * tqv, k, k, v, v

    cqn, ckvn, qm, km, kmt, vm, vmt = _rowwise(
        prep, name=n("mla_prep"), rows=s,
        ins=[("row", proj, 256, C_CQ[0] // 256), ("row", proj, 128, C_CKV[0] // 128),
             ("row", proj, 128, C_KR[0] // 128), ("row", proj, 128, C_KRP[0] // 128),
             ("full", w["mla_q_norm"]), ("full", w["mla_kv_norm"]), ("full", w["w_uq"]), ("full", w["w_ukv"]),
             ("row", tq, 512, 0), ("row", tkc, 128, 0), ("row", tks, 128, 0)],
        outs=[(256, BF16), (128, BF16), (512, BF16, MLA_HEADS), (512, BF16, MLA_HEADS), (512, BF16, MLA_HEADS, "T"),
              (512, BF16, MLA_HEADS), (512, BF16, MLA_HEADS, "T")])
    y_a, lse_row = _mla_fwd(qm, km, vmt, name=n("mla_fwd"), s=s, nh=MLA_HEADS)
    mix_norm = w["mix_norm"]
    if after_attention is not None:
        mix_norm = mix_norm + after_attention(y_a)[0, 0]
    y_b = _conv_fwd(proj, w["conv_w"], name=n("conv_fwd"), s=s)
    y_c = _pool_fwd(proj, w["pool_wbd"], w["pool_scale"], name=n("pool_fwd"), s=s)
    y_d, lse_d = _swa_fwd(proj, w["hp_swa"], name=n("swa_fwd"), s=s, scale=1.0 / math.sqrt(HEAD))

    def mix(ya, yb, yc, yd, mn):
        return (jnp.concatenate([_gn(_unpad_heads(ya, 4)), _gn(yb), _gn(yc), _gn(_unpad_heads(yd, 4))], axis=1) * mn,)

    mixed = _rowwise(mix, name=n("group_norm"), rows=s,
                     ins=[("heads", y_a), ("row", y_b, 256, 0), ("row", y_c, 256, 0), ("row", y_d, 512, 0),
                          ("full", mix_norm)], outs=[(D_MODEL, BF16)])[0]
    x1 = _mm(mixed, w["w_o"], res=x, name=n("out_proj"))
    saved = dict(x=x, h=h, proj=proj, cqn=cqn, ckvn=ckvn, qm=qm, km=km, kmt=kmt, vm=vm, y_a=y_a, lse_row=lse_row,
                 y_b=y_b, y_c=y_c, y_d=y_d, lse_d=lse_d, mixed=mixed)
    return x1, saved


def _ffn_fwd(x1, w, l):
    s = x1.shape[0]
    n = lambda t: f"l{l}_{t}"
    h2 = _rms_fwd(("row", x1, D_MODEL, 0), w["ffn_norm"], name=n("ffn_norm"), rows=s, width=D_MODEL)

    def swiglu(gu):
        g, u = gu[:, :GU_TILE], gu[:, GU_TILE:]
        return gu, g * jax.nn.sigmoid(g) * u

    gu, act = _mm(h2, w["w_gate_up"], tm=2048, tn=2 * GU_TILE, name=n("gate_up"),
                  epilogue=(swiglu, [], [(2 * D_FF, BF16), (D_FF, BF16)]))
    x2 = _mm(act, w["w_down"], res=x1, tk=D_FF // 2, name=n("down"))
    return x2, dict(x1=x1, h2=h2, gu=gu, act=act)


def _ffn_bwd_down(dx2, sv, w, l):
    n = lambda t: f"l{l}_{t}"

    def swiglu_bwd(da, gu):
        gt, u = gu[:, :GU_TILE].astype(F32), gu[:, GU_TILE:].astype(F32)
        sg = jax.nn.sigmoid(gt)
        return (jnp.concatenate([da * u * sg * (1.0 + gt * (1.0 - sg)), da * gt * sg], axis=1),)

    dgu = _mm(dx2[1], w["w_down"], tb=True, tm=2048, tn=GU_TILE, name=n("d_act"),
              epilogue=(swiglu_bwd, [sv["gu"]], [(2 * D_FF, BF16)]))[0]
    g = dict(w_down=_mm(sv["act"], dx2[1], ta=True, tm=D_FF // 2, name=n("dw_down")))
    return dgu, g


def _ffn_bwd_up(dx2, dgu, sv, w, l):
    s = dgu.shape[0]
    n = lambda t: f"l{l}_{t}"
    dh2 = _mm(dgu, w["w_gate_up"], tb=True, tk=D_FF // 2, name=n("d_h2"))
    g = dict(w_gate_up=_mm(sv["h2"], dgu, ta=True, tn=D_FF // 2, name=n("dw_gate_up")))
    dx1, dx1_b, g["ffn_norm"] = _rms_bwd(("row", sv["x1"], D_MODEL, 0), w["ffn_norm"], dh2, dx2[0],
                                         name=n("ffn_norm_bwd"), rows=s, width=D_MODEL, out_dtypes=(F32, BF16))
    return (dx1, dx1_b), g


def _mixer_bwd_out(dx1, sv, w, l):
    s = dx1[1].shape[0]
    n = lambda t: f"l{l}_{t}"
    dmixed = _mm(dx1[1], w["w_o"], tb=True, name=n("d_mixed"))
    g = dict(w_o=_mm(sv["mixed"], dx1[1], ta=True, name=n("dw_o")))

    def mix_bwd(dm, ya, yb, yc, yd, mn):
        outs, dmn = [], []
        for i, y in enumerate((_unpad_heads(ya, 4), yb, yc, _unpad_heads(yd, 4))):
            lo, hi = i * D_GROUP, (i + 1) * D_GROUP
            r = _rstd(y, D_GROUP)
            nrm = y * r
            dmg = dm[:, lo:hi]
            dn = dmg * mn[:, lo:hi]
            dy = r * (dn - nrm * (jnp.sum(dn * nrm, axis=-1, keepdims=True) * (1.0 / D_GROUP)))
            outs.append(_pad_heads(dy, 4) if i in (0, 3) else dy)
            dmn.append(jnp.sum(dmg * nrm, axis=0, keepdims=True))
        return (*outs, jnp.concatenate(dmn, axis=1))

    dy_a, dy_b, dy_c, dy_d, g["mix_norm"] = _rowwise(
        mix_bwd, name=n("group_norm_bwd"), rows=s,
        ins=[("row", dmixed, D_MODEL, 0), ("heads", sv["y_a"]), ("row", sv["y_b"], 256, 0),
             ("row", sv["y_c"], 256, 0), ("row", sv["y_d"], 512, 0), ("full", w["mix_norm"])],
        outs=[(512, F32, MLA_HEADS), (256, F32), (256, F32), (512, F32)], reds=[(1, D_MODEL)])
    return (dy_a, dy_b, dy_c, dy_d), g


def _mixer_bwd_in(dx1, dys, sv, w, tabs, l):
    s = dx1[0].shape[0]
    tq, tkc, tks = tabs
    n = lambda t: f"l{l}_{t}"
    dy_a, dy_b, dy_c, dy_d = dys
    g = {}

    proj = sv["proj"]
    dq_sw, dk_sw, dv_sw, dsink = _swa_bwd(proj, sv["y_d"], dy_d, sv["lse_d"], w["hp_swa"], name=n("swa_bwd"), s=s,
                                          scale=1.0 / math.sqrt(HEAD))
    g["swa_sinks"] = dsink[:, 0, 0]

    dqm, dkm, dvm = _mla_bwd(sv["qm"], sv["km"], sv["kmt"], sv["vm"], sv["y_a"], dy_a, sv["lse_row"], w["hp_swa"],
                             name=n("mla_bwd"), s=s, nh=MLA_HEADS)

    def rms_bwd(x, gv, dy, width):
        r = _rstd(x, width)
        dyg = dy * gv
        dx = r * dyg - x * (r * r * r) * (jnp.sum(dyg * x, axis=-1, keepdims=True) * (1.0 / width))
        return dx, jnp.sum(dy * x * r, axis=0, keepdims=True)

    def prep_bwd(dq, dk, dv, cq, ckv, gq, gkv, wuq, wukv, tqv, tc, ts):
        dkb = dk[:, 0:128] + dk[:, 128:256] + dk[:, 256:384] + dk[:, 384:512]
        dq_ext = (dq * tqv).astype(BF16)
        dkv_ext = jnp.concatenate([dk.astype(BF16), dv], axis=1)
        dcqn = lax.dot_general(dq_ext, wuq, NT_DIMS, preferred_element_type=F32)
        dckvn = lax.dot_general(dkv_ext, wukv, NT_DIMS, preferred_element_type=F32)
        dcq, dgq = rms_bwd(cq, gq, dcqn, 256)
        dckv, dgkv = rms_bwd(ckv, gkv, dckvn, 128)
        return dq_ext, dkv_ext, dkb * tc, dkb * ts, dcq, dckv, dgq, dgkv

    dq_ext, dkv_ext, dkr, dkrp, dcq, dckv, g["mla_q_norm"], g["mla_kv_norm"] = _rowwise(
        prep_bwd, name=n("mla_prep_bwd"), rows=s,
        ins=[("heads", dqm), ("heads", dkm), ("heads", dvm), ("row", proj, 256, C_CQ[0] // 256),
             ("row", proj, 128, C_CKV[0] // 128), ("full", w["mla_q_norm"]), ("full", w["mla_kv_norm"]),
             ("full", w["w_uq"]), ("full", w["w_ukv"]), ("row", tq, 512, 0), ("row", tkc, 128, 0), ("row", tks, 128, 0)],
        outs=[(512, BF16), (1024, BF16), (128, BF16), (128, BF16), (256, BF16), (128, BF16)],
        reds=[(1, 256), (1, 128)])
    g["w_uq"] = _mm(sv["cqn"], dq_ext, ta=True, name=n("dw_uq"))
    g["w_ukv"] = _mm(sv["ckvn"], dkv_ext, ta=True, name=n("dw_ukv"))

    dgb, dgc, duc, g["conv_w"] = _conv_bwd(dy_b, proj, w["conv_w"], name=n("conv_bwd"), s=s)
    dup, g["pool_wbd"], g["pool_scale"] = _pool_bwd(dy_c, proj, w["pool_wbd"], w["pool_scale"], name=n("pool_bwd"), s=s)

    dproj = jnp.concatenate([dq_sw, dcq, dgb, dgc, duc, dup, dk_sw.astype(BF16), dv_sw.astype(BF16), dckv, dkr, dkrp],
                            axis=1)
    g["w_in"] = _mm(sv["h"], dproj, ta=True, name=n("dw_in"))
    return dproj, g


def _mixer_bwd_norm(dx1, dproj, sv, w, l):
    n = lambda t: f"l{l}_{t}"
    dh = _mm(dproj, w["w_in"], tb=True, name=n("d_h"))
    dx0, dx0_b, dg = _rms_bwd(("row", sv["x"], D_MODEL, 0), w["attn_norm"], dh, dx1[0], name=n("attn_norm_bwd"),
                              rows=dh.shape[0], width=D_MODEL, out_dtypes=(F32, BF16))
    return (dx0, dx0_b), dict(attn_norm=dg)


def _loss_head(x, target, g, *, s):
    def fn(xv, tv, gv):
        r = _rstd(xv, D_MODEL)
        e = xv * r * gv - tv
        part = jnp.sum(jnp.sum(e * e, axis=1, keepdims=True), axis=0, keepdims=True) * (0.5 / D_MODEL)
        dy = e * (1.0 / D_MODEL)
        dyg = dy * gv
        dx = r * dyg - xv * (r * r * r) * (jnp.sum(dyg * xv, axis=-1, keepdims=True) * (1.0 / D_MODEL))
        return dx, dx, jnp.sum(dy * xv * r, axis=0, keepdims=True), jnp.broadcast_to(part, (1, LANES))

    return _rowwise(fn, name="loss_head", rows=s,
                    ins=[("row", x, D_MODEL, 0), ("row", target, D_MODEL, 0), ("full", g)],
                    outs=[(D_MODEL, F32), (D_MODEL, BF16)], reds=[(1, D_MODEL), (1, LANES)])


def _alibi_slopes(n):
    return np.asarray([2.0 ** (-8.0 * (i + 1) / n) for i in range(n)], dtype=np.float32)


MIXER_WEIGHTS = ("w_in", "w_uq", "w_ukv", "conv_w", "w_o")
FFN_WEIGHTS = ("w_gate_up", "w_down")


def _mixer_weights(full, rep, l):
    pw = rep["pool_w"][l]
    z = jnp.zeros((HEAD, HEAD), F32)
    wbd = jnp.stack([jnp.block([[pw[2 * j], z], [z, pw[2 * j + 1]]]) for j in range(2)])
    return dict(
        attn_norm=rep["attn_norm"][l][None], w_in=full["w_in"], mla_q_norm=rep["mla_q_norm"][l][None],
        w_uq=full["w_uq"], mla_kv_norm=rep["mla_kv_norm"][l][None], w_ukv=full["w_ukv"],
        conv_w=full["conv_w"], pool_wbd=wbd, pool_scale=rep["pool_scale"][l][None],
        mix_norm=rep["mix_norm"][l][None], w_o=full["w_o"],
        hp_swa=jnp.stack([jnp.asarray(_alibi_slopes(SWA_HEADS)), rep["swa_sinks"][l]], axis=1))


def _ffn_weights(full, rep, l):
    return dict(ffn_norm=rep["ffn_norm"][l][None], w_gate_up=full["w_gate_up"], w_down=full["w_down"])


def _small_grads(g):
    rows = ("attn_norm", "mla_q_norm", "mla_kv_norm", "pool_scale", "ffn_norm", "mix_norm")
    out = {nm: g[nm][0] for nm in rows if nm in g}
    if "swa_sinks" in g:
        out["swa_sinks"] = g["swa_sinks"]
    if "pool_wbd" in g:
        e = g["pool_wbd"]
        out["pool_w"] = jnp.stack([e[j // 2][HEAD * (j % 2):HEAD * (j % 2 + 1), HEAD * (j % 2):HEAD * (j % 2 + 1)]
                                   for j in range(4)])
    return out


def kernel(x, attn_norm, w_in, mla_q_norm, w_uq, mla_kv_norm, w_ukv, conv_w, pool_w, pool_scale, swa_sinks, mix_norm, w_o, ffn_norm, w_gate_up, w_down, final_norm, loss_target, m_attn_norm, m_w_in, m_mla_q_norm, m_w_uq, m_mla_kv_norm, m_w_ukv, m_conv_w, m_pool_w, m_pool_scale, m_swa_sinks, m_mix_norm, m_w_o, m_ffn_norm, m_w_gate_up, m_w_down, m_final_norm, v_attn_norm, v_w_in, v_mla_q_norm, v_w_uq, v_mla_kv_norm, v_w_ukv, v_conv_w, v_pool_w, v_pool_scale, v_swa_sinks, v_mix_norm, v_w_o, v_ffn_norm, v_w_gate_up, v_w_down, v_final_norm):
    given = dict(locals())
    sh_names = [nm for nm, _, _ in SHARDED]
    sh_axis = {nm: ax - 1 for nm, _, ax in SHARDED}
    rep_names = [nm for nm, _ in REPLICATED]
    rep_shapes = [shp for _, shp in REPLICATED]
    rep = {nm: given[nm] for nm in rep_names if nm != "loss"}
    me = 4 * lax.axis_index("x") + 2 * lax.axis_index("y") + lax.axis_index("c")
    my_chip = 2 * lax.axis_index("x") + lax.axis_index("y")
    core = lax.axis_index("c").astype(jnp.int32).reshape(1)

    def behind(token, a):
        return a + token[0, 0].astype(a.dtype)

    def wire(nm, l):
        if nm == "conv_w":
            return lax.bitcast_convert_type(given[nm][l], BF16).reshape(3, -1)
        return given[nm][l].astype(BF16)

    def whole(nm, g, tag):
        if nm in LAYOUTS:
            return _extend(nm, g, name=f"extend_{nm}_{tag}")
        if nm == "conv_w":
            g = lax.bitcast_convert_type(g.reshape(N_DEV, 3, -1, 2), F32)
        return _shards_to_full(g, sh_axis[nm])

    def near_start(names, l, after, tag):
        srcs = [wire(nm, l) for nm in names]
        return _start_copies(_plan_gather_near, 4, srcs, [(N_DEV,) + a.shape for a in srcs], after, name=f"start_gather_{tag}")

    def pass_on(handle, after, tag):
        srcs, lands = _wait_copies(_plan_gather_near, handle, after, name=f"wait_gather_{tag}")
        handle, token = _start_copies(_plan_gather_pass, 3, [], lands, [], name=f"start_pass_{tag}")
        return (srcs, handle), token

    def gathered(names, state, after, tag):
        srcs, handle = state
        _, lands = _wait_copies(_plan_gather_pass, handle, after, name=f"wait_pass_{tag}")
        return {nm: whole(nm, lax.dynamic_update_index_in_dim(land, src, me, 0), tag)
                for nm, src, land in zip(names, srcs, lands)}

    layer1 = MIXER_WEIGHTS + FFN_WEIGHTS
    got = _all_gather([wire(nm, 0) for nm in MIXER_WEIGHTS], name="gather_mixer0")
    full_m0 = {nm: whole(nm, g, "mixer0") for nm, g in zip(MIXER_WEIGHTS, got)}
    h_f0, tok = near_start(FFN_WEIGHTS, 0, [], "ffn0")
    h_l1, tok = near_start(layer1, 1, [tok], "layer1")

    xs, target = x[0], loss_target[0]
    s = xs.shape[0]
    tabs = _rope_tables(s)
    wm, wf, svm, svf = [None] * DEPTH, [None] * DEPTH, [None] * DEPTH, [None] * DEPTH
    wm[0] = _mixer_weights(full_m0, rep, 0)
    wm[0]["attn_norm"] = behind(tok, wm[0]["attn_norm"])
    passing = {}

    def pass_ffn0(y_a):
        passing["ffn0"], token = pass_on(h_f0, [y_a], "ffn0")
        return token

    x1, svm[0] = _mixer_fwd(xs, wm[0], tabs, 0, after_attention=pass_ffn0)
    wf[0] = _ffn_weights(gathered(FFN_WEIGHTS, passing["ffn0"], [x1], "ffn0"), rep, 0)
    passing["layer1"], tok = pass_on(h_l1, [x1], "layer1")
    wf[0]["ffn_norm"] = behind(tok, wf[0]["ffn_norm"])
    x2, svf[0] = _ffn_fwd(x1, wf[0], 0)
    full_1 = gathered(layer1, passing["layer1"], [x2], "layer1")
    wm[1], wf[1] = _mixer_weights(full_1, rep, 1), _ffn_weights(full_1, rep, 1)
    x1, svm[1] = _mixer_fwd(x2, wm[1], tabs, 1)
    x2, svf[1] = _ffn_fwd(x1, wf[1], 1)
    dx_f, dx_b, d_final, loss = _loss_head(x2, target, rep["final_norm"][None], s=s)
    dx = (dx_f, dx_b)

    parts = {}

    def reduce_start(grads, l, after, tag):
        names = [nm for nm in sh_names if nm in grads]
        mine = [_fold_to_shards(nm, grads[nm], name=f"fold_{nm}_{l}") if nm in LAYOUTS
                else _full_to_shards(grads[nm], sh_axis[nm]) for nm in names]
        handle, token = _start_copies(_plan_sibling, 1, mine, [m.shape[1:] for m in mine], after, name=f"start_sibling_{tag}")
        return (names, l, handle), token

    def reduce_mid(state, after, tag):
        names, l, handle = state
        mine, theirs = _wait_copies(_plan_sibling, handle, after, name=f"wait_sibling_{tag}")
        sums = [_add_own(g, o, core, name=f"chip_sum_{nm}_{l}") for nm, g, o in zip(names, mine, theirs)]
        handle, token = _start_copies(_plan_chips, 3, sums, [a.shape for a in sums], [], name=f"start_chips_{tag}")
        return (names, l, handle), token

    def reduce_end(state, after, tag):
        names, l, handle = state
        sums, lands = _wait_copies(_plan_chips, handle, after, name=f"wait_chips_{tag}")
        for nm, own, land in zip(names, sums, lands):
            parts[nm, l] = lax.dynamic_update_index_in_dim(land, lax.dynamic_index_in_dim(own, my_chip, 0, keepdims=False),
                                                           my_chip, 0)

    small = [None] * DEPTH
    in_flight = []
    pending = None
    for l in reversed(range(DEPTH)):
        dgu, g_down = _ffn_bwd_down(dx, svf[l], wf[l], l)
        if pending is not None:
            state, token = reduce_mid(pending, [dgu], f"mixer{l + 1}")
            in_flight.append((state, f"mixer{l + 1}"))
            wf[l]["ffn_norm"] = behind(token, wf[l]["ffn_norm"])
        dx1, g_up = _ffn_bwd_up(dx, dgu, svf[l], wf[l], l)
        g_ffn = {**g_down, **g_up}
        state, token = reduce_start(g_ffn, l, [], f"ffn{l}")
        wm[l]["mix_norm"] = behind(token, wm[l]["mix_norm"])
        dys, g_out = _mixer_bwd_out(dx1, svm[l], wm[l], l)
        state, token = reduce_mid(state, [dys[0]], f"ffn{l}")
        in_flight.append((state, f"ffn{l}"))
        wm[l]["hp_swa"] = behind(token, wm[l]["hp_swa"])
        dproj, g_in = _mixer_bwd_in(dx1, dys, svm[l], wm[l], tabs, l)
        g_mixer = {**g_out, **g_in}
        pending, token = reduce_start(g_mixer, l, [], f"mixer{l}")
        if l == 0:
            last, token = reduce_mid(pending, [dproj], "mixer0")
        wm[l]["attn_norm"] = behind(token, wm[l]["attn_norm"])
        dx, g_norm = _mixer_bwd_norm(dx1, dproj, svm[l], wm[l], l)
        small[l] = _small_grads({**g_ffn, **g_mixer, **g_norm})
    grads = {nm: jnp.stack([small[l][nm] for l in range(DEPTH)]) for nm in rep_names if nm in small[0]}
    grads["final_norm"] = d_final[0]
    grads["loss"] = loss[0, :1]
    zero = jnp.zeros((1,), F32)
    small = _all_gather([behind(token, _pack([grads[nm] for nm in rep_names]))], name="gather_small_grads")

    for state, tag in in_flight:
        reduce_end(state, [], tag)
    grad_x = dx[0]

    def adamw(nm, after):
        swap = (lambda a: jnp.swapaxes(a, 1, 2)) if nm in SWAPPED_UPDATE else (lambda a: a)
        res = _adamw([parts[nm, l] for l in range(DEPTH)], swap(given[nm]), swap(given["m_" + nm]), swap(given["v_" + nm]),
                     after, name=f"adamw_{nm}")
        return [swap(a) for a in res]

    sh_out = {nm: adamw(nm, token) for nm in FFN_WEIGHTS}
    packs = [_pack([given.get(pre + nm, zero) for nm in rep_names])[None] for pre in ("", "m_", "v_")]
    rep_res = _adamw(small, *packs, token, name="adamw_replicated")
    rep_out = [dict(zip(rep_names, _unpack(o[0], rep_shapes))) for o in rep_res]

    reduce_end(last, [rep_res[0], sh_out["w_down"][0]], "mixer0")
    sh_out.update({nm: adamw(nm, token) for nm in MIXER_WEIGHTS})

    out = [rep_out[0]["loss"][0], grad_x[None]]
    for i in range(4):
        out += [sh_out[nm][i] if nm in sh_axis else rep_out[i][nm] for nm in WEIGHT_ORDER]
    return tuple(out)
```

```python
import functools
import math

import numpy as np
import jax
import jax.numpy as jnp
from jax import lax
from jax.experimental import pallas as pl
from jax.experimental.pallas import tpu as pltpu

F32 = jnp.float32
BF16 = jnp.bfloat16

D_MODEL = 1024
DEPTH = 2
D_GROUP = 256
MLA_HEADS = 4
MLA_NOPE = 64
MLA_ROPE = 32
ROPE_THETA = 10000.0
POOL_WINDOWS = (2, 4, 8, 16)
SWA_HEADS = 4
SWA_KV_HEADS = 2
SWA_WINDOW = 128
D_FF = 2816
GU_TILE = 256
RMS_EPS = 1e-6
LANES = 128
HEAD = 64
VMEM_LIMIT = 48 * 1024 * 1024
NEG = -1e30

ADAM_LR = 0.001
ADAM_B1 = 0.9
ADAM_B2 = 0.999
ADAM_EPS = 1e-08
ADAM_WD = 0.01
ADAM_STEP = 10

N_DEV = 8
PACK_COLS = 1024

C_QSW, C_CQ, C_GB, C_GC, C_UCONV, C_UPOOL = (0, 512), (512, 256), (768, 256), (1024, 256), (1280, 256), (1536, 256)
C_KSW, C_VSW, C_CKV, C_KR, C_KRP = (1792, 256), (2048, 256), (2304, 128), (2432, 128), (2560, 128)

SHARDED = (("w_in", (DEPTH, 1024, 244), 2), ("w_uq", (DEPTH, 256, 48), 2), ("w_ukv", (DEPTH, 128, 64), 2),
           ("conv_w", (DEPTH, 3, 32), 2), ("w_o", (DEPTH, 128, 1024), 1), ("w_gate_up", (DEPTH, 1024, 704), 2),
           ("w_down", (DEPTH, 352, 1024), 1))
REPLICATED = (("attn_norm", (DEPTH, 1024)), ("mla_q_norm", (DEPTH, 256)), ("mla_kv_norm", (DEPTH, 128)),
              ("pool_w", (DEPTH, 4, 64, 64)), ("pool_scale", (DEPTH, 256)), ("swa_sinks", (DEPTH, 4)),
              ("mix_norm", (DEPTH, 1024)), ("ffn_norm", (DEPTH, 1024)), ("final_norm", (1024,)), ("loss", (1,)))
WEIGHT_ORDER = ("attn_norm", "w_in", "mla_q_norm", "w_uq", "mla_kv_norm", "w_ukv", "conv_w", "pool_w", "pool_scale",
                "swa_sinks", "mix_norm", "w_o", "ffn_norm", "w_gate_up", "w_down", "final_norm")


def _params(sem):
    return pltpu.CompilerParams(dimension_semantics=sem, vmem_limit_bytes=VMEM_LIMIT)


def _pick(dim, target):
    if dim <= target:
        return dim
    best = None
    for t in range(LANES, target + 1, LANES):
        if dim % t == 0:
            best = t
    assert best is not None, (dim, target)
    return best


def _mm(a, b, *, name, ta=False, tb=False, res=None, out_dtype=F32, tm=1024, tn=1024, tk=1024, epilogue=None):
    m, k = (a.shape[1], a.shape[0]) if ta else a.shape
    n = b.shape[0] if tb else b.shape[1]
    assert (b.shape[1] if tb else b.shape[0]) == k
    tm, tn, tk = _pick(m, tm), _pick(n, tn), _pick(k, tk)
    nj, nk = n // tn, k // tk
    dims = (((0 if ta else 1,), (1 if tb else 0,)), ((), ()))
    fn, extra, outs = epilogue if epilogue is not None else (None, [], [(n, out_dtype)])
    if res is not None:
        assert epilogue is None
        fn, extra = (lambda acc, r: (acc + r,)), [res]
    n_in, n_out = 2 + len(extra), len(outs)

    def body(*refs):
        a_ref, b_ref, acc_ref = refs[0], refs[1], refs[-1]
        kk = pl.program_id(2)

        def product():
            return lax.dot_general(a_ref[...].astype(BF16), b_ref[...].astype(BF16), dims, preferred_element_type=F32)

        def finish(acc):
            tiles = (acc,) if fn is None else fn(acc, *[r[...] for r in refs[2:n_in]])
            for o_ref, tile in zip(refs[n_in:n_in + n_out], tiles):
                o_ref[...] = tile.astype(o_ref.dtype)

        if nk == 1:
            finish(product())
            return

        @pl.when(kk == 0)
        def _():
            acc_ref[...] = jnp.zeros_like(acc_ref)

        acc_ref[...] += product()

        @pl.when(kk == nk - 1)
        def _():
            finish(acc_ref[...])

    def col_tiles(width):
        assert width % (nj * LANES) == 0, (width, nj)
        return pl.BlockSpec((tm, width // nj), lambda i, j, kk: (i, j))

    a_spec = pl.BlockSpec((tk, tm), lambda i, j, kk: (kk, i)) if ta else pl.BlockSpec((tm, tk), lambda i, j, kk: (i, kk))
    b_spec = pl.BlockSpec((tn, tk), lambda i, j, kk: (j, kk)) if tb else pl.BlockSpec((tk, tn), lambda i, j, kk: (kk, j))
    res_ = pl.pallas_call(
        body, name=name, grid=(m // tm, nj, nk), in_specs=[a_spec, b_spec] + [col_tiles(e.shape[1]) for e in extra],
        out_specs=[col_tiles(w) for w, _ in outs],
        out_shape=[jax.ShapeDtypeStruct((m, w), dt) for w, dt in outs],
        scratch_shapes=[pltpu.VMEM((tm, tn), F32)] if nk > 1 else [],
        compiler_params=_params(("parallel", "parallel", "arbitrary")),
    )(a, b, *extra)
    return res_[0] if epilogue is None else res_


def _rowwise(fn, *, name, rows, ins, outs, reds=(), tm=512):
    tm = min(tm, rows)
    assert rows % tm == 0
    n_in, n_out = len(ins), len(outs)

    def body(*refs):
        vals = [jnp.concatenate([r[h] for h in range(r.shape[0])], axis=1) if spec[0] == "heads" else r[...]
                for spec, r in zip(ins, refs[:n_in])]
        res = fn(*vals)
        for out, r, v in zip(outs, refs[n_in:n_in + n_out], res[:n_out]):
            if len(out) >= 3:
                for h in range(out[2]):
                    piece = v[:, h * LANES:(h + 1) * LANES]
                    r[h] = (piece.T if len(out) == 4 else piece).astype(r.dtype)
            else:
                r[...] = v.astype(r.dtype)
        if reds:
            @pl.when(pl.program_id(0) == 0)
            def _():
                for r in refs[n_in + n_out:]:
                    r[...] = jnp.zeros_like(r)

            for r, v in zip(refs[n_in + n_out:], res[n_out:]):
                r[...] += v

    in_specs, args = [], []
    for spec in ins:
        if spec[0] == "row":
            _, arr, width, blk = spec
            in_specs.append(pl.BlockSpec((tm, width), functools.partial(lambda i, blk: (i, blk), blk=blk)))
        elif spec[0] == "heads":
            arr = spec[1]
            in_specs.append(pl.BlockSpec((arr.shape[0], tm, LANES), lambda i: (0, i, 0)))
        else:
            arr = spec[1]
            in_specs.append(pl.BlockSpec(arr.shape, functools.partial(lambda i, nd: (0,) * nd, nd=arr.ndim)))
        args.append(arr)
    def out_spec(o):
        if len(o) == 4:
            return pl.BlockSpec((o[2], LANES, tm), lambda i: (0, 0, i)), (o[2], LANES, rows)
        if len(o) == 3:
            return pl.BlockSpec((o[2], tm, LANES), lambda i: (0, i, 0)), (o[2], rows, LANES)
        return pl.BlockSpec((tm, o[0]), lambda i: (i, 0)), (rows, o[0])

    out_specs = [out_spec(o)[0] for o in outs]
    out_shape = [jax.ShapeDtypeStruct(out_spec(o)[1], o[1]) for o in outs]
    out_specs += [pl.BlockSpec((r, w), lambda i: (0, 0)) for r, w in reds]
    out_shape += [jax.ShapeDtypeStruct((r, w), F32) for r, w in reds]
    return pl.pallas_call(body, name=name, grid=(rows // tm,), in_specs=in_specs, out_specs=out_specs,
                          out_shape=out_shape, compiler_params=_params(("arbitrary",)))(*args)


def _rstd(x, n):
    return lax.rsqrt(jnp.sum(x * x, axis=-1, keepdims=True) * (1.0 / n) + RMS_EPS)


def _rms_fwd(x_spec, g, *, name, rows, width):
    def fn(x, gv):
        return (x * _rstd(x, width) * gv,)
    return _rowwise(fn, name=name, rows=rows, ins=[x_spec, ("full", g)], outs=[(width, BF16)])[0]


def _rms_bwd(x_spec, g, dy, res, *, name, rows, width, out_dtypes):
    def fn(x, gv, dyv, *rest):
        r = _rstd(x, width)
        dyg = dyv * gv
        dx = r * dyg - x * (r * r * r) * (jnp.sum(dyg * x, axis=-1, keepdims=True) * (1.0 / width))
        if rest:
            dx = dx + rest[0]
        return (dx,) * len(out_dtypes) + (jnp.sum(dyv * x * r, axis=0, keepdims=True),)

    ins = [x_spec, ("full", g), ("row", dy, width, 0)]
    if res is not None:
        ins.append(("row", res, width, 0))
    return _rowwise(fn, name=name, rows=rows, ins=ins, outs=[(width, dt) for dt in out_dtypes], reds=[(1, width)])


NT_DIMS = (((1,), (1,)), ((), ()))
TN_DIMS = (((0,), (0,)), ((), ()))
BNT_DIMS = (((2,), (2,)), ((0,), (0,)))
BNN_DIMS = (((2,), (1,)), ((0,), (0,)))


def _mla_tile(s):
    return min(512, s)


def _block_tables(pairs):
    return jnp.asarray([a for a, _ in pairs], jnp.int32), jnp.asarray([b for _, b in pairs], jnp.int32)


def _causal(shape, query_axis):
    return lax.broadcasted_iota(jnp.int32, shape, query_axis) >= lax.broadcasted_iota(jnp.int32, shape, 1 - query_axis)


def _mla_fwd(qa, ka, vta, *, name, s, nh):
    t = _mla_tile(s)
    nq = s // t

    pairs = [(i, j) for i in range(nq) for j in range(i + 1)]

    def body(qb_ref, kb_ref, q_ref, k_ref, vt_ref, o_ref, lse_ref, m_s, acc_s):
        i, j = qb_ref[pl.program_id(0)], kb_ref[pl.program_id(0)]

        @pl.when(j == 0)
        def _():
            m_s[...] = jnp.full_like(m_s, NEG)
            acc_s[...] = jnp.zeros_like(acc_s)

        def step(diag):
            sc = lax.dot_general(k_ref[...], q_ref[...], BNT_DIMS, preferred_element_type=F32)
            if diag:
                sc = jnp.where(_causal(sc.shape[1:], 1)[None], sc, NEG)
            m_prev = m_s[...]
            m_new = jnp.maximum(m_prev, jnp.max(sc, axis=1, keepdims=True))
            p = jnp.exp(sc - m_new).astype(BF16)
            acc_s[...] = (jnp.exp(m_prev - m_new) * acc_s[...]
                          + lax.dot_general(vt_ref[...], p, BNN_DIMS, preferred_element_type=F32))
            m_s[...] = m_new

        pl.when(j < i)(functools.partial(step, False))
        pl.when(j == i)(functools.partial(step, True))

        @pl.when(j == i)
        def _():
            row = lax.broadcasted_iota(jnp.int32, (LANES, t), 0)
            for h in range(nh):
                acc = acc_s[h]
                l = acc[HEAD:HEAD + 1, :]
                o_ref[h] = jnp.where(row < HEAD, acc / l, 0.0).T
                lse_ref[h] = m_s[h] + jnp.log(l)

    q_spec = pl.BlockSpec((nh, t, LANES), lambda p, qb, kb: (0, qb[p], 0))
    k_spec = pl.BlockSpec((nh, t, LANES), lambda p, qb, kb: (0, kb[p], 0))
    vt_spec = pl.BlockSpec((nh, LANES, t), lambda p, qb, kb: (0, 0, kb[p]))
    return pl.pallas_call(
        body, name=name,
        grid_spec=pltpu.PrefetchScalarGridSpec(
            num_scalar_prefetch=2, grid=(len(pairs),), in_specs=[q_spec, k_spec, vt_spec],
            out_specs=[q_spec, pl.BlockSpec((nh, 1, t), lambda p, qb, kb: (0, 0, qb[p]))],
            scratch_shapes=[pltpu.VMEM((nh, 1, t), F32), pltpu.VMEM((nh, LANES, t), F32)]),
        out_shape=[jax.ShapeDtypeStruct((nh, s, LANES), F32), jax.ShapeDtypeStruct((nh, 1, s), F32)],
        compiler_params=_params(("arbitrary",)),
    )(*_block_tables(pairs), qa, ka, vta)


def _mla_bwd(qa, ka, kta, va, o, do, lse_row, after, *, name, s, nh):
    t = _mla_tile(s)
    nq = s // t

    pairs = [(kb, j) for kb in range(nq) for j in range(kb, nq)]

    def body(kb_ref, qb_ref, q_ref, k_ref, kt_ref, v_ref, o_ref, do_ref, lse_ref, after_ref, dq_hbm, dk_ref, dv_ref,
             dqt_s, dk_s, dv_s, d_s, stage, sem):
        kb, j = kb_ref[pl.program_id(0)], qb_ref[pl.program_id(0)]
        cols = pl.ds(pl.multiple_of(j * t, t), t)

        @pl.when(j == kb)
        def _():
            dk_s[...] = jnp.zeros_like(dk_s)
            dv_s[...] = jnp.zeros_like(dv_s)

        @pl.when(kb == 0)
        def _():
            dqt_s[:, :, cols] = jnp.zeros((nh, LANES, t), F32)
            for h in range(nh):
                d_col = jnp.sum(do_ref[h] * o_ref[h], axis=1, keepdims=True)
                d_s[h, :, cols] = jnp.broadcast_to(d_col, (t, LANES)).T[0:1, :]

        def step(diag):
            q = q_ref[...]
            do_b = do_ref[...].astype(BF16)
            sc = lax.dot_general(k_ref[...], q, BNT_DIMS, preferred_element_type=F32)
            if diag:
                sc = jnp.where(_causal(sc.shape[1:], 1)[None], sc, NEG)
            p = jnp.exp(sc - lse_ref[...])
            dv_s[...] += lax.dot_general(p.astype(BF16), do_b, BNN_DIMS, preferred_element_type=F32)
            dp = lax.dot_general(v_ref[...], do_b, BNT_DIMS, preferred_element_type=F32)
            ds = (p * (dp - d_s[:, :, cols])).astype(BF16)
            dk_s[...] += lax.dot_general(ds, q, BNN_DIMS, preferred_element_type=F32)
            dqt_s[:, :, cols] += lax.dot_general(kt_ref[...], ds, BNN_DIMS, preferred_element_type=F32)

        pl.when(j > kb)(functools.partial(step, False))
        pl.when(j == kb)(functools.partial(step, True))

        @pl.when(j == kb)
        def _():
            slot = kb % 2

            def write_out(sl):
                return pltpu.make_async_copy(stage.at[sl], dq_hbm.at[:, cols, :], sem.at[sl])

            @pl.when(kb >= 2)
            def _():
                write_out(slot).wait()

            for h in range(nh):
                stage[slot, h] = dqt_s[h, :, cols].T
            write_out(slot).start()

            @pl.when(kb == nq - 1)
            def _():
                write_out(slot).wait()
                if nq >= 2:
                    write_out(1 - slot).wait()

        @pl.when(j == nq - 1)
        def _():
            dk_ref[...] = dk_s[...]
            dv_ref[...] = dv_s[...].astype(dv_ref.dtype)

    q_spec = pl.BlockSpec((nh, t, LANES), lambda p, kb, qb: (0, qb[p], 0))
    kv_spec = pl.BlockSpec((nh, t, LANES), lambda p, kb, qb: (0, kb[p], 0))
    kt_spec = pl.BlockSpec((nh, LANES, t), lambda p, kb, qb: (0, 0, kb[p]))
    row_spec = pl.BlockSpec((nh, 1, t), lambda p, kb, qb: (0, 0, qb[p]))
    whole = jax.ShapeDtypeStruct((nh, s, LANES), F32)
    return pl.pallas_call(
        body, name=name,
        grid_spec=pltpu.PrefetchScalarGridSpec(
            num_scalar_prefetch=2, grid=(len(pairs),),
            in_specs=[q_spec, kv_spec, kt_spec, kv_spec, q_spec, q_spec, row_spec, pl.BlockSpec(memory_space=pl.ANY)],
            out_specs=[pl.BlockSpec(memory_space=pl.ANY), kv_spec, kv_spec],
            scratch_shapes=[pltpu.VMEM((nh, LANES, s), F32), pltpu.VMEM((nh, t, LANES), F32),
                            pltpu.VMEM((nh, t, LANES), F32), pltpu.VMEM((nh, 1, s), F32),
                            pltpu.VMEM((2, nh, t, LANES), F32), pltpu.SemaphoreType.DMA((2,))]),
        out_shape=[whole, whole, jax.ShapeDtypeStruct((nh, s, LANES), BF16)],
        compiler_params=_params(("arbitrary",)),
    )(*_block_tables(pairs), qa, ka, kta, va, o, do, lse_row, after)


SWA_PIECE = 128
SWA_KEYS = 2 * SWA_PIECE


def _swa_block(s):
    return min(512, s)


def _swa_piece(hp_ref, h, q, k_ref, v_ref, qpos0, scale):
    kstart = pl.multiple_of(jnp.maximum(qpos0 - SWA_PIECE, 0), SWA_PIECE)
    k = k_ref[pl.ds(kstart, SWA_KEYS), :].astype(BF16)
    v = v_ref[pl.ds(kstart, SWA_KEYS), :].astype(BF16)
    sc = lax.dot_general(q, k, NT_DIMS, preferred_element_type=F32)
    dist = (qpos0 + lax.broadcasted_iota(jnp.int32, sc.shape, 0)) - (kstart + lax.broadcasted_iota(jnp.int32, sc.shape, 1))
    sc = sc * scale - hp_ref[h, 0] * dist.astype(F32)
    sc = jnp.where((dist >= 0) & (dist < SWA_WINDOW), sc, NEG)
    return kstart, k, v, sc


def _swa_fwd(proj, hp, *, name, s, scale):
    tb = _swa_block(s)
    group = SWA_HEADS // SWA_KV_HEADS
    q_off, k_off, v_off = C_QSW[0] // LANES, C_KSW[0] // LANES, C_VSW[0] // LANES

    def body(hp_ref, q_ref, k_ref, v_ref, o_ref, lse_ref):
        h, i = pl.program_id(0), pl.program_id(1)
        sink = hp_ref[h, 1]
        for r in range(0, tb, SWA_PIECE):
            rows = pl.ds(r, SWA_PIECE)
            _, _, v, sc = _swa_piece(hp_ref, h, q_ref[rows, :].astype(BF16), k_ref, v_ref, i * tb + r, scale)
            m = jnp.maximum(jnp.max(sc, axis=1, keepdims=True), sink)
            p = jnp.exp(sc - m)
            l = jnp.sum(p, axis=1, keepdims=True) + jnp.exp(sink - m)
            o_ref[rows, :] = jnp.dot(p.astype(BF16), v, preferred_element_type=F32) / l
            lse_ref[rows, :] = m + jnp.log(l)

    whole = lambda off: pl.BlockSpec((s, LANES), lambda h, i: (0, off + h // group))
    return pl.pallas_call(
        body, name=name, grid=(SWA_HEADS, s // tb),
        in_specs=[pl.BlockSpec(memory_space=pltpu.SMEM), pl.BlockSpec((tb, LANES), lambda h, i: (i, q_off + h)),
                  whole(k_off), whole(v_off)],
        out_specs=[pl.BlockSpec((tb, LANES), lambda h, i: (i, h)), pl.BlockSpec((None, tb, 1), lambda h, i: (h, i, 0))],
        out_shape=[jax.ShapeDtypeStruct((s, SWA_HEADS * LANES), F32), jax.ShapeDtypeStruct((SWA_HEADS, s, 1), F32)],
        compiler_params=_params(("parallel", "parallel")),
    )(hp, proj, proj, proj)


def _swa_bwd(proj, o, do, lse, hp, *, name, s, scale):
    tb = _swa_block(s)
    nqb = s // tb
    group = SWA_HEADS // SWA_KV_HEADS
    q_off, k_off, v_off = C_QSW[0] // LANES, C_KSW[0] // LANES, C_VSW[0] // LANES

    def body(hp_ref, q_ref, k_ref, v_ref, o_ref, do_ref, lse_ref, dq_ref, dk_ref, dv_ref, dsink_ref):
        kh, g, i = pl.program_id(0), pl.program_id(1), pl.program_id(2)
        h = kh * group + g
        sink = hp_ref[h, 1]

        @pl.when((g == 0) & (i == 0))
        def _():
            dk_ref[...] = jnp.zeros_like(dk_ref)
            dv_ref[...] = jnp.zeros_like(dv_ref)

        @pl.when(i == 0)
        def _():
            dsink_ref[...] = jnp.zeros_like(dsink_ref)

        for r in range(0, tb, SWA_PIECE):
            rows = pl.ds(r, SWA_PIECE)
            q = q_ref[rows, :].astype(BF16)
            dov = do_ref[rows, :]
            do_b = dov.astype(BF16)
            lse_r = lse_ref[rows, :]
            d_r = jnp.sum(dov * o_ref[rows, :], axis=1, keepdims=True)
            kstart, k, v, sc = _swa_piece(hp_ref, h, q, k_ref, v_ref, i * tb + r, scale)
            p = jnp.exp(sc - lse_r)
            dp = lax.dot_general(do_b, v, NT_DIMS, preferred_element_type=F32)
            ds = (p * (dp - d_r)).astype(BF16)
            dq_ref[rows, :] = (jnp.dot(ds, k, preferred_element_type=F32) * scale).astype(dq_ref.dtype)
            win = pl.ds(kstart, SWA_KEYS)
            dk_ref[win, :] += lax.dot_general(ds, q, TN_DIMS, preferred_element_type=F32) * scale
            dv_ref[win, :] += lax.dot_general(p.astype(BF16), do_b, TN_DIMS, preferred_element_type=F32)
            part = jnp.sum(-jnp.exp(sink - lse_r) * d_r, axis=0, keepdims=True)
            dsink_ref[...] += jnp.broadcast_to(part, (1, LANES))

    whole = lambda off: pl.BlockSpec((s, LANES), lambda kh, g, i: (0, off + kh))
    q_map = lambda kh, g, i: (i, kh * group + g)
    return pl.pallas_call(
        body, name=name, grid=(SWA_KV_HEADS, group, nqb),
        in_specs=[pl.BlockSpec(memory_space=pltpu.SMEM),
                  pl.BlockSpec((tb, LANES), lambda kh, g, i: (i, q_off + kh * group + g)), whole(k_off), whole(v_off),
                  pl.BlockSpec((tb, LANES), q_map), pl.BlockSpec((tb, LANES), q_map),
                  pl.BlockSpec((None, tb, 1), lambda kh, g, i: (kh * group + g, i, 0))],
        out_specs=[pl.BlockSpec((tb, LANES), q_map), whole(0), whole(0),
                   pl.BlockSpec((None, 1, LANES), lambda kh, g, i: (kh * group + g, 0, 0))],
        out_shape=[jax.ShapeDtypeStruct((s, SWA_HEADS * LANES), BF16),
                   jax.ShapeDtypeStruct((s, SWA_KV_HEADS * LANES), F32), jax.ShapeDtypeStruct((s, SWA_KV_HEADS * LANES), F32),
                   jax.ShapeDtypeStruct((SWA_HEADS, 1, LANES), F32)],
        compiler_params=_params(("parallel", "arbitrary", "arbitrary")),
    )(hp, proj, proj, proj, o, do, lse)


def _shift_down(z, k):
    rows = lax.broadcasted_iota(jnp.int32, z.shape, 0)
    return jnp.where(rows >= k, pltpu.roll(z, k, 0), 0.0)


def _shift_up(z, k):
    n = z.shape[0]
    rows = lax.broadcasted_iota(jnp.int32, z.shape, 0)
    return jnp.where(rows < n - k, pltpu.roll(z, n - k, 0), 0.0)


def _rows3(a, b, c):
    r = lax.broadcasted_iota(jnp.int32, (3, a.shape[1]), 0)
    return jnp.where(r == 0, a, jnp.where(r == 1, b, c))


def _col_spec(s, off):
    return pl.BlockSpec((s, LANES), functools.partial(lambda j, off: (0, off + j), off=off))


def _conv_fwd(proj, conv_w, *, name, s):
    def body(gb_ref, gc_ref, u_ref, w_ref, y_ref):
        w0, w1, w2 = w_ref[0:1, :], w_ref[1:2, :], w_ref[2:3, :]
        z = gc_ref[...] * u_ref[...]
        c = w2 * z + w1 * _shift_down(z, 1) + w0 * _shift_down(z, 2)
        y_ref[...] = gb_ref[...] * c

    return pl.pallas_call(
        body, name=name, grid=(2,),
        in_specs=[_col_spec(s, C_GB[0] // LANES), _col_spec(s, C_GC[0] // LANES), _col_spec(s, C_UCONV[0] // LANES),
                  pl.BlockSpec((3, LANES), lambda j: (0, j))],
        out_specs=_col_spec(s, 0), out_shape=jax.ShapeDtypeStruct((s, D_GROUP), F32),
        compiler_params=_params(("parallel",)),
    )(proj, proj, proj, conv_w)


def _conv_bwd(dy, proj, conv_w, *, name, s):
    def body(dy_ref, gb_ref, gc_ref, u_ref, w_ref, dgb_ref, dgc_ref, du_ref, dw_ref):
        w0, w1, w2 = w_ref[0:1, :], w_ref[1:2, :], w_ref[2:3, :]
        gc, u, dyv = gc_ref[...], u_ref[...], dy_ref[...]
        z = gc * u
        z1, z2 = _shift_down(z, 1), _shift_down(z, 2)
        c = w2 * z + w1 * z1 + w0 * z2
        dgb_ref[...] = (dyv * c).astype(dgb_ref.dtype)
        dc = dyv * gb_ref[...]
        dz = w2 * dc + w1 * _shift_up(dc, 1) + w0 * _shift_up(dc, 2)
        dgc_ref[...] = (dz * u).astype(dgc_ref.dtype)
        du_ref[...] = (dz * gc).astype(du_ref.dtype)
        dw_ref[...] = _rows3(jnp.sum(dc * z2, axis=0, keepdims=True), jnp.sum(dc * z1, axis=0, keepdims=True),
                             jnp.sum(dc * z, axis=0, keepdims=True))

    act = jax.ShapeDtypeStruct((s, D_GROUP), BF16)
    return pl.pallas_call(
        body, name=name, grid=(2,),
        in_specs=[_col_spec(s, 0), _col_spec(s, C_GB[0] // LANES), _col_spec(s, C_GC[0] // LANES),
                  _col_spec(s, C_UCONV[0] // LANES), pl.BlockSpec((3, LANES), lambda j: (0, j))],
        out_specs=[_col_spec(s, 0), _col_spec(s, 0), _col_spec(s, 0), pl.BlockSpec((3, LANES), lambda j: (0, j))],
        out_shape=[act, act, act, jax.ShapeDtypeStruct((3, D_GROUP), F32)],
        compiler_params=_params(("parallel",)),
    )(dy, proj, proj, proj, conv_w)


def _pool_select(j, lane, a2, a4, a8, a16):
    lo = lane < HEAD
    return jnp.where(j == 0, jnp.where(lo, a2, a4), jnp.where(lo, a8, a16))


def _pooled(u, j):
    s2 = u + _shift_down(u, 1)
    s4 = s2 + _shift_down(s2, 2)
    s8 = s4 + _shift_down(s4, 4)
    s16 = s8 + _shift_down(s8, 8)
    lane = lax.broadcasted_iota(jnp.int32, u.shape, 1)
    rows = lax.broadcasted_iota(jnp.int32, u.shape, 0)
    win = _pool_select(j, lane, *POOL_WINDOWS)
    count = jnp.minimum(rows + 1, win).astype(F32)
    return _pool_select(j, lane, s2, s4, s8, s16) / count - u, count


def _pool_fwd(proj, wbd, scale, *, name, s):
    def body(u_ref, w_ref, sc_ref, y_ref):
        pooled, _ = _pooled(u_ref[...], pl.program_id(0))
        y_ref[...] = jnp.dot(pooled.astype(BF16), w_ref[...].astype(BF16), preferred_element_type=F32) * sc_ref[...]

    return pl.pallas_call(
        body, name=name, grid=(2,),
        in_specs=[_col_spec(s, C_UPOOL[0] // LANES), pl.BlockSpec((None, LANES, LANES), lambda j: (j, 0, 0)),
                  pl.BlockSpec((1, LANES), lambda j: (0, j))],
        out_specs=_col_spec(s, 0), out_shape=jax.ShapeDtypeStruct((s, D_GROUP), F32),
        compiler_params=_params(("parallel",)),
    )(proj, wbd, scale)


def _pool_bwd(dy, proj, wbd, scale, *, name, s):
    def body(dy_ref, u_ref, w_ref, sc_ref, du_ref, dw_ref, dsc_ref):
        j = pl.program_id(0)
        pooled, count = _pooled(u_ref[...], j)
        pooled_b = pooled.astype(BF16)
        w_b = w_ref[...].astype(BF16)
        dyv = dy_ref[...]
        mixed = jnp.dot(pooled_b, w_b, preferred_element_type=F32)
        dsc_ref[...] = jnp.sum(dyv * mixed, axis=0, keepdims=True)
        dms = (dyv * sc_ref[...]).astype(BF16)
        dw_ref[...] = lax.dot_general(pooled_b, dms, (((0,), (0,)), ((), ())), preferred_element_type=F32)
        dpooled = lax.dot_general(dms, w_b, (((1,), (1,)), ((), ())), preferred_element_type=F32)
        r = dpooled / count
        a2 = r + _shift_up(r, 1)
        a4 = a2 + _shift_up(a2, 2)
        a8 = a4 + _shift_up(a4, 4)
        a16 = a8 + _shift_up(a8, 8)
        lane = lax.broadcasted_iota(jnp.int32, r.shape, 1)
        du_ref[...] = (_pool_select(j, lane, a2, a4, a8, a16) - dpooled).astype(du_ref.dtype)

    return pl.pallas_call(
        body, name=name, grid=(2,),
        in_specs=[_col_spec(s, 0), _col_spec(s, C_UPOOL[0] // LANES),
                  pl.BlockSpec((None, LANES, LANES), lambda j: (j, 0, 0)), pl.BlockSpec((1, LANES), lambda j: (0, j))],
        out_specs=[_col_spec(s, 0), pl.BlockSpec((None, LANES, LANES), lambda j: (j, 0, 0)),
                   pl.BlockSpec((1, LANES), lambda j: (0, j))],
        out_shape=[jax.ShapeDtypeStruct((s, D_GROUP), BF16), jax.ShapeDtypeStruct((2, LANES, LANES), F32),
                   jax.ShapeDtypeStruct((1, D_GROUP), F32)],
        compiler_params=_params(("parallel",)),
    )(dy, proj, wbd, scale)


def _mesh_pos():
    return lax.axis_index("x"), lax.axis_index("y"), lax.axis_index("c")


def _any_specs(n):
    return [pl.BlockSpec(memory_space=pl.ANY)] * n


def _all_gather(xs, *, name):
    n = len(xs)

    def body(*refs):
        x_refs, out_refs = refs[:n], refs[n:2 * n]
        send_sems, recv_sems, local_sems = refs[2 * n:]
        x, y, cc = _mesh_pos()
        me, sibling = (x, y, cc), (x, y, 1 - cc)
        chips = [(1 - x, y), (x, 1 - y), (1 - x, 1 - y)]

        def slot(a, px, py, pc):
            return out_refs[a].at[4 * px + 2 * py + pc]

        def copy(a, k, block, to, src=None):
            return pltpu.make_async_remote_copy(
                src_ref=slot(a, *block) if src is None else src, dst_ref=slot(a, *block), send_sem=send_sems.at[a, k],
                recv_sem=recv_sems.at[a, k], device_id=to, device_id_type=pl.DeviceIdType.MESH)

        mine = [pltpu.make_async_copy(x_refs[a], slot(a, *me), local_sems.at[a]) for a in range(n)]
        first = []
        for a in range(n):
            first.append(copy(a, 0, me, sibling, src=x_refs[a]))
            first += [copy(a, 1 + j, me, (*chip, cc), src=x_refs[a]) for j, chip in enumerate(chips)]
        for cp in mine + first:
            cp.start()
        passed = []
        for j, chip in enumerate(chips):
            for a in range(n):
                copy(a, 1 + j, (*chip, cc), me).wait_recv()
                passed.append(copy(a, 4 + j, (*chip, cc), sibling))
                passed[-1].start()
        for a in range(n):
            copy(a, 0, sibling, me).wait_recv()
        for j, chip in enumerate(chips):
            for a in range(n):
                copy(a, 4 + j, (*chip, 1 - cc), me).wait_recv()
        for cp in first + passed:
            cp.wait_send()
        for cp in mine:
            cp.wait()

    return pl.pallas_call(
        body, name=name, out_shape=[jax.ShapeDtypeStruct((N_DEV,) + a.shape, a.dtype) for a in xs],
        in_specs=_any_specs(n), out_specs=_any_specs(n),
        scratch_shapes=[pltpu.SemaphoreType.DMA((n, 7)), pltpu.SemaphoreType.DMA((n, 7)), pltpu.SemaphoreType.DMA((n,))],
    )(*xs)


def _plan_gather_near(src_refs, land_refs, send_sems, recv_sems):
    x, y, cc = _mesh_pos()
    me = 4 * x + 2 * y + cc
    plan = []
    for a, (src, land) in enumerate(zip(src_refs, land_refs)):
        for k, (px, py, pc) in enumerate([(x, y, 1 - cc), (1 - x, y, cc), (x, 1 - y, cc), (1 - x, 1 - y, cc)]):
            sems = dict(send_sem=send_sems.at[4 * a + k], recv_sem=recv_sems.at[4 * a + k], device_id=(px, py, pc),
                        device_id_type=pl.DeviceIdType.MESH)
            plan.append((pltpu.make_async_remote_copy(src_ref=src, dst_ref=land.at[me], **sems),
                         pltpu.make_async_remote_copy(src_ref=src, dst_ref=land.at[4 * px + 2 * py + pc], **sems)))
    return plan


def _plan_gather_pass(src_refs, land_refs, send_sems, recv_sems):
    x, y, cc = _mesh_pos()
    plan = []
    for a, land in enumerate(land_refs):
        for j, (px, py) in enumerate([(1 - x, y), (x, 1 - y), (1 - x, 1 - y)]):
            mine, theirs = land.at[4 * px + 2 * py + cc], land.at[4 * px + 2 * py + 1 - cc]
            sems = dict(send_sem=send_sems.at[3 * a + j], recv_sem=recv_sems.at[3 * a + j], device_id=(x, y, 1 - cc),
                        device_id_type=pl.DeviceIdType.MESH)
            plan.append((pltpu.make_async_remote_copy(src_ref=mine, dst_ref=mine, **sems),
                         pltpu.make_async_remote_copy(src_ref=mine, dst_ref=theirs, **sems)))
    return plan


def _plan_sibling(src_refs, land_refs, send_sems, recv_sems):
    x, y, cc = _mesh_pos()
    plan = []
    for a, (src, land) in enumerate(zip(src_refs, land_refs)):
        cp = pltpu.make_async_remote_copy(
            src_ref=src.at[1 - cc], dst_ref=land, send_sem=send_sems.at[a], recv_sem=recv_sems.at[a],
            device_id=(x, y, 1 - cc), device_id_type=pl.DeviceIdType.MESH)
        plan.append((cp, cp))
    return plan


def _plan_chips(src_refs, land_refs, send_sems, recv_sems):
    x, y, cc = _mesh_pos()
    my_chip = 2 * x + y
    plan = []
    for a, (src, land) in enumerate(zip(src_refs, land_refs)):
        for j, (px, py) in enumerate([(1 - x, y), (x, 1 - y), (1 - x, 1 - y)]):
            peer = 2 * px + py
            sems = dict(send_sem=send_sems.at[3 * a + j], recv_sem=recv_sems.at[3 * a + j], device_id=(px, py, cc),
                        device_id_type=pl.DeviceIdType.MESH)
            plan.append((pltpu.make_async_remote_copy(src_ref=src.at[peer], dst_ref=land.at[my_chip], **sems),
                         pltpu.make_async_remote_copy(src_ref=src.at[peer], dst_ref=land.at[peer], **sems)))
    return plan


HBM_SPEC = pl.BlockSpec(memory_space=pltpu.HBM)
SEM_SPEC = pl.BlockSpec(memory_space=pltpu.SEMAPHORE)
ANY_SPEC = pl.BlockSpec(memory_space=pl.ANY)
SIDE_EFFECT = pltpu.CompilerParams(has_side_effects=pltpu.SideEffectType.DATAFLOW_SIDE_EFFECTING)


def _start_copies(plan, sems_per_array, srcs, lands, after, *, name):
    lands = [lax.empty(l, a.dtype) if isinstance(l, tuple) else l for l, a in zip(lands, srcs or lands)]
    ns, n = len(srcs), len(srcs) + len(lands)

    def body(*refs):
        send_sems, recv_sems = refs[n + len(after)], refs[n + len(after) + 1]
        for out, _ in plan(refs[:ns], refs[ns:n], send_sems, recv_sems):
            out.start()
        refs[-1][...] = jnp.zeros_like(refs[-1])

    sem = pltpu.SemaphoreType.DMA((len(lands) * sems_per_array,))
    res = pl.pallas_call(
        body, name=name,
        out_shape=(sem, sem, *[pltpu.HBM(a.shape, a.dtype) for a in srcs + lands], jax.ShapeDtypeStruct((8, LANES), F32)),
        in_specs=[HBM_SPEC] * n + [ANY_SPEC] * len(after),
        out_specs=(SEM_SPEC, SEM_SPEC, *[HBM_SPEC] * n, pl.BlockSpec(memory_space=pltpu.VMEM)),
        input_output_aliases={i: 2 + i for i in range(n)}, compiler_params=SIDE_EFFECT,
    )(*[pltpu.with_memory_space_constraint(a, pltpu.HBM) for a in srcs + lands], *after)
    return (res[0], res[1], list(res[2:2 + ns]), list(res[2 + ns:2 + n])), res[-1]


def _wait_copies(plan, handle, after, *, name):
    send, recv, srcs, lands = handle
    ns, n = len(srcs), len(srcs) + len(lands)

    def body(*refs):
        for out, inc in plan(refs[:ns], refs[ns:n], refs[n], refs[n + 1]):
            out.wait_send()
            inc.wait_recv()

    res = pl.pallas_call(
        body, name=name, out_shape=tuple(pltpu.HBM(a.shape, a.dtype) for a in srcs + lands),
        in_specs=[HBM_SPEC] * n + [SEM_SPEC, SEM_SPEC] + [ANY_SPEC] * len(after), out_specs=[HBM_SPEC] * n,
        input_output_aliases={i: i for i in range(n)}, compiler_params=SIDE_EFFECT,
    )(*srcs, *lands, send, recv, *after)
    return list(res[:ns]), list(res[ns:])


def _row_tile(rows, target=512):
    if rows <= target:
        return rows
    best = None
    for t in range(8, target + 1, 8):
        if rows % t == 0:
            best = t
    assert best is not None, (rows, target)
    return best


def _add_own(g, other, core, *, name):
    _, _, rows, cols = g.shape
    tm = _row_tile(rows)

    def body(c_ref, g_ref, o_ref, out_ref):
        out_ref[...] = g_ref[...] + o_ref[...]

    return pl.pallas_call(
        body, name=name, out_shape=jax.ShapeDtypeStruct(other.shape, other.dtype),
        grid_spec=pltpu.PrefetchScalarGridSpec(
            num_scalar_prefetch=1, grid=(4, rows // tm),
            in_specs=[pl.BlockSpec((None, None, tm, cols), lambda p, i, c_ref: (c_ref[0], p, i, 0)),
                      pl.BlockSpec((None, tm, cols), lambda p, i, c_ref: (p, i, 0))],
            out_specs=pl.BlockSpec((None, tm, cols), lambda p, i, c_ref: (p, i, 0))),
        compiler_params=_params(("parallel", "parallel")),
    )(core, g, other)


def _adamw(parts, w, m, v, after, *, name):
    layers, rows, cols = w.shape
    assert len(parts) == layers
    tm = _row_tile(rows, 256)
    nr = rows // tm

    def body(*refs):
        p_refs = refs[:layers]
        w_ref, m_ref, v_ref, _, g_ref, d_ref, nm_ref, nv_ref, g_s = refs[layers:]
        for ll in range(layers):
            @pl.when(pl.program_id(0) == ll)
            def _(ll=ll):
                g = p_refs[ll][0]
                for q in range(1, p_refs[ll].shape[0]):
                    g = g + p_refs[ll][q]
                g_s[...] = g

        g = g_s[...]
        mm = ADAM_B1 * m_ref[...] + (1.0 - ADAM_B1) * g
        vv = ADAM_B2 * v_ref[...] + (1.0 - ADAM_B2) * jnp.square(g)
        m_hat = mm / (1.0 - ADAM_B1 ** ADAM_STEP)
        v_hat = vv / (1.0 - ADAM_B2 ** ADAM_STEP)
        g_ref[...] = g
        d_ref[...] = -ADAM_LR * (m_hat / (jnp.sqrt(v_hat) + ADAM_EPS) + ADAM_WD * w_ref[...])
        nm_ref[...] = mm
        nv_ref[...] = vv

    def part_spec(ll, p):
        return pl.BlockSpec((p, tm, cols), lambda l, i: (0, jnp.where(l == ll, i, jnp.where(l < ll, 0, nr - 1)), 0))

    spec = pl.BlockSpec((None, tm, cols), lambda l, i: (l, i, 0))
    out = jax.ShapeDtypeStruct(w.shape, F32)
    return pl.pallas_call(
        body, name=name, grid=(layers, nr),
        in_specs=[part_spec(ll, parts[ll].shape[0]) for ll in range(layers)] + [spec] * 3 + [pl.BlockSpec(memory_space=pl.ANY)],
        out_specs=[spec] * 4, out_shape=[out] * 4, scratch_shapes=[pltpu.VMEM((tm, cols), F32)],
        compiler_params=_params(("arbitrary", "arbitrary")),
    )(*parts, w, m, v, after)


def _pack(arrs):
    flat = jnp.concatenate([a.reshape(-1) for a in arrs])
    rows = -(-flat.shape[0] // (PACK_COLS * 16)) * 16
    return jnp.pad(flat, (0, rows * PACK_COLS - flat.shape[0])).reshape(rows, PACK_COLS)


def _unpack(packed, shapes):
    flat = packed.reshape(-1)
    out, off = [], 0
    for shp in shapes:
        n = int(np.prod(shp))
        out.append(flat[off:off + n].reshape(shp))
        off += n
    return out


def _shards_to_full(g, axis):
    if axis == 0:
        return g.reshape(g.shape[0] * g.shape[1], g.shape[2])
    return jnp.transpose(g, (1, 0, 2)).reshape(g.shape[1], g.shape[0] * g.shape[2])


def _full_to_shards(a, axis):
    if axis == 0:
        return jnp.transpose(a.reshape(4, 2, a.shape[0] // N_DEV, a.shape[1]), (1, 0, 2, 3))
    return jnp.transpose(a.reshape(a.shape[0], 4, 2, a.shape[1] // N_DEV), (2, 1, 0, 3))


def _zeros_like_cols(a, n):
    return jnp.zeros(a.shape[:-1] + (n,), a.dtype)


def _pad_heads(a, n):
    z = _zeros_like_cols(a, HEAD)
    return jnp.concatenate([p for h in range(n) for p in (a[..., h * HEAD:(h + 1) * HEAD], z)], axis=-1)


def _unpad_heads(a, n):
    return jnp.concatenate([a[..., h * LANES:h * LANES + HEAD] for h in range(n)], axis=-1)


def _seg(first, width, sign=1):
    return (width, [(first, sign)])


def _zero(width):
    return (width, [])


def _swapped(first):
    half = MLA_ROPE // 2
    return [_seg(first + half, half, -1), _seg(first, half)]


def _padded_heads(first, n):
    return [s for h in range(n) for s in (_seg(first + HEAD * h, HEAD), _zero(HEAD))]


def _layout_w_in():
    kr = 384
    return (_padded_heads(1440, 4) + [_seg(0, 256), _seg(416, 256), _seg(672, 256), _seg(928, 256), _seg(1184, 256)]
            + _padded_heads(1696, 2) + _padded_heads(1824, 2) + [_seg(256, 128)]
            + [_zero(HEAD), _seg(kr, MLA_ROPE), _seg(kr, MLA_ROPE)] + [_zero(HEAD)] + _swapped(kr) + _swapped(kr))


def _layout_w_uq():
    out = []
    for h in range(MLA_HEADS):
        out += [_seg(96 * h, MLA_NOPE), _seg(96 * h + MLA_NOPE, MLA_ROPE)] + _swapped(96 * h + MLA_NOPE)
    return out


def _layout_w_ukv():
    keys = [s for h in range(MLA_HEADS) for s in (_seg(LANES * h, HEAD), _zero(HEAD))]
    values = [s for h in range(MLA_HEADS) for s in (_seg(LANES * h + HEAD, HEAD), _zero(HEAD))]
    return keys + values


def _layout_w_gate_up():
    return [_seg(half + j, GU_TILE) for j in range(0, D_FF, GU_TILE) for half in (0, D_FF)]


LAYOUTS = dict(w_in=_layout_w_in(), w_uq=_layout_w_uq(), w_ukv=_layout_w_ukv(), w_gate_up=_layout_w_gate_up())
SWAPPED_UPDATE = ("w_gate_up",)
OWN_COLS = dict(w_in=1952, w_uq=384, w_ukv=512, w_gate_up=2 * D_FF)


def _plan_extend(layout, shard):
    plan = []
    for width, terms in layout:
        if not terms:
            plan.append((width, []))
            continue
        (first, sign), = terms
        while width:
            g, off = divmod(first, shard)
            w = min(width, shard - off)
            plan.append((w, [(g, off, sign)]))
            first, width = first + w, width - w
    return [plan]


def _plan_fold(layout, own_cols):
    sources = [[] for _ in range(own_cols)]
    e = 0
    for width, terms in layout:
        for first, sign in terms:
            for i in range(width):
                sources[first + i].append((e + i, sign))
        e += width
    shard = own_cols // N_DEV
    plans = {}
    for g in range(N_DEV):
        plan, n = [], g * shard
        while n < (g + 1) * shard:
            w = 1
            while n + w < (g + 1) * shard and [(c + w, sg) for c, sg in sources[n]] == sources[n + w]:
                w += 1
            plan.append((w, [(0, c, sg) for c, sg in sources[n]]))
            n += w
        plans[g] = plan
    return [plans[2 * p + c] for c in range(2) for p in range(4)]


def _assemble(src, plans, out_cols, out_dtype, *, name, transposed=False):
    g, rows, c = src.shape
    tm = _row_tile(rows, 256)
    pad = -out_cols % LANES if transposed else 0

    def body(s_ref, o_ref):
        blocks = [s_ref[i].astype(F32) for i in range(g)]
        for d, plan in enumerate(plans):
            pieces = []
            for width, terms in plan + ([(pad, [])] if pad else []):
                v = None
                for b, first, sign in terms:
                    t = blocks[b][:, first:first + width]
                    t = -t if sign < 0 else t
                    v = t if v is None else v + t
                pieces.append(jnp.zeros((tm, width), F32) if v is None else v)
            block = pieces[0] if len(pieces) == 1 else jnp.concatenate(pieces, axis=1)
            o_ref[d] = (block.T[:out_cols, :] if transposed else block).astype(o_ref.dtype)

    block, shape = ((out_cols, tm), (out_cols, rows)) if transposed else ((tm, out_cols), (rows, out_cols))
    return pl.pallas_call(
        body, name=name, grid=(rows // tm,), in_specs=[pl.BlockSpec((g, tm, c), lambda i: (0, i, 0))],
        out_specs=pl.BlockSpec((len(plans),) + block, (lambda i: (0, 0, i)) if transposed else (lambda i: (0, i, 0))),
        out_shape=jax.ShapeDtypeStruct((len(plans),) + shape, out_dtype), compiler_params=_params(("parallel",)),
    )(src)


def _extend(nm, gathered, *, name):
    layout = LAYOUTS[nm]
    return _assemble(gathered, _plan_extend(layout, OWN_COLS[nm] // N_DEV), sum(w for w, _ in layout), BF16, name=name)[0]


def _fold_to_shards(nm, grad_ext, *, name):
    shards = _assemble(grad_ext[None], _plan_fold(LAYOUTS[nm], OWN_COLS[nm]), OWN_COLS[nm] // N_DEV, F32, name=name,
                       transposed=nm in SWAPPED_UPDATE)
    return shards.reshape((2, 4) + shards.shape[1:])


def _rope_tables(s):
    inv = 1.0 / (ROPE_THETA ** (jnp.arange(0, MLA_ROPE, 2, dtype=F32) / MLA_ROPE))
    ang = jnp.arange(s, dtype=F32)[:, None] * inv[None, :]
    cos, sin = jnp.cos(ang), jnp.sin(ang)
    c32, s32 = jnp.concatenate([cos, cos], axis=1), jnp.concatenate([sin, sin], axis=1)
    zeros, ones = jnp.zeros((s, HEAD), F32), jnp.ones((s, HEAD), F32)
    tq = jnp.concatenate([ones, c32, s32], axis=1) * (1.0 / math.sqrt(MLA_NOPE + MLA_ROPE))
    return (jnp.tile(tq, (1, MLA_HEADS)), jnp.concatenate([zeros, c32, c32], axis=1),
            jnp.concatenate([zeros, s32, s32], axis=1))


def _gn(y):
    return y * _rstd(y, D_GROUP)


def _mixer_fwd(x, w, tabs, l, after_attention=None):
    s = x.shape[0]
    tq, tkc, tks = tabs
    n = lambda t: f"l{l}_{t}"
    h = _rms_fwd(("row", x, D_MODEL, 0), w["attn_norm"], name=n("attn_norm"), rows=s, width=D_MODEL)
    proj = _mm(h, w["w_in"], name=n("in_proj"))
    def prep(cq, ckv, kr, krp, gq, gkv, wuq, wukv, tqv, tc, ts):
        cqn = (cq * _rstd(cq, 256) * gq).astype(BF16)
        ckvn = (ckv * _rstd(ckv, 128) * gkv).astype(BF16)
        qe = jnp.dot(cqn, wuq, preferred_element_type=F32)
        kve = jnp.dot(ckvn, wukv, preferred_element_type=F32)
        kb = kr * tc + krp * ts
        kvv = kve[:, 512:]
        lane = lax.broadcasted_iota(jnp.int32, kvv.shape, 1) & (LANES - 1)
        v = jnp.where(lane == HEAD, 1.0, kvv)
        k = kve[:, :512] + jnp.tile(kb, (1, MLA_HEADS))
        return cqn, ckvn, qe * tqv, k, k, v, v

    cqn, ckvn, qm, km, kmt, vm, vmt = _rowwise(
        prep, name=n("mla_prep"), rows=s,
        ins=[("row", proj, 256, C_CQ[0] // 256), ("row", proj, 128, C_CKV[0] // 128),
             ("row", proj, 128, C_KR[0] // 128), ("row", proj, 128, C_KRP[0] // 128),
             ("full", w["mla_q_norm"]), ("full", w["mla_kv_norm"]), ("full", w["w_uq"]), ("full", w["w_ukv"]),
             ("row", tq, 512, 0), ("row", tkc, 128, 0), ("row", tks, 128, 0)],
        outs=[(256, BF16), (128, BF16), (512, BF16, MLA_HEADS), (512, BF16, MLA_HEADS), (512, BF16, MLA_HEADS, "T"),
              (512, BF16, MLA_HEADS), (512, BF16, MLA_HEADS, "T")])
    y_a, lse_row = _mla_fwd(qm, km, vmt, name=n("mla_fwd"), s=s, nh=MLA_HEADS)
    mix_norm = w["mix_norm"]
    if after_attention is not None:
        mix_norm = mix_norm + after_attention(y_a)[0, 0]
    y_b = _conv_fwd(proj, w["conv_w"], name=n("conv_fwd"), s=s)
    y_c = _pool_fwd(proj, w["pool_wbd"], w["pool_scale"], name=n("pool_fwd"), s=s)
    y_d, lse_d = _swa_fwd(proj, w["hp_swa"], name=n("swa_fwd"), s=s, scale=1.0 / math.sqrt(HEAD))

    def mix(ya, yb, yc, yd, mn):
        return (jnp.concatenate([_gn(_unpad_heads(ya, 4)), _gn(yb), _gn(yc), _gn(_unpad_heads(yd, 4))], axis=1) * mn,)

    mixed = _rowwise(mix, name=n("group_norm"), rows=s,
                     ins=[("heads", y_a), ("row", y_b, 256, 0), ("row", y_c, 256, 0), ("row", y_d, 512, 0),
                          ("full", mix_norm)], outs=[(D_MODEL, BF16)])[0]
    x1 = _mm(mixed, w["w_o"], res=x, name=n("out_proj"))
    saved = dict(x=x, h=h, proj=proj, cqn=cqn, ckvn=ckvn, qm=qm, km=km, kmt=kmt, vm=vm, y_a=y_a, lse_row=lse_row,
                 y_b=y_b, y_c=y_c, y_d=y_d, lse_d=lse_d, mixed=mixed)
    return x1, saved


def _ffn_fwd(x1, w, l):
    s = x1.shape[0]
    n = lambda t: f"l{l}_{t}"
    h2 = _rms_fwd(("row", x1, D_MODEL, 0), w["ffn_norm"], name=n("ffn_norm"), rows=s, width=D_MODEL)

    def swiglu(gu):
        g, u = gu[:, :GU_TILE], gu[:, GU_TILE:]
        return gu, g * jax.nn.sigmoid(g) * u

    gu, act = _mm(h2, w["w_gate_up"], tm=2048, tn=2 * GU_TILE, name=n("gate_up"),
                  epilogue=(swiglu, [], [(2 * D_FF, BF16), (D_FF, BF16)]))
    x2 = _mm(act, w["w_down"], res=x1, tk=D_FF // 2, name=n("down"))
    return x2, dict(x1=x1, h2=h2, gu=gu, act=act)


def _ffn_bwd_down(dx2, sv, w, l):
    n = lambda t: f"l{l}_{t}"

    def swiglu_bwd(da, gu):
        gt, u = gu[:, :GU_TILE].astype(F32), gu[:, GU_TILE:].astype(F32)
        sg = jax.nn.sigmoid(gt)
        return (jnp.concatenate([da * u * sg * (1.0 + gt * (1.0 - sg)), da * gt * sg], axis=1),)

    dgu = _mm(dx2[1], w["w_down"], tb=True, tm=2048, tn=GU_TILE, name=n("d_act"),
              epilogue=(swiglu_bwd, [sv["gu"]], [(2 * D_FF, BF16)]))[0]
    g = dict(w_down=_mm(sv["act"], dx2[1], ta=True, tm=D_FF // 2, name=n("dw_down")))
    return dgu, g


def _ffn_bwd_up(dx2, dgu, sv, w, l):
    s = dgu.shape[0]
    n = lambda t: f"l{l}_{t}"
    dh2 = _mm(dgu, w["w_gate_up"], tb=True, tk=D_FF // 2, name=n("d_h2"))
    g = dict(w_gate_up=_mm(sv["h2"], dgu, ta=True, tn=D_FF // 2, name=n("dw_gate_up")))
    dx1, dx1_b, g["ffn_norm"] = _rms_bwd(("row", sv["x1"], D_MODEL, 0), w["ffn_norm"], dh2, dx2[0],
                                         name=n("ffn_norm_bwd"), rows=s, width=D_MODEL, out_dtypes=(F32, BF16))
    return (dx1, dx1_b), g


def _mixer_bwd_out(dx1, sv, w, l):
    s = dx1[1].shape[0]
    n = lambda t: f"l{l}_{t}"
    dmixed = _mm(dx1[1], w["w_o"], tb=True, name=n("d_mixed"))
    g = dict(w_o=_mm(sv["mixed"], dx1[1], ta=True, name=n("dw_o")))

    def mix_bwd(dm, ya, yb, yc, yd, mn):
        outs, dmn = [], []
        for i, y in enumerate((_unpad_heads(ya, 4), yb, yc, _unpad_heads(yd, 4))):
            lo, hi = i * D_GROUP, (i + 1) * D_GROUP
            r = _rstd(y, D_GROUP)
            nrm = y * r
            dmg = dm[:, lo:hi]
            dn = dmg * mn[:, lo:hi]
            dy = r * (dn - nrm * (jnp.sum(dn * nrm, axis=-1, keepdims=True) * (1.0 / D_GROUP)))
            outs.append(_pad_heads(dy, 4) if i in (0, 3) else dy)
            dmn.append(jnp.sum(dmg * nrm, axis=0, keepdims=True))
        return (*outs, jnp.concatenate(dmn, axis=1))

    dy_a, dy_b, dy_c, dy_d, g["mix_norm"] = _rowwise(
        mix_bwd, name=n("group_norm_bwd"), rows=s,
        ins=[("row", dmixed, D_MODEL, 0), ("heads", sv["y_a"]), ("row", sv["y_b"], 256, 0),
             ("row", sv["y_c"], 256, 0), ("row", sv["y_d"], 512, 0), ("full", w["mix_norm"])],
        outs=[(512, F32, MLA_HEADS), (256, F32), (256, F32), (512, F32)], reds=[(1, D_MODEL)])
    return (dy_a, dy_b, dy_c, dy_d), g


def _mixer_bwd_in(dx1, dys, sv, w, tabs, l):
    s = dx1[0].shape[0]
    tq, tkc, tks = tabs
    n = lambda t: f"l{l}_{t}"
    dy_a, dy_b, dy_c, dy_d = dys
    g = {}

    proj = sv["proj"]
    dq_sw, dk_sw, dv_sw, dsink = _swa_bwd(proj, sv["y_d"], dy_d, sv["lse_d"], w["hp_swa"], name=n("swa_bwd"), s=s,
                                          scale=1.0 / math.sqrt(HEAD))
    g["swa_sinks"] = dsink[:, 0, 0]

    dqm, dkm, dvm = _mla_bwd(sv["qm"], sv["km"], sv["kmt"], sv["vm"], sv["y_a"], dy_a, sv["lse_row"], w["hp_swa"],
                             name=n("mla_bwd"), s=s, nh=MLA_HEADS)

    def rms_bwd(x, gv, dy, width):
        r = _rstd(x, width)
        dyg = dy * gv
        dx = r * dyg - x * (r * r * r) * (jnp.sum(dyg * x, axis=-1, keepdims=True) * (1.0 / width))
        return dx, jnp.sum(dy * x * r, axis=0, keepdims=True)

    def prep_bwd(dq, dk, dv, cq, ckv, gq, gkv, wuq, wukv, tqv, tc, ts):
        dkb = dk[:, 0:128] + dk[:, 128:256] + dk[:, 256:384] + dk[:, 384:512]
        dq_ext = (dq * tqv).astype(BF16)
        dkv_ext = jnp.concatenate([dk.astype(BF16), dv], axis=1)
        dcqn = lax.dot_general(dq_ext, wuq, NT_DIMS, preferred_element_type=F32)
        dckvn = lax.dot_general(dkv_ext, wukv, NT_DIMS, preferred_element_type=F32)
        dcq, dgq = rms_bwd(cq, gq, dcqn, 256)
        dckv, dgkv = rms_bwd(ckv, gkv, dckvn, 128)
        return dq_ext, dkv_ext, dkb * tc, dkb * ts, dcq, dckv, dgq, dgkv

    dq_ext, dkv_ext, dkr, dkrp, dcq, dckv, g["mla_q_norm"], g["mla_kv_norm"] = _rowwise(
        prep_bwd, name=n("mla_prep_bwd"), rows=s,
        ins=[("heads", dqm), ("heads", dkm), ("heads", dvm), ("row", proj, 256, C_CQ[0] // 256),
             ("row", proj, 128, C_CKV[0] // 128), ("full", w["mla_q_norm"]), ("full", w["mla_kv_norm"]),
             ("full", w["w_uq"]), ("full", w["w_ukv"]), ("row", tq, 512, 0), ("row", tkc, 128, 0), ("row", tks, 128, 0)],
        outs=[(512, BF16), (1024, BF16), (128, BF16), (128, BF16), (256, BF16), (128, BF16)],
        reds=[(1, 256), (1, 128)])
    g["w_uq"] = _mm(sv["cqn"], dq_ext, ta=True, name=n("dw_uq"))
    g["w_ukv"] = _mm(sv["ckvn"], dkv_ext, ta=True, name=n("dw_ukv"))

    dgb, dgc, duc, g["conv_w"] = _conv_bwd(dy_b, proj, w["conv_w"], name=n("conv_bwd"), s=s)
    dup, g["pool_wbd"], g["pool_scale"] = _pool_bwd(dy_c, proj, w["pool_wbd"], w["pool_scale"], name=n("pool_bwd"), s=s)

    dproj = jnp.concatenate([dq_sw, dcq, dgb, dgc, duc, dup, dk_sw.astype(BF16), dv_sw.astype(BF16), dckv, dkr, dkrp],
                            axis=1)
    g["w_in"] = _mm(sv["h"], dproj, ta=True, name=n("dw_in"))
    return dproj, g


def _mixer_bwd_norm(dx1, dproj, sv, w, l):
    n = lambda t: f"l{l}_{t}"
    dh = _mm(dproj, w["w_in"], tb=True, name=n("d_h"))
    dx0, dx0_b, dg = _rms_bwd(("row", sv["x"], D_MODEL, 0), w["attn_norm"], dh, dx1[0], name=n("attn_norm_bwd"),
                              rows=dh.shape[0], width=D_MODEL, out_dtypes=(F32, BF16))
    return (dx0, dx0_b), dict(attn_norm=dg)


def _loss_head(x, target, g, *, s):
    def fn(xv, tv, gv):
        r = _rstd(xv, D_MODEL)
        e = xv * r * gv - tv
        part = jnp.sum(jnp.sum(e * e, axis=1, keepdims=True), axis=0, keepdims=True) * (0.5 / D_MODEL)
        dy = e * (1.0 / D_MODEL)
        dyg = dy * gv
        dx = r * dyg - xv * (r * r * r) * (jnp.sum(dyg * xv, axis=-1, keepdims=True) * (1.0 / D_MODEL))
        return dx, dx, jnp.sum(dy * xv * r, axis=0, keepdims=True), jnp.broadcast_to(part, (1, LANES))

    return _rowwise(fn, name="loss_head", rows=s,
                    ins=[("row", x, D_MODEL, 0), ("row", target, D_MODEL, 0), ("full", g)],
                    outs=[(D_MODEL, F32), (D_MODEL, BF16)], reds=[(1, D_MODEL), (1, LANES)])


def _alibi_slopes(n):
    return np.asarray([2.0 ** (-8.0 * (i + 1) / n) for i in range(n)], dtype=np.float32)


MIXER_WEIGHTS = ("w_in", "w_uq", "w_ukv", "conv_w", "w_o")
FFN_WEIGHTS = ("w_gate_up", "w_down")


def _mixer_weights(full, rep, l):
    pw = rep["pool_w"][l]
    z = jnp.zeros((HEAD, HEAD), F32)
    wbd = jnp.stack([jnp.block([[pw[2 * j], z], [z, pw[2 * j + 1]]]) for j in range(2)])
    return dict(
        attn_norm=rep["attn_norm"][l][None], w_in=full["w_in"], mla_q_norm=rep["mla_q_norm"][l][None],
        w_uq=full["w_uq"], mla_kv_norm=rep["mla_kv_norm"][l][None], w_ukv=full["w_ukv"],
        conv_w=full["conv_w"], pool_wbd=wbd, pool_scale=rep["pool_scale"][l][None],
        mix_norm=rep["mix_norm"][l][None], w_o=full["w_o"],
        hp_swa=jnp.stack([jnp.asarray(_alibi_slopes(SWA_HEADS)), rep["swa_sinks"][l]], axis=1))


def _ffn_weights(full, rep, l):
    return dict(ffn_norm=rep["ffn_norm"][l][None], w_gate_up=full["w_gate_up"], w_down=full["w_down"])


def _small_grads(g):
    rows = ("attn_norm", "mla_q_norm", "mla_kv_norm", "pool_scale", "ffn_norm", "mix_norm")
    out = {nm: g[nm][0] for nm in rows if nm in g}
    if "swa_sinks" in g:
        out["swa_sinks"] = g["swa_sinks"]
    if "pool_wbd" in g:
        e = g["pool_wbd"]
        out["pool_w"] = jnp.stack([e[j // 2][HEAD * (j % 2):HEAD * (j % 2 + 1), HEAD * (j % 2):HEAD * (j % 2 + 1)]
                                   for j in range(4)])
    return out


def kernel(x, attn_norm, w_in, mla_q_norm, w_uq, mla_kv_norm, w_ukv, conv_w, pool_w, pool_scale, swa_sinks, mix_norm, w_o, ffn_norm, w_gate_up, w_down, final_norm, loss_target, m_attn_norm, m_w_in, m_mla_q_norm, m_w_uq, m_mla_kv_norm, m_w_ukv, m_conv_w, m_pool_w, m_pool_scale, m_swa_sinks, m_mix_norm, m_w_o, m_ffn_norm, m_w_gate_up, m_w_down, m_final_norm, v_attn_norm, v_w_in, v_mla_q_norm, v_w_uq, v_mla_kv_norm, v_w_ukv, v_conv_w, v_pool_w, v_pool_scale, v_swa_sinks, v_mix_norm, v_w_o, v_ffn_norm, v_w_gate_up, v_w_down, v_final_norm):
    given = dict(locals())
    sh_names = [nm for nm, _, _ in SHARDED]
    sh_axis = {nm: ax - 1 for nm, _, ax in SHARDED}
    rep_names = [nm for nm, _ in REPLICATED]
    rep_shapes = [shp for _, shp in REPLICATED]
    rep = {nm: given[nm] for nm in rep_names if nm != "loss"}
    me = 4 * lax.axis_index("x") + 2 * lax.axis_index("y") + lax.axis_index("c")
    my_chip = 2 * lax.axis_index("x") + lax.axis_index("y")
    core = lax.axis_index("c").astype(jnp.int32).reshape(1)

    def behind(token, a):
        return a + token[0, 0].astype(a.dtype)

    def wire(nm, l):
        if nm == "conv_w":
            return lax.bitcast_convert_type(given[nm][l], BF16).reshape(3, -1)
        return given[nm][l].astype(BF16)

    def whole(nm, g, tag):
        if nm in LAYOUTS:
            return _extend(nm, g, name=f"extend_{nm}_{tag}")
        if nm == "conv_w":
            g = lax.bitcast_convert_type(g.reshape(N_DEV, 3, -1, 2), F32)
        return _shards_to_full(g, sh_axis[nm])

    def near_start(names, l, after, tag):
        srcs = [wire(nm, l) for nm in names]
        return _start_copies(_plan_gather_near, 4, srcs, [(N_DEV,) + a.shape for a in srcs], after, name=f"start_gather_{tag}")

    def pass_on(handle, after, tag):
        srcs, lands = _wait_copies(_plan_gather_near, handle, after, name=f"wait_gather_{tag}")
        handle, token = _start_copies(_plan_gather_pass, 3, [], lands, [], name=f"start_pass_{tag}")
        return (srcs, handle), token

    def gathered(names, state, after, tag):
        srcs, handle = state
        _, lands = _wait_copies(_plan_gather_pass, handle, after, name=f"wait_pass_{tag}")
        return {nm: whole(nm, lax.dynamic_update_index_in_dim(land, src, me, 0), tag)
                for nm, src, land in zip(names, srcs, lands)}

    layer1 = MIXER_WEIGHTS + FFN_WEIGHTS
    got = _all_gather([wire(nm, 0) for nm in MIXER_WEIGHTS], name="gather_mixer0")
    full_m0 = {nm: whole(nm, g, "mixer0") for nm, g in zip(MIXER_WEIGHTS, got)}
    h_f0, tok = near_start(FFN_WEIGHTS, 0, [], "ffn0")
    h_l1, tok = near_start(layer1, 1, [tok], "layer1")

    xs, target = x[0], loss_target[0]
    s = xs.shape[0]
    tabs = _rope_tables(s)
    wm, wf, svm, svf = [None] * DEPTH, [None] * DEPTH, [None] * DEPTH, [None] * DEPTH
    wm[0] = _mixer_weights(full_m0, rep, 0)
    wm[0]["attn_norm"] = behind(tok, wm[0]["attn_norm"])
    passing = {}

    def pass_ffn0(y_a):
        passing["ffn0"], token = pass_on(h_f0, [y_a], "ffn0")
        return token

    x1, svm[0] = _mixer_fwd(xs, wm[0], tabs, 0, after_attention=pass_ffn0)
    wf[0] = _ffn_weights(gathered(FFN_WEIGHTS, passing["ffn0"], [x1], "ffn0"), rep, 0)
    passing["layer1"], tok = pass_on(h_l1, [x1], "layer1")
    wf[0]["ffn_norm"] = behind(tok, wf[0]["ffn_norm"])
    x2, svf[0] = _ffn_fwd(x1, wf[0], 0)
    full_1 = gathered(layer1, passing["layer1"], [x2], "layer1")
    wm[1], wf[1] = _mixer_weights(full_1, rep, 1), _ffn_weights(full_1, rep, 1)
    x1, svm[1] = _mixer_fwd(x2, wm[1], tabs, 1)
    x2, svf[1] = _ffn_fwd(x1, wf[1], 1)
    dx_f, dx_b, d_final, loss = _loss_head(x2, target, rep["final_norm"][None], s=s)
    dx = (dx_f, dx_b)

    parts = {}

    def reduce_start(grads, l, after, tag):
        names = [nm for nm in sh_names if nm in grads]
        mine = [_fold_to_shards(nm, grads[nm], name=f"fold_{nm}_{l}") if nm in LAYOUTS
                else _full_to_shards(grads[nm], sh_axis[nm]) for nm in names]
        handle, token = _start_copies(_plan_sibling, 1, mine, [m.shape[1:] for m in mine], after, name=f"start_sibling_{tag}")
        return (names, l, handle), token

    def reduce_mid(state, after, tag):
        names, l, handle = state
        mine, theirs = _wait_copies(_plan_sibling, handle, after, name=f"wait_sibling_{tag}")
        sums = [_add_own(g, o, core, name=f"chip_sum_{nm}_{l}") for nm, g, o in zip(names, mine, theirs)]
        handle, token = _start_copies(_plan_chips, 3, sums, [a.shape for a in sums], [], name=f"start_chips_{tag}")
        return (names, l, handle), token

    def reduce_end(state, after, tag):
        names, l, handle = state
        sums, lands = _wait_copies(_plan_chips, handle, after, name=f"wait_chips_{tag}")
        for nm, own, land in zip(names, sums, lands):
            parts[nm, l] = lax.dynamic_update_index_in_dim(land, lax.dynamic_index_in_dim(own, my_chip, 0, keepdims=False),
                                                           my_chip, 0)

    small = [None] * DEPTH
    in_flight = []
    pending = None
    for l in reversed(range(DEPTH)):
        dgu, g_down = _ffn_bwd_down(dx, svf[l], wf[l], l)
        if pending is not None:
            state, token = reduce_mid(pending, [dgu], f"mixer{l + 1}")
            in_flight.append((state, f"mixer{l + 1}"))
            wf[l]["ffn_norm"] = behind(token, wf[l]["ffn_norm"])
        dx1, g_up = _ffn_bwd_up(dx, dgu, svf[l], wf[l], l)
        g_ffn = {**g_down, **g_up}
        state, token = reduce_start(g_ffn, l, [], f"ffn{l}")
        wm[l]["mix_norm"] = behind(token, wm[l]["mix_norm"])
        dys, g_out = _mixer_bwd_out(dx1, svm[l], wm[l], l)
        state, token = reduce_mid(state, [dys[0]], f"ffn{l}")
        in_flight.append((state, f"ffn{l}"))
        wm[l]["hp_swa"] = behind(token, wm[l]["hp_swa"])
        dproj, g_in = _mixer_bwd_in(dx1, dys, svm[l], wm[l], tabs, l)
        g_mixer = {**g_out, **g_in}
        pending, token = reduce_start(g_mixer, l, [], f"mixer{l}")
        wm[l]["attn_norm"] = behind(token, wm[l]["attn_norm"])
        dx, g_norm = _mixer_bwd_norm(dx1, dproj, svm[l], wm[l], l)
        small[l] = _small_grads({**g_ffn, **g_mixer, **g_norm})
    grads = {nm: jnp.stack([small[l][nm] for l in range(DEPTH)]) for nm in rep_names if nm in small[0]}
    grads["final_norm"] = d_final[0]
    grads["loss"] = loss[0, :1]
    zero = jnp.zeros((1,), F32)
    small = _all_gather([behind(token, _pack([grads[nm] for nm in rep_names]))], name="gather_small_grads")

    last, token = reduce_mid(pending, [dx[0], small[0]], "mixer0")
    for state, tag in in_flight:
        reduce_end(state, [], tag)
    grad_x = dx[0]

    def adamw(nm, after):
        swap = (lambda a: jnp.swapaxes(a, 1, 2)) if nm in SWAPPED_UPDATE else (lambda a: a)
        res = _adamw([parts[nm, l] for l in range(DEPTH)], swap(given[nm]), swap(given["m_" + nm]), swap(given["v_" + nm]),
                     after, name=f"adamw_{nm}")
        return [swap(a) for a in res]

    sh_out = {nm: adamw(nm, token) for nm in FFN_WEIGHTS}
    packs = [_pack([given.get(pre + nm, zero) for nm in rep_names])[None] for pre in ("", "m_", "v_")]
    rep_res = _adamw(small, *packs, token, name="adamw_replicated")
    rep_out = [dict(zip(rep_names, _unpack(o[0], rep_shapes))) for o in rep_res]

    reduce_end(last, [rep_res[0], sh_out["w_down"][0]], "mixer0")
    sh_out.update({nm: adamw(nm, token) for nm in MIXER_WEIGHTS})

    out = [rep_out[0]["loss"][0], grad_x[None]]
    for i in range(4):
        out += [sh_out[nm][i] if nm in sh_axis else rep_out[i][nm] for nm in WEIGHT_ORDER]
    return tuple(out)
```

```python
import functools
import math

import numpy as np
import jax
import jax.numpy as jnp
from jax import lax
from jax.experimental import pallas as pl
from jax.experimental.pallas import tpu as pltpu

F32 = jnp.float32
BF16 = jnp.bfloat16

D_MODEL = 1024
DEPTH = 2
D_GROUP = 256
MLA_HEADS = 4
MLA_NOPE = 64
MLA_ROPE = 32
ROPE_THETA = 10000.0
POOL_WINDOWS = (2, 4, 8, 16)
SWA_HEADS = 4
SWA_KV_HEADS = 2
SWA_WINDOW = 128
D_FF = 2816
GU_TILE = 256
RMS_EPS = 1e-6
LANES = 128
HEAD = 64
VMEM_LIMIT = 48 * 1024 * 1024
NEG = -1e30

ADAM_LR = 0.001
ADAM_B1 = 0.9
ADAM_B2 = 0.999
ADAM_EPS = 1e-08
ADAM_WD = 0.01
ADAM_STEP = 10

N_DEV = 8
PACK_COLS = 1024

C_QSW, C_CQ, C_GB, C_GC, C_UCONV, C_UPOOL = (0, 512), (512, 256), (768, 256), (1024, 256), (1280, 256), (1536, 256)
C_KSW, C_VSW, C_CKV, C_KR, C_KRP = (1792, 256), (2048, 256), (2304, 128), (2432, 128), (2560, 128)

SHARDED = (("w_in", (DEPTH, 1024, 244), 2), ("w_uq", (DEPTH, 256, 48), 2), ("w_ukv", (DEPTH, 128, 64), 2),
           ("conv_w", (DEPTH, 3, 32), 2), ("w_o", (DEPTH, 128, 1024), 1), ("w_gate_up", (DEPTH, 1024, 704), 2),
           ("w_down", (DEPTH, 352, 1024), 1))
REPLICATED = (("attn_norm", (DEPTH, 1024)), ("mla_q_norm", (DEPTH, 256)), ("mla_kv_norm", (DEPTH, 128)),
              ("pool_w", (DEPTH, 4, 64, 64)), ("pool_scale", (DEPTH, 256)), ("swa_sinks", (DEPTH, 4)),
              ("mix_norm", (DEPTH, 1024)), ("ffn_norm", (DEPTH, 1024)), ("final_norm", (1024,)), ("loss", (1,)))
WEIGHT_ORDER = ("attn_norm", "w_in", "mla_q_norm", "w_uq", "mla_kv_norm", "w_ukv", "conv_w", "pool_w", "pool_scale",
                "swa_sinks", "mix_norm", "w_o", "ffn_norm", "w_gate_up", "w_down", "final_norm")


def _params(sem):
    return pltpu.CompilerParams(dimension_semantics=sem, vmem_limit_bytes=VMEM_LIMIT)


def _pick(dim, target):
    if dim <= target:
        return dim
    best = None
    for t in range(LANES, target + 1, LANES):
        if dim % t == 0:
            best = t
    assert best is not None, (dim, target)
    return best


def _mm(a, b, *, name, ta=False, tb=False, res=None, out_dtype=F32, tm=1024, tn=1024, tk=1024, epilogue=None):
    m, k = (a.shape[1], a.shape[0]) if ta else a.shape
    n = b.shape[0] if tb else b.shape[1]
    assert (b.shape[1] if tb else b.shape[0]) == k
    tm, tn, tk = _pick(m, tm), _pick(n, tn), _pick(k, tk)
    nj, nk = n // tn, k // tk
    dims = (((0 if ta else 1,), (1 if tb else 0,)), ((), ()))
    fn, extra, outs = epilogue if epilogue is not None else (None, [], [(n, out_dtype)])
    if res is not None:
        assert epilogue is None
        fn, extra = (lambda acc, r: (acc + r,)), [res]
    n_in, n_out = 2 + len(extra), len(outs)

    def body(*refs):
        a_ref, b_ref, acc_ref = refs[0], refs[1], refs[-1]
        kk = pl.program_id(2)

        def product():
            return lax.dot_general(a_ref[...].astype(BF16), b_ref[...].astype(BF16), dims, preferred_element_type=F32)

        def finish(acc):
            tiles = (acc,) if fn is None else fn(acc, *[r[...] for r in refs[2:n_in]])
            for o_ref, tile in zip(refs[n_in:n_in + n_out], tiles):
                o_ref[...] = tile.astype(o_ref.dtype)

        if nk == 1:
            finish(product())
            return

        @pl.when(kk == 0)
        def _():
            acc_ref[...] = jnp.zeros_like(acc_ref)

        acc_ref[...] += product()

        @pl.when(kk == nk - 1)
        def _():
            finish(acc_ref[...])

    def col_tiles(width):
        assert width % (nj * LANES) == 0, (width, nj)
        return pl.BlockSpec((tm, width // nj), lambda i, j, kk: (i, j))

    a_spec = pl.BlockSpec((tk, tm), lambda i, j, kk: (kk, i)) if ta else pl.BlockSpec((tm, tk), lambda i, j, kk: (i, kk))
    b_spec = pl.BlockSpec((tn, tk), lambda i, j, kk: (j, kk)) if tb else pl.BlockSpec((tk, tn), lambda i, j, kk: (kk, j))
    res_ = pl.pallas_call(
        body, name=name, grid=(m // tm, nj, nk), in_specs=[a_spec, b_spec] + [col_tiles(e.shape[1]) for e in extra],
        out_specs=[col_tiles(w) for w, _ in outs],
        out_shape=[jax.ShapeDtypeStruct((m, w), dt) for w, dt in outs],
        scratch_shapes=[pltpu.VMEM((tm, tn), F32)] if nk > 1 else [],
        compiler_params=_params(("parallel", "parallel", "arbitrary")),
    )(a, b, *extra)
    return res_[0] if epilogue is None else res_


def _rowwise(fn, *, name, rows, ins, outs, reds=(), tm=512):
    tm = min(tm, rows)
    assert rows % tm == 0
    n_in, n_out = len(ins), len(outs)

    def body(*refs):
        vals = [jnp.concatenate([r[h] for h in range(r.shape[0])], axis=1) if spec[0] == "heads" else r[...]
                for spec, r in zip(ins, refs[:n_in])]
        res = fn(*vals)
        for out, r, v in zip(outs, refs[n_in:n_in + n_out], res[:n_out]):
            if len(out) >= 3:
                for h in range(out[2]):
                    piece = v[:, h * LANES:(h + 1) * LANES]
                    r[h] = (piece.T if len(out) == 4 else piece).astype(r.dtype)
            else:
                r[...] = v.astype(r.dtype)
        if reds:
            @pl.when(pl.program_id(0) == 0)
            def _():
                for r in refs[n_in + n_out:]:
                    r[...] = jnp.zeros_like(r)

            for r, v in zip(refs[n_in + n_out:], res[n_out:]):
                r[...] += v

    in_specs, args = [], []
    for spec in ins:
        if spec[0] == "row":
            _, arr, width, blk = spec
            in_specs.append(pl.BlockSpec((tm, width), functools.partial(lambda i, blk: (i, blk), blk=blk)))
        elif spec[0] == "heads":
            arr = spec[1]
            in_specs.append(pl.BlockSpec((arr.shape[0], tm, LANES), lambda i: (0, i, 0)))
        else:
            arr = spec[1]
            in_specs.append(pl.BlockSpec(arr.shape, functools.partial(lambda i, nd: (0,) * nd, nd=arr.ndim)))
        args.append(arr)
    def out_spec(o):
        if len(o) == 4:
            return pl.BlockSpec((o[2], LANES, tm), lambda i: (0, 0, i)), (o[2], LANES, rows)
        if len(o) == 3:
            return pl.BlockSpec((o[2], tm, LANES), lambda i: (0, i, 0)), (o[2], rows, LANES)
        return pl.BlockSpec((tm, o[0]), lambda i: (i, 0)), (rows, o[0])

    out_specs = [out_spec(o)[0] for o in outs]
    out_shape = [jax.ShapeDtypeStruct(out_spec(o)[1], o[1]) for o in outs]
    out_specs += [pl.BlockSpec((r, w), lambda i: (0, 0)) for r, w in reds]
    out_shape += [jax.ShapeDtypeStruct((r, w), F32) for r, w in reds]
    return pl.pallas_call(body, name=name, grid=(rows // tm,), in_specs=in_specs, out_specs=out_specs,
                          out_shape=out_shape, compiler_params=_params(("arbitrary",)))(*args)


def _rstd(x, n):
    return lax.rsqrt(jnp.sum(x * x, axis=-1, keepdims=True) * (1.0 / n) + RMS_EPS)


def _rms_fwd(x_spec, g, *, name, rows, width):
    def fn(x, gv):
        return (x * _rstd(x, width) * gv,)
    return _rowwise(fn, name=name, rows=rows, ins=[x_spec, ("full", g)], outs=[(width, BF16)])[0]


def _rms_bwd(x_spec, g, dy, res, *, name, rows, width, out_dtypes):
    def fn(x, gv, dyv, *rest):
        r = _rstd(x, width)
        dyg = dyv * gv
        dx = r * dyg - x * (r * r * r) * (jnp.sum(dyg * x, axis=-1, keepdims=True) * (1.0 / width))
        if rest:
            dx = dx + rest[0]
        return (dx,) * len(out_dtypes) + (jnp.sum(dyv * x * r, axis=0, keepdims=True),)

    ins = [x_spec, ("full", g), ("row", dy, width, 0)]
    if res is not None:
        ins.append(("row", res, width, 0))
    return _rowwise(fn, name=name, rows=rows, ins=ins, outs=[(width, dt) for dt in out_dtypes], reds=[(1, width)])


NT_DIMS = (((1,), (1,)), ((), ()))
TN_DIMS = (((0,), (0,)), ((), ()))
BNT_DIMS = (((2,), (2,)), ((0,), (0,)))
BNN_DIMS = (((2,), (1,)), ((0,), (0,)))


def _mla_tile(s):
    return min(512, s)


def _block_tables(pairs):
    return jnp.asarray([a for a, _ in pairs], jnp.int32), jnp.asarray([b for _, b in pairs], jnp.int32)


def _causal(shape, query_axis):
    return lax.broadcasted_iota(jnp.int32, shape, query_axis) >= lax.broadcasted_iota(jnp.int32, shape, 1 - query_axis)


def _mla_fwd(qa, ka, vta, *, name, s, nh):
    t = _mla_tile(s)
    nq = s // t

    pairs = [(i, j) for i in range(nq) for j in range(i + 1)]

    def body(qb_ref, kb_ref, q_ref, k_ref, vt_ref, o_ref, lse_ref, m_s, acc_s):
        i, j = qb_ref[pl.program_id(0)], kb_ref[pl.program_id(0)]

        @pl.when(j == 0)
        def _():
            m_s[...] = jnp.full_like(m_s, NEG)
            acc_s[...] = jnp.zeros_like(acc_s)

        def step(diag):
            sc = lax.dot_general(k_ref[...], q_ref[...], BNT_DIMS, preferred_element_type=F32)
            if diag:
                sc = jnp.where(_causal(sc.shape[1:], 1)[None], sc, NEG)
            m_prev = m_s[...]
            m_new = jnp.maximum(m_prev, jnp.max(sc, axis=1, keepdims=True))
            p = jnp.exp(sc - m_new).astype(BF16)
            acc_s[...] = (jnp.exp(m_prev - m_new) * acc_s[...]
                          + lax.dot_general(vt_ref[...], p, BNN_DIMS, preferred_element_type=F32))
            m_s[...] = m_new

        pl.when(j < i)(functools.partial(step, False))
        pl.when(j == i)(functools.partial(step, True))

        @pl.when(j == i)
        def _():
            row = lax.broadcasted_iota(jnp.int32, (LANES, t), 0)
            for h in range(nh):
                acc = acc_s[h]
                l = acc[HEAD:HEAD + 1, :]
                o_ref[h] = jnp.where(row < HEAD, acc / l, 0.0).T
                lse_ref[h] = m_s[h] + jnp.log(l)

    q_spec = pl.BlockSpec((nh, t, LANES), lambda p, qb, kb: (0, qb[p], 0))
    k_spec = pl.BlockSpec((nh, t, LANES), lambda p, qb, kb: (0, kb[p], 0))
    vt_spec = pl.BlockSpec((nh, LANES, t), lambda p, qb, kb: (0, 0, kb[p]))
    return pl.pallas_call(
        body, name=name,
        grid_spec=pltpu.PrefetchScalarGridSpec(
            num_scalar_prefetch=2, grid=(len(pairs),), in_specs=[q_spec, k_spec, vt_spec],
            out_specs=[q_spec, pl.BlockSpec((nh, 1, t), lambda p, qb, kb: (0, 0, qb[p]))],
            scratch_shapes=[pltpu.VMEM((nh, 1, t), F32), pltpu.VMEM((nh, LANES, t), F32)]),
        out_shape=[jax.ShapeDtypeStruct((nh, s, LANES), F32), jax.ShapeDtypeStruct((nh, 1, s), F32)],
        compiler_params=_params(("arbitrary",)),
    )(*_block_tables(pairs), qa, ka, vta)


def _mla_bwd(qa, ka, kta, va, o, do, lse_row, after, *, name, s, nh):
    t = _mla_tile(s)
    nq = s // t

    pairs = [(kb, j) for kb in range(nq) for j in range(kb, nq)]

    def body(kb_ref, qb_ref, q_ref, k_ref, kt_ref, v_ref, o_ref, do_ref, lse_ref, after_ref, dq_hbm, dk_ref, dv_ref,
             dqt_s, dk_s, dv_s, d_s, stage, sem):
        kb, j = kb_ref[pl.program_id(0)], qb_ref[pl.program_id(0)]
        cols = pl.ds(pl.multiple_of(j * t, t), t)

        @pl.when(j == kb)
        def _():
            dk_s[...] = jnp.zeros_like(dk_s)
            dv_s[...] = jnp.zeros_like(dv_s)

        @pl.when(kb == 0)
        def _():
            dqt_s[:, :, cols] = jnp.zeros((nh, LANES, t), F32)
            for h in range(nh):
                d_col = jnp.sum(do_ref[h] * o_ref[h], axis=1, keepdims=True)
                d_s[h, :, cols] = jnp.broadcast_to(d_col, (t, LANES)).T[0:1, :]

        def step(diag):
            q = q_ref[...]
            do_b = do_ref[...].astype(BF16)
            sc = lax.dot_general(k_ref[...], q, BNT_DIMS, preferred_element_type=F32)
            if diag:
                sc = jnp.where(_causal(sc.shape[1:], 1)[None], sc, NEG)
            p = jnp.exp(sc - lse_ref[...])
            dv_s[...] += lax.dot_general(p.astype(BF16), do_b, BNN_DIMS, preferred_element_type=F32)
            dp = lax.dot_general(v_ref[...], do_b, BNT_DIMS, preferred_element_type=F32)
            ds = (p * (dp - d_s[:, :, cols])).astype(BF16)
            dk_s[...] += lax.dot_general(ds, q, BNN_DIMS, preferred_element_type=F32)
            dqt_s[:, :, cols] += lax.dot_general(kt_ref[...], ds, BNN_DIMS, preferred_element_type=F32)

        pl.when(j > kb)(functools.partial(step, False))
        pl.when(j == kb)(functools.partial(step, True))

        @pl.when(j == kb)
        def _():
            slot = kb % 2

            def write_out(sl):
                return pltpu.make_async_copy(stage.at[sl], dq_hbm.at[:, cols, :], sem.at[sl])

            @pl.when(kb >= 2)
            def _():
                write_out(slot).wait()

            for h in range(nh):
                stage[slot, h] = dqt_s[h, :, cols].T
            write_out(slot).start()

            @pl.when(kb == nq - 1)
            def _():
                write_out(slot).wait()
                if nq >= 2:
                    write_out(1 - slot).wait()

        @pl.when(j == nq - 1)
        def _():
            dk_ref[...] = dk_s[...]
            dv_ref[...] = dv_s[...].astype(dv_ref.dtype)

    q_spec = pl.BlockSpec((nh, t, LANES), lambda p, kb, qb: (0, qb[p], 0))
    kv_spec = pl.BlockSpec((nh, t, LANES), lambda p, kb, qb: (0, kb[p], 0))
    kt_spec = pl.BlockSpec((nh, LANES, t), lambda p, kb, qb: (0, 0, kb[p]))
    row_spec = pl.BlockSpec((nh, 1, t), lambda p, kb, qb: (0, 0, qb[p]))
    whole = jax.ShapeDtypeStruct((nh, s, LANES), F32)
    return pl.pallas_call(
        body, name=name,
        grid_spec=pltpu.PrefetchScalarGridSpec(
            num_scalar_prefetch=2, grid=(len(pairs),),
            in_specs=[q_spec, kv_spec, kt_spec, kv_spec, q_spec, q_spec, row_spec, pl.BlockSpec(memory_space=pl.ANY)],
            out_specs=[pl.BlockSpec(memory_space=pl.ANY), kv_spec, kv_spec],
            scratch_shapes=[pltpu.VMEM((nh, LANES, s), F32), pltpu.VMEM((nh, t, LANES), F32),
                            pltpu.VMEM((nh, t, LANES), F32), pltpu.VMEM((nh, 1, s), F32),
                            pltpu.VMEM((2, nh, t, LANES), F32), pltpu.SemaphoreType.DMA((2,))]),
        out_shape=[whole, whole, jax.ShapeDtypeStruct((nh, s, LANES), BF16)],
        compiler_params=_params(("arbitrary",)),
    )(*_block_tables(pairs), qa, ka, kta, va, o, do, lse_row, after)


SWA_PIECE = 128
SWA_KEYS = 2 * SWA_PIECE


def _swa_block(s):
    return min(1024, s)


def _swa_piece(hp_ref, h, q, k_ref, v_ref, qpos0, scale):
    kstart = pl.multiple_of(jnp.maximum(qpos0 - SWA_PIECE, 0), SWA_PIECE)
    k = k_ref[pl.ds(kstart, SWA_KEYS), :].astype(BF16)
    v = v_ref[pl.ds(kstart, SWA_KEYS), :].astype(BF16)
    sc = lax.dot_general(q, k, NT_DIMS, preferred_element_type=F32)
    dist = (qpos0 + lax.broadcasted_iota(jnp.int32, sc.shape, 0)) - (kstart + lax.broadcasted_iota(jnp.int32, sc.shape, 1))
    sc = sc * scale - hp_ref[h, 0] * dist.astype(F32)
    sc = jnp.where((dist >= 0) & (dist < SWA_WINDOW), sc, NEG)
    return kstart, k, v, sc


def _swa_fwd(proj, hp, *, name, s, scale):
    tb = _swa_block(s)
    group = SWA_HEADS // SWA_KV_HEADS
    q_off, k_off, v_off = C_QSW[0] // LANES, C_KSW[0] // LANES, C_VSW[0] // LANES

    def body(hp_ref, q_ref, k_ref, v_ref, o_ref, lse_ref):
        h, i = pl.program_id(0), pl.program_id(1)
        sink = hp_ref[h, 1]
        for r in range(0, tb, SWA_PIECE):
            rows = pl.ds(r, SWA_PIECE)
            _, _, v, sc = _swa_piece(hp_ref, h, q_ref[rows, :].astype(BF16), k_ref, v_ref, i * tb + r, scale)
            m = jnp.maximum(jnp.max(sc, axis=1, keepdims=True), sink)
            p = jnp.exp(sc - m)
            l = jnp.sum(p, axis=1, keepdims=True) + jnp.exp(sink - m)
            o_ref[rows, :] = jnp.dot(p.astype(BF16), v, preferred_element_type=F32) / l
            lse_ref[rows, :] = m + jnp.log(l)

    whole = lambda off: pl.BlockSpec((s, LANES), lambda h, i: (0, off + h // group))
    return pl.pallas_call(
        body, name=name, grid=(SWA_HEADS, s // tb),
        in_specs=[pl.BlockSpec(memory_space=pltpu.SMEM), pl.BlockSpec((tb, LANES), lambda h, i: (i, q_off + h)),
                  whole(k_off), whole(v_off)],
        out_specs=[pl.BlockSpec((tb, LANES), lambda h, i: (i, h)), pl.BlockSpec((None, tb, 1), lambda h, i: (h, i, 0))],
        out_shape=[jax.ShapeDtypeStruct((s, SWA_HEADS * LANES), F32), jax.ShapeDtypeStruct((SWA_HEADS, s, 1), F32)],
        compiler_params=_params(("parallel", "parallel")),
    )(hp, proj, proj, proj)


def _swa_bwd(proj, o, do, lse, hp, *, name, s, scale):
    tb = _swa_block(s)
    nqb = s // tb
    group = SWA_HEADS // SWA_KV_HEADS
    q_off, k_off, v_off = C_QSW[0] // LANES, C_KSW[0] // LANES, C_VSW[0] // LANES

    def body(hp_ref, q_ref, k_ref, v_ref, o_ref, do_ref, lse_ref, dq_ref, dk_ref, dv_ref, dsink_ref):
        kh, g, i = pl.program_id(0), pl.program_id(1), pl.program_id(2)
        h = kh * group + g
        sink = hp_ref[h, 1]

        @pl.when((g == 0) & (i == 0))
        def _():
            dk_ref[...] = jnp.zeros_like(dk_ref)
            dv_ref[...] = jnp.zeros_like(dv_ref)

        @pl.when(i == 0)
        def _():
            dsink_ref[...] = jnp.zeros_like(dsink_ref)

        for r in range(0, tb, SWA_PIECE):
            rows = pl.ds(r, SWA_PIECE)
            q = q_ref[rows, :].astype(BF16)
            dov = do_ref[rows, :]
            do_b = dov.astype(BF16)
            lse_r = lse_ref[rows, :]
            d_r = jnp.sum(dov * o_ref[rows, :], axis=1, keepdims=True)
            kstart, k, v, sc = _swa_piece(hp_ref, h, q, k_ref, v_ref, i * tb + r, scale)
            p = jnp.exp(sc - lse_r)
            dp = lax.dot_general(do_b, v, NT_DIMS, preferred_element_type=F32)
            ds = (p * (dp - d_r)).astype(BF16)
            dq_ref[rows, :] = (jnp.dot(ds, k, preferred_element_type=F32) * scale).astype(dq_ref.dtype)
            win = pl.ds(kstart, SWA_KEYS)
            dk_ref[win, :] += lax.dot_general(ds, q, TN_DIMS, preferred_element_type=F32) * scale
            dv_ref[win, :] += lax.dot_general(p.astype(BF16), do_b, TN_DIMS, preferred_element_type=F32)
            part = jnp.sum(-jnp.exp(sink - lse_r) * d_r, axis=0, keepdims=True)
            dsink_ref[...] += jnp.broadcast_to(part, (1, LANES))

    whole = lambda off: pl.BlockSpec((s, LANES), lambda kh, g, i: (0, off + kh))
    q_map = lambda kh, g, i: (i, kh * group + g)
    return pl.pallas_call(
        body, name=name, grid=(SWA_KV_HEADS, group, nqb),
        in_specs=[pl.BlockSpec(memory_space=pltpu.SMEM),
                  pl.BlockSpec((tb, LANES), lambda kh, g, i: (i, q_off + kh * group + g)), whole(k_off), whole(v_off),
                  pl.BlockSpec((tb, LANES), q_map), pl.BlockSpec((tb, LANES), q_map),
                  pl.BlockSpec((None, tb, 1), lambda kh, g, i: (kh * group + g, i, 0))],
        out_specs=[pl.BlockSpec((tb, LANES), q_map), whole(0), whole(0),
                   pl.BlockSpec((None, 1, LANES), lambda kh, g, i: (kh * group + g, 0, 0))],
        out_shape=[jax.ShapeDtypeStruct((s, SWA_HEADS * LANES), BF16),
                   jax.ShapeDtypeStruct((s, SWA_KV_HEADS * LANES), F32), jax.ShapeDtypeStruct((s, SWA_KV_HEADS * LANES), F32),
                   jax.ShapeDtypeStruct((SWA_HEADS, 1, LANES), F32)],
        compiler_params=_params(("parallel", "arbitrary", "arbitrary")),
    )(hp, proj, proj, proj, o, do, lse)


def _shift_down(z, k):
    rows = lax.broadcasted_iota(jnp.int32, z.shape, 0)
    return jnp.where(rows >= k, pltpu.roll(z, k, 0), 0.0)


def _shift_up(z, k):
    n = z.shape[0]
    rows = lax.broadcasted_iota(jnp.int32, z.shape, 0)
    return jnp.where(rows < n - k, pltpu.roll(z, n - k, 0), 0.0)


def _rows3(a, b, c):
    r = lax.broadcasted_iota(jnp.int32, (3, a.shape[1]), 0)
    return jnp.where(r == 0, a, jnp.where(r == 1, b, c))


def _col_spec(s, off):
    return pl.BlockSpec((s, LANES), functools.partial(lambda j, off: (0, off + j), off=off))


def _conv_fwd(proj, conv_w, *, name, s):
    def body(gb_ref, gc_ref, u_ref, w_ref, y_ref):
        w0, w1, w2 = w_ref[0:1, :], w_ref[1:2, :], w_ref[2:3, :]
        z = gc_ref[...] * u_ref[...]
        c = w2 * z + w1 * _shift_down(z, 1) + w0 * _shift_down(z, 2)
        y_ref[...] = gb_ref[...] * c

    return pl.pallas_call(
        body, name=name, grid=(2,),
        in_specs=[_col_spec(s, C_GB[0] // LANES), _col_spec(s, C_GC[0] // LANES), _col_spec(s, C_UCONV[0] // LANES),
                  pl.BlockSpec((3, LANES), lambda j: (0, j))],
        out_specs=_col_spec(s, 0), out_shape=jax.ShapeDtypeStruct((s, D_GROUP), F32),
        compiler_params=_params(("parallel",)),
    )(proj, proj, proj, conv_w)


def _conv_bwd(dy, proj, conv_w, *, name, s):
    def body(dy_ref, gb_ref, gc_ref, u_ref, w_ref, dgb_ref, dgc_ref, du_ref, dw_ref):
        w0, w1, w2 = w_ref[0:1, :], w_ref[1:2, :], w_ref[2:3, :]
        gc, u, dyv = gc_ref[...], u_ref[...], dy_ref[...]
        z = gc * u
        z1, z2 = _shift_down(z, 1), _shift_down(z, 2)
        c = w2 * z + w1 * z1 + w0 * z2
        dgb_ref[...] = (dyv * c).astype(dgb_ref.dtype)
        dc = dyv * gb_ref[...]
        dz = w2 * dc + w1 * _shift_up(dc, 1) + w0 * _shift_up(dc, 2)
        dgc_ref[...] = (dz * u).astype(dgc_ref.dtype)
        du_ref[...] = (dz * gc).astype(du_ref.dtype)
        dw_ref[...] = _rows3(jnp.sum(dc * z2, axis=0, keepdims=True), jnp.sum(dc * z1, axis=0, keepdims=True),
                             jnp.sum(dc * z, axis=0, keepdims=True))

    act = jax.ShapeDtypeStruct((s, D_GROUP), BF16)
    return pl.pallas_call(
        body, name=name, grid=(2,),
        in_specs=[_col_spec(s, 0), _col_spec(s, C_GB[0] // LANES), _col_spec(s, C_GC[0] // LANES),
                  _col_spec(s, C_UCONV[0] // LANES), pl.BlockSpec((3, LANES), lambda j: (0, j))],
        out_specs=[_col_spec(s, 0), _col_spec(s, 0), _col_spec(s, 0), pl.BlockSpec((3, LANES), lambda j: (0, j))],
        out_shape=[act, act, act, jax.ShapeDtypeStruct((3, D_GROUP), F32)],
        compiler_params=_params(("parallel",)),
    )(dy, proj, proj, proj, conv_w)


def _pool_select(j, lane, a2, a4, a8, a16):
    lo = lane < HEAD
    return jnp.where(j == 0, jnp.where(lo, a2, a4), jnp.where(lo, a8, a16))


def _pooled(u, j):
    s2 = u + _shift_down(u, 1)
    s4 = s2 + _shift_down(s2, 2)
    s8 = s4 + _shift_down(s4, 4)
    s16 = s8 + _shift_down(s8, 8)
    lane = lax.broadcasted_iota(jnp.int32, u.shape, 1)
    rows = lax.broadcasted_iota(jnp.int32, u.shape, 0)
    win = _pool_select(j, lane, *POOL_WINDOWS)
    count = jnp.minimum(rows + 1, win).astype(F32)
    return _pool_select(j, lane, s2, s4, s8, s16) / count - u, count


def _pool_fwd(proj, wbd, scale, *, name, s):
    def body(u_ref, w_ref, sc_ref, y_ref):
        pooled, _ = _pooled(u_ref[...], pl.program_id(0))
        y_ref[...] = jnp.dot(pooled.astype(BF16), w_ref[...].astype(BF16), preferred_element_type=F32) * sc_ref[...]

    return pl.pallas_call(
        body, name=name, grid=(2,),
        in_specs=[_col_spec(s, C_UPOOL[0] // LANES), pl.BlockSpec((None, LANES, LANES), lambda j: (j, 0, 0)),
                  pl.BlockSpec((1, LANES), lambda j: (0, j))],
        out_specs=_col_spec(s, 0), out_shape=jax.ShapeDtypeStruct((s, D_GROUP), F32),
        compiler_params=_params(("parallel",)),
    )(proj, wbd, scale)


def _pool_bwd(dy, proj, wbd, scale, *, name, s):
    def body(dy_ref, u_ref, w_ref, sc_ref, du_ref, dw_ref, dsc_ref):
        j = pl.program_id(0)
        pooled, count = _pooled(u_ref[...], j)
        pooled_b = pooled.astype(BF16)
        w_b = w_ref[...].astype(BF16)
        dyv = dy_ref[...]
        mixed = jnp.dot(pooled_b, w_b, preferred_element_type=F32)
        dsc_ref[...] = jnp.sum(dyv * mixed, axis=0, keepdims=True)
        dms = (dyv * sc_ref[...]).astype(BF16)
        dw_ref[...] = lax.dot_general(pooled_b, dms, (((0,), (0,)), ((), ())), preferred_element_type=F32)
        dpooled = lax.dot_general(dms, w_b, (((1,), (1,)), ((), ())), preferred_element_type=F32)
        r = dpooled / count
        a2 = r + _shift_up(r, 1)
        a4 = a2 + _shift_up(a2, 2)
        a8 = a4 + _shift_up(a4, 4)
        a16 = a8 + _shift_up(a8, 8)
        lane = lax.broadcasted_iota(jnp.int32, r.shape, 1)
        du_ref[...] = (_pool_select(j, lane, a2, a4, a8, a16) - dpooled).astype(du_ref.dtype)

    return pl.pallas_call(
        body, name=name, grid=(2,),
        in_specs=[_col_spec(s, 0), _col_spec(s, C_UPOOL[0] // LANES),
                  pl.BlockSpec((None, LANES, LANES), lambda j: (j, 0, 0)), pl.BlockSpec((1, LANES), lambda j: (0, j))],
        out_specs=[_col_spec(s, 0), pl.BlockSpec((None, LANES, LANES), lambda j: (j, 0, 0)),
                   pl.BlockSpec((1, LANES), lambda j: (0, j))],
        out_shape=[jax.ShapeDtypeStruct((s, D_GROUP), BF16), jax.ShapeDtypeStruct((2, LANES, LANES), F32),
                   jax.ShapeDtypeStruct((1, D_GROUP), F32)],
        compiler_params=_params(("parallel",)),
    )(dy, proj, wbd, scale)


def _mesh_pos():
    return lax.axis_index("x"), lax.axis_index("y"), lax.axis_index("c")


def _any_specs(n):
    return [pl.BlockSpec(memory_space=pl.ANY)] * n


def _all_gather(xs, *, name):
    n = len(xs)

    def body(*refs):
        x_refs, out_refs = refs[:n], refs[n:2 * n]
        send_sems, recv_sems, local_sems = refs[2 * n:]
        x, y, cc = _mesh_pos()
        me, sibling = (x, y, cc), (x, y, 1 - cc)
        chips = [(1 - x, y), (x, 1 - y), (1 - x, 1 - y)]

        def slot(a, px, py, pc):
            return out_refs[a].at[4 * px + 2 * py + pc]

        def copy(a, k, block, to, src=None):
            return pltpu.make_async_remote_copy(
                src_ref=slot(a, *block) if src is None else src, dst_ref=slot(a, *block), send_sem=send_sems.at[a, k],
                recv_sem=recv_sems.at[a, k], device_id=to, device_id_type=pl.DeviceIdType.MESH)

        mine = [pltpu.make_async_copy(x_refs[a], slot(a, *me), local_sems.at[a]) for a in range(n)]
        first = []
        for a in range(n):
            first.append(copy(a, 0, me, sibling, src=x_refs[a]))
            first += [copy(a, 1 + j, me, (*chip, cc), src=x_refs[a]) for j, chip in enumerate(chips)]
        for cp in mine + first:
            cp.start()
        passed = []
        for j, chip in enumerate(chips):
            for a in range(n):
                copy(a, 1 + j, (*chip, cc), me).wait_recv()
                passed.append(copy(a, 4 + j, (*chip, cc), sibling))
                passed[-1].start()
        for a in range(n):
            copy(a, 0, sibling, me).wait_recv()
        for j, chip in enumerate(chips):
            for a in range(n):
                copy(a, 4 + j, (*chip, 1 - cc), me).wait_recv()
        for cp in first + passed:
            cp.wait_send()
        for cp in mine:
            cp.wait()

    return pl.pallas_call(
        body, name=name, out_shape=[jax.ShapeDtypeStruct((N_DEV,) + a.shape, a.dtype) for a in xs],
        in_specs=_any_specs(n), out_specs=_any_specs(n),
        scratch_shapes=[pltpu.SemaphoreType.DMA((n, 7)), pltpu.SemaphoreType.DMA((n, 7)), pltpu.SemaphoreType.DMA((n,))],
    )(*xs)


def _plan_gather_near(src_refs, land_refs, send_sems, recv_sems):
    x, y, cc = _mesh_pos()
    me = 4 * x + 2 * y + cc
    plan = []
    for a, (src, land) in enumerate(zip(src_refs, land_refs)):
        for k, (px, py, pc) in enumerate([(x, y, 1 - cc), (1 - x, y, cc), (x, 1 - y, cc), (1 - x, 1 - y, cc)]):
            sems = dict(send_sem=send_sems.at[4 * a + k], recv_sem=recv_sems.at[4 * a + k], device_id=(px, py, pc),
                        device_id_type=pl.DeviceIdType.MESH)
            plan.append((pltpu.make_async_remote_copy(src_ref=src, dst_ref=land.at[me], **sems),
                         pltpu.make_async_remote_copy(src_ref=src, dst_ref=land.at[4 * px + 2 * py + pc], **sems)))
    return plan


def _plan_gather_pass(src_refs, land_refs, send_sems, recv_sems):
    x, y, cc = _mesh_pos()
    plan = []
    for a, land in enumerate(land_refs):
        for j, (px, py) in enumerate([(1 - x, y), (x, 1 - y), (1 - x, 1 - y)]):
            mine, theirs = land.at[4 * px + 2 * py + cc], land.at[4 * px + 2 * py + 1 - cc]
            sems = dict(send_sem=send_sems.at[3 * a + j], recv_sem=recv_sems.at[3 * a + j], device_id=(x, y, 1 - cc),
                        device_id_type=pl.DeviceIdType.MESH)
            plan.append((pltpu.make_async_remote_copy(src_ref=mine, dst_ref=mine, **sems),
                         pltpu.make_async_remote_copy(src_ref=mine, dst_ref=theirs, **sems)))
    return plan


def _plan_sibling(src_refs, land_refs, send_sems, recv_sems):
    x, y, cc = _mesh_pos()
    plan = []
    for a, (src, land) in enumerate(zip(src_refs, land_refs)):
        cp = pltpu.make_async_remote_copy(
            src_ref=src.at[1 - cc], dst_ref=land, send_sem=send_sems.at[a], recv_sem=recv_sems.at[a],
            device_id=(x, y, 1 - cc), device_id_type=pl.DeviceIdType.MESH)
        plan.append((cp, cp))
    return plan


def _plan_chips(src_refs, land_refs, send_sems, recv_sems):
    x, y, cc = _mesh_pos()
    my_chip = 2 * x + y
    plan = []
    for a, (src, land) in enumerate(zip(src_refs, land_refs)):
        for j, (px, py) in enumerate([(1 - x, y), (x, 1 - y), (1 - x, 1 - y)]):
            peer = 2 * px + py
            sems = dict(send_sem=send_sems.at[3 * a + j], recv_sem=recv_sems.at[3 * a + j], device_id=(px, py, cc),
                        device_id_type=pl.DeviceIdType.MESH)
            plan.append((pltpu.make_async_remote_copy(src_ref=src.at[peer], dst_ref=land.at[my_chip], **sems),
                         pltpu.make_async_remote_copy(src_ref=src.at[peer], dst_ref=land.at[peer], **sems)))
    return plan


HBM_SPEC = pl.BlockSpec(memory_space=pltpu.HBM)
SEM_SPEC = pl.BlockSpec(memory_space=pltpu.SEMAPHORE)
ANY_SPEC = pl.BlockSpec(memory_space=pl.ANY)
SIDE_EFFECT = pltpu.CompilerParams(has_side_effects=pltpu.SideEffectType.DATAFLOW_SIDE_EFFECTING)


def _start_copies(plan, sems_per_array, srcs, lands, after, *, name):
    lands = [lax.empty(l, a.dtype) if isinstance(l, tuple) else l for l, a in zip(lands, srcs or lands)]
    ns, n = len(srcs), len(srcs) + len(lands)

    def body(*refs):
        send_sems, recv_sems = refs[n + len(after)], refs[n + len(after) + 1]
        for out, _ in plan(refs[:ns], refs[ns:n], send_sems, recv_sems):
            out.start()
        refs[-1][...] = jnp.zeros_like(refs[-1])

    sem = pltpu.SemaphoreType.DMA((len(lands) * sems_per_array,))
    res = pl.pallas_call(
        body, name=name,
        out_shape=(sem, sem, *[pltpu.HBM(a.shape, a.dtype) for a in srcs + lands], jax.ShapeDtypeStruct((8, LANES), F32)),
        in_specs=[HBM_SPEC] * n + [ANY_SPEC] * len(after),
        out_specs=(SEM_SPEC, SEM_SPEC, *[HBM_SPEC] * n, pl.BlockSpec(memory_space=pltpu.VMEM)),
        input_output_aliases={i: 2 + i for i in range(n)}, compiler_params=SIDE_EFFECT,
    )(*[pltpu.with_memory_space_constraint(a, pltpu.HBM) for a in srcs + lands], *after)
    return (res[0], res[1], list(res[2:2 + ns]), list(res[2 + ns:2 + n])), res[-1]


def _wait_copies(plan, handle, after, *, name):
    send, recv, srcs, lands = handle
    ns, n = len(srcs), len(srcs) + len(lands)

    def body(*refs):
        for out, inc in plan(refs[:ns], refs[ns:n], refs[n], refs[n + 1]):
            out.wait_send()
            inc.wait_recv()

    res = pl.pallas_call(
        body, name=name, out_shape=tuple(pltpu.HBM(a.shape, a.dtype) for a in srcs + lands),
        in_specs=[HBM_SPEC] * n + [SEM_SPEC, SEM_SPEC] + [ANY_SPEC] * len(after), out_specs=[HBM_SPEC] * n,
        input_output_aliases={i: i for i in range(n)}, compiler_params=SIDE_EFFECT,
    )(*srcs, *lands, send, recv, *after)
    return list(res[:ns]), list(res[ns:])


def _row_tile(rows, target=512):
    if rows <= target:
        return rows
    best = None
    for t in range(8, target + 1, 8):
        if rows % t == 0:
            best = t
    assert best is not None, (rows, target)
    return best


def _add_own(g, other, core, *, name):
    _, _, rows, cols = g.shape
    tm = _row_tile(rows)

    def body(c_ref, g_ref, o_ref, out_ref):
        out_ref[...] = g_ref[...] + o_ref[...]

    return pl.pallas_call(
        body, name=name, out_shape=jax.ShapeDtypeStruct(other.shape, other.dtype),
        grid_spec=pltpu.PrefetchScalarGridSpec(
            num_scalar_prefetch=1, grid=(4, rows // tm),
            in_specs=[pl.BlockSpec((None, None, tm, cols), lambda p, i, c_ref: (c_ref[0], p, i, 0)),
                      pl.BlockSpec((None, tm, cols), lambda p, i, c_ref: (p, i, 0))],
            out_specs=pl.BlockSpec((None, tm, cols), lambda p, i, c_ref: (p, i, 0))),
        compiler_params=_params(("parallel", "parallel")),
    )(core, g, other)


def _adamw(parts, w, m, v, after, *, name):
    layers, rows, cols = w.shape
    assert len(parts) == layers
    tm = _row_tile(rows, 256)
    nr = rows // tm

    def body(*refs):
        p_refs = refs[:layers]
        w_ref, m_ref, v_ref, _, g_ref, d_ref, nm_ref, nv_ref, g_s = refs[layers:]
        for ll in range(layers):
            @pl.when(pl.program_id(0) == ll)
            def _(ll=ll):
                g = p_refs[ll][0]
                for q in range(1, p_refs[ll].shape[0]):
                    g = g + p_refs[ll][q]
                g_s[...] = g

        g = g_s[...]
        mm = ADAM_B1 * m_ref[...] + (1.0 - ADAM_B1) * g
        vv = ADAM_B2 * v_ref[...] + (1.0 - ADAM_B2) * jnp.square(g)
        m_hat = mm / (1.0 - ADAM_B1 ** ADAM_STEP)
        v_hat = vv / (1.0 - ADAM_B2 ** ADAM_STEP)
        g_ref[...] = g
        d_ref[...] = -ADAM_LR * (m_hat / (jnp.sqrt(v_hat) + ADAM_EPS) + ADAM_WD * w_ref[...])
        nm_ref[...] = mm
        nv_ref[...] = vv

    def part_spec(ll, p):
        return pl.BlockSpec((p, tm, cols), lambda l, i: (0, jnp.where(l == ll, i, jnp.where(l < ll, 0, nr - 1)), 0))

    spec = pl.BlockSpec((None, tm, cols), lambda l, i: (l, i, 0))
    out = jax.ShapeDtypeStruct(w.shape, F32)
    return pl.pallas_call(
        body, name=name, grid=(layers, nr),
        in_specs=[part_spec(ll, parts[ll].shape[0]) for ll in range(layers)] + [spec] * 3 + [pl.BlockSpec(memory_space=pl.ANY)],
        out_specs=[spec] * 4, out_shape=[out] * 4, scratch_shapes=[pltpu.VMEM((tm, cols), F32)],
        compiler_params=_params(("arbitrary", "arbitrary")),
    )(*parts, w, m, v, after)


def _pack(arrs):
    flat = jnp.concatenate([a.reshape(-1) for a in arrs])
    rows = -(-flat.shape[0] // (PACK_COLS * 16)) * 16
    return jnp.pad(flat, (0, rows * PACK_COLS - flat.shape[0])).reshape(rows, PACK_COLS)


def _unpack(packed, shapes):
    flat = packed.reshape(-1)
    out, off = [], 0
    for shp in shapes:
        n = int(np.prod(shp))
        out.append(flat[off:off + n].reshape(shp))
        off += n
    return out


def _shards_to_full(g, axis):
    if axis == 0:
        return g.reshape(g.shape[0] * g.shape[1], g.shape[2])
    return jnp.transpose(g, (1, 0, 2)).reshape(g.shape[1], g.shape[0] * g.shape[2])


def _full_to_shards(a, axis):
    if axis == 0:
        return jnp.transpose(a.reshape(4, 2, a.shape[0] // N_DEV, a.shape[1]), (1, 0, 2, 3))
    return jnp.transpose(a.reshape(a.shape[0], 4, 2, a.shape[1] // N_DEV), (2, 1, 0, 3))


def _zeros_like_cols(a, n):
    return jnp.zeros(a.shape[:-1] + (n,), a.dtype)


def _pad_heads(a, n):
    z = _zeros_like_cols(a, HEAD)
    return jnp.concatenate([p for h in range(n) for p in (a[..., h * HEAD:(h + 1) * HEAD], z)], axis=-1)


def _unpad_heads(a, n):
    return jnp.concatenate([a[..., h * LANES:h * LANES + HEAD] for h in range(n)], axis=-1)


def _seg(first, width, sign=1):
    return (width, [(first, sign)])


def _zero(width):
    return (width, [])


def _swapped(first):
    half = MLA_ROPE // 2
    return [_seg(first + half, half, -1), _seg(first, half)]


def _padded_heads(first, n):
    return [s for h in range(n) for s in (_seg(first + HEAD * h, HEAD), _zero(HEAD))]


def _layout_w_in():
    kr = 384
    return (_padded_heads(1440, 4) + [_seg(0, 256), _seg(416, 256), _seg(672, 256), _seg(928, 256), _seg(1184, 256)]
            + _padded_heads(1696, 2) + _padded_heads(1824, 2) + [_seg(256, 128)]
            + [_zero(HEAD), _seg(kr, MLA_ROPE), _seg(kr, MLA_ROPE)] + [_zero(HEAD)] + _swapped(kr) + _swapped(kr))


def _layout_w_uq():
    out = []
    for h in range(MLA_HEADS):
        out += [_seg(96 * h, MLA_NOPE), _seg(96 * h + MLA_NOPE, MLA_ROPE)] + _swapped(96 * h + MLA_NOPE)
    return out


def _layout_w_ukv():
    keys = [s for h in range(MLA_HEADS) for s in (_seg(LANES * h, HEAD), _zero(HEAD))]
    values = [s for h in range(MLA_HEADS) for s in (_seg(LANES * h + HEAD, HEAD), _zero(HEAD))]
    return keys + values


def _layout_w_gate_up():
    return [_seg(half + j, GU_TILE) for j in range(0, D_FF, GU_TILE) for half in (0, D_FF)]


LAYOUTS = dict(w_in=_layout_w_in(), w_uq=_layout_w_uq(), w_ukv=_layout_w_ukv(), w_gate_up=_layout_w_gate_up())
SWAPPED_UPDATE = ("w_gate_up",)
OWN_COLS = dict(w_in=1952, w_uq=384, w_ukv=512, w_gate_up=2 * D_FF)


def _plan_extend(layout, shard):
    plan = []
    for width, terms in layout:
        if not terms:
            plan.append((width, []))
            continue
        (first, sign), = terms
        while width:
            g, off = divmod(first, shard)
            w = min(width, shard - off)
            plan.append((w, [(g, off, sign)]))
            first, width = first + w, width - w
    return [plan]


def _plan_fold(layout, own_cols):
    sources = [[] for _ in range(own_cols)]
    e = 0
    for width, terms in layout:
        for first, sign in terms:
            for i in range(width):
                sources[first + i].append((e + i, sign))
        e += width
    shard = own_cols // N_DEV
    plans = {}
    for g in range(N_DEV):
        plan, n = [], g * shard
        while n < (g + 1) * shard:
            w = 1
            while n + w < (g + 1) * shard and [(c + w, sg) for c, sg in sources[n]] == sources[n + w]:
                w += 1
            plan.append((w, [(0, c, sg) for c, sg in sources[n]]))
            n += w
        plans[g] = plan
    return [plans[2 * p + c] for c in range(2) for p in range(4)]


def _assemble(src, plans, out_cols, out_dtype, *, name, transposed=False):
    g, rows, c = src.shape
    tm = _row_tile(rows, 256)
    pad = -out_cols % LANES if transposed else 0

    def body(s_ref, o_ref):
        blocks = [s_ref[i].astype(F32) for i in range(g)]
        for d, plan in enumerate(plans):
            pieces = []
            for width, terms in plan + ([(pad, [])] if pad else []):
                v = None
                for b, first, sign in terms:
                    t = blocks[b][:, first:first + width]
                    t = -t if sign < 0 else t
                    v = t if v is None else v + t
                pieces.append(jnp.zeros((tm, width), F32) if v is None else v)
            block = pieces[0] if len(pieces) == 1 else jnp.concatenate(pieces, axis=1)
            o_ref[d] = (block.T[:out_cols, :] if transposed else block).astype(o_ref.dtype)

    block, shape = ((out_cols, tm), (out_cols, rows)) if transposed else ((tm, out_cols), (rows, out_cols))
    return pl.pallas_call(
        body, name=name, grid=(rows // tm,), in_specs=[pl.BlockSpec((g, tm, c), lambda i: (0, i, 0))],
        out_specs=pl.BlockSpec((len(plans),) + block, (lambda i: (0, 0, i)) if transposed else (lambda i: (0, i, 0))),
        out_shape=jax.ShapeDtypeStruct((len(plans),) + shape, out_dtype), compiler_params=_params(("parallel",)),
    )(src)


def _extend(nm, gathered, *, name):
    layout = LAYOUTS[nm]
    return _assemble(gathered, _plan_extend(layout, OWN_COLS[nm] // N_DEV), sum(w for w, _ in layout), BF16, name=name)[0]


def _fold_to_shards(nm, grad_ext, *, name):
    shards = _assemble(grad_ext[None], _plan_fold(LAYOUTS[nm], OWN_COLS[nm]), OWN_COLS[nm] // N_DEV, F32, name=name,
                       transposed=nm in SWAPPED_UPDATE)
    return shards.reshape((2, 4) + shards.shape[1:])


def _rope_tables(s):
    inv = 1.0 / (ROPE_THETA ** (jnp.arange(0, MLA_ROPE, 2, dtype=F32) / MLA_ROPE))
    ang = jnp.arange(s, dtype=F32)[:, None] * inv[None, :]
    cos, sin = jnp.cos(ang), jnp.sin(ang)
    c32, s32 = jnp.concatenate([cos, cos], axis=1), jnp.concatenate([sin, sin], axis=1)
    zeros, ones = jnp.zeros((s, HEAD), F32), jnp.ones((s, HEAD), F32)
    tq = jnp.concatenate([ones, c32, s32], axis=1) * (1.0 / math.sqrt(MLA_NOPE + MLA_ROPE))
    return (jnp.tile(tq, (1, MLA_HEADS)), jnp.concatenate([zeros, c32, c32], axis=1),
            jnp.concatenate([zeros, s32, s32], axis=1))


def _gn(y):
    return y * _rstd(y, D_GROUP)


def _mixer_fwd(x, w, tabs, l, after_attention=None):
    s = x.shape[0]
    tq, tkc, tks = tabs
    n = lambda t: f"l{l}_{t}"
    h = _rms_fwd(("row", x, D_MODEL, 0), w["attn_norm"], name=n("attn_norm"), rows=s, width=D_MODEL)
    proj = _mm(h, w["w_in"], name=n("in_proj"))
    def prep(cq, ckv, kr, krp, gq, gkv, wuq, wukv, tqv, tc, ts):
        cqn = (cq * _rstd(cq, 256) * gq).astype(BF16)
        ckvn = (ckv * _rstd(ckv, 128) * gkv).astype(BF16)
        qe = jnp.dot(cqn, wuq, preferred_element_type=F32)
        kve = jnp.dot(ckvn, wukv, preferred_element_type=F32)
        kb = kr * tc + krp * ts
        kvv = kve[:, 512:]
        lane = lax.broadcasted_iota(jnp.int32, kvv.shape, 1) & (LANES - 1)
        v = jnp.where(lane == HEAD, 1.0, kvv)
        k = kve[:, :512] + jnp.tile(kb, (1, MLA_HEADS))
        return cqn, ckvn, qe * tqv, k, k, v, v

    cqn, ckvn, qm, km, kmt, vm, vmt = _rowwise(
        prep, name=n("mla_prep"), rows=s,
        ins=[("row", proj, 256, C_CQ[0] // 256), ("row", proj, 128, C_CKV[0] // 128),
             ("row", proj, 128, C_KR[0] // 128), ("row", proj, 128, C_KRP[0] // 128),
             ("full", w["mla_q_norm"]), ("full", w["mla_kv_norm"]), ("full", w["w_uq"]), ("full", w["w_ukv"]),
             ("row", tq, 512, 0), ("row", tkc, 128, 0), ("row", tks, 128, 0)],
        outs=[(256, BF16), (128, BF16), (512, BF16, MLA_HEADS), (512, BF16, MLA_HEADS), (512, BF16, MLA_HEADS, "T"),
              (512, BF16, MLA_HEADS), (512, BF16, MLA_HEADS, "T")])
    y_a, lse_row = _mla_fwd(qm, km, vmt, name=n("mla_fwd"), s=s, nh=MLA_HEADS)
    mix_norm = w["mix_norm"]
    if after_attention is not None:
        mix_norm = mix_norm + after_attention(y_a)[0, 0]
    y_b = _conv_fwd(proj, w["conv_w"], name=n("conv_fwd"), s=s)
    y_c = _pool_fwd(proj, w["pool_wbd"], w["pool_scale"], name=n("pool_fwd"), s=s)
    y_d, lse_d = _swa_fwd(proj, w["hp_swa"], name=n("swa_fwd"), s=s, scale=1.0 / math.sqrt(HEAD))

    def mix(ya, yb, yc, yd, mn):
        return (jnp.concatenate([_gn(_unpad_heads(ya, 4)), _gn(yb), _gn(yc), _gn(_unpad_heads(yd, 4))], axis=1) * mn,)

    mixed = _rowwise(mix, name=n("group_norm"), rows=s,
                     ins=[("heads", y_a), ("row", y_b, 256, 0), ("row", y_c, 256, 0), ("row", y_d, 512, 0),
                          ("full", mix_norm)], outs=[(D_MODEL, BF16)])[0]
    x1 = _mm(mixed, w["w_o"], res=x, name=n("out_proj"))
    saved = dict(x=x, h=h, proj=proj, cqn=cqn, ckvn=ckvn, qm=qm, km=km, kmt=kmt, vm=vm, y_a=y_a, lse_row=lse_row,
                 y_b=y_b, y_c=y_c, y_d=y_d, lse_d=lse_d, mixed=mixed)
    return x1, saved


def _ffn_fwd(x1, w, l):
    s = x1.shape[0]
    n = lambda t: f"l{l}_{t}"
    h2 = _rms_fwd(("row", x1, D_MODEL, 0), w["ffn_norm"], name=n("ffn_norm"), rows=s, width=D_MODEL)

    def swiglu(gu):
        g, u = gu[:, :GU_TILE], gu[:, GU_TILE:]
        return gu, g * jax.nn.sigmoid(g) * u

    gu, act = _mm(h2, w["w_gate_up"], tm=2048, tn=2 * GU_TILE, name=n("gate_up"),
                  epilogue=(swiglu, [], [(2 * D_FF, BF16), (D_FF, BF16)]))
    x2 = _mm(act, w["w_down"], res=x1, tk=D_FF // 2, name=n("down"))
    return x2, dict(x1=x1, h2=h2, gu=gu, act=act)


def _ffn_bwd_down(dx2, sv, w, l):
    n = lambda t: f"l{l}_{t}"

    def swiglu_bwd(da, gu):
        gt, u = gu[:, :GU_TILE].astype(F32), gu[:, GU_TILE:].astype(F32)
        sg = jax.nn.sigmoid(gt)
        return (jnp.concatenate([da * u * sg * (1.0 + gt * (1.0 - sg)), da * gt * sg], axis=1),)

    dgu = _mm(dx2[1], w["w_down"], tb=True, tm=2048, tn=GU_TILE, name=n("d_act"),
              epilogue=(swiglu_bwd, [sv["gu"]], [(2 * D_FF, BF16)]))[0]
    g = dict(w_down=_mm(sv["act"], dx2[1], ta=True, tm=D_FF // 2, name=n("dw_down")))
    return dgu, g


def _ffn_bwd_up(dx2, dgu, sv, w, l):
    s = dgu.shape[0]
    n = lambda t: f"l{l}_{t}"
    dh2 = _mm(dgu, w["w_gate_up"], tb=True, tk=D_FF // 2, name=n("d_h2"))
    g = dict(w_gate_up=_mm(sv["h2"], dgu, ta=True, tn=D_FF // 2, name=n("dw_gate_up")))
    dx1, dx1_b, g["ffn_norm"] = _rms_bwd(("row", sv["x1"], D_MODEL, 0), w["ffn_norm"], dh2, dx2[0],
                                         name=n("ffn_norm_bwd"), rows=s, width=D_MODEL, out_dtypes=(F32, BF16))
    return (dx1, dx1_b), g


def _mixer_bwd_out(dx1, sv, w, l):
    s = dx1[1].shape[0]
    n = lambda t: f"l{l}_{t}"
    dmixed = _mm(dx1[1], w["w_o"], tb=True, name=n("d_mixed"))
    g = dict(w_o=_mm(sv["mixed"], dx1[1], ta=True, name=n("dw_o")))

    def mix_bwd(dm, ya, yb, yc, yd, mn):
        outs, dmn = [], []
        for i, y in enumerate((_unpad_heads(ya, 4), yb, yc, _unpad_heads(yd, 4))):
            lo, hi = i * D_GROUP, (i + 1) * D_GROUP
            r = _rstd(y, D_GROUP)
            nrm = y * r
            dmg = dm[:, lo:hi]
            dn = dmg * mn[:, lo:hi]
            dy = r * (dn - nrm * (jnp.sum(dn * nrm, axis=-1, keepdims=True) * (1.0 / D_GROUP)))
            outs.append(_pad_heads(dy, 4) if i in (0, 3) else dy)
            dmn.append(jnp.sum(dmg * nrm, axis=0, keepdims=True))
        return (*outs, jnp.concatenate(dmn, axis=1))

    dy_a, dy_b, dy_c, dy_d, g["mix_norm"] = _rowwise(
        mix_bwd, name=n("group_norm_bwd"), rows=s,
        ins=[("row", dmixed, D_MODEL, 0), ("heads", sv["y_a"]), ("row", sv["y_b"], 256, 0),
             ("row", sv["y_c"], 256, 0), ("row", sv["y_d"], 512, 0), ("full", w["mix_norm"])],
        outs=[(512, F32, MLA_HEADS), (256, F32), (256, F32), (512, F32)], reds=[(1, D_MODEL)])
    return (dy_a, dy_b, dy_c, dy_d), g


def _mixer_bwd_in(dx1, dys, sv, w, tabs, l):
    s = dx1[0].shape[0]
    tq, tkc, tks = tabs
    n = lambda t: f"l{l}_{t}"
    dy_a, dy_b, dy_c, dy_d = dys
    g = {}

    proj = sv["proj"]
    dq_sw, dk_sw, dv_sw, dsink = _swa_bwd(proj, sv["y_d"], dy_d, sv["lse_d"], w["hp_swa"], name=n("swa_bwd"), s=s,
                                          scale=1.0 / math.sqrt(HEAD))
    g["swa_sinks"] = dsink[:, 0, 0]

    dqm, dkm, dvm = _mla_bwd(sv["qm"], sv["km"], sv["kmt"], sv["vm"], sv["y_a"], dy_a, sv["lse_row"], w["hp_swa"],
                             name=n("mla_bwd"), s=s, nh=MLA_HEADS)

    def rms_bwd(x, gv, dy, width):
        r = _rstd(x, width)
        dyg = dy * gv
        dx = r * dyg - x * (r * r * r) * (jnp.sum(dyg * x, axis=-1, keepdims=True) * (1.0 / width))
        return dx, jnp.sum(dy * x * r, axis=0, keepdims=True)

    def prep_bwd(dq, dk, dv, cq, ckv, gq, gkv, wuq, wukv, tqv, tc, ts):
        dkb = dk[:, 0:128] + dk[:, 128:256] + dk[:, 256:384] + dk[:, 384:512]
        dq_ext = (dq * tqv).astype(BF16)
        dkv_ext = jnp.concatenate([dk.astype(BF16), dv], axis=1)
        dcqn = lax.dot_general(dq_ext, wuq, NT_DIMS, preferred_element_type=F32)
        dckvn = lax.dot_general(dkv_ext, wukv, NT_DIMS, preferred_element_type=F32)
        dcq, dgq = rms_bwd(cq, gq, dcqn, 256)
        dckv, dgkv = rms_bwd(ckv, gkv, dckvn, 128)
        return dq_ext, dkv_ext, dkb * tc, dkb * ts, dcq, dckv, dgq, dgkv

    dq_ext, dkv_ext, dkr, dkrp, dcq, dckv, g["mla_q_norm"], g["mla_kv_norm"] = _rowwise(
        prep_bwd, name=n("mla_prep_bwd"), rows=s,
        ins=[("heads", dqm), ("heads", dkm), ("heads", dvm), ("row", proj, 256, C_CQ[0] // 256),
             ("row", proj, 128, C_CKV[0] // 128), ("full", w["mla_q_norm"]), ("full", w["mla_kv_norm"]),
             ("full", w["w_uq"]), ("full", w["w_ukv"]), ("row", tq, 512, 0), ("row", tkc, 128, 0), ("row", tks, 128, 0)],
        outs=[(512, BF16), (1024, BF16), (128, BF16), (128, BF16), (256, BF16), (128, BF16)],
        reds=[(1, 256), (1, 128)])
    g["w_uq"] = _mm(sv["cqn"], dq_ext, ta=True, name=n("dw_uq"))
    g["w_ukv"] = _mm(sv["ckvn"], dkv_ext, ta=True, name=n("dw_ukv"))

    dgb, dgc, duc, g["conv_w"] = _conv_bwd(dy_b, proj, w["conv_w"], name=n("conv_bwd"), s=s)
    dup, g["pool_wbd"], g["pool_scale"] = _pool_bwd(dy_c, proj, w["pool_wbd"], w["pool_scale"], name=n("pool_bwd"), s=s)

    dproj = jnp.concatenate([dq_sw, dcq, dgb, dgc, duc, dup, dk_sw.astype(BF16), dv_sw.astype(BF16), dckv, dkr, dkrp],
                            axis=1)
    g["w_in"] = _mm(sv["h"], dproj, ta=True, name=n("dw_in"))
    return dproj, g


def _mixer_bwd_norm(dx1, dproj, sv, w, l):
    n = lambda t: f"l{l}_{t}"
    dh = _mm(dproj, w["w_in"], tb=True, name=n("d_h"))
    dx0, dx0_b, dg = _rms_bwd(("row", sv["x"], D_MODEL, 0), w["attn_norm"], dh, dx1[0], name=n("attn_norm_bwd"),
                              rows=dh.shape[0], width=D_MODEL, out_dtypes=(F32, BF16))
    return (dx0, dx0_b), dict(attn_norm=dg)


def _loss_head(x, target, g, *, s):
    def fn(xv, tv, gv):
        r = _rstd(xv, D_MODEL)
        e = xv * r * gv - tv
        part = jnp.sum(jnp.sum(e * e, axis=1, keepdims=True), axis=0, keepdims=True) * (0.5 / D_MODEL)
        dy = e * (1.0 / D_MODEL)
        dyg = dy * gv
        dx = r * dyg - xv * (r * r * r) * (jnp.sum(dyg * xv, axis=-1, keepdims=True) * (1.0 / D_MODEL))
        return dx, dx, jnp.sum(dy * xv * r, axis=0, keepdims=True), jnp.broadcast_to(part, (1, LANES))

    return _rowwise(fn, name="loss_head", rows=s,
                    ins=[("row", x, D_MODEL, 0), ("row", target, D_MODEL, 0), ("full", g)],
                    outs=[(D_MODEL, F32), (D_MODEL, BF16)], reds=[(1, D_MODEL), (1, LANES)])


def _alibi_slopes(n):
    return np.asarray([2.0 ** (-8.0 * (i + 1) / n) for i in range(n)], dtype=np.float32)


MIXER_WEIGHTS = ("w_in", "w_uq", "w_ukv", "conv_w", "w_o")
FFN_WEIGHTS = ("w_gate_up", "w_down")


def _mixer_weights(full, rep, l):
    pw = rep["pool_w"][l]
    z = jnp.zeros((HEAD, HEAD), F32)
    wbd = jnp.stack([jnp.block([[pw[2 * j], z], [z, pw[2 * j + 1]]]) for j in range(2)])
    return dict(
        attn_norm=rep["attn_norm"][l][None], w_in=full["w_in"], mla_q_norm=rep["mla_q_norm"][l][None],
        w_uq=full["w_uq"], mla_kv_norm=rep["mla_kv_norm"][l][None], w_ukv=full["w_ukv"],
        conv_w=full["conv_w"], pool_wbd=wbd, pool_scale=rep["pool_scale"][l][None],
        mix_norm=rep["mix_norm"][l][None], w_o=full["w_o"],
        hp_swa=jnp.stack([jnp.asarray(_alibi_slopes(SWA_HEADS)), rep["swa_sinks"][l]], axis=1))


def _ffn_weights(full, rep, l):
    return dict(ffn_norm=rep["ffn_norm"][l][None], w_gate_up=full["w_gate_up"], w_down=full["w_down"])


def _small_grads(g):
    rows = ("attn_norm", "mla_q_norm", "mla_kv_norm", "pool_scale", "ffn_norm", "mix_norm")
    out = {nm: g[nm][0] for nm in rows if nm in g}
    if "swa_sinks" in g:
        out["swa_sinks"] = g["swa_sinks"]
    if "pool_wbd" in g:
        e = g["pool_wbd"]
        out["pool_w"] = jnp.stack([e[j // 2][HEAD * (j % 2):HEAD * (j % 2 + 1), HEAD * (j % 2):HEAD * (j % 2 + 1)]
                                   for j in range(4)])
    return out


def kernel(x, attn_norm, w_in, mla_q_norm, w_uq, mla_kv_norm, w_ukv, conv_w, pool_w, pool_scale, swa_sinks, mix_norm, w_o, ffn_norm, w_gate_up, w_down, final_norm, loss_target, m_attn_norm, m_w_in, m_mla_q_norm, m_w_uq, m_mla_kv_norm, m_w_ukv, m_conv_w, m_pool_w, m_pool_scale, m_swa_sinks, m_mix_norm, m_w_o, m_ffn_norm, m_w_gate_up, m_w_down, m_final_norm, v_attn_norm, v_w_in, v_mla_q_norm, v_w_uq, v_mla_kv_norm, v_w_ukv, v_conv_w, v_pool_w, v_pool_scale, v_swa_sinks, v_mix_norm, v_w_o, v_ffn_norm, v_w_gate_up, v_w_down, v_final_norm):
    given = dict(locals())
    sh_names = [nm for nm, _, _ in SHARDED]
    sh_axis = {nm: ax - 1 for nm, _, ax in SHARDED}
    rep_names = [nm for nm, _ in REPLICATED]
    rep_shapes = [shp for _, shp in REPLICATED]
    rep = {nm: given[nm] for nm in rep_names if nm != "loss"}
    me = 4 * lax.axis_index("x") + 2 * lax.axis_index("y") + lax.axis_index("c")
    my_chip = 2 * lax.axis_index("x") + lax.axis_index("y")
    core = lax.axis_index("c").astype(jnp.int32).reshape(1)

    def behind(token, a):
        return a + token[0, 0].astype(a.dtype)

    def wire(nm, l):
        if nm == "conv_w":
            return lax.bitcast_convert_type(given[nm][l], BF16).reshape(3, -1)
        return given[nm][l].astype(BF16)

    def whole(nm, g, tag):
        if nm in LAYOUTS:
            return _extend(nm, g, name=f"extend_{nm}_{tag}")
        if nm == "conv_w":
            g = lax.bitcast_convert_type(g.reshape(N_DEV, 3, -1, 2), F32)
        return _shards_to_full(g, sh_axis[nm])

    def near_start(names, l, after, tag):
        srcs = [wire(nm, l) for nm in names]
        return _start_copies(_plan_gather_near, 4, srcs, [(N_DEV,) + a.shape for a in srcs], after, name=f"start_gather_{tag}")

    def pass_on(handle, after, tag):
        srcs, lands = _wait_copies(_plan_gather_near, handle, after, name=f"wait_gather_{tag}")
        handle, token = _start_copies(_plan_gather_pass, 3, [], lands, [], name=f"start_pass_{tag}")
        return (srcs, handle), token

    def gathered(names, state, after, tag):
        srcs, handle = state
        _, lands = _wait_copies(_plan_gather_pass, handle, after, name=f"wait_pass_{tag}")
        return {nm: whole(nm, lax.dynamic_update_index_in_dim(land, src, me, 0), tag)
                for nm, src, land in zip(names, srcs, lands)}

    layer1 = MIXER_WEIGHTS + FFN_WEIGHTS
    got = _all_gather([wire(nm, 0) for nm in MIXER_WEIGHTS], name="gather_mixer0")
    full_m0 = {nm: whole(nm, g, "mixer0") for nm, g in zip(MIXER_WEIGHTS, got)}
    h_f0, tok = near_start(FFN_WEIGHTS, 0, [], "ffn0")
    h_l1, tok = near_start(layer1, 1, [tok], "layer1")

    xs, target = x[0], loss_target[0]
    s = xs.shape[0]
    tabs = _rope_tables(s)
    wm, wf, svm, svf = [None] * DEPTH, [None] * DEPTH, [None] * DEPTH, [None] * DEPTH
    wm[0] = _mixer_weights(full_m0, rep, 0)
    wm[0]["attn_norm"] = behind(tok, wm[0]["attn_norm"])
    passing = {}

    def pass_ffn0(y_a):
        passing["ffn0"], token = pass_on(h_f0, [y_a], "ffn0")
        return token

    x1, svm[0] = _mixer_fwd(xs, wm[0], tabs, 0, after_attention=pass_ffn0)
    wf[0] = _ffn_weights(gathered(FFN_WEIGHTS, passing["ffn0"], [x1], "ffn0"), rep, 0)
    passing["layer1"], tok = pass_on(h_l1, [x1], "layer1")
    wf[0]["ffn_norm"] = behind(tok, wf[0]["ffn_norm"])
    x2, svf[0] = _ffn_fwd(x1, wf[0], 0)
    full_1 = gathered(layer1, passing["layer1"], [x2], "layer1")
    wm[1], wf[1] = _mixer_weights(full_1, rep, 1), _ffn_weights(full_1, rep, 1)
    x1, svm[1] = _mixer_fwd(x2, wm[1], tabs, 1)
    x2, svf[1] = _ffn_fwd(x1, wf[1], 1)
    dx_f, dx_b, d_final, loss = _loss_head(x2, target, rep["final_norm"][None], s=s)
    dx = (dx_f, dx_b)

    parts = {}

    def reduce_start(grads, l, after, tag):
        names = [nm for nm in sh_names if nm in grads]
        mine = [_fold_to_shards(nm, grads[nm], name=f"fold_{nm}_{l}") if nm in LAYOUTS
                else _full_to_shards(grads[nm], sh_axis[nm]) for nm in names]
        handle, token = _start_copies(_plan_sibling, 1, mine, [m.shape[1:] for m in mine], after, name=f"start_sibling_{tag}")
        return (names, l, handle), token

    def reduce_mid(state, after, tag):
        names, l, handle = state
        mine, theirs = _wait_copies(_plan_sibling, handle, after, name=f"wait_sibling_{tag}")
        sums = [_add_own(g, o, core, name=f"chip_sum_{nm}_{l}") for nm, g, o in zip(names, mine, theirs)]
        handle, token = _start_copies(_plan_chips, 3, sums, [a.shape for a in sums], [], name=f"start_chips_{tag}")
        return (names, l, handle), token

    def reduce_end(state, after, tag):
        names, l, handle = state
        sums, lands = _wait_copies(_plan_chips, handle, after, name=f"wait_chips_{tag}")
        for nm, own, land in zip(names, sums, lands):
            parts[nm, l] = lax.dynamic_update_index_in_dim(land, lax.dynamic_index_in_dim(own, my_chip, 0, keepdims=False),
                                                           my_chip, 0)

    small = [None] * DEPTH
    in_flight = []
    pending = None
    for l in reversed(range(DEPTH)):
        dgu, g_down = _ffn_bwd_down(dx, svf[l], wf[l], l)
        if pending is not None:
            state, token = reduce_mid(pending, [dgu], f"mixer{l + 1}")
            in_flight.append((state, f"mixer{l + 1}"))
            wf[l]["ffn_norm"] = behind(token, wf[l]["ffn_norm"])
        dx1, g_up = _ffn_bwd_up(dx, dgu, svf[l], wf[l], l)
        g_ffn = {**g_down, **g_up}
        state, token = reduce_start(g_ffn, l, [], f"ffn{l}")
        wm[l]["mix_norm"] = behind(token, wm[l]["mix_norm"])
        dys, g_out = _mixer_bwd_out(dx1, svm[l], wm[l], l)
        state, token = reduce_mid(state, [dys[0]], f"ffn{l}")
        in_flight.append((state, f"ffn{l}"))
        wm[l]["hp_swa"] = behind(token, wm[l]["hp_swa"])
        dproj, g_in = _mixer_bwd_in(dx1, dys, svm[l], wm[l], tabs, l)
        g_mixer = {**g_out, **g_in}
        pending, token = reduce_start(g_mixer, l, [], f"mixer{l}")
        wm[l]["attn_norm"] = behind(token, wm[l]["attn_norm"])
        dx, g_norm = _mixer_bwd_norm(dx1, dproj, svm[l], wm[l], l)
        small[l] = _small_grads({**g_ffn, **g_mixer, **g_norm})
    grads = {nm: jnp.stack([small[l][nm] for l in range(DEPTH)]) for nm in rep_names if nm in small[0]}
    grads["final_norm"] = d_final[0]
    grads["loss"] = loss[0, :1]
    zero = jnp.zeros((1,), F32)
    small = _all_gather([behind(token, _pack([grads[nm] for nm in rep_names]))], name="gather_small_grads")

    last, token = reduce_mid(pending, [dx[0], small[0]], "mixer0")
    for state, tag in in_flight:
        reduce_end(state, [], tag)
    grad_x = dx[0]

    def adamw(nm, after):
        swap = (lambda a: jnp.swapaxes(a, 1, 2)) if nm in SWAPPED_UPDATE else (lambda a: a)
        res = _adamw([parts[nm, l] for l in range(DEPTH)], swap(given[nm]), swap(given["m_" + nm]), swap(given["v_" + nm]),
                     after, name=f"adamw_{nm}")
        return [swap(a) for a in res]

    sh_out = {nm: adamw(nm, token) for nm in FFN_WEIGHTS}
    packs = [_pack([given.get(pre + nm, zero) for nm in rep_names])[None] for pre in ("", "m_", "v_")]
    rep_res = _adamw(small, *packs, token, name="adamw_replicated")
    rep_out = [dict(zip(rep_names, _unpack(o[0], rep_shapes))) for o in rep_res]

    reduce_end(last, [rep_res[0], sh_out["w_down"][0]], "mixer0")
    sh_out.update({nm: adamw(nm, token) for nm in MIXER_WEIGHTS})

    out = [rep_out[0]["loss"][0], grad_x[None]]
    for i in range(4):
        out += [sh_out[nm][i] if nm in sh_axis else rep_out[i][nm] for nm in WEIGHT_ORDER]
    return tuple(out)
```

```python
import functools
import math

import numpy as np
import jax
import jax.numpy as jnp
from jax import lax
from jax.experimental import pallas as pl
from jax.experimental.pallas import tpu as pltpu

F32 = jnp.float32
BF16 = jnp.bfloat16

D_MODEL = 1024
DEPTH = 2
D_GROUP = 256
MLA_HEADS = 4
MLA_NOPE = 64
MLA_ROPE = 32
ROPE_THETA = 10000.0
POOL_WINDOWS = (2, 4, 8, 16)
SWA_HEADS = 4
SWA_KV_HEADS = 2
SWA_WINDOW = 128
D_FF = 2816
GU_TILE = 256
RMS_EPS = 1e-6
LANES = 128
HEAD = 64
VMEM_LIMIT = 48 * 1024 * 1024
NEG = -1e30

ADAM_LR = 0.001
ADAM_B1 = 0.9
ADAM_B2 = 0.999
ADAM_EPS = 1e-08
ADAM_WD = 0.01
ADAM_STEP = 10

N_DEV = 8
PACK_COLS = 1024

C_QSW, C_CQ, C_GB, C_GC, C_UCONV, C_UPOOL = (0, 512), (512, 256), (768, 256), (1024, 256), (1280, 256), (1536, 256)
C_KSW, C_VSW, C_CKV, C_KR, C_KRP = (1792, 256), (2048, 256), (2304, 128), (2432, 128), (2560, 128)

SHARDED = (("w_in", (DEPTH, 1024, 244), 2), ("w_uq", (DEPTH, 256, 48), 2), ("w_ukv", (DEPTH, 128, 64), 2),
           ("conv_w", (DEPTH, 3, 32), 2), ("w_o", (DEPTH, 128, 1024), 1), ("w_gate_up", (DEPTH, 1024, 704), 2),
           ("w_down", (DEPTH, 352, 1024), 1))
REPLICATED = (("attn_norm", (DEPTH, 1024)), ("mla_q_norm", (DEPTH, 256)), ("mla_kv_norm", (DEPTH, 128)),
              ("pool_w", (DEPTH, 4, 64, 64)), ("pool_scale", (DEPTH, 256)), ("swa_sinks", (DEPTH, 4)),
              ("mix_norm", (DEPTH, 1024)), ("ffn_norm", (DEPTH, 1024)), ("final_norm", (1024,)), ("loss", (1,)))
WEIGHT_ORDER = ("attn_norm", "w_in", "mla_q_norm", "w_uq", "mla_kv_norm", "w_ukv", "conv_w", "pool_w", "pool_scale",
                "swa_sinks", "mix_norm", "w_o", "ffn_norm", "w_gate_up", "w_down", "final_norm")


def _params(sem):
    return pltpu.CompilerParams(dimension_semantics=sem, vmem_limit_bytes=VMEM_LIMIT)


def _pick(dim, target):
    if dim <= target:
        return dim
    best = None
    for t in range(LANES, target + 1, LANES):
        if dim % t == 0:
            best = t
    assert best is not None, (dim, target)
    return best


def _mm(a, b, *, name, ta=False, tb=False, res=None, out_dtype=F32, tm=1024, tn=1024, tk=1024, epilogue=None):
    m, k = (a.shape[1], a.shape[0]) if ta else a.shape
    n = b.shape[0] if tb else b.shape[1]
    assert (b.shape[1] if tb else b.shape[0]) == k
    tm, tn, tk = _pick(m, tm), _pick(n, tn), _pick(k, tk)
    nj, nk = n // tn, k // tk
    dims = (((0 if ta else 1,), (1 if tb else 0,)), ((), ()))
    fn, extra, outs = epilogue if epilogue is not None else (None, [], [(n, out_dtype)])
    if res is not None:
        assert epilogue is None
        fn, extra = (lambda acc, r: (acc + r,)), [res]
    n_in, n_out = 2 + len(extra), len(outs)

    def body(*refs):
        a_ref, b_ref, acc_ref = refs[0], refs[1], refs[-1]
        kk = pl.program_id(2)

        def product():
            return lax.dot_general(a_ref[...].astype(BF16), b_ref[...].astype(BF16), dims, preferred_element_type=F32)

        def finish(acc):
            tiles = (acc,) if fn is None else fn(acc, *[r[...] for r in refs[2:n_in]])
            for o_ref, tile in zip(refs[n_in:n_in + n_out], tiles):
                o_ref[...] = tile.astype(o_ref.dtype)

        if nk == 1:
            finish(product())
            return

        @pl.when(kk == 0)
        def _():
            acc_ref[...] = jnp.zeros_like(acc_ref)

        acc_ref[...] += product()

        @pl.when(kk == nk - 1)
        def _():
            finish(acc_ref[...])

    def col_tiles(width):
        assert width % (nj * LANES) == 0, (width, nj)
        return pl.BlockSpec((tm, width // nj), lambda i, j, kk: (i, j))

    a_spec = pl.BlockSpec((tk, tm), lambda i, j, kk: (kk, i)) if ta else pl.BlockSpec((tm, tk), lambda i, j, kk: (i, kk))
    b_spec = pl.BlockSpec((tn, tk), lambda i, j, kk: (j, kk)) if tb else pl.BlockSpec((tk, tn), lambda i, j, kk: (kk, j))
    res_ = pl.pallas_call(
        body, name=name, grid=(m // tm, nj, nk), in_specs=[a_spec, b_spec] + [col_tiles(e.shape[1]) for e in extra],
        out_specs=[col_tiles(w) for w, _ in outs],
        out_shape=[jax.ShapeDtypeStruct((m, w), dt) for w, dt in outs],
        scratch_shapes=[pltpu.VMEM((tm, tn), F32)] if nk > 1 else [],
        compiler_params=_params(("parallel", "parallel", "arbitrary")),
    )(a, b, *extra)
    return res_[0] if epilogue is None else res_


def _mm_nt_deep(a, b, *, name, tm=512, nk=4):
    m, k = a.shape
    n = b.shape[0]
    tm = min(tm, m)
    tk, ni = k // nk, m // tm
    assert k % (nk * LANES) == 0 and m % tm == 0

    def body(a_hbm, b_ref, o_ref, abuf, sem, acc):
        i = pl.program_id(0)

        def chunk(row, kk):
            return pltpu.make_async_copy(a_hbm.at[pl.ds(pl.multiple_of(row * tm, tm), tm), pl.ds(kk * tk, tk)],
                                         abuf.at[kk], sem.at[kk])

        @pl.when(i == 0)
        def _():
            for kk in range(nk):
                chunk(0, kk).start()

        for kk in range(nk):
            chunk(i, kk).wait()
            part = lax.dot_general(abuf[kk], b_ref[:, kk * tk:(kk + 1) * tk], NT_DIMS, preferred_element_type=F32)
            if kk == 0:
                acc[...] = part
            else:
                acc[...] += part

            @pl.when(i + 1 < ni)
            def _(kk=kk):
                chunk(i + 1, kk).start()

        o_ref[...] = acc[...]

    return pl.pallas_call(
        body, name=name, grid=(ni,),
        in_specs=[pl.BlockSpec(memory_space=pl.ANY), pl.BlockSpec((n, k), lambda i: (0, 0))],
        out_specs=pl.BlockSpec((tm, n), lambda i: (i, 0)), out_shape=jax.ShapeDtypeStruct((m, n), F32),
        scratch_shapes=[pltpu.VMEM((nk, tm, tk), a.dtype), pltpu.SemaphoreType.DMA((nk,)), pltpu.VMEM((tm, n), F32)],
        compiler_params=_params(("arbitrary",)),
    )(a, b)


def _rowwise(fn, *, name, rows, ins, outs, reds=(), tm=512):
    tm = min(tm, rows)
    assert rows % tm == 0
    n_in, n_out = len(ins), len(outs)

    def body(*refs):
        vals = [jnp.concatenate([r[h] for h in range(r.shape[0])], axis=1) if spec[0] == "heads" else r[...]
                for spec, r in zip(ins, refs[:n_in])]
        res = fn(*vals)
        for out, r, v in zip(outs, refs[n_in:n_in + n_out], res[:n_out]):
            if len(out) >= 3:
                for h in range(out[2]):
                    piece = v[:, h * LANES:(h + 1) * LANES]
                    r[h] = (piece.T if len(out) == 4 else piece).astype(r.dtype)
            else:
                r[...] = v.astype(r.dtype)
        if reds:
            @pl.when(pl.program_id(0) == 0)
            def _():
                for r in refs[n_in + n_out:]:
                    r[...] = jnp.zeros_like(r)

            for r, v in zip(refs[n_in + n_out:], res[n_out:]):
                r[...] += v

    in_specs, args = [], []
    for spec in ins:
        if spec[0] == "row":
            _, arr, width, blk = spec
            in_specs.append(pl.BlockSpec((tm, width), functools.partial(lambda i, blk: (i, blk), blk=blk)))
        elif spec[0] == "heads":
            arr = spec[1]
            in_specs.append(pl.BlockSpec((arr.shape[0], tm, LANES), lambda i: (0, i, 0)))
        else:
            arr = spec[1]
            in_specs.append(pl.BlockSpec(arr.shape, functools.partial(lambda i, nd: (0,) * nd, nd=arr.ndim)))
        args.append(arr)
    def out_spec(o):
        if len(o) == 4:
            return pl.BlockSpec((o[2], LANES, tm), lambda i: (0, 0, i)), (o[2], LANES, rows)
        if len(o) == 3:
            return pl.BlockSpec((o[2], tm, LANES), lambda i: (0, i, 0)), (o[2], rows, LANES)
        return pl.BlockSpec((tm, o[0]), lambda i: (i, 0)), (rows, o[0])

    out_specs = [out_spec(o)[0] for o in outs]
    out_shape = [jax.ShapeDtypeStruct(out_spec(o)[1], o[1]) for o in outs]
    out_specs += [pl.BlockSpec((r, w), lambda i: (0, 0)) for r, w in reds]
    out_shape += [jax.ShapeDtypeStruct((r, w), F32) for r, w in reds]
    return pl.pallas_call(body, name=name, grid=(rows // tm,), in_specs=in_specs, out_specs=out_specs,
                          out_shape=out_shape, compiler_params=_params(("arbitrary",)))(*args)


def _rstd(x, n):
    return lax.rsqrt(jnp.sum(x * x, axis=-1, keepdims=True) * (1.0 / n) + RMS_EPS)


def _rms_fwd(x_spec, g, *, name, rows, width):
    def fn(x, gv):
        return (x * _rstd(x, width) * gv,)
    return _rowwise(fn, name=name, rows=rows, ins=[x_spec, ("full", g)], outs=[(width, BF16)])[0]


def _rms_bwd(x_spec, g, dy, res, *, name, rows, width, out_dtypes):
    def fn(x, gv, dyv, *rest):
        r = _rstd(x, width)
        dyg = dyv * gv
        dx = r * dyg - x * (r * r * r) * (jnp.sum(dyg * x, axis=-1, keepdims=True) * (1.0 / width))
        if rest:
            dx = dx + rest[0]
        return (dx,) * len(out_dtypes) + (jnp.sum(dyv * x * r, axis=0, keepdims=True),)

    ins = [x_spec, ("full", g), ("row", dy, width, 0)]
    if res is not None:
        ins.append(("row", res, width, 0))
    return _rowwise(fn, name=name, rows=rows, ins=ins, outs=[(width, dt) for dt in out_dtypes], reds=[(1, width)])


NT_DIMS = (((1,), (1,)), ((), ()))
TN_DIMS = (((0,), (0,)), ((), ()))
BNT_DIMS = (((2,), (2,)), ((0,), (0,)))
BNN_DIMS = (((2,), (1,)), ((0,), (0,)))


def _mla_tile(s):
    return min(512, s)


def _block_tables(pairs):
    return jnp.asarray([a for a, _ in pairs], jnp.int32), jnp.asarray([b for _, b in pairs], jnp.int32)


def _causal(shape, query_axis):
    return lax.broadcasted_iota(jnp.int32, shape, query_axis) >= lax.broadcasted_iota(jnp.int32, shape, 1 - query_axis)


def _mla_fwd(qa, ka, vta, *, name, s, nh):
    t = _mla_tile(s)
    nq = s // t

    pairs = [(i, j) for i in range(nq) for j in range(i + 1)]

    def body(qb_ref, kb_ref, q_ref, k_ref, vt_ref, o_ref, lse_ref, m_s, acc_s):
        i, j = qb_ref[pl.program_id(0)], kb_ref[pl.program_id(0)]

        @pl.when(j == 0)
        def _():
            m_s[...] = jnp.full_like(m_s, NEG)
            acc_s[...] = jnp.zeros_like(acc_s)

        def step(diag):
            sc = lax.dot_general(k_ref[...], q_ref[...], BNT_DIMS, preferred_element_type=F32)
            if diag:
                sc = jnp.where(_causal(sc.shape[1:], 1)[None], sc, NEG)
            m_prev = m_s[...]
            m_new = jnp.maximum(m_prev, jnp.max(sc, axis=1, keepdims=True))
            p = jnp.exp(sc - m_new).astype(BF16)
            acc_s[...] = (jnp.exp(m_prev - m_new) * acc_s[...]
                          + lax.dot_general(vt_ref[...], p, BNN_DIMS, preferred_element_type=F32))
            m_s[...] = m_new

        pl.when(j < i)(functools.partial(step, False))
        pl.when(j == i)(functools.partial(step, True))

        @pl.when(j == i)
        def _():
            row = lax.broadcasted_iota(jnp.int32, (LANES, t), 0)
            for h in range(nh):
                acc = acc_s[h]
                l = acc[HEAD:HEAD + 1, :]
                o_ref[h] = jnp.where(row < HEAD, acc / l, 0.0).T
                lse_ref[h] = m_s[h] + jnp.log(l)

    q_spec = pl.BlockSpec((nh, t, LANES), lambda p, qb, kb: (0, qb[p], 0))
    k_spec = pl.BlockSpec((nh, t, LANES), lambda p, qb, kb: (0, kb[p], 0))
    vt_spec = pl.BlockSpec((nh, LANES, t), lambda p, qb, kb: (0, 0, kb[p]))
    return pl.pallas_call(
        body, name=name,
        grid_spec=pltpu.PrefetchScalarGridSpec(
            num_scalar_prefetch=2, grid=(len(pairs),), in_specs=[q_spec, k_spec, vt_spec],
            out_specs=[q_spec, pl.BlockSpec((nh, 1, t), lambda p, qb, kb: (0, 0, qb[p]))],
            scratch_shapes=[pltpu.VMEM((nh, 1, t), F32), pltpu.VMEM((nh, LANES, t), F32)]),
        out_shape=[jax.ShapeDtypeStruct((nh, s, LANES), F32), jax.ShapeDtypeStruct((nh, 1, s), F32)],
        compiler_params=_params(("arbitrary",)),
    )(*_block_tables(pairs), qa, ka, vta)


def _mla_bwd(qa, ka, kta, va, o, do, lse_row, after, *, name, s, nh):
    t = _mla_tile(s)
    nq = s // t

    pairs = [(kb, j) for kb in range(nq) for j in range(kb, nq)]

    def body(kb_ref, qb_ref, q_ref, k_ref, kt_ref, v_ref, o_ref, do_ref, lse_ref, after_ref, dq_hbm, dk_ref, dv_ref,
             dqt_s, dk_s, dv_s, d_s, stage, sem):
        kb, j = kb_ref[pl.program_id(0)], qb_ref[pl.program_id(0)]
        cols = pl.ds(pl.multiple_of(j * t, t), t)

        @pl.when(j == kb)
        def _():
            dk_s[...] = jnp.zeros_like(dk_s)
            dv_s[...] = jnp.zeros_like(dv_s)

        @pl.when(kb == 0)
        def _():
            dqt_s[:, :, cols] = jnp.zeros((nh, LANES, t), F32)
            for h in range(nh):
                d_col = jnp.sum(do_ref[h] * o_ref[h], axis=1, keepdims=True)
                d_s[h, :, cols] = jnp.broadcast_to(d_col, (t, LANES)).T[0:1, :]

        def step(diag):
            q = q_ref[...]
            do_b = do_ref[...].astype(BF16)
            sc = lax.dot_general(k_ref[...], q, BNT_DIMS, preferred_element_type=F32)
            if diag:
                sc = jnp.where(_causal(sc.shape[1:], 1)[None], sc, NEG)
            p = jnp.exp(sc - lse_ref[...])
            dv_s[...] += lax.dot_general(p.astype(BF16), do_b, BNN_DIMS, preferred_element_type=F32)
            dp = lax.dot_general(v_ref[...], do_b, BNT_DIMS, preferred_element_type=F32)
            ds = (p * (dp - d_s[:, :, cols])).astype(BF16)
            dk_s[...] += lax.dot_general(ds, q, BNN_DIMS, preferred_element_type=F32)
            dqt_s[:, :, cols] += lax.dot_general(kt_ref[...], ds, BNN_DIMS, preferred_element_type=F32)

        pl.when(j > kb)(functools.partial(step, False))
        pl.when(j == kb)(functools.partial(step, True))

        @pl.when(j == kb)
        def _():
            slot = kb % 2

            def write_out(sl):
                return pltpu.make_async_copy(stage.at[sl], dq_hbm.at[:, cols, :], sem.at[sl])

            @pl.when(kb >= 2)
            def _():
                write_out(slot).wait()

            for h in range(nh):
                stage[slot, h] = dqt_s[h, :, cols].T
            write_out(slot).start()

            @pl.when(kb == nq - 1)
            def _():
                write_out(slot).wait()
                if nq >= 2:
                    write_out(1 - slot).wait()

        @pl.when(j == nq - 1)
        def _():
            dk_ref[...] = dk_s[...]
            dv_ref[...] = dv_s[...].astype(dv_ref.dtype)

    q_spec = pl.BlockSpec((nh, t, LANES), lambda p, kb, qb: (0, qb[p], 0))
    kv_spec = pl.BlockSpec((nh, t, LANES), lambda p, kb, qb: (0, kb[p], 0))
    kt_spec = pl.BlockSpec((nh, LANES, t), lambda p, kb, qb: (0, 0, kb[p]))
    row_spec = pl.BlockSpec((nh, 1, t), lambda p, kb, qb: (0, 0, qb[p]))
    whole = jax.ShapeDtypeStruct((nh, s, LANES), F32)
    return pl.pallas_call(
        body, name=name,
        grid_spec=pltpu.PrefetchScalarGridSpec(
            num_scalar_prefetch=2, grid=(len(pairs),),
            in_specs=[q_spec, kv_spec, kt_spec, kv_spec, q_spec, q_spec, row_spec, pl.BlockSpec(memory_space=pl.ANY)],
            out_specs=[pl.BlockSpec(memory_space=pl.ANY), kv_spec, kv_spec],
            scratch_shapes=[pltpu.VMEM((nh, LANES, s), F32), pltpu.VMEM((nh, t, LANES), F32),
                            pltpu.VMEM((nh, t, LANES), F32), pltpu.VMEM((nh, 1, s), F32),
                            pltpu.VMEM((2, nh, t, LANES), F32), pltpu.SemaphoreType.DMA((2,))]),
        out_shape=[whole, whole, jax.ShapeDtypeStruct((nh, s, LANES), BF16)],
        compiler_params=_params(("arbitrary",)),
    )(*_block_tables(pairs), qa, ka, kta, va, o, do, lse_row, after)


SWA_PIECE = 128
SWA_KEYS = 2 * SWA_PIECE


def _swa_block(s):
    return min(1024, s)


def _swa_piece(hp_ref, h, q, k_ref, v_ref, qpos0, scale):
    kstart = pl.multiple_of(jnp.maximum(qpos0 - SWA_PIECE, 0), SWA_PIECE)
    k = k_ref[pl.ds(kstart, SWA_KEYS), :].astype(BF16)
    v = v_ref[pl.ds(kstart, SWA_KEYS), :].astype(BF16)
    sc = lax.dot_general(q, k, NT_DIMS, preferred_element_type=F32)
    dist = (qpos0 + lax.broadcasted_iota(jnp.int32, sc.shape, 0)) - (kstart + lax.broadcasted_iota(jnp.int32, sc.shape, 1))
    sc = sc * scale - hp_ref[h, 0] * dist.astype(F32)
    sc = jnp.where((dist >= 0) & (dist < SWA_WINDOW), sc, NEG)
    return kstart, k, v, sc


def _swa_fwd(proj, hp, *, name, s, scale):
    tb = _swa_block(s)
    group = SWA_HEADS // SWA_KV_HEADS
    q_off, k_off, v_off = C_QSW[0] // LANES, C_KSW[0] // LANES, C_VSW[0] // LANES

    def body(hp_ref, q_ref, k_ref, v_ref, o_ref, lse_ref):
        h, i = pl.program_id(0), pl.program_id(1)
        sink = hp_ref[h, 1]
        for r in range(0, tb, SWA_PIECE):
            rows = pl.ds(r, SWA_PIECE)
            _, _, v, sc = _swa_piece(hp_ref, h, q_ref[rows, :].astype(BF16), k_ref, v_ref, i * tb + r, scale)
            m = jnp.maximum(jnp.max(sc, axis=1, keepdims=True), sink)
            p = jnp.exp(sc - m)
            l = jnp.sum(p, axis=1, keepdims=True) + jnp.exp(sink - m)
            o_ref[rows, :] = jnp.dot(p.astype(BF16), v, preferred_element_type=F32) / l
            lse_ref[rows, :] = m + jnp.log(l)

    whole = lambda off: pl.BlockSpec((s, LANES), lambda h, i: (0, off + h // group))
    return pl.pallas_call(
        body, name=name, grid=(SWA_HEADS, s // tb),
        in_specs=[pl.BlockSpec(memory_space=pltpu.SMEM), pl.BlockSpec((tb, LANES), lambda h, i: (i, q_off + h)),
                  whole(k_off), whole(v_off)],
        out_specs=[pl.BlockSpec((tb, LANES), lambda h, i: (i, h)), pl.BlockSpec((None, tb, 1), lambda h, i: (h, i, 0))],
        out_shape=[jax.ShapeDtypeStruct((s, SWA_HEADS * LANES), F32), jax.ShapeDtypeStruct((SWA_HEADS, s, 1), F32)],
        compiler_params=_params(("parallel", "parallel")),
    )(hp, proj, proj, proj)


def _swa_bwd(proj, o, do, lse, hp, *, name, s, scale):
    tb = _swa_block(s)
    nqb = s // tb
    group = SWA_HEADS // SWA_KV_HEADS
    q_off, k_off, v_off = C_QSW[0] // LANES, C_KSW[0] // LANES, C_VSW[0] // LANES

    def body(hp_ref, q_ref, k_ref, v_ref, o_ref, do_ref, lse_ref, dq_ref, dk_ref, dv_ref, dsink_ref):
        kh, g, i = pl.program_id(0), pl.program_id(1), pl.program_id(2)
        h = kh * group + g
        sink = hp_ref[h, 1]

        @pl.when((g == 0) & (i == 0))
        def _():
            dk_ref[...] = jnp.zeros_like(dk_ref)
            dv_ref[...] = jnp.zeros_like(dv_ref)

        @pl.when(i == 0)
        def _():
            dsink_ref[...] = jnp.zeros_like(dsink_ref)

        for r in range(0, tb, SWA_PIECE):
            rows = pl.ds(r, SWA_PIECE)
            q = q_ref[rows, :].astype(BF16)
            dov = do_ref[rows, :]
            do_b = dov.astype(BF16)
            lse_r = lse_ref[rows, :]
            d_r = jnp.sum(dov * o_ref[rows, :], axis=1, keepdims=True)
            kstart, k, v, sc = _swa_piece(hp_ref, h, q, k_ref, v_ref, i * tb + r, scale)
            p = jnp.exp(sc - lse_r)
            dp = lax.dot_general(do_b, v, NT_DIMS, preferred_element_type=F32)
            ds = (p * (dp - d_r)).astype(BF16)
            dq_ref[rows, :] = (jnp.dot(ds, k, preferred_element_type=F32) * scale).astype(dq_ref.dtype)
            win = pl.ds(kstart, SWA_KEYS)
            dk_ref[win, :] += lax.dot_general(ds, q, TN_DIMS, preferred_element_type=F32) * scale
            dv_ref[win, :] += lax.dot_general(p.astype(BF16), do_b, TN_DIMS, preferred_element_type=F32)
            part = jnp.sum(-jnp.exp(sink - lse_r) * d_r, axis=0, keepdims=True)
            dsink_ref[...] += jnp.broadcast_to(part, (1, LANES))

    whole = lambda off: pl.BlockSpec((s, LANES), lambda kh, g, i: (0, off + kh))
    q_map = lambda kh, g, i: (i, kh * group + g)
    return pl.pallas_call(
        body, name=name, grid=(SWA_KV_HEADS, group, nqb),
        in_specs=[pl.BlockSpec(memory_space=pltpu.SMEM),
                  pl.BlockSpec((tb, LANES), lambda kh, g, i: (i, q_off + kh * group + g)), whole(k_off), whole(v_off),
                  pl.BlockSpec((tb, LANES), q_map), pl.BlockSpec((tb, LANES), q_map),
                  pl.BlockSpec((None, tb, 1), lambda kh, g, i: (kh * group + g, i, 0))],
        out_specs=[pl.BlockSpec((tb, LANES), q_map), whole(0), whole(0),
                   pl.BlockSpec((None, 1, LANES), lambda kh, g, i: (kh * group + g, 0, 0))],
        out_shape=[jax.ShapeDtypeStruct((s, SWA_HEADS * LANES), BF16),
                   jax.ShapeDtypeStruct((s, SWA_KV_HEADS * LANES), F32), jax.ShapeDtypeStruct((s, SWA_KV_HEADS * LANES), F32),
                   jax.ShapeDtypeStruct((SWA_HEADS, 1, LANES), F32)],
        compiler_params=_params(("parallel", "arbitrary", "arbitrary")),
    )(hp, proj, proj, proj, o, do, lse)


def _shift_down(z, k):
    rows = lax.broadcasted_iota(jnp.int32, z.shape, 0)
    return jnp.where(rows >= k, pltpu.roll(z, k, 0), 0.0)


def _shift_up(z, k):
    n = z.shape[0]
    rows = lax.broadcasted_iota(jnp.int32, z.shape, 0)
    return jnp.where(rows < n - k, pltpu.roll(z, n - k, 0), 0.0)


def _rows3(a, b, c):
    r = lax.broadcasted_iota(jnp.int32, (3, a.shape[1]), 0)
    return jnp.where(r == 0, a, jnp.where(r == 1, b, c))


def _col_spec(s, off):
    return pl.BlockSpec((s, LANES), functools.partial(lambda j, off: (0, off + j), off=off))


def _conv_fwd(proj, conv_w, *, name, s):
    def body(gb_ref, gc_ref, u_ref, w_ref, y_ref):
        w0, w1, w2 = w_ref[0:1, :], w_ref[1:2, :], w_ref[2:3, :]
        z = gc_ref[...] * u_ref[...]
        c = w2 * z + w1 * _shift_down(z, 1) + w0 * _shift_down(z, 2)
        y_ref[...] = gb_ref[...] * c

    return pl.pallas_call(
        body, name=name, grid=(2,),
        in_specs=[_col_spec(s, C_GB[0] // LANES), _col_spec(s, C_GC[0] // LANES), _col_spec(s, C_UCONV[0] // LANES),
                  pl.BlockSpec((3, LANES), lambda j: (0, j))],
        out_specs=_col_spec(s, 0), out_shape=jax.ShapeDtypeStruct((s, D_GROUP), F32),
        compiler_params=_params(("parallel",)),
    )(proj, proj, proj, conv_w)


def _conv_bwd(dy, proj, conv_w, *, name, s):
    def body(dy_ref, gb_ref, gc_ref, u_ref, w_ref, dgb_ref, dgc_ref, du_ref, dw_ref):
        w0, w1, w2 = w_ref[0:1, :], w_ref[1:2, :], w_ref[2:3, :]
        gc, u, dyv = gc_ref[...], u_ref[...], dy_ref[...]
        z = gc * u
        z1, z2 = _shift_down(z, 1), _shift_down(z, 2)
        c = w2 * z + w1 * z1 + w0 * z2
        dgb_ref[...] = (dyv * c).astype(dgb_ref.dtype)
        dc = dyv * gb_ref[...]
        dz = w2 * dc + w1 * _shift_up(dc, 1) + w0 * _shift_up(dc, 2)
        dgc_ref[...] = (dz * u).astype(dgc_ref.dtype)
        du_ref[...] = (dz * gc).astype(du_ref.dtype)
        dw_ref[...] = _rows3(jnp.sum(dc * z2, axis=0, keepdims=True), jnp.sum(dc * z1, axis=0, keepdims=True),
                             jnp.sum(dc * z, axis=0, keepdims=True))

    act = jax.ShapeDtypeStruct((s, D_GROUP), BF16)
    return pl.pallas_call(
        body, name=name, grid=(2,),
        in_specs=[_col_spec(s, 0), _col_spec(s, C_GB[0] // LANES), _col_spec(s, C_GC[0] // LANES),
                  _col_spec(s, C_UCONV[0] // LANES), pl.BlockSpec((3, LANES), lambda j: (0, j))],
        out_specs=[_col_spec(s, 0), _col_spec(s, 0), _col_spec(s, 0), pl.BlockSpec((3, LANES), lambda j: (0, j))],
        out_shape=[act, act, act, jax.ShapeDtypeStruct((3, D_GROUP), F32)],
        compiler_params=_params(("parallel",)),
    )(dy, proj, proj, proj, conv_w)


def _pool_select(j, lane, a2, a4, a8, a16):
    lo = lane < HEAD
    return jnp.where(j == 0, jnp.where(lo, a2, a4), jnp.where(lo, a8, a16))


def _pooled(u, j):
    s2 = u + _shift_down(u, 1)
    s4 = s2 + _shift_down(s2, 2)
    s8 = s4 + _shift_down(s4, 4)
    s16 = s8 + _shift_down(s8, 8)
    lane = lax.broadcasted_iota(jnp.int32, u.shape, 1)
    rows = lax.broadcasted_iota(jnp.int32, u.shape, 0)
    win = _pool_select(j, lane, *POOL_WINDOWS)
    count = jnp.minimum(rows + 1, win).astype(F32)
    return _pool_select(j, lane, s2, s4, s8, s16) / count - u, count


def _pool_fwd(proj, wbd, scale, *, name, s):
    def body(u_ref, w_ref, sc_ref, y_ref):
        pooled, _ = _pooled(u_ref[...], pl.program_id(0))
        y_ref[...] = jnp.dot(pooled.astype(BF16), w_ref[...].astype(BF16), preferred_element_type=F32) * sc_ref[...]

    return pl.pallas_call(
        body, name=name, grid=(2,),
        in_specs=[_col_spec(s, C_UPOOL[0] // LANES), pl.BlockSpec((None, LANES, LANES), lambda j: (j, 0, 0)),
                  pl.BlockSpec((1, LANES), lambda j: (0, j))],
        out_specs=_col_spec(s, 0), out_shape=jax.ShapeDtypeStruct((s, D_GROUP), F32),
        compiler_params=_params(("parallel",)),
    )(proj, wbd, scale)


def _pool_bwd(dy, proj, wbd, scale, *, name, s):
    def body(dy_ref, u_ref, w_ref, sc_ref, du_ref, dw_ref, dsc_ref):
        j = pl.program_id(0)
        pooled, count = _pooled(u_ref[...], j)
        pooled_b = pooled.astype(BF16)
        w_b = w_ref[...].astype(BF16)
        dyv = dy_ref[...]
        mixed = jnp.dot(pooled_b, w_b, preferred_element_type=F32)
        dsc_ref[...] = jnp.sum(dyv * mixed, axis=0, keepdims=True)
        dms = (dyv * sc_ref[...]).astype(BF16)
        dw_ref[...] = lax.dot_general(pooled_b, dms, (((0,), (0,)), ((), ())), preferred_element_type=F32)
        dpooled = lax.dot_general(dms, w_b, (((1,), (1,)), ((), ())), preferred_element_type=F32)
        r = dpooled / count
        a2 = r + _shift_up(r, 1)
        a4 = a2 + _shift_up(a2, 2)
        a8 = a4 + _shift_up(a4, 4)
        a16 = a8 + _shift_up(a8, 8)
        lane = lax.broadcasted_iota(jnp.int32, r.shape, 1)
        du_ref[...] = (_pool_select(j, lane, a2, a4, a8, a16) - dpooled).astype(du_ref.dtype)

    return pl.pallas_call(
        body, name=name, grid=(2,),
        in_specs=[_col_spec(s, 0), _col_spec(s, C_UPOOL[0] // LANES),
                  pl.BlockSpec((None, LANES, LANES), lambda j: (j, 0, 0)), pl.BlockSpec((1, LANES), lambda j: (0, j))],
        out_specs=[_col_spec(s, 0), pl.BlockSpec((None, LANES, LANES), lambda j: (j, 0, 0)),
                   pl.BlockSpec((1, LANES), lambda j: (0, j))],
        out_shape=[jax.ShapeDtypeStruct((s, D_GROUP), BF16), jax.ShapeDtypeStruct((2, LANES, LANES), F32),
                   jax.ShapeDtypeStruct((1, D_GROUP), F32)],
        compiler_params=_params(("parallel",)),
    )(dy, proj, wbd, scale)


def _mesh_pos():
    return lax.axis_index("x"), lax.axis_index("y"), lax.axis_index("c")


def _any_specs(n):
    return [pl.BlockSpec(memory_space=pl.ANY)] * n


def _all_gather(xs, *, name):
    n = len(xs)

    def body(*refs):
        x_refs, out_refs = refs[:n], refs[n:2 * n]
        send_sems, recv_sems, local_sems = refs[2 * n:]
        x, y, cc = _mesh_pos()
        me, sibling = (x, y, cc), (x, y, 1 - cc)
        chips = [(1 - x, y), (x, 1 - y), (1 - x, 1 - y)]

        def slot(a, px, py, pc):
            return out_refs[a].at[4 * px + 2 * py + pc]

        def copy(a, k, block, to, src=None):
            return pltpu.make_async_remote_copy(
                src_ref=slot(a, *block) if src is None else src, dst_ref=slot(a, *block), send_sem=send_sems.at[a, k],
                recv_sem=recv_sems.at[a, k], device_id=to, device_id_type=pl.DeviceIdType.MESH)

        mine = [pltpu.make_async_copy(x_refs[a], slot(a, *me), local_sems.at[a]) for a in range(n)]
        first = []
        for a in range(n):
            first.append(copy(a, 0, me, sibling, src=x_refs[a]))
            first += [copy(a, 1 + j, me, (*chip, cc), src=x_refs[a]) for j, chip in enumerate(chips)]
        for cp in mine + first:
            cp.start()
        passed = []
        for j, chip in enumerate(chips):
            for a in range(n):
                copy(a, 1 + j, (*chip, cc), me).wait_recv()
                passed.append(copy(a, 4 + j, (*chip, cc), sibling))
                passed[-1].start()
        for a in range(n):
            copy(a, 0, sibling, me).wait_recv()
        for j, chip in enumerate(chips):
            for a in range(n):
                copy(a, 4 + j, (*chip, 1 - cc), me).wait_recv()
        for cp in first + passed:
            cp.wait_send()
        for cp in mine:
            cp.wait()

    return pl.pallas_call(
        body, name=name, out_shape=[jax.ShapeDtypeStruct((N_DEV,) + a.shape, a.dtype) for a in xs],
        in_specs=_any_specs(n), out_specs=_any_specs(n),
        scratch_shapes=[pltpu.SemaphoreType.DMA((n, 7)), pltpu.SemaphoreType.DMA((n, 7)), pltpu.SemaphoreType.DMA((n,))],
    )(*xs)


def _plan_gather_near(src_refs, land_refs, send_sems, recv_sems):
    x, y, cc = _mesh_pos()
    me = 4 * x + 2 * y + cc
    plan = []
    for a, (src, land) in enumerate(zip(src_refs, land_refs)):
        for k, (px, py, pc) in enumerate([(x, y, 1 - cc), (1 - x, y, cc), (x, 1 - y, cc), (1 - x, 1 - y, cc)]):
            sems = dict(send_sem=send_sems.at[4 * a + k], recv_sem=recv_sems.at[4 * a + k], device_id=(px, py, pc),
                        device_id_type=pl.DeviceIdType.MESH)
            plan.append((pltpu.make_async_remote_copy(src_ref=src, dst_ref=land.at[me], **sems),
                         pltpu.make_async_remote_copy(src_ref=src, dst_ref=land.at[4 * px + 2 * py + pc], **sems)))
    return plan


def _plan_gather_pass(src_refs, land_refs, send_sems, recv_sems):
    x, y, cc = _mesh_pos()
    plan = []
    for a, land in enumerate(land_refs):
        for j, (px, py) in enumerate([(1 - x, y), (x, 1 - y), (1 - x, 1 - y)]):
            mine, theirs = land.at[4 * px + 2 * py + cc], land.at[4 * px + 2 * py + 1 - cc]
            sems = dict(send_sem=send_sems.at[3 * a + j], recv_sem=recv_sems.at[3 * a + j], device_id=(x, y, 1 - cc),
                        device_id_type=pl.DeviceIdType.MESH)
            plan.append((pltpu.make_async_remote_copy(src_ref=mine, dst_ref=mine, **sems),
                         pltpu.make_async_remote_copy(src_ref=mine, dst_ref=theirs, **sems)))
    return plan


def _plan_sibling(src_refs, land_refs, send_sems, recv_sems):
    x, y, cc = _mesh_pos()
    plan = []
    for a, (src, land) in enumerate(zip(src_refs, land_refs)):
        cp = pltpu.make_async_remote_copy(
            src_ref=src.at[1 - cc], dst_ref=land, send_sem=send_sems.at[a], recv_sem=recv_sems.at[a],
            device_id=(x, y, 1 - cc), device_id_type=pl.DeviceIdType.MESH)
        plan.append((cp, cp))
    return plan


def _plan_chips(src_refs, land_refs, send_sems, recv_sems):
    x, y, cc = _mesh_pos()
    my_chip = 2 * x + y
    plan = []
    for a, (src, land) in enumerate(zip(src_refs, land_refs)):
        for j, (px, py) in enumerate([(1 - x, y), (x, 1 - y), (1 - x, 1 - y)]):
            peer = 2 * px + py
            sems = dict(send_sem=send_sems.at[3 * a + j], recv_sem=recv_sems.at[3 * a + j], device_id=(px, py, cc),
                        device_id_type=pl.DeviceIdType.MESH)
            plan.append((pltpu.make_async_remote_copy(src_ref=src.at[peer], dst_ref=land.at[my_chip], **sems),
                         pltpu.make_async_remote_copy(src_ref=src.at[peer], dst_ref=land.at[peer], **sems)))
    return plan


HBM_SPEC = pl.BlockSpec(memory_space=pltpu.HBM)
SEM_SPEC = pl.BlockSpec(memory_space=pltpu.SEMAPHORE)
ANY_SPEC = pl.BlockSpec(memory_space=pl.ANY)
SIDE_EFFECT = pltpu.CompilerParams(has_side_effects=pltpu.SideEffectType.DATAFLOW_SIDE_EFFECTING)


def _start_copies(plan, sems_per_array, srcs, lands, after, *, name):
    lands = [lax.empty(l, a.dtype) if isinstance(l, tuple) else l for l, a in zip(lands, srcs or lands)]
    ns, n = len(srcs), len(srcs) + len(lands)

    def body(*refs):
        send_sems, recv_sems = refs[n + len(after)], refs[n + len(after) + 1]
        for out, _ in plan(refs[:ns], refs[ns:n], send_sems, recv_sems):
            out.start()
        refs[-1][...] = jnp.zeros_like(refs[-1])

    sem = pltpu.SemaphoreType.DMA((len(lands) * sems_per_array,))
    res = pl.pallas_call(
        body, name=name,
        out_shape=(sem, sem, *[pltpu.HBM(a.shape, a.dtype) for a in srcs + lands], jax.ShapeDtypeStruct((8, LANES), F32)),
        in_specs=[HBM_SPEC] * n + [ANY_SPEC] * len(after),
        out_specs=(SEM_SPEC, SEM_SPEC, *[HBM_SPEC] * n, pl.BlockSpec(memory_space=pltpu.VMEM)),
        input_output_aliases={i: 2 + i for i in range(n)}, compiler_params=SIDE_EFFECT,
    )(*[pltpu.with_memory_space_constraint(a, pltpu.HBM) for a in srcs + lands], *after)
    return (res[0], res[1], list(res[2:2 + ns]), list(res[2 + ns:2 + n])), res[-1]


def _wait_copies(plan, handle, after, *, name):
    send, recv, srcs, lands = handle
    ns, n = len(srcs), len(srcs) + len(lands)

    def body(*refs):
        for out, inc in plan(refs[:ns], refs[ns:n], refs[n], refs[n + 1]):
            out.wait_send()
            inc.wait_recv()

    res = pl.pallas_call(
        body, name=name, out_shape=tuple(pltpu.HBM(a.shape, a.dtype) for a in srcs + lands),
        in_specs=[HBM_SPEC] * n + [SEM_SPEC, SEM_SPEC] + [ANY_SPEC] * len(after), out_specs=[HBM_SPEC] * n,
        input_output_aliases={i: i for i in range(n)}, compiler_params=SIDE_EFFECT,
    )(*srcs, *lands, send, recv, *after)
    return list(res[:ns]), list(res[ns:])


def _row_tile(rows, target=512):
    if rows <= target:
        return rows
    best = None
    for t in range(8, target + 1, 8):
        if rows % t == 0:
            best = t
    assert best is not None, (rows, target)
    return best


def _add_own(g, other, core, *, name):
    _, _, rows, cols = g.shape
    tm = _row_tile(rows)

    def body(c_ref, g_ref, o_ref, out_ref):
        out_ref[...] = g_ref[...] + o_ref[...]

    return pl.pallas_call(
        body, name=name, out_shape=jax.ShapeDtypeStruct(other.shape, other.dtype),
        grid_spec=pltpu.PrefetchScalarGridSpec(
            num_scalar_prefetch=1, grid=(4, rows // tm),
            in_specs=[pl.BlockSpec((None, None, tm, cols), lambda p, i, c_ref: (c_ref[0], p, i, 0)),
                      pl.BlockSpec((None, tm, cols), lambda p, i, c_ref: (p, i, 0))],
            out_specs=pl.BlockSpec((None, tm, cols), lambda p, i, c_ref: (p, i, 0))),
        compiler_params=_params(("parallel", "parallel")),
    )(core, g, other)


def _adamw(parts, w, m, v, after, *, name):
    layers, rows, cols = w.shape
    assert len(parts) == layers
    tm = _row_tile(rows, 256)
    nr = rows // tm

    def body(*refs):
        p_refs = refs[:layers]
        w_ref, m_ref, v_ref, _, g_ref, d_ref, nm_ref, nv_ref, g_s = refs[layers:]
        for ll in range(layers):
            @pl.when(pl.program_id(0) == ll)
            def _(ll=ll):
                g = p_refs[ll][0]
                for q in range(1, p_refs[ll].shape[0]):
                    g = g + p_refs[ll][q]
                g_s[...] = g

        g = g_s[...]
        mm = ADAM_B1 * m_ref[...] + (1.0 - ADAM_B1) * g
        vv = ADAM_B2 * v_ref[...] + (1.0 - ADAM_B2) * jnp.square(g)
        m_hat = mm / (1.0 - ADAM_B1 ** ADAM_STEP)
        v_hat = vv / (1.0 - ADAM_B2 ** ADAM_STEP)
        g_ref[...] = g
        d_ref[...] = -ADAM_LR * (m_hat / (jnp.sqrt(v_hat) + ADAM_EPS) + ADAM_WD * w_ref[...])
        nm_ref[...] = mm
        nv_ref[...] = vv

    def part_spec(ll, p):
        return pl.BlockSpec((p, tm, cols), lambda l, i: (0, jnp.where(l == ll, i, jnp.where(l < ll, 0, nr - 1)), 0))

    spec = pl.BlockSpec((None, tm, cols), lambda l, i: (l, i, 0))
    out = jax.ShapeDtypeStruct(w.shape, F32)
    return pl.pallas_call(
        body, name=name, grid=(layers, nr),
        in_specs=[part_spec(ll, parts[ll].shape[0]) for ll in range(layers)] + [spec] * 3 + [pl.BlockSpec(memory_space=pl.ANY)],
        out_specs=[spec] * 4, out_shape=[out] * 4, scratch_shapes=[pltpu.VMEM((tm, cols), F32)],
        compiler_params=_params(("arbitrary", "arbitrary")),
    )(*parts, w, m, v, after)


def _pack(arrs):
    flat = jnp.concatenate([a.reshape(-1) for a in arrs])
    rows = -(-flat.shape[0] // (PACK_COLS * 16)) * 16
    return jnp.pad(flat, (0, rows * PACK_COLS - flat.shape[0])).reshape(rows, PACK_COLS)


def _unpack(packed, shapes):
    flat = packed.reshape(-1)
    out, off = [], 0
    for shp in shapes:
        n = int(np.prod(shp))
        out.append(flat[off:off + n].reshape(shp))
        off += n
    return out


def _shards_to_full(g, axis):
    if axis == 0:
        return g.reshape(g.shape[0] * g.shape[1], g.shape[2])
    return jnp.transpose(g, (1, 0, 2)).reshape(g.shape[1], g.shape[0] * g.shape[2])


def _full_to_shards(a, axis):
    if axis == 0:
        return jnp.transpose(a.reshape(4, 2, a.shape[0] // N_DEV, a.shape[1]), (1, 0, 2, 3))
    return jnp.transpose(a.reshape(a.shape[0], 4, 2, a.shape[1] // N_DEV), (2, 1, 0, 3))


def _zeros_like_cols(a, n):
    return jnp.zeros(a.shape[:-1] + (n,), a.dtype)


def _pad_heads(a, n):
    z = _zeros_like_cols(a, HEAD)
    return jnp.concatenate([p for h in range(n) for p in (a[..., h * HEAD:(h + 1) * HEAD], z)], axis=-1)


def _unpad_heads(a, n):
    return jnp.concatenate([a[..., h * LANES:h * LANES + HEAD] for h in range(n)], axis=-1)


def _seg(first, width, sign=1):
    return (width, [(first, sign)])


def _zero(width):
    return (width, [])


def _swapped(first):
    half = MLA_ROPE // 2
    return [_seg(first + half, half, -1), _seg(first, half)]


def _padded_heads(first, n):
    return [s for h in range(n) for s in (_seg(first + HEAD * h, HEAD), _zero(HEAD))]


def _layout_w_in():
    kr = 384
    return (_padded_heads(1440, 4) + [_seg(0, 256), _seg(416, 256), _seg(672, 256), _seg(928, 256), _seg(1184, 256)]
            + _padded_heads(1696, 2) + _padded_heads(1824, 2) + [_seg(256, 128)]
            + [_zero(HEAD), _seg(kr, MLA_ROPE), _seg(kr, MLA_ROPE)] + [_zero(HEAD)] + _swapped(kr) + _swapped(kr))


def _layout_w_uq():
    out = []
    for h in range(MLA_HEADS):
        out += [_seg(96 * h, MLA_NOPE), _seg(96 * h + MLA_NOPE, MLA_ROPE)] + _swapped(96 * h + MLA_NOPE)
    return out


def _layout_w_ukv():
    keys = [s for h in range(MLA_HEADS) for s in (_seg(LANES * h, HEAD), _zero(HEAD))]
    values = [s for h in range(MLA_HEADS) for s in (_seg(LANES * h + HEAD, HEAD), _zero(HEAD))]
    return keys + values


def _layout_w_gate_up():
    return [_seg(half + j, GU_TILE) for j in range(0, D_FF, GU_TILE) for half in (0, D_FF)]


LAYOUTS = dict(w_in=_layout_w_in(), w_uq=_layout_w_uq(), w_ukv=_layout_w_ukv(), w_gate_up=_layout_w_gate_up())
SWAPPED_UPDATE = ("w_gate_up",)
OWN_COLS = dict(w_in=1952, w_uq=384, w_ukv=512, w_gate_up=2 * D_FF)


def _plan_extend(layout, shard):
    plan = []
    for width, terms in layout:
        if not terms:
            plan.append((width, []))
            continue
        (first, sign), = terms
        while width:
            g, off = divmod(first, shard)
            w = min(width, shard - off)
            plan.append((w, [(g, off, sign)]))
            first, width = first + w, width - w
    return [plan]


def _plan_fold(layout, own_cols):
    sources = [[] for _ in range(own_cols)]
    e = 0
    for width, terms in layout:
        for first, sign in terms:
            for i in range(width):
                sources[first + i].append((e + i, sign))
        e += width
    shard = own_cols // N_DEV
    plans = {}
    for g in range(N_DEV):
        plan, n = [], g * shard
        while n < (g + 1) * shard:
            w = 1
            while n + w < (g + 1) * shard and [(c + w, sg) for c, sg in sources[n]] == sources[n + w]:
                w += 1
            plan.append((w, [(0, c, sg) for c, sg in sources[n]]))
            n += w
        plans[g] = plan
    return [plans[2 * p + c] for c in range(2) for p in range(4)]


def _assemble(src, plans, out_cols, out_dtype, *, name, transposed=False):
    g, rows, c = src.shape
    tm = _row_tile(rows, 256)
    pad = -out_cols % LANES if transposed else 0

    def body(s_ref, o_ref):
        blocks = [s_ref[i].astype(F32) for i in range(g)]
        for d, plan in enumerate(plans):
            pieces = []
            for width, terms in plan + ([(pad, [])] if pad else []):
                v = None
                for b, first, sign in terms:
                    t = blocks[b][:, first:first + width]
                    t = -t if sign < 0 else t
                    v = t if v is None else v + t
                pieces.append(jnp.zeros((tm, width), F32) if v is None else v)
            block = pieces[0] if len(pieces) == 1 else jnp.concatenate(pieces, axis=1)
            o_ref[d] = (block.T[:out_cols, :] if transposed else block).astype(o_ref.dtype)

    block, shape = ((out_cols, tm), (out_cols, rows)) if transposed else ((tm, out_cols), (rows, out_cols))
    return pl.pallas_call(
        body, name=name, grid=(rows // tm,), in_specs=[pl.BlockSpec((g, tm, c), lambda i: (0, i, 0))],
        out_specs=pl.BlockSpec((len(plans),) + block, (lambda i: (0, 0, i)) if transposed else (lambda i: (0, i, 0))),
        out_shape=jax.ShapeDtypeStruct((len(plans),) + shape, out_dtype), compiler_params=_params(("parallel",)),
    )(src)


def _extend(nm, gathered, *, name):
    layout = LAYOUTS[nm]
    return _assemble(gathered, _plan_extend(layout, OWN_COLS[nm] // N_DEV), sum(w for w, _ in layout), BF16, name=name)[0]


def _fold_to_shards(nm, grad_ext, *, name):
    shards = _assemble(grad_ext[None], _plan_fold(LAYOUTS[nm], OWN_COLS[nm]), OWN_COLS[nm] // N_DEV, F32, name=name,
                       transposed=nm in SWAPPED_UPDATE)
    return shards.reshape((2, 4) + shards.shape[1:])


def _rope_tables(s):
    inv = 1.0 / (ROPE_THETA ** (jnp.arange(0, MLA_ROPE, 2, dtype=F32) / MLA_ROPE))
    ang = jnp.arange(s, dtype=F32)[:, None] * inv[None, :]
    cos, sin = jnp.cos(ang), jnp.sin(ang)
    c32, s32 = jnp.concatenate([cos, cos], axis=1), jnp.concatenate([sin, sin], axis=1)
    zeros, ones = jnp.zeros((s, HEAD), F32), jnp.ones((s, HEAD), F32)
    tq = jnp.concatenate([ones, c32, s32], axis=1) * (1.0 / math.sqrt(MLA_NOPE + MLA_ROPE))
    return (jnp.tile(tq, (1, MLA_HEADS)), jnp.concatenate([zeros, c32, c32], axis=1),
            jnp.concatenate([zeros, s32, s32], axis=1))


def _gn(y):
    return y * _rstd(y, D_GROUP)


def _mixer_fwd(x, w, tabs, l, after_attention=None):
    s = x.shape[0]
    tq, tkc, tks = tabs
    n = lambda t: f"l{l}_{t}"
    h = _rms_fwd(("row", x, D_MODEL, 0), w["attn_norm"], name=n("attn_norm"), rows=s, width=D_MODEL)
    proj = _mm(h, w["w_in"], name=n("in_proj"))
    def prep(cq, ckv, kr, krp, gq, gkv, wuq, wukv, tqv, tc, ts):
        cqn = (cq * _rstd(cq, 256) * gq).astype(BF16)
        ckvn = (ckv * _rstd(ckv, 128) * gkv).astype(BF16)
        qe = jnp.dot(cqn, wuq, preferred_element_type=F32)
        kve = jnp.dot(ckvn, wukv, preferred_element_type=F32)
        kb = kr * tc + krp * ts
        kvv = kve[:, 512:]
        lane = lax.broadcasted_iota(jnp.int32, kvv.shape, 1) & (LANES - 1)
        v = jnp.where(lane == HEAD, 1.0, kvv)
        k = kve[:, :512] + jnp.tile(kb, (1, MLA_HEADS))
        return cqn, ckvn, qe * tqv, k, k, v, v

    cqn, ckvn, qm, km, kmt, vm, vmt = _rowwise(
        prep, name=n("mla_prep"), rows=s,
        ins=[("row", proj, 256, C_CQ[0] // 256), ("row", proj, 128, C_CKV[0] // 128),
             ("row", proj, 128, C_KR[0] // 128), ("row", proj, 128, C_KRP[0] // 128),
             ("full", w["mla_q_norm"]), ("full", w["mla_kv_norm"]), ("full", w["w_uq"]), ("full", w["w_ukv"]),
             ("row", tq, 512, 0), ("row", tkc, 128, 0), ("row", tks, 128, 0)],
        outs=[(256, BF16), (128, BF16), (512, BF16, MLA_HEADS), (512, BF16, MLA_HEADS), (512, BF16, MLA_HEADS, "T"),
              (512, BF16, MLA_HEADS), (512, BF16, MLA_HEADS, "T")])
    y_a, lse_row = _mla_fwd(qm, km, vmt, name=n("mla_fwd"), s=s, nh=MLA_HEADS)
    mix_norm = w["mix_norm"]
    if after_attention is not None:
        mix_norm = mix_norm + after_attention(y_a)[0, 0]
    y_b = _conv_fwd(proj, w["conv_w"], name=n("conv_fwd"), s=s)
    y_c = _pool_fwd(proj, w["pool_wbd"], w["pool_scale"], name=n("pool_fwd"), s=s)
    y_d, lse_d = _swa_fwd(proj, w["hp_swa"], name=n("swa_fwd"), s=s, scale=1.0 / math.sqrt(HEAD))

    def mix(ya, yb, yc, yd, mn):
        return (jnp.concatenate([_gn(_unpad_heads(ya, 4)), _gn(yb), _gn(yc), _gn(_unpad_heads(yd, 4))], axis=1) * mn,)

    mixed = _rowwise(mix, name=n("group_norm"), rows=s,
                     ins=[("heads", y_a), ("row", y_b, 256, 0), ("row", y_c, 256, 0), ("row", y_d, 512, 0),
                          ("full", mix_norm)], outs=[(D_MODEL, BF16)])[0]
    x1 = _mm(mixed, w["w_o"], res=x, name=n("out_proj"))
    saved = dict(x=x, h=h, proj=proj, cqn=cqn, ckvn=ckvn, qm=qm, km=km, kmt=kmt, vm=vm, y_a=y_a, lse_row=lse_row,
                 y_b=y_b, y_c=y_c, y_d=y_d, lse_d=lse_d, mixed=mixed)
    return x1, saved


def _ffn_fwd(x1, w, l):
    s = x1.shape[0]
    n = lambda t: f"l{l}_{t}"
    h2 = _rms_fwd(("row", x1, D_MODEL, 0), w["ffn_norm"], name=n("ffn_norm"), rows=s, width=D_MODEL)

    def swiglu(gu):
        g, u = gu[:, :GU_TILE], gu[:, GU_TILE:]
        return gu, g * jax.nn.sigmoid(g) * u

    gu, act = _mm(h2, w["w_gate_up"], tm=2048, tn=2 * GU_TILE, name=n("gate_up"),
                  epilogue=(swiglu, [], [(2 * D_FF, BF16), (D_FF, BF16)]))
    x2 = _mm(act, w["w_down"], res=x1, tk=D_FF // 2, name=n("down"))
    return x2, dict(x1=x1, h2=h2, gu=gu, act=act)


def _ffn_bwd_down(dx2, sv, w, l):
    n = lambda t: f"l{l}_{t}"

    def swiglu_bwd(da, gu):
        gt, u = gu[:, :GU_TILE].astype(F32), gu[:, GU_TILE:].astype(F32)
        sg = jax.nn.sigmoid(gt)
        return (jnp.concatenate([da * u * sg * (1.0 + gt * (1.0 - sg)), da * gt * sg], axis=1),)

    dgu = _mm(dx2[1], w["w_down"], tb=True, tm=2048, tn=GU_TILE, name=n("d_act"),
              epilogue=(swiglu_bwd, [sv["gu"]], [(2 * D_FF, BF16)]))[0]
    g = dict(w_down=_mm(sv["act"], dx2[1], ta=True, tm=D_FF // 2, name=n("dw_down")))
    return dgu, g


def _ffn_bwd_up(dx2, dgu, sv, w, l):
    s = dgu.shape[0]
    n = lambda t: f"l{l}_{t}"
    dh2 = _mm_nt_deep(dgu, w["w_gate_up"], name=n("d_h2"))
    g = dict(w_gate_up=_mm(sv["h2"], dgu, ta=True, tn=D_FF // 2, name=n("dw_gate_up")))
    dx1, dx1_b, g["ffn_norm"] = _rms_bwd(("row", sv["x1"], D_MODEL, 0), w["ffn_norm"], dh2, dx2[0],
                                         name=n("ffn_norm_bwd"), rows=s, width=D_MODEL, out_dtypes=(F32, BF16))
    return (dx1, dx1_b), g


def _mixer_bwd_out(dx1, sv, w, l):
    s = dx1[1].shape[0]
    n = lambda t: f"l{l}_{t}"
    dmixed = _mm(dx1[1], w["w_o"], tb=True, name=n("d_mixed"))
    g = dict(w_o=_mm(sv["mixed"], dx1[1], ta=True, name=n("dw_o")))

    def mix_bwd(dm, ya, yb, yc, yd, mn):
        outs, dmn = [], []
        for i, y in enumerate((_unpad_heads(ya, 4), yb, yc, _unpad_heads(yd, 4))):
            lo, hi = i * D_GROUP, (i + 1) * D_GROUP
            r = _rstd(y, D_GROUP)
            nrm = y * r
            dmg = dm[:, lo:hi]
            dn = dmg * mn[:, lo:hi]
            dy = r * (dn - nrm * (jnp.sum(dn * nrm, axis=-1, keepdims=True) * (1.0 / D_GROUP)))
            outs.append(_pad_heads(dy, 4) if i in (0, 3) else dy)
            dmn.append(jnp.sum(dmg * nrm, axis=0, keepdims=True))
        return (*outs, jnp.concatenate(dmn, axis=1))

    dy_a, dy_b, dy_c, dy_d, g["mix_norm"] = _rowwise(
        mix_bwd, name=n("group_norm_bwd"), rows=s,
        ins=[("row", dmixed, D_MODEL, 0), ("heads", sv["y_a"]), ("row", sv["y_b"], 256, 0),
             ("row", sv["y_c"], 256, 0), ("row", sv["y_d"], 512, 0), ("full", w["mix_norm"])],
        outs=[(512, F32, MLA_HEADS), (256, F32), (256, F32), (512, F32)], reds=[(1, D_MODEL)])
    return (dy_a, dy_b, dy_c, dy_d), g


def _mixer_bwd_in(dx1, dys, sv, w, tabs, l):
    s = dx1[0].shape[0]
    tq, tkc, tks = tabs
    n = lambda t: f"l{l}_{t}"
    dy_a, dy_b, dy_c, dy_d = dys
    g = {}

    proj = sv["proj"]
    dq_sw, dk_sw, dv_sw, dsink = _swa_bwd(proj, sv["y_d"], dy_d, sv["lse_d"], w["hp_swa"], name=n("swa_bwd"), s=s,
                                          scale=1.0 / math.sqrt(HEAD))
    g["swa_sinks"] = dsink[:, 0, 0]

    dqm, dkm, dvm = _mla_bwd(sv["qm"], sv["km"], sv["kmt"], sv["vm"], sv["y_a"], dy_a, sv["lse_row"], w["hp_swa"],
                             name=n("mla_bwd"), s=s, nh=MLA_HEADS)

    def rms_bwd(x, gv, dy, width):
        r = _rstd(x, width)
        dyg = dy * gv
        dx = r * dyg - x * (r * r * r) * (jnp.sum(dyg * x, axis=-1, keepdims=True) * (1.0 / width))
        return dx, jnp.sum(dy * x * r, axis=0, keepdims=True)

    def prep_bwd(dq, dk, dv, cq, ckv, gq, gkv, wuq, wukv, tqv, tc, ts):
        dkb = dk[:, 0:128] + dk[:, 128:256] + dk[:, 256:384] + dk[:, 384:512]
        dq_ext = (dq * tqv).astype(BF16)
        dkv_ext = jnp.concatenate([dk.astype(BF16), dv], axis=1)
        dcqn = lax.dot_general(dq_ext, wuq, NT_DIMS, preferred_element_type=F32)
        dckvn = lax.dot_general(dkv_ext, wukv, NT_DIMS, preferred_element_type=F32)
        dcq, dgq = rms_bwd(cq, gq, dcqn, 256)
        dckv, dgkv = rms_bwd(ckv, gkv, dckvn, 128)
        return dq_ext, dkv_ext, dkb * tc, dkb * ts, dcq, dckv, dgq, dgkv

    dq_ext, dkv_ext, dkr, dkrp, dcq, dckv, g["mla_q_norm"], g["mla_kv_norm"] = _rowwise(
        prep_bwd, name=n("mla_prep_bwd"), rows=s,
        ins=[("heads", dqm), ("heads", dkm), ("heads", dvm), ("row", proj, 256, C_CQ[0] // 256),
             ("row", proj, 128, C_CKV[0] // 128), ("full", w["mla_q_norm"]), ("full", w["mla_kv_norm"]),
             ("full", w["w_uq"]), ("full", w["w_ukv"]), ("row", tq, 512, 0), ("row", tkc, 128, 0), ("row", tks, 128, 0)],
        outs=[(512, BF16), (1024, BF16), (128, BF16), (128, BF16), (256, BF16), (128, BF16)],
        reds=[(1, 256), (1, 128)])
    g["w_uq"] = _mm(sv["cqn"], dq_ext, ta=True, name=n("dw_uq"))
    g["w_ukv"] = _mm(sv["ckvn"], dkv_ext, ta=True, name=n("dw_ukv"))

    dgb, dgc, duc, g["conv_w"] = _conv_bwd(dy_b, proj, w["conv_w"], name=n("conv_bwd"), s=s)
    dup, g["pool_wbd"], g["pool_scale"] = _pool_bwd(dy_c, proj, w["pool_wbd"], w["pool_scale"], name=n("pool_bwd"), s=s)

    dproj = jnp.concatenate([dq_sw, dcq, dgb, dgc, duc, dup, dk_sw.astype(BF16), dv_sw.astype(BF16), dckv, dkr, dkrp],
                            axis=1)
    g["w_in"] = _mm(sv["h"], dproj, ta=True, name=n("dw_in"))
    return dproj, g


def _mixer_bwd_norm(dx1, dproj, sv, w, l):
    n = lambda t: f"l{l}_{t}"
    dh = _mm(dproj, w["w_in"], tb=True, name=n("d_h"))
    dx0, dx0_b, dg = _rms_bwd(("row", sv["x"], D_MODEL, 0), w["attn_norm"], dh, dx1[0], name=n("attn_norm_bwd"),
                              rows=dh.shape[0], width=D_MODEL, out_dtypes=(F32, BF16))
    return (dx0, dx0_b), dict(attn_norm=dg)


def _loss_head(x, target, g, *, s):
    def fn(xv, tv, gv):
        r = _rstd(xv, D_MODEL)
        e = xv * r * gv - tv
        part = jnp.sum(jnp.sum(e * e, axis=1, keepdims=True), axis=0, keepdims=True) * (0.5 / D_MODEL)
        dy = e * (1.0 / D_MODEL)
        dyg = dy * gv
        dx = r * dyg - xv * (r * r * r) * (jnp.sum(dyg * xv, axis=-1, keepdims=True) * (1.0 / D_MODEL))
        return dx, dx, jnp.sum(dy * xv * r, axis=0, keepdims=True), jnp.broadcast_to(part, (1, LANES))

    return _rowwise(fn, name="loss_head", rows=s,
                    ins=[("row", x, D_MODEL, 0), ("row", target, D_MODEL, 0), ("full", g)],
                    outs=[(D_MODEL, F32), (D_MODEL, BF16)], reds=[(1, D_MODEL), (1, LANES)])


def _alibi_slopes(n):
    return np.asarray([2.0 ** (-8.0 * (i + 1) / n) for i in range(n)], dtype=np.float32)


MIXER_WEIGHTS = ("w_in", "w_uq", "w_ukv", "conv_w", "w_o")
FFN_WEIGHTS = ("w_gate_up", "w_down")


def _mixer_weights(full, rep, l):
    pw = rep["pool_w"][l]
    z = jnp.zeros((HEAD, HEAD), F32)
    wbd = jnp.stack([jnp.block([[pw[2 * j], z], [z, pw[2 * j + 1]]]) for j in range(2)])
    return dict(
        attn_norm=rep["attn_norm"][l][None], w_in=full["w_in"], mla_q_norm=rep["mla_q_norm"][l][None],
        w_uq=full["w_uq"], mla_kv_norm=rep["mla_kv_norm"][l][None], w_ukv=full["w_ukv"],
        conv_w=full["conv_w"], pool_wbd=wbd, pool_scale=rep["pool_scale"][l][None],
        mix_norm=rep["mix_norm"][l][None], w_o=full["w_o"],
        hp_swa=jnp.stack([jnp.asarray(_alibi_slopes(SWA_HEADS)), rep["swa_sinks"][l]], axis=1))


def _ffn_weights(full, rep, l):
    return dict(ffn_norm=rep["ffn_norm"][l][None], w_gate_up=full["w_gate_up"], w_down=full["w_down"])


def _small_grads(g):
    rows = ("attn_norm", "mla_q_norm", "mla_kv_norm", "pool_scale", "ffn_norm", "mix_norm")
    out = {nm: g[nm][0] for nm in rows if nm in g}
    if "swa_sinks" in g:
        out["swa_sinks"] = g["swa_sinks"]
    if "pool_wbd" in g:
        e = g["pool_wbd"]
        out["pool_w"] = jnp.stack([e[j // 2][HEAD * (j % 2):HEAD * (j % 2 + 1), HEAD * (j % 2):HEAD * (j % 2 + 1)]
                                   for j in range(4)])
    return out


def kernel(x, attn_norm, w_in, mla_q_norm, w_uq, mla_kv_norm, w_ukv, conv_w, pool_w, pool_scale, swa_sinks, mix_norm, w_o, ffn_norm, w_gate_up, w_down, final_norm, loss_target, m_attn_norm, m_w_in, m_mla_q_norm, m_w_uq, m_mla_kv_norm, m_w_ukv, m_conv_w, m_pool_w, m_pool_scale, m_swa_sinks, m_mix_norm, m_w_o, m_ffn_norm, m_w_gate_up, m_w_down, m_final_norm, v_attn_norm, v_w_in, v_mla_q_norm, v_w_uq, v_mla_kv_norm, v_w_ukv, v_conv_w, v_pool_w, v_pool_scale, v_swa_sinks, v_mix_norm, v_w_o, v_ffn_norm, v_w_gate_up, v_w_down, v_final_norm):
    given = dict(locals())
    sh_names = [nm for nm, _, _ in SHARDED]
    sh_axis = {nm: ax - 1 for nm, _, ax in SHARDED}
    rep_names = [nm for nm, _ in REPLICATED]
    rep_shapes = [shp for _, shp in REPLICATED]
    rep = {nm: given[nm] for nm in rep_names if nm != "loss"}
    me = 4 * lax.axis_index("x") + 2 * lax.axis_index("y") + lax.axis_index("c")
    my_chip = 2 * lax.axis_index("x") + lax.axis_index("y")
    core = lax.axis_index("c").astype(jnp.int32).reshape(1)

    def behind(token, a):
        return a + token[0, 0].astype(a.dtype)

    def wire(nm, l):
        if nm == "conv_w":
            return lax.bitcast_convert_type(given[nm][l], BF16).reshape(3, -1)
        return given[nm][l].astype(BF16)

    def whole(nm, g, tag):
        if nm in LAYOUTS:
            return _extend(nm, g, name=f"extend_{nm}_{tag}")
        if nm == "conv_w":
            g = lax.bitcast_convert_type(g.reshape(N_DEV, 3, -1, 2), F32)
        return _shards_to_full(g, sh_axis[nm])

    def near_start(names, l, after, tag):
        srcs = [wire(nm, l) for nm in names]
        return _start_copies(_plan_gather_near, 4, srcs, [(N_DEV,) + a.shape for a in srcs], after, name=f"start_gather_{tag}")

    def pass_on(handle, after, tag):
        srcs, lands = _wait_copies(_plan_gather_near, handle, after, name=f"wait_gather_{tag}")
        handle, token = _start_copies(_plan_gather_pass, 3, [], lands, [], name=f"start_pass_{tag}")
        return (srcs, handle), token

    def gathered(names, state, after, tag):
        srcs, handle = state
        _, lands = _wait_copies(_plan_gather_pass, handle, after, name=f"wait_pass_{tag}")
        return {nm: whole(nm, lax.dynamic_update_index_in_dim(land, src, me, 0), tag)
                for nm, src, land in zip(names, srcs, lands)}

    layer1 = MIXER_WEIGHTS + FFN_WEIGHTS
    got = _all_gather([wire(nm, 0) for nm in MIXER_WEIGHTS], name="gather_mixer0")
    full_m0 = {nm: whole(nm, g, "mixer0") for nm, g in zip(MIXER_WEIGHTS, got)}
    h_f0, tok = near_start(FFN_WEIGHTS, 0, [], "ffn0")
    h_l1, tok = near_start(layer1, 1, [tok], "layer1")

    xs, target = x[0], loss_target[0]
    s = xs.shape[0]
    tabs = _rope_tables(s)
    wm, wf, svm, svf = [None] * DEPTH, [None] * DEPTH, [None] * DEPTH, [None] * DEPTH
    wm[0] = _mixer_weights(full_m0, rep, 0)
    wm[0]["attn_norm"] = behind(tok, wm[0]["attn_norm"])
    passing = {}

    def pass_ffn0(y_a):
        passing["ffn0"], token = pass_on(h_f0, [y_a], "ffn0")
        return token

    x1, svm[0] = _mixer_fwd(xs, wm[0], tabs, 0, after_attention=pass_ffn0)
    wf[0] = _ffn_weights(gathered(FFN_WEIGHTS, passing["ffn0"], [x1], "ffn0"), rep, 0)
    passing["layer1"], tok = pass_on(h_l1, [x1], "layer1")
    wf[0]["ffn_norm"] = behind(tok, wf[0]["ffn_norm"])
    x2, svf[0] = _ffn_fwd(x1, wf[0], 0)
    full_1 = gathered(layer1, passing["layer1"], [x2], "layer1")
    wm[1], wf[1] = _mixer_weights(full_1, rep, 1), _ffn_weights(full_1, rep, 1)
    x1, svm[1] = _mixer_fwd(x2, wm[1], tabs, 1)
    x2, svf[1] = _ffn_fwd(x1, wf[1], 1)
    dx_f, dx_b, d_final, loss = _loss_head(x2, target, rep["final_norm"][None], s=s)
    dx = (dx_f, dx_b)

    parts = {}

    def reduce_start(grads, l, after, tag):
        names = [nm for nm in sh_names if nm in grads]
        mine = [_fold_to_shards(nm, grads[nm], name=f"fold_{nm}_{l}") if nm in LAYOUTS
                else _full_to_shards(grads[nm], sh_axis[nm]) for nm in names]
        handle, token = _start_copies(_plan_sibling, 1, mine, [m.shape[1:] for m in mine], after, name=f"start_sibling_{tag}")
        return (names, l, handle), token

    def reduce_mid(state, after, tag):
        names, l, handle = state
        mine, theirs = _wait_copies(_plan_sibling, handle, after, name=f"wait_sibling_{tag}")
        sums = [_add_own(g, o, core, name=f"chip_sum_{nm}_{l}") for nm, g, o in zip(names, mine, theirs)]
        handle, token = _start_copies(_plan_chips, 3, sums, [a.shape for a in sums], [], name=f"start_chips_{tag}")
        return (names, l, handle), token

    def reduce_end(state, after, tag):
        names, l, handle = state
        sums, lands = _wait_copies(_plan_chips, handle, after, name=f"wait_chips_{tag}")
        for nm, own, land in zip(names, sums, lands):
            parts[nm, l] = lax.dynamic_update_index_in_dim(land, lax.dynamic_index_in_dim(own, my_chip, 0, keepdims=False),
                                                           my_chip, 0)

    small = [None] * DEPTH
    in_flight = []
    pending = None
    for l in reversed(range(DEPTH)):
        dgu, g_down = _ffn_bwd_down(dx, svf[l], wf[l], l)
        if pending is not None:
            state, token = reduce_mid(pending, [dgu], f"mixer{l + 1}")
            in_flight.append((state, f"mixer{l + 1}"))
            wf[l]["ffn_norm"] = behind(token, wf[l]["ffn_norm"])
        dx1, g_up = _ffn_bwd_up(dx, dgu, svf[l], wf[l], l)
        g_ffn = {**g_down, **g_up}
        state, token = reduce_start(g_ffn, l, [], f"ffn{l}")
        wm[l]["mix_norm"] = behind(token, wm[l]["mix_norm"])
        dys, g_out = _mixer_bwd_out(dx1, svm[l], wm[l], l)
        state, token = reduce_mid(state, [dys[0]], f"ffn{l}")
        in_flight.append((state, f"ffn{l}"))
        wm[l]["hp_swa"] = behind(token, wm[l]["hp_swa"])
        dproj, g_in = _mixer_bwd_in(dx1, dys, svm[l], wm[l], tabs, l)
        g_mixer = {**g_out, **g_in}
        pending, token = reduce_start(g_mixer, l, [], f"mixer{l}")
        wm[l]["attn_norm"] = behind(token, wm[l]["attn_norm"])
        dx, g_norm = _mixer_bwd_norm(dx1, dproj, svm[l], wm[l], l)
        small[l] = _small_grads({**g_ffn, **g_mixer, **g_norm})
    grads = {nm: jnp.stack([small[l][nm] for l in range(DEPTH)]) for nm in rep_names if nm in small[0]}
    grads["final_norm"] = d_final[0]
    grads["loss"] = loss[0, :1]
    zero = jnp.zeros((1,), F32)
    small = _all_gather([behind(token, _pack([grads[nm] for nm in rep_names]))], name="gather_small_grads")

    last, token = reduce_mid(pending, [dx[0], small[0]], "mixer0")
    for state, tag in in_flight:
        reduce_end(state, [], tag)
    grad_x = dx[0]

    def adamw(nm, after):
        swap = (lambda a: jnp.swapaxes(a, 1, 2)) if nm in SWAPPED_UPDATE else (lambda a: a)
        res = _adamw([parts[nm, l] for l in range(DEPTH)], swap(given[nm]), swap(given["m_" + nm]), swap(given["v_" + nm]),
                     after, name=f"adamw_{nm}")
        return [swap(a) for a in res]

    sh_out = {nm: adamw(nm, token) for nm in FFN_WEIGHTS}
    packs = [_pack([given.get(pre + nm, zero) for nm in rep_names])[None] for pre in ("", "m_", "v_")]
    rep_res = _adamw(small, *packs, token, name="adamw_replicated")
    rep_out = [dict(zip(rep_names, _unpack(o[0], rep_shapes))) for o in rep_res]

    reduce_end(last, [rep_res[0], sh_out["w_down"][0]], "mixer0")
    sh_out.update({nm: adamw(nm, token) for nm in MIXER_WEIGHTS})

    out = [rep_out[0]["loss"][0], grad_x[None]]
    for i in range(4):
        out += [sh_out[nm][i] if nm in sh_axis else rep_out[i][nm] for nm in WEIGHT_ORDER]
    return tuple(out)
```

```python
import functools
import math

import numpy as np
import jax
import jax.numpy as jnp
from jax import lax
from jax.experimental import pallas as pl
from jax.experimental.pallas import tpu as pltpu

F32 = jnp.float32
BF16 = jnp.bfloat16

D_MODEL = 1024
DEPTH = 2
D_GROUP = 256
MLA_HEADS = 4
MLA_NOPE = 64
MLA_ROPE = 32
ROPE_THETA = 10000.0
POOL_WINDOWS = (2, 4, 8, 16)
SWA_HEADS = 4
SWA_KV_HEADS = 2
SWA_WINDOW = 128
D_FF = 2816
GU_TILE = 256
RMS_EPS = 1e-6
LANES = 128
HEAD = 64
VMEM_LIMIT = 48 * 1024 * 1024
NEG = -1e30

ADAM_LR = 0.001
ADAM_B1 = 0.9
ADAM_B2 = 0.999
ADAM_EPS = 1e-08
ADAM_WD = 0.01
ADAM_STEP = 10

N_DEV = 8
PACK_COLS = 1024

C_QSW, C_CQ, C_GB, C_GC, C_UCONV, C_UPOOL = (0, 512), (512, 256), (768, 256), (1024, 256), (1280, 256), (1536, 256)
C_KSW, C_VSW, C_CKV, C_KR, C_KRP = (1792, 256), (2048, 256), (2304, 128), (2432, 128), (2560, 128)

SHARDED = (("w_in", (DEPTH, 1024, 244), 2), ("w_uq", (DEPTH, 256, 48), 2), ("w_ukv", (DEPTH, 128, 64), 2),
           ("conv_w", (DEPTH, 3, 32), 2), ("w_o", (DEPTH, 128, 1024), 1), ("w_gate_up", (DEPTH, 1024, 704), 2),
           ("w_down", (DEPTH, 352, 1024), 1))
REPLICATED = (("attn_norm", (DEPTH, 1024)), ("mla_q_norm", (DEPTH, 256)), ("mla_kv_norm", (DEPTH, 128)),
              ("pool_w", (DEPTH, 4, 64, 64)), ("pool_scale", (DEPTH, 256)), ("swa_sinks", (DEPTH, 4)),
              ("mix_norm", (DEPTH, 1024)), ("ffn_norm", (DEPTH, 1024)), ("final_norm", (1024,)), ("loss", (1,)))
WEIGHT_ORDER = ("attn_norm", "w_in", "mla_q_norm", "w_uq", "mla_kv_norm", "w_ukv", "conv_w", "pool_w", "pool_scale",
                "swa_sinks", "mix_norm", "w_o", "ffn_norm", "w_gate_up", "w_down", "final_norm")


def _params(sem):
    return pltpu.CompilerParams(dimension_semantics=sem, vmem_limit_bytes=VMEM_LIMIT)


def _pick(dim, target):
    if dim <= target:
        return dim
    best = None
    for t in range(LANES, target + 1, LANES):
        if dim % t == 0:
            best = t
    assert best is not None, (dim, target)
    return best


def _mm(a, b, *, name, ta=False, tb=False, res=None, out_dtype=F32, tm=1024, tn=1024, tk=1024, epilogue=None):
    m, k = (a.shape[1], a.shape[0]) if ta else a.shape
    n = b.shape[0] if tb else b.shape[1]
    assert (b.shape[1] if tb else b.shape[0]) == k
    tm, tn, tk = _pick(m, tm), _pick(n, tn), _pick(k, tk)
    nj, nk = n // tn, k // tk
    dims = (((0 if ta else 1,), (1 if tb else 0,)), ((), ()))
    fn, extra, outs = epilogue if epilogue is not None else (None, [], [(n, out_dtype)])
    if res is not None:
        assert epilogue is None
        fn, extra = (lambda acc, r: (acc + r,)), [res]
    n_in, n_out = 2 + len(extra), len(outs)

    def body(*refs):
        a_ref, b_ref, acc_ref = refs[0], refs[1], refs[-1]
        kk = pl.program_id(2)

        def product():
            return lax.dot_general(a_ref[...].astype(BF16), b_ref[...].astype(BF16), dims, preferred_element_type=F32)

        def finish(acc):
            tiles = (acc,) if fn is None else fn(acc, *[r[...] for r in refs[2:n_in]])
            for o_ref, tile in zip(refs[n_in:n_in + n_out], tiles):
                o_ref[...] = tile.astype(o_ref.dtype)

        if nk == 1:
            finish(product())
            return

        @pl.when(kk == 0)
        def _():
            acc_ref[...] = jnp.zeros_like(acc_ref)

        acc_ref[...] += product()

        @pl.when(kk == nk - 1)
        def _():
            finish(acc_ref[...])

    def col_tiles(width):
        assert width % (nj * LANES) == 0, (width, nj)
        return pl.BlockSpec((tm, width // nj), lambda i, j, kk: (i, j))

    a_spec = pl.BlockSpec((tk, tm), lambda i, j, kk: (kk, i)) if ta else pl.BlockSpec((tm, tk), lambda i, j, kk: (i, kk))
    b_spec = pl.BlockSpec((tn, tk), lambda i, j, kk: (j, kk)) if tb else pl.BlockSpec((tk, tn), lambda i, j, kk: (kk, j))
    res_ = pl.pallas_call(
        body, name=name, grid=(m // tm, nj, nk), in_specs=[a_spec, b_spec] + [col_tiles(e.shape[1]) for e in extra],
        out_specs=[col_tiles(w) for w, _ in outs],
        out_shape=[jax.ShapeDtypeStruct((m, w), dt) for w, dt in outs],
        scratch_shapes=[pltpu.VMEM((tm, tn), F32)] if nk > 1 else [],
        compiler_params=_params(("parallel", "parallel", "arbitrary")),
    )(a, b, *extra)
    return res_[0] if epilogue is None else res_


def _mm_nt_deep(a, b, *, name, tm=512, nk=4):
    m, k = a.shape
    n = b.shape[0]
    tm = min(tm, m)
    tk, ni = k // nk, m // tm
    assert k % (nk * LANES) == 0 and m % tm == 0

    def body(a_hbm, b_ref, o_ref, abuf, sem):
        i = pl.program_id(0)
        slot = i % 2

        def chunk(row, sl, kk):
            return pltpu.make_async_copy(a_hbm.at[pl.ds(pl.multiple_of(row * tm, tm), tm), pl.ds(kk * tk, tk)],
                                         abuf.at[sl, :, pl.ds(kk * tk, tk)], sem.at[sl * nk + kk])

        @pl.when(i == 0)
        def _():
            for kk in range(nk):
                chunk(0, 0, kk).start()

        @pl.when(i + 1 < ni)
        def _():
            for kk in range(nk):
                chunk(i + 1, 1 - slot, kk).start()

        for kk in range(nk):
            chunk(i, slot, kk).wait()
        o_ref[...] = lax.dot_general(abuf[slot], b_ref[...], NT_DIMS, preferred_element_type=F32)

    return pl.pallas_call(
        body, name=name, grid=(ni,),
        in_specs=[pl.BlockSpec(memory_space=pl.ANY), pl.BlockSpec((n, k), lambda i: (0, 0))],
        out_specs=pl.BlockSpec((tm, n), lambda i: (i, 0)), out_shape=jax.ShapeDtypeStruct((m, n), F32),
        scratch_shapes=[pltpu.VMEM((2, tm, k), a.dtype), pltpu.SemaphoreType.DMA((2 * nk,))],
        compiler_params=_params(("arbitrary",)),
    )(a, b)


def _rowwise(fn, *, name, rows, ins, outs, reds=(), tm=512):
    tm = min(tm, rows)
    assert rows % tm == 0
    n_in, n_out = len(ins), len(outs)

    def body(*refs):
        vals = [jnp.concatenate([r[h] for h in range(r.shape[0])], axis=1) if spec[0] == "heads" else r[...]
                for spec, r in zip(ins, refs[:n_in])]
        res = fn(*vals)
        for out, r, v in zip(outs, refs[n_in:n_in + n_out], res[:n_out]):
            if len(out) >= 3:
                for h in range(out[2]):
                    piece = v[:, h * LANES:(h + 1) * LANES]
                    r[h] = (piece.T if len(out) == 4 else piece).astype(r.dtype)
            else:
                r[...] = v.astype(r.dtype)
        if reds:
            @pl.when(pl.program_id(0) == 0)
            def _():
                for r in refs[n_in + n_out:]:
                    r[...] = jnp.zeros_like(r)

            for r, v in zip(refs[n_in + n_out:], res[n_out:]):
                r[...] += v

    in_specs, args = [], []
    for spec in ins:
        if spec[0] == "row":
            _, arr, width, blk = spec
            in_specs.append(pl.BlockSpec((tm, width), functools.partial(lambda i, blk: (i, blk), blk=blk)))
        elif spec[0] == "heads":
            arr = spec[1]
            in_specs.append(pl.BlockSpec((arr.shape[0], tm, LANES), lambda i: (0, i, 0)))
        else:
            arr = spec[1]
            in_specs.append(pl.BlockSpec(arr.shape, functools.partial(lambda i, nd: (0,) * nd, nd=arr.ndim)))
        args.append(arr)
    def out_spec(o):
        if len(o) == 4:
            return pl.BlockSpec((o[2], LANES, tm), lambda i: (0, 0, i)), (o[2], LANES, rows)
        if len(o) == 3:
            return pl.BlockSpec((o[2], tm, LANES), lambda i: (0, i, 0)), (o[2], rows, LANES)
        return pl.BlockSpec((tm, o[0]), lambda i: (i, 0)), (rows, o[0])

    out_specs = [out_spec(o)[0] for o in outs]
    out_shape = [jax.ShapeDtypeStruct(out_spec(o)[1], o[1]) for o in outs]
    out_specs += [pl.BlockSpec((r, w), lambda i: (0, 0)) for r, w in reds]
    out_shape += [jax.ShapeDtypeStruct((r, w), F32) for r, w in reds]
    return pl.pallas_call(body, name=name, grid=(rows // tm,), in_specs=in_specs, out_specs=out_specs,
                          out_shape=out_shape, compiler_params=_params(("arbitrary",)))(*args)


def _rstd(x, n):
    return lax.rsqrt(jnp.sum(x * x, axis=-1, keepdims=True) * (1.0 / n) + RMS_EPS)


def _rms_fwd(x_spec, g, *, name, rows, width):
    def fn(x, gv):
        return (x * _rstd(x, width) * gv,)
    return _rowwise(fn, name=name, rows=rows, ins=[x_spec, ("full", g)], outs=[(width, BF16)])[0]


def _rms_bwd(x_spec, g, dy, res, *, name, rows, width, out_dtypes):
    def fn(x, gv, dyv, *rest):
        r = _rstd(x, width)
        dyg = dyv * gv
        dx = r * dyg - x * (r * r * r) * (jnp.sum(dyg * x, axis=-1, keepdims=True) * (1.0 / width))
        if rest:
            dx = dx + rest[0]
        return (dx,) * len(out_dtypes) + (jnp.sum(dyv * x * r, axis=0, keepdims=True),)

    ins = [x_spec, ("full", g), ("row", dy, width, 0)]
    if res is not None:
        ins.append(("row", res, width, 0))
    return _rowwise(fn, name=name, rows=rows, ins=ins, outs=[(width, dt) for dt in out_dtypes], reds=[(1, width)])


NT_DIMS = (((1,), (1,)), ((), ()))
TN_DIMS = (((0,), (0,)), ((), ()))
BNT_DIMS = (((2,), (2,)), ((0,), (0,)))
BNN_DIMS = (((2,), (1,)), ((0,), (0,)))


def _mla_tile(s):
    return min(512, s)


def _block_tables(pairs):
    return jnp.asarray([a for a, _ in pairs], jnp.int32), jnp.asarray([b for _, b in pairs], jnp.int32)


def _causal(shape, query_axis):
    return lax.broadcasted_iota(jnp.int32, shape, query_axis) >= lax.broadcasted_iota(jnp.int32, shape, 1 - query_axis)


def _mla_fwd(qa, ka, vta, *, name, s, nh):
    t = _mla_tile(s)
    nq = s // t

    pairs = [(i, j) for i in range(nq) for j in range(i + 1)]

    def body(qb_ref, kb_ref, q_ref, k_ref, vt_ref, o_ref, lse_ref, m_s, acc_s):
        i, j = qb_ref[pl.program_id(0)], kb_ref[pl.program_id(0)]

        @pl.when(j == 0)
        def _():
            m_s[...] = jnp.full_like(m_s, NEG)
            acc_s[...] = jnp.zeros_like(acc_s)

        def step(diag):
            sc = lax.dot_general(k_ref[...], q_ref[...], BNT_DIMS, preferred_element_type=F32)
            if diag:
                sc = jnp.where(_causal(sc.shape[1:], 1)[None], sc, NEG)
            m_prev = m_s[...]
            m_new = jnp.maximum(m_prev, jnp.max(sc, axis=1, keepdims=True))
            p = jnp.exp(sc - m_new).astype(BF16)
            acc_s[...] = (jnp.exp(m_prev - m_new) * acc_s[...]
                          + lax.dot_general(vt_ref[...], p, BNN_DIMS, preferred_element_type=F32))
            m_s[...] = m_new

        pl.when(j < i)(functools.partial(step, False))
        pl.when(j == i)(functools.partial(step, True))

        @pl.when(j == i)
        def _():
            row = lax.broadcasted_iota(jnp.int32, (LANES, t), 0)
            for h in range(nh):
                acc = acc_s[h]
                l = acc[HEAD:HEAD + 1, :]
                o_ref[h] = jnp.where(row < HEAD, acc / l, 0.0).T
                lse_ref[h] = m_s[h] + jnp.log(l)

    q_spec = pl.BlockSpec((nh, t, LANES), lambda p, qb, kb: (0, qb[p], 0))
    k_spec = pl.BlockSpec((nh, t, LANES), lambda p, qb, kb: (0, kb[p], 0))
    vt_spec = pl.BlockSpec((nh, LANES, t), lambda p, qb, kb: (0, 0, kb[p]))
    return pl.pallas_call(
        body, name=name,
        grid_spec=pltpu.PrefetchScalarGridSpec(
            num_scalar_prefetch=2, grid=(len(pairs),), in_specs=[q_spec, k_spec, vt_spec],
            out_specs=[q_spec, pl.BlockSpec((nh, 1, t), lambda p, qb, kb: (0, 0, qb[p]))],
            scratch_shapes=[pltpu.VMEM((nh, 1, t), F32), pltpu.VMEM((nh, LANES, t), F32)]),
        out_shape=[jax.ShapeDtypeStruct((nh, s, LANES), F32), jax.ShapeDtypeStruct((nh, 1, s), F32)],
        compiler_params=_params(("arbitrary",)),
    )(*_block_tables(pairs), qa, ka, vta)


def _mla_bwd(qa, ka, kta, va, o, do, lse_row, after, *, name, s, nh):
    t = _mla_tile(s)
    nq = s // t

    pairs = [(kb, j) for kb in range(nq) for j in range(kb, nq)]

    def body(kb_ref, qb_ref, q_ref, k_ref, kt_ref, v_ref, o_ref, do_ref, lse_ref, after_ref, dq_hbm, dk_ref, dv_ref,
             dqt_s, dk_s, dv_s, d_s, stage, sem):
        kb, j = kb_ref[pl.program_id(0)], qb_ref[pl.program_id(0)]
        cols = pl.ds(pl.multiple_of(j * t, t), t)

        @pl.when(j == kb)
        def _():
            dk_s[...] = jnp.zeros_like(dk_s)
            dv_s[...] = jnp.zeros_like(dv_s)

        @pl.when(kb == 0)
        def _():
            dqt_s[:, :, cols] = jnp.zeros((nh, LANES, t), F32)
            for h in range(nh):
                d_col = jnp.sum(do_ref[h] * o_ref[h], axis=1, keepdims=True)
                d_s[h, :, cols] = jnp.broadcast_to(d_col, (t, LANES)).T[0:1, :]

        def step(diag):
            q = q_ref[...]
            do_b = do_ref[...].astype(BF16)
            sc = lax.dot_general(k_ref[...], q, BNT_DIMS, preferred_element_type=F32)
            if diag:
                sc = jnp.where(_causal(sc.shape[1:], 1)[None], sc, NEG)
            p = jnp.exp(sc - lse_ref[...])
            dv_s[...] += lax.dot_general(p.astype(BF16), do_b, BNN_DIMS, preferred_element_type=F32)
            dp = lax.dot_general(v_ref[...], do_b, BNT_DIMS, preferred_element_type=F32)
            ds = (p * (dp - d_s[:, :, cols])).astype(BF16)
            dk_s[...] += lax.dot_general(ds, q, BNN_DIMS, preferred_element_type=F32)
            dqt_s[:, :, cols] += lax.dot_general(kt_ref[...], ds, BNN_DIMS, preferred_element_type=F32)

        pl.when(j > kb)(functools.partial(step, False))
        pl.when(j == kb)(functools.partial(step, True))

        @pl.when(j == kb)
        def _():
            slot = kb % 2

            def write_out(sl):
                return pltpu.make_async_copy(stage.at[sl], dq_hbm.at[:, cols, :], sem.at[sl])

            @pl.when(kb >= 2)
            def _():
                write_out(slot).wait()

            for h in range(nh):
                stage[slot, h] = dqt_s[h, :, cols].T
            write_out(slot).start()

            @pl.when(kb == nq - 1)
            def _():
                write_out(slot).wait()
                if nq >= 2:
                    write_out(1 - slot).wait()

        @pl.when(j == nq - 1)
        def _():
            dk_ref[...] = dk_s[...]
            dv_ref[...] = dv_s[...].astype(dv_ref.dtype)

    q_spec = pl.BlockSpec((nh, t, LANES), lambda p, kb, qb: (0, qb[p], 0))
    kv_spec = pl.BlockSpec((nh, t, LANES), lambda p, kb, qb: (0, kb[p], 0))
    kt_spec = pl.BlockSpec((nh, LANES, t), lambda p, kb, qb: (0, 0, kb[p]))
    row_spec = pl.BlockSpec((nh, 1, t), lambda p, kb, qb: (0, 0, qb[p]))
    whole = jax.ShapeDtypeStruct((nh, s, LANES), F32)
    return pl.pallas_call(
        body, name=name,
        grid_spec=pltpu.PrefetchScalarGridSpec(
            num_scalar_prefetch=2, grid=(len(pairs),),
            in_specs=[q_spec, kv_spec, kt_spec, kv_spec, q_spec, q_spec, row_spec, pl.BlockSpec(memory_space=pl.ANY)],
            out_specs=[pl.BlockSpec(memory_space=pl.ANY), kv_spec, kv_spec],
            scratch_shapes=[pltpu.VMEM((nh, LANES, s), F32), pltpu.VMEM((nh, t, LANES), F32),
                            pltpu.VMEM((nh, t, LANES), F32), pltpu.VMEM((nh, 1, s), F32),
                            pltpu.VMEM((2, nh, t, LANES), F32), pltpu.SemaphoreType.DMA((2,))]),
        out_shape=[whole, whole, jax.ShapeDtypeStruct((nh, s, LANES), BF16)],
        compiler_params=_params(("arbitrary",)),
    )(*_block_tables(pairs), qa, ka, kta, va, o, do, lse_row, after)


SWA_PIECE = 128
SWA_KEYS = 2 * SWA_PIECE


def _swa_block(s):
    return min(1024, s)


def _swa_piece(hp_ref, h, q, k_ref, v_ref, qpos0, scale):
    kstart = pl.multiple_of(jnp.maximum(qpos0 - SWA_PIECE, 0), SWA_PIECE)
    k = k_ref[pl.ds(kstart, SWA_KEYS), :].astype(BF16)
    v = v_ref[pl.ds(kstart, SWA_KEYS), :].astype(BF16)
    sc = lax.dot_general(q, k, NT_DIMS, preferred_element_type=F32)
    dist = (qpos0 + lax.broadcasted_iota(jnp.int32, sc.shape, 0)) - (kstart + lax.broadcasted_iota(jnp.int32, sc.shape, 1))
    sc = sc * scale - hp_ref[h, 0] * dist.astype(F32)
    sc = jnp.where((dist >= 0) & (dist < SWA_WINDOW), sc, NEG)
    return kstart, k, v, sc


def _swa_fwd(proj, hp, *, name, s, scale):
    tb = _swa_block(s)
    group = SWA_HEADS // SWA_KV_HEADS
    q_off, k_off, v_off = C_QSW[0] // LANES, C_KSW[0] // LANES, C_VSW[0] // LANES

    def body(hp_ref, q_ref, k_ref, v_ref, o_ref, lse_ref):
        h, i = pl.program_id(0), pl.program_id(1)
        sink = hp_ref[h, 1]
        for r in range(0, tb, SWA_PIECE):
            rows = pl.ds(r, SWA_PIECE)
            _, _, v, sc = _swa_piece(hp_ref, h, q_ref[rows, :].astype(BF16), k_ref, v_ref, i * tb + r, scale)
            m = jnp.maximum(jnp.max(sc, axis=1, keepdims=True), sink)
            p = jnp.exp(sc - m)
            l = jnp.sum(p, axis=1, keepdims=True) + jnp.exp(sink - m)
            o_ref[rows, :] = jnp.dot(p.astype(BF16), v, preferred_element_type=F32) / l
            lse_ref[rows, :] = m + jnp.log(l)

    whole = lambda off: pl.BlockSpec((s, LANES), lambda h, i: (0, off + h // group))
    return pl.pallas_call(
        body, name=name, grid=(SWA_HEADS, s // tb),
        in_specs=[pl.BlockSpec(memory_space=pltpu.SMEM), pl.BlockSpec((tb, LANES), lambda h, i: (i, q_off + h)),
                  whole(k_off), whole(v_off)],
        out_specs=[pl.BlockSpec((tb, LANES), lambda h, i: (i, h)), pl.BlockSpec((None, tb, 1), lambda h, i: (h, i, 0))],
        out_shape=[jax.ShapeDtypeStruct((s, SWA_HEADS * LANES), F32), jax.ShapeDtypeStruct((SWA_HEADS, s, 1), F32)],
        compiler_params=_params(("parallel", "parallel")),
    )(hp, proj, proj, proj)


def _swa_bwd(proj, o, do, lse, hp, *, name, s, scale):
    tb = _swa_block(s)
    nqb = s // tb
    group = SWA_HEADS // SWA_KV_HEADS
    q_off, k_off, v_off = C_QSW[0] // LANES, C_KSW[0] // LANES, C_VSW[0] // LANES

    def body(hp_ref, q_ref, k_ref, v_ref, o_ref, do_ref, lse_ref, dq_ref, dk_ref, dv_ref, dsink_ref):
        kh, g, i = pl.program_id(0), pl.program_id(1), pl.program_id(2)
        h = kh * group + g
        sink = hp_ref[h, 1]

        @pl.when((g == 0) & (i == 0))
        def _():
            dk_ref[...] = jnp.zeros_like(dk_ref)
            dv_ref[...] = jnp.zeros_like(dv_ref)

        @pl.when(i == 0)
        def _():
            dsink_ref[...] = jnp.zeros_like(dsink_ref)

        for r in range(0, tb, SWA_PIECE):
            rows = pl.ds(r, SWA_PIECE)
            q = q_ref[rows, :].astype(BF16)
            dov = do_ref[rows, :]
            do_b = dov.astype(BF16)
            lse_r = lse_ref[rows, :]
            d_r = jnp.sum(dov * o_ref[rows, :], axis=1, keepdims=True)
            kstart, k, v, sc = _swa_piece(hp_ref, h, q, k_ref, v_ref, i * tb + r, scale)
            p = jnp.exp(sc - lse_r)
            dp = lax.dot_general(do_b, v, NT_DIMS, preferred_element_type=F32)
            ds = (p * (dp - d_r)).astype(BF16)
            dq_ref[rows, :] = (jnp.dot(ds, k, preferred_element_type=F32) * scale).astype(dq_ref.dtype)
            win = pl.ds(kstart, SWA_KEYS)
            dk_ref[win, :] += lax.dot_general(ds, q, TN_DIMS, preferred_element_type=F32) * scale
            dv_ref[win, :] += lax.dot_general(p.astype(BF16), do_b, TN_DIMS, preferred_element_type=F32)
            part = jnp.sum(-jnp.exp(sink - lse_r) * d_r, axis=0, keepdims=True)
            dsink_ref[...] += jnp.broadcast_to(part, (1, LANES))

    whole = lambda off: pl.BlockSpec((s, LANES), lambda kh, g, i: (0, off + kh))
    q_map = lambda kh, g, i: (i, kh * group + g)
    return pl.pallas_call(
        body, name=name, grid=(SWA_KV_HEADS, group, nqb),
        in_specs=[pl.BlockSpec(memory_space=pltpu.SMEM),
                  pl.BlockSpec((tb, LANES), lambda kh, g, i: (i, q_off + kh * group + g)), whole(k_off), whole(v_off),
                  pl.BlockSpec((tb, LANES), q_map), pl.BlockSpec((tb, LANES), q_map),
                  pl.BlockSpec((None, tb, 1), lambda kh, g, i: (kh * group + g, i, 0))],
        out_specs=[pl.BlockSpec((tb, LANES), q_map), whole(0), whole(0),
                   pl.BlockSpec((None, 1, LANES), lambda kh, g, i: (kh * group + g, 0, 0))],
        out_shape=[jax.ShapeDtypeStruct((s, SWA_HEADS * LANES), BF16),
                   jax.ShapeDtypeStruct((s, SWA_KV_HEADS * LANES), F32), jax.ShapeDtypeStruct((s, SWA_KV_HEADS * LANES), F32),
                   jax.ShapeDtypeStruct((SWA_HEADS, 1, LANES), F32)],
        compiler_params=_params(("parallel", "arbitrary", "arbitrary")),
    )(hp, proj, proj, proj, o, do, lse)


def _shift_down(z, k):
    rows = lax.broadcasted_iota(jnp.int32, z.shape, 0)
    return jnp.where(rows >= k, pltpu.roll(z, k, 0), 0.0)


def _shift_up(z, k):
    n = z.shape[0]
    rows = lax.broadcasted_iota(jnp.int32, z.shape, 0)
    return jnp.where(rows < n - k, pltpu.roll(z, n - k, 0), 0.0)


def _rows3(a, b, c):
    r = lax.broadcasted_iota(jnp.int32, (3, a.shape[1]), 0)
    return jnp.where(r == 0, a, jnp.where(r == 1, b, c))


def _col_spec(s, off):
    return pl.BlockSpec((s, LANES), functools.partial(lambda j, off: (0, off + j), off=off))


def _conv_fwd(proj, conv_w, *, name, s):
    def body(gb_ref, gc_ref, u_ref, w_ref, y_ref):
        w0, w1, w2 = w_ref[0:1, :], w_ref[1:2, :], w_ref[2:3, :]
        z = gc_ref[...] * u_ref[...]
        c = w2 * z + w1 * _shift_down(z, 1) + w0 * _shift_down(z, 2)
        y_ref[...] = gb_ref[...] * c

    return pl.pallas_call(
        body, name=name, grid=(2,),
        in_specs=[_col_spec(s, C_GB[0] // LANES), _col_spec(s, C_GC[0] // LANES), _col_spec(s, C_UCONV[0] // LANES),
                  pl.BlockSpec((3, LANES), lambda j: (0, j))],
        out_specs=_col_spec(s, 0), out_shape=jax.ShapeDtypeStruct((s, D_GROUP), F32),
        compiler_params=_params(("parallel",)),
    )(proj, proj, proj, conv_w)


def _conv_bwd(dy, proj, conv_w, *, name, s):
    def body(dy_ref, gb_ref, gc_ref, u_ref, w_ref, dgb_ref, dgc_ref, du_ref, dw_ref):
        w0, w1, w2 = w_ref[0:1, :], w_ref[1:2, :], w_ref[2:3, :]
        gc, u, dyv = gc_ref[...], u_ref[...], dy_ref[...]
        z = gc * u
        z1, z2 = _shift_down(z, 1), _shift_down(z, 2)
        c = w2 * z + w1 * z1 + w0 * z2
        dgb_ref[...] = (dyv * c).astype(dgb_ref.dtype)
        dc = dyv * gb_ref[...]
        dz = w2 * dc + w1 * _shift_up(dc, 1) + w0 * _shift_up(dc, 2)
        dgc_ref[...] = (dz * u).astype(dgc_ref.dtype)
        du_ref[...] = (dz * gc).astype(du_ref.dtype)
        dw_ref[...] = _rows3(jnp.sum(dc * z2, axis=0, keepdims=True), jnp.sum(dc * z1, axis=0, keepdims=True),
                             jnp.sum(dc * z, axis=0, keepdims=True))

    act = jax.ShapeDtypeStruct((s, D_GROUP), BF16)
    return pl.pallas_call(
        body, name=name, grid=(2,),
        in_specs=[_col_spec(s, 0), _col_spec(s, C_GB[0] // LANES), _col_spec(s, C_GC[0] // LANES),
                  _col_spec(s, C_UCONV[0] // LANES), pl.BlockSpec((3, LANES), lambda j: (0, j))],
        out_specs=[_col_spec(s, 0), _col_spec(s, 0), _col_spec(s, 0), pl.BlockSpec((3, LANES), lambda j: (0, j))],
        out_shape=[act, act, act, jax.ShapeDtypeStruct((3, D_GROUP), F32)],
        compiler_params=_params(("parallel",)),
    )(dy, proj, proj, proj, conv_w)


def _pool_select(j, lane, a2, a4, a8, a16):
    lo = lane < HEAD
    return jnp.where(j == 0, jnp.where(lo, a2, a4), jnp.where(lo, a8, a16))


def _pooled(u, j):
    s2 = u + _shift_down(u, 1)
    s4 = s2 + _shift_down(s2, 2)
    s8 = s4 + _shift_down(s4, 4)
    s16 = s8 + _shift_down(s8, 8)
    lane = lax.broadcasted_iota(jnp.int32, u.shape, 1)
    rows = lax.broadcasted_iota(jnp.int32, u.shape, 0)
    win = _pool_select(j, lane, *POOL_WINDOWS)
    count = jnp.minimum(rows + 1, win).astype(F32)
    return _pool_select(j, lane, s2, s4, s8, s16) / count - u, count


def _pool_fwd(proj, wbd, scale, *, name, s):
    def body(u_ref, w_ref, sc_ref, y_ref):
        pooled, _ = _pooled(u_ref[...], pl.program_id(0))
        y_ref[...] = jnp.dot(pooled.astype(BF16), w_ref[...].astype(BF16), preferred_element_type=F32) * sc_ref[...]

    return pl.pallas_call(
        body, name=name, grid=(2,),
        in_specs=[_col_spec(s, C_UPOOL[0] // LANES), pl.BlockSpec((None, LANES, LANES), lambda j: (j, 0, 0)),
                  pl.BlockSpec((1, LANES), lambda j: (0, j))],
        out_specs=_col_spec(s, 0), out_shape=jax.ShapeDtypeStruct((s, D_GROUP), F32),
        compiler_params=_params(("parallel",)),
    )(proj, wbd, scale)


def _pool_bwd(dy, proj, wbd, scale, *, name, s):
    def body(dy_ref, u_ref, w_ref, sc_ref, du_ref, dw_ref, dsc_ref):
        j = pl.program_id(0)
        pooled, count = _pooled(u_ref[...], j)
        pooled_b = pooled.astype(BF16)
        w_b = w_ref[...].astype(BF16)
        dyv = dy_ref[...]
        mixed = jnp.dot(pooled_b, w_b, preferred_element_type=F32)
        dsc_ref[...] = jnp.sum(dyv * mixed, axis=0, keepdims=True)
        dms = (dyv * sc_ref[...]).astype(BF16)
        dw_ref[...] = lax.dot_general(pooled_b, dms, (((0,), (0,)), ((), ())), preferred_element_type=F32)
        dpooled = lax.dot_general(dms, w_b, (((1,), (1,)), ((), ())), preferred_element_type=F32)
        r = dpooled / count
        a2 = r + _shift_up(r, 1)
        a4 = a2 + _shift_up(a2, 2)
        a8 = a4 + _shift_up(a4, 4)
        a16 = a8 + _shift_up(a8, 8)
        lane = lax.broadcasted_iota(jnp.int32, r.shape, 1)
        du_ref[...] = (_pool_select(j, lane, a2, a4, a8, a16) - dpooled).astype(du_ref.dtype)

    return pl.pallas_call(
        body, name=name, grid=(2,),
        in_specs=[_col_spec(s, 0), _col_spec(s, C_UPOOL[0] // LANES),
                  pl.BlockSpec((None, LANES, LANES), lambda j: (j, 0, 0)), pl.BlockSpec((1, LANES), lambda j: (0, j))],
        out_specs=[_col_spec(s, 0), pl.BlockSpec((None, LANES, LANES), lambda j: (j, 0, 0)),
                   pl.BlockSpec((1, LANES), lambda j: (0, j))],
        out_shape=[jax.ShapeDtypeStruct((s, D_GROUP), BF16), jax.ShapeDtypeStruct((2, LANES, LANES), F32),
                   jax.ShapeDtypeStruct((1, D_GROUP), F32)],
        compiler_params=_params(("parallel",)),
    )(dy, proj, wbd, scale)


def _mesh_pos():
    return lax.axis_index("x"), lax.axis_index("y"), lax.axis_index("c")


def _any_specs(n):
    return [pl.BlockSpec(memory_space=pl.ANY)] * n


def _all_gather(xs, *, name):
    n = len(xs)

    def body(*refs):
        x_refs, out_refs = refs[:n], refs[n:2 * n]
        send_sems, recv_sems, local_sems = refs[2 * n:]
        x, y, cc = _mesh_pos()
        me, sibling = (x, y, cc), (x, y, 1 - cc)
        chips = [(1 - x, y), (x, 1 - y), (1 - x, 1 - y)]

        def slot(a, px, py, pc):
            return out_refs[a].at[4 * px + 2 * py + pc]

        def copy(a, k, block, to, src=None):
            return pltpu.make_async_remote_copy(
                src_ref=slot(a, *block) if src is None else src, dst_ref=slot(a, *block), send_sem=send_sems.at[a, k],
                recv_sem=recv_sems.at[a, k], device_id=to, device_id_type=pl.DeviceIdType.MESH)

        mine = [pltpu.make_async_copy(x_refs[a], slot(a, *me), local_sems.at[a]) for a in range(n)]
        first = []
        for a in range(n):
            first.append(copy(a, 0, me, sibling, src=x_refs[a]))
            first += [copy(a, 1 + j, me, (*chip, cc), src=x_refs[a]) for j, chip in enumerate(chips)]
        for cp in mine + first:
            cp.start()
        passed = []
        for j, chip in enumerate(chips):
            for a in range(n):
                copy(a, 1 + j, (*chip, cc), me).wait_recv()
                passed.append(copy(a, 4 + j, (*chip, cc), sibling))
                passed[-1].start()
        for a in range(n):
            copy(a, 0, sibling, me).wait_recv()
        for j, chip in enumerate(chips):
            for a in range(n):
                copy(a, 4 + j, (*chip, 1 - cc), me).wait_recv()
        for cp in first + passed:
            cp.wait_send()
        for cp in mine:
            cp.wait()

    return pl.pallas_call(
        body, name=name, out_shape=[jax.ShapeDtypeStruct((N_DEV,) + a.shape, a.dtype) for a in xs],
        in_specs=_any_specs(n), out_specs=_any_specs(n),
        scratch_shapes=[pltpu.SemaphoreType.DMA((n, 7)), pltpu.SemaphoreType.DMA((n, 7)), pltpu.SemaphoreType.DMA((n,))],
    )(*xs)


def _plan_gather_near(src_refs, land_refs, send_sems, recv_sems):
    x, y, cc = _mesh_pos()
    me = 4 * x + 2 * y + cc
    plan = []
    for a, (src, land) in enumerate(zip(src_refs, land_refs)):
        for k, (px, py, pc) in enumerate([(x, y, 1 - cc), (1 - x, y, cc), (x, 1 - y, cc), (1 - x, 1 - y, cc)]):
            sems = dict(send_sem=send_sems.at[4 * a + k], recv_sem=recv_sems.at[4 * a + k], device_id=(px, py, pc),
                        device_id_type=pl.DeviceIdType.MESH)
            plan.append((pltpu.make_async_remote_copy(src_ref=src, dst_ref=land.at[me], **sems),
                         pltpu.make_async_remote_copy(src_ref=src, dst_ref=land.at[4 * px + 2 * py + pc], **sems)))
    return plan


def _plan_gather_pass(src_refs, land_refs, send_sems, recv_sems):
    x, y, cc = _mesh_pos()
    plan = []
    for a, land in enumerate(land_refs):
        for j, (px, py) in enumerate([(1 - x, y), (x, 1 - y), (1 - x, 1 - y)]):
            mine, theirs = land.at[4 * px + 2 * py + cc], land.at[4 * px + 2 * py + 1 - cc]
            sems = dict(send_sem=send_sems.at[3 * a + j], recv_sem=recv_sems.at[3 * a + j], device_id=(x, y, 1 - cc),
                        device_id_type=pl.DeviceIdType.MESH)
            plan.append((pltpu.make_async_remote_copy(src_ref=mine, dst_ref=mine, **sems),
                         pltpu.make_async_remote_copy(src_ref=mine, dst_ref=theirs, **sems)))
    return plan


def _plan_sibling(src_refs, land_refs, send_sems, recv_sems):
    x, y, cc = _mesh_pos()
    plan = []
    for a, (src, land) in enumerate(zip(src_refs, land_refs)):
        cp = pltpu.make_async_remote_copy(
            src_ref=src.at[1 - cc], dst_ref=land, send_sem=send_sems.at[a], recv_sem=recv_sems.at[a],
            device_id=(x, y, 1 - cc), device_id_type=pl.DeviceIdType.MESH)
        plan.append((cp, cp))
    return plan


def _plan_chips(src_refs, land_refs, send_sems, recv_sems):
    x, y, cc = _mesh_pos()
    my_chip = 2 * x + y
    plan = []
    for a, (src, land) in enumerate(zip(src_refs, land_refs)):
        for j, (px, py) in enumerate([(1 - x, y), (x, 1 - y), (1 - x, 1 - y)]):
            peer = 2 * px + py
            sems = dict(send_sem=send_sems.at[3 * a + j], recv_sem=recv_sems.at[3 * a + j], device_id=(px, py, cc),
                        device_id_type=pl.DeviceIdType.MESH)
            plan.append((pltpu.make_async_remote_copy(src_ref=src.at[peer], dst_ref=land.at[my_chip], **sems),
                         pltpu.make_async_remote_copy(src_ref=src.at[peer], dst_ref=land.at[peer], **sems)))
    return plan


HBM_SPEC = pl.BlockSpec(memory_space=pltpu.HBM)
SEM_SPEC = pl.BlockSpec(memory_space=pltpu.SEMAPHORE)
ANY_SPEC = pl.BlockSpec(memory_space=pl.ANY)
SIDE_EFFECT = pltpu.CompilerParams(has_side_effects=pltpu.SideEffectType.DATAFLOW_SIDE_EFFECTING)


def _start_copies(plan, sems_per_array, srcs, lands, after, *, name):
    lands = [lax.empty(l, a.dtype) if isinstance(l, tuple) else l for l, a in zip(lands, srcs or lands)]
    ns, n = len(srcs), len(srcs) + len(lands)

    def body(*refs):
        send_sems, recv_sems = refs[n + len(after)], refs[n + len(after) + 1]
        for out, _ in plan(refs[:ns], refs[ns:n], send_sems, recv_sems):
            out.start()
        refs[-1][...] = jnp.zeros_like(refs[-1])

    sem = pltpu.SemaphoreType.DMA((len(lands) * sems_per_array,))
    res = pl.pallas_call(
        body, name=name,
        out_shape=(sem, sem, *[pltpu.HBM(a.shape, a.dtype) for a in srcs + lands], jax.ShapeDtypeStruct((8, LANES), F32)),
        in_specs=[HBM_SPEC] * n + [ANY_SPEC] * len(after),
        out_specs=(SEM_SPEC, SEM_SPEC, *[HBM_SPEC] * n, pl.BlockSpec(memory_space=pltpu.VMEM)),
        input_output_aliases={i: 2 + i for i in range(n)}, compiler_params=SIDE_EFFECT,
    )(*[pltpu.with_memory_space_constraint(a, pltpu.HBM) for a in srcs + lands], *after)
    return (res[0], res[1], list(res[2:2 + ns]), list(res[2 + ns:2 + n])), res[-1]


def _wait_copies(plan, handle, after, *, name):
    send, recv, srcs, lands = handle
    ns, n = len(srcs), len(srcs) + len(lands)

    def body(*refs):
        for out, inc in plan(refs[:ns], refs[ns:n], refs[n], refs[n + 1]):
            out.wait_send()
            inc.wait_recv()

    res = pl.pallas_call(
        body, name=name, out_shape=tuple(pltpu.HBM(a.shape, a.dtype) for a in srcs + lands),
        in_specs=[HBM_SPEC] * n + [SEM_SPEC, SEM_SPEC] + [ANY_SPEC] * len(after), out_specs=[HBM_SPEC] * n,
        input_output_aliases={i: i for i in range(n)}, compiler_params=SIDE_EFFECT,
    )(*srcs, *lands, send, recv, *after)
    return list(res[:ns]), list(res[ns:])


def _row_tile(rows, target=512):
    if rows <= target:
        return rows
    best = None
    for t in range(8, target + 1, 8):
        if rows % t == 0:
            best = t
    assert best is not None, (rows, target)
    return best


def _add_own(g, other, core, *, name):
    _, _, rows, cols = g.shape
    tm = _row_tile(rows)

    def body(c_ref, g_ref, o_ref, out_ref):
        out_ref[...] = g_ref[...] + o_ref[...]

    return pl.pallas_call(
        body, name=name, out_shape=jax.ShapeDtypeStruct(other.shape, other.dtype),
        grid_spec=pltpu.PrefetchScalarGridSpec(
            num_scalar_prefetch=1, grid=(4, rows // tm),
            in_specs=[pl.BlockSpec((None, None, tm, cols), lambda p, i, c_ref: (c_ref[0], p, i, 0)),
                      pl.BlockSpec((None, tm, cols), lambda p, i, c_ref: (p, i, 0))],
            out_specs=pl.BlockSpec((None, tm, cols), lambda p, i, c_ref: (p, i, 0))),
        compiler_params=_params(("parallel", "parallel")),
    )(core, g, other)


def _adamw(parts, w, m, v, after, *, name):
    layers, rows, cols = w.shape
    assert len(parts) == layers
    tm = _row_tile(rows, 256)
    nr = rows // tm

    def body(*refs):
        p_refs = refs[:layers]
        w_ref, m_ref, v_ref, _, g_ref, d_ref, nm_ref, nv_ref, g_s = refs[layers:]
        for ll in range(layers):
            @pl.when(pl.program_id(0) == ll)
            def _(ll=ll):
                g = p_refs[ll][0]
                for q in range(1, p_refs[ll].shape[0]):
                    g = g + p_refs[ll][q]
                g_s[...] = g

        g = g_s[...]
        mm = ADAM_B1 * m_ref[...] + (1.0 - ADAM_B1) * g
        vv = ADAM_B2 * v_ref[...] + (1.0 - ADAM_B2) * jnp.square(g)
        m_hat = mm / (1.0 - ADAM_B1 ** ADAM_STEP)
        v_hat = vv / (1.0 - ADAM_B2 ** ADAM_STEP)
        g_ref[...] = g
        d_ref[...] = -ADAM_LR * (m_hat / (jnp.sqrt(v_hat) + ADAM_EPS) + ADAM_WD * w_ref[...])
        nm_ref[...] = mm
        nv_ref[...] = vv

    def part_spec(ll, p):
        return pl.BlockSpec((p, tm, cols), lambda l, i: (0, jnp.where(l == ll, i, jnp.where(l < ll, 0, nr - 1)), 0))

    spec = pl.BlockSpec((None, tm, cols), lambda l, i: (l, i, 0))
    out = jax.ShapeDtypeStruct(w.shape, F32)
    return pl.pallas_call(
        body, name=name, grid=(layers, nr),
        in_specs=[part_spec(ll, parts[ll].shape[0]) for ll in range(layers)] + [spec] * 3 + [pl.BlockSpec(memory_space=pl.ANY)],
        out_specs=[spec] * 4, out_shape=[out] * 4, scratch_shapes=[pltpu.VMEM((tm, cols), F32)],
        compiler_params=_params(("arbitrary", "arbitrary")),
    )(*parts, w, m, v, after)


def _pack(arrs):
    flat = jnp.concatenate([a.reshape(-1) for a in arrs])
    rows = -(-flat.shape[0] // (PACK_COLS * 16)) * 16
    return jnp.pad(flat, (0, rows * PACK_COLS - flat.shape[0])).reshape(rows, PACK_COLS)


def _unpack(packed, shapes):
    flat = packed.reshape(-1)
    out, off = [], 0
    for shp in shapes:
        n = int(np.prod(shp))
        out.append(flat[off:off + n].reshape(shp))
        off += n
    return out


def _shards_to_full(g, axis):
    if axis == 0:
        return g.reshape(g.shape[0] * g.shape[1], g.shape[2])
    return jnp.transpose(g, (1, 0, 2)).reshape(g.shape[1], g.shape[0] * g.shape[2])


def _full_to_shards(a, axis):
    if axis == 0:
        return jnp.transpose(a.reshape(4, 2, a.shape[0] // N_DEV, a.shape[1]), (1, 0, 2, 3))
    return jnp.transpose(a.reshape(a.shape[0], 4, 2, a.shape[1] // N_DEV), (2, 1, 0, 3))


def _zeros_like_cols(a, n):
    return jnp.zeros(a.shape[:-1] + (n,), a.dtype)


def _pad_heads(a, n):
    z = _zeros_like_cols(a, HEAD)
    return jnp.concatenate([p for h in range(n) for p in (a[..., h * HEAD:(h + 1) * HEAD], z)], axis=-1)


def _unpad_heads(a, n):
    return jnp.concatenate([a[..., h * LANES:h * LANES + HEAD] for h in range(n)], axis=-1)


def _seg(first, width, sign=1):
    return (width, [(first, sign)])


def _zero(width):
    return (width, [])


def _swapped(first):
    half = MLA_ROPE // 2
    return [_seg(first + half, half, -1), _seg(first, half)]


def _padded_heads(first, n):
    return [s for h in range(n) for s in (_seg(first + HEAD * h, HEAD), _zero(HEAD))]


def _layout_w_in():
    kr = 384
    return (_padded_heads(1440, 4) + [_seg(0, 256), _seg(416, 256), _seg(672, 256), _seg(928, 256), _seg(1184, 256)]
            + _padded_heads(1696, 2) + _padded_heads(1824, 2) + [_seg(256, 128)]
            + [_zero(HEAD), _seg(kr, MLA_ROPE), _seg(kr, MLA_ROPE)] + [_zero(HEAD)] + _swapped(kr) + _swapped(kr))


def _layout_w_uq():
    out = []
    for h in range(MLA_HEADS):
        out += [_seg(96 * h, MLA_NOPE), _seg(96 * h + MLA_NOPE, MLA_ROPE)] + _swapped(96 * h + MLA_NOPE)
    return out


def _layout_w_ukv():
    keys = [s for h in range(MLA_HEADS) for s in (_seg(LANES * h, HEAD), _zero(HEAD))]
    values = [s for h in range(MLA_HEADS) for s in (_seg(LANES * h + HEAD, HEAD), _zero(HEAD))]
    return keys + values


def _layout_w_gate_up():
    return [_seg(half + j, GU_TILE) for j in range(0, D_FF, GU_TILE) for half in (0, D_FF)]


LAYOUTS = dict(w_in=_layout_w_in(), w_uq=_layout_w_uq(), w_ukv=_layout_w_ukv(), w_gate_up=_layout_w_gate_up())
SWAPPED_UPDATE = ("w_gate_up",)
OWN_COLS = dict(w_in=1952, w_uq=384, w_ukv=512, w_gate_up=2 * D_FF)


def _plan_extend(layout, shard):
    plan = []
    for width, terms in layout:
        if not terms:
            plan.append((width, []))
            continue
        (first, sign), = terms
        while width:
            g, off = divmod(first, shard)
            w = min(width, shard - off)
            plan.append((w, [(g, off, sign)]))
            first, width = first + w, width - w
    return [plan]


def _plan_fold(layout, own_cols):
    sources = [[] for _ in range(own_cols)]
    e = 0
    for width, terms in layout:
        for first, sign in terms:
            for i in range(width):
                sources[first + i].append((e + i, sign))
        e += width
    shard = own_cols // N_DEV
    plans = {}
    for g in range(N_DEV):
        plan, n = [], g * shard
        while n < (g + 1) * shard:
            w = 1
            while n + w < (g + 1) * shard and [(c + w, sg) for c, sg in sources[n]] == sources[n + w]:
                w += 1
            plan.append((w, [(0, c, sg) for c, sg in sources[n]]))
            n += w
        plans[g] = plan
    return [plans[2 * p + c] for c in range(2) for p in range(4)]


def _assemble(src, plans, out_cols, out_dtype, *, name, transposed=False):
    g, rows, c = src.shape
    tm = _row_tile(rows, 256)
    pad = -out_cols % LANES if transposed else 0

    def body(s_ref, o_ref):
        blocks = [s_ref[i].astype(F32) for i in range(g)]
        for d, plan in enumerate(plans):
            pieces = []
            for width, terms in plan + ([(pad, [])] if pad else []):
                v = None
                for b, first, sign in terms:
                    t = blocks[b][:, first:first + width]
                    t = -t if sign < 0 else t
                    v = t if v is None else v + t
                pieces.append(jnp.zeros((tm, width), F32) if v is None else v)
            block = pieces[0] if len(pieces) == 1 else jnp.concatenate(pieces, axis=1)
            o_ref[d] = (block.T[:out_cols, :] if transposed else block).astype(o_ref.dtype)

    block, shape = ((out_cols, tm), (out_cols, rows)) if transposed else ((tm, out_cols), (rows, out_cols))
    return pl.pallas_call(
        body, name=name, grid=(rows // tm,), in_specs=[pl.BlockSpec((g, tm, c), lambda i: (0, i, 0))],
        out_specs=pl.BlockSpec((len(plans),) + block, (lambda i: (0, 0, i)) if transposed else (lambda i: (0, i, 0))),
        out_shape=jax.ShapeDtypeStruct((len(plans),) + shape, out_dtype), compiler_params=_params(("parallel",)),
    )(src)


def _extend(nm, gathered, *, name):
    layout = LAYOUTS[nm]
    return _assemble(gathered, _plan_extend(layout, OWN_COLS[nm] // N_DEV), sum(w for w, _ in layout), BF16, name=name)[0]


def _fold_to_shards(nm, grad_ext, *, name):
    shards = _assemble(grad_ext[None], _plan_fold(LAYOUTS[nm], OWN_COLS[nm]), OWN_COLS[nm] // N_DEV, F32, name=name,
                       transposed=nm in SWAPPED_UPDATE)
    return shards.reshape((2, 4) + shards.shape[1:])


def _rope_tables(s):
    inv = 1.0 / (ROPE_THETA ** (jnp.arange(0, MLA_ROPE, 2, dtype=F32) / MLA_ROPE))
    ang = jnp.arange(s, dtype=F32)[:, None] * inv[None, :]
    cos, sin = jnp.cos(ang), jnp.sin(ang)
    c32, s32 = jnp.concatenate([cos, cos], axis=1), jnp.concatenate([sin, sin], axis=1)
    zeros, ones = jnp.zeros((s, HEAD), F32), jnp.ones((s, HEAD), F32)
    tq = jnp.concatenate([ones, c32, s32], axis=1) * (1.0 / math.sqrt(MLA_NOPE + MLA_ROPE))
    return (jnp.tile(tq, (1, MLA_HEADS)), jnp.concatenate([zeros, c32, c32], axis=1),
            jnp.concatenate([zeros, s32, s32], axis=1))


def _gn(y):
    return y * _rstd(y, D_GROUP)


def _mixer_fwd(x, w, tabs, l, after_attention=None):
    s = x.shape[0]
    tq, tkc, tks = tabs
    n = lambda t: f"l{l}_{t}"
    h = _rms_fwd(("row", x, D_MODEL, 0), w["attn_norm"], name=n("attn_norm"), rows=s, width=D_MODEL)
    proj = _mm(h, w["w_in"], name=n("in_proj"))
    def prep(cq, ckv, kr, krp, gq, gkv, wuq, wukv, tqv, tc, ts):
        cqn = (cq * _rstd(cq, 256) * gq).astype(BF16)
        ckvn = (ckv * _rstd(ckv, 128) * gkv).astype(BF16)
        qe = jnp.dot(cqn, wuq, preferred_element_type=F32)
        kve = jnp.dot(ckvn, wukv, preferred_element_type=F32)
        kb = kr * tc + krp * ts
        kvv = kve[:, 512:]
        lane = lax.broadcasted_iota(jnp.int32, kvv.shape, 1) & (LANES - 1)
        v = jnp.where(lane == HEAD, 1.0, kvv)
        k = kve[:, :512] + jnp.tile(kb, (1, MLA_HEADS))
        return cqn, ckvn, qe * tqv, k, k, v, v

    cqn, ckvn, qm, km, kmt, vm, vmt = _rowwise(
        prep, name=n("mla_prep"), rows=s,
        ins=[("row", proj, 256, C_CQ[0] // 256), ("row", proj, 128, C_CKV[0] // 128),
             ("row", proj, 128, C_KR[0] // 128), ("row", proj, 128, C_KRP[0] // 128),
             ("full", w["mla_q_norm"]), ("full", w["mla_kv_norm"]), ("full", w["w_uq"]), ("full", w["w_ukv"]),
             ("row", tq, 512, 0), ("row", tkc, 128, 0), ("row", tks, 128, 0)],
        outs=[(256, BF16), (128, BF16), (512, BF16, MLA_HEADS), (512, BF16, MLA_HEADS), (512, BF16, MLA_HEADS, "T"),
              (512, BF16, MLA_HEADS), (512, BF16, MLA_HEADS, "T")])
    y_a, lse_row = _mla_fwd(qm, km, vmt, name=n("mla_fwd"), s=s, nh=MLA_HEADS)
    mix_norm = w["mix_norm"]
    if after_attention is not None:
        mix_norm = mix_norm + after_attention(y_a)[0, 0]
    y_b = _conv_fwd(proj, w["conv_w"], name=n("conv_fwd"), s=s)
    y_c = _pool_fwd(proj, w["pool_wbd"], w["pool_scale"], name=n("pool_fwd"), s=s)
    y_d, lse_d = _swa_fwd(proj, w["hp_swa"], name=n("swa_fwd"), s=s, scale=1.0 / math.sqrt(HEAD))

    def mix(ya, yb, yc, yd, mn):
        return (jnp.concatenate([_gn(_unpad_heads(ya, 4)), _gn(yb), _gn(yc), _gn(_unpad_heads(yd, 4))], axis=1) * mn,)

    mixed = _rowwise(mix, name=n("group_norm"), rows=s,
                     ins=[("heads", y_a), ("row", y_b, 256, 0), ("row", y_c, 256, 0), ("row", y_d, 512, 0),
                          ("full", mix_norm)], outs=[(D_MODEL, BF16)])[0]
    x1 = _mm(mixed, w["w_o"], res=x, name=n("out_proj"))
    saved = dict(x=x, h=h, proj=proj, cqn=cqn, ckvn=ckvn, qm=qm, km=km, kmt=kmt, vm=vm, y_a=y_a, lse_row=lse_row,
                 y_b=y_b, y_c=y_c, y_d=y_d, lse_d=lse_d, mixed=mixed)
    return x1, saved


def _ffn_fwd(x1, w, l):
    s = x1.shape[0]
    n = lambda t: f"l{l}_{t}"
    h2 = _rms_fwd(("row", x1, D_MODEL, 0), w["ffn_norm"], name=n("ffn_norm"), rows=s, width=D_MODEL)

    def swiglu(gu):
        g, u = gu[:, :GU_TILE], gu[:, GU_TILE:]
        return gu, g * jax.nn.sigmoid(g) * u

    gu, act = _mm(h2, w["w_gate_up"], tm=2048, tn=2 * GU_TILE, name=n("gate_up"),
                  epilogue=(swiglu, [], [(2 * D_FF, BF16), (D_FF, BF16)]))
    x2 = _mm(act, w["w_down"], res=x1, tk=D_FF // 2, name=n("down"))
    return x2, dict(x1=x1, h2=h2, gu=gu, act=act)


def _ffn_bwd_down(dx2, sv, w, l):
    n = lambda t: f"l{l}_{t}"

    def swiglu_bwd(da, gu):
        gt, u = gu[:, :GU_TILE].astype(F32), gu[:, GU_TILE:].astype(F32)
        sg = jax.nn.sigmoid(gt)
        return (jnp.concatenate([da * u * sg * (1.0 + gt * (1.0 - sg)), da * gt * sg], axis=1),)

    dgu = _mm(dx2[1], w["w_down"], tb=True, tm=2048, tn=GU_TILE, name=n("d_act"),
              epilogue=(swiglu_bwd, [sv["gu"]], [(2 * D_FF, BF16)]))[0]
    g = dict(w_down=_mm(sv["act"], dx2[1], ta=True, tm=D_FF // 2, name=n("dw_down")))
    return dgu, g


def _ffn_bwd_up(dx2, dgu, sv, w, l):
    s = dgu.shape[0]
    n = lambda t: f"l{l}_{t}"
    dh2 = _mm_nt_deep(dgu, w["w_gate_up"], name=n("d_h2"))
    g = dict(w_gate_up=_mm(sv["h2"], dgu, ta=True, tn=D_FF // 2, name=n("dw_gate_up")))
    dx1, dx1_b, g["ffn_norm"] = _rms_bwd(("row", sv["x1"], D_MODEL, 0), w["ffn_norm"], dh2, dx2[0],
                                         name=n("ffn_norm_bwd"), rows=s, width=D_MODEL, out_dtypes=(F32, BF16))
    return (dx1, dx1_b), g


def _mixer_bwd_out(dx1, sv, w, l):
    s = dx1[1].shape[0]
    n = lambda t: f"l{l}_{t}"
    dmixed = _mm(dx1[1], w["w_o"], tb=True, name=n("d_mixed"))
    g = dict(w_o=_mm(sv["mixed"], dx1[1], ta=True, name=n("dw_o")))

    def mix_bwd(dm, ya, yb, yc, yd, mn):
        outs, dmn = [], []
        for i, y in enumerate((_unpad_heads(ya, 4), yb, yc, _unpad_heads(yd, 4))):
            lo, hi = i * D_GROUP, (i + 1) * D_GROUP
            r = _rstd(y, D_GROUP)
            nrm = y * r
            dmg = dm[:, lo:hi]
            dn = dmg * mn[:, lo:hi]
            dy = r * (dn - nrm * (jnp.sum(dn * nrm, axis=-1, keepdims=True) * (1.0 / D_GROUP)))
            outs.append(_pad_heads(dy, 4) if i in (0, 3) else dy)
            dmn.append(jnp.sum(dmg * nrm, axis=0, keepdims=True))
        return (*outs, jnp.concatenate(dmn, axis=1))

    dy_a, dy_b, dy_c, dy_d, g["mix_norm"] = _rowwise(
        mix_bwd, name=n("group_norm_bwd"), rows=s,
        ins=[("row", dmixed, D_MODEL, 0), ("heads", sv["y_a"]), ("row", sv["y_b"], 256, 0),
             ("row", sv["y_c"], 256, 0), ("row", sv["y_d"], 512, 0), ("full", w["mix_norm"])],
        outs=[(512, F32, MLA_HEADS), (256, F32), (256, F32), (512, F32)], reds=[(1, D_MODEL)])
    return (dy_a, dy_b, dy_c, dy_d), g


def _mixer_bwd_in(dx1, dys, sv, w, tabs, l):
    s = dx1[0].shape[0]
    tq, tkc, tks = tabs
    n = lambda t: f"l{l}_{t}"
    dy_a, dy_b, dy_c, dy_d = dys
    g = {}

    proj = sv["proj"]
    dq_sw, dk_sw, dv_sw, dsink = _swa_bwd(proj, sv["y_d"], dy_d, sv["lse_d"], w["hp_swa"], name=n("swa_bwd"), s=s,
                                          scale=1.0 / math.sqrt(HEAD))
    g["swa_sinks"] = dsink[:, 0, 0]

    dqm, dkm, dvm = _mla_bwd(sv["qm"], sv["km"], sv["kmt"], sv["vm"], sv["y_a"], dy_a, sv["lse_row"], w["hp_swa"],
                             name=n("mla_bwd"), s=s, nh=MLA_HEADS)

    def rms_bwd(x, gv, dy, width):
        r = _rstd(x, width)
        dyg = dy * gv
        dx = r * dyg - x * (r * r * r) * (jnp.sum(dyg * x, axis=-1, keepdims=True) * (1.0 / width))
        return dx, jnp.sum(dy * x * r, axis=0, keepdims=True)

    def prep_bwd(dq, dk, dv, cq, ckv, gq, gkv, wuq, wukv, tqv, tc, ts):
        dkb = dk[:, 0:128] + dk[:, 128:256] + dk[:, 256:384] + dk[:, 384:512]
        dq_ext = (dq * tqv).astype(BF16)
        dkv_ext = jnp.concatenate([dk.astype(BF16), dv], axis=1)
        dcqn = lax.dot_general(dq_ext, wuq, NT_DIMS, preferred_element_type=F32)
        dckvn = lax.dot_general(dkv_ext, wukv, NT_DIMS, preferred_element_type=F32)
        dcq, dgq = rms_bwd(cq, gq, dcqn, 256)
        dckv, dgkv = rms_bwd(ckv, gkv, dckvn, 128)
        return dq_ext, dkv_ext, dkb * tc, dkb * ts, dcq, dckv, dgq, dgkv

    dq_ext, dkv_ext, dkr, dkrp, dcq, dckv, g["mla_q_norm"], g["mla_kv_norm"] = _rowwise(
        prep_bwd, name=n("mla_prep_bwd"), rows=s,
        ins=[("heads", dqm), ("heads", dkm), ("heads", dvm), ("row", proj, 256, C_CQ[0] // 256),
             ("row", proj, 128, C_CKV[0] // 128), ("full", w["mla_q_norm"]), ("full", w["mla_kv_norm"]),
             ("full", w["w_uq"]), ("full", w["w_ukv"]), ("row", tq, 512, 0), ("row", tkc, 128, 0), ("row", tks, 128, 0)],
        outs=[(512, BF16), (1024, BF16), (128, BF16), (128, BF16), (256, BF16), (128, BF16)],
        reds=[(1, 256), (1, 128)])
    g["w_uq"] = _mm(sv["cqn"], dq_ext, ta=True, name=n("dw_uq"))
    g["w_ukv"] = _mm(sv["ckvn"], dkv_ext, ta=True, name=n("dw_ukv"))

    dgb, dgc, duc, g["conv_w"] = _conv_bwd(dy_b, proj, w["conv_w"], name=n("conv_bwd"), s=s)
    dup, g["pool_wbd"], g["pool_scale"] = _pool_bwd(dy_c, proj, w["pool_wbd"], w["pool_scale"], name=n("pool_bwd"), s=s)

    dproj = jnp.concatenate([dq_sw, dcq, dgb, dgc, duc, dup, dk_sw.astype(BF16), dv_sw.astype(BF16), dckv, dkr, dkrp],
                            axis=1)
    g["w_in"] = _mm(sv["h"], dproj, ta=True, name=n("dw_in"))
    return dproj, g


def _mixer_bwd_norm(dx1, dproj, sv, w, l):
    n = lambda t: f"l{l}_{t}"
    dh = _mm(dproj, w["w_in"], tb=True, name=n("d_h"))
    dx0, dx0_b, dg = _rms_bwd(("row", sv["x"], D_MODEL, 0), w["attn_norm"], dh, dx1[0], name=n("attn_norm_bwd"),
                              rows=dh.shape[0], width=D_MODEL, out_dtypes=(F32, BF16))
    return (dx0, dx0_b), dict(attn_norm=dg)


def _loss_head(x, target, g, *, s):
    def fn(xv, tv, gv):
        r = _rstd(xv, D_MODEL)
        e = xv * r * gv - tv
        part = jnp.sum(jnp.sum(e * e, axis=1, keepdims=True), axis=0, keepdims=True) * (0.5 / D_MODEL)
        dy = e * (1.0 / D_MODEL)
        dyg = dy * gv
        dx = r * dyg - xv * (r * r * r) * (jnp.sum(dyg * xv, axis=-1, keepdims=True) * (1.0 / D_MODEL))
        return dx, dx, jnp.sum(dy * xv * r, axis=0, keepdims=True), jnp.broadcast_to(part, (1, LANES))

    return _rowwise(fn, name="loss_head", rows=s,
                    ins=[("row", x, D_MODEL, 0), ("row", target, D_MODEL, 0), ("full", g)],
                    outs=[(D_MODEL, F32), (D_MODEL, BF16)], reds=[(1, D_MODEL), (1, LANES)])


def _alibi_slopes(n):
    return np.asarray([2.0 ** (-8.0 * (i + 1) / n) for i in range(n)], dtype=np.float32)


MIXER_WEIGHTS = ("w_in", "w_uq", "w_ukv", "conv_w", "w_o")
FFN_WEIGHTS = ("w_gate_up", "w_down")


def _mixer_weights(full, rep, l):
    pw = rep["pool_w"][l]
    z = jnp.zeros((HEAD, HEAD), F32)
    wbd = jnp.stack([jnp.block([[pw[2 * j], z], [z, pw[2 * j + 1]]]) for j in range(2)])
    return dict(
        attn_norm=rep["attn_norm"][l][None], w_in=full["w_in"], mla_q_norm=rep["mla_q_norm"][l][None],
        w_uq=full["w_uq"], mla_kv_norm=rep["mla_kv_norm"][l][None], w_ukv=full["w_ukv"],
        conv_w=full["conv_w"], pool_wbd=wbd, pool_scale=rep["pool_scale"][l][None],
        mix_norm=rep["mix_norm"][l][None], w_o=full["w_o"],
        hp_swa=jnp.stack([jnp.asarray(_alibi_slopes(SWA_HEADS)), rep["swa_sinks"][l]], axis=1))


def _ffn_weights(full, rep, l):
    return dict(ffn_norm=rep["ffn_norm"][l][None], w_gate_up=full["w_gate_up"], w_down=full["w_down"])


def _small_grads(g):
    rows = ("attn_norm", "mla_q_norm", "mla_kv_norm", "pool_scale", "ffn_norm", "mix_norm")
    out = {nm: g[nm][0] for nm in rows if nm in g}
    if "swa_sinks" in g:
        out["swa_sinks"] = g["swa_sinks"]
    if "pool_wbd" in g:
        e = g["pool_wbd"]
        out["pool_w"] = jnp.stack([e[j // 2][HEAD * (j % 2):HEAD * (j % 2 + 1), HEAD * (j % 2):HEAD * (j % 2 + 1)]
                                   for j in range(4)])
    return out


def kernel(x, attn_norm, w_in, mla_q_norm, w_uq, mla_kv_norm, w_ukv, conv_w, pool_w, pool_scale, swa_sinks, mix_norm, w_o, ffn_norm, w_gate_up, w_down, final_norm, loss_target, m_attn_norm, m_w_in, m_mla_q_norm, m_w_uq, m_mla_kv_norm, m_w_ukv, m_conv_w, m_pool_w, m_pool_scale, m_swa_sinks, m_mix_norm, m_w_o, m_ffn_norm, m_w_gate_up, m_w_down, m_final_norm, v_attn_norm, v_w_in, v_mla_q_norm, v_w_uq, v_mla_kv_norm, v_w_ukv, v_conv_w, v_pool_w, v_pool_scale, v_swa_sinks, v_mix_norm, v_w_o, v_ffn_norm, v_w_gate_up, v_w_down, v_final_norm):
    given = dict(locals())
    sh_names = [nm for nm, _, _ in SHARDED]
    sh_axis = {nm: ax - 1 for nm, _, ax in SHARDED}
    rep_names = [nm for nm, _ in REPLICATED]
    rep_shapes = [shp for _, shp in REPLICATED]
    rep = {nm: given[nm] for nm in rep_names if nm != "loss"}
    me = 4 * lax.axis_index("x") + 2 * lax.axis_index("y") + lax.axis_index("c")
    my_chip = 2 * lax.axis_index("x") + lax.axis_index("y")
    core = lax.axis_index("c").astype(jnp.int32).reshape(1)

    def behind(token, a):
        return a + token[0, 0].astype(a.dtype)

    def wire(nm, l):
        if nm == "conv_w":
            return lax.bitcast_convert_type(given[nm][l], BF16).reshape(3, -1)
        return given[nm][l].astype(BF16)

    def whole(nm, g, tag):
        if nm in LAYOUTS:
            return _extend(nm, g, name=f"extend_{nm}_{tag}")
        if nm == "conv_w":
            g = lax.bitcast_convert_type(g.reshape(N_DEV, 3, -1, 2), F32)
        return _shards_to_full(g, sh_axis[nm])

    def near_start(names, l, after, tag):
        srcs = [wire(nm, l) for nm in names]
        return _start_copies(_plan_gather_near, 4, srcs, [(N_DEV,) + a.shape for a in srcs], after, name=f"start_gather_{tag}")

    def pass_on(handle, after, tag):
        srcs, lands = _wait_copies(_plan_gather_near, handle, after, name=f"wait_gather_{tag}")
        handle, token = _start_copies(_plan_gather_pass, 3, [], lands, [], name=f"start_pass_{tag}")
        return (srcs, handle), token

    def gathered(names, state, after, tag):
        srcs, handle = state
        _, lands = _wait_copies(_plan_gather_pass, handle, after, name=f"wait_pass_{tag}")
        return {nm: whole(nm, lax.dynamic_update_index_in_dim(land, src, me, 0), tag)
                for nm, src, land in zip(names, srcs, lands)}

    layer1 = MIXER_WEIGHTS + FFN_WEIGHTS
    got = _all_gather([wire(nm, 0) for nm in MIXER_WEIGHTS], name="gather_mixer0")
    full_m0 = {nm: whole(nm, g, "mixer0") for nm, g in zip(MIXER_WEIGHTS, got)}
    h_f0, tok = near_start(FFN_WEIGHTS, 0, [], "ffn0")
    h_l1, tok = near_start(layer1, 1, [tok], "layer1")

    xs, target = x[0], loss_target[0]
    s = xs.shape[0]
    tabs = _rope_tables(s)
    wm, wf, svm, svf = [None] * DEPTH, [None] * DEPTH, [None] * DEPTH, [None] * DEPTH
    wm[0] = _mixer_weights(full_m0, rep, 0)
    wm[0]["attn_norm"] = behind(tok, wm[0]["attn_norm"])
    passing = {}

    def pass_ffn0(y_a):
        passing["ffn0"], token = pass_on(h_f0, [y_a], "ffn0")
        return token

    x1, svm[0] = _mixer_fwd(xs, wm[0], tabs, 0, after_attention=pass_ffn0)
    wf[0] = _ffn_weights(gathered(FFN_WEIGHTS, passing["ffn0"], [x1], "ffn0"), rep, 0)
    passing["layer1"], tok = pass_on(h_l1, [x1], "layer1")
    wf[0]["ffn_norm"] = behind(tok, wf[0]["ffn_norm"])
    x2, svf[0] = _ffn_fwd(x1, wf[0], 0)
    full_1 = gathered(layer1, passing["layer1"], [x2], "layer1")
    wm[1], wf[1] = _mixer_weights(full_1, rep, 1), _ffn_weights(full_1, rep, 1)
    x1, svm[1] = _mixer_fwd(x2, wm[1], tabs, 1)
    x2, svf[1] = _ffn_fwd(x1, wf[1], 1)
    dx_f, dx_b, d_final, loss = _loss_head(x2, target, rep["final_norm"][None], s=s)
    dx = (dx_f, dx_b)

    parts = {}

    def reduce_start(grads, l, after, tag):
        names = [nm for nm in sh_names if nm in grads]
        mine = [_fold_to_shards(nm, grads[nm], name=f"fold_{nm}_{l}") if nm in LAYOUTS
                else _full_to_shards(grads[nm], sh_axis[nm]) for nm in names]
        handle, token = _start_copies(_plan_sibling, 1, mine, [m.shape[1:] for m in mine], after, name=f"start_sibling_{tag}")
        return (names, l, handle), token

    def reduce_mid(state, after, tag):
        names, l, handle = state
        mine, theirs = _wait_copies(_plan_sibling, handle, after, name=f"wait_sibling_{tag}")
        sums = [_add_own(g, o, core, name=f"chip_sum_{nm}_{l}") for nm, g, o in zip(names, mine, theirs)]
        handle, token = _start_copies(_plan_chips, 3, sums, [a.shape for a in sums], [], name=f"start_chips_{tag}")
        return (names, l, handle), token

    def reduce_end(state, after, tag):
        names, l, handle = state
        sums, lands = _wait_copies(_plan_chips, handle, after, name=f"wait_chips_{tag}")
        for nm, own, land in zip(names, sums, lands):
            parts[nm, l] = lax.dynamic_update_index_in_dim(land, lax.dynamic_index_in_dim(own, my_chip, 0, keepdims=False),
                                                           my_chip, 0)

    small = [None] * DEPTH
    in_flight = []
    pending = None
    for l in reversed(range(DEPTH)):
        dgu, g_down = _ffn_bwd_down(dx, svf[l], wf[l], l)
        if pending is not None:
            state, token = reduce_mid(pending, [dgu], f"mixer{l + 1}")
            in_flight.append((state, f"mixer{l + 1}"))
            wf[l]["ffn_norm"] = behind(token, wf[l]["ffn_norm"])
        dx1, g_up = _ffn_bwd_up(dx, dgu, svf[l], wf[l], l)
        g_ffn = {**g_down, **g_up}
        state, token = reduce_start(g_ffn, l, [], f"ffn{l}")
        wm[l]["mix_norm"] = behind(token, wm[l]["mix_norm"])
        dys, g_out = _mixer_bwd_out(dx1, svm[l], wm[l], l)
        state, token = reduce_mid(state, [dys[0]], f"ffn{l}")
        in_flight.append((state, f"ffn{l}"))
        wm[l]["hp_swa"] = behind(token, wm[l]["hp_swa"])
        dproj, g_in = _mixer_bwd_in(dx1, dys, svm[l], wm[l], tabs, l)
        g_mixer = {**g_out, **g_in}
        pending, token = reduce_start(g_mixer, l, [], f"mixer{l}")
        wm[l]["attn_norm"] = behind(token, wm[l]["attn_norm"])
        dx, g_norm = _mixer_bwd_norm(dx1, dproj, svm[l], wm[l], l)
        small[l] = _small_grads({**g_ffn, **g_mixer, **g_norm})
    grads = {nm: jnp.stack([small[l][nm] for l in range(DEPTH)]) for nm in rep_names if nm in small[0]}
    grads["final_norm"] = d_final[0]
    grads["loss"] = loss[0, :1]
    zero = jnp.zeros((1,), F32)
    small = _all_gather([behind(token, _pack([grads[nm] for nm in rep_names]))], name="gather_small_grads")

    last, token = reduce_mid(pending, [dx[0], small[0]], "mixer0")
    for state, tag in in_flight:
        reduce_end(state, [], tag)
    grad_x = dx[0]

    def adamw(nm, after):
        swap = (lambda a: jnp.swapaxes(a, 1, 2)) if nm in SWAPPED_UPDATE else (lambda a: a)
        res = _adamw([parts[nm, l] for l in range(DEPTH)], swap(given[nm]), swap(given["m_" + nm]), swap(given["v_" + nm]),
                     after, name=f"adamw_{nm}")
        return [swap(a) for a in res]

    sh_out = {nm: adamw(nm, token) for nm in FFN_WEIGHTS}
    packs = [_pack([given.get(pre + nm, zero) for nm in rep_names])[None] for pre in ("", "m_", "v_")]
    rep_res = _adamw(small, *packs, token, name="adamw_replicated")
    rep_out = [dict(zip(rep_names, _unpack(o[0], rep_shapes))) for o in rep_res]

    reduce_end(last, [rep_res[0], sh_out["w_down"][0]], "mixer0")
    sh_out.update({nm: adamw(nm, token) for nm in MIXER_WEIGHTS})

    out = [rep_out[0]["loss"][0], grad_x[None]]
    for i in range(4):
        out += [sh_out[nm][i] if nm in sh_axis else rep_out[i][nm] for nm in WEIGHT_ORDER]
    return tuple(out)
```

```python
import functools
import math

import numpy as np
import jax
import jax.numpy as jnp
from jax import lax
from jax.experimental import pallas as pl
from jax.experimental.pallas import tpu as pltpu

F32 = jnp.float32
BF16 = jnp.bfloat16

D_MODEL = 1024
DEPTH = 2
D_GROUP = 256
MLA_HEADS = 4
MLA_NOPE = 64
MLA_ROPE = 32
ROPE_THETA = 10000.0
POOL_WINDOWS = (2, 4, 8, 16)
SWA_HEADS = 4
SWA_KV_HEADS = 2
SWA_WINDOW = 128
D_FF = 2816
GU_TILE = 256
RMS_EPS = 1e-6
LANES = 128
HEAD = 64
VMEM_LIMIT = 48 * 1024 * 1024
NEG = -1e30

ADAM_LR = 0.001
ADAM_B1 = 0.9
ADAM_B2 = 0.999
ADAM_EPS = 1e-08
ADAM_WD = 0.01
ADAM_STEP = 10

N_DEV = 8
PACK_COLS = 1024

C_QSW, C_CQ, C_GB, C_GC, C_UCONV, C_UPOOL = (0, 512), (512, 256), (768, 256), (1024, 256), (1280, 256), (1536, 256)
C_KSW, C_VSW, C_CKV, C_KR, C_KRP = (1792, 256), (2048, 256), (2304, 128), (2432, 128), (2560, 128)

SHARDED = (("w_in", (DEPTH, 1024, 244), 2), ("w_uq", (DEPTH, 256, 48), 2), ("w_ukv", (DEPTH, 128, 64), 2),
           ("conv_w", (DEPTH, 3, 32), 2), ("w_o", (DEPTH, 128, 1024), 1), ("w_gate_up", (DEPTH, 1024, 704), 2),
           ("w_down", (DEPTH, 352, 1024), 1))
REPLICATED = (("attn_norm", (DEPTH, 1024)), ("mla_q_norm", (DEPTH, 256)), ("mla_kv_norm", (DEPTH, 128)),
              ("pool_w", (DEPTH, 4, 64, 64)), ("pool_scale", (DEPTH, 256)), ("swa_sinks", (DEPTH, 4)),
              ("mix_norm", (DEPTH, 1024)), ("ffn_norm", (DEPTH, 1024)), ("final_norm", (1024,)), ("loss", (1,)))
WEIGHT_ORDER = ("attn_norm", "w_in", "mla_q_norm", "w_uq", "mla_kv_norm", "w_ukv", "conv_w", "pool_w", "pool_scale",
                "swa_sinks", "mix_norm", "w_o", "ffn_norm", "w_gate_up", "w_down", "final_norm")


def _params(sem):
    return pltpu.CompilerParams(dimension_semantics=sem, vmem_limit_bytes=VMEM_LIMIT)


def _pick(dim, target):
    if dim <= target:
        return dim
    best = None
    for t in range(LANES, target + 1, LANES):
        if dim % t == 0:
            best = t
    assert best is not None, (dim, target)
    return best


def _mm(a, b, *, name, ta=False, tb=False, res=None, out_dtype=F32, tm=1024, tn=1024, tk=1024, epilogue=None):
    m, k = (a.shape[1], a.shape[0]) if ta else a.shape
    n = b.shape[0] if tb else b.shape[1]
    assert (b.shape[1] if tb else b.shape[0]) == k
    tm, tn, tk = _pick(m, tm), _pick(n, tn), _pick(k, tk)
    nj, nk = n // tn, k // tk
    dims = (((0 if ta else 1,), (1 if tb else 0,)), ((), ()))
    fn, extra, outs = epilogue if epilogue is not None else (None, [], [(n, out_dtype)])
    if res is not None:
        assert epilogue is None
        fn, extra = (lambda acc, r: (acc + r,)), [res]
    n_in, n_out = 2 + len(extra), len(outs)

    def body(*refs):
        a_ref, b_ref, acc_ref = refs[0], refs[1], refs[-1]
        kk = pl.program_id(2)

        def product():
            return lax.dot_general(a_ref[...].astype(BF16), b_ref[...].astype(BF16), dims, preferred_element_type=F32)

        def finish(acc):
            tiles = (acc,) if fn is None else fn(acc, *[r[...] for r in refs[2:n_in]])
            for o_ref, tile in zip(refs[n_in:n_in + n_out], tiles):
                o_ref[...] = tile.astype(o_ref.dtype)

        if nk == 1:
            finish(product())
            return

        @pl.when(kk == 0)
        def _():
            acc_ref[...] = jnp.zeros_like(acc_ref)

        acc_ref[...] += product()

        @pl.when(kk == nk - 1)
        def _():
            finish(acc_ref[...])

    def col_tiles(width):
        assert width % (nj * LANES) == 0, (width, nj)
        return pl.BlockSpec((tm, width // nj), lambda i, j, kk: (i, j))

    a_spec = pl.BlockSpec((tk, tm), lambda i, j, kk: (kk, i)) if ta else pl.BlockSpec((tm, tk), lambda i, j, kk: (i, kk))
    b_spec = pl.BlockSpec((tn, tk), lambda i, j, kk: (j, kk)) if tb else pl.BlockSpec((tk, tn), lambda i, j, kk: (kk, j))
    res_ = pl.pallas_call(
        body, name=name, grid=(m // tm, nj, nk), in_specs=[a_spec, b_spec] + [col_tiles(e.shape[1]) for e in extra],
        out_specs=[col_tiles(w) for w, _ in outs],
        out_shape=[jax.ShapeDtypeStruct((m, w), dt) for w, dt in outs],
        scratch_shapes=[pltpu.VMEM((tm, tn), F32)] if nk > 1 else [],
        compiler_params=_params(("parallel", "parallel", "arbitrary")),
    )(a, b, *extra)
    return res_[0] if epilogue is None else res_


def _mm_nt_deep(a, b, *, name, tm=512, nk=4):
    m, k = a.shape
    n = b.shape[0]
    tm = min(tm, m)
    tk, ni = k // nk, m // tm
    assert k % (nk * LANES) == 0 and m % tm == 0

    def body(a_hbm, b_ref, o_ref, abuf, sem):
        i = pl.program_id(0)
        slot = i % 2

        def chunk(row, sl, kk):
            return pltpu.make_async_copy(a_hbm.at[pl.ds(pl.multiple_of(row * tm, tm), tm), pl.ds(kk * tk, tk)],
                                         abuf.at[sl, :, pl.ds(kk * tk, tk)], sem.at[sl * nk + kk])

        @pl.when(i == 0)
        def _():
            for kk in range(nk):
                chunk(0, 0, kk).start()

        @pl.when(i + 1 < ni)
        def _():
            for kk in range(nk):
                chunk(i + 1, 1 - slot, kk).start()

        for kk in range(nk):
            chunk(i, slot, kk).wait()
        o_ref[...] = lax.dot_general(abuf[slot], b_ref[...], NT_DIMS, preferred_element_type=F32)

    return pl.pallas_call(
        body, name=name, grid=(ni,),
        in_specs=[pl.BlockSpec(memory_space=pl.ANY), pl.BlockSpec((n, k), lambda i: (0, 0))],
        out_specs=pl.BlockSpec((tm, n), lambda i: (i, 0)), out_shape=jax.ShapeDtypeStruct((m, n), F32),
        scratch_shapes=[pltpu.VMEM((2, tm, k), a.dtype), pltpu.SemaphoreType.DMA((2 * nk,))],
        compiler_params=_params(("arbitrary",)),
    )(a, b)


def _rowwise(fn, *, name, rows, ins, outs, reds=(), tm=512):
    tm = min(tm, rows)
    assert rows % tm == 0
    n_in, n_out = len(ins), len(outs)

    def body(*refs):
        vals = [jnp.concatenate([r[h] for h in range(r.shape[0])], axis=1) if spec[0] == "heads" else r[...]
                for spec, r in zip(ins, refs[:n_in])]
        res = fn(*vals)
        for out, r, v in zip(outs, refs[n_in:n_in + n_out], res[:n_out]):
            if len(out) >= 3:
                for h in range(out[2]):
                    piece = v[:, h * LANES:(h + 1) * LANES]
                    r[h] = (piece.T if len(out) == 4 else piece).astype(r.dtype)
            else:
                r[...] = v.astype(r.dtype)
        if reds:
            @pl.when(pl.program_id(0) == 0)
            def _():
                for r in refs[n_in + n_out:]:
                    r[...] = jnp.zeros_like(r)

            for r, v in zip(refs[n_in + n_out:], res[n_out:]):
                r[...] += v

    in_specs, args = [], []
    for spec in ins:
        if spec[0] == "row":
            _, arr, width, blk = spec
            in_specs.append(pl.BlockSpec((tm, width), functools.partial(lambda i, blk: (i, blk), blk=blk)))
        elif spec[0] == "heads":
            arr = spec[1]
            in_specs.append(pl.BlockSpec((arr.shape[0], tm, LANES), lambda i: (0, i, 0)))
        else:
            arr = spec[1]
            in_specs.append(pl.BlockSpec(arr.shape, functools.partial(lambda i, nd: (0,) * nd, nd=arr.ndim)))
        args.append(arr)
    def out_spec(o):
        if len(o) == 4:
            return pl.BlockSpec((o[2], LANES, tm), lambda i: (0, 0, i)), (o[2], LANES, rows)
        if len(o) == 3:
            return pl.BlockSpec((o[2], tm, LANES), lambda i: (0, i, 0)), (o[2], rows, LANES)
        return pl.BlockSpec((tm, o[0]), lambda i: (i, 0)), (rows, o[0])

    out_specs = [out_spec(o)[0] for o in outs]
    out_shape = [jax.ShapeDtypeStruct(out_spec(o)[1], o[1]) for o in outs]
    out_specs += [pl.BlockSpec((r, w), lambda i: (0, 0)) for r, w in reds]
    out_shape += [jax.ShapeDtypeStruct((r, w), F32) for r, w in reds]
    return pl.pallas_call(body, name=name, grid=(rows // tm,), in_specs=in_specs, out_specs=out_specs,
                          out_shape=out_shape, compiler_params=_params(("arbitrary",)))(*args)


def _rstd(x, n):
    return lax.rsqrt(jnp.sum(x * x, axis=-1, keepdims=True) * (1.0 / n) + RMS_EPS)


def _rms_fwd(x_spec, g, *, name, rows, width):
    def fn(x, gv):
        return (x * _rstd(x, width) * gv,)
    return _rowwise(fn, name=name, rows=rows, ins=[x_spec, ("full", g)], outs=[(width, BF16)])[0]


def _rms_bwd(x_spec, g, dy, res, *, name, rows, width, out_dtypes):
    def fn(x, gv, dyv, *rest):
        r = _rstd(x, width)
        dyg = dyv * gv
        dx = r * dyg - x * (r * r * r) * (jnp.sum(dyg * x, axis=-1, keepdims=True) * (1.0 / width))
        if rest:
            dx = dx + rest[0]
        return (dx,) * len(out_dtypes) + (jnp.sum(dyv * x * r, axis=0, keepdims=True),)

    ins = [x_spec, ("full", g), ("row", dy, width, 0)]
    if res is not None:
        ins.append(("row", res, width, 0))
    return _rowwise(fn, name=name, rows=rows, ins=ins, outs=[(width, dt) for dt in out_dtypes], reds=[(1, width)])


NT_DIMS = (((1,), (1,)), ((), ()))
TN_DIMS = (((0,), (0,)), ((), ()))
BNT_DIMS = (((2,), (2,)), ((0,), (0,)))
BNN_DIMS = (((2,), (1,)), ((0,), (0,)))


def _mla_tile(s):
    return min(512, s)


def _block_tables(pairs):
    return jnp.asarray([a for a, _ in pairs], jnp.int32), jnp.asarray([b for _, b in pairs], jnp.int32)


def _causal(shape, query_axis):
    return lax.broadcasted_iota(jnp.int32, shape, query_axis) >= lax.broadcasted_iota(jnp.int32, shape, 1 - query_axis)


def _mla_fwd(qa, ka, vta, *, name, s, nh):
    t = _mla_tile(s)
    nq = s // t

    pairs = [(i, j) for i in range(nq) for j in range(i + 1)]

    def body(qb_ref, kb_ref, q_ref, k_ref, vt_ref, o_ref, lse_ref, m_s, acc_s):
        i, j = qb_ref[pl.program_id(0)], kb_ref[pl.program_id(0)]

        @pl.when(j == 0)
        def _():
            m_s[...] = jnp.full_like(m_s, NEG)
            acc_s[...] = jnp.zeros_like(acc_s)

        def step(diag):
            sc = lax.dot_general(k_ref[...], q_ref[...], BNT_DIMS, preferred_element_type=F32)
            if diag:
                sc = jnp.where(_causal(sc.shape[1:], 1)[None], sc, NEG)
            m_prev = m_s[...]
            m_new = jnp.maximum(m_prev, jnp.max(sc, axis=1, keepdims=True))
            p = jnp.exp(sc - m_new).astype(BF16)
            acc_s[...] = (jnp.exp(m_prev - m_new) * acc_s[...]
                          + lax.dot_general(vt_ref[...], p, BNN_DIMS, preferred_element_type=F32))
            m_s[...] = m_new

        pl.when(j < i)(functools.partial(step, False))
        pl.when(j == i)(functools.partial(step, True))

        @pl.when(j == i)
        def _():
            row = lax.broadcasted_iota(jnp.int32, (LANES, t), 0)
            for h in range(nh):
                acc = acc_s[h]
                l = acc[HEAD:HEAD + 1, :]
                o_ref[h] = jnp.where(row < HEAD, acc / l, 0.0).T
                lse_ref[h] = m_s[h] + jnp.log(l)

    q_spec = pl.BlockSpec((nh, t, LANES), lambda p, qb, kb: (0, qb[p], 0))
    k_spec = pl.BlockSpec((nh, t, LANES), lambda p, qb, kb: (0, kb[p], 0))
    vt_spec = pl.BlockSpec((nh, LANES, t), lambda p, qb, kb: (0, 0, kb[p]))
    return pl.pallas_call(
        body, name=name,
        grid_spec=pltpu.PrefetchScalarGridSpec(
            num_scalar_prefetch=2, grid=(len(pairs),), in_specs=[q_spec, k_spec, vt_spec],
            out_specs=[q_spec, pl.BlockSpec((nh, 1, t), lambda p, qb, kb: (0, 0, qb[p]))],
            scratch_shapes=[pltpu.VMEM((nh, 1, t), F32), pltpu.VMEM((nh, LANES, t), F32)]),
        out_shape=[jax.ShapeDtypeStruct((nh, s, LANES), F32), jax.ShapeDtypeStruct((nh, 1, s), F32)],
        compiler_params=_params(("arbitrary",)),
    )(*_block_tables(pairs), qa, ka, vta)


def _mla_bwd(qa, ka, kta, va, o, do, lse_row, after, *, name, s, nh):
    t = _mla_tile(s)
    nq = s // t

    pairs = [(kb, j) for kb in range(nq) for j in range(kb, nq)]

    def body(kb_ref, qb_ref, q_ref, k_ref, kt_ref, v_ref, o_ref, do_ref, lse_ref, after_ref, dq_hbm, dk_ref, dv_ref,
             dqt_s, dk_s, dv_s, d_s, stage, sem):
        kb, j = kb_ref[pl.program_id(0)], qb_ref[pl.program_id(0)]
        cols = pl.ds(pl.multiple_of(j * t, t), t)

        @pl.when(j == kb)
        def _():
            dk_s[...] = jnp.zeros_like(dk_s)
            dv_s[...] = jnp.zeros_like(dv_s)

        @pl.when(kb == 0)
        def _():
            dqt_s[:, :, cols] = jnp.zeros((nh, LANES, t), F32)
            for h in range(nh):
                d_col = jnp.sum(do_ref[h] * o_ref[h], axis=1, keepdims=True)
                d_s[h, :, cols] = jnp.broadcast_to(d_col, (t, LANES)).T[0:1, :]

        def step(diag):
            q = q_ref[...]
            do_b = do_ref[...].astype(BF16)
            sc = lax.dot_general(k_ref[...], q, BNT_DIMS, preferred_element_type=F32)
            if diag:
                sc = jnp.where(_causal(sc.shape[1:], 1)[None], sc, NEG)
            p = jnp.exp(sc - lse_ref[...])
            dv_s[...] += lax.dot_general(p.astype(BF16), do_b, BNN_DIMS, preferred_element_type=F32)
            dp = lax.dot_general(v_ref[...], do_b, BNT_DIMS, preferred_element_type=F32)
            ds = (p * (dp - d_s[:, :, cols])).astype(BF16)
            dk_s[...] += lax.dot_general(ds, q, BNN_DIMS, preferred_element_type=F32)
            dqt_s[:, :, cols] += lax.dot_general(kt_ref[...], ds, BNN_DIMS, preferred_element_type=F32)

        pl.when(j > kb)(functools.partial(step, False))
        pl.when(j == kb)(functools.partial(step, True))

        @pl.when(j == kb)
        def _():
            slot = kb % 2

            def write_out(sl):
                return pltpu.make_async_copy(stage.at[sl], dq_hbm.at[:, cols, :], sem.at[sl])

            @pl.when(kb >= 2)
            def _():
                write_out(slot).wait()

            for h in range(nh):
                stage[slot, h] = dqt_s[h, :, cols].T
            write_out(slot).start()

            @pl.when(kb == nq - 1)
            def _():
                write_out(slot).wait()
                if nq >= 2:
                    write_out(1 - slot).wait()

        @pl.when(j == nq - 1)
        def _():
            dk_ref[...] = dk_s[...]
            dv_ref[...] = dv_s[...].astype(dv_ref.dtype)

    q_spec = pl.BlockSpec((nh, t, LANES), lambda p, kb, qb: (0, qb[p], 0))
    kv_spec = pl.BlockSpec((nh, t, LANES), lambda p, kb, qb: (0, kb[p], 0))
    kt_spec = pl.BlockSpec((nh, LANES, t), lambda p, kb, qb: (0, 0, kb[p]))
    row_spec = pl.BlockSpec((nh, 1, t), lambda p, kb, qb: (0, 0, qb[p]))
    whole = jax.ShapeDtypeStruct((nh, s, LANES), F32)
    return pl.pallas_call(
        body, name=name,
        grid_spec=pltpu.PrefetchScalarGridSpec(
            num_scalar_prefetch=2, grid=(len(pairs),),
            in_specs=[q_spec, kv_spec, kt_spec, kv_spec, q_spec, q_spec, row_spec, pl.BlockSpec(memory_space=pl.ANY)],
            out_specs=[pl.BlockSpec(memory_space=pl.ANY), kv_spec, kv_spec],
            scratch_shapes=[pltpu.VMEM((nh, LANES, s), F32), pltpu.VMEM((nh, t, LANES), F32),
                            pltpu.VMEM((nh, t, LANES), F32), pltpu.VMEM((nh, 1, s), F32),
                            pltpu.VMEM((2, nh, t, LANES), F32), pltpu.SemaphoreType.DMA((2,))]),
        out_shape=[whole, whole, jax.ShapeDtypeStruct((nh, s, LANES), BF16)],
        compiler_params=_params(("arbitrary",)),
    )(*_block_tables(pairs), qa, ka, kta, va, o, do, lse_row, after)


SWA_PIECE = 128
SWA_KEYS = 2 * SWA_PIECE


def _swa_block(s):
    return min(1024, s)


def _swa_piece(hp_ref, h, q, k_ref, v_ref, qpos0, scale):
    kstart = pl.multiple_of(jnp.maximum(qpos0 - SWA_PIECE, 0), SWA_PIECE)
    k = k_ref[pl.ds(kstart, SWA_KEYS), :].astype(BF16)
    v = v_ref[pl.ds(kstart, SWA_KEYS), :].astype(BF16)
    sc = lax.dot_general(q, k, NT_DIMS, preferred_element_type=F32)
    dist = (qpos0 + lax.broadcasted_iota(jnp.int32, sc.shape, 0)) - (kstart + lax.broadcasted_iota(jnp.int32, sc.shape, 1))
    sc = sc * scale - hp_ref[h, 0] * dist.astype(F32)
    sc = jnp.where((dist >= 0) & (dist < SWA_WINDOW), sc, NEG)
    return kstart, k, v, sc


def _swa_fwd(proj, hp, *, name, s, scale):
    tb = _swa_block(s)
    group = SWA_HEADS // SWA_KV_HEADS
    q_off, k_off, v_off = C_QSW[0] // LANES, C_KSW[0] // LANES, C_VSW[0] // LANES

    def body(hp_ref, q_ref, k_ref, v_ref, o_ref, lse_ref):
        h, i = pl.program_id(0), pl.program_id(1)
        sink = hp_ref[h, 1]
        for r in range(0, tb, SWA_PIECE):
            rows = pl.ds(r, SWA_PIECE)
            _, _, v, sc = _swa_piece(hp_ref, h, q_ref[rows, :].astype(BF16), k_ref, v_ref, i * tb + r, scale)
            m = jnp.maximum(jnp.max(sc, axis=1, keepdims=True), sink)
            p = jnp.exp(sc - m)
            l = jnp.sum(p, axis=1, keepdims=True) + jnp.exp(sink - m)
            o_ref[rows, :] = jnp.dot(p.astype(BF16), v, preferred_element_type=F32) / l
            lse_ref[rows, :] = m + jnp.log(l)

    whole = lambda off: pl.BlockSpec((s, LANES), lambda h, i: (0, off + h // group))
    return pl.pallas_call(
        body, name=name, grid=(SWA_HEADS, s // tb),
        in_specs=[pl.BlockSpec(memory_space=pltpu.SMEM), pl.BlockSpec((tb, LANES), lambda h, i: (i, q_off + h)),
                  whole(k_off), whole(v_off)],
        out_specs=[pl.BlockSpec((tb, LANES), lambda h, i: (i, h)), pl.BlockSpec((None, tb, 1), lambda h, i: (h, i, 0))],
        out_shape=[jax.ShapeDtypeStruct((s, SWA_HEADS * LANES), F32), jax.ShapeDtypeStruct((SWA_HEADS, s, 1), F32)],
        compiler_params=_params(("parallel", "parallel")),
    )(hp, proj, proj, proj)


def _swa_bwd(proj, o, do, lse, hp, *, name, s, scale):
    tb = _swa_block(s)
    nqb = s // tb
    group = SWA_HEADS // SWA_KV_HEADS
    q_off, k_off, v_off = C_QSW[0] // LANES, C_KSW[0] // LANES, C_VSW[0] // LANES

    def body(hp_ref, q_ref, k_ref, v_ref, o_ref, do_ref, lse_ref, dq_ref, dk_ref, dv_ref, dsink_ref):
        kh, g, i = pl.program_id(0), pl.program_id(1), pl.program_id(2)
        h = kh * group + g
        sink = hp_ref[h, 1]

        @pl.when((g == 0) & (i == 0))
        def _():
            dk_ref[...] = jnp.zeros_like(dk_ref)
            dv_ref[...] = jnp.zeros_like(dv_ref)

        @pl.when(i == 0)
        def _():
            dsink_ref[...] = jnp.zeros_like(dsink_ref)

        for r in range(0, tb, SWA_PIECE):
            rows = pl.ds(r, SWA_PIECE)
            q = q_ref[rows, :].astype(BF16)
            dov = do_ref[rows, :]
            do_b = dov.astype(BF16)
            lse_r = lse_ref[rows, :]
            d_r = jnp.sum(dov * o_ref[rows, :], axis=1, keepdims=True)
            kstart, k, v, sc = _swa_piece(hp_ref, h, q, k_ref, v_ref, i * tb + r, scale)
            p = jnp.exp(sc - lse_r)
            dp = lax.dot_general(do_b, v, NT_DIMS, preferred_element_type=F32)
            ds = (p * (dp - d_r)).astype(BF16)
            dq_ref[rows, :] = (jnp.dot(ds, k, preferred_element_type=F32) * scale).astype(dq_ref.dtype)
            win = pl.ds(kstart, SWA_KEYS)
            dk_ref[win, :] += lax.dot_general(ds, q, TN_DIMS, preferred_element_type=F32) * scale
            dv_ref[win, :] += lax.dot_general(p.astype(BF16), do_b, TN_DIMS, preferred_element_type=F32)
            part = jnp.sum(-jnp.exp(sink - lse_r) * d_r, axis=0, keepdims=True)
            dsink_ref[...] += jnp.broadcast_to(part, (1, LANES))

    whole = lambda off: pl.BlockSpec((s, LANES), lambda kh, g, i: (0, off + kh))
    q_map = lambda kh, g, i: (i, kh * group + g)
    return pl.pallas_call(
        body, name=name, grid=(SWA_KV_HEADS, group, nqb),
        in_specs=[pl.BlockSpec(memory_space=pltpu.SMEM),
                  pl.BlockSpec((tb, LANES), lambda kh, g, i: (i, q_off + kh * group + g)), whole(k_off), whole(v_off),
                  pl.BlockSpec((tb, LANES), q_map), pl.BlockSpec((tb, LANES), q_map),
                  pl.BlockSpec((None, tb, 1), lambda kh, g, i: (kh * group + g, i, 0))],
        out_specs=[pl.BlockSpec((tb, LANES), q_map), whole(0), whole(0),
                   pl.BlockSpec((None, 1, LANES), lambda kh, g, i: (kh * group + g, 0, 0))],
        out_shape=[jax.ShapeDtypeStruct((s, SWA_HEADS * LANES), BF16),
                   jax.ShapeDtypeStruct((s, SWA_KV_HEADS * LANES), F32), jax.ShapeDtypeStruct((s, SWA_KV_HEADS * LANES), F32),
                   jax.ShapeDtypeStruct((SWA_HEADS, 1, LANES), F32)],
        compiler_params=_params(("parallel", "arbitrary", "arbitrary")),
    )(hp, proj, proj, proj, o, do, lse)


def _shift_down(z, k):
    rows = lax.broadcasted_iota(jnp.int32, z.shape, 0)
    return jnp.where(rows >= k, pltpu.roll(z, k, 0), 0.0)


def _shift_up(z, k):
    n = z.shape[0]
    rows = lax.broadcasted_iota(jnp.int32, z.shape, 0)
    return jnp.where(rows < n - k, pltpu.roll(z, n - k, 0), 0.0)


def _rows3(a, b, c):
    r = lax.broadcasted_iota(jnp.int32, (3, a.shape[1]), 0)
    return jnp.where(r == 0, a, jnp.where(r == 1, b, c))


def _col_spec(s, off):
    return pl.BlockSpec((s, LANES), functools.partial(lambda j, off: (0, off + j), off=off))


def _conv_fwd(proj, conv_w, *, name, s):
    def body(gb_ref, gc_ref, u_ref, w_ref, y_ref):
        w0, w1, w2 = w_ref[0:1, :], w_ref[1:2, :], w_ref[2:3, :]
        z = gc_ref[...] * u_ref[...]
        c = w2 * z + w1 * _shift_down(z, 1) + w0 * _shift_down(z, 2)
        y_ref[...] = gb_ref[...] * c

    return pl.pallas_call(
        body, name=name, grid=(2,),
        in_specs=[_col_spec(s, C_GB[0] // LANES), _col_spec(s, C_GC[0] // LANES), _col_spec(s, C_UCONV[0] // LANES),
                  pl.BlockSpec((3, LANES), lambda j: (0, j))],
        out_specs=_col_spec(s, 0), out_shape=jax.ShapeDtypeStruct((s, D_GROUP), F32),
        compiler_params=_params(("parallel",)),
    )(proj, proj, proj, conv_w)


def _conv_bwd(dy, proj, conv_w, *, name, s):
    def body(dy_ref, gb_ref, gc_ref, u_ref, w_ref, dgb_ref, dgc_ref, du_ref, dw_ref):
        w0, w1, w2 = w_ref[0:1, :], w_ref[1:2, :], w_ref[2:3, :]
        gc, u, dyv = gc_ref[...], u_ref[...], dy_ref[...]
        z = gc * u
        z1, z2 = _shift_down(z, 1), _shift_down(z, 2)
        c = w2 * z + w1 * z1 + w0 * z2
        dgb_ref[...] = (dyv * c).astype(dgb_ref.dtype)
        dc = dyv * gb_ref[...]
        dz = w2 * dc + w1 * _shift_up(dc, 1) + w0 * _shift_up(dc, 2)
        dgc_ref[...] = (dz * u).astype(dgc_ref.dtype)
        du_ref[...] = (dz * gc).astype(du_ref.dtype)
        dw_ref[...] = _rows3(jnp.sum(dc * z2, axis=0, keepdims=True), jnp.sum(dc * z1, axis=0, keepdims=True),
                             jnp.sum(dc * z, axis=0, keepdims=True))

    act = jax.ShapeDtypeStruct((s, D_GROUP), BF16)
    return pl.pallas_call(
        body, name=name, grid=(2,),
        in_specs=[_col_spec(s, 0), _col_spec(s, C_GB[0] // LANES), _col_spec(s, C_GC[0] // LANES),
                  _col_spec(s, C_UCONV[0] // LANES), pl.BlockSpec((3, LANES), lambda j: (0, j))],
        out_specs=[_col_spec(s, 0), _col_spec(s, 0), _col_spec(s, 0), pl.BlockSpec((3, LANES), lambda j: (0, j))],
        out_shape=[act, act, act, jax.ShapeDtypeStruct((3, D_GROUP), F32)],
        compiler_params=_params(("parallel",)),
    )(dy, proj, proj, proj, conv_w)


def _pool_select(j, lane, a2, a4, a8, a16):
    lo = lane < HEAD
    return jnp.where(j == 0, jnp.where(lo, a2, a4), jnp.where(lo, a8, a16))


def _pooled(u, j):
    s2 = u + _shift_down(u, 1)
    s4 = s2 + _shift_down(s2, 2)
    s8 = s4 + _shift_down(s4, 4)
    s16 = s8 + _shift_down(s8, 8)
    lane = lax.broadcasted_iota(jnp.int32, u.shape, 1)
    rows = lax.broadcasted_iota(jnp.int32, u.shape, 0)
    win = _pool_select(j, lane, *POOL_WINDOWS)
    count = jnp.minimum(rows + 1, win).astype(F32)
    return _pool_select(j, lane, s2, s4, s8, s16) / count - u, count


def _pool_fwd(proj, wbd, scale, *, name, s):
    def body(u_ref, w_ref, sc_ref, y_ref):
        pooled, _ = _pooled(u_ref[...], pl.program_id(0))
        y_ref[...] = jnp.dot(pooled.astype(BF16), w_ref[...].astype(BF16), preferred_element_type=F32) * sc_ref[...]

    return pl.pallas_call(
        body, name=name, grid=(2,),
        in_specs=[_col_spec(s, C_UPOOL[0] // LANES), pl.BlockSpec((None, LANES, LANES), lambda j: (j, 0, 0)),
                  pl.BlockSpec((1, LANES), lambda j: (0, j))],
        out_specs=_col_spec(s, 0), out_shape=jax.ShapeDtypeStruct((s, D_GROUP), F32),
        compiler_params=_params(("parallel",)),
    )(proj, wbd, scale)


def _pool_bwd(dy, proj, wbd, scale, *, name, s):
    def body(dy_ref, u_ref, w_ref, sc_ref, du_ref, dw_ref, dsc_ref):
        j = pl.program_id(0)
        pooled, count = _pooled(u_ref[...], j)
        pooled_b = pooled.astype(BF16)
        w_b = w_ref[...].astype(BF16)
        dyv = dy_ref[...]
        mixed = jnp.dot(pooled_b, w_b, preferred_element_type=F32)
        dsc_ref[...] = jnp.sum(dyv * mixed, axis=0, keepdims=True)
        dms = (dyv * sc_ref[...]).astype(BF16)
        dw_ref[...] = lax.dot_general(pooled_b, dms, (((0,), (0,)), ((), ())), preferred_element_type=F32)
        dpooled = lax.dot_general(dms, w_b, (((1,), (1,)), ((), ())), preferred_element_type=F32)
        r = dpooled / count
        a2 = r + _shift_up(r, 1)
        a4 = a2 + _shift_up(a2, 2)
        a8 = a4 + _shift_up(a4, 4)
        a16 = a8 + _shift_up(a8, 8)
        lane = lax.broadcasted_iota(jnp.int32, r.shape, 1)
        du_ref[...] = (_pool_select(j, lane, a2, a4, a8, a16) - dpooled).astype(du_ref.dtype)

    return pl.pallas_call(
        body, name=name, grid=(2,),
        in_specs=[_col_spec(s, 0), _col_spec(s, C_UPOOL[0] // LANES),
                  pl.BlockSpec((None, LANES, LANES), lambda j: (j, 0, 0)), pl.BlockSpec((1, LANES), lambda j: (0, j))],
        out_specs=[_col_spec(s, 0), pl.BlockSpec((None, LANES, LANES), lambda j: (j, 0, 0)),
                   pl.BlockSpec((1, LANES), lambda j: (0, j))],
        out_shape=[jax.ShapeDtypeStruct((s, D_GROUP), BF16), jax.ShapeDtypeStruct((2, LANES, LANES), F32),
                   jax.ShapeDtypeStruct((1, D_GROUP), F32)],
        compiler_params=_params(("parallel",)),
    )(dy, proj, wbd, scale)


def _mesh_pos():
    return lax.axis_index("x"), lax.axis_index("y"), lax.axis_index("c")


def _any_specs(n):
    return [pl.BlockSpec(memory_space=pl.ANY)] * n


def _all_gather(xs, *, name):
    n = len(xs)

    def body(*refs):
        x_refs, out_refs = refs[:n], refs[n:2 * n]
        send_sems, recv_sems, local_sems = refs[2 * n:]
        x, y, cc = _mesh_pos()
        me, sibling = (x, y, cc), (x, y, 1 - cc)
        chips = [(1 - x, y), (x, 1 - y), (1 - x, 1 - y)]

        def slot(a, px, py, pc):
            return out_refs[a].at[4 * px + 2 * py + pc]

        def copy(a, k, block, to, src=None):
            return pltpu.make_async_remote_copy(
                src_ref=slot(a, *block) if src is None else src, dst_ref=slot(a, *block), send_sem=send_sems.at[a, k],
                recv_sem=recv_sems.at[a, k], device_id=to, device_id_type=pl.DeviceIdType.MESH)

        mine = [pltpu.make_async_copy(x_refs[a], slot(a, *me), local_sems.at[a]) for a in range(n)]
        first = []
        for a in range(n):
            first.append(copy(a, 0, me, sibling, src=x_refs[a]))
            first += [copy(a, 1 + j, me, (*chip, cc), src=x_refs[a]) for j, chip in enumerate(chips)]
        for cp in mine + first:
            cp.start()
        passed = []
        for j, chip in enumerate(chips):
            for a in range(n):
                copy(a, 1 + j, (*chip, cc), me).wait_recv()
                passed.append(copy(a, 4 + j, (*chip, cc), sibling))
                passed[-1].start()
        for a in range(n):
            copy(a, 0, sibling, me).wait_recv()
        for j, chip in enumerate(chips):
            for a in range(n):
                copy(a, 4 + j, (*chip, 1 - cc), me).wait_recv()
        for cp in first + passed:
            cp.wait_send()
        for cp in mine:
            cp.wait()

    return pl.pallas_call(
        body, name=name, out_shape=[jax.ShapeDtypeStruct((N_DEV,) + a.shape, a.dtype) for a in xs],
        in_specs=_any_specs(n), out_specs=_any_specs(n),
        scratch_shapes=[pltpu.SemaphoreType.DMA((n, 7)), pltpu.SemaphoreType.DMA((n, 7)), pltpu.SemaphoreType.DMA((n,))],
    )(*xs)


def _plan_gather_near(src_refs, land_refs, send_sems, recv_sems):
    x, y, cc = _mesh_pos()
    me = 4 * x + 2 * y + cc
    plan = []
    for a, (src, land) in enumerate(zip(src_refs, land_refs)):
        for k, (px, py, pc) in enumerate([(x, y, 1 - cc), (1 - x, y, cc), (x, 1 - y, cc), (1 - x, 1 - y, cc)]):
            sems = dict(send_sem=send_sems.at[4 * a + k], recv_sem=recv_sems.at[4 * a + k], device_id=(px, py, pc),
                        device_id_type=pl.DeviceIdType.MESH)
            plan.append((pltpu.make_async_remote_copy(src_ref=src, dst_ref=land.at[me], **sems),
                         pltpu.make_async_remote_copy(src_ref=src, dst_ref=land.at[4 * px + 2 * py + pc], **sems)))
    return plan


def _plan_gather_pass(src_refs, land_refs, send_sems, recv_sems):
    x, y, cc = _mesh_pos()
    plan = []
    for a, land in enumerate(land_refs):
        for j, (px, py) in enumerate([(1 - x, y), (x, 1 - y), (1 - x, 1 - y)]):
            mine, theirs = land.at[4 * px + 2 * py + cc], land.at[4 * px + 2 * py + 1 - cc]
            sems = dict(send_sem=send_sems.at[3 * a + j], recv_sem=recv_sems.at[3 * a + j], device_id=(x, y, 1 - cc),
                        device_id_type=pl.DeviceIdType.MESH)
            plan.append((pltpu.make_async_remote_copy(src_ref=mine, dst_ref=mine, **sems),
                         pltpu.make_async_remote_copy(src_ref=mine, dst_ref=theirs, **sems)))
    return plan


def _plan_sibling(src_refs, land_refs, send_sems, recv_sems):
    x, y, cc = _mesh_pos()
    plan = []
    for a, (src, land) in enumerate(zip(src_refs, land_refs)):
        cp = pltpu.make_async_remote_copy(
            src_ref=src.at[1 - cc], dst_ref=land, send_sem=send_sems.at[a], recv_sem=recv_sems.at[a],
            device_id=(x, y, 1 - cc), device_id_type=pl.DeviceIdType.MESH)
        plan.append((cp, cp))
    return plan


def _plan_chips(src_refs, land_refs, send_sems, recv_sems):
    x, y, cc = _mesh_pos()
    my_chip = 2 * x + y
    plan = []
    for a, (src, land) in enumerate(zip(src_refs, land_refs)):
        for j, (px, py) in enumerate([(1 - x, y), (x, 1 - y), (1 - x, 1 - y)]):
            peer = 2 * px + py
            sems = dict(send_sem=send_sems.at[3 * a + j], recv_sem=recv_sems.at[3 * a + j], device_id=(px, py, cc),
                        device_id_type=pl.DeviceIdType.MESH)
            plan.append((pltpu.make_async_remote_copy(src_ref=src.at[peer], dst_ref=land.at[my_chip], **sems),
                         pltpu.make_async_remote_copy(src_ref=src.at[peer], dst_ref=land.at[peer], **sems)))
    return plan


HBM_SPEC = pl.BlockSpec(memory_space=pltpu.HBM)
SEM_SPEC = pl.BlockSpec(memory_space=pltpu.SEMAPHORE)
ANY_SPEC = pl.BlockSpec(memory_space=pl.ANY)
SIDE_EFFECT = pltpu.CompilerParams(has_side_effects=pltpu.SideEffectType.DATAFLOW_SIDE_EFFECTING)


def _start_copies(plan, sems_per_array, srcs, lands, after, *, name):
    lands = [lax.empty(l, a.dtype) if isinstance(l, tuple) else l for l, a in zip(lands, srcs or lands)]
    ns, n = len(srcs), len(srcs) + len(lands)

    def body(*refs):
        send_sems, recv_sems = refs[n + len(after)], refs[n + len(after) + 1]
        for out, _ in plan(refs[:ns], refs[ns:n], send_sems, recv_sems):
            out.start()
        refs[-1][...] = jnp.zeros_like(refs[-1])

    sem = pltpu.SemaphoreType.DMA((len(lands) * sems_per_array,))
    res = pl.pallas_call(
        body, name=name,
        out_shape=(sem, sem, *[pltpu.HBM(a.shape, a.dtype) for a in srcs + lands], jax.ShapeDtypeStruct((8, LANES), F32)),
        in_specs=[HBM_SPEC] * n + [ANY_SPEC] * len(after),
        out_specs=(SEM_SPEC, SEM_SPEC, *[HBM_SPEC] * n, pl.BlockSpec(memory_space=pltpu.VMEM)),
        input_output_aliases={i: 2 + i for i in range(n)}, compiler_params=SIDE_EFFECT,
    )(*[pltpu.with_memory_space_constraint(a, pltpu.HBM) for a in srcs + lands], *after)
    return (res[0], res[1], list(res[2:2 + ns]), list(res[2 + ns:2 + n])), res[-1]


def _wait_copies(plan, handle, after, *, name):
    send, recv, srcs, lands = handle
    ns, n = len(srcs), len(srcs) + len(lands)

    def body(*refs):
        for out, inc in plan(refs[:ns], refs[ns:n], refs[n], refs[n + 1]):
            out.wait_send()
            inc.wait_recv()

    res = pl.pallas_call(
        body, name=name, out_shape=tuple(pltpu.HBM(a.shape, a.dtype) for a in srcs + lands),
        in_specs=[HBM_SPEC] * n + [SEM_SPEC, SEM_SPEC] + [ANY_SPEC] * len(after), out_specs=[HBM_SPEC] * n,
        input_output_aliases={i: i for i in range(n)}, compiler_params=SIDE_EFFECT,
    )(*srcs, *lands, send, recv, *after)
    return list(res[:ns]), list(res[ns:])


def _row_tile(rows, target=512):
    if rows <= target:
        return rows
    best = None
    for t in range(8, target + 1, 8):
        if rows % t == 0:
            best = t
    assert best is not None, (rows, target)
    return best


def _add_own(g, other, core, *, name):
    _, _, rows, cols = g.shape
    tm = _row_tile(rows)

    def body(c_ref, g_ref, o_ref, out_ref):
        out_ref[...] = g_ref[...] + o_ref[...]

    return pl.pallas_call(
        body, name=name, out_shape=jax.ShapeDtypeStruct(other.shape, other.dtype),
        grid_spec=pltpu.PrefetchScalarGridSpec(
            num_scalar_prefetch=1, grid=(4, rows // tm),
            in_specs=[pl.BlockSpec((None, None, tm, cols), lambda p, i, c_ref: (c_ref[0], p, i, 0)),
                      pl.BlockSpec((None, tm, cols), lambda p, i, c_ref: (p, i, 0))],
            out_specs=pl.BlockSpec((None, tm, cols), lambda p, i, c_ref: (p, i, 0))),
        compiler_params=_params(("parallel", "parallel")),
    )(core, g, other)


def _adamw(parts, w, m, v, after, *, name):
    layers, rows, cols = w.shape
    assert len(parts) == layers
    tm = _row_tile(rows, 256)
    nr = rows // tm

    def body(*refs):
        p_refs = refs[:layers]
        w_ref, m_ref, v_ref, _, g_ref, d_ref, nm_ref, nv_ref, g_s = refs[layers:]
        for ll in range(layers):
            @pl.when(pl.program_id(0) == ll)
            def _(ll=ll):
                g = p_refs[ll][0]
                for q in range(1, p_refs[ll].shape[0]):
                    g = g + p_refs[ll][q]
                g_s[...] = g

        g = g_s[...]
        mm = ADAM_B1 * m_ref[...] + (1.0 - ADAM_B1) * g
        vv = ADAM_B2 * v_ref[...] + (1.0 - ADAM_B2) * jnp.square(g)
        m_hat = mm / (1.0 - ADAM_B1 ** ADAM_STEP)
        v_hat = vv / (1.0 - ADAM_B2 ** ADAM_STEP)
        g_ref[...] = g
        d_ref[...] = -ADAM_LR * (m_hat / (jnp.sqrt(v_hat) + ADAM_EPS) + ADAM_WD * w_ref[...])
        nm_ref[...] = mm
        nv_ref[...] = vv

    def part_spec(ll, p):
        return pl.BlockSpec((p, tm, cols), lambda l, i: (0, jnp.where(l == ll, i, jnp.where(l < ll, 0, nr - 1)), 0))

    spec = pl.BlockSpec((None, tm, cols), lambda l, i: (l, i, 0))
    out = jax.ShapeDtypeStruct(w.shape, F32)
    return pl.pallas_call(
        body, name=name, grid=(layers, nr),
        in_specs=[part_spec(ll, parts[ll].shape[0]) for ll in range(layers)] + [spec] * 3 + [pl.BlockSpec(memory_space=pl.ANY)],
        out_specs=[spec] * 4, out_shape=[out] * 4, scratch_shapes=[pltpu.VMEM((tm, cols), F32)],
        compiler_params=_params(("arbitrary", "arbitrary")),
    )(*parts, w, m, v, after)


def _pack(arrs):
    flat = jnp.concatenate([a.reshape(-1) for a in arrs])
    rows = -(-flat.shape[0] // (PACK_COLS * 16)) * 16
    return jnp.pad(flat, (0, rows * PACK_COLS - flat.shape[0])).reshape(rows, PACK_COLS)


def _unpack(packed, shapes):
    flat = packed.reshape(-1)
    out, off = [], 0
    for shp in shapes:
        n = int(np.prod(shp))
        out.append(flat[off:off + n].reshape(shp))
        off += n
    return out


def _shards_to_full(g, axis):
    if axis == 0:
        return g.reshape(g.shape[0] * g.shape[1], g.shape[2])
    return jnp.transpose(g, (1, 0, 2)).reshape(g.shape[1], g.shape[0] * g.shape[2])


def _full_to_shards(a, axis):
    if axis == 0:
        return jnp.transpose(a.reshape(4, 2, a.shape[0] // N_DEV, a.shape[1]), (1, 0, 2, 3))
    return jnp.transpose(a.reshape(a.shape[0], 4, 2, a.shape[1] // N_DEV), (2, 1, 0, 3))


def _zeros_like_cols(a, n):
    return jnp.zeros(a.shape[:-1] + (n,), a.dtype)


def _pad_heads(a, n):
    z = _zeros_like_cols(a, HEAD)
    return jnp.concatenate([p for h in range(n) for p in (a[..., h * HEAD:(h + 1) * HEAD], z)], axis=-1)


def _unpad_heads(a, n):
    return jnp.concatenate([a[..., h * LANES:h * LANES + HEAD] for h in range(n)], axis=-1)


def _seg(first, width, sign=1):
    return (width, [(first, sign)])


def _zero(width):
    return (width, [])


def _swapped(first):
    half = MLA_ROPE // 2
    return [_seg(first + half, half, -1), _seg(first, half)]


def _padded_heads(first, n):
    return [s for h in range(n) for s in (_seg(first + HEAD * h, HEAD), _zero(HEAD))]


def _layout_w_in():
    kr = 384
    return (_padded_heads(1440, 4) + [_seg(0, 256), _seg(416, 256), _seg(672, 256), _seg(928, 256), _seg(1184, 256)]
            + _padded_heads(1696, 2) + _padded_heads(1824, 2) + [_seg(256, 128)]
            + [_zero(HEAD), _seg(kr, MLA_ROPE), _seg(kr, MLA_ROPE)] + [_zero(HEAD)] + _swapped(kr) + _swapped(kr))


def _layout_w_uq():
    out = []
    for h in range(MLA_HEADS):
        out += [_seg(96 * h, MLA_NOPE), _seg(96 * h + MLA_NOPE, MLA_ROPE)] + _swapped(96 * h + MLA_NOPE)
    return out


def _layout_w_ukv():
    keys = [s for h in range(MLA_HEADS) for s in (_seg(LANES * h, HEAD), _zero(HEAD))]
    values = [s for h in range(MLA_HEADS) for s in (_seg(LANES * h + HEAD, HEAD), _zero(HEAD))]
    return keys + values


def _layout_w_gate_up():
    return [_seg(half + j, GU_TILE) for j in range(0, D_FF, GU_TILE) for half in (0, D_FF)]


LAYOUTS = dict(w_in=_layout_w_in(), w_uq=_layout_w_uq(), w_ukv=_layout_w_ukv(), w_gate_up=_layout_w_gate_up())
SWAPPED_UPDATE = ("w_gate_up",)
OWN_COLS = dict(w_in=1952, w_uq=384, w_ukv=512, w_gate_up=2 * D_FF)


def _plan_extend(layout, shard):
    plan = []
    for width, terms in layout:
        if not terms:
            plan.append((width, []))
            continue
        (first, sign), = terms
        while width:
            g, off = divmod(first, shard)
            w = min(width, shard - off)
            plan.append((w, [(g, off, sign)]))
            first, width = first + w, width - w
    return [plan]


def _plan_fold(layout, own_cols):
    sources = [[] for _ in range(own_cols)]
    e = 0
    for width, terms in layout:
        for first, sign in terms:
            for i in range(width):
                sources[first + i].append((e + i, sign))
        e += width
    shard = own_cols // N_DEV
    plans = {}
    for g in range(N_DEV):
        plan, n = [], g * shard
        while n < (g + 1) * shard:
            w = 1
            while n + w < (g + 1) * shard and [(c + w, sg) for c, sg in sources[n]] == sources[n + w]:
                w += 1
            plan.append((w, [(0, c, sg) for c, sg in sources[n]]))
            n += w
        plans[g] = plan
    return [plans[2 * p + c] for c in range(2) for p in range(4)]


def _assemble(src, plans, out_cols, out_dtype, *, name, transposed=False):
    g, rows, c = src.shape
    tm = _row_tile(rows, 256)
    pad = -out_cols % LANES if transposed else 0

    def body(s_ref, o_ref):
        blocks = [s_ref[i].astype(F32) for i in range(g)]
        for d, plan in enumerate(plans):
            pieces = []
            for width, terms in plan + ([(pad, [])] if pad else []):
                v = None
                for b, first, sign in terms:
                    t = blocks[b][:, first:first + width]
                    t = -t if sign < 0 else t
                    v = t if v is None else v + t
                pieces.append(jnp.zeros((tm, width), F32) if v is None else v)
            block = pieces[0] if len(pieces) == 1 else jnp.concatenate(pieces, axis=1)
            o_ref[d] = (block.T[:out_cols, :] if transposed else block).astype(o_ref.dtype)

    block, shape = ((out_cols, tm), (out_cols, rows)) if transposed else ((tm, out_cols), (rows, out_cols))
    return pl.pallas_call(
        body, name=name, grid=(rows // tm,), in_specs=[pl.BlockSpec((g, tm, c), lambda i: (0, i, 0))],
        out_specs=pl.BlockSpec((len(plans),) + block, (lambda i: (0, 0, i)) if transposed else (lambda i: (0, i, 0))),
        out_shape=jax.ShapeDtypeStruct((len(plans),) + shape, out_dtype), compiler_params=_params(("parallel",)),
    )(src)


def _extend(nm, gathered, *, name):
    layout = LAYOUTS[nm]
    return _assemble(gathered, _plan_extend(layout, OWN_COLS[nm] // N_DEV), sum(w for w, _ in layout), BF16, name=name)[0]


def _fold_to_shards(nm, grad_ext, *, name):
    shards = _assemble(grad_ext[None], _plan_fold(LAYOUTS[nm], OWN_COLS[nm]), OWN_COLS[nm] // N_DEV, F32, name=name,
                       transposed=nm in SWAPPED_UPDATE)
    return shards.reshape((2, 4) + shards.shape[1:])


def _rope_tables(s):
    inv = 1.0 / (ROPE_THETA ** (jnp.arange(0, MLA_ROPE, 2, dtype=F32) / MLA_ROPE))
    ang = jnp.arange(s, dtype=F32)[:, None] * inv[None, :]
    cos, sin = jnp.cos(ang), jnp.sin(ang)
    c32, s32 = jnp.concatenate([cos, cos], axis=1), jnp.concatenate([sin, sin], axis=1)
    zeros, ones = jnp.zeros((s, HEAD), F32), jnp.ones((s, HEAD), F32)
    tq = jnp.concatenate([ones, c32, s32], axis=1) * (1.0 / math.sqrt(MLA_NOPE + MLA_ROPE))
    return (jnp.tile(tq, (1, MLA_HEADS)), jnp.concatenate([zeros, c32, c32], axis=1),
            jnp.concatenate([zeros, s32, s32], axis=1))


def _gn(y):
    return y * _rstd(y, D_GROUP)


def _mixer_fwd(x, w, tabs, l, after_attention=None):
    s = x.shape[0]
    tq, tkc, tks = tabs
    n = lambda t: f"l{l}_{t}"
    h = _rms_fwd(("row", x, D_MODEL, 0), w["attn_norm"], name=n("attn_norm"), rows=s, width=D_MODEL)
    proj = _mm(h, w["w_in"], name=n("in_proj"))
    def prep(cq, ckv, kr, krp, gq, gkv, wuq, wukv, tqv, tc, ts):
        cqn = (cq * _rstd(cq, 256) * gq).astype(BF16)
        ckvn = (ckv * _rstd(ckv, 128) * gkv).astype(BF16)
        qe = jnp.dot(cqn, wuq, preferred_element_type=F32)
        kve = jnp.dot(ckvn, wukv, preferred_element_type=F32)
        kb = kr * tc + krp * ts
        kvv = kve[:, 512:]
        lane = lax.broadcasted_iota(jnp.int32, kvv.shape, 1) & (LANES - 1)
        v = jnp.where(lane == HEAD, 1.0, kvv)
        k = kve[:, :512] + jnp.tile(kb, (1, MLA_HEADS))
        return cqn, ckvn, qe * tqv, k, k, v, v

    cqn, ckvn, qm, km, kmt, vm, vmt = _rowwise(
        prep, name=n("mla_prep"), rows=s,
        ins=[("row", proj, 256, C_CQ[0] // 256), ("row", proj, 128, C_CKV[0] // 128),
             ("row", proj, 128, C_KR[0] // 128), ("row", proj, 128, C_KRP[0] // 128),
             ("full", w["mla_q_norm"]), ("full", w["mla_kv_norm"]), ("full", w["w_uq"]), ("full", w["w_ukv"]),
             ("row", tq, 512, 0), ("row", tkc, 128, 0), ("row", tks, 128, 0)],
        outs=[(256, BF16), (128, BF16), (512, BF16, MLA_HEADS), (512, BF16, MLA_HEADS), (512, BF16, MLA_HEADS, "T"),
              (512, BF16, MLA_HEADS), (512, BF16, MLA_HEADS, "T")])
    y_a, lse_row = _mla_fwd(qm, km, vmt, name=n("mla_fwd"), s=s, nh=MLA_HEADS)
    mix_norm = w["mix_norm"]
    if after_attention is not None:
        mix_norm = mix_norm + after_attention(y_a)[0, 0]
    y_b = _conv_fwd(proj, w["conv_w"], name=n("conv_fwd"), s=s)
    y_c = _pool_fwd(proj, w["pool_wbd"], w["pool_scale"], name=n("pool_fwd"), s=s)
    y_d, lse_d = _swa_fwd(proj, w["hp_swa"], name=n("swa_fwd"), s=s, scale=1.0 / math.sqrt(HEAD))

    def mix(ya, yb, yc, yd, mn):
        return (jnp.concatenate([_gn(_unpad_heads(ya, 4)), _gn(yb), _gn(yc), _gn(_unpad_heads(yd, 4))], axis=1) * mn,)

    mixed = _rowwise(mix, name=n("group_norm"), rows=s,
                     ins=[("heads", y_a), ("row", y_b, 256, 0), ("row", y_c, 256, 0), ("row", y_d, 512, 0),
                          ("full", mix_norm)], outs=[(D_MODEL, BF16)])[0]
    x1 = _mm(mixed, w["w_o"], res=x, name=n("out_proj"))
    saved = dict(x=x, h=h, proj=proj, cqn=cqn, ckvn=ckvn, qm=qm, km=km, kmt=kmt, vm=vm, y_a=y_a, lse_row=lse_row,
                 y_b=y_b, y_c=y_c, y_d=y_d, lse_d=lse_d, mixed=mixed)
    return x1, saved


def _ffn_fwd(x1, w, l):
    s = x1.shape[0]
    n = lambda t: f"l{l}_{t}"
    h2 = _rms_fwd(("row", x1, D_MODEL, 0), w["ffn_norm"], name=n("ffn_norm"), rows=s, width=D_MODEL)

    def swiglu(gu):
        g, u = gu[:, :GU_TILE], gu[:, GU_TILE:]
        return gu, g * jax.nn.sigmoid(g) * u

    gu, act = _mm(h2, w["w_gate_up"], tm=2048, tn=2 * GU_TILE, name=n("gate_up"),
                  epilogue=(swiglu, [], [(2 * D_FF, BF16), (D_FF, BF16)]))
    x2 = _mm(act, w["w_down"], res=x1, tm=512, tk=D_FF, name=n("down"))
    return x2, dict(x1=x1, h2=h2, gu=gu, act=act)


def _ffn_bwd_down(dx2, sv, w, l):
    n = lambda t: f"l{l}_{t}"

    def swiglu_bwd(da, gu):
        gt, u = gu[:, :GU_TILE].astype(F32), gu[:, GU_TILE:].astype(F32)
        sg = jax.nn.sigmoid(gt)
        return (jnp.concatenate([da * u * sg * (1.0 + gt * (1.0 - sg)), da * gt * sg], axis=1),)

    dgu = _mm(dx2[1], w["w_down"], tb=True, tm=2048, tn=GU_TILE, name=n("d_act"),
              epilogue=(swiglu_bwd, [sv["gu"]], [(2 * D_FF, BF16)]))[0]
    g = dict(w_down=_mm(sv["act"], dx2[1], ta=True, tm=D_FF // 2, name=n("dw_down")))
    return dgu, g


def _ffn_bwd_up(dx2, dgu, sv, w, l):
    s = dgu.shape[0]
    n = lambda t: f"l{l}_{t}"
    dh2 = _mm_nt_deep(dgu, w["w_gate_up"], name=n("d_h2"))
    g = dict(w_gate_up=_mm(sv["h2"], dgu, ta=True, tn=D_FF // 2, name=n("dw_gate_up")))
    dx1, dx1_b, g["ffn_norm"] = _rms_bwd(("row", sv["x1"], D_MODEL, 0), w["ffn_norm"], dh2, dx2[0],
                                         name=n("ffn_norm_bwd"), rows=s, width=D_MODEL, out_dtypes=(F32, BF16))
    return (dx1, dx1_b), g


def _mixer_bwd_out(dx1, sv, w, l):
    s = dx1[1].shape[0]
    n = lambda t: f"l{l}_{t}"
    dmixed = _mm(dx1[1], w["w_o"], tb=True, name=n("d_mixed"))
    g = dict(w_o=_mm(sv["mixed"], dx1[1], ta=True, name=n("dw_o")))

    def mix_bwd(dm, ya, yb, yc, yd, mn):
        outs, dmn = [], []
        for i, y in enumerate((_unpad_heads(ya, 4), yb, yc, _unpad_heads(yd, 4))):
            lo, hi = i * D_GROUP, (i + 1) * D_GROUP
            r = _rstd(y, D_GROUP)
            nrm = y * r
            dmg = dm[:, lo:hi]
            dn = dmg * mn[:, lo:hi]
            dy = r * (dn - nrm * (jnp.sum(dn * nrm, axis=-1, keepdims=True) * (1.0 / D_GROUP)))
            outs.append(_pad_heads(dy, 4) if i in (0, 3) else dy)
            dmn.append(jnp.sum(dmg * nrm, axis=0, keepdims=True))
        return (*outs, jnp.concatenate(dmn, axis=1))

    dy_a, dy_b, dy_c, dy_d, g["mix_norm"] = _rowwise(
        mix_bwd, name=n("group_norm_bwd"), rows=s,
        ins=[("row", dmixed, D_MODEL, 0), ("heads", sv["y_a"]), ("row", sv["y_b"], 256, 0),
             ("row", sv["y_c"], 256, 0), ("row", sv["y_d"], 512, 0), ("full", w["mix_norm"])],
        outs=[(512, F32, MLA_HEADS), (256, F32), (256, F32), (512, F32)], reds=[(1, D_MODEL)])
    return (dy_a, dy_b, dy_c, dy_d), g


def _mixer_bwd_in(dx1, dys, sv, w, tabs, l):
    s = dx1[0].shape[0]
    tq, tkc, tks = tabs
    n = lambda t: f"l{l}_{t}"
    dy_a, dy_b, dy_c, dy_d = dys
    g = {}

    proj = sv["proj"]
    dq_sw, dk_sw, dv_sw, dsink = _swa_bwd(proj, sv["y_d"], dy_d, sv["lse_d"], w["hp_swa"], name=n("swa_bwd"), s=s,
                                          scale=1.0 / math.sqrt(HEAD))
    g["swa_sinks"] = dsink[:, 0, 0]

    dqm, dkm, dvm = _mla_bwd(sv["qm"], sv["km"], sv["kmt"], sv["vm"], sv["y_a"], dy_a, sv["lse_row"], w["hp_swa"],
                             name=n("mla_bwd"), s=s, nh=MLA_HEADS)

    def rms_bwd(x, gv, dy, width):
        r = _rstd(x, width)
        dyg = dy * gv
        dx = r * dyg - x * (r * r * r) * (jnp.sum(dyg * x, axis=-1, keepdims=True) * (1.0 / width))
        return dx, jnp.sum(dy * x * r, axis=0, keepdims=True)

    def prep_bwd(dq, dk, dv, cq, ckv, gq, gkv, wuq, wukv, tqv, tc, ts):
        dkb = dk[:, 0:128] + dk[:, 128:256] + dk[:, 256:384] + dk[:, 384:512]
        dq_ext = (dq * tqv).astype(BF16)
        dkv_ext = jnp.concatenate([dk.astype(BF16), dv], axis=1)
        dcqn = lax.dot_general(dq_ext, wuq, NT_DIMS, preferred_element_type=F32)
        dckvn = lax.dot_general(dkv_ext, wukv, NT_DIMS, preferred_element_type=F32)
        dcq, dgq = rms_bwd(cq, gq, dcqn, 256)
        dckv, dgkv = rms_bwd(ckv, gkv, dckvn, 128)
        return dq_ext, dkv_ext, dkb * tc, dkb * ts, dcq, dckv, dgq, dgkv

    dq_ext, dkv_ext, dkr, dkrp, dcq, dckv, g["mla_q_norm"], g["mla_kv_norm"] = _rowwise(
        prep_bwd, name=n("mla_prep_bwd"), rows=s,
        ins=[("heads", dqm), ("heads", dkm), ("heads", dvm), ("row", proj, 256, C_CQ[0] // 256),
             ("row", proj, 128, C_CKV[0] // 128), ("full", w["mla_q_norm"]), ("full", w["mla_kv_norm"]),
             ("full", w["w_uq"]), ("full", w["w_ukv"]), ("row", tq, 512, 0), ("row", tkc, 128, 0), ("row", tks, 128, 0)],
        outs=[(512, BF16), (1024, BF16), (128, BF16), (128, BF16), (256, BF16), (128, BF16)],
        reds=[(1, 256), (1, 128)])
    g["w_uq"] = _mm(sv["cqn"], dq_ext, ta=True, name=n("dw_uq"))
    g["w_ukv"] = _mm(sv["ckvn"], dkv_ext, ta=True, name=n("dw_ukv"))

    dgb, dgc, duc, g["conv_w"] = _conv_bwd(dy_b, proj, w["conv_w"], name=n("conv_bwd"), s=s)
    dup, g["pool_wbd"], g["pool_scale"] = _pool_bwd(dy_c, proj, w["pool_wbd"], w["pool_scale"], name=n("pool_bwd"), s=s)

    dproj = jnp.concatenate([dq_sw, dcq, dgb, dgc, duc, dup, dk_sw.astype(BF16), dv_sw.astype(BF16), dckv, dkr, dkrp],
                            axis=1)
    g["w_in"] = _mm(sv["h"], dproj, ta=True, name=n("dw_in"))
    return dproj, g


def _mixer_bwd_norm(dx1, dproj, sv, w, l):
    n = lambda t: f"l{l}_{t}"
    dh = _mm(dproj, w["w_in"], tb=True, name=n("d_h"))
    dx0, dx0_b, dg = _rms_bwd(("row", sv["x"], D_MODEL, 0), w["attn_norm"], dh, dx1[0], name=n("attn_norm_bwd"),
                              rows=dh.shape[0], width=D_MODEL, out_dtypes=(F32, BF16))
    return (dx0, dx0_b), dict(attn_norm=dg)


def _loss_head(x, target, g, *, s):
    def fn(xv, tv, gv):
        r = _rstd(xv, D_MODEL)
        e = xv * r * gv - tv
        part = jnp.sum(jnp.sum(e * e, axis=1, keepdims=True), axis=0, keepdims=True) * (0.5 / D_MODEL)
        dy = e * (1.0 / D_MODEL)
        dyg = dy * gv
        dx = r * dyg - xv * (r * r * r) * (jnp.sum(dyg * xv, axis=-1, keepdims=True) * (1.0 / D_MODEL))
        return dx, dx, jnp.sum(dy * xv * r, axis=0, keepdims=True), jnp.broadcast_to(part, (1, LANES))

    return _rowwise(fn, name="loss_head", rows=s,
                    ins=[("row", x, D_MODEL, 0), ("row", target, D_MODEL, 0), ("full", g)],
                    outs=[(D_MODEL, F32), (D_MODEL, BF16)], reds=[(1, D_MODEL), (1, LANES)])


def _alibi_slopes(n):
    return np.asarray([2.0 ** (-8.0 * (i + 1) / n) for i in range(n)], dtype=np.float32)


MIXER_WEIGHTS = ("w_in", "w_uq", "w_ukv", "conv_w", "w_o")
FFN_WEIGHTS = ("w_gate_up", "w_down")


def _mixer_weights(full, rep, l):
    pw = rep["pool_w"][l]
    z = jnp.zeros((HEAD, HEAD), F32)
    wbd = jnp.stack([jnp.block([[pw[2 * j], z], [z, pw[2 * j + 1]]]) for j in range(2)])
    return dict(
        attn_norm=rep["attn_norm"][l][None], w_in=full["w_in"], mla_q_norm=rep["mla_q_norm"][l][None],
        w_uq=full["w_uq"], mla_kv_norm=rep["mla_kv_norm"][l][None], w_ukv=full["w_ukv"],
        conv_w=full["conv_w"], pool_wbd=wbd, pool_scale=rep["pool_scale"][l][None],
        mix_norm=rep["mix_norm"][l][None], w_o=full["w_o"],
        hp_swa=jnp.stack([jnp.asarray(_alibi_slopes(SWA_HEADS)), rep["swa_sinks"][l]], axis=1))


def _ffn_weights(full, rep, l):
    return dict(ffn_norm=rep["ffn_norm"][l][None], w_gate_up=full["w_gate_up"], w_down=full["w_down"])


def _small_grads(g):
    rows = ("attn_norm", "mla_q_norm", "mla_kv_norm", "pool_scale", "ffn_norm", "mix_norm")
    out = {nm: g[nm][0] for nm in rows if nm in g}
    if "swa_sinks" in g:
        out["swa_sinks"] = g["swa_sinks"]
    if "pool_wbd" in g:
        e = g["pool_wbd"]
        out["pool_w"] = jnp.stack([e[j // 2][HEAD * (j % 2):HEAD * (j % 2 + 1), HEAD * (j % 2):HEAD * (j % 2 + 1)]
                                   for j in range(4)])
    return out


def kernel(x, attn_norm, w_in, mla_q_norm, w_uq, mla_kv_norm, w_ukv, conv_w, pool_w, pool_scale, swa_sinks, mix_norm, w_o, ffn_norm, w_gate_up, w_down, final_norm, loss_target, m_attn_norm, m_w_in, m_mla_q_norm, m_w_uq, m_mla_kv_norm, m_w_ukv, m_conv_w, m_pool_w, m_pool_scale, m_swa_sinks, m_mix_norm, m_w_o, m_ffn_norm, m_w_gate_up, m_w_down, m_final_norm, v_attn_norm, v_w_in, v_mla_q_norm, v_w_uq, v_mla_kv_norm, v_w_ukv, v_conv_w, v_pool_w, v_pool_scale, v_swa_sinks, v_mix_norm, v_w_o, v_ffn_norm, v_w_gate_up, v_w_down, v_final_norm):
    given = dict(locals())
    sh_names = [nm for nm, _, _ in SHARDED]
    sh_axis = {nm: ax - 1 for nm, _, ax in SHARDED}
    rep_names = [nm for nm, _ in REPLICATED]
    rep_shapes = [shp for _, shp in REPLICATED]
    rep = {nm: given[nm] for nm in rep_names if nm != "loss"}
    me = 4 * lax.axis_index("x") + 2 * lax.axis_index("y") + lax.axis_index("c")
    my_chip = 2 * lax.axis_index("x") + lax.axis_index("y")
    core = lax.axis_index("c").astype(jnp.int32).reshape(1)

    def behind(token, a):
        return a + token[0, 0].astype(a.dtype)

    def wire(nm, l):
        if nm == "conv_w":
            return lax.bitcast_convert_type(given[nm][l], BF16).reshape(3, -1)
        return given[nm][l].astype(BF16)

    def whole(nm, g, tag):
        if nm in LAYOUTS:
            return _extend(nm, g, name=f"extend_{nm}_{tag}")
        if nm == "conv_w":
            g = lax.bitcast_convert_type(g.reshape(N_DEV, 3, -1, 2), F32)
        return _shards_to_full(g, sh_axis[nm])

    def near_start(names, l, after, tag):
        srcs = [wire(nm, l) for nm in names]
        return _start_copies(_plan_gather_near, 4, srcs, [(N_DEV,) + a.shape for a in srcs], after, name=f"start_gather_{tag}")

    def pass_on(handle, after, tag):
        srcs, lands = _wait_copies(_plan_gather_near, handle, after, name=f"wait_gather_{tag}")
        handle, token = _start_copies(_plan_gather_pass, 3, [], lands, [], name=f"start_pass_{tag}")
        return (srcs, handle), token

    def gathered(names, state, after, tag):
        srcs, handle = state
        _, lands = _wait_copies(_plan_gather_pass, handle, after, name=f"wait_pass_{tag}")
        return {nm: whole(nm, lax.dynamic_update_index_in_dim(land, src, me, 0), tag)
                for nm, src, land in zip(names, srcs, lands)}

    layer1 = MIXER_WEIGHTS + FFN_WEIGHTS
    got = _all_gather([wire(nm, 0) for nm in MIXER_WEIGHTS], name="gather_mixer0")
    full_m0 = {nm: whole(nm, g, "mixer0") for nm, g in zip(MIXER_WEIGHTS, got)}
    h_f0, tok = near_start(FFN_WEIGHTS, 0, [], "ffn0")
    h_l1, tok = near_start(layer1, 1, [tok], "layer1")

    xs, target = x[0], loss_target[0]
    s = xs.shape[0]
    tabs = _rope_tables(s)
    wm, wf, svm, svf = [None] * DEPTH, [None] * DEPTH, [None] * DEPTH, [None] * DEPTH
    wm[0] = _mixer_weights(full_m0, rep, 0)
    wm[0]["attn_norm"] = behind(tok, wm[0]["attn_norm"])
    passing = {}

    def pass_ffn0(y_a):
        passing["ffn0"], token = pass_on(h_f0, [y_a], "ffn0")
        return token

    x1, svm[0] = _mixer_fwd(xs, wm[0], tabs, 0, after_attention=pass_ffn0)
    wf[0] = _ffn_weights(gathered(FFN_WEIGHTS, passing["ffn0"], [x1], "ffn0"), rep, 0)
    passing["layer1"], tok = pass_on(h_l1, [x1], "layer1")
    wf[0]["ffn_norm"] = behind(tok, wf[0]["ffn_norm"])
    x2, svf[0] = _ffn_fwd(x1, wf[0], 0)
    full_1 = gathered(layer1, passing["layer1"], [x2], "layer1")
    wm[1], wf[1] = _mixer_weights(full_1, rep, 1), _ffn_weights(full_1, rep, 1)
    x1, svm[1] = _mixer_fwd(x2, wm[1], tabs, 1)
    x2, svf[1] = _ffn_fwd(x1, wf[1], 1)
    dx_f, dx_b, d_final, loss = _loss_head(x2, target, rep["final_norm"][None], s=s)
    dx = (dx_f, dx_b)

    parts = {}

    def reduce_start(grads, l, after, tag):
        names = [nm for nm in sh_names if nm in grads]
        mine = [_fold_to_shards(nm, grads[nm], name=f"fold_{nm}_{l}") if nm in LAYOUTS
                else _full_to_shards(grads[nm], sh_axis[nm]) for nm in names]
        handle, token = _start_copies(_plan_sibling, 1, mine, [m.shape[1:] for m in mine], after, name=f"start_sibling_{tag}")
        return (names, l, handle), token

    def reduce_mid(state, after, tag):
        names, l, handle = state
        mine, theirs = _wait_copies(_plan_sibling, handle, after, name=f"wait_sibling_{tag}")
        sums = [_add_own(g, o, core, name=f"chip_sum_{nm}_{l}") for nm, g, o in zip(names, mine, theirs)]
        handle, token = _start_copies(_plan_chips, 3, sums, [a.shape for a in sums], [], name=f"start_chips_{tag}")
        return (names, l, handle), token

    def reduce_end(state, after, tag):
        names, l, handle = state
        sums, lands = _wait_copies(_plan_chips, handle, after, name=f"wait_chips_{tag}")
        for nm, own, land in zip(names, sums, lands):
            parts[nm, l] = lax.dynamic_update_index_in_dim(land, lax.dynamic_index_in_dim(own, my_chip, 0, keepdims=False),
                                                           my_chip, 0)

    small = [None] * DEPTH
    in_flight = []
    pending = None
    for l in reversed(range(DEPTH)):
        dgu, g_down = _ffn_bwd_down(dx, svf[l], wf[l], l)
        if pending is not None:
            state, token = reduce_mid(pending, [dgu], f"mixer{l + 1}")
            in_flight.append((state, f"mixer{l + 1}"))
            wf[l]["ffn_norm"] = behind(token, wf[l]["ffn_norm"])
        dx1, g_up = _ffn_bwd_up(dx, dgu, svf[l], wf[l], l)
        g_ffn = {**g_down, **g_up}
        state, token = reduce_start(g_ffn, l, [], f"ffn{l}")
        wm[l]["mix_norm"] = behind(token, wm[l]["mix_norm"])
        dys, g_out = _mixer_bwd_out(dx1, svm[l], wm[l], l)
        state, token = reduce_mid(state, [dys[0]], f"ffn{l}")
        in_flight.append((state, f"ffn{l}"))
        wm[l]["hp_swa"] = behind(token, wm[l]["hp_swa"])
        dproj, g_in = _mixer_bwd_in(dx1, dys, svm[l], wm[l], tabs, l)
        g_mixer = {**g_out, **g_in}
        pending, token = reduce_start(g_mixer, l, [], f"mixer{l}")
        wm[l]["attn_norm"] = behind(token, wm[l]["attn_norm"])
        dx, g_norm = _mixer_bwd_norm(dx1, dproj, svm[l], wm[l], l)
        small[l] = _small_grads({**g_ffn, **g_mixer, **g_norm})
    grads = {nm: jnp.stack([small[l][nm] for l in range(DEPTH)]) for nm in rep_names if nm in small[0]}
    grads["final_norm"] = d_final[0]
    grads["loss"] = loss[0, :1]
    zero = jnp.zeros((1,), F32)
    small = _all_gather([behind(token, _pack([grads[nm] for nm in rep_names]))], name="gather_small_grads")

    last, token = reduce_mid(pending, [dx[0], small[0]], "mixer0")
    for state, tag in in_flight:
        reduce_end(state, [], tag)
    grad_x = dx[0]

    def adamw(nm, after):
        swap = (lambda a: jnp.swapaxes(a, 1, 2)) if nm in SWAPPED_UPDATE else (lambda a: a)
        res = _adamw([parts[nm, l] for l in range(DEPTH)], swap(given[nm]), swap(given["m_" + nm]), swap(given["v_" + nm]),
                     after, name=f"adamw_{nm}")
        return [swap(a) for a in res]

    sh_out = {nm: adamw(nm, token) for nm in FFN_WEIGHTS}
    packs = [_pack([given.get(pre + nm, zero) for nm in rep_names])[None] for pre in ("", "m_", "v_")]
    rep_res = _adamw(small, *packs, token, name="adamw_replicated")
    rep_out = [dict(zip(rep_names, _unpack(o[0], rep_shapes))) for o in rep_res]

    reduce_end(last, [rep_res[0], sh_out["w_down"][0]], "mixer0")
    sh_out.update({nm: adamw(nm, token) for nm in MIXER_WEIGHTS})

    out = [rep_out[0]["loss"][0], grad_x[None]]
    for i in range(4):
        out += [sh_out[nm][i] if nm in sh_axis else rep_out[i][nm] for nm in WEIGHT_ORDER]
    return tuple(out)
```
